```python
import math
import jax, jax.numpy as jnp
from jax import lax
import numpy as np

D_MODEL = 1024
BATCH = 8
SEQ = 4096
DEPTH = 1

GRID_W = 64
CTX_LEN = 256
N_HEADS = 8
QK_NOPE_DIM = 64
QK_ROPE_DIM = 32
V_HEAD_DIM = 64
Q_LORA_RANK = 256
KV_LORA_RANK = 128
MLA_WIDTH = N_HEADS * V_HEAD_DIM
CONV_WIDTH = D_MODEL - MLA_WIDTH
CONV_K = 3
D_FF = 4 * D_MODEL
ROPE_THETA = 10000.0
ROPE_AXIS_DIM = QK_ROPE_DIM // 2
Q_BLOCK = 128
EPS = 1e-6
MLA_IN = Q_LORA_RANK + KV_LORA_RANK + QK_ROPE_DIM
IN_COLS = MLA_IN + 3 * CONV_WIDTH
QK_DIM = QK_NOPE_DIM + QK_ROPE_DIM
ATTN_SCALE = 1.0 / math.sqrt(QK_DIM)

kernel_name = 'hybrid_mla_shortconv_dit_layer'


def rmsnorm(x):
    xf = x.astype(jnp.float32)
    y = xf * lax.rsqrt(jnp.mean(xf * xf, axis=-1, keepdims=True) + EPS)
    return y.astype(x.dtype)


def modulate(x, shift, scale):
    return rmsnorm(x) * (1 + scale) + shift


def adaln(cvec, w_mod, b_mod):
    m = jax.nn.silu(cvec) @ w_mod + b_mod
    return jnp.split(m, 6, axis=-1)


def rope_tables(rows):
    row = jnp.broadcast_to(jnp.arange(rows)[:, None], (rows, GRID_W)).reshape(-1)
    col = jnp.broadcast_to(jnp.arange(GRID_W)[None, :], (rows, GRID_W)).reshape(-1)
    freqs = ROPE_THETA ** (-jnp.arange(0, ROPE_AXIS_DIM, 2, dtype=jnp.float32) / ROPE_AXIS_DIM)
    ang = jnp.stack([row.astype(jnp.float32)[:, None] * freqs,
                     col.astype(jnp.float32)[:, None] * freqs], axis=1)
    ang = ang[:, None]
    return jnp.cos(ang), jnp.sin(ang)


def apply_rope(x, cos, sin):
    xs = x.reshape(x.shape[:-1] + (2, 2, ROPE_AXIS_DIM // 2))
    x1, x2 = xs[..., 0, :], xs[..., 1, :]
    cos = cos.astype(x.dtype)
    sin = sin.astype(x.dtype)
    out = jnp.stack([x1 * cos - x2 * sin, x2 * cos + x1 * sin], axis=-2)
    return out.reshape(x.shape)


def mla_q(z, q_g, w_uq, cos, sin):
    cq = rmsnorm(z[..., :Q_LORA_RANK]) * q_g
    q = (cq @ w_uq).reshape(z.shape[:-1] + (N_HEADS, QK_DIM))
    q_nope, q_rope = q[..., :QK_NOPE_DIM], q[..., QK_NOPE_DIM:]
    if cos is not None:
        q_rope = apply_rope(q_rope, cos, sin)
    return jnp.concatenate([q_nope, q_rope], axis=-1)


def mla_kv(z, kv_g, w_ukv, cos, sin):
    ckv = rmsnorm(z[..., Q_LORA_RANK:Q_LORA_RANK + KV_LORA_RANK]) * kv_g
    k_rope = z[..., Q_LORA_RANK + KV_LORA_RANK:MLA_IN][..., None, :]
    kv = (ckv @ w_ukv).reshape(z.shape[:-1] + (N_HEADS, QK_NOPE_DIM + V_HEAD_DIM))
    k_nope, v = kv[..., :QK_NOPE_DIM], kv[..., QK_NOPE_DIM:]
    if cos is not None:
        k_rope = apply_rope(k_rope, cos, sin)
    k_rope = jnp.broadcast_to(k_rope, k_nope.shape[:-1] + (QK_ROPE_DIM,))
    return jnp.concatenate([k_nope, k_rope], axis=-1), v


def attention_dense(q, k, v):
    s = jnp.einsum('bqhd,bkhd->bhqk', q, k).astype(jnp.float32) * ATTN_SCALE
    p = jax.nn.softmax(s, axis=-1).astype(v.dtype)
    o = jnp.einsum('bhqk,bkhd->bqhd', p, v)
    return o.reshape(o.shape[:2] + (N_HEADS * V_HEAD_DIM,))


def attention_blocked(q, k, v):
    b, s = q.shape[0], q.shape[1]
    nblk = s // Q_BLOCK
    qb = q.reshape(b, nblk, Q_BLOCK, N_HEADS, QK_DIM).swapaxes(0, 1)
    o = lax.map(lambda qq: attention_dense(qq, k, v), qb)
    return o.swapaxes(0, 1).reshape(b, s, N_HEADS * V_HEAD_DIM)


def short_conv(z, conv_w):
    gb, gc, xin = jnp.split(z[..., MLA_IN:], 3, axis=-1)
    u = gc * xin
    n = u.shape[1]
    up = jnp.pad(u, ((0, 0), (1, 1), (0, 0)))
    y = conv_w[0] * up[:, :n] + conv_w[1] * up[:, 1:n + 1] + conv_w[2] * up[:, 2:n + 2]
    return gb * y


def sq_relu_mlp(h, w1, w2):
    return jnp.square(jax.nn.relu(h @ w1)) @ w2


def _fwd_setup_inputs(seed: int = 0) -> dict:
    key = jax.random.key(seed)
    ks = jax.random.split(key, 16)
    f32 = jnp.float32
    n = lambda k, shape, s: jax.random.normal(k, shape, f32) * s
    return {
        'x': n(ks[0], (BATCH, SEQ, D_MODEL), 1.0),
        'c': n(ks[1], (BATCH, D_MODEL), 1.0),
        'ctx': n(ks[2], (BATCH, CTX_LEN, D_MODEL), 1.0),
        'c_ctx': n(ks[3], (D_MODEL,), 1.0),
        'w_mod': n(ks[4], (DEPTH, D_MODEL, 6 * D_MODEL), D_MODEL ** -0.5),
        'b_mod': n(ks[5], (DEPTH, 6 * D_MODEL), 0.02),
        'w_in': n(ks[6], (DEPTH, D_MODEL, IN_COLS), D_MODEL ** -0.5),
        'q_norm_g': 1.0 + n(ks[7], (DEPTH, Q_LORA_RANK), 0.1),
        'w_uq': n(ks[8], (DEPTH, Q_LORA_RANK, N_HEADS * QK_DIM), Q_LORA_RANK ** -0.5),
        'kv_norm_g': 1.0 + n(ks[9], (DEPTH, KV_LORA_RANK), 0.1),
        'w_ukv': n(ks[10], (DEPTH, KV_LORA_RANK, N_HEADS * (QK_NOPE_DIM + V_HEAD_DIM)), KV_LORA_RANK ** -0.5),
        'conv_w': n(ks[11], (DEPTH, CONV_K, CONV_WIDTH), CONV_K ** -0.5),
        'w_out': n(ks[12], (DEPTH, D_MODEL, D_MODEL), D_MODEL ** -0.5),
        'w_mlp1': n(ks[13], (DEPTH, D_MODEL, D_FF), D_MODEL ** -0.5),
        'w_mlp2': n(ks[14], (DEPTH, D_FF, D_MODEL), D_FF ** -0.5),
        'final_norm_g': 1.0 + n(ks[15], (D_MODEL,), 0.1),
    }


def _fwd_reference(x, c, ctx, c_ctx, w_mod, b_mod, w_in, q_norm_g, w_uq, kv_norm_g, w_ukv,
              conv_w, w_out, w_mlp1, w_mlp2, final_norm_g):
    rows = x.shape[1] // GRID_W
    cos, sin = rope_tables(rows)
    ctx_s = ctx
    for i in range(DEPTH):
        sh1, sc1, g1, sh2, sc2, g2 = [m[:, None, :] for m in adaln(c, w_mod[i], b_mod[i])]
        sh1c, sc1c, g1c, sh2c, sc2c, g2c = adaln(c_ctx, w_mod[i], b_mod[i])

        z = modulate(x, sh1, sc1) @ w_in[i]
        zc = modulate(ctx_s, sh1c, sc1c) @ w_in[i]

        q = mla_q(z, q_norm_g[i], w_uq[i], cos, sin)
        k, v = mla_kv(z, kv_norm_g[i], w_ukv[i], cos, sin)
        kc, vc = mla_kv(zc, kv_norm_g[i], w_ukv[i], None, None)
        k_all = jnp.concatenate([k, kc], axis=1)
        v_all = jnp.concatenate([v, vc], axis=1)
        attn = attention_blocked(q, k_all, v_all)
        conv = short_conv(z, conv_w[i])
        x = x + g1 * (jnp.concatenate([attn, conv], axis=-1) @ w_out[i])

        x = x + g2 * sq_relu_mlp(modulate(x, sh2, sc2), w_mlp1[i], w_mlp2[i])

        if i + 1 < DEPTH:
            qc = mla_q(zc, q_norm_g[i], w_uq[i], None, None)
            attn_c = attention_dense(qc, kc, vc)
            conv_c = short_conv(zc, conv_w[i])
            ctx_s = ctx_s + g1c * (jnp.concatenate([attn_c, conv_c], axis=-1) @ w_out[i])
            ctx_s = ctx_s + g2c * sq_relu_mlp(modulate(ctx_s, sh2c, sc2c), w_mlp1[i], w_mlp2[i])

    return rmsnorm(x) * final_norm_g


import jax as _jax
import jax.numpy as _jnp

TWIN_FORMAT = 'train_step'
FWD_PARAMS = ['x', 'c', 'ctx', 'c_ctx', 'w_mod', 'b_mod', 'w_in', 'q_norm_g', 'w_uq', 'kv_norm_g', 'w_ukv', 'conv_w', 'w_out', 'w_mlp1', 'w_mlp2', 'final_norm_g']
TWIN_WEIGHTS = ['c_ctx', 'w_mod', 'b_mod', 'w_in', 'q_norm_g', 'w_uq', 'kv_norm_g', 'w_ukv', 'conv_w', 'w_out', 'w_mlp1', 'w_mlp2', 'final_norm_g']
TWIN_DIFF_INPUT = 'x'
TWIN_INPUTS = ['x', 'c', 'ctx', 'c_ctx', 'w_mod', 'b_mod', 'w_in', 'q_norm_g', 'w_uq', 'kv_norm_g', 'w_ukv', 'conv_w', 'w_out', 'w_mlp1', 'w_mlp2', 'final_norm_g', 'loss_target', 'm_c_ctx', 'm_w_mod', 'm_b_mod', 'm_w_in', 'm_q_norm_g', 'm_w_uq', 'm_kv_norm_g', 'm_w_ukv', 'm_conv_w', 'm_w_out', 'm_w_mlp1', 'm_w_mlp2', 'm_final_norm_g', 'v_c_ctx', 'v_w_mod', 'v_b_mod', 'v_w_in', 'v_q_norm_g', 'v_w_uq', 'v_kv_norm_g', 'v_w_ukv', 'v_conv_w', 'v_w_out', 'v_w_mlp1', 'v_w_mlp2', 'v_final_norm_g']
TWIN_OUTPUTS = ['loss', 'grad_x', 'grad_c_ctx', 'grad_w_mod', 'grad_b_mod', 'grad_w_in', 'grad_q_norm_g', 'grad_w_uq', 'grad_kv_norm_g', 'grad_w_ukv', 'grad_conv_w', 'grad_w_out', 'grad_w_mlp1', 'grad_w_mlp2', 'grad_final_norm_g', 'delta_c_ctx', 'delta_w_mod', 'delta_b_mod', 'delta_w_in', 'delta_q_norm_g', 'delta_w_uq', 'delta_kv_norm_g', 'delta_w_ukv', 'delta_conv_w', 'delta_w_out', 'delta_w_mlp1', 'delta_w_mlp2', 'delta_final_norm_g', 'new_m_c_ctx', 'new_m_w_mod', 'new_m_b_mod', 'new_m_w_in', 'new_m_q_norm_g', 'new_m_w_uq', 'new_m_kv_norm_g', 'new_m_w_ukv', 'new_m_conv_w', 'new_m_w_out', 'new_m_w_mlp1', 'new_m_w_mlp2', 'new_m_final_norm_g', 'new_v_c_ctx', 'new_v_w_mod', 'new_v_b_mod', 'new_v_w_in', 'new_v_q_norm_g', 'new_v_w_uq', 'new_v_kv_norm_g', 'new_v_w_ukv', 'new_v_conv_w', 'new_v_w_out', 'new_v_w_mlp1', 'new_v_w_mlp2', 'new_v_final_norm_g']
TWIN_LEAF_KINDS = {'loss': 'loss', 'grad_x': 'grad_x', 'grad_c_ctx': 'grad_w', 'grad_w_mod': 'grad_w', 'grad_b_mod': 'grad_w', 'grad_w_in': 'grad_w', 'grad_q_norm_g': 'grad_w', 'grad_w_uq': 'grad_w', 'grad_kv_norm_g': 'grad_w', 'grad_w_ukv': 'grad_w', 'grad_conv_w': 'grad_w', 'grad_w_out': 'grad_w', 'grad_w_mlp1': 'grad_w', 'grad_w_mlp2': 'grad_w', 'grad_final_norm_g': 'grad_w', 'delta_c_ctx': 'delta_w', 'delta_w_mod': 'delta_w', 'delta_b_mod': 'delta_w', 'delta_w_in': 'delta_w', 'delta_q_norm_g': 'delta_w', 'delta_w_uq': 'delta_w', 'delta_kv_norm_g': 'delta_w', 'delta_w_ukv': 'delta_w', 'delta_conv_w': 'delta_w', 'delta_w_out': 'delta_w', 'delta_w_mlp1': 'delta_w', 'delta_w_mlp2': 'delta_w', 'delta_final_norm_g': 'delta_w', 'new_m_c_ctx': 'new_m', 'new_m_w_mod': 'new_m', 'new_m_b_mod': 'new_m', 'new_m_w_in': 'new_m', 'new_m_q_norm_g': 'new_m', 'new_m_w_uq': 'new_m', 'new_m_kv_norm_g': 'new_m', 'new_m_w_ukv': 'new_m', 'new_m_conv_w': 'new_m', 'new_m_w_out': 'new_m', 'new_m_w_mlp1': 'new_m', 'new_m_w_mlp2': 'new_m', 'new_m_final_norm_g': 'new_m', 'new_v_c_ctx': 'new_v', 'new_v_w_mod': 'new_v', 'new_v_b_mod': 'new_v', 'new_v_w_in': 'new_v', 'new_v_q_norm_g': 'new_v', 'new_v_w_uq': 'new_v', 'new_v_kv_norm_g': 'new_v', 'new_v_w_ukv': 'new_v', 'new_v_conv_w': 'new_v', 'new_v_w_out': 'new_v', 'new_v_w_mlp1': 'new_v', 'new_v_w_mlp2': 'new_v', 'new_v_final_norm_g': 'new_v'}


def _forward(args):
    return _fwd_reference(*[args[k] for k in FWD_PARAMS])


def _output_shape():
    out = _jax.eval_shape(lambda: _forward(_fwd_setup_inputs(0)))
    return out.shape, out.dtype

N_MICROBATCH = 1
ADAM_LR = 0.001
ADAM_B1 = 0.9
ADAM_B2 = 0.999
ADAM_EPS = 1e-08
ADAM_WD = 0.01
ADAM_STEP = 10
PER_EXAMPLE_BATCH_AXIS = {'x': 0, 'c': 0, 'ctx': 0, 'loss_target': 0}
SHARED_INPUTS = []
_WEIGHT_DTYPES = {'c_ctx': _jnp.float32, 'w_mod': _jnp.float32, 'b_mod': _jnp.float32, 'w_in': _jnp.float32, 'q_norm_g': _jnp.float32, 'w_uq': _jnp.float32, 'kv_norm_g': _jnp.float32, 'w_ukv': _jnp.float32, 'conv_w': _jnp.float32, 'w_out': _jnp.float32, 'w_mlp1': _jnp.float32, 'w_mlp2': _jnp.float32, 'final_norm_g': _jnp.float32}
MOMENT_SCALE = {'c_ctx': 2.381705e-02, 'w_mod': 7.099430e-01, 'b_mod': 1.315391e+00, 'w_in': 1.960220e-01, 'q_norm_g': 2.273152e-02, 'w_uq': 1.255901e-02, 'kv_norm_g': 5.075789e-01, 'w_ukv': 1.732253e-01, 'conv_w': 2.397746e-01, 'w_out': 2.007645e-01, 'w_mlp1': 2.225979e-01, 'w_mlp2': 8.147952e-01, 'final_norm_g': 3.512743e+01}


def _to_microbatches(a, axis):
    t = _jnp.moveaxis(a, axis, 0)
    t = t.reshape((N_MICROBATCH, t.shape[0] // N_MICROBATCH) + t.shape[1:])
    return _jnp.moveaxis(t, 1, axis + 1)


def setup_inputs(seed: int = 0) -> dict:
    inp = _fwd_setup_inputs(seed)
    key = _jax.random.fold_in(_jax.random.key(seed), 7919)
    shape, _ = _output_shape()
    out = dict(inp)
    out["loss_target"] = _jax.random.normal(_jax.random.fold_in(key, 0), shape, _jnp.float32)
    for i, name in enumerate(TWIN_WEIGHTS):
        w = inp[name].astype(_jnp.float32)
        if MOMENT_SCALE is None:
            s = _jnp.sqrt(_jnp.mean(_jnp.square(w)) + 1e-30)
        else:
            s = MOMENT_SCALE[name]
        km, kv = _jax.random.split(_jax.random.fold_in(key, i + 1))
        out[name] = w
        out["m_" + name] = s * _jax.random.normal(km, w.shape, _jnp.float32)
        out["v_" + name] = (s * s) * _jax.random.uniform(kv, w.shape, _jnp.float32, 0.5, 1.5)
    if N_MICROBATCH > 1:
        for name, axis in PER_EXAMPLE_BATCH_AXIS.items():
            out[name] = _to_microbatches(out[name], axis)
    return {'x': out['x'], 'c': out['c'], 'ctx': out['ctx'], 'c_ctx': out['c_ctx'], 'w_mod': out['w_mod'], 'b_mod': out['b_mod'], 'w_in': out['w_in'], 'q_norm_g': out['q_norm_g'], 'w_uq': out['w_uq'], 'kv_norm_g': out['kv_norm_g'], 'w_ukv': out['w_ukv'], 'conv_w': out['conv_w'], 'w_out': out['w_out'], 'w_mlp1': out['w_mlp1'], 'w_mlp2': out['w_mlp2'], 'final_norm_g': out['final_norm_g'], 'loss_target': out['loss_target'], 'm_c_ctx': out['m_c_ctx'], 'm_w_mod': out['m_w_mod'], 'm_b_mod': out['m_b_mod'], 'm_w_in': out['m_w_in'], 'm_q_norm_g': out['m_q_norm_g'], 'm_w_uq': out['m_w_uq'], 'm_kv_norm_g': out['m_kv_norm_g'], 'm_w_ukv': out['m_w_ukv'], 'm_conv_w': out['m_conv_w'], 'm_w_out': out['m_w_out'], 'm_w_mlp1': out['m_w_mlp1'], 'm_w_mlp2': out['m_w_mlp2'], 'm_final_norm_g': out['m_final_norm_g'], 'v_c_ctx': out['v_c_ctx'], 'v_w_mod': out['v_w_mod'], 'v_b_mod': out['v_b_mod'], 'v_w_in': out['v_w_in'], 'v_q_norm_g': out['v_q_norm_g'], 'v_w_uq': out['v_w_uq'], 'v_kv_norm_g': out['v_kv_norm_g'], 'v_w_ukv': out['v_w_ukv'], 'v_conv_w': out['v_conv_w'], 'v_w_out': out['v_w_out'], 'v_w_mlp1': out['v_w_mlp1'], 'v_w_mlp2': out['v_w_mlp2'], 'v_final_norm_g': out['v_final_norm_g']}


def _loss(weights, diff, rest, loss_target):
    with _jax.named_scope("forward"):
        args = {**rest, TWIN_DIFF_INPUT: diff, **{k: w.astype(_WEIGHT_DTYPES[k]) for k, w in weights.items()}}
        y = _forward(args)
    with _jax.named_scope("loss_head"):
        err = _jnp.square(y.astype(_jnp.float32) - loss_target)
        return 0.5 * _jnp.sum(_jnp.mean(err, axis=-1)) if err.ndim else 0.5 * err


def _adamw(w, g, m, v):
    m = ADAM_B1 * m + (1.0 - ADAM_B1) * g
    v = ADAM_B2 * v + (1.0 - ADAM_B2) * _jnp.square(g)
    m_hat = m / (1.0 - ADAM_B1 ** ADAM_STEP)
    v_hat = v / (1.0 - ADAM_B2 ** ADAM_STEP)
    delta = -ADAM_LR * (m_hat / (_jnp.sqrt(v_hat) + ADAM_EPS) + ADAM_WD * w)
    return delta, m, v


def reference(x, c, ctx, c_ctx, w_mod, b_mod, w_in, q_norm_g, w_uq, kv_norm_g, w_ukv, conv_w, w_out, w_mlp1, w_mlp2, final_norm_g, loss_target, m_c_ctx, m_w_mod, m_b_mod, m_w_in, m_q_norm_g, m_w_uq, m_kv_norm_g, m_w_ukv, m_conv_w, m_w_out, m_w_mlp1, m_w_mlp2, m_final_norm_g, v_c_ctx, v_w_mod, v_b_mod, v_w_in, v_q_norm_g, v_w_uq, v_kv_norm_g, v_w_ukv, v_conv_w, v_w_out, v_w_mlp1, v_w_mlp2, v_final_norm_g):
    given = dict(x=x, c=c, ctx=ctx, c_ctx=c_ctx, w_mod=w_mod, b_mod=b_mod, w_in=w_in, q_norm_g=q_norm_g, w_uq=w_uq, kv_norm_g=kv_norm_g, w_ukv=w_ukv, conv_w=conv_w, w_out=w_out, w_mlp1=w_mlp1, w_mlp2=w_mlp2, final_norm_g=final_norm_g, loss_target=loss_target, m_c_ctx=m_c_ctx, m_w_mod=m_w_mod, m_b_mod=m_b_mod, m_w_in=m_w_in, m_q_norm_g=m_q_norm_g, m_w_uq=m_w_uq, m_kv_norm_g=m_kv_norm_g, m_w_ukv=m_w_ukv, m_conv_w=m_conv_w, m_w_out=m_w_out, m_w_mlp1=m_w_mlp1, m_w_mlp2=m_w_mlp2, m_final_norm_g=m_final_norm_g, v_c_ctx=v_c_ctx, v_w_mod=v_w_mod, v_b_mod=v_b_mod, v_w_in=v_w_in, v_q_norm_g=v_q_norm_g, v_w_uq=v_w_uq, v_kv_norm_g=v_kv_norm_g, v_w_ukv=v_w_ukv, v_conv_w=v_conv_w, v_w_out=v_w_out, v_w_mlp1=v_w_mlp1, v_w_mlp2=v_w_mlp2, v_final_norm_g=v_final_norm_g)
    weights = {n: given[n] for n in TWIN_WEIGHTS}
    shared = {n: given[n] for n in SHARED_INPUTS}
    per_example = {n: given[n] for n in ['x', 'c', 'ctx']}
    grad_fn = _jax.value_and_grad(_loss, argnums=(0, 1))

    def one_microbatch(ex, loss_target):
        ex = dict(ex)
        diff = ex.pop(TWIN_DIFF_INPUT)
        return grad_fn(weights, diff, {**shared, **ex}, loss_target)

    if N_MICROBATCH == 1:
        loss, (grad_w, grad_x) = one_microbatch(per_example, given["loss_target"])
    else:
        def body(carry, xs):
            loss_sum, grad_sum = carry
            l_k, (gw_k, gx_k) = one_microbatch(xs[0], xs[1])
            with _jax.named_scope("update"):
                return (loss_sum + l_k, _jax.tree.map(_jnp.add, grad_sum, gw_k)), gx_k

        init = (_jnp.zeros((), _jnp.float32), _jax.tree.map(_jnp.zeros_like, weights))
        (loss, grad_w), grad_x = _jax.lax.scan(body, init, (per_example, given["loss_target"]))
    with _jax.named_scope("update"):
        delta_w, new_m, new_v = {}, {}, {}
        for n in TWIN_WEIGHTS:
            delta_w[n], new_m[n], new_v[n] = _adamw(weights[n], grad_w[n], given["m_" + n], given["v_" + n])
    return (loss, grad_x, *[grad_w[n] for n in TWIN_WEIGHTS], *[delta_w[n] for n in TWIN_WEIGHTS],
            *[new_m[n] for n in TWIN_WEIGHTS], *[new_v[n] for n in TWIN_WEIGHTS])
```

```python
import functools
import math

import jax
import jax.numpy as jnp
from jax import lax
from jax.experimental import pallas as pl
from jax.experimental.pallas import tpu as pltpu

F32 = jnp.float32
BF16 = jnp.bfloat16
MESH = pl.DeviceIdType.MESH
HIGHEST = lax.Precision.HIGHEST

D_MODEL = 1024
N_HEADS = 8
QK_NOPE = 64
QK_ROPE = 32
QK_DIM = QK_NOPE + QK_ROPE
V_DIM = 64
Q_RANK = 256
KV_RANK = 128
CONV_W = 512
D_FF = 4096
GRID_W = 64
ROPE_THETA = 10000.0
EPS = 1e-6
ATTN_SCALE = 1.0 / math.sqrt(QK_DIM)
HEAD_PAD = 128
Z_COLS = 2048
ROPE_LANE0 = QK_NOPE
N_SHARD = 4
TOK_TILE = 256
FF_CHUNK = 1024

ADAM_LR = 0.001
ADAM_B1 = 0.9
ADAM_B2 = 0.999
ADAM_EPS = 1e-08
ADAM_WD = 0.01
ADAM_STEP = 10

PK_IN, PK_UQ, PK_UKV, PK_PAD, PK_OUT, PK_M1, PK_M2 = 488, 48, 32, 8, 256, 1024, 1024
PK_ROWS = PK_IN + PK_UQ + PK_UKV + PK_PAD + PK_OUT + PK_M1 + PK_M2
PK_HALF = PK_ROWS // 2

VMEM_LIMIT = 56 * 1024 * 1024


def _pos():
    return lax.axis_index("x"), lax.axis_index("y"), lax.axis_index("c")


def _dot(a, b):
    return jnp.dot(a, b, preferred_element_type=F32)


def _dot_nt(a, b):
    return lax.dot_general(a, b, (((1,), (1,)), ((), ())), preferred_element_type=F32)


def _dot_tn(a, b):
    return lax.dot_general(a, b, (((0,), (0,)), ((), ())), preferred_element_type=F32)


def _rope(v, cos, sa, sb):
    return v * cos + pltpu.roll(v, 8, 1) * sa + pltpu.roll(v, HEAD_PAD - 8, 1) * sb


def _unrope(g, cos, sa, sb):
    return g * cos + pltpu.roll(g * sa, HEAD_PAD - 8, 1) + pltpu.roll(g * sb, 8, 1)


def _sigmoid(v):
    return 1.0 / (1.0 + jnp.exp(-v))


def _adamw(w, g, m, v):
    m = ADAM_B1 * m + (1.0 - ADAM_B1) * g
    v = ADAM_B2 * v + (1.0 - ADAM_B2) * (g * g)
    m_hat = m / (1.0 - ADAM_B1 ** ADAM_STEP)
    v_hat = v / (1.0 - ADAM_B2 ** ADAM_STEP)
    delta = -ADAM_LR * (m_hat / (jnp.sqrt(v_hat) + ADAM_EPS) + ADAM_WD * w)
    return delta, m, v


def _shift_rows(u, prev_row, next_row):
    n = u.shape[0]
    rows = lax.broadcasted_iota(jnp.int32, u.shape, 0)
    um1 = jnp.where(rows == 0, prev_row, pltpu.roll(u, 1, 0))
    up1 = jnp.where(rows == n - 1, next_row, pltpu.roll(u, n - 1, 0))
    return um1, up1


def _const_spec(shape):
    nd = len(shape)
    return pl.BlockSpec(shape, lambda *_: (0,) * nd)


def _resident_spec(shape):
    nd = len(shape)
    return pl.BlockSpec(shape, lambda *_: (0,) * nd, pipeline_mode=pl.Buffered(1))


def _peer(r, x, y, c):
    px = 1 - x if r & 4 else x
    py = 1 - y if r & 2 else y
    pc = 1 - c if r & 1 else c
    return (px, py, pc)


def _mod_exchange(c_row, cctx_row, w_mod_sh, b_sh, cw_sh):
    ncol = w_mod_sh.shape[1]

    def body(c_ref, cctx_ref, w_ref, b_ref, cw_ref, c8_ref, m_ref, mine_ref, ssem, rsem, ssem2, rsem2):
        x, y, c = _pos()
        me = 4 * x + 2 * y + c
        j = 2 * x + y
        mine_ref[...] = jnp.zeros(mine_ref.shape, F32)
        mine_ref[0:1, :] = c_ref[...]
        my_rows = pl.ds(pl.multiple_of(8 * me, 8), 8)
        sends = []
        for r in range(1, 8):
            cp = pltpu.make_async_remote_copy(
                src_ref=mine_ref, dst_ref=c8_ref.at[my_rows], send_sem=ssem.at[r - 1], recv_sem=rsem.at[r - 1],
                device_id=_peer(r, x, y, c), device_id_type=MESH)
            cp.start()
            sends.append(cp)
        for cp in sends:
            cp.wait()
        c8_ref[my_rows, :] = mine_ref[...]
        c8_ref[64:72, :] = jnp.zeros((8, D_MODEL), F32)
        c8_ref[64:65, :] = cctx_ref[...]
        cv = c8_ref[...]
        s = cv * _sigmoid(cv)
        m = jnp.dot(s, w_ref[...], precision=HIGHEST, preferred_element_type=F32) + b_ref[...]
        m_ref[j, 0:72, :] = m
        m_ref[j, 72:80, :] = jnp.zeros((8, ncol), F32)
        m_ref[j, 72:80, 0:128] = cw_ref[...]
        sends2 = []
        for k, r in enumerate((4, 2, 6)):
            cp = pltpu.make_async_remote_copy(
                src_ref=m_ref.at[j], dst_ref=m_ref.at[j], send_sem=ssem2.at[k], recv_sem=rsem2.at[k],
                device_id=_peer(r, x, y, c), device_id_type=MESH)
            cp.start()
            sends2.append(cp)
        for cp in sends2:
            cp.wait()

    vm = pl.BlockSpec(memory_space=pltpu.VMEM)
    return pl.pallas_call(
        body, name="mod_exchange",
        out_shape=(jax.ShapeDtypeStruct((72, D_MODEL), F32), jax.ShapeDtypeStruct((N_SHARD, 80, ncol), F32)),
        in_specs=[vm] * 5, out_specs=(vm, vm),
        scratch_shapes=[pltpu.VMEM((8, D_MODEL), F32), pltpu.SemaphoreType.DMA((7,)), pltpu.SemaphoreType.DMA((7,)),
                        pltpu.SemaphoreType.DMA((3,)), pltpu.SemaphoreType.DMA((3,))],
        compiler_params=pltpu.CompilerParams(vmem_limit_bytes=VMEM_LIMIT),
    )(c_row, cctx_row, w_mod_sh, b_sh, cw_sh)


def _weight_gather(w_pack):
    def body(w_ref, g_ref, lsem, ssem, rsem, fsend, frecv):
        x, y, c = _pos()
        j = 2 * x + y
        half = pl.ds(pl.multiple_of(c * PK_HALF, 16), PK_HALF)
        ohalf = pl.ds(pl.multiple_of((1 - c) * PK_HALF, 16), PK_HALF)
        chips = [(1 - x, y), (x, 1 - y), (1 - x, 1 - y)]
        own = pltpu.make_async_copy(w_ref, g_ref.at[j], lsem)
        own.start()
        firsts = []
        for k, (px, py) in enumerate(chips):
            cp = pltpu.make_async_remote_copy(
                src_ref=w_ref.at[half], dst_ref=g_ref.at[j, half], send_sem=ssem.at[k], recv_sem=rsem.at[k],
                device_id=(px, py, c), device_id_type=MESH)
            cp.start()
            firsts.append(cp)
        passed = []
        for k, (px, py) in enumerate(chips):
            jk = 2 * px + py
            pltpu.make_async_remote_copy(
                src_ref=w_ref.at[half], dst_ref=g_ref.at[jk, half], send_sem=ssem.at[k], recv_sem=rsem.at[k],
                device_id=(px, py, c), device_id_type=MESH).wait_recv()
            fw = pltpu.make_async_remote_copy(
                src_ref=g_ref.at[jk, half], dst_ref=g_ref.at[jk, half], send_sem=fsend.at[k], recv_sem=frecv.at[k],
                device_id=(x, y, 1 - c), device_id_type=MESH)
            fw.start()
            passed.append(fw)
        for k, (px, py) in enumerate(chips):
            jk = 2 * px + py
            pltpu.make_async_remote_copy(
                src_ref=g_ref.at[jk, ohalf], dst_ref=g_ref.at[jk, ohalf], send_sem=fsend.at[k], recv_sem=frecv.at[k],
                device_id=(x, y, 1 - c), device_id_type=MESH).wait_recv()
        for cp in firsts + passed:
            cp.wait_send()
        own.wait()

    hbm = pl.BlockSpec(memory_space=pl.ANY)
    return pl.pallas_call(
        body, name="weight_gather",
        out_shape=jax.ShapeDtypeStruct((N_SHARD, PK_ROWS, D_MODEL), BF16),
        in_specs=[hbm], out_specs=hbm,
        scratch_shapes=[pltpu.SemaphoreType.DMA, pltpu.SemaphoreType.DMA((3,)), pltpu.SemaphoreType.DMA((3,)),
                        pltpu.SemaphoreType.DMA((3,)), pltpu.SemaphoreType.DMA((3,))],
    )(w_pack)


def _rs_sibling(grads_pack):
    def body(g_ref, own_ref, got_ref, lsem, ssem, rsem):
        x, y, c = _pos()
        half = pl.ds(pl.multiple_of(c * PK_HALF, 16), PK_HALF)
        ohalf = pl.ds(pl.multiple_of((1 - c) * PK_HALF, 16), PK_HALF)
        own = pltpu.make_async_copy(g_ref.at[:, half], own_ref, lsem)
        own.start()
        cp = pltpu.make_async_remote_copy(
            src_ref=g_ref.at[:, ohalf], dst_ref=got_ref, send_sem=ssem, recv_sem=rsem,
            device_id=(x, y, 1 - c), device_id_type=MESH)
        cp.start()
        cp.wait()
        own.wait()

    hbm = pl.BlockSpec(memory_space=pl.ANY)
    shp = jax.ShapeDtypeStruct((N_SHARD, PK_HALF, D_MODEL), BF16)
    return pl.pallas_call(
        body, name="rs_sibling", out_shape=(shp, shp), in_specs=[hbm], out_specs=(hbm, hbm),
        scratch_shapes=[pltpu.SemaphoreType.DMA, pltpu.SemaphoreType.DMA, pltpu.SemaphoreType.DMA],
    )(grads_pack)


def _rs_chips(part):
    def body(p_ref, got_ref, lsem, ssem, rsem):
        x, y, c = _pos()
        j = 2 * x + y
        chips = [(1 - x, y), (x, 1 - y), (1 - x, 1 - y)]
        own = pltpu.make_async_copy(p_ref.at[j], got_ref.at[j], lsem)
        own.start()
        sends = []
        for k, (px, py) in enumerate(chips):
            jk = 2 * px + py
            cp = pltpu.make_async_remote_copy(
                src_ref=p_ref.at[jk], dst_ref=got_ref.at[j], send_sem=ssem.at[k], recv_sem=rsem.at[k],
                device_id=(px, py, c), device_id_type=MESH)
            cp.start()
            sends.append(cp)
        for cp in sends:
            cp.wait()
        own.wait()

    hbm = pl.BlockSpec(memory_space=pl.ANY)
    return pl.pallas_call(
        body, name="rs_chips", out_shape=jax.ShapeDtypeStruct((N_SHARD, PK_HALF, D_MODEL), BF16),
        in_specs=[hbm], out_specs=hbm,
        scratch_shapes=[pltpu.SemaphoreType.DMA, pltpu.SemaphoreType.DMA((3,)), pltpu.SemaphoreType.DMA((3,))],
    )(part)


def _rs_join(g_half):
    def body(h_ref, f_ref, lsem, ssem, rsem):
        x, y, c = _pos()
        half = pl.ds(pl.multiple_of(c * PK_HALF, 8), PK_HALF)
        own = pltpu.make_async_copy(h_ref, f_ref.at[half], lsem)
        own.start()
        cp = pltpu.make_async_remote_copy(
            src_ref=h_ref, dst_ref=f_ref.at[half], send_sem=ssem, recv_sem=rsem,
            device_id=(x, y, 1 - c), device_id_type=MESH)
        cp.start()
        cp.wait()
        own.wait()

    hbm = pl.BlockSpec(memory_space=pl.ANY)
    return pl.pallas_call(
        body, name="rs_join", out_shape=jax.ShapeDtypeStruct((PK_ROWS, D_MODEL), F32),
        in_specs=[hbm], out_specs=hbm,
        scratch_shapes=[pltpu.SemaphoreType.DMA, pltpu.SemaphoreType.DMA, pltpu.SemaphoreType.DMA],
    )(g_half)


def _add_pairs(a, b):
    rb = 480

    def body(a_ref, b_ref, o_ref):
        o_ref[...] = (a_ref[...].astype(F32) + b_ref[...].astype(F32)).astype(BF16)

    spec = pl.BlockSpec((1, rb, D_MODEL), lambda s, r: (s, r, 0))
    return pl.pallas_call(
        body, name="rs_add_pairs", grid=(N_SHARD, PK_HALF // rb), out_shape=jax.ShapeDtypeStruct(a.shape, BF16),
        in_specs=[spec, spec], out_specs=spec,
    )(a, b)


def _add_chips(got):
    rb = 480

    def body(g_ref, o_ref):
        acc = g_ref[0].astype(F32)
        for s in range(1, N_SHARD):
            acc = acc + g_ref[s].astype(F32)
        o_ref[...] = acc

    return pl.pallas_call(
        body, name="rs_add_chips", grid=(PK_HALF // rb,), out_shape=jax.ShapeDtypeStruct((PK_HALF, D_MODEL), F32),
        in_specs=[pl.BlockSpec((N_SHARD, rb, D_MODEL), lambda r: (0, r, 0))],
        out_specs=pl.BlockSpec((rb, D_MODEL), lambda r: (r, 0)),
    )(got)


def _small_exchange(sv, w_mod_sh, cctx, m_cctx, v_cctx, bmod, m_bmod, v_bmod, qg, m_qg, v_qg, kvg, m_kvg, v_kvg,
                    gf, m_gf, v_gf):
    ncol = w_mod_sh.shape[1]

    def body(sv_ref, w_ref, cctx_ref, mcc_ref, vcc_ref, b_ref, mb_ref, vb_ref, qg_ref, mq_ref, vq_ref,
             kg_ref, mk_ref, vk_ref, gf_ref, mgf_ref, vgf_ref,
             all_ref, red_ref, occ_ref, ob_ref, oq_ref, ok_ref, ogf_ref,
             vec_ref, part_ref, ssem, rsem, ssem2, rsem2):
        x, y, c = _pos()
        me = 4 * x + 2 * y + c
        j = 2 * x + y
        sends = []
        for r in range(1, 8):
            cp = pltpu.make_async_remote_copy(
                src_ref=sv_ref, dst_ref=all_ref.at[me], send_sem=ssem.at[r - 1], recv_sem=rsem.at[r - 1],
                device_id=_peer(r, x, y, c), device_id_type=MESH)
            cp.start()
            sends.append(cp)
        for cp in sends:
            cp.wait()
        all_ref[me] = sv_ref[...]
        red = all_ref[0]
        for d in range(1, 8):
            red = red + all_ref[d]
        red_ref[...] = red
        vec_ref[...] = jnp.zeros(vec_ref.shape, F32)

        @pl.when(j == 0)
        def _():
            vec_ref[0:1, 0:1024] = red[6:7, :]
            vec_ref[0:1, 1024:1536] = red[7:8, 0:512]

        @pl.when(j == 1)
        def _():
            vec_ref[0:1, 0:512] = red[7:8, 512:1024]

        part = lax.dot_general(vec_ref[...], w_ref[...], (((1,), (1,)), ((), ())), precision=HIGHEST,
                               preferred_element_type=F32)
        part_ref[j] = part
        sends2 = []
        for k, r in enumerate((4, 2, 6)):
            cp = pltpu.make_async_remote_copy(
                src_ref=part_ref.at[j], dst_ref=part_ref.at[j], send_sem=ssem2.at[k], recv_sem=rsem2.at[k],
                device_id=_peer(r, x, y, c), device_id_type=MESH)
            cp.start()
            sends2.append(cp)
        for cp in sends2:
            cp.wait()
        tot = part_ref[0]
        for s in range(1, N_SHARD):
            tot = tot + part_ref[s]
        cc = cctx_ref[...]
        sg = _sigmoid(cc)
        g_cc = tot[0:1, :] * (sg * (1.0 + cc * (1.0 - sg)))
        d_, m_, v_ = _adamw(cc, g_cc, mcc_ref[...], vcc_ref[...])
        occ_ref[0:1, :] = g_cc
        occ_ref[1:2, :] = d_
        occ_ref[2:3, :] = m_
        occ_ref[3:4, :] = v_
        occ_ref[4:8, :] = jnp.zeros((4, D_MODEL), F32)
        g_b = red[0:6, :]
        pad = jnp.concatenate([red[6:8, :], jnp.zeros((4, D_MODEL), F32)], axis=0)
        g_b = g_b + pad
        d_, m_, v_ = _adamw(b_ref[...], g_b, mb_ref[...], vb_ref[...])
        ob_ref[0] = g_b
        ob_ref[1] = d_
        ob_ref[2] = m_
        ob_ref[3] = v_
        g_q = red[9:10, 0:Q_RANK]
        d_, m_, v_ = _adamw(qg_ref[...], g_q, mq_ref[...], vq_ref[...])
        oq_ref[0:1, :] = g_q
        oq_ref[1:2, :] = d_
        oq_ref[2:3, :] = m_
        oq_ref[3:4, :] = v_
        oq_ref[4:8, :] = jnp.zeros((4, Q_RANK), F32)
        g_k = red[10:11, 0:KV_RANK]
        d_, m_, v_ = _adamw(kg_ref[...], g_k, mk_ref[...], vk_ref[...])
        ok_ref[0:1, :] = g_k
        ok_ref[1:2, :] = d_
        ok_ref[2:3, :] = m_
        ok_ref[3:4, :] = v_
        ok_ref[4:8, :] = jnp.zeros((4, KV_RANK), F32)
        g_f = red[8:9, :]
        d_, m_, v_ = _adamw(gf_ref[...], g_f, mgf_ref[...], vgf_ref[...])
        ogf_ref[0:1, :] = g_f
        ogf_ref[1:2, :] = d_
        ogf_ref[2:3, :] = m_
        ogf_ref[3:4, :] = v_
        ogf_ref[4:8, :] = jnp.zeros((4, D_MODEL), F32)

    vm = pl.BlockSpec(memory_space=pltpu.VMEM)
    out_shape = (
        jax.ShapeDtypeStruct((8, 16, D_MODEL), F32),
        jax.ShapeDtypeStruct((16, D_MODEL), F32),
        jax.ShapeDtypeStruct((8, D_MODEL), F32),
        jax.ShapeDtypeStruct((4, 6, D_MODEL), F32),
        jax.ShapeDtypeStruct((8, Q_RANK), F32),
        jax.ShapeDtypeStruct((8, KV_RANK), F32),
        jax.ShapeDtypeStruct((8, D_MODEL), F32),
    )
    return pl.pallas_call(
        body, name="small_exchange", out_shape=out_shape, in_specs=[vm] * 17, out_specs=tuple([vm] * 7),
        scratch_shapes=[pltpu.VMEM((8, ncol), F32), pltpu.VMEM((N_SHARD, 8, D_MODEL), F32),
                        pltpu.SemaphoreType.DMA((7,)), pltpu.SemaphoreType.DMA((7,)),
                        pltpu.SemaphoreType.DMA((3,)), pltpu.SemaphoreType.DMA((3,))],
        compiler_params=pltpu.CompilerParams(vmem_limit_bytes=VMEM_LIMIT),
    )(sv, w_mod_sh, cctx, m_cctx, v_cctx, bmod, m_bmod, v_bmod, qg, m_qg, v_qg, kvg, m_kvg, v_kvg, gf, m_gf, v_gf)


def _inproj_fwd(x2, ctx2, mod_a, w_in, qg, kvg, w_uq, w_ukv, cos_t, sin_a, sin_b):
    t_lat, t_ctx = x2.shape[0], ctx2.shape[0]
    tm = TOK_TILE
    n_lat = t_lat // tm
    n_all = n_lat + t_ctx // tm
    e_rows = t_lat + t_ctx

    def body(x_ref, ctx_ref, mod_ref, win_ref, qg_ref, kvg_ref, wuq_ref, wukv_ref, cos_ref, sa_ref, sb_ref,
             z_ref, q_ref, k_ref, v_ref):
        i = pl.program_id(0)
        xin = jnp.where(i < n_lat, x_ref[...], ctx_ref[...])
        xn = xin * lax.rsqrt(jnp.mean(xin * xin, axis=-1, keepdims=True) + EPS)
        h1 = (xn * (1.0 + mod_ref[0, 1:2, :]) + mod_ref[0, 0:1, :]).astype(BF16)
        z = _dot(h1, win_ref[...])
        z_ref[...] = z
        cos, sa, sb = cos_ref[...], sa_ref[...], sb_ref[...]
        cq = z[:, 0:Q_RANK]
        cqn = (cq * lax.rsqrt(jnp.mean(cq * cq, axis=-1, keepdims=True) + EPS) * qg_ref[...]).astype(BF16)
        q = _dot(cqn, wuq_ref[...])
        ckv = z[:, Q_RANK:Q_RANK + KV_RANK]
        ckvn = (ckv * lax.rsqrt(jnp.mean(ckv * ckv, axis=-1, keepdims=True) + EPS) * kvg_ref[...]).astype(BF16)
        kv = _dot(ckvn, wukv_ref[...])
        kr = _rope(z[:, Q_RANK + KV_RANK:Q_RANK + KV_RANK + HEAD_PAD], cos, sa, sb)
        for h in range(N_HEADS):
            lo = h * HEAD_PAD
            q_ref[h] = _rope(q[:, lo:lo + HEAD_PAD], cos, sa, sb).astype(BF16)
            k_ref[h] = (kv[:, lo:lo + HEAD_PAD] + kr).astype(BF16)
            v_ref[h] = kv[:, N_HEADS * HEAD_PAD + lo:N_HEADS * HEAD_PAD + lo + HEAD_PAD].astype(BF16)

    row = lambda i: (i, 0)
    head_spec = pl.BlockSpec((N_HEADS, tm, HEAD_PAD), lambda i: (0, i, 0))
    head_shape = jax.ShapeDtypeStruct((N_HEADS, e_rows, HEAD_PAD), BF16)
    return pl.pallas_call(
        body, name="inproj_fwd", grid=(n_all,),
        out_shape=(jax.ShapeDtypeStruct((e_rows, Z_COLS), F32), head_shape, head_shape, head_shape),
        in_specs=[
            pl.BlockSpec((tm, D_MODEL), lambda i: (jnp.minimum(i, n_lat - 1), 0)),
            _const_spec((tm, D_MODEL)),
            pl.BlockSpec((1, 8, D_MODEL), lambda i: (i // n_lat, 0, 0)),
            _const_spec(w_in.shape), _const_spec(qg.shape), _const_spec(kvg.shape),
            _const_spec(w_uq.shape), _const_spec(w_ukv.shape),
            pl.BlockSpec((tm, HEAD_PAD), row), pl.BlockSpec((tm, HEAD_PAD), row), pl.BlockSpec((tm, HEAD_PAD), row),
        ],
        out_specs=(pl.BlockSpec((tm, Z_COLS), row), head_spec, head_spec, head_spec),
        compiler_params=pltpu.CompilerParams(vmem_limit_bytes=VMEM_LIMIT),
    )(x2, ctx2, mod_a, w_in, qg, kvg, w_uq, w_ukv, cos_t, sin_a, sin_b)


def _attn_fwd(q, k, v, t_lat):
    e_rows = k.shape[1]
    tq = 256

    def body(q_ref, k_ref, v_ref, o_ref, lse_ref):
        s = _dot_nt(q_ref[0], k_ref[0]) * ATTN_SCALE
        m = jnp.max(s, axis=-1, keepdims=True)
        p = jnp.exp(s - m)
        l = jnp.sum(p, axis=-1, keepdims=True)
        p = (p * (1.0 / l)).astype(BF16)
        o_ref[0] = _dot(p, v_ref[0]).astype(BF16)
        lse_ref[0] = m + jnp.log(l)

    return pl.pallas_call(
        body, name="attn_fwd", grid=(N_HEADS, t_lat // tq),
        out_shape=(jax.ShapeDtypeStruct((N_HEADS, t_lat, HEAD_PAD), BF16),
                   jax.ShapeDtypeStruct((N_HEADS, t_lat, 1), F32)),
        in_specs=[pl.BlockSpec((1, tq, HEAD_PAD), lambda h, i: (h, i, 0)),
                  pl.BlockSpec((1, e_rows, HEAD_PAD), lambda h, i: (h, 0, 0)),
                  pl.BlockSpec((1, e_rows, HEAD_PAD), lambda h, i: (h, 0, 0))],
        out_specs=(pl.BlockSpec((1, tq, HEAD_PAD), lambda h, i: (h, i, 0)),
                   pl.BlockSpec((1, tq, 1), lambda h, i: (h, i, 0))),
        compiler_params=pltpu.CompilerParams(vmem_limit_bytes=VMEM_LIMIT),
    )(q, k, v)


def _attn_bwd(q, k, v, kt, o, do, lse_row, t_lat):
    e_rows = k.shape[1]
    tq = 256

    def body(q_ref, k_ref, v_ref, kt_ref, o_ref, do_ref, lse_ref, dqt_ref, dk_ref, dv_ref):
        i = pl.program_id(1)
        qb, dob = q_ref[0], do_ref[0]
        prod = o_ref[0].astype(F32) * dob.astype(F32)
        delta = lax.dot_general(jnp.ones((8, HEAD_PAD), F32), prod, (((1,), (1,)), ((), ())), precision=HIGHEST,
                                preferred_element_type=F32)[0:1, :]
        st = _dot_nt(k_ref[0], qb) * ATTN_SCALE
        pt = jnp.exp(st - lse_ref[0])
        dpt = _dot_nt(v_ref[0], dob)
        dst = (pt * (dpt - delta) * ATTN_SCALE).astype(BF16)
        dv_c = _dot(pt.astype(BF16), dob)
        dk_c = _dot(dst, qb)
        dqt_ref[0] = _dot(kt_ref[0], dst)

        @pl.when(i == 0)
        def _():
            dk_ref[0] = dk_c
            dv_ref[0] = dv_c

        @pl.when(i > 0)
        def _():
            dk_ref[0] += dk_c
            dv_ref[0] += dv_c

    qspec = pl.BlockSpec((1, tq, HEAD_PAD), lambda h, i: (h, i, 0))
    kspec = pl.BlockSpec((1, e_rows, HEAD_PAD), lambda h, i: (h, 0, 0))
    return pl.pallas_call(
        body, name="attn_bwd", grid=(N_HEADS, t_lat // tq),
        out_shape=(jax.ShapeDtypeStruct((N_HEADS, HEAD_PAD, t_lat), F32),
                   jax.ShapeDtypeStruct((N_HEADS, e_rows, HEAD_PAD), F32),
                   jax.ShapeDtypeStruct((N_HEADS, e_rows, HEAD_PAD), F32)),
        in_specs=[qspec, kspec, kspec, pl.BlockSpec((1, HEAD_PAD, e_rows), lambda h, i: (h, 0, 0)), qspec, qspec,
                  pl.BlockSpec((1, 1, tq), lambda h, i: (h, 0, i))],
        out_specs=(pl.BlockSpec((1, HEAD_PAD, tq), lambda h, i: (h, 0, i)), kspec, kspec),
        compiler_params=pltpu.CompilerParams(vmem_limit_bytes=VMEM_LIMIT),
    )(q, k, v, kt, o, do, lse_row)


def _halo_specs(tm, col_block):
    per = tm // 8
    prev = pl.BlockSpec((8, CONV_W), lambda i: (jnp.maximum(i * per - 1, 0), col_block))
    nxt = pl.BlockSpec((8, CONV_W), lambda i: ((i + 1) * per, col_block))
    return prev, nxt


def _mlp_fwdbwd(o, z, x2, tgt, mod_b, gf, cw, wo_attn, wo_conv, w1, w2):
    t_lat = x2.shape[0]
    tm = TOK_TILE
    n_lat = t_lat // tm
    n_ff = D_FF // FF_CHUNK

    def body(o_ref, gb_ref, gc_ref, xi_ref, gcp_ref, xip_ref, gcn_ref, xin_ref, cw_ref, woa_ref, woc_ref,
             x_ref, t_ref, mod_ref, gf_ref, w1_ref, w2_ref,
             r_ref, da_ref, h2_ref, dy2_ref, dx1_ref, y1_ref, yv_ref, conv_ref, acc_ref, ra_ref):
        i = pl.program_id(0)

        @pl.when(i == 0)
        def _():
            acc_ref[...] = jnp.zeros(acc_ref.shape, F32)

        g1, sh2, sc2, g2 = mod_ref[0:1, :], mod_ref[1:2, :], mod_ref[2:3, :], mod_ref[3:4, :]
        u = gc_ref[...] * xi_ref[...]
        u_prev = jnp.where(i > 0, gcp_ref[7:8, :] * xip_ref[7:8, :], 0.0)
        u_next = jnp.where(i < n_lat - 1, gcn_ref[0:1, :] * xin_ref[0:1, :], 0.0)
        um1, up1 = _shift_rows(u, u_prev, u_next)
        yv = cw_ref[0:1, :] * um1 + cw_ref[1:2, :] * u + cw_ref[2:3, :] * up1
        yv_ref[...] = yv
        conv = (gb_ref[...] * yv).astype(BF16)
        conv_ref[...] = conv
        y1 = _dot(conv, woc_ref[...])
        for h in range(N_HEADS):
            y1 = y1 + _dot(o_ref[h], woa_ref[h])
        y1_ref[...] = y1
        x1 = x_ref[...] + g1 * y1
        rstd2 = lax.rsqrt(jnp.mean(x1 * x1, axis=-1, keepdims=True) + EPS)
        xn1 = x1 * rstd2
        h2 = (xn1 * (1.0 + sc2) + sh2).astype(BF16)
        h2_ref[...] = h2
        y2 = jnp.zeros((tm, D_MODEL), F32)
        for jj in range(n_ff):
            ra = jnp.maximum(_dot(h2, w1_ref[jj]), 0.0)
            ra_ref[jj] = ra
            r = (ra * ra).astype(BF16)
            r_ref[:, jj * FF_CHUNK:(jj + 1) * FF_CHUNK] = r
            y2 = y2 + _dot(r, w2_ref[jj * FF_CHUNK:(jj + 1) * FF_CHUNK, :])
        x2v = x1 + g2 * y2
        rstd3 = lax.rsqrt(jnp.mean(x2v * x2v, axis=-1, keepdims=True) + EPS)
        xn3 = x2v * rstd3
        gfv = gf_ref[...]
        diff = xn3 * gfv - t_ref[...]
        loss_t = 0.5 * jnp.sum(jnp.sum(diff * diff, axis=-1, keepdims=True), axis=0, keepdims=True) * (1.0 / D_MODEL)
        dy = diff * (1.0 / D_MODEL)
        dxn3 = dy * gfv
        dx2 = rstd3 * (dxn3 - xn3 * jnp.mean(dxn3 * xn3, axis=-1, keepdims=True))
        dy2 = (dx2 * g2).astype(BF16)
        dy2_ref[...] = dy2
        dh2 = jnp.zeros((tm, D_MODEL), F32)
        for jj in range(n_ff):
            dr = _dot_nt(dy2, w2_ref[jj * FF_CHUNK:(jj + 1) * FF_CHUNK, :])
            da = (2.0 * ra_ref[jj] * dr).astype(BF16)
            da_ref[:, jj * FF_CHUNK:(jj + 1) * FF_CHUNK] = da
            dh2 = dh2 + _dot_nt(da, w1_ref[jj])
        dxn1 = dh2 * (1.0 + sc2)
        dx1_ref[...] = dx2 + rstd2 * (dxn1 - xn1 * jnp.mean(dxn1 * xn1, axis=-1, keepdims=True))
        acc_ref[0:1, :] += jnp.sum(dy * xn3, axis=0, keepdims=True)
        acc_ref[1:2, :] += jnp.sum(dx2 * y2, axis=0, keepdims=True)
        acc_ref[2:3, :] += jnp.sum(dh2, axis=0, keepdims=True)
        acc_ref[3:4, :] += jnp.sum(dh2 * xn1, axis=0, keepdims=True)
        acc_ref[4:5, :] += jnp.broadcast_to(loss_t, (1, D_MODEL))

    row = lambda i: (i, 0)
    gcp, gcn = _halo_specs(tm, 2)
    xip, xin = _halo_specs(tm, 3)
    tile = pl.BlockSpec((tm, D_MODEL), row)
    wide = pl.BlockSpec((tm, D_FF), row)
    half = pl.BlockSpec((tm, CONV_W), row)
    return pl.pallas_call(
        body, name="mlp_fwdbwd", grid=(n_lat,),
        out_shape=(jax.ShapeDtypeStruct((t_lat, D_FF), BF16), jax.ShapeDtypeStruct((t_lat, D_FF), BF16),
                   jax.ShapeDtypeStruct((t_lat, D_MODEL), BF16), jax.ShapeDtypeStruct((t_lat, D_MODEL), BF16),
                   jax.ShapeDtypeStruct((t_lat, D_MODEL), F32), jax.ShapeDtypeStruct((t_lat, D_MODEL), F32),
                   jax.ShapeDtypeStruct((t_lat, CONV_W), F32), jax.ShapeDtypeStruct((t_lat, CONV_W), BF16),
                   jax.ShapeDtypeStruct((8, D_MODEL), F32)),
        in_specs=[
            pl.BlockSpec((N_HEADS, tm, HEAD_PAD), lambda i: (0, i, 0)),
            pl.BlockSpec((tm, CONV_W), lambda i: (i, 1)), pl.BlockSpec((tm, CONV_W), lambda i: (i, 2)),
            pl.BlockSpec((tm, CONV_W), lambda i: (i, 3)),
            gcp, xip, gcn, xin,
            _const_spec(cw.shape), _resident_spec(wo_attn.shape), _resident_spec(wo_conv.shape),
            tile, tile, _const_spec(mod_b.shape), _const_spec(gf.shape),
            _resident_spec(w1.shape), _resident_spec(w2.shape),
        ],
        out_specs=(wide, wide, tile, tile, tile, tile, half, half, _const_spec((8, D_MODEL))),
        scratch_shapes=[pltpu.VMEM((n_ff, tm, FF_CHUNK), F32)],
        compiler_params=pltpu.CompilerParams(vmem_limit_bytes=VMEM_LIMIT),
    )(o, z, z, z, z, z, z, z, cw, wo_attn, wo_conv, x2, tgt, mod_b, gf, w1, w2)


def _mid_bwd(dx1, y1, z, yv, mod_b, wo_attn, wo_conv):
    t_lat = dx1.shape[0]
    tm = TOK_TILE

    def body(dx1_ref, y1_ref, gb_ref, yv_ref, mod_ref, woa_ref, woc_ref, dy1_ref, do_ref, dgb_ref, dyv_ref, acc_ref):
        i = pl.program_id(0)

        @pl.when(i == 0)
        def _():
            acc_ref[...] = jnp.zeros(acc_ref.shape, F32)

        dx1v = dx1_ref[...]
        acc_ref[0:1, :] += jnp.sum(dx1v * y1_ref[...], axis=0, keepdims=True)
        dy1 = (dx1v * mod_ref[0:1, :]).astype(BF16)
        dy1_ref[...] = dy1
        for h in range(N_HEADS):
            do_ref[h] = _dot_nt(dy1, woa_ref[h]).astype(BF16)
        dconv = _dot_nt(dy1, woc_ref[...])
        dgb_ref[...] = dconv * yv_ref[...]
        dyv_ref[...] = dconv * gb_ref[...]

    row = lambda i: (i, 0)
    tile = pl.BlockSpec((tm, D_MODEL), row)
    half = pl.BlockSpec((tm, CONV_W), row)
    return pl.pallas_call(
        body, name="mid_bwd", grid=(t_lat // tm,),
        out_shape=(jax.ShapeDtypeStruct((t_lat, D_MODEL), BF16),
                   jax.ShapeDtypeStruct((N_HEADS, t_lat, HEAD_PAD), BF16),
                   jax.ShapeDtypeStruct((t_lat, CONV_W), F32), jax.ShapeDtypeStruct((t_lat, CONV_W), F32),
                   jax.ShapeDtypeStruct((8, D_MODEL), F32)),
        in_specs=[tile, tile, pl.BlockSpec((tm, CONV_W), lambda i: (i, 1)), half, _const_spec(mod_b.shape),
                  _const_spec(wo_attn.shape), _const_spec(wo_conv.shape)],
        out_specs=(tile, pl.BlockSpec((N_HEADS, tm, HEAD_PAD), lambda i: (0, i, 0)), half, half,
                   _const_spec((8, D_MODEL))),
        compiler_params=pltpu.CompilerParams(vmem_limit_bytes=VMEM_LIMIT),
    )(dx1, y1, z, yv, mod_b, wo_attn, wo_conv)


def _inproj_bwd(x2, ctx2, mod_a, z, dyv, dgb, dx1, dqt, dk, dv, cos_t, sin_a, sin_b, w_in, w_uq, w_ukv, qg, kvg, cw):
    t_lat, t_ctx = x2.shape[0], ctx2.shape[0]
    tm = TOK_TILE
    n_lat = t_lat // tm
    n_all = n_lat + t_ctx // tm

    def body(x_ref, ctx_ref, mod_ref, z_ref, gcp_ref, xip_ref, gcn_ref, xin_ref, dyv_ref, dyvp_ref, dyvn_ref,
             dgb_ref, dx1_ref, dqt_ref, dk_ref, dv_ref, cos_ref, sa_ref, sb_ref, win_ref, wuq_ref, wukv_ref,
             qg_ref, kvg_ref, cw_ref, gx_ref, dwin_ref, dwuq_ref, dwukv_ref, acc_ref):
        i = pl.program_id(0)
        lat = i < n_lat

        @pl.when(i == 0)
        def _():
            dwin_ref[...] = jnp.zeros(dwin_ref.shape, F32)
            dwuq_ref[...] = jnp.zeros(dwuq_ref.shape, F32)
            dwukv_ref[...] = jnp.zeros(dwukv_ref.shape, F32)
            acc_ref[...] = jnp.zeros(acc_ref.shape, F32)

        xin = jnp.where(lat, x_ref[...], ctx_ref[...])
        rstd = lax.rsqrt(jnp.mean(xin * xin, axis=-1, keepdims=True) + EPS)
        xn = xin * rstd
        sc = mod_ref[0, 1:2, :]
        h1 = (xn * (1.0 + sc) + mod_ref[0, 0:1, :]).astype(BF16)
        z = z_ref[...]
        cos, sa, sb = cos_ref[...], sa_ref[...], sb_ref[...]
        qgv, kvgv = qg_ref[...], kvg_ref[...]
        cq = z[:, 0:Q_RANK]
        cqh = cq * lax.rsqrt(jnp.mean(cq * cq, axis=-1, keepdims=True) + EPS)
        rq = lax.rsqrt(jnp.mean(cq * cq, axis=-1, keepdims=True) + EPS)
        cqn = (cqh * qgv).astype(BF16)
        parts = []
        for h in range(N_HEADS):
            g = jnp.where(lat, dqt_ref[h].T, 0.0)
            parts.append(_unrope(g, cos, sa, sb))
        dq = jnp.concatenate(parts, axis=1).astype(BF16)
        dcqn = _dot_nt(dq, wuq_ref[...])
        dwuq_ref[...] += _dot_tn(cqn, dq)
        acc_ref[4:5, 0:Q_RANK] += jnp.sum(dcqn * cqh, axis=0, keepdims=True)
        dxn = dcqn * qgv
        dcq = rq * (dxn - cqh * jnp.mean(dxn * cqh, axis=-1, keepdims=True))
        ckv = z[:, Q_RANK:Q_RANK + KV_RANK]
        rk = lax.rsqrt(jnp.mean(ckv * ckv, axis=-1, keepdims=True) + EPS)
        ckvh = ckv * rk
        ckvn = (ckvh * kvgv).astype(BF16)
        dks = [dk_ref[h] for h in range(N_HEADS)]
        dkr = dks[0]
        for h in range(1, N_HEADS):
            dkr = dkr + dks[h]
        dkv = jnp.concatenate(dks + [dv_ref[h] for h in range(N_HEADS)], axis=1).astype(BF16)
        dckvn = _dot_nt(dkv, wukv_ref[...])
        dwukv_ref[...] += _dot_tn(ckvn, dkv)
        acc_ref[5:6, 0:KV_RANK] += jnp.sum(dckvn * ckvh, axis=0, keepdims=True)
        dxn = dckvn * kvgv
        dckv = rk * (dxn - ckvh * jnp.mean(dxn * ckvh, axis=-1, keepdims=True))
        dkr = _unrope(dkr, cos, sa, sb)
        gb, gc, xi = z[:, 512:1024], z[:, 1024:1536], z[:, 1536:2048]
        u = gc * xi
        u_prev = jnp.where(i > 0, gcp_ref[7:8, :] * xip_ref[7:8, :], 0.0)
        u_next = jnp.where(i < n_lat - 1, gcn_ref[0:1, :] * xin_ref[0:1, :], 0.0)
        um1, up1 = _shift_rows(u, u_prev, u_next)
        dyv = jnp.where(lat, dyv_ref[...], 0.0)
        dyv_prev = jnp.where(jnp.logical_and(i > 0, lat), dyvp_ref[7:8, :], 0.0)
        dyv_next = jnp.where(i < n_lat - 1, dyvn_ref[0:1, :], 0.0)
        dyv_m1, dyv_p1 = _shift_rows(dyv, dyv_prev, dyv_next)
        du = cw_ref[0:1, :] * dyv_p1 + cw_ref[1:2, :] * dyv + cw_ref[2:3, :] * dyv_m1
        dgc = du * xi
        dxi = du * gc
        dgb = jnp.where(lat, dgb_ref[...], 0.0)
        acc_ref[6:7, 0:CONV_W] += jnp.sum(dyv * um1, axis=0, keepdims=True)
        acc_ref[7:8, 0:CONV_W] += jnp.sum(dyv * u, axis=0, keepdims=True)
        acc_ref[8:9, 0:CONV_W] += jnp.sum(dyv * up1, axis=0, keepdims=True)
        dz = jnp.concatenate([dcq, dckv, dkr, dgb, dgc, dxi], axis=1).astype(BF16)
        dh1 = _dot_nt(dz, win_ref[...])
        dwin_ref[...] += _dot_tn(h1, dz)
        s_sh = jnp.sum(dh1, axis=0, keepdims=True)
        s_sc = jnp.sum(dh1 * xn, axis=0, keepdims=True)
        zero = jnp.zeros_like(s_sh)
        acc_ref[0:1, :] += jnp.where(lat, s_sh, zero)
        acc_ref[1:2, :] += jnp.where(lat, s_sc, zero)
        acc_ref[2:3, :] += jnp.where(lat, zero, s_sh)
        acc_ref[3:4, :] += jnp.where(lat, zero, s_sc)
        dxn = dh1 * (1.0 + sc)
        dx = rstd * (dxn - xn * jnp.mean(dxn * xn, axis=-1, keepdims=True))

        @pl.when(lat)
        def _():
            gx_ref[...] = dx1_ref[...] + dx

    last = n_lat - 1
    per = tm // 8
    lat_row = lambda i: (jnp.minimum(i, last), 0)
    row = lambda i: (i, 0)
    gcp, gcn = _halo_specs(tm, 2)
    xip, xin = _halo_specs(tm, 3)
    n_halo = t_lat // 8
    dyvp = pl.BlockSpec((8, CONV_W), lambda i: (jnp.clip(i * per - 1, 0, n_halo - 1), 0))
    dyvn = pl.BlockSpec((8, CONV_W), lambda i: (jnp.minimum((i + 1) * per, n_halo - 1), 0))
    gcn = pl.BlockSpec((8, CONV_W), lambda i: (jnp.minimum((i + 1) * per, (t_lat + t_ctx) // 8 - 1), 2))
    xin = pl.BlockSpec((8, CONV_W), lambda i: (jnp.minimum((i + 1) * per, (t_lat + t_ctx) // 8 - 1), 3))
    head_f32 = pl.BlockSpec((N_HEADS, tm, HEAD_PAD), lambda i: (0, i, 0))
    tab = pl.BlockSpec((tm, HEAD_PAD), row)
    return pl.pallas_call(
        body, name="inproj_bwd", grid=(n_all,),
        out_shape=(jax.ShapeDtypeStruct((t_lat, D_MODEL), F32), jax.ShapeDtypeStruct(w_in.shape, F32),
                   jax.ShapeDtypeStruct(w_uq.shape, F32), jax.ShapeDtypeStruct(w_ukv.shape, F32),
                   jax.ShapeDtypeStruct((16, D_MODEL), F32)),
        in_specs=[
            pl.BlockSpec((tm, D_MODEL), lat_row), _const_spec((tm, D_MODEL)),
            pl.BlockSpec((1, 8, D_MODEL), lambda i: (i // n_lat, 0, 0)),
            pl.BlockSpec((tm, Z_COLS), row), gcp, xip, gcn, xin,
            pl.BlockSpec((tm, CONV_W), lat_row), dyvp, dyvn,
            pl.BlockSpec((tm, CONV_W), lat_row), pl.BlockSpec((tm, D_MODEL), lat_row),
            pl.BlockSpec((N_HEADS, HEAD_PAD, tm), lambda i: (0, 0, jnp.minimum(i, last))),
            head_f32, head_f32, tab, tab, tab,
            _const_spec(w_in.shape), _const_spec(w_uq.shape), _const_spec(w_ukv.shape),
            _const_spec(qg.shape), _const_spec(kvg.shape), _const_spec(cw.shape),
        ],
        out_specs=(pl.BlockSpec((tm, D_MODEL), lat_row), _const_spec(w_in.shape), _const_spec(w_uq.shape),
                   _const_spec(w_ukv.shape), _const_spec((16, D_MODEL))),
        compiler_params=pltpu.CompilerParams(vmem_limit_bytes=VMEM_LIMIT),
    )(x2, ctx2, mod_a, z, z, z, z, z, dyv, dyv, dyv, dgb, dx1, dqt, dk, dv, cos_t, sin_a, sin_b, w_in, w_uq, w_ukv,
      qg, kvg, cw)


def _wgrad(a, b, name, bm, bn, out_blocks=None):
    t, m = a.shape
    n = b.shape[1]
    bk = 512
    nk = t // bk

    def body(a_ref, b_ref, o_ref):
        k = pl.program_id(2)
        part = _dot_tn(a_ref[...], b_ref[...])

        @pl.when(k == 0)
        def _():
            o_ref[...] = part

        @pl.when(k > 0)
        def _():
            o_ref[...] += part

    if out_blocks:
        out_shape = jax.ShapeDtypeStruct((n // bn, m, bn), F32)
        out_spec = pl.BlockSpec((None, bm, bn), lambda i, j, k: (j, i, 0))
    else:
        out_shape = jax.ShapeDtypeStruct((m, n), F32)
        out_spec = pl.BlockSpec((bm, bn), lambda i, j, k: (i, j))
    return pl.pallas_call(
        body, name=name, grid=(m // bm, n // bn, nk), out_shape=out_shape,
        in_specs=[pl.BlockSpec((bk, bm), lambda i, j, k: (k, i)), pl.BlockSpec((bk, bn), lambda i, j, k: (k, j))],
        out_specs=out_spec,
        compiler_params=pltpu.CompilerParams(vmem_limit_bytes=VMEM_LIMIT),
    )(a, b)


def _wgrad_heads(o, dy1):
    t = o.shape[1]
    bk = 512

    def body(o_ref, d_ref, w_ref):
        k = pl.program_id(1)
        part = _dot_tn(o_ref[0], d_ref[...])

        @pl.when(k == 0)
        def _():
            w_ref[0] = part

        @pl.when(k > 0)
        def _():
            w_ref[0] += part

    return pl.pallas_call(
        body, name="wgrad_out_attn", grid=(N_HEADS, t // bk),
        out_shape=jax.ShapeDtypeStruct((N_HEADS, HEAD_PAD, D_MODEL), F32),
        in_specs=[pl.BlockSpec((1, bk, HEAD_PAD), lambda h, k: (h, k, 0)),
                  pl.BlockSpec((bk, D_MODEL), lambda h, k: (k, 0))],
        out_specs=pl.BlockSpec((1, HEAD_PAD, D_MODEL), lambda h, k: (h, 0, 0)),
    )(o, dy1)


def _adamw_call(w, g, m, v, name):
    rows, cols = w.shape
    rb = 256 if rows % 256 == 0 else rows

    def body(w_ref, g_ref, m_ref, v_ref, d_ref, nm_ref, nv_ref):
        d_, m_, v_ = _adamw(w_ref[...], g_ref[...], m_ref[...], v_ref[...])
        d_ref[...] = d_
        nm_ref[...] = m_
        nv_ref[...] = v_

    spec = pl.BlockSpec((rb, cols), lambda i: (i, 0))
    shp = jax.ShapeDtypeStruct((rows, cols), F32)
    return pl.pallas_call(
        body, name=name, grid=(rows // rb,), out_shape=(shp, shp, shp),
        in_specs=[spec] * 4, out_specs=(spec, spec, spec),
    )(w, g, m, v)


def _wmod_update(s_t, dm, w, m, v):
    rows, cols = w.shape
    cb = 512

    def body(s_ref, dm_ref, w_ref, m_ref, v_ref, g_ref, d_ref, nm_ref, nv_ref):
        g = jnp.dot(s_ref[...], dm_ref[...], precision=HIGHEST, preferred_element_type=F32)
        d_, m_, v_ = _adamw(w_ref[...], g, m_ref[...], v_ref[...])
        g_ref[...] = g
        d_ref[...] = d_
        nm_ref[...] = m_
        nv_ref[...] = v_

    spec = pl.BlockSpec((rows, cb), lambda i: (0, i))
    shp = jax.ShapeDtypeStruct((rows, cols), F32)
    return pl.pallas_call(
        body, name="wmod_update", grid=(cols // cb,), out_shape=(shp, shp, shp, shp),
        in_specs=[_const_spec(s_t.shape), pl.BlockSpec((16, cb), lambda i: (0, i)), spec, spec, spec],
        out_specs=(spec, spec, spec, spec),
        compiler_params=pltpu.CompilerParams(vmem_limit_bytes=VMEM_LIMIT),
    )(s_t, dm, w, m, v)


def _rope_tables(t_lat, t_ctx):
    t = jnp.arange(t_lat)
    pos = jnp.stack([(t // GRID_W).astype(F32), (t % GRID_W).astype(F32)], axis=1)
    half = QK_ROPE // 4
    freqs = ROPE_THETA ** (-jnp.arange(0, 2 * half, 2, dtype=F32) / (2 * half))
    ang = pos[:, :, None] * freqs[None, None, :]
    cos, sin = jnp.cos(ang), jnp.sin(ang)
    zero = jnp.zeros_like(sin)
    cos32 = jnp.concatenate([cos, cos], axis=2).reshape(t_lat, QK_ROPE)
    sa32 = jnp.concatenate([zero, sin], axis=2).reshape(t_lat, QK_ROPE)
    sb32 = jnp.concatenate([-sin, zero], axis=2).reshape(t_lat, QK_ROPE)

    def widen(tab, fill):
        left = jnp.full((t_lat, ROPE_LANE0), fill, F32)
        right = jnp.full((t_lat, HEAD_PAD - ROPE_LANE0 - QK_ROPE), fill, F32)
        lat = jnp.concatenate([left, tab, right], axis=1)
        return jnp.concatenate([lat, jnp.full((t_ctx, HEAD_PAD), fill, F32)], axis=0)

    return widen(cos32, 1.0), widen(sa32, 0.0), widen(sb32, 0.0)


def _pack_shards(w_in, w_uq, w_ukv, w_out, w1, w2, dtype):
    lead = w_in.shape[:-2]
    parts = [w_in.reshape(lead + (PK_IN, D_MODEL)), w_uq.reshape(lead + (PK_UQ, D_MODEL)),
             w_ukv.reshape(lead + (PK_UKV, D_MODEL)), jnp.zeros(lead + (PK_PAD, D_MODEL), w_in.dtype),
             w_out, w1, w2]
    return jnp.concatenate([p.astype(dtype) for p in parts], axis=-2)


def _unpack_rows(p):
    lead = p.shape[:-2]
    o = 0
    out = []
    for rows, shape in ((PK_IN, (1024, 488)), (PK_UQ, (256, 192)), (PK_UKV, (128, 256)), (PK_PAD, None),
                        (PK_OUT, (256, 1024)), (PK_M1, (1024, 1024)), (PK_M2, (1024, 1024))):
        if shape is not None:
            out.append(p[..., o:o + rows, :].reshape(lead + shape))
        o += rows
    return out


def _cols_from_shards(s):
    return jnp.transpose(s, (1, 0, 2)).reshape(s.shape[1], -1)


def _cols_to_shards(w):
    k, n = w.shape
    return jnp.transpose(w.reshape(k, N_SHARD, n // N_SHARD), (1, 0, 2))


def kernel(x, c, ctx, c_ctx, w_mod, b_mod, w_in, q_norm_g, w_uq, kv_norm_g, w_ukv, conv_w, w_out, w_mlp1, w_mlp2, final_norm_g, loss_target, m_c_ctx, m_w_mod, m_b_mod, m_w_in, m_q_norm_g, m_w_uq, m_kv_norm_g, m_w_ukv, m_conv_w, m_w_out, m_w_mlp1, m_w_mlp2, m_final_norm_g, v_c_ctx, v_w_mod, v_b_mod, v_w_in, v_q_norm_g, v_w_uq, v_kv_norm_g, v_w_ukv, v_conv_w, v_w_out, v_w_mlp1, v_w_mlp2, v_final_norm_g):
    t_lat, t_ctx = x.shape[1], ctx.shape[1]
    assert t_ctx == TOK_TILE and t_lat % TOK_TILE == 0 and t_lat % GRID_W == 0
    mx, my, mc = _pos()
    me = 4 * mx + 2 * my + mc
    j = 2 * mx + my
    ncol = w_mod.shape[2]
    x2, ctx2, tgt = x[0], ctx[0], loss_target[0]
    cctx_row = c_ctx.reshape(1, D_MODEL)

    b_sh = lax.dynamic_slice(b_mod, (0, j * ncol), (1, ncol))
    cw_pad = jnp.zeros((8, 128), F32).at[0:3, :].set(conv_w[0])
    c8, m_all = _mod_exchange(c, cctx_row, w_mod[0], b_sh, cw_pad)
    m_rows = jnp.transpose(m_all[:, 0:72, :], (1, 0, 2)).reshape(72, N_SHARD * ncol)
    mvec = lax.dynamic_slice(m_rows, (8 * me, 0), (1, 6 * D_MODEL)).reshape(6, D_MODEL)
    mctx = m_rows[64].reshape(6, D_MODEL)
    zeros6 = jnp.zeros((6, D_MODEL), F32)
    mod_a = jnp.stack([jnp.concatenate([mvec[0:2], zeros6], axis=0), jnp.concatenate([mctx[0:2], zeros6], axis=0)])
    mod_b = jnp.concatenate([mvec[2:6], jnp.zeros((4, D_MODEL), F32)], axis=0)
    cw_full = jnp.transpose(m_all[:, 72:80, 0:128], (1, 0, 2)).reshape(8, CONV_W)

    w_pack = _pack_shards(w_in[0], w_uq[0], w_ukv[0], w_out[0], w_mlp1[0], w_mlp2[0], BF16)
    gathered = _weight_gather(w_pack)
    g_in, g_uq, g_ukv, g_out, g_m1, g_m2 = _unpack_rows(gathered)
    w_in_f = _cols_from_shards(g_in)
    zc = lambda n: jnp.zeros((D_MODEL, n), BF16)
    w_in_p = jnp.concatenate([w_in_f[:, 0:384], zc(64), w_in_f[:, 384:416], zc(32), w_in_f[:, 416:]], axis=1)
    w_uq_f = _cols_from_shards(g_uq).reshape(Q_RANK, N_HEADS, QK_DIM)
    w_uq_p = jnp.pad(w_uq_f, ((0, 0), (0, 0), (0, HEAD_PAD - QK_DIM))).reshape(Q_RANK, N_HEADS * HEAD_PAD)
    w_ukv_f = _cols_from_shards(g_ukv).reshape(KV_RANK, N_HEADS, QK_NOPE + V_DIM)
    padh = lambda a: jnp.pad(a, ((0, 0), (0, 0), (0, HEAD_PAD - a.shape[2]))).reshape(KV_RANK, N_HEADS * HEAD_PAD)
    w_ukv_p = jnp.concatenate([padh(w_ukv_f[:, :, :QK_NOPE]), padh(w_ukv_f[:, :, QK_NOPE:])], axis=1)
    w_out_f = g_out.reshape(D_MODEL, D_MODEL)
    wo_attn = jnp.pad(w_out_f[0:512].reshape(N_HEADS, V_DIM, D_MODEL), ((0, 0), (0, HEAD_PAD - V_DIM), (0, 0)))
    wo_conv = w_out_f[512:]
    w1 = g_m1
    w2 = g_m2.reshape(D_FF, D_MODEL)

    cos_t, sin_a, sin_b = _rope_tables(t_lat, t_ctx)
    gf_row = final_norm_g.reshape(1, D_MODEL)

    z, q, k, v = _inproj_fwd(x2, ctx2, mod_a, w_in_p, q_norm_g, kv_norm_g, w_uq_p, w_ukv_p, cos_t, sin_a, sin_b)
    o, lse = _attn_fwd(q, k, v, t_lat)
    r, da, h2, dy2, dx1, y1, yv, conv, acc_mlp = _mlp_fwdbwd(o, z, x2, tgt, mod_b, gf_row, cw_full, wo_attn, wo_conv,
                                                               w1, w2)
    dy1, do, dgb, dyv, acc_mid = _mid_bwd(dx1, y1, z, yv, mod_b, wo_attn, wo_conv)
    kt = jnp.transpose(k, (0, 2, 1))
    dqt, dk, dv = _attn_bwd(q, k, v, kt, o, do, lse.reshape(N_HEADS, 1, t_lat), t_lat)
    gx, d_win, d_wuq, d_wukv, acc_in = _inproj_bwd(x2, ctx2, mod_a, z, dyv, dgb, dx1, dqt, dk, dv, cos_t, sin_a, sin_b,
                                                   w_in_p, w_uq_p, w_ukv_p, q_norm_g, kv_norm_g, cw_full)
    d_w1 = _wgrad(h2, da, "wgrad_mlp1", D_MODEL, FF_CHUNK, out_blocks=True)
    d_w2t = _wgrad(dy2, r, "wgrad_mlp2", D_MODEL, FF_CHUNK)
    d_wo_attn = _wgrad_heads(o, dy1)
    d_wo_conv = _wgrad(conv, dy1, "wgrad_out_conv", CONV_W, D_MODEL)

    pad_row = lambda a: jnp.pad(a, ((0, 0), (0, D_MODEL - a.shape[1])))
    sv = jnp.concatenate([
        acc_in[0:2], acc_mid[0:1], acc_mlp[2:4], acc_mlp[1:2],
        acc_in[2:4], acc_mlp[0:1], acc_in[4:5], acc_in[5:6], acc_in[6:9], acc_mlp[4:5],
        jnp.zeros((1, D_MODEL), F32)], axis=0)
    all_sv, red, o_cc, o_b, o_q, o_k, o_gf = _small_exchange(
        sv, w_mod[0], cctx_row, m_c_ctx.reshape(1, D_MODEL), v_c_ctx.reshape(1, D_MODEL),
        b_mod.reshape(6, D_MODEL), m_b_mod.reshape(6, D_MODEL), v_b_mod.reshape(6, D_MODEL),
        q_norm_g, m_q_norm_g, v_q_norm_g, kv_norm_g, m_kv_norm_g, v_kv_norm_g,
        gf_row, m_final_norm_g.reshape(1, D_MODEL), v_final_norm_g.reshape(1, D_MODEL))
    loss = red[14, 0]

    c9 = jnp.concatenate([c8[0::8], jnp.zeros((7, D_MODEL), F32)], axis=0)
    s_t = jnp.transpose(c9 * jax.nn.sigmoid(c9))
    dm_ex = all_sv[:, 0:6, :].reshape(8, 6 * D_MODEL)
    dm_ctx = jnp.concatenate([red[6:8].reshape(1, 2 * D_MODEL), jnp.zeros((1, 4 * D_MODEL), F32)], axis=1)
    dm16 = jnp.concatenate([dm_ex, dm_ctx, jnp.zeros((7, 6 * D_MODEL), F32)], axis=0)
    dm_sh = lax.dynamic_slice(dm16, (0, j * ncol), (16, ncol))
    g_wmod, d_wmod, nm_wmod, nv_wmod = _wmod_update(s_t, dm_sh, w_mod[0], m_w_mod[0], v_w_mod[0])

    g_cw = lax.dynamic_slice(red[11:14, 0:CONV_W], (0, j * 128), (3, 128))
    d_cw, nm_cw, nv_cw = _adamw_call(conv_w[0], g_cw, m_conv_w[0], v_conv_w[0], "adamw_conv")

    d_win_f = jnp.concatenate([d_win[:, 0:384], d_win[:, 448:480], d_win[:, 512:]], axis=1)
    d_wuq_f = d_wuq.reshape(Q_RANK, N_HEADS, HEAD_PAD)[:, :, 0:QK_DIM].reshape(Q_RANK, N_HEADS * QK_DIM)
    d_wukv3 = d_wukv.reshape(KV_RANK, 2, N_HEADS, HEAD_PAD)
    d_wukv_f = jnp.concatenate([d_wukv3[:, 0, :, 0:QK_NOPE], d_wukv3[:, 1, :, 0:V_DIM]], axis=2).reshape(KV_RANK, -1)
    d_wout_f = jnp.concatenate([d_wo_attn[:, 0:V_DIM, :].reshape(512, D_MODEL), d_wo_conv], axis=0)
    d_w2 = jnp.transpose(d_w2t)
    grads_pack = _pack_shards(_cols_to_shards(d_win_f), _cols_to_shards(d_wuq_f), _cols_to_shards(d_wukv_f),
                              d_wout_f.reshape(N_SHARD, 256, D_MODEL), d_w1,
                              d_w2.reshape(N_SHARD, 1024, D_MODEL), BF16)
    own, got = _rs_sibling(grads_pack)
    part = _add_pairs(own, got)
    slots = _rs_chips(part)
    g_half = _add_chips(slots)
    g_full = _rs_join(g_half)
    g_win, g_wuq, g_wukv, g_wout, g_w1, g_w2 = _unpack_rows(g_full)
    upd = {}
    for name, w_, g_, m_, v_ in (("in", w_in, g_win, m_w_in, v_w_in), ("uq", w_uq, g_wuq, m_w_uq, v_w_uq),
                                 ("ukv", w_ukv, g_wukv, m_w_ukv, v_w_ukv), ("out", w_out, g_wout, m_w_out, v_w_out),
                                 ("mlp1", w_mlp1, g_w1, m_w_mlp1, v_w_mlp1), ("mlp2", w_mlp2, g_w2, m_w_mlp2, v_w_mlp2)):
        upd[name] = _adamw_call(w_[0], g_, m_[0], v_[0], "adamw_" + name)

    def four(o4, shape):
        return [o4[r].reshape(shape) for r in range(4)]

    cc4 = four(o_cc, (D_MODEL,))
    b4 = [o_b[r].reshape(1, 6 * D_MODEL) for r in range(4)]
    q4 = four(o_q, (1, Q_RANK))
    k4 = four(o_k, (1, KV_RANK))
    gf4 = four(o_gf, (D_MODEL,))
    big = {"in": g_win, "uq": g_wuq, "ukv": g_wukv, "out": g_wout, "mlp1": g_w1, "mlp2": g_w2}

    def leaf(idx):
        wm = (g_wmod, d_wmod, nm_wmod, nv_wmod)[idx]
        cwv = (g_cw, d_cw, nm_cw, nv_cw)[idx]
        bigv = {n: (big[n] if idx == 0 else upd[n][idx - 1]) for n in big}
        return [cc4[idx], wm[None], b4[idx], bigv["in"][None], q4[idx], bigv["uq"][None], k4[idx], bigv["ukv"][None],
                cwv[None], bigv["out"][None], bigv["mlp1"][None], bigv["mlp2"][None], gf4[idx]]

    return (loss, gx[None], *leaf(0), *leaf(1), *leaf(2), *leaf(3))
```

```python
import functools
import math

import jax
import jax.numpy as jnp
from jax import lax
from jax.experimental import pallas as pl
from jax.experimental.pallas import tpu as pltpu

F32 = jnp.float32
BF16 = jnp.bfloat16
MESH = pl.DeviceIdType.MESH
HIGHEST = lax.Precision.HIGHEST

D_MODEL = 1024
N_HEADS = 8
QK_NOPE = 64
QK_ROPE = 32
QK_DIM = QK_NOPE + QK_ROPE
V_DIM = 64
Q_RANK = 256
KV_RANK = 128
CONV_W = 512
D_FF = 4096
GRID_W = 64
ROPE_THETA = 10000.0
EPS = 1e-6
ATTN_SCALE = 1.0 / math.sqrt(QK_DIM)
HEAD_PAD = 128
Z_COLS = 2048
ROPE_LANE0 = QK_NOPE
N_SHARD = 4
TOK_TILE = 256
FF_CHUNK = 1024

ADAM_LR = 0.001
ADAM_B1 = 0.9
ADAM_B2 = 0.999
ADAM_EPS = 1e-08
ADAM_WD = 0.01
ADAM_STEP = 10

PK_IN, PK_UQ, PK_UKV, PK_PAD, PK_OUT, PK_M1, PK_M2 = 488, 48, 32, 8, 256, 1024, 1024
PK_ROWS = PK_IN + PK_UQ + PK_UKV + PK_PAD + PK_OUT + PK_M1 + PK_M2
PK_HALF = PK_ROWS // 2
GATHER_CHUNKS = 6
LOG2E = 1.4426950408889634

VMEM_LIMIT = 56 * 1024 * 1024


def _pos():
    return lax.axis_index("x"), lax.axis_index("y"), lax.axis_index("c")


def _dot(a, b):
    return jnp.dot(a, b, preferred_element_type=F32)


def _dot_nt(a, b):
    return lax.dot_general(a, b, (((1,), (1,)), ((), ())), preferred_element_type=F32)


def _dot_tn(a, b):
    return lax.dot_general(a, b, (((0,), (0,)), ((), ())), preferred_element_type=F32)


def _rope(v, cos, sa, sb):
    return v * cos + pltpu.roll(v, 8, 1) * sa + pltpu.roll(v, HEAD_PAD - 8, 1) * sb


def _unrope(g, cos, sa, sb):
    return g * cos + pltpu.roll(g * sa, HEAD_PAD - 8, 1) + pltpu.roll(g * sb, 8, 1)


def _sigmoid(v):
    return 1.0 / (1.0 + jnp.exp(-v))


def _adamw(w, g, m, v):
    m = ADAM_B1 * m + (1.0 - ADAM_B1) * g
    v = ADAM_B2 * v + (1.0 - ADAM_B2) * (g * g)
    m_hat = m / (1.0 - ADAM_B1 ** ADAM_STEP)
    v_hat = v / (1.0 - ADAM_B2 ** ADAM_STEP)
    delta = -ADAM_LR * (m_hat / (jnp.sqrt(v_hat) + ADAM_EPS) + ADAM_WD * w)
    return delta, m, v


def _shift_rows(u, prev_row, next_row):
    n = u.shape[0]
    rows = lax.broadcasted_iota(jnp.int32, u.shape, 0)
    um1 = jnp.where(rows == 0, prev_row, pltpu.roll(u, 1, 0))
    up1 = jnp.where(rows == n - 1, next_row, pltpu.roll(u, n - 1, 0))
    return um1, up1


def _const_spec(shape):
    nd = len(shape)
    return pl.BlockSpec(shape, lambda *_: (0,) * nd)


def _resident_spec(shape):
    nd = len(shape)
    return pl.BlockSpec(shape, lambda *_: (0,) * nd, pipeline_mode=pl.Buffered(1))


def _peer(r, x, y, c):
    px = 1 - x if r & 4 else x
    py = 1 - y if r & 2 else y
    pc = 1 - c if r & 1 else c
    return (px, py, pc)


def _mod_exchange(c_row, cctx_row, w_mod_sh, b_sh, cw_sh):
    ncol = w_mod_sh.shape[1]

    def body(c_ref, cctx_ref, w_ref, b_ref, cw_ref, c8_ref, m_ref, mine_ref, ssem, rsem, ssem2, rsem2):
        x, y, c = _pos()
        me = 4 * x + 2 * y + c
        j = 2 * x + y
        mine_ref[...] = jnp.zeros(mine_ref.shape, F32)
        mine_ref[0:1, :] = c_ref[...]
        my_rows = pl.ds(pl.multiple_of(8 * me, 8), 8)
        sends = []
        for r in range(1, 8):
            cp = pltpu.make_async_remote_copy(
                src_ref=mine_ref, dst_ref=c8_ref.at[my_rows], send_sem=ssem.at[r - 1], recv_sem=rsem.at[r - 1],
                device_id=_peer(r, x, y, c), device_id_type=MESH)
            cp.start()
            sends.append(cp)
        for cp in sends:
            cp.wait()
        c8_ref[my_rows, :] = mine_ref[...]
        c8_ref[64:72, :] = jnp.zeros((8, D_MODEL), F32)
        c8_ref[64:65, :] = cctx_ref[...]
        cv = c8_ref[...]
        s = cv * _sigmoid(cv)
        m = jnp.dot(s, w_ref[...], precision=HIGHEST, preferred_element_type=F32) + b_ref[...]
        m_ref[j, 0:72, :] = m
        m_ref[j, 72:80, :] = jnp.zeros((8, ncol), F32)
        m_ref[j, 72:80, 0:128] = cw_ref[...]
        sends2 = []
        for k, r in enumerate((4, 2, 6)):
            cp = pltpu.make_async_remote_copy(
                src_ref=m_ref.at[j], dst_ref=m_ref.at[j], send_sem=ssem2.at[k], recv_sem=rsem2.at[k],
                device_id=_peer(r, x, y, c), device_id_type=MESH)
            cp.start()
            sends2.append(cp)
        for cp in sends2:
            cp.wait()

    vm = pl.BlockSpec(memory_space=pltpu.VMEM)
    return pl.pallas_call(
        body, name="mod_exchange",
        out_shape=(jax.ShapeDtypeStruct((72, D_MODEL), F32), jax.ShapeDtypeStruct((N_SHARD, 80, ncol), F32)),
        in_specs=[vm] * 5, out_specs=(vm, vm),
        scratch_shapes=[pltpu.VMEM((8, D_MODEL), F32), pltpu.SemaphoreType.DMA((7,)), pltpu.SemaphoreType.DMA((7,)),
                        pltpu.SemaphoreType.DMA((3,)), pltpu.SemaphoreType.DMA((3,))],
        compiler_params=pltpu.CompilerParams(vmem_limit_bytes=VMEM_LIMIT),
    )(c_row, cctx_row, w_mod_sh, b_sh, cw_sh)


def _weight_gather(w_pack):
    nch = GATHER_CHUNKS
    cr = PK_HALF // nch

    def body(w_ref, g_ref, lsem, ssem, rsem, fsend, frecv):
        x, y, c = _pos()
        j = 2 * x + y
        chips = [(1 - x, y), (x, 1 - y), (1 - x, 1 - y)]

        def rows(core, q):
            return pl.ds(pl.multiple_of(core * PK_HALF + q * cr, 16), cr)

        def whole(core):
            return pl.ds(pl.multiple_of(core * PK_HALF, 16), PK_HALF)

        for q in range(2 * nch):
            sl = pl.ds(q * cr, cr)
            pltpu.make_async_copy(w_ref.at[sl], g_ref.at[j, sl], lsem).start()
        firsts = []
        for q in range(nch):
            for k, (px, py) in enumerate(chips):
                cp = pltpu.make_async_remote_copy(
                    src_ref=w_ref.at[rows(c, q)], dst_ref=g_ref.at[j, rows(c, q)],
                    send_sem=ssem.at[k * nch + q], recv_sem=rsem.at[k * nch + q],
                    device_id=(px, py, c), device_id_type=MESH)
                cp.start()
                firsts.append(cp)
        passed = []
        for q in range(nch):
            for k, (px, py) in enumerate(chips):
                jk = 2 * px + py
                pltpu.make_async_remote_copy(
                    src_ref=w_ref.at[rows(c, q)], dst_ref=g_ref.at[jk, rows(c, q)],
                    send_sem=ssem.at[k * nch + q], recv_sem=rsem.at[k * nch + q],
                    device_id=(px, py, c), device_id_type=MESH).wait_recv()
                fw = pltpu.make_async_remote_copy(
                    src_ref=g_ref.at[jk, rows(c, q)], dst_ref=g_ref.at[jk, rows(c, q)],
                    send_sem=fsend.at[k], recv_sem=frecv.at[k],
                    device_id=(x, y, 1 - c), device_id_type=MESH)
                fw.start()
                passed.append(fw)
        for k, (px, py) in enumerate(chips):
            jk = 2 * px + py
            pltpu.make_async_remote_copy(
                src_ref=g_ref.at[jk, whole(c)], dst_ref=g_ref.at[jk, whole(1 - c)],
                send_sem=fsend.at[k], recv_sem=frecv.at[k],
                device_id=(x, y, 1 - c), device_id_type=MESH).wait()
        for cp in firsts:
            cp.wait_send()
        pltpu.make_async_copy(w_ref, g_ref.at[j], lsem).wait()

    hbm = pl.BlockSpec(memory_space=pl.ANY)
    return pl.pallas_call(
        body, name="weight_gather",
        out_shape=jax.ShapeDtypeStruct((N_SHARD, PK_ROWS, D_MODEL), BF16),
        in_specs=[hbm], out_specs=hbm,
        scratch_shapes=[pltpu.SemaphoreType.DMA, pltpu.SemaphoreType.DMA((3 * nch,)),
                        pltpu.SemaphoreType.DMA((3 * nch,)), pltpu.SemaphoreType.DMA((3,)),
                        pltpu.SemaphoreType.DMA((3,))],
    )(w_pack)


def _rs_sibling(grads_pack):
    nch = GATHER_CHUNKS
    cr = PK_HALF // nch

    def body(g_ref, own_ref, got_ref, lsem, ssem, rsem):
        x, y, c = _pos()
        half = pl.ds(pl.multiple_of(c * PK_HALF, 16), PK_HALF)
        ohalf = pl.ds(pl.multiple_of((1 - c) * PK_HALF, 16), PK_HALF)
        for s in range(N_SHARD):
            for q in range(nch):
                mine = pl.ds(pl.multiple_of(c * PK_HALF + q * cr, 16), cr)
                theirs = pl.ds(pl.multiple_of((1 - c) * PK_HALF + q * cr, 16), cr)
                dst = pl.ds(q * cr, cr)
                pltpu.make_async_remote_copy(
                    src_ref=g_ref.at[s, theirs], dst_ref=got_ref.at[s, dst], send_sem=ssem, recv_sem=rsem,
                    device_id=(x, y, 1 - c), device_id_type=MESH).start()
                pltpu.make_async_copy(g_ref.at[s, mine], own_ref.at[s, dst], lsem).start()
        pltpu.make_async_remote_copy(
            src_ref=g_ref.at[:, ohalf], dst_ref=got_ref, send_sem=ssem, recv_sem=rsem,
            device_id=(x, y, 1 - c), device_id_type=MESH).wait()
        pltpu.make_async_copy(g_ref.at[:, half], own_ref, lsem).wait()

    hbm = pl.BlockSpec(memory_space=pl.ANY)
    shp = jax.ShapeDtypeStruct((N_SHARD, PK_HALF, D_MODEL), BF16)
    return pl.pallas_call(
        body, name="rs_sibling", out_shape=(shp, shp), in_specs=[hbm], out_specs=(hbm, hbm),
        scratch_shapes=[pltpu.SemaphoreType.DMA, pltpu.SemaphoreType.DMA, pltpu.SemaphoreType.DMA],
    )(grads_pack)


def _rs_chips(part):
    def body(p_ref, got_ref, lsem, ssem, rsem):
        x, y, c = _pos()
        j = 2 * x + y
        chips = [(1 - x, y), (x, 1 - y), (1 - x, 1 - y)]
        nch = GATHER_CHUNKS
        cr = PK_HALF // nch
        for q in range(nch):
            sl = pl.ds(q * cr, cr)
            pltpu.make_async_copy(p_ref.at[j, sl], got_ref.at[j, sl], lsem).start()
            for k, (px, py) in enumerate(chips):
                jk = 2 * px + py
                pltpu.make_async_remote_copy(
                    src_ref=p_ref.at[jk, sl], dst_ref=got_ref.at[j, sl], send_sem=ssem.at[k], recv_sem=rsem.at[k],
                    device_id=(px, py, c), device_id_type=MESH).start()
        for k, (px, py) in enumerate(chips):
            jk = 2 * px + py
            pltpu.make_async_remote_copy(
                src_ref=p_ref.at[jk], dst_ref=got_ref.at[j], send_sem=ssem.at[k], recv_sem=rsem.at[k],
                device_id=(px, py, c), device_id_type=MESH).wait()
        pltpu.make_async_copy(p_ref.at[j], got_ref.at[j], lsem).wait()

    hbm = pl.BlockSpec(memory_space=pl.ANY)
    return pl.pallas_call(
        body, name="rs_chips", out_shape=jax.ShapeDtypeStruct((N_SHARD, PK_HALF, D_MODEL), BF16),
        in_specs=[hbm], out_specs=hbm,
        scratch_shapes=[pltpu.SemaphoreType.DMA, pltpu.SemaphoreType.DMA((3,)), pltpu.SemaphoreType.DMA((3,))],
    )(part)


def _rs_join(g_half):
    def body(h_ref, f_ref, lsem, ssem, rsem):
        x, y, c = _pos()
        half = pl.ds(pl.multiple_of(c * PK_HALF, 8), PK_HALF)
        nch = 2 * GATHER_CHUNKS
        cr = PK_HALF // nch
        for q in range(nch):
            src = pl.ds(q * cr, cr)
            dst = pl.ds(pl.multiple_of(c * PK_HALF + q * cr, 8), cr)
            pltpu.make_async_remote_copy(
                src_ref=h_ref.at[src], dst_ref=f_ref.at[dst], send_sem=ssem, recv_sem=rsem,
                device_id=(x, y, 1 - c), device_id_type=MESH).start()
            pltpu.make_async_copy(h_ref.at[src], f_ref.at[dst], lsem).start()
        pltpu.make_async_remote_copy(
            src_ref=h_ref, dst_ref=f_ref.at[half], send_sem=ssem, recv_sem=rsem,
            device_id=(x, y, 1 - c), device_id_type=MESH).wait()
        pltpu.make_async_copy(h_ref, f_ref.at[half], lsem).wait()

    hbm = pl.BlockSpec(memory_space=pl.ANY)
    return pl.pallas_call(
        body, name="rs_join", out_shape=jax.ShapeDtypeStruct((PK_ROWS, D_MODEL), F32),
        in_specs=[hbm], out_specs=hbm,
        scratch_shapes=[pltpu.SemaphoreType.DMA, pltpu.SemaphoreType.DMA, pltpu.SemaphoreType.DMA],
    )(g_half)


def _add_pairs(a, b):
    rb = 480

    def body(a_ref, b_ref, o_ref):
        o_ref[...] = (a_ref[...].astype(F32) + b_ref[...].astype(F32)).astype(BF16)

    spec = pl.BlockSpec((1, rb, D_MODEL), lambda s, r: (s, r, 0))
    return pl.pallas_call(
        body, name="rs_add_pairs", grid=(N_SHARD, PK_HALF // rb), out_shape=jax.ShapeDtypeStruct(a.shape, BF16),
        in_specs=[spec, spec], out_specs=spec,
    )(a, b)


def _add_chips(got):
    rb = 480

    def body(g_ref, o_ref):
        acc = g_ref[0].astype(F32)
        for s in range(1, N_SHARD):
            acc = acc + g_ref[s].astype(F32)
        o_ref[...] = acc

    return pl.pallas_call(
        body, name="rs_add_chips", grid=(PK_HALF // rb,), out_shape=jax.ShapeDtypeStruct((PK_HALF, D_MODEL), F32),
        in_specs=[pl.BlockSpec((N_SHARD, rb, D_MODEL), lambda r: (0, r, 0))],
        out_specs=pl.BlockSpec((rb, D_MODEL), lambda r: (r, 0)),
    )(got)


def _small_exchange(sv, w_mod_sh, cctx, m_cctx, v_cctx, bmod, m_bmod, v_bmod, qg, m_qg, v_qg, kvg, m_kvg, v_kvg,
                    gf, m_gf, v_gf):
    ncol = w_mod_sh.shape[1]

    def body(sv_ref, w_ref, cctx_ref, mcc_ref, vcc_ref, b_ref, mb_ref, vb_ref, qg_ref, mq_ref, vq_ref,
             kg_ref, mk_ref, vk_ref, gf_ref, mgf_ref, vgf_ref,
             all_ref, red_ref, occ_ref, ob_ref, oq_ref, ok_ref, ogf_ref,
             vec_ref, part_ref, ssem, rsem, ssem2, rsem2):
        x, y, c = _pos()
        me = 4 * x + 2 * y + c
        j = 2 * x + y
        sends = []
        for r in range(1, 8):
            cp = pltpu.make_async_remote_copy(
                src_ref=sv_ref, dst_ref=all_ref.at[me], send_sem=ssem.at[r - 1], recv_sem=rsem.at[r - 1],
                device_id=_peer(r, x, y, c), device_id_type=MESH)
            cp.start()
            sends.append(cp)
        for cp in sends:
            cp.wait()
        all_ref[me] = sv_ref[...]
        red = all_ref[0]
        for d in range(1, 8):
            red = red + all_ref[d]
        red_ref[...] = red
        vec_ref[...] = jnp.zeros(vec_ref.shape, F32)

        @pl.when(j == 0)
        def _():
            vec_ref[0:1, 0:1024] = red[6:7, :]
            vec_ref[0:1, 1024:1536] = red[7:8, 0:512]

        @pl.when(j == 1)
        def _():
            vec_ref[0:1, 0:512] = red[7:8, 512:1024]

        part = lax.dot_general(vec_ref[...], w_ref[...], (((1,), (1,)), ((), ())), precision=HIGHEST,
                               preferred_element_type=F32)
        part_ref[j] = part
        sends2 = []
        for k, r in enumerate((4, 2, 6)):
            cp = pltpu.make_async_remote_copy(
                src_ref=part_ref.at[j], dst_ref=part_ref.at[j], send_sem=ssem2.at[k], recv_sem=rsem2.at[k],
                device_id=_peer(r, x, y, c), device_id_type=MESH)
            cp.start()
            sends2.append(cp)
        for cp in sends2:
            cp.wait()
        tot = part_ref[0]
        for s in range(1, N_SHARD):
            tot = tot + part_ref[s]
        cc = cctx_ref[...]
        sg = _sigmoid(cc)
        g_cc = tot[0:1, :] * (sg * (1.0 + cc * (1.0 - sg)))
        d_, m_, v_ = _adamw(cc, g_cc, mcc_ref[...], vcc_ref[...])
        occ_ref[0:1, :] = g_cc
        occ_ref[1:2, :] = d_
        occ_ref[2:3, :] = m_
        occ_ref[3:4, :] = v_
        occ_ref[4:8, :] = jnp.zeros((4, D_MODEL), F32)
        g_b = red[0:6, :]
        pad = jnp.concatenate([red[6:8, :], jnp.zeros((4, D_MODEL), F32)], axis=0)
        g_b = g_b + pad
        d_, m_, v_ = _adamw(b_ref[...], g_b, mb_ref[...], vb_ref[...])
        ob_ref[0] = g_b
        ob_ref[1] = d_
        ob_ref[2] = m_
        ob_ref[3] = v_
        g_q = red[9:10, 0:Q_RANK]
        d_, m_, v_ = _adamw(qg_ref[...], g_q, mq_ref[...], vq_ref[...])
        oq_ref[0:1, :] = g_q
        oq_ref[1:2, :] = d_
        oq_ref[2:3, :] = m_
        oq_ref[3:4, :] = v_
        oq_ref[4:8, :] = jnp.zeros((4, Q_RANK), F32)
        g_k = red[10:11, 0:KV_RANK]
        d_, m_, v_ = _adamw(kg_ref[...], g_k, mk_ref[...], vk_ref[...])
        ok_ref[0:1, :] = g_k
        ok_ref[1:2, :] = d_
        ok_ref[2:3, :] = m_
        ok_ref[3:4, :] = v_
        ok_ref[4:8, :] = jnp.zeros((4, KV_RANK), F32)
        g_f = red[8:9, :]
        d_, m_, v_ = _adamw(gf_ref[...], g_f, mgf_ref[...], vgf_ref[...])
        ogf_ref[0:1, :] = g_f
        ogf_ref[1:2, :] = d_
        ogf_ref[2:3, :] = m_
        ogf_ref[3:4, :] = v_
        ogf_ref[4:8, :] = jnp.zeros((4, D_MODEL), F32)

    vm = pl.BlockSpec(memory_space=pltpu.VMEM)
    out_shape = (
        jax.ShapeDtypeStruct((8, 16, D_MODEL), F32),
        jax.ShapeDtypeStruct((16, D_MODEL), F32),
        jax.ShapeDtypeStruct((8, D_MODEL), F32),
        jax.ShapeDtypeStruct((4, 6, D_MODEL), F32),
        jax.ShapeDtypeStruct((8, Q_RANK), F32),
        jax.ShapeDtypeStruct((8, KV_RANK), F32),
        jax.ShapeDtypeStruct((8, D_MODEL), F32),
    )
    return pl.pallas_call(
        body, name="small_exchange", out_shape=out_shape, in_specs=[vm] * 17, out_specs=tuple([vm] * 7),
        scratch_shapes=[pltpu.VMEM((8, ncol), F32), pltpu.VMEM((N_SHARD, 8, D_MODEL), F32),
                        pltpu.SemaphoreType.DMA((7,)), pltpu.SemaphoreType.DMA((7,)),
                        pltpu.SemaphoreType.DMA((3,)), pltpu.SemaphoreType.DMA((3,))],
        compiler_params=pltpu.CompilerParams(vmem_limit_bytes=VMEM_LIMIT),
    )(sv, w_mod_sh, cctx, m_cctx, v_cctx, bmod, m_bmod, v_bmod, qg, m_qg, v_qg, kvg, m_kvg, v_kvg, gf, m_gf, v_gf)


def _inproj_fwd(x2, ctx2, mod_a, w_in, qg, kvg, w_uq, w_ukv, cos_t, sin_a, sin_b):
    t_lat, t_ctx = x2.shape[0], ctx2.shape[0]
    tm = TOK_TILE
    n_lat = t_lat // tm
    n_all = n_lat + t_ctx // tm
    e_rows = t_lat + t_ctx

    def body(x_ref, ctx_ref, mod_ref, win_ref, qg_ref, kvg_ref, wuq_ref, wukv_ref, cos_ref, sa_ref, sb_ref,
             z_ref, q_ref, k_ref, v_ref):
        i = pl.program_id(0)
        xin = jnp.where(i < n_lat, x_ref[...], ctx_ref[...])
        xn = xin * lax.rsqrt(jnp.mean(xin * xin, axis=-1, keepdims=True) + EPS)
        h1 = (xn * (1.0 + mod_ref[0, 1:2, :]) + mod_ref[0, 0:1, :]).astype(BF16)
        z = _dot(h1, win_ref[...])
        z_ref[...] = z
        cos, sa, sb = cos_ref[...], sa_ref[...], sb_ref[...]
        cq = z[:, 0:Q_RANK]
        cqn = (cq * lax.rsqrt(jnp.mean(cq * cq, axis=-1, keepdims=True) + EPS) * qg_ref[...]).astype(BF16)
        q = _dot(cqn, wuq_ref[...])
        ckv = z[:, Q_RANK:Q_RANK + KV_RANK]
        ckvn = (ckv * lax.rsqrt(jnp.mean(ckv * ckv, axis=-1, keepdims=True) + EPS) * kvg_ref[...]).astype(BF16)
        kv = _dot(ckvn, wukv_ref[...])
        kr = _rope(z[:, Q_RANK + KV_RANK:Q_RANK + KV_RANK + HEAD_PAD], cos, sa, sb)
        for h in range(N_HEADS):
            lo = h * HEAD_PAD
            q_ref[h] = _rope(q[:, lo:lo + HEAD_PAD], cos, sa, sb).astype(BF16)
            k_ref[h] = (kv[:, lo:lo + HEAD_PAD] + kr).astype(BF16)
            v_ref[h] = kv[:, N_HEADS * HEAD_PAD + lo:N_HEADS * HEAD_PAD + lo + HEAD_PAD].astype(BF16)

    row = lambda i: (i, 0)
    head_spec = pl.BlockSpec((N_HEADS, tm, HEAD_PAD), lambda i: (0, i, 0))
    head_shape = jax.ShapeDtypeStruct((N_HEADS, e_rows, HEAD_PAD), BF16)
    return pl.pallas_call(
        body, name="inproj_fwd", grid=(n_all,),
        out_shape=(jax.ShapeDtypeStruct((e_rows, Z_COLS), F32), head_shape, head_shape, head_shape),
        in_specs=[
            pl.BlockSpec((tm, D_MODEL), lambda i: (jnp.minimum(i, n_lat - 1), 0)),
            _const_spec((tm, D_MODEL)),
            pl.BlockSpec((1, 8, D_MODEL), lambda i: (i // n_lat, 0, 0)),
            _const_spec(w_in.shape), _const_spec(qg.shape), _const_spec(kvg.shape),
            _const_spec(w_uq.shape), _const_spec(w_ukv.shape),
            pl.BlockSpec((tm, HEAD_PAD), row), pl.BlockSpec((tm, HEAD_PAD), row), pl.BlockSpec((tm, HEAD_PAD), row),
        ],
        out_specs=(pl.BlockSpec((tm, Z_COLS), row), head_spec, head_spec, head_spec),
        compiler_params=pltpu.CompilerParams(vmem_limit_bytes=VMEM_LIMIT),
    )(x2, ctx2, mod_a, w_in, qg, kvg, w_uq, w_ukv, cos_t, sin_a, sin_b)


def _attn_fwd(q, k, v, t_lat):
    e_rows = k.shape[1]
    tq = 512

    def body(q_ref, k_ref, v_ref, o_ref, lse_ref):
        s = _dot_nt(q_ref[0], k_ref[0])
        m = jnp.max(s, axis=-1, keepdims=True)
        p = jnp.exp2((s - m) * (ATTN_SCALE * LOG2E))
        l = jnp.sum(p, axis=-1, keepdims=True)
        o_ref[0] = (_dot(p.astype(BF16), v_ref[0]) * (1.0 / l)).astype(BF16)
        lse_ref[0] = (m * ATTN_SCALE + jnp.log(l)) * LOG2E

    return pl.pallas_call(
        body, name="attn_fwd", grid=(N_HEADS, t_lat // tq),
        out_shape=(jax.ShapeDtypeStruct((N_HEADS, t_lat, HEAD_PAD), BF16),
                   jax.ShapeDtypeStruct((N_HEADS, t_lat, 1), F32)),
        in_specs=[pl.BlockSpec((1, tq, HEAD_PAD), lambda h, i: (h, i, 0)),
                  pl.BlockSpec((1, e_rows, HEAD_PAD), lambda h, i: (h, 0, 0)),
                  pl.BlockSpec((1, e_rows, HEAD_PAD), lambda h, i: (h, 0, 0))],
        out_specs=(pl.BlockSpec((1, tq, HEAD_PAD), lambda h, i: (h, i, 0)),
                   pl.BlockSpec((1, tq, 1), lambda h, i: (h, i, 0))),
        compiler_params=pltpu.CompilerParams(vmem_limit_bytes=VMEM_LIMIT),
    )(q, k, v)


def _attn_bwd(q, k, v, kt, o, do, lse_row, t_lat):
    e_rows = k.shape[1]
    tq = 256

    def body(q_ref, k_ref, v_ref, kt_ref, o_ref, do_ref, lse_ref, dqt_ref, dk_ref, dv_ref):
        i = pl.program_id(1)
        qb, dob = q_ref[0], do_ref[0]
        prod = o_ref[0].astype(F32) * dob.astype(F32)
        delta = lax.dot_general(jnp.ones((8, HEAD_PAD), F32), prod, (((1,), (1,)), ((), ())), precision=HIGHEST,
                                preferred_element_type=F32)[0:1, :]
        pt = jnp.exp2(_dot_nt(k_ref[0], qb) * (ATTN_SCALE * LOG2E) - lse_ref[0])
        dpt = _dot_nt(v_ref[0], dob)
        dst = (pt * (dpt - delta) * ATTN_SCALE).astype(BF16)
        dv_c = _dot(pt.astype(BF16), dob)
        dk_c = _dot(dst, qb)
        dqt_ref[0] = _dot(kt_ref[0], dst)

        @pl.when(i == 0)
        def _():
            dk_ref[0] = dk_c
            dv_ref[0] = dv_c

        @pl.when(i > 0)
        def _():
            dk_ref[0] += dk_c
            dv_ref[0] += dv_c

    qspec = pl.BlockSpec((1, tq, HEAD_PAD), lambda h, i: (h, i, 0))
    kspec = pl.BlockSpec((1, e_rows, HEAD_PAD), lambda h, i: (h, 0, 0))
    return pl.pallas_call(
        body, name="attn_bwd", grid=(N_HEADS, t_lat // tq),
        out_shape=(jax.ShapeDtypeStruct((N_HEADS, HEAD_PAD, t_lat), F32),
                   jax.ShapeDtypeStruct((N_HEADS, e_rows, HEAD_PAD), F32),
                   jax.ShapeDtypeStruct((N_HEADS, e_rows, HEAD_PAD), F32)),
        in_specs=[qspec, kspec, kspec, pl.BlockSpec((1, HEAD_PAD, e_rows), lambda h, i: (h, 0, 0)), qspec, qspec,
                  pl.BlockSpec((1, 1, tq), lambda h, i: (h, 0, i))],
        out_specs=(pl.BlockSpec((1, HEAD_PAD, tq), lambda h, i: (h, 0, i)), kspec, kspec),
        compiler_params=pltpu.CompilerParams(vmem_limit_bytes=VMEM_LIMIT),
    )(q, k, v, kt, o, do, lse_row)


def _halo_specs(tm, col_block):
    per = tm // 8
    prev = pl.BlockSpec((8, CONV_W), lambda i: (jnp.maximum(i * per - 1, 0), col_block))
    nxt = pl.BlockSpec((8, CONV_W), lambda i: ((i + 1) * per, col_block))
    return prev, nxt


def _mlp_fwdbwd(o, z, x2, tgt, mod_b, gf, cw, wo_attn, wo_conv, w1, w2):
    t_lat = x2.shape[0]
    tm = TOK_TILE
    n_lat = t_lat // tm
    n_ff = D_FF // FF_CHUNK

    def body(o_ref, gb_ref, gc_ref, xi_ref, gcp_ref, xip_ref, gcn_ref, xin_ref, cw_ref, woa_ref, woc_ref,
             x_ref, t_ref, mod_ref, gf_ref, w1_ref, w2_ref,
             r_ref, da_ref, h2_ref, dy2_ref, dx1_ref, y1_ref, yv_ref, conv_ref, acc_ref, ra_ref):
        i = pl.program_id(0)

        @pl.when(i == 0)
        def _():
            acc_ref[...] = jnp.zeros(acc_ref.shape, F32)

        g1, sh2, sc2, g2 = mod_ref[0:1, :], mod_ref[1:2, :], mod_ref[2:3, :], mod_ref[3:4, :]
        u = gc_ref[...] * xi_ref[...]
        u_prev = jnp.where(i > 0, gcp_ref[7:8, :] * xip_ref[7:8, :], 0.0)
        u_next = jnp.where(i < n_lat - 1, gcn_ref[0:1, :] * xin_ref[0:1, :], 0.0)
        um1, up1 = _shift_rows(u, u_prev, u_next)
        yv = cw_ref[0:1, :] * um1 + cw_ref[1:2, :] * u + cw_ref[2:3, :] * up1
        yv_ref[...] = yv
        conv = (gb_ref[...] * yv).astype(BF16)
        conv_ref[...] = conv
        y1 = _dot(conv, woc_ref[...])
        for h in range(N_HEADS):
            y1 = y1 + _dot(o_ref[h], woa_ref[h])
        y1_ref[...] = y1
        x1 = x_ref[...] + g1 * y1
        rstd2 = lax.rsqrt(jnp.mean(x1 * x1, axis=-1, keepdims=True) + EPS)
        xn1 = x1 * rstd2
        h2 = (xn1 * (1.0 + sc2) + sh2).astype(BF16)
        h2_ref[...] = h2
        y2 = jnp.zeros((tm, D_MODEL), F32)
        for jj in range(n_ff):
            ra = jnp.maximum(_dot(h2, w1_ref[jj]), 0.0)
            ra_ref[jj] = ra
            r = (ra * ra).astype(BF16)
            r_ref[:, jj * FF_CHUNK:(jj + 1) * FF_CHUNK] = r
            y2 = y2 + _dot(r, w2_ref[jj * FF_CHUNK:(jj + 1) * FF_CHUNK, :])
        x2v = x1 + g2 * y2
        rstd3 = lax.rsqrt(jnp.mean(x2v * x2v, axis=-1, keepdims=True) + EPS)
        xn3 = x2v * rstd3
        gfv = gf_ref[...]
        diff = xn3 * gfv - t_ref[...]
        loss_t = 0.5 * jnp.sum(jnp.sum(diff * diff, axis=-1, keepdims=True), axis=0, keepdims=True) * (1.0 / D_MODEL)
        dy = diff * (1.0 / D_MODEL)
        dxn3 = dy * gfv
        dx2 = rstd3 * (dxn3 - xn3 * jnp.mean(dxn3 * xn3, axis=-1, keepdims=True))
        dy2 = (dx2 * g2).astype(BF16)
        dy2_ref[...] = dy2
        dh2 = jnp.zeros((tm, D_MODEL), F32)
        for jj in range(n_ff):
            dr = _dot_nt(dy2, w2_ref[jj * FF_CHUNK:(jj + 1) * FF_CHUNK, :])
            da = (2.0 * ra_ref[jj] * dr).astype(BF16)
            da_ref[:, jj * FF_CHUNK:(jj + 1) * FF_CHUNK] = da
            dh2 = dh2 + _dot_nt(da, w1_ref[jj])
        dxn1 = dh2 * (1.0 + sc2)
        dx1_ref[...] = dx2 + rstd2 * (dxn1 - xn1 * jnp.mean(dxn1 * xn1, axis=-1, keepdims=True))
        acc_ref[0:1, :] += jnp.sum(dy * xn3, axis=0, keepdims=True)
        acc_ref[1:2, :] += jnp.sum(dx2 * y2, axis=0, keepdims=True)
        acc_ref[2:3, :] += jnp.sum(dh2, axis=0, keepdims=True)
        acc_ref[3:4, :] += jnp.sum(dh2 * xn1, axis=0, keepdims=True)
        acc_ref[4:5, :] += jnp.broadcast_to(loss_t, (1, D_MODEL))

    row = lambda i: (i, 0)
    gcp, gcn = _halo_specs(tm, 2)
    xip, xin = _halo_specs(tm, 3)
    tile = pl.BlockSpec((tm, D_MODEL), row)
    wide = pl.BlockSpec((tm, D_FF), row)
    half = pl.BlockSpec((tm, CONV_W), row)
    return pl.pallas_call(
        body, name="mlp_fwdbwd", grid=(n_lat,),
        out_shape=(jax.ShapeDtypeStruct((t_lat, D_FF), BF16), jax.ShapeDtypeStruct((t_lat, D_FF), BF16),
                   jax.ShapeDtypeStruct((t_lat, D_MODEL), BF16), jax.ShapeDtypeStruct((t_lat, D_MODEL), BF16),
                   jax.ShapeDtypeStruct((t_lat, D_MODEL), F32), jax.ShapeDtypeStruct((t_lat, D_MODEL), F32),
                   jax.ShapeDtypeStruct((t_lat, CONV_W), F32), jax.ShapeDtypeStruct((t_lat, CONV_W), BF16),
                   jax.ShapeDtypeStruct((8, D_MODEL), F32)),
        in_specs=[
            pl.BlockSpec((N_HEADS, tm, HEAD_PAD), lambda i: (0, i, 0)),
            pl.BlockSpec((tm, CONV_W), lambda i: (i, 1)), pl.BlockSpec((tm, CONV_W), lambda i: (i, 2)),
            pl.BlockSpec((tm, CONV_W), lambda i: (i, 3)),
            gcp, xip, gcn, xin,
            _const_spec(cw.shape), _resident_spec(wo_attn.shape), _resident_spec(wo_conv.shape),
            tile, tile, _const_spec(mod_b.shape), _const_spec(gf.shape),
            _resident_spec(w1.shape), _resident_spec(w2.shape),
        ],
        out_specs=(wide, wide, tile, tile, tile, tile, half, half, _const_spec((8, D_MODEL))),
        scratch_shapes=[pltpu.VMEM((n_ff, tm, FF_CHUNK), F32)],
        compiler_params=pltpu.CompilerParams(vmem_limit_bytes=VMEM_LIMIT),
    )(o, z, z, z, z, z, z, z, cw, wo_attn, wo_conv, x2, tgt, mod_b, gf, w1, w2)


def _mid_bwd(dx1, y1, z, yv, mod_b, wo_attn, wo_conv):
    t_lat = dx1.shape[0]
    tm = TOK_TILE

    def body(dx1_ref, y1_ref, gb_ref, yv_ref, mod_ref, woa_ref, woc_ref, dy1_ref, do_ref, dgb_ref, dyv_ref, acc_ref):
        i = pl.program_id(0)

        @pl.when(i == 0)
        def _():
            acc_ref[...] = jnp.zeros(acc_ref.shape, F32)

        dx1v = dx1_ref[...]
        acc_ref[0:1, :] += jnp.sum(dx1v * y1_ref[...], axis=0, keepdims=True)
        dy1 = (dx1v * mod_ref[0:1, :]).astype(BF16)
        dy1_ref[...] = dy1
        for h in range(N_HEADS):
            do_ref[h] = _dot_nt(dy1, woa_ref[h]).astype(BF16)
        dconv = _dot_nt(dy1, woc_ref[...])
        dgb_ref[...] = dconv * yv_ref[...]
        dyv_ref[...] = dconv * gb_ref[...]

    row = lambda i: (i, 0)
    tile = pl.BlockSpec((tm, D_MODEL), row)
    half = pl.BlockSpec((tm, CONV_W), row)
    return pl.pallas_call(
        body, name="mid_bwd", grid=(t_lat // tm,),
        out_shape=(jax.ShapeDtypeStruct((t_lat, D_MODEL), BF16),
                   jax.ShapeDtypeStruct((N_HEADS, t_lat, HEAD_PAD), BF16),
                   jax.ShapeDtypeStruct((t_lat, CONV_W), F32), jax.ShapeDtypeStruct((t_lat, CONV_W), F32),
                   jax.ShapeDtypeStruct((8, D_MODEL), F32)),
        in_specs=[tile, tile, pl.BlockSpec((tm, CONV_W), lambda i: (i, 1)), half, _const_spec(mod_b.shape),
                  _const_spec(wo_attn.shape), _const_spec(wo_conv.shape)],
        out_specs=(tile, pl.BlockSpec((N_HEADS, tm, HEAD_PAD), lambda i: (0, i, 0)), half, half,
                   _const_spec((8, D_MODEL))),
        compiler_params=pltpu.CompilerParams(vmem_limit_bytes=VMEM_LIMIT),
    )(dx1, y1, z, yv, mod_b, wo_attn, wo_conv)


def _inproj_bwd(x2, ctx2, mod_a, z, dyv, dgb, dx1, dqt, dk, dv, cos_t, sin_a, sin_b, w_in, w_uq, w_ukv, qg, kvg, cw):
    t_lat, t_ctx = x2.shape[0], ctx2.shape[0]
    tm = TOK_TILE
    n_lat = t_lat // tm
    n_all = n_lat + t_ctx // tm

    def body(x_ref, ctx_ref, mod_ref, z_ref, gcp_ref, xip_ref, gcn_ref, xin_ref, dyv_ref, dyvp_ref, dyvn_ref,
             dgb_ref, dx1_ref, dqt_ref, dk_ref, dv_ref, cos_ref, sa_ref, sb_ref, win_ref, wuq_ref, wukv_ref,
             qg_ref, kvg_ref, cw_ref, gx_ref, dwin_ref, dwuq_ref, dwukv_ref, acc_ref):
        i = pl.program_id(0)
        lat = i < n_lat

        @pl.when(i == 0)
        def _():
            dwin_ref[...] = jnp.zeros(dwin_ref.shape, F32)
            dwuq_ref[...] = jnp.zeros(dwuq_ref.shape, F32)
            dwukv_ref[...] = jnp.zeros(dwukv_ref.shape, F32)
            acc_ref[...] = jnp.zeros(acc_ref.shape, F32)

        xin = jnp.where(lat, x_ref[...], ctx_ref[...])
        rstd = lax.rsqrt(jnp.mean(xin * xin, axis=-1, keepdims=True) + EPS)
        xn = xin * rstd
        sc = mod_ref[0, 1:2, :]
        h1 = (xn * (1.0 + sc) + mod_ref[0, 0:1, :]).astype(BF16)
        z = z_ref[...]
        cos, sa, sb = cos_ref[...], sa_ref[...], sb_ref[...]
        qgv, kvgv = qg_ref[...], kvg_ref[...]
        cq = z[:, 0:Q_RANK]
        cqh = cq * lax.rsqrt(jnp.mean(cq * cq, axis=-1, keepdims=True) + EPS)
        rq = lax.rsqrt(jnp.mean(cq * cq, axis=-1, keepdims=True) + EPS)
        cqn = (cqh * qgv).astype(BF16)
        parts = []
        for h in range(N_HEADS):
            g = jnp.where(lat, dqt_ref[h].T, 0.0)
            parts.append(_unrope(g, cos, sa, sb))
        dq = jnp.concatenate(parts, axis=1).astype(BF16)
        dcqn = _dot_nt(dq, wuq_ref[...])
        dwuq_ref[...] += _dot_tn(cqn, dq)
        acc_ref[4:5, 0:Q_RANK] += jnp.sum(dcqn * cqh, axis=0, keepdims=True)
        dxn = dcqn * qgv
        dcq = rq * (dxn - cqh * jnp.mean(dxn * cqh, axis=-1, keepdims=True))
        ckv = z[:, Q_RANK:Q_RANK + KV_RANK]
        rk = lax.rsqrt(jnp.mean(ckv * ckv, axis=-1, keepdims=True) + EPS)
        ckvh = ckv * rk
        ckvn = (ckvh * kvgv).astype(BF16)
        dks = [dk_ref[h] for h in range(N_HEADS)]
        dkr = dks[0]
        for h in range(1, N_HEADS):
            dkr = dkr + dks[h]
        dkv = jnp.concatenate(dks + [dv_ref[h] for h in range(N_HEADS)], axis=1).astype(BF16)
        dckvn = _dot_nt(dkv, wukv_ref[...])
        dwukv_ref[...] += _dot_tn(ckvn, dkv)
        acc_ref[5:6, 0:KV_RANK] += jnp.sum(dckvn * ckvh, axis=0, keepdims=True)
        dxn = dckvn * kvgv
        dckv = rk * (dxn - ckvh * jnp.mean(dxn * ckvh, axis=-1, keepdims=True))
        dkr = _unrope(dkr, cos, sa, sb)
        gb, gc, xi = z[:, 512:1024], z[:, 1024:1536], z[:, 1536:2048]
        u = gc * xi
        u_prev = jnp.where(i > 0, gcp_ref[7:8, :] * xip_ref[7:8, :], 0.0)
        u_next = jnp.where(i < n_lat - 1, gcn_ref[0:1, :] * xin_ref[0:1, :], 0.0)
        um1, up1 = _shift_rows(u, u_prev, u_next)
        dyv = jnp.where(lat, dyv_ref[...], 0.0)
        dyv_prev = jnp.where(jnp.logical_and(i > 0, lat), dyvp_ref[7:8, :], 0.0)
        dyv_next = jnp.where(i < n_lat - 1, dyvn_ref[0:1, :], 0.0)
        dyv_m1, dyv_p1 = _shift_rows(dyv, dyv_prev, dyv_next)
        du = cw_ref[0:1, :] * dyv_p1 + cw_ref[1:2, :] * dyv + cw_ref[2:3, :] * dyv_m1
        dgc = du * xi
        dxi = du * gc
        dgb = jnp.where(lat, dgb_ref[...], 0.0)
        acc_ref[6:7, 0:CONV_W] += jnp.sum(dyv * um1, axis=0, keepdims=True)
        acc_ref[7:8, 0:CONV_W] += jnp.sum(dyv * u, axis=0, keepdims=True)
        acc_ref[8:9, 0:CONV_W] += jnp.sum(dyv * up1, axis=0, keepdims=True)
        dz = jnp.concatenate([dcq, dckv, dkr, dgb, dgc, dxi], axis=1).astype(BF16)
        dh1 = _dot_nt(dz, win_ref[...])
        dwin_ref[...] += _dot_tn(h1, dz)
        s_sh = jnp.sum(dh1, axis=0, keepdims=True)
        s_sc = jnp.sum(dh1 * xn, axis=0, keepdims=True)
        zero = jnp.zeros_like(s_sh)
        acc_ref[0:1, :] += jnp.where(lat, s_sh, zero)
        acc_ref[1:2, :] += jnp.where(lat, s_sc, zero)
        acc_ref[2:3, :] += jnp.where(lat, zero, s_sh)
        acc_ref[3:4, :] += jnp.where(lat, zero, s_sc)
        dxn = dh1 * (1.0 + sc)
        dx = rstd * (dxn - xn * jnp.mean(dxn * xn, axis=-1, keepdims=True))

        @pl.when(lat)
        def _():
            gx_ref[...] = dx1_ref[...] + dx

    last = n_lat - 1
    per = tm // 8
    lat_row = lambda i: (jnp.minimum(i, last), 0)
    row = lambda i: (i, 0)
    gcp, gcn = _halo_specs(tm, 2)
    xip, xin = _halo_specs(tm, 3)
    n_halo = t_lat // 8
    dyvp = pl.BlockSpec((8, CONV_W), lambda i: (jnp.clip(i * per - 1, 0, n_halo - 1), 0))
    dyvn = pl.BlockSpec((8, CONV_W), lambda i: (jnp.minimum((i + 1) * per, n_halo - 1), 0))
    gcn = pl.BlockSpec((8, CONV_W), lambda i: (jnp.minimum((i + 1) * per, (t_lat + t_ctx) // 8 - 1), 2))
    xin = pl.BlockSpec((8, CONV_W), lambda i: (jnp.minimum((i + 1) * per, (t_lat + t_ctx) // 8 - 1), 3))
    head_f32 = pl.BlockSpec((N_HEADS, tm, HEAD_PAD), lambda i: (0, i, 0))
    tab = pl.BlockSpec((tm, HEAD_PAD), row)
    return pl.pallas_call(
        body, name="inproj_bwd", grid=(n_all,),
        out_shape=(jax.ShapeDtypeStruct((t_lat, D_MODEL), F32), jax.ShapeDtypeStruct(w_in.shape, F32),
                   jax.ShapeDtypeStruct(w_uq.shape, F32), jax.ShapeDtypeStruct(w_ukv.shape, F32),
                   jax.ShapeDtypeStruct((16, D_MODEL), F32)),
        in_specs=[
            pl.BlockSpec((tm, D_MODEL), lat_row), _const_spec((tm, D_MODEL)),
            pl.BlockSpec((1, 8, D_MODEL), lambda i: (i // n_lat, 0, 0)),
            pl.BlockSpec((tm, Z_COLS), row), gcp, xip, gcn, xin,
            pl.BlockSpec((tm, CONV_W), lat_row), dyvp, dyvn,
            pl.BlockSpec((tm, CONV_W), lat_row), pl.BlockSpec((tm, D_MODEL), lat_row),
            pl.BlockSpec((N_HEADS, HEAD_PAD, tm), lambda i: (0, 0, jnp.minimum(i, last))),
            head_f32, head_f32, tab, tab, tab,
            _const_spec(w_in.shape), _const_spec(w_uq.shape), _const_spec(w_ukv.shape),
            _const_spec(qg.shape), _const_spec(kvg.shape), _const_spec(cw.shape),
        ],
        out_specs=(pl.BlockSpec((tm, D_MODEL), lat_row), _const_spec(w_in.shape), _const_spec(w_uq.shape),
                   _const_spec(w_ukv.shape), _const_spec((16, D_MODEL))),
        compiler_params=pltpu.CompilerParams(vmem_limit_bytes=VMEM_LIMIT),
    )(x2, ctx2, mod_a, z, z, z, z, z, dyv, dyv, dyv, dgb, dx1, dqt, dk, dv, cos_t, sin_a, sin_b, w_in, w_uq, w_ukv,
      qg, kvg, cw)


def _wgrad(a, b, name, bm, bn, out_blocks=None):
    t, m = a.shape
    n = b.shape[1]
    bk = 512
    nk = t // bk

    def body(a_ref, b_ref, o_ref):
        k = pl.program_id(2)
        part = _dot_tn(a_ref[...], b_ref[...])

        @pl.when(k == 0)
        def _():
            o_ref[...] = part

        @pl.when(k > 0)
        def _():
            o_ref[...] += part

    if out_blocks:
        out_shape = jax.ShapeDtypeStruct((n // bn, m, bn), F32)
        out_spec = pl.BlockSpec((None, bm, bn), lambda i, j, k: (j, i, 0))
    else:
        out_shape = jax.ShapeDtypeStruct((m, n), F32)
        out_spec = pl.BlockSpec((bm, bn), lambda i, j, k: (i, j))
    return pl.pallas_call(
        body, name=name, grid=(m // bm, n // bn, nk), out_shape=out_shape,
        in_specs=[pl.BlockSpec((bk, bm), lambda i, j, k: (k, i)), pl.BlockSpec((bk, bn), lambda i, j, k: (k, j))],
        out_specs=out_spec,
        compiler_params=pltpu.CompilerParams(vmem_limit_bytes=VMEM_LIMIT),
    )(a, b)


def _wgrad_out(o, conv, dy1):
    t = o.shape[1]
    bk = 512
    rows = N_HEADS * HEAD_PAD + CONV_W

    def body(o_ref, c_ref, d_ref, w_ref):
        k = pl.program_id(0)
        cat = jnp.concatenate([o_ref[h] for h in range(N_HEADS)] + [c_ref[...]], axis=1)
        part = _dot_tn(cat, d_ref[...])

        @pl.when(k == 0)
        def _():
            w_ref[...] = part

        @pl.when(k > 0)
        def _():
            w_ref[...] += part

    return pl.pallas_call(
        body, name="wgrad_out", grid=(t // bk,),
        out_shape=jax.ShapeDtypeStruct((rows, D_MODEL), F32),
        in_specs=[pl.BlockSpec((N_HEADS, bk, HEAD_PAD), lambda k: (0, k, 0)),
                  pl.BlockSpec((bk, CONV_W), lambda k: (k, 0)),
                  pl.BlockSpec((bk, D_MODEL), lambda k: (k, 0))],
        out_specs=_const_spec((rows, D_MODEL)),
        compiler_params=pltpu.CompilerParams(vmem_limit_bytes=VMEM_LIMIT),
    )(o, conv, dy1)


def _adamw_call(w, g, m, v, name):
    rows, cols = w.shape
    rb = 256 if rows % 256 == 0 else rows

    def body(w_ref, g_ref, m_ref, v_ref, d_ref, nm_ref, nv_ref):
        d_, m_, v_ = _adamw(w_ref[...], g_ref[...], m_ref[...], v_ref[...])
        d_ref[...] = d_
        nm_ref[...] = m_
        nv_ref[...] = v_

    spec = pl.BlockSpec((rb, cols), lambda i: (i, 0))
    shp = jax.ShapeDtypeStruct((rows, cols), F32)
    return pl.pallas_call(
        body, name=name, grid=(rows // rb,), out_shape=(shp, shp, shp),
        in_specs=[spec] * 4, out_specs=(spec, spec, spec),
    )(w, g, m, v)


def _wmod_update(s_t, dm, w, m, v):
    rows, cols = w.shape
    cb = 512

    def body(s_ref, dm_ref, w_ref, m_ref, v_ref, g_ref, d_ref, nm_ref, nv_ref):
        g = jnp.dot(s_ref[...], dm_ref[...], precision=HIGHEST, preferred_element_type=F32)
        d_, m_, v_ = _adamw(w_ref[...], g, m_ref[...], v_ref[...])
        g_ref[...] = g
        d_ref[...] = d_
        nm_ref[...] = m_
        nv_ref[...] = v_

    spec = pl.BlockSpec((rows, cb), lambda i: (0, i))
    shp = jax.ShapeDtypeStruct((rows, cols), F32)
    return pl.pallas_call(
        body, name="wmod_update", grid=(cols // cb,), out_shape=(shp, shp, shp, shp),
        in_specs=[_const_spec(s_t.shape), pl.BlockSpec((16, cb), lambda i: (0, i)), spec, spec, spec],
        out_specs=(spec, spec, spec, spec),
        compiler_params=pltpu.CompilerParams(vmem_limit_bytes=VMEM_LIMIT),
    )(s_t, dm, w, m, v)


def _rope_tables(t_lat, t_ctx):
    t = jnp.arange(t_lat)
    pos = jnp.stack([(t // GRID_W).astype(F32), (t % GRID_W).astype(F32)], axis=1)
    half = QK_ROPE // 4
    freqs = ROPE_THETA ** (-jnp.arange(0, 2 * half, 2, dtype=F32) / (2 * half))
    ang = pos[:, :, None] * freqs[None, None, :]
    cos, sin = jnp.cos(ang), jnp.sin(ang)
    zero = jnp.zeros_like(sin)
    cos32 = jnp.concatenate([cos, cos], axis=2).reshape(t_lat, QK_ROPE)
    sa32 = jnp.concatenate([zero, sin], axis=2).reshape(t_lat, QK_ROPE)
    sb32 = jnp.concatenate([-sin, zero], axis=2).reshape(t_lat, QK_ROPE)

    def widen(tab, fill):
        left = jnp.full((t_lat, ROPE_LANE0), fill, F32)
        right = jnp.full((t_lat, HEAD_PAD - ROPE_LANE0 - QK_ROPE), fill, F32)
        lat = jnp.concatenate([left, tab, right], axis=1)
        return jnp.concatenate([lat, jnp.full((t_ctx, HEAD_PAD), fill, F32)], axis=0)

    return widen(cos32, 1.0), widen(sa32, 0.0), widen(sb32, 0.0)


def _pack_shards(w_in, w_uq, w_ukv, w_out, w1, w2, dtype):
    lead = w_in.shape[:-2]
    parts = [w_in.reshape(lead + (PK_IN, D_MODEL)), w_uq.reshape(lead + (PK_UQ, D_MODEL)),
             w_ukv.reshape(lead + (PK_UKV, D_MODEL)), jnp.zeros(lead + (PK_PAD, D_MODEL), w_in.dtype),
             w_out, w1, w2]
    return jnp.concatenate([p.astype(dtype) for p in parts], axis=-2)


def _unpack_rows(p):
    lead = p.shape[:-2]
    o = 0
    out = []
    for rows, shape in ((PK_IN, (1024, 488)), (PK_UQ, (256, 192)), (PK_UKV, (128, 256)), (PK_PAD, None),
                        (PK_OUT, (256, 1024)), (PK_M1, (1024, 1024)), (PK_M2, (1024, 1024))):
        if shape is not None:
            out.append(p[..., o:o + rows, :].reshape(lead + shape))
        o += rows
    return out


def _cols_from_shards(s):
    return jnp.transpose(s, (1, 0, 2)).reshape(s.shape[1], -1)


def _cols_to_shards(w):
    k, n = w.shape
    return jnp.transpose(w.reshape(k, N_SHARD, n // N_SHARD), (1, 0, 2))


def kernel(x, c, ctx, c_ctx, w_mod, b_mod, w_in, q_norm_g, w_uq, kv_norm_g, w_ukv, conv_w, w_out, w_mlp1, w_mlp2, final_norm_g, loss_target, m_c_ctx, m_w_mod, m_b_mod, m_w_in, m_q_norm_g, m_w_uq, m_kv_norm_g, m_w_ukv, m_conv_w, m_w_out, m_w_mlp1, m_w_mlp2, m_final_norm_g, v_c_ctx, v_w_mod, v_b_mod, v_w_in, v_q_norm_g, v_w_uq, v_kv_norm_g, v_w_ukv, v_conv_w, v_w_out, v_w_mlp1, v_w_mlp2, v_final_norm_g):
    t_lat, t_ctx = x.shape[1], ctx.shape[1]
    assert t_ctx == TOK_TILE and t_lat % TOK_TILE == 0 and t_lat % GRID_W == 0
    mx, my, mc = _pos()
    me = 4 * mx + 2 * my + mc
    j = 2 * mx + my
    ncol = w_mod.shape[2]
    x2, ctx2, tgt = x[0], ctx[0], loss_target[0]
    cctx_row = c_ctx.reshape(1, D_MODEL)

    b_sh = lax.dynamic_slice(b_mod, (0, j * ncol), (1, ncol))
    cw_pad = jnp.zeros((8, 128), F32).at[0:3, :].set(conv_w[0])
    c8, m_all = _mod_exchange(c, cctx_row, w_mod[0], b_sh, cw_pad)
    m_rows = jnp.transpose(m_all[:, 0:72, :], (1, 0, 2)).reshape(72, N_SHARD * ncol)
    mvec = lax.dynamic_slice(m_rows, (8 * me, 0), (1, 6 * D_MODEL)).reshape(6, D_MODEL)
    mctx = m_rows[64].reshape(6, D_MODEL)
    zeros6 = jnp.zeros((6, D_MODEL), F32)
    mod_a = jnp.stack([jnp.concatenate([mvec[0:2], zeros6], axis=0), jnp.concatenate([mctx[0:2], zeros6], axis=0)])
    mod_b = jnp.concatenate([mvec[2:6], jnp.zeros((4, D_MODEL), F32)], axis=0)
    cw_full = jnp.transpose(m_all[:, 72:80, 0:128], (1, 0, 2)).reshape(8, CONV_W)

    w_pack = _pack_shards(w_in[0], w_uq[0], w_ukv[0], w_out[0], w_mlp1[0], w_mlp2[0], BF16)
    gathered = _weight_gather(w_pack)
    g_in, g_uq, g_ukv, g_out, g_m1, g_m2 = _unpack_rows(gathered)
    w_in_f = _cols_from_shards(g_in)
    zc = lambda n: jnp.zeros((D_MODEL, n), BF16)
    w_in_p = jnp.concatenate([w_in_f[:, 0:384], zc(64), w_in_f[:, 384:416], zc(32), w_in_f[:, 416:]], axis=1)
    w_uq_f = _cols_from_shards(g_uq).reshape(Q_RANK, N_HEADS, QK_DIM)
    w_uq_p = jnp.pad(w_uq_f, ((0, 0), (0, 0), (0, HEAD_PAD - QK_DIM))).reshape(Q_RANK, N_HEADS * HEAD_PAD)
    w_ukv_f = _cols_from_shards(g_ukv).reshape(KV_RANK, N_HEADS, QK_NOPE + V_DIM)
    padh = lambda a: jnp.pad(a, ((0, 0), (0, 0), (0, HEAD_PAD - a.shape[2]))).reshape(KV_RANK, N_HEADS * HEAD_PAD)
    w_ukv_p = jnp.concatenate([padh(w_ukv_f[:, :, :QK_NOPE]), padh(w_ukv_f[:, :, QK_NOPE:])], axis=1)
    w_out_f = g_out.reshape(D_MODEL, D_MODEL)
    wo_attn = jnp.pad(w_out_f[0:512].reshape(N_HEADS, V_DIM, D_MODEL), ((0, 0), (0, HEAD_PAD - V_DIM), (0, 0)))
    wo_conv = w_out_f[512:]
    w1 = g_m1
    w2 = g_m2.reshape(D_FF, D_MODEL)

    cos_t, sin_a, sin_b = _rope_tables(t_lat, t_ctx)
    gf_row = final_norm_g.reshape(1, D_MODEL)

    z, q, k, v = _inproj_fwd(x2, ctx2, mod_a, w_in_p, q_norm_g, kv_norm_g, w_uq_p, w_ukv_p, cos_t, sin_a, sin_b)
    o, lse = _attn_fwd(q, k, v, t_lat)
    r, da, h2, dy2, dx1, y1, yv, conv, acc_mlp = _mlp_fwdbwd(o, z, x2, tgt, mod_b, gf_row, cw_full, wo_attn, wo_conv,
                                                               w1, w2)
    dy1, do, dgb, dyv, acc_mid = _mid_bwd(dx1, y1, z, yv, mod_b, wo_attn, wo_conv)
    kt = jnp.transpose(k, (0, 2, 1))
    dqt, dk, dv = _attn_bwd(q, k, v, kt, o, do, lse.reshape(N_HEADS, 1, t_lat), t_lat)
    gx, d_win, d_wuq, d_wukv, acc_in = _inproj_bwd(x2, ctx2, mod_a, z, dyv, dgb, dx1, dqt, dk, dv, cos_t, sin_a, sin_b,
                                                   w_in_p, w_uq_p, w_ukv_p, q_norm_g, kv_norm_g, cw_full)
    d_w1 = _wgrad(h2, da, "wgrad_mlp1", D_MODEL, FF_CHUNK, out_blocks=True)
    d_w2t = _wgrad(dy2, r, "wgrad_mlp2", D_MODEL, FF_CHUNK)
    d_wo = _wgrad_out(o, conv, dy1)
    d_wo_attn = d_wo[0:N_HEADS * HEAD_PAD].reshape(N_HEADS, HEAD_PAD, D_MODEL)
    d_wo_conv = d_wo[N_HEADS * HEAD_PAD:]

    pad_row = lambda a: jnp.pad(a, ((0, 0), (0, D_MODEL - a.shape[1])))
    sv = jnp.concatenate([
        acc_in[0:2], acc_mid[0:1], acc_mlp[2:4], acc_mlp[1:2],
        acc_in[2:4], acc_mlp[0:1], acc_in[4:5], acc_in[5:6], acc_in[6:9], acc_mlp[4:5],
        jnp.zeros((1, D_MODEL), F32)], axis=0)
    all_sv, red, o_cc, o_b, o_q, o_k, o_gf = _small_exchange(
        sv, w_mod[0], cctx_row, m_c_ctx.reshape(1, D_MODEL), v_c_ctx.reshape(1, D_MODEL),
        b_mod.reshape(6, D_MODEL), m_b_mod.reshape(6, D_MODEL), v_b_mod.reshape(6, D_MODEL),
        q_norm_g, m_q_norm_g, v_q_norm_g, kv_norm_g, m_kv_norm_g, v_kv_norm_g,
        gf_row, m_final_norm_g.reshape(1, D_MODEL), v_final_norm_g.reshape(1, D_MODEL))
    loss = red[14, 0]

    c9 = jnp.concatenate([c8[0::8], jnp.zeros((7, D_MODEL), F32)], axis=0)
    s_t = jnp.transpose(c9 * jax.nn.sigmoid(c9))
    dm_ex = all_sv[:, 0:6, :].reshape(8, 6 * D_MODEL)
    dm_ctx = jnp.concatenate([red[6:8].reshape(1, 2 * D_MODEL), jnp.zeros((1, 4 * D_MODEL), F32)], axis=1)
    dm16 = jnp.concatenate([dm_ex, dm_ctx, jnp.zeros((7, 6 * D_MODEL), F32)], axis=0)
    dm_sh = lax.dynamic_slice(dm16, (0, j * ncol), (16, ncol))
    g_wmod, d_wmod, nm_wmod, nv_wmod = _wmod_update(s_t, dm_sh, w_mod[0], m_w_mod[0], v_w_mod[0])

    g_cw = lax.dynamic_slice(red[11:14, 0:CONV_W], (0, j * 128), (3, 128))
    d_cw, nm_cw, nv_cw = _adamw_call(conv_w[0], g_cw, m_conv_w[0], v_conv_w[0], "adamw_conv")

    d_win_f = jnp.concatenate([d_win[:, 0:384], d_win[:, 448:480], d_win[:, 512:]], axis=1)
    d_wuq_f = d_wuq.reshape(Q_RANK, N_HEADS, HEAD_PAD)[:, :, 0:QK_DIM].reshape(Q_RANK, N_HEADS * QK_DIM)
    d_wukv3 = d_wukv.reshape(KV_RANK, 2, N_HEADS, HEAD_PAD)
    d_wukv_f = jnp.concatenate([d_wukv3[:, 0, :, 0:QK_NOPE], d_wukv3[:, 1, :, 0:V_DIM]], axis=2).reshape(KV_RANK, -1)
    d_wout_f = jnp.concatenate([d_wo_attn[:, 0:V_DIM, :].reshape(512, D_MODEL), d_wo_conv], axis=0)
    d_w2 = jnp.transpose(d_w2t)
    grads_pack = _pack_shards(_cols_to_shards(d_win_f), _cols_to_shards(d_wuq_f), _cols_to_shards(d_wukv_f),
                              d_wout_f.reshape(N_SHARD, 256, D_MODEL), d_w1,
                              d_w2.reshape(N_SHARD, 1024, D_MODEL), BF16)
    own, got = _rs_sibling(grads_pack)
    part = _add_pairs(own, got)
    slots = _rs_chips(part)
    g_half = _add_chips(slots)
    g_full = _rs_join(g_half)
    g_win, g_wuq, g_wukv, g_wout, g_w1, g_w2 = _unpack_rows(g_full)
    upd = {}
    for name, w_, g_, m_, v_ in (("in", w_in, g_win, m_w_in, v_w_in), ("uq", w_uq, g_wuq, m_w_uq, v_w_uq),
                                 ("ukv", w_ukv, g_wukv, m_w_ukv, v_w_ukv), ("out", w_out, g_wout, m_w_out, v_w_out),
                                 ("mlp1", w_mlp1, g_w1, m_w_mlp1, v_w_mlp1), ("mlp2", w_mlp2, g_w2, m_w_mlp2, v_w_mlp2)):
        upd[name] = _adamw_call(w_[0], g_, m_[0], v_[0], "adamw_" + name)

    def four(o4, shape):
        return [o4[r].reshape(shape) for r in range(4)]

    cc4 = four(o_cc, (D_MODEL,))
    b4 = [o_b[r].reshape(1, 6 * D_MODEL) for r in range(4)]
    q4 = four(o_q, (1, Q_RANK))
    k4 = four(o_k, (1, KV_RANK))
    gf4 = four(o_gf, (D_MODEL,))
    big = {"in": g_win, "uq": g_wuq, "ukv": g_wukv, "out": g_wout, "mlp1": g_w1, "mlp2": g_w2}

    def leaf(idx):
        wm = (g_wmod, d_wmod, nm_wmod, nv_wmod)[idx]
        cwv = (g_cw, d_cw, nm_cw, nv_cw)[idx]
        bigv = {n: (big[n] if idx == 0 else upd[n][idx - 1]) for n in big}
        return [cc4[idx], wm[None], b4[idx], bigv["in"][None], q4[idx], bigv["uq"][None], k4[idx], bigv["ukv"][None],
                cwv[None], bigv["out"][None], bigv["mlp1"][None], bigv["mlp2"][None], gf4[idx]]

    return (loss, gx[None], *leaf(0), *leaf(1), *leaf(2), *leaf(3))
```

```python
import functools
import math

import jax
import jax.numpy as jnp
from jax import lax
from jax.experimental import pallas as pl
from jax.experimental.pallas import tpu as pltpu

F32 = jnp.float32
BF16 = jnp.bfloat16
MESH = pl.DeviceIdType.MESH
HIGHEST = lax.Precision.HIGHEST

D_MODEL = 1024
N_HEADS = 8
QK_NOPE = 64
QK_ROPE = 32
QK_DIM = QK_NOPE + QK_ROPE
V_DIM = 64
Q_RANK = 256
KV_RANK = 128
CONV_W = 512
D_FF = 4096
GRID_W = 64
ROPE_THETA = 10000.0
EPS = 1e-6
ATTN_SCALE = 1.0 / math.sqrt(QK_DIM)
HEAD_PAD = 128
Z_COLS = 2048
ROPE_LANE0 = QK_NOPE
N_SHARD = 4
TOK_TILE = 256
FF_CHUNK = 1024
KEY_CHUNK = 1024

ADAM_LR = 0.001
ADAM_B1 = 0.9
ADAM_B2 = 0.999
ADAM_EPS = 1e-08
ADAM_WD = 0.01
ADAM_STEP = 10

PK_IN, PK_UQ, PK_UKV, PK_PAD, PK_OUT, PK_M1, PK_M2 = 488, 48, 32, 8, 256, 1024, 1024
PK_ROWS = PK_IN + PK_UQ + PK_UKV + PK_PAD + PK_OUT + PK_M1 + PK_M2
PK_HALF = PK_ROWS // 2
GATHER_CHUNKS = 6
LOG2E = 1.4426950408889634

VMEM_LIMIT = 56 * 1024 * 1024
STAGE_VMEM_LIMIT = 32 * 1024 * 1024


def _pos():
    return lax.axis_index("x"), lax.axis_index("y"), lax.axis_index("c")


def _dot(a, b):
    return jnp.dot(a, b, preferred_element_type=F32)


def _dot_nt(a, b):
    return lax.dot_general(a, b, (((1,), (1,)), ((), ())), preferred_element_type=F32)


def _dot_tn(a, b):
    return lax.dot_general(a, b, (((0,), (0,)), ((), ())), preferred_element_type=F32)


def _rope(v, cos, sa, sb):
    return v * cos + pltpu.roll(v, 8, 1) * sa + pltpu.roll(v, HEAD_PAD - 8, 1) * sb


def _unrope(g, cos, sa, sb):
    return g * cos + pltpu.roll(g * sa, HEAD_PAD - 8, 1) + pltpu.roll(g * sb, 8, 1)


def _sigmoid(v):
    return 1.0 / (1.0 + jnp.exp(-v))


def _adamw(w, g, m, v):
    m = ADAM_B1 * m + (1.0 - ADAM_B1) * g
    v = ADAM_B2 * v + (1.0 - ADAM_B2) * (g * g)
    m_hat = m / (1.0 - ADAM_B1 ** ADAM_STEP)
    v_hat = v / (1.0 - ADAM_B2 ** ADAM_STEP)
    delta = -ADAM_LR * (m_hat / (jnp.sqrt(v_hat) + ADAM_EPS) + ADAM_WD * w)
    return delta, m, v


def _shift_rows(u, prev_row, next_row):
    n = u.shape[0]
    rows = lax.broadcasted_iota(jnp.int32, u.shape, 0)
    um1 = jnp.where(rows == 0, prev_row, pltpu.roll(u, 1, 0))
    up1 = jnp.where(rows == n - 1, next_row, pltpu.roll(u, n - 1, 0))
    return um1, up1


def _const_spec(shape):
    nd = len(shape)
    return pl.BlockSpec(shape, lambda *_: (0,) * nd)


def _resident_spec(shape):
    nd = len(shape)
    return pl.BlockSpec(shape, lambda *_: (0,) * nd, pipeline_mode=pl.Buffered(1))


def _peer(r, x, y, c):
    px = 1 - x if r & 4 else x
    py = 1 - y if r & 2 else y
    pc = 1 - c if r & 1 else c
    return (px, py, pc)


def _mod_exchange(c_row, cctx_row, w_mod_sh, b_sh, cw_sh):
    ncol = w_mod_sh.shape[1]

    def body(c_ref, cctx_ref, w_ref, b_ref, cw_ref, c8_ref, m_ref, mine_ref, ssem, rsem, ssem2, rsem2):
        x, y, c = _pos()
        me = 4 * x + 2 * y + c
        j = 2 * x + y
        mine_ref[...] = jnp.zeros(mine_ref.shape, F32)
        mine_ref[0:1, :] = c_ref[...]
        my_rows = pl.ds(pl.multiple_of(8 * me, 8), 8)
        sends = []
        for r in range(1, 8):
            cp = pltpu.make_async_remote_copy(
                src_ref=mine_ref, dst_ref=c8_ref.at[my_rows], send_sem=ssem.at[r - 1], recv_sem=rsem.at[r - 1],
                device_id=_peer(r, x, y, c), device_id_type=MESH)
            cp.start()
            sends.append(cp)
        for cp in sends:
            cp.wait()
        c8_ref[my_rows, :] = mine_ref[...]
        c8_ref[64:72, :] = jnp.zeros((8, D_MODEL), F32)
        c8_ref[64:65, :] = cctx_ref[...]
        cv = c8_ref[...]
        s = cv * _sigmoid(cv)
        m = jnp.dot(s, w_ref[...], precision=HIGHEST, preferred_element_type=F32) + b_ref[...]
        m_ref[j, 0:72, :] = m
        m_ref[j, 72:80, :] = jnp.zeros((8, ncol), F32)
        m_ref[j, 72:80, 0:128] = cw_ref[...]
        sends2 = []
        for k, r in enumerate((4, 2, 6)):
            cp = pltpu.make_async_remote_copy(
                src_ref=m_ref.at[j], dst_ref=m_ref.at[j], send_sem=ssem2.at[k], recv_sem=rsem2.at[k],
                device_id=_peer(r, x, y, c), device_id_type=MESH)
            cp.start()
            sends2.append(cp)
        for cp in sends2:
            cp.wait()

    vm = pl.BlockSpec(memory_space=pltpu.VMEM)
    return pl.pallas_call(
        body, name="mod_exchange",
        out_shape=(jax.ShapeDtypeStruct((72, D_MODEL), F32), jax.ShapeDtypeStruct((N_SHARD, 80, ncol), F32)),
        in_specs=[vm] * 5, out_specs=(vm, vm),
        scratch_shapes=[pltpu.VMEM((8, D_MODEL), F32), pltpu.SemaphoreType.DMA((7,)), pltpu.SemaphoreType.DMA((7,)),
                        pltpu.SemaphoreType.DMA((3,)), pltpu.SemaphoreType.DMA((3,))],
        compiler_params=pltpu.CompilerParams(vmem_limit_bytes=VMEM_LIMIT),
    )(c_row, cctx_row, w_mod_sh, b_sh, cw_sh)


def _weight_gather(w_pack):
    nch = GATHER_CHUNKS
    cr = PK_HALF // nch

    def body(w_ref, g_ref, stage_ref, lsem_in, lsem_out, ssem, rsem, fsend, frecv):
        x, y, c = _pos()
        j = 2 * x + y
        chips = [(1 - x, y), (x, 1 - y), (1 - x, 1 - y)]

        def rows(core, q):
            return pl.ds(pl.multiple_of(core * PK_HALF + q * cr, 16), cr)

        def whole(core):
            return pl.ds(pl.multiple_of(core * PK_HALF, 16), PK_HALF)

        for q in range(2 * nch):
            sl = pl.ds(q * cr, cr)
            pltpu.make_async_copy(w_ref.at[sl], stage_ref.at[sl], lsem_in).start()
        firsts = []
        for q in range(nch):
            for k, (px, py) in enumerate(chips):
                cp = pltpu.make_async_remote_copy(
                    src_ref=w_ref.at[rows(c, q)], dst_ref=g_ref.at[j, rows(c, q)],
                    send_sem=ssem.at[k * nch + q], recv_sem=rsem.at[k * nch + q],
                    device_id=(px, py, c), device_id_type=MESH)
                cp.start()
                firsts.append(cp)
        pltpu.make_async_copy(w_ref, stage_ref, lsem_in).wait()
        for q in range(2 * nch):
            sl = pl.ds(q * cr, cr)
            pltpu.make_async_copy(stage_ref.at[sl], g_ref.at[j, sl], lsem_out).start()
        passed = []
        for q in range(nch):
            for k, (px, py) in enumerate(chips):
                jk = 2 * px + py
                pltpu.make_async_remote_copy(
                    src_ref=w_ref.at[rows(c, q)], dst_ref=g_ref.at[jk, rows(c, q)],
                    send_sem=ssem.at[k * nch + q], recv_sem=rsem.at[k * nch + q],
                    device_id=(px, py, c), device_id_type=MESH).wait_recv()
                fw = pltpu.make_async_remote_copy(
                    src_ref=g_ref.at[jk, rows(c, q)], dst_ref=g_ref.at[jk, rows(c, q)],
                    send_sem=fsend.at[k], recv_sem=frecv.at[k],
                    device_id=(x, y, 1 - c), device_id_type=MESH)
                fw.start()
                passed.append(fw)
        for k, (px, py) in enumerate(chips):
            jk = 2 * px + py
            pltpu.make_async_remote_copy(
                src_ref=g_ref.at[jk, whole(c)], dst_ref=g_ref.at[jk, whole(1 - c)],
                send_sem=fsend.at[k], recv_sem=frecv.at[k],
                device_id=(x, y, 1 - c), device_id_type=MESH).wait()
        for cp in firsts:
            cp.wait_send()
        pltpu.make_async_copy(stage_ref, g_ref.at[j], lsem_out).wait()

    hbm = pl.BlockSpec(memory_space=pl.ANY)
    return pl.pallas_call(
        body, name="weight_gather",
        out_shape=jax.ShapeDtypeStruct((N_SHARD, PK_ROWS, D_MODEL), BF16),
        in_specs=[hbm], out_specs=hbm,
        scratch_shapes=[pltpu.VMEM((PK_ROWS, D_MODEL), BF16), pltpu.SemaphoreType.DMA, pltpu.SemaphoreType.DMA,
                        pltpu.SemaphoreType.DMA((3 * nch,)), pltpu.SemaphoreType.DMA((3 * nch,)),
                        pltpu.SemaphoreType.DMA((3,)), pltpu.SemaphoreType.DMA((3,))],
        compiler_params=pltpu.CompilerParams(vmem_limit_bytes=STAGE_VMEM_LIMIT),
    )(w_pack)


def _rs_sibling(grads_pack):
    nch = GATHER_CHUNKS
    cr = PK_HALF // nch

    def body(g_ref, own_ref, got_ref, stage_ref, lsem_in, lsem_out, ssem, rsem):
        x, y, c = _pos()
        half = pl.ds(pl.multiple_of(c * PK_HALF, 16), PK_HALF)
        ohalf = pl.ds(pl.multiple_of((1 - c) * PK_HALF, 16), PK_HALF)
        for s in range(N_SHARD):
            for q in range(nch):
                mine = pl.ds(pl.multiple_of(c * PK_HALF + q * cr, 16), cr)
                theirs = pl.ds(pl.multiple_of((1 - c) * PK_HALF + q * cr, 16), cr)
                dst = pl.ds(q * cr, cr)
                pltpu.make_async_remote_copy(
                    src_ref=g_ref.at[s, theirs], dst_ref=got_ref.at[s, dst], send_sem=ssem, recv_sem=rsem,
                    device_id=(x, y, 1 - c), device_id_type=MESH).start()
                pltpu.make_async_copy(g_ref.at[s, mine], stage_ref.at[s, dst], lsem_in).start()
        pltpu.make_async_copy(g_ref.at[:, half], stage_ref, lsem_in).wait()
        for s in range(N_SHARD):
            for q in range(nch):
                dst = pl.ds(q * cr, cr)
                pltpu.make_async_copy(stage_ref.at[s, dst], own_ref.at[s, dst], lsem_out).start()
        pltpu.make_async_remote_copy(
            src_ref=g_ref.at[:, ohalf], dst_ref=got_ref, send_sem=ssem, recv_sem=rsem,
            device_id=(x, y, 1 - c), device_id_type=MESH).wait()
        pltpu.make_async_copy(stage_ref, own_ref, lsem_out).wait()

    hbm = pl.BlockSpec(memory_space=pl.ANY)
    shp = jax.ShapeDtypeStruct((N_SHARD, PK_HALF, D_MODEL), BF16)
    return pl.pallas_call(
        body, name="rs_sibling", out_shape=(shp, shp), in_specs=[hbm], out_specs=(hbm, hbm),
        scratch_shapes=[pltpu.VMEM((N_SHARD, PK_HALF, D_MODEL), BF16), pltpu.SemaphoreType.DMA,
                        pltpu.SemaphoreType.DMA, pltpu.SemaphoreType.DMA, pltpu.SemaphoreType.DMA],
        compiler_params=pltpu.CompilerParams(vmem_limit_bytes=STAGE_VMEM_LIMIT),
    )(grads_pack)


def _rs_chips(part):
    def body(p_ref, got_ref, stage_ref, lsem_in, lsem_out, ssem, rsem):
        x, y, c = _pos()
        j = 2 * x + y
        chips = [(1 - x, y), (x, 1 - y), (1 - x, 1 - y)]
        nch = GATHER_CHUNKS
        cr = PK_HALF // nch
        for q in range(nch):
            sl = pl.ds(q * cr, cr)
            pltpu.make_async_copy(p_ref.at[j, sl], stage_ref.at[sl], lsem_in).start()
            for k, (px, py) in enumerate(chips):
                jk = 2 * px + py
                pltpu.make_async_remote_copy(
                    src_ref=p_ref.at[jk, sl], dst_ref=got_ref.at[j, sl], send_sem=ssem.at[k], recv_sem=rsem.at[k],
                    device_id=(px, py, c), device_id_type=MESH).start()
        pltpu.make_async_copy(p_ref.at[j], stage_ref, lsem_in).wait()
        for q in range(nch):
            sl = pl.ds(q * cr, cr)
            pltpu.make_async_copy(stage_ref.at[sl], got_ref.at[j, sl], lsem_out).start()
        for k, (px, py) in enumerate(chips):
            jk = 2 * px + py
            pltpu.make_async_remote_copy(
                src_ref=p_ref.at[jk], dst_ref=got_ref.at[j], send_sem=ssem.at[k], recv_sem=rsem.at[k],
                device_id=(px, py, c), device_id_type=MESH).wait()
        pltpu.make_async_copy(stage_ref, got_ref.at[j], lsem_out).wait()

    hbm = pl.BlockSpec(memory_space=pl.ANY)
    return pl.pallas_call(
        body, name="rs_chips", out_shape=jax.ShapeDtypeStruct((N_SHARD, PK_HALF, D_MODEL), BF16),
        in_specs=[hbm], out_specs=hbm,
        scratch_shapes=[pltpu.VMEM((PK_HALF, D_MODEL), BF16), pltpu.SemaphoreType.DMA, pltpu.SemaphoreType.DMA,
                        pltpu.SemaphoreType.DMA((3,)), pltpu.SemaphoreType.DMA((3,))],
        compiler_params=pltpu.CompilerParams(vmem_limit_bytes=STAGE_VMEM_LIMIT),
    )(part)


def _rs_join(g_half):
    def body(h_ref, f_ref, stage_ref, lsem_in, lsem_out, ssem, rsem):
        x, y, c = _pos()
        half = pl.ds(pl.multiple_of(c * PK_HALF, 8), PK_HALF)
        nch = 2 * GATHER_CHUNKS
        cr = PK_HALF // nch
        for q in range(nch):
            src = pl.ds(q * cr, cr)
            dst = pl.ds(pl.multiple_of(c * PK_HALF + q * cr, 8), cr)
            pltpu.make_async_remote_copy(
                src_ref=h_ref.at[src], dst_ref=f_ref.at[dst], send_sem=ssem, recv_sem=rsem,
                device_id=(x, y, 1 - c), device_id_type=MESH).start()
            pltpu.make_async_copy(h_ref.at[src], stage_ref.at[src], lsem_in).start()
        pltpu.make_async_copy(h_ref, stage_ref, lsem_in).wait()
        for q in range(nch):
            src = pl.ds(q * cr, cr)
            dst = pl.ds(pl.multiple_of(c * PK_HALF + q * cr, 8), cr)
            pltpu.make_async_copy(stage_ref.at[src], f_ref.at[dst], lsem_out).start()
        pltpu.make_async_remote_copy(
            src_ref=h_ref, dst_ref=f_ref.at[half], send_sem=ssem, recv_sem=rsem,
            device_id=(x, y, 1 - c), device_id_type=MESH).wait()
        pltpu.make_async_copy(stage_ref, f_ref.at[half], lsem_out).wait()

    hbm = pl.BlockSpec(memory_space=pl.ANY)
    return pl.pallas_call(
        body, name="rs_join", out_shape=jax.ShapeDtypeStruct((PK_ROWS, D_MODEL), F32),
        in_specs=[hbm], out_specs=hbm,
        scratch_shapes=[pltpu.VMEM((PK_HALF, D_MODEL), F32), pltpu.SemaphoreType.DMA, pltpu.SemaphoreType.DMA,
                        pltpu.SemaphoreType.DMA, pltpu.SemaphoreType.DMA],
        compiler_params=pltpu.CompilerParams(vmem_limit_bytes=STAGE_VMEM_LIMIT),
    )(g_half)


def _add_pairs(a, b):
    rb = 480

    def body(a_ref, b_ref, o_ref):
        o_ref[...] = (a_ref[...].astype(F32) + b_ref[...].astype(F32)).astype(BF16)

    spec = pl.BlockSpec((1, rb, D_MODEL), lambda s, r: (s, r, 0))
    return pl.pallas_call(
        body, name="rs_add_pairs", grid=(N_SHARD, PK_HALF // rb), out_shape=jax.ShapeDtypeStruct(a.shape, BF16),
        in_specs=[spec, spec], out_specs=spec,
    )(a, b)


def _add_chips(got):
    rb = 480

    def body(g_ref, o_ref):
        acc = g_ref[0].astype(F32)
        for s in range(1, N_SHARD):
            acc = acc + g_ref[s].astype(F32)
        o_ref[...] = acc

    return pl.pallas_call(
        body, name="rs_add_chips", grid=(PK_HALF // rb,), out_shape=jax.ShapeDtypeStruct((PK_HALF, D_MODEL), F32),
        in_specs=[pl.BlockSpec((N_SHARD, rb, D_MODEL), lambda r: (0, r, 0))],
        out_specs=pl.BlockSpec((rb, D_MODEL), lambda r: (r, 0)),
    )(got)


def _small_exchange(sv, w_mod_sh, cctx, m_cctx, v_cctx, bmod, m_bmod, v_bmod, qg, m_qg, v_qg, kvg, m_kvg, v_kvg,
                    gf, m_gf, v_gf):
    ncol = w_mod_sh.shape[1]

    def body(sv_ref, w_ref, cctx_ref, mcc_ref, vcc_ref, b_ref, mb_ref, vb_ref, qg_ref, mq_ref, vq_ref,
             kg_ref, mk_ref, vk_ref, gf_ref, mgf_ref, vgf_ref,
             all_ref, red_ref, occ_ref, ob_ref, oq_ref, ok_ref, ogf_ref,
             vec_ref, part_ref, ssem, rsem, ssem2, rsem2):
        x, y, c = _pos()
        me = 4 * x + 2 * y + c
        j = 2 * x + y
        sends = []
        for r in range(1, 8):
            cp = pltpu.make_async_remote_copy(
                src_ref=sv_ref, dst_ref=all_ref.at[me], send_sem=ssem.at[r - 1], recv_sem=rsem.at[r - 1],
                device_id=_peer(r, x, y, c), device_id_type=MESH)
            cp.start()
            sends.append(cp)
        for cp in sends:
            cp.wait()
        all_ref[me] = sv_ref[...]
        red = all_ref[0]
        for d in range(1, 8):
            red = red + all_ref[d]
        red_ref[...] = red
        vec_ref[...] = jnp.zeros(vec_ref.shape, F32)

        @pl.when(j == 0)
        def _():
            vec_ref[0:1, 0:1024] = red[6:7, :]
            vec_ref[0:1, 1024:1536] = red[7:8, 0:512]

        @pl.when(j == 1)
        def _():
            vec_ref[0:1, 0:512] = red[7:8, 512:1024]

        part = lax.dot_general(vec_ref[...], w_ref[...], (((1,), (1,)), ((), ())), precision=HIGHEST,
                               preferred_element_type=F32)
        part_ref[j] = part
        sends2 = []
        for k, r in enumerate((4, 2, 6)):
            cp = pltpu.make_async_remote_copy(
                src_ref=part_ref.at[j], dst_ref=part_ref.at[j], send_sem=ssem2.at[k], recv_sem=rsem2.at[k],
                device_id=_peer(r, x, y, c), device_id_type=MESH)
            cp.start()
            sends2.append(cp)
        for cp in sends2:
            cp.wait()
        tot = part_ref[0]
        for s in range(1, N_SHARD):
            tot = tot + part_ref[s]
        cc = cctx_ref[...]
        sg = _sigmoid(cc)
        g_cc = tot[0:1, :] * (sg * (1.0 + cc * (1.0 - sg)))
        d_, m_, v_ = _adamw(cc, g_cc, mcc_ref[...], vcc_ref[...])
        occ_ref[0:1, :] = g_cc
        occ_ref[1:2, :] = d_
        occ_ref[2:3, :] = m_
        occ_ref[3:4, :] = v_
        occ_ref[4:8, :] = jnp.zeros((4, D_MODEL), F32)
        g_b = red[0:6, :]
        pad = jnp.concatenate([red[6:8, :], jnp.zeros((4, D_MODEL), F32)], axis=0)
        g_b = g_b + pad
        d_, m_, v_ = _adamw(b_ref[...], g_b, mb_ref[...], vb_ref[...])
        ob_ref[0] = g_b
        ob_ref[1] = d_
        ob_ref[2] = m_
        ob_ref[3] = v_
        g_q = red[9:10, 0:Q_RANK]
        d_, m_, v_ = _adamw(qg_ref[...], g_q, mq_ref[...], vq_ref[...])
        oq_ref[0:1, :] = g_q
        oq_ref[1:2, :] = d_
        oq_ref[2:3, :] = m_
        oq_ref[3:4, :] = v_
        oq_ref[4:8, :] = jnp.zeros((4, Q_RANK), F32)
        g_k = red[10:11, 0:KV_RANK]
        d_, m_, v_ = _adamw(kg_ref[...], g_k, mk_ref[...], vk_ref[...])
        ok_ref[0:1, :] = g_k
        ok_ref[1:2, :] = d_
        ok_ref[2:3, :] = m_
        ok_ref[3:4, :] = v_
        ok_ref[4:8, :] = jnp.zeros((4, KV_RANK), F32)
        g_f = red[8:9, :]
        d_, m_, v_ = _adamw(gf_ref[...], g_f, mgf_ref[...], vgf_ref[...])
        ogf_ref[0:1, :] = g_f
        ogf_ref[1:2, :] = d_
        ogf_ref[2:3, :] = m_
        ogf_ref[3:4, :] = v_
        ogf_ref[4:8, :] = jnp.zeros((4, D_MODEL), F32)

    vm = pl.BlockSpec(memory_space=pltpu.VMEM)
    out_shape = (
        jax.ShapeDtypeStruct((8, 16, D_MODEL), F32),
        jax.ShapeDtypeStruct((16, D_MODEL), F32),
        jax.ShapeDtypeStruct((8, D_MODEL), F32),
        jax.ShapeDtypeStruct((4, 6, D_MODEL), F32),
        jax.ShapeDtypeStruct((8, Q_RANK), F32),
        jax.ShapeDtypeStruct((8, KV_RANK), F32),
        jax.ShapeDtypeStruct((8, D_MODEL), F32),
    )
    return pl.pallas_call(
        body, name="small_exchange", out_shape=out_shape, in_specs=[vm] * 17, out_specs=tuple([vm] * 7),
        scratch_shapes=[pltpu.VMEM((8, ncol), F32), pltpu.VMEM((N_SHARD, 8, D_MODEL), F32),
                        pltpu.SemaphoreType.DMA((7,)), pltpu.SemaphoreType.DMA((7,)),
                        pltpu.SemaphoreType.DMA((3,)), pltpu.SemaphoreType.DMA((3,))],
        compiler_params=pltpu.CompilerParams(vmem_limit_bytes=VMEM_LIMIT),
    )(sv, w_mod_sh, cctx, m_cctx, v_cctx, bmod, m_bmod, v_bmod, qg, m_qg, v_qg, kvg, m_kvg, v_kvg, gf, m_gf, v_gf)


def _inproj_fwd(x2, ctx2, mod_a, w_in, qg, kvg, w_uq, w_ukv, cos_t, sin_a, sin_b):
    t_lat, t_ctx = x2.shape[0], ctx2.shape[0]
    tm = TOK_TILE
    n_lat = t_lat // tm
    n_all = n_lat + t_ctx // tm
    e_rows = t_lat + t_ctx

    def body(x_ref, ctx_ref, mod_ref, win_ref, qg_ref, kvg_ref, wuq_ref, wukv_ref, cos_ref, sa_ref, sb_ref,
             z_ref, q_ref, k_ref, v_ref):
        i = pl.program_id(0)
        xin = jnp.where(i < n_lat, x_ref[...], ctx_ref[...])
        xn = xin * lax.rsqrt(jnp.mean(xin * xin, axis=-1, keepdims=True) + EPS)
        h1 = (xn * (1.0 + mod_ref[0, 1:2, :]) + mod_ref[0, 0:1, :]).astype(BF16)
        z = _dot(h1, win_ref[...])
        z_ref[...] = z
        cos, sa, sb = cos_ref[...], sa_ref[...], sb_ref[...]
        cq = z[:, 0:Q_RANK]
        cqn = (cq * lax.rsqrt(jnp.mean(cq * cq, axis=-1, keepdims=True) + EPS) * qg_ref[...]).astype(BF16)
        q = _dot(cqn, wuq_ref[...])
        ckv = z[:, Q_RANK:Q_RANK + KV_RANK]
        ckvn = (ckv * lax.rsqrt(jnp.mean(ckv * ckv, axis=-1, keepdims=True) + EPS) * kvg_ref[...]).astype(BF16)
        kv = _dot(ckvn, wukv_ref[...])
        kr = _rope(z[:, Q_RANK + KV_RANK:Q_RANK + KV_RANK + HEAD_PAD], cos, sa, sb)
        for h in range(N_HEADS):
            lo = h * HEAD_PAD
            q_ref[h] = _rope(q[:, lo:lo + HEAD_PAD], cos, sa, sb).astype(BF16)
            k_ref[h] = (kv[:, lo:lo + HEAD_PAD] + kr).astype(BF16)
            v_ref[h] = kv[:, N_HEADS * HEAD_PAD + lo:N_HEADS * HEAD_PAD + lo + HEAD_PAD].astype(BF16)

    row = lambda i: (i, 0)
    head_spec = pl.BlockSpec((N_HEADS, tm, HEAD_PAD), lambda i: (0, i, 0))
    head_shape = jax.ShapeDtypeStruct((N_HEADS, e_rows, HEAD_PAD), BF16)
    return pl.pallas_call(
        body, name="inproj_fwd", grid=(n_all,),
        out_shape=(jax.ShapeDtypeStruct((e_rows, Z_COLS), F32), head_shape, head_shape, head_shape),
        in_specs=[
            pl.BlockSpec((tm, D_MODEL), lambda i: (jnp.minimum(i, n_lat - 1), 0)),
            _const_spec((tm, D_MODEL)),
            pl.BlockSpec((1, 8, D_MODEL), lambda i: (i // n_lat, 0, 0)),
            _const_spec(w_in.shape), _const_spec(qg.shape), _const_spec(kvg.shape),
            _const_spec(w_uq.shape), _const_spec(w_ukv.shape),
            pl.BlockSpec((tm, HEAD_PAD), row), pl.BlockSpec((tm, HEAD_PAD), row), pl.BlockSpec((tm, HEAD_PAD), row),
        ],
        out_specs=(pl.BlockSpec((tm, Z_COLS), row), head_spec, head_spec, head_spec),
        compiler_params=pltpu.CompilerParams(vmem_limit_bytes=VMEM_LIMIT),
    )(x2, ctx2, mod_a, w_in, qg, kvg, w_uq, w_ukv, cos_t, sin_a, sin_b)


def _attn_fwd(q, k, v, t_lat):
    e_rows = k.shape[1]
    tq = 256
    n_chunks = max(1, e_rows // KEY_CHUNK)
    bounds = [(ci * KEY_CHUNK, KEY_CHUNK if ci < n_chunks - 1 else e_rows - ci * KEY_CHUNK) for ci in range(n_chunks)]
    c2 = ATTN_SCALE * LOG2E

    def body(q_ref, k_ref, v_ref, o_ref, lse_ref):
        qb = q_ref[0]
        m = l = acc = None
        for lo, n in bounds:
            s = _dot_nt(qb, k_ref[0, lo:lo + n, :])
            mc = jnp.max(s, axis=-1, keepdims=True)
            m_new = mc if m is None else jnp.maximum(m, mc)
            p = jnp.exp2((s - m_new) * c2)
            pv = _dot(p.astype(BF16), v_ref[0, lo:lo + n, :])
            ps = jnp.sum(p, axis=-1, keepdims=True)
            if m is None:
                l, acc = ps, pv
            else:
                alpha = jnp.exp2((m - m_new) * c2)
                l = l * alpha + ps
                acc = acc * alpha + pv
            m = m_new
        o_ref[0] = (acc * (1.0 / l)).astype(BF16)
        lse_ref[0] = (m * ATTN_SCALE + jnp.log(l)) * LOG2E

    return pl.pallas_call(
        body, name="attn_fwd", grid=(N_HEADS, t_lat // tq),
        out_shape=(jax.ShapeDtypeStruct((N_HEADS, t_lat, HEAD_PAD), BF16),
                   jax.ShapeDtypeStruct((N_HEADS, t_lat, 1), F32)),
        in_specs=[pl.BlockSpec((1, tq, HEAD_PAD), lambda h, i: (h, i, 0)),
                  pl.BlockSpec((1, e_rows, HEAD_PAD), lambda h, i: (h, 0, 0)),
                  pl.BlockSpec((1, e_rows, HEAD_PAD), lambda h, i: (h, 0, 0))],
        out_specs=(pl.BlockSpec((1, tq, HEAD_PAD), lambda h, i: (h, i, 0)),
                   pl.BlockSpec((1, tq, 1), lambda h, i: (h, i, 0))),
        compiler_params=pltpu.CompilerParams(vmem_limit_bytes=VMEM_LIMIT),
    )(q, k, v)


def _attn_bwd(q, k, v, kt, o, do, lse_row, t_lat):
    e_rows = k.shape[1]
    tq = 256

    def body(q_ref, k_ref, v_ref, kt_ref, o_ref, do_ref, lse_ref, dqt_ref, dk_ref, dv_ref):
        i = pl.program_id(1)
        qb, dob = q_ref[0], do_ref[0]
        prod = o_ref[0].astype(F32) * dob.astype(F32)
        delta = lax.dot_general(jnp.ones((8, HEAD_PAD), F32), prod, (((1,), (1,)), ((), ())), precision=HIGHEST,
                                preferred_element_type=F32)[0:1, :]
        pt = jnp.exp2(_dot_nt(k_ref[0], qb) * (ATTN_SCALE * LOG2E) - lse_ref[0])
        dpt = _dot_nt(v_ref[0], dob)
        dst = (pt * (dpt - delta) * ATTN_SCALE).astype(BF16)
        dv_c = _dot(pt.astype(BF16), dob)
        dk_c = _dot(dst, qb)
        dqt_ref[0] = _dot(kt_ref[0], dst)

        @pl.when(i == 0)
        def _():
            dk_ref[0] = dk_c
            dv_ref[0] = dv_c

        @pl.when(i > 0)
        def _():
            dk_ref[0] += dk_c
            dv_ref[0] += dv_c

    qspec = pl.BlockSpec((1, tq, HEAD_PAD), lambda h, i: (h, i, 0))
    kspec = pl.BlockSpec((1, e_rows, HEAD_PAD), lambda h, i: (h, 0, 0))
    return pl.pallas_call(
        body, name="attn_bwd", grid=(N_HEADS, t_lat // tq),
        out_shape=(jax.ShapeDtypeStruct((N_HEADS, HEAD_PAD, t_lat), F32),
                   jax.ShapeDtypeStruct((N_HEADS, e_rows, HEAD_PAD), F32),
                   jax.ShapeDtypeStruct((N_HEADS, e_rows, HEAD_PAD), F32)),
        in_specs=[qspec, kspec, kspec, pl.BlockSpec((1, HEAD_PAD, e_rows), lambda h, i: (h, 0, 0)), qspec, qspec,
                  pl.BlockSpec((1, 1, tq), lambda h, i: (h, 0, i))],
        out_specs=(pl.BlockSpec((1, HEAD_PAD, tq), lambda h, i: (h, 0, i)), kspec, kspec),
        compiler_params=pltpu.CompilerParams(vmem_limit_bytes=VMEM_LIMIT),
    )(q, k, v, kt, o, do, lse_row)


def _halo_specs(tm, col_block):
    per = tm // 8
    prev = pl.BlockSpec((8, CONV_W), lambda i: (jnp.maximum(i * per - 1, 0), col_block))
    nxt = pl.BlockSpec((8, CONV_W), lambda i: ((i + 1) * per, col_block))
    return prev, nxt


def _mlp_fwdbwd(o, z, x2, tgt, mod_b, gf, cw, wo_attn, wo_conv, w1, w2):
    t_lat = x2.shape[0]
    tm = TOK_TILE
    n_lat = t_lat // tm
    n_ff = D_FF // FF_CHUNK

    def body(o_ref, gb_ref, gc_ref, xi_ref, gcp_ref, xip_ref, gcn_ref, xin_ref, cw_ref, woa_ref, woc_ref,
             x_ref, t_ref, mod_ref, gf_ref, w1_ref, w2_ref,
             r_ref, da_ref, h2_ref, dy2_ref, dx1_ref, y1_ref, yv_ref, conv_ref, acc_ref, ra_ref):
        i = pl.program_id(0)

        @pl.when(i == 0)
        def _():
            acc_ref[...] = jnp.zeros(acc_ref.shape, F32)

        g1, sh2, sc2, g2 = mod_ref[0:1, :], mod_ref[1:2, :], mod_ref[2:3, :], mod_ref[3:4, :]
        u = gc_ref[...] * xi_ref[...]
        u_prev = jnp.where(i > 0, gcp_ref[7:8, :] * xip_ref[7:8, :], 0.0)
        u_next = jnp.where(i < n_lat - 1, gcn_ref[0:1, :] * xin_ref[0:1, :], 0.0)
        um1, up1 = _shift_rows(u, u_prev, u_next)
        yv = cw_ref[0:1, :] * um1 + cw_ref[1:2, :] * u + cw_ref[2:3, :] * up1
        yv_ref[...] = yv
        conv = (gb_ref[...] * yv).astype(BF16)
        conv_ref[...] = conv
        y1 = _dot(conv, woc_ref[...])
        for h in range(N_HEADS):
            y1 = y1 + _dot(o_ref[h], woa_ref[h])
        y1_ref[...] = y1
        x1 = x_ref[...] + g1 * y1
        rstd2 = lax.rsqrt(jnp.mean(x1 * x1, axis=-1, keepdims=True) + EPS)
        xn1 = x1 * rstd2
        h2 = (xn1 * (1.0 + sc2) + sh2).astype(BF16)
        h2_ref[...] = h2
        y2 = jnp.zeros((tm, D_MODEL), F32)
        for jj in range(n_ff):
            ra = jnp.maximum(_dot(h2, w1_ref[jj]), 0.0)
            ra_ref[jj] = ra
            r = (ra * ra).astype(BF16)
            r_ref[:, jj * FF_CHUNK:(jj + 1) * FF_CHUNK] = r
            y2 = y2 + _dot(r, w2_ref[jj * FF_CHUNK:(jj + 1) * FF_CHUNK, :])
        x2v = x1 + g2 * y2
        rstd3 = lax.rsqrt(jnp.mean(x2v * x2v, axis=-1, keepdims=True) + EPS)
        xn3 = x2v * rstd3
        gfv = gf_ref[...]
        diff = xn3 * gfv - t_ref[...]
        loss_t = 0.5 * jnp.sum(jnp.sum(diff * diff, axis=-1, keepdims=True), axis=0, keepdims=True) * (1.0 / D_MODEL)
        dy = diff * (1.0 / D_MODEL)
        dxn3 = dy * gfv
        dx2 = rstd3 * (dxn3 - xn3 * jnp.mean(dxn3 * xn3, axis=-1, keepdims=True))
        dy2 = (dx2 * g2).astype(BF16)
        dy2_ref[...] = dy2
        dh2 = jnp.zeros((tm, D_MODEL), F32)
        for jj in range(n_ff):
            dr = _dot_nt(dy2, w2_ref[jj * FF_CHUNK:(jj + 1) * FF_CHUNK, :])
            da = (2.0 * ra_ref[jj] * dr).astype(BF16)
            da_ref[:, jj * FF_CHUNK:(jj + 1) * FF_CHUNK] = da
            dh2 = dh2 + _dot_nt(da, w1_ref[jj])
        dxn1 = dh2 * (1.0 + sc2)
        dx1_ref[...] = dx2 + rstd2 * (dxn1 - xn1 * jnp.mean(dxn1 * xn1, axis=-1, keepdims=True))
        acc_ref[0:1, :] += jnp.sum(dy * xn3, axis=0, keepdims=True)
        acc_ref[1:2, :] += jnp.sum(dx2 * y2, axis=0, keepdims=True)
        acc_ref[2:3, :] += jnp.sum(dh2, axis=0, keepdims=True)
        acc_ref[3:4, :] += jnp.sum(dh2 * xn1, axis=0, keepdims=True)
        acc_ref[4:5, :] += jnp.broadcast_to(loss_t, (1, D_MODEL))

    row = lambda i: (i, 0)
    gcp, gcn = _halo_specs(tm, 2)
    xip, xin = _halo_specs(tm, 3)
    tile = pl.BlockSpec((tm, D_MODEL), row)
    wide = pl.BlockSpec((tm, D_FF), row)
    half = pl.BlockSpec((tm, CONV_W), row)
    return pl.pallas_call(
        body, name="mlp_fwdbwd", grid=(n_lat,),
        out_shape=(jax.ShapeDtypeStruct((t_lat, D_FF), BF16), jax.ShapeDtypeStruct((t_lat, D_FF), BF16),
                   jax.ShapeDtypeStruct((t_lat, D_MODEL), BF16), jax.ShapeDtypeStruct((t_lat, D_MODEL), BF16),
                   jax.ShapeDtypeStruct((t_lat, D_MODEL), F32), jax.ShapeDtypeStruct((t_lat, D_MODEL), F32),
                   jax.ShapeDtypeStruct((t_lat, CONV_W), F32), jax.ShapeDtypeStruct((t_lat, CONV_W), BF16),
                   jax.ShapeDtypeStruct((8, D_MODEL), F32)),
        in_specs=[
            pl.BlockSpec((N_HEADS, tm, HEAD_PAD), lambda i: (0, i, 0)),
            pl.BlockSpec((tm, CONV_W), lambda i: (i, 1)), pl.BlockSpec((tm, CONV_W), lambda i: (i, 2)),
            pl.BlockSpec((tm, CONV_W), lambda i: (i, 3)),
            gcp, xip, gcn, xin,
            _const_spec(cw.shape), _resident_spec(wo_attn.shape), _resident_spec(wo_conv.shape),
            tile, tile, _const_spec(mod_b.shape), _const_spec(gf.shape),
            _resident_spec(w1.shape), _resident_spec(w2.shape),
        ],
        out_specs=(wide, wide, tile, tile, tile, tile, half, half, _const_spec((8, D_MODEL))),
        scratch_shapes=[pltpu.VMEM((n_ff, tm, FF_CHUNK), F32)],
        compiler_params=pltpu.CompilerParams(vmem_limit_bytes=VMEM_LIMIT),
    )(o, z, z, z, z, z, z, z, cw, wo_attn, wo_conv, x2, tgt, mod_b, gf, w1, w2)


def _mid_bwd(dx1, y1, z, yv, mod_b, wo_attn, wo_conv):
    t_lat = dx1.shape[0]
    tm = TOK_TILE

    def body(dx1_ref, y1_ref, gb_ref, yv_ref, mod_ref, woa_ref, woc_ref, dy1_ref, do_ref, dgb_ref, dyv_ref, acc_ref):
        i = pl.program_id(0)

        @pl.when(i == 0)
        def _():
            acc_ref[...] = jnp.zeros(acc_ref.shape, F32)

        dx1v = dx1_ref[...]
        acc_ref[0:1, :] += jnp.sum(dx1v * y1_ref[...], axis=0, keepdims=True)
        dy1 = (dx1v * mod_ref[0:1, :]).astype(BF16)
        dy1_ref[...] = dy1
        for h in range(N_HEADS):
            do_ref[h] = _dot_nt(dy1, woa_ref[h]).astype(BF16)
        dconv = _dot_nt(dy1, woc_ref[...])
        dgb_ref[...] = dconv * yv_ref[...]
        dyv_ref[...] = dconv * gb_ref[...]

    row = lambda i: (i, 0)
    tile = pl.BlockSpec((tm, D_MODEL), row)
    half = pl.BlockSpec((tm, CONV_W), row)
    return pl.pallas_call(
        body, name="mid_bwd", grid=(t_lat // tm,),
        out_shape=(jax.ShapeDtypeStruct((t_lat, D_MODEL), BF16),
                   jax.ShapeDtypeStruct((N_HEADS, t_lat, HEAD_PAD), BF16),
                   jax.ShapeDtypeStruct((t_lat, CONV_W), F32), jax.ShapeDtypeStruct((t_lat, CONV_W), F32),
                   jax.ShapeDtypeStruct((8, D_MODEL), F32)),
        in_specs=[tile, tile, pl.BlockSpec((tm, CONV_W), lambda i: (i, 1)), half, _const_spec(mod_b.shape),
                  _const_spec(wo_attn.shape), _const_spec(wo_conv.shape)],
        out_specs=(tile, pl.BlockSpec((N_HEADS, tm, HEAD_PAD), lambda i: (0, i, 0)), half, half,
                   _const_spec((8, D_MODEL))),
        compiler_params=pltpu.CompilerParams(vmem_limit_bytes=VMEM_LIMIT),
    )(dx1, y1, z, yv, mod_b, wo_attn, wo_conv)


def _inproj_bwd(x2, ctx2, mod_a, z, dyv, dgb, dx1, dqt, dk, dv, cos_t, sin_a, sin_b, w_in, w_uq, w_ukv, qg, kvg, cw):
    t_lat, t_ctx = x2.shape[0], ctx2.shape[0]
    tm = TOK_TILE
    n_lat = t_lat // tm
    n_all = n_lat + t_ctx // tm

    def body(x_ref, ctx_ref, mod_ref, z_ref, gcp_ref, xip_ref, gcn_ref, xin_ref, dyv_ref, dyvp_ref, dyvn_ref,
             dgb_ref, dx1_ref, dqt_ref, dk_ref, dv_ref, cos_ref, sa_ref, sb_ref, win_ref, wuq_ref, wukv_ref,
             qg_ref, kvg_ref, cw_ref, gx_ref, dwin_ref, dwuq_ref, dwukv_ref, acc_ref):
        i = pl.program_id(0)
        lat = i < n_lat

        @pl.when(i == 0)
        def _():
            dwin_ref[...] = jnp.zeros(dwin_ref.shape, F32)
            dwuq_ref[...] = jnp.zeros(dwuq_ref.shape, F32)
            dwukv_ref[...] = jnp.zeros(dwukv_ref.shape, F32)
            acc_ref[...] = jnp.zeros(acc_ref.shape, F32)

        xin = jnp.where(lat, x_ref[...], ctx_ref[...])
        rstd = lax.rsqrt(jnp.mean(xin * xin, axis=-1, keepdims=True) + EPS)
        xn = xin * rstd
        sc = mod_ref[0, 1:2, :]
        h1 = (xn * (1.0 + sc) + mod_ref[0, 0:1, :]).astype(BF16)
        z = z_ref[...]
        cos, sa, sb = cos_ref[...], sa_ref[...], sb_ref[...]
        qgv, kvgv = qg_ref[...], kvg_ref[...]
        cq = z[:, 0:Q_RANK]
        cqh = cq * lax.rsqrt(jnp.mean(cq * cq, axis=-1, keepdims=True) + EPS)
        rq = lax.rsqrt(jnp.mean(cq * cq, axis=-1, keepdims=True) + EPS)
        cqn = (cqh * qgv).astype(BF16)
        parts = []
        for h in range(N_HEADS):
            g = jnp.where(lat, dqt_ref[h].T, 0.0)
            parts.append(_unrope(g, cos, sa, sb))
        dq = jnp.concatenate(parts, axis=1).astype(BF16)
        dcqn = _dot_nt(dq, wuq_ref[...])
        dwuq_ref[...] += _dot_tn(cqn, dq)
        acc_ref[4:5, 0:Q_RANK] += jnp.sum(dcqn * cqh, axis=0, keepdims=True)
        dxn = dcqn * qgv
        dcq = rq * (dxn - cqh * jnp.mean(dxn * cqh, axis=-1, keepdims=True))
        ckv = z[:, Q_RANK:Q_RANK + KV_RANK]
        rk = lax.rsqrt(jnp.mean(ckv * ckv, axis=-1, keepdims=True) + EPS)
        ckvh = ckv * rk
        ckvn = (ckvh * kvgv).astype(BF16)
        dks = [dk_ref[h] for h in range(N_HEADS)]
        dkr = dks[0]
        for h in range(1, N_HEADS):
            dkr = dkr + dks[h]
        dkv = jnp.concatenate(dks + [dv_ref[h] for h in range(N_HEADS)], axis=1).astype(BF16)
        dckvn = _dot_nt(dkv, wukv_ref[...])
        dwukv_ref[...] += _dot_tn(ckvn, dkv)
        acc_ref[5:6, 0:KV_RANK] += jnp.sum(dckvn * ckvh, axis=0, keepdims=True)
        dxn = dckvn * kvgv
        dckv = rk * (dxn - ckvh * jnp.mean(dxn * ckvh, axis=-1, keepdims=True))
        dkr = _unrope(dkr, cos, sa, sb)
        gb, gc, xi = z[:, 512:1024], z[:, 1024:1536], z[:, 1536:2048]
        u = gc * xi
        u_prev = jnp.where(i > 0, gcp_ref[7:8, :] * xip_ref[7:8, :], 0.0)
        u_next = jnp.where(i < n_lat - 1, gcn_ref[0:1, :] * xin_ref[0:1, :], 0.0)
        um1, up1 = _shift_rows(u, u_prev, u_next)
        dyv = jnp.where(lat, dyv_ref[...], 0.0)
        dyv_prev = jnp.where(jnp.logical_and(i > 0, lat), dyvp_ref[7:8, :], 0.0)
        dyv_next = jnp.where(i < n_lat - 1, dyvn_ref[0:1, :], 0.0)
        dyv_m1, dyv_p1 = _shift_rows(dyv, dyv_prev, dyv_next)
        du = cw_ref[0:1, :] * dyv_p1 + cw_ref[1:2, :] * dyv + cw_ref[2:3, :] * dyv_m1
        dgc = du * xi
        dxi = du * gc
        dgb = jnp.where(lat, dgb_ref[...], 0.0)
        acc_ref[6:7, 0:CONV_W] += jnp.sum(dyv * um1, axis=0, keepdims=True)
        acc_ref[7:8, 0:CONV_W] += jnp.sum(dyv * u, axis=0, keepdims=True)
        acc_ref[8:9, 0:CONV_W] += jnp.sum(dyv * up1, axis=0, keepdims=True)
        dz = jnp.concatenate([dcq, dckv, dkr, dgb, dgc, dxi], axis=1).astype(BF16)
        dh1 = _dot_nt(dz, win_ref[...])
        dwin_ref[...] += _dot_tn(h1, dz)
        s_sh = jnp.sum(dh1, axis=0, keepdims=True)
        s_sc = jnp.sum(dh1 * xn, axis=0, keepdims=True)
        zero = jnp.zeros_like(s_sh)
        acc_ref[0:1, :] += jnp.where(lat, s_sh, zero)
        acc_ref[1:2, :] += jnp.where(lat, s_sc, zero)
        acc_ref[2:3, :] += jnp.where(lat, zero, s_sh)
        acc_ref[3:4, :] += jnp.where(lat, zero, s_sc)
        dxn = dh1 * (1.0 + sc)
        dx = rstd * (dxn - xn * jnp.mean(dxn * xn, axis=-1, keepdims=True))

        @pl.when(lat)
        def _():
            gx_ref[...] = dx1_ref[...] + dx

    last = n_lat - 1
    per = tm // 8
    lat_row = lambda i: (jnp.minimum(i, last), 0)
    row = lambda i: (i, 0)
    gcp, gcn = _halo_specs(tm, 2)
    xip, xin = _halo_specs(tm, 3)
    n_halo = t_lat // 8
    dyvp = pl.BlockSpec((8, CONV_W), lambda i: (jnp.clip(i * per - 1, 0, n_halo - 1), 0))
    dyvn = pl.BlockSpec((8, CONV_W), lambda i: (jnp.minimum((i + 1) * per, n_halo - 1), 0))
    gcn = pl.BlockSpec((8, CONV_W), lambda i: (jnp.minimum((i + 1) * per, (t_lat + t_ctx) // 8 - 1), 2))
    xin = pl.BlockSpec((8, CONV_W), lambda i: (jnp.minimum((i + 1) * per, (t_lat + t_ctx) // 8 - 1), 3))
    head_f32 = pl.BlockSpec((N_HEADS, tm, HEAD_PAD), lambda i: (0, i, 0))
    tab = pl.BlockSpec((tm, HEAD_PAD), row)
    return pl.pallas_call(
        body, name="inproj_bwd", grid=(n_all,),
        out_shape=(jax.ShapeDtypeStruct((t_lat, D_MODEL), F32), jax.ShapeDtypeStruct(w_in.shape, F32),
                   jax.ShapeDtypeStruct(w_uq.shape, F32), jax.ShapeDtypeStruct(w_ukv.shape, F32),
                   jax.ShapeDtypeStruct((16, D_MODEL), F32)),
        in_specs=[
            pl.BlockSpec((tm, D_MODEL), lat_row), _const_spec((tm, D_MODEL)),
            pl.BlockSpec((1, 8, D_MODEL), lambda i: (i // n_lat, 0, 0)),
            pl.BlockSpec((tm, Z_COLS), row), gcp, xip, gcn, xin,
            pl.BlockSpec((tm, CONV_W), lat_row), dyvp, dyvn,
            pl.BlockSpec((tm, CONV_W), lat_row), pl.BlockSpec((tm, D_MODEL), lat_row),
            pl.BlockSpec((N_HEADS, HEAD_PAD, tm), lambda i: (0, 0, jnp.minimum(i, last))),
            head_f32, head_f32, tab, tab, tab,
            _const_spec(w_in.shape), _const_spec(w_uq.shape), _const_spec(w_ukv.shape),
            _const_spec(qg.shape), _const_spec(kvg.shape), _const_spec(cw.shape),
        ],
        out_specs=(pl.BlockSpec((tm, D_MODEL), lat_row), _const_spec(w_in.shape), _const_spec(w_uq.shape),
                   _const_spec(w_ukv.shape), _const_spec((16, D_MODEL))),
        compiler_params=pltpu.CompilerParams(vmem_limit_bytes=VMEM_LIMIT),
    )(x2, ctx2, mod_a, z, z, z, z, z, dyv, dyv, dyv, dgb, dx1, dqt, dk, dv, cos_t, sin_a, sin_b, w_in, w_uq, w_ukv,
      qg, kvg, cw)


def _wgrad(a, b, name, bm, bn, out_blocks=None):
    t, m = a.shape
    n = b.shape[1]
    bk = 512
    nk = t // bk

    def body(a_ref, b_ref, o_ref):
        k = pl.program_id(2)
        part = _dot_tn(a_ref[...], b_ref[...])

        @pl.when(k == 0)
        def _():
            o_ref[...] = part

        @pl.when(k > 0)
        def _():
            o_ref[...] += part

    if out_blocks:
        out_shape = jax.ShapeDtypeStruct((n // bn, m, bn), F32)
        out_spec = pl.BlockSpec((None, bm, bn), lambda i, j, k: (j, i, 0))
    else:
        out_shape = jax.ShapeDtypeStruct((m, n), F32)
        out_spec = pl.BlockSpec((bm, bn), lambda i, j, k: (i, j))
    return pl.pallas_call(
        body, name=name, grid=(m // bm, n // bn, nk), out_shape=out_shape,
        in_specs=[pl.BlockSpec((bk, bm), lambda i, j, k: (k, i)), pl.BlockSpec((bk, bn), lambda i, j, k: (k, j))],
        out_specs=out_spec,
        compiler_params=pltpu.CompilerParams(vmem_limit_bytes=VMEM_LIMIT),
    )(a, b)


def _wgrad_out(o, conv, dy1):
    t = o.shape[1]
    bk = 512
    rows = N_HEADS * HEAD_PAD + CONV_W

    def body(o_ref, c_ref, d_ref, w_ref):
        k = pl.program_id(0)
        cat = jnp.concatenate([o_ref[h] for h in range(N_HEADS)] + [c_ref[...]], axis=1)
        part = _dot_tn(cat, d_ref[...])

        @pl.when(k == 0)
        def _():
            w_ref[...] = part

        @pl.when(k > 0)
        def _():
            w_ref[...] += part

    return pl.pallas_call(
        body, name="wgrad_out", grid=(t // bk,),
        out_shape=jax.ShapeDtypeStruct((rows, D_MODEL), F32),
        in_specs=[pl.BlockSpec((N_HEADS, bk, HEAD_PAD), lambda k: (0, k, 0)),
                  pl.BlockSpec((bk, CONV_W), lambda k: (k, 0)),
                  pl.BlockSpec((bk, D_MODEL), lambda k: (k, 0))],
        out_specs=_const_spec((rows, D_MODEL)),
        compiler_params=pltpu.CompilerParams(vmem_limit_bytes=VMEM_LIMIT),
    )(o, conv, dy1)


def _adamw_call(w, g, m, v, name):
    rows, cols = w.shape
    rb = 256 if rows % 256 == 0 else rows

    def body(w_ref, g_ref, m_ref, v_ref, d_ref, nm_ref, nv_ref):
        d_, m_, v_ = _adamw(w_ref[...], g_ref[...], m_ref[...], v_ref[...])
        d_ref[...] = d_
        nm_ref[...] = m_
        nv_ref[...] = v_

    spec = pl.BlockSpec((rb, cols), lambda i: (i, 0))
    shp = jax.ShapeDtypeStruct((rows, cols), F32)
    return pl.pallas_call(
        body, name=name, grid=(rows // rb,), out_shape=(shp, shp, shp),
        in_specs=[spec] * 4, out_specs=(spec, spec, spec),
    )(w, g, m, v)


def _wmod_update(s_t, dm, w, m, v):
    rows, cols = w.shape
    cb = 512

    def body(s_ref, dm_ref, w_ref, m_ref, v_ref, g_ref, d_ref, nm_ref, nv_ref):
        g = jnp.dot(s_ref[...], dm_ref[...], precision=HIGHEST, preferred_element_type=F32)
        d_, m_, v_ = _adamw(w_ref[...], g, m_ref[...], v_ref[...])
        g_ref[...] = g
        d_ref[...] = d_
        nm_ref[...] = m_
        nv_ref[...] = v_

    spec = pl.BlockSpec((rows, cb), lambda i: (0, i))
    shp = jax.ShapeDtypeStruct((rows, cols), F32)
    return pl.pallas_call(
        body, name="wmod_update", grid=(cols // cb,), out_shape=(shp, shp, shp, shp),
        in_specs=[_const_spec(s_t.shape), pl.BlockSpec((16, cb), lambda i: (0, i)), spec, spec, spec],
        out_specs=(spec, spec, spec, spec),
        compiler_params=pltpu.CompilerParams(vmem_limit_bytes=VMEM_LIMIT),
    )(s_t, dm, w, m, v)


def _rope_tables(t_lat, t_ctx):
    t = jnp.arange(t_lat)
    pos = jnp.stack([(t // GRID_W).astype(F32), (t % GRID_W).astype(F32)], axis=1)
    half = QK_ROPE // 4
    freqs = ROPE_THETA ** (-jnp.arange(0, 2 * half, 2, dtype=F32) / (2 * half))
    ang = pos[:, :, None] * freqs[None, None, :]
    cos, sin = jnp.cos(ang), jnp.sin(ang)
    zero = jnp.zeros_like(sin)
    cos32 = jnp.concatenate([cos, cos], axis=2).reshape(t_lat, QK_ROPE)
    sa32 = jnp.concatenate([zero, sin], axis=2).reshape(t_lat, QK_ROPE)
    sb32 = jnp.concatenate([-sin, zero], axis=2).reshape(t_lat, QK_ROPE)

    def widen(tab, fill):
        left = jnp.full((t_lat, ROPE_LANE0), fill, F32)
        right = jnp.full((t_lat, HEAD_PAD - ROPE_LANE0 - QK_ROPE), fill, F32)
        lat = jnp.concatenate([left, tab, right], axis=1)
        return jnp.concatenate([lat, jnp.full((t_ctx, HEAD_PAD), fill, F32)], axis=0)

    return widen(cos32, 1.0), widen(sa32, 0.0), widen(sb32, 0.0)


def _pack_shards(w_in, w_uq, w_ukv, w_out, w1, w2, dtype):
    lead = w_in.shape[:-2]
    parts = [w_in.reshape(lead + (PK_IN, D_MODEL)), w_uq.reshape(lead + (PK_UQ, D_MODEL)),
             w_ukv.reshape(lead + (PK_UKV, D_MODEL)), jnp.zeros(lead + (PK_PAD, D_MODEL), w_in.dtype),
             w_out, w1, w2]
    return jnp.concatenate([p.astype(dtype) for p in parts], axis=-2)


def _unpack_rows(p):
    lead = p.shape[:-2]
    o = 0
    out = []
    for rows, shape in ((PK_IN, (1024, 488)), (PK_UQ, (256, 192)), (PK_UKV, (128, 256)), (PK_PAD, None),
                        (PK_OUT, (256, 1024)), (PK_M1, (1024, 1024)), (PK_M2, (1024, 1024))):
        if shape is not None:
            out.append(p[..., o:o + rows, :].reshape(lead + shape))
        o += rows
    return out


def _cols_from_shards(s):
    return jnp.transpose(s, (1, 0, 2)).reshape(s.shape[1], -1)


def _cols_to_shards(w):
    k, n = w.shape
    return jnp.transpose(w.reshape(k, N_SHARD, n // N_SHARD), (1, 0, 2))


def kernel(x, c, ctx, c_ctx, w_mod, b_mod, w_in, q_norm_g, w_uq, kv_norm_g, w_ukv, conv_w, w_out, w_mlp1, w_mlp2, final_norm_g, loss_target, m_c_ctx, m_w_mod, m_b_mod, m_w_in, m_q_norm_g, m_w_uq, m_kv_norm_g, m_w_ukv, m_conv_w, m_w_out, m_w_mlp1, m_w_mlp2, m_final_norm_g, v_c_ctx, v_w_mod, v_b_mod, v_w_in, v_q_norm_g, v_w_uq, v_kv_norm_g, v_w_ukv, v_conv_w, v_w_out, v_w_mlp1, v_w_mlp2, v_final_norm_g):
    t_lat, t_ctx = x.shape[1], ctx.shape[1]
    assert t_ctx == TOK_TILE and t_lat % TOK_TILE == 0 and t_lat % GRID_W == 0
    mx, my, mc = _pos()
    me = 4 * mx + 2 * my + mc
    j = 2 * mx + my
    ncol = w_mod.shape[2]
    x2, ctx2, tgt = x[0], ctx[0], loss_target[0]
    cctx_row = c_ctx.reshape(1, D_MODEL)

    b_sh = lax.dynamic_slice(b_mod, (0, j * ncol), (1, ncol))
    cw_pad = jnp.zeros((8, 128), F32).at[0:3, :].set(conv_w[0])
    c8, m_all = _mod_exchange(c, cctx_row, w_mod[0], b_sh, cw_pad)
    m_rows = jnp.transpose(m_all[:, 0:72, :], (1, 0, 2)).reshape(72, N_SHARD * ncol)
    mvec = lax.dynamic_slice(m_rows, (8 * me, 0), (1, 6 * D_MODEL)).reshape(6, D_MODEL)
    mctx = m_rows[64].reshape(6, D_MODEL)
    zeros6 = jnp.zeros((6, D_MODEL), F32)
    mod_a = jnp.stack([jnp.concatenate([mvec[0:2], zeros6], axis=0), jnp.concatenate([mctx[0:2], zeros6], axis=0)])
    mod_b = jnp.concatenate([mvec[2:6], jnp.zeros((4, D_MODEL), F32)], axis=0)
    cw_full = jnp.transpose(m_all[:, 72:80, 0:128], (1, 0, 2)).reshape(8, CONV_W)

    w_pack = _pack_shards(w_in[0], w_uq[0], w_ukv[0], w_out[0], w_mlp1[0], w_mlp2[0], BF16)
    gathered = _weight_gather(w_pack)
    g_in, g_uq, g_ukv, g_out, g_m1, g_m2 = _unpack_rows(gathered)
    w_in_f = _cols_from_shards(g_in)
    zc = lambda n: jnp.zeros((D_MODEL, n), BF16)
    w_in_p = jnp.concatenate([w_in_f[:, 0:384], zc(64), w_in_f[:, 384:416], zc(32), w_in_f[:, 416:]], axis=1)
    w_uq_f = _cols_from_shards(g_uq).reshape(Q_RANK, N_HEADS, QK_DIM)
    w_uq_p = jnp.pad(w_uq_f, ((0, 0), (0, 0), (0, HEAD_PAD - QK_DIM))).reshape(Q_RANK, N_HEADS * HEAD_PAD)
    w_ukv_f = _cols_from_shards(g_ukv).reshape(KV_RANK, N_HEADS, QK_NOPE + V_DIM)
    padh = lambda a: jnp.pad(a, ((0, 0), (0, 0), (0, HEAD_PAD - a.shape[2]))).reshape(KV_RANK, N_HEADS * HEAD_PAD)
    w_ukv_p = jnp.concatenate([padh(w_ukv_f[:, :, :QK_NOPE]), padh(w_ukv_f[:, :, QK_NOPE:])], axis=1)
    w_out_f = g_out.reshape(D_MODEL, D_MODEL)
    wo_attn = jnp.pad(w_out_f[0:512].reshape(N_HEADS, V_DIM, D_MODEL), ((0, 0), (0, HEAD_PAD - V_DIM), (0, 0)))
    wo_conv = w_out_f[512:]
    w1 = g_m1
    w2 = g_m2.reshape(D_FF, D_MODEL)

    cos_t, sin_a, sin_b = _rope_tables(t_lat, t_ctx)
    gf_row = final_norm_g.reshape(1, D_MODEL)

    z, q, k, v = _inproj_fwd(x2, ctx2, mod_a, w_in_p, q_norm_g, kv_norm_g, w_uq_p, w_ukv_p, cos_t, sin_a, sin_b)
    o, lse = _attn_fwd(q, k, v, t_lat)
    r, da, h2, dy2, dx1, y1, yv, conv, acc_mlp = _mlp_fwdbwd(o, z, x2, tgt, mod_b, gf_row, cw_full, wo_attn, wo_conv,
                                                               w1, w2)
    dy1, do, dgb, dyv, acc_mid = _mid_bwd(dx1, y1, z, yv, mod_b, wo_attn, wo_conv)
    kt = jnp.transpose(k, (0, 2, 1))
    dqt, dk, dv = _attn_bwd(q, k, v, kt, o, do, lse.reshape(N_HEADS, 1, t_lat), t_lat)
    gx, d_win, d_wuq, d_wukv, acc_in = _inproj_bwd(x2, ctx2, mod_a, z, dyv, dgb, dx1, dqt, dk, dv, cos_t, sin_a, sin_b,
                                                   w_in_p, w_uq_p, w_ukv_p, q_norm_g, kv_norm_g, cw_full)
    d_w1 = _wgrad(h2, da, "wgrad_mlp1", D_MODEL, FF_CHUNK, out_blocks=True)
    d_w2t = _wgrad(dy2, r, "wgrad_mlp2", D_MODEL, FF_CHUNK)
    d_wo = _wgrad_out(o, conv, dy1)
    d_wo_attn = d_wo[0:N_HEADS * HEAD_PAD].reshape(N_HEADS, HEAD_PAD, D_MODEL)
    d_wo_conv = d_wo[N_HEADS * HEAD_PAD:]

    pad_row = lambda a: jnp.pad(a, ((0, 0), (0, D_MODEL - a.shape[1])))
    sv = jnp.concatenate([
        acc_in[0:2], acc_mid[0:1], acc_mlp[2:4], acc_mlp[1:2],
        acc_in[2:4], acc_mlp[0:1], acc_in[4:5], acc_in[5:6], acc_in[6:9], acc_mlp[4:5],
        jnp.zeros((1, D_MODEL), F32)], axis=0)
    all_sv, red, o_cc, o_b, o_q, o_k, o_gf = _small_exchange(
        sv, w_mod[0], cctx_row, m_c_ctx.reshape(1, D_MODEL), v_c_ctx.reshape(1, D_MODEL),
        b_mod.reshape(6, D_MODEL), m_b_mod.reshape(6, D_MODEL), v_b_mod.reshape(6, D_MODEL),
        q_norm_g, m_q_norm_g, v_q_norm_g, kv_norm_g, m_kv_norm_g, v_kv_norm_g,
        gf_row, m_final_norm_g.reshape(1, D_MODEL), v_final_norm_g.reshape(1, D_MODEL))
    loss = red[14, 0]

    c9 = jnp.concatenate([c8[0::8], jnp.zeros((7, D_MODEL), F32)], axis=0)
    s_t = jnp.transpose(c9 * jax.nn.sigmoid(c9))
    dm_ex = all_sv[:, 0:6, :].reshape(8, 6 * D_MODEL)
    dm_ctx = jnp.concatenate([red[6:8].reshape(1, 2 * D_MODEL), jnp.zeros((1, 4 * D_MODEL), F32)], axis=1)
    dm16 = jnp.concatenate([dm_ex, dm_ctx, jnp.zeros((7, 6 * D_MODEL), F32)], axis=0)
    dm_sh = lax.dynamic_slice(dm16, (0, j * ncol), (16, ncol))
    g_wmod, d_wmod, nm_wmod, nv_wmod = _wmod_update(s_t, dm_sh, w_mod[0], m_w_mod[0], v_w_mod[0])

    g_cw = lax.dynamic_slice(red[11:14, 0:CONV_W], (0, j * 128), (3, 128))
    d_cw, nm_cw, nv_cw = _adamw_call(conv_w[0], g_cw, m_conv_w[0], v_conv_w[0], "adamw_conv")

    d_win_f = jnp.concatenate([d_win[:, 0:384], d_win[:, 448:480], d_win[:, 512:]], axis=1)
    d_wuq_f = d_wuq.reshape(Q_RANK, N_HEADS, HEAD_PAD)[:, :, 0:QK_DIM].reshape(Q_RANK, N_HEADS * QK_DIM)
    d_wukv3 = d_wukv.reshape(KV_RANK, 2, N_HEADS, HEAD_PAD)
    d_wukv_f = jnp.concatenate([d_wukv3[:, 0, :, 0:QK_NOPE], d_wukv3[:, 1, :, 0:V_DIM]], axis=2).reshape(KV_RANK, -1)
    d_wout_f = jnp.concatenate([d_wo_attn[:, 0:V_DIM, :].reshape(512, D_MODEL), d_wo_conv], axis=0)
    d_w2 = jnp.transpose(d_w2t)
    grads_pack = _pack_shards(_cols_to_shards(d_win_f), _cols_to_shards(d_wuq_f), _cols_to_shards(d_wukv_f),
                              d_wout_f.reshape(N_SHARD, 256, D_MODEL), d_w1,
                              d_w2.reshape(N_SHARD, 1024, D_MODEL), BF16)
    own, got = _rs_sibling(grads_pack)
    part = _add_pairs(own, got)
    slots = _rs_chips(part)
    g_half = _add_chips(slots)
    g_full = _rs_join(g_half)
    g_win, g_wuq, g_wukv, g_wout, g_w1, g_w2 = _unpack_rows(g_full)
    upd = {}
    for name, w_, g_, m_, v_ in (("in", w_in, g_win, m_w_in, v_w_in), ("uq", w_uq, g_wuq, m_w_uq, v_w_uq),
                                 ("ukv", w_ukv, g_wukv, m_w_ukv, v_w_ukv), ("out", w_out, g_wout, m_w_out, v_w_out),
                                 ("mlp1", w_mlp1, g_w1, m_w_mlp1, v_w_mlp1), ("mlp2", w_mlp2, g_w2, m_w_mlp2, v_w_mlp2)):
        upd[name] = _adamw_call(w_[0], g_, m_[0], v_[0], "adamw_" + name)

    def four(o4, shape):
        return [o4[r].reshape(shape) for r in range(4)]

    cc4 = four(o_cc, (D_MODEL,))
    b4 = [o_b[r].reshape(1, 6 * D_MODEL) for r in range(4)]
    q4 = four(o_q, (1, Q_RANK))
    k4 = four(o_k, (1, KV_RANK))
    gf4 = four(o_gf, (D_MODEL,))
    big = {"in": g_win, "uq": g_wuq, "ukv": g_wukv, "out": g_wout, "mlp1": g_w1, "mlp2": g_w2}

    def leaf(idx):
        wm = (g_wmod, d_wmod, nm_wmod, nv_wmod)[idx]
        cwv = (g_cw, d_cw, nm_cw, nv_cw)[idx]
        bigv = {n: (big[n] if idx == 0 else upd[n][idx - 1]) for n in big}
        return [cc4[idx], wm[None], b4[idx], bigv["in"][None], q4[idx], bigv["uq"][None], k4[idx], bigv["ukv"][None],
                cwv[None], bigv["out"][None], bigv["mlp1"][None], bigv["mlp2"][None], gf4[idx]]

    return (loss, gx[None], *leaf(0), *leaf(1), *leaf(2), *leaf(3))
```

```python
import functools
import math

import jax
import jax.numpy as jnp
from jax import lax
from jax.experimental import pallas as pl
from jax.experimental.pallas import tpu as pltpu

F32 = jnp.float32
BF16 = jnp.bfloat16
MESH = pl.DeviceIdType.MESH
HIGHEST = lax.Precision.HIGHEST

D_MODEL = 1024
N_HEADS = 8
QK_NOPE = 64
QK_ROPE = 32
QK_DIM = QK_NOPE + QK_ROPE
V_DIM = 64
Q_RANK = 256
KV_RANK = 128
CONV_W = 512
D_FF = 4096
GRID_W = 64
ROPE_THETA = 10000.0
EPS = 1e-6
ATTN_SCALE = 1.0 / math.sqrt(QK_DIM)
HEAD_PAD = 128
Z_COLS = 2048
ROPE_LANE0 = QK_NOPE
N_SHARD = 4
TOK_TILE = 256
FF_CHUNK = 1024
KEY_CHUNK = 1024

ADAM_LR = 0.001
ADAM_B1 = 0.9
ADAM_B2 = 0.999
ADAM_EPS = 1e-08
ADAM_WD = 0.01
ADAM_STEP = 10

IN_COLS = Q_RANK + KV_RANK + QK_ROPE + 3 * CONV_W
PK_IN, PK_UQ, PK_UKV, PK_PAD = 488, 48, 32, 8
PK_ROWS = PK_IN + PK_UQ + PK_UKV + PK_PAD
LOG2E = 1.4426950408889634

VMEM_LIMIT = 56 * 1024 * 1024
STAGE_VMEM_LIMIT = 32 * 1024 * 1024


def _pos():
    return lax.axis_index("x"), lax.axis_index("y"), lax.axis_index("c")


def _dot(a, b):
    return jnp.dot(a, b, preferred_element_type=F32)


def _dot_nt(a, b):
    return lax.dot_general(a, b, (((1,), (1,)), ((), ())), preferred_element_type=F32)


def _dot_tn(a, b):
    return lax.dot_general(a, b, (((0,), (0,)), ((), ())), preferred_element_type=F32)


def _rope(v, cos, sa, sb):
    return v * cos + pltpu.roll(v, 8, 1) * sa + pltpu.roll(v, HEAD_PAD - 8, 1) * sb


def _unrope(g, cos, sa, sb):
    return g * cos + pltpu.roll(g * sa, HEAD_PAD - 8, 1) + pltpu.roll(g * sb, 8, 1)


def _sigmoid(v):
    return 1.0 / (1.0 + jnp.exp(-v))


def _adamw(w, g, m, v):
    m = ADAM_B1 * m + (1.0 - ADAM_B1) * g
    v = ADAM_B2 * v + (1.0 - ADAM_B2) * (g * g)
    m_hat = m / (1.0 - ADAM_B1 ** ADAM_STEP)
    v_hat = v / (1.0 - ADAM_B2 ** ADAM_STEP)
    delta = -ADAM_LR * (m_hat / (jnp.sqrt(v_hat) + ADAM_EPS) + ADAM_WD * w)
    return delta, m, v


def _shift_rows(u, prev_row, next_row):
    n = u.shape[0]
    rows = lax.broadcasted_iota(jnp.int32, u.shape, 0)
    um1 = jnp.where(rows == 0, prev_row, pltpu.roll(u, 1, 0))
    up1 = jnp.where(rows == n - 1, next_row, pltpu.roll(u, n - 1, 0))
    return um1, up1


def _const_spec(shape):
    nd = len(shape)
    return pl.BlockSpec(shape, lambda *_: (0,) * nd)


def _resident_spec(shape):
    nd = len(shape)
    return pl.BlockSpec(shape, lambda *_: (0,) * nd, pipeline_mode=pl.Buffered(1))


def _peer(r, x, y, c):
    px = 1 - x if r & 4 else x
    py = 1 - y if r & 2 else y
    pc = 1 - c if r & 1 else c
    return (px, py, pc)


def _mod_exchange(c_row, cctx_row, w_mod_sh, b_sh, cw_sh):
    ncol = w_mod_sh.shape[1]

    def body(c_ref, cctx_ref, w_ref, b_ref, cw_ref, c8_ref, m_ref, mine_ref, ssem, rsem, ssem2, rsem2):
        x, y, c = _pos()
        me = 4 * x + 2 * y + c
        j = 2 * x + y
        mine_ref[...] = jnp.zeros(mine_ref.shape, F32)
        mine_ref[0:1, :] = c_ref[...]
        my_rows = pl.ds(pl.multiple_of(8 * me, 8), 8)
        sends = []
        for r in range(1, 8):
            cp = pltpu.make_async_remote_copy(
                src_ref=mine_ref, dst_ref=c8_ref.at[my_rows], send_sem=ssem.at[r - 1], recv_sem=rsem.at[r - 1],
                device_id=_peer(r, x, y, c), device_id_type=MESH)
            cp.start()
            sends.append(cp)
        for cp in sends:
            cp.wait()
        c8_ref[my_rows, :] = mine_ref[...]
        c8_ref[64:72, :] = jnp.zeros((8, D_MODEL), F32)
        c8_ref[64:65, :] = cctx_ref[...]
        cv = c8_ref[...]
        s = cv * _sigmoid(cv)
        m = jnp.dot(s, w_ref[...], precision=HIGHEST, preferred_element_type=F32) + b_ref[...]
        m_ref[j, 0:72, :] = m
        m_ref[j, 72:80, :] = jnp.zeros((8, ncol), F32)
        m_ref[j, 72:80, 0:128] = cw_ref[...]
        sends2 = []
        for k, r in enumerate((4, 2, 6)):
            cp = pltpu.make_async_remote_copy(
                src_ref=m_ref.at[j], dst_ref=m_ref.at[j], send_sem=ssem2.at[k], recv_sem=rsem2.at[k],
                device_id=_peer(r, x, y, c), device_id_type=MESH)
            cp.start()
            sends2.append(cp)
        for cp in sends2:
            cp.wait()

    vm = pl.BlockSpec(memory_space=pltpu.VMEM)
    return pl.pallas_call(
        body, name="mod_exchange",
        out_shape=(jax.ShapeDtypeStruct((72, D_MODEL), F32), jax.ShapeDtypeStruct((N_SHARD, 80, ncol), F32)),
        in_specs=[vm] * 5, out_specs=(vm, vm),
        scratch_shapes=[pltpu.VMEM((8, D_MODEL), F32), pltpu.SemaphoreType.DMA((7,)), pltpu.SemaphoreType.DMA((7,)),
                        pltpu.SemaphoreType.DMA((3,)), pltpu.SemaphoreType.DMA((3,))],
        compiler_params=pltpu.CompilerParams(vmem_limit_bytes=VMEM_LIMIT),
    )(c_row, cctx_row, w_mod_sh, b_sh, cw_sh)


def _chips(x, y):
    return [(1 - x, y), (x, 1 - y), (1 - x, 1 - y)]


def _halves(ref, c, align):
    hr = ref.shape[-2] // 2
    return (pl.ds(pl.multiple_of(c * hr, align), hr), pl.ds(pl.multiple_of((1 - c) * hr, align), hr))


class _ShardGather:
    def __init__(self, refs, ssem, rsem, fsend, frecv):
        self.refs, self.sems = refs, (ssem, rsem, fsend, frecv)
        self.x, self.y, self.c = _pos()
        self.j = 2 * self.x + self.y

    def _ici(self, a, k, slot):
        g = self.refs[a]
        ssem, rsem, _, _ = self.sems
        mine, _ = _halves(g, self.c, 16)
        px, py = _chips(self.x, self.y)[k]
        return pltpu.make_async_remote_copy(
            src_ref=g.at[self.j, mine], dst_ref=g.at[slot, mine], send_sem=ssem.at[3 * a + k],
            recv_sem=rsem.at[3 * a + k], device_id=(px, py, self.c), device_id_type=MESH)

    def _d2d(self, a, k, to_other_half):
        g = self.refs[a]
        _, _, fsend, frecv = self.sems
        mine, theirs = _halves(g, self.c, 16)
        px, py = _chips(self.x, self.y)[k]
        jk = 2 * px + py
        return pltpu.make_async_remote_copy(
            src_ref=g.at[jk, mine], dst_ref=g.at[jk, theirs if to_other_half else mine],
            send_sem=fsend.at[3 * a + k], recv_sem=frecv.at[3 * a + k],
            device_id=(self.x, self.y, 1 - self.c), device_id_type=MESH)

    def start(self):
        for a in range(len(self.refs)):
            for k in range(3):
                self._ici(a, k, self.j).start()

    def forward(self):
        for a in range(len(self.refs)):
            for k, (px, py) in enumerate(_chips(self.x, self.y)):
                self._ici(a, k, 2 * px + py).wait_recv()
                self._d2d(a, k, False).start()

    def finish(self):
        for a in range(len(self.refs)):
            for k in range(3):
                self._d2d(a, k, True).wait()
                self._ici(a, k, self.j).wait_send()


def _gather_sems(n_arrays):
    return [pltpu.SemaphoreType.DMA((3 * n_arrays,)) for _ in range(4)]


def _weight_prep(small, wout, w1, w2):
    srcs = (small, wout, w1, w2)
    n_split = 4

    def body(s_ref, o_ref, a_ref, b_ref, ga_ref, gout_ref, g1_ref, g2_ref,
             f0, f1, f2, f3, b0, b1, b2, b3, lsem_in, lsem_out, ssem, rsem, fsend, frecv):
        x, y, c = _pos()
        j = 2 * x + y
        ins, f32s, bfs = (s_ref, o_ref, a_ref, b_ref), (f0, f1, f2, f3), (b0, b1, b2, b3)
        outs = (ga_ref, gout_ref, g1_ref, g2_ref)

        def pieces(rows):
            step = rows // n_split
            return [pl.ds(q * step, step) for q in range(n_split)]

        for t in range(4):
            for sl in pieces(ins[t].shape[0]):
                pltpu.make_async_copy(ins[t].at[sl], f32s[t].at[sl], lsem_in.at[t]).start()
        gather = _ShardGather((ga_ref,), ssem, rsem, fsend, frecv)
        for t in range(4):
            pltpu.make_async_copy(ins[t], f32s[t], lsem_in.at[t]).wait()
            bfs[t][...] = f32s[t][...].astype(BF16)
            for sl in pieces(ins[t].shape[0]):
                pltpu.make_async_copy(bfs[t].at[sl], outs[t].at[j, sl], lsem_out.at[t]).start()
            if t == 0:
                pltpu.make_async_copy(bfs[0], ga_ref.at[j], lsem_out.at[0]).wait()
                gather.start()
        gather.forward()
        gather.finish()
        for t in range(1, 4):
            pltpu.make_async_copy(bfs[t], outs[t].at[j], lsem_out.at[t]).wait()

    hbm = pl.BlockSpec(memory_space=pl.ANY)
    return pl.pallas_call(
        body, name="weight_prep",
        out_shape=tuple(jax.ShapeDtypeStruct((N_SHARD,) + a.shape, BF16) for a in srcs),
        in_specs=[hbm] * 4, out_specs=(hbm,) * 4,
        scratch_shapes=[pltpu.VMEM(a.shape, F32) for a in srcs] + [pltpu.VMEM(a.shape, BF16) for a in srcs]
        + [pltpu.SemaphoreType.DMA((4,)), pltpu.SemaphoreType.DMA((4,))] + _gather_sems(1),
        compiler_params=pltpu.CompilerParams(vmem_limit_bytes=STAGE_VMEM_LIMIT),
    )(*srcs)


def _rs_sibling(arrs, name):
    n = len(arrs)

    def body(*refs):
        g_refs, got_refs, ssem, rsem = refs[:n], refs[n:2 * n], refs[2 * n], refs[2 * n + 1]
        x, y, c = _pos()
        for a in range(n):
            _, theirs = _halves(g_refs[a], c, 16)
            for s in range(N_SHARD):
                pltpu.make_async_remote_copy(
                    src_ref=g_refs[a].at[s, theirs], dst_ref=got_refs[a].at[s], send_sem=ssem.at[a],
                    recv_sem=rsem.at[a], device_id=(x, y, 1 - c), device_id_type=MESH).start()
        for a in range(n):
            _, theirs = _halves(g_refs[a], c, 16)
            pltpu.make_async_remote_copy(
                src_ref=g_refs[a].at[:, theirs], dst_ref=got_refs[a], send_sem=ssem.at[a], recv_sem=rsem.at[a],
                device_id=(x, y, 1 - c), device_id_type=MESH).wait()

    hbm = pl.BlockSpec(memory_space=pl.ANY)
    return pl.pallas_call(
        body, name=name,
        out_shape=tuple(jax.ShapeDtypeStruct((N_SHARD, a.shape[1] // 2, D_MODEL), BF16) for a in arrs),
        in_specs=[hbm] * n, out_specs=(hbm,) * n,
        scratch_shapes=[pltpu.SemaphoreType.DMA((n,)), pltpu.SemaphoreType.DMA((n,))],
    )(*arrs)


class _ChipScatter:
    def __init__(self, parts, gots, ssem, rsem):
        self.parts, self.gots, self.ssem, self.rsem = parts, gots, ssem, rsem
        self.x, self.y, self.c = _pos()

    def _copy(self, a, k):
        px, py = _chips(self.x, self.y)[k]
        return pltpu.make_async_remote_copy(
            src_ref=self.parts[a].at[2 * px + py], dst_ref=self.gots[a].at[k], send_sem=self.ssem.at[3 * a + k],
            recv_sem=self.rsem.at[3 * a + k], device_id=(px, py, self.c), device_id_type=MESH)

    def start(self):
        for a in range(len(self.parts)):
            for k in range(3):
                self._copy(a, k).start()

    def finish(self):
        for a in range(len(self.parts)):
            for k in range(3):
                self._copy(a, k).wait()


def _rs_chips(parts, name):
    n = len(parts)

    def body(*refs):
        scatter = _ChipScatter(refs[:n], refs[n:2 * n], refs[2 * n], refs[2 * n + 1])
        scatter.start()
        scatter.finish()

    hbm = pl.BlockSpec(memory_space=pl.ANY)
    return pl.pallas_call(
        body, name=name, out_shape=tuple(jax.ShapeDtypeStruct((3,) + p.shape[1:], BF16) for p in parts),
        in_specs=[hbm] * n, out_specs=(hbm,) * n,
        scratch_shapes=[pltpu.SemaphoreType.DMA((3 * n,)), pltpu.SemaphoreType.DMA((3 * n,))],
    )(*parts)


def _rs_join(halves, name):
    n = len(halves)

    def body(*refs):
        h_refs, f_refs, stages = refs[:n], refs[n:2 * n], refs[2 * n:3 * n]
        lsem_in, lsem_out, ssem, rsem = refs[3 * n:]
        x, y, c = _pos()
        remote = []
        for a in range(n):
            mine, _ = _halves(f_refs[a], c, 8)
            cp = pltpu.make_async_remote_copy(
                src_ref=h_refs[a], dst_ref=f_refs[a].at[mine], send_sem=ssem.at[a], recv_sem=rsem.at[a],
                device_id=(x, y, 1 - c), device_id_type=MESH)
            cp.start()
            remote.append(cp)
            pltpu.make_async_copy(h_refs[a], stages[a], lsem_in.at[a]).start()
        local = []
        for a in range(n):
            mine, _ = _halves(f_refs[a], c, 8)
            pltpu.make_async_copy(h_refs[a], stages[a], lsem_in.at[a]).wait()
            cp = pltpu.make_async_copy(stages[a], f_refs[a].at[mine], lsem_out.at[a])
            cp.start()
            local.append(cp)
        for cp in remote + local:
            cp.wait()

    hbm = pl.BlockSpec(memory_space=pl.ANY)
    return pl.pallas_call(
        body, name=name,
        out_shape=tuple(jax.ShapeDtypeStruct((2 * h.shape[0], D_MODEL), F32) for h in halves),
        in_specs=[hbm] * n, out_specs=(hbm,) * n,
        scratch_shapes=[pltpu.VMEM(h.shape, F32) for h in halves]
        + [pltpu.SemaphoreType.DMA((n,)) for _ in range(4)],
        compiler_params=pltpu.CompilerParams(vmem_limit_bytes=STAGE_VMEM_LIMIT),
    )(*halves)


def _row_block(rows):
    return max(b for b in range(16, 513, 16) if rows % b == 0)


def _add_pairs(arr, got, c_idx, name):
    hr = got.shape[1]
    rb = _row_block(hr)
    nb = hr // rb

    def body(c_ref, a_ref, b_ref, o_ref):
        o_ref[...] = (a_ref[...].astype(F32) + b_ref[...].astype(F32)).astype(BF16)

    spec = pl.BlockSpec((1, rb, D_MODEL), lambda s, r, c_ref: (s, r, 0))
    grid_spec = pltpu.PrefetchScalarGridSpec(
        num_scalar_prefetch=1, grid=(N_SHARD, nb),
        in_specs=[pl.BlockSpec((1, rb, D_MODEL), lambda s, r, c_ref: (s, c_ref[0] * nb + r, 0)), spec],
        out_specs=spec)
    return pl.pallas_call(
        body, name=name, grid_spec=grid_spec, out_shape=jax.ShapeDtypeStruct(got.shape, BF16),
    )(c_idx, arr, got)


def _add_chips(part, got, j_idx, name):
    hr = got.shape[1]
    rb = _row_block(hr)

    def body(j_ref, p_ref, g_ref, o_ref):
        acc = p_ref[0].astype(F32)
        for k in range(3):
            acc = acc + g_ref[k].astype(F32)
        o_ref[...] = acc

    grid_spec = pltpu.PrefetchScalarGridSpec(
        num_scalar_prefetch=1, grid=(hr // rb,),
        in_specs=[pl.BlockSpec((1, rb, D_MODEL), lambda r, j_ref: (j_ref[0], r, 0)),
                  pl.BlockSpec((3, rb, D_MODEL), lambda r, j_ref: (0, r, 0))],
        out_specs=pl.BlockSpec((rb, D_MODEL), lambda r, j_ref: (r, 0)))
    return pl.pallas_call(
        body, name=name, grid_spec=grid_spec, out_shape=jax.ShapeDtypeStruct((hr, D_MODEL), F32),
    )(j_idx, part, got)


def _small_exchange(sv, w_mod_sh, cctx, m_cctx, v_cctx, bmod, m_bmod, v_bmod, qg, m_qg, v_qg, kvg, m_kvg, v_kvg,
                    gf, m_gf, v_gf):
    ncol = w_mod_sh.shape[1]

    def body(sv_ref, w_ref, cctx_ref, mcc_ref, vcc_ref, b_ref, mb_ref, vb_ref, qg_ref, mq_ref, vq_ref,
             kg_ref, mk_ref, vk_ref, gf_ref, mgf_ref, vgf_ref,
             all_ref, red_ref, occ_ref, ob_ref, oq_ref, ok_ref, ogf_ref,
             vec_ref, part_ref, ssem, rsem, ssem2, rsem2):
        x, y, c = _pos()
        me = 4 * x + 2 * y + c
        j = 2 * x + y
        sends = []
        for r in range(1, 8):
            cp = pltpu.make_async_remote_copy(
                src_ref=sv_ref, dst_ref=all_ref.at[me], send_sem=ssem.at[r - 1], recv_sem=rsem.at[r - 1],
                device_id=_peer(r, x, y, c), device_id_type=MESH)
            cp.start()
            sends.append(cp)
        for cp in sends:
            cp.wait()
        all_ref[me] = sv_ref[...]
        red = all_ref[0]
        for d in range(1, 8):
            red = red + all_ref[d]
        red_ref[...] = red
        vec_ref[...] = jnp.zeros(vec_ref.shape, F32)

        @pl.when(j == 0)
        def _():
            vec_ref[0:1, 0:1024] = red[6:7, :]
            vec_ref[0:1, 1024:1536] = red[7:8, 0:512]

        @pl.when(j == 1)
        def _():
            vec_ref[0:1, 0:512] = red[7:8, 512:1024]

        part = lax.dot_general(vec_ref[...], w_ref[...], (((1,), (1,)), ((), ())), precision=HIGHEST,
                               preferred_element_type=F32)
        part_ref[j] = part
        sends2 = []
        for k, r in enumerate((4, 2, 6)):
            cp = pltpu.make_async_remote_copy(
                src_ref=part_ref.at[j], dst_ref=part_ref.at[j], send_sem=ssem2.at[k], recv_sem=rsem2.at[k],
                device_id=_peer(r, x, y, c), device_id_type=MESH)
            cp.start()
            sends2.append(cp)
        for cp in sends2:
            cp.wait()
        tot = part_ref[0]
        for s in range(1, N_SHARD):
            tot = tot + part_ref[s]
        cc = cctx_ref[...]
        sg = _sigmoid(cc)
        g_cc = tot[0:1, :] * (sg * (1.0 + cc * (1.0 - sg)))
        d_, m_, v_ = _adamw(cc, g_cc, mcc_ref[...], vcc_ref[...])
        occ_ref[0:1, :] = g_cc
        occ_ref[1:2, :] = d_
        occ_ref[2:3, :] = m_
        occ_ref[3:4, :] = v_
        occ_ref[4:8, :] = jnp.zeros((4, D_MODEL), F32)
        g_b = red[0:6, :]
        pad = jnp.concatenate([red[6:8, :], jnp.zeros((4, D_MODEL), F32)], axis=0)
        g_b = g_b + pad
        d_, m_, v_ = _adamw(b_ref[...], g_b, mb_ref[...], vb_ref[...])
        ob_ref[0] = g_b
        ob_ref[1] = d_
        ob_ref[2] = m_
        ob_ref[3] = v_
        g_q = red[9:10, 0:Q_RANK]
        d_, m_, v_ = _adamw(qg_ref[...], g_q, mq_ref[...], vq_ref[...])
        oq_ref[0:1, :] = g_q
        oq_ref[1:2, :] = d_
        oq_ref[2:3, :] = m_
        oq_ref[3:4, :] = v_
        oq_ref[4:8, :] = jnp.zeros((4, Q_RANK), F32)
        g_k = red[10:11, 0:KV_RANK]
        d_, m_, v_ = _adamw(kg_ref[...], g_k, mk_ref[...], vk_ref[...])
        ok_ref[0:1, :] = g_k
        ok_ref[1:2, :] = d_
        ok_ref[2:3, :] = m_
        ok_ref[3:4, :] = v_
        ok_ref[4:8, :] = jnp.zeros((4, KV_RANK), F32)
        g_f = red[8:9, :]
        d_, m_, v_ = _adamw(gf_ref[...], g_f, mgf_ref[...], vgf_ref[...])
        ogf_ref[0:1, :] = g_f
        ogf_ref[1:2, :] = d_
        ogf_ref[2:3, :] = m_
        ogf_ref[3:4, :] = v_
        ogf_ref[4:8, :] = jnp.zeros((4, D_MODEL), F32)

    vm = pl.BlockSpec(memory_space=pltpu.VMEM)
    out_shape = (
        jax.ShapeDtypeStruct((8, 16, D_MODEL), F32),
        jax.ShapeDtypeStruct((16, D_MODEL), F32),
        jax.ShapeDtypeStruct((8, D_MODEL), F32),
        jax.ShapeDtypeStruct((4, 6, D_MODEL), F32),
        jax.ShapeDtypeStruct((8, Q_RANK), F32),
        jax.ShapeDtypeStruct((8, KV_RANK), F32),
        jax.ShapeDtypeStruct((8, D_MODEL), F32),
    )
    return pl.pallas_call(
        body, name="small_exchange", out_shape=out_shape, in_specs=[vm] * 17, out_specs=tuple([vm] * 7),
        scratch_shapes=[pltpu.VMEM((8, ncol), F32), pltpu.VMEM((N_SHARD, 8, D_MODEL), F32),
                        pltpu.SemaphoreType.DMA((7,)), pltpu.SemaphoreType.DMA((7,)),
                        pltpu.SemaphoreType.DMA((3,)), pltpu.SemaphoreType.DMA((3,))],
        compiler_params=pltpu.CompilerParams(vmem_limit_bytes=VMEM_LIMIT),
    )(sv, w_mod_sh, cctx, m_cctx, v_cctx, bmod, m_bmod, v_bmod, qg, m_qg, v_qg, kvg, m_kvg, v_kvg, gf, m_gf, v_gf)


def _inproj_fwd(x2, ctx2, mod_a, w_in, qg, kvg, w_uq, w_ukv, cos_t, sin_a, sin_b):
    t_lat, t_ctx = x2.shape[0], ctx2.shape[0]
    tm = TOK_TILE
    n_lat = t_lat // tm
    n_all = n_lat + t_ctx // tm
    e_rows = t_lat + t_ctx

    def body(x_ref, ctx_ref, mod_ref, win_ref, qg_ref, kvg_ref, wuq_ref, wukv_ref, cos_ref, sa_ref, sb_ref,
             z_ref, q_ref, k_ref, v_ref, kt_ref):
        i = pl.program_id(0)
        xin = jnp.where(i < n_lat, x_ref[...], ctx_ref[...])
        xn = xin * lax.rsqrt(jnp.mean(xin * xin, axis=-1, keepdims=True) + EPS)
        h1 = (xn * (1.0 + mod_ref[0, 1:2, :]) + mod_ref[0, 0:1, :]).astype(BF16)
        z = _dot(h1, win_ref[...])
        z_ref[...] = z
        cos, sa, sb = cos_ref[...], sa_ref[...], sb_ref[...]
        cq = z[:, 0:Q_RANK]
        cqn = (cq * lax.rsqrt(jnp.mean(cq * cq, axis=-1, keepdims=True) + EPS) * qg_ref[...]).astype(BF16)
        q = _dot(cqn, wuq_ref[...])
        ckv = z[:, Q_RANK:Q_RANK + KV_RANK]
        ckvn = (ckv * lax.rsqrt(jnp.mean(ckv * ckv, axis=-1, keepdims=True) + EPS) * kvg_ref[...]).astype(BF16)
        kv = _dot(ckvn, wukv_ref[...])
        kr = _rope(z[:, Q_RANK + KV_RANK:Q_RANK + KV_RANK + HEAD_PAD], cos, sa, sb)
        for h in range(N_HEADS):
            lo = h * HEAD_PAD
            q_ref[h] = _rope(q[:, lo:lo + HEAD_PAD], cos, sa, sb).astype(BF16)
            kh = kv[:, lo:lo + HEAD_PAD] + kr
            k_ref[h] = kh.astype(BF16)
            kt_ref[h] = kh.T.astype(BF16)
            v_ref[h] = kv[:, N_HEADS * HEAD_PAD + lo:N_HEADS * HEAD_PAD + lo + HEAD_PAD].astype(BF16)

    row = lambda i: (i, 0)
    head_spec = pl.BlockSpec((N_HEADS, tm, HEAD_PAD), lambda i: (0, i, 0))
    head_shape = jax.ShapeDtypeStruct((N_HEADS, e_rows, HEAD_PAD), BF16)
    return pl.pallas_call(
        body, name="inproj_fwd", grid=(n_all,),
        out_shape=(jax.ShapeDtypeStruct((e_rows, Z_COLS), F32), head_shape, head_shape, head_shape,
                   jax.ShapeDtypeStruct((N_HEADS, HEAD_PAD, e_rows), BF16)),
        in_specs=[
            pl.BlockSpec((tm, D_MODEL), lambda i: (jnp.minimum(i, n_lat - 1), 0)),
            _const_spec((tm, D_MODEL)),
            pl.BlockSpec((1, 8, D_MODEL), lambda i: (i // n_lat, 0, 0)),
            _const_spec(w_in.shape), _const_spec(qg.shape), _const_spec(kvg.shape),
            _const_spec(w_uq.shape), _const_spec(w_ukv.shape),
            pl.BlockSpec((tm, HEAD_PAD), row), pl.BlockSpec((tm, HEAD_PAD), row), pl.BlockSpec((tm, HEAD_PAD), row),
        ],
        out_specs=(pl.BlockSpec((tm, Z_COLS), row), head_spec, head_spec, head_spec,
                   pl.BlockSpec((N_HEADS, HEAD_PAD, tm), lambda i: (0, 0, i))),
        compiler_params=pltpu.CompilerParams(vmem_limit_bytes=VMEM_LIMIT),
    )(x2, ctx2, mod_a, w_in, qg, kvg, w_uq, w_ukv, cos_t, sin_a, sin_b)


def _attn_fwd(q, k, v, t_lat, shard_arrays):
    e_rows = k.shape[1]
    tq = 256
    n_chunks = max(1, e_rows // KEY_CHUNK)
    bounds = [(ci * KEY_CHUNK, KEY_CHUNK if ci < n_chunks - 1 else e_rows - ci * KEY_CHUNK) for ci in range(n_chunks)]
    c2 = ATTN_SCALE * LOG2E

    def body(q_ref, k_ref, v_ref, o_ref, lse_ref):
        qb = q_ref[0]
        m = l = acc = None
        for lo, n in bounds:
            s = _dot_nt(qb, k_ref[0, lo:lo + n, :])
            mc = jnp.max(s, axis=-1, keepdims=True)
            m_new = mc if m is None else jnp.maximum(m, mc)
            p = jnp.exp2((s - m_new) * c2)
            pv = _dot(p.astype(BF16), v_ref[0, lo:lo + n, :])
            ps = jnp.sum(p, axis=-1, keepdims=True)
            if m is None:
                l, acc = ps, pv
            else:
                alpha = jnp.exp2((m - m_new) * c2)
                l = l * alpha + ps
                acc = acc * alpha + pv
            m = m_new
        o_ref[0] = (acc * (1.0 / l)).astype(BF16)
        lse_ref[0] = (m * ATTN_SCALE + jnp.log(l)) * LOG2E

    n_w = len(shard_arrays)
    n_q = t_lat // tq

    def body_with_gather(q_ref, k_ref, v_ref, *rest):
        o_ref, lse_ref = rest[n_w], rest[n_w + 1]
        gather = _ShardGather(rest[n_w + 2:2 * n_w + 2], *rest[2 * n_w + 2:])
        h, i = pl.program_id(0), pl.program_id(1)
        pl.when(jnp.logical_and(h == 0, i == 0))(gather.start)
        pl.when(jnp.logical_and(h == N_HEADS // 2, i == 0))(gather.forward)
        body(q_ref, k_ref, v_ref, o_ref, lse_ref)
        pl.when(jnp.logical_and(h == N_HEADS - 1, i == n_q - 1))(gather.finish)

    hbm = pl.BlockSpec(memory_space=pl.ANY)
    return pl.pallas_call(
        body_with_gather, name="attn_fwd", grid=(N_HEADS, n_q),
        out_shape=(jax.ShapeDtypeStruct((N_HEADS, t_lat, HEAD_PAD), BF16),
                   jax.ShapeDtypeStruct((N_HEADS, t_lat, 1), F32))
        + tuple(jax.ShapeDtypeStruct(a.shape, a.dtype) for a in shard_arrays),
        in_specs=[pl.BlockSpec((1, tq, HEAD_PAD), lambda h, i: (h, i, 0)),
                  pl.BlockSpec((1, e_rows, HEAD_PAD), lambda h, i: (h, 0, 0)),
                  pl.BlockSpec((1, e_rows, HEAD_PAD), lambda h, i: (h, 0, 0))] + [hbm] * n_w,
        out_specs=(pl.BlockSpec((1, tq, HEAD_PAD), lambda h, i: (h, i, 0)),
                   pl.BlockSpec((1, tq, 1), lambda h, i: (h, i, 0))) + (hbm,) * n_w,
        input_output_aliases={3 + a: 2 + a for a in range(n_w)},
        scratch_shapes=_gather_sems(n_w),
        compiler_params=pltpu.CompilerParams(vmem_limit_bytes=VMEM_LIMIT),
    )(q, k, v, *shard_arrays)


def _attn_bwd(q, k, v, kt, o, do, lse_row, t_lat, parts):
    e_rows = k.shape[1]
    tq = 256
    n_p = len(parts)
    n_q = t_lat // tq

    def body(q_ref, k_ref, v_ref, kt_ref, o_ref, do_ref, lse_ref, *rest):
        dqt_ref, dk_ref, dv_ref = rest[n_p:n_p + 3]
        scatter = _ChipScatter(rest[:n_p], rest[n_p + 3:2 * n_p + 3], rest[2 * n_p + 3], rest[2 * n_p + 4])
        h, i = pl.program_id(0), pl.program_id(1)
        pl.when(jnp.logical_and(h == 0, i == 0))(scatter.start)
        qb, dob = q_ref[0], do_ref[0]
        prod = o_ref[0].astype(F32) * dob.astype(F32)
        delta = lax.dot_general(jnp.ones((8, HEAD_PAD), F32), prod, (((1,), (1,)), ((), ())), precision=HIGHEST,
                                preferred_element_type=F32)[0:1, :]
        pt = jnp.exp2(_dot_nt(k_ref[0], qb) * (ATTN_SCALE * LOG2E) - lse_ref[0])
        dpt = _dot_nt(v_ref[0], dob)
        dst = (pt * (dpt - delta) * ATTN_SCALE).astype(BF16)
        dv_c = _dot(pt.astype(BF16), dob)
        dk_c = _dot(dst, qb)
        dqt_ref[0] = _dot(kt_ref[0], dst)

        @pl.when(i == 0)
        def _():
            dk_ref[0] = dk_c
            dv_ref[0] = dv_c

        @pl.when(i > 0)
        def _():
            dk_ref[0] += dk_c
            dv_ref[0] += dv_c

        pl.when(jnp.logical_and(h == N_HEADS - 1, i == n_q - 1))(scatter.finish)

    hbm = pl.BlockSpec(memory_space=pl.ANY)
    qspec = pl.BlockSpec((1, tq, HEAD_PAD), lambda h, i: (h, i, 0))
    kspec = pl.BlockSpec((1, e_rows, HEAD_PAD), lambda h, i: (h, 0, 0))
    return pl.pallas_call(
        body, name="attn_bwd", grid=(N_HEADS, n_q),
        out_shape=(jax.ShapeDtypeStruct((N_HEADS, HEAD_PAD, t_lat), F32),
                   jax.ShapeDtypeStruct((N_HEADS, e_rows, HEAD_PAD), F32),
                   jax.ShapeDtypeStruct((N_HEADS, e_rows, HEAD_PAD), F32))
        + tuple(jax.ShapeDtypeStruct((3,) + p.shape[1:], BF16) for p in parts),
        in_specs=[qspec, kspec, kspec, pl.BlockSpec((1, HEAD_PAD, e_rows), lambda h, i: (h, 0, 0)), qspec, qspec,
                  pl.BlockSpec((1, 1, tq), lambda h, i: (h, 0, i))] + [hbm] * n_p,
        out_specs=(pl.BlockSpec((1, HEAD_PAD, tq), lambda h, i: (h, 0, i)), kspec, kspec) + (hbm,) * n_p,
        scratch_shapes=[pltpu.SemaphoreType.DMA((3 * n_p,)), pltpu.SemaphoreType.DMA((3 * n_p,))],
        compiler_params=pltpu.CompilerParams(vmem_limit_bytes=VMEM_LIMIT),
    )(q, k, v, kt, o, do, lse_row, *parts)


def _halo_specs(tm, col_block):
    per = tm // 8
    prev = pl.BlockSpec((8, CONV_W), lambda i: (jnp.maximum(i * per - 1, 0), col_block))
    nxt = pl.BlockSpec((8, CONV_W), lambda i: ((i + 1) * per, col_block))
    return prev, nxt


def _mlp_fwdbwd(o, z, x2, tgt, mod_b, gf, cw, wo_attn, wo_conv, w1, w2):
    t_lat = x2.shape[0]
    tm = TOK_TILE
    n_lat = t_lat // tm
    n_ff = D_FF // FF_CHUNK

    def body(o_ref, gb_ref, gc_ref, xi_ref, gcp_ref, xip_ref, gcn_ref, xin_ref, cw_ref, woa_ref, woc_ref,
             x_ref, t_ref, mod_ref, gf_ref, w1_ref, w2_ref,
             r_ref, da_ref, h2_ref, dy2_ref, dx1_ref, y1_ref, yv_ref, conv_ref, acc_ref, ra_ref):
        i = pl.program_id(0)

        @pl.when(i == 0)
        def _():
            acc_ref[...] = jnp.zeros(acc_ref.shape, F32)

        g1, sh2, sc2, g2 = mod_ref[0:1, :], mod_ref[1:2, :], mod_ref[2:3, :], mod_ref[3:4, :]
        u = gc_ref[...] * xi_ref[...]
        u_prev = jnp.where(i > 0, gcp_ref[7:8, :] * xip_ref[7:8, :], 0.0)
        u_next = jnp.where(i < n_lat - 1, gcn_ref[0:1, :] * xin_ref[0:1, :], 0.0)
        um1, up1 = _shift_rows(u, u_prev, u_next)
        yv = cw_ref[0:1, :] * um1 + cw_ref[1:2, :] * u + cw_ref[2:3, :] * up1
        yv_ref[...] = yv
        conv = (gb_ref[...] * yv).astype(BF16)
        conv_ref[...] = conv
        y1 = _dot(conv, woc_ref[...])
        for h in range(N_HEADS):
            y1 = y1 + _dot(o_ref[h], woa_ref[h])
        y1_ref[...] = y1
        x1 = x_ref[...] + g1 * y1
        rstd2 = lax.rsqrt(jnp.mean(x1 * x1, axis=-1, keepdims=True) + EPS)
        xn1 = x1 * rstd2
        h2 = (xn1 * (1.0 + sc2) + sh2).astype(BF16)
        h2_ref[...] = h2
        y2 = jnp.zeros((tm, D_MODEL), F32)
        for jj in range(n_ff):
            ra = jnp.maximum(_dot(h2, w1_ref[jj]), 0.0)
            ra_ref[jj] = ra
            r = (ra * ra).astype(BF16)
            r_ref[:, jj * FF_CHUNK:(jj + 1) * FF_CHUNK] = r
            y2 = y2 + _dot(r, w2_ref[jj * FF_CHUNK:(jj + 1) * FF_CHUNK, :])
        x2v = x1 + g2 * y2
        rstd3 = lax.rsqrt(jnp.mean(x2v * x2v, axis=-1, keepdims=True) + EPS)
        xn3 = x2v * rstd3
        gfv = gf_ref[...]
        diff = xn3 * gfv - t_ref[...]
        loss_t = 0.5 * jnp.sum(jnp.sum(diff * diff, axis=-1, keepdims=True), axis=0, keepdims=True) * (1.0 / D_MODEL)
        dy = diff * (1.0 / D_MODEL)
        dxn3 = dy * gfv
        dx2 = rstd3 * (dxn3 - xn3 * jnp.mean(dxn3 * xn3, axis=-1, keepdims=True))
        dy2 = (dx2 * g2).astype(BF16)
        dy2_ref[...] = dy2
        dh2 = jnp.zeros((tm, D_MODEL), F32)
        for jj in range(n_ff):
            dr = _dot_nt(dy2, w2_ref[jj * FF_CHUNK:(jj + 1) * FF_CHUNK, :])
            da = (2.0 * ra_ref[jj] * dr).astype(BF16)
            da_ref[:, jj * FF_CHUNK:(jj + 1) * FF_CHUNK] = da
            dh2 = dh2 + _dot_nt(da, w1_ref[jj])
        dxn1 = dh2 * (1.0 + sc2)
        dx1_ref[...] = dx2 + rstd2 * (dxn1 - xn1 * jnp.mean(dxn1 * xn1, axis=-1, keepdims=True))
        acc_ref[0:1, :] += jnp.sum(dy * xn3, axis=0, keepdims=True)
        acc_ref[1:2, :] += jnp.sum(dx2 * y2, axis=0, keepdims=True)
        acc_ref[2:3, :] += jnp.sum(dh2, axis=0, keepdims=True)
        acc_ref[3:4, :] += jnp.sum(dh2 * xn1, axis=0, keepdims=True)
        acc_ref[4:5, :] += jnp.broadcast_to(loss_t, (1, D_MODEL))

    row = lambda i: (i, 0)
    gcp, gcn = _halo_specs(tm, 2)
    xip, xin = _halo_specs(tm, 3)
    tile = pl.BlockSpec((tm, D_MODEL), row)
    wide = pl.BlockSpec((tm, D_FF), row)
    half = pl.BlockSpec((tm, CONV_W), row)
    return pl.pallas_call(
        body, name="mlp_fwdbwd", grid=(n_lat,),
        out_shape=(jax.ShapeDtypeStruct((t_lat, D_FF), BF16), jax.ShapeDtypeStruct((t_lat, D_FF), BF16),
                   jax.ShapeDtypeStruct((t_lat, D_MODEL), BF16), jax.ShapeDtypeStruct((t_lat, D_MODEL), BF16),
                   jax.ShapeDtypeStruct((t_lat, D_MODEL), F32), jax.ShapeDtypeStruct((t_lat, D_MODEL), F32),
                   jax.ShapeDtypeStruct((t_lat, CONV_W), F32), jax.ShapeDtypeStruct((t_lat, CONV_W), BF16),
                   jax.ShapeDtypeStruct((8, D_MODEL), F32)),
        in_specs=[
            pl.BlockSpec((N_HEADS, tm, HEAD_PAD), lambda i: (0, i, 0)),
            pl.BlockSpec((tm, CONV_W), lambda i: (i, 1)), pl.BlockSpec((tm, CONV_W), lambda i: (i, 2)),
            pl.BlockSpec((tm, CONV_W), lambda i: (i, 3)),
            gcp, xip, gcn, xin,
            _const_spec(cw.shape), _resident_spec(wo_attn.shape), _resident_spec(wo_conv.shape),
            tile, tile, _const_spec(mod_b.shape), _const_spec(gf.shape),
            _resident_spec(w1.shape), _resident_spec(w2.shape),
        ],
        out_specs=(wide, wide, tile, tile, tile, tile, half, half, _const_spec((8, D_MODEL))),
        scratch_shapes=[pltpu.VMEM((n_ff, tm, FF_CHUNK), F32)],
        compiler_params=pltpu.CompilerParams(vmem_limit_bytes=VMEM_LIMIT),
    )(o, z, z, z, z, z, z, z, cw, wo_attn, wo_conv, x2, tgt, mod_b, gf, w1, w2)


def _mid_bwd(dx1, y1, z, yv, mod_b, wo_attn, wo_conv):
    t_lat = dx1.shape[0]
    tm = TOK_TILE

    def body(dx1_ref, y1_ref, gb_ref, yv_ref, mod_ref, woa_ref, woc_ref, dy1_ref, do_ref, dgb_ref, dyv_ref, acc_ref):
        i = pl.program_id(0)

        @pl.when(i == 0)
        def _():
            acc_ref[...] = jnp.zeros(acc_ref.shape, F32)

        dx1v = dx1_ref[...]
        acc_ref[0:1, :] += jnp.sum(dx1v * y1_ref[...], axis=0, keepdims=True)
        dy1 = (dx1v * mod_ref[0:1, :]).astype(BF16)
        dy1_ref[...] = dy1
        for h in range(N_HEADS):
            do_ref[h] = _dot_nt(dy1, woa_ref[h]).astype(BF16)
        dconv = _dot_nt(dy1, woc_ref[...])
        dgb_ref[...] = dconv * yv_ref[...]
        dyv_ref[...] = dconv * gb_ref[...]

    row = lambda i: (i, 0)
    tile = pl.BlockSpec((tm, D_MODEL), row)
    half = pl.BlockSpec((tm, CONV_W), row)
    return pl.pallas_call(
        body, name="mid_bwd", grid=(t_lat // tm,),
        out_shape=(jax.ShapeDtypeStruct((t_lat, D_MODEL), BF16),
                   jax.ShapeDtypeStruct((N_HEADS, t_lat, HEAD_PAD), BF16),
                   jax.ShapeDtypeStruct((t_lat, CONV_W), F32), jax.ShapeDtypeStruct((t_lat, CONV_W), F32),
                   jax.ShapeDtypeStruct((8, D_MODEL), F32)),
        in_specs=[tile, tile, pl.BlockSpec((tm, CONV_W), lambda i: (i, 1)), half, _const_spec(mod_b.shape),
                  _const_spec(wo_attn.shape), _const_spec(wo_conv.shape)],
        out_specs=(tile, pl.BlockSpec((N_HEADS, tm, HEAD_PAD), lambda i: (0, i, 0)), half, half,
                   _const_spec((8, D_MODEL))),
        compiler_params=pltpu.CompilerParams(vmem_limit_bytes=VMEM_LIMIT),
    )(dx1, y1, z, yv, mod_b, wo_attn, wo_conv)


def _inproj_bwd(x2, ctx2, mod_a, z, dyv, dgb, dx1, dqt, dk, dv, cos_t, sin_a, sin_b, w_in, w_uq, w_ukv, qg, kvg, cw):
    t_lat, t_ctx = x2.shape[0], ctx2.shape[0]
    tm = TOK_TILE
    n_lat = t_lat // tm
    n_all = n_lat + t_ctx // tm

    def body(x_ref, ctx_ref, mod_ref, z_ref, gcp_ref, xip_ref, gcn_ref, xin_ref, dyv_ref, dyvp_ref, dyvn_ref,
             dgb_ref, dx1_ref, dqt_ref, dk_ref, dv_ref, cos_ref, sa_ref, sb_ref, win_ref, wuq_ref, wukv_ref,
             qg_ref, kvg_ref, cw_ref, gx_ref, dwin_ref, dwuq_ref, dwukv_ref, acc_ref):
        i = pl.program_id(0)
        lat = i < n_lat

        @pl.when(i == 0)
        def _():
            dwin_ref[...] = jnp.zeros(dwin_ref.shape, F32)
            dwuq_ref[...] = jnp.zeros(dwuq_ref.shape, F32)
            dwukv_ref[...] = jnp.zeros(dwukv_ref.shape, F32)
            acc_ref[...] = jnp.zeros(acc_ref.shape, F32)

        xin = jnp.where(lat, x_ref[...], ctx_ref[...])
        rstd = lax.rsqrt(jnp.mean(xin * xin, axis=-1, keepdims=True) + EPS)
        xn = xin * rstd
        sc = mod_ref[0, 1:2, :]
        h1 = (xn * (1.0 + sc) + mod_ref[0, 0:1, :]).astype(BF16)
        z = z_ref[...]
        cos, sa, sb = cos_ref[...], sa_ref[...], sb_ref[...]
        qgv, kvgv = qg_ref[...], kvg_ref[...]
        cq = z[:, 0:Q_RANK]
        cqh = cq * lax.rsqrt(jnp.mean(cq * cq, axis=-1, keepdims=True) + EPS)
        rq = lax.rsqrt(jnp.mean(cq * cq, axis=-1, keepdims=True) + EPS)
        cqn = (cqh * qgv).astype(BF16)
        parts = []
        for h in range(N_HEADS):
            g = jnp.where(lat, dqt_ref[h].T, 0.0)
            parts.append(_unrope(g, cos, sa, sb))
        dq = jnp.concatenate(parts, axis=1).astype(BF16)
        dcqn = _dot_nt(dq, wuq_ref[...])
        dwuq_ref[...] += _dot_tn(cqn, dq)
        acc_ref[4:5, 0:Q_RANK] += jnp.sum(dcqn * cqh, axis=0, keepdims=True)
        dxn = dcqn * qgv
        dcq = rq * (dxn - cqh * jnp.mean(dxn * cqh, axis=-1, keepdims=True))
        ckv = z[:, Q_RANK:Q_RANK + KV_RANK]
        rk = lax.rsqrt(jnp.mean(ckv * ckv, axis=-1, keepdims=True) + EPS)
        ckvh = ckv * rk
        ckvn = (ckvh * kvgv).astype(BF16)
        dks = [dk_ref[h] for h in range(N_HEADS)]
        dkr = dks[0]
        for h in range(1, N_HEADS):
            dkr = dkr + dks[h]
        dkv = jnp.concatenate(dks + [dv_ref[h] for h in range(N_HEADS)], axis=1).astype(BF16)
        dckvn = _dot_nt(dkv, wukv_ref[...])
        dwukv_ref[...] += _dot_tn(ckvn, dkv)
        acc_ref[5:6, 0:KV_RANK] += jnp.sum(dckvn * ckvh, axis=0, keepdims=True)
        dxn = dckvn * kvgv
        dckv = rk * (dxn - ckvh * jnp.mean(dxn * ckvh, axis=-1, keepdims=True))
        dkr = _unrope(dkr, cos, sa, sb)
        gb, gc, xi = z[:, 512:1024], z[:, 1024:1536], z[:, 1536:2048]
        u = gc * xi
        u_prev = jnp.where(i > 0, gcp_ref[7:8, :] * xip_ref[7:8, :], 0.0)
        u_next = jnp.where(i < n_lat - 1, gcn_ref[0:1, :] * xin_ref[0:1, :], 0.0)
        um1, up1 = _shift_rows(u, u_prev, u_next)
        dyv = jnp.where(lat, dyv_ref[...], 0.0)
        dyv_prev = jnp.where(jnp.logical_and(i > 0, lat), dyvp_ref[7:8, :], 0.0)
        dyv_next = jnp.where(i < n_lat - 1, dyvn_ref[0:1, :], 0.0)
        dyv_m1, dyv_p1 = _shift_rows(dyv, dyv_prev, dyv_next)
        du = cw_ref[0:1, :] * dyv_p1 + cw_ref[1:2, :] * dyv + cw_ref[2:3, :] * dyv_m1
        dgc = du * xi
        dxi = du * gc
        dgb = jnp.where(lat, dgb_ref[...], 0.0)
        acc_ref[6:7, 0:CONV_W] += jnp.sum(dyv * um1, axis=0, keepdims=True)
        acc_ref[7:8, 0:CONV_W] += jnp.sum(dyv * u, axis=0, keepdims=True)
        acc_ref[8:9, 0:CONV_W] += jnp.sum(dyv * up1, axis=0, keepdims=True)
        dz = jnp.concatenate([dcq, dckv, dkr, dgb, dgc, dxi], axis=1).astype(BF16)
        dh1 = _dot_nt(dz, win_ref[...])
        dwin_ref[...] += _dot_tn(h1, dz)
        s_sh = jnp.sum(dh1, axis=0, keepdims=True)
        s_sc = jnp.sum(dh1 * xn, axis=0, keepdims=True)
        zero = jnp.zeros_like(s_sh)
        acc_ref[0:1, :] += jnp.where(lat, s_sh, zero)
        acc_ref[1:2, :] += jnp.where(lat, s_sc, zero)
        acc_ref[2:3, :] += jnp.where(lat, zero, s_sh)
        acc_ref[3:4, :] += jnp.where(lat, zero, s_sc)
        dxn = dh1 * (1.0 + sc)
        dx = rstd * (dxn - xn * jnp.mean(dxn * xn, axis=-1, keepdims=True))

        @pl.when(lat)
        def _():
            gx_ref[...] = dx1_ref[...] + dx

    last = n_lat - 1
    per = tm // 8
    lat_row = lambda i: (jnp.minimum(i, last), 0)
    row = lambda i: (i, 0)
    gcp, gcn = _halo_specs(tm, 2)
    xip, xin = _halo_specs(tm, 3)
    n_halo = t_lat // 8
    dyvp = pl.BlockSpec((8, CONV_W), lambda i: (jnp.clip(i * per - 1, 0, n_halo - 1), 0))
    dyvn = pl.BlockSpec((8, CONV_W), lambda i: (jnp.minimum((i + 1) * per, n_halo - 1), 0))
    gcn = pl.BlockSpec((8, CONV_W), lambda i: (jnp.minimum((i + 1) * per, (t_lat + t_ctx) // 8 - 1), 2))
    xin = pl.BlockSpec((8, CONV_W), lambda i: (jnp.minimum((i + 1) * per, (t_lat + t_ctx) // 8 - 1), 3))
    head_f32 = pl.BlockSpec((N_HEADS, tm, HEAD_PAD), lambda i: (0, i, 0))
    tab = pl.BlockSpec((tm, HEAD_PAD), row)
    return pl.pallas_call(
        body, name="inproj_bwd", grid=(n_all,),
        out_shape=(jax.ShapeDtypeStruct((t_lat, D_MODEL), F32), jax.ShapeDtypeStruct(w_in.shape, F32),
                   jax.ShapeDtypeStruct(w_uq.shape, F32), jax.ShapeDtypeStruct(w_ukv.shape, F32),
                   jax.ShapeDtypeStruct((16, D_MODEL), F32)),
        in_specs=[
            pl.BlockSpec((tm, D_MODEL), lat_row), _const_spec((tm, D_MODEL)),
            pl.BlockSpec((1, 8, D_MODEL), lambda i: (i // n_lat, 0, 0)),
            pl.BlockSpec((tm, Z_COLS), row), gcp, xip, gcn, xin,
            pl.BlockSpec((tm, CONV_W), lat_row), dyvp, dyvn,
            pl.BlockSpec((tm, CONV_W), lat_row), pl.BlockSpec((tm, D_MODEL), lat_row),
            pl.BlockSpec((N_HEADS, HEAD_PAD, tm), lambda i: (0, 0, jnp.minimum(i, last))),
            head_f32, head_f32, tab, tab, tab,
            _const_spec(w_in.shape), _const_spec(w_uq.shape), _const_spec(w_ukv.shape),
            _const_spec(qg.shape), _const_spec(kvg.shape), _const_spec(cw.shape),
        ],
        out_specs=(pl.BlockSpec((tm, D_MODEL), lat_row), _const_spec(w_in.shape), _const_spec(w_uq.shape),
                   _const_spec(w_ukv.shape), _const_spec((16, D_MODEL))),
        compiler_params=pltpu.CompilerParams(vmem_limit_bytes=VMEM_LIMIT),
    )(x2, ctx2, mod_a, z, z, z, z, z, dyv, dyv, dyv, dgb, dx1, dqt, dk, dv, cos_t, sin_a, sin_b, w_in, w_uq, w_ukv,
      qg, kvg, cw)


def _wgrad(a, b, name, bm, bn):
    t, m = a.shape
    n = b.shape[1]
    bk = 512
    nk = t // bk
    nj = n // bn

    def body(a_ref, b_ref, o_ref, acc_ref):
        k = pl.program_id(2)
        part = _dot_tn(a_ref[...], b_ref[...])

        @pl.when(k == 0)
        def _():
            acc_ref[...] = part

        @pl.when(k > 0)
        def _():
            acc_ref[...] += part

        @pl.when(k == nk - 1)
        def _():
            o_ref[...] = acc_ref[...].astype(BF16)

    return pl.pallas_call(
        body, name=name, grid=(m // bm, nj, nk), out_shape=jax.ShapeDtypeStruct((m // bm * nj, bm, bn), BF16),
        in_specs=[pl.BlockSpec((bk, bm), lambda i, j, k: (k, i)), pl.BlockSpec((bk, bn), lambda i, j, k: (k, j))],
        out_specs=pl.BlockSpec((None, bm, bn), lambda i, j, k: (i * nj + j, 0, 0)),
        scratch_shapes=[pltpu.VMEM((bm, bn), F32)],
        compiler_params=pltpu.CompilerParams(vmem_limit_bytes=VMEM_LIMIT),
    )(a, b)


def _wgrad_out(o, conv, dy1):
    t = o.shape[1]
    bk = 512
    nk = t // bk
    rows = N_HEADS * HEAD_PAD + CONV_W

    def body(o_ref, c_ref, d_ref, w_ref, acc_ref):
        k = pl.program_id(0)
        cat = jnp.concatenate([o_ref[h] for h in range(N_HEADS)] + [c_ref[...]], axis=1)
        part = _dot_tn(cat, d_ref[...])

        @pl.when(k == 0)
        def _():
            acc_ref[...] = part

        @pl.when(k > 0)
        def _():
            acc_ref[...] += part

        @pl.when(k == nk - 1)
        def _():
            for h in range(N_HEADS):
                w_ref[h * V_DIM:(h + 1) * V_DIM, :] = acc_ref[h * HEAD_PAD:h * HEAD_PAD + V_DIM, :].astype(BF16)
            w_ref[N_HEADS * V_DIM:, :] = acc_ref[N_HEADS * HEAD_PAD:, :].astype(BF16)

    return pl.pallas_call(
        body, name="wgrad_out", grid=(nk,),
        out_shape=jax.ShapeDtypeStruct((D_MODEL, D_MODEL), BF16),
        in_specs=[pl.BlockSpec((N_HEADS, bk, HEAD_PAD), lambda k: (0, k, 0)),
                  pl.BlockSpec((bk, CONV_W), lambda k: (k, 0)),
                  pl.BlockSpec((bk, D_MODEL), lambda k: (k, 0))],
        out_specs=_const_spec((D_MODEL, D_MODEL)),
        scratch_shapes=[pltpu.VMEM((rows, D_MODEL), F32)],
        compiler_params=pltpu.CompilerParams(vmem_limit_bytes=VMEM_LIMIT),
    )(o, conv, dy1)


def _adamw_call(w, g, m, v, name):
    rows, cols = w.shape
    rb = 256 if rows % 256 == 0 else rows

    def body(w_ref, g_ref, m_ref, v_ref, d_ref, nm_ref, nv_ref):
        d_, m_, v_ = _adamw(w_ref[...], g_ref[...], m_ref[...], v_ref[...])
        d_ref[...] = d_
        nm_ref[...] = m_
        nv_ref[...] = v_

    spec = pl.BlockSpec((rb, cols), lambda i: (i, 0))
    shp = jax.ShapeDtypeStruct((rows, cols), F32)
    return pl.pallas_call(
        body, name=name, grid=(rows // rb,), out_shape=(shp, shp, shp),
        in_specs=[spec] * 4, out_specs=(spec, spec, spec),
    )(w, g, m, v)


def _wmod_update(s_t, dm, w, m, v):
    rows, cols = w.shape
    cb = 512

    def body(s_ref, dm_ref, w_ref, m_ref, v_ref, g_ref, d_ref, nm_ref, nv_ref):
        g = jnp.dot(s_ref[...], dm_ref[...], precision=HIGHEST, preferred_element_type=F32)
        d_, m_, v_ = _adamw(w_ref[...], g, m_ref[...], v_ref[...])
        g_ref[...] = g
        d_ref[...] = d_
        nm_ref[...] = m_
        nv_ref[...] = v_

    spec = pl.BlockSpec((rows, cb), lambda i: (0, i))
    shp = jax.ShapeDtypeStruct((rows, cols), F32)
    return pl.pallas_call(
        body, name="wmod_update", grid=(cols // cb,), out_shape=(shp, shp, shp, shp),
        in_specs=[_const_spec(s_t.shape), pl.BlockSpec((16, cb), lambda i: (0, i)), spec, spec, spec],
        out_specs=(spec, spec, spec, spec),
        compiler_params=pltpu.CompilerParams(vmem_limit_bytes=VMEM_LIMIT),
    )(s_t, dm, w, m, v)


def _rope_tables(t_lat, t_ctx):
    t = jnp.arange(t_lat)
    pos = jnp.stack([(t // GRID_W).astype(F32), (t % GRID_W).astype(F32)], axis=1)
    half = QK_ROPE // 4
    freqs = ROPE_THETA ** (-jnp.arange(0, 2 * half, 2, dtype=F32) / (2 * half))
    ang = pos[:, :, None] * freqs[None, None, :]
    cos, sin = jnp.cos(ang), jnp.sin(ang)
    zero = jnp.zeros_like(sin)
    cos32 = jnp.concatenate([cos, cos], axis=2).reshape(t_lat, QK_ROPE)
    sa32 = jnp.concatenate([zero, sin], axis=2).reshape(t_lat, QK_ROPE)
    sb32 = jnp.concatenate([-sin, zero], axis=2).reshape(t_lat, QK_ROPE)

    def widen(tab, fill):
        left = jnp.full((t_lat, ROPE_LANE0), fill, F32)
        right = jnp.full((t_lat, HEAD_PAD - ROPE_LANE0 - QK_ROPE), fill, F32)
        lat = jnp.concatenate([left, tab, right], axis=1)
        return jnp.concatenate([lat, jnp.full((t_ctx, HEAD_PAD), fill, F32)], axis=0)

    return widen(cos32, 1.0), widen(sa32, 0.0), widen(sb32, 0.0)


def _pack_small(w_in, w_uq, w_ukv):
    lead = w_in.shape[:-2]
    parts = [w_in.reshape(lead + (PK_IN, D_MODEL)), w_uq.reshape(lead + (PK_UQ, D_MODEL)),
             w_ukv.reshape(lead + (PK_UKV, D_MODEL)), jnp.zeros(lead + (PK_PAD, D_MODEL), w_in.dtype)]
    return jnp.concatenate(parts, axis=-2)


def _unpack_small(p):
    lead = p.shape[:-2]
    w_in = p[..., 0:PK_IN, :].reshape(lead + (D_MODEL, IN_COLS // N_SHARD))
    w_uq = p[..., PK_IN:PK_IN + PK_UQ, :].reshape(lead + (Q_RANK, N_HEADS * QK_DIM // N_SHARD))
    w_ukv = p[..., PK_IN + PK_UQ:PK_IN + PK_UQ + PK_UKV, :].reshape(
        lead + (KV_RANK, N_HEADS * (QK_NOPE + V_DIM) // N_SHARD))
    return w_in, w_uq, w_ukv


def _cols_from_shards(s):
    return jnp.transpose(s, (1, 0, 2)).reshape(s.shape[1], -1)


def _cols_to_shards(w):
    k, n = w.shape
    return jnp.transpose(w.reshape(k, N_SHARD, n // N_SHARD), (1, 0, 2))


def kernel(x, c, ctx, c_ctx, w_mod, b_mod, w_in, q_norm_g, w_uq, kv_norm_g, w_ukv, conv_w, w_out, w_mlp1, w_mlp2, final_norm_g, loss_target, m_c_ctx, m_w_mod, m_b_mod, m_w_in, m_q_norm_g, m_w_uq, m_kv_norm_g, m_w_ukv, m_conv_w, m_w_out, m_w_mlp1, m_w_mlp2, m_final_norm_g, v_c_ctx, v_w_mod, v_b_mod, v_w_in, v_q_norm_g, v_w_uq, v_kv_norm_g, v_w_ukv, v_conv_w, v_w_out, v_w_mlp1, v_w_mlp2, v_final_norm_g):
    t_lat, t_ctx = x.shape[1], ctx.shape[1]
    assert t_ctx == TOK_TILE and t_lat % TOK_TILE == 0 and t_lat % GRID_W == 0
    mx, my, mc = _pos()
    me = 4 * mx + 2 * my + mc
    j = 2 * mx + my
    ncol = w_mod.shape[2]
    x2, ctx2, tgt = x[0], ctx[0], loss_target[0]
    cctx_row = c_ctx.reshape(1, D_MODEL)

    b_sh = lax.dynamic_slice(b_mod, (0, j * ncol), (1, ncol))
    cw_pad = jnp.zeros((8, 128), F32).at[0:3, :].set(conv_w[0])
    c8, m_all = _mod_exchange(c, cctx_row, w_mod[0], b_sh, cw_pad)
    m_rows = jnp.transpose(m_all[:, 0:72, :], (1, 0, 2)).reshape(72, N_SHARD * ncol)
    mvec = lax.dynamic_slice(m_rows, (8 * me, 0), (1, 6 * D_MODEL)).reshape(6, D_MODEL)
    mctx = m_rows[64].reshape(6, D_MODEL)
    zeros6 = jnp.zeros((6, D_MODEL), F32)
    mod_a = jnp.stack([jnp.concatenate([mvec[0:2], zeros6], axis=0), jnp.concatenate([mctx[0:2], zeros6], axis=0)])
    mod_b = jnp.concatenate([mvec[2:6], jnp.zeros((4, D_MODEL), F32)], axis=0)
    cw_full = jnp.transpose(m_all[:, 72:80, 0:128], (1, 0, 2)).reshape(8, CONV_W)

    small = _pack_small(w_in[0], w_uq[0], w_ukv[0])
    g_small, g_out, g_m1, g_m2 = _weight_prep(small, w_out[0], w_mlp1[0], w_mlp2[0])
    g_in, g_uq, g_ukv = _unpack_small(g_small)
    w_in_f = _cols_from_shards(g_in)
    zc = lambda n: jnp.zeros((D_MODEL, n), BF16)
    w_in_p = jnp.concatenate([w_in_f[:, 0:384], zc(64), w_in_f[:, 384:416], zc(32), w_in_f[:, 416:]], axis=1)
    w_uq_f = _cols_from_shards(g_uq).reshape(Q_RANK, N_HEADS, QK_DIM)
    w_uq_p = jnp.pad(w_uq_f, ((0, 0), (0, 0), (0, HEAD_PAD - QK_DIM))).reshape(Q_RANK, N_HEADS * HEAD_PAD)
    w_ukv_f = _cols_from_shards(g_ukv).reshape(KV_RANK, N_HEADS, QK_NOPE + V_DIM)
    padh = lambda a: jnp.pad(a, ((0, 0), (0, 0), (0, HEAD_PAD - a.shape[2]))).reshape(KV_RANK, N_HEADS * HEAD_PAD)
    w_ukv_p = jnp.concatenate([padh(w_ukv_f[:, :, :QK_NOPE]), padh(w_ukv_f[:, :, QK_NOPE:])], axis=1)
    cos_t, sin_a, sin_b = _rope_tables(t_lat, t_ctx)
    gf_row = final_norm_g.reshape(1, D_MODEL)
    c_idx = mc.reshape(1).astype(jnp.int32)
    j_idx = j.reshape(1).astype(jnp.int32)

    z, q, k, v, kt = _inproj_fwd(x2, ctx2, mod_a, w_in_p, q_norm_g, kv_norm_g, w_uq_p, w_ukv_p, cos_t, sin_a, sin_b)
    o, lse, g_out, w1, g_m2 = _attn_fwd(q, k, v, t_lat, (g_out, g_m1, g_m2))
    w_out_f = g_out.reshape(D_MODEL, D_MODEL)
    wo_attn = jnp.pad(w_out_f[0:512].reshape(N_HEADS, V_DIM, D_MODEL), ((0, 0), (0, HEAD_PAD - V_DIM), (0, 0)))
    wo_conv = w_out_f[512:]
    w2 = g_m2.reshape(D_FF, D_MODEL)
    r, da, h2, dy2, dx1, y1, yv, conv, acc_mlp = _mlp_fwdbwd(o, z, x2, tgt, mod_b, gf_row, cw_full, wo_attn, wo_conv,
                                                               w1, w2)
    d_w1 = _wgrad(h2, da, "wgrad_mlp1", D_MODEL, FF_CHUNK)
    d_w2 = _wgrad(r, dy2, "wgrad_mlp2", FF_CHUNK, D_MODEL)
    mlp_got = _rs_sibling((d_w1, d_w2), "rs_sibling_mlp")
    mlp_parts = [_add_pairs(a, g, c_idx, "rs_add_pairs_" + n) for a, g, n in zip((d_w1, d_w2), mlp_got, ("mlp1", "mlp2"))]
    dy1, do, dgb, dyv, acc_mid = _mid_bwd(dx1, y1, z, yv, mod_b, wo_attn, wo_conv)
    dqt, dk, dv, got_w1, got_w2 = _attn_bwd(q, k, v, kt, o, do, lse.reshape(N_HEADS, 1, t_lat), t_lat, mlp_parts)
    mlp_halves = [_add_chips(p, g, j_idx, "rs_add_chips_" + n)
                  for p, g, n in zip(mlp_parts, (got_w1, got_w2), ("mlp1", "mlp2"))]
    g_w1, g_w2 = _rs_join(mlp_halves, "rs_join_mlp")
    gx, d_win, d_wuq, d_wukv, acc_in = _inproj_bwd(x2, ctx2, mod_a, z, dyv, dgb, dx1, dqt, dk, dv, cos_t, sin_a, sin_b,
                                                   w_in_p, w_uq_p, w_ukv_p, q_norm_g, kv_norm_g, cw_full)
    d_wout = _wgrad_out(o, conv, dy1).reshape(N_SHARD, D_MODEL // N_SHARD, D_MODEL)

    pad_row = lambda a: jnp.pad(a, ((0, 0), (0, D_MODEL - a.shape[1])))
    sv = jnp.concatenate([
        acc_in[0:2], acc_mid[0:1], acc_mlp[2:4], acc_mlp[1:2],
        acc_in[2:4], acc_mlp[0:1], acc_in[4:5], acc_in[5:6], acc_in[6:9], acc_mlp[4:5],
        jnp.zeros((1, D_MODEL), F32)], axis=0)
    all_sv, red, o_cc, o_b, o_q, o_k, o_gf = _small_exchange(
        sv, w_mod[0], cctx_row, m_c_ctx.reshape(1, D_MODEL), v_c_ctx.reshape(1, D_MODEL),
        b_mod.reshape(6, D_MODEL), m_b_mod.reshape(6, D_MODEL), v_b_mod.reshape(6, D_MODEL),
        q_norm_g, m_q_norm_g, v_q_norm_g, kv_norm_g, m_kv_norm_g, v_kv_norm_g,
        gf_row, m_final_norm_g.reshape(1, D_MODEL), v_final_norm_g.reshape(1, D_MODEL))
    loss = red[14, 0]

    c9 = jnp.concatenate([c8[0::8], jnp.zeros((7, D_MODEL), F32)], axis=0)
    s_t = jnp.transpose(c9 * jax.nn.sigmoid(c9))
    dm_ex = all_sv[:, 0:6, :].reshape(8, 6 * D_MODEL)
    dm_ctx = jnp.concatenate([red[6:8].reshape(1, 2 * D_MODEL), jnp.zeros((1, 4 * D_MODEL), F32)], axis=1)
    dm16 = jnp.concatenate([dm_ex, dm_ctx, jnp.zeros((7, 6 * D_MODEL), F32)], axis=0)
    dm_sh = lax.dynamic_slice(dm16, (0, j * ncol), (16, ncol))
    g_wmod, d_wmod, nm_wmod, nv_wmod = _wmod_update(s_t, dm_sh, w_mod[0], m_w_mod[0], v_w_mod[0])

    g_cw = lax.dynamic_slice(red[11:14, 0:CONV_W], (0, j * 128), (3, 128))
    d_cw, nm_cw, nv_cw = _adamw_call(conv_w[0], g_cw, m_conv_w[0], v_conv_w[0], "adamw_conv")

    d_win_f = jnp.concatenate([d_win[:, 0:384], d_win[:, 448:480], d_win[:, 512:]], axis=1)
    d_wuq_f = d_wuq.reshape(Q_RANK, N_HEADS, HEAD_PAD)[:, :, 0:QK_DIM].reshape(Q_RANK, N_HEADS * QK_DIM)
    d_wukv3 = d_wukv.reshape(KV_RANK, 2, N_HEADS, HEAD_PAD)
    d_wukv_f = jnp.concatenate([d_wukv3[:, 0, :, 0:QK_NOPE], d_wukv3[:, 1, :, 0:V_DIM]], axis=2).reshape(KV_RANK, -1)
    d_small = _pack_small(_cols_to_shards(d_win_f), _cols_to_shards(d_wuq_f), _cols_to_shards(d_wukv_f)).astype(BF16)
    rest = (d_small, d_wout)
    rest_got = _rs_sibling(rest, "rs_sibling_rest")
    rest_parts = [_add_pairs(a, g, c_idx, "rs_add_pairs_" + n) for a, g, n in zip(rest, rest_got, ("small", "out"))]
    rest_recv = _rs_chips(rest_parts, "rs_chips_rest")
    rest_halves = [_add_chips(p, g, j_idx, "rs_add_chips_" + n)
                   for p, g, n in zip(rest_parts, rest_recv, ("small", "out"))]
    g_small_f, g_wout = _rs_join(rest_halves, "rs_join_rest")
    g_win, g_wuq, g_wukv = _unpack_small(g_small_f)
    upd = {}
    for name, w_, g_, m_, v_ in (("in", w_in, g_win, m_w_in, v_w_in), ("uq", w_uq, g_wuq, m_w_uq, v_w_uq),
                                 ("ukv", w_ukv, g_wukv, m_w_ukv, v_w_ukv), ("out", w_out, g_wout, m_w_out, v_w_out),
                                 ("mlp1", w_mlp1, g_w1, m_w_mlp1, v_w_mlp1), ("mlp2", w_mlp2, g_w2, m_w_mlp2, v_w_mlp2)):
        upd[name] = _adamw_call(w_[0], g_, m_[0], v_[0], "adamw_" + name)

    def four(o4, shape):
        return [o4[r].reshape(shape) for r in range(4)]

    cc4 = four(o_cc, (D_MODEL,))
    b4 = [o_b[r].reshape(1, 6 * D_MODEL) for r in range(4)]
    q4 = four(o_q, (1, Q_RANK))
    k4 = four(o_k, (1, KV_RANK))
    gf4 = four(o_gf, (D_MODEL,))
    big = {"in": g_win, "uq": g_wuq, "ukv": g_wukv, "out": g_wout, "mlp1": g_w1, "mlp2": g_w2}

    def leaf(idx):
        wm = (g_wmod, d_wmod, nm_wmod, nv_wmod)[idx]
        cwv = (g_cw, d_cw, nm_cw, nv_cw)[idx]
        bigv = {n: (big[n] if idx == 0 else upd[n][idx - 1]) for n in big}
        return [cc4[idx], wm[None], b4[idx], bigv["in"][None], q4[idx], bigv["uq"][None], k4[idx], bigv["ukv"][None],
                cwv[None], bigv["out"][None], bigv["mlp1"][None], bigv["mlp2"][None], gf4[idx]]

    return (loss, gx[None], *leaf(0), *leaf(1), *leaf(2), *leaf(3))
```

```python
import functools
import math

import jax
import jax.numpy as jnp
from jax import lax
from jax.experimental import pallas as pl
from jax.experimental.pallas import tpu as pltpu

F32 = jnp.float32
BF16 = jnp.bfloat16
MESH = pl.DeviceIdType.MESH
HIGHEST = lax.Precision.HIGHEST

D_MODEL = 1024
N_HEADS = 8
QK_NOPE = 64
QK_ROPE = 32
QK_DIM = QK_NOPE + QK_ROPE
V_DIM = 64
Q_RANK = 256
KV_RANK = 128
CONV_W = 512
D_FF = 4096
GRID_W = 64
ROPE_THETA = 10000.0
EPS = 1e-6
ATTN_SCALE = 1.0 / math.sqrt(QK_DIM)
HEAD_PAD = 128
Z_COLS = 2048
ROPE_LANE0 = QK_NOPE
N_SHARD = 4
TOK_TILE = 256
FF_CHUNK = 1024
KEY_CHUNK = 1024
ATTN_HEADS_PER_STEP = 4

ADAM_LR = 0.001
ADAM_B1 = 0.9
ADAM_B2 = 0.999
ADAM_EPS = 1e-08
ADAM_WD = 0.01
ADAM_STEP = 10

LOG2E = 1.4426950408889634

VMEM_LIMIT = 56 * 1024 * 1024
STAGE_VMEM_LIMIT = 32 * 1024 * 1024


def _pos():
    return lax.axis_index("x"), lax.axis_index("y"), lax.axis_index("c")


def _dot(a, b):
    return jnp.dot(a, b, preferred_element_type=F32)


def _dot_nt(a, b):
    return lax.dot_general(a, b, (((1,), (1,)), ((), ())), preferred_element_type=F32)


def _dot_tn(a, b):
    return lax.dot_general(a, b, (((0,), (0,)), ((), ())), preferred_element_type=F32)


def _rope(v, cos, sa, sb):
    return v * cos + pltpu.roll(v, 8, 1) * sa + pltpu.roll(v, HEAD_PAD - 8, 1) * sb


def _unrope(g, cos, sa, sb):
    return g * cos + pltpu.roll(g * sa, HEAD_PAD - 8, 1) + pltpu.roll(g * sb, 8, 1)


def _sigmoid(v):
    return 1.0 / (1.0 + jnp.exp(-v))


def _adamw(w, g, m, v):
    m = ADAM_B1 * m + (1.0 - ADAM_B1) * g
    v = ADAM_B2 * v + (1.0 - ADAM_B2) * (g * g)
    m_hat = m / (1.0 - ADAM_B1 ** ADAM_STEP)
    v_hat = v / (1.0 - ADAM_B2 ** ADAM_STEP)
    delta = -ADAM_LR * (m_hat / (jnp.sqrt(v_hat) + ADAM_EPS) + ADAM_WD * w)
    return delta, m, v


def _shift_rows(u, prev_row, next_row):
    n = u.shape[0]
    rows = lax.broadcasted_iota(jnp.int32, u.shape, 0)
    um1 = jnp.where(rows == 0, prev_row, pltpu.roll(u, 1, 0))
    up1 = jnp.where(rows == n - 1, next_row, pltpu.roll(u, n - 1, 0))
    return um1, up1


def _const_spec(shape):
    nd = len(shape)
    return pl.BlockSpec(shape, lambda *_: (0,) * nd)


def _resident_spec(shape):
    nd = len(shape)
    return pl.BlockSpec(shape, lambda *_: (0,) * nd, pipeline_mode=pl.Buffered(1))


def _peer(r, x, y, c):
    px = 1 - x if r & 4 else x
    py = 1 - y if r & 2 else y
    pc = 1 - c if r & 1 else c
    return (px, py, pc)


def _mod_exchange(c_row, cctx_row, w_mod_sh, b_sh, cw_sh):
    ncol = w_mod_sh.shape[1]

    def body(c_ref, cctx_ref, w_ref, b_ref, cw_ref, c8_ref, m_ref, mine_ref, ssem, rsem, ssem2, rsem2):
        x, y, c = _pos()
        me = 4 * x + 2 * y + c
        j = 2 * x + y
        mine_ref[...] = jnp.zeros(mine_ref.shape, F32)
        mine_ref[0:1, :] = c_ref[...]
        my_rows = pl.ds(pl.multiple_of(8 * me, 8), 8)
        sends = []
        for r in range(1, 8):
            cp = pltpu.make_async_remote_copy(
                src_ref=mine_ref, dst_ref=c8_ref.at[my_rows], send_sem=ssem.at[r - 1], recv_sem=rsem.at[r - 1],
                device_id=_peer(r, x, y, c), device_id_type=MESH)
            cp.start()
            sends.append(cp)
        for cp in sends:
            cp.wait()
        c8_ref[my_rows, :] = mine_ref[...]
        c8_ref[64:72, :] = jnp.zeros((8, D_MODEL), F32)
        c8_ref[64:65, :] = cctx_ref[...]
        cv = c8_ref[...]
        s = cv * _sigmoid(cv)
        m = jnp.dot(s, w_ref[...], precision=HIGHEST, preferred_element_type=F32) + b_ref[...]
        m_ref[j, 0:72, :] = m
        m_ref[j, 72:80, :] = jnp.zeros((8, ncol), F32)
        m_ref[j, 72:80, 0:128] = cw_ref[...]
        sends2 = []
        for k, r in enumerate((4, 2, 6)):
            cp = pltpu.make_async_remote_copy(
                src_ref=m_ref.at[j], dst_ref=m_ref.at[j], send_sem=ssem2.at[k], recv_sem=rsem2.at[k],
                device_id=_peer(r, x, y, c), device_id_type=MESH)
            cp.start()
            sends2.append(cp)
        for cp in sends2:
            cp.wait()

    vm = pl.BlockSpec(memory_space=pltpu.VMEM)
    return pl.pallas_call(
        body, name="mod_exchange",
        out_shape=(jax.ShapeDtypeStruct((72, D_MODEL), F32), jax.ShapeDtypeStruct((N_SHARD, 80, ncol), F32)),
        in_specs=[vm] * 5, out_specs=(vm, vm),
        scratch_shapes=[pltpu.VMEM((8, D_MODEL), F32), pltpu.SemaphoreType.DMA((7,)), pltpu.SemaphoreType.DMA((7,)),
                        pltpu.SemaphoreType.DMA((3,)), pltpu.SemaphoreType.DMA((3,))],
        compiler_params=pltpu.CompilerParams(vmem_limit_bytes=VMEM_LIMIT),
    )(c_row, cctx_row, w_mod_sh, b_sh, cw_sh)


def _chips(x, y):
    return [(1 - x, y), (x, 1 - y), (1 - x, 1 - y)]


def _halves(ref, c, align):
    hr = ref.shape[-2] // 2
    return (pl.ds(pl.multiple_of(c * hr, align), hr), pl.ds(pl.multiple_of((1 - c) * hr, align), hr))


class _ShardGather:
    def __init__(self, refs, ssem, rsem, fsend, frecv):
        self.refs, self.sems = refs, (ssem, rsem, fsend, frecv)
        self.x, self.y, self.c = _pos()
        self.j = 2 * self.x + self.y

    def _ici(self, a, k, slot):
        g = self.refs[a]
        ssem, rsem, _, _ = self.sems
        mine, _ = _halves(g, self.c, 16)
        px, py = _chips(self.x, self.y)[k]
        return pltpu.make_async_remote_copy(
            src_ref=g.at[self.j, mine], dst_ref=g.at[slot, mine], send_sem=ssem.at[3 * a + k],
            recv_sem=rsem.at[3 * a + k], device_id=(px, py, self.c), device_id_type=MESH)

    def _d2d(self, a, k, to_other_half):
        g = self.refs[a]
        _, _, fsend, frecv = self.sems
        mine, theirs = _halves(g, self.c, 16)
        px, py = _chips(self.x, self.y)[k]
        jk = 2 * px + py
        return pltpu.make_async_remote_copy(
            src_ref=g.at[jk, mine], dst_ref=g.at[jk, theirs if to_other_half else mine],
            send_sem=fsend.at[3 * a + k], recv_sem=frecv.at[3 * a + k],
            device_id=(self.x, self.y, 1 - self.c), device_id_type=MESH)

    def start(self):
        for a in range(len(self.refs)):
            for k in range(3):
                self._ici(a, k, self.j).start()

    def forward(self):
        for a in range(len(self.refs)):
            for k, (px, py) in enumerate(_chips(self.x, self.y)):
                self._ici(a, k, 2 * px + py).wait_recv()
                self._d2d(a, k, False).start()

    def finish(self):
        for a in range(len(self.refs)):
            for k in range(3):
                self._d2d(a, k, True).wait()
                self._ici(a, k, self.j).wait_send()


def _gather_sems(n_arrays):
    return [pltpu.SemaphoreType.DMA((3 * n_arrays,)) for _ in range(4)]


def _weight_prep(srcs, n_gather):
    n = len(srcs)
    n_split = 4

    def body(*refs):
        ins, outs, f32s, bfs = refs[:n], refs[n:2 * n], refs[2 * n:3 * n], refs[3 * n:4 * n]
        lsem_in, lsem_out = refs[4 * n], refs[4 * n + 1]
        x, y, c = _pos()
        j = 2 * x + y

        def pieces(rows):
            step = rows // n_split
            return [pl.ds(q * step, step) for q in range(n_split)]

        for t in range(n):
            for sl in pieces(ins[t].shape[0]):
                pltpu.make_async_copy(ins[t].at[sl], f32s[t].at[sl], lsem_in.at[t]).start()
        gather = _ShardGather(outs[:n_gather], *refs[4 * n + 2:])
        for t in range(n):
            pltpu.make_async_copy(ins[t], f32s[t], lsem_in.at[t]).wait()
            bfs[t][...] = f32s[t][...].astype(BF16)
            for sl in pieces(ins[t].shape[0]):
                pltpu.make_async_copy(bfs[t].at[sl], outs[t].at[j, sl], lsem_out.at[t]).start()
            if t == n_gather - 1:
                for u in range(n_gather):
                    pltpu.make_async_copy(bfs[u], outs[u].at[j], lsem_out.at[u]).wait()
                gather.start()
        gather.forward()
        gather.finish()
        for t in range(n_gather, n):
            pltpu.make_async_copy(bfs[t], outs[t].at[j], lsem_out.at[t]).wait()

    hbm = pl.BlockSpec(memory_space=pl.ANY)
    return pl.pallas_call(
        body, name="weight_prep",
        out_shape=tuple(jax.ShapeDtypeStruct((N_SHARD,) + a.shape, BF16) for a in srcs),
        in_specs=[hbm] * n, out_specs=(hbm,) * n,
        scratch_shapes=[pltpu.VMEM(a.shape, F32) for a in srcs] + [pltpu.VMEM(a.shape, BF16) for a in srcs]
        + [pltpu.SemaphoreType.DMA((n,)), pltpu.SemaphoreType.DMA((n,))] + _gather_sems(n_gather),
        compiler_params=pltpu.CompilerParams(vmem_limit_bytes=STAGE_VMEM_LIMIT),
    )(*srcs)


def _rs_sibling(arrs, name):
    n = len(arrs)

    def body(*refs):
        g_refs, got_refs, ssem, rsem = refs[:n], refs[n:2 * n], refs[2 * n], refs[2 * n + 1]
        x, y, c = _pos()
        for a in range(n):
            _, theirs = _halves(g_refs[a], c, 16)
            for s in range(N_SHARD):
                pltpu.make_async_remote_copy(
                    src_ref=g_refs[a].at[s, theirs], dst_ref=got_refs[a].at[s], send_sem=ssem.at[a],
                    recv_sem=rsem.at[a], device_id=(x, y, 1 - c), device_id_type=MESH).start()
        for a in range(n):
            _, theirs = _halves(g_refs[a], c, 16)
            pltpu.make_async_remote_copy(
                src_ref=g_refs[a].at[:, theirs], dst_ref=got_refs[a], send_sem=ssem.at[a], recv_sem=rsem.at[a],
                device_id=(x, y, 1 - c), device_id_type=MESH).wait()

    hbm = pl.BlockSpec(memory_space=pl.ANY)
    return pl.pallas_call(
        body, name=name,
        out_shape=tuple(jax.ShapeDtypeStruct((N_SHARD, a.shape[1] // 2, a.shape[2]), BF16) for a in arrs),
        in_specs=[hbm] * n, out_specs=(hbm,) * n,
        scratch_shapes=[pltpu.SemaphoreType.DMA((n,)), pltpu.SemaphoreType.DMA((n,))],
    )(*arrs)


class _ChipScatter:
    def __init__(self, parts, gots, ssem, rsem):
        self.parts, self.gots, self.ssem, self.rsem = parts, gots, ssem, rsem
        self.x, self.y, self.c = _pos()

    def _copy(self, a, k):
        px, py = _chips(self.x, self.y)[k]
        return pltpu.make_async_remote_copy(
            src_ref=self.parts[a].at[2 * px + py], dst_ref=self.gots[a].at[k], send_sem=self.ssem.at[3 * a + k],
            recv_sem=self.rsem.at[3 * a + k], device_id=(px, py, self.c), device_id_type=MESH)

    def start(self):
        for a in range(len(self.parts)):
            for k in range(3):
                self._copy(a, k).start()

    def finish(self):
        for a in range(len(self.parts)):
            for k in range(3):
                self._copy(a, k).wait()


def _rs_chips(parts, name):
    n = len(parts)

    def body(*refs):
        scatter = _ChipScatter(refs[:n], refs[n:2 * n], refs[2 * n], refs[2 * n + 1])
        scatter.start()
        scatter.finish()

    hbm = pl.BlockSpec(memory_space=pl.ANY)
    return pl.pallas_call(
        body, name=name, out_shape=tuple(jax.ShapeDtypeStruct((3,) + p.shape[1:], BF16) for p in parts),
        in_specs=[hbm] * n, out_specs=(hbm,) * n,
        scratch_shapes=[pltpu.SemaphoreType.DMA((3 * n,)), pltpu.SemaphoreType.DMA((3 * n,))],
    )(*parts)


def _rs_join(halves, name):
    n = len(halves)

    def body(*refs):
        h_refs, f_refs, stages = refs[:n], refs[n:2 * n], refs[2 * n:3 * n]
        lsem_in, lsem_out, ssem, rsem = refs[3 * n:]
        x, y, c = _pos()
        remote = []
        for a in range(n):
            mine, _ = _halves(f_refs[a], c, 8)
            cp = pltpu.make_async_remote_copy(
                src_ref=h_refs[a], dst_ref=f_refs[a].at[mine], send_sem=ssem.at[a], recv_sem=rsem.at[a],
                device_id=(x, y, 1 - c), device_id_type=MESH)
            cp.start()
            remote.append(cp)
            pltpu.make_async_copy(h_refs[a], stages[a], lsem_in.at[a]).start()
        local = []
        for a in range(n):
            mine, _ = _halves(f_refs[a], c, 8)
            pltpu.make_async_copy(h_refs[a], stages[a], lsem_in.at[a]).wait()
            cp = pltpu.make_async_copy(stages[a], f_refs[a].at[mine], lsem_out.at[a])
            cp.start()
            local.append(cp)
        for cp in remote + local:
            cp.wait()

    hbm = pl.BlockSpec(memory_space=pl.ANY)
    return pl.pallas_call(
        body, name=name,
        out_shape=tuple(jax.ShapeDtypeStruct((2 * h.shape[0], h.shape[1]), F32) for h in halves),
        in_specs=[hbm] * n, out_specs=(hbm,) * n,
        scratch_shapes=[pltpu.VMEM(h.shape, F32) for h in halves]
        + [pltpu.SemaphoreType.DMA((n,)) for _ in range(4)],
        compiler_params=pltpu.CompilerParams(vmem_limit_bytes=STAGE_VMEM_LIMIT),
    )(*halves)


def _row_block(rows):
    return max(b for b in range(16, 513, 16) if rows % b == 0)


def _add_pairs(arr, got, c_idx, name):
    hr, cols = got.shape[1], got.shape[2]
    rb = _row_block(hr)
    nb = hr // rb

    def body(c_ref, a_ref, b_ref, o_ref):
        o_ref[...] = (a_ref[...].astype(F32) + b_ref[...].astype(F32)).astype(BF16)

    spec = pl.BlockSpec((1, rb, cols), lambda s, r, c_ref: (s, r, 0))
    grid_spec = pltpu.PrefetchScalarGridSpec(
        num_scalar_prefetch=1, grid=(N_SHARD, nb),
        in_specs=[pl.BlockSpec((1, rb, cols), lambda s, r, c_ref: (s, c_ref[0] * nb + r, 0)), spec],
        out_specs=spec)
    return pl.pallas_call(
        body, name=name, grid_spec=grid_spec, out_shape=jax.ShapeDtypeStruct(got.shape, BF16),
    )(c_idx, arr, got)


def _add_chips(part, got, j_idx, name):
    hr, cols = got.shape[1], got.shape[2]
    rb = _row_block(hr)

    def body(j_ref, p_ref, g_ref, o_ref):
        acc = p_ref[0].astype(F32)
        for k in range(3):
            acc = acc + g_ref[k].astype(F32)
        o_ref[...] = acc

    grid_spec = pltpu.PrefetchScalarGridSpec(
        num_scalar_prefetch=1, grid=(hr // rb,),
        in_specs=[pl.BlockSpec((1, rb, cols), lambda r, j_ref: (j_ref[0], r, 0)),
                  pl.BlockSpec((3, rb, cols), lambda r, j_ref: (0, r, 0))],
        out_specs=pl.BlockSpec((rb, cols), lambda r, j_ref: (r, 0)))
    return pl.pallas_call(
        body, name=name, grid_spec=grid_spec, out_shape=jax.ShapeDtypeStruct((hr, cols), F32),
    )(j_idx, part, got)


def _small_exchange(sv, w_mod_sh, cctx, m_cctx, v_cctx, bmod, m_bmod, v_bmod, qg, m_qg, v_qg, kvg, m_kvg, v_kvg,
                    gf, m_gf, v_gf):
    ncol = w_mod_sh.shape[1]

    def body(sv_ref, w_ref, cctx_ref, mcc_ref, vcc_ref, b_ref, mb_ref, vb_ref, qg_ref, mq_ref, vq_ref,
             kg_ref, mk_ref, vk_ref, gf_ref, mgf_ref, vgf_ref,
             all_ref, red_ref, occ_ref, ob_ref, oq_ref, ok_ref, ogf_ref,
             vec_ref, part_ref, ssem, rsem, ssem2, rsem2):
        x, y, c = _pos()
        me = 4 * x + 2 * y + c
        j = 2 * x + y
        sends = []
        for r in range(1, 8):
            cp = pltpu.make_async_remote_copy(
                src_ref=sv_ref, dst_ref=all_ref.at[me], send_sem=ssem.at[r - 1], recv_sem=rsem.at[r - 1],
                device_id=_peer(r, x, y, c), device_id_type=MESH)
            cp.start()
            sends.append(cp)
        for cp in sends:
            cp.wait()
        all_ref[me] = sv_ref[...]
        red = all_ref[0]
        for d in range(1, 8):
            red = red + all_ref[d]
        red_ref[...] = red
        vec_ref[...] = jnp.zeros(vec_ref.shape, F32)

        @pl.when(j == 0)
        def _():
            vec_ref[0:1, 0:1024] = red[6:7, :]
            vec_ref[0:1, 1024:1536] = red[7:8, 0:512]

        @pl.when(j == 1)
        def _():
            vec_ref[0:1, 0:512] = red[7:8, 512:1024]

        part = lax.dot_general(vec_ref[...], w_ref[...], (((1,), (1,)), ((), ())), precision=HIGHEST,
                               preferred_element_type=F32)
        part_ref[j] = part
        sends2 = []
        for k, r in enumerate((4, 2, 6)):
            cp = pltpu.make_async_remote_copy(
                src_ref=part_ref.at[j], dst_ref=part_ref.at[j], send_sem=ssem2.at[k], recv_sem=rsem2.at[k],
                device_id=_peer(r, x, y, c), device_id_type=MESH)
            cp.start()
            sends2.append(cp)
        for cp in sends2:
            cp.wait()
        tot = part_ref[0]
        for s in range(1, N_SHARD):
            tot = tot + part_ref[s]
        cc = cctx_ref[...]
        sg = _sigmoid(cc)
        g_cc = tot[0:1, :] * (sg * (1.0 + cc * (1.0 - sg)))
        d_, m_, v_ = _adamw(cc, g_cc, mcc_ref[...], vcc_ref[...])
        occ_ref[0:1, :] = g_cc
        occ_ref[1:2, :] = d_
        occ_ref[2:3, :] = m_
        occ_ref[3:4, :] = v_
        occ_ref[4:8, :] = jnp.zeros((4, D_MODEL), F32)
        g_b = red[0:6, :]
        pad = jnp.concatenate([red[6:8, :], jnp.zeros((4, D_MODEL), F32)], axis=0)
        g_b = g_b + pad
        d_, m_, v_ = _adamw(b_ref[...], g_b, mb_ref[...], vb_ref[...])
        ob_ref[0] = g_b
        ob_ref[1] = d_
        ob_ref[2] = m_
        ob_ref[3] = v_
        g_q = red[9:10, 0:Q_RANK]
        d_, m_, v_ = _adamw(qg_ref[...], g_q, mq_ref[...], vq_ref[...])
        oq_ref[0:1, :] = g_q
        oq_ref[1:2, :] = d_
        oq_ref[2:3, :] = m_
        oq_ref[3:4, :] = v_
        oq_ref[4:8, :] = jnp.zeros((4, Q_RANK), F32)
        g_k = red[10:11, 0:KV_RANK]
        d_, m_, v_ = _adamw(kg_ref[...], g_k, mk_ref[...], vk_ref[...])
        ok_ref[0:1, :] = g_k
        ok_ref[1:2, :] = d_
        ok_ref[2:3, :] = m_
        ok_ref[3:4, :] = v_
        ok_ref[4:8, :] = jnp.zeros((4, KV_RANK), F32)
        g_f = red[8:9, :]
        d_, m_, v_ = _adamw(gf_ref[...], g_f, mgf_ref[...], vgf_ref[...])
        ogf_ref[0:1, :] = g_f
        ogf_ref[1:2, :] = d_
        ogf_ref[2:3, :] = m_
        ogf_ref[3:4, :] = v_
        ogf_ref[4:8, :] = jnp.zeros((4, D_MODEL), F32)

    vm = pl.BlockSpec(memory_space=pltpu.VMEM)
    out_shape = (
        jax.ShapeDtypeStruct((8, 16, D_MODEL), F32),
        jax.ShapeDtypeStruct((16, D_MODEL), F32),
        jax.ShapeDtypeStruct((8, D_MODEL), F32),
        jax.ShapeDtypeStruct((4, 6, D_MODEL), F32),
        jax.ShapeDtypeStruct((8, Q_RANK), F32),
        jax.ShapeDtypeStruct((8, KV_RANK), F32),
        jax.ShapeDtypeStruct((8, D_MODEL), F32),
    )
    return pl.pallas_call(
        body, name="small_exchange", out_shape=out_shape, in_specs=[vm] * 17, out_specs=tuple([vm] * 7),
        scratch_shapes=[pltpu.VMEM((8, ncol), F32), pltpu.VMEM((N_SHARD, 8, D_MODEL), F32),
                        pltpu.SemaphoreType.DMA((7,)), pltpu.SemaphoreType.DMA((7,)),
                        pltpu.SemaphoreType.DMA((3,)), pltpu.SemaphoreType.DMA((3,))],
        compiler_params=pltpu.CompilerParams(vmem_limit_bytes=VMEM_LIMIT),
    )(sv, w_mod_sh, cctx, m_cctx, v_cctx, bmod, m_bmod, v_bmod, qg, m_qg, v_qg, kvg, m_kvg, v_kvg, gf, m_gf, v_gf)


def _inproj_fwd(x2, ctx2, mod_a, w_in, qg, kvg, w_uq, w_ukv, cos_t, sin_a, sin_b):
    t_lat, t_ctx = x2.shape[0], ctx2.shape[0]
    tm = TOK_TILE
    n_lat = t_lat // tm
    n_all = n_lat + t_ctx // tm
    e_rows = t_lat + t_ctx

    def body(x_ref, ctx_ref, mod_ref, win_ref, qg_ref, kvg_ref, wuq_ref, wukv_ref, cos_ref, sa_ref, sb_ref,
             z_ref, q_ref, k_ref, v_ref, kt_ref):
        i = pl.program_id(0)
        xin = jnp.where(i < n_lat, x_ref[...], ctx_ref[...])
        xn = xin * lax.rsqrt(jnp.mean(xin * xin, axis=-1, keepdims=True) + EPS)
        h1 = (xn * (1.0 + mod_ref[0, 1:2, :]) + mod_ref[0, 0:1, :]).astype(BF16)
        z = _dot(h1, win_ref[...])
        z_ref[...] = z
        cos, sa, sb = cos_ref[...], sa_ref[...], sb_ref[...]
        cq = z[:, 0:Q_RANK]
        cqn = (cq * lax.rsqrt(jnp.mean(cq * cq, axis=-1, keepdims=True) + EPS) * qg_ref[...]).astype(BF16)
        q = _dot(cqn, wuq_ref[...])
        ckv = z[:, Q_RANK:Q_RANK + KV_RANK]
        ckvn = (ckv * lax.rsqrt(jnp.mean(ckv * ckv, axis=-1, keepdims=True) + EPS) * kvg_ref[...]).astype(BF16)
        kv = _dot(ckvn, wukv_ref[...])
        kr = _rope(z[:, Q_RANK + KV_RANK:Q_RANK + KV_RANK + HEAD_PAD], cos, sa, sb)
        for h in range(N_HEADS):
            lo = h * HEAD_PAD
            q_ref[h] = _rope(q[:, lo:lo + HEAD_PAD], cos, sa, sb).astype(BF16)
            kh = kv[:, lo:lo + HEAD_PAD] + kr
            k_ref[h] = kh.astype(BF16)
            kt_ref[h] = kh.T.astype(BF16)
            v_ref[h] = kv[:, N_HEADS * HEAD_PAD + lo:N_HEADS * HEAD_PAD + lo + HEAD_PAD].astype(BF16)

    row = lambda i: (i, 0)
    head_spec = pl.BlockSpec((N_HEADS, tm, HEAD_PAD), lambda i: (0, i, 0))
    head_shape = jax.ShapeDtypeStruct((N_HEADS, e_rows, HEAD_PAD), BF16)
    return pl.pallas_call(
        body, name="inproj_fwd", grid=(n_all,),
        out_shape=(jax.ShapeDtypeStruct((e_rows, Z_COLS), F32), head_shape, head_shape, head_shape,
                   jax.ShapeDtypeStruct((N_HEADS, HEAD_PAD, e_rows), BF16)),
        in_specs=[
            pl.BlockSpec((tm, D_MODEL), lambda i: (jnp.minimum(i, n_lat - 1), 0)),
            _const_spec((tm, D_MODEL)),
            pl.BlockSpec((1, 8, D_MODEL), lambda i: (i // n_lat, 0, 0)),
            _const_spec(w_in.shape), _const_spec(qg.shape), _const_spec(kvg.shape),
            _const_spec(w_uq.shape), _const_spec(w_ukv.shape),
            pl.BlockSpec((tm, HEAD_PAD), row), pl.BlockSpec((tm, HEAD_PAD), row), pl.BlockSpec((tm, HEAD_PAD), row),
        ],
        out_specs=(pl.BlockSpec((tm, Z_COLS), row), head_spec, head_spec, head_spec,
                   pl.BlockSpec((N_HEADS, HEAD_PAD, tm), lambda i: (0, 0, i))),
        compiler_params=pltpu.CompilerParams(vmem_limit_bytes=VMEM_LIMIT),
    )(x2, ctx2, mod_a, w_in, qg, kvg, w_uq, w_ukv, cos_t, sin_a, sin_b)


def _attn_fwd(q, k, v, t_lat, shard_arrays):
    e_rows = k.shape[1]
    tq = 256
    n_chunks = max(1, e_rows // KEY_CHUNK)
    bounds = [(ci * KEY_CHUNK, KEY_CHUNK if ci < n_chunks - 1 else e_rows - ci * KEY_CHUNK) for ci in range(n_chunks)]
    c2 = ATTN_SCALE * LOG2E

    hb = ATTN_HEADS_PER_STEP
    n_hb = N_HEADS // hb

    def body(q_ref, k_ref, v_ref, o_ref, lse_ref):
        qs = [q_ref[b] for b in range(hb)]
        m, l, acc = [None] * hb, [None] * hb, [None] * hb
        for lo, n in bounds:
            for b in range(hb):
                s = _dot_nt(qs[b], k_ref[b, lo:lo + n, :])
                mc = jnp.max(s, axis=-1, keepdims=True)
                m_new = mc if m[b] is None else jnp.maximum(m[b], mc)
                p = jnp.exp2((s - m_new) * c2)
                pv = _dot(p.astype(BF16), v_ref[b, lo:lo + n, :])
                ps = jnp.sum(p, axis=-1, keepdims=True)
                if m[b] is None:
                    l[b], acc[b] = ps, pv
                else:
                    alpha = jnp.exp2((m[b] - m_new) * c2)
                    l[b] = l[b] * alpha + ps
                    acc[b] = acc[b] * alpha + pv
                m[b] = m_new
        for b in range(hb):
            o_ref[b] = (acc[b] * (1.0 / l[b])).astype(BF16)
            lse = (m[b] * ATTN_SCALE + jnp.log(l[b])) * LOG2E
            lse_ref[b] = jnp.broadcast_to(lse, (tq, HEAD_PAD)).T[0:1, :]

    n_w = len(shard_arrays)
    n_q = t_lat // tq

    def body_with_gather(q_ref, k_ref, v_ref, *rest):
        o_ref, lse_ref = rest[n_w], rest[n_w + 1]
        gather = _ShardGather(rest[n_w + 2:2 * n_w + 2], *rest[2 * n_w + 2:])
        h, i = pl.program_id(0), pl.program_id(1)
        pl.when(jnp.logical_and(h == 0, i == 0))(gather.start)
        pl.when(jnp.logical_and(h == n_hb // 2, i == 0))(gather.forward)
        body(q_ref, k_ref, v_ref, o_ref, lse_ref)
        pl.when(jnp.logical_and(h == n_hb - 1, i == n_q - 1))(gather.finish)

    hbm = pl.BlockSpec(memory_space=pl.ANY)
    return pl.pallas_call(
        body_with_gather, name="attn_fwd", grid=(n_hb, n_q),
        out_shape=(jax.ShapeDtypeStruct((N_HEADS, t_lat, HEAD_PAD), BF16),
                   jax.ShapeDtypeStruct((N_HEADS, 1, t_lat), F32))
        + tuple(jax.ShapeDtypeStruct(a.shape, a.dtype) for a in shard_arrays),
        in_specs=[pl.BlockSpec((hb, tq, HEAD_PAD), lambda h, i: (h, i, 0)),
                  pl.BlockSpec((hb, e_rows, HEAD_PAD), lambda h, i: (h, 0, 0)),
                  pl.BlockSpec((hb, e_rows, HEAD_PAD), lambda h, i: (h, 0, 0))] + [hbm] * n_w,
        out_specs=(pl.BlockSpec((hb, tq, HEAD_PAD), lambda h, i: (h, i, 0)),
                   pl.BlockSpec((hb, 1, tq), lambda h, i: (h, 0, i))) + (hbm,) * n_w,
        input_output_aliases={3 + a: 2 + a for a in range(n_w)},
        scratch_shapes=_gather_sems(n_w),
        compiler_params=pltpu.CompilerParams(vmem_limit_bytes=VMEM_LIMIT),
    )(q, k, v, *shard_arrays)


def _attn_bwd(q, k, v, kt, o, do, lse_row, t_lat, parts):
    e_rows = k.shape[1]
    tq = 256
    n_p = len(parts)
    n_q = t_lat // tq

    def body(q_ref, k_ref, v_ref, kt_ref, o_ref, do_ref, lse_ref, *rest):
        dqt_ref, dk_ref, dv_ref = rest[n_p:n_p + 3]
        scatter = _ChipScatter(rest[:n_p], rest[n_p + 3:2 * n_p + 3], rest[2 * n_p + 3], rest[2 * n_p + 4])
        h, i = pl.program_id(0), pl.program_id(1)
        pl.when(jnp.logical_and(h == 0, i == 0))(scatter.start)
        qb, dob = q_ref[0], do_ref[0]
        prod = o_ref[0].astype(F32) * dob.astype(F32)
        delta = lax.dot_general(jnp.ones((8, HEAD_PAD), F32), prod, (((1,), (1,)), ((), ())), precision=HIGHEST,
                                preferred_element_type=F32)[0:1, :]
        pt = jnp.exp2(_dot_nt(k_ref[0], qb) * (ATTN_SCALE * LOG2E) - lse_ref[0])
        dpt = _dot_nt(v_ref[0], dob)
        dst = (pt * (dpt - delta) * ATTN_SCALE).astype(BF16)
        dv_c = _dot(pt.astype(BF16), dob)
        dk_c = _dot(dst, qb)
        dqt_ref[0] = _dot(kt_ref[0], dst)

        @pl.when(i == 0)
        def _():
            dk_ref[0] = dk_c
            dv_ref[0] = dv_c

        @pl.when(i > 0)
        def _():
            dk_ref[0] += dk_c
            dv_ref[0] += dv_c

        pl.when(jnp.logical_and(h == N_HEADS - 1, i == n_q - 1))(scatter.finish)

    hbm = pl.BlockSpec(memory_space=pl.ANY)
    qspec = pl.BlockSpec((1, tq, HEAD_PAD), lambda h, i: (h, i, 0))
    kspec = pl.BlockSpec((1, e_rows, HEAD_PAD), lambda h, i: (h, 0, 0))
    return pl.pallas_call(
        body, name="attn_bwd", grid=(N_HEADS, n_q),
        out_shape=(jax.ShapeDtypeStruct((N_HEADS, HEAD_PAD, t_lat), F32),
                   jax.ShapeDtypeStruct((N_HEADS, e_rows, HEAD_PAD), F32),
                   jax.ShapeDtypeStruct((N_HEADS, e_rows, HEAD_PAD), F32))
        + tuple(jax.ShapeDtypeStruct((3,) + p.shape[1:], BF16) for p in parts),
        in_specs=[qspec, kspec, kspec, pl.BlockSpec((1, HEAD_PAD, e_rows), lambda h, i: (h, 0, 0)), qspec, qspec,
                  pl.BlockSpec((1, 1, tq), lambda h, i: (h, 0, i))] + [hbm] * n_p,
        out_specs=(pl.BlockSpec((1, HEAD_PAD, tq), lambda h, i: (h, 0, i)), kspec, kspec) + (hbm,) * n_p,
        scratch_shapes=[pltpu.SemaphoreType.DMA((3 * n_p,)), pltpu.SemaphoreType.DMA((3 * n_p,))],
        compiler_params=pltpu.CompilerParams(vmem_limit_bytes=VMEM_LIMIT),
    )(q, k, v, kt, o, do, lse_row, *parts)


def _halo_specs(tm, col_block):
    per = tm // 8
    prev = pl.BlockSpec((8, CONV_W), lambda i: (jnp.maximum(i * per - 1, 0), col_block))
    nxt = pl.BlockSpec((8, CONV_W), lambda i: ((i + 1) * per, col_block))
    return prev, nxt


def _mlp_fwdbwd(o, z, x2, tgt, mod_b, gf, cw, wo_attn, wo_conv, w1, w2):
    t_lat = x2.shape[0]
    tm = TOK_TILE
    n_lat = t_lat // tm
    n_ff = D_FF // FF_CHUNK

    def body(o_ref, gb_ref, gc_ref, xi_ref, gcp_ref, xip_ref, gcn_ref, xin_ref, cw_ref, woa_ref, woc_ref,
             x_ref, t_ref, mod_ref, gf_ref, w1_ref, w2_ref,
             r_ref, da_ref, h2_ref, dy2_ref, dx1_ref, y1_ref, yv_ref, conv_ref, acc_ref, ra_ref):
        i = pl.program_id(0)

        @pl.when(i == 0)
        def _():
            acc_ref[...] = jnp.zeros(acc_ref.shape, F32)

        g1, sh2, sc2, g2 = mod_ref[0:1, :], mod_ref[1:2, :], mod_ref[2:3, :], mod_ref[3:4, :]
        u = gc_ref[...] * xi_ref[...]
        u_prev = jnp.where(i > 0, gcp_ref[7:8, :] * xip_ref[7:8, :], 0.0)
        u_next = jnp.where(i < n_lat - 1, gcn_ref[0:1, :] * xin_ref[0:1, :], 0.0)
        um1, up1 = _shift_rows(u, u_prev, u_next)
        yv = cw_ref[0:1, :] * um1 + cw_ref[1:2, :] * u + cw_ref[2:3, :] * up1
        yv_ref[...] = yv
        conv = (gb_ref[...] * yv).astype(BF16)
        conv_ref[...] = conv
        y1 = _dot(conv, woc_ref[...])
        for h in range(N_HEADS):
            y1 = y1 + _dot(o_ref[h], woa_ref[h])
        y1_ref[...] = y1
        x1 = x_ref[...] + g1 * y1
        rstd2 = lax.rsqrt(jnp.mean(x1 * x1, axis=-1, keepdims=True) + EPS)
        xn1 = x1 * rstd2
        h2 = (xn1 * (1.0 + sc2) + sh2).astype(BF16)
        h2_ref[...] = h2
        y2 = jnp.zeros((tm, D_MODEL), F32)
        for jj in range(n_ff):
            ra = jnp.maximum(_dot(h2, w1_ref[jj]), 0.0)
            ra_ref[jj] = ra
            r = (ra * ra).astype(BF16)
            r_ref[:, jj * FF_CHUNK:(jj + 1) * FF_CHUNK] = r
            y2 = y2 + _dot(r, w2_ref[jj * FF_CHUNK:(jj + 1) * FF_CHUNK, :])
        x2v = x1 + g2 * y2
        rstd3 = lax.rsqrt(jnp.mean(x2v * x2v, axis=-1, keepdims=True) + EPS)
        xn3 = x2v * rstd3
        gfv = gf_ref[...]
        diff = xn3 * gfv - t_ref[...]
        loss_t = 0.5 * jnp.sum(jnp.sum(diff * diff, axis=-1, keepdims=True), axis=0, keepdims=True) * (1.0 / D_MODEL)
        dy = diff * (1.0 / D_MODEL)
        dxn3 = dy * gfv
        dx2 = rstd3 * (dxn3 - xn3 * jnp.mean(dxn3 * xn3, axis=-1, keepdims=True))
        dy2 = (dx2 * g2).astype(BF16)
        dy2_ref[...] = dy2
        dh2 = jnp.zeros((tm, D_MODEL), F32)
        for jj in range(n_ff):
            dr = _dot_nt(dy2, w2_ref[jj * FF_CHUNK:(jj + 1) * FF_CHUNK, :])
            da = (2.0 * ra_ref[jj] * dr).astype(BF16)
            da_ref[:, jj * FF_CHUNK:(jj + 1) * FF_CHUNK] = da
            dh2 = dh2 + _dot_nt(da, w1_ref[jj])
        dxn1 = dh2 * (1.0 + sc2)
        dx1_ref[...] = dx2 + rstd2 * (dxn1 - xn1 * jnp.mean(dxn1 * xn1, axis=-1, keepdims=True))
        acc_ref[0:1, :] += jnp.sum(dy * xn3, axis=0, keepdims=True)
        acc_ref[1:2, :] += jnp.sum(dx2 * y2, axis=0, keepdims=True)
        acc_ref[2:3, :] += jnp.sum(dh2, axis=0, keepdims=True)
        acc_ref[3:4, :] += jnp.sum(dh2 * xn1, axis=0, keepdims=True)
        acc_ref[4:5, :] += jnp.broadcast_to(loss_t, (1, D_MODEL))

    row = lambda i: (i, 0)
    gcp, gcn = _halo_specs(tm, 2)
    xip, xin = _halo_specs(tm, 3)
    tile = pl.BlockSpec((tm, D_MODEL), row)
    wide = pl.BlockSpec((tm, D_FF), row)
    half = pl.BlockSpec((tm, CONV_W), row)
    return pl.pallas_call(
        body, name="mlp_fwdbwd", grid=(n_lat,),
        out_shape=(jax.ShapeDtypeStruct((t_lat, D_FF), BF16), jax.ShapeDtypeStruct((t_lat, D_FF), BF16),
                   jax.ShapeDtypeStruct((t_lat, D_MODEL), BF16), jax.ShapeDtypeStruct((t_lat, D_MODEL), BF16),
                   jax.ShapeDtypeStruct((t_lat, D_MODEL), F32), jax.ShapeDtypeStruct((t_lat, D_MODEL), F32),
                   jax.ShapeDtypeStruct((t_lat, CONV_W), F32), jax.ShapeDtypeStruct((t_lat, CONV_W), BF16),
                   jax.ShapeDtypeStruct((8, D_MODEL), F32)),
        in_specs=[
            pl.BlockSpec((N_HEADS, tm, HEAD_PAD), lambda i: (0, i, 0)),
            pl.BlockSpec((tm, CONV_W), lambda i: (i, 1)), pl.BlockSpec((tm, CONV_W), lambda i: (i, 2)),
            pl.BlockSpec((tm, CONV_W), lambda i: (i, 3)),
            gcp, xip, gcn, xin,
            _const_spec(cw.shape), _resident_spec(wo_attn.shape), _resident_spec(wo_conv.shape),
            tile, tile, _const_spec(mod_b.shape), _const_spec(gf.shape),
            _resident_spec(w1.shape), _resident_spec(w2.shape),
        ],
        out_specs=(wide, wide, tile, tile, tile, tile, half, half, _const_spec((8, D_MODEL))),
        scratch_shapes=[pltpu.VMEM((n_ff, tm, FF_CHUNK), F32)],
        compiler_params=pltpu.CompilerParams(vmem_limit_bytes=VMEM_LIMIT),
    )(o, z, z, z, z, z, z, z, cw, wo_attn, wo_conv, x2, tgt, mod_b, gf, w1, w2)


def _mid_bwd(dx1, y1, z, yv, mod_b, wo_attn, wo_conv):
    t_lat = dx1.shape[0]
    tm = TOK_TILE

    def body(dx1_ref, y1_ref, gb_ref, yv_ref, mod_ref, woa_ref, woc_ref, dy1_ref, do_ref, dgb_ref, dyv_ref, acc_ref):
        i = pl.program_id(0)

        @pl.when(i == 0)
        def _():
            acc_ref[...] = jnp.zeros(acc_ref.shape, F32)

        dx1v = dx1_ref[...]
        acc_ref[0:1, :] += jnp.sum(dx1v * y1_ref[...], axis=0, keepdims=True)
        dy1 = (dx1v * mod_ref[0:1, :]).astype(BF16)
        dy1_ref[...] = dy1
        for h in range(N_HEADS):
            do_ref[h] = _dot_nt(dy1, woa_ref[h]).astype(BF16)
        dconv = _dot_nt(dy1, woc_ref[...])
        dgb_ref[...] = dconv * yv_ref[...]
        dyv_ref[...] = dconv * gb_ref[...]

    row = lambda i: (i, 0)
    tile = pl.BlockSpec((tm, D_MODEL), row)
    half = pl.BlockSpec((tm, CONV_W), row)
    return pl.pallas_call(
        body, name="mid_bwd", grid=(t_lat // tm,),
        out_shape=(jax.ShapeDtypeStruct((t_lat, D_MODEL), BF16),
                   jax.ShapeDtypeStruct((N_HEADS, t_lat, HEAD_PAD), BF16),
                   jax.ShapeDtypeStruct((t_lat, CONV_W), F32), jax.ShapeDtypeStruct((t_lat, CONV_W), F32),
                   jax.ShapeDtypeStruct((8, D_MODEL), F32)),
        in_specs=[tile, tile, pl.BlockSpec((tm, CONV_W), lambda i: (i, 1)), half, _const_spec(mod_b.shape),
                  _const_spec(wo_attn.shape), _const_spec(wo_conv.shape)],
        out_specs=(tile, pl.BlockSpec((N_HEADS, tm, HEAD_PAD), lambda i: (0, i, 0)), half, half,
                   _const_spec((8, D_MODEL))),
        compiler_params=pltpu.CompilerParams(vmem_limit_bytes=VMEM_LIMIT),
    )(dx1, y1, z, yv, mod_b, wo_attn, wo_conv)


def _inproj_bwd(x2, ctx2, mod_a, z, dyv, dgb, dx1, dqt, dk, dv, cos_t, sin_a, sin_b, w_in, w_uq, w_ukv, qg, kvg, cw):
    t_lat, t_ctx = x2.shape[0], ctx2.shape[0]
    tm = TOK_TILE
    n_lat = t_lat // tm
    n_all = n_lat + t_ctx // tm

    def body(x_ref, ctx_ref, mod_ref, z_ref, gcp_ref, xip_ref, gcn_ref, xin_ref, dyv_ref, dyvp_ref, dyvn_ref,
             dgb_ref, dx1_ref, dqt_ref, dk_ref, dv_ref, cos_ref, sa_ref, sb_ref, win_ref, wuq_ref, wukv_ref,
             qg_ref, kvg_ref, cw_ref, gx_ref, dwin_ref, dwuq_ref, dwukv_ref, acc_ref):
        i = pl.program_id(0)
        lat = i < n_lat

        @pl.when(i == 0)
        def _():
            dwin_ref[...] = jnp.zeros(dwin_ref.shape, F32)
            dwuq_ref[...] = jnp.zeros(dwuq_ref.shape, F32)
            dwukv_ref[...] = jnp.zeros(dwukv_ref.shape, F32)
            acc_ref[...] = jnp.zeros(acc_ref.shape, F32)

        xin = jnp.where(lat, x_ref[...], ctx_ref[...])
        rstd = lax.rsqrt(jnp.mean(xin * xin, axis=-1, keepdims=True) + EPS)
        xn = xin * rstd
        sc = mod_ref[0, 1:2, :]
        h1 = (xn * (1.0 + sc) + mod_ref[0, 0:1, :]).astype(BF16)
        z = z_ref[...]
        cos, sa, sb = cos_ref[...], sa_ref[...], sb_ref[...]
        qgv, kvgv = qg_ref[...], kvg_ref[...]
        cq = z[:, 0:Q_RANK]
        cqh = cq * lax.rsqrt(jnp.mean(cq * cq, axis=-1, keepdims=True) + EPS)
        rq = lax.rsqrt(jnp.mean(cq * cq, axis=-1, keepdims=True) + EPS)
        cqn = (cqh * qgv).astype(BF16)
        parts = []
        for h in range(N_HEADS):
            g = jnp.where(lat, dqt_ref[h].T, 0.0)
            parts.append(_unrope(g, cos, sa, sb))
        dq = jnp.concatenate(parts, axis=1).astype(BF16)
        dcqn = _dot_nt(dq, wuq_ref[...])
        dwuq_ref[...] += _dot_tn(cqn, dq)
        acc_ref[4:5, 0:Q_RANK] += jnp.sum(dcqn * cqh, axis=0, keepdims=True)
        dxn = dcqn * qgv
        dcq = rq * (dxn - cqh * jnp.mean(dxn * cqh, axis=-1, keepdims=True))
        ckv = z[:, Q_RANK:Q_RANK + KV_RANK]
        rk = lax.rsqrt(jnp.mean(ckv * ckv, axis=-1, keepdims=True) + EPS)
        ckvh = ckv * rk
        ckvn = (ckvh * kvgv).astype(BF16)
        dks = [dk_ref[h] for h in range(N_HEADS)]
        dkr = dks[0]
        for h in range(1, N_HEADS):
            dkr = dkr + dks[h]
        dkv = jnp.concatenate(dks + [dv_ref[h] for h in range(N_HEADS)], axis=1).astype(BF16)
        dckvn = _dot_nt(dkv, wukv_ref[...])
        dwukv_ref[...] += _dot_tn(ckvn, dkv)
        acc_ref[5:6, 0:KV_RANK] += jnp.sum(dckvn * ckvh, axis=0, keepdims=True)
        dxn = dckvn * kvgv
        dckv = rk * (dxn - ckvh * jnp.mean(dxn * ckvh, axis=-1, keepdims=True))
        dkr = _unrope(dkr, cos, sa, sb)
        gb, gc, xi = z[:, 512:1024], z[:, 1024:1536], z[:, 1536:2048]
        u = gc * xi
        u_prev = jnp.where(i > 0, gcp_ref[7:8, :] * xip_ref[7:8, :], 0.0)
        u_next = jnp.where(i < n_lat - 1, gcn_ref[0:1, :] * xin_ref[0:1, :], 0.0)
        um1, up1 = _shift_rows(u, u_prev, u_next)
        dyv = jnp.where(lat, dyv_ref[...], 0.0)
        dyv_prev = jnp.where(jnp.logical_and(i > 0, lat), dyvp_ref[7:8, :], 0.0)
        dyv_next = jnp.where(i < n_lat - 1, dyvn_ref[0:1, :], 0.0)
        dyv_m1, dyv_p1 = _shift_rows(dyv, dyv_prev, dyv_next)
        du = cw_ref[0:1, :] * dyv_p1 + cw_ref[1:2, :] * dyv + cw_ref[2:3, :] * dyv_m1
        dgc = du * xi
        dxi = du * gc
        dgb = jnp.where(lat, dgb_ref[...], 0.0)
        acc_ref[6:7, 0:CONV_W] += jnp.sum(dyv * um1, axis=0, keepdims=True)
        acc_ref[7:8, 0:CONV_W] += jnp.sum(dyv * u, axis=0, keepdims=True)
        acc_ref[8:9, 0:CONV_W] += jnp.sum(dyv * up1, axis=0, keepdims=True)
        dz = jnp.concatenate([dcq, dckv, dkr, dgb, dgc, dxi], axis=1).astype(BF16)
        dh1 = _dot_nt(dz, win_ref[...])
        dwin_ref[...] += _dot_tn(h1, dz)
        s_sh = jnp.sum(dh1, axis=0, keepdims=True)
        s_sc = jnp.sum(dh1 * xn, axis=0, keepdims=True)
        zero = jnp.zeros_like(s_sh)
        acc_ref[0:1, :] += jnp.where(lat, s_sh, zero)
        acc_ref[1:2, :] += jnp.where(lat, s_sc, zero)
        acc_ref[2:3, :] += jnp.where(lat, zero, s_sh)
        acc_ref[3:4, :] += jnp.where(lat, zero, s_sc)
        dxn = dh1 * (1.0 + sc)
        dx = rstd * (dxn - xn * jnp.mean(dxn * xn, axis=-1, keepdims=True))

        @pl.when(lat)
        def _():
            gx_ref[...] = dx1_ref[...] + dx

    last = n_lat - 1
    per = tm // 8
    lat_row = lambda i: (jnp.minimum(i, last), 0)
    row = lambda i: (i, 0)
    gcp, gcn = _halo_specs(tm, 2)
    xip, xin = _halo_specs(tm, 3)
    n_halo = t_lat // 8
    dyvp = pl.BlockSpec((8, CONV_W), lambda i: (jnp.clip(i * per - 1, 0, n_halo - 1), 0))
    dyvn = pl.BlockSpec((8, CONV_W), lambda i: (jnp.minimum((i + 1) * per, n_halo - 1), 0))
    gcn = pl.BlockSpec((8, CONV_W), lambda i: (jnp.minimum((i + 1) * per, (t_lat + t_ctx) // 8 - 1), 2))
    xin = pl.BlockSpec((8, CONV_W), lambda i: (jnp.minimum((i + 1) * per, (t_lat + t_ctx) // 8 - 1), 3))
    head_f32 = pl.BlockSpec((N_HEADS, tm, HEAD_PAD), lambda i: (0, i, 0))
    tab = pl.BlockSpec((tm, HEAD_PAD), row)
    return pl.pallas_call(
        body, name="inproj_bwd", grid=(n_all,),
        out_shape=(jax.ShapeDtypeStruct((t_lat, D_MODEL), F32), jax.ShapeDtypeStruct(w_in.shape, F32),
                   jax.ShapeDtypeStruct(w_uq.shape, F32), jax.ShapeDtypeStruct(w_ukv.shape, F32),
                   jax.ShapeDtypeStruct((16, D_MODEL), F32)),
        in_specs=[
            pl.BlockSpec((tm, D_MODEL), lat_row), _const_spec((tm, D_MODEL)),
            pl.BlockSpec((1, 8, D_MODEL), lambda i: (i // n_lat, 0, 0)),
            pl.BlockSpec((tm, Z_COLS), row), gcp, xip, gcn, xin,
            pl.BlockSpec((tm, CONV_W), lat_row), dyvp, dyvn,
            pl.BlockSpec((tm, CONV_W), lat_row), pl.BlockSpec((tm, D_MODEL), lat_row),
            pl.BlockSpec((N_HEADS, HEAD_PAD, tm), lambda i: (0, 0, jnp.minimum(i, last))),
            head_f32, head_f32, tab, tab, tab,
            _const_spec(w_in.shape), _const_spec(w_uq.shape), _const_spec(w_ukv.shape),
            _const_spec(qg.shape), _const_spec(kvg.shape), _const_spec(cw.shape),
        ],
        out_specs=(pl.BlockSpec((tm, D_MODEL), lat_row), _const_spec(w_in.shape), _const_spec(w_uq.shape),
                   _const_spec(w_ukv.shape), _const_spec((16, D_MODEL))),
        compiler_params=pltpu.CompilerParams(vmem_limit_bytes=VMEM_LIMIT),
    )(x2, ctx2, mod_a, z, z, z, z, z, dyv, dyv, dyv, dgb, dx1, dqt, dk, dv, cos_t, sin_a, sin_b, w_in, w_uq, w_ukv,
      qg, kvg, cw)


def _wgrad(a, b, name, bm, bn):
    t, m = a.shape
    n = b.shape[1]
    bk = 512
    nk = t // bk
    nj = n // bn

    def body(a_ref, b_ref, o_ref, acc_ref):
        k = pl.program_id(2)
        part = _dot_tn(a_ref[...], b_ref[...])

        @pl.when(k == 0)
        def _():
            acc_ref[...] = part

        @pl.when(k > 0)
        def _():
            acc_ref[...] += part

        @pl.when(k == nk - 1)
        def _():
            o_ref[...] = acc_ref[...].astype(BF16)

    return pl.pallas_call(
        body, name=name, grid=(m // bm, nj, nk), out_shape=jax.ShapeDtypeStruct((m // bm * nj, bm, bn), BF16),
        in_specs=[pl.BlockSpec((bk, bm), lambda i, j, k: (k, i)), pl.BlockSpec((bk, bn), lambda i, j, k: (k, j))],
        out_specs=pl.BlockSpec((None, bm, bn), lambda i, j, k: (i * nj + j, 0, 0)),
        scratch_shapes=[pltpu.VMEM((bm, bn), F32)],
        compiler_params=pltpu.CompilerParams(vmem_limit_bytes=VMEM_LIMIT),
    )(a, b)


def _wgrad_out(o, conv, dy1):
    t = o.shape[1]
    bk = 512
    nk = t // bk
    rows = N_HEADS * HEAD_PAD + CONV_W

    def body(o_ref, c_ref, d_ref, w_ref, acc_ref):
        k = pl.program_id(0)
        cat = jnp.concatenate([o_ref[h] for h in range(N_HEADS)] + [c_ref[...]], axis=1)
        part = _dot_tn(cat, d_ref[...])

        @pl.when(k == 0)
        def _():
            acc_ref[...] = part

        @pl.when(k > 0)
        def _():
            acc_ref[...] += part

        @pl.when(k == nk - 1)
        def _():
            for h in range(N_HEADS):
                w_ref[h * V_DIM:(h + 1) * V_DIM, :] = acc_ref[h * HEAD_PAD:h * HEAD_PAD + V_DIM, :].astype(BF16)
            w_ref[N_HEADS * V_DIM:, :] = acc_ref[N_HEADS * HEAD_PAD:, :].astype(BF16)

    return pl.pallas_call(
        body, name="wgrad_out", grid=(nk,),
        out_shape=jax.ShapeDtypeStruct((D_MODEL, D_MODEL), BF16),
        in_specs=[pl.BlockSpec((N_HEADS, bk, HEAD_PAD), lambda k: (0, k, 0)),
                  pl.BlockSpec((bk, CONV_W), lambda k: (k, 0)),
                  pl.BlockSpec((bk, D_MODEL), lambda k: (k, 0))],
        out_specs=_const_spec((D_MODEL, D_MODEL)),
        scratch_shapes=[pltpu.VMEM((rows, D_MODEL), F32)],
        compiler_params=pltpu.CompilerParams(vmem_limit_bytes=VMEM_LIMIT),
    )(o, conv, dy1)


def _adamw_call(w, g, m, v, name):
    rows, cols = w.shape
    rb = 256 if rows % 256 == 0 else rows

    def body(w_ref, g_ref, m_ref, v_ref, d_ref, nm_ref, nv_ref):
        d_, m_, v_ = _adamw(w_ref[...], g_ref[...], m_ref[...], v_ref[...])
        d_ref[...] = d_
        nm_ref[...] = m_
        nv_ref[...] = v_

    spec = pl.BlockSpec((rb, cols), lambda i: (i, 0))
    shp = jax.ShapeDtypeStruct((rows, cols), F32)
    return pl.pallas_call(
        body, name=name, grid=(rows // rb,), out_shape=(shp, shp, shp),
        in_specs=[spec] * 4, out_specs=(spec, spec, spec),
    )(w, g, m, v)


def _wmod_update(s_t, dm, w, m, v):
    rows, cols = w.shape
    cb = 512

    def body(s_ref, dm_ref, w_ref, m_ref, v_ref, g_ref, d_ref, nm_ref, nv_ref):
        g = jnp.dot(s_ref[...], dm_ref[...], precision=HIGHEST, preferred_element_type=F32)
        d_, m_, v_ = _adamw(w_ref[...], g, m_ref[...], v_ref[...])
        g_ref[...] = g
        d_ref[...] = d_
        nm_ref[...] = m_
        nv_ref[...] = v_

    spec = pl.BlockSpec((rows, cb), lambda i: (0, i))
    shp = jax.ShapeDtypeStruct((rows, cols), F32)
    return pl.pallas_call(
        body, name="wmod_update", grid=(cols // cb,), out_shape=(shp, shp, shp, shp),
        in_specs=[_const_spec(s_t.shape), pl.BlockSpec((16, cb), lambda i: (0, i)), spec, spec, spec],
        out_specs=(spec, spec, spec, spec),
        compiler_params=pltpu.CompilerParams(vmem_limit_bytes=VMEM_LIMIT),
    )(s_t, dm, w, m, v)


def _rope_tables(t_lat, t_ctx):
    t = jnp.arange(t_lat)
    pos = jnp.stack([(t // GRID_W).astype(F32), (t % GRID_W).astype(F32)], axis=1)
    half = QK_ROPE // 4
    freqs = ROPE_THETA ** (-jnp.arange(0, 2 * half, 2, dtype=F32) / (2 * half))
    ang = pos[:, :, None] * freqs[None, None, :]
    cos, sin = jnp.cos(ang), jnp.sin(ang)
    zero = jnp.zeros_like(sin)
    cos32 = jnp.concatenate([cos, cos], axis=2).reshape(t_lat, QK_ROPE)
    sa32 = jnp.concatenate([zero, sin], axis=2).reshape(t_lat, QK_ROPE)
    sb32 = jnp.concatenate([-sin, zero], axis=2).reshape(t_lat, QK_ROPE)

    def widen(tab, fill):
        left = jnp.full((t_lat, ROPE_LANE0), fill, F32)
        right = jnp.full((t_lat, HEAD_PAD - ROPE_LANE0 - QK_ROPE), fill, F32)
        lat = jnp.concatenate([left, tab, right], axis=1)
        return jnp.concatenate([lat, jnp.full((t_ctx, HEAD_PAD), fill, F32)], axis=0)

    return widen(cos32, 1.0), widen(sa32, 0.0), widen(sb32, 0.0)


def _cols_from_shards(s):
    return jnp.transpose(s, (1, 0, 2)).reshape(s.shape[1], -1)


def _cols_to_shards(w):
    k, n = w.shape
    return jnp.transpose(w.reshape(k, N_SHARD, n // N_SHARD), (1, 0, 2))


def kernel(x, c, ctx, c_ctx, w_mod, b_mod, w_in, q_norm_g, w_uq, kv_norm_g, w_ukv, conv_w, w_out, w_mlp1, w_mlp2, final_norm_g, loss_target, m_c_ctx, m_w_mod, m_b_mod, m_w_in, m_q_norm_g, m_w_uq, m_kv_norm_g, m_w_ukv, m_conv_w, m_w_out, m_w_mlp1, m_w_mlp2, m_final_norm_g, v_c_ctx, v_w_mod, v_b_mod, v_w_in, v_q_norm_g, v_w_uq, v_kv_norm_g, v_w_ukv, v_conv_w, v_w_out, v_w_mlp1, v_w_mlp2, v_final_norm_g):
    t_lat, t_ctx = x.shape[1], ctx.shape[1]
    assert t_ctx == TOK_TILE and t_lat % TOK_TILE == 0 and t_lat % GRID_W == 0
    mx, my, mc = _pos()
    me = 4 * mx + 2 * my + mc
    j = 2 * mx + my
    ncol = w_mod.shape[2]
    x2, ctx2, tgt = x[0], ctx[0], loss_target[0]
    cctx_row = c_ctx.reshape(1, D_MODEL)

    b_sh = lax.dynamic_slice(b_mod, (0, j * ncol), (1, ncol))
    cw_pad = jnp.zeros((8, 128), F32).at[0:3, :].set(conv_w[0])
    c8, m_all = _mod_exchange(c, cctx_row, w_mod[0], b_sh, cw_pad)
    m_rows = jnp.transpose(m_all[:, 0:72, :], (1, 0, 2)).reshape(72, N_SHARD * ncol)
    mvec = lax.dynamic_slice(m_rows, (8 * me, 0), (1, 6 * D_MODEL)).reshape(6, D_MODEL)
    mctx = m_rows[64].reshape(6, D_MODEL)
    zeros6 = jnp.zeros((6, D_MODEL), F32)
    mod_a = jnp.stack([jnp.concatenate([mvec[0:2], zeros6], axis=0), jnp.concatenate([mctx[0:2], zeros6], axis=0)])
    mod_b = jnp.concatenate([mvec[2:6], jnp.zeros((4, D_MODEL), F32)], axis=0)
    cw_full = jnp.transpose(m_all[:, 72:80, 0:128], (1, 0, 2)).reshape(8, CONV_W)

    g_in, g_uq, g_ukv, g_out, g_m1, g_m2 = _weight_prep(
        (w_in[0], w_uq[0], w_ukv[0], w_out[0], w_mlp1[0], w_mlp2[0]), 3)
    w_in_f = _cols_from_shards(g_in)
    zc = lambda n: jnp.zeros((D_MODEL, n), BF16)
    w_in_p = jnp.concatenate([w_in_f[:, 0:384], zc(64), w_in_f[:, 384:416], zc(32), w_in_f[:, 416:]], axis=1)
    w_uq_f = _cols_from_shards(g_uq).reshape(Q_RANK, N_HEADS, QK_DIM)
    w_uq_p = jnp.pad(w_uq_f, ((0, 0), (0, 0), (0, HEAD_PAD - QK_DIM))).reshape(Q_RANK, N_HEADS * HEAD_PAD)
    w_ukv_f = _cols_from_shards(g_ukv).reshape(KV_RANK, N_HEADS, QK_NOPE + V_DIM)
    padh = lambda a: jnp.pad(a, ((0, 0), (0, 0), (0, HEAD_PAD - a.shape[2]))).reshape(KV_RANK, N_HEADS * HEAD_PAD)
    w_ukv_p = jnp.concatenate([padh(w_ukv_f[:, :, :QK_NOPE]), padh(w_ukv_f[:, :, QK_NOPE:])], axis=1)
    cos_t, sin_a, sin_b = _rope_tables(t_lat, t_ctx)
    gf_row = final_norm_g.reshape(1, D_MODEL)
    c_idx = mc.reshape(1).astype(jnp.int32)
    j_idx = j.reshape(1).astype(jnp.int32)

    z, q, k, v, kt = _inproj_fwd(x2, ctx2, mod_a, w_in_p, q_norm_g, kv_norm_g, w_uq_p, w_ukv_p, cos_t, sin_a, sin_b)
    o, lse, g_out, w1, g_m2 = _attn_fwd(q, k, v, t_lat, (g_out, g_m1, g_m2))
    w_out_f = g_out.reshape(D_MODEL, D_MODEL)
    wo_attn = jnp.pad(w_out_f[0:512].reshape(N_HEADS, V_DIM, D_MODEL), ((0, 0), (0, HEAD_PAD - V_DIM), (0, 0)))
    wo_conv = w_out_f[512:]
    w2 = g_m2.reshape(D_FF, D_MODEL)
    r, da, h2, dy2, dx1, y1, yv, conv, acc_mlp = _mlp_fwdbwd(o, z, x2, tgt, mod_b, gf_row, cw_full, wo_attn, wo_conv,
                                                               w1, w2)
    dy1, do, dgb, dyv, acc_mid = _mid_bwd(dx1, y1, z, yv, mod_b, wo_attn, wo_conv)
    d_w1 = _wgrad(h2, da, "wgrad_mlp1", D_MODEL, FF_CHUNK)
    d_w2 = _wgrad(r, dy2, "wgrad_mlp2", FF_CHUNK, D_MODEL)
    d_wout = _wgrad_out(o, conv, dy1).reshape(N_SHARD, D_MODEL // N_SHARD, D_MODEL)
    big_grads, big_names = (d_w1, d_w2, d_wout), ("mlp1", "mlp2", "out")
    big_got = _rs_sibling(big_grads, "rs_sibling_big")
    big_parts = [_add_pairs(a, g, c_idx, "rs_add_pairs_" + n) for a, g, n in zip(big_grads, big_got, big_names)]
    dqt, dk, dv, *big_recv = _attn_bwd(q, k, v, kt, o, do, lse, t_lat, big_parts)
    big_halves = [_add_chips(p, g, j_idx, "rs_add_chips_" + n) for p, g, n in zip(big_parts, big_recv, big_names)]
    g_w1, g_w2, g_wout = _rs_join(big_halves, "rs_join_big")
    gx, d_win, d_wuq, d_wukv, acc_in = _inproj_bwd(x2, ctx2, mod_a, z, dyv, dgb, dx1, dqt, dk, dv, cos_t, sin_a, sin_b,
                                                   w_in_p, w_uq_p, w_ukv_p, q_norm_g, kv_norm_g, cw_full)

    pad_row = lambda a: jnp.pad(a, ((0, 0), (0, D_MODEL - a.shape[1])))
    sv = jnp.concatenate([
        acc_in[0:2], acc_mid[0:1], acc_mlp[2:4], acc_mlp[1:2],
        acc_in[2:4], acc_mlp[0:1], acc_in[4:5], acc_in[5:6], acc_in[6:9], acc_mlp[4:5],
        jnp.zeros((1, D_MODEL), F32)], axis=0)
    all_sv, red, o_cc, o_b, o_q, o_k, o_gf = _small_exchange(
        sv, w_mod[0], cctx_row, m_c_ctx.reshape(1, D_MODEL), v_c_ctx.reshape(1, D_MODEL),
        b_mod.reshape(6, D_MODEL), m_b_mod.reshape(6, D_MODEL), v_b_mod.reshape(6, D_MODEL),
        q_norm_g, m_q_norm_g, v_q_norm_g, kv_norm_g, m_kv_norm_g, v_kv_norm_g,
        gf_row, m_final_norm_g.reshape(1, D_MODEL), v_final_norm_g.reshape(1, D_MODEL))
    loss = red[14, 0]

    c9 = jnp.concatenate([c8[0::8], jnp.zeros((7, D_MODEL), F32)], axis=0)
    s_t = jnp.transpose(c9 * jax.nn.sigmoid(c9))
    dm_ex = all_sv[:, 0:6, :].reshape(8, 6 * D_MODEL)
    dm_ctx = jnp.concatenate([red[6:8].reshape(1, 2 * D_MODEL), jnp.zeros((1, 4 * D_MODEL), F32)], axis=1)
    dm16 = jnp.concatenate([dm_ex, dm_ctx, jnp.zeros((7, 6 * D_MODEL), F32)], axis=0)
    dm_sh = lax.dynamic_slice(dm16, (0, j * ncol), (16, ncol))
    g_wmod, d_wmod, nm_wmod, nv_wmod = _wmod_update(s_t, dm_sh, w_mod[0], m_w_mod[0], v_w_mod[0])

    g_cw = lax.dynamic_slice(red[11:14, 0:CONV_W], (0, j * 128), (3, 128))
    d_cw, nm_cw, nv_cw = _adamw_call(conv_w[0], g_cw, m_conv_w[0], v_conv_w[0], "adamw_conv")

    d_win_f = jnp.concatenate([d_win[:, 0:384], d_win[:, 448:480], d_win[:, 512:]], axis=1)
    d_wuq_f = d_wuq.reshape(Q_RANK, N_HEADS, HEAD_PAD)[:, :, 0:QK_DIM].reshape(Q_RANK, N_HEADS * QK_DIM)
    d_wukv3 = d_wukv.reshape(KV_RANK, 2, N_HEADS, HEAD_PAD)
    d_wukv_f = jnp.concatenate([d_wukv3[:, 0, :, 0:QK_NOPE], d_wukv3[:, 1, :, 0:V_DIM]], axis=2).reshape(KV_RANK, -1)
    rest = tuple(_cols_to_shards(a).astype(BF16) for a in (d_win_f, d_wuq_f, d_wukv_f))
    rest_names = ("in", "uq", "ukv")
    rest_got = _rs_sibling(rest, "rs_sibling_rest")
    rest_parts = [_add_pairs(a, g, c_idx, "rs_add_pairs_" + n) for a, g, n in zip(rest, rest_got, rest_names)]
    rest_recv = _rs_chips(rest_parts, "rs_chips_rest")
    rest_halves = [_add_chips(p, g, j_idx, "rs_add_chips_" + n) for p, g, n in zip(rest_parts, rest_recv, rest_names)]
    g_win, g_wuq, g_wukv = _rs_join(rest_halves, "rs_join_rest")
    upd = {}
    for name, w_, g_, m_, v_ in (("in", w_in, g_win, m_w_in, v_w_in), ("uq", w_uq, g_wuq, m_w_uq, v_w_uq),
                                 ("ukv", w_ukv, g_wukv, m_w_ukv, v_w_ukv), ("out", w_out, g_wout, m_w_out, v_w_out),
                                 ("mlp1", w_mlp1, g_w1, m_w_mlp1, v_w_mlp1), ("mlp2", w_mlp2, g_w2, m_w_mlp2, v_w_mlp2)):
        upd[name] = _adamw_call(w_[0], g_, m_[0], v_[0], "adamw_" + name)

    def four(o4, shape):
        return [o4[r].reshape(shape) for r in range(4)]

    cc4 = four(o_cc, (D_MODEL,))
    b4 = [o_b[r].reshape(1, 6 * D_MODEL) for r in range(4)]
    q4 = four(o_q, (1, Q_RANK))
    k4 = four(o_k, (1, KV_RANK))
    gf4 = four(o_gf, (D_MODEL,))
    big = {"in": g_win, "uq": g_wuq, "ukv": g_wukv, "out": g_wout, "mlp1": g_w1, "mlp2": g_w2}

    def leaf(idx):
        wm = (g_wmod, d_wmod, nm_wmod, nv_wmod)[idx]
        cwv = (g_cw, d_cw, nm_cw, nv_cw)[idx]
        bigv = {n: (big[n] if idx == 0 else upd[n][idx - 1]) for n in big}
        return [cc4[idx], wm[None], b4[idx], bigv["in"][None], q4[idx], bigv["uq"][None], k4[idx], bigv["ukv"][None],
                cwv[None], bigv["out"][None], bigv["mlp1"][None], bigv["mlp2"][None], gf4[idx]]

    return (loss, gx[None], *leaf(0), *leaf(1), *leaf(2), *leaf(3))
```

```python
import functools
import math

import jax
import jax.numpy as jnp
from jax import lax
from jax.experimental import pallas as pl
from jax.experimental.pallas import tpu as pltpu

F32 = jnp.float32
BF16 = jnp.bfloat16
MESH = pl.DeviceIdType.MESH
HIGHEST = lax.Precision.HIGHEST

D_MODEL = 1024
N_HEADS = 8
QK_NOPE = 64
QK_ROPE = 32
QK_DIM = QK_NOPE + QK_ROPE
V_DIM = 64
Q_RANK = 256
KV_RANK = 128
CONV_W = 512
D_FF = 4096
GRID_W = 64
ROPE_THETA = 10000.0
EPS = 1e-6
ATTN_SCALE = 1.0 / math.sqrt(QK_DIM)
HEAD_PAD = 128
Z_COLS = 2048
ROPE_LANE0 = QK_NOPE
N_SHARD = 4
TOK_TILE = 256
FF_CHUNK = 1024
KEY_CHUNK = 1024
ATTN_HEADS_PER_STEP = 4

ADAM_LR = 0.001
ADAM_B1 = 0.9
ADAM_B2 = 0.999
ADAM_EPS = 1e-08
ADAM_WD = 0.01
ADAM_STEP = 10

LOG2E = 1.4426950408889634

VMEM_LIMIT = 56 * 1024 * 1024
STAGE_VMEM_LIMIT = 32 * 1024 * 1024


def _pos():
    return lax.axis_index("x"), lax.axis_index("y"), lax.axis_index("c")


def _dot(a, b):
    return jnp.dot(a, b, preferred_element_type=F32)


def _dot_nt(a, b):
    return lax.dot_general(a, b, (((1,), (1,)), ((), ())), preferred_element_type=F32)


def _dot_tn(a, b):
    return lax.dot_general(a, b, (((0,), (0,)), ((), ())), preferred_element_type=F32)


def _rope(v, cos, sa, sb):
    return v * cos + pltpu.roll(v, 8, 1) * sa + pltpu.roll(v, HEAD_PAD - 8, 1) * sb


def _unrope(g, cos, sa, sb):
    return g * cos + pltpu.roll(g * sa, HEAD_PAD - 8, 1) + pltpu.roll(g * sb, 8, 1)


def _sigmoid(v):
    return 1.0 / (1.0 + jnp.exp(-v))


def _adamw(w, g, m, v):
    m = ADAM_B1 * m + (1.0 - ADAM_B1) * g
    v = ADAM_B2 * v + (1.0 - ADAM_B2) * (g * g)
    m_hat = m / (1.0 - ADAM_B1 ** ADAM_STEP)
    v_hat = v / (1.0 - ADAM_B2 ** ADAM_STEP)
    delta = -ADAM_LR * (m_hat / (jnp.sqrt(v_hat) + ADAM_EPS) + ADAM_WD * w)
    return delta, m, v


def _shift_rows(u, prev_row, next_row):
    n = u.shape[0]
    rows = lax.broadcasted_iota(jnp.int32, u.shape, 0)
    um1 = jnp.where(rows == 0, prev_row, pltpu.roll(u, 1, 0))
    up1 = jnp.where(rows == n - 1, next_row, pltpu.roll(u, n - 1, 0))
    return um1, up1


def _const_spec(shape):
    nd = len(shape)
    return pl.BlockSpec(shape, lambda *_: (0,) * nd)


def _resident_spec(shape):
    nd = len(shape)
    return pl.BlockSpec(shape, lambda *_: (0,) * nd, pipeline_mode=pl.Buffered(1))


def _peer(r, x, y, c):
    px = 1 - x if r & 4 else x
    py = 1 - y if r & 2 else y
    pc = 1 - c if r & 1 else c
    return (px, py, pc)


def _mod_exchange(c_row, cctx_row, w_mod_sh, b_sh, cw_sh):
    ncol = w_mod_sh.shape[1]

    def body(c_ref, cctx_ref, w_ref, b_ref, cw_ref, c8_ref, m_ref, mine_ref, msh_ref, ssem, rsem, ssem2, rsem2):
        x, y, c = _pos()
        me = 4 * x + 2 * y + c
        j = 2 * x + y
        mine_ref[...] = jnp.zeros(mine_ref.shape, F32)
        mine_ref[0:1, :] = c_ref[...]
        my_rows = pl.ds(pl.multiple_of(8 * me, 8), 8)
        sends = []
        for r in range(1, 8):
            cp = pltpu.make_async_remote_copy(
                src_ref=mine_ref, dst_ref=c8_ref.at[my_rows], send_sem=ssem.at[r - 1], recv_sem=rsem.at[r - 1],
                device_id=_peer(r, x, y, c), device_id_type=MESH)
            cp.start()
            sends.append(cp)
        for cp in sends:
            cp.wait()
        c8_ref[my_rows, :] = mine_ref[...]
        c8_ref[64:72, :] = jnp.zeros((8, D_MODEL), F32)
        c8_ref[64:65, :] = cctx_ref[...]
        cv = c8_ref[...]
        s = cv * _sigmoid(cv)
        m = jnp.dot(s, w_ref[...], precision=HIGHEST, preferred_element_type=F32) + b_ref[...]
        msh_ref[0:64, :] = m[0:64, :]
        msh_ref[64:72, :] = jnp.zeros((8, ncol), F32)
        msh_ref[64:65, :] = m[64:65, :]
        msh_ref[65:68, 0:128] = cw_ref[0:3, :]
        m_ref[j, 0:8, :] = msh_ref[my_rows, :]
        m_ref[j, 8:16, :] = msh_ref[64:72, :]
        sends2 = []
        for k, (px, py) in enumerate(_chips(x, y)):
            theirs = pl.ds(pl.multiple_of(8 * (4 * px + 2 * py + c), 8), 8)
            for half, src in enumerate((msh_ref.at[theirs], msh_ref.at[64:72])):
                cp = pltpu.make_async_remote_copy(
                    src_ref=src, dst_ref=m_ref.at[j, 8 * half:8 * half + 8], send_sem=ssem2.at[2 * k + half],
                    recv_sem=rsem2.at[2 * k + half], device_id=(px, py, c), device_id_type=MESH)
                cp.start()
                sends2.append(cp)
        for cp in sends2:
            cp.wait()

    vm = pl.BlockSpec(memory_space=pltpu.VMEM)
    return pl.pallas_call(
        body, name="mod_exchange",
        out_shape=(jax.ShapeDtypeStruct((72, D_MODEL), F32), jax.ShapeDtypeStruct((N_SHARD, 16, ncol), F32)),
        in_specs=[vm] * 5, out_specs=(vm, vm),
        scratch_shapes=[pltpu.VMEM((8, D_MODEL), F32), pltpu.VMEM((72, ncol), F32),
                        pltpu.SemaphoreType.DMA((7,)), pltpu.SemaphoreType.DMA((7,)),
                        pltpu.SemaphoreType.DMA((6,)), pltpu.SemaphoreType.DMA((6,))],
        compiler_params=pltpu.CompilerParams(vmem_limit_bytes=VMEM_LIMIT),
    )(c_row, cctx_row, w_mod_sh, b_sh, cw_sh)


def _chips(x, y):
    return [(1 - x, y), (x, 1 - y), (1 - x, 1 - y)]


def _halves(ref, c, align):
    hr = ref.shape[-2] // 2
    return (pl.ds(pl.multiple_of(c * hr, align), hr), pl.ds(pl.multiple_of((1 - c) * hr, align), hr))


class _ShardGather:
    def __init__(self, refs, ssem, rsem, fsend, frecv):
        self.refs, self.sems = refs, (ssem, rsem, fsend, frecv)
        self.x, self.y, self.c = _pos()
        self.j = 2 * self.x + self.y

    def _ici(self, a, k, slot):
        g = self.refs[a]
        ssem, rsem, _, _ = self.sems
        mine, _ = _halves(g, self.c, 16)
        px, py = _chips(self.x, self.y)[k]
        return pltpu.make_async_remote_copy(
            src_ref=g.at[self.j, mine], dst_ref=g.at[slot, mine], send_sem=ssem.at[3 * a + k],
            recv_sem=rsem.at[3 * a + k], device_id=(px, py, self.c), device_id_type=MESH)

    def _d2d(self, a, k, to_other_half):
        g = self.refs[a]
        _, _, fsend, frecv = self.sems
        mine, theirs = _halves(g, self.c, 16)
        px, py = _chips(self.x, self.y)[k]
        jk = 2 * px + py
        return pltpu.make_async_remote_copy(
            src_ref=g.at[jk, mine], dst_ref=g.at[jk, theirs if to_other_half else mine],
            send_sem=fsend.at[3 * a + k], recv_sem=frecv.at[3 * a + k],
            device_id=(self.x, self.y, 1 - self.c), device_id_type=MESH)

    def start(self):
        for a in range(len(self.refs)):
            for k in range(3):
                self._ici(a, k, self.j).start()

    def forward(self):
        for a in range(len(self.refs)):
            for k, (px, py) in enumerate(_chips(self.x, self.y)):
                self._ici(a, k, 2 * px + py).wait_recv()
                self._d2d(a, k, False).start()

    def finish(self):
        for a in range(len(self.refs)):
            for k in range(3):
                self._d2d(a, k, True).wait()
                self._ici(a, k, self.j).wait_send()


def _gather_sems(n_arrays):
    return [pltpu.SemaphoreType.DMA((3 * n_arrays,)) for _ in range(4)]


def _weight_prep(srcs, n_gather):
    n = len(srcs)
    n_split = 4

    def body(*refs):
        ins, outs, f32s, bfs = refs[:n], refs[n:2 * n], refs[2 * n:3 * n], refs[3 * n:4 * n]
        lsem_in, lsem_out = refs[4 * n], refs[4 * n + 1]
        x, y, c = _pos()
        j = 2 * x + y

        def pieces(rows):
            step = rows // n_split
            return [pl.ds(q * step, step) for q in range(n_split)]

        for t in range(n):
            for sl in pieces(ins[t].shape[0]):
                pltpu.make_async_copy(ins[t].at[sl], f32s[t].at[sl], lsem_in.at[t]).start()
        gather = _ShardGather(outs[:n_gather], *refs[4 * n + 2:])
        for t in range(n):
            pltpu.make_async_copy(ins[t], f32s[t], lsem_in.at[t]).wait()
            bfs[t][...] = f32s[t][...].astype(BF16)
            for sl in pieces(ins[t].shape[0]):
                pltpu.make_async_copy(bfs[t].at[sl], outs[t].at[j, sl], lsem_out.at[t]).start()
            if t == n_gather - 1:
                for u in range(n_gather):
                    pltpu.make_async_copy(bfs[u], outs[u].at[j], lsem_out.at[u]).wait()
                gather.start()
        gather.forward()
        gather.finish()
        for t in range(n_gather, n):
            pltpu.make_async_copy(bfs[t], outs[t].at[j], lsem_out.at[t]).wait()

    hbm = pl.BlockSpec(memory_space=pl.ANY)
    return pl.pallas_call(
        body, name="weight_prep",
        out_shape=tuple(jax.ShapeDtypeStruct((N_SHARD,) + a.shape, BF16) for a in srcs),
        in_specs=[hbm] * n, out_specs=(hbm,) * n,
        scratch_shapes=[pltpu.VMEM(a.shape, F32) for a in srcs] + [pltpu.VMEM(a.shape, BF16) for a in srcs]
        + [pltpu.SemaphoreType.DMA((n,)), pltpu.SemaphoreType.DMA((n,))] + _gather_sems(n_gather),
        compiler_params=pltpu.CompilerParams(vmem_limit_bytes=STAGE_VMEM_LIMIT),
    )(*srcs)


def _rs_sibling(arrs, name):
    n = len(arrs)

    def body(*refs):
        g_refs, got_refs, ssem, rsem = refs[:n], refs[n:2 * n], refs[2 * n], refs[2 * n + 1]
        x, y, c = _pos()
        for a in range(n):
            _, theirs = _halves(g_refs[a], c, 16)
            for s in range(N_SHARD):
                pltpu.make_async_remote_copy(
                    src_ref=g_refs[a].at[s, theirs], dst_ref=got_refs[a].at[s], send_sem=ssem.at[a],
                    recv_sem=rsem.at[a], device_id=(x, y, 1 - c), device_id_type=MESH).start()
        for a in range(n):
            _, theirs = _halves(g_refs[a], c, 16)
            pltpu.make_async_remote_copy(
                src_ref=g_refs[a].at[:, theirs], dst_ref=got_refs[a], send_sem=ssem.at[a], recv_sem=rsem.at[a],
                device_id=(x, y, 1 - c), device_id_type=MESH).wait()

    hbm = pl.BlockSpec(memory_space=pl.ANY)
    return pl.pallas_call(
        body, name=name,
        out_shape=tuple(jax.ShapeDtypeStruct((N_SHARD, a.shape[1] // 2, a.shape[2]), BF16) for a in arrs),
        in_specs=[hbm] * n, out_specs=(hbm,) * n,
        scratch_shapes=[pltpu.SemaphoreType.DMA((n,)), pltpu.SemaphoreType.DMA((n,))],
    )(*arrs)


class _ChipScatter:
    def __init__(self, parts, gots, ssem, rsem):
        self.parts, self.gots, self.ssem, self.rsem = parts, gots, ssem, rsem
        self.x, self.y, self.c = _pos()

    def _copy(self, a, k):
        px, py = _chips(self.x, self.y)[k]
        return pltpu.make_async_remote_copy(
            src_ref=self.parts[a].at[2 * px + py], dst_ref=self.gots[a].at[k], send_sem=self.ssem.at[3 * a + k],
            recv_sem=self.rsem.at[3 * a + k], device_id=(px, py, self.c), device_id_type=MESH)

    def start(self):
        for a in range(len(self.parts)):
            for k in range(3):
                self._copy(a, k).start()

    def finish(self):
        for a in range(len(self.parts)):
            for k in range(3):
                self._copy(a, k).wait()


def _rs_chips(parts, name):
    n = len(parts)

    def body(*refs):
        scatter = _ChipScatter(refs[:n], refs[n:2 * n], refs[2 * n], refs[2 * n + 1])
        scatter.start()
        scatter.finish()

    hbm = pl.BlockSpec(memory_space=pl.ANY)
    return pl.pallas_call(
        body, name=name, out_shape=tuple(jax.ShapeDtypeStruct((3,) + p.shape[1:], BF16) for p in parts),
        in_specs=[hbm] * n, out_specs=(hbm,) * n,
        scratch_shapes=[pltpu.SemaphoreType.DMA((3 * n,)), pltpu.SemaphoreType.DMA((3 * n,))],
    )(*parts)


def _rs_join(halves, name):
    n = len(halves)

    def body(*refs):
        h_refs, f_refs, stages = refs[:n], refs[n:2 * n], refs[2 * n:3 * n]
        lsem_in, lsem_out, ssem, rsem = refs[3 * n:]
        x, y, c = _pos()
        remote = []
        for a in range(n):
            mine, _ = _halves(f_refs[a], c, 8)
            cp = pltpu.make_async_remote_copy(
                src_ref=h_refs[a], dst_ref=f_refs[a].at[mine], send_sem=ssem.at[a], recv_sem=rsem.at[a],
                device_id=(x, y, 1 - c), device_id_type=MESH)
            cp.start()
            remote.append(cp)
            pltpu.make_async_copy(h_refs[a], stages[a], lsem_in.at[a]).start()
        local = []
        for a in range(n):
            mine, _ = _halves(f_refs[a], c, 8)
            pltpu.make_async_copy(h_refs[a], stages[a], lsem_in.at[a]).wait()
            cp = pltpu.make_async_copy(stages[a], f_refs[a].at[mine], lsem_out.at[a])
            cp.start()
            local.append(cp)
        for cp in remote + local:
            cp.wait()

    hbm = pl.BlockSpec(memory_space=pl.ANY)
    return pl.pallas_call(
        body, name=name,
        out_shape=tuple(jax.ShapeDtypeStruct((2 * h.shape[0], h.shape[1]), F32) for h in halves),
        in_specs=[hbm] * n, out_specs=(hbm,) * n,
        scratch_shapes=[pltpu.VMEM(h.shape, F32) for h in halves]
        + [pltpu.SemaphoreType.DMA((n,)) for _ in range(4)],
        compiler_params=pltpu.CompilerParams(vmem_limit_bytes=STAGE_VMEM_LIMIT),
    )(*halves)


def _row_block(rows):
    return max(b for b in range(16, 513, 16) if rows % b == 0)


def _add_pairs(arr, got, c_idx, name):
    hr, cols = got.shape[1], got.shape[2]
    rb = _row_block(hr)
    nb = hr // rb

    def body(c_ref, a_ref, b_ref, o_ref):
        o_ref[...] = (a_ref[...].astype(F32) + b_ref[...].astype(F32)).astype(BF16)

    spec = pl.BlockSpec((1, rb, cols), lambda s, r, c_ref: (s, r, 0))
    grid_spec = pltpu.PrefetchScalarGridSpec(
        num_scalar_prefetch=1, grid=(N_SHARD, nb),
        in_specs=[pl.BlockSpec((1, rb, cols), lambda s, r, c_ref: (s, c_ref[0] * nb + r, 0)), spec],
        out_specs=spec)
    return pl.pallas_call(
        body, name=name, grid_spec=grid_spec, out_shape=jax.ShapeDtypeStruct(got.shape, BF16),
    )(c_idx, arr, got)


def _add_chips(part, got, j_idx, name):
    hr, cols = got.shape[1], got.shape[2]
    rb = _row_block(hr)

    def body(j_ref, p_ref, g_ref, o_ref):
        acc = p_ref[0].astype(F32)
        for k in range(3):
            acc = acc + g_ref[k].astype(F32)
        o_ref[...] = acc

    grid_spec = pltpu.PrefetchScalarGridSpec(
        num_scalar_prefetch=1, grid=(hr // rb,),
        in_specs=[pl.BlockSpec((1, rb, cols), lambda r, j_ref: (j_ref[0], r, 0)),
                  pl.BlockSpec((3, rb, cols), lambda r, j_ref: (0, r, 0))],
        out_specs=pl.BlockSpec((rb, cols), lambda r, j_ref: (r, 0)))
    return pl.pallas_call(
        body, name=name, grid_spec=grid_spec, out_shape=jax.ShapeDtypeStruct((hr, cols), F32),
    )(j_idx, part, got)


def _small_exchange(sv, w_mod_sh, cctx, m_cctx, v_cctx, bmod, m_bmod, v_bmod, qg, m_qg, v_qg, kvg, m_kvg, v_kvg,
                    gf, m_gf, v_gf):
    ncol = w_mod_sh.shape[1]

    def body(sv_ref, w_ref, cctx_ref, mcc_ref, vcc_ref, b_ref, mb_ref, vb_ref, qg_ref, mq_ref, vq_ref,
             kg_ref, mk_ref, vk_ref, gf_ref, mgf_ref, vgf_ref,
             all_ref, red_ref, occ_ref, ob_ref, oq_ref, ok_ref, ogf_ref,
             vec_ref, part_ref, ssem, rsem, ssem2, rsem2):
        x, y, c = _pos()
        me = 4 * x + 2 * y + c
        j = 2 * x + y
        sends = []
        for r in range(1, 8):
            cp = pltpu.make_async_remote_copy(
                src_ref=sv_ref, dst_ref=all_ref.at[me], send_sem=ssem.at[r - 1], recv_sem=rsem.at[r - 1],
                device_id=_peer(r, x, y, c), device_id_type=MESH)
            cp.start()
            sends.append(cp)
        for cp in sends:
            cp.wait()
        all_ref[me] = sv_ref[...]
        red = all_ref[0]
        for d in range(1, 8):
            red = red + all_ref[d]
        red_ref[...] = red
        vec_ref[...] = jnp.zeros(vec_ref.shape, F32)

        @pl.when(j == 0)
        def _():
            vec_ref[0:1, 0:1024] = red[6:7, :]
            vec_ref[0:1, 1024:1536] = red[7:8, 0:512]

        @pl.when(j == 1)
        def _():
            vec_ref[0:1, 0:512] = red[7:8, 512:1024]

        part = lax.dot_general(vec_ref[...], w_ref[...], (((1,), (1,)), ((), ())), precision=HIGHEST,
                               preferred_element_type=F32)
        part_ref[j] = part
        sends2 = []
        for k, r in enumerate((4, 2, 6)):
            cp = pltpu.make_async_remote_copy(
                src_ref=part_ref.at[j], dst_ref=part_ref.at[j], send_sem=ssem2.at[k], recv_sem=rsem2.at[k],
                device_id=_peer(r, x, y, c), device_id_type=MESH)
            cp.start()
            sends2.append(cp)
        for cp in sends2:
            cp.wait()
        tot = part_ref[0]
        for s in range(1, N_SHARD):
            tot = tot + part_ref[s]
        cc = cctx_ref[...]
        sg = _sigmoid(cc)
        g_cc = tot[0:1, :] * (sg * (1.0 + cc * (1.0 - sg)))
        d_, m_, v_ = _adamw(cc, g_cc, mcc_ref[...], vcc_ref[...])
        occ_ref[0:1, :] = g_cc
        occ_ref[1:2, :] = d_
        occ_ref[2:3, :] = m_
        occ_ref[3:4, :] = v_
        occ_ref[4:8, :] = jnp.zeros((4, D_MODEL), F32)
        g_b = red[0:6, :]
        pad = jnp.concatenate([red[6:8, :], jnp.zeros((4, D_MODEL), F32)], axis=0)
        g_b = g_b + pad
        d_, m_, v_ = _adamw(b_ref[...], g_b, mb_ref[...], vb_ref[...])
        ob_ref[0] = g_b
        ob_ref[1] = d_
        ob_ref[2] = m_
        ob_ref[3] = v_
        g_q = red[9:10, 0:Q_RANK]
        d_, m_, v_ = _adamw(qg_ref[...], g_q, mq_ref[...], vq_ref[...])
        oq_ref[0:1, :] = g_q
        oq_ref[1:2, :] = d_
        oq_ref[2:3, :] = m_
        oq_ref[3:4, :] = v_
        oq_ref[4:8, :] = jnp.zeros((4, Q_RANK), F32)
        g_k = red[10:11, 0:KV_RANK]
        d_, m_, v_ = _adamw(kg_ref[...], g_k, mk_ref[...], vk_ref[...])
        ok_ref[0:1, :] = g_k
        ok_ref[1:2, :] = d_
        ok_ref[2:3, :] = m_
        ok_ref[3:4, :] = v_
        ok_ref[4:8, :] = jnp.zeros((4, KV_RANK), F32)
        g_f = red[8:9, :]
        d_, m_, v_ = _adamw(gf_ref[...], g_f, mgf_ref[...], vgf_ref[...])
        ogf_ref[0:1, :] = g_f
        ogf_ref[1:2, :] = d_
        ogf_ref[2:3, :] = m_
        ogf_ref[3:4, :] = v_
        ogf_ref[4:8, :] = jnp.zeros((4, D_MODEL), F32)

    vm = pl.BlockSpec(memory_space=pltpu.VMEM)
    out_shape = (
        jax.ShapeDtypeStruct((8, 16, D_MODEL), F32),
        jax.ShapeDtypeStruct((16, D_MODEL), F32),
        jax.ShapeDtypeStruct((8, D_MODEL), F32),
        jax.ShapeDtypeStruct((4, 6, D_MODEL), F32),
        jax.ShapeDtypeStruct((8, Q_RANK), F32),
        jax.ShapeDtypeStruct((8, KV_RANK), F32),
        jax.ShapeDtypeStruct((8, D_MODEL), F32),
    )
    return pl.pallas_call(
        body, name="small_exchange", out_shape=out_shape, in_specs=[vm] * 17, out_specs=tuple([vm] * 7),
        scratch_shapes=[pltpu.VMEM((8, ncol), F32), pltpu.VMEM((N_SHARD, 8, D_MODEL), F32),
                        pltpu.SemaphoreType.DMA((7,)), pltpu.SemaphoreType.DMA((7,)),
                        pltpu.SemaphoreType.DMA((3,)), pltpu.SemaphoreType.DMA((3,))],
        compiler_params=pltpu.CompilerParams(vmem_limit_bytes=VMEM_LIMIT),
    )(sv, w_mod_sh, cctx, m_cctx, v_cctx, bmod, m_bmod, v_bmod, qg, m_qg, v_qg, kvg, m_kvg, v_kvg, gf, m_gf, v_gf)


def _inproj_fwd(x2, ctx2, mod_a, w_in, qg, kvg, w_uq, w_ukv, cos_t, sin_a, sin_b):
    t_lat, t_ctx = x2.shape[0], ctx2.shape[0]
    tm = TOK_TILE
    n_lat = t_lat // tm
    n_all = n_lat + t_ctx // tm
    e_rows = t_lat + t_ctx

    def body(x_ref, ctx_ref, mod_ref, win_ref, qg_ref, kvg_ref, wuq_ref, wukv_ref, cos_ref, sa_ref, sb_ref,
             z_ref, q_ref, k_ref, v_ref, kt_ref):
        i = pl.program_id(0)
        xin = jnp.where(i < n_lat, x_ref[...], ctx_ref[...])
        xn = xin * lax.rsqrt(jnp.mean(xin * xin, axis=-1, keepdims=True) + EPS)
        h1 = (xn * (1.0 + mod_ref[0, 1:2, :]) + mod_ref[0, 0:1, :]).astype(BF16)
        z = _dot(h1, win_ref[...])
        z_ref[...] = z
        cos, sa, sb = cos_ref[...], sa_ref[...], sb_ref[...]
        cq = z[:, 0:Q_RANK]
        cqn = (cq * lax.rsqrt(jnp.mean(cq * cq, axis=-1, keepdims=True) + EPS) * qg_ref[...]).astype(BF16)
        q = _dot(cqn, wuq_ref[...])
        ckv = z[:, Q_RANK:Q_RANK + KV_RANK]
        ckvn = (ckv * lax.rsqrt(jnp.mean(ckv * ckv, axis=-1, keepdims=True) + EPS) * kvg_ref[...]).astype(BF16)
        kv = _dot(ckvn, wukv_ref[...])
        kr = _rope(z[:, Q_RANK + KV_RANK:Q_RANK + KV_RANK + HEAD_PAD], cos, sa, sb)
        for h in range(N_HEADS):
            lo = h * HEAD_PAD
            q_ref[h] = _rope(q[:, lo:lo + HEAD_PAD], cos, sa, sb).astype(BF16)
            kh = kv[:, lo:lo + HEAD_PAD] + kr
            k_ref[h] = kh.astype(BF16)
            kt_ref[h] = kh.T.astype(BF16)
            v_ref[h] = kv[:, N_HEADS * HEAD_PAD + lo:N_HEADS * HEAD_PAD + lo + HEAD_PAD].astype(BF16)

    row = lambda i: (i, 0)
    head_spec = pl.BlockSpec((N_HEADS, tm, HEAD_PAD), lambda i: (0, i, 0))
    head_shape = jax.ShapeDtypeStruct((N_HEADS, e_rows, HEAD_PAD), BF16)
    return pl.pallas_call(
        body, name="inproj_fwd", grid=(n_all,),
        out_shape=(jax.ShapeDtypeStruct((e_rows, Z_COLS), F32), head_shape, head_shape, head_shape,
                   jax.ShapeDtypeStruct((N_HEADS, HEAD_PAD, e_rows), BF16)),
        in_specs=[
            pl.BlockSpec((tm, D_MODEL), lambda i: (jnp.minimum(i, n_lat - 1), 0)),
            _const_spec((tm, D_MODEL)),
            pl.BlockSpec((1, 8, D_MODEL), lambda i: (i // n_lat, 0, 0)),
            _const_spec(w_in.shape), _const_spec(qg.shape), _const_spec(kvg.shape),
            _const_spec(w_uq.shape), _const_spec(w_ukv.shape),
            pl.BlockSpec((tm, HEAD_PAD), row), pl.BlockSpec((tm, HEAD_PAD), row), pl.BlockSpec((tm, HEAD_PAD), row),
        ],
        out_specs=(pl.BlockSpec((tm, Z_COLS), row), head_spec, head_spec, head_spec,
                   pl.BlockSpec((N_HEADS, HEAD_PAD, tm), lambda i: (0, 0, i))),
        compiler_params=pltpu.CompilerParams(vmem_limit_bytes=VMEM_LIMIT),
    )(x2, ctx2, mod_a, w_in, qg, kvg, w_uq, w_ukv, cos_t, sin_a, sin_b)


def _attn_fwd(q, k, v, t_lat, shard_arrays):
    e_rows = k.shape[1]
    tq = 256
    n_chunks = max(1, e_rows // KEY_CHUNK)
    bounds = [(ci * KEY_CHUNK, KEY_CHUNK if ci < n_chunks - 1 else e_rows - ci * KEY_CHUNK) for ci in range(n_chunks)]
    c2 = ATTN_SCALE * LOG2E

    hb = ATTN_HEADS_PER_STEP
    n_hb = N_HEADS // hb

    def body(q_ref, k_ref, v_ref, o_ref, lse_ref):
        qs = [q_ref[b] for b in range(hb)]
        m, l, acc = [None] * hb, [None] * hb, [None] * hb
        for lo, n in bounds:
            for b in range(hb):
                s = _dot_nt(qs[b], k_ref[b, lo:lo + n, :])
                mc = jnp.max(s, axis=-1, keepdims=True)
                m_new = mc if m[b] is None else jnp.maximum(m[b], mc)
                p = jnp.exp2((s - m_new) * c2)
                pv = _dot(p.astype(BF16), v_ref[b, lo:lo + n, :])
                ps = jnp.sum(p, axis=-1, keepdims=True)
                if m[b] is None:
                    l[b], acc[b] = ps, pv
                else:
                    alpha = jnp.exp2((m[b] - m_new) * c2)
                    l[b] = l[b] * alpha + ps
                    acc[b] = acc[b] * alpha + pv
                m[b] = m_new
        for b in range(hb):
            o_ref[b] = (acc[b] * (1.0 / l[b])).astype(BF16)
            lse = (m[b] * ATTN_SCALE + jnp.log(l[b])) * LOG2E
            lse_ref[b] = jnp.broadcast_to(lse, (tq, HEAD_PAD)).T[0:1, :]

    n_w = len(shard_arrays)
    n_q = t_lat // tq

    def body_with_gather(q_ref, k_ref, v_ref, *rest):
        o_ref, lse_ref = rest[n_w], rest[n_w + 1]
        gather = _ShardGather(rest[n_w + 2:2 * n_w + 2], *rest[2 * n_w + 2:])
        h, i = pl.program_id(0), pl.program_id(1)
        pl.when(jnp.logical_and(h == 0, i == 0))(gather.start)
        pl.when(jnp.logical_and(h == n_hb // 2, i == 0))(gather.forward)
        body(q_ref, k_ref, v_ref, o_ref, lse_ref)
        pl.when(jnp.logical_and(h == n_hb - 1, i == n_q - 1))(gather.finish)

    hbm = pl.BlockSpec(memory_space=pl.ANY)
    return pl.pallas_call(
        body_with_gather, name="attn_fwd", grid=(n_hb, n_q),
        out_shape=(jax.ShapeDtypeStruct((N_HEADS, t_lat, HEAD_PAD), BF16),
                   jax.ShapeDtypeStruct((N_HEADS, 1, t_lat), F32))
        + tuple(jax.ShapeDtypeStruct(a.shape, a.dtype) for a in shard_arrays),
        in_specs=[pl.BlockSpec((hb, tq, HEAD_PAD), lambda h, i: (h, i, 0)),
                  pl.BlockSpec((hb, e_rows, HEAD_PAD), lambda h, i: (h, 0, 0)),
                  pl.BlockSpec((hb, e_rows, HEAD_PAD), lambda h, i: (h, 0, 0))] + [hbm] * n_w,
        out_specs=(pl.BlockSpec((hb, tq, HEAD_PAD), lambda h, i: (h, i, 0)),
                   pl.BlockSpec((hb, 1, tq), lambda h, i: (h, 0, i))) + (hbm,) * n_w,
        input_output_aliases={3 + a: 2 + a for a in range(n_w)},
        scratch_shapes=_gather_sems(n_w),
        compiler_params=pltpu.CompilerParams(vmem_limit_bytes=VMEM_LIMIT),
    )(q, k, v, *shard_arrays)


def _attn_bwd(q, k, v, kt, o, do, lse_row, t_lat, parts):
    e_rows = k.shape[1]
    tq = 256
    n_p = len(parts)
    n_q = t_lat // tq

    def body(q_ref, k_ref, v_ref, kt_ref, o_ref, do_ref, lse_ref, *rest):
        dqt_ref, dk_ref, dv_ref = rest[n_p:n_p + 3]
        scatter = _ChipScatter(rest[:n_p], rest[n_p + 3:2 * n_p + 3], rest[2 * n_p + 3], rest[2 * n_p + 4])
        h, i = pl.program_id(0), pl.program_id(1)
        pl.when(jnp.logical_and(h == 0, i == 0))(scatter.start)
        qb, dob = q_ref[0], do_ref[0]
        prod = o_ref[0].astype(F32) * dob.astype(F32)
        delta = lax.dot_general(jnp.ones((8, HEAD_PAD), F32), prod, (((1,), (1,)), ((), ())), precision=HIGHEST,
                                preferred_element_type=F32)[0:1, :]
        pt = jnp.exp2(_dot_nt(k_ref[0], qb) * (ATTN_SCALE * LOG2E) - lse_ref[0])
        dpt = _dot_nt(v_ref[0], dob)
        dst = (pt * (dpt - delta) * ATTN_SCALE).astype(BF16)
        dv_c = _dot(pt.astype(BF16), dob)
        dk_c = _dot(dst, qb)
        dqt_ref[0] = _dot(kt_ref[0], dst)

        @pl.when(i == 0)
        def _():
            dk_ref[0] = dk_c
            dv_ref[0] = dv_c

        @pl.when(i > 0)
        def _():
            dk_ref[0] += dk_c
            dv_ref[0] += dv_c

        pl.when(jnp.logical_and(h == N_HEADS - 1, i == n_q - 1))(scatter.finish)

    hbm = pl.BlockSpec(memory_space=pl.ANY)
    qspec = pl.BlockSpec((1, tq, HEAD_PAD), lambda h, i: (h, i, 0))
    kspec = pl.BlockSpec((1, e_rows, HEAD_PAD), lambda h, i: (h, 0, 0))
    return pl.pallas_call(
        body, name="attn_bwd", grid=(N_HEADS, n_q),
        out_shape=(jax.ShapeDtypeStruct((N_HEADS, HEAD_PAD, t_lat), F32),
                   jax.ShapeDtypeStruct((N_HEADS, e_rows, HEAD_PAD), F32),
                   jax.ShapeDtypeStruct((N_HEADS, e_rows, HEAD_PAD), F32))
        + tuple(jax.ShapeDtypeStruct((3,) + p.shape[1:], BF16) for p in parts),
        in_specs=[qspec, kspec, kspec, pl.BlockSpec((1, HEAD_PAD, e_rows), lambda h, i: (h, 0, 0)), qspec, qspec,
                  pl.BlockSpec((1, 1, tq), lambda h, i: (h, 0, i))] + [hbm] * n_p,
        out_specs=(pl.BlockSpec((1, HEAD_PAD, tq), lambda h, i: (h, 0, i)), kspec, kspec) + (hbm,) * n_p,
        scratch_shapes=[pltpu.SemaphoreType.DMA((3 * n_p,)), pltpu.SemaphoreType.DMA((3 * n_p,))],
        compiler_params=pltpu.CompilerParams(vmem_limit_bytes=VMEM_LIMIT),
    )(q, k, v, kt, o, do, lse_row, *parts)


def _halo_specs(tm, col_block):
    per = tm // 8
    prev = pl.BlockSpec((8, CONV_W), lambda i: (jnp.maximum(i * per - 1, 0), col_block))
    nxt = pl.BlockSpec((8, CONV_W), lambda i: ((i + 1) * per, col_block))
    return prev, nxt


def _mlp_fwdbwd(o, z, x2, tgt, mod_b, gf, cw, wo_attn, wo_conv, w1, w2):
    t_lat = x2.shape[0]
    tm = TOK_TILE
    n_lat = t_lat // tm
    n_ff = D_FF // FF_CHUNK

    def body(o_ref, gb_ref, gc_ref, xi_ref, gcp_ref, xip_ref, gcn_ref, xin_ref, cw_ref, woa_ref, woc_ref,
             x_ref, t_ref, mod_ref, gf_ref, w1_ref, w2_ref,
             r_ref, da_ref, h2_ref, dy2_ref, dx1_ref, y1_ref, yv_ref, conv_ref, acc_ref, ra_ref):
        i = pl.program_id(0)

        @pl.when(i == 0)
        def _():
            acc_ref[...] = jnp.zeros(acc_ref.shape, F32)

        g1, sh2, sc2, g2 = mod_ref[0:1, :], mod_ref[1:2, :], mod_ref[2:3, :], mod_ref[3:4, :]
        u = gc_ref[...] * xi_ref[...]
        u_prev = jnp.where(i > 0, gcp_ref[7:8, :] * xip_ref[7:8, :], 0.0)
        u_next = jnp.where(i < n_lat - 1, gcn_ref[0:1, :] * xin_ref[0:1, :], 0.0)
        um1, up1 = _shift_rows(u, u_prev, u_next)
        yv = cw_ref[0:1, :] * um1 + cw_ref[1:2, :] * u + cw_ref[2:3, :] * up1
        yv_ref[...] = yv
        conv = (gb_ref[...] * yv).astype(BF16)
        conv_ref[...] = conv
        y1 = _dot(conv, woc_ref[...])
        for h in range(N_HEADS):
            y1 = y1 + _dot(o_ref[h], woa_ref[h])
        y1_ref[...] = y1
        x1 = x_ref[...] + g1 * y1
        rstd2 = lax.rsqrt(jnp.mean(x1 * x1, axis=-1, keepdims=True) + EPS)
        xn1 = x1 * rstd2
        h2 = (xn1 * (1.0 + sc2) + sh2).astype(BF16)
        h2_ref[...] = h2
        y2 = jnp.zeros((tm, D_MODEL), F32)
        for jj in range(n_ff):
            ra = jnp.maximum(_dot(h2, w1_ref[jj]), 0.0)
            ra_ref[jj] = ra
            r = (ra * ra).astype(BF16)
            r_ref[:, jj * FF_CHUNK:(jj + 1) * FF_CHUNK] = r
            y2 = y2 + _dot(r, w2_ref[jj * FF_CHUNK:(jj + 1) * FF_CHUNK, :])
        x2v = x1 + g2 * y2
        rstd3 = lax.rsqrt(jnp.mean(x2v * x2v, axis=-1, keepdims=True) + EPS)
        xn3 = x2v * rstd3
        gfv = gf_ref[...]
        diff = xn3 * gfv - t_ref[...]
        loss_t = 0.5 * jnp.sum(jnp.sum(diff * diff, axis=-1, keepdims=True), axis=0, keepdims=True) * (1.0 / D_MODEL)
        dy = diff * (1.0 / D_MODEL)
        dxn3 = dy * gfv
        dx2 = rstd3 * (dxn3 - xn3 * jnp.mean(dxn3 * xn3, axis=-1, keepdims=True))
        dy2 = (dx2 * g2).astype(BF16)
        dy2_ref[...] = dy2
        dh2 = jnp.zeros((tm, D_MODEL), F32)
        for jj in range(n_ff):
            dr = _dot_nt(dy2, w2_ref[jj * FF_CHUNK:(jj + 1) * FF_CHUNK, :])
            da = (2.0 * ra_ref[jj] * dr).astype(BF16)
            da_ref[:, jj * FF_CHUNK:(jj + 1) * FF_CHUNK] = da
            dh2 = dh2 + _dot_nt(da, w1_ref[jj])
        dxn1 = dh2 * (1.0 + sc2)
        dx1_ref[...] = dx2 + rstd2 * (dxn1 - xn1 * jnp.mean(dxn1 * xn1, axis=-1, keepdims=True))
        acc_ref[0:1, :] += jnp.sum(dy * xn3, axis=0, keepdims=True)
        acc_ref[1:2, :] += jnp.sum(dx2 * y2, axis=0, keepdims=True)
        acc_ref[2:3, :] += jnp.sum(dh2, axis=0, keepdims=True)
        acc_ref[3:4, :] += jnp.sum(dh2 * xn1, axis=0, keepdims=True)
        acc_ref[4:5, :] += jnp.broadcast_to(loss_t, (1, D_MODEL))

    row = lambda i: (i, 0)
    gcp, gcn = _halo_specs(tm, 2)
    xip, xin = _halo_specs(tm, 3)
    tile = pl.BlockSpec((tm, D_MODEL), row)
    wide = pl.BlockSpec((tm, D_FF), row)
    half = pl.BlockSpec((tm, CONV_W), row)
    return pl.pallas_call(
        body, name="mlp_fwdbwd", grid=(n_lat,),
        out_shape=(jax.ShapeDtypeStruct((t_lat, D_FF), BF16), jax.ShapeDtypeStruct((t_lat, D_FF), BF16),
                   jax.ShapeDtypeStruct((t_lat, D_MODEL), BF16), jax.ShapeDtypeStruct((t_lat, D_MODEL), BF16),
                   jax.ShapeDtypeStruct((t_lat, D_MODEL), F32), jax.ShapeDtypeStruct((t_lat, D_MODEL), F32),
                   jax.ShapeDtypeStruct((t_lat, CONV_W), F32), jax.ShapeDtypeStruct((t_lat, CONV_W), BF16),
                   jax.ShapeDtypeStruct((8, D_MODEL), F32)),
        in_specs=[
            pl.BlockSpec((N_HEADS, tm, HEAD_PAD), lambda i: (0, i, 0)),
            pl.BlockSpec((tm, CONV_W), lambda i: (i, 1)), pl.BlockSpec((tm, CONV_W), lambda i: (i, 2)),
            pl.BlockSpec((tm, CONV_W), lambda i: (i, 3)),
            gcp, xip, gcn, xin,
            _const_spec(cw.shape), _resident_spec(wo_attn.shape), _resident_spec(wo_conv.shape),
            tile, tile, _const_spec(mod_b.shape), _const_spec(gf.shape),
            _resident_spec(w1.shape), _resident_spec(w2.shape),
        ],
        out_specs=(wide, wide, tile, tile, tile, tile, half, half, _const_spec((8, D_MODEL))),
        scratch_shapes=[pltpu.VMEM((n_ff, tm, FF_CHUNK), F32)],
        compiler_params=pltpu.CompilerParams(vmem_limit_bytes=VMEM_LIMIT),
    )(o, z, z, z, z, z, z, z, cw, wo_attn, wo_conv, x2, tgt, mod_b, gf, w1, w2)


def _mid_bwd(dx1, y1, z, yv, mod_b, wo_attn, wo_conv):
    t_lat = dx1.shape[0]
    tm = TOK_TILE

    def body(dx1_ref, y1_ref, gb_ref, yv_ref, mod_ref, woa_ref, woc_ref, dy1_ref, do_ref, dgb_ref, dyv_ref, acc_ref):
        i = pl.program_id(0)

        @pl.when(i == 0)
        def _():
            acc_ref[...] = jnp.zeros(acc_ref.shape, F32)

        dx1v = dx1_ref[...]
        acc_ref[0:1, :] += jnp.sum(dx1v * y1_ref[...], axis=0, keepdims=True)
        dy1 = (dx1v * mod_ref[0:1, :]).astype(BF16)
        dy1_ref[...] = dy1
        for h in range(N_HEADS):
            do_ref[h] = _dot_nt(dy1, woa_ref[h]).astype(BF16)
        dconv = _dot_nt(dy1, woc_ref[...])
        dgb_ref[...] = dconv * yv_ref[...]
        dyv_ref[...] = dconv * gb_ref[...]

    row = lambda i: (i, 0)
    tile = pl.BlockSpec((tm, D_MODEL), row)
    half = pl.BlockSpec((tm, CONV_W), row)
    return pl.pallas_call(
        body, name="mid_bwd", grid=(t_lat // tm,),
        out_shape=(jax.ShapeDtypeStruct((t_lat, D_MODEL), BF16),
                   jax.ShapeDtypeStruct((N_HEADS, t_lat, HEAD_PAD), BF16),
                   jax.ShapeDtypeStruct((t_lat, CONV_W), F32), jax.ShapeDtypeStruct((t_lat, CONV_W), F32),
                   jax.ShapeDtypeStruct((8, D_MODEL), F32)),
        in_specs=[tile, tile, pl.BlockSpec((tm, CONV_W), lambda i: (i, 1)), half, _const_spec(mod_b.shape),
                  _const_spec(wo_attn.shape), _const_spec(wo_conv.shape)],
        out_specs=(tile, pl.BlockSpec((N_HEADS, tm, HEAD_PAD), lambda i: (0, i, 0)), half, half,
                   _const_spec((8, D_MODEL))),
        compiler_params=pltpu.CompilerParams(vmem_limit_bytes=VMEM_LIMIT),
    )(dx1, y1, z, yv, mod_b, wo_attn, wo_conv)


def _inproj_bwd(x2, ctx2, mod_a, z, dyv, dgb, dx1, dqt, dk, dv, cos_t, sin_a, sin_b, w_in, w_uq, w_ukv, qg, kvg, cw):
    t_lat, t_ctx = x2.shape[0], ctx2.shape[0]
    tm = TOK_TILE
    n_lat = t_lat // tm
    n_all = n_lat + t_ctx // tm
    group = max(g for g in (1, 2, 4) if n_lat % g == 0)

    def body(x_ref, ctx_ref, mod_ref, z_ref, gcp_ref, xip_ref, gcn_ref, xin_ref, dyv_ref, dyvp_ref, dyvn_ref,
             dgb_ref, dx1_ref, dqt_ref, dk_ref, dv_ref, cos_ref, sa_ref, sb_ref, win_ref, wuq_ref, wukv_ref,
             qg_ref, kvg_ref, cw_ref, gx_ref, dwin_ref, dwuq_ref, dwukv_ref, acc_ref, h1_buf, dz_buf):
        i = pl.program_id(0)
        lat = i < n_lat

        @pl.when(i == 0)
        def _():
            dwin_ref[...] = jnp.zeros(dwin_ref.shape, F32)
            dwuq_ref[...] = jnp.zeros(dwuq_ref.shape, F32)
            dwukv_ref[...] = jnp.zeros(dwukv_ref.shape, F32)
            acc_ref[...] = jnp.zeros(acc_ref.shape, F32)

        xin = jnp.where(lat, x_ref[...], ctx_ref[...])
        rstd = lax.rsqrt(jnp.mean(xin * xin, axis=-1, keepdims=True) + EPS)
        xn = xin * rstd
        sc = mod_ref[0, 1:2, :]
        h1 = (xn * (1.0 + sc) + mod_ref[0, 0:1, :]).astype(BF16)
        z = z_ref[...]
        cos, sa, sb = cos_ref[...], sa_ref[...], sb_ref[...]
        qgv, kvgv = qg_ref[...], kvg_ref[...]
        cq = z[:, 0:Q_RANK]
        cqh = cq * lax.rsqrt(jnp.mean(cq * cq, axis=-1, keepdims=True) + EPS)
        rq = lax.rsqrt(jnp.mean(cq * cq, axis=-1, keepdims=True) + EPS)
        cqn = (cqh * qgv).astype(BF16)
        parts = []
        for h in range(N_HEADS):
            g = jnp.where(lat, dqt_ref[h].T, 0.0)
            parts.append(_unrope(g, cos, sa, sb))
        dq = jnp.concatenate(parts, axis=1).astype(BF16)
        dcqn = _dot_nt(dq, wuq_ref[...])
        dwuq_ref[...] += _dot_tn(cqn, dq)
        acc_ref[4:5, 0:Q_RANK] += jnp.sum(dcqn * cqh, axis=0, keepdims=True)
        dxn = dcqn * qgv
        dcq = rq * (dxn - cqh * jnp.mean(dxn * cqh, axis=-1, keepdims=True))
        ckv = z[:, Q_RANK:Q_RANK + KV_RANK]
        rk = lax.rsqrt(jnp.mean(ckv * ckv, axis=-1, keepdims=True) + EPS)
        ckvh = ckv * rk
        ckvn = (ckvh * kvgv).astype(BF16)
        dks = [dk_ref[h] for h in range(N_HEADS)]
        dkr = dks[0]
        for h in range(1, N_HEADS):
            dkr = dkr + dks[h]
        dkv = jnp.concatenate(dks + [dv_ref[h] for h in range(N_HEADS)], axis=1).astype(BF16)
        dckvn = _dot_nt(dkv, wukv_ref[...])
        dwukv_ref[...] += _dot_tn(ckvn, dkv)
        acc_ref[5:6, 0:KV_RANK] += jnp.sum(dckvn * ckvh, axis=0, keepdims=True)
        dxn = dckvn * kvgv
        dckv = rk * (dxn - ckvh * jnp.mean(dxn * ckvh, axis=-1, keepdims=True))
        dkr = _unrope(dkr, cos, sa, sb)
        gb, gc, xi = z[:, 512:1024], z[:, 1024:1536], z[:, 1536:2048]
        u = gc * xi
        u_prev = jnp.where(i > 0, gcp_ref[7:8, :] * xip_ref[7:8, :], 0.0)
        u_next = jnp.where(i < n_lat - 1, gcn_ref[0:1, :] * xin_ref[0:1, :], 0.0)
        um1, up1 = _shift_rows(u, u_prev, u_next)
        dyv = jnp.where(lat, dyv_ref[...], 0.0)
        dyv_prev = jnp.where(jnp.logical_and(i > 0, lat), dyvp_ref[7:8, :], 0.0)
        dyv_next = jnp.where(i < n_lat - 1, dyvn_ref[0:1, :], 0.0)
        dyv_m1, dyv_p1 = _shift_rows(dyv, dyv_prev, dyv_next)
        du = cw_ref[0:1, :] * dyv_p1 + cw_ref[1:2, :] * dyv + cw_ref[2:3, :] * dyv_m1
        dgc = du * xi
        dxi = du * gc
        dgb = jnp.where(lat, dgb_ref[...], 0.0)
        acc_ref[6:7, 0:CONV_W] += jnp.sum(dyv * um1, axis=0, keepdims=True)
        acc_ref[7:8, 0:CONV_W] += jnp.sum(dyv * u, axis=0, keepdims=True)
        acc_ref[8:9, 0:CONV_W] += jnp.sum(dyv * up1, axis=0, keepdims=True)
        dz = jnp.concatenate([dcq, dckv, dkr, dgb, dgc, dxi], axis=1).astype(BF16)
        dh1 = _dot_nt(dz, win_ref[...])
        slot = i % group
        rows_g = pl.ds(pl.multiple_of(slot * tm, tm), tm)
        h1_buf[rows_g, :] = h1
        dz_buf[rows_g, :] = dz

        @pl.when(jnp.logical_and(lat, slot == group - 1))
        def _():
            dwin_ref[...] += _dot_tn(h1_buf[...], dz_buf[...])

        @pl.when(jnp.logical_not(lat))
        def _():
            dwin_ref[...] += _dot_tn(h1, dz)
        s_sh = jnp.sum(dh1, axis=0, keepdims=True)
        s_sc = jnp.sum(dh1 * xn, axis=0, keepdims=True)
        zero = jnp.zeros_like(s_sh)
        acc_ref[0:1, :] += jnp.where(lat, s_sh, zero)
        acc_ref[1:2, :] += jnp.where(lat, s_sc, zero)
        acc_ref[2:3, :] += jnp.where(lat, zero, s_sh)
        acc_ref[3:4, :] += jnp.where(lat, zero, s_sc)
        dxn = dh1 * (1.0 + sc)
        dx = rstd * (dxn - xn * jnp.mean(dxn * xn, axis=-1, keepdims=True))

        @pl.when(lat)
        def _():
            gx_ref[...] = dx1_ref[...] + dx

    last = n_lat - 1
    per = tm // 8
    lat_row = lambda i: (jnp.minimum(i, last), 0)
    row = lambda i: (i, 0)
    gcp, gcn = _halo_specs(tm, 2)
    xip, xin = _halo_specs(tm, 3)
    n_halo = t_lat // 8
    dyvp = pl.BlockSpec((8, CONV_W), lambda i: (jnp.clip(i * per - 1, 0, n_halo - 1), 0))
    dyvn = pl.BlockSpec((8, CONV_W), lambda i: (jnp.minimum((i + 1) * per, n_halo - 1), 0))
    gcn = pl.BlockSpec((8, CONV_W), lambda i: (jnp.minimum((i + 1) * per, (t_lat + t_ctx) // 8 - 1), 2))
    xin = pl.BlockSpec((8, CONV_W), lambda i: (jnp.minimum((i + 1) * per, (t_lat + t_ctx) // 8 - 1), 3))
    head_f32 = pl.BlockSpec((N_HEADS, tm, HEAD_PAD), lambda i: (0, i, 0))
    tab = pl.BlockSpec((tm, HEAD_PAD), row)
    return pl.pallas_call(
        body, name="inproj_bwd", grid=(n_all,),
        out_shape=(jax.ShapeDtypeStruct((t_lat, D_MODEL), F32), jax.ShapeDtypeStruct(w_in.shape, F32),
                   jax.ShapeDtypeStruct(w_uq.shape, F32), jax.ShapeDtypeStruct(w_ukv.shape, F32),
                   jax.ShapeDtypeStruct((16, D_MODEL), F32)),
        in_specs=[
            pl.BlockSpec((tm, D_MODEL), lat_row), _const_spec((tm, D_MODEL)),
            pl.BlockSpec((1, 8, D_MODEL), lambda i: (i // n_lat, 0, 0)),
            pl.BlockSpec((tm, Z_COLS), row), gcp, xip, gcn, xin,
            pl.BlockSpec((tm, CONV_W), lat_row), dyvp, dyvn,
            pl.BlockSpec((tm, CONV_W), lat_row), pl.BlockSpec((tm, D_MODEL), lat_row),
            pl.BlockSpec((N_HEADS, HEAD_PAD, tm), lambda i: (0, 0, jnp.minimum(i, last))),
            head_f32, head_f32, tab, tab, tab,
            _const_spec(w_in.shape), _const_spec(w_uq.shape), _const_spec(w_ukv.shape),
            _const_spec(qg.shape), _const_spec(kvg.shape), _const_spec(cw.shape),
        ],
        out_specs=(pl.BlockSpec((tm, D_MODEL), lat_row), _const_spec(w_in.shape), _const_spec(w_uq.shape),
                   _const_spec(w_ukv.shape), _const_spec((16, D_MODEL))),
        scratch_shapes=[pltpu.VMEM((group * tm, D_MODEL), BF16), pltpu.VMEM((group * tm, Z_COLS), BF16)],
        compiler_params=pltpu.CompilerParams(vmem_limit_bytes=VMEM_LIMIT),
    )(x2, ctx2, mod_a, z, z, z, z, z, dyv, dyv, dyv, dgb, dx1, dqt, dk, dv, cos_t, sin_a, sin_b, w_in, w_uq, w_ukv,
      qg, kvg, cw)


def _wgrad(a, b, name, bm, bn):
    t, m = a.shape
    n = b.shape[1]
    bk = min(t, 4096)
    nk = t // bk
    nj = n // bn

    def body(a_ref, b_ref, o_ref, acc_ref):
        k = pl.program_id(2)
        part = _dot_tn(a_ref[...], b_ref[...])

        @pl.when(k == 0)
        def _():
            acc_ref[...] = part

        @pl.when(k > 0)
        def _():
            acc_ref[...] += part

        @pl.when(k == nk - 1)
        def _():
            o_ref[...] = acc_ref[...].astype(BF16)

    return pl.pallas_call(
        body, name=name, grid=(m // bm, nj, nk), out_shape=jax.ShapeDtypeStruct((m // bm * nj, bm, bn), BF16),
        in_specs=[pl.BlockSpec((bk, bm), lambda i, j, k: (k, i)), pl.BlockSpec((bk, bn), lambda i, j, k: (k, j))],
        out_specs=pl.BlockSpec((None, bm, bn), lambda i, j, k: (i * nj + j, 0, 0)),
        scratch_shapes=[pltpu.VMEM((bm, bn), F32)],
        compiler_params=pltpu.CompilerParams(vmem_limit_bytes=VMEM_LIMIT),
    )(a, b)


def _wgrad_out(o, conv, dy1):
    t = o.shape[1]
    bk = min(t, 2048)
    nk = t // bk
    rows = N_HEADS * HEAD_PAD + CONV_W

    def body(o_ref, c_ref, d_ref, w_ref, acc_ref):
        k = pl.program_id(0)
        cat = jnp.concatenate([o_ref[h] for h in range(N_HEADS)] + [c_ref[...]], axis=1)
        part = _dot_tn(cat, d_ref[...])

        @pl.when(k == 0)
        def _():
            acc_ref[...] = part

        @pl.when(k > 0)
        def _():
            acc_ref[...] += part

        @pl.when(k == nk - 1)
        def _():
            for h in range(N_HEADS):
                w_ref[h * V_DIM:(h + 1) * V_DIM, :] = acc_ref[h * HEAD_PAD:h * HEAD_PAD + V_DIM, :].astype(BF16)
            w_ref[N_HEADS * V_DIM:, :] = acc_ref[N_HEADS * HEAD_PAD:, :].astype(BF16)

    return pl.pallas_call(
        body, name="wgrad_out", grid=(nk,),
        out_shape=jax.ShapeDtypeStruct((D_MODEL, D_MODEL), BF16),
        in_specs=[pl.BlockSpec((N_HEADS, bk, HEAD_PAD), lambda k: (0, k, 0)),
                  pl.BlockSpec((bk, CONV_W), lambda k: (k, 0)),
                  pl.BlockSpec((bk, D_MODEL), lambda k: (k, 0))],
        out_specs=_const_spec((D_MODEL, D_MODEL)),
        scratch_shapes=[pltpu.VMEM((rows, D_MODEL), F32)],
        compiler_params=pltpu.CompilerParams(vmem_limit_bytes=VMEM_LIMIT),
    )(o, conv, dy1)


def _adamw_call(w, g, m, v, name):
    rows, cols = w.shape
    rb = 256 if rows % 256 == 0 else rows

    def body(w_ref, g_ref, m_ref, v_ref, d_ref, nm_ref, nv_ref):
        d_, m_, v_ = _adamw(w_ref[...], g_ref[...], m_ref[...], v_ref[...])
        d_ref[...] = d_
        nm_ref[...] = m_
        nv_ref[...] = v_

    spec = pl.BlockSpec((rb, cols), lambda i: (i, 0))
    shp = jax.ShapeDtypeStruct((rows, cols), F32)
    return pl.pallas_call(
        body, name=name, grid=(rows // rb,), out_shape=(shp, shp, shp),
        in_specs=[spec] * 4, out_specs=(spec, spec, spec),
    )(w, g, m, v)


def _wmod_update(s_t, dm, w, m, v):
    rows, cols = w.shape
    cb = 512

    def body(s_ref, dm_ref, w_ref, m_ref, v_ref, g_ref, d_ref, nm_ref, nv_ref):
        g = jnp.dot(s_ref[...], dm_ref[...], precision=HIGHEST, preferred_element_type=F32)
        d_, m_, v_ = _adamw(w_ref[...], g, m_ref[...], v_ref[...])
        g_ref[...] = g
        d_ref[...] = d_
        nm_ref[...] = m_
        nv_ref[...] = v_

    spec = pl.BlockSpec((rows, cb), lambda i: (0, i))
    shp = jax.ShapeDtypeStruct((rows, cols), F32)
    return pl.pallas_call(
        body, name="wmod_update", grid=(cols // cb,), out_shape=(shp, shp, shp, shp),
        in_specs=[_const_spec(s_t.shape), pl.BlockSpec((16, cb), lambda i: (0, i)), spec, spec, spec],
        out_specs=(spec, spec, spec, spec),
        compiler_params=pltpu.CompilerParams(vmem_limit_bytes=VMEM_LIMIT),
    )(s_t, dm, w, m, v)


def _rope_tables(t_lat, t_ctx):
    t = jnp.arange(t_lat)
    pos = jnp.stack([(t // GRID_W).astype(F32), (t % GRID_W).astype(F32)], axis=1)
    half = QK_ROPE // 4
    freqs = ROPE_THETA ** (-jnp.arange(0, 2 * half, 2, dtype=F32) / (2 * half))
    ang = pos[:, :, None] * freqs[None, None, :]
    cos, sin = jnp.cos(ang), jnp.sin(ang)
    zero = jnp.zeros_like(sin)
    cos32 = jnp.concatenate([cos, cos], axis=2).reshape(t_lat, QK_ROPE)
    sa32 = jnp.concatenate([zero, sin], axis=2).reshape(t_lat, QK_ROPE)
    sb32 = jnp.concatenate([-sin, zero], axis=2).reshape(t_lat, QK_ROPE)

    def widen(tab, fill):
        left = jnp.full((t_lat, ROPE_LANE0), fill, F32)
        right = jnp.full((t_lat, HEAD_PAD - ROPE_LANE0 - QK_ROPE), fill, F32)
        lat = jnp.concatenate([left, tab, right], axis=1)
        return jnp.concatenate([lat, jnp.full((t_ctx, HEAD_PAD), fill, F32)], axis=0)

    return widen(cos32, 1.0), widen(sa32, 0.0), widen(sb32, 0.0)


def _cols_from_shards(s):
    return jnp.transpose(s, (1, 0, 2)).reshape(s.shape[1], -1)


def _cols_to_shards(w):
    k, n = w.shape
    return jnp.transpose(w.reshape(k, N_SHARD, n // N_SHARD), (1, 0, 2))


def kernel(x, c, ctx, c_ctx, w_mod, b_mod, w_in, q_norm_g, w_uq, kv_norm_g, w_ukv, conv_w, w_out, w_mlp1, w_mlp2, final_norm_g, loss_target, m_c_ctx, m_w_mod, m_b_mod, m_w_in, m_q_norm_g, m_w_uq, m_kv_norm_g, m_w_ukv, m_conv_w, m_w_out, m_w_mlp1, m_w_mlp2, m_final_norm_g, v_c_ctx, v_w_mod, v_b_mod, v_w_in, v_q_norm_g, v_w_uq, v_kv_norm_g, v_w_ukv, v_conv_w, v_w_out, v_w_mlp1, v_w_mlp2, v_final_norm_g):
    t_lat, t_ctx = x.shape[1], ctx.shape[1]
    assert t_ctx == TOK_TILE and t_lat % TOK_TILE == 0 and t_lat % GRID_W == 0
    mx, my, mc = _pos()
    me = 4 * mx + 2 * my + mc
    j = 2 * mx + my
    ncol = w_mod.shape[2]
    x2, ctx2, tgt = x[0], ctx[0], loss_target[0]
    cctx_row = c_ctx.reshape(1, D_MODEL)

    b_sh = lax.dynamic_slice(b_mod, (0, j * ncol), (1, ncol))
    cw_pad = jnp.zeros((8, 128), F32).at[0:3, :].set(conv_w[0])
    c8, m_all = _mod_exchange(c, cctx_row, w_mod[0], b_sh, cw_pad)
    mvec = m_all[:, 0, :].reshape(6, D_MODEL)
    mctx = m_all[:, 8, :].reshape(6, D_MODEL)
    zeros6 = jnp.zeros((6, D_MODEL), F32)
    mod_a = jnp.stack([jnp.concatenate([mvec[0:2], zeros6], axis=0), jnp.concatenate([mctx[0:2], zeros6], axis=0)])
    mod_b = jnp.concatenate([mvec[2:6], jnp.zeros((4, D_MODEL), F32)], axis=0)
    cw_full = jnp.pad(jnp.transpose(m_all[:, 9:12, 0:128], (1, 0, 2)).reshape(3, CONV_W), ((0, 5), (0, 0)))

    g_in, g_uq, g_ukv, g_out, g_m1, g_m2 = _weight_prep(
        (w_in[0], w_uq[0], w_ukv[0], w_out[0], w_mlp1[0], w_mlp2[0]), 3)
    w_in_f = _cols_from_shards(g_in)
    zc = lambda n: jnp.zeros((D_MODEL, n), BF16)
    w_in_p = jnp.concatenate([w_in_f[:, 0:384], zc(64), w_in_f[:, 384:416], zc(32), w_in_f[:, 416:]], axis=1)
    w_uq_f = _cols_from_shards(g_uq).reshape(Q_RANK, N_HEADS, QK_DIM)
    w_uq_p = jnp.pad(w_uq_f, ((0, 0), (0, 0), (0, HEAD_PAD - QK_DIM))).reshape(Q_RANK, N_HEADS * HEAD_PAD)
    w_ukv_f = _cols_from_shards(g_ukv).reshape(KV_RANK, N_HEADS, QK_NOPE + V_DIM)
    padh = lambda a: jnp.pad(a, ((0, 0), (0, 0), (0, HEAD_PAD - a.shape[2]))).reshape(KV_RANK, N_HEADS * HEAD_PAD)
    w_ukv_p = jnp.concatenate([padh(w_ukv_f[:, :, :QK_NOPE]), padh(w_ukv_f[:, :, QK_NOPE:])], axis=1)
    cos_t, sin_a, sin_b = _rope_tables(t_lat, t_ctx)
    gf_row = final_norm_g.reshape(1, D_MODEL)
    c_idx = mc.reshape(1).astype(jnp.int32)
    j_idx = j.reshape(1).astype(jnp.int32)

    z, q, k, v, kt = _inproj_fwd(x2, ctx2, mod_a, w_in_p, q_norm_g, kv_norm_g, w_uq_p, w_ukv_p, cos_t, sin_a, sin_b)
    o, lse, g_out, w1, g_m2 = _attn_fwd(q, k, v, t_lat, (g_out, g_m1, g_m2))
    w_out_f = g_out.reshape(D_MODEL, D_MODEL)
    wo_attn = jnp.pad(w_out_f[0:512].reshape(N_HEADS, V_DIM, D_MODEL), ((0, 0), (0, HEAD_PAD - V_DIM), (0, 0)))
    wo_conv = w_out_f[512:]
    w2 = g_m2.reshape(D_FF, D_MODEL)
    r, da, h2, dy2, dx1, y1, yv, conv, acc_mlp = _mlp_fwdbwd(o, z, x2, tgt, mod_b, gf_row, cw_full, wo_attn, wo_conv,
                                                               w1, w2)
    dy1, do, dgb, dyv, acc_mid = _mid_bwd(dx1, y1, z, yv, mod_b, wo_attn, wo_conv)
    d_w1 = _wgrad(h2, da, "wgrad_mlp1", D_MODEL, FF_CHUNK)
    d_w2 = _wgrad(r, dy2, "wgrad_mlp2", FF_CHUNK, D_MODEL)
    d_wout = _wgrad_out(o, conv, dy1).reshape(N_SHARD, D_MODEL // N_SHARD, D_MODEL)
    big_grads, big_names = (d_w1, d_w2, d_wout), ("mlp1", "mlp2", "out")
    big_got = _rs_sibling(big_grads, "rs_sibling_big")
    big_parts = [_add_pairs(a, g, c_idx, "rs_add_pairs_" + n) for a, g, n in zip(big_grads, big_got, big_names)]
    dqt, dk, dv, *big_recv = _attn_bwd(q, k, v, kt, o, do, lse, t_lat, big_parts)
    big_halves = [_add_chips(p, g, j_idx, "rs_add_chips_" + n) for p, g, n in zip(big_parts, big_recv, big_names)]
    g_w1, g_w2, g_wout = _rs_join(big_halves, "rs_join_big")
    gx, d_win, d_wuq, d_wukv, acc_in = _inproj_bwd(x2, ctx2, mod_a, z, dyv, dgb, dx1, dqt, dk, dv, cos_t, sin_a, sin_b,
                                                   w_in_p, w_uq_p, w_ukv_p, q_norm_g, kv_norm_g, cw_full)

    pad_row = lambda a: jnp.pad(a, ((0, 0), (0, D_MODEL - a.shape[1])))
    sv = jnp.concatenate([
        acc_in[0:2], acc_mid[0:1], acc_mlp[2:4], acc_mlp[1:2],
        acc_in[2:4], acc_mlp[0:1], acc_in[4:5], acc_in[5:6], acc_in[6:9], acc_mlp[4:5],
        jnp.zeros((1, D_MODEL), F32)], axis=0)
    all_sv, red, o_cc, o_b, o_q, o_k, o_gf = _small_exchange(
        sv, w_mod[0], cctx_row, m_c_ctx.reshape(1, D_MODEL), v_c_ctx.reshape(1, D_MODEL),
        b_mod.reshape(6, D_MODEL), m_b_mod.reshape(6, D_MODEL), v_b_mod.reshape(6, D_MODEL),
        q_norm_g, m_q_norm_g, v_q_norm_g, kv_norm_g, m_kv_norm_g, v_kv_norm_g,
        gf_row, m_final_norm_g.reshape(1, D_MODEL), v_final_norm_g.reshape(1, D_MODEL))
    loss = red[14, 0]

    c9 = jnp.concatenate([c8[0::8], jnp.zeros((7, D_MODEL), F32)], axis=0)
    s_t = jnp.transpose(c9 * jax.nn.sigmoid(c9))
    dm_ex = all_sv[:, 0:6, :].reshape(8, 6 * D_MODEL)
    dm_ctx = jnp.concatenate([red[6:8].reshape(1, 2 * D_MODEL), jnp.zeros((1, 4 * D_MODEL), F32)], axis=1)
    dm16 = jnp.concatenate([dm_ex, dm_ctx, jnp.zeros((7, 6 * D_MODEL), F32)], axis=0)
    dm_sh = lax.dynamic_slice(dm16, (0, j * ncol), (16, ncol))
    g_wmod, d_wmod, nm_wmod, nv_wmod = _wmod_update(s_t, dm_sh, w_mod[0], m_w_mod[0], v_w_mod[0])

    g_cw = lax.dynamic_slice(red[11:14, 0:CONV_W], (0, j * 128), (3, 128))
    d_cw, nm_cw, nv_cw = _adamw_call(conv_w[0], g_cw, m_conv_w[0], v_conv_w[0], "adamw_conv")

    d_win_f = jnp.concatenate([d_win[:, 0:384], d_win[:, 448:480], d_win[:, 512:]], axis=1)
    d_wuq_f = d_wuq.reshape(Q_RANK, N_HEADS, HEAD_PAD)[:, :, 0:QK_DIM].reshape(Q_RANK, N_HEADS * QK_DIM)
    d_wukv3 = d_wukv.reshape(KV_RANK, 2, N_HEADS, HEAD_PAD)
    d_wukv_f = jnp.concatenate([d_wukv3[:, 0, :, 0:QK_NOPE], d_wukv3[:, 1, :, 0:V_DIM]], axis=2).reshape(KV_RANK, -1)
    rest = tuple(_cols_to_shards(a).astype(BF16) for a in (d_win_f, d_wuq_f, d_wukv_f))
    rest_names = ("in", "uq", "ukv")
    rest_got = _rs_sibling(rest, "rs_sibling_rest")
    rest_parts = [_add_pairs(a, g, c_idx, "rs_add_pairs_" + n) for a, g, n in zip(rest, rest_got, rest_names)]
    rest_recv = _rs_chips(rest_parts, "rs_chips_rest")
    rest_halves = [_add_chips(p, g, j_idx, "rs_add_chips_" + n) for p, g, n in zip(rest_parts, rest_recv, rest_names)]
    g_win, g_wuq, g_wukv = _rs_join(rest_halves, "rs_join_rest")
    upd = {}
    for name, w_, g_, m_, v_ in (("in", w_in, g_win, m_w_in, v_w_in), ("uq", w_uq, g_wuq, m_w_uq, v_w_uq),
                                 ("ukv", w_ukv, g_wukv, m_w_ukv, v_w_ukv), ("out", w_out, g_wout, m_w_out, v_w_out),
                                 ("mlp1", w_mlp1, g_w1, m_w_mlp1, v_w_mlp1), ("mlp2", w_mlp2, g_w2, m_w_mlp2, v_w_mlp2)):
        upd[name] = _adamw_call(w_[0], g_, m_[0], v_[0], "adamw_" + name)

    def four(o4, shape):
        return [o4[r].reshape(shape) for r in range(4)]

    cc4 = four(o_cc, (D_MODEL,))
    b4 = [o_b[r].reshape(1, 6 * D_MODEL) for r in range(4)]
    q4 = four(o_q, (1, Q_RANK))
    k4 = four(o_k, (1, KV_RANK))
    gf4 = four(o_gf, (D_MODEL,))
    big = {"in": g_win, "uq": g_wuq, "ukv": g_wukv, "out": g_wout, "mlp1": g_w1, "mlp2": g_w2}

    def leaf(idx):
        wm = (g_wmod, d_wmod, nm_wmod, nv_wmod)[idx]
        cwv = (g_cw, d_cw, nm_cw, nv_cw)[idx]
        bigv = {n: (big[n] if idx == 0 else upd[n][idx - 1]) for n in big}
        return [cc4[idx], wm[None], b4[idx], bigv["in"][None], q4[idx], bigv["uq"][None], k4[idx], bigv["ukv"][None],
                cwv[None], bigv["out"][None], bigv["mlp1"][None], bigv["mlp2"][None], gf4[idx]]

    return (loss, gx[None], *leaf(0), *leaf(1), *leaf(2), *leaf(3))
```

```python
import functools
import math

import jax
import jax.numpy as jnp
from jax import lax
from jax.experimental import pallas as pl
from jax.experimental.pallas import tpu as pltpu

F32 = jnp.float32
BF16 = jnp.bfloat16
MESH = pl.DeviceIdType.MESH
HIGHEST = lax.Precision.HIGHEST

D_MODEL = 1024
N_HEADS = 8
QK_NOPE = 64
QK_ROPE = 32
QK_DIM = QK_NOPE + QK_ROPE
V_DIM = 64
Q_RANK = 256
KV_RANK = 128
CONV_W = 512
D_FF = 4096
GRID_W = 64
ROPE_THETA = 10000.0
EPS = 1e-6
ATTN_SCALE = 1.0 / math.sqrt(QK_DIM)
HEAD_PAD = 128
Z_COLS = 2048
ROPE_LANE0 = QK_NOPE
N_SHARD = 4
TOK_TILE = 256
FF_CHUNK = 1024
KEY_CHUNK = 1024
ATTN_HEADS_PER_STEP = 4

ADAM_LR = 0.001
ADAM_B1 = 0.9
ADAM_B2 = 0.999
ADAM_EPS = 1e-08
ADAM_WD = 0.01
ADAM_STEP = 10

LOG2E = 1.4426950408889634

VMEM_LIMIT = 56 * 1024 * 1024
STAGE_VMEM_LIMIT = 32 * 1024 * 1024


def _pos():
    return lax.axis_index("x"), lax.axis_index("y"), lax.axis_index("c")


def _dot(a, b):
    return jnp.dot(a, b, preferred_element_type=F32)


def _dot_nt(a, b):
    return lax.dot_general(a, b, (((1,), (1,)), ((), ())), preferred_element_type=F32)


def _dot_tn(a, b):
    return lax.dot_general(a, b, (((0,), (0,)), ((), ())), preferred_element_type=F32)


def _rope(v, cos, sa, sb):
    return v * cos + pltpu.roll(v, 8, 1) * sa + pltpu.roll(v, HEAD_PAD - 8, 1) * sb


def _unrope(g, cos, sa, sb):
    return g * cos + pltpu.roll(g * sa, HEAD_PAD - 8, 1) + pltpu.roll(g * sb, 8, 1)


def _sigmoid(v):
    return 1.0 / (1.0 + jnp.exp(-v))


def _adamw(w, g, m, v):
    m = ADAM_B1 * m + (1.0 - ADAM_B1) * g
    v = ADAM_B2 * v + (1.0 - ADAM_B2) * (g * g)
    m_hat = m / (1.0 - ADAM_B1 ** ADAM_STEP)
    v_hat = v / (1.0 - ADAM_B2 ** ADAM_STEP)
    delta = -ADAM_LR * (m_hat / (jnp.sqrt(v_hat) + ADAM_EPS) + ADAM_WD * w)
    return delta, m, v


def _shift_rows(u, prev_row, next_row):
    n = u.shape[0]
    rows = lax.broadcasted_iota(jnp.int32, u.shape, 0)
    um1 = jnp.where(rows == 0, prev_row, pltpu.roll(u, 1, 0))
    up1 = jnp.where(rows == n - 1, next_row, pltpu.roll(u, n - 1, 0))
    return um1, up1


def _const_spec(shape):
    nd = len(shape)
    return pl.BlockSpec(shape, lambda *_: (0,) * nd)


def _resident_spec(shape):
    nd = len(shape)
    return pl.BlockSpec(shape, lambda *_: (0,) * nd, pipeline_mode=pl.Buffered(1))


def _peer(r, x, y, c):
    px = 1 - x if r & 4 else x
    py = 1 - y if r & 2 else y
    pc = 1 - c if r & 1 else c
    return (px, py, pc)


def _mod_exchange(c_row, cctx_row, w_mod_sh, b_sh, cw_sh):
    ncol = w_mod_sh.shape[1]

    def body(c_ref, cctx_ref, w_ref, b_ref, cw_ref, c8_ref, m_ref, mine_ref, msh_ref, ssem, rsem, ssem2, rsem2):
        x, y, c = _pos()
        me = 4 * x + 2 * y + c
        j = 2 * x + y
        mine_ref[...] = jnp.zeros(mine_ref.shape, F32)
        mine_ref[0:1, :] = c_ref[...]
        my_rows = pl.ds(pl.multiple_of(8 * me, 8), 8)
        sends = []
        for r in range(1, 8):
            cp = pltpu.make_async_remote_copy(
                src_ref=mine_ref, dst_ref=c8_ref.at[my_rows], send_sem=ssem.at[r - 1], recv_sem=rsem.at[r - 1],
                device_id=_peer(r, x, y, c), device_id_type=MESH)
            cp.start()
            sends.append(cp)
        for cp in sends:
            cp.wait()
        c8_ref[my_rows, :] = mine_ref[...]
        c8_ref[64:72, :] = jnp.zeros((8, D_MODEL), F32)
        c8_ref[64:65, :] = cctx_ref[...]
        cv = c8_ref[...]
        s = cv * _sigmoid(cv)
        m = jnp.dot(s, w_ref[...], precision=HIGHEST, preferred_element_type=F32) + b_ref[...]
        msh_ref[0:64, :] = m[0:64, :]
        msh_ref[64:72, :] = jnp.zeros((8, ncol), F32)
        msh_ref[64:65, :] = m[64:65, :]
        msh_ref[65:68, 0:128] = cw_ref[0:3, :]
        m_ref[j, 0:8, :] = msh_ref[my_rows, :]
        m_ref[j, 8:16, :] = msh_ref[64:72, :]
        sends2 = []
        for k, (px, py) in enumerate(_chips(x, y)):
            theirs = pl.ds(pl.multiple_of(8 * (4 * px + 2 * py + c), 8), 8)
            for half, src in enumerate((msh_ref.at[theirs], msh_ref.at[64:72])):
                cp = pltpu.make_async_remote_copy(
                    src_ref=src, dst_ref=m_ref.at[j, 8 * half:8 * half + 8], send_sem=ssem2.at[2 * k + half],
                    recv_sem=rsem2.at[2 * k + half], device_id=(px, py, c), device_id_type=MESH)
                cp.start()
                sends2.append(cp)
        for cp in sends2:
            cp.wait()

    vm = pl.BlockSpec(memory_space=pltpu.VMEM)
    return pl.pallas_call(
        body, name="mod_exchange",
        out_shape=(jax.ShapeDtypeStruct((72, D_MODEL), F32), jax.ShapeDtypeStruct((N_SHARD, 16, ncol), F32)),
        in_specs=[vm] * 5, out_specs=(vm, vm),
        scratch_shapes=[pltpu.VMEM((8, D_MODEL), F32), pltpu.VMEM((72, ncol), F32),
                        pltpu.SemaphoreType.DMA((7,)), pltpu.SemaphoreType.DMA((7,)),
                        pltpu.SemaphoreType.DMA((6,)), pltpu.SemaphoreType.DMA((6,))],
        compiler_params=pltpu.CompilerParams(vmem_limit_bytes=VMEM_LIMIT),
    )(c_row, cctx_row, w_mod_sh, b_sh, cw_sh)


def _chips(x, y):
    return [(1 - x, y), (x, 1 - y), (1 - x, 1 - y)]


def _halves(ref, c, align):
    hr = ref.shape[-2] // 2
    return (pl.ds(pl.multiple_of(c * hr, align), hr), pl.ds(pl.multiple_of((1 - c) * hr, align), hr))


class _ShardGather:
    def __init__(self, refs, ssem, rsem, fsend, frecv):
        self.refs, self.sems = refs, (ssem, rsem, fsend, frecv)
        self.x, self.y, self.c = _pos()
        self.j = 2 * self.x + self.y

    def _ici(self, a, k, slot):
        g = self.refs[a]
        ssem, rsem, _, _ = self.sems
        mine, _ = _halves(g, self.c, 16)
        px, py = _chips(self.x, self.y)[k]
        return pltpu.make_async_remote_copy(
            src_ref=g.at[self.j, mine], dst_ref=g.at[slot, mine], send_sem=ssem.at[3 * a + k],
            recv_sem=rsem.at[3 * a + k], device_id=(px, py, self.c), device_id_type=MESH)

    def _d2d(self, a, k, to_other_half):
        g = self.refs[a]
        _, _, fsend, frecv = self.sems
        mine, theirs = _halves(g, self.c, 16)
        px, py = _chips(self.x, self.y)[k]
        jk = 2 * px + py
        return pltpu.make_async_remote_copy(
            src_ref=g.at[jk, mine], dst_ref=g.at[jk, theirs if to_other_half else mine],
            send_sem=fsend.at[3 * a + k], recv_sem=frecv.at[3 * a + k],
            device_id=(self.x, self.y, 1 - self.c), device_id_type=MESH)

    def start(self):
        for a in range(len(self.refs)):
            for k in range(3):
                self._ici(a, k, self.j).start()

    def forward(self):
        for a in range(len(self.refs)):
            for k, (px, py) in enumerate(_chips(self.x, self.y)):
                self._ici(a, k, 2 * px + py).wait_recv()
                self._d2d(a, k, False).start()

    def finish(self):
        for a in range(len(self.refs)):
            for k in range(3):
                self._d2d(a, k, True).wait()
                self._ici(a, k, self.j).wait_send()


def _gather_sems(n_arrays):
    return [pltpu.SemaphoreType.DMA((3 * n_arrays,)) for _ in range(4)]


def _weight_prep(srcs, n_gather):
    n = len(srcs)
    n_split = 4

    def body(*refs):
        ins, outs, f32s, bfs = refs[:n], refs[n:2 * n], refs[2 * n:3 * n], refs[3 * n:4 * n]
        lsem_in, lsem_out = refs[4 * n], refs[4 * n + 1]
        x, y, c = _pos()
        j = 2 * x + y

        def pieces(rows):
            step = rows // n_split
            return [pl.ds(q * step, step) for q in range(n_split)]

        for t in range(n):
            for sl in pieces(ins[t].shape[0]):
                pltpu.make_async_copy(ins[t].at[sl], f32s[t].at[sl], lsem_in.at[t]).start()
        gather = _ShardGather(outs[:n_gather], *refs[4 * n + 2:])
        for t in range(n):
            pltpu.make_async_copy(ins[t], f32s[t], lsem_in.at[t]).wait()
            bfs[t][...] = f32s[t][...].astype(BF16)
            for sl in pieces(ins[t].shape[0]):
                pltpu.make_async_copy(bfs[t].at[sl], outs[t].at[j, sl], lsem_out.at[t]).start()
            if t == n_gather - 1:
                for u in range(n_gather):
                    pltpu.make_async_copy(bfs[u], outs[u].at[j], lsem_out.at[u]).wait()
                gather.start()
        gather.forward()
        gather.finish()
        for t in range(n_gather, n):
            pltpu.make_async_copy(bfs[t], outs[t].at[j], lsem_out.at[t]).wait()

    hbm = pl.BlockSpec(memory_space=pl.ANY)
    return pl.pallas_call(
        body, name="weight_prep",
        out_shape=tuple(jax.ShapeDtypeStruct((N_SHARD,) + a.shape, BF16) for a in srcs),
        in_specs=[hbm] * n, out_specs=(hbm,) * n,
        scratch_shapes=[pltpu.VMEM(a.shape, F32) for a in srcs] + [pltpu.VMEM(a.shape, BF16) for a in srcs]
        + [pltpu.SemaphoreType.DMA((n,)), pltpu.SemaphoreType.DMA((n,))] + _gather_sems(n_gather),
        compiler_params=pltpu.CompilerParams(vmem_limit_bytes=STAGE_VMEM_LIMIT),
    )(*srcs)


def _rs_sibling(arrs, name):
    n = len(arrs)

    def body(*refs):
        g_refs, got_refs, ssem, rsem = refs[:n], refs[n:2 * n], refs[2 * n], refs[2 * n + 1]
        x, y, c = _pos()
        for a in range(n):
            _, theirs = _halves(g_refs[a], c, 16)
            for s in range(N_SHARD):
                pltpu.make_async_remote_copy(
                    src_ref=g_refs[a].at[s, theirs], dst_ref=got_refs[a].at[s], send_sem=ssem.at[a],
                    recv_sem=rsem.at[a], device_id=(x, y, 1 - c), device_id_type=MESH).start()
        for a in range(n):
            _, theirs = _halves(g_refs[a], c, 16)
            pltpu.make_async_remote_copy(
                src_ref=g_refs[a].at[:, theirs], dst_ref=got_refs[a], send_sem=ssem.at[a], recv_sem=rsem.at[a],
                device_id=(x, y, 1 - c), device_id_type=MESH).wait()

    hbm = pl.BlockSpec(memory_space=pl.ANY)
    return pl.pallas_call(
        body, name=name,
        out_shape=tuple(jax.ShapeDtypeStruct((N_SHARD, a.shape[1] // 2, a.shape[2]), BF16) for a in arrs),
        in_specs=[hbm] * n, out_specs=(hbm,) * n,
        scratch_shapes=[pltpu.SemaphoreType.DMA((n,)), pltpu.SemaphoreType.DMA((n,))],
    )(*arrs)


class _ChipScatter:
    def __init__(self, parts, gots, ssem, rsem):
        self.parts, self.gots, self.ssem, self.rsem = parts, gots, ssem, rsem
        self.x, self.y, self.c = _pos()

    def _copy(self, a, k):
        px, py = _chips(self.x, self.y)[k]
        return pltpu.make_async_remote_copy(
            src_ref=self.parts[a].at[2 * px + py], dst_ref=self.gots[a].at[k], send_sem=self.ssem.at[3 * a + k],
            recv_sem=self.rsem.at[3 * a + k], device_id=(px, py, self.c), device_id_type=MESH)

    def start(self):
        for a in range(len(self.parts)):
            for k in range(3):
                self._copy(a, k).start()

    def finish(self):
        for a in range(len(self.parts)):
            for k in range(3):
                self._copy(a, k).wait()


def _rs_chips(parts, name):
    n = len(parts)

    def body(*refs):
        scatter = _ChipScatter(refs[:n], refs[n:2 * n], refs[2 * n], refs[2 * n + 1])
        scatter.start()
        scatter.finish()

    hbm = pl.BlockSpec(memory_space=pl.ANY)
    return pl.pallas_call(
        body, name=name, out_shape=tuple(jax.ShapeDtypeStruct((3,) + p.shape[1:], BF16) for p in parts),
        in_specs=[hbm] * n, out_specs=(hbm,) * n,
        scratch_shapes=[pltpu.SemaphoreType.DMA((3 * n,)), pltpu.SemaphoreType.DMA((3 * n,))],
    )(*parts)


def _rs_join(halves, name):
    n = len(halves)

    def body(*refs):
        h_refs, f_refs, stages = refs[:n], refs[n:2 * n], refs[2 * n:3 * n]
        lsem_in, lsem_out, ssem, rsem = refs[3 * n:]
        x, y, c = _pos()
        remote = []
        for a in range(n):
            mine, _ = _halves(f_refs[a], c, 8)
            cp = pltpu.make_async_remote_copy(
                src_ref=h_refs[a], dst_ref=f_refs[a].at[mine], send_sem=ssem.at[a], recv_sem=rsem.at[a],
                device_id=(x, y, 1 - c), device_id_type=MESH)
            cp.start()
            remote.append(cp)
            pltpu.make_async_copy(h_refs[a], stages[a], lsem_in.at[a]).start()
        local = []
        for a in range(n):
            mine, _ = _halves(f_refs[a], c, 8)
            pltpu.make_async_copy(h_refs[a], stages[a], lsem_in.at[a]).wait()
            cp = pltpu.make_async_copy(stages[a], f_refs[a].at[mine], lsem_out.at[a])
            cp.start()
            local.append(cp)
        for cp in remote + local:
            cp.wait()

    hbm = pl.BlockSpec(memory_space=pl.ANY)
    return pl.pallas_call(
        body, name=name,
        out_shape=tuple(jax.ShapeDtypeStruct((2 * h.shape[0], h.shape[1]), F32) for h in halves),
        in_specs=[hbm] * n, out_specs=(hbm,) * n,
        scratch_shapes=[pltpu.VMEM(h.shape, F32) for h in halves]
        + [pltpu.SemaphoreType.DMA((n,)) for _ in range(4)],
        compiler_params=pltpu.CompilerParams(vmem_limit_bytes=STAGE_VMEM_LIMIT),
    )(*halves)


def _row_block(rows):
    return max(b for b in range(16, 513, 16) if rows % b == 0)


def _add_pairs(arr, got, c_idx, name):
    hr, cols = got.shape[1], got.shape[2]
    rb = _row_block(hr)
    nb = hr // rb

    def body(c_ref, a_ref, b_ref, o_ref):
        o_ref[...] = (a_ref[...].astype(F32) + b_ref[...].astype(F32)).astype(BF16)

    spec = pl.BlockSpec((1, rb, cols), lambda s, r, c_ref: (s, r, 0))
    grid_spec = pltpu.PrefetchScalarGridSpec(
        num_scalar_prefetch=1, grid=(N_SHARD, nb),
        in_specs=[pl.BlockSpec((1, rb, cols), lambda s, r, c_ref: (s, c_ref[0] * nb + r, 0)), spec],
        out_specs=spec)
    return pl.pallas_call(
        body, name=name, grid_spec=grid_spec, out_shape=jax.ShapeDtypeStruct(got.shape, BF16),
    )(c_idx, arr, got)


def _add_chips(part, got, j_idx, name):
    hr, cols = got.shape[1], got.shape[2]
    rb = _row_block(hr)

    def body(j_ref, p_ref, g_ref, o_ref):
        acc = p_ref[0].astype(F32)
        for k in range(3):
            acc = acc + g_ref[k].astype(F32)
        o_ref[...] = acc

    grid_spec = pltpu.PrefetchScalarGridSpec(
        num_scalar_prefetch=1, grid=(hr // rb,),
        in_specs=[pl.BlockSpec((1, rb, cols), lambda r, j_ref: (j_ref[0], r, 0)),
                  pl.BlockSpec((3, rb, cols), lambda r, j_ref: (0, r, 0))],
        out_specs=pl.BlockSpec((rb, cols), lambda r, j_ref: (r, 0)))
    return pl.pallas_call(
        body, name=name, grid_spec=grid_spec, out_shape=jax.ShapeDtypeStruct((hr, cols), F32),
    )(j_idx, part, got)


def _small_exchange(sv, w_mod_sh, cctx, m_cctx, v_cctx, bmod, m_bmod, v_bmod, qg, m_qg, v_qg, kvg, m_kvg, v_kvg,
                    gf, m_gf, v_gf):
    ncol = w_mod_sh.shape[1]

    def body(sv_ref, w_ref, cctx_ref, mcc_ref, vcc_ref, b_ref, mb_ref, vb_ref, qg_ref, mq_ref, vq_ref,
             kg_ref, mk_ref, vk_ref, gf_ref, mgf_ref, vgf_ref,
             all_ref, red_ref, occ_ref, ob_ref, oq_ref, ok_ref, ogf_ref,
             vec_ref, part_ref, ssem, rsem, ssem2, rsem2):
        x, y, c = _pos()
        me = 4 * x + 2 * y + c
        j = 2 * x + y
        sends = []
        for r in range(1, 8):
            cp = pltpu.make_async_remote_copy(
                src_ref=sv_ref, dst_ref=all_ref.at[me], send_sem=ssem.at[r - 1], recv_sem=rsem.at[r - 1],
                device_id=_peer(r, x, y, c), device_id_type=MESH)
            cp.start()
            sends.append(cp)
        for cp in sends:
            cp.wait()
        all_ref[me] = sv_ref[...]
        red = all_ref[0]
        for d in range(1, 8):
            red = red + all_ref[d]
        red_ref[...] = red
        vec_ref[...] = jnp.zeros(vec_ref.shape, F32)

        @pl.when(j == 0)
        def _():
            vec_ref[0:1, 0:1024] = red[6:7, :]
            vec_ref[0:1, 1024:1536] = red[7:8, 0:512]

        @pl.when(j == 1)
        def _():
            vec_ref[0:1, 0:512] = red[7:8, 512:1024]

        part = lax.dot_general(vec_ref[...], w_ref[...], (((1,), (1,)), ((), ())), precision=HIGHEST,
                               preferred_element_type=F32)
        part_ref[j] = part
        sends2 = []
        for k, r in enumerate((4, 2, 6)):
            cp = pltpu.make_async_remote_copy(
                src_ref=part_ref.at[j], dst_ref=part_ref.at[j], send_sem=ssem2.at[k], recv_sem=rsem2.at[k],
                device_id=_peer(r, x, y, c), device_id_type=MESH)
            cp.start()
            sends2.append(cp)
        for cp in sends2:
            cp.wait()
        tot = part_ref[0]
        for s in range(1, N_SHARD):
            tot = tot + part_ref[s]
        cc = cctx_ref[...]
        sg = _sigmoid(cc)
        g_cc = tot[0:1, :] * (sg * (1.0 + cc * (1.0 - sg)))
        d_, m_, v_ = _adamw(cc, g_cc, mcc_ref[...], vcc_ref[...])
        occ_ref[0:1, :] = g_cc
        occ_ref[1:2, :] = d_
        occ_ref[2:3, :] = m_
        occ_ref[3:4, :] = v_
        occ_ref[4:8, :] = jnp.zeros((4, D_MODEL), F32)
        g_b = red[0:6, :]
        pad = jnp.concatenate([red[6:8, :], jnp.zeros((4, D_MODEL), F32)], axis=0)
        g_b = g_b + pad
        d_, m_, v_ = _adamw(b_ref[...], g_b, mb_ref[...], vb_ref[...])
        ob_ref[0] = g_b
        ob_ref[1] = d_
        ob_ref[2] = m_
        ob_ref[3] = v_
        g_q = red[9:10, 0:Q_RANK]
        d_, m_, v_ = _adamw(qg_ref[...], g_q, mq_ref[...], vq_ref[...])
        oq_ref[0:1, :] = g_q
        oq_ref[1:2, :] = d_
        oq_ref[2:3, :] = m_
        oq_ref[3:4, :] = v_
        oq_ref[4:8, :] = jnp.zeros((4, Q_RANK), F32)
        g_k = red[10:11, 0:KV_RANK]
        d_, m_, v_ = _adamw(kg_ref[...], g_k, mk_ref[...], vk_ref[...])
        ok_ref[0:1, :] = g_k
        ok_ref[1:2, :] = d_
        ok_ref[2:3, :] = m_
        ok_ref[3:4, :] = v_
        ok_ref[4:8, :] = jnp.zeros((4, KV_RANK), F32)
        g_f = red[8:9, :]
        d_, m_, v_ = _adamw(gf_ref[...], g_f, mgf_ref[...], vgf_ref[...])
        ogf_ref[0:1, :] = g_f
        ogf_ref[1:2, :] = d_
        ogf_ref[2:3, :] = m_
        ogf_ref[3:4, :] = v_
        ogf_ref[4:8, :] = jnp.zeros((4, D_MODEL), F32)

    vm = pl.BlockSpec(memory_space=pltpu.VMEM)
    out_shape = (
        jax.ShapeDtypeStruct((8, 16, D_MODEL), F32),
        jax.ShapeDtypeStruct((16, D_MODEL), F32),
        jax.ShapeDtypeStruct((8, D_MODEL), F32),
        jax.ShapeDtypeStruct((4, 6, D_MODEL), F32),
        jax.ShapeDtypeStruct((8, Q_RANK), F32),
        jax.ShapeDtypeStruct((8, KV_RANK), F32),
        jax.ShapeDtypeStruct((8, D_MODEL), F32),
    )
    return pl.pallas_call(
        body, name="small_exchange", out_shape=out_shape, in_specs=[vm] * 17, out_specs=tuple([vm] * 7),
        scratch_shapes=[pltpu.VMEM((8, ncol), F32), pltpu.VMEM((N_SHARD, 8, D_MODEL), F32),
                        pltpu.SemaphoreType.DMA((7,)), pltpu.SemaphoreType.DMA((7,)),
                        pltpu.SemaphoreType.DMA((3,)), pltpu.SemaphoreType.DMA((3,))],
        compiler_params=pltpu.CompilerParams(vmem_limit_bytes=VMEM_LIMIT),
    )(sv, w_mod_sh, cctx, m_cctx, v_cctx, bmod, m_bmod, v_bmod, qg, m_qg, v_qg, kvg, m_kvg, v_kvg, gf, m_gf, v_gf)


def _inproj_fwd(x2, ctx2, mod_a, w_in, qg, kvg, w_uq, w_ukv, cos_t, sin_a, sin_b):
    t_lat, t_ctx = x2.shape[0], ctx2.shape[0]
    tm = TOK_TILE
    n_lat = t_lat // tm
    n_all = n_lat + t_ctx // tm
    e_rows = t_lat + t_ctx

    def body(x_ref, ctx_ref, mod_ref, win_ref, qg_ref, kvg_ref, wuq_ref, wukv_ref, cos_ref, sa_ref, sb_ref,
             z_ref, q_ref, k_ref, v_ref, kt_ref):
        i = pl.program_id(0)
        xin = jnp.where(i < n_lat, x_ref[...], ctx_ref[...])
        xn = xin * lax.rsqrt(jnp.mean(xin * xin, axis=-1, keepdims=True) + EPS)
        h1 = (xn * (1.0 + mod_ref[0, 1:2, :]) + mod_ref[0, 0:1, :]).astype(BF16)
        z = _dot(h1, win_ref[...])
        z_ref[...] = z
        cos, sa, sb = cos_ref[...], sa_ref[...], sb_ref[...]
        cq = z[:, 0:Q_RANK]
        cqn = (cq * lax.rsqrt(jnp.mean(cq * cq, axis=-1, keepdims=True) + EPS) * qg_ref[...]).astype(BF16)
        q = _dot(cqn, wuq_ref[...])
        ckv = z[:, Q_RANK:Q_RANK + KV_RANK]
        ckvn = (ckv * lax.rsqrt(jnp.mean(ckv * ckv, axis=-1, keepdims=True) + EPS) * kvg_ref[...]).astype(BF16)
        kv = _dot(ckvn, wukv_ref[...])
        kr = _rope(z[:, Q_RANK + KV_RANK:Q_RANK + KV_RANK + HEAD_PAD], cos, sa, sb)
        ones_lane = lax.broadcasted_iota(jnp.int32, (tm, HEAD_PAD), 1) == V_DIM
        for h in range(N_HEADS):
            lo = h * HEAD_PAD
            q_ref[h] = _rope(q[:, lo:lo + HEAD_PAD], cos, sa, sb).astype(BF16)
            kh = kv[:, lo:lo + HEAD_PAD] + kr
            k_ref[h] = kh.astype(BF16)
            kt_ref[h] = kh.T.astype(BF16)
            vh = kv[:, N_HEADS * HEAD_PAD + lo:N_HEADS * HEAD_PAD + lo + HEAD_PAD]
            v_ref[h] = jnp.where(ones_lane, 1.0, vh).astype(BF16)

    row = lambda i: (i, 0)
    head_spec = pl.BlockSpec((N_HEADS, tm, HEAD_PAD), lambda i: (0, i, 0))
    head_shape = jax.ShapeDtypeStruct((N_HEADS, e_rows, HEAD_PAD), BF16)
    return pl.pallas_call(
        body, name="inproj_fwd", grid=(n_all,),
        out_shape=(jax.ShapeDtypeStruct((e_rows, Z_COLS), F32), head_shape, head_shape, head_shape,
                   jax.ShapeDtypeStruct((N_HEADS, HEAD_PAD, e_rows), BF16)),
        in_specs=[
            pl.BlockSpec((tm, D_MODEL), lambda i: (jnp.minimum(i, n_lat - 1), 0)),
            _const_spec((tm, D_MODEL)),
            pl.BlockSpec((1, 8, D_MODEL), lambda i: (i // n_lat, 0, 0)),
            _const_spec(w_in.shape), _const_spec(qg.shape), _const_spec(kvg.shape),
            _const_spec(w_uq.shape), _const_spec(w_ukv.shape),
            pl.BlockSpec((tm, HEAD_PAD), row), pl.BlockSpec((tm, HEAD_PAD), row), pl.BlockSpec((tm, HEAD_PAD), row),
        ],
        out_specs=(pl.BlockSpec((tm, Z_COLS), row), head_spec, head_spec, head_spec,
                   pl.BlockSpec((N_HEADS, HEAD_PAD, tm), lambda i: (0, 0, i))),
        compiler_params=pltpu.CompilerParams(vmem_limit_bytes=VMEM_LIMIT),
    )(x2, ctx2, mod_a, w_in, qg, kvg, w_uq, w_ukv, cos_t, sin_a, sin_b)


def _key_chunks(e_rows):
    n_chunks = max(1, e_rows // KEY_CHUNK)
    return [(ci * KEY_CHUNK, KEY_CHUNK if ci < n_chunks - 1 else e_rows - ci * KEY_CHUNK) for ci in range(n_chunks)]


def _attn_fwd(q, k, v, t_lat, shard_arrays):
    e_rows = k.shape[1]
    tq = 256
    bounds = _key_chunks(e_rows)
    c2 = ATTN_SCALE * LOG2E

    hb = ATTN_HEADS_PER_STEP
    n_hb = N_HEADS // hb

    def body(q_ref, k_ref, v_ref, o_ref, lse_ref):
        qs = [q_ref[b] for b in range(hb)]
        m, acc = [None] * hb, [None] * hb
        for lo, n in bounds:
            for b in range(hb):
                s = _dot_nt(qs[b], k_ref[b, lo:lo + n, :])
                mc = jnp.max(s, axis=-1, keepdims=True)
                m_new = mc if m[b] is None else jnp.maximum(m[b], mc)
                p = jnp.exp2((s - m_new) * c2)
                pv = _dot(p.astype(BF16), v_ref[b, lo:lo + n, :])
                acc[b] = pv if m[b] is None else acc[b] * jnp.exp2((m[b] - m_new) * c2) + pv
                m[b] = m_new
        for b in range(hb):
            l = acc[b][:, V_DIM:V_DIM + 1]
            o_ref[b] = (acc[b] * (1.0 / l)).astype(BF16)
            lse = (m[b] * ATTN_SCALE + jnp.log(l)) * LOG2E
            lse_ref[b] = jnp.broadcast_to(lse, (tq, HEAD_PAD)).T[0:1, :]

    n_w = len(shard_arrays)
    n_q = t_lat // tq

    def body_with_gather(q_ref, k_ref, v_ref, *rest):
        o_ref, lse_ref = rest[n_w], rest[n_w + 1]
        gather = _ShardGather(rest[n_w + 2:2 * n_w + 2], *rest[2 * n_w + 2:])
        h, i = pl.program_id(0), pl.program_id(1)
        pl.when(jnp.logical_and(h == 0, i == 0))(gather.start)
        pl.when(jnp.logical_and(h == n_hb // 2, i == 0))(gather.forward)
        body(q_ref, k_ref, v_ref, o_ref, lse_ref)
        pl.when(jnp.logical_and(h == n_hb - 1, i == n_q - 1))(gather.finish)

    hbm = pl.BlockSpec(memory_space=pl.ANY)
    return pl.pallas_call(
        body_with_gather, name="attn_fwd", grid=(n_hb, n_q),
        out_shape=(jax.ShapeDtypeStruct((N_HEADS, t_lat, HEAD_PAD), BF16),
                   jax.ShapeDtypeStruct((N_HEADS, 1, t_lat), F32))
        + tuple(jax.ShapeDtypeStruct(a.shape, a.dtype) for a in shard_arrays),
        in_specs=[pl.BlockSpec((hb, tq, HEAD_PAD), lambda h, i: (h, i, 0)),
                  pl.BlockSpec((hb, e_rows, HEAD_PAD), lambda h, i: (h, 0, 0)),
                  pl.BlockSpec((hb, e_rows, HEAD_PAD), lambda h, i: (h, 0, 0))] + [hbm] * n_w,
        out_specs=(pl.BlockSpec((hb, tq, HEAD_PAD), lambda h, i: (h, i, 0)),
                   pl.BlockSpec((hb, 1, tq), lambda h, i: (h, 0, i))) + (hbm,) * n_w,
        input_output_aliases={3 + a: 2 + a for a in range(n_w)},
        scratch_shapes=_gather_sems(n_w),
        compiler_params=pltpu.CompilerParams(vmem_limit_bytes=VMEM_LIMIT),
    )(q, k, v, *shard_arrays)


def _attn_bwd(q, k, v, kt, o, do, lse_row, t_lat, parts):
    e_rows = k.shape[1]
    tq = 256
    n_p = len(parts)
    n_q = t_lat // tq
    bounds = _key_chunks(e_rows)

    def body(q_ref, k_ref, v_ref, kt_ref, o_ref, do_ref, lse_ref, *rest):
        dqt_ref, dk_ref, dv_ref = rest[n_p:n_p + 3]
        scatter = _ChipScatter(rest[:n_p], rest[n_p + 3:2 * n_p + 3], rest[2 * n_p + 3], rest[2 * n_p + 4])
        h, i = pl.program_id(0), pl.program_id(1)
        pl.when(jnp.logical_and(h == 0, i == 0))(scatter.start)
        qb, dob = q_ref[0], do_ref[0]
        prod = o_ref[0].astype(F32) * dob.astype(F32)
        delta = lax.dot_general(jnp.ones((8, HEAD_PAD), F32), prod, (((1,), (1,)), ((), ())), precision=HIGHEST,
                                preferred_element_type=F32)[0:1, :]
        lse = lse_ref[0]

        @pl.when(i == 0)
        def _():
            dk_ref[...] = jnp.zeros(dk_ref.shape, F32)
            dv_ref[...] = jnp.zeros(dv_ref.shape, F32)

        dqt = None
        for lo, n in bounds:
            pt = jnp.exp2(_dot_nt(k_ref[0, lo:lo + n, :], qb) * (ATTN_SCALE * LOG2E) - lse)
            dpt = _dot_nt(v_ref[0, lo:lo + n, :], dob)
            dst = (pt * (dpt - delta)).astype(BF16)
            dv_c = _dot(pt.astype(BF16), dob)
            dk_c = _dot(dst, qb)
            part = _dot(kt_ref[0, :, lo:lo + n], dst)
            dqt = part if dqt is None else dqt + part
            dk_ref[0, lo:lo + n, :] += dk_c * ATTN_SCALE
            dv_ref[0, lo:lo + n, :] += dv_c
        dqt_ref[0] = dqt * ATTN_SCALE

        pl.when(jnp.logical_and(h == N_HEADS - 1, i == n_q - 1))(scatter.finish)

    hbm = pl.BlockSpec(memory_space=pl.ANY)
    qspec = pl.BlockSpec((1, tq, HEAD_PAD), lambda h, i: (h, i, 0))
    kspec = pl.BlockSpec((1, e_rows, HEAD_PAD), lambda h, i: (h, 0, 0))
    return pl.pallas_call(
        body, name="attn_bwd", grid=(N_HEADS, n_q),
        out_shape=(jax.ShapeDtypeStruct((N_HEADS, HEAD_PAD, t_lat), F32),
                   jax.ShapeDtypeStruct((N_HEADS, e_rows, HEAD_PAD), F32),
                   jax.ShapeDtypeStruct((N_HEADS, e_rows, HEAD_PAD), F32))
        + tuple(jax.ShapeDtypeStruct((3,) + p.shape[1:], BF16) for p in parts),
        in_specs=[qspec, kspec, kspec, pl.BlockSpec((1, HEAD_PAD, e_rows), lambda h, i: (h, 0, 0)), qspec, qspec,
                  pl.BlockSpec((1, 1, tq), lambda h, i: (h, 0, i))] + [hbm] * n_p,
        out_specs=(pl.BlockSpec((1, HEAD_PAD, tq), lambda h, i: (h, 0, i)), kspec, kspec) + (hbm,) * n_p,
        scratch_shapes=[pltpu.SemaphoreType.DMA((3 * n_p,)), pltpu.SemaphoreType.DMA((3 * n_p,))],
        compiler_params=pltpu.CompilerParams(vmem_limit_bytes=VMEM_LIMIT),
    )(q, k, v, kt, o, do, lse_row, *parts)


def _halo_specs(tm, col_block):
    per = tm // 8
    prev = pl.BlockSpec((8, CONV_W), lambda i: (jnp.maximum(i * per - 1, 0), col_block))
    nxt = pl.BlockSpec((8, CONV_W), lambda i: ((i + 1) * per, col_block))
    return prev, nxt


def _mlp_fwdbwd(o, z, x2, tgt, mod_b, gf, cw, wo_attn, wo_conv, w1, w2):
    t_lat = x2.shape[0]
    tm = TOK_TILE
    n_lat = t_lat // tm
    n_ff = D_FF // FF_CHUNK

    def body(o_ref, gb_ref, gc_ref, xi_ref, gcp_ref, xip_ref, gcn_ref, xin_ref, cw_ref, woa_ref, woc_ref,
             x_ref, t_ref, mod_ref, gf_ref, w1_ref, w2_ref,
             r_ref, da_ref, h2_ref, dy2_ref, dx1_ref, conv_ref, acc_ref, dy1_ref, do_ref, dgb_ref, dyv_ref, ra_ref):
        i = pl.program_id(0)

        @pl.when(i == 0)
        def _():
            acc_ref[...] = jnp.zeros(acc_ref.shape, F32)

        g1, sh2, sc2, g2 = mod_ref[0:1, :], mod_ref[1:2, :], mod_ref[2:3, :], mod_ref[3:4, :]
        u = gc_ref[...] * xi_ref[...]
        u_prev = jnp.where(i > 0, gcp_ref[7:8, :] * xip_ref[7:8, :], 0.0)
        u_next = jnp.where(i < n_lat - 1, gcn_ref[0:1, :] * xin_ref[0:1, :], 0.0)
        um1, up1 = _shift_rows(u, u_prev, u_next)
        yv = cw_ref[0:1, :] * um1 + cw_ref[1:2, :] * u + cw_ref[2:3, :] * up1
        gb = gb_ref[...]
        conv = (gb * yv).astype(BF16)
        conv_ref[...] = conv
        y1 = _dot(conv, woc_ref[...])
        for h in range(N_HEADS):
            y1 = y1 + _dot(o_ref[h], woa_ref[h])
        x1 = x_ref[...] + g1 * y1
        rstd2 = lax.rsqrt(jnp.mean(x1 * x1, axis=-1, keepdims=True) + EPS)
        xn1 = x1 * rstd2
        h2 = (xn1 * (1.0 + sc2) + sh2).astype(BF16)
        h2_ref[...] = h2
        y2 = jnp.zeros((tm, D_MODEL), F32)
        for jj in range(n_ff):
            ra = jnp.maximum(_dot(h2, w1_ref[jj]), 0.0)
            ra_ref[jj] = ra
            r = (ra * ra).astype(BF16)
            r_ref[:, jj * FF_CHUNK:(jj + 1) * FF_CHUNK] = r
            y2 = y2 + _dot(r, w2_ref[jj * FF_CHUNK:(jj + 1) * FF_CHUNK, :])
        x2v = x1 + g2 * y2
        rstd3 = lax.rsqrt(jnp.mean(x2v * x2v, axis=-1, keepdims=True) + EPS)
        xn3 = x2v * rstd3
        gfv = gf_ref[...]
        diff = xn3 * gfv - t_ref[...]
        loss_t = 0.5 * jnp.sum(jnp.sum(diff * diff, axis=-1, keepdims=True), axis=0, keepdims=True) * (1.0 / D_MODEL)
        dy = diff * (1.0 / D_MODEL)
        dxn3 = dy * gfv
        dx2 = rstd3 * (dxn3 - xn3 * jnp.mean(dxn3 * xn3, axis=-1, keepdims=True))
        dy2 = (dx2 * g2).astype(BF16)
        dy2_ref[...] = dy2
        dh2 = jnp.zeros((tm, D_MODEL), F32)
        for jj in range(n_ff):
            dr = _dot_nt(dy2, w2_ref[jj * FF_CHUNK:(jj + 1) * FF_CHUNK, :])
            da = (2.0 * ra_ref[jj] * dr).astype(BF16)
            da_ref[:, jj * FF_CHUNK:(jj + 1) * FF_CHUNK] = da
            dh2 = dh2 + _dot_nt(da, w1_ref[jj])
        dxn1 = dh2 * (1.0 + sc2)
        dx1 = dx2 + rstd2 * (dxn1 - xn1 * jnp.mean(dxn1 * xn1, axis=-1, keepdims=True))
        dx1_ref[...] = dx1
        dy1 = (dx1 * g1).astype(BF16)
        dy1_ref[...] = dy1
        for h in range(N_HEADS):
            do_ref[h] = _dot_nt(dy1, woa_ref[h]).astype(BF16)
        dconv = _dot_nt(dy1, woc_ref[...])
        dgb_ref[...] = dconv * yv
        dyv_ref[...] = dconv * gb
        acc_ref[5:6, :] += jnp.sum(dx1 * y1, axis=0, keepdims=True)
        acc_ref[0:1, :] += jnp.sum(dy * xn3, axis=0, keepdims=True)
        acc_ref[1:2, :] += jnp.sum(dx2 * y2, axis=0, keepdims=True)
        acc_ref[2:3, :] += jnp.sum(dh2, axis=0, keepdims=True)
        acc_ref[3:4, :] += jnp.sum(dh2 * xn1, axis=0, keepdims=True)
        acc_ref[4:5, :] += jnp.broadcast_to(loss_t, (1, D_MODEL))

    row = lambda i: (i, 0)
    gcp, gcn = _halo_specs(tm, 2)
    xip, xin = _halo_specs(tm, 3)
    tile = pl.BlockSpec((tm, D_MODEL), row)
    wide = pl.BlockSpec((tm, D_FF), row)
    half = pl.BlockSpec((tm, CONV_W), row)
    return pl.pallas_call(
        body, name="mlp_fwdbwd", grid=(n_lat,),
        out_shape=(jax.ShapeDtypeStruct((t_lat, D_FF), BF16), jax.ShapeDtypeStruct((t_lat, D_FF), BF16),
                   jax.ShapeDtypeStruct((t_lat, D_MODEL), BF16), jax.ShapeDtypeStruct((t_lat, D_MODEL), BF16),
                   jax.ShapeDtypeStruct((t_lat, D_MODEL), F32), jax.ShapeDtypeStruct((t_lat, CONV_W), BF16),
                   jax.ShapeDtypeStruct((8, D_MODEL), F32),
                   jax.ShapeDtypeStruct((t_lat, D_MODEL), BF16),
                   jax.ShapeDtypeStruct((N_HEADS, t_lat, HEAD_PAD), BF16),
                   jax.ShapeDtypeStruct((t_lat, CONV_W), F32), jax.ShapeDtypeStruct((t_lat, CONV_W), F32)),
        in_specs=[
            pl.BlockSpec((N_HEADS, tm, HEAD_PAD), lambda i: (0, i, 0)),
            pl.BlockSpec((tm, CONV_W), lambda i: (i, 1)), pl.BlockSpec((tm, CONV_W), lambda i: (i, 2)),
            pl.BlockSpec((tm, CONV_W), lambda i: (i, 3)),
            gcp, xip, gcn, xin,
            _const_spec(cw.shape), _resident_spec(wo_attn.shape), _resident_spec(wo_conv.shape),
            tile, tile, _const_spec(mod_b.shape), _const_spec(gf.shape),
            _resident_spec(w1.shape), _resident_spec(w2.shape),
        ],
        out_specs=(wide, wide, tile, tile, tile, half, _const_spec((8, D_MODEL)),
                   tile, pl.BlockSpec((N_HEADS, tm, HEAD_PAD), lambda i: (0, i, 0)), half, half),
        scratch_shapes=[pltpu.VMEM((n_ff, tm, FF_CHUNK), F32)],
        compiler_params=pltpu.CompilerParams(vmem_limit_bytes=VMEM_LIMIT),
    )(o, z, z, z, z, z, z, z, cw, wo_attn, wo_conv, x2, tgt, mod_b, gf, w1, w2)


def _inproj_bwd(x2, ctx2, mod_a, z, dyv, dgb, dx1, dqt, dk, dv, cos_t, sin_a, sin_b, w_in, w_uq, w_ukv, qg, kvg, cw):
    t_lat, t_ctx = x2.shape[0], ctx2.shape[0]
    tm = TOK_TILE
    n_lat = t_lat // tm
    n_all = n_lat + t_ctx // tm
    group = max(g for g in (1, 2, 4) if n_lat % g == 0)

    def body(x_ref, ctx_ref, mod_ref, z_ref, gcp_ref, xip_ref, gcn_ref, xin_ref, dyv_ref, dyvp_ref, dyvn_ref,
             dgb_ref, dx1_ref, dqt_ref, dk_ref, dv_ref, cos_ref, sa_ref, sb_ref, win_ref, wuq_ref, wukv_ref,
             qg_ref, kvg_ref, cw_ref, gx_ref, dwin_ref, dwuq_ref, dwukv_ref, acc_ref, h1_buf, dz_buf):
        i = pl.program_id(0)
        lat = i < n_lat

        @pl.when(i == 0)
        def _():
            dwin_ref[...] = jnp.zeros(dwin_ref.shape, F32)
            dwuq_ref[...] = jnp.zeros(dwuq_ref.shape, F32)
            dwukv_ref[...] = jnp.zeros(dwukv_ref.shape, F32)
            acc_ref[...] = jnp.zeros(acc_ref.shape, F32)

        xin = jnp.where(lat, x_ref[...], ctx_ref[...])
        rstd = lax.rsqrt(jnp.mean(xin * xin, axis=-1, keepdims=True) + EPS)
        xn = xin * rstd
        sc = mod_ref[0, 1:2, :]
        h1 = (xn * (1.0 + sc) + mod_ref[0, 0:1, :]).astype(BF16)
        z = z_ref[...]
        cos, sa, sb = cos_ref[...], sa_ref[...], sb_ref[...]
        qgv, kvgv = qg_ref[...], kvg_ref[...]
        cq = z[:, 0:Q_RANK]
        cqh = cq * lax.rsqrt(jnp.mean(cq * cq, axis=-1, keepdims=True) + EPS)
        rq = lax.rsqrt(jnp.mean(cq * cq, axis=-1, keepdims=True) + EPS)
        cqn = (cqh * qgv).astype(BF16)
        parts = []
        for h in range(N_HEADS):
            g = jnp.where(lat, dqt_ref[h].T, 0.0)
            parts.append(_unrope(g, cos, sa, sb))
        dq = jnp.concatenate(parts, axis=1).astype(BF16)
        dcqn = _dot_nt(dq, wuq_ref[...])
        dwuq_ref[...] += _dot_tn(cqn, dq)
        acc_ref[4:5, 0:Q_RANK] += jnp.sum(dcqn * cqh, axis=0, keepdims=True)
        dxn = dcqn * qgv
        dcq = rq * (dxn - cqh * jnp.mean(dxn * cqh, axis=-1, keepdims=True))
        ckv = z[:, Q_RANK:Q_RANK + KV_RANK]
        rk = lax.rsqrt(jnp.mean(ckv * ckv, axis=-1, keepdims=True) + EPS)
        ckvh = ckv * rk
        ckvn = (ckvh * kvgv).astype(BF16)
        dks = [dk_ref[h] for h in range(N_HEADS)]
        dkr = dks[0]
        for h in range(1, N_HEADS):
            dkr = dkr + dks[h]
        dkv = jnp.concatenate(dks + [dv_ref[h] for h in range(N_HEADS)], axis=1).astype(BF16)
        dckvn = _dot_nt(dkv, wukv_ref[...])
        dwukv_ref[...] += _dot_tn(ckvn, dkv)
        acc_ref[5:6, 0:KV_RANK] += jnp.sum(dckvn * ckvh, axis=0, keepdims=True)
        dxn = dckvn * kvgv
        dckv = rk * (dxn - ckvh * jnp.mean(dxn * ckvh, axis=-1, keepdims=True))
        dkr = _unrope(dkr, cos, sa, sb)
        gb, gc, xi = z[:, 512:1024], z[:, 1024:1536], z[:, 1536:2048]
        u = gc * xi
        u_prev = jnp.where(i > 0, gcp_ref[7:8, :] * xip_ref[7:8, :], 0.0)
        u_next = jnp.where(i < n_lat - 1, gcn_ref[0:1, :] * xin_ref[0:1, :], 0.0)
        um1, up1 = _shift_rows(u, u_prev, u_next)
        dyv = jnp.where(lat, dyv_ref[...], 0.0)
        dyv_prev = jnp.where(jnp.logical_and(i > 0, lat), dyvp_ref[7:8, :], 0.0)
        dyv_next = jnp.where(i < n_lat - 1, dyvn_ref[0:1, :], 0.0)
        dyv_m1, dyv_p1 = _shift_rows(dyv, dyv_prev, dyv_next)
        du = cw_ref[0:1, :] * dyv_p1 + cw_ref[1:2, :] * dyv + cw_ref[2:3, :] * dyv_m1
        dgc = du * xi
        dxi = du * gc
        dgb = jnp.where(lat, dgb_ref[...], 0.0)
        acc_ref[6:7, 0:CONV_W] += jnp.sum(dyv * um1, axis=0, keepdims=True)
        acc_ref[7:8, 0:CONV_W] += jnp.sum(dyv * u, axis=0, keepdims=True)
        acc_ref[8:9, 0:CONV_W] += jnp.sum(dyv * up1, axis=0, keepdims=True)
        dz = jnp.concatenate([dcq, dckv, dkr, dgb, dgc, dxi], axis=1).astype(BF16)
        dh1 = _dot_nt(dz, win_ref[...])
        slot = i % group
        rows_g = pl.ds(pl.multiple_of(slot * tm, tm), tm)
        h1_buf[rows_g, :] = h1
        dz_buf[rows_g, :] = dz

        @pl.when(jnp.logical_and(lat, slot == group - 1))
        def _():
            dwin_ref[...] += _dot_tn(h1_buf[...], dz_buf[...])

        @pl.when(jnp.logical_not(lat))
        def _():
            dwin_ref[...] += _dot_tn(h1, dz)
        s_sh = jnp.sum(dh1, axis=0, keepdims=True)
        s_sc = jnp.sum(dh1 * xn, axis=0, keepdims=True)
        zero = jnp.zeros_like(s_sh)
        acc_ref[0:1, :] += jnp.where(lat, s_sh, zero)
        acc_ref[1:2, :] += jnp.where(lat, s_sc, zero)
        acc_ref[2:3, :] += jnp.where(lat, zero, s_sh)
        acc_ref[3:4, :] += jnp.where(lat, zero, s_sc)
        dxn = dh1 * (1.0 + sc)
        dx = rstd * (dxn - xn * jnp.mean(dxn * xn, axis=-1, keepdims=True))

        @pl.when(lat)
        def _():
            gx_ref[...] = dx1_ref[...] + dx

    last = n_lat - 1
    per = tm // 8
    lat_row = lambda i: (jnp.minimum(i, last), 0)
    row = lambda i: (i, 0)
    gcp, gcn = _halo_specs(tm, 2)
    xip, xin = _halo_specs(tm, 3)
    n_halo = t_lat // 8
    dyvp = pl.BlockSpec((8, CONV_W), lambda i: (jnp.clip(i * per - 1, 0, n_halo - 1), 0))
    dyvn = pl.BlockSpec((8, CONV_W), lambda i: (jnp.minimum((i + 1) * per, n_halo - 1), 0))
    gcn = pl.BlockSpec((8, CONV_W), lambda i: (jnp.minimum((i + 1) * per, (t_lat + t_ctx) // 8 - 1), 2))
    xin = pl.BlockSpec((8, CONV_W), lambda i: (jnp.minimum((i + 1) * per, (t_lat + t_ctx) // 8 - 1), 3))
    head_f32 = pl.BlockSpec((N_HEADS, tm, HEAD_PAD), lambda i: (0, i, 0))
    tab = pl.BlockSpec((tm, HEAD_PAD), row)
    return pl.pallas_call(
        body, name="inproj_bwd", grid=(n_all,),
        out_shape=(jax.ShapeDtypeStruct((t_lat, D_MODEL), F32), jax.ShapeDtypeStruct(w_in.shape, F32),
                   jax.ShapeDtypeStruct(w_uq.shape, F32), jax.ShapeDtypeStruct(w_ukv.shape, F32),
                   jax.ShapeDtypeStruct((16, D_MODEL), F32)),
        in_specs=[
            pl.BlockSpec((tm, D_MODEL), lat_row), _const_spec((tm, D_MODEL)),
            pl.BlockSpec((1, 8, D_MODEL), lambda i: (i // n_lat, 0, 0)),
            pl.BlockSpec((tm, Z_COLS), row), gcp, xip, gcn, xin,
            pl.BlockSpec((tm, CONV_W), lat_row), dyvp, dyvn,
            pl.BlockSpec((tm, CONV_W), lat_row), pl.BlockSpec((tm, D_MODEL), lat_row),
            pl.BlockSpec((N_HEADS, HEAD_PAD, tm), lambda i: (0, 0, jnp.minimum(i, last))),
            head_f32, head_f32, tab, tab, tab,
            _const_spec(w_in.shape), _const_spec(w_uq.shape), _const_spec(w_ukv.shape),
            _const_spec(qg.shape), _const_spec(kvg.shape), _const_spec(cw.shape),
        ],
        out_specs=(pl.BlockSpec((tm, D_MODEL), lat_row), _const_spec(w_in.shape), _const_spec(w_uq.shape),
                   _const_spec(w_ukv.shape), _const_spec((16, D_MODEL))),
        scratch_shapes=[pltpu.VMEM((group * tm, D_MODEL), BF16), pltpu.VMEM((group * tm, Z_COLS), BF16)],
        compiler_params=pltpu.CompilerParams(vmem_limit_bytes=VMEM_LIMIT),
    )(x2, ctx2, mod_a, z, z, z, z, z, dyv, dyv, dyv, dgb, dx1, dqt, dk, dv, cos_t, sin_a, sin_b, w_in, w_uq, w_ukv,
      qg, kvg, cw)


def _wgrad(a, b, name, bm, bn):
    t, m = a.shape
    n = b.shape[1]
    bk = min(t, 4096)
    nk = t // bk
    nj = n // bn

    def body(a_ref, b_ref, o_ref, acc_ref):
        k = pl.program_id(2)
        part = _dot_tn(a_ref[...], b_ref[...])

        @pl.when(k == 0)
        def _():
            acc_ref[...] = part

        @pl.when(k > 0)
        def _():
            acc_ref[...] += part

        @pl.when(k == nk - 1)
        def _():
            o_ref[...] = acc_ref[...].astype(BF16)

    return pl.pallas_call(
        body, name=name, grid=(m // bm, nj, nk), out_shape=jax.ShapeDtypeStruct((m // bm * nj, bm, bn), BF16),
        in_specs=[pl.BlockSpec((bk, bm), lambda i, j, k: (k, i)), pl.BlockSpec((bk, bn), lambda i, j, k: (k, j))],
        out_specs=pl.BlockSpec((None, bm, bn), lambda i, j, k: (i * nj + j, 0, 0)),
        scratch_shapes=[pltpu.VMEM((bm, bn), F32)],
        compiler_params=pltpu.CompilerParams(vmem_limit_bytes=VMEM_LIMIT),
    )(a, b)


def _wgrad_out(o, conv, dy1):
    t = o.shape[1]
    bk = min(t, 2048)
    nk = t // bk
    rows = N_HEADS * HEAD_PAD + CONV_W

    def body(o_ref, c_ref, d_ref, w_ref, acc_ref):
        k = pl.program_id(0)
        cat = jnp.concatenate([o_ref[h] for h in range(N_HEADS)] + [c_ref[...]], axis=1)
        part = _dot_tn(cat, d_ref[...])

        @pl.when(k == 0)
        def _():
            acc_ref[...] = part

        @pl.when(k > 0)
        def _():
            acc_ref[...] += part

        @pl.when(k == nk - 1)
        def _():
            for h in range(N_HEADS):
                w_ref[h * V_DIM:(h + 1) * V_DIM, :] = acc_ref[h * HEAD_PAD:h * HEAD_PAD + V_DIM, :].astype(BF16)
            w_ref[N_HEADS * V_DIM:, :] = acc_ref[N_HEADS * HEAD_PAD:, :].astype(BF16)

    return pl.pallas_call(
        body, name="wgrad_out", grid=(nk,),
        out_shape=jax.ShapeDtypeStruct((D_MODEL, D_MODEL), BF16),
        in_specs=[pl.BlockSpec((N_HEADS, bk, HEAD_PAD), lambda k: (0, k, 0)),
                  pl.BlockSpec((bk, CONV_W), lambda k: (k, 0)),
                  pl.BlockSpec((bk, D_MODEL), lambda k: (k, 0))],
        out_specs=_const_spec((D_MODEL, D_MODEL)),
        scratch_shapes=[pltpu.VMEM((rows, D_MODEL), F32)],
        compiler_params=pltpu.CompilerParams(vmem_limit_bytes=VMEM_LIMIT),
    )(o, conv, dy1)


def _adamw_call(w, g, m, v, name):
    rows, cols = w.shape
    rb = 256 if rows % 256 == 0 else rows

    def body(w_ref, g_ref, m_ref, v_ref, d_ref, nm_ref, nv_ref):
        d_, m_, v_ = _adamw(w_ref[...], g_ref[...], m_ref[...], v_ref[...])
        d_ref[...] = d_
        nm_ref[...] = m_
        nv_ref[...] = v_

    spec = pl.BlockSpec((rb, cols), lambda i: (i, 0))
    shp = jax.ShapeDtypeStruct((rows, cols), F32)
    return pl.pallas_call(
        body, name=name, grid=(rows // rb,), out_shape=(shp, shp, shp),
        in_specs=[spec] * 4, out_specs=(spec, spec, spec),
    )(w, g, m, v)


def _wmod_update(s_t, dm, w, m, v):
    rows, cols = w.shape
    cb = 512

    def body(s_ref, dm_ref, w_ref, m_ref, v_ref, g_ref, d_ref, nm_ref, nv_ref):
        g = jnp.dot(s_ref[...], dm_ref[...], precision=HIGHEST, preferred_element_type=F32)
        d_, m_, v_ = _adamw(w_ref[...], g, m_ref[...], v_ref[...])
        g_ref[...] = g
        d_ref[...] = d_
        nm_ref[...] = m_
        nv_ref[...] = v_

    spec = pl.BlockSpec((rows, cb), lambda i: (0, i))
    shp = jax.ShapeDtypeStruct((rows, cols), F32)
    return pl.pallas_call(
        body, name="wmod_update", grid=(cols // cb,), out_shape=(shp, shp, shp, shp),
        in_specs=[_const_spec(s_t.shape), pl.BlockSpec((16, cb), lambda i: (0, i)), spec, spec, spec],
        out_specs=(spec, spec, spec, spec),
        compiler_params=pltpu.CompilerParams(vmem_limit_bytes=VMEM_LIMIT),
    )(s_t, dm, w, m, v)


def _rope_tables(t_lat, t_ctx):
    t = jnp.arange(t_lat)
    pos = jnp.stack([(t // GRID_W).astype(F32), (t % GRID_W).astype(F32)], axis=1)
    half = QK_ROPE // 4
    freqs = ROPE_THETA ** (-jnp.arange(0, 2 * half, 2, dtype=F32) / (2 * half))
    ang = pos[:, :, None] * freqs[None, None, :]
    cos, sin = jnp.cos(ang), jnp.sin(ang)
    zero = jnp.zeros_like(sin)
    cos32 = jnp.concatenate([cos, cos], axis=2).reshape(t_lat, QK_ROPE)
    sa32 = jnp.concatenate([zero, sin], axis=2).reshape(t_lat, QK_ROPE)
    sb32 = jnp.concatenate([-sin, zero], axis=2).reshape(t_lat, QK_ROPE)

    def widen(tab, fill):
        left = jnp.full((t_lat, ROPE_LANE0), fill, F32)
        right = jnp.full((t_lat, HEAD_PAD - ROPE_LANE0 - QK_ROPE), fill, F32)
        lat = jnp.concatenate([left, tab, right], axis=1)
        return jnp.concatenate([lat, jnp.full((t_ctx, HEAD_PAD), fill, F32)], axis=0)

    return widen(cos32, 1.0), widen(sa32, 0.0), widen(sb32, 0.0)


def _cols_from_shards(s):
    return jnp.transpose(s, (1, 0, 2)).reshape(s.shape[1], -1)


def _cols_to_shards(w):
    k, n = w.shape
    return jnp.transpose(w.reshape(k, N_SHARD, n // N_SHARD), (1, 0, 2))


def kernel(x, c, ctx, c_ctx, w_mod, b_mod, w_in, q_norm_g, w_uq, kv_norm_g, w_ukv, conv_w, w_out, w_mlp1, w_mlp2, final_norm_g, loss_target, m_c_ctx, m_w_mod, m_b_mod, m_w_in, m_q_norm_g, m_w_uq, m_kv_norm_g, m_w_ukv, m_conv_w, m_w_out, m_w_mlp1, m_w_mlp2, m_final_norm_g, v_c_ctx, v_w_mod, v_b_mod, v_w_in, v_q_norm_g, v_w_uq, v_kv_norm_g, v_w_ukv, v_conv_w, v_w_out, v_w_mlp1, v_w_mlp2, v_final_norm_g):
    t_lat, t_ctx = x.shape[1], ctx.shape[1]
    assert t_ctx == TOK_TILE and t_lat % TOK_TILE == 0 and t_lat % GRID_W == 0
    mx, my, mc = _pos()
    me = 4 * mx + 2 * my + mc
    j = 2 * mx + my
    ncol = w_mod.shape[2]
    x2, ctx2, tgt = x[0], ctx[0], loss_target[0]
    cctx_row = c_ctx.reshape(1, D_MODEL)

    b_sh = lax.dynamic_slice(b_mod, (0, j * ncol), (1, ncol))
    cw_pad = jnp.zeros((8, 128), F32).at[0:3, :].set(conv_w[0])
    c8, m_all = _mod_exchange(c, cctx_row, w_mod[0], b_sh, cw_pad)
    mvec = m_all[:, 0, :].reshape(6, D_MODEL)
    mctx = m_all[:, 8, :].reshape(6, D_MODEL)
    zeros6 = jnp.zeros((6, D_MODEL), F32)
    mod_a = jnp.stack([jnp.concatenate([mvec[0:2], zeros6], axis=0), jnp.concatenate([mctx[0:2], zeros6], axis=0)])
    mod_b = jnp.concatenate([mvec[2:6], jnp.zeros((4, D_MODEL), F32)], axis=0)
    cw_full = jnp.pad(jnp.transpose(m_all[:, 9:12, 0:128], (1, 0, 2)).reshape(3, CONV_W), ((0, 5), (0, 0)))

    g_in, g_uq, g_ukv, g_out, g_m1, g_m2 = _weight_prep(
        (w_in[0], w_uq[0], w_ukv[0], w_out[0], w_mlp1[0], w_mlp2[0]), 3)
    w_in_f = _cols_from_shards(g_in)
    zc = lambda n: jnp.zeros((D_MODEL, n), BF16)
    w_in_p = jnp.concatenate([w_in_f[:, 0:384], zc(64), w_in_f[:, 384:416], zc(32), w_in_f[:, 416:]], axis=1)
    w_uq_f = _cols_from_shards(g_uq).reshape(Q_RANK, N_HEADS, QK_DIM)
    w_uq_p = jnp.pad(w_uq_f, ((0, 0), (0, 0), (0, HEAD_PAD - QK_DIM))).reshape(Q_RANK, N_HEADS * HEAD_PAD)
    w_ukv_f = _cols_from_shards(g_ukv).reshape(KV_RANK, N_HEADS, QK_NOPE + V_DIM)
    padh = lambda a: jnp.pad(a, ((0, 0), (0, 0), (0, HEAD_PAD - a.shape[2]))).reshape(KV_RANK, N_HEADS * HEAD_PAD)
    w_ukv_p = jnp.concatenate([padh(w_ukv_f[:, :, :QK_NOPE]), padh(w_ukv_f[:, :, QK_NOPE:])], axis=1)
    cos_t, sin_a, sin_b = _rope_tables(t_lat, t_ctx)
    gf_row = final_norm_g.reshape(1, D_MODEL)
    c_idx = mc.reshape(1).astype(jnp.int32)
    j_idx = j.reshape(1).astype(jnp.int32)

    z, q, k, v, kt = _inproj_fwd(x2, ctx2, mod_a, w_in_p, q_norm_g, kv_norm_g, w_uq_p, w_ukv_p, cos_t, sin_a, sin_b)
    o, lse, g_out, w1, g_m2 = _attn_fwd(q, k, v, t_lat, (g_out, g_m1, g_m2))
    w_out_f = g_out.reshape(D_MODEL, D_MODEL)
    wo_attn = jnp.pad(w_out_f[0:512].reshape(N_HEADS, V_DIM, D_MODEL), ((0, 0), (0, HEAD_PAD - V_DIM), (0, 0)))
    wo_conv = w_out_f[512:]
    w2 = g_m2.reshape(D_FF, D_MODEL)
    r, da, h2, dy2, dx1, conv, acc_mlp, dy1, do, dgb, dyv = _mlp_fwdbwd(o, z, x2, tgt, mod_b, gf_row, cw_full, wo_attn,
                                                                         wo_conv, w1, w2)
    d_w1 = _wgrad(h2, da, "wgrad_mlp1", D_MODEL, FF_CHUNK)
    d_w2 = _wgrad(r, dy2, "wgrad_mlp2", FF_CHUNK, D_MODEL)
    d_wout = _wgrad_out(o, conv, dy1).reshape(N_SHARD, D_MODEL // N_SHARD, D_MODEL)
    big_grads, big_names = (d_w1, d_w2, d_wout), ("mlp1", "mlp2", "out")
    big_got = _rs_sibling(big_grads, "rs_sibling_big")
    big_parts = [_add_pairs(a, g, c_idx, "rs_add_pairs_" + n) for a, g, n in zip(big_grads, big_got, big_names)]
    dqt, dk, dv, *big_recv = _attn_bwd(q, k, v, kt, o, do, lse, t_lat, big_parts)
    big_halves = [_add_chips(p, g, j_idx, "rs_add_chips_" + n) for p, g, n in zip(big_parts, big_recv, big_names)]
    g_w1, g_w2, g_wout = _rs_join(big_halves, "rs_join_big")
    gx, d_win, d_wuq, d_wukv, acc_in = _inproj_bwd(x2, ctx2, mod_a, z, dyv, dgb, dx1, dqt, dk, dv, cos_t, sin_a, sin_b,
                                                   w_in_p, w_uq_p, w_ukv_p, q_norm_g, kv_norm_g, cw_full)

    pad_row = lambda a: jnp.pad(a, ((0, 0), (0, D_MODEL - a.shape[1])))
    sv = jnp.concatenate([
        acc_in[0:2], acc_mlp[5:6], acc_mlp[2:4], acc_mlp[1:2],
        acc_in[2:4], acc_mlp[0:1], acc_in[4:5], acc_in[5:6], acc_in[6:9], acc_mlp[4:5],
        jnp.zeros((1, D_MODEL), F32)], axis=0)
    all_sv, red, o_cc, o_b, o_q, o_k, o_gf = _small_exchange(
        sv, w_mod[0], cctx_row, m_c_ctx.reshape(1, D_MODEL), v_c_ctx.reshape(1, D_MODEL),
        b_mod.reshape(6, D_MODEL), m_b_mod.reshape(6, D_MODEL), v_b_mod.reshape(6, D_MODEL),
        q_norm_g, m_q_norm_g, v_q_norm_g, kv_norm_g, m_kv_norm_g, v_kv_norm_g,
        gf_row, m_final_norm_g.reshape(1, D_MODEL), v_final_norm_g.reshape(1, D_MODEL))
    loss = red[14, 0]

    c9 = jnp.concatenate([c8[0::8], jnp.zeros((7, D_MODEL), F32)], axis=0)
    s_t = jnp.transpose(c9 * jax.nn.sigmoid(c9))
    dm_ex = all_sv[:, 0:6, :].reshape(8, 6 * D_MODEL)
    dm_ctx = jnp.concatenate([red[6:8].reshape(1, 2 * D_MODEL), jnp.zeros((1, 4 * D_MODEL), F32)], axis=1)
    dm16 = jnp.concatenate([dm_ex, dm_ctx, jnp.zeros((7, 6 * D_MODEL), F32)], axis=0)
    dm_sh = lax.dynamic_slice(dm16, (0, j * ncol), (16, ncol))
    g_wmod, d_wmod, nm_wmod, nv_wmod = _wmod_update(s_t, dm_sh, w_mod[0], m_w_mod[0], v_w_mod[0])

    g_cw = lax.dynamic_slice(red[11:14, 0:CONV_W], (0, j * 128), (3, 128))
    d_cw, nm_cw, nv_cw = _adamw_call(conv_w[0], g_cw, m_conv_w[0], v_conv_w[0], "adamw_conv")

    d_win_f = jnp.concatenate([d_win[:, 0:384], d_win[:, 448:480], d_win[:, 512:]], axis=1)
    d_wuq_f = d_wuq.reshape(Q_RANK, N_HEADS, HEAD_PAD)[:, :, 0:QK_DIM].reshape(Q_RANK, N_HEADS * QK_DIM)
    d_wukv3 = d_wukv.reshape(KV_RANK, 2, N_HEADS, HEAD_PAD)
    d_wukv_f = jnp.concatenate([d_wukv3[:, 0, :, 0:QK_NOPE], d_wukv3[:, 1, :, 0:V_DIM]], axis=2).reshape(KV_RANK, -1)
    rest = tuple(_cols_to_shards(a).astype(BF16) for a in (d_win_f, d_wuq_f, d_wukv_f))
    rest_names = ("in", "uq", "ukv")
    rest_got = _rs_sibling(rest, "rs_sibling_rest")
    rest_parts = [_add_pairs(a, g, c_idx, "rs_add_pairs_" + n) for a, g, n in zip(rest, rest_got, rest_names)]
    rest_recv = _rs_chips(rest_parts, "rs_chips_rest")
    rest_halves = [_add_chips(p, g, j_idx, "rs_add_chips_" + n) for p, g, n in zip(rest_parts, rest_recv, rest_names)]
    g_win, g_wuq, g_wukv = _rs_join(rest_halves, "rs_join_rest")
    upd = {}
    for name, w_, g_, m_, v_ in (("in", w_in, g_win, m_w_in, v_w_in), ("uq", w_uq, g_wuq, m_w_uq, v_w_uq),
                                 ("ukv", w_ukv, g_wukv, m_w_ukv, v_w_ukv), ("out", w_out, g_wout, m_w_out, v_w_out),
                                 ("mlp1", w_mlp1, g_w1, m_w_mlp1, v_w_mlp1), ("mlp2", w_mlp2, g_w2, m_w_mlp2, v_w_mlp2)):
        upd[name] = _adamw_call(w_[0], g_, m_[0], v_[0], "adamw_" + name)

    def four(o4, shape):
        return [o4[r].reshape(shape) for r in range(4)]

    cc4 = four(o_cc, (D_MODEL,))
    b4 = [o_b[r].reshape(1, 6 * D_MODEL) for r in range(4)]
    q4 = four(o_q, (1, Q_RANK))
    k4 = four(o_k, (1, KV_RANK))
    gf4 = four(o_gf, (D_MODEL,))
    big = {"in": g_win, "uq": g_wuq, "ukv": g_wukv, "out": g_wout, "mlp1": g_w1, "mlp2": g_w2}

    def leaf(idx):
        wm = (g_wmod, d_wmod, nm_wmod, nv_wmod)[idx]
        cwv = (g_cw, d_cw, nm_cw, nv_cw)[idx]
        bigv = {n: (big[n] if idx == 0 else upd[n][idx - 1]) for n in big}
        return [cc4[idx], wm[None], b4[idx], bigv["in"][None], q4[idx], bigv["uq"][None], k4[idx], bigv["ukv"][None],
                cwv[None], bigv["out"][None], bigv["mlp1"][None], bigv["mlp2"][None], gf4[idx]]

    return (loss, gx[None], *leaf(0), *leaf(1), *leaf(2), *leaf(3))
```

```python
import functools
import math

import jax
import jax.numpy as jnp
from jax import lax
from jax.experimental import pallas as pl
from jax.experimental.pallas import tpu as pltpu

F32 = jnp.float32
BF16 = jnp.bfloat16
MESH = pl.DeviceIdType.MESH
HIGHEST = lax.Precision.HIGHEST

D_MODEL = 1024
N_HEADS = 8
QK_NOPE = 64
QK_ROPE = 32
QK_DIM = QK_NOPE + QK_ROPE
V_DIM = 64
Q_RANK = 256
KV_RANK = 128
CONV_W = 512
D_FF = 4096
GRID_W = 64
ROPE_THETA = 10000.0
EPS = 1e-6
ATTN_SCALE = 1.0 / math.sqrt(QK_DIM)
HEAD_PAD = 128
Z_COLS = 2048
ROPE_LANE0 = QK_NOPE
N_SHARD = 4
TOK_TILE = 256
FF_CHUNK = 1024
KEY_CHUNK = 1024
ATTN_HEADS_PER_STEP = 4
ATTN_BWD_HEADS_PER_STEP = 2

ADAM_LR = 0.001
ADAM_B1 = 0.9
ADAM_B2 = 0.999
ADAM_EPS = 1e-08
ADAM_WD = 0.01
ADAM_STEP = 10

LOG2E = 1.4426950408889634

VMEM_LIMIT = 56 * 1024 * 1024
STAGE_VMEM_LIMIT = 32 * 1024 * 1024


def _pos():
    return lax.axis_index("x"), lax.axis_index("y"), lax.axis_index("c")


def _dot(a, b):
    return jnp.dot(a, b, preferred_element_type=F32)


def _dot_nt(a, b):
    return lax.dot_general(a, b, (((1,), (1,)), ((), ())), preferred_element_type=F32)


def _dot_tn(a, b):
    return lax.dot_general(a, b, (((0,), (0,)), ((), ())), preferred_element_type=F32)


def _rope(v, cos, sa, sb):
    return v * cos + pltpu.roll(v, 8, 1) * sa + pltpu.roll(v, HEAD_PAD - 8, 1) * sb


def _unrope(g, cos, sa, sb):
    return g * cos + pltpu.roll(g * sa, HEAD_PAD - 8, 1) + pltpu.roll(g * sb, 8, 1)


def _sigmoid(v):
    return 1.0 / (1.0 + jnp.exp(-v))


def _adamw(w, g, m, v):
    m = ADAM_B1 * m + (1.0 - ADAM_B1) * g
    v = ADAM_B2 * v + (1.0 - ADAM_B2) * (g * g)
    m_hat = m / (1.0 - ADAM_B1 ** ADAM_STEP)
    v_hat = v / (1.0 - ADAM_B2 ** ADAM_STEP)
    delta = -ADAM_LR * (m_hat / (jnp.sqrt(v_hat) + ADAM_EPS) + ADAM_WD * w)
    return delta, m, v


def _shift_rows(u, prev_row, next_row):
    n = u.shape[0]
    rows = lax.broadcasted_iota(jnp.int32, u.shape, 0)
    um1 = jnp.where(rows == 0, prev_row, pltpu.roll(u, 1, 0))
    up1 = jnp.where(rows == n - 1, next_row, pltpu.roll(u, n - 1, 0))
    return um1, up1


def _const_spec(shape):
    nd = len(shape)
    return pl.BlockSpec(shape, lambda *_: (0,) * nd)


def _resident_spec(shape):
    nd = len(shape)
    return pl.BlockSpec(shape, lambda *_: (0,) * nd, pipeline_mode=pl.Buffered(1))


def _peer(r, x, y, c):
    px = 1 - x if r & 4 else x
    py = 1 - y if r & 2 else y
    pc = 1 - c if r & 1 else c
    return (px, py, pc)


def _mod_exchange(c_row, cctx_row, w_mod_sh, b_sh, cw_sh):
    ncol = w_mod_sh.shape[1]

    def body(c_ref, cctx_ref, w_ref, b_ref, cw_ref, c8_ref, m_ref, mine_ref, msh_ref, ssem, rsem, ssem2, rsem2):
        x, y, c = _pos()
        me = 4 * x + 2 * y + c
        j = 2 * x + y
        mine_ref[...] = jnp.zeros(mine_ref.shape, F32)
        mine_ref[0:1, :] = c_ref[...]
        my_rows = pl.ds(pl.multiple_of(8 * me, 8), 8)
        sends = []
        for r in range(1, 8):
            cp = pltpu.make_async_remote_copy(
                src_ref=mine_ref, dst_ref=c8_ref.at[my_rows], send_sem=ssem.at[r - 1], recv_sem=rsem.at[r - 1],
                device_id=_peer(r, x, y, c), device_id_type=MESH)
            cp.start()
            sends.append(cp)
        for cp in sends:
            cp.wait()
        c8_ref[my_rows, :] = mine_ref[...]
        c8_ref[64:72, :] = jnp.zeros((8, D_MODEL), F32)
        c8_ref[64:65, :] = cctx_ref[...]
        cv = c8_ref[...]
        s = cv * _sigmoid(cv)
        m = jnp.dot(s, w_ref[...], precision=HIGHEST, preferred_element_type=F32) + b_ref[...]
        msh_ref[0:64, :] = m[0:64, :]
        msh_ref[64:72, :] = jnp.zeros((8, ncol), F32)
        msh_ref[64:65, :] = m[64:65, :]
        msh_ref[65:68, 0:128] = cw_ref[0:3, :]
        m_ref[j, 0:8, :] = msh_ref[my_rows, :]
        m_ref[j, 8:16, :] = msh_ref[64:72, :]
        sends2 = []
        for k, (px, py) in enumerate(_chips(x, y)):
            theirs = pl.ds(pl.multiple_of(8 * (4 * px + 2 * py + c), 8), 8)
            for half, src in enumerate((msh_ref.at[theirs], msh_ref.at[64:72])):
                cp = pltpu.make_async_remote_copy(
                    src_ref=src, dst_ref=m_ref.at[j, 8 * half:8 * half + 8], send_sem=ssem2.at[2 * k + half],
                    recv_sem=rsem2.at[2 * k + half], device_id=(px, py, c), device_id_type=MESH)
                cp.start()
                sends2.append(cp)
        for cp in sends2:
            cp.wait()

    vm = pl.BlockSpec(memory_space=pltpu.VMEM)
    return pl.pallas_call(
        body, name="mod_exchange",
        out_shape=(jax.ShapeDtypeStruct((72, D_MODEL), F32), jax.ShapeDtypeStruct((N_SHARD, 16, ncol), F32)),
        in_specs=[vm] * 5, out_specs=(vm, vm),
        scratch_shapes=[pltpu.VMEM((8, D_MODEL), F32), pltpu.VMEM((72, ncol), F32),
                        pltpu.SemaphoreType.DMA((7,)), pltpu.SemaphoreType.DMA((7,)),
                        pltpu.SemaphoreType.DMA((6,)), pltpu.SemaphoreType.DMA((6,))],
        compiler_params=pltpu.CompilerParams(vmem_limit_bytes=VMEM_LIMIT),
    )(c_row, cctx_row, w_mod_sh, b_sh, cw_sh)


def _chips(x, y):
    return [(1 - x, y), (x, 1 - y), (1 - x, 1 - y)]


def _halves(ref, c, align):
    hr = ref.shape[-2] // 2
    return (pl.ds(pl.multiple_of(c * hr, align), hr), pl.ds(pl.multiple_of((1 - c) * hr, align), hr))


class _ShardGather:
    def __init__(self, refs, ssem, rsem, fsend, frecv):
        self.refs, self.sems = refs, (ssem, rsem, fsend, frecv)
        self.x, self.y, self.c = _pos()
        self.j = 2 * self.x + self.y

    def _ici(self, a, k, slot):
        g = self.refs[a]
        ssem, rsem, _, _ = self.sems
        mine, _ = _halves(g, self.c, 16)
        px, py = _chips(self.x, self.y)[k]
        return pltpu.make_async_remote_copy(
            src_ref=g.at[self.j, mine], dst_ref=g.at[slot, mine], send_sem=ssem.at[3 * a + k],
            recv_sem=rsem.at[3 * a + k], device_id=(px, py, self.c), device_id_type=MESH)

    def _d2d(self, a, k, to_other_half):
        g = self.refs[a]
        _, _, fsend, frecv = self.sems
        mine, theirs = _halves(g, self.c, 16)
        px, py = _chips(self.x, self.y)[k]
        jk = 2 * px + py
        return pltpu.make_async_remote_copy(
            src_ref=g.at[jk, mine], dst_ref=g.at[jk, theirs if to_other_half else mine],
            send_sem=fsend.at[3 * a + k], recv_sem=frecv.at[3 * a + k],
            device_id=(self.x, self.y, 1 - self.c), device_id_type=MESH)

    def start(self):
        for a in range(len(self.refs)):
            for k in range(3):
                self._ici(a, k, self.j).start()

    def forward(self):
        for a in range(len(self.refs)):
            for k, (px, py) in enumerate(_chips(self.x, self.y)):
                self._ici(a, k, 2 * px + py).wait_recv()
                self._d2d(a, k, False).start()

    def finish(self):
        for a in range(len(self.refs)):
            for k in range(3):
                self._d2d(a, k, True).wait()
                self._ici(a, k, self.j).wait_send()


def _gather_sems(n_arrays):
    return [pltpu.SemaphoreType.DMA((3 * n_arrays,)) for _ in range(4)]


def _weight_prep(srcs, n_gather):
    n = len(srcs)
    n_split = 4

    def body(*refs):
        ins, outs, f32s, bfs = refs[:n], refs[n:2 * n], refs[2 * n:3 * n], refs[3 * n:4 * n]
        lsem_in, lsem_out = refs[4 * n], refs[4 * n + 1]
        x, y, c = _pos()
        j = 2 * x + y

        def pieces(rows):
            step = rows // n_split
            return [pl.ds(q * step, step) for q in range(n_split)]

        for t in range(n):
            for sl in pieces(ins[t].shape[0]):
                pltpu.make_async_copy(ins[t].at[sl], f32s[t].at[sl], lsem_in.at[t]).start()
        gather = _ShardGather(outs[:n_gather], *refs[4 * n + 2:])
        for t in range(n):
            pltpu.make_async_copy(ins[t], f32s[t], lsem_in.at[t]).wait()
            bfs[t][...] = f32s[t][...].astype(BF16)
            for sl in pieces(ins[t].shape[0]):
                pltpu.make_async_copy(bfs[t].at[sl], outs[t].at[j, sl], lsem_out.at[t]).start()
            if t == n_gather - 1:
                for u in range(n_gather):
                    pltpu.make_async_copy(bfs[u], outs[u].at[j], lsem_out.at[u]).wait()
                gather.start()
        gather.forward()
        gather.finish()
        for t in range(n_gather, n):
            pltpu.make_async_copy(bfs[t], outs[t].at[j], lsem_out.at[t]).wait()

    hbm = pl.BlockSpec(memory_space=pl.ANY)
    return pl.pallas_call(
        body, name="weight_prep",
        out_shape=tuple(jax.ShapeDtypeStruct((N_SHARD,) + a.shape, BF16) for a in srcs),
        in_specs=[hbm] * n, out_specs=(hbm,) * n,
        scratch_shapes=[pltpu.VMEM(a.shape, F32) for a in srcs] + [pltpu.VMEM(a.shape, BF16) for a in srcs]
        + [pltpu.SemaphoreType.DMA((n,)), pltpu.SemaphoreType.DMA((n,))] + _gather_sems(n_gather),
        compiler_params=pltpu.CompilerParams(vmem_limit_bytes=STAGE_VMEM_LIMIT),
    )(*srcs)


def _rs_sibling(arrs, name):
    n = len(arrs)

    def body(*refs):
        g_refs, got_refs, ssem, rsem = refs[:n], refs[n:2 * n], refs[2 * n], refs[2 * n + 1]
        x, y, c = _pos()
        for a in range(n):
            _, theirs = _halves(g_refs[a], c, 16)
            for s in range(N_SHARD):
                pltpu.make_async_remote_copy(
                    src_ref=g_refs[a].at[s, theirs], dst_ref=got_refs[a].at[s], send_sem=ssem.at[a],
                    recv_sem=rsem.at[a], device_id=(x, y, 1 - c), device_id_type=MESH).start()
        for a in range(n):
            _, theirs = _halves(g_refs[a], c, 16)
            pltpu.make_async_remote_copy(
                src_ref=g_refs[a].at[:, theirs], dst_ref=got_refs[a], send_sem=ssem.at[a], recv_sem=rsem.at[a],
                device_id=(x, y, 1 - c), device_id_type=MESH).wait()

    hbm = pl.BlockSpec(memory_space=pl.ANY)
    return pl.pallas_call(
        body, name=name,
        out_shape=tuple(jax.ShapeDtypeStruct((N_SHARD, a.shape[1] // 2, a.shape[2]), BF16) for a in arrs),
        in_specs=[hbm] * n, out_specs=(hbm,) * n,
        scratch_shapes=[pltpu.SemaphoreType.DMA((n,)), pltpu.SemaphoreType.DMA((n,))],
    )(*arrs)


class _ChipScatter:
    def __init__(self, parts, gots, ssem, rsem):
        self.parts, self.gots, self.ssem, self.rsem = parts, gots, ssem, rsem
        self.x, self.y, self.c = _pos()

    def _copy(self, a, k):
        px, py = _chips(self.x, self.y)[k]
        return pltpu.make_async_remote_copy(
            src_ref=self.parts[a].at[2 * px + py], dst_ref=self.gots[a].at[k], send_sem=self.ssem.at[3 * a + k],
            recv_sem=self.rsem.at[3 * a + k], device_id=(px, py, self.c), device_id_type=MESH)

    def start(self):
        for a in range(len(self.parts)):
            for k in range(3):
                self._copy(a, k).start()

    def finish(self):
        for a in range(len(self.parts)):
            for k in range(3):
                self._copy(a, k).wait()


def _rs_chips(parts, name):
    n = len(parts)

    def body(*refs):
        scatter = _ChipScatter(refs[:n], refs[n:2 * n], refs[2 * n], refs[2 * n + 1])
        scatter.start()
        scatter.finish()

    hbm = pl.BlockSpec(memory_space=pl.ANY)
    return pl.pallas_call(
        body, name=name, out_shape=tuple(jax.ShapeDtypeStruct((3,) + p.shape[1:], BF16) for p in parts),
        in_specs=[hbm] * n, out_specs=(hbm,) * n,
        scratch_shapes=[pltpu.SemaphoreType.DMA((3 * n,)), pltpu.SemaphoreType.DMA((3 * n,))],
    )(*parts)


def _rs_join(halves, name):
    n = len(halves)

    def body(*refs):
        h_refs, f_refs, stages = refs[:n], refs[n:2 * n], refs[2 * n:3 * n]
        lsem_in, lsem_out, ssem, rsem = refs[3 * n:]
        x, y, c = _pos()
        remote = []
        for a in range(n):
            mine, _ = _halves(f_refs[a], c, 8)
            cp = pltpu.make_async_remote_copy(
                src_ref=h_refs[a], dst_ref=f_refs[a].at[mine], send_sem=ssem.at[a], recv_sem=rsem.at[a],
                device_id=(x, y, 1 - c), device_id_type=MESH)
            cp.start()
            remote.append(cp)
            pltpu.make_async_copy(h_refs[a], stages[a], lsem_in.at[a]).start()
        local = []
        for a in range(n):
            mine, _ = _halves(f_refs[a], c, 8)
            pltpu.make_async_copy(h_refs[a], stages[a], lsem_in.at[a]).wait()
            cp = pltpu.make_async_copy(stages[a], f_refs[a].at[mine], lsem_out.at[a])
            cp.start()
            local.append(cp)
        for cp in remote + local:
            cp.wait()

    hbm = pl.BlockSpec(memory_space=pl.ANY)
    return pl.pallas_call(
        body, name=name,
        out_shape=tuple(jax.ShapeDtypeStruct((2 * h.shape[0], h.shape[1]), F32) for h in halves),
        in_specs=[hbm] * n, out_specs=(hbm,) * n,
        scratch_shapes=[pltpu.VMEM(h.shape, F32) for h in halves]
        + [pltpu.SemaphoreType.DMA((n,)) for _ in range(4)],
        compiler_params=pltpu.CompilerParams(vmem_limit_bytes=STAGE_VMEM_LIMIT),
    )(*halves)


def _row_block(rows):
    return max(b for b in range(16, 513, 16) if rows % b == 0)


def _add_pairs(arr, got, c_idx, name):
    hr, cols = got.shape[1], got.shape[2]
    rb = _row_block(hr)
    nb = hr // rb

    def body(c_ref, a_ref, b_ref, o_ref):
        o_ref[...] = (a_ref[...].astype(F32) + b_ref[...].astype(F32)).astype(BF16)

    spec = pl.BlockSpec((1, rb, cols), lambda s, r, c_ref: (s, r, 0))
    grid_spec = pltpu.PrefetchScalarGridSpec(
        num_scalar_prefetch=1, grid=(N_SHARD, nb),
        in_specs=[pl.BlockSpec((1, rb, cols), lambda s, r, c_ref: (s, c_ref[0] * nb + r, 0)), spec],
        out_specs=spec)
    return pl.pallas_call(
        body, name=name, grid_spec=grid_spec, out_shape=jax.ShapeDtypeStruct(got.shape, BF16),
    )(c_idx, arr, got)


def _add_chips(part, got, j_idx, name):
    hr, cols = got.shape[1], got.shape[2]
    rb = _row_block(hr)

    def body(j_ref, p_ref, g_ref, o_ref):
        acc = p_ref[0].astype(F32)
        for k in range(3):
            acc = acc + g_ref[k].astype(F32)
        o_ref[...] = acc

    grid_spec = pltpu.PrefetchScalarGridSpec(
        num_scalar_prefetch=1, grid=(hr // rb,),
        in_specs=[pl.BlockSpec((1, rb, cols), lambda r, j_ref: (j_ref[0], r, 0)),
                  pl.BlockSpec((3, rb, cols), lambda r, j_ref: (0, r, 0))],
        out_specs=pl.BlockSpec((rb, cols), lambda r, j_ref: (r, 0)))
    return pl.pallas_call(
        body, name=name, grid_spec=grid_spec, out_shape=jax.ShapeDtypeStruct((hr, cols), F32),
    )(j_idx, part, got)


def _small_exchange(sv, w_mod_sh, cctx, m_cctx, v_cctx, bmod, m_bmod, v_bmod, qg, m_qg, v_qg, kvg, m_kvg, v_kvg,
                    gf, m_gf, v_gf):
    ncol = w_mod_sh.shape[1]

    def body(sv_ref, w_ref, cctx_ref, mcc_ref, vcc_ref, b_ref, mb_ref, vb_ref, qg_ref, mq_ref, vq_ref,
             kg_ref, mk_ref, vk_ref, gf_ref, mgf_ref, vgf_ref,
             all_ref, red_ref, occ_ref, ob_ref, oq_ref, ok_ref, ogf_ref,
             vec_ref, part_ref, ssem, rsem, ssem2, rsem2):
        x, y, c = _pos()
        me = 4 * x + 2 * y + c
        j = 2 * x + y
        sends = []
        for r in range(1, 8):
            cp = pltpu.make_async_remote_copy(
                src_ref=sv_ref, dst_ref=all_ref.at[me], send_sem=ssem.at[r - 1], recv_sem=rsem.at[r - 1],
                device_id=_peer(r, x, y, c), device_id_type=MESH)
            cp.start()
            sends.append(cp)
        for cp in sends:
            cp.wait()
        all_ref[me] = sv_ref[...]
        red = all_ref[0]
        for d in range(1, 8):
            red = red + all_ref[d]
        red_ref[...] = red
        vec_ref[...] = jnp.zeros(vec_ref.shape, F32)

        @pl.when(j == 0)
        def _():
            vec_ref[0:1, 0:1024] = red[6:7, :]
            vec_ref[0:1, 1024:1536] = red[7:8, 0:512]

        @pl.when(j == 1)
        def _():
            vec_ref[0:1, 0:512] = red[7:8, 512:1024]

        part = lax.dot_general(vec_ref[...], w_ref[...], (((1,), (1,)), ((), ())), precision=HIGHEST,
                               preferred_element_type=F32)
        part_ref[j] = part
        sends2 = []
        for k, r in enumerate((4, 2, 6)):
            cp = pltpu.make_async_remote_copy(
                src_ref=part_ref.at[j], dst_ref=part_ref.at[j], send_sem=ssem2.at[k], recv_sem=rsem2.at[k],
                device_id=_peer(r, x, y, c), device_id_type=MESH)
            cp.start()
            sends2.append(cp)
        for cp in sends2:
            cp.wait()
        tot = part_ref[0]
        for s in range(1, N_SHARD):
            tot = tot + part_ref[s]
        cc = cctx_ref[...]
        sg = _sigmoid(cc)
        g_cc = tot[0:1, :] * (sg * (1.0 + cc * (1.0 - sg)))
        d_, m_, v_ = _adamw(cc, g_cc, mcc_ref[...], vcc_ref[...])
        occ_ref[0:1, :] = g_cc
        occ_ref[1:2, :] = d_
        occ_ref[2:3, :] = m_
        occ_ref[3:4, :] = v_
        occ_ref[4:8, :] = jnp.zeros((4, D_MODEL), F32)
        g_b = red[0:6, :]
        pad = jnp.concatenate([red[6:8, :], jnp.zeros((4, D_MODEL), F32)], axis=0)
        g_b = g_b + pad
        d_, m_, v_ = _adamw(b_ref[...], g_b, mb_ref[...], vb_ref[...])
        ob_ref[0] = g_b
        ob_ref[1] = d_
        ob_ref[2] = m_
        ob_ref[3] = v_
        g_q = red[9:10, 0:Q_RANK]
        d_, m_, v_ = _adamw(qg_ref[...], g_q, mq_ref[...], vq_ref[...])
        oq_ref[0:1, :] = g_q
        oq_ref[1:2, :] = d_
        oq_ref[2:3, :] = m_
        oq_ref[3:4, :] = v_
        oq_ref[4:8, :] = jnp.zeros((4, Q_RANK), F32)
        g_k = red[10:11, 0:KV_RANK]
        d_, m_, v_ = _adamw(kg_ref[...], g_k, mk_ref[...], vk_ref[...])
        ok_ref[0:1, :] = g_k
        ok_ref[1:2, :] = d_
        ok_ref[2:3, :] = m_
        ok_ref[3:4, :] = v_
        ok_ref[4:8, :] = jnp.zeros((4, KV_RANK), F32)
        g_f = red[8:9, :]
        d_, m_, v_ = _adamw(gf_ref[...], g_f, mgf_ref[...], vgf_ref[...])
        ogf_ref[0:1, :] = g_f
        ogf_ref[1:2, :] = d_
        ogf_ref[2:3, :] = m_
        ogf_ref[3:4, :] = v_
        ogf_ref[4:8, :] = jnp.zeros((4, D_MODEL), F32)

    vm = pl.BlockSpec(memory_space=pltpu.VMEM)
    out_shape = (
        jax.ShapeDtypeStruct((8, 16, D_MODEL), F32),
        jax.ShapeDtypeStruct((16, D_MODEL), F32),
        jax.ShapeDtypeStruct((8, D_MODEL), F32),
        jax.ShapeDtypeStruct((4, 6, D_MODEL), F32),
        jax.ShapeDtypeStruct((8, Q_RANK), F32),
        jax.ShapeDtypeStruct((8, KV_RANK), F32),
        jax.ShapeDtypeStruct((8, D_MODEL), F32),
    )
    return pl.pallas_call(
        body, name="small_exchange", out_shape=out_shape, in_specs=[vm] * 17, out_specs=tuple([vm] * 7),
        scratch_shapes=[pltpu.VMEM((8, ncol), F32), pltpu.VMEM((N_SHARD, 8, D_MODEL), F32),
                        pltpu.SemaphoreType.DMA((7,)), pltpu.SemaphoreType.DMA((7,)),
                        pltpu.SemaphoreType.DMA((3,)), pltpu.SemaphoreType.DMA((3,))],
        compiler_params=pltpu.CompilerParams(vmem_limit_bytes=VMEM_LIMIT),
    )(sv, w_mod_sh, cctx, m_cctx, v_cctx, bmod, m_bmod, v_bmod, qg, m_qg, v_qg, kvg, m_kvg, v_kvg, gf, m_gf, v_gf)


def _inproj_fwd(x2, ctx2, mod_a, w_in, qg, kvg, w_uq, w_ukv, cos_t, sin_a, sin_b):
    t_lat, t_ctx = x2.shape[0], ctx2.shape[0]
    tm = TOK_TILE
    n_lat = t_lat // tm
    n_all = n_lat + t_ctx // tm
    e_rows = t_lat + t_ctx

    def body(x_ref, ctx_ref, mod_ref, win_ref, qg_ref, kvg_ref, wuq_ref, wukv_ref, cos_ref, sa_ref, sb_ref,
             z_ref, q_ref, k_ref, v_ref, kt_ref):
        i = pl.program_id(0)
        xin = jnp.where(i < n_lat, x_ref[...], ctx_ref[...])
        xn = xin * lax.rsqrt(jnp.mean(xin * xin, axis=-1, keepdims=True) + EPS)
        h1 = (xn * (1.0 + mod_ref[0, 1:2, :]) + mod_ref[0, 0:1, :]).astype(BF16)
        z = _dot(h1, win_ref[...])
        z_ref[...] = z
        cos, sa, sb = cos_ref[...], sa_ref[...], sb_ref[...]
        cq = z[:, 0:Q_RANK]
        cqn = (cq * lax.rsqrt(jnp.mean(cq * cq, axis=-1, keepdims=True) + EPS) * qg_ref[...]).astype(BF16)
        q = _dot(cqn, wuq_ref[...])
        ckv = z[:, Q_RANK:Q_RANK + KV_RANK]
        ckvn = (ckv * lax.rsqrt(jnp.mean(ckv * ckv, axis=-1, keepdims=True) + EPS) * kvg_ref[...]).astype(BF16)
        kv = _dot(ckvn, wukv_ref[...])
        kr = _rope(z[:, Q_RANK + KV_RANK:Q_RANK + KV_RANK + HEAD_PAD], cos, sa, sb)
        ones_lane = lax.broadcasted_iota(jnp.int32, (tm, HEAD_PAD), 1) == V_DIM
        for h in range(N_HEADS):
            lo = h * HEAD_PAD
            q_ref[h] = _rope(q[:, lo:lo + HEAD_PAD], cos, sa, sb).astype(BF16)
            kh = kv[:, lo:lo + HEAD_PAD] + kr
            k_ref[h] = kh.astype(BF16)
            kt_ref[h] = kh.T.astype(BF16)
            vh = kv[:, N_HEADS * HEAD_PAD + lo:N_HEADS * HEAD_PAD + lo + HEAD_PAD]
            v_ref[h] = jnp.where(ones_lane, 1.0, vh).astype(BF16)

    row = lambda i: (i, 0)
    head_spec = pl.BlockSpec((N_HEADS, tm, HEAD_PAD), lambda i: (0, i, 0))
    head_shape = jax.ShapeDtypeStruct((N_HEADS, e_rows, HEAD_PAD), BF16)
    return pl.pallas_call(
        body, name="inproj_fwd", grid=(n_all,),
        out_shape=(jax.ShapeDtypeStruct((e_rows, Z_COLS), F32), head_shape, head_shape, head_shape,
                   jax.ShapeDtypeStruct((N_HEADS, HEAD_PAD, e_rows), BF16)),
        in_specs=[
            pl.BlockSpec((tm, D_MODEL), lambda i: (jnp.minimum(i, n_lat - 1), 0)),
            _const_spec((tm, D_MODEL)),
            pl.BlockSpec((1, 8, D_MODEL), lambda i: (i // n_lat, 0, 0)),
            _const_spec(w_in.shape), _const_spec(qg.shape), _const_spec(kvg.shape),
            _const_spec(w_uq.shape), _const_spec(w_ukv.shape),
            pl.BlockSpec((tm, HEAD_PAD), row), pl.BlockSpec((tm, HEAD_PAD), row), pl.BlockSpec((tm, HEAD_PAD), row),
        ],
        out_specs=(pl.BlockSpec((tm, Z_COLS), row), head_spec, head_spec, head_spec,
                   pl.BlockSpec((N_HEADS, HEAD_PAD, tm), lambda i: (0, 0, i))),
        compiler_params=pltpu.CompilerParams(vmem_limit_bytes=VMEM_LIMIT),
    )(x2, ctx2, mod_a, w_in, qg, kvg, w_uq, w_ukv, cos_t, sin_a, sin_b)


def _key_chunks(e_rows):
    n_chunks = max(1, e_rows // KEY_CHUNK)
    return [(ci * KEY_CHUNK, KEY_CHUNK if ci < n_chunks - 1 else e_rows - ci * KEY_CHUNK) for ci in range(n_chunks)]


def _attn_fwd(q, k, v, t_lat, shard_arrays):
    e_rows = k.shape[1]
    tq = 256
    bounds = _key_chunks(e_rows)
    c2 = ATTN_SCALE * LOG2E

    hb = ATTN_HEADS_PER_STEP
    n_hb = N_HEADS // hb

    def body(q_ref, k_ref, v_ref, o_ref, lse_ref):
        qs = [q_ref[b] for b in range(hb)]
        m, acc = [None] * hb, [None] * hb
        for lo, n in bounds:
            for b in range(hb):
                s = _dot_nt(qs[b], k_ref[b, lo:lo + n, :])
                mc = jnp.max(s, axis=-1, keepdims=True)
                m_new = mc if m[b] is None else jnp.maximum(m[b], mc)
                p = jnp.exp2((s - m_new) * c2)
                pv = _dot(p.astype(BF16), v_ref[b, lo:lo + n, :])
                acc[b] = pv if m[b] is None else acc[b] * jnp.exp2((m[b] - m_new) * c2) + pv
                m[b] = m_new
        for b in range(hb):
            l = acc[b][:, V_DIM:V_DIM + 1]
            o_ref[b] = (acc[b] * (1.0 / l)).astype(BF16)
            lse = (m[b] * ATTN_SCALE + jnp.log(l)) * LOG2E
            lse_ref[b] = jnp.broadcast_to(lse, (tq, HEAD_PAD)).T[0:1, :]

    n_w = len(shard_arrays)
    n_q = t_lat // tq

    def body_with_gather(q_ref, k_ref, v_ref, *rest):
        o_ref, lse_ref = rest[n_w], rest[n_w + 1]
        gather = _ShardGather(rest[n_w + 2:2 * n_w + 2], *rest[2 * n_w + 2:])
        h, i = pl.program_id(0), pl.program_id(1)
        pl.when(jnp.logical_and(h == 0, i == 0))(gather.start)
        pl.when(jnp.logical_and(h == n_hb // 2, i == 0))(gather.forward)
        body(q_ref, k_ref, v_ref, o_ref, lse_ref)
        pl.when(jnp.logical_and(h == n_hb - 1, i == n_q - 1))(gather.finish)

    hbm = pl.BlockSpec(memory_space=pl.ANY)
    return pl.pallas_call(
        body_with_gather, name="attn_fwd", grid=(n_hb, n_q),
        out_shape=(jax.ShapeDtypeStruct((N_HEADS, t_lat, HEAD_PAD), BF16),
                   jax.ShapeDtypeStruct((N_HEADS, 1, t_lat), F32))
        + tuple(jax.ShapeDtypeStruct(a.shape, a.dtype) for a in shard_arrays),
        in_specs=[pl.BlockSpec((hb, tq, HEAD_PAD), lambda h, i: (h, i, 0)),
                  pl.BlockSpec((hb, e_rows, HEAD_PAD), lambda h, i: (h, 0, 0)),
                  pl.BlockSpec((hb, e_rows, HEAD_PAD), lambda h, i: (h, 0, 0))] + [hbm] * n_w,
        out_specs=(pl.BlockSpec((hb, tq, HEAD_PAD), lambda h, i: (h, i, 0)),
                   pl.BlockSpec((hb, 1, tq), lambda h, i: (h, 0, i))) + (hbm,) * n_w,
        input_output_aliases={3 + a: 2 + a for a in range(n_w)},
        scratch_shapes=_gather_sems(n_w),
        compiler_params=pltpu.CompilerParams(vmem_limit_bytes=VMEM_LIMIT),
    )(q, k, v, *shard_arrays)


def _attn_bwd(q, k, v, kt, o, do, lse_row, t_lat, parts):
    e_rows = k.shape[1]
    tq = 256
    n_p = len(parts)
    n_q = t_lat // tq
    bounds = _key_chunks(e_rows)

    hb = ATTN_BWD_HEADS_PER_STEP
    n_hb = N_HEADS // hb

    def body(q_ref, k_ref, v_ref, kt_ref, o_ref, do_ref, lse_ref, *rest):
        dqt_ref, dk_ref, dv_ref = rest[n_p:n_p + 3]
        scatter = _ChipScatter(rest[:n_p], rest[n_p + 3:2 * n_p + 3], rest[2 * n_p + 3], rest[2 * n_p + 4])
        h, i = pl.program_id(0), pl.program_id(1)
        pl.when(jnp.logical_and(h == 0, i == 0))(scatter.start)

        @pl.when(i == 0)
        def _():
            dk_ref[...] = jnp.zeros(dk_ref.shape, F32)
            dv_ref[...] = jnp.zeros(dv_ref.shape, F32)

        qs, dos, lses, deltas = [], [], [], []
        for b in range(hb):
            qs.append(q_ref[b])
            dos.append(do_ref[b])
            lses.append(lse_ref[b])
            prod = o_ref[b].astype(F32) * dos[b].astype(F32)
            deltas.append(lax.dot_general(jnp.ones((8, HEAD_PAD), F32), prod, (((1,), (1,)), ((), ())),
                                          precision=HIGHEST, preferred_element_type=F32)[0:1, :])
        dqt = [None] * hb
        for lo, n in bounds:
            for b in range(hb):
                pt = jnp.exp2(_dot_nt(k_ref[b, lo:lo + n, :], qs[b]) * (ATTN_SCALE * LOG2E) - lses[b])
                dpt = _dot_nt(v_ref[b, lo:lo + n, :], dos[b])
                dst = (pt * (dpt - deltas[b])).astype(BF16)
                dv_c = _dot(pt.astype(BF16), dos[b])
                dk_c = _dot(dst, qs[b])
                part = _dot(kt_ref[b, :, lo:lo + n], dst)
                dqt[b] = part if dqt[b] is None else dqt[b] + part
                dk_ref[b, lo:lo + n, :] += dk_c * ATTN_SCALE
                dv_ref[b, lo:lo + n, :] += dv_c
        for b in range(hb):
            dqt_ref[b] = dqt[b] * ATTN_SCALE

        pl.when(jnp.logical_and(h == n_hb - 1, i == n_q - 1))(scatter.finish)

    hbm = pl.BlockSpec(memory_space=pl.ANY)
    qspec = pl.BlockSpec((hb, tq, HEAD_PAD), lambda h, i: (h, i, 0))
    kspec = pl.BlockSpec((hb, e_rows, HEAD_PAD), lambda h, i: (h, 0, 0))
    return pl.pallas_call(
        body, name="attn_bwd", grid=(n_hb, n_q),
        out_shape=(jax.ShapeDtypeStruct((N_HEADS, HEAD_PAD, t_lat), F32),
                   jax.ShapeDtypeStruct((N_HEADS, e_rows, HEAD_PAD), F32),
                   jax.ShapeDtypeStruct((N_HEADS, e_rows, HEAD_PAD), F32))
        + tuple(jax.ShapeDtypeStruct((3,) + p.shape[1:], BF16) for p in parts),
        in_specs=[qspec, kspec, kspec, pl.BlockSpec((hb, HEAD_PAD, e_rows), lambda h, i: (h, 0, 0)), qspec, qspec,
                  pl.BlockSpec((hb, 1, tq), lambda h, i: (h, 0, i))] + [hbm] * n_p,
        out_specs=(pl.BlockSpec((hb, HEAD_PAD, tq), lambda h, i: (h, 0, i)), kspec, kspec) + (hbm,) * n_p,
        scratch_shapes=[pltpu.SemaphoreType.DMA((3 * n_p,)), pltpu.SemaphoreType.DMA((3 * n_p,))],
        compiler_params=pltpu.CompilerParams(vmem_limit_bytes=VMEM_LIMIT),
    )(q, k, v, kt, o, do, lse_row, *parts)


def _halo_specs(tm, col_block):
    per = tm // 8
    prev = pl.BlockSpec((8, CONV_W), lambda i: (jnp.maximum(i * per - 1, 0), col_block))
    nxt = pl.BlockSpec((8, CONV_W), lambda i: ((i + 1) * per, col_block))
    return prev, nxt


def _mlp_fwdbwd(o, z, x2, tgt, mod_b, gf, cw, wo_attn, wo_conv, w1, w2):
    t_lat = x2.shape[0]
    tm = TOK_TILE
    n_lat = t_lat // tm
    n_ff = D_FF // FF_CHUNK

    def body(o_ref, gb_ref, gc_ref, xi_ref, gcp_ref, xip_ref, gcn_ref, xin_ref, cw_ref, woa_ref, woc_ref,
             x_ref, t_ref, mod_ref, gf_ref, w1_ref, w2_ref,
             r_ref, da_ref, h2_ref, dy2_ref, dx1_ref, conv_ref, acc_ref, dy1_ref, do_ref, dgb_ref, dyv_ref, ra_ref):
        i = pl.program_id(0)

        @pl.when(i == 0)
        def _():
            acc_ref[...] = jnp.zeros(acc_ref.shape, F32)

        g1, sh2, sc2, g2 = mod_ref[0:1, :], mod_ref[1:2, :], mod_ref[2:3, :], mod_ref[3:4, :]
        u = gc_ref[...] * xi_ref[...]
        u_prev = jnp.where(i > 0, gcp_ref[7:8, :] * xip_ref[7:8, :], 0.0)
        u_next = jnp.where(i < n_lat - 1, gcn_ref[0:1, :] * xin_ref[0:1, :], 0.0)
        um1, up1 = _shift_rows(u, u_prev, u_next)
        yv = cw_ref[0:1, :] * um1 + cw_ref[1:2, :] * u + cw_ref[2:3, :] * up1
        gb = gb_ref[...]
        conv = (gb * yv).astype(BF16)
        conv_ref[...] = conv
        y1 = _dot(conv, woc_ref[...])
        for h in range(N_HEADS):
            y1 = y1 + _dot(o_ref[h], woa_ref[h])
        x1 = x_ref[...] + g1 * y1
        rstd2 = lax.rsqrt(jnp.mean(x1 * x1, axis=-1, keepdims=True) + EPS)
        xn1 = x1 * rstd2
        h2 = (xn1 * (1.0 + sc2) + sh2).astype(BF16)
        h2_ref[...] = h2
        y2 = jnp.zeros((tm, D_MODEL), F32)
        for jj in range(n_ff):
            ra = jnp.maximum(_dot(h2, w1_ref[jj]), 0.0)
            ra_ref[jj] = ra
            r = (ra * ra).astype(BF16)
            r_ref[:, jj * FF_CHUNK:(jj + 1) * FF_CHUNK] = r
            y2 = y2 + _dot(r, w2_ref[jj * FF_CHUNK:(jj + 1) * FF_CHUNK, :])
        x2v = x1 + g2 * y2
        rstd3 = lax.rsqrt(jnp.mean(x2v * x2v, axis=-1, keepdims=True) + EPS)
        xn3 = x2v * rstd3
        gfv = gf_ref[...]
        diff = xn3 * gfv - t_ref[...]
        loss_t = 0.5 * jnp.sum(jnp.sum(diff * diff, axis=-1, keepdims=True), axis=0, keepdims=True) * (1.0 / D_MODEL)
        dy = diff * (1.0 / D_MODEL)
        dxn3 = dy * gfv
        dx2 = rstd3 * (dxn3 - xn3 * jnp.mean(dxn3 * xn3, axis=-1, keepdims=True))
        dy2 = (dx2 * g2).astype(BF16)
        dy2_ref[...] = dy2
        dh2 = jnp.zeros((tm, D_MODEL), F32)
        for jj in range(n_ff):
            dr = _dot_nt(dy2, w2_ref[jj * FF_CHUNK:(jj + 1) * FF_CHUNK, :])
            da = (2.0 * ra_ref[jj] * dr).astype(BF16)
            da_ref[:, jj * FF_CHUNK:(jj + 1) * FF_CHUNK] = da
            dh2 = dh2 + _dot_nt(da, w1_ref[jj])
        dxn1 = dh2 * (1.0 + sc2)
        dx1 = dx2 + rstd2 * (dxn1 - xn1 * jnp.mean(dxn1 * xn1, axis=-1, keepdims=True))
        dx1_ref[...] = dx1
        dy1 = (dx1 * g1).astype(BF16)
        dy1_ref[...] = dy1
        for h in range(N_HEADS):
            do_ref[h] = _dot_nt(dy1, woa_ref[h]).astype(BF16)
        dconv = _dot_nt(dy1, woc_ref[...])
        dgb_ref[...] = dconv * yv
        dyv_ref[...] = dconv * gb
        acc_ref[5:6, :] += jnp.sum(dx1 * y1, axis=0, keepdims=True)
        acc_ref[0:1, :] += jnp.sum(dy * xn3, axis=0, keepdims=True)
        acc_ref[1:2, :] += jnp.sum(dx2 * y2, axis=0, keepdims=True)
        acc_ref[2:3, :] += jnp.sum(dh2, axis=0, keepdims=True)
        acc_ref[3:4, :] += jnp.sum(dh2 * xn1, axis=0, keepdims=True)
        acc_ref[4:5, :] += jnp.broadcast_to(loss_t, (1, D_MODEL))

    row = lambda i: (i, 0)
    gcp, gcn = _halo_specs(tm, 2)
    xip, xin = _halo_specs(tm, 3)
    tile = pl.BlockSpec((tm, D_MODEL), row)
    wide = pl.BlockSpec((tm, D_FF), row)
    half = pl.BlockSpec((tm, CONV_W), row)
    return pl.pallas_call(
        body, name="mlp_fwdbwd", grid=(n_lat,),
        out_shape=(jax.ShapeDtypeStruct((t_lat, D_FF), BF16), jax.ShapeDtypeStruct((t_lat, D_FF), BF16),
                   jax.ShapeDtypeStruct((t_lat, D_MODEL), BF16), jax.ShapeDtypeStruct((t_lat, D_MODEL), BF16),
                   jax.ShapeDtypeStruct((t_lat, D_MODEL), F32), jax.ShapeDtypeStruct((t_lat, CONV_W), BF16),
                   jax.ShapeDtypeStruct((8, D_MODEL), F32),
                   jax.ShapeDtypeStruct((t_lat, D_MODEL), BF16),
                   jax.ShapeDtypeStruct((N_HEADS, t_lat, HEAD_PAD), BF16),
                   jax.ShapeDtypeStruct((t_lat, CONV_W), F32), jax.ShapeDtypeStruct((t_lat, CONV_W), F32)),
        in_specs=[
            pl.BlockSpec((N_HEADS, tm, HEAD_PAD), lambda i: (0, i, 0)),
            pl.BlockSpec((tm, CONV_W), lambda i: (i, 1)), pl.BlockSpec((tm, CONV_W), lambda i: (i, 2)),
            pl.BlockSpec((tm, CONV_W), lambda i: (i, 3)),
            gcp, xip, gcn, xin,
            _const_spec(cw.shape), _resident_spec(wo_attn.shape), _resident_spec(wo_conv.shape),
            tile, tile, _const_spec(mod_b.shape), _const_spec(gf.shape),
            _resident_spec(w1.shape), _resident_spec(w2.shape),
        ],
        out_specs=(wide, wide, tile, tile, tile, half, _const_spec((8, D_MODEL)),
                   tile, pl.BlockSpec((N_HEADS, tm, HEAD_PAD), lambda i: (0, i, 0)), half, half),
        scratch_shapes=[pltpu.VMEM((n_ff, tm, FF_CHUNK), F32)],
        compiler_params=pltpu.CompilerParams(vmem_limit_bytes=VMEM_LIMIT),
    )(o, z, z, z, z, z, z, z, cw, wo_attn, wo_conv, x2, tgt, mod_b, gf, w1, w2)


def _inproj_bwd(x2, ctx2, mod_a, z, dyv, dgb, dx1, dqt, dk, dv, cos_t, sin_a, sin_b, w_in, w_uq, w_ukv, qg, kvg, cw):
    t_lat, t_ctx = x2.shape[0], ctx2.shape[0]
    tm = TOK_TILE
    n_lat = t_lat // tm
    n_all = n_lat + t_ctx // tm
    group = max(g for g in (1, 2, 4) if n_lat % g == 0)

    def body(x_ref, ctx_ref, mod_ref, z_ref, gcp_ref, xip_ref, gcn_ref, xin_ref, dyv_ref, dyvp_ref, dyvn_ref,
             dgb_ref, dx1_ref, dqt_ref, dk_ref, dv_ref, cos_ref, sa_ref, sb_ref, win_ref, wuq_ref, wukv_ref,
             qg_ref, kvg_ref, cw_ref, gx_ref, dwin_ref, dwuq_ref, dwukv_ref, acc_ref, h1_buf, dz_buf):
        i = pl.program_id(0)
        lat = i < n_lat

        @pl.when(i == 0)
        def _():
            dwin_ref[...] = jnp.zeros(dwin_ref.shape, F32)
            dwuq_ref[...] = jnp.zeros(dwuq_ref.shape, F32)
            dwukv_ref[...] = jnp.zeros(dwukv_ref.shape, F32)
            acc_ref[...] = jnp.zeros(acc_ref.shape, F32)

        xin = jnp.where(lat, x_ref[...], ctx_ref[...])
        rstd = lax.rsqrt(jnp.mean(xin * xin, axis=-1, keepdims=True) + EPS)
        xn = xin * rstd
        sc = mod_ref[0, 1:2, :]
        h1 = (xn * (1.0 + sc) + mod_ref[0, 0:1, :]).astype(BF16)
        z = z_ref[...]
        cos, sa, sb = cos_ref[...], sa_ref[...], sb_ref[...]
        qgv, kvgv = qg_ref[...], kvg_ref[...]
        cq = z[:, 0:Q_RANK]
        cqh = cq * lax.rsqrt(jnp.mean(cq * cq, axis=-1, keepdims=True) + EPS)
        rq = lax.rsqrt(jnp.mean(cq * cq, axis=-1, keepdims=True) + EPS)
        cqn = (cqh * qgv).astype(BF16)
        parts = []
        for h in range(N_HEADS):
            g = jnp.where(lat, dqt_ref[h].T, 0.0)
            parts.append(_unrope(g, cos, sa, sb))
        dq = jnp.concatenate(parts, axis=1).astype(BF16)
        dcqn = _dot_nt(dq, wuq_ref[...])
        dwuq_ref[...] += _dot_tn(cqn, dq)
        acc_ref[4:5, 0:Q_RANK] += jnp.sum(dcqn * cqh, axis=0, keepdims=True)
        dxn = dcqn * qgv
        dcq = rq * (dxn - cqh * jnp.mean(dxn * cqh, axis=-1, keepdims=True))
        ckv = z[:, Q_RANK:Q_RANK + KV_RANK]
        rk = lax.rsqrt(jnp.mean(ckv * ckv, axis=-1, keepdims=True) + EPS)
        ckvh = ckv * rk
        ckvn = (ckvh * kvgv).astype(BF16)
        dks = [dk_ref[h] for h in range(N_HEADS)]
        dkr = dks[0]
        for h in range(1, N_HEADS):
            dkr = dkr + dks[h]
        dkv = jnp.concatenate(dks + [dv_ref[h] for h in range(N_HEADS)], axis=1).astype(BF16)
        dckvn = _dot_nt(dkv, wukv_ref[...])
        dwukv_ref[...] += _dot_tn(ckvn, dkv)
        acc_ref[5:6, 0:KV_RANK] += jnp.sum(dckvn * ckvh, axis=0, keepdims=True)
        dxn = dckvn * kvgv
        dckv = rk * (dxn - ckvh * jnp.mean(dxn * ckvh, axis=-1, keepdims=True))
        dkr = _unrope(dkr, cos, sa, sb)
        gb, gc, xi = z[:, 512:1024], z[:, 1024:1536], z[:, 1536:2048]
        u = gc * xi
        u_prev = jnp.where(i > 0, gcp_ref[7:8, :] * xip_ref[7:8, :], 0.0)
        u_next = jnp.where(i < n_lat - 1, gcn_ref[0:1, :] * xin_ref[0:1, :], 0.0)
        um1, up1 = _shift_rows(u, u_prev, u_next)
        dyv = jnp.where(lat, dyv_ref[...], 0.0)
        dyv_prev = jnp.where(jnp.logical_and(i > 0, lat), dyvp_ref[7:8, :], 0.0)
        dyv_next = jnp.where(i < n_lat - 1, dyvn_ref[0:1, :], 0.0)
        dyv_m1, dyv_p1 = _shift_rows(dyv, dyv_prev, dyv_next)
        du = cw_ref[0:1, :] * dyv_p1 + cw_ref[1:2, :] * dyv + cw_ref[2:3, :] * dyv_m1
        dgc = du * xi
        dxi = du * gc
        dgb = jnp.where(lat, dgb_ref[...], 0.0)
        acc_ref[6:7, 0:CONV_W] += jnp.sum(dyv * um1, axis=0, keepdims=True)
        acc_ref[7:8, 0:CONV_W] += jnp.sum(dyv * u, axis=0, keepdims=True)
        acc_ref[8:9, 0:CONV_W] += jnp.sum(dyv * up1, axis=0, keepdims=True)
        dz = jnp.concatenate([dcq, dckv, dkr, dgb, dgc, dxi], axis=1).astype(BF16)
        dh1 = _dot_nt(dz, win_ref[...])
        slot = i % group
        rows_g = pl.ds(pl.multiple_of(slot * tm, tm), tm)
        h1_buf[rows_g, :] = h1
        dz_buf[rows_g, :] = dz

        @pl.when(jnp.logical_and(lat, slot == group - 1))
        def _():
            dwin_ref[...] += _dot_tn(h1_buf[...], dz_buf[...])

        @pl.when(jnp.logical_not(lat))
        def _():
            dwin_ref[...] += _dot_tn(h1, dz)
        s_sh = jnp.sum(dh1, axis=0, keepdims=True)
        s_sc = jnp.sum(dh1 * xn, axis=0, keepdims=True)
        zero = jnp.zeros_like(s_sh)
        acc_ref[0:1, :] += jnp.where(lat, s_sh, zero)
        acc_ref[1:2, :] += jnp.where(lat, s_sc, zero)
        acc_ref[2:3, :] += jnp.where(lat, zero, s_sh)
        acc_ref[3:4, :] += jnp.where(lat, zero, s_sc)
        dxn = dh1 * (1.0 + sc)
        dx = rstd * (dxn - xn * jnp.mean(dxn * xn, axis=-1, keepdims=True))

        @pl.when(lat)
        def _():
            gx_ref[...] = dx1_ref[...] + dx

    last = n_lat - 1
    per = tm // 8
    lat_row = lambda i: (jnp.minimum(i, last), 0)
    row = lambda i: (i, 0)
    gcp, gcn = _halo_specs(tm, 2)
    xip, xin = _halo_specs(tm, 3)
    n_halo = t_lat // 8
    dyvp = pl.BlockSpec((8, CONV_W), lambda i: (jnp.clip(i * per - 1, 0, n_halo - 1), 0))
    dyvn = pl.BlockSpec((8, CONV_W), lambda i: (jnp.minimum((i + 1) * per, n_halo - 1), 0))
    gcn = pl.BlockSpec((8, CONV_W), lambda i: (jnp.minimum((i + 1) * per, (t_lat + t_ctx) // 8 - 1), 2))
    xin = pl.BlockSpec((8, CONV_W), lambda i: (jnp.minimum((i + 1) * per, (t_lat + t_ctx) // 8 - 1), 3))
    head_f32 = pl.BlockSpec((N_HEADS, tm, HEAD_PAD), lambda i: (0, i, 0))
    tab = pl.BlockSpec((tm, HEAD_PAD), row)
    return pl.pallas_call(
        body, name="inproj_bwd", grid=(n_all,),
        out_shape=(jax.ShapeDtypeStruct((t_lat, D_MODEL), F32), jax.ShapeDtypeStruct(w_in.shape, F32),
                   jax.ShapeDtypeStruct(w_uq.shape, F32), jax.ShapeDtypeStruct(w_ukv.shape, F32),
                   jax.ShapeDtypeStruct((16, D_MODEL), F32)),
        in_specs=[
            pl.BlockSpec((tm, D_MODEL), lat_row), _const_spec((tm, D_MODEL)),
            pl.BlockSpec((1, 8, D_MODEL), lambda i: (i // n_lat, 0, 0)),
            pl.BlockSpec((tm, Z_COLS), row), gcp, xip, gcn, xin,
            pl.BlockSpec((tm, CONV_W), lat_row), dyvp, dyvn,
            pl.BlockSpec((tm, CONV_W), lat_row), pl.BlockSpec((tm, D_MODEL), lat_row),
            pl.BlockSpec((N_HEADS, HEAD_PAD, tm), lambda i: (0, 0, jnp.minimum(i, last))),
            head_f32, head_f32, tab, tab, tab,
            _const_spec(w_in.shape), _const_spec(w_uq.shape), _const_spec(w_ukv.shape),
            _const_spec(qg.shape), _const_spec(kvg.shape), _const_spec(cw.shape),
        ],
        out_specs=(pl.BlockSpec((tm, D_MODEL), lat_row), _const_spec(w_in.shape), _const_spec(w_uq.shape),
                   _const_spec(w_ukv.shape), _const_spec((16, D_MODEL))),
        scratch_shapes=[pltpu.VMEM((group * tm, D_MODEL), BF16), pltpu.VMEM((group * tm, Z_COLS), BF16)],
        compiler_params=pltpu.CompilerParams(vmem_limit_bytes=VMEM_LIMIT),
    )(x2, ctx2, mod_a, z, z, z, z, z, dyv, dyv, dyv, dgb, dx1, dqt, dk, dv, cos_t, sin_a, sin_b, w_in, w_uq, w_ukv,
      qg, kvg, cw)


def _wgrad(a, b, name, bm, bn):
    t, m = a.shape
    n = b.shape[1]
    bk = min(t, 4096)
    nk = t // bk
    nj = n // bn

    def body(a_ref, b_ref, o_ref, acc_ref):
        k = pl.program_id(2)
        part = _dot_tn(a_ref[...], b_ref[...])

        @pl.when(k == 0)
        def _():
            acc_ref[...] = part

        @pl.when(k > 0)
        def _():
            acc_ref[...] += part

        @pl.when(k == nk - 1)
        def _():
            o_ref[...] = acc_ref[...].astype(BF16)

    return pl.pallas_call(
        body, name=name, grid=(m // bm, nj, nk), out_shape=jax.ShapeDtypeStruct((m // bm * nj, bm, bn), BF16),
        in_specs=[pl.BlockSpec((bk, bm), lambda i, j, k: (k, i)), pl.BlockSpec((bk, bn), lambda i, j, k: (k, j))],
        out_specs=pl.BlockSpec((None, bm, bn), lambda i, j, k: (i * nj + j, 0, 0)),
        scratch_shapes=[pltpu.VMEM((bm, bn), F32)],
        compiler_params=pltpu.CompilerParams(vmem_limit_bytes=VMEM_LIMIT),
    )(a, b)


def _wgrad_out(o, conv, dy1):
    t = o.shape[1]
    bk = min(t, 2048)
    nk = t // bk
    rows = N_HEADS * HEAD_PAD + CONV_W

    def body(o_ref, c_ref, d_ref, w_ref, acc_ref):
        k = pl.program_id(0)
        cat = jnp.concatenate([o_ref[h] for h in range(N_HEADS)] + [c_ref[...]], axis=1)
        part = _dot_tn(cat, d_ref[...])

        @pl.when(k == 0)
        def _():
            acc_ref[...] = part

        @pl.when(k > 0)
        def _():
            acc_ref[...] += part

        @pl.when(k == nk - 1)
        def _():
            for h in range(N_HEADS):
                w_ref[h * V_DIM:(h + 1) * V_DIM, :] = acc_ref[h * HEAD_PAD:h * HEAD_PAD + V_DIM, :].astype(BF16)
            w_ref[N_HEADS * V_DIM:, :] = acc_ref[N_HEADS * HEAD_PAD:, :].astype(BF16)

    return pl.pallas_call(
        body, name="wgrad_out", grid=(nk,),
        out_shape=jax.ShapeDtypeStruct((D_MODEL, D_MODEL), BF16),
        in_specs=[pl.BlockSpec((N_HEADS, bk, HEAD_PAD), lambda k: (0, k, 0)),
                  pl.BlockSpec((bk, CONV_W), lambda k: (k, 0)),
                  pl.BlockSpec((bk, D_MODEL), lambda k: (k, 0))],
        out_specs=_const_spec((D_MODEL, D_MODEL)),
        scratch_shapes=[pltpu.VMEM((rows, D_MODEL), F32)],
        compiler_params=pltpu.CompilerParams(vmem_limit_bytes=VMEM_LIMIT),
    )(o, conv, dy1)


def _adamw_call(w, g, m, v, name):
    rows, cols = w.shape
    rb = 256 if rows % 256 == 0 else rows

    def body(w_ref, g_ref, m_ref, v_ref, d_ref, nm_ref, nv_ref):
        d_, m_, v_ = _adamw(w_ref[...], g_ref[...], m_ref[...], v_ref[...])
        d_ref[...] = d_
        nm_ref[...] = m_
        nv_ref[...] = v_

    spec = pl.BlockSpec((rb, cols), lambda i: (i, 0))
    shp = jax.ShapeDtypeStruct((rows, cols), F32)
    return pl.pallas_call(
        body, name=name, grid=(rows // rb,), out_shape=(shp, shp, shp),
        in_specs=[spec] * 4, out_specs=(spec, spec, spec),
    )(w, g, m, v)


def _wmod_update(s_t, dm, w, m, v):
    rows, cols = w.shape
    cb = 512

    def body(s_ref, dm_ref, w_ref, m_ref, v_ref, g_ref, d_ref, nm_ref, nv_ref):
        g = jnp.dot(s_ref[...], dm_ref[...], precision=HIGHEST, preferred_element_type=F32)
        d_, m_, v_ = _adamw(w_ref[...], g, m_ref[...], v_ref[...])
        g_ref[...] = g
        d_ref[...] = d_
        nm_ref[...] = m_
        nv_ref[...] = v_

    spec = pl.BlockSpec((rows, cb), lambda i: (0, i))
    shp = jax.ShapeDtypeStruct((rows, cols), F32)
    return pl.pallas_call(
        body, name="wmod_update", grid=(cols // cb,), out_shape=(shp, shp, shp, shp),
        in_specs=[_const_spec(s_t.shape), pl.BlockSpec((16, cb), lambda i: (0, i)), spec, spec, spec],
        out_specs=(spec, spec, spec, spec),
        compiler_params=pltpu.CompilerParams(vmem_limit_bytes=VMEM_LIMIT),
    )(s_t, dm, w, m, v)


def _rope_tables(t_lat, t_ctx):
    t = jnp.arange(t_lat)
    pos = jnp.stack([(t // GRID_W).astype(F32), (t % GRID_W).astype(F32)], axis=1)
    half = QK_ROPE // 4
    freqs = ROPE_THETA ** (-jnp.arange(0, 2 * half, 2, dtype=F32) / (2 * half))
    ang = pos[:, :, None] * freqs[None, None, :]
    cos, sin = jnp.cos(ang), jnp.sin(ang)
    zero = jnp.zeros_like(sin)
    cos32 = jnp.concatenate([cos, cos], axis=2).reshape(t_lat, QK_ROPE)
    sa32 = jnp.concatenate([zero, sin], axis=2).reshape(t_lat, QK_ROPE)
    sb32 = jnp.concatenate([-sin, zero], axis=2).reshape(t_lat, QK_ROPE)

    def widen(tab, fill):
        left = jnp.full((t_lat, ROPE_LANE0), fill, F32)
        right = jnp.full((t_lat, HEAD_PAD - ROPE_LANE0 - QK_ROPE), fill, F32)
        lat = jnp.concatenate([left, tab, right], axis=1)
        return jnp.concatenate([lat, jnp.full((t_ctx, HEAD_PAD), fill, F32)], axis=0)

    return widen(cos32, 1.0), widen(sa32, 0.0), widen(sb32, 0.0)


def _cols_from_shards(s):
    return jnp.transpose(s, (1, 0, 2)).reshape(s.shape[1], -1)


def _cols_to_shards(w):
    k, n = w.shape
    return jnp.transpose(w.reshape(k, N_SHARD, n // N_SHARD), (1, 0, 2))


def kernel(x, c, ctx, c_ctx, w_mod, b_mod, w_in, q_norm_g, w_uq, kv_norm_g, w_ukv, conv_w, w_out, w_mlp1, w_mlp2, final_norm_g, loss_target, m_c_ctx, m_w_mod, m_b_mod, m_w_in, m_q_norm_g, m_w_uq, m_kv_norm_g, m_w_ukv, m_conv_w, m_w_out, m_w_mlp1, m_w_mlp2, m_final_norm_g, v_c_ctx, v_w_mod, v_b_mod, v_w_in, v_q_norm_g, v_w_uq, v_kv_norm_g, v_w_ukv, v_conv_w, v_w_out, v_w_mlp1, v_w_mlp2, v_final_norm_g):
    t_lat, t_ctx = x.shape[1], ctx.shape[1]
    assert t_ctx == TOK_TILE and t_lat % TOK_TILE == 0 and t_lat % GRID_W == 0
    mx, my, mc = _pos()
    me = 4 * mx + 2 * my + mc
    j = 2 * mx + my
    ncol = w_mod.shape[2]
    x2, ctx2, tgt = x[0], ctx[0], loss_target[0]
    cctx_row = c_ctx.reshape(1, D_MODEL)

    b_sh = lax.dynamic_slice(b_mod, (0, j * ncol), (1, ncol))
    cw_pad = jnp.zeros((8, 128), F32).at[0:3, :].set(conv_w[0])
    c8, m_all = _mod_exchange(c, cctx_row, w_mod[0], b_sh, cw_pad)
    mvec = m_all[:, 0, :].reshape(6, D_MODEL)
    mctx = m_all[:, 8, :].reshape(6, D_MODEL)
    zeros6 = jnp.zeros((6, D_MODEL), F32)
    mod_a = jnp.stack([jnp.concatenate([mvec[0:2], zeros6], axis=0), jnp.concatenate([mctx[0:2], zeros6], axis=0)])
    mod_b = jnp.concatenate([mvec[2:6], jnp.zeros((4, D_MODEL), F32)], axis=0)
    cw_full = jnp.pad(jnp.transpose(m_all[:, 9:12, 0:128], (1, 0, 2)).reshape(3, CONV_W), ((0, 5), (0, 0)))

    g_in, g_uq, g_ukv, g_out, g_m1, g_m2 = _weight_prep(
        (w_in[0], w_uq[0], w_ukv[0], w_out[0], w_mlp1[0], w_mlp2[0]), 3)
    w_in_f = _cols_from_shards(g_in)
    zc = lambda n: jnp.zeros((D_MODEL, n), BF16)
    w_in_p = jnp.concatenate([w_in_f[:, 0:384], zc(64), w_in_f[:, 384:416], zc(32), w_in_f[:, 416:]], axis=1)
    w_uq_f = _cols_from_shards(g_uq).reshape(Q_RANK, N_HEADS, QK_DIM)
    w_uq_p = jnp.pad(w_uq_f, ((0, 0), (0, 0), (0, HEAD_PAD - QK_DIM))).reshape(Q_RANK, N_HEADS * HEAD_PAD)
    w_ukv_f = _cols_from_shards(g_ukv).reshape(KV_RANK, N_HEADS, QK_NOPE + V_DIM)
    padh = lambda a: jnp.pad(a, ((0, 0), (0, 0), (0, HEAD_PAD - a.shape[2]))).reshape(KV_RANK, N_HEADS * HEAD_PAD)
    w_ukv_p = jnp.concatenate([padh(w_ukv_f[:, :, :QK_NOPE]), padh(w_ukv_f[:, :, QK_NOPE:])], axis=1)
    cos_t, sin_a, sin_b = _rope_tables(t_lat, t_ctx)
    gf_row = final_norm_g.reshape(1, D_MODEL)
    c_idx = mc.reshape(1).astype(jnp.int32)
    j_idx = j.reshape(1).astype(jnp.int32)

    z, q, k, v, kt = _inproj_fwd(x2, ctx2, mod_a, w_in_p, q_norm_g, kv_norm_g, w_uq_p, w_ukv_p, cos_t, sin_a, sin_b)
    o, lse, g_out, w1, g_m2 = _attn_fwd(q, k, v, t_lat, (g_out, g_m1, g_m2))
    w_out_f = g_out.reshape(D_MODEL, D_MODEL)
    wo_attn = jnp.pad(w_out_f[0:512].reshape(N_HEADS, V_DIM, D_MODEL), ((0, 0), (0, HEAD_PAD - V_DIM), (0, 0)))
    wo_conv = w_out_f[512:]
    w2 = g_m2.reshape(D_FF, D_MODEL)
    r, da, h2, dy2, dx1, conv, acc_mlp, dy1, do, dgb, dyv = _mlp_fwdbwd(o, z, x2, tgt, mod_b, gf_row, cw_full, wo_attn,
                                                                         wo_conv, w1, w2)
    d_w1 = _wgrad(h2, da, "wgrad_mlp1", D_MODEL, FF_CHUNK)
    d_w2 = _wgrad(r, dy2, "wgrad_mlp2", FF_CHUNK, D_MODEL)
    d_wout = _wgrad_out(o, conv, dy1).reshape(N_SHARD, D_MODEL // N_SHARD, D_MODEL)
    big_grads, big_names = (d_w1, d_w2, d_wout), ("mlp1", "mlp2", "out")
    big_got = _rs_sibling(big_grads, "rs_sibling_big")
    big_parts = [_add_pairs(a, g, c_idx, "rs_add_pairs_" + n) for a, g, n in zip(big_grads, big_got, big_names)]
    dqt, dk, dv, *big_recv = _attn_bwd(q, k, v, kt, o, do, lse, t_lat, big_parts)
    big_halves = [_add_chips(p, g, j_idx, "rs_add_chips_" + n) for p, g, n in zip(big_parts, big_recv, big_names)]
    g_w1, g_w2, g_wout = _rs_join(big_halves, "rs_join_big")
    gx, d_win, d_wuq, d_wukv, acc_in = _inproj_bwd(x2, ctx2, mod_a, z, dyv, dgb, dx1, dqt, dk, dv, cos_t, sin_a, sin_b,
                                                   w_in_p, w_uq_p, w_ukv_p, q_norm_g, kv_norm_g, cw_full)

    pad_row = lambda a: jnp.pad(a, ((0, 0), (0, D_MODEL - a.shape[1])))
    sv = jnp.concatenate([
        acc_in[0:2], acc_mlp[5:6], acc_mlp[2:4], acc_mlp[1:2],
        acc_in[2:4], acc_mlp[0:1], acc_in[4:5], acc_in[5:6], acc_in[6:9], acc_mlp[4:5],
        jnp.zeros((1, D_MODEL), F32)], axis=0)
    all_sv, red, o_cc, o_b, o_q, o_k, o_gf = _small_exchange(
        sv, w_mod[0], cctx_row, m_c_ctx.reshape(1, D_MODEL), v_c_ctx.reshape(1, D_MODEL),
        b_mod.reshape(6, D_MODEL), m_b_mod.reshape(6, D_MODEL), v_b_mod.reshape(6, D_MODEL),
        q_norm_g, m_q_norm_g, v_q_norm_g, kv_norm_g, m_kv_norm_g, v_kv_norm_g,
        gf_row, m_final_norm_g.reshape(1, D_MODEL), v_final_norm_g.reshape(1, D_MODEL))
    loss = red[14, 0]

    c9 = jnp.concatenate([c8[0::8], jnp.zeros((7, D_MODEL), F32)], axis=0)
    s_t = jnp.transpose(c9 * jax.nn.sigmoid(c9))
    dm_ex = all_sv[:, 0:6, :].reshape(8, 6 * D_MODEL)
    dm_ctx = jnp.concatenate([red[6:8].reshape(1, 2 * D_MODEL), jnp.zeros((1, 4 * D_MODEL), F32)], axis=1)
    dm16 = jnp.concatenate([dm_ex, dm_ctx, jnp.zeros((7, 6 * D_MODEL), F32)], axis=0)
    dm_sh = lax.dynamic_slice(dm16, (0, j * ncol), (16, ncol))
    g_wmod, d_wmod, nm_wmod, nv_wmod = _wmod_update(s_t, dm_sh, w_mod[0], m_w_mod[0], v_w_mod[0])

    g_cw = lax.dynamic_slice(red[11:14, 0:CONV_W], (0, j * 128), (3, 128))
    d_cw, nm_cw, nv_cw = _adamw_call(conv_w[0], g_cw, m_conv_w[0], v_conv_w[0], "adamw_conv")

    d_win_f = jnp.concatenate([d_win[:, 0:384], d_win[:, 448:480], d_win[:, 512:]], axis=1)
    d_wuq_f = d_wuq.reshape(Q_RANK, N_HEADS, HEAD_PAD)[:, :, 0:QK_DIM].reshape(Q_RANK, N_HEADS * QK_DIM)
    d_wukv3 = d_wukv.reshape(KV_RANK, 2, N_HEADS, HEAD_PAD)
    d_wukv_f = jnp.concatenate([d_wukv3[:, 0, :, 0:QK_NOPE], d_wukv3[:, 1, :, 0:V_DIM]], axis=2).reshape(KV_RANK, -1)
    rest = tuple(_cols_to_shards(a).astype(BF16) for a in (d_win_f, d_wuq_f, d_wukv_f))
    rest_names = ("in", "uq", "ukv")
    rest_got = _rs_sibling(rest, "rs_sibling_rest")
    rest_parts = [_add_pairs(a, g, c_idx, "rs_add_pairs_" + n) for a, g, n in zip(rest, rest_got, rest_names)]
    rest_recv = _rs_chips(rest_parts, "rs_chips_rest")
    rest_halves = [_add_chips(p, g, j_idx, "rs_add_chips_" + n) for p, g, n in zip(rest_parts, rest_recv, rest_names)]
    g_win, g_wuq, g_wukv = _rs_join(rest_halves, "rs_join_rest")
    upd = {}
    for name, w_, g_, m_, v_ in (("in", w_in, g_win, m_w_in, v_w_in), ("uq", w_uq, g_wuq, m_w_uq, v_w_uq),
                                 ("ukv", w_ukv, g_wukv, m_w_ukv, v_w_ukv), ("out", w_out, g_wout, m_w_out, v_w_out),
                                 ("mlp1", w_mlp1, g_w1, m_w_mlp1, v_w_mlp1), ("mlp2", w_mlp2, g_w2, m_w_mlp2, v_w_mlp2)):
        upd[name] = _adamw_call(w_[0], g_, m_[0], v_[0], "adamw_" + name)

    def four(o4, shape):
        return [o4[r].reshape(shape) for r in range(4)]

    cc4 = four(o_cc, (D_MODEL,))
    b4 = [o_b[r].reshape(1, 6 * D_MODEL) for r in range(4)]
    q4 = four(o_q, (1, Q_RANK))
    k4 = four(o_k, (1, KV_RANK))
    gf4 = four(o_gf, (D_MODEL,))
    big = {"in": g_win, "uq": g_wuq, "ukv": g_wukv, "out": g_wout, "mlp1": g_w1, "mlp2": g_w2}

    def leaf(idx):
        wm = (g_wmod, d_wmod, nm_wmod, nv_wmod)[idx]
        cwv = (g_cw, d_cw, nm_cw, nv_cw)[idx]
        bigv = {n: (big[n] if idx == 0 else upd[n][idx - 1]) for n in big}
        return [cc4[idx], wm[None], b4[idx], bigv["in"][None], q4[idx], bigv["uq"][None], k4[idx], bigv["ukv"][None],
                cwv[None], bigv["out"][None], bigv["mlp1"][None], bigv["mlp2"][None], gf4[idx]]

    return (loss, gx[None], *leaf(0), *leaf(1), *leaf(2), *leaf(3))
```

```python
import functools
import math

import jax
import jax.numpy as jnp
from jax import lax
from jax.experimental import pallas as pl
from jax.experimental.pallas import tpu as pltpu

F32 = jnp.float32
BF16 = jnp.bfloat16
MESH = pl.DeviceIdType.MESH
HIGHEST = lax.Precision.HIGHEST

D_MODEL = 1024
N_HEADS = 8
QK_NOPE = 64
QK_ROPE = 32
QK_DIM = QK_NOPE + QK_ROPE
V_DIM = 64
Q_RANK = 256
KV_RANK = 128
CONV_W = 512
D_FF = 4096
GRID_W = 64
ROPE_THETA = 10000.0
EPS = 1e-6
ATTN_SCALE = 1.0 / math.sqrt(QK_DIM)
HEAD_PAD = 128
Z_COLS = 2048
ROPE_LANE0 = QK_NOPE
N_SHARD = 4
TOK_TILE = 256
FF_CHUNK = 1024
KEY_CHUNK = 512
ATTN_FWD_Q_BLOCK = 512
ATTN_HEADS_PER_STEP = 4
ATTN_BWD_HEADS_PER_STEP = 2
ATTN_BWD_Q_BLOCK = 512
KEY_CHUNK_BWD = 512

ADAM_LR = 0.001
ADAM_B1 = 0.9
ADAM_B2 = 0.999
ADAM_EPS = 1e-08
ADAM_WD = 0.01
ADAM_STEP = 10

LOG2E = 1.4426950408889634

VMEM_LIMIT = 56 * 1024 * 1024
STAGE_VMEM_LIMIT = 32 * 1024 * 1024


def _pos():
    return lax.axis_index("x"), lax.axis_index("y"), lax.axis_index("c")


def _dot(a, b):
    return jnp.dot(a, b, preferred_element_type=F32)


def _dot_nt(a, b):
    return lax.dot_general(a, b, (((1,), (1,)), ((), ())), preferred_element_type=F32)


def _dot_tn(a, b):
    return lax.dot_general(a, b, (((0,), (0,)), ((), ())), preferred_element_type=F32)


def _rope(v, cos, sa, sb):
    return v * cos + pltpu.roll(v, 8, 1) * sa + pltpu.roll(v, HEAD_PAD - 8, 1) * sb


def _unrope(g, cos, sa, sb):
    return g * cos + pltpu.roll(g * sa, HEAD_PAD - 8, 1) + pltpu.roll(g * sb, 8, 1)


def _sigmoid(v):
    return 1.0 / (1.0 + jnp.exp(-v))


def _adamw(w, g, m, v):
    m = ADAM_B1 * m + (1.0 - ADAM_B1) * g
    v = ADAM_B2 * v + (1.0 - ADAM_B2) * (g * g)
    m_hat = m / (1.0 - ADAM_B1 ** ADAM_STEP)
    v_hat = v / (1.0 - ADAM_B2 ** ADAM_STEP)
    delta = -ADAM_LR * (m_hat / (jnp.sqrt(v_hat) + ADAM_EPS) + ADAM_WD * w)
    return delta, m, v


def _shift_rows(u, prev_row, next_row):
    n = u.shape[0]
    rows = lax.broadcasted_iota(jnp.int32, u.shape, 0)
    um1 = jnp.where(rows == 0, prev_row, pltpu.roll(u, 1, 0))
    up1 = jnp.where(rows == n - 1, next_row, pltpu.roll(u, n - 1, 0))
    return um1, up1


def _const_spec(shape):
    nd = len(shape)
    return pl.BlockSpec(shape, lambda *_: (0,) * nd)


def _resident_spec(shape):
    nd = len(shape)
    return pl.BlockSpec(shape, lambda *_: (0,) * nd, pipeline_mode=pl.Buffered(1))


def _peer(r, x, y, c):
    px = 1 - x if r & 4 else x
    py = 1 - y if r & 2 else y
    pc = 1 - c if r & 1 else c
    return (px, py, pc)


def _mod_exchange(c_row, cctx_row, w_mod_sh, b_sh, cw_sh):
    ncol = w_mod_sh.shape[1]

    def body(c_ref, cctx_ref, w_ref, b_ref, cw_ref, c8_ref, m_ref, mine_ref, msh_ref, ssem, rsem, ssem2, rsem2):
        x, y, c = _pos()
        me = 4 * x + 2 * y + c
        j = 2 * x + y
        mine_ref[...] = jnp.zeros(mine_ref.shape, F32)
        mine_ref[0:1, :] = c_ref[...]
        my_rows = pl.ds(pl.multiple_of(8 * me, 8), 8)
        sends = []
        for r in range(1, 8):
            cp = pltpu.make_async_remote_copy(
                src_ref=mine_ref, dst_ref=c8_ref.at[my_rows], send_sem=ssem.at[r - 1], recv_sem=rsem.at[r - 1],
                device_id=_peer(r, x, y, c), device_id_type=MESH)
            cp.start()
            sends.append(cp)
        for cp in sends:
            cp.wait()
        c8_ref[my_rows, :] = mine_ref[...]
        c8_ref[64:72, :] = jnp.zeros((8, D_MODEL), F32)
        c8_ref[64:65, :] = cctx_ref[...]
        cv = c8_ref[...]
        s = cv * _sigmoid(cv)
        m = jnp.dot(s, w_ref[...], precision=HIGHEST, preferred_element_type=F32) + b_ref[...]
        msh_ref[0:64, :] = m[0:64, :]
        msh_ref[64:72, :] = jnp.zeros((8, ncol), F32)
        msh_ref[64:65, :] = m[64:65, :]
        msh_ref[65:68, 0:128] = cw_ref[0:3, :]
        m_ref[j, 0:8, :] = msh_ref[my_rows, :]
        m_ref[j, 8:16, :] = msh_ref[64:72, :]
        sends2 = []
        for k, (px, py) in enumerate(_chips(x, y)):
            theirs = pl.ds(pl.multiple_of(8 * (4 * px + 2 * py + c), 8), 8)
            for half, src in enumerate((msh_ref.at[theirs], msh_ref.at[64:72])):
                cp = pltpu.make_async_remote_copy(
                    src_ref=src, dst_ref=m_ref.at[j, 8 * half:8 * half + 8], send_sem=ssem2.at[2 * k + half],
                    recv_sem=rsem2.at[2 * k + half], device_id=(px, py, c), device_id_type=MESH)
                cp.start()
                sends2.append(cp)
        for cp in sends2:
            cp.wait()

    vm = pl.BlockSpec(memory_space=pltpu.VMEM)
    return pl.pallas_call(
        body, name="mod_exchange",
        out_shape=(jax.ShapeDtypeStruct((72, D_MODEL), F32), jax.ShapeDtypeStruct((N_SHARD, 16, ncol), F32)),
        in_specs=[vm] * 5, out_specs=(vm, vm),
        scratch_shapes=[pltpu.VMEM((8, D_MODEL), F32), pltpu.VMEM((72, ncol), F32),
                        pltpu.SemaphoreType.DMA((7,)), pltpu.SemaphoreType.DMA((7,)),
                        pltpu.SemaphoreType.DMA((6,)), pltpu.SemaphoreType.DMA((6,))],
        compiler_params=pltpu.CompilerParams(vmem_limit_bytes=VMEM_LIMIT),
    )(c_row, cctx_row, w_mod_sh, b_sh, cw_sh)


def _chips(x, y):
    return [(1 - x, y), (x, 1 - y), (1 - x, 1 - y)]


def _halves(ref, c, align):
    hr = ref.shape[-2] // 2
    return (pl.ds(pl.multiple_of(c * hr, align), hr), pl.ds(pl.multiple_of((1 - c) * hr, align), hr))


class _ShardGather:
    def __init__(self, refs, ssem, rsem, fsend, frecv):
        self.refs, self.sems = refs, (ssem, rsem, fsend, frecv)
        self.x, self.y, self.c = _pos()
        self.j = 2 * self.x + self.y

    def _ici(self, a, k, slot):
        g = self.refs[a]
        ssem, rsem, _, _ = self.sems
        mine, _ = _halves(g, self.c, 16)
        px, py = _chips(self.x, self.y)[k]
        return pltpu.make_async_remote_copy(
            src_ref=g.at[self.j, mine], dst_ref=g.at[slot, mine], send_sem=ssem.at[3 * a + k],
            recv_sem=rsem.at[3 * a + k], device_id=(px, py, self.c), device_id_type=MESH)

    def _d2d(self, a, k, to_other_half):
        g = self.refs[a]
        _, _, fsend, frecv = self.sems
        mine, theirs = _halves(g, self.c, 16)
        px, py = _chips(self.x, self.y)[k]
        jk = 2 * px + py
        return pltpu.make_async_remote_copy(
            src_ref=g.at[jk, mine], dst_ref=g.at[jk, theirs if to_other_half else mine],
            send_sem=fsend.at[3 * a + k], recv_sem=frecv.at[3 * a + k],
            device_id=(self.x, self.y, 1 - self.c), device_id_type=MESH)

    def start(self):
        for a in range(len(self.refs)):
            for k in range(3):
                self._ici(a, k, self.j).start()

    def forward(self):
        for a in range(len(self.refs)):
            for k, (px, py) in enumerate(_chips(self.x, self.y)):
                self._ici(a, k, 2 * px + py).wait_recv()
                self._d2d(a, k, False).start()

    def finish(self):
        for a in range(len(self.refs)):
            for k in range(3):
                self._d2d(a, k, True).wait()
                self._ici(a, k, self.j).wait_send()


def _gather_sems(n_arrays):
    return [pltpu.SemaphoreType.DMA((3 * n_arrays,)) for _ in range(4)]


def _weight_prep(srcs, n_gather):
    n = len(srcs)
    n_split = 4

    def body(*refs):
        ins, outs, f32s, bfs = refs[:n], refs[n:2 * n], refs[2 * n:3 * n], refs[3 * n:4 * n]
        lsem_in, lsem_out = refs[4 * n], refs[4 * n + 1]
        x, y, c = _pos()
        j = 2 * x + y

        def pieces(rows):
            step = rows // n_split
            return [pl.ds(q * step, step) for q in range(n_split)]

        for t in range(n):
            for sl in pieces(ins[t].shape[0]):
                pltpu.make_async_copy(ins[t].at[sl], f32s[t].at[sl], lsem_in.at[t]).start()
        gather = _ShardGather(outs[:n_gather], *refs[4 * n + 2:])
        for t in range(n):
            pltpu.make_async_copy(ins[t], f32s[t], lsem_in.at[t]).wait()
            bfs[t][...] = f32s[t][...].astype(BF16)
            for sl in pieces(ins[t].shape[0]):
                pltpu.make_async_copy(bfs[t].at[sl], outs[t].at[j, sl], lsem_out.at[t]).start()
            if t == n_gather - 1:
                for u in range(n_gather):
                    pltpu.make_async_copy(bfs[u], outs[u].at[j], lsem_out.at[u]).wait()
                gather.start()
        gather.forward()
        gather.finish()
        for t in range(n_gather, n):
            pltpu.make_async_copy(bfs[t], outs[t].at[j], lsem_out.at[t]).wait()

    hbm = pl.BlockSpec(memory_space=pl.ANY)
    return pl.pallas_call(
        body, name="weight_prep",
        out_shape=tuple(jax.ShapeDtypeStruct((N_SHARD,) + a.shape, BF16) for a in srcs),
        in_specs=[hbm] * n, out_specs=(hbm,) * n,
        scratch_shapes=[pltpu.VMEM(a.shape, F32) for a in srcs] + [pltpu.VMEM(a.shape, BF16) for a in srcs]
        + [pltpu.SemaphoreType.DMA((n,)), pltpu.SemaphoreType.DMA((n,))] + _gather_sems(n_gather),
        compiler_params=pltpu.CompilerParams(vmem_limit_bytes=STAGE_VMEM_LIMIT),
    )(*srcs)


def _rs_sibling(arrs, name):
    n = len(arrs)

    def body(*refs):
        g_refs, got_refs, ssem, rsem = refs[:n], refs[n:2 * n], refs[2 * n], refs[2 * n + 1]
        x, y, c = _pos()
        for a in range(n):
            _, theirs = _halves(g_refs[a], c, 16)
            for s in range(N_SHARD):
                pltpu.make_async_remote_copy(
                    src_ref=g_refs[a].at[s, theirs], dst_ref=got_refs[a].at[s], send_sem=ssem.at[a],
                    recv_sem=rsem.at[a], device_id=(x, y, 1 - c), device_id_type=MESH).start()
        for a in range(n):
            _, theirs = _halves(g_refs[a], c, 16)
            pltpu.make_async_remote_copy(
                src_ref=g_refs[a].at[:, theirs], dst_ref=got_refs[a], send_sem=ssem.at[a], recv_sem=rsem.at[a],
                device_id=(x, y, 1 - c), device_id_type=MESH).wait()

    hbm = pl.BlockSpec(memory_space=pl.ANY)
    return pl.pallas_call(
        body, name=name,
        out_shape=tuple(jax.ShapeDtypeStruct((N_SHARD, a.shape[1] // 2, a.shape[2]), BF16) for a in arrs),
        in_specs=[hbm] * n, out_specs=(hbm,) * n,
        scratch_shapes=[pltpu.SemaphoreType.DMA((n,)), pltpu.SemaphoreType.DMA((n,))],
    )(*arrs)


class _ChipScatter:
    def __init__(self, parts, gots, ssem, rsem):
        self.parts, self.gots, self.ssem, self.rsem = parts, gots, ssem, rsem
        self.x, self.y, self.c = _pos()

    def _copy(self, a, k):
        px, py = _chips(self.x, self.y)[k]
        return pltpu.make_async_remote_copy(
            src_ref=self.parts[a].at[2 * px + py], dst_ref=self.gots[a].at[k], send_sem=self.ssem.at[3 * a + k],
            recv_sem=self.rsem.at[3 * a + k], device_id=(px, py, self.c), device_id_type=MESH)

    def start(self):
        for a in range(len(self.parts)):
            for k in range(3):
                self._copy(a, k).start()

    def finish(self):
        for a in range(len(self.parts)):
            for k in range(3):
                self._copy(a, k).wait()


def _rs_chips(parts, name):
    n = len(parts)

    def body(*refs):
        scatter = _ChipScatter(refs[:n], refs[n:2 * n], refs[2 * n], refs[2 * n + 1])
        scatter.start()
        scatter.finish()

    hbm = pl.BlockSpec(memory_space=pl.ANY)
    return pl.pallas_call(
        body, name=name, out_shape=tuple(jax.ShapeDtypeStruct((3,) + p.shape[1:], BF16) for p in parts),
        in_specs=[hbm] * n, out_specs=(hbm,) * n,
        scratch_shapes=[pltpu.SemaphoreType.DMA((3 * n,)), pltpu.SemaphoreType.DMA((3 * n,))],
    )(*parts)


def _rs_join(halves, name):
    n = len(halves)

    def body(*refs):
        h_refs, f_refs, stages = refs[:n], refs[n:2 * n], refs[2 * n:3 * n]
        lsem_in, lsem_out, ssem, rsem = refs[3 * n:]
        x, y, c = _pos()
        remote = []
        for a in range(n):
            mine, _ = _halves(f_refs[a], c, 8)
            cp = pltpu.make_async_remote_copy(
                src_ref=h_refs[a], dst_ref=f_refs[a].at[mine], send_sem=ssem.at[a], recv_sem=rsem.at[a],
                device_id=(x, y, 1 - c), device_id_type=MESH)
            cp.start()
            remote.append(cp)
            pltpu.make_async_copy(h_refs[a], stages[a], lsem_in.at[a]).start()
        local = []
        for a in range(n):
            mine, _ = _halves(f_refs[a], c, 8)
            pltpu.make_async_copy(h_refs[a], stages[a], lsem_in.at[a]).wait()
            cp = pltpu.make_async_copy(stages[a], f_refs[a].at[mine], lsem_out.at[a])
            cp.start()
            local.append(cp)
        for cp in remote + local:
            cp.wait()

    hbm = pl.BlockSpec(memory_space=pl.ANY)
    return pl.pallas_call(
        body, name=name,
        out_shape=tuple(jax.ShapeDtypeStruct((2 * h.shape[0], h.shape[1]), F32) for h in halves),
        in_specs=[hbm] * n, out_specs=(hbm,) * n,
        scratch_shapes=[pltpu.VMEM(h.shape, F32) for h in halves]
        + [pltpu.SemaphoreType.DMA((n,)) for _ in range(4)],
        compiler_params=pltpu.CompilerParams(vmem_limit_bytes=STAGE_VMEM_LIMIT),
    )(*halves)


def _row_block(rows):
    return max(b for b in range(16, 513, 16) if rows % b == 0)


def _add_pairs(arr, got, c_idx, name):
    hr, cols = got.shape[1], got.shape[2]
    rb = _row_block(hr)
    nb = hr // rb

    def body(c_ref, a_ref, b_ref, o_ref):
        o_ref[...] = (a_ref[...].astype(F32) + b_ref[...].astype(F32)).astype(BF16)

    spec = pl.BlockSpec((1, rb, cols), lambda s, r, c_ref: (s, r, 0))
    grid_spec = pltpu.PrefetchScalarGridSpec(
        num_scalar_prefetch=1, grid=(N_SHARD, nb),
        in_specs=[pl.BlockSpec((1, rb, cols), lambda s, r, c_ref: (s, c_ref[0] * nb + r, 0)), spec],
        out_specs=spec)
    return pl.pallas_call(
        body, name=name, grid_spec=grid_spec, out_shape=jax.ShapeDtypeStruct(got.shape, BF16),
    )(c_idx, arr, got)


def _add_chips(part, got, j_idx, name):
    hr, cols = got.shape[1], got.shape[2]
    rb = _row_block(hr)

    def body(j_ref, p_ref, g_ref, o_ref):
        acc = p_ref[0].astype(F32)
        for k in range(3):
            acc = acc + g_ref[k].astype(F32)
        o_ref[...] = acc

    grid_spec = pltpu.PrefetchScalarGridSpec(
        num_scalar_prefetch=1, grid=(hr // rb,),
        in_specs=[pl.BlockSpec((1, rb, cols), lambda r, j_ref: (j_ref[0], r, 0)),
                  pl.BlockSpec((3, rb, cols), lambda r, j_ref: (0, r, 0))],
        out_specs=pl.BlockSpec((rb, cols), lambda r, j_ref: (r, 0)))
    return pl.pallas_call(
        body, name=name, grid_spec=grid_spec, out_shape=jax.ShapeDtypeStruct((hr, cols), F32),
    )(j_idx, part, got)


def _small_exchange(sv, w_mod_sh, cctx, m_cctx, v_cctx, bmod, m_bmod, v_bmod, qg, m_qg, v_qg, kvg, m_kvg, v_kvg,
                    gf, m_gf, v_gf):
    ncol = w_mod_sh.shape[1]

    def body(sv_ref, w_ref, cctx_ref, mcc_ref, vcc_ref, b_ref, mb_ref, vb_ref, qg_ref, mq_ref, vq_ref,
             kg_ref, mk_ref, vk_ref, gf_ref, mgf_ref, vgf_ref,
             all_ref, red_ref, occ_ref, ob_ref, oq_ref, ok_ref, ogf_ref,
             vec_ref, part_ref, ssem, rsem, ssem2, rsem2):
        x, y, c = _pos()
        me = 4 * x + 2 * y + c
        j = 2 * x + y
        sends = []
        for r in range(1, 8):
            cp = pltpu.make_async_remote_copy(
                src_ref=sv_ref, dst_ref=all_ref.at[me], send_sem=ssem.at[r - 1], recv_sem=rsem.at[r - 1],
                device_id=_peer(r, x, y, c), device_id_type=MESH)
            cp.start()
            sends.append(cp)
        for cp in sends:
            cp.wait()
        all_ref[me] = sv_ref[...]
        red = all_ref[0]
        for d in range(1, 8):
            red = red + all_ref[d]
        red_ref[...] = red
        vec_ref[...] = jnp.zeros(vec_ref.shape, F32)

        @pl.when(j == 0)
        def _():
            vec_ref[0:1, 0:1024] = red[6:7, :]
            vec_ref[0:1, 1024:1536] = red[7:8, 0:512]

        @pl.when(j == 1)
        def _():
            vec_ref[0:1, 0:512] = red[7:8, 512:1024]

        part = lax.dot_general(vec_ref[...], w_ref[...], (((1,), (1,)), ((), ())), precision=HIGHEST,
                               preferred_element_type=F32)
        part_ref[j] = part
        sends2 = []
        for k, r in enumerate((4, 2, 6)):
            cp = pltpu.make_async_remote_copy(
                src_ref=part_ref.at[j], dst_ref=part_ref.at[j], send_sem=ssem2.at[k], recv_sem=rsem2.at[k],
                device_id=_peer(r, x, y, c), device_id_type=MESH)
            cp.start()
            sends2.append(cp)
        for cp in sends2:
            cp.wait()
        tot = part_ref[0]
        for s in range(1, N_SHARD):
            tot = tot + part_ref[s]
        cc = cctx_ref[...]
        sg = _sigmoid(cc)
        g_cc = tot[0:1, :] * (sg * (1.0 + cc * (1.0 - sg)))
        d_, m_, v_ = _adamw(cc, g_cc, mcc_ref[...], vcc_ref[...])
        occ_ref[0:1, :] = g_cc
        occ_ref[1:2, :] = d_
        occ_ref[2:3, :] = m_
        occ_ref[3:4, :] = v_
        occ_ref[4:8, :] = jnp.zeros((4, D_MODEL), F32)
        g_b = red[0:6, :]
        pad = jnp.concatenate([red[6:8, :], jnp.zeros((4, D_MODEL), F32)], axis=0)
        g_b = g_b + pad
        d_, m_, v_ = _adamw(b_ref[...], g_b, mb_ref[...], vb_ref[...])
        ob_ref[0] = g_b
        ob_ref[1] = d_
        ob_ref[2] = m_
        ob_ref[3] = v_
        g_q = red[9:10, 0:Q_RANK]
        d_, m_, v_ = _adamw(qg_ref[...], g_q, mq_ref[...], vq_ref[...])
        oq_ref[0:1, :] = g_q
        oq_ref[1:2, :] = d_
        oq_ref[2:3, :] = m_
        oq_ref[3:4, :] = v_
        oq_ref[4:8, :] = jnp.zeros((4, Q_RANK), F32)
        g_k = red[10:11, 0:KV_RANK]
        d_, m_, v_ = _adamw(kg_ref[...], g_k, mk_ref[...], vk_ref[...])
        ok_ref[0:1, :] = g_k
        ok_ref[1:2, :] = d_
        ok_ref[2:3, :] = m_
        ok_ref[3:4, :] = v_
        ok_ref[4:8, :] = jnp.zeros((4, KV_RANK), F32)
        g_f = red[8:9, :]
        d_, m_, v_ = _adamw(gf_ref[...], g_f, mgf_ref[...], vgf_ref[...])
        ogf_ref[0:1, :] = g_f
        ogf_ref[1:2, :] = d_
        ogf_ref[2:3, :] = m_
        ogf_ref[3:4, :] = v_
        ogf_ref[4:8, :] = jnp.zeros((4, D_MODEL), F32)

    vm = pl.BlockSpec(memory_space=pltpu.VMEM)
    out_shape = (
        jax.ShapeDtypeStruct((8, 16, D_MODEL), F32),
        jax.ShapeDtypeStruct((16, D_MODEL), F32),
        jax.ShapeDtypeStruct((8, D_MODEL), F32),
        jax.ShapeDtypeStruct((4, 6, D_MODEL), F32),
        jax.ShapeDtypeStruct((8, Q_RANK), F32),
        jax.ShapeDtypeStruct((8, KV_RANK), F32),
        jax.ShapeDtypeStruct((8, D_MODEL), F32),
    )
    return pl.pallas_call(
        body, name="small_exchange", out_shape=out_shape, in_specs=[vm] * 17, out_specs=tuple([vm] * 7),
        scratch_shapes=[pltpu.VMEM((8, ncol), F32), pltpu.VMEM((N_SHARD, 8, D_MODEL), F32),
                        pltpu.SemaphoreType.DMA((7,)), pltpu.SemaphoreType.DMA((7,)),
                        pltpu.SemaphoreType.DMA((3,)), pltpu.SemaphoreType.DMA((3,))],
        compiler_params=pltpu.CompilerParams(vmem_limit_bytes=VMEM_LIMIT),
    )(sv, w_mod_sh, cctx, m_cctx, v_cctx, bmod, m_bmod, v_bmod, qg, m_qg, v_qg, kvg, m_kvg, v_kvg, gf, m_gf, v_gf)


def _inproj_fwd(x2, ctx2, mod_a, w_in, qg, kvg, w_uq, w_ukv, cos_t, sin_a, sin_b):
    t_lat, t_ctx = x2.shape[0], ctx2.shape[0]
    tm = TOK_TILE
    n_lat = t_lat // tm
    n_all = n_lat + t_ctx // tm
    e_rows = t_lat + t_ctx

    def body(x_ref, ctx_ref, mod_ref, win_ref, qg_ref, kvg_ref, wuq_ref, wukv_ref, cos_ref, sa_ref, sb_ref,
             z_ref, q_ref, k_ref, v_ref, kt_ref):
        i = pl.program_id(0)
        xin = jnp.where(i < n_lat, x_ref[...], ctx_ref[...])
        xn = xin * lax.rsqrt(jnp.mean(xin * xin, axis=-1, keepdims=True) + EPS)
        h1 = (xn * (1.0 + mod_ref[0, 1:2, :]) + mod_ref[0, 0:1, :]).astype(BF16)
        z = _dot(h1, win_ref[...])
        z_ref[...] = z
        cos, sa, sb = cos_ref[...], sa_ref[...], sb_ref[...]
        cq = z[:, 0:Q_RANK]
        cqn = (cq * lax.rsqrt(jnp.mean(cq * cq, axis=-1, keepdims=True) + EPS) * qg_ref[...]).astype(BF16)
        q = _dot(cqn, wuq_ref[...])
        ckv = z[:, Q_RANK:Q_RANK + KV_RANK]
        ckvn = (ckv * lax.rsqrt(jnp.mean(ckv * ckv, axis=-1, keepdims=True) + EPS) * kvg_ref[...]).astype(BF16)
        kv = _dot(ckvn, wukv_ref[...])
        kr = _rope(z[:, Q_RANK + KV_RANK:Q_RANK + KV_RANK + HEAD_PAD], cos, sa, sb)
        ones_lane = lax.broadcasted_iota(jnp.int32, (tm, HEAD_PAD), 1) == V_DIM
        for h in range(N_HEADS):
            lo = h * HEAD_PAD
            q_ref[h] = _rope(q[:, lo:lo + HEAD_PAD], cos, sa, sb).astype(BF16)
            kh = kv[:, lo:lo + HEAD_PAD] + kr
            k_ref[h] = kh.astype(BF16)
            kt_ref[h] = kh.T.astype(BF16)
            vh = kv[:, N_HEADS * HEAD_PAD + lo:N_HEADS * HEAD_PAD + lo + HEAD_PAD]
            v_ref[h] = jnp.where(ones_lane, 1.0, vh).astype(BF16)

    row = lambda i: (i, 0)
    head_spec = pl.BlockSpec((N_HEADS, tm, HEAD_PAD), lambda i: (0, i, 0))
    head_shape = jax.ShapeDtypeStruct((N_HEADS, e_rows, HEAD_PAD), BF16)
    return pl.pallas_call(
        body, name="inproj_fwd", grid=(n_all,),
        out_shape=(jax.ShapeDtypeStruct((e_rows, Z_COLS), F32), head_shape, head_shape, head_shape,
                   jax.ShapeDtypeStruct((N_HEADS, HEAD_PAD, e_rows), BF16)),
        in_specs=[
            pl.BlockSpec((tm, D_MODEL), lambda i: (jnp.minimum(i, n_lat - 1), 0)),
            _const_spec((tm, D_MODEL)),
            pl.BlockSpec((1, 8, D_MODEL), lambda i: (i // n_lat, 0, 0)),
            _const_spec(w_in.shape), _const_spec(qg.shape), _const_spec(kvg.shape),
            _const_spec(w_uq.shape), _const_spec(w_ukv.shape),
            pl.BlockSpec((tm, HEAD_PAD), row), pl.BlockSpec((tm, HEAD_PAD), row), pl.BlockSpec((tm, HEAD_PAD), row),
        ],
        out_specs=(pl.BlockSpec((tm, Z_COLS), row), head_spec, head_spec, head_spec,
                   pl.BlockSpec((N_HEADS, HEAD_PAD, tm), lambda i: (0, 0, i))),
        compiler_params=pltpu.CompilerParams(vmem_limit_bytes=VMEM_LIMIT),
    )(x2, ctx2, mod_a, w_in, qg, kvg, w_uq, w_ukv, cos_t, sin_a, sin_b)


def _key_chunks(e_rows, size):
    n_chunks = max(1, e_rows // size)
    return [(ci * size, size if ci < n_chunks - 1 else e_rows - ci * size) for ci in range(n_chunks)]


def _attn_fwd(q, k, v, t_lat, shard_arrays):
    e_rows = k.shape[1]
    tq = min(t_lat, ATTN_FWD_Q_BLOCK)
    bounds = _key_chunks(e_rows, KEY_CHUNK)
    c2 = ATTN_SCALE * LOG2E

    hb = ATTN_HEADS_PER_STEP
    n_hb = N_HEADS // hb

    def body(q_ref, k_ref, v_ref, o_ref, lse_ref):
        qs = [q_ref[b] for b in range(hb)]
        m, acc = [None] * hb, [None] * hb
        for lo, n in bounds:
            for b in range(hb):
                s = _dot_nt(qs[b], k_ref[b, lo:lo + n, :])
                mc = jnp.max(s, axis=-1, keepdims=True)
                m_new = mc if m[b] is None else jnp.maximum(m[b], mc)
                p = jnp.exp2((s - m_new) * c2)
                pv = _dot(p.astype(BF16), v_ref[b, lo:lo + n, :])
                acc[b] = pv if m[b] is None else acc[b] * jnp.exp2((m[b] - m_new) * c2) + pv
                m[b] = m_new
        for b in range(hb):
            l = acc[b][:, V_DIM:V_DIM + 1]
            o_ref[b] = (acc[b] * (1.0 / l)).astype(BF16)
            lse = (m[b] * ATTN_SCALE + jnp.log(l)) * LOG2E
            lse_ref[b] = jnp.broadcast_to(lse, (tq, HEAD_PAD)).T[0:1, :]

    n_w = len(shard_arrays)
    n_q = t_lat // tq

    def body_with_gather(q_ref, k_ref, v_ref, *rest):
        o_ref, lse_ref = rest[n_w], rest[n_w + 1]
        gather = _ShardGather(rest[n_w + 2:2 * n_w + 2], *rest[2 * n_w + 2:])
        h, i = pl.program_id(0), pl.program_id(1)
        pl.when(jnp.logical_and(h == 0, i == 0))(gather.start)
        pl.when(jnp.logical_and(h == n_hb // 2, i == 0))(gather.forward)
        body(q_ref, k_ref, v_ref, o_ref, lse_ref)
        pl.when(jnp.logical_and(h == n_hb - 1, i == n_q - 1))(gather.finish)

    hbm = pl.BlockSpec(memory_space=pl.ANY)
    return pl.pallas_call(
        body_with_gather, name="attn_fwd", grid=(n_hb, n_q),
        out_shape=(jax.ShapeDtypeStruct((N_HEADS, t_lat, HEAD_PAD), BF16),
                   jax.ShapeDtypeStruct((N_HEADS, 1, t_lat), F32))
        + tuple(jax.ShapeDtypeStruct(a.shape, a.dtype) for a in shard_arrays),
        in_specs=[pl.BlockSpec((hb, tq, HEAD_PAD), lambda h, i: (h, i, 0)),
                  pl.BlockSpec((hb, e_rows, HEAD_PAD), lambda h, i: (h, 0, 0)),
                  pl.BlockSpec((hb, e_rows, HEAD_PAD), lambda h, i: (h, 0, 0))] + [hbm] * n_w,
        out_specs=(pl.BlockSpec((hb, tq, HEAD_PAD), lambda h, i: (h, i, 0)),
                   pl.BlockSpec((hb, 1, tq), lambda h, i: (h, 0, i))) + (hbm,) * n_w,
        input_output_aliases={3 + a: 2 + a for a in range(n_w)},
        scratch_shapes=_gather_sems(n_w),
        compiler_params=pltpu.CompilerParams(vmem_limit_bytes=VMEM_LIMIT),
    )(q, k, v, *shard_arrays)


def _attn_bwd(q, k, v, kt, o, do, lse_row, t_lat, parts):
    e_rows = k.shape[1]
    tq = min(t_lat, ATTN_BWD_Q_BLOCK)
    n_p = len(parts)
    n_q = t_lat // tq
    bounds = _key_chunks(e_rows, KEY_CHUNK_BWD)

    hb = ATTN_BWD_HEADS_PER_STEP
    n_hb = N_HEADS // hb

    def body(q_ref, k_ref, v_ref, kt_ref, o_ref, do_ref, lse_ref, *rest):
        dqt_ref, dk_ref, dv_ref = rest[n_p:n_p + 3]
        scatter = _ChipScatter(rest[:n_p], rest[n_p + 3:2 * n_p + 3], rest[2 * n_p + 3], rest[2 * n_p + 4])
        h, i = pl.program_id(0), pl.program_id(1)
        pl.when(jnp.logical_and(h == 0, i == 0))(scatter.start)

        @pl.when(i == 0)
        def _():
            dk_ref[...] = jnp.zeros(dk_ref.shape, F32)
            dv_ref[...] = jnp.zeros(dv_ref.shape, F32)

        qs, dos, lses, deltas = [], [], [], []
        for b in range(hb):
            qs.append(q_ref[b])
            dos.append(do_ref[b])
            lses.append(lse_ref[b])
            prod = o_ref[b].astype(F32) * dos[b].astype(F32)
            deltas.append(lax.dot_general(jnp.ones((8, HEAD_PAD), F32), prod, (((1,), (1,)), ((), ())),
                                          precision=HIGHEST, preferred_element_type=F32)[0:1, :])
        dqt = [None] * hb
        for lo, n in bounds:
            for b in range(hb):
                pt = jnp.exp2(_dot_nt(k_ref[b, lo:lo + n, :], qs[b]) * (ATTN_SCALE * LOG2E) - lses[b])
                dpt = _dot_nt(v_ref[b, lo:lo + n, :], dos[b])
                dst = (pt * (dpt - deltas[b])).astype(BF16)
                dv_c = _dot(pt.astype(BF16), dos[b])
                dk_c = _dot(dst, qs[b])
                part = _dot(kt_ref[b, :, lo:lo + n], dst)
                dqt[b] = part if dqt[b] is None else dqt[b] + part
                dk_ref[b, lo:lo + n, :] += dk_c * ATTN_SCALE
                dv_ref[b, lo:lo + n, :] += dv_c
        for b in range(hb):
            dqt_ref[b] = dqt[b] * ATTN_SCALE

        pl.when(jnp.logical_and(h == n_hb - 1, i == n_q - 1))(scatter.finish)

    hbm = pl.BlockSpec(memory_space=pl.ANY)
    qspec = pl.BlockSpec((hb, tq, HEAD_PAD), lambda h, i: (h, i, 0))
    kspec = pl.BlockSpec((hb, e_rows, HEAD_PAD), lambda h, i: (h, 0, 0))
    return pl.pallas_call(
        body, name="attn_bwd", grid=(n_hb, n_q),
        out_shape=(jax.ShapeDtypeStruct((N_HEADS, HEAD_PAD, t_lat), F32),
                   jax.ShapeDtypeStruct((N_HEADS, e_rows, HEAD_PAD), F32),
                   jax.ShapeDtypeStruct((N_HEADS, e_rows, HEAD_PAD), F32))
        + tuple(jax.ShapeDtypeStruct((3,) + p.shape[1:], BF16) for p in parts),
        in_specs=[qspec, kspec, kspec, pl.BlockSpec((hb, HEAD_PAD, e_rows), lambda h, i: (h, 0, 0)), qspec, qspec,
                  pl.BlockSpec((hb, 1, tq), lambda h, i: (h, 0, i))] + [hbm] * n_p,
        out_specs=(pl.BlockSpec((hb, HEAD_PAD, tq), lambda h, i: (h, 0, i)), kspec, kspec) + (hbm,) * n_p,
        scratch_shapes=[pltpu.SemaphoreType.DMA((3 * n_p,)), pltpu.SemaphoreType.DMA((3 * n_p,))],
        compiler_params=pltpu.CompilerParams(vmem_limit_bytes=VMEM_LIMIT),
    )(q, k, v, kt, o, do, lse_row, *parts)


def _halo_specs(tm, col_block):
    per = tm // 8
    prev = pl.BlockSpec((8, CONV_W), lambda i: (jnp.maximum(i * per - 1, 0), col_block))
    nxt = pl.BlockSpec((8, CONV_W), lambda i: ((i + 1) * per, col_block))
    return prev, nxt


def _mlp_fwdbwd(o, z, x2, tgt, mod_b, gf, cw, wo_attn, wo_conv, w1, w2):
    t_lat = x2.shape[0]
    tm = TOK_TILE
    n_lat = t_lat // tm
    n_ff = D_FF // FF_CHUNK

    def body(o_ref, gb_ref, gc_ref, xi_ref, gcp_ref, xip_ref, gcn_ref, xin_ref, cw_ref, woa_ref, woc_ref,
             x_ref, t_ref, mod_ref, gf_ref, w1_ref, w2_ref,
             r_ref, da_ref, h2_ref, dy2_ref, dx1_ref, conv_ref, acc_ref, dy1_ref, do_ref, dgb_ref, dyv_ref, ra_ref):
        i = pl.program_id(0)

        @pl.when(i == 0)
        def _():
            acc_ref[...] = jnp.zeros(acc_ref.shape, F32)

        g1, sh2, sc2, g2 = mod_ref[0:1, :], mod_ref[1:2, :], mod_ref[2:3, :], mod_ref[3:4, :]
        u = gc_ref[...] * xi_ref[...]
        u_prev = jnp.where(i > 0, gcp_ref[7:8, :] * xip_ref[7:8, :], 0.0)
        u_next = jnp.where(i < n_lat - 1, gcn_ref[0:1, :] * xin_ref[0:1, :], 0.0)
        um1, up1 = _shift_rows(u, u_prev, u_next)
        yv = cw_ref[0:1, :] * um1 + cw_ref[1:2, :] * u + cw_ref[2:3, :] * up1
        gb = gb_ref[...]
        conv = (gb * yv).astype(BF16)
        conv_ref[...] = conv
        y1 = _dot(conv, woc_ref[...])
        for h in range(N_HEADS):
            y1 = y1 + _dot(o_ref[h], woa_ref[h])
        x1 = x_ref[...] + g1 * y1
        rstd2 = lax.rsqrt(jnp.mean(x1 * x1, axis=-1, keepdims=True) + EPS)
        xn1 = x1 * rstd2
        h2 = (xn1 * (1.0 + sc2) + sh2).astype(BF16)
        h2_ref[...] = h2
        y2 = jnp.zeros((tm, D_MODEL), F32)
        for jj in range(n_ff):
            ra = jnp.maximum(_dot(h2, w1_ref[jj]), 0.0)
            ra_ref[jj] = ra
            r = (ra * ra).astype(BF16)
            r_ref[:, jj * FF_CHUNK:(jj + 1) * FF_CHUNK] = r
            y2 = y2 + _dot(r, w2_ref[jj * FF_CHUNK:(jj + 1) * FF_CHUNK, :])
        x2v = x1 + g2 * y2
        rstd3 = lax.rsqrt(jnp.mean(x2v * x2v, axis=-1, keepdims=True) + EPS)
        xn3 = x2v * rstd3
        gfv = gf_ref[...]
        diff = xn3 * gfv - t_ref[...]
        loss_t = 0.5 * jnp.sum(jnp.sum(diff * diff, axis=-1, keepdims=True), axis=0, keepdims=True) * (1.0 / D_MODEL)
        dy = diff * (1.0 / D_MODEL)
        dxn3 = dy * gfv
        dx2 = rstd3 * (dxn3 - xn3 * jnp.mean(dxn3 * xn3, axis=-1, keepdims=True))
        dy2 = (dx2 * g2).astype(BF16)
        dy2_ref[...] = dy2
        dh2 = jnp.zeros((tm, D_MODEL), F32)
        for jj in range(n_ff):
            dr = _dot_nt(dy2, w2_ref[jj * FF_CHUNK:(jj + 1) * FF_CHUNK, :])
            da = (2.0 * ra_ref[jj] * dr).astype(BF16)
            da_ref[:, jj * FF_CHUNK:(jj + 1) * FF_CHUNK] = da
            dh2 = dh2 + _dot_nt(da, w1_ref[jj])
        dxn1 = dh2 * (1.0 + sc2)
        dx1 = dx2 + rstd2 * (dxn1 - xn1 * jnp.mean(dxn1 * xn1, axis=-1, keepdims=True))
        dx1_ref[...] = dx1
        dy1 = (dx1 * g1).astype(BF16)
        dy1_ref[...] = dy1
        for h in range(N_HEADS):
            do_ref[h] = _dot_nt(dy1, woa_ref[h]).astype(BF16)
        dconv = _dot_nt(dy1, woc_ref[...])
        dgb_ref[...] = dconv * yv
        dyv_ref[...] = dconv * gb
        acc_ref[5:6, :] += jnp.sum(dx1 * y1, axis=0, keepdims=True)
        acc_ref[0:1, :] += jnp.sum(dy * xn3, axis=0, keepdims=True)
        acc_ref[1:2, :] += jnp.sum(dx2 * y2, axis=0, keepdims=True)
        acc_ref[2:3, :] += jnp.sum(dh2, axis=0, keepdims=True)
        acc_ref[3:4, :] += jnp.sum(dh2 * xn1, axis=0, keepdims=True)
        acc_ref[4:5, :] += jnp.broadcast_to(loss_t, (1, D_MODEL))

    row = lambda i: (i, 0)
    gcp, gcn = _halo_specs(tm, 2)
    xip, xin = _halo_specs(tm, 3)
    tile = pl.BlockSpec((tm, D_MODEL), row)
    wide = pl.BlockSpec((tm, D_FF), row)
    half = pl.BlockSpec((tm, CONV_W), row)
    return pl.pallas_call(
        body, name="mlp_fwdbwd", grid=(n_lat,),
        out_shape=(jax.ShapeDtypeStruct((t_lat, D_FF), BF16), jax.ShapeDtypeStruct((t_lat, D_FF), BF16),
                   jax.ShapeDtypeStruct((t_lat, D_MODEL), BF16), jax.ShapeDtypeStruct((t_lat, D_MODEL), BF16),
                   jax.ShapeDtypeStruct((t_lat, D_MODEL), F32), jax.ShapeDtypeStruct((t_lat, CONV_W), BF16),
                   jax.ShapeDtypeStruct((8, D_MODEL), F32),
                   jax.ShapeDtypeStruct((t_lat, D_MODEL), BF16),
                   jax.ShapeDtypeStruct((N_HEADS, t_lat, HEAD_PAD), BF16),
                   jax.ShapeDtypeStruct((t_lat, CONV_W), F32), jax.ShapeDtypeStruct((t_lat, CONV_W), F32)),
        in_specs=[
            pl.BlockSpec((N_HEADS, tm, HEAD_PAD), lambda i: (0, i, 0)),
            pl.BlockSpec((tm, CONV_W), lambda i: (i, 1)), pl.BlockSpec((tm, CONV_W), lambda i: (i, 2)),
            pl.BlockSpec((tm, CONV_W), lambda i: (i, 3)),
            gcp, xip, gcn, xin,
            _const_spec(cw.shape), _resident_spec(wo_attn.shape), _resident_spec(wo_conv.shape),
            tile, tile, _const_spec(mod_b.shape), _const_spec(gf.shape),
            _resident_spec(w1.shape), _resident_spec(w2.shape),
        ],
        out_specs=(wide, wide, tile, tile, tile, half, _const_spec((8, D_MODEL)),
                   tile, pl.BlockSpec((N_HEADS, tm, HEAD_PAD), lambda i: (0, i, 0)), half, half),
        scratch_shapes=[pltpu.VMEM((n_ff, tm, FF_CHUNK), F32)],
        compiler_params=pltpu.CompilerParams(vmem_limit_bytes=VMEM_LIMIT),
    )(o, z, z, z, z, z, z, z, cw, wo_attn, wo_conv, x2, tgt, mod_b, gf, w1, w2)


def _inproj_bwd(x2, ctx2, mod_a, z, dyv, dgb, dx1, dqt, dk, dv, cos_t, sin_a, sin_b, w_in, w_uq, w_ukv, qg, kvg, cw):
    t_lat, t_ctx = x2.shape[0], ctx2.shape[0]
    tm = TOK_TILE
    n_lat = t_lat // tm
    n_all = n_lat + t_ctx // tm
    group = max(g for g in (1, 2, 4) if n_lat % g == 0)

    def body(x_ref, ctx_ref, mod_ref, z_ref, gcp_ref, xip_ref, gcn_ref, xin_ref, dyv_ref, dyvp_ref, dyvn_ref,
             dgb_ref, dx1_ref, dqt_ref, dk_ref, dv_ref, cos_ref, sa_ref, sb_ref, win_ref, wuq_ref, wukv_ref,
             qg_ref, kvg_ref, cw_ref, gx_ref, dwin_ref, dwuq_ref, dwukv_ref, acc_ref, h1_buf, dz_buf):
        i = pl.program_id(0)
        lat = i < n_lat

        @pl.when(i == 0)
        def _():
            dwin_ref[...] = jnp.zeros(dwin_ref.shape, F32)
            dwuq_ref[...] = jnp.zeros(dwuq_ref.shape, F32)
            dwukv_ref[...] = jnp.zeros(dwukv_ref.shape, F32)
            acc_ref[...] = jnp.zeros(acc_ref.shape, F32)

        xin = jnp.where(lat, x_ref[...], ctx_ref[...])
        rstd = lax.rsqrt(jnp.mean(xin * xin, axis=-1, keepdims=True) + EPS)
        xn = xin * rstd
        sc = mod_ref[0, 1:2, :]
        h1 = (xn * (1.0 + sc) + mod_ref[0, 0:1, :]).astype(BF16)
        z = z_ref[...]
        cos, sa, sb = cos_ref[...], sa_ref[...], sb_ref[...]
        qgv, kvgv = qg_ref[...], kvg_ref[...]
        cq = z[:, 0:Q_RANK]
        cqh = cq * lax.rsqrt(jnp.mean(cq * cq, axis=-1, keepdims=True) + EPS)
        rq = lax.rsqrt(jnp.mean(cq * cq, axis=-1, keepdims=True) + EPS)
        cqn = (cqh * qgv).astype(BF16)
        parts = []
        for h in range(N_HEADS):
            g = jnp.where(lat, dqt_ref[h].T, 0.0)
            parts.append(_unrope(g, cos, sa, sb))
        dq = jnp.concatenate(parts, axis=1).astype(BF16)
        dcqn = _dot_nt(dq, wuq_ref[...])
        dwuq_ref[...] += _dot_tn(cqn, dq)
        acc_ref[4:5, 0:Q_RANK] += jnp.sum(dcqn * cqh, axis=0, keepdims=True)
        dxn = dcqn * qgv
        dcq = rq * (dxn - cqh * jnp.mean(dxn * cqh, axis=-1, keepdims=True))
        ckv = z[:, Q_RANK:Q_RANK + KV_RANK]
        rk = lax.rsqrt(jnp.mean(ckv * ckv, axis=-1, keepdims=True) + EPS)
        ckvh = ckv * rk
        ckvn = (ckvh * kvgv).astype(BF16)
        dks = [dk_ref[h] for h in range(N_HEADS)]
        dkr = dks[0]
        for h in range(1, N_HEADS):
            dkr = dkr + dks[h]
        dkv = jnp.concatenate(dks + [dv_ref[h] for h in range(N_HEADS)], axis=1).astype(BF16)
        dckvn = _dot_nt(dkv, wukv_ref[...])
        dwukv_ref[...] += _dot_tn(ckvn, dkv)
        acc_ref[5:6, 0:KV_RANK] += jnp.sum(dckvn * ckvh, axis=0, keepdims=True)
        dxn = dckvn * kvgv
        dckv = rk * (dxn - ckvh * jnp.mean(dxn * ckvh, axis=-1, keepdims=True))
        dkr = _unrope(dkr, cos, sa, sb)
        gb, gc, xi = z[:, 512:1024], z[:, 1024:1536], z[:, 1536:2048]
        u = gc * xi
        u_prev = jnp.where(i > 0, gcp_ref[7:8, :] * xip_ref[7:8, :], 0.0)
        u_next = jnp.where(i < n_lat - 1, gcn_ref[0:1, :] * xin_ref[0:1, :], 0.0)
        um1, up1 = _shift_rows(u, u_prev, u_next)
        dyv = jnp.where(lat, dyv_ref[...], 0.0)
        dyv_prev = jnp.where(jnp.logical_and(i > 0, lat), dyvp_ref[7:8, :], 0.0)
        dyv_next = jnp.where(i < n_lat - 1, dyvn_ref[0:1, :], 0.0)
        dyv_m1, dyv_p1 = _shift_rows(dyv, dyv_prev, dyv_next)
        du = cw_ref[0:1, :] * dyv_p1 + cw_ref[1:2, :] * dyv + cw_ref[2:3, :] * dyv_m1
        dgc = du * xi
        dxi = du * gc
        dgb = jnp.where(lat, dgb_ref[...], 0.0)
        acc_ref[6:7, 0:CONV_W] += jnp.sum(dyv * um1, axis=0, keepdims=True)
        acc_ref[7:8, 0:CONV_W] += jnp.sum(dyv * u, axis=0, keepdims=True)
        acc_ref[8:9, 0:CONV_W] += jnp.sum(dyv * up1, axis=0, keepdims=True)
        dz = jnp.concatenate([dcq, dckv, dkr, dgb, dgc, dxi], axis=1).astype(BF16)
        dh1 = _dot_nt(dz, win_ref[...])
        slot = i % group
        rows_g = pl.ds(pl.multiple_of(slot * tm, tm), tm)
        h1_buf[rows_g, :] = h1
        dz_buf[rows_g, :] = dz

        @pl.when(jnp.logical_and(lat, slot == group - 1))
        def _():
            dwin_ref[...] += _dot_tn(h1_buf[...], dz_buf[...])

        @pl.when(jnp.logical_not(lat))
        def _():
            dwin_ref[...] += _dot_tn(h1, dz)
        s_sh = jnp.sum(dh1, axis=0, keepdims=True)
        s_sc = jnp.sum(dh1 * xn, axis=0, keepdims=True)
        zero = jnp.zeros_like(s_sh)
        acc_ref[0:1, :] += jnp.where(lat, s_sh, zero)
        acc_ref[1:2, :] += jnp.where(lat, s_sc, zero)
        acc_ref[2:3, :] += jnp.where(lat, zero, s_sh)
        acc_ref[3:4, :] += jnp.where(lat, zero, s_sc)
        dxn = dh1 * (1.0 + sc)
        dx = rstd * (dxn - xn * jnp.mean(dxn * xn, axis=-1, keepdims=True))

        @pl.when(lat)
        def _():
            gx_ref[...] = dx1_ref[...] + dx

    last = n_lat - 1
    per = tm // 8
    lat_row = lambda i: (jnp.minimum(i, last), 0)
    row = lambda i: (i, 0)
    gcp, gcn = _halo_specs(tm, 2)
    xip, xin = _halo_specs(tm, 3)
    n_halo = t_lat // 8
    dyvp = pl.BlockSpec((8, CONV_W), lambda i: (jnp.clip(i * per - 1, 0, n_halo - 1), 0))
    dyvn = pl.BlockSpec((8, CONV_W), lambda i: (jnp.minimum((i + 1) * per, n_halo - 1), 0))
    gcn = pl.BlockSpec((8, CONV_W), lambda i: (jnp.minimum((i + 1) * per, (t_lat + t_ctx) // 8 - 1), 2))
    xin = pl.BlockSpec((8, CONV_W), lambda i: (jnp.minimum((i + 1) * per, (t_lat + t_ctx) // 8 - 1), 3))
    head_f32 = pl.BlockSpec((N_HEADS, tm, HEAD_PAD), lambda i: (0, i, 0))
    tab = pl.BlockSpec((tm, HEAD_PAD), row)
    return pl.pallas_call(
        body, name="inproj_bwd", grid=(n_all,),
        out_shape=(jax.ShapeDtypeStruct((t_lat, D_MODEL), F32), jax.ShapeDtypeStruct(w_in.shape, F32),
                   jax.ShapeDtypeStruct(w_uq.shape, F32), jax.ShapeDtypeStruct(w_ukv.shape, F32),
                   jax.ShapeDtypeStruct((16, D_MODEL), F32)),
        in_specs=[
            pl.BlockSpec((tm, D_MODEL), lat_row), _const_spec((tm, D_MODEL)),
            pl.BlockSpec((1, 8, D_MODEL), lambda i: (i // n_lat, 0, 0)),
            pl.BlockSpec((tm, Z_COLS), row), gcp, xip, gcn, xin,
            pl.BlockSpec((tm, CONV_W), lat_row), dyvp, dyvn,
            pl.BlockSpec((tm, CONV_W), lat_row), pl.BlockSpec((tm, D_MODEL), lat_row),
            pl.BlockSpec((N_HEADS, HEAD_PAD, tm), lambda i: (0, 0, jnp.minimum(i, last))),
            head_f32, head_f32, tab, tab, tab,
            _const_spec(w_in.shape), _const_spec(w_uq.shape), _const_spec(w_ukv.shape),
            _const_spec(qg.shape), _const_spec(kvg.shape), _const_spec(cw.shape),
        ],
        out_specs=(pl.BlockSpec((tm, D_MODEL), lat_row), _const_spec(w_in.shape), _const_spec(w_uq.shape),
                   _const_spec(w_ukv.shape), _const_spec((16, D_MODEL))),
        scratch_shapes=[pltpu.VMEM((group * tm, D_MODEL), BF16), pltpu.VMEM((group * tm, Z_COLS), BF16)],
        compiler_params=pltpu.CompilerParams(vmem_limit_bytes=VMEM_LIMIT),
    )(x2, ctx2, mod_a, z, z, z, z, z, dyv, dyv, dyv, dgb, dx1, dqt, dk, dv, cos_t, sin_a, sin_b, w_in, w_uq, w_ukv,
      qg, kvg, cw)


def _wgrad(a, b, name, bm, bn):
    t, m = a.shape
    n = b.shape[1]
    bk = min(t, 4096)
    nk = t // bk
    nj = n // bn

    def body(a_ref, b_ref, o_ref, acc_ref):
        k = pl.program_id(2)
        part = _dot_tn(a_ref[...], b_ref[...])

        @pl.when(k == 0)
        def _():
            acc_ref[...] = part

        @pl.when(k > 0)
        def _():
            acc_ref[...] += part

        @pl.when(k == nk - 1)
        def _():
            o_ref[...] = acc_ref[...].astype(BF16)

    return pl.pallas_call(
        body, name=name, grid=(m // bm, nj, nk), out_shape=jax.ShapeDtypeStruct((m // bm * nj, bm, bn), BF16),
        in_specs=[pl.BlockSpec((bk, bm), lambda i, j, k: (k, i)), pl.BlockSpec((bk, bn), lambda i, j, k: (k, j))],
        out_specs=pl.BlockSpec((None, bm, bn), lambda i, j, k: (i * nj + j, 0, 0)),
        scratch_shapes=[pltpu.VMEM((bm, bn), F32)],
        compiler_params=pltpu.CompilerParams(vmem_limit_bytes=VMEM_LIMIT),
    )(a, b)


def _wgrad_out(o, conv, dy1):
    t = o.shape[1]
    bk = min(t, 2048)
    nk = t // bk
    rows = N_HEADS * HEAD_PAD + CONV_W

    def body(o_ref, c_ref, d_ref, w_ref, acc_ref):
        k = pl.program_id(0)
        cat = jnp.concatenate([o_ref[h] for h in range(N_HEADS)] + [c_ref[...]], axis=1)
        part = _dot_tn(cat, d_ref[...])

        @pl.when(k == 0)
        def _():
            acc_ref[...] = part

        @pl.when(k > 0)
        def _():
            acc_ref[...] += part

        @pl.when(k == nk - 1)
        def _():
            for h in range(N_HEADS):
                w_ref[h * V_DIM:(h + 1) * V_DIM, :] = acc_ref[h * HEAD_PAD:h * HEAD_PAD + V_DIM, :].astype(BF16)
            w_ref[N_HEADS * V_DIM:, :] = acc_ref[N_HEADS * HEAD_PAD:, :].astype(BF16)

    return pl.pallas_call(
        body, name="wgrad_out", grid=(nk,),
        out_shape=jax.ShapeDtypeStruct((D_MODEL, D_MODEL), BF16),
        in_specs=[pl.BlockSpec((N_HEADS, bk, HEAD_PAD), lambda k: (0, k, 0)),
                  pl.BlockSpec((bk, CONV_W), lambda k: (k, 0)),
                  pl.BlockSpec((bk, D_MODEL), lambda k: (k, 0))],
        out_specs=_const_spec((D_MODEL, D_MODEL)),
        scratch_shapes=[pltpu.VMEM((rows, D_MODEL), F32)],
        compiler_params=pltpu.CompilerParams(vmem_limit_bytes=VMEM_LIMIT),
    )(o, conv, dy1)


def _adamw_call(w, g, m, v, name):
    rows, cols = w.shape
    rb = 256 if rows % 256 == 0 else rows

    def body(w_ref, g_ref, m_ref, v_ref, d_ref, nm_ref, nv_ref):
        d_, m_, v_ = _adamw(w_ref[...], g_ref[...], m_ref[...], v_ref[...])
        d_ref[...] = d_
        nm_ref[...] = m_
        nv_ref[...] = v_

    spec = pl.BlockSpec((rb, cols), lambda i: (i, 0))
    shp = jax.ShapeDtypeStruct((rows, cols), F32)
    return pl.pallas_call(
        body, name=name, grid=(rows // rb,), out_shape=(shp, shp, shp),
        in_specs=[spec] * 4, out_specs=(spec, spec, spec),
    )(w, g, m, v)


def _wmod_update(s_t, dm, w, m, v):
    rows, cols = w.shape
    cb = 512

    def body(s_ref, dm_ref, w_ref, m_ref, v_ref, g_ref, d_ref, nm_ref, nv_ref):
        g = jnp.dot(s_ref[...], dm_ref[...], precision=HIGHEST, preferred_element_type=F32)
        d_, m_, v_ = _adamw(w_ref[...], g, m_ref[...], v_ref[...])
        g_ref[...] = g
        d_ref[...] = d_
        nm_ref[...] = m_
        nv_ref[...] = v_

    spec = pl.BlockSpec((rows, cb), lambda i: (0, i))
    shp = jax.ShapeDtypeStruct((rows, cols), F32)
    return pl.pallas_call(
        body, name="wmod_update", grid=(cols // cb,), out_shape=(shp, shp, shp, shp),
        in_specs=[_const_spec(s_t.shape), pl.BlockSpec((16, cb), lambda i: (0, i)), spec, spec, spec],
        out_specs=(spec, spec, spec, spec),
        compiler_params=pltpu.CompilerParams(vmem_limit_bytes=VMEM_LIMIT),
    )(s_t, dm, w, m, v)


def _rope_tables(t_lat, t_ctx):
    t = jnp.arange(t_lat)
    pos = jnp.stack([(t // GRID_W).astype(F32), (t % GRID_W).astype(F32)], axis=1)
    half = QK_ROPE // 4
    freqs = ROPE_THETA ** (-jnp.arange(0, 2 * half, 2, dtype=F32) / (2 * half))
    ang = pos[:, :, None] * freqs[None, None, :]
    cos, sin = jnp.cos(ang), jnp.sin(ang)
    zero = jnp.zeros_like(sin)
    cos32 = jnp.concatenate([cos, cos], axis=2).reshape(t_lat, QK_ROPE)
    sa32 = jnp.concatenate([zero, sin], axis=2).reshape(t_lat, QK_ROPE)
    sb32 = jnp.concatenate([-sin, zero], axis=2).reshape(t_lat, QK_ROPE)

    def widen(tab, fill):
        left = jnp.full((t_lat, ROPE_LANE0), fill, F32)
        right = jnp.full((t_lat, HEAD_PAD - ROPE_LANE0 - QK_ROPE), fill, F32)
        lat = jnp.concatenate([left, tab, right], axis=1)
        return jnp.concatenate([lat, jnp.full((t_ctx, HEAD_PAD), fill, F32)], axis=0)

    return widen(cos32, 1.0), widen(sa32, 0.0), widen(sb32, 0.0)


def _cols_from_shards(s):
    return jnp.transpose(s, (1, 0, 2)).reshape(s.shape[1], -1)


def _cols_to_shards(w):
    k, n = w.shape
    return jnp.transpose(w.reshape(k, N_SHARD, n // N_SHARD), (1, 0, 2))


def kernel(x, c, ctx, c_ctx, w_mod, b_mod, w_in, q_norm_g, w_uq, kv_norm_g, w_ukv, conv_w, w_out, w_mlp1, w_mlp2, final_norm_g, loss_target, m_c_ctx, m_w_mod, m_b_mod, m_w_in, m_q_norm_g, m_w_uq, m_kv_norm_g, m_w_ukv, m_conv_w, m_w_out, m_w_mlp1, m_w_mlp2, m_final_norm_g, v_c_ctx, v_w_mod, v_b_mod, v_w_in, v_q_norm_g, v_w_uq, v_kv_norm_g, v_w_ukv, v_conv_w, v_w_out, v_w_mlp1, v_w_mlp2, v_final_norm_g):
    t_lat, t_ctx = x.shape[1], ctx.shape[1]
    assert t_ctx == TOK_TILE and t_lat % TOK_TILE == 0 and t_lat % GRID_W == 0
    mx, my, mc = _pos()
    me = 4 * mx + 2 * my + mc
    j = 2 * mx + my
    ncol = w_mod.shape[2]
    x2, ctx2, tgt = x[0], ctx[0], loss_target[0]
    cctx_row = c_ctx.reshape(1, D_MODEL)

    b_sh = lax.dynamic_slice(b_mod, (0, j * ncol), (1, ncol))
    cw_pad = jnp.zeros((8, 128), F32).at[0:3, :].set(conv_w[0])
    c8, m_all = _mod_exchange(c, cctx_row, w_mod[0], b_sh, cw_pad)
    mvec = m_all[:, 0, :].reshape(6, D_MODEL)
    mctx = m_all[:, 8, :].reshape(6, D_MODEL)
    zeros6 = jnp.zeros((6, D_MODEL), F32)
    mod_a = jnp.stack([jnp.concatenate([mvec[0:2], zeros6], axis=0), jnp.concatenate([mctx[0:2], zeros6], axis=0)])
    mod_b = jnp.concatenate([mvec[2:6], jnp.zeros((4, D_MODEL), F32)], axis=0)
    cw_full = jnp.pad(jnp.transpose(m_all[:, 9:12, 0:128], (1, 0, 2)).reshape(3, CONV_W), ((0, 5), (0, 0)))

    g_in, g_uq, g_ukv, g_out, g_m1, g_m2 = _weight_prep(
        (w_in[0], w_uq[0], w_ukv[0], w_out[0], w_mlp1[0], w_mlp2[0]), 3)
    w_in_f = _cols_from_shards(g_in)
    zc = lambda n: jnp.zeros((D_MODEL, n), BF16)
    w_in_p = jnp.concatenate([w_in_f[:, 0:384], zc(64), w_in_f[:, 384:416], zc(32), w_in_f[:, 416:]], axis=1)
    w_uq_f = _cols_from_shards(g_uq).reshape(Q_RANK, N_HEADS, QK_DIM)
    w_uq_p = jnp.pad(w_uq_f, ((0, 0), (0, 0), (0, HEAD_PAD - QK_DIM))).reshape(Q_RANK, N_HEADS * HEAD_PAD)
    w_ukv_f = _cols_from_shards(g_ukv).reshape(KV_RANK, N_HEADS, QK_NOPE + V_DIM)
    padh = lambda a: jnp.pad(a, ((0, 0), (0, 0), (0, HEAD_PAD - a.shape[2]))).reshape(KV_RANK, N_HEADS * HEAD_PAD)
    w_ukv_p = jnp.concatenate([padh(w_ukv_f[:, :, :QK_NOPE]), padh(w_ukv_f[:, :, QK_NOPE:])], axis=1)
    cos_t, sin_a, sin_b = _rope_tables(t_lat, t_ctx)
    gf_row = final_norm_g.reshape(1, D_MODEL)
    c_idx = mc.reshape(1).astype(jnp.int32)
    j_idx = j.reshape(1).astype(jnp.int32)

    z, q, k, v, kt = _inproj_fwd(x2, ctx2, mod_a, w_in_p, q_norm_g, kv_norm_g, w_uq_p, w_ukv_p, cos_t, sin_a, sin_b)
    o, lse, g_out, w1, g_m2 = _attn_fwd(q, k, v, t_lat, (g_out, g_m1, g_m2))
    w_out_f = g_out.reshape(D_MODEL, D_MODEL)
    wo_attn = jnp.pad(w_out_f[0:512].reshape(N_HEADS, V_DIM, D_MODEL), ((0, 0), (0, HEAD_PAD - V_DIM), (0, 0)))
    wo_conv = w_out_f[512:]
    w2 = g_m2.reshape(D_FF, D_MODEL)
    r, da, h2, dy2, dx1, conv, acc_mlp, dy1, do, dgb, dyv = _mlp_fwdbwd(o, z, x2, tgt, mod_b, gf_row, cw_full, wo_attn,
                                                                         wo_conv, w1, w2)
    d_w1 = _wgrad(h2, da, "wgrad_mlp1", D_MODEL, FF_CHUNK)
    d_w2 = _wgrad(r, dy2, "wgrad_mlp2", FF_CHUNK, D_MODEL)
    d_wout = _wgrad_out(o, conv, dy1).reshape(N_SHARD, D_MODEL // N_SHARD, D_MODEL)
    big_grads, big_names = (d_w1, d_w2, d_wout), ("mlp1", "mlp2", "out")
    big_got = _rs_sibling(big_grads, "rs_sibling_big")
    big_parts = [_add_pairs(a, g, c_idx, "rs_add_pairs_" + n) for a, g, n in zip(big_grads, big_got, big_names)]
    dqt, dk, dv, *big_recv = _attn_bwd(q, k, v, kt, o, do, lse, t_lat, big_parts)
    big_halves = [_add_chips(p, g, j_idx, "rs_add_chips_" + n) for p, g, n in zip(big_parts, big_recv, big_names)]
    g_w1, g_w2, g_wout = _rs_join(big_halves, "rs_join_big")
    gx, d_win, d_wuq, d_wukv, acc_in = _inproj_bwd(x2, ctx2, mod_a, z, dyv, dgb, dx1, dqt, dk, dv, cos_t, sin_a, sin_b,
                                                   w_in_p, w_uq_p, w_ukv_p, q_norm_g, kv_norm_g, cw_full)

    pad_row = lambda a: jnp.pad(a, ((0, 0), (0, D_MODEL - a.shape[1])))
    sv = jnp.concatenate([
        acc_in[0:2], acc_mlp[5:6], acc_mlp[2:4], acc_mlp[1:2],
        acc_in[2:4], acc_mlp[0:1], acc_in[4:5], acc_in[5:6], acc_in[6:9], acc_mlp[4:5],
        jnp.zeros((1, D_MODEL), F32)], axis=0)
    all_sv, red, o_cc, o_b, o_q, o_k, o_gf = _small_exchange(
        sv, w_mod[0], cctx_row, m_c_ctx.reshape(1, D_MODEL), v_c_ctx.reshape(1, D_MODEL),
        b_mod.reshape(6, D_MODEL), m_b_mod.reshape(6, D_MODEL), v_b_mod.reshape(6, D_MODEL),
        q_norm_g, m_q_norm_g, v_q_norm_g, kv_norm_g, m_kv_norm_g, v_kv_norm_g,
        gf_row, m_final_norm_g.reshape(1, D_MODEL), v_final_norm_g.reshape(1, D_MODEL))
    loss = red[14, 0]

    c9 = jnp.concatenate([c8[0::8], jnp.zeros((7, D_MODEL), F32)], axis=0)
    s_t = jnp.transpose(c9 * jax.nn.sigmoid(c9))
    dm_ex = all_sv[:, 0:6, :].reshape(8, 6 * D_MODEL)
    dm_ctx = jnp.concatenate([red[6:8].reshape(1, 2 * D_MODEL), jnp.zeros((1, 4 * D_MODEL), F32)], axis=1)
    dm16 = jnp.concatenate([dm_ex, dm_ctx, jnp.zeros((7, 6 * D_MODEL), F32)], axis=0)
    dm_sh = lax.dynamic_slice(dm16, (0, j * ncol), (16, ncol))
    g_wmod, d_wmod, nm_wmod, nv_wmod = _wmod_update(s_t, dm_sh, w_mod[0], m_w_mod[0], v_w_mod[0])

    g_cw = lax.dynamic_slice(red[11:14, 0:CONV_W], (0, j * 128), (3, 128))
    d_cw, nm_cw, nv_cw = _adamw_call(conv_w[0], g_cw, m_conv_w[0], v_conv_w[0], "adamw_conv")

    d_win_f = jnp.concatenate([d_win[:, 0:384], d_win[:, 448:480], d_win[:, 512:]], axis=1)
    d_wuq_f = d_wuq.reshape(Q_RANK, N_HEADS, HEAD_PAD)[:, :, 0:QK_DIM].reshape(Q_RANK, N_HEADS * QK_DIM)
    d_wukv3 = d_wukv.reshape(KV_RANK, 2, N_HEADS, HEAD_PAD)
    d_wukv_f = jnp.concatenate([d_wukv3[:, 0, :, 0:QK_NOPE], d_wukv3[:, 1, :, 0:V_DIM]], axis=2).reshape(KV_RANK, -1)
    rest = tuple(_cols_to_shards(a).astype(BF16) for a in (d_win_f, d_wuq_f, d_wukv_f))
    rest_names = ("in", "uq", "ukv")
    rest_got = _rs_sibling(rest, "rs_sibling_rest")
    rest_parts = [_add_pairs(a, g, c_idx, "rs_add_pairs_" + n) for a, g, n in zip(rest, rest_got, rest_names)]
    rest_recv = _rs_chips(rest_parts, "rs_chips_rest")
    rest_halves = [_add_chips(p, g, j_idx, "rs_add_chips_" + n) for p, g, n in zip(rest_parts, rest_recv, rest_names)]
    g_win, g_wuq, g_wukv = _rs_join(rest_halves, "rs_join_rest")
    upd = {}
    for name, w_, g_, m_, v_ in (("in", w_in, g_win, m_w_in, v_w_in), ("uq", w_uq, g_wuq, m_w_uq, v_w_uq),
                                 ("ukv", w_ukv, g_wukv, m_w_ukv, v_w_ukv), ("out", w_out, g_wout, m_w_out, v_w_out),
                                 ("mlp1", w_mlp1, g_w1, m_w_mlp1, v_w_mlp1), ("mlp2", w_mlp2, g_w2, m_w_mlp2, v_w_mlp2)):
        upd[name] = _adamw_call(w_[0], g_, m_[0], v_[0], "adamw_" + name)

    def four(o4, shape):
        return [o4[r].reshape(shape) for r in range(4)]

    cc4 = four(o_cc, (D_MODEL,))
    b4 = [o_b[r].reshape(1, 6 * D_MODEL) for r in range(4)]
    q4 = four(o_q, (1, Q_RANK))
    k4 = four(o_k, (1, KV_RANK))
    gf4 = four(o_gf, (D_MODEL,))
    big = {"in": g_win, "uq": g_wuq, "ukv": g_wukv, "out": g_wout, "mlp1": g_w1, "mlp2": g_w2}

    def leaf(idx):
        wm = (g_wmod, d_wmod, nm_wmod, nv_wmod)[idx]
        cwv = (g_cw, d_cw, nm_cw, nv_cw)[idx]
        bigv = {n: (big[n] if idx == 0 else upd[n][idx - 1]) for n in big}
        return [cc4[idx], wm[None], b4[idx], bigv["in"][None], q4[idx], bigv["uq"][None], k4[idx], bigv["ukv"][None],
                cwv[None], bigv["out"][None], bigv["mlp1"][None], bigv["mlp2"][None], gf4[idx]]

    return (loss, gx[None], *leaf(0), *leaf(1), *leaf(2), *leaf(3))
```

```python
import functools
import math

import jax
import jax.numpy as jnp
from jax import lax
from jax.experimental import pallas as pl
from jax.experimental.pallas import tpu as pltpu

F32 = jnp.float32
BF16 = jnp.bfloat16
MESH = pl.DeviceIdType.MESH
HIGHEST = lax.Precision.HIGHEST

D_MODEL = 1024
N_HEADS = 8
QK_NOPE = 64
QK_ROPE = 32
QK_DIM = QK_NOPE + QK_ROPE
V_DIM = 64
Q_RANK = 256
KV_RANK = 128
CONV_W = 512
D_FF = 4096
GRID_W = 64
ROPE_THETA = 10000.0
EPS = 1e-6
ATTN_SCALE = 1.0 / math.sqrt(QK_DIM)
HEAD_PAD = 128
Z_COLS = 2048
ROPE_LANE0 = QK_NOPE
N_SHARD = 4
TOK_TILE = 256
FF_CHUNK = 1024
MLP_FF_CHUNK = 1024
KEY_CHUNK = 512
ATTN_FWD_Q_BLOCK = 512
ATTN_HEADS_PER_STEP = 4
ATTN_BWD_HEADS_PER_STEP = 2
ATTN_BWD_Q_BLOCK = 512
KEY_CHUNK_BWD = 512

ADAM_LR = 0.001
ADAM_B1 = 0.9
ADAM_B2 = 0.999
ADAM_EPS = 1e-08
ADAM_WD = 0.01
ADAM_STEP = 10

LOG2E = 1.4426950408889634

VMEM_LIMIT = 56 * 1024 * 1024
STAGE_VMEM_LIMIT = 32 * 1024 * 1024


def _pos():
    return lax.axis_index("x"), lax.axis_index("y"), lax.axis_index("c")


def _dot(a, b):
    return jnp.dot(a, b, preferred_element_type=F32)


def _dot_nt(a, b):
    return lax.dot_general(a, b, (((1,), (1,)), ((), ())), preferred_element_type=F32)


def _dot_tn(a, b):
    return lax.dot_general(a, b, (((0,), (0,)), ((), ())), preferred_element_type=F32)


def _rope(v, cos, sa, sb):
    return v * cos + pltpu.roll(v, 8, 1) * sa + pltpu.roll(v, HEAD_PAD - 8, 1) * sb


def _unrope(g, cos, sa, sb):
    return g * cos + pltpu.roll(g * sa, HEAD_PAD - 8, 1) + pltpu.roll(g * sb, 8, 1)


def _sigmoid(v):
    return 1.0 / (1.0 + jnp.exp(-v))


def _adamw(w, g, m, v):
    m = ADAM_B1 * m + (1.0 - ADAM_B1) * g
    v = ADAM_B2 * v + (1.0 - ADAM_B2) * (g * g)
    m_hat = m / (1.0 - ADAM_B1 ** ADAM_STEP)
    v_hat = v / (1.0 - ADAM_B2 ** ADAM_STEP)
    delta = -ADAM_LR * (m_hat / (jnp.sqrt(v_hat) + ADAM_EPS) + ADAM_WD * w)
    return delta, m, v


def _shift_rows(u, prev_row, next_row):
    n = u.shape[0]
    rows = lax.broadcasted_iota(jnp.int32, u.shape, 0)
    um1 = jnp.where(rows == 0, prev_row, pltpu.roll(u, 1, 0))
    up1 = jnp.where(rows == n - 1, next_row, pltpu.roll(u, n - 1, 0))
    return um1, up1


def _const_spec(shape):
    nd = len(shape)
    return pl.BlockSpec(shape, lambda *_: (0,) * nd)


def _resident_spec(shape):
    nd = len(shape)
    return pl.BlockSpec(shape, lambda *_: (0,) * nd, pipeline_mode=pl.Buffered(1))


def _peer(r, x, y, c):
    px = 1 - x if r & 4 else x
    py = 1 - y if r & 2 else y
    pc = 1 - c if r & 1 else c
    return (px, py, pc)


def _prologue(c_row, cctx_row, w_mod_sh, b_sh, cw_sh, srcs, n_gather):
    ncol = w_mod_sh.shape[1]
    n = len(srcs)
    n_split = 4

    def body(c_ref, cctx_ref, w_ref, b_ref, cw_ref, *refs):
        ins, (c8_ref, m_ref), outs = refs[:n], refs[n:n + 2], refs[n + 2:2 * n + 2]
        mine_ref, msh_ref = refs[2 * n + 2:2 * n + 4]
        f32s, bfs = refs[2 * n + 4:3 * n + 4], refs[3 * n + 4:4 * n + 4]
        ssem, rsem, ssem2, rsem2, lsem_in, lsem_out = refs[4 * n + 4:4 * n + 10]
        x, y, c = _pos()
        me = 4 * x + 2 * y + c
        j = 2 * x + y

        def pieces(rows):
            step = rows // n_split
            return [pl.ds(q * step, step) for q in range(n_split)]

        for t in range(n):
            for sl in pieces(ins[t].shape[0]):
                pltpu.make_async_copy(ins[t].at[sl], f32s[t].at[sl], lsem_in.at[t]).start()
        mine_ref[...] = jnp.zeros(mine_ref.shape, F32)
        mine_ref[0:1, :] = c_ref[...]
        my_rows = pl.ds(pl.multiple_of(8 * me, 8), 8)
        sends = []
        for r in range(1, 8):
            cp = pltpu.make_async_remote_copy(
                src_ref=mine_ref, dst_ref=c8_ref.at[my_rows], send_sem=ssem.at[r - 1], recv_sem=rsem.at[r - 1],
                device_id=_peer(r, x, y, c), device_id_type=MESH)
            cp.start()
            sends.append(cp)
        for cp in sends:
            cp.wait()
        c8_ref[my_rows, :] = mine_ref[...]
        c8_ref[64:72, :] = jnp.zeros((8, D_MODEL), F32)
        c8_ref[64:65, :] = cctx_ref[...]
        cv = c8_ref[...]
        s = cv * _sigmoid(cv)
        m = jnp.dot(s, w_ref[...], precision=HIGHEST, preferred_element_type=F32) + b_ref[...]
        msh_ref[0:64, :] = m[0:64, :]
        msh_ref[64:72, :] = jnp.zeros((8, ncol), F32)
        msh_ref[64:65, :] = m[64:65, :]
        msh_ref[65:68, 0:128] = cw_ref[0:3, :]
        m_ref[j, 0:8, :] = msh_ref[my_rows, :]
        m_ref[j, 8:16, :] = msh_ref[64:72, :]
        sends2 = []
        for k, (px, py) in enumerate(_chips(x, y)):
            theirs = pl.ds(pl.multiple_of(8 * (4 * px + 2 * py + c), 8), 8)
            for half, src in enumerate((msh_ref.at[theirs], msh_ref.at[64:72])):
                cp = pltpu.make_async_remote_copy(
                    src_ref=src, dst_ref=m_ref.at[j, 8 * half:8 * half + 8], send_sem=ssem2.at[2 * k + half],
                    recv_sem=rsem2.at[2 * k + half], device_id=(px, py, c), device_id_type=MESH)
                cp.start()
                sends2.append(cp)
        gather = _ShardGather(outs[:n_gather], *refs[4 * n + 10:])
        for t in range(n):
            pltpu.make_async_copy(ins[t], f32s[t], lsem_in.at[t]).wait()
            bfs[t][...] = f32s[t][...].astype(BF16)
            for sl in pieces(ins[t].shape[0]):
                pltpu.make_async_copy(bfs[t].at[sl], outs[t].at[j, sl], lsem_out.at[t]).start()
            if t == n_gather - 1:
                for u in range(n_gather):
                    pltpu.make_async_copy(bfs[u], outs[u].at[j], lsem_out.at[u]).wait()
                gather.start()
        gather.forward()
        gather.finish()
        for t in range(n_gather, n):
            pltpu.make_async_copy(bfs[t], outs[t].at[j], lsem_out.at[t]).wait()
        for cp in sends2:
            cp.wait()

    vm = pl.BlockSpec(memory_space=pltpu.VMEM)
    hbm = pl.BlockSpec(memory_space=pl.ANY)
    return pl.pallas_call(
        body, name="prologue",
        out_shape=(jax.ShapeDtypeStruct((72, D_MODEL), F32), jax.ShapeDtypeStruct((N_SHARD, 16, ncol), F32))
        + tuple(jax.ShapeDtypeStruct((N_SHARD,) + a.shape, BF16) for a in srcs),
        in_specs=[vm] * 5 + [hbm] * n, out_specs=(vm, vm) + (hbm,) * n,
        scratch_shapes=[pltpu.VMEM((8, D_MODEL), F32), pltpu.VMEM((72, ncol), F32)]
        + [pltpu.VMEM(a.shape, F32) for a in srcs] + [pltpu.VMEM(a.shape, BF16) for a in srcs]
        + [pltpu.SemaphoreType.DMA((7,)), pltpu.SemaphoreType.DMA((7,)),
           pltpu.SemaphoreType.DMA((6,)), pltpu.SemaphoreType.DMA((6,)),
           pltpu.SemaphoreType.DMA((n,)), pltpu.SemaphoreType.DMA((n,))] + _gather_sems(n_gather),
        compiler_params=pltpu.CompilerParams(vmem_limit_bytes=VMEM_LIMIT),
    )(c_row, cctx_row, w_mod_sh, b_sh, cw_sh, *srcs)


def _chips(x, y):
    return [(1 - x, y), (x, 1 - y), (1 - x, 1 - y)]


def _halves(ref, c, align):
    hr = ref.shape[-2] // 2
    return (pl.ds(pl.multiple_of(c * hr, align), hr), pl.ds(pl.multiple_of((1 - c) * hr, align), hr))


class _ShardGather:
    def __init__(self, refs, ssem, rsem, fsend, frecv):
        self.refs, self.sems = refs, (ssem, rsem, fsend, frecv)
        self.x, self.y, self.c = _pos()
        self.j = 2 * self.x + self.y

    def _ici(self, a, k, slot):
        g = self.refs[a]
        ssem, rsem, _, _ = self.sems
        mine, _ = _halves(g, self.c, 16)
        px, py = _chips(self.x, self.y)[k]
        return pltpu.make_async_remote_copy(
            src_ref=g.at[self.j, mine], dst_ref=g.at[slot, mine], send_sem=ssem.at[3 * a + k],
            recv_sem=rsem.at[3 * a + k], device_id=(px, py, self.c), device_id_type=MESH)

    def _d2d(self, a, k, to_other_half):
        g = self.refs[a]
        _, _, fsend, frecv = self.sems
        mine, theirs = _halves(g, self.c, 16)
        px, py = _chips(self.x, self.y)[k]
        jk = 2 * px + py
        return pltpu.make_async_remote_copy(
            src_ref=g.at[jk, mine], dst_ref=g.at[jk, theirs if to_other_half else mine],
            send_sem=fsend.at[3 * a + k], recv_sem=frecv.at[3 * a + k],
            device_id=(self.x, self.y, 1 - self.c), device_id_type=MESH)

    def start(self):
        for a in range(len(self.refs)):
            for k in range(3):
                self._ici(a, k, self.j).start()

    def forward(self):
        for a in range(len(self.refs)):
            for k, (px, py) in enumerate(_chips(self.x, self.y)):
                self._ici(a, k, 2 * px + py).wait_recv()
                self._d2d(a, k, False).start()

    def finish(self):
        for a in range(len(self.refs)):
            for k in range(3):
                self._d2d(a, k, True).wait()
                self._ici(a, k, self.j).wait_send()


def _gather_sems(n_arrays):
    return [pltpu.SemaphoreType.DMA((3 * n_arrays,)) for _ in range(4)]


class _SiblingSend:
    def __init__(self, g_refs, got_refs, ssem, rsem):
        self.g_refs, self.got_refs, self.ssem, self.rsem = g_refs, got_refs, ssem, rsem
        self.x, self.y, self.c = _pos()

    def _copy(self, a, shard):
        _, theirs = _halves(self.g_refs[a], self.c, 16)
        src = self.g_refs[a].at[:, theirs] if shard is None else self.g_refs[a].at[shard, theirs]
        dst = self.got_refs[a] if shard is None else self.got_refs[a].at[shard]
        return pltpu.make_async_remote_copy(
            src_ref=src, dst_ref=dst, send_sem=self.ssem.at[a], recv_sem=self.rsem.at[a],
            device_id=(self.x, self.y, 1 - self.c), device_id_type=MESH)

    def start(self):
        for a in range(len(self.g_refs)):
            for s in range(N_SHARD):
                self._copy(a, s).start()

    def finish(self):
        for a in range(len(self.g_refs)):
            self._copy(a, None).wait()


class _SiblingSwap:
    def __init__(self, h_refs, t_refs, ssem, rsem):
        self.h_refs, self.t_refs, self.ssem, self.rsem = h_refs, t_refs, ssem, rsem
        self.x, self.y, self.c = _pos()

    def _copy(self, a):
        return pltpu.make_async_remote_copy(
            src_ref=self.h_refs[a], dst_ref=self.t_refs[a], send_sem=self.ssem.at[a], recv_sem=self.rsem.at[a],
            device_id=(self.x, self.y, 1 - self.c), device_id_type=MESH)

    def start(self):
        for a in range(len(self.h_refs)):
            self._copy(a).start()

    def finish(self):
        for a in range(len(self.h_refs)):
            self._copy(a).wait()


def _rs_sibling(arrs, name):
    n = len(arrs)

    def body(*refs):
        send = _SiblingSend(refs[:n], refs[n:2 * n], refs[2 * n], refs[2 * n + 1])
        send.start()
        send.finish()

    hbm = pl.BlockSpec(memory_space=pl.ANY)
    return pl.pallas_call(
        body, name=name,
        out_shape=tuple(jax.ShapeDtypeStruct((N_SHARD, a.shape[1] // 2, a.shape[2]), BF16) for a in arrs),
        in_specs=[hbm] * n, out_specs=(hbm,) * n,
        scratch_shapes=[pltpu.SemaphoreType.DMA((n,)), pltpu.SemaphoreType.DMA((n,))],
    )(*arrs)


class _ChipScatter:
    def __init__(self, parts, gots, ssem, rsem):
        self.parts, self.gots, self.ssem, self.rsem = parts, gots, ssem, rsem
        self.x, self.y, self.c = _pos()

    def _copy(self, a, k):
        px, py = _chips(self.x, self.y)[k]
        return pltpu.make_async_remote_copy(
            src_ref=self.parts[a].at[2 * px + py], dst_ref=self.gots[a].at[k], send_sem=self.ssem.at[3 * a + k],
            recv_sem=self.rsem.at[3 * a + k], device_id=(px, py, self.c), device_id_type=MESH)

    def start(self):
        for a in range(len(self.parts)):
            for k in range(3):
                self._copy(a, k).start()

    def finish(self):
        for a in range(len(self.parts)):
            for k in range(3):
                self._copy(a, k).wait()


def _rs_chips(parts, name):
    n = len(parts)

    def body(*refs):
        scatter = _ChipScatter(refs[:n], refs[n:2 * n], refs[2 * n], refs[2 * n + 1])
        scatter.start()
        scatter.finish()

    hbm = pl.BlockSpec(memory_space=pl.ANY)
    return pl.pallas_call(
        body, name=name, out_shape=tuple(jax.ShapeDtypeStruct((3,) + p.shape[1:], BF16) for p in parts),
        in_specs=[hbm] * n, out_specs=(hbm,) * n,
        scratch_shapes=[pltpu.SemaphoreType.DMA((3 * n,)), pltpu.SemaphoreType.DMA((3 * n,))],
    )(*parts)


def _rs_join(halves, name):
    n = len(halves)

    def body(*refs):
        h_refs, f_refs, stages = refs[:n], refs[n:2 * n], refs[2 * n:3 * n]
        lsem_in, lsem_out, ssem, rsem = refs[3 * n:]
        x, y, c = _pos()
        remote = []
        for a in range(n):
            mine, _ = _halves(f_refs[a], c, 8)
            cp = pltpu.make_async_remote_copy(
                src_ref=h_refs[a], dst_ref=f_refs[a].at[mine], send_sem=ssem.at[a], recv_sem=rsem.at[a],
                device_id=(x, y, 1 - c), device_id_type=MESH)
            cp.start()
            remote.append(cp)
            pltpu.make_async_copy(h_refs[a], stages[a], lsem_in.at[a]).start()
        local = []
        for a in range(n):
            mine, _ = _halves(f_refs[a], c, 8)
            pltpu.make_async_copy(h_refs[a], stages[a], lsem_in.at[a]).wait()
            cp = pltpu.make_async_copy(stages[a], f_refs[a].at[mine], lsem_out.at[a])
            cp.start()
            local.append(cp)
        for cp in remote + local:
            cp.wait()

    hbm = pl.BlockSpec(memory_space=pl.ANY)
    return pl.pallas_call(
        body, name=name,
        out_shape=tuple(jax.ShapeDtypeStruct((2 * h.shape[0], h.shape[1]), F32) for h in halves),
        in_specs=[hbm] * n, out_specs=(hbm,) * n,
        scratch_shapes=[pltpu.VMEM(h.shape, F32) for h in halves]
        + [pltpu.SemaphoreType.DMA((n,)) for _ in range(4)],
        compiler_params=pltpu.CompilerParams(vmem_limit_bytes=STAGE_VMEM_LIMIT),
    )(*halves)


def _row_block(rows):
    return max(b for b in range(16, 513, 16) if rows % b == 0)


def _add_pairs(arr, got, c_idx, name):
    hr, cols = got.shape[1], got.shape[2]
    rb = _row_block(hr)
    nb = hr // rb

    def body(c_ref, a_ref, b_ref, o_ref):
        o_ref[...] = (a_ref[...].astype(F32) + b_ref[...].astype(F32)).astype(BF16)

    spec = pl.BlockSpec((1, rb, cols), lambda s, r, c_ref: (s, r, 0))
    grid_spec = pltpu.PrefetchScalarGridSpec(
        num_scalar_prefetch=1, grid=(N_SHARD, nb),
        in_specs=[pl.BlockSpec((1, rb, cols), lambda s, r, c_ref: (s, c_ref[0] * nb + r, 0)), spec],
        out_specs=spec)
    return pl.pallas_call(
        body, name=name, grid_spec=grid_spec, out_shape=jax.ShapeDtypeStruct(got.shape, BF16),
    )(c_idx, arr, got)


def _add_chips(part, got, j_idx, name):
    hr, cols = got.shape[1], got.shape[2]
    rb = _row_block(hr)

    def body(j_ref, p_ref, g_ref, o_ref):
        acc = p_ref[0].astype(F32)
        for k in range(3):
            acc = acc + g_ref[k].astype(F32)
        o_ref[...] = acc

    grid_spec = pltpu.PrefetchScalarGridSpec(
        num_scalar_prefetch=1, grid=(hr // rb,),
        in_specs=[pl.BlockSpec((1, rb, cols), lambda r, j_ref: (j_ref[0], r, 0)),
                  pl.BlockSpec((3, rb, cols), lambda r, j_ref: (0, r, 0))],
        out_specs=pl.BlockSpec((rb, cols), lambda r, j_ref: (r, 0)))
    return pl.pallas_call(
        body, name=name, grid_spec=grid_spec, out_shape=jax.ShapeDtypeStruct((hr, cols), F32),
    )(j_idx, part, got)


def _small_exchange(sv, w_mod_sh, cctx, m_cctx, v_cctx, bmod, m_bmod, v_bmod, qg, m_qg, v_qg, kvg, m_kvg, v_kvg,
                    gf, m_gf, v_gf):
    ncol = w_mod_sh.shape[1]

    def body(sv_ref, w_ref, cctx_ref, mcc_ref, vcc_ref, b_ref, mb_ref, vb_ref, qg_ref, mq_ref, vq_ref,
             kg_ref, mk_ref, vk_ref, gf_ref, mgf_ref, vgf_ref,
             all_ref, red_ref, occ_ref, ob_ref, oq_ref, ok_ref, ogf_ref,
             vec_ref, part_ref, ssem, rsem, ssem2, rsem2):
        x, y, c = _pos()
        me = 4 * x + 2 * y + c
        j = 2 * x + y
        sends = []
        for r in range(1, 8):
            cp = pltpu.make_async_remote_copy(
                src_ref=sv_ref, dst_ref=all_ref.at[me], send_sem=ssem.at[r - 1], recv_sem=rsem.at[r - 1],
                device_id=_peer(r, x, y, c), device_id_type=MESH)
            cp.start()
            sends.append(cp)
        for cp in sends:
            cp.wait()
        all_ref[me] = sv_ref[...]
        red = all_ref[0]
        for d in range(1, 8):
            red = red + all_ref[d]
        red_ref[...] = red
        vec_ref[...] = jnp.zeros(vec_ref.shape, F32)

        @pl.when(j == 0)
        def _():
            vec_ref[0:1, 0:1024] = red[6:7, :]
            vec_ref[0:1, 1024:1536] = red[7:8, 0:512]

        @pl.when(j == 1)
        def _():
            vec_ref[0:1, 0:512] = red[7:8, 512:1024]

        part = lax.dot_general(vec_ref[...], w_ref[...], (((1,), (1,)), ((), ())), precision=HIGHEST,
                               preferred_element_type=F32)
        part_ref[j] = part
        sends2 = []
        for k, r in enumerate((4, 2, 6)):
            cp = pltpu.make_async_remote_copy(
                src_ref=part_ref.at[j], dst_ref=part_ref.at[j], send_sem=ssem2.at[k], recv_sem=rsem2.at[k],
                device_id=_peer(r, x, y, c), device_id_type=MESH)
            cp.start()
            sends2.append(cp)
        for cp in sends2:
            cp.wait()
        tot = part_ref[0]
        for s in range(1, N_SHARD):
            tot = tot + part_ref[s]
        cc = cctx_ref[...]
        sg = _sigmoid(cc)
        g_cc = tot[0:1, :] * (sg * (1.0 + cc * (1.0 - sg)))
        d_, m_, v_ = _adamw(cc, g_cc, mcc_ref[...], vcc_ref[...])
        occ_ref[0:1, :] = g_cc
        occ_ref[1:2, :] = d_
        occ_ref[2:3, :] = m_
        occ_ref[3:4, :] = v_
        occ_ref[4:8, :] = jnp.zeros((4, D_MODEL), F32)
        g_b = red[0:6, :]
        pad = jnp.concatenate([red[6:8, :], jnp.zeros((4, D_MODEL), F32)], axis=0)
        g_b = g_b + pad
        d_, m_, v_ = _adamw(b_ref[...], g_b, mb_ref[...], vb_ref[...])
        ob_ref[0] = g_b
        ob_ref[1] = d_
        ob_ref[2] = m_
        ob_ref[3] = v_
        g_q = red[9:10, 0:Q_RANK]
        d_, m_, v_ = _adamw(qg_ref[...], g_q, mq_ref[...], vq_ref[...])
        oq_ref[0:1, :] = g_q
        oq_ref[1:2, :] = d_
        oq_ref[2:3, :] = m_
        oq_ref[3:4, :] = v_
        oq_ref[4:8, :] = jnp.zeros((4, Q_RANK), F32)
        g_k = red[10:11, 0:KV_RANK]
        d_, m_, v_ = _adamw(kg_ref[...], g_k, mk_ref[...], vk_ref[...])
        ok_ref[0:1, :] = g_k
        ok_ref[1:2, :] = d_
        ok_ref[2:3, :] = m_
        ok_ref[3:4, :] = v_
        ok_ref[4:8, :] = jnp.zeros((4, KV_RANK), F32)
        g_f = red[8:9, :]
        d_, m_, v_ = _adamw(gf_ref[...], g_f, mgf_ref[...], vgf_ref[...])
        ogf_ref[0:1, :] = g_f
        ogf_ref[1:2, :] = d_
        ogf_ref[2:3, :] = m_
        ogf_ref[3:4, :] = v_
        ogf_ref[4:8, :] = jnp.zeros((4, D_MODEL), F32)

    vm = pl.BlockSpec(memory_space=pltpu.VMEM)
    out_shape = (
        jax.ShapeDtypeStruct((8, 16, D_MODEL), F32),
        jax.ShapeDtypeStruct((16, D_MODEL), F32),
        jax.ShapeDtypeStruct((8, D_MODEL), F32),
        jax.ShapeDtypeStruct((4, 6, D_MODEL), F32),
        jax.ShapeDtypeStruct((8, Q_RANK), F32),
        jax.ShapeDtypeStruct((8, KV_RANK), F32),
        jax.ShapeDtypeStruct((8, D_MODEL), F32),
    )
    return pl.pallas_call(
        body, name="small_exchange", out_shape=out_shape, in_specs=[vm] * 17, out_specs=tuple([vm] * 7),
        scratch_shapes=[pltpu.VMEM((8, ncol), F32), pltpu.VMEM((N_SHARD, 8, D_MODEL), F32),
                        pltpu.SemaphoreType.DMA((7,)), pltpu.SemaphoreType.DMA((7,)),
                        pltpu.SemaphoreType.DMA((3,)), pltpu.SemaphoreType.DMA((3,))],
        compiler_params=pltpu.CompilerParams(vmem_limit_bytes=VMEM_LIMIT),
    )(sv, w_mod_sh, cctx, m_cctx, v_cctx, bmod, m_bmod, v_bmod, qg, m_qg, v_qg, kvg, m_kvg, v_kvg, gf, m_gf, v_gf)


def _inproj_fwd(x2, ctx2, mod_a, w_in, qg, kvg, w_uq, w_ukv, cos_t, sin_a, sin_b):
    t_lat, t_ctx = x2.shape[0], ctx2.shape[0]
    tm = TOK_TILE
    n_lat = t_lat // tm
    n_all = n_lat + t_ctx // tm
    e_rows = t_lat + t_ctx

    def body(x_ref, ctx_ref, mod_ref, win_ref, qg_ref, kvg_ref, wuq_ref, wukv_ref, cos_ref, sa_ref, sb_ref,
             z_ref, q_ref, k_ref, v_ref, kt_ref):
        i = pl.program_id(0)
        xin = jnp.where(i < n_lat, x_ref[...], ctx_ref[...])
        xn = xin * lax.rsqrt(jnp.mean(xin * xin, axis=-1, keepdims=True) + EPS)
        h1 = (xn * (1.0 + mod_ref[0, 1:2, :]) + mod_ref[0, 0:1, :]).astype(BF16)
        z = _dot(h1, win_ref[...])
        z_ref[...] = z
        cos, sa, sb = cos_ref[...], sa_ref[...], sb_ref[...]
        cq = z[:, 0:Q_RANK]
        cqn = (cq * lax.rsqrt(jnp.mean(cq * cq, axis=-1, keepdims=True) + EPS) * qg_ref[...]).astype(BF16)
        q = _dot(cqn, wuq_ref[...])
        ckv = z[:, Q_RANK:Q_RANK + KV_RANK]
        ckvn = (ckv * lax.rsqrt(jnp.mean(ckv * ckv, axis=-1, keepdims=True) + EPS) * kvg_ref[...]).astype(BF16)
        kv = _dot(ckvn, wukv_ref[...])
        kr = _rope(z[:, Q_RANK + KV_RANK:Q_RANK + KV_RANK + HEAD_PAD], cos, sa, sb)
        ones_lane = lax.broadcasted_iota(jnp.int32, (tm, HEAD_PAD), 1) == V_DIM
        for h in range(N_HEADS):
            lo = h * HEAD_PAD
            q_ref[h] = _rope(q[:, lo:lo + HEAD_PAD], cos, sa, sb).astype(BF16)
            kh = kv[:, lo:lo + HEAD_PAD] + kr
            k_ref[h] = kh.astype(BF16)
            kt_ref[h] = kh.T.astype(BF16)
            vh = kv[:, N_HEADS * HEAD_PAD + lo:N_HEADS * HEAD_PAD + lo + HEAD_PAD]
            v_ref[h] = jnp.where(ones_lane, 1.0, vh).astype(BF16)

    row = lambda i: (i, 0)
    head_spec = pl.BlockSpec((N_HEADS, tm, HEAD_PAD), lambda i: (0, i, 0))
    head_shape = jax.ShapeDtypeStruct((N_HEADS, e_rows, HEAD_PAD), BF16)
    return pl.pallas_call(
        body, name="inproj_fwd", grid=(n_all,),
        out_shape=(jax.ShapeDtypeStruct((e_rows, Z_COLS), F32), head_shape, head_shape, head_shape,
                   jax.ShapeDtypeStruct((N_HEADS, HEAD_PAD, e_rows), BF16)),
        in_specs=[
            pl.BlockSpec((tm, D_MODEL), lambda i: (jnp.minimum(i, n_lat - 1), 0)),
            _const_spec((tm, D_MODEL)),
            pl.BlockSpec((1, 8, D_MODEL), lambda i: (i // n_lat, 0, 0)),
            _const_spec(w_in.shape), _const_spec(qg.shape), _const_spec(kvg.shape),
            _const_spec(w_uq.shape), _const_spec(w_ukv.shape),
            pl.BlockSpec((tm, HEAD_PAD), row), pl.BlockSpec((tm, HEAD_PAD), row), pl.BlockSpec((tm, HEAD_PAD), row),
        ],
        out_specs=(pl.BlockSpec((tm, Z_COLS), row), head_spec, head_spec, head_spec,
                   pl.BlockSpec((N_HEADS, HEAD_PAD, tm), lambda i: (0, 0, i))),
        compiler_params=pltpu.CompilerParams(vmem_limit_bytes=VMEM_LIMIT),
    )(x2, ctx2, mod_a, w_in, qg, kvg, w_uq, w_ukv, cos_t, sin_a, sin_b)


def _key_chunks(e_rows, size):
    n_chunks = max(1, e_rows // size)
    return [(ci * size, size if ci < n_chunks - 1 else e_rows - ci * size) for ci in range(n_chunks)]


def _attn_fwd(q, k, v, t_lat, shard_arrays):
    e_rows = k.shape[1]
    tq = min(t_lat, ATTN_FWD_Q_BLOCK)
    bounds = _key_chunks(e_rows, KEY_CHUNK)
    c2 = ATTN_SCALE * LOG2E

    hb = ATTN_HEADS_PER_STEP
    n_hb = N_HEADS // hb

    def body(q_ref, k_ref, v_ref, o_ref, lse_ref):
        qs = [q_ref[b] for b in range(hb)]
        m, acc = [None] * hb, [None] * hb
        for lo, n in bounds:
            for b in range(hb):
                s = _dot_nt(qs[b], k_ref[b, lo:lo + n, :])
                mc = jnp.max(s, axis=-1, keepdims=True)
                m_new = mc if m[b] is None else jnp.maximum(m[b], mc)
                p = jnp.exp2((s - m_new) * c2)
                pv = _dot(p.astype(BF16), v_ref[b, lo:lo + n, :])
                acc[b] = pv if m[b] is None else acc[b] * jnp.exp2((m[b] - m_new) * c2) + pv
                m[b] = m_new
        for b in range(hb):
            l = acc[b][:, V_DIM:V_DIM + 1]
            o_ref[b] = (acc[b] * (1.0 / l)).astype(BF16)
            lse = (m[b] * ATTN_SCALE + jnp.log(l)) * LOG2E
            lse_ref[b] = jnp.broadcast_to(lse, (tq, HEAD_PAD)).T[0:1, :]

    n_w = len(shard_arrays)
    n_q = t_lat // tq

    def body_with_gather(q_ref, k_ref, v_ref, *rest):
        o_ref, lse_ref = rest[n_w], rest[n_w + 1]
        gather = _ShardGather(rest[n_w + 2:2 * n_w + 2], *rest[2 * n_w + 2:])
        h, i = pl.program_id(0), pl.program_id(1)
        pl.when(jnp.logical_and(h == 0, i == 0))(gather.start)
        pl.when(jnp.logical_and(h == n_hb // 2, i == 0))(gather.forward)
        body(q_ref, k_ref, v_ref, o_ref, lse_ref)
        pl.when(jnp.logical_and(h == n_hb - 1, i == n_q - 1))(gather.finish)

    hbm = pl.BlockSpec(memory_space=pl.ANY)
    return pl.pallas_call(
        body_with_gather, name="attn_fwd", grid=(n_hb, n_q),
        out_shape=(jax.ShapeDtypeStruct((N_HEADS, t_lat, HEAD_PAD), BF16),
                   jax.ShapeDtypeStruct((N_HEADS, 1, t_lat), F32))
        + tuple(jax.ShapeDtypeStruct(a.shape, a.dtype) for a in shard_arrays),
        in_specs=[pl.BlockSpec((hb, tq, HEAD_PAD), lambda h, i: (h, i, 0)),
                  pl.BlockSpec((hb, e_rows, HEAD_PAD), lambda h, i: (h, 0, 0)),
                  pl.BlockSpec((hb, e_rows, HEAD_PAD), lambda h, i: (h, 0, 0))] + [hbm] * n_w,
        out_specs=(pl.BlockSpec((hb, tq, HEAD_PAD), lambda h, i: (h, i, 0)),
                   pl.BlockSpec((hb, 1, tq), lambda h, i: (h, 0, i))) + (hbm,) * n_w,
        input_output_aliases={3 + a: 2 + a for a in range(n_w)},
        scratch_shapes=_gather_sems(n_w),
        compiler_params=pltpu.CompilerParams(vmem_limit_bytes=VMEM_LIMIT),
    )(q, k, v, *shard_arrays)


def _attn_bwd(q, k, v, kt, o, do, lse_row, t_lat, parts):
    e_rows = k.shape[1]
    tq = min(t_lat, ATTN_BWD_Q_BLOCK)
    n_p = len(parts)
    n_q = t_lat // tq
    bounds = _key_chunks(e_rows, KEY_CHUNK_BWD)

    hb = ATTN_BWD_HEADS_PER_STEP
    n_hb = N_HEADS // hb

    def body(q_ref, k_ref, v_ref, kt_ref, o_ref, do_ref, lse_ref, *rest):
        dqt_ref, dk_ref, dv_ref = rest[n_p:n_p + 3]
        scatter = _ChipScatter(rest[:n_p], rest[n_p + 3:2 * n_p + 3], rest[2 * n_p + 3], rest[2 * n_p + 4])
        h, i = pl.program_id(0), pl.program_id(1)
        pl.when(jnp.logical_and(h == 0, i == 0))(scatter.start)

        @pl.when(i == 0)
        def _():
            dk_ref[...] = jnp.zeros(dk_ref.shape, F32)
            dv_ref[...] = jnp.zeros(dv_ref.shape, F32)

        qs, dos, lses, deltas = [], [], [], []
        for b in range(hb):
            qs.append(q_ref[b])
            dos.append(do_ref[b])
            lses.append(lse_ref[b])
            prod = o_ref[b].astype(F32) * dos[b].astype(F32)
            deltas.append(lax.dot_general(jnp.ones((8, HEAD_PAD), F32), prod, (((1,), (1,)), ((), ())),
                                          precision=HIGHEST, preferred_element_type=F32)[0:1, :])
        dqt = [None] * hb
        for lo, n in bounds:
            for b in range(hb):
                pt = jnp.exp2(_dot_nt(k_ref[b, lo:lo + n, :], qs[b]) * (ATTN_SCALE * LOG2E) - lses[b])
                dpt = _dot_nt(v_ref[b, lo:lo + n, :], dos[b])
                dst = (pt * (dpt - deltas[b])).astype(BF16)
                dv_c = _dot(pt.astype(BF16), dos[b])
                dk_c = _dot(dst, qs[b])
                part = _dot(kt_ref[b, :, lo:lo + n], dst)
                dqt[b] = part if dqt[b] is None else dqt[b] + part
                dk_ref[b, lo:lo + n, :] += dk_c * ATTN_SCALE
                dv_ref[b, lo:lo + n, :] += dv_c
        for b in range(hb):
            dqt_ref[b] = dqt[b] * ATTN_SCALE

        pl.when(jnp.logical_and(h == n_hb - 1, i == n_q - 1))(scatter.finish)

    hbm = pl.BlockSpec(memory_space=pl.ANY)
    qspec = pl.BlockSpec((hb, tq, HEAD_PAD), lambda h, i: (h, i, 0))
    kspec = pl.BlockSpec((hb, e_rows, HEAD_PAD), lambda h, i: (h, 0, 0))
    return pl.pallas_call(
        body, name="attn_bwd", grid=(n_hb, n_q),
        out_shape=(jax.ShapeDtypeStruct((N_HEADS, HEAD_PAD, t_lat), F32),
                   jax.ShapeDtypeStruct((N_HEADS, e_rows, HEAD_PAD), F32),
                   jax.ShapeDtypeStruct((N_HEADS, e_rows, HEAD_PAD), F32))
        + tuple(jax.ShapeDtypeStruct((3,) + p.shape[1:], BF16) for p in parts),
        in_specs=[qspec, kspec, kspec, pl.BlockSpec((hb, HEAD_PAD, e_rows), lambda h, i: (h, 0, 0)), qspec, qspec,
                  pl.BlockSpec((hb, 1, tq), lambda h, i: (h, 0, i))] + [hbm] * n_p,
        out_specs=(pl.BlockSpec((hb, HEAD_PAD, tq), lambda h, i: (h, 0, i)), kspec, kspec) + (hbm,) * n_p,
        scratch_shapes=[pltpu.SemaphoreType.DMA((3 * n_p,)), pltpu.SemaphoreType.DMA((3 * n_p,))],
        compiler_params=pltpu.CompilerParams(vmem_limit_bytes=VMEM_LIMIT),
    )(q, k, v, kt, o, do, lse_row, *parts)


def _halo_specs(tm, col_block):
    per = tm // 8
    prev = pl.BlockSpec((8, CONV_W), lambda i: (jnp.maximum(i * per - 1, 0), col_block))
    nxt = pl.BlockSpec((8, CONV_W), lambda i: ((i + 1) * per, col_block))
    return prev, nxt


def _mlp_fwdbwd(o, z, x2, tgt, mod_b, gf, cw, wo_attn, wo_conv, w1, w2):
    t_lat = x2.shape[0]
    tm = TOK_TILE
    n_lat = t_lat // tm
    fc = MLP_FF_CHUNK
    n_ff = D_FF // fc

    def body(o_ref, gb_ref, gc_ref, xi_ref, gcp_ref, xip_ref, gcn_ref, xin_ref, cw_ref, woa_ref, woc_ref,
             x_ref, t_ref, mod_ref, gf_ref, w1_ref, w2_ref,
             r_ref, da_ref, h2_ref, dy2_ref, dx1_ref, conv_ref, acc_ref, dy1_ref, do_ref, dgb_ref, dyv_ref, ra_ref):
        i = pl.program_id(0)

        @pl.when(i == 0)
        def _():
            acc_ref[...] = jnp.zeros(acc_ref.shape, F32)

        g1, sh2, sc2, g2 = mod_ref[0:1, :], mod_ref[1:2, :], mod_ref[2:3, :], mod_ref[3:4, :]
        u = gc_ref[...] * xi_ref[...]
        u_prev = jnp.where(i > 0, gcp_ref[7:8, :] * xip_ref[7:8, :], 0.0)
        u_next = jnp.where(i < n_lat - 1, gcn_ref[0:1, :] * xin_ref[0:1, :], 0.0)
        um1, up1 = _shift_rows(u, u_prev, u_next)
        yv = cw_ref[0:1, :] * um1 + cw_ref[1:2, :] * u + cw_ref[2:3, :] * up1
        gb = gb_ref[...]
        conv = (gb * yv).astype(BF16)
        conv_ref[...] = conv
        y1 = _dot(conv, woc_ref[...])
        for h in range(N_HEADS):
            y1 = y1 + _dot(o_ref[h], woa_ref[h])
        x1 = x_ref[...] + g1 * y1
        rstd2 = lax.rsqrt(jnp.mean(x1 * x1, axis=-1, keepdims=True) + EPS)
        xn1 = x1 * rstd2
        h2 = (xn1 * (1.0 + sc2) + sh2).astype(BF16)
        h2_ref[...] = h2
        y2 = jnp.zeros((tm, D_MODEL), F32)
        for jj in range(n_ff):
            lo = jj * fc
            ra = jnp.maximum(_dot(h2, w1_ref[lo // FF_CHUNK, :, lo % FF_CHUNK:lo % FF_CHUNK + fc]), 0.0)
            ra_ref[jj] = ra
            r = (ra * ra).astype(BF16)
            r_ref[:, lo:lo + fc] = r
            y2 = y2 + _dot(r, w2_ref[lo:lo + fc, :])
        x2v = x1 + g2 * y2
        rstd3 = lax.rsqrt(jnp.mean(x2v * x2v, axis=-1, keepdims=True) + EPS)
        xn3 = x2v * rstd3
        gfv = gf_ref[...]
        diff = xn3 * gfv - t_ref[...]
        loss_t = 0.5 * jnp.sum(jnp.sum(diff * diff, axis=-1, keepdims=True), axis=0, keepdims=True) * (1.0 / D_MODEL)
        dy = diff * (1.0 / D_MODEL)
        dxn3 = dy * gfv
        dx2 = rstd3 * (dxn3 - xn3 * jnp.mean(dxn3 * xn3, axis=-1, keepdims=True))
        dy2 = (dx2 * g2).astype(BF16)
        dy2_ref[...] = dy2
        dh2 = jnp.zeros((tm, D_MODEL), F32)
        for jj in range(n_ff):
            lo = jj * fc
            dr = _dot_nt(dy2, w2_ref[lo:lo + fc, :])
            da = (2.0 * ra_ref[jj] * dr).astype(BF16)
            da_ref[:, lo:lo + fc] = da
            dh2 = dh2 + _dot_nt(da, w1_ref[lo // FF_CHUNK, :, lo % FF_CHUNK:lo % FF_CHUNK + fc])
        dxn1 = dh2 * (1.0 + sc2)
        dx1 = dx2 + rstd2 * (dxn1 - xn1 * jnp.mean(dxn1 * xn1, axis=-1, keepdims=True))
        dx1_ref[...] = dx1
        dy1 = (dx1 * g1).astype(BF16)
        dy1_ref[...] = dy1
        for h in range(N_HEADS):
            do_ref[h] = _dot_nt(dy1, woa_ref[h]).astype(BF16)
        dconv = _dot_nt(dy1, woc_ref[...])
        dgb_ref[...] = dconv * yv
        dyv_ref[...] = dconv * gb
        acc_ref[5:6, :] += jnp.sum(dx1 * y1, axis=0, keepdims=True)
        acc_ref[0:1, :] += jnp.sum(dy * xn3, axis=0, keepdims=True)
        acc_ref[1:2, :] += jnp.sum(dx2 * y2, axis=0, keepdims=True)
        acc_ref[2:3, :] += jnp.sum(dh2, axis=0, keepdims=True)
        acc_ref[3:4, :] += jnp.sum(dh2 * xn1, axis=0, keepdims=True)
        acc_ref[4:5, :] += jnp.broadcast_to(loss_t, (1, D_MODEL))

    row = lambda i: (i, 0)
    gcp, gcn = _halo_specs(tm, 2)
    xip, xin = _halo_specs(tm, 3)
    tile = pl.BlockSpec((tm, D_MODEL), row)
    wide = pl.BlockSpec((tm, D_FF), row)
    half = pl.BlockSpec((tm, CONV_W), row)
    return pl.pallas_call(
        body, name="mlp_fwdbwd", grid=(n_lat,),
        out_shape=(jax.ShapeDtypeStruct((t_lat, D_FF), BF16), jax.ShapeDtypeStruct((t_lat, D_FF), BF16),
                   jax.ShapeDtypeStruct((t_lat, D_MODEL), BF16), jax.ShapeDtypeStruct((t_lat, D_MODEL), BF16),
                   jax.ShapeDtypeStruct((t_lat, D_MODEL), F32), jax.ShapeDtypeStruct((t_lat, CONV_W), BF16),
                   jax.ShapeDtypeStruct((8, D_MODEL), F32),
                   jax.ShapeDtypeStruct((t_lat, D_MODEL), BF16),
                   jax.ShapeDtypeStruct((N_HEADS, t_lat, HEAD_PAD), BF16),
                   jax.ShapeDtypeStruct((t_lat, CONV_W), F32), jax.ShapeDtypeStruct((t_lat, CONV_W), F32)),
        in_specs=[
            pl.BlockSpec((N_HEADS, tm, HEAD_PAD), lambda i: (0, i, 0)),
            pl.BlockSpec((tm, CONV_W), lambda i: (i, 1)), pl.BlockSpec((tm, CONV_W), lambda i: (i, 2)),
            pl.BlockSpec((tm, CONV_W), lambda i: (i, 3)),
            gcp, xip, gcn, xin,
            _const_spec(cw.shape), _resident_spec(wo_attn.shape), _resident_spec(wo_conv.shape),
            tile, tile, _const_spec(mod_b.shape), _const_spec(gf.shape),
            _resident_spec(w1.shape), _resident_spec(w2.shape),
        ],
        out_specs=(wide, wide, tile, tile, tile, half, _const_spec((8, D_MODEL)),
                   tile, pl.BlockSpec((N_HEADS, tm, HEAD_PAD), lambda i: (0, i, 0)), half, half),
        scratch_shapes=[pltpu.VMEM((n_ff, tm, fc), F32)],
        compiler_params=pltpu.CompilerParams(vmem_limit_bytes=VMEM_LIMIT),
    )(o, z, z, z, z, z, z, z, cw, wo_attn, wo_conv, x2, tgt, mod_b, gf, w1, w2)


def _inproj_bwd(x2, ctx2, mod_a, z, dyv, dgb, dx1, dqt, dk, dv, cos_t, sin_a, sin_b, w_in, w_uq, w_ukv, qg, kvg, cw,
                swap_halves):
    t_lat, t_ctx = x2.shape[0], ctx2.shape[0]
    tm = TOK_TILE
    n_lat = t_lat // tm
    n_all = n_lat + t_ctx // tm
    group = max(g for g in (1, 2, 4) if n_lat % g == 0)
    n_s = len(swap_halves)

    def body(x_ref, ctx_ref, mod_ref, z_ref, gcp_ref, xip_ref, gcn_ref, xin_ref, dyv_ref, dyvp_ref, dyvn_ref,
             dgb_ref, dx1_ref, dqt_ref, dk_ref, dv_ref, cos_ref, sa_ref, sb_ref, win_ref, wuq_ref, wukv_ref,
             qg_ref, kvg_ref, cw_ref, *rest):
        gx_ref, dwin_ref, dwuq_ref, dwukv_ref, acc_ref = rest[n_s:n_s + 5]
        h1_buf, dz_buf = rest[2 * n_s + 5:2 * n_s + 7]
        swap = _SiblingSwap(rest[:n_s], rest[n_s + 5:2 * n_s + 5], rest[2 * n_s + 7], rest[2 * n_s + 8])
        i = pl.program_id(0)
        lat = i < n_lat
        pl.when(i == 0)(swap.start)

        @pl.when(i == 0)
        def _():
            dwin_ref[...] = jnp.zeros(dwin_ref.shape, F32)
            dwuq_ref[...] = jnp.zeros(dwuq_ref.shape, F32)
            dwukv_ref[...] = jnp.zeros(dwukv_ref.shape, F32)
            acc_ref[...] = jnp.zeros(acc_ref.shape, F32)

        xin = jnp.where(lat, x_ref[...], ctx_ref[...])
        rstd = lax.rsqrt(jnp.mean(xin * xin, axis=-1, keepdims=True) + EPS)
        xn = xin * rstd
        sc = mod_ref[0, 1:2, :]
        h1 = (xn * (1.0 + sc) + mod_ref[0, 0:1, :]).astype(BF16)
        z = z_ref[...]
        cos, sa, sb = cos_ref[...], sa_ref[...], sb_ref[...]
        qgv, kvgv = qg_ref[...], kvg_ref[...]
        cq = z[:, 0:Q_RANK]
        cqh = cq * lax.rsqrt(jnp.mean(cq * cq, axis=-1, keepdims=True) + EPS)
        rq = lax.rsqrt(jnp.mean(cq * cq, axis=-1, keepdims=True) + EPS)
        cqn = (cqh * qgv).astype(BF16)
        parts = []
        for h in range(N_HEADS):
            g = jnp.where(lat, dqt_ref[h].T, 0.0)
            parts.append(_unrope(g, cos, sa, sb))
        dq = jnp.concatenate(parts, axis=1).astype(BF16)
        dcqn = _dot_nt(dq, wuq_ref[...])
        dwuq_ref[...] += _dot_tn(cqn, dq)
        acc_ref[4:5, 0:Q_RANK] += jnp.sum(dcqn * cqh, axis=0, keepdims=True)
        dxn = dcqn * qgv
        dcq = rq * (dxn - cqh * jnp.mean(dxn * cqh, axis=-1, keepdims=True))
        ckv = z[:, Q_RANK:Q_RANK + KV_RANK]
        rk = lax.rsqrt(jnp.mean(ckv * ckv, axis=-1, keepdims=True) + EPS)
        ckvh = ckv * rk
        ckvn = (ckvh * kvgv).astype(BF16)
        dks = [dk_ref[h] for h in range(N_HEADS)]
        dkr = dks[0]
        for h in range(1, N_HEADS):
            dkr = dkr + dks[h]
        dkv = jnp.concatenate(dks + [dv_ref[h] for h in range(N_HEADS)], axis=1).astype(BF16)
        dckvn = _dot_nt(dkv, wukv_ref[...])
        dwukv_ref[...] += _dot_tn(ckvn, dkv)
        acc_ref[5:6, 0:KV_RANK] += jnp.sum(dckvn * ckvh, axis=0, keepdims=True)
        dxn = dckvn * kvgv
        dckv = rk * (dxn - ckvh * jnp.mean(dxn * ckvh, axis=-1, keepdims=True))
        dkr = _unrope(dkr, cos, sa, sb)
        gb, gc, xi = z[:, 512:1024], z[:, 1024:1536], z[:, 1536:2048]
        u = gc * xi
        u_prev = jnp.where(i > 0, gcp_ref[7:8, :] * xip_ref[7:8, :], 0.0)
        u_next = jnp.where(i < n_lat - 1, gcn_ref[0:1, :] * xin_ref[0:1, :], 0.0)
        um1, up1 = _shift_rows(u, u_prev, u_next)
        dyv = jnp.where(lat, dyv_ref[...], 0.0)
        dyv_prev = jnp.where(jnp.logical_and(i > 0, lat), dyvp_ref[7:8, :], 0.0)
        dyv_next = jnp.where(i < n_lat - 1, dyvn_ref[0:1, :], 0.0)
        dyv_m1, dyv_p1 = _shift_rows(dyv, dyv_prev, dyv_next)
        du = cw_ref[0:1, :] * dyv_p1 + cw_ref[1:2, :] * dyv + cw_ref[2:3, :] * dyv_m1
        dgc = du * xi
        dxi = du * gc
        dgb = jnp.where(lat, dgb_ref[...], 0.0)
        acc_ref[6:7, 0:CONV_W] += jnp.sum(dyv * um1, axis=0, keepdims=True)
        acc_ref[7:8, 0:CONV_W] += jnp.sum(dyv * u, axis=0, keepdims=True)
        acc_ref[8:9, 0:CONV_W] += jnp.sum(dyv * up1, axis=0, keepdims=True)
        dz = jnp.concatenate([dcq, dckv, dkr, dgb, dgc, dxi], axis=1).astype(BF16)
        dh1 = _dot_nt(dz, win_ref[...])
        slot = i % group
        rows_g = pl.ds(pl.multiple_of(slot * tm, tm), tm)
        h1_buf[rows_g, :] = h1
        dz_buf[rows_g, :] = dz

        @pl.when(jnp.logical_and(lat, slot == group - 1))
        def _():
            dwin_ref[...] += _dot_tn(h1_buf[...], dz_buf[...])

        @pl.when(jnp.logical_not(lat))
        def _():
            dwin_ref[...] += _dot_tn(h1, dz)
        s_sh = jnp.sum(dh1, axis=0, keepdims=True)
        s_sc = jnp.sum(dh1 * xn, axis=0, keepdims=True)
        zero = jnp.zeros_like(s_sh)
        acc_ref[0:1, :] += jnp.where(lat, s_sh, zero)
        acc_ref[1:2, :] += jnp.where(lat, s_sc, zero)
        acc_ref[2:3, :] += jnp.where(lat, zero, s_sh)
        acc_ref[3:4, :] += jnp.where(lat, zero, s_sc)
        dxn = dh1 * (1.0 + sc)
        dx = rstd * (dxn - xn * jnp.mean(dxn * xn, axis=-1, keepdims=True))

        @pl.when(lat)
        def _():
            gx_ref[...] = dx1_ref[...] + dx

        pl.when(i == n_all - 1)(swap.finish)

    hbm = pl.BlockSpec(memory_space=pl.ANY)
    last = n_lat - 1
    per = tm // 8
    lat_row = lambda i: (jnp.minimum(i, last), 0)
    row = lambda i: (i, 0)
    gcp, gcn = _halo_specs(tm, 2)
    xip, xin = _halo_specs(tm, 3)
    n_halo = t_lat // 8
    dyvp = pl.BlockSpec((8, CONV_W), lambda i: (jnp.clip(i * per - 1, 0, n_halo - 1), 0))
    dyvn = pl.BlockSpec((8, CONV_W), lambda i: (jnp.minimum((i + 1) * per, n_halo - 1), 0))
    gcn = pl.BlockSpec((8, CONV_W), lambda i: (jnp.minimum((i + 1) * per, (t_lat + t_ctx) // 8 - 1), 2))
    xin = pl.BlockSpec((8, CONV_W), lambda i: (jnp.minimum((i + 1) * per, (t_lat + t_ctx) // 8 - 1), 3))
    head_f32 = pl.BlockSpec((N_HEADS, tm, HEAD_PAD), lambda i: (0, i, 0))
    tab = pl.BlockSpec((tm, HEAD_PAD), row)
    return pl.pallas_call(
        body, name="inproj_bwd", grid=(n_all,),
        out_shape=(jax.ShapeDtypeStruct((t_lat, D_MODEL), F32), jax.ShapeDtypeStruct(w_in.shape, F32),
                   jax.ShapeDtypeStruct(w_uq.shape, F32), jax.ShapeDtypeStruct(w_ukv.shape, F32),
                   jax.ShapeDtypeStruct((16, D_MODEL), F32))
        + tuple(jax.ShapeDtypeStruct(h.shape, F32) for h in swap_halves),
        in_specs=[
            pl.BlockSpec((tm, D_MODEL), lat_row), _const_spec((tm, D_MODEL)),
            pl.BlockSpec((1, 8, D_MODEL), lambda i: (i // n_lat, 0, 0)),
            pl.BlockSpec((tm, Z_COLS), row), gcp, xip, gcn, xin,
            pl.BlockSpec((tm, CONV_W), lat_row), dyvp, dyvn,
            pl.BlockSpec((tm, CONV_W), lat_row), pl.BlockSpec((tm, D_MODEL), lat_row),
            pl.BlockSpec((N_HEADS, HEAD_PAD, tm), lambda i: (0, 0, jnp.minimum(i, last))),
            head_f32, head_f32, tab, tab, tab,
            _const_spec(w_in.shape), _const_spec(w_uq.shape), _const_spec(w_ukv.shape),
            _const_spec(qg.shape), _const_spec(kvg.shape), _const_spec(cw.shape),
        ] + [hbm] * n_s,
        out_specs=(pl.BlockSpec((tm, D_MODEL), lat_row), _const_spec(w_in.shape), _const_spec(w_uq.shape),
                   _const_spec(w_ukv.shape), _const_spec((16, D_MODEL))) + (hbm,) * n_s,
        scratch_shapes=[pltpu.VMEM((group * tm, D_MODEL), BF16), pltpu.VMEM((group * tm, Z_COLS), BF16),
                        pltpu.SemaphoreType.DMA((n_s,)), pltpu.SemaphoreType.DMA((n_s,))],
        compiler_params=pltpu.CompilerParams(vmem_limit_bytes=VMEM_LIMIT),
    )(x2, ctx2, mod_a, z, z, z, z, z, dyv, dyv, dyv, dgb, dx1, dqt, dk, dv, cos_t, sin_a, sin_b, w_in, w_uq, w_ukv,
      qg, kvg, cw, *swap_halves)


def _wgrad(a, b, name, bm, bn):
    t, m = a.shape
    n = b.shape[1]
    bk = min(t, 4096)
    nk = t // bk
    nj = n // bn

    def body(a_ref, b_ref, o_ref, acc_ref):
        k = pl.program_id(2)
        part = _dot_tn(a_ref[...], b_ref[...])

        @pl.when(k == 0)
        def _():
            acc_ref[...] = part

        @pl.when(k > 0)
        def _():
            acc_ref[...] += part

        @pl.when(k == nk - 1)
        def _():
            o_ref[...] = acc_ref[...].astype(BF16)

    return pl.pallas_call(
        body, name=name, grid=(m // bm, nj, nk), out_shape=jax.ShapeDtypeStruct((m // bm * nj, bm, bn), BF16),
        in_specs=[pl.BlockSpec((bk, bm), lambda i, j, k: (k, i)), pl.BlockSpec((bk, bn), lambda i, j, k: (k, j))],
        out_specs=pl.BlockSpec((None, bm, bn), lambda i, j, k: (i * nj + j, 0, 0)),
        scratch_shapes=[pltpu.VMEM((bm, bn), F32)],
        compiler_params=pltpu.CompilerParams(vmem_limit_bytes=VMEM_LIMIT),
    )(a, b)


def _wgrad_out(o, conv, dy1, sib_arrays):
    t = o.shape[1]
    bk = min(t, 2048)
    nk = t // bk
    rows = N_HEADS * HEAD_PAD + CONV_W
    n_s = len(sib_arrays)

    def body(o_ref, c_ref, d_ref, *rest):
        w_ref, acc_ref = rest[n_s], rest[2 * n_s + 1]
        send = _SiblingSend(rest[:n_s], rest[n_s + 1:2 * n_s + 1], rest[2 * n_s + 2], rest[2 * n_s + 3])
        k = pl.program_id(0)
        pl.when(k == 0)(send.start)
        cat = jnp.concatenate([o_ref[h] for h in range(N_HEADS)] + [c_ref[...]], axis=1)
        part = _dot_tn(cat, d_ref[...])

        @pl.when(k == 0)
        def _():
            acc_ref[...] = part

        @pl.when(k > 0)
        def _():
            acc_ref[...] += part

        @pl.when(k == nk - 1)
        def _():
            for h in range(N_HEADS):
                w_ref[h * V_DIM:(h + 1) * V_DIM, :] = acc_ref[h * HEAD_PAD:h * HEAD_PAD + V_DIM, :].astype(BF16)
            w_ref[N_HEADS * V_DIM:, :] = acc_ref[N_HEADS * HEAD_PAD:, :].astype(BF16)

        pl.when(k == nk - 1)(send.finish)

    hbm = pl.BlockSpec(memory_space=pl.ANY)
    return pl.pallas_call(
        body, name="wgrad_out", grid=(nk,),
        out_shape=(jax.ShapeDtypeStruct((D_MODEL, D_MODEL), BF16),)
        + tuple(jax.ShapeDtypeStruct((N_SHARD, a.shape[1] // 2, a.shape[2]), BF16) for a in sib_arrays),
        in_specs=[pl.BlockSpec((N_HEADS, bk, HEAD_PAD), lambda k: (0, k, 0)),
                  pl.BlockSpec((bk, CONV_W), lambda k: (k, 0)),
                  pl.BlockSpec((bk, D_MODEL), lambda k: (k, 0))] + [hbm] * n_s,
        out_specs=(_const_spec((D_MODEL, D_MODEL)),) + (hbm,) * n_s,
        scratch_shapes=[pltpu.VMEM((rows, D_MODEL), F32), pltpu.SemaphoreType.DMA((n_s,)),
                        pltpu.SemaphoreType.DMA((n_s,))],
        compiler_params=pltpu.CompilerParams(vmem_limit_bytes=VMEM_LIMIT),
    )(o, conv, dy1, *sib_arrays)


def _adamw_call(w, g, m, v, name):
    rows, cols = w.shape
    rb = 256 if rows % 256 == 0 else rows

    def body(w_ref, g_ref, m_ref, v_ref, d_ref, nm_ref, nv_ref):
        d_, m_, v_ = _adamw(w_ref[...], g_ref[...], m_ref[...], v_ref[...])
        d_ref[...] = d_
        nm_ref[...] = m_
        nv_ref[...] = v_

    spec = pl.BlockSpec((rb, cols), lambda i: (i, 0))
    shp = jax.ShapeDtypeStruct((rows, cols), F32)
    return pl.pallas_call(
        body, name=name, grid=(rows // rb,), out_shape=(shp, shp, shp),
        in_specs=[spec] * 4, out_specs=(spec, spec, spec),
    )(w, g, m, v)


def _adamw_halves(w, g_mine, g_theirs, m, v, c_idx, name):
    rows, cols = w.shape
    half = rows // 2
    rb = min(256, half)
    nb = half // rb

    def body(c_ref, w_ref, gm_ref, gt_ref, m_ref, v_ref, g_ref, d_ref, nm_ref, nv_ref):
        mine = pl.program_id(0) // nb == c_ref[0]
        g = jnp.where(mine, gm_ref[...], gt_ref[...])
        d_, m_, v_ = _adamw(w_ref[...], g, m_ref[...], v_ref[...])
        g_ref[...] = g
        d_ref[...] = d_
        nm_ref[...] = m_
        nv_ref[...] = v_

    spec = pl.BlockSpec((rb, cols), lambda i, c_ref: (i, 0))
    hspec = pl.BlockSpec((rb, cols), lambda i, c_ref: (i % nb, 0))
    shp = jax.ShapeDtypeStruct((rows, cols), F32)
    grid_spec = pltpu.PrefetchScalarGridSpec(
        num_scalar_prefetch=1, grid=(rows // rb,), in_specs=[spec, hspec, hspec, spec, spec],
        out_specs=(spec, spec, spec, spec))
    return pl.pallas_call(
        body, name=name, grid_spec=grid_spec, out_shape=(shp, shp, shp, shp),
    )(c_idx, w, g_mine, g_theirs, m, v)


def _wmod_update(s_t, dm, w, m, v):
    rows, cols = w.shape
    cb = 512

    def body(s_ref, dm_ref, w_ref, m_ref, v_ref, g_ref, d_ref, nm_ref, nv_ref):
        g = jnp.dot(s_ref[...], dm_ref[...], precision=HIGHEST, preferred_element_type=F32)
        d_, m_, v_ = _adamw(w_ref[...], g, m_ref[...], v_ref[...])
        g_ref[...] = g
        d_ref[...] = d_
        nm_ref[...] = m_
        nv_ref[...] = v_

    spec = pl.BlockSpec((rows, cb), lambda i: (0, i))
    shp = jax.ShapeDtypeStruct((rows, cols), F32)
    return pl.pallas_call(
        body, name="wmod_update", grid=(cols // cb,), out_shape=(shp, shp, shp, shp),
        in_specs=[_const_spec(s_t.shape), pl.BlockSpec((16, cb), lambda i: (0, i)), spec, spec, spec],
        out_specs=(spec, spec, spec, spec),
        compiler_params=pltpu.CompilerParams(vmem_limit_bytes=VMEM_LIMIT),
    )(s_t, dm, w, m, v)


def _rope_tables(t_lat, t_ctx):
    t = jnp.arange(t_lat)
    pos = jnp.stack([(t // GRID_W).astype(F32), (t % GRID_W).astype(F32)], axis=1)
    half = QK_ROPE // 4
    freqs = ROPE_THETA ** (-jnp.arange(0, 2 * half, 2, dtype=F32) / (2 * half))
    ang = pos[:, :, None] * freqs[None, None, :]
    cos, sin = jnp.cos(ang), jnp.sin(ang)
    zero = jnp.zeros_like(sin)
    cos32 = jnp.concatenate([cos, cos], axis=2).reshape(t_lat, QK_ROPE)
    sa32 = jnp.concatenate([zero, sin], axis=2).reshape(t_lat, QK_ROPE)
    sb32 = jnp.concatenate([-sin, zero], axis=2).reshape(t_lat, QK_ROPE)

    def widen(tab, fill):
        left = jnp.full((t_lat, ROPE_LANE0), fill, F32)
        right = jnp.full((t_lat, HEAD_PAD - ROPE_LANE0 - QK_ROPE), fill, F32)
        lat = jnp.concatenate([left, tab, right], axis=1)
        return jnp.concatenate([lat, jnp.full((t_ctx, HEAD_PAD), fill, F32)], axis=0)

    return widen(cos32, 1.0), widen(sa32, 0.0), widen(sb32, 0.0)


def _cols_from_shards(s):
    return jnp.transpose(s, (1, 0, 2)).reshape(s.shape[1], -1)


def _cols_to_shards(w):
    k, n = w.shape
    return jnp.transpose(w.reshape(k, N_SHARD, n // N_SHARD), (1, 0, 2))


def kernel(x, c, ctx, c_ctx, w_mod, b_mod, w_in, q_norm_g, w_uq, kv_norm_g, w_ukv, conv_w, w_out, w_mlp1, w_mlp2, final_norm_g, loss_target, m_c_ctx, m_w_mod, m_b_mod, m_w_in, m_q_norm_g, m_w_uq, m_kv_norm_g, m_w_ukv, m_conv_w, m_w_out, m_w_mlp1, m_w_mlp2, m_final_norm_g, v_c_ctx, v_w_mod, v_b_mod, v_w_in, v_q_norm_g, v_w_uq, v_kv_norm_g, v_w_ukv, v_conv_w, v_w_out, v_w_mlp1, v_w_mlp2, v_final_norm_g):
    t_lat, t_ctx = x.shape[1], ctx.shape[1]
    assert t_ctx == TOK_TILE and t_lat % TOK_TILE == 0 and t_lat % GRID_W == 0
    mx, my, mc = _pos()
    me = 4 * mx + 2 * my + mc
    j = 2 * mx + my
    ncol = w_mod.shape[2]
    x2, ctx2, tgt = x[0], ctx[0], loss_target[0]
    cctx_row = c_ctx.reshape(1, D_MODEL)

    b_sh = lax.dynamic_slice(b_mod, (0, j * ncol), (1, ncol))
    cw_pad = jnp.zeros((8, 128), F32).at[0:3, :].set(conv_w[0])
    c8, m_all, g_in, g_uq, g_ukv, g_out, g_m1, g_m2 = _prologue(
        c, cctx_row, w_mod[0], b_sh, cw_pad, (w_in[0], w_uq[0], w_ukv[0], w_out[0], w_mlp1[0], w_mlp2[0]), 3)
    mvec = m_all[:, 0, :].reshape(6, D_MODEL)
    mctx = m_all[:, 8, :].reshape(6, D_MODEL)
    zeros6 = jnp.zeros((6, D_MODEL), F32)
    mod_a = jnp.stack([jnp.concatenate([mvec[0:2], zeros6], axis=0), jnp.concatenate([mctx[0:2], zeros6], axis=0)])
    mod_b = jnp.concatenate([mvec[2:6], jnp.zeros((4, D_MODEL), F32)], axis=0)
    cw_full = jnp.pad(jnp.transpose(m_all[:, 9:12, 0:128], (1, 0, 2)).reshape(3, CONV_W), ((0, 5), (0, 0)))

    w_in_f = _cols_from_shards(g_in)
    zc = lambda n: jnp.zeros((D_MODEL, n), BF16)
    w_in_p = jnp.concatenate([w_in_f[:, 0:384], zc(64), w_in_f[:, 384:416], zc(32), w_in_f[:, 416:]], axis=1)
    w_uq_f = _cols_from_shards(g_uq).reshape(Q_RANK, N_HEADS, QK_DIM)
    w_uq_p = jnp.pad(w_uq_f, ((0, 0), (0, 0), (0, HEAD_PAD - QK_DIM))).reshape(Q_RANK, N_HEADS * HEAD_PAD)
    w_ukv_f = _cols_from_shards(g_ukv).reshape(KV_RANK, N_HEADS, QK_NOPE + V_DIM)
    padh = lambda a: jnp.pad(a, ((0, 0), (0, 0), (0, HEAD_PAD - a.shape[2]))).reshape(KV_RANK, N_HEADS * HEAD_PAD)
    w_ukv_p = jnp.concatenate([padh(w_ukv_f[:, :, :QK_NOPE]), padh(w_ukv_f[:, :, QK_NOPE:])], axis=1)
    cos_t, sin_a, sin_b = _rope_tables(t_lat, t_ctx)
    gf_row = final_norm_g.reshape(1, D_MODEL)
    c_idx = mc.reshape(1).astype(jnp.int32)
    j_idx = j.reshape(1).astype(jnp.int32)

    z, q, k, v, kt = _inproj_fwd(x2, ctx2, mod_a, w_in_p, q_norm_g, kv_norm_g, w_uq_p, w_ukv_p, cos_t, sin_a, sin_b)
    o, lse, g_out, w1, g_m2 = _attn_fwd(q, k, v, t_lat, (g_out, g_m1, g_m2))
    w_out_f = g_out.reshape(D_MODEL, D_MODEL)
    wo_attn = jnp.pad(w_out_f[0:512].reshape(N_HEADS, V_DIM, D_MODEL), ((0, 0), (0, HEAD_PAD - V_DIM), (0, 0)))
    wo_conv = w_out_f[512:]
    w2 = g_m2.reshape(D_FF, D_MODEL)
    r, da, h2, dy2, dx1, conv, acc_mlp, dy1, do, dgb, dyv = _mlp_fwdbwd(o, z, x2, tgt, mod_b, gf_row, cw_full, wo_attn,
                                                                         wo_conv, w1, w2)
    d_w1 = _wgrad(h2, da, "wgrad_mlp1", D_MODEL, FF_CHUNK)
    d_w2 = _wgrad(r, dy2, "wgrad_mlp2", FF_CHUNK, D_MODEL)
    big_grads, big_names = (d_w1, d_w2), ("mlp1", "mlp2")
    d_wout, *big_got = _wgrad_out(o, conv, dy1, big_grads)
    d_wout = d_wout.reshape(N_SHARD, D_MODEL // N_SHARD, D_MODEL)
    big_parts = [_add_pairs(a, g, c_idx, "rs_add_pairs_" + n) for a, g, n in zip(big_grads, big_got, big_names)]
    dqt, dk, dv, *big_recv = _attn_bwd(q, k, v, kt, o, do, lse, t_lat, big_parts)
    big_halves = [_add_chips(p, g, j_idx, "rs_add_chips_" + n) for p, g, n in zip(big_parts, big_recv, big_names)]
    gx, d_win, d_wuq, d_wukv, acc_in, *big_theirs = _inproj_bwd(
        x2, ctx2, mod_a, z, dyv, dgb, dx1, dqt, dk, dv, cos_t, sin_a, sin_b, w_in_p, w_uq_p, w_ukv_p, q_norm_g,
        kv_norm_g, cw_full, big_halves)

    pad_row = lambda a: jnp.pad(a, ((0, 0), (0, D_MODEL - a.shape[1])))
    sv = jnp.concatenate([
        acc_in[0:2], acc_mlp[5:6], acc_mlp[2:4], acc_mlp[1:2],
        acc_in[2:4], acc_mlp[0:1], acc_in[4:5], acc_in[5:6], acc_in[6:9], acc_mlp[4:5],
        jnp.zeros((1, D_MODEL), F32)], axis=0)
    all_sv, red, o_cc, o_b, o_q, o_k, o_gf = _small_exchange(
        sv, w_mod[0], cctx_row, m_c_ctx.reshape(1, D_MODEL), v_c_ctx.reshape(1, D_MODEL),
        b_mod.reshape(6, D_MODEL), m_b_mod.reshape(6, D_MODEL), v_b_mod.reshape(6, D_MODEL),
        q_norm_g, m_q_norm_g, v_q_norm_g, kv_norm_g, m_kv_norm_g, v_kv_norm_g,
        gf_row, m_final_norm_g.reshape(1, D_MODEL), v_final_norm_g.reshape(1, D_MODEL))
    loss = red[14, 0]

    c9 = jnp.concatenate([c8[0::8], jnp.zeros((7, D_MODEL), F32)], axis=0)
    s_t = jnp.transpose(c9 * jax.nn.sigmoid(c9))
    dm_ex = all_sv[:, 0:6, :].reshape(8, 6 * D_MODEL)
    dm_ctx = jnp.concatenate([red[6:8].reshape(1, 2 * D_MODEL), jnp.zeros((1, 4 * D_MODEL), F32)], axis=1)
    dm16 = jnp.concatenate([dm_ex, dm_ctx, jnp.zeros((7, 6 * D_MODEL), F32)], axis=0)
    dm_sh = lax.dynamic_slice(dm16, (0, j * ncol), (16, ncol))
    g_wmod, d_wmod, nm_wmod, nv_wmod = _wmod_update(s_t, dm_sh, w_mod[0], m_w_mod[0], v_w_mod[0])

    g_cw = lax.dynamic_slice(red[11:14, 0:CONV_W], (0, j * 128), (3, 128))
    d_cw, nm_cw, nv_cw = _adamw_call(conv_w[0], g_cw, m_conv_w[0], v_conv_w[0], "adamw_conv")

    d_win_f = jnp.concatenate([d_win[:, 0:384], d_win[:, 448:480], d_win[:, 512:]], axis=1)
    d_wuq_f = d_wuq.reshape(Q_RANK, N_HEADS, HEAD_PAD)[:, :, 0:QK_DIM].reshape(Q_RANK, N_HEADS * QK_DIM)
    d_wukv3 = d_wukv.reshape(KV_RANK, 2, N_HEADS, HEAD_PAD)
    d_wukv_f = jnp.concatenate([d_wukv3[:, 0, :, 0:QK_NOPE], d_wukv3[:, 1, :, 0:V_DIM]], axis=2).reshape(KV_RANK, -1)
    rest = tuple(_cols_to_shards(a).astype(BF16) for a in (d_win_f, d_wuq_f, d_wukv_f)) + (d_wout,)
    rest_names = ("in", "uq", "ukv", "out")
    rest_got = _rs_sibling(rest, "rs_sibling_rest")
    rest_parts = [_add_pairs(a, g, c_idx, "rs_add_pairs_" + n) for a, g, n in zip(rest, rest_got, rest_names)]
    rest_recv = _rs_chips(rest_parts, "rs_chips_rest")
    rest_halves = [_add_chips(p, g, j_idx, "rs_add_chips_" + n) for p, g, n in zip(rest_parts, rest_recv, rest_names)]
    g_win, g_wuq, g_wukv, g_wout = _rs_join(rest_halves, "rs_join_rest")
    upd = {}
    for name, w_, g_, m_, v_ in (("in", w_in, g_win, m_w_in, v_w_in), ("uq", w_uq, g_wuq, m_w_uq, v_w_uq),
                                 ("ukv", w_ukv, g_wukv, m_w_ukv, v_w_ukv), ("out", w_out, g_wout, m_w_out, v_w_out)):
        upd[name] = _adamw_call(w_[0], g_, m_[0], v_[0], "adamw_" + name)
    g_w1, *upd["mlp1"] = _adamw_halves(w_mlp1[0], big_halves[0], big_theirs[0], m_w_mlp1[0], v_w_mlp1[0], c_idx,
                                       "adamw_mlp1")
    g_w2, *upd["mlp2"] = _adamw_halves(w_mlp2[0], big_halves[1], big_theirs[1], m_w_mlp2[0], v_w_mlp2[0], c_idx,
                                       "adamw_mlp2")

    def four(o4, shape):
        return [o4[r].reshape(shape) for r in range(4)]

    cc4 = four(o_cc, (D_MODEL,))
    b4 = [o_b[r].reshape(1, 6 * D_MODEL) for r in range(4)]
    q4 = four(o_q, (1, Q_RANK))
    k4 = four(o_k, (1, KV_RANK))
    gf4 = four(o_gf, (D_MODEL,))
    big = {"in": g_win, "uq": g_wuq, "ukv": g_wukv, "out": g_wout, "mlp1": g_w1, "mlp2": g_w2}

    def leaf(idx):
        wm = (g_wmod, d_wmod, nm_wmod, nv_wmod)[idx]
        cwv = (g_cw, d_cw, nm_cw, nv_cw)[idx]
        bigv = {n: (big[n] if idx == 0 else upd[n][idx - 1]) for n in big}
        return [cc4[idx], wm[None], b4[idx], bigv["in"][None], q4[idx], bigv["uq"][None], k4[idx], bigv["ukv"][None],
                cwv[None], bigv["out"][None], bigv["mlp1"][None], bigv["mlp2"][None], gf4[idx]]

    return (loss, gx[None], *leaf(0), *leaf(1), *leaf(2), *leaf(3))
```

```python
import functools
import math

import jax
import jax.numpy as jnp
from jax import lax
from jax.experimental import pallas as pl
from jax.experimental.pallas import tpu as pltpu

F32 = jnp.float32
BF16 = jnp.bfloat16
MESH = pl.DeviceIdType.MESH
HIGHEST = lax.Precision.HIGHEST

D_MODEL = 1024
N_HEADS = 8
QK_NOPE = 64
QK_ROPE = 32
QK_DIM = QK_NOPE + QK_ROPE
V_DIM = 64
Q_RANK = 256
KV_RANK = 128
CONV_W = 512
D_FF = 4096
GRID_W = 64
ROPE_THETA = 10000.0
EPS = 1e-6
ATTN_SCALE = 1.0 / math.sqrt(QK_DIM)
HEAD_PAD = 128
Z_COLS = 2048
ROPE_LANE0 = QK_NOPE
N_SHARD = 4
TOK_TILE = 256
FF_CHUNK = 1024
MLP_FF_CHUNK = 1024
KEY_CHUNK = 512
ATTN_FWD_Q_BLOCK = 512
ATTN_HEADS_PER_STEP = 4
ATTN_BWD_HEADS_PER_STEP = 2
ATTN_BWD_Q_BLOCK = 512
KEY_CHUNK_BWD = 512

ADAM_LR = 0.001
ADAM_B1 = 0.9
ADAM_B2 = 0.999
ADAM_EPS = 1e-08
ADAM_WD = 0.01
ADAM_STEP = 10

LOG2E = 1.4426950408889634

VMEM_LIMIT = 56 * 1024 * 1024
STAGE_VMEM_LIMIT = 32 * 1024 * 1024


def _pos():
    return lax.axis_index("x"), lax.axis_index("y"), lax.axis_index("c")


def _dot(a, b):
    return jnp.dot(a, b, preferred_element_type=F32)


def _dot_nt(a, b):
    return lax.dot_general(a, b, (((1,), (1,)), ((), ())), preferred_element_type=F32)


def _dot_tn(a, b):
    return lax.dot_general(a, b, (((0,), (0,)), ((), ())), preferred_element_type=F32)


def _rope(v, cos, sa, sb):
    return v * cos + pltpu.roll(v, 8, 1) * sa + pltpu.roll(v, HEAD_PAD - 8, 1) * sb


def _unrope(g, cos, sa, sb):
    return g * cos + pltpu.roll(g * sa, HEAD_PAD - 8, 1) + pltpu.roll(g * sb, 8, 1)


def _sigmoid(v):
    return 1.0 / (1.0 + jnp.exp(-v))


def _adamw(w, g, m, v):
    m = ADAM_B1 * m + (1.0 - ADAM_B1) * g
    v = ADAM_B2 * v + (1.0 - ADAM_B2) * (g * g)
    m_hat = m / (1.0 - ADAM_B1 ** ADAM_STEP)
    v_hat = v / (1.0 - ADAM_B2 ** ADAM_STEP)
    delta = -ADAM_LR * (m_hat / (jnp.sqrt(v_hat) + ADAM_EPS) + ADAM_WD * w)
    return delta, m, v


def _shift_rows(u, prev_row, next_row):
    n = u.shape[0]
    rows = lax.broadcasted_iota(jnp.int32, u.shape, 0)
    um1 = jnp.where(rows == 0, prev_row, pltpu.roll(u, 1, 0))
    up1 = jnp.where(rows == n - 1, next_row, pltpu.roll(u, n - 1, 0))
    return um1, up1


def _const_spec(shape):
    nd = len(shape)
    return pl.BlockSpec(shape, lambda *_: (0,) * nd)


def _resident_spec(shape):
    nd = len(shape)
    return pl.BlockSpec(shape, lambda *_: (0,) * nd, pipeline_mode=pl.Buffered(1))


def _peer(r, x, y, c):
    px = 1 - x if r & 4 else x
    py = 1 - y if r & 2 else y
    pc = 1 - c if r & 1 else c
    return (px, py, pc)


def _prologue(c_row, cctx_row, w_mod_sh, b_sh, cw_sh, srcs, n_gather):
    ncol = w_mod_sh.shape[1]
    n = len(srcs)
    n_split = 4

    def body(c_ref, cctx_ref, w_ref, b_ref, cw_ref, *refs):
        ins, (c8_ref, m_ref), outs = refs[:n], refs[n:n + 2], refs[n + 2:2 * n + 2]
        mine_ref, msh_ref = refs[2 * n + 2:2 * n + 4]
        f32s, bfs = refs[2 * n + 4:3 * n + 4], refs[3 * n + 4:4 * n + 4]
        ssem, rsem, ssem2, rsem2, lsem_in, lsem_out = refs[4 * n + 4:4 * n + 10]
        x, y, c = _pos()
        me = 4 * x + 2 * y + c
        j = 2 * x + y

        def pieces(rows):
            step = rows // n_split
            return [pl.ds(q * step, step) for q in range(n_split)]

        for t in range(n):
            for sl in pieces(ins[t].shape[0]):
                pltpu.make_async_copy(ins[t].at[sl], f32s[t].at[sl], lsem_in.at[t]).start()
        mine_ref[...] = jnp.zeros(mine_ref.shape, F32)
        mine_ref[0:1, :] = c_ref[...]
        my_rows = pl.ds(pl.multiple_of(8 * me, 8), 8)
        sends = []
        for r in range(1, 8):
            cp = pltpu.make_async_remote_copy(
                src_ref=mine_ref, dst_ref=c8_ref.at[my_rows], send_sem=ssem.at[r - 1], recv_sem=rsem.at[r - 1],
                device_id=_peer(r, x, y, c), device_id_type=MESH)
            cp.start()
            sends.append(cp)

        def cast_and_store(t):
            pltpu.make_async_copy(ins[t], f32s[t], lsem_in.at[t]).wait()
            bfs[t][...] = f32s[t][...].astype(BF16)
            for sl in pieces(ins[t].shape[0]):
                pltpu.make_async_copy(bfs[t].at[sl], outs[t].at[j, sl], lsem_out.at[t]).start()

        gather = _ShardGather(outs[:n_gather], *refs[4 * n + 10:])
        for t in range(n_gather):
            cast_and_store(t)
        for t in range(n_gather):
            pltpu.make_async_copy(bfs[t], outs[t].at[j], lsem_out.at[t]).wait()
        gather.start()
        for cp in sends:
            cp.wait()
        c8_ref[my_rows, :] = mine_ref[...]
        c8_ref[64:72, :] = jnp.zeros((8, D_MODEL), F32)
        c8_ref[64:65, :] = cctx_ref[...]
        cv = c8_ref[...]
        s = cv * _sigmoid(cv)
        m = jnp.dot(s, w_ref[...], precision=HIGHEST, preferred_element_type=F32) + b_ref[...]
        msh_ref[0:64, :] = m[0:64, :]
        msh_ref[64:72, :] = jnp.zeros((8, ncol), F32)
        msh_ref[64:65, :] = m[64:65, :]
        msh_ref[65:68, 0:128] = cw_ref[0:3, :]
        m_ref[j, 0:8, :] = msh_ref[my_rows, :]
        m_ref[j, 8:16, :] = msh_ref[64:72, :]
        sends2 = []
        for k, (px, py) in enumerate(_chips(x, y)):
            theirs = pl.ds(pl.multiple_of(8 * (4 * px + 2 * py + c), 8), 8)
            for half, src in enumerate((msh_ref.at[theirs], msh_ref.at[64:72])):
                cp = pltpu.make_async_remote_copy(
                    src_ref=src, dst_ref=m_ref.at[j, 8 * half:8 * half + 8], send_sem=ssem2.at[2 * k + half],
                    recv_sem=rsem2.at[2 * k + half], device_id=(px, py, c), device_id_type=MESH)
                cp.start()
                sends2.append(cp)
        for t in range(n_gather, n):
            cast_and_store(t)
        gather.forward()
        gather.finish()
        for t in range(n_gather, n):
            pltpu.make_async_copy(bfs[t], outs[t].at[j], lsem_out.at[t]).wait()
        for cp in sends2:
            cp.wait()

    vm = pl.BlockSpec(memory_space=pltpu.VMEM)
    hbm = pl.BlockSpec(memory_space=pl.ANY)
    return pl.pallas_call(
        body, name="prologue",
        out_shape=(jax.ShapeDtypeStruct((72, D_MODEL), F32), jax.ShapeDtypeStruct((N_SHARD, 16, ncol), F32))
        + tuple(jax.ShapeDtypeStruct((N_SHARD,) + a.shape, BF16) for a in srcs),
        in_specs=[vm] * 5 + [hbm] * n, out_specs=(vm, vm) + (hbm,) * n,
        scratch_shapes=[pltpu.VMEM((8, D_MODEL), F32), pltpu.VMEM((72, ncol), F32)]
        + [pltpu.VMEM(a.shape, F32) for a in srcs] + [pltpu.VMEM(a.shape, BF16) for a in srcs]
        + [pltpu.SemaphoreType.DMA((7,)), pltpu.SemaphoreType.DMA((7,)),
           pltpu.SemaphoreType.DMA((6,)), pltpu.SemaphoreType.DMA((6,)),
           pltpu.SemaphoreType.DMA((n,)), pltpu.SemaphoreType.DMA((n,))] + _gather_sems(n_gather),
        compiler_params=pltpu.CompilerParams(vmem_limit_bytes=VMEM_LIMIT),
    )(c_row, cctx_row, w_mod_sh, b_sh, cw_sh, *srcs)


def _chips(x, y):
    return [(1 - x, y), (x, 1 - y), (1 - x, 1 - y)]


def _halves(ref, c, align):
    hr = ref.shape[-2] // 2
    return (pl.ds(pl.multiple_of(c * hr, align), hr), pl.ds(pl.multiple_of((1 - c) * hr, align), hr))


class _ShardGather:
    def __init__(self, refs, ssem, rsem, fsend, frecv):
        self.refs, self.sems = refs, (ssem, rsem, fsend, frecv)
        self.x, self.y, self.c = _pos()
        self.j = 2 * self.x + self.y

    def _ici(self, a, k, slot):
        g = self.refs[a]
        ssem, rsem, _, _ = self.sems
        mine, _ = _halves(g, self.c, 16)
        px, py = _chips(self.x, self.y)[k]
        return pltpu.make_async_remote_copy(
            src_ref=g.at[self.j, mine], dst_ref=g.at[slot, mine], send_sem=ssem.at[3 * a + k],
            recv_sem=rsem.at[3 * a + k], device_id=(px, py, self.c), device_id_type=MESH)

    def _d2d(self, a, k, to_other_half):
        g = self.refs[a]
        _, _, fsend, frecv = self.sems
        mine, theirs = _halves(g, self.c, 16)
        px, py = _chips(self.x, self.y)[k]
        jk = 2 * px + py
        return pltpu.make_async_remote_copy(
            src_ref=g.at[jk, mine], dst_ref=g.at[jk, theirs if to_other_half else mine],
            send_sem=fsend.at[3 * a + k], recv_sem=frecv.at[3 * a + k],
            device_id=(self.x, self.y, 1 - self.c), device_id_type=MESH)

    def start(self):
        for a in range(len(self.refs)):
            for k in range(3):
                self._ici(a, k, self.j).start()

    def forward(self):
        for a in range(len(self.refs)):
            for k, (px, py) in enumerate(_chips(self.x, self.y)):
                self._ici(a, k, 2 * px + py).wait_recv()
                self._d2d(a, k, False).start()

    def finish(self):
        for a in range(len(self.refs)):
            for k in range(3):
                self._d2d(a, k, True).wait()
                self._ici(a, k, self.j).wait_send()


def _gather_sems(n_arrays):
    return [pltpu.SemaphoreType.DMA((3 * n_arrays,)) for _ in range(4)]


class _SiblingSend:
    def __init__(self, g_refs, got_refs, ssem, rsem):
        self.g_refs, self.got_refs, self.ssem, self.rsem = g_refs, got_refs, ssem, rsem
        self.x, self.y, self.c = _pos()

    def _copy(self, a, shard):
        _, theirs = _halves(self.g_refs[a], self.c, 16)
        src = self.g_refs[a].at[:, theirs] if shard is None else self.g_refs[a].at[shard, theirs]
        dst = self.got_refs[a] if shard is None else self.got_refs[a].at[shard]
        return pltpu.make_async_remote_copy(
            src_ref=src, dst_ref=dst, send_sem=self.ssem.at[a], recv_sem=self.rsem.at[a],
            device_id=(self.x, self.y, 1 - self.c), device_id_type=MESH)

    def start(self):
        for a in range(len(self.g_refs)):
            for s in range(N_SHARD):
                self._copy(a, s).start()

    def finish(self):
        for a in range(len(self.g_refs)):
            self._copy(a, None).wait()


class _SiblingSwap:
    def __init__(self, h_refs, t_refs, ssem, rsem):
        self.h_refs, self.t_refs, self.ssem, self.rsem = h_refs, t_refs, ssem, rsem
        self.x, self.y, self.c = _pos()

    def _copy(self, a):
        return pltpu.make_async_remote_copy(
            src_ref=self.h_refs[a], dst_ref=self.t_refs[a], send_sem=self.ssem.at[a], recv_sem=self.rsem.at[a],
            device_id=(self.x, self.y, 1 - self.c), device_id_type=MESH)

    def start(self):
        for a in range(len(self.h_refs)):
            self._copy(a).start()

    def finish(self):
        for a in range(len(self.h_refs)):
            self._copy(a).wait()


def _rs_sibling(arrs, name):
    n = len(arrs)

    def body(*refs):
        send = _SiblingSend(refs[:n], refs[n:2 * n], refs[2 * n], refs[2 * n + 1])
        send.start()
        send.finish()

    hbm = pl.BlockSpec(memory_space=pl.ANY)
    return pl.pallas_call(
        body, name=name,
        out_shape=tuple(jax.ShapeDtypeStruct((N_SHARD, a.shape[1] // 2, a.shape[2]), BF16) for a in arrs),
        in_specs=[hbm] * n, out_specs=(hbm,) * n,
        scratch_shapes=[pltpu.SemaphoreType.DMA((n,)), pltpu.SemaphoreType.DMA((n,))],
    )(*arrs)


class _ChipScatter:
    def __init__(self, parts, gots, ssem, rsem):
        self.parts, self.gots, self.ssem, self.rsem = parts, gots, ssem, rsem
        self.x, self.y, self.c = _pos()

    def _copy(self, a, k):
        px, py = _chips(self.x, self.y)[k]
        return pltpu.make_async_remote_copy(
            src_ref=self.parts[a].at[2 * px + py], dst_ref=self.gots[a].at[k], send_sem=self.ssem.at[3 * a + k],
            recv_sem=self.rsem.at[3 * a + k], device_id=(px, py, self.c), device_id_type=MESH)

    def start(self):
        for a in range(len(self.parts)):
            for k in range(3):
                self._copy(a, k).start()

    def finish(self):
        for a in range(len(self.parts)):
            for k in range(3):
                self._copy(a, k).wait()


def _rs_chips(parts, name):
    n = len(parts)

    def body(*refs):
        scatter = _ChipScatter(refs[:n], refs[n:2 * n], refs[2 * n], refs[2 * n + 1])
        scatter.start()
        scatter.finish()

    hbm = pl.BlockSpec(memory_space=pl.ANY)
    return pl.pallas_call(
        body, name=name, out_shape=tuple(jax.ShapeDtypeStruct((3,) + p.shape[1:], BF16) for p in parts),
        in_specs=[hbm] * n, out_specs=(hbm,) * n,
        scratch_shapes=[pltpu.SemaphoreType.DMA((3 * n,)), pltpu.SemaphoreType.DMA((3 * n,))],
    )(*parts)


def _rs_join(halves, name):
    n = len(halves)

    def body(*refs):
        h_refs, f_refs, stages = refs[:n], refs[n:2 * n], refs[2 * n:3 * n]
        lsem_in, lsem_out, ssem, rsem = refs[3 * n:]
        x, y, c = _pos()
        remote = []
        for a in range(n):
            mine, _ = _halves(f_refs[a], c, 8)
            cp = pltpu.make_async_remote_copy(
                src_ref=h_refs[a], dst_ref=f_refs[a].at[mine], send_sem=ssem.at[a], recv_sem=rsem.at[a],
                device_id=(x, y, 1 - c), device_id_type=MESH)
            cp.start()
            remote.append(cp)
            pltpu.make_async_copy(h_refs[a], stages[a], lsem_in.at[a]).start()
        local = []
        for a in range(n):
            mine, _ = _halves(f_refs[a], c, 8)
            pltpu.make_async_copy(h_refs[a], stages[a], lsem_in.at[a]).wait()
            cp = pltpu.make_async_copy(stages[a], f_refs[a].at[mine], lsem_out.at[a])
            cp.start()
            local.append(cp)
        for cp in remote + local:
            cp.wait()

    hbm = pl.BlockSpec(memory_space=pl.ANY)
    return pl.pallas_call(
        body, name=name,
        out_shape=tuple(jax.ShapeDtypeStruct((2 * h.shape[0], h.shape[1]), F32) for h in halves),
        in_specs=[hbm] * n, out_specs=(hbm,) * n,
        scratch_shapes=[pltpu.VMEM(h.shape, F32) for h in halves]
        + [pltpu.SemaphoreType.DMA((n,)) for _ in range(4)],
        compiler_params=pltpu.CompilerParams(vmem_limit_bytes=STAGE_VMEM_LIMIT),
    )(*halves)


def _row_block(rows):
    return max(b for b in range(16, 513, 16) if rows % b == 0)


def _add_pairs(arr, got, c_idx, name):
    hr, cols = got.shape[1], got.shape[2]
    rb = _row_block(hr)
    nb = hr // rb

    def body(c_ref, a_ref, b_ref, o_ref):
        o_ref[...] = (a_ref[...].astype(F32) + b_ref[...].astype(F32)).astype(BF16)

    spec = pl.BlockSpec((1, rb, cols), lambda s, r, c_ref: (s, r, 0))
    grid_spec = pltpu.PrefetchScalarGridSpec(
        num_scalar_prefetch=1, grid=(N_SHARD, nb),
        in_specs=[pl.BlockSpec((1, rb, cols), lambda s, r, c_ref: (s, c_ref[0] * nb + r, 0)), spec],
        out_specs=spec)
    return pl.pallas_call(
        body, name=name, grid_spec=grid_spec, out_shape=jax.ShapeDtypeStruct(got.shape, BF16),
    )(c_idx, arr, got)


def _add_chips(part, got, j_idx, name):
    hr, cols = got.shape[1], got.shape[2]
    rb = _row_block(hr)

    def body(j_ref, p_ref, g_ref, o_ref):
        acc = p_ref[0].astype(F32)
        for k in range(3):
            acc = acc + g_ref[k].astype(F32)
        o_ref[...] = acc

    grid_spec = pltpu.PrefetchScalarGridSpec(
        num_scalar_prefetch=1, grid=(hr // rb,),
        in_specs=[pl.BlockSpec((1, rb, cols), lambda r, j_ref: (j_ref[0], r, 0)),
                  pl.BlockSpec((3, rb, cols), lambda r, j_ref: (0, r, 0))],
        out_specs=pl.BlockSpec((rb, cols), lambda r, j_ref: (r, 0)))
    return pl.pallas_call(
        body, name=name, grid_spec=grid_spec, out_shape=jax.ShapeDtypeStruct((hr, cols), F32),
    )(j_idx, part, got)


def _small_exchange(sv, w_mod_sh, cctx, m_cctx, v_cctx, bmod, m_bmod, v_bmod, qg, m_qg, v_qg, kvg, m_kvg, v_kvg,
                    gf, m_gf, v_gf, swap_halves):
    ncol = w_mod_sh.shape[1]
    n_s = len(swap_halves)

    def body(sv_ref, w_ref, cctx_ref, mcc_ref, vcc_ref, b_ref, mb_ref, vb_ref, qg_ref, mq_ref, vq_ref,
             kg_ref, mk_ref, vk_ref, gf_ref, mgf_ref, vgf_ref, *rest):
        all_ref, red_ref, occ_ref, ob_ref, oq_ref, ok_ref, ogf_ref = rest[n_s:n_s + 7]
        vec_ref, part_ref, ssem, rsem, ssem2, rsem2, wsend, wrecv = rest[2 * n_s + 7:]
        swap = _SiblingSwap(rest[:n_s], rest[n_s + 7:2 * n_s + 7], wsend, wrecv)
        swap.start()
        x, y, c = _pos()
        me = 4 * x + 2 * y + c
        j = 2 * x + y
        sends = []
        for r in range(1, 8):
            cp = pltpu.make_async_remote_copy(
                src_ref=sv_ref, dst_ref=all_ref.at[me], send_sem=ssem.at[r - 1], recv_sem=rsem.at[r - 1],
                device_id=_peer(r, x, y, c), device_id_type=MESH)
            cp.start()
            sends.append(cp)
        for cp in sends:
            cp.wait()
        all_ref[me] = sv_ref[...]
        red = all_ref[0]
        for d in range(1, 8):
            red = red + all_ref[d]
        red_ref[...] = red
        vec_ref[...] = jnp.zeros(vec_ref.shape, F32)

        @pl.when(j == 0)
        def _():
            vec_ref[0:1, 0:1024] = red[6:7, :]
            vec_ref[0:1, 1024:1536] = red[7:8, 0:512]

        @pl.when(j == 1)
        def _():
            vec_ref[0:1, 0:512] = red[7:8, 512:1024]

        part = lax.dot_general(vec_ref[...], w_ref[...], (((1,), (1,)), ((), ())), precision=HIGHEST,
                               preferred_element_type=F32)
        part_ref[j] = part
        sends2 = []
        for k, r in enumerate((4, 2, 6)):
            cp = pltpu.make_async_remote_copy(
                src_ref=part_ref.at[j], dst_ref=part_ref.at[j], send_sem=ssem2.at[k], recv_sem=rsem2.at[k],
                device_id=_peer(r, x, y, c), device_id_type=MESH)
            cp.start()
            sends2.append(cp)
        for cp in sends2:
            cp.wait()
        tot = part_ref[0]
        for s in range(1, N_SHARD):
            tot = tot + part_ref[s]
        cc = cctx_ref[...]
        sg = _sigmoid(cc)
        g_cc = tot[0:1, :] * (sg * (1.0 + cc * (1.0 - sg)))
        d_, m_, v_ = _adamw(cc, g_cc, mcc_ref[...], vcc_ref[...])
        occ_ref[0:1, :] = g_cc
        occ_ref[1:2, :] = d_
        occ_ref[2:3, :] = m_
        occ_ref[3:4, :] = v_
        occ_ref[4:8, :] = jnp.zeros((4, D_MODEL), F32)
        g_b = red[0:6, :]
        pad = jnp.concatenate([red[6:8, :], jnp.zeros((4, D_MODEL), F32)], axis=0)
        g_b = g_b + pad
        d_, m_, v_ = _adamw(b_ref[...], g_b, mb_ref[...], vb_ref[...])
        ob_ref[0] = g_b
        ob_ref[1] = d_
        ob_ref[2] = m_
        ob_ref[3] = v_
        g_q = red[9:10, 0:Q_RANK]
        d_, m_, v_ = _adamw(qg_ref[...], g_q, mq_ref[...], vq_ref[...])
        oq_ref[0:1, :] = g_q
        oq_ref[1:2, :] = d_
        oq_ref[2:3, :] = m_
        oq_ref[3:4, :] = v_
        oq_ref[4:8, :] = jnp.zeros((4, Q_RANK), F32)
        g_k = red[10:11, 0:KV_RANK]
        d_, m_, v_ = _adamw(kg_ref[...], g_k, mk_ref[...], vk_ref[...])
        ok_ref[0:1, :] = g_k
        ok_ref[1:2, :] = d_
        ok_ref[2:3, :] = m_
        ok_ref[3:4, :] = v_
        ok_ref[4:8, :] = jnp.zeros((4, KV_RANK), F32)
        g_f = red[8:9, :]
        d_, m_, v_ = _adamw(gf_ref[...], g_f, mgf_ref[...], vgf_ref[...])
        ogf_ref[0:1, :] = g_f
        ogf_ref[1:2, :] = d_
        ogf_ref[2:3, :] = m_
        ogf_ref[3:4, :] = v_
        ogf_ref[4:8, :] = jnp.zeros((4, D_MODEL), F32)
        swap.finish()

    vm = pl.BlockSpec(memory_space=pltpu.VMEM)
    hbm = pl.BlockSpec(memory_space=pl.ANY)
    out_shape = (
        jax.ShapeDtypeStruct((8, 16, D_MODEL), F32),
        jax.ShapeDtypeStruct((16, D_MODEL), F32),
        jax.ShapeDtypeStruct((8, D_MODEL), F32),
        jax.ShapeDtypeStruct((4, 6, D_MODEL), F32),
        jax.ShapeDtypeStruct((8, Q_RANK), F32),
        jax.ShapeDtypeStruct((8, KV_RANK), F32),
        jax.ShapeDtypeStruct((8, D_MODEL), F32),
    ) + tuple(jax.ShapeDtypeStruct(h.shape, F32) for h in swap_halves)
    return pl.pallas_call(
        body, name="small_exchange", out_shape=out_shape, in_specs=[vm] * 17 + [hbm] * n_s,
        out_specs=tuple([vm] * 7) + (hbm,) * n_s,
        scratch_shapes=[pltpu.VMEM((8, ncol), F32), pltpu.VMEM((N_SHARD, 8, D_MODEL), F32),
                        pltpu.SemaphoreType.DMA((7,)), pltpu.SemaphoreType.DMA((7,)),
                        pltpu.SemaphoreType.DMA((3,)), pltpu.SemaphoreType.DMA((3,)),
                        pltpu.SemaphoreType.DMA((n_s,)), pltpu.SemaphoreType.DMA((n_s,))],
        compiler_params=pltpu.CompilerParams(vmem_limit_bytes=VMEM_LIMIT),
    )(sv, w_mod_sh, cctx, m_cctx, v_cctx, bmod, m_bmod, v_bmod, qg, m_qg, v_qg, kvg, m_kvg, v_kvg, gf, m_gf, v_gf,
      *swap_halves)


def _inproj_fwd(x2, ctx2, mod_a, w_in, qg, kvg, w_uq, w_ukv, cos_t, sin_a, sin_b):
    t_lat, t_ctx = x2.shape[0], ctx2.shape[0]
    tm = TOK_TILE
    n_lat = t_lat // tm
    n_all = n_lat + t_ctx // tm
    e_rows = t_lat + t_ctx

    def body(x_ref, ctx_ref, mod_ref, win_ref, qg_ref, kvg_ref, wuq_ref, wukv_ref, cos_ref, sa_ref, sb_ref,
             z_ref, q_ref, k_ref, v_ref, kt_ref):
        i = pl.program_id(0)
        xin = jnp.where(i < n_lat, x_ref[...], ctx_ref[...])
        xn = xin * lax.rsqrt(jnp.mean(xin * xin, axis=-1, keepdims=True) + EPS)
        h1 = (xn * (1.0 + mod_ref[0, 1:2, :]) + mod_ref[0, 0:1, :]).astype(BF16)
        z = _dot(h1, win_ref[...])
        z_ref[...] = z
        cos, sa, sb = cos_ref[...], sa_ref[...], sb_ref[...]
        cq = z[:, 0:Q_RANK]
        cqn = (cq * lax.rsqrt(jnp.mean(cq * cq, axis=-1, keepdims=True) + EPS) * qg_ref[...]).astype(BF16)
        q = _dot(cqn, wuq_ref[...])
        ckv = z[:, Q_RANK:Q_RANK + KV_RANK]
        ckvn = (ckv * lax.rsqrt(jnp.mean(ckv * ckv, axis=-1, keepdims=True) + EPS) * kvg_ref[...]).astype(BF16)
        kv = _dot(ckvn, wukv_ref[...])
        kr = _rope(z[:, Q_RANK + KV_RANK:Q_RANK + KV_RANK + HEAD_PAD], cos, sa, sb)
        ones_lane = lax.broadcasted_iota(jnp.int32, (tm, HEAD_PAD), 1) == V_DIM
        for h in range(N_HEADS):
            lo = h * HEAD_PAD
            q_ref[h] = _rope(q[:, lo:lo + HEAD_PAD], cos, sa, sb).astype(BF16)
            kh = kv[:, lo:lo + HEAD_PAD] + kr
            k_ref[h] = kh.astype(BF16)
            kt_ref[h] = kh.T.astype(BF16)
            vh = kv[:, N_HEADS * HEAD_PAD + lo:N_HEADS * HEAD_PAD + lo + HEAD_PAD]
            v_ref[h] = jnp.where(ones_lane, 1.0, vh).astype(BF16)

    row = lambda i: (i, 0)
    head_spec = pl.BlockSpec((N_HEADS, tm, HEAD_PAD), lambda i: (0, i, 0))
    head_shape = jax.ShapeDtypeStruct((N_HEADS, e_rows, HEAD_PAD), BF16)
    return pl.pallas_call(
        body, name="inproj_fwd", grid=(n_all,),
        out_shape=(jax.ShapeDtypeStruct((e_rows, Z_COLS), F32), head_shape, head_shape, head_shape,
                   jax.ShapeDtypeStruct((N_HEADS, HEAD_PAD, e_rows), BF16)),
        in_specs=[
            pl.BlockSpec((tm, D_MODEL), lambda i: (jnp.minimum(i, n_lat - 1), 0)),
            _const_spec((tm, D_MODEL)),
            pl.BlockSpec((1, 8, D_MODEL), lambda i: (i // n_lat, 0, 0)),
            _const_spec(w_in.shape), _const_spec(qg.shape), _const_spec(kvg.shape),
            _const_spec(w_uq.shape), _const_spec(w_ukv.shape),
            pl.BlockSpec((tm, HEAD_PAD), row), pl.BlockSpec((tm, HEAD_PAD), row), pl.BlockSpec((tm, HEAD_PAD), row),
        ],
        out_specs=(pl.BlockSpec((tm, Z_COLS), row), head_spec, head_spec, head_spec,
                   pl.BlockSpec((N_HEADS, HEAD_PAD, tm), lambda i: (0, 0, i))),
        compiler_params=pltpu.CompilerParams(vmem_limit_bytes=VMEM_LIMIT),
    )(x2, ctx2, mod_a, w_in, qg, kvg, w_uq, w_ukv, cos_t, sin_a, sin_b)


def _key_chunks(e_rows, size):
    n_chunks = max(1, e_rows // size)
    return [(ci * size, size if ci < n_chunks - 1 else e_rows - ci * size) for ci in range(n_chunks)]


def _attn_fwd(q, k, v, t_lat, shard_arrays):
    e_rows = k.shape[1]
    tq = min(t_lat, ATTN_FWD_Q_BLOCK)
    bounds = _key_chunks(e_rows, KEY_CHUNK)
    c2 = ATTN_SCALE * LOG2E

    hb = ATTN_HEADS_PER_STEP
    n_hb = N_HEADS // hb

    def body(q_ref, k_ref, v_ref, o_ref, lse_ref):
        qs = [q_ref[b] for b in range(hb)]
        m, acc = [None] * hb, [None] * hb
        for lo, n in bounds:
            for b in range(hb):
                s = _dot_nt(qs[b], k_ref[b, lo:lo + n, :])
                mc = jnp.max(s, axis=-1, keepdims=True)
                m_new = mc if m[b] is None else jnp.maximum(m[b], mc)
                p = jnp.exp2((s - m_new) * c2)
                pv = _dot(p.astype(BF16), v_ref[b, lo:lo + n, :])
                acc[b] = pv if m[b] is None else acc[b] * jnp.exp2((m[b] - m_new) * c2) + pv
                m[b] = m_new
        for b in range(hb):
            l = acc[b][:, V_DIM:V_DIM + 1]
            o_ref[b] = (acc[b] * (1.0 / l)).astype(BF16)
            lse = (m[b] * ATTN_SCALE + jnp.log(l)) * LOG2E
            lse_ref[b] = jnp.broadcast_to(lse, (tq, HEAD_PAD)).T[0:1, :]

    n_w = len(shard_arrays)
    n_q = t_lat // tq

    def body_with_gather(q_ref, k_ref, v_ref, *rest):
        o_ref, lse_ref = rest[n_w], rest[n_w + 1]
        gather = _ShardGather(rest[n_w + 2:2 * n_w + 2], *rest[2 * n_w + 2:])
        h, i = pl.program_id(0), pl.program_id(1)
        pl.when(jnp.logical_and(h == 0, i == 0))(gather.start)
        pl.when(jnp.logical_and(h == n_hb // 2, i == 0))(gather.forward)
        body(q_ref, k_ref, v_ref, o_ref, lse_ref)
        pl.when(jnp.logical_and(h == n_hb - 1, i == n_q - 1))(gather.finish)

    hbm = pl.BlockSpec(memory_space=pl.ANY)
    return pl.pallas_call(
        body_with_gather, name="attn_fwd", grid=(n_hb, n_q),
        out_shape=(jax.ShapeDtypeStruct((N_HEADS, t_lat, HEAD_PAD), BF16),
                   jax.ShapeDtypeStruct((N_HEADS, 1, t_lat), F32))
        + tuple(jax.ShapeDtypeStruct(a.shape, a.dtype) for a in shard_arrays),
        in_specs=[pl.BlockSpec((hb, tq, HEAD_PAD), lambda h, i: (h, i, 0)),
                  pl.BlockSpec((hb, e_rows, HEAD_PAD), lambda h, i: (h, 0, 0)),
                  pl.BlockSpec((hb, e_rows, HEAD_PAD), lambda h, i: (h, 0, 0))] + [hbm] * n_w,
        out_specs=(pl.BlockSpec((hb, tq, HEAD_PAD), lambda h, i: (h, i, 0)),
                   pl.BlockSpec((hb, 1, tq), lambda h, i: (h, 0, i))) + (hbm,) * n_w,
        input_output_aliases={3 + a: 2 + a for a in range(n_w)},
        scratch_shapes=_gather_sems(n_w),
        compiler_params=pltpu.CompilerParams(vmem_limit_bytes=VMEM_LIMIT),
    )(q, k, v, *shard_arrays)


def _attn_bwd(q, k, v, kt, o, do, lse_row, t_lat, parts):
    e_rows = k.shape[1]
    tq = min(t_lat, ATTN_BWD_Q_BLOCK)
    n_p = len(parts)
    n_q = t_lat // tq
    bounds = _key_chunks(e_rows, KEY_CHUNK_BWD)

    hb = ATTN_BWD_HEADS_PER_STEP
    n_hb = N_HEADS // hb

    def body(q_ref, k_ref, v_ref, kt_ref, o_ref, do_ref, lse_ref, *rest):
        dqt_ref, dk_ref, dv_ref = rest[n_p:n_p + 3]
        scatter = _ChipScatter(rest[:n_p], rest[n_p + 3:2 * n_p + 3], rest[2 * n_p + 3], rest[2 * n_p + 4])
        h, i = pl.program_id(0), pl.program_id(1)
        pl.when(jnp.logical_and(h == 0, i == 0))(scatter.start)

        @pl.when(i == 0)
        def _():
            dk_ref[...] = jnp.zeros(dk_ref.shape, F32)
            dv_ref[...] = jnp.zeros(dv_ref.shape, F32)

        qs, dos, lses, deltas = [], [], [], []
        for b in range(hb):
            qs.append(q_ref[b])
            dos.append(do_ref[b])
            lses.append(lse_ref[b])
            prod = o_ref[b].astype(F32) * dos[b].astype(F32)
            deltas.append(lax.dot_general(jnp.ones((8, HEAD_PAD), F32), prod, (((1,), (1,)), ((), ())),
                                          precision=HIGHEST, preferred_element_type=F32)[0:1, :])
        dqt = [None] * hb
        for lo, n in bounds:
            for b in range(hb):
                pt = jnp.exp2(_dot_nt(k_ref[b, lo:lo + n, :], qs[b]) * (ATTN_SCALE * LOG2E) - lses[b])
                dpt = _dot_nt(v_ref[b, lo:lo + n, :], dos[b])
                dst = (pt * (dpt - deltas[b])).astype(BF16)
                dv_c = _dot(pt.astype(BF16), dos[b])
                dk_c = _dot(dst, qs[b])
                part = _dot(kt_ref[b, :, lo:lo + n], dst)
                dqt[b] = part if dqt[b] is None else dqt[b] + part
                dk_ref[b, lo:lo + n, :] += dk_c * ATTN_SCALE
                dv_ref[b, lo:lo + n, :] += dv_c
        for b in range(hb):
            dqt_ref[b] = dqt[b] * ATTN_SCALE

        pl.when(jnp.logical_and(h == n_hb - 1, i == n_q - 1))(scatter.finish)

    hbm = pl.BlockSpec(memory_space=pl.ANY)
    qspec = pl.BlockSpec((hb, tq, HEAD_PAD), lambda h, i: (h, i, 0))
    kspec = pl.BlockSpec((hb, e_rows, HEAD_PAD), lambda h, i: (h, 0, 0))
    return pl.pallas_call(
        body, name="attn_bwd", grid=(n_hb, n_q),
        out_shape=(jax.ShapeDtypeStruct((N_HEADS, HEAD_PAD, t_lat), F32),
                   jax.ShapeDtypeStruct((N_HEADS, e_rows, HEAD_PAD), F32),
                   jax.ShapeDtypeStruct((N_HEADS, e_rows, HEAD_PAD), F32))
        + tuple(jax.ShapeDtypeStruct((3,) + p.shape[1:], BF16) for p in parts),
        in_specs=[qspec, kspec, kspec, pl.BlockSpec((hb, HEAD_PAD, e_rows), lambda h, i: (h, 0, 0)), qspec, qspec,
                  pl.BlockSpec((hb, 1, tq), lambda h, i: (h, 0, i))] + [hbm] * n_p,
        out_specs=(pl.BlockSpec((hb, HEAD_PAD, tq), lambda h, i: (h, 0, i)), kspec, kspec) + (hbm,) * n_p,
        scratch_shapes=[pltpu.SemaphoreType.DMA((3 * n_p,)), pltpu.SemaphoreType.DMA((3 * n_p,))],
        compiler_params=pltpu.CompilerParams(vmem_limit_bytes=VMEM_LIMIT),
    )(q, k, v, kt, o, do, lse_row, *parts)


def _halo_specs(tm, col_block):
    per = tm // 8
    prev = pl.BlockSpec((8, CONV_W), lambda i: (jnp.maximum(i * per - 1, 0), col_block))
    nxt = pl.BlockSpec((8, CONV_W), lambda i: ((i + 1) * per, col_block))
    return prev, nxt


def _mlp_fwdbwd(o, z, x2, tgt, mod_b, gf, cw, wo_attn, wo_conv, w1, w2):
    t_lat = x2.shape[0]
    tm = TOK_TILE
    n_lat = t_lat // tm
    fc = MLP_FF_CHUNK
    n_ff = D_FF // fc

    def body(o_ref, gb_ref, gc_ref, xi_ref, gcp_ref, xip_ref, gcn_ref, xin_ref, cw_ref, woa_ref, woc_ref,
             x_ref, t_ref, mod_ref, gf_ref, w1_ref, w2_ref,
             r_ref, da_ref, h2_ref, dy2_ref, dx1_ref, conv_ref, acc_ref, dy1_ref, do_ref, dgb_ref, dyv_ref, ra_ref):
        i = pl.program_id(0)

        @pl.when(i == 0)
        def _():
            acc_ref[...] = jnp.zeros(acc_ref.shape, F32)

        g1, sh2, sc2, g2 = mod_ref[0:1, :], mod_ref[1:2, :], mod_ref[2:3, :], mod_ref[3:4, :]
        u = gc_ref[...] * xi_ref[...]
        u_prev = jnp.where(i > 0, gcp_ref[7:8, :] * xip_ref[7:8, :], 0.0)
        u_next = jnp.where(i < n_lat - 1, gcn_ref[0:1, :] * xin_ref[0:1, :], 0.0)
        um1, up1 = _shift_rows(u, u_prev, u_next)
        yv = cw_ref[0:1, :] * um1 + cw_ref[1:2, :] * u + cw_ref[2:3, :] * up1
        gb = gb_ref[...]
        conv = (gb * yv).astype(BF16)
        conv_ref[...] = conv
        y1 = _dot(conv, woc_ref[...])
        for h in range(N_HEADS):
            y1 = y1 + _dot(o_ref[h], woa_ref[h])
        x1 = x_ref[...] + g1 * y1
        rstd2 = lax.rsqrt(jnp.mean(x1 * x1, axis=-1, keepdims=True) + EPS)
        xn1 = x1 * rstd2
        h2 = (xn1 * (1.0 + sc2) + sh2).astype(BF16)
        h2_ref[...] = h2
        y2 = jnp.zeros((tm, D_MODEL), F32)
        for jj in range(n_ff):
            lo = jj * fc
            ra = jnp.maximum(_dot(h2, w1_ref[lo // FF_CHUNK, :, lo % FF_CHUNK:lo % FF_CHUNK + fc]), 0.0)
            ra_ref[jj] = ra
            r = (ra * ra).astype(BF16)
            r_ref[:, lo:lo + fc] = r
            y2 = y2 + _dot(r, w2_ref[lo:lo + fc, :])
        x2v = x1 + g2 * y2
        rstd3 = lax.rsqrt(jnp.mean(x2v * x2v, axis=-1, keepdims=True) + EPS)
        xn3 = x2v * rstd3
        gfv = gf_ref[...]
        diff = xn3 * gfv - t_ref[...]
        loss_t = 0.5 * jnp.sum(jnp.sum(diff * diff, axis=-1, keepdims=True), axis=0, keepdims=True) * (1.0 / D_MODEL)
        dy = diff * (1.0 / D_MODEL)
        dxn3 = dy * gfv
        dx2 = rstd3 * (dxn3 - xn3 * jnp.mean(dxn3 * xn3, axis=-1, keepdims=True))
        dy2 = (dx2 * g2).astype(BF16)
        dy2_ref[...] = dy2
        dh2 = jnp.zeros((tm, D_MODEL), F32)
        for jj in range(n_ff):
            lo = jj * fc
            dr = _dot_nt(dy2, w2_ref[lo:lo + fc, :])
            da = (2.0 * ra_ref[jj] * dr).astype(BF16)
            da_ref[:, lo:lo + fc] = da
            dh2 = dh2 + _dot_nt(da, w1_ref[lo // FF_CHUNK, :, lo % FF_CHUNK:lo % FF_CHUNK + fc])
        dxn1 = dh2 * (1.0 + sc2)
        dx1 = dx2 + rstd2 * (dxn1 - xn1 * jnp.mean(dxn1 * xn1, axis=-1, keepdims=True))
        dx1_ref[...] = dx1
        dy1 = (dx1 * g1).astype(BF16)
        dy1_ref[...] = dy1
        for h in range(N_HEADS):
            do_ref[h] = _dot_nt(dy1, woa_ref[h]).astype(BF16)
        dconv = _dot_nt(dy1, woc_ref[...])
        dgb_ref[...] = dconv * yv
        dyv_ref[...] = dconv * gb
        acc_ref[5:6, :] += jnp.sum(dx1 * y1, axis=0, keepdims=True)
        acc_ref[0:1, :] += jnp.sum(dy * xn3, axis=0, keepdims=True)
        acc_ref[1:2, :] += jnp.sum(dx2 * y2, axis=0, keepdims=True)
        acc_ref[2:3, :] += jnp.sum(dh2, axis=0, keepdims=True)
        acc_ref[3:4, :] += jnp.sum(dh2 * xn1, axis=0, keepdims=True)
        acc_ref[4:5, :] += jnp.broadcast_to(loss_t, (1, D_MODEL))

    row = lambda i: (i, 0)
    gcp, gcn = _halo_specs(tm, 2)
    xip, xin = _halo_specs(tm, 3)
    tile = pl.BlockSpec((tm, D_MODEL), row)
    wide = pl.BlockSpec((tm, D_FF), row)
    half = pl.BlockSpec((tm, CONV_W), row)
    return pl.pallas_call(
        body, name="mlp_fwdbwd", grid=(n_lat,),
        out_shape=(jax.ShapeDtypeStruct((t_lat, D_FF), BF16), jax.ShapeDtypeStruct((t_lat, D_FF), BF16),
                   jax.ShapeDtypeStruct((t_lat, D_MODEL), BF16), jax.ShapeDtypeStruct((t_lat, D_MODEL), BF16),
                   jax.ShapeDtypeStruct((t_lat, D_MODEL), F32), jax.ShapeDtypeStruct((t_lat, CONV_W), BF16),
                   jax.ShapeDtypeStruct((8, D_MODEL), F32),
                   jax.ShapeDtypeStruct((t_lat, D_MODEL), BF16),
                   jax.ShapeDtypeStruct((N_HEADS, t_lat, HEAD_PAD), BF16),
                   jax.ShapeDtypeStruct((t_lat, CONV_W), F32), jax.ShapeDtypeStruct((t_lat, CONV_W), F32)),
        in_specs=[
            pl.BlockSpec((N_HEADS, tm, HEAD_PAD), lambda i: (0, i, 0)),
            pl.BlockSpec((tm, CONV_W), lambda i: (i, 1)), pl.BlockSpec((tm, CONV_W), lambda i: (i, 2)),
            pl.BlockSpec((tm, CONV_W), lambda i: (i, 3)),
            gcp, xip, gcn, xin,
            _const_spec(cw.shape), _resident_spec(wo_attn.shape), _resident_spec(wo_conv.shape),
            tile, tile, _const_spec(mod_b.shape), _const_spec(gf.shape),
            _resident_spec(w1.shape), _resident_spec(w2.shape),
        ],
        out_specs=(wide, wide, tile, tile, tile, half, _const_spec((8, D_MODEL)),
                   tile, pl.BlockSpec((N_HEADS, tm, HEAD_PAD), lambda i: (0, i, 0)), half, half),
        scratch_shapes=[pltpu.VMEM((n_ff, tm, fc), F32)],
        compiler_params=pltpu.CompilerParams(vmem_limit_bytes=VMEM_LIMIT),
    )(o, z, z, z, z, z, z, z, cw, wo_attn, wo_conv, x2, tgt, mod_b, gf, w1, w2)


def _inproj_bwd(x2, ctx2, mod_a, z, dyv, dgb, dx1, dqt, dk, dv, cos_t, sin_a, sin_b, w_in, w_uq, w_ukv, qg, kvg, cw):
    t_lat, t_ctx = x2.shape[0], ctx2.shape[0]
    tm = TOK_TILE
    n_lat = t_lat // tm
    n_all = n_lat + t_ctx // tm
    group = max(g for g in (1, 2, 4) if n_lat % g == 0)

    def body(x_ref, ctx_ref, mod_ref, z_ref, gcp_ref, xip_ref, gcn_ref, xin_ref, dyv_ref, dyvp_ref, dyvn_ref,
             dgb_ref, dx1_ref, dqt_ref, dk_ref, dv_ref, cos_ref, sa_ref, sb_ref, win_ref, wuq_ref, wukv_ref,
             qg_ref, kvg_ref, cw_ref, gx_ref, dwin_ref, dwuq_ref, dwukv_ref, acc_ref, h1_buf, dz_buf):
        i = pl.program_id(0)
        lat = i < n_lat

        @pl.when(i == 0)
        def _():
            dwin_ref[...] = jnp.zeros(dwin_ref.shape, F32)
            dwuq_ref[...] = jnp.zeros(dwuq_ref.shape, F32)
            dwukv_ref[...] = jnp.zeros(dwukv_ref.shape, F32)
            acc_ref[...] = jnp.zeros(acc_ref.shape, F32)

        xin = jnp.where(lat, x_ref[...], ctx_ref[...])
        rstd = lax.rsqrt(jnp.mean(xin * xin, axis=-1, keepdims=True) + EPS)
        xn = xin * rstd
        sc = mod_ref[0, 1:2, :]
        h1 = (xn * (1.0 + sc) + mod_ref[0, 0:1, :]).astype(BF16)
        z = z_ref[...]
        cos, sa, sb = cos_ref[...], sa_ref[...], sb_ref[...]
        qgv, kvgv = qg_ref[...], kvg_ref[...]
        cq = z[:, 0:Q_RANK]
        cqh = cq * lax.rsqrt(jnp.mean(cq * cq, axis=-1, keepdims=True) + EPS)
        rq = lax.rsqrt(jnp.mean(cq * cq, axis=-1, keepdims=True) + EPS)
        cqn = (cqh * qgv).astype(BF16)
        parts = []
        for h in range(N_HEADS):
            g = jnp.where(lat, dqt_ref[h].T, 0.0)
            parts.append(_unrope(g, cos, sa, sb))
        dq = jnp.concatenate(parts, axis=1).astype(BF16)
        dcqn = _dot_nt(dq, wuq_ref[...])
        dwuq_ref[...] += _dot_tn(cqn, dq)
        acc_ref[4:5, 0:Q_RANK] += jnp.sum(dcqn * cqh, axis=0, keepdims=True)
        dxn = dcqn * qgv
        dcq = rq * (dxn - cqh * jnp.mean(dxn * cqh, axis=-1, keepdims=True))
        ckv = z[:, Q_RANK:Q_RANK + KV_RANK]
        rk = lax.rsqrt(jnp.mean(ckv * ckv, axis=-1, keepdims=True) + EPS)
        ckvh = ckv * rk
        ckvn = (ckvh * kvgv).astype(BF16)
        dks = [dk_ref[h] for h in range(N_HEADS)]
        dkr = dks[0]
        for h in range(1, N_HEADS):
            dkr = dkr + dks[h]
        dkv = jnp.concatenate(dks + [dv_ref[h] for h in range(N_HEADS)], axis=1).astype(BF16)
        dckvn = _dot_nt(dkv, wukv_ref[...])
        dwukv_ref[...] += _dot_tn(ckvn, dkv)
        acc_ref[5:6, 0:KV_RANK] += jnp.sum(dckvn * ckvh, axis=0, keepdims=True)
        dxn = dckvn * kvgv
        dckv = rk * (dxn - ckvh * jnp.mean(dxn * ckvh, axis=-1, keepdims=True))
        dkr = _unrope(dkr, cos, sa, sb)
        gb, gc, xi = z[:, 512:1024], z[:, 1024:1536], z[:, 1536:2048]
        u = gc * xi
        u_prev = jnp.where(i > 0, gcp_ref[7:8, :] * xip_ref[7:8, :], 0.0)
        u_next = jnp.where(i < n_lat - 1, gcn_ref[0:1, :] * xin_ref[0:1, :], 0.0)
        um1, up1 = _shift_rows(u, u_prev, u_next)
        dyv = jnp.where(lat, dyv_ref[...], 0.0)
        dyv_prev = jnp.where(jnp.logical_and(i > 0, lat), dyvp_ref[7:8, :], 0.0)
        dyv_next = jnp.where(i < n_lat - 1, dyvn_ref[0:1, :], 0.0)
        dyv_m1, dyv_p1 = _shift_rows(dyv, dyv_prev, dyv_next)
        du = cw_ref[0:1, :] * dyv_p1 + cw_ref[1:2, :] * dyv + cw_ref[2:3, :] * dyv_m1
        dgc = du * xi
        dxi = du * gc
        dgb = jnp.where(lat, dgb_ref[...], 0.0)
        acc_ref[6:7, 0:CONV_W] += jnp.sum(dyv * um1, axis=0, keepdims=True)
        acc_ref[7:8, 0:CONV_W] += jnp.sum(dyv * u, axis=0, keepdims=True)
        acc_ref[8:9, 0:CONV_W] += jnp.sum(dyv * up1, axis=0, keepdims=True)
        dz = jnp.concatenate([dcq, dckv, dkr, dgb, dgc, dxi], axis=1).astype(BF16)
        dh1 = _dot_nt(dz, win_ref[...])
        slot = i % group
        rows_g = pl.ds(pl.multiple_of(slot * tm, tm), tm)
        h1_buf[rows_g, :] = h1
        dz_buf[rows_g, :] = dz

        @pl.when(jnp.logical_and(lat, slot == group - 1))
        def _():
            dwin_ref[...] += _dot_tn(h1_buf[...], dz_buf[...])

        @pl.when(jnp.logical_not(lat))
        def _():
            dwin_ref[...] += _dot_tn(h1, dz)
        s_sh = jnp.sum(dh1, axis=0, keepdims=True)
        s_sc = jnp.sum(dh1 * xn, axis=0, keepdims=True)
        zero = jnp.zeros_like(s_sh)
        acc_ref[0:1, :] += jnp.where(lat, s_sh, zero)
        acc_ref[1:2, :] += jnp.where(lat, s_sc, zero)
        acc_ref[2:3, :] += jnp.where(lat, zero, s_sh)
        acc_ref[3:4, :] += jnp.where(lat, zero, s_sc)
        dxn = dh1 * (1.0 + sc)
        dx = rstd * (dxn - xn * jnp.mean(dxn * xn, axis=-1, keepdims=True))

        @pl.when(lat)
        def _():
            gx_ref[...] = dx1_ref[...] + dx

    last = n_lat - 1
    per = tm // 8
    lat_row = lambda i: (jnp.minimum(i, last), 0)
    row = lambda i: (i, 0)
    gcp, gcn = _halo_specs(tm, 2)
    xip, xin = _halo_specs(tm, 3)
    n_halo = t_lat // 8
    dyvp = pl.BlockSpec((8, CONV_W), lambda i: (jnp.clip(i * per - 1, 0, n_halo - 1), 0))
    dyvn = pl.BlockSpec((8, CONV_W), lambda i: (jnp.minimum((i + 1) * per, n_halo - 1), 0))
    gcn = pl.BlockSpec((8, CONV_W), lambda i: (jnp.minimum((i + 1) * per, (t_lat + t_ctx) // 8 - 1), 2))
    xin = pl.BlockSpec((8, CONV_W), lambda i: (jnp.minimum((i + 1) * per, (t_lat + t_ctx) // 8 - 1), 3))
    head_f32 = pl.BlockSpec((N_HEADS, tm, HEAD_PAD), lambda i: (0, i, 0))
    tab = pl.BlockSpec((tm, HEAD_PAD), row)
    return pl.pallas_call(
        body, name="inproj_bwd", grid=(n_all,),
        out_shape=(jax.ShapeDtypeStruct((t_lat, D_MODEL), F32), jax.ShapeDtypeStruct(w_in.shape, F32),
                   jax.ShapeDtypeStruct(w_uq.shape, F32), jax.ShapeDtypeStruct(w_ukv.shape, F32),
                   jax.ShapeDtypeStruct((16, D_MODEL), F32)),
        in_specs=[
            pl.BlockSpec((tm, D_MODEL), lat_row), _const_spec((tm, D_MODEL)),
            pl.BlockSpec((1, 8, D_MODEL), lambda i: (i // n_lat, 0, 0)),
            pl.BlockSpec((tm, Z_COLS), row), gcp, xip, gcn, xin,
            pl.BlockSpec((tm, CONV_W), lat_row), dyvp, dyvn,
            pl.BlockSpec((tm, CONV_W), lat_row), pl.BlockSpec((tm, D_MODEL), lat_row),
            pl.BlockSpec((N_HEADS, HEAD_PAD, tm), lambda i: (0, 0, jnp.minimum(i, last))),
            head_f32, head_f32, tab, tab, tab,
            _const_spec(w_in.shape), _const_spec(w_uq.shape), _const_spec(w_ukv.shape),
            _const_spec(qg.shape), _const_spec(kvg.shape), _const_spec(cw.shape),
        ],
        out_specs=(pl.BlockSpec((tm, D_MODEL), lat_row), _const_spec(w_in.shape), _const_spec(w_uq.shape),
                   _const_spec(w_ukv.shape), _const_spec((16, D_MODEL))),
        scratch_shapes=[pltpu.VMEM((group * tm, D_MODEL), BF16), pltpu.VMEM((group * tm, Z_COLS), BF16)],
        compiler_params=pltpu.CompilerParams(vmem_limit_bytes=VMEM_LIMIT),
    )(x2, ctx2, mod_a, z, z, z, z, z, dyv, dyv, dyv, dgb, dx1, dqt, dk, dv, cos_t, sin_a, sin_b, w_in, w_uq, w_ukv,
      qg, kvg, cw)


def _wgrad(a, b, name, bm, bn):
    t, m = a.shape
    n = b.shape[1]
    bk = min(t, 4096)
    nk = t // bk
    nj = n // bn

    def body(a_ref, b_ref, o_ref, acc_ref):
        k = pl.program_id(2)
        part = _dot_tn(a_ref[...], b_ref[...])

        @pl.when(k == 0)
        def _():
            acc_ref[...] = part

        @pl.when(k > 0)
        def _():
            acc_ref[...] += part

        @pl.when(k == nk - 1)
        def _():
            o_ref[...] = acc_ref[...].astype(BF16)

    return pl.pallas_call(
        body, name=name, grid=(m // bm, nj, nk), out_shape=jax.ShapeDtypeStruct((m // bm * nj, bm, bn), BF16),
        in_specs=[pl.BlockSpec((bk, bm), lambda i, j, k: (k, i)), pl.BlockSpec((bk, bn), lambda i, j, k: (k, j))],
        out_specs=pl.BlockSpec((None, bm, bn), lambda i, j, k: (i * nj + j, 0, 0)),
        scratch_shapes=[pltpu.VMEM((bm, bn), F32)],
        compiler_params=pltpu.CompilerParams(vmem_limit_bytes=VMEM_LIMIT),
    )(a, b)


def _wgrad_out(o, conv, dy1, sib_arrays):
    t = o.shape[1]
    bk = min(t, 2048)
    nk = t // bk
    rows = N_HEADS * HEAD_PAD + CONV_W
    n_s = len(sib_arrays)

    def body(o_ref, c_ref, d_ref, *rest):
        w_ref, got_w_ref, acc_ref = rest[n_s], rest[2 * n_s + 1], rest[2 * n_s + 2]
        send = _SiblingSend(rest[:n_s], rest[n_s + 1:2 * n_s + 1], rest[2 * n_s + 3], rest[2 * n_s + 4])
        wsend, wrecv = rest[2 * n_s + 5], rest[2 * n_s + 6]
        k = pl.program_id(0)
        pl.when(k == 0)(send.start)
        cat = jnp.concatenate([o_ref[h] for h in range(N_HEADS)] + [c_ref[...]], axis=1)
        part = _dot_tn(cat, d_ref[...])

        @pl.when(k == 0)
        def _():
            acc_ref[...] = part

        @pl.when(k > 0)
        def _():
            acc_ref[...] += part

        @pl.when(k == nk - 1)
        def _():
            for h in range(N_HEADS):
                w_ref[h * V_DIM:(h + 1) * V_DIM, :] = acc_ref[h * HEAD_PAD:h * HEAD_PAD + V_DIM, :].astype(BF16)
            w_ref[N_HEADS * V_DIM:, :] = acc_ref[N_HEADS * HEAD_PAD:, :].astype(BF16)
            x, y, c = _pos()
            own = []
            for s in range(N_SHARD):
                theirs = pl.ds(pl.multiple_of(s * shard_rows + (1 - c) * (shard_rows // 2), 16), shard_rows // 2)
                cp = pltpu.make_async_remote_copy(
                    src_ref=w_ref.at[theirs], dst_ref=got_w_ref.at[s], send_sem=wsend.at[s], recv_sem=wrecv.at[s],
                    device_id=(x, y, 1 - c), device_id_type=MESH)
                cp.start()
                own.append(cp)
            for cp in own:
                cp.wait()

        pl.when(k == nk - 1)(send.finish)

    hbm = pl.BlockSpec(memory_space=pl.ANY)
    shard_rows = D_MODEL // N_SHARD
    return pl.pallas_call(
        body, name="wgrad_out", grid=(nk,),
        out_shape=(jax.ShapeDtypeStruct((D_MODEL, D_MODEL), BF16),)
        + tuple(jax.ShapeDtypeStruct((N_SHARD, a.shape[1] // 2, a.shape[2]), BF16) for a in sib_arrays)
        + (jax.ShapeDtypeStruct((N_SHARD, shard_rows // 2, D_MODEL), BF16),),
        in_specs=[pl.BlockSpec((N_HEADS, bk, HEAD_PAD), lambda k: (0, k, 0)),
                  pl.BlockSpec((bk, CONV_W), lambda k: (k, 0)),
                  pl.BlockSpec((bk, D_MODEL), lambda k: (k, 0))] + [hbm] * n_s,
        out_specs=(_const_spec((D_MODEL, D_MODEL)),) + (hbm,) * (n_s + 1),
        scratch_shapes=[pltpu.VMEM((rows, D_MODEL), F32), pltpu.SemaphoreType.DMA((n_s,)),
                        pltpu.SemaphoreType.DMA((n_s,)), pltpu.SemaphoreType.DMA((N_SHARD,)),
                        pltpu.SemaphoreType.DMA((N_SHARD,))],
        compiler_params=pltpu.CompilerParams(vmem_limit_bytes=VMEM_LIMIT),
    )(o, conv, dy1, *sib_arrays)


def _adamw_call(w, g, m, v, name):
    rows, cols = w.shape
    rb = 256 if rows % 256 == 0 else rows

    def body(w_ref, g_ref, m_ref, v_ref, d_ref, nm_ref, nv_ref):
        d_, m_, v_ = _adamw(w_ref[...], g_ref[...], m_ref[...], v_ref[...])
        d_ref[...] = d_
        nm_ref[...] = m_
        nv_ref[...] = v_

    spec = pl.BlockSpec((rb, cols), lambda i: (i, 0))
    shp = jax.ShapeDtypeStruct((rows, cols), F32)
    return pl.pallas_call(
        body, name=name, grid=(rows // rb,), out_shape=(shp, shp, shp),
        in_specs=[spec] * 4, out_specs=(spec, spec, spec),
    )(w, g, m, v)


def _adamw_halves(w, g_mine, g_theirs, m, v, c_idx, name):
    rows, cols = w.shape
    half = rows // 2
    rb = min(256, half)
    nb = half // rb

    def body(c_ref, w_ref, gm_ref, gt_ref, m_ref, v_ref, g_ref, d_ref, nm_ref, nv_ref):
        mine = pl.program_id(0) // nb == c_ref[0]
        g = jnp.where(mine, gm_ref[...], gt_ref[...])
        d_, m_, v_ = _adamw(w_ref[...], g, m_ref[...], v_ref[...])
        g_ref[...] = g
        d_ref[...] = d_
        nm_ref[...] = m_
        nv_ref[...] = v_

    spec = pl.BlockSpec((rb, cols), lambda i, c_ref: (i, 0))
    hspec = pl.BlockSpec((rb, cols), lambda i, c_ref: (i % nb, 0))
    shp = jax.ShapeDtypeStruct((rows, cols), F32)
    grid_spec = pltpu.PrefetchScalarGridSpec(
        num_scalar_prefetch=1, grid=(rows // rb,), in_specs=[spec, hspec, hspec, spec, spec],
        out_specs=(spec, spec, spec, spec))
    return pl.pallas_call(
        body, name=name, grid_spec=grid_spec, out_shape=(shp, shp, shp, shp),
    )(c_idx, w, g_mine, g_theirs, m, v)


def _wmod_update(s_t, dm, w, m, v):
    rows, cols = w.shape
    cb = 512

    def body(s_ref, dm_ref, w_ref, m_ref, v_ref, g_ref, d_ref, nm_ref, nv_ref):
        g = jnp.dot(s_ref[...], dm_ref[...], precision=HIGHEST, preferred_element_type=F32)
        d_, m_, v_ = _adamw(w_ref[...], g, m_ref[...], v_ref[...])
        g_ref[...] = g
        d_ref[...] = d_
        nm_ref[...] = m_
        nv_ref[...] = v_

    spec = pl.BlockSpec((rows, cb), lambda i: (0, i))
    shp = jax.ShapeDtypeStruct((rows, cols), F32)
    return pl.pallas_call(
        body, name="wmod_update", grid=(cols // cb,), out_shape=(shp, shp, shp, shp),
        in_specs=[_const_spec(s_t.shape), pl.BlockSpec((16, cb), lambda i: (0, i)), spec, spec, spec],
        out_specs=(spec, spec, spec, spec),
        compiler_params=pltpu.CompilerParams(vmem_limit_bytes=VMEM_LIMIT),
    )(s_t, dm, w, m, v)


def _rope_tables(t_lat, t_ctx):
    t = jnp.arange(t_lat)
    pos = jnp.stack([(t // GRID_W).astype(F32), (t % GRID_W).astype(F32)], axis=1)
    half = QK_ROPE // 4
    freqs = ROPE_THETA ** (-jnp.arange(0, 2 * half, 2, dtype=F32) / (2 * half))
    ang = pos[:, :, None] * freqs[None, None, :]
    cos, sin = jnp.cos(ang), jnp.sin(ang)
    zero = jnp.zeros_like(sin)
    cos32 = jnp.concatenate([cos, cos], axis=2).reshape(t_lat, QK_ROPE)
    sa32 = jnp.concatenate([zero, sin], axis=2).reshape(t_lat, QK_ROPE)
    sb32 = jnp.concatenate([-sin, zero], axis=2).reshape(t_lat, QK_ROPE)

    def widen(tab, fill):
        left = jnp.full((t_lat, ROPE_LANE0), fill, F32)
        right = jnp.full((t_lat, HEAD_PAD - ROPE_LANE0 - QK_ROPE), fill, F32)
        lat = jnp.concatenate([left, tab, right], axis=1)
        return jnp.concatenate([lat, jnp.full((t_ctx, HEAD_PAD), fill, F32)], axis=0)

    return widen(cos32, 1.0), widen(sa32, 0.0), widen(sb32, 0.0)


def _cols_from_shards(s):
    return jnp.transpose(s, (1, 0, 2)).reshape(s.shape[1], -1)


def _cols_to_shards(w):
    k, n = w.shape
    return jnp.transpose(w.reshape(k, N_SHARD, n // N_SHARD), (1, 0, 2))


def kernel(x, c, ctx, c_ctx, w_mod, b_mod, w_in, q_norm_g, w_uq, kv_norm_g, w_ukv, conv_w, w_out, w_mlp1, w_mlp2, final_norm_g, loss_target, m_c_ctx, m_w_mod, m_b_mod, m_w_in, m_q_norm_g, m_w_uq, m_kv_norm_g, m_w_ukv, m_conv_w, m_w_out, m_w_mlp1, m_w_mlp2, m_final_norm_g, v_c_ctx, v_w_mod, v_b_mod, v_w_in, v_q_norm_g, v_w_uq, v_kv_norm_g, v_w_ukv, v_conv_w, v_w_out, v_w_mlp1, v_w_mlp2, v_final_norm_g):
    t_lat, t_ctx = x.shape[1], ctx.shape[1]
    assert t_ctx == TOK_TILE and t_lat % TOK_TILE == 0 and t_lat % GRID_W == 0
    mx, my, mc = _pos()
    me = 4 * mx + 2 * my + mc
    j = 2 * mx + my
    ncol = w_mod.shape[2]
    x2, ctx2, tgt = x[0], ctx[0], loss_target[0]
    cctx_row = c_ctx.reshape(1, D_MODEL)

    b_sh = lax.dynamic_slice(b_mod, (0, j * ncol), (1, ncol))
    cw_pad = jnp.zeros((8, 128), F32).at[0:3, :].set(conv_w[0])
    c8, m_all, g_in, g_uq, g_ukv, g_out, g_m1, g_m2 = _prologue(
        c, cctx_row, w_mod[0], b_sh, cw_pad, (w_in[0], w_uq[0], w_ukv[0], w_out[0], w_mlp1[0], w_mlp2[0]), 3)
    mvec = m_all[:, 0, :].reshape(6, D_MODEL)
    mctx = m_all[:, 8, :].reshape(6, D_MODEL)
    zeros6 = jnp.zeros((6, D_MODEL), F32)
    mod_a = jnp.stack([jnp.concatenate([mvec[0:2], zeros6], axis=0), jnp.concatenate([mctx[0:2], zeros6], axis=0)])
    mod_b = jnp.concatenate([mvec[2:6], jnp.zeros((4, D_MODEL), F32)], axis=0)
    cw_full = jnp.pad(jnp.transpose(m_all[:, 9:12, 0:128], (1, 0, 2)).reshape(3, CONV_W), ((0, 5), (0, 0)))

    w_in_f = _cols_from_shards(g_in)
    zc = lambda n: jnp.zeros((D_MODEL, n), BF16)
    w_in_p = jnp.concatenate([w_in_f[:, 0:384], zc(64), w_in_f[:, 384:416], zc(32), w_in_f[:, 416:]], axis=1)
    w_uq_f = _cols_from_shards(g_uq).reshape(Q_RANK, N_HEADS, QK_DIM)
    w_uq_p = jnp.pad(w_uq_f, ((0, 0), (0, 0), (0, HEAD_PAD - QK_DIM))).reshape(Q_RANK, N_HEADS * HEAD_PAD)
    w_ukv_f = _cols_from_shards(g_ukv).reshape(KV_RANK, N_HEADS, QK_NOPE + V_DIM)
    padh = lambda a: jnp.pad(a, ((0, 0), (0, 0), (0, HEAD_PAD - a.shape[2]))).reshape(KV_RANK, N_HEADS * HEAD_PAD)
    w_ukv_p = jnp.concatenate([padh(w_ukv_f[:, :, :QK_NOPE]), padh(w_ukv_f[:, :, QK_NOPE:])], axis=1)
    cos_t, sin_a, sin_b = _rope_tables(t_lat, t_ctx)
    gf_row = final_norm_g.reshape(1, D_MODEL)
    c_idx = mc.reshape(1).astype(jnp.int32)
    j_idx = j.reshape(1).astype(jnp.int32)

    z, q, k, v, kt = _inproj_fwd(x2, ctx2, mod_a, w_in_p, q_norm_g, kv_norm_g, w_uq_p, w_ukv_p, cos_t, sin_a, sin_b)
    o, lse, g_out, w1, g_m2 = _attn_fwd(q, k, v, t_lat, (g_out, g_m1, g_m2))
    w_out_f = g_out.reshape(D_MODEL, D_MODEL)
    wo_attn = jnp.pad(w_out_f[0:512].reshape(N_HEADS, V_DIM, D_MODEL), ((0, 0), (0, HEAD_PAD - V_DIM), (0, 0)))
    wo_conv = w_out_f[512:]
    w2 = g_m2.reshape(D_FF, D_MODEL)
    r, da, h2, dy2, dx1, conv, acc_mlp, dy1, do, dgb, dyv = _mlp_fwdbwd(o, z, x2, tgt, mod_b, gf_row, cw_full, wo_attn,
                                                                         wo_conv, w1, w2)
    d_w1 = _wgrad(h2, da, "wgrad_mlp1", D_MODEL, FF_CHUNK)
    d_w2 = _wgrad(r, dy2, "wgrad_mlp2", FF_CHUNK, D_MODEL)
    d_wout, *big_got = _wgrad_out(o, conv, dy1, (d_w1, d_w2))
    d_wout = d_wout.reshape(N_SHARD, D_MODEL // N_SHARD, D_MODEL)
    big_grads, big_names = (d_w1, d_w2, d_wout), ("mlp1", "mlp2", "out")
    big_parts = [_add_pairs(a, g, c_idx, "rs_add_pairs_" + n) for a, g, n in zip(big_grads, big_got, big_names)]
    dqt, dk, dv, *big_recv = _attn_bwd(q, k, v, kt, o, do, lse, t_lat, big_parts)
    big_halves = [_add_chips(p, g, j_idx, "rs_add_chips_" + n) for p, g, n in zip(big_parts, big_recv, big_names)]
    gx, d_win, d_wuq, d_wukv, acc_in = _inproj_bwd(x2, ctx2, mod_a, z, dyv, dgb, dx1, dqt, dk, dv, cos_t, sin_a, sin_b,
                                                   w_in_p, w_uq_p, w_ukv_p, q_norm_g, kv_norm_g, cw_full)

    pad_row = lambda a: jnp.pad(a, ((0, 0), (0, D_MODEL - a.shape[1])))
    sv = jnp.concatenate([
        acc_in[0:2], acc_mlp[5:6], acc_mlp[2:4], acc_mlp[1:2],
        acc_in[2:4], acc_mlp[0:1], acc_in[4:5], acc_in[5:6], acc_in[6:9], acc_mlp[4:5],
        jnp.zeros((1, D_MODEL), F32)], axis=0)
    all_sv, red, o_cc, o_b, o_q, o_k, o_gf, *big_theirs = _small_exchange(
        sv, w_mod[0], cctx_row, m_c_ctx.reshape(1, D_MODEL), v_c_ctx.reshape(1, D_MODEL),
        b_mod.reshape(6, D_MODEL), m_b_mod.reshape(6, D_MODEL), v_b_mod.reshape(6, D_MODEL),
        q_norm_g, m_q_norm_g, v_q_norm_g, kv_norm_g, m_kv_norm_g, v_kv_norm_g,
        gf_row, m_final_norm_g.reshape(1, D_MODEL), v_final_norm_g.reshape(1, D_MODEL), big_halves)
    loss = red[14, 0]

    c9 = jnp.concatenate([c8[0::8], jnp.zeros((7, D_MODEL), F32)], axis=0)
    s_t = jnp.transpose(c9 * jax.nn.sigmoid(c9))
    dm_ex = all_sv[:, 0:6, :].reshape(8, 6 * D_MODEL)
    dm_ctx = jnp.concatenate([red[6:8].reshape(1, 2 * D_MODEL), jnp.zeros((1, 4 * D_MODEL), F32)], axis=1)
    dm16 = jnp.concatenate([dm_ex, dm_ctx, jnp.zeros((7, 6 * D_MODEL), F32)], axis=0)
    dm_sh = lax.dynamic_slice(dm16, (0, j * ncol), (16, ncol))
    g_wmod, d_wmod, nm_wmod, nv_wmod = _wmod_update(s_t, dm_sh, w_mod[0], m_w_mod[0], v_w_mod[0])

    g_cw = lax.dynamic_slice(red[11:14, 0:CONV_W], (0, j * 128), (3, 128))
    d_cw, nm_cw, nv_cw = _adamw_call(conv_w[0], g_cw, m_conv_w[0], v_conv_w[0], "adamw_conv")

    d_win_f = jnp.concatenate([d_win[:, 0:384], d_win[:, 448:480], d_win[:, 512:]], axis=1)
    d_wuq_f = d_wuq.reshape(Q_RANK, N_HEADS, HEAD_PAD)[:, :, 0:QK_DIM].reshape(Q_RANK, N_HEADS * QK_DIM)
    d_wukv3 = d_wukv.reshape(KV_RANK, 2, N_HEADS, HEAD_PAD)
    d_wukv_f = jnp.concatenate([d_wukv3[:, 0, :, 0:QK_NOPE], d_wukv3[:, 1, :, 0:V_DIM]], axis=2).reshape(KV_RANK, -1)
    rest = tuple(_cols_to_shards(a).astype(BF16) for a in (d_win_f, d_wuq_f, d_wukv_f))
    rest_names = ("in", "uq", "ukv")
    rest_got = _rs_sibling(rest, "rs_sibling_rest")
    rest_parts = [_add_pairs(a, g, c_idx, "rs_add_pairs_" + n) for a, g, n in zip(rest, rest_got, rest_names)]
    rest_recv = _rs_chips(rest_parts, "rs_chips_rest")
    rest_halves = [_add_chips(p, g, j_idx, "rs_add_chips_" + n) for p, g, n in zip(rest_parts, rest_recv, rest_names)]
    g_win, g_wuq, g_wukv = _rs_join(rest_halves, "rs_join_rest")
    upd = {}
    for name, w_, g_, m_, v_ in (("in", w_in, g_win, m_w_in, v_w_in), ("uq", w_uq, g_wuq, m_w_uq, v_w_uq),
                                 ("ukv", w_ukv, g_wukv, m_w_ukv, v_w_ukv)):
        upd[name] = _adamw_call(w_[0], g_, m_[0], v_[0], "adamw_" + name)
    g_w1, *upd["mlp1"] = _adamw_halves(w_mlp1[0], big_halves[0], big_theirs[0], m_w_mlp1[0], v_w_mlp1[0], c_idx,
                                       "adamw_mlp1")
    g_w2, *upd["mlp2"] = _adamw_halves(w_mlp2[0], big_halves[1], big_theirs[1], m_w_mlp2[0], v_w_mlp2[0], c_idx,
                                       "adamw_mlp2")
    g_wout, *upd["out"] = _adamw_halves(w_out[0], big_halves[2], big_theirs[2], m_w_out[0], v_w_out[0], c_idx,
                                        "adamw_out")

    def four(o4, shape):
        return [o4[r].reshape(shape) for r in range(4)]

    cc4 = four(o_cc, (D_MODEL,))
    b4 = [o_b[r].reshape(1, 6 * D_MODEL) for r in range(4)]
    q4 = four(o_q, (1, Q_RANK))
    k4 = four(o_k, (1, KV_RANK))
    gf4 = four(o_gf, (D_MODEL,))
    big = {"in": g_win, "uq": g_wuq, "ukv": g_wukv, "out": g_wout, "mlp1": g_w1, "mlp2": g_w2}

    def leaf(idx):
        wm = (g_wmod, d_wmod, nm_wmod, nv_wmod)[idx]
        cwv = (g_cw, d_cw, nm_cw, nv_cw)[idx]
        bigv = {n: (big[n] if idx == 0 else upd[n][idx - 1]) for n in big}
        return [cc4[idx], wm[None], b4[idx], bigv["in"][None], q4[idx], bigv["uq"][None], k4[idx], bigv["ukv"][None],
                cwv[None], bigv["out"][None], bigv["mlp1"][None], bigv["mlp2"][None], gf4[idx]]

    return (loss, gx[None], *leaf(0), *leaf(1), *leaf(2), *leaf(3))
```

```python
import functools
import math

import jax
import jax.numpy as jnp
from jax import lax
from jax.experimental import pallas as pl
from jax.experimental.pallas import tpu as pltpu

F32 = jnp.float32
BF16 = jnp.bfloat16
MESH = pl.DeviceIdType.MESH
HIGHEST = lax.Precision.HIGHEST

D_MODEL = 1024
N_HEADS = 8
QK_NOPE = 64
QK_ROPE = 32
QK_DIM = QK_NOPE + QK_ROPE
V_DIM = 64
Q_RANK = 256
KV_RANK = 128
CONV_W = 512
D_FF = 4096
GRID_W = 64
ROPE_THETA = 10000.0
EPS = 1e-6
ATTN_SCALE = 1.0 / math.sqrt(QK_DIM)
HEAD_PAD = 128
Z_COLS = 2048
ROPE_LANE0 = QK_NOPE
N_SHARD = 4
TOK_TILE = 256
FF_CHUNK = 1024
MLP_FF_CHUNK = 1024
KEY_CHUNK = 512
ATTN_FWD_Q_BLOCK = 512
ATTN_HEADS_PER_STEP = 4
ATTN_BWD_HEADS_PER_STEP = 2
ATTN_BWD_Q_BLOCK = 512
KEY_CHUNK_BWD = 512

ADAM_LR = 0.001
ADAM_B1 = 0.9
ADAM_B2 = 0.999
ADAM_EPS = 1e-08
ADAM_WD = 0.01
ADAM_STEP = 10

LOG2E = 1.4426950408889634

VMEM_LIMIT = 56 * 1024 * 1024
STAGE_VMEM_LIMIT = 32 * 1024 * 1024


def _pos():
    return lax.axis_index("x"), lax.axis_index("y"), lax.axis_index("c")


def _dot(a, b):
    return jnp.dot(a, b, preferred_element_type=F32)


def _dot_nt(a, b):
    return lax.dot_general(a, b, (((1,), (1,)), ((), ())), preferred_element_type=F32)


def _dot_tn(a, b):
    return lax.dot_general(a, b, (((0,), (0,)), ((), ())), preferred_element_type=F32)


def _rope(v, cos, sa, sb):
    return v * cos + pltpu.roll(v, 8, 1) * sa + pltpu.roll(v, HEAD_PAD - 8, 1) * sb


def _unrope(g, cos, sa, sb):
    return g * cos + pltpu.roll(g * sa, HEAD_PAD - 8, 1) + pltpu.roll(g * sb, 8, 1)


def _sigmoid(v):
    return 1.0 / (1.0 + jnp.exp(-v))


def _adamw(w, g, m, v):
    m = ADAM_B1 * m + (1.0 - ADAM_B1) * g
    v = ADAM_B2 * v + (1.0 - ADAM_B2) * (g * g)
    m_hat = m / (1.0 - ADAM_B1 ** ADAM_STEP)
    v_hat = v / (1.0 - ADAM_B2 ** ADAM_STEP)
    delta = -ADAM_LR * (m_hat / (jnp.sqrt(v_hat) + ADAM_EPS) + ADAM_WD * w)
    return delta, m, v


def _shift_rows(u, prev_row, next_row):
    n = u.shape[0]
    rows = lax.broadcasted_iota(jnp.int32, u.shape, 0)
    um1 = jnp.where(rows == 0, prev_row, pltpu.roll(u, 1, 0))
    up1 = jnp.where(rows == n - 1, next_row, pltpu.roll(u, n - 1, 0))
    return um1, up1


def _const_spec(shape):
    nd = len(shape)
    return pl.BlockSpec(shape, lambda *_: (0,) * nd)


def _resident_spec(shape):
    nd = len(shape)
    return pl.BlockSpec(shape, lambda *_: (0,) * nd, pipeline_mode=pl.Buffered(1))


def _peer(r, x, y, c):
    px = 1 - x if r & 4 else x
    py = 1 - y if r & 2 else y
    pc = 1 - c if r & 1 else c
    return (px, py, pc)


def _prologue(c_row, cctx_row, w_mod_sh, b_sh, cw_sh, srcs, n_gather):
    ncol = w_mod_sh.shape[1]
    n = len(srcs)
    n_split = 4

    def body(c_ref, cctx_ref, w_ref, b_ref, cw_ref, *refs):
        ins, (c8_ref, m_ref), outs = refs[:n], refs[n:n + 2], refs[n + 2:2 * n + 2]
        mine_ref, msh_ref = refs[2 * n + 2:2 * n + 4]
        f32s, bfs = refs[2 * n + 4:3 * n + 4], refs[3 * n + 4:4 * n + 4]
        ssem, rsem, ssem2, rsem2, lsem_in, lsem_out = refs[4 * n + 4:4 * n + 10]
        x, y, c = _pos()
        me = 4 * x + 2 * y + c
        j = 2 * x + y

        def pieces(rows):
            step = rows // n_split
            return [pl.ds(q * step, step) for q in range(n_split)]

        for t in range(n):
            for sl in pieces(ins[t].shape[0]):
                pltpu.make_async_copy(ins[t].at[sl], f32s[t].at[sl], lsem_in.at[t]).start()
        mine_ref[...] = jnp.zeros(mine_ref.shape, F32)
        mine_ref[0:1, :] = c_ref[...]
        my_rows = pl.ds(pl.multiple_of(8 * me, 8), 8)
        sends = []
        for r in range(1, 8):
            cp = pltpu.make_async_remote_copy(
                src_ref=mine_ref, dst_ref=c8_ref.at[my_rows], send_sem=ssem.at[r - 1], recv_sem=rsem.at[r - 1],
                device_id=_peer(r, x, y, c), device_id_type=MESH)
            cp.start()
            sends.append(cp)

        def cast_and_store(t):
            pltpu.make_async_copy(ins[t], f32s[t], lsem_in.at[t]).wait()
            bfs[t][...] = f32s[t][...].astype(BF16)
            for sl in pieces(ins[t].shape[0]):
                pltpu.make_async_copy(bfs[t].at[sl], outs[t].at[j, sl], lsem_out.at[t]).start()

        gather = _ShardGather(outs[:n_gather], *refs[4 * n + 10:])
        for t in range(n_gather):
            cast_and_store(t)
        for t in range(n_gather):
            pltpu.make_async_copy(bfs[t], outs[t].at[j], lsem_out.at[t]).wait()
        gather.start()
        for cp in sends:
            cp.wait()
        c8_ref[my_rows, :] = mine_ref[...]
        c8_ref[64:72, :] = jnp.zeros((8, D_MODEL), F32)
        c8_ref[64:65, :] = cctx_ref[...]
        cv = c8_ref[...]
        s = cv * _sigmoid(cv)
        m = jnp.dot(s, w_ref[...], precision=HIGHEST, preferred_element_type=F32) + b_ref[...]
        msh_ref[0:64, :] = m[0:64, :]
        msh_ref[64:72, :] = jnp.zeros((8, ncol), F32)
        msh_ref[64:65, :] = m[64:65, :]
        msh_ref[65:68, 0:128] = cw_ref[0:3, :]
        m_ref[j, 0:8, :] = msh_ref[my_rows, :]
        m_ref[j, 8:16, :] = msh_ref[64:72, :]
        sends2 = []
        for k, (px, py) in enumerate(_chips(x, y)):
            theirs = pl.ds(pl.multiple_of(8 * (4 * px + 2 * py + c), 8), 8)
            for half, src in enumerate((msh_ref.at[theirs], msh_ref.at[64:72])):
                cp = pltpu.make_async_remote_copy(
                    src_ref=src, dst_ref=m_ref.at[j, 8 * half:8 * half + 8], send_sem=ssem2.at[2 * k + half],
                    recv_sem=rsem2.at[2 * k + half], device_id=(px, py, c), device_id_type=MESH)
                cp.start()
                sends2.append(cp)
        for t in range(n_gather, n):
            cast_and_store(t)
        gather.forward()
        gather.finish()
        for t in range(n_gather, n):
            pltpu.make_async_copy(bfs[t], outs[t].at[j], lsem_out.at[t]).wait()
        for cp in sends2:
            cp.wait()

    vm = pl.BlockSpec(memory_space=pltpu.VMEM)
    hbm = pl.BlockSpec(memory_space=pl.ANY)
    return pl.pallas_call(
        body, name="prologue",
        out_shape=(jax.ShapeDtypeStruct((72, D_MODEL), F32), jax.ShapeDtypeStruct((N_SHARD, 16, ncol), F32))
        + tuple(jax.ShapeDtypeStruct((N_SHARD,) + a.shape, BF16) for a in srcs),
        in_specs=[vm] * 5 + [hbm] * n, out_specs=(vm, vm) + (hbm,) * n,
        scratch_shapes=[pltpu.VMEM((8, D_MODEL), F32), pltpu.VMEM((72, ncol), F32)]
        + [pltpu.VMEM(a.shape, F32) for a in srcs] + [pltpu.VMEM(a.shape, BF16) for a in srcs]
        + [pltpu.SemaphoreType.DMA((7,)), pltpu.SemaphoreType.DMA((7,)),
           pltpu.SemaphoreType.DMA((6,)), pltpu.SemaphoreType.DMA((6,)),
           pltpu.SemaphoreType.DMA((n,)), pltpu.SemaphoreType.DMA((n,))] + _gather_sems(n_gather),
        compiler_params=pltpu.CompilerParams(vmem_limit_bytes=VMEM_LIMIT),
    )(c_row, cctx_row, w_mod_sh, b_sh, cw_sh, *srcs)


def _chips(x, y):
    return [(1 - x, y), (x, 1 - y), (1 - x, 1 - y)]


def _halves(ref, c, align):
    hr = ref.shape[-2] // 2
    return (pl.ds(pl.multiple_of(c * hr, align), hr), pl.ds(pl.multiple_of((1 - c) * hr, align), hr))


class _ShardGather:
    def __init__(self, refs, ssem, rsem, fsend, frecv):
        self.refs, self.sems = refs, (ssem, rsem, fsend, frecv)
        self.x, self.y, self.c = _pos()
        self.j = 2 * self.x + self.y

    def _ici(self, a, k, slot):
        g = self.refs[a]
        ssem, rsem, _, _ = self.sems
        mine, _ = _halves(g, self.c, 16)
        px, py = _chips(self.x, self.y)[k]
        return pltpu.make_async_remote_copy(
            src_ref=g.at[self.j, mine], dst_ref=g.at[slot, mine], send_sem=ssem.at[3 * a + k],
            recv_sem=rsem.at[3 * a + k], device_id=(px, py, self.c), device_id_type=MESH)

    def _d2d(self, a, k, to_other_half):
        g = self.refs[a]
        _, _, fsend, frecv = self.sems
        mine, theirs = _halves(g, self.c, 16)
        px, py = _chips(self.x, self.y)[k]
        jk = 2 * px + py
        return pltpu.make_async_remote_copy(
            src_ref=g.at[jk, mine], dst_ref=g.at[jk, theirs if to_other_half else mine],
            send_sem=fsend.at[3 * a + k], recv_sem=frecv.at[3 * a + k],
            device_id=(self.x, self.y, 1 - self.c), device_id_type=MESH)

    def start(self):
        for a in range(len(self.refs)):
            for k in range(3):
                self._ici(a, k, self.j).start()

    def forward(self):
        for a in range(len(self.refs)):
            for k, (px, py) in enumerate(_chips(self.x, self.y)):
                self._ici(a, k, 2 * px + py).wait_recv()
                self._d2d(a, k, False).start()

    def finish(self):
        for a in range(len(self.refs)):
            for k in range(3):
                self._d2d(a, k, True).wait()
                self._ici(a, k, self.j).wait_send()


def _gather_sems(n_arrays):
    return [pltpu.SemaphoreType.DMA((3 * n_arrays,)) for _ in range(4)]


class _SiblingSend:
    def __init__(self, g_refs, got_refs, ssem, rsem):
        self.g_refs, self.got_refs, self.ssem, self.rsem = g_refs, got_refs, ssem, rsem
        self.x, self.y, self.c = _pos()

    def _copy(self, a, shard):
        _, theirs = _halves(self.g_refs[a], self.c, 16)
        src = self.g_refs[a].at[:, theirs] if shard is None else self.g_refs[a].at[shard, theirs]
        dst = self.got_refs[a] if shard is None else self.got_refs[a].at[shard]
        return pltpu.make_async_remote_copy(
            src_ref=src, dst_ref=dst, send_sem=self.ssem.at[a], recv_sem=self.rsem.at[a],
            device_id=(self.x, self.y, 1 - self.c), device_id_type=MESH)

    def start(self):
        for a in range(len(self.g_refs)):
            for s in range(N_SHARD):
                self._copy(a, s).start()

    def finish(self):
        for a in range(len(self.g_refs)):
            self._copy(a, None).wait()


class _SiblingSwap:
    def __init__(self, h_refs, t_refs, ssem, rsem):
        self.h_refs, self.t_refs, self.ssem, self.rsem = h_refs, t_refs, ssem, rsem
        self.x, self.y, self.c = _pos()

    def _copy(self, a):
        return pltpu.make_async_remote_copy(
            src_ref=self.h_refs[a], dst_ref=self.t_refs[a], send_sem=self.ssem.at[a], recv_sem=self.rsem.at[a],
            device_id=(self.x, self.y, 1 - self.c), device_id_type=MESH)

    def start(self):
        for a in range(len(self.h_refs)):
            self._copy(a).start()

    def finish(self):
        for a in range(len(self.h_refs)):
            self._copy(a).wait()


def _rs_sibling(arrs, name):
    n = len(arrs)

    def body(*refs):
        send = _SiblingSend(refs[:n], refs[n:2 * n], refs[2 * n], refs[2 * n + 1])
        send.start()
        send.finish()

    hbm = pl.BlockSpec(memory_space=pl.ANY)
    return pl.pallas_call(
        body, name=name,
        out_shape=tuple(jax.ShapeDtypeStruct((N_SHARD, a.shape[1] // 2, a.shape[2]), BF16) for a in arrs),
        in_specs=[hbm] * n, out_specs=(hbm,) * n,
        scratch_shapes=[pltpu.SemaphoreType.DMA((n,)), pltpu.SemaphoreType.DMA((n,))],
    )(*arrs)


class _ChipScatter:
    def __init__(self, parts, gots, ssem, rsem):
        self.parts, self.gots, self.ssem, self.rsem = parts, gots, ssem, rsem
        self.x, self.y, self.c = _pos()

    def _copy(self, a, k):
        px, py = _chips(self.x, self.y)[k]
        return pltpu.make_async_remote_copy(
            src_ref=self.parts[a].at[2 * px + py], dst_ref=self.gots[a].at[k], send_sem=self.ssem.at[3 * a + k],
            recv_sem=self.rsem.at[3 * a + k], device_id=(px, py, self.c), device_id_type=MESH)

    def start(self):
        for a in range(len(self.parts)):
            for k in range(3):
                self._copy(a, k).start()

    def finish(self):
        for a in range(len(self.parts)):
            for k in range(3):
                self._copy(a, k).wait()


def _rs_join(halves, name):
    n = len(halves)

    def body(*refs):
        h_refs, f_refs, stages = refs[:n], refs[n:2 * n], refs[2 * n:3 * n]
        lsem_in, lsem_out, ssem, rsem = refs[3 * n:]
        x, y, c = _pos()
        remote = []
        for a in range(n):
            mine, _ = _halves(f_refs[a], c, 8)
            cp = pltpu.make_async_remote_copy(
                src_ref=h_refs[a], dst_ref=f_refs[a].at[mine], send_sem=ssem.at[a], recv_sem=rsem.at[a],
                device_id=(x, y, 1 - c), device_id_type=MESH)
            cp.start()
            remote.append(cp)
            pltpu.make_async_copy(h_refs[a], stages[a], lsem_in.at[a]).start()
        local = []
        for a in range(n):
            mine, _ = _halves(f_refs[a], c, 8)
            pltpu.make_async_copy(h_refs[a], stages[a], lsem_in.at[a]).wait()
            cp = pltpu.make_async_copy(stages[a], f_refs[a].at[mine], lsem_out.at[a])
            cp.start()
            local.append(cp)
        for cp in remote + local:
            cp.wait()

    hbm = pl.BlockSpec(memory_space=pl.ANY)
    return pl.pallas_call(
        body, name=name,
        out_shape=tuple(jax.ShapeDtypeStruct((2 * h.shape[0], h.shape[1]), F32) for h in halves),
        in_specs=[hbm] * n, out_specs=(hbm,) * n,
        scratch_shapes=[pltpu.VMEM(h.shape, F32) for h in halves]
        + [pltpu.SemaphoreType.DMA((n,)) for _ in range(4)],
        compiler_params=pltpu.CompilerParams(vmem_limit_bytes=STAGE_VMEM_LIMIT),
    )(*halves)


def _row_block(rows):
    return max(b for b in range(16, 513, 16) if rows % b == 0)


def _add_pairs(arr, got, c_idx, name):
    hr, cols = got.shape[1], got.shape[2]
    rb = _row_block(hr)
    nb = hr // rb

    def body(c_ref, a_ref, b_ref, o_ref):
        o_ref[...] = (a_ref[...].astype(F32) + b_ref[...].astype(F32)).astype(BF16)

    spec = pl.BlockSpec((1, rb, cols), lambda s, r, c_ref: (s, r, 0))
    grid_spec = pltpu.PrefetchScalarGridSpec(
        num_scalar_prefetch=1, grid=(N_SHARD, nb),
        in_specs=[pl.BlockSpec((1, rb, cols), lambda s, r, c_ref: (s, c_ref[0] * nb + r, 0)), spec],
        out_specs=spec)
    return pl.pallas_call(
        body, name=name, grid_spec=grid_spec, out_shape=jax.ShapeDtypeStruct(got.shape, BF16),
    )(c_idx, arr, got)


def _add_chips(part, got, j_idx, name):
    hr, cols = got.shape[1], got.shape[2]
    rb = _row_block(hr)

    def body(j_ref, p_ref, g_ref, o_ref):
        acc = p_ref[0].astype(F32)
        for k in range(3):
            acc = acc + g_ref[k].astype(F32)
        o_ref[...] = acc

    grid_spec = pltpu.PrefetchScalarGridSpec(
        num_scalar_prefetch=1, grid=(hr // rb,),
        in_specs=[pl.BlockSpec((1, rb, cols), lambda r, j_ref: (j_ref[0], r, 0)),
                  pl.BlockSpec((3, rb, cols), lambda r, j_ref: (0, r, 0))],
        out_specs=pl.BlockSpec((rb, cols), lambda r, j_ref: (r, 0)))
    return pl.pallas_call(
        body, name=name, grid_spec=grid_spec, out_shape=jax.ShapeDtypeStruct((hr, cols), F32),
    )(j_idx, part, got)


def _small_exchange(sv, w_mod_sh, cctx, m_cctx, v_cctx, bmod, m_bmod, v_bmod, qg, m_qg, v_qg, kvg, m_kvg, v_kvg,
                    gf, m_gf, v_gf, swap_halves, scatter_parts):
    ncol = w_mod_sh.shape[1]
    n_s, n_p = len(swap_halves), len(scatter_parts)
    n_x = n_s + n_p

    def body(sv_ref, w_ref, cctx_ref, mcc_ref, vcc_ref, b_ref, mb_ref, vb_ref, qg_ref, mq_ref, vq_ref,
             kg_ref, mk_ref, vk_ref, gf_ref, mgf_ref, vgf_ref, *rest):
        all_ref, red_ref, occ_ref, ob_ref, oq_ref, ok_ref, ogf_ref = rest[n_x:n_x + 7]
        vec_ref, part_ref, ssem, rsem, ssem2, rsem2, wsend, wrecv, psend, precv = rest[2 * n_x + 7:]
        swap = _SiblingSwap(rest[:n_s], rest[n_x + 7:n_x + 7 + n_s], wsend, wrecv)
        scatter = _ChipScatter(rest[n_s:n_x], rest[n_x + 7 + n_s:2 * n_x + 7], psend, precv)
        swap.start()
        scatter.start()
        x, y, c = _pos()
        me = 4 * x + 2 * y + c
        j = 2 * x + y
        sends = []
        for r in range(1, 8):
            cp = pltpu.make_async_remote_copy(
                src_ref=sv_ref, dst_ref=all_ref.at[me], send_sem=ssem.at[r - 1], recv_sem=rsem.at[r - 1],
                device_id=_peer(r, x, y, c), device_id_type=MESH)
            cp.start()
            sends.append(cp)
        for cp in sends:
            cp.wait()
        all_ref[me] = sv_ref[...]
        red = all_ref[0]
        for d in range(1, 8):
            red = red + all_ref[d]
        red_ref[...] = red
        vec_ref[...] = jnp.zeros(vec_ref.shape, F32)

        @pl.when(j == 0)
        def _():
            vec_ref[0:1, 0:1024] = red[6:7, :]
            vec_ref[0:1, 1024:1536] = red[7:8, 0:512]

        @pl.when(j == 1)
        def _():
            vec_ref[0:1, 0:512] = red[7:8, 512:1024]

        part = lax.dot_general(vec_ref[...], w_ref[...], (((1,), (1,)), ((), ())), precision=HIGHEST,
                               preferred_element_type=F32)
        part_ref[j] = part
        sends2 = []
        for k, r in enumerate((4, 2, 6)):
            cp = pltpu.make_async_remote_copy(
                src_ref=part_ref.at[j], dst_ref=part_ref.at[j], send_sem=ssem2.at[k], recv_sem=rsem2.at[k],
                device_id=_peer(r, x, y, c), device_id_type=MESH)
            cp.start()
            sends2.append(cp)
        for cp in sends2:
            cp.wait()
        tot = part_ref[0]
        for s in range(1, N_SHARD):
            tot = tot + part_ref[s]
        cc = cctx_ref[...]
        sg = _sigmoid(cc)
        g_cc = tot[0:1, :] * (sg * (1.0 + cc * (1.0 - sg)))
        d_, m_, v_ = _adamw(cc, g_cc, mcc_ref[...], vcc_ref[...])
        occ_ref[0:1, :] = g_cc
        occ_ref[1:2, :] = d_
        occ_ref[2:3, :] = m_
        occ_ref[3:4, :] = v_
        occ_ref[4:8, :] = jnp.zeros((4, D_MODEL), F32)
        g_b = red[0:6, :]
        pad = jnp.concatenate([red[6:8, :], jnp.zeros((4, D_MODEL), F32)], axis=0)
        g_b = g_b + pad
        d_, m_, v_ = _adamw(b_ref[...], g_b, mb_ref[...], vb_ref[...])
        ob_ref[0] = g_b
        ob_ref[1] = d_
        ob_ref[2] = m_
        ob_ref[3] = v_
        g_q = red[9:10, 0:Q_RANK]
        d_, m_, v_ = _adamw(qg_ref[...], g_q, mq_ref[...], vq_ref[...])
        oq_ref[0:1, :] = g_q
        oq_ref[1:2, :] = d_
        oq_ref[2:3, :] = m_
        oq_ref[3:4, :] = v_
        oq_ref[4:8, :] = jnp.zeros((4, Q_RANK), F32)
        g_k = red[10:11, 0:KV_RANK]
        d_, m_, v_ = _adamw(kg_ref[...], g_k, mk_ref[...], vk_ref[...])
        ok_ref[0:1, :] = g_k
        ok_ref[1:2, :] = d_
        ok_ref[2:3, :] = m_
        ok_ref[3:4, :] = v_
        ok_ref[4:8, :] = jnp.zeros((4, KV_RANK), F32)
        g_f = red[8:9, :]
        d_, m_, v_ = _adamw(gf_ref[...], g_f, mgf_ref[...], vgf_ref[...])
        ogf_ref[0:1, :] = g_f
        ogf_ref[1:2, :] = d_
        ogf_ref[2:3, :] = m_
        ogf_ref[3:4, :] = v_
        ogf_ref[4:8, :] = jnp.zeros((4, D_MODEL), F32)
        swap.finish()
        scatter.finish()

    vm = pl.BlockSpec(memory_space=pltpu.VMEM)
    hbm = pl.BlockSpec(memory_space=pl.ANY)
    out_shape = (
        jax.ShapeDtypeStruct((8, 16, D_MODEL), F32),
        jax.ShapeDtypeStruct((16, D_MODEL), F32),
        jax.ShapeDtypeStruct((8, D_MODEL), F32),
        jax.ShapeDtypeStruct((4, 6, D_MODEL), F32),
        jax.ShapeDtypeStruct((8, Q_RANK), F32),
        jax.ShapeDtypeStruct((8, KV_RANK), F32),
        jax.ShapeDtypeStruct((8, D_MODEL), F32),
    ) + tuple(jax.ShapeDtypeStruct(h.shape, F32) for h in swap_halves) + tuple(
        jax.ShapeDtypeStruct((3,) + p.shape[1:], BF16) for p in scatter_parts)
    return pl.pallas_call(
        body, name="small_exchange", out_shape=out_shape, in_specs=[vm] * 17 + [hbm] * n_x,
        out_specs=tuple([vm] * 7) + (hbm,) * n_x,
        scratch_shapes=[pltpu.VMEM((8, ncol), F32), pltpu.VMEM((N_SHARD, 8, D_MODEL), F32),
                        pltpu.SemaphoreType.DMA((7,)), pltpu.SemaphoreType.DMA((7,)),
                        pltpu.SemaphoreType.DMA((3,)), pltpu.SemaphoreType.DMA((3,)),
                        pltpu.SemaphoreType.DMA((n_s,)), pltpu.SemaphoreType.DMA((n_s,)),
                        pltpu.SemaphoreType.DMA((3 * n_p,)), pltpu.SemaphoreType.DMA((3 * n_p,))],
        compiler_params=pltpu.CompilerParams(vmem_limit_bytes=VMEM_LIMIT),
    )(sv, w_mod_sh, cctx, m_cctx, v_cctx, bmod, m_bmod, v_bmod, qg, m_qg, v_qg, kvg, m_kvg, v_kvg, gf, m_gf, v_gf,
      *swap_halves, *scatter_parts)


def _inproj_fwd(x2, ctx2, mod_a, w_in, qg, kvg, w_uq, w_ukv, cos_t, sin_a, sin_b):
    t_lat, t_ctx = x2.shape[0], ctx2.shape[0]
    tm = TOK_TILE
    n_lat = t_lat // tm
    n_all = n_lat + t_ctx // tm
    e_rows = t_lat + t_ctx

    def body(x_ref, ctx_ref, mod_ref, win_ref, qg_ref, kvg_ref, wuq_ref, wukv_ref, cos_ref, sa_ref, sb_ref,
             z_ref, q_ref, k_ref, v_ref, kt_ref):
        i = pl.program_id(0)
        xin = jnp.where(i < n_lat, x_ref[...], ctx_ref[...])
        xn = xin * lax.rsqrt(jnp.mean(xin * xin, axis=-1, keepdims=True) + EPS)
        h1 = (xn * (1.0 + mod_ref[0, 1:2, :]) + mod_ref[0, 0:1, :]).astype(BF16)
        z = _dot(h1, win_ref[...])
        z_ref[...] = z
        cos, sa, sb = cos_ref[...], sa_ref[...], sb_ref[...]
        cq = z[:, 0:Q_RANK]
        cqn = (cq * lax.rsqrt(jnp.mean(cq * cq, axis=-1, keepdims=True) + EPS) * qg_ref[...]).astype(BF16)
        q = _dot(cqn, wuq_ref[...])
        ckv = z[:, Q_RANK:Q_RANK + KV_RANK]
        ckvn = (ckv * lax.rsqrt(jnp.mean(ckv * ckv, axis=-1, keepdims=True) + EPS) * kvg_ref[...]).astype(BF16)
        kv = _dot(ckvn, wukv_ref[...])
        kr = _rope(z[:, Q_RANK + KV_RANK:Q_RANK + KV_RANK + HEAD_PAD], cos, sa, sb)
        ones_lane = lax.broadcasted_iota(jnp.int32, (tm, HEAD_PAD), 1) == V_DIM
        for h in range(N_HEADS):
            lo = h * HEAD_PAD
            q_ref[h] = _rope(q[:, lo:lo + HEAD_PAD], cos, sa, sb).astype(BF16)
            kh = kv[:, lo:lo + HEAD_PAD] + kr
            k_ref[h] = kh.astype(BF16)
            kt_ref[h] = kh.T.astype(BF16)
            vh = kv[:, N_HEADS * HEAD_PAD + lo:N_HEADS * HEAD_PAD + lo + HEAD_PAD]
            v_ref[h] = jnp.where(ones_lane, 1.0, vh).astype(BF16)

    row = lambda i: (i, 0)
    head_spec = pl.BlockSpec((N_HEADS, tm, HEAD_PAD), lambda i: (0, i, 0))
    head_shape = jax.ShapeDtypeStruct((N_HEADS, e_rows, HEAD_PAD), BF16)
    return pl.pallas_call(
        body, name="inproj_fwd", grid=(n_all,),
        out_shape=(jax.ShapeDtypeStruct((e_rows, Z_COLS), F32), head_shape, head_shape, head_shape,
                   jax.ShapeDtypeStruct((N_HEADS, HEAD_PAD, e_rows), BF16)),
        in_specs=[
            pl.BlockSpec((tm, D_MODEL), lambda i: (jnp.minimum(i, n_lat - 1), 0)),
            _const_spec((tm, D_MODEL)),
            pl.BlockSpec((1, 8, D_MODEL), lambda i: (i // n_lat, 0, 0)),
            _const_spec(w_in.shape), _const_spec(qg.shape), _const_spec(kvg.shape),
            _const_spec(w_uq.shape), _const_spec(w_ukv.shape),
            pl.BlockSpec((tm, HEAD_PAD), row), pl.BlockSpec((tm, HEAD_PAD), row), pl.BlockSpec((tm, HEAD_PAD), row),
        ],
        out_specs=(pl.BlockSpec((tm, Z_COLS), row), head_spec, head_spec, head_spec,
                   pl.BlockSpec((N_HEADS, HEAD_PAD, tm), lambda i: (0, 0, i))),
        compiler_params=pltpu.CompilerParams(vmem_limit_bytes=VMEM_LIMIT),
    )(x2, ctx2, mod_a, w_in, qg, kvg, w_uq, w_ukv, cos_t, sin_a, sin_b)


def _key_chunks(e_rows, size):
    n_chunks = max(1, e_rows // size)
    return [(ci * size, size if ci < n_chunks - 1 else e_rows - ci * size) for ci in range(n_chunks)]


def _attn_fwd(q, k, v, t_lat, shard_arrays):
    e_rows = k.shape[1]
    tq = min(t_lat, ATTN_FWD_Q_BLOCK)
    bounds = _key_chunks(e_rows, KEY_CHUNK)
    c2 = ATTN_SCALE * LOG2E

    hb = ATTN_HEADS_PER_STEP
    n_hb = N_HEADS // hb

    def body(q_ref, k_ref, v_ref, o_ref, lse_ref):
        qs = [q_ref[b] for b in range(hb)]
        m, acc = [None] * hb, [None] * hb
        for lo, n in bounds:
            for b in range(hb):
                s = _dot_nt(qs[b], k_ref[b, lo:lo + n, :])
                mc = jnp.max(s, axis=-1, keepdims=True)
                m_new = mc if m[b] is None else jnp.maximum(m[b], mc)
                p = jnp.exp2((s - m_new) * c2)
                pv = _dot(p.astype(BF16), v_ref[b, lo:lo + n, :])
                acc[b] = pv if m[b] is None else acc[b] * jnp.exp2((m[b] - m_new) * c2) + pv
                m[b] = m_new
        for b in range(hb):
            l = acc[b][:, V_DIM:V_DIM + 1]
            o_ref[b] = (acc[b] * (1.0 / l)).astype(BF16)
            lse = (m[b] * ATTN_SCALE + jnp.log(l)) * LOG2E
            lse_ref[b] = jnp.broadcast_to(lse, (tq, HEAD_PAD)).T[0:1, :]

    n_w = len(shard_arrays)
    n_q = t_lat // tq

    def body_with_gather(q_ref, k_ref, v_ref, *rest):
        o_ref, lse_ref = rest[n_w], rest[n_w + 1]
        gather = _ShardGather(rest[n_w + 2:2 * n_w + 2], *rest[2 * n_w + 2:])
        h, i = pl.program_id(0), pl.program_id(1)
        pl.when(jnp.logical_and(h == 0, i == 0))(gather.start)
        pl.when(jnp.logical_and(h == n_hb // 2, i == 0))(gather.forward)
        body(q_ref, k_ref, v_ref, o_ref, lse_ref)
        pl.when(jnp.logical_and(h == n_hb - 1, i == n_q - 1))(gather.finish)

    hbm = pl.BlockSpec(memory_space=pl.ANY)
    return pl.pallas_call(
        body_with_gather, name="attn_fwd", grid=(n_hb, n_q),
        out_shape=(jax.ShapeDtypeStruct((N_HEADS, t_lat, HEAD_PAD), BF16),
                   jax.ShapeDtypeStruct((N_HEADS, 1, t_lat), F32))
        + tuple(jax.ShapeDtypeStruct(a.shape, a.dtype) for a in shard_arrays),
        in_specs=[pl.BlockSpec((hb, tq, HEAD_PAD), lambda h, i: (h, i, 0)),
                  pl.BlockSpec((hb, e_rows, HEAD_PAD), lambda h, i: (h, 0, 0)),
                  pl.BlockSpec((hb, e_rows, HEAD_PAD), lambda h, i: (h, 0, 0))] + [hbm] * n_w,
        out_specs=(pl.BlockSpec((hb, tq, HEAD_PAD), lambda h, i: (h, i, 0)),
                   pl.BlockSpec((hb, 1, tq), lambda h, i: (h, 0, i))) + (hbm,) * n_w,
        input_output_aliases={3 + a: 2 + a for a in range(n_w)},
        scratch_shapes=_gather_sems(n_w),
        compiler_params=pltpu.CompilerParams(vmem_limit_bytes=VMEM_LIMIT),
    )(q, k, v, *shard_arrays)


def _attn_bwd(q, k, v, kt, o, do, lse_row, t_lat, parts):
    e_rows = k.shape[1]
    tq = min(t_lat, ATTN_BWD_Q_BLOCK)
    n_p = len(parts)
    n_q = t_lat // tq
    bounds = _key_chunks(e_rows, KEY_CHUNK_BWD)

    hb = ATTN_BWD_HEADS_PER_STEP
    n_hb = N_HEADS // hb

    def body(q_ref, k_ref, v_ref, kt_ref, o_ref, do_ref, lse_ref, *rest):
        dqt_ref, dk_ref, dv_ref = rest[n_p:n_p + 3]
        scatter = _ChipScatter(rest[:n_p], rest[n_p + 3:2 * n_p + 3], rest[2 * n_p + 3], rest[2 * n_p + 4])
        h, i = pl.program_id(0), pl.program_id(1)
        pl.when(jnp.logical_and(h == 0, i == 0))(scatter.start)

        @pl.when(i == 0)
        def _():
            dk_ref[...] = jnp.zeros(dk_ref.shape, F32)
            dv_ref[...] = jnp.zeros(dv_ref.shape, F32)

        qs, dos, lses, deltas = [], [], [], []
        for b in range(hb):
            qs.append(q_ref[b])
            dos.append(do_ref[b])
            lses.append(lse_ref[b])
            prod = o_ref[b].astype(F32) * dos[b].astype(F32)
            deltas.append(lax.dot_general(jnp.ones((8, HEAD_PAD), F32), prod, (((1,), (1,)), ((), ())),
                                          precision=HIGHEST, preferred_element_type=F32)[0:1, :])
        dqt = [None] * hb
        for lo, n in bounds:
            for b in range(hb):
                pt = jnp.exp2(_dot_nt(k_ref[b, lo:lo + n, :], qs[b]) * (ATTN_SCALE * LOG2E) - lses[b])
                dpt = _dot_nt(v_ref[b, lo:lo + n, :], dos[b])
                dst = (pt * (dpt - deltas[b])).astype(BF16)
                dv_c = _dot(pt.astype(BF16), dos[b])
                dk_c = _dot(dst, qs[b])
                part = _dot(kt_ref[b, :, lo:lo + n], dst)
                dqt[b] = part if dqt[b] is None else dqt[b] + part
                dk_ref[b, lo:lo + n, :] += dk_c * ATTN_SCALE
                dv_ref[b, lo:lo + n, :] += dv_c
        for b in range(hb):
            dqt_ref[b] = dqt[b] * ATTN_SCALE

        pl.when(jnp.logical_and(h == n_hb - 1, i == n_q - 1))(scatter.finish)

    hbm = pl.BlockSpec(memory_space=pl.ANY)
    qspec = pl.BlockSpec((hb, tq, HEAD_PAD), lambda h, i: (h, i, 0))
    kspec = pl.BlockSpec((hb, e_rows, HEAD_PAD), lambda h, i: (h, 0, 0))
    return pl.pallas_call(
        body, name="attn_bwd", grid=(n_hb, n_q),
        out_shape=(jax.ShapeDtypeStruct((N_HEADS, HEAD_PAD, t_lat), F32),
                   jax.ShapeDtypeStruct((N_HEADS, e_rows, HEAD_PAD), F32),
                   jax.ShapeDtypeStruct((N_HEADS, e_rows, HEAD_PAD), F32))
        + tuple(jax.ShapeDtypeStruct((3,) + p.shape[1:], BF16) for p in parts),
        in_specs=[qspec, kspec, kspec, pl.BlockSpec((hb, HEAD_PAD, e_rows), lambda h, i: (h, 0, 0)), qspec, qspec,
                  pl.BlockSpec((hb, 1, tq), lambda h, i: (h, 0, i))] + [hbm] * n_p,
        out_specs=(pl.BlockSpec((hb, HEAD_PAD, tq), lambda h, i: (h, 0, i)), kspec, kspec) + (hbm,) * n_p,
        scratch_shapes=[pltpu.SemaphoreType.DMA((3 * n_p,)), pltpu.SemaphoreType.DMA((3 * n_p,))],
        compiler_params=pltpu.CompilerParams(vmem_limit_bytes=VMEM_LIMIT),
    )(q, k, v, kt, o, do, lse_row, *parts)


def _halo_specs(tm, col_block):
    per = tm // 8
    prev = pl.BlockSpec((8, CONV_W), lambda i: (jnp.maximum(i * per - 1, 0), col_block))
    nxt = pl.BlockSpec((8, CONV_W), lambda i: ((i + 1) * per, col_block))
    return prev, nxt


def _mlp_fwdbwd(o, z, x2, tgt, mod_b, gf, cw, wo_attn, wo_conv, w1, w2):
    t_lat = x2.shape[0]
    tm = TOK_TILE
    n_lat = t_lat // tm
    fc = MLP_FF_CHUNK
    n_ff = D_FF // fc

    def body(o_ref, gb_ref, gc_ref, xi_ref, gcp_ref, xip_ref, gcn_ref, xin_ref, cw_ref, woa_ref, woc_ref,
             x_ref, t_ref, mod_ref, gf_ref, w1_ref, w2_ref,
             r_ref, da_ref, h2_ref, dy2_ref, dx1_ref, conv_ref, acc_ref, dy1_ref, do_ref, dgb_ref, dyv_ref, ra_ref):
        i = pl.program_id(0)

        @pl.when(i == 0)
        def _():
            acc_ref[...] = jnp.zeros(acc_ref.shape, F32)

        g1, sh2, sc2, g2 = mod_ref[0:1, :], mod_ref[1:2, :], mod_ref[2:3, :], mod_ref[3:4, :]
        u = gc_ref[...] * xi_ref[...]
        u_prev = jnp.where(i > 0, gcp_ref[7:8, :] * xip_ref[7:8, :], 0.0)
        u_next = jnp.where(i < n_lat - 1, gcn_ref[0:1, :] * xin_ref[0:1, :], 0.0)
        um1, up1 = _shift_rows(u, u_prev, u_next)
        yv = cw_ref[0:1, :] * um1 + cw_ref[1:2, :] * u + cw_ref[2:3, :] * up1
        gb = gb_ref[...]
        conv = (gb * yv).astype(BF16)
        conv_ref[...] = conv
        y1 = _dot(conv, woc_ref[...])
        for h in range(N_HEADS):
            y1 = y1 + _dot(o_ref[h], woa_ref[h])
        x1 = x_ref[...] + g1 * y1
        rstd2 = lax.rsqrt(jnp.mean(x1 * x1, axis=-1, keepdims=True) + EPS)
        xn1 = x1 * rstd2
        h2 = (xn1 * (1.0 + sc2) + sh2).astype(BF16)
        h2_ref[...] = h2
        y2 = jnp.zeros((tm, D_MODEL), F32)
        for jj in range(n_ff):
            lo = jj * fc
            ra = jnp.maximum(_dot(h2, w1_ref[lo // FF_CHUNK, :, lo % FF_CHUNK:lo % FF_CHUNK + fc]), 0.0)
            ra_ref[jj] = ra
            r = (ra * ra).astype(BF16)
            r_ref[:, lo:lo + fc] = r
            y2 = y2 + _dot(r, w2_ref[lo:lo + fc, :])
        x2v = x1 + g2 * y2
        rstd3 = lax.rsqrt(jnp.mean(x2v * x2v, axis=-1, keepdims=True) + EPS)
        xn3 = x2v * rstd3
        gfv = gf_ref[...]
        diff = xn3 * gfv - t_ref[...]
        loss_t = 0.5 * jnp.sum(jnp.sum(diff * diff, axis=-1, keepdims=True), axis=0, keepdims=True) * (1.0 / D_MODEL)
        dy = diff * (1.0 / D_MODEL)
        dxn3 = dy * gfv
        dx2 = rstd3 * (dxn3 - xn3 * jnp.mean(dxn3 * xn3, axis=-1, keepdims=True))
        dy2 = (dx2 * g2).astype(BF16)
        dy2_ref[...] = dy2
        dh2 = jnp.zeros((tm, D_MODEL), F32)
        for jj in range(n_ff):
            lo = jj * fc
            dr = _dot_nt(dy2, w2_ref[lo:lo + fc, :])
            da = (2.0 * ra_ref[jj] * dr).astype(BF16)
            da_ref[:, lo:lo + fc] = da
            dh2 = dh2 + _dot_nt(da, w1_ref[lo // FF_CHUNK, :, lo % FF_CHUNK:lo % FF_CHUNK + fc])
        dxn1 = dh2 * (1.0 + sc2)
        dx1 = dx2 + rstd2 * (dxn1 - xn1 * jnp.mean(dxn1 * xn1, axis=-1, keepdims=True))
        dx1_ref[...] = dx1
        dy1 = (dx1 * g1).astype(BF16)
        dy1_ref[...] = dy1
        for h in range(N_HEADS):
            do_ref[h] = _dot_nt(dy1, woa_ref[h]).astype(BF16)
        dconv = _dot_nt(dy1, woc_ref[...])
        dgb_ref[...] = dconv * yv
        dyv_ref[...] = dconv * gb
        acc_ref[5:6, :] += jnp.sum(dx1 * y1, axis=0, keepdims=True)
        acc_ref[0:1, :] += jnp.sum(dy * xn3, axis=0, keepdims=True)
        acc_ref[1:2, :] += jnp.sum(dx2 * y2, axis=0, keepdims=True)
        acc_ref[2:3, :] += jnp.sum(dh2, axis=0, keepdims=True)
        acc_ref[3:4, :] += jnp.sum(dh2 * xn1, axis=0, keepdims=True)
        acc_ref[4:5, :] += jnp.broadcast_to(loss_t, (1, D_MODEL))

    row = lambda i: (i, 0)
    gcp, gcn = _halo_specs(tm, 2)
    xip, xin = _halo_specs(tm, 3)
    tile = pl.BlockSpec((tm, D_MODEL), row)
    wide = pl.BlockSpec((tm, D_FF), row)
    half = pl.BlockSpec((tm, CONV_W), row)
    return pl.pallas_call(
        body, name="mlp_fwdbwd", grid=(n_lat,),
        out_shape=(jax.ShapeDtypeStruct((t_lat, D_FF), BF16), jax.ShapeDtypeStruct((t_lat, D_FF), BF16),
                   jax.ShapeDtypeStruct((t_lat, D_MODEL), BF16), jax.ShapeDtypeStruct((t_lat, D_MODEL), BF16),
                   jax.ShapeDtypeStruct((t_lat, D_MODEL), F32), jax.ShapeDtypeStruct((t_lat, CONV_W), BF16),
                   jax.ShapeDtypeStruct((8, D_MODEL), F32),
                   jax.ShapeDtypeStruct((t_lat, D_MODEL), BF16),
                   jax.ShapeDtypeStruct((N_HEADS, t_lat, HEAD_PAD), BF16),
                   jax.ShapeDtypeStruct((t_lat, CONV_W), F32), jax.ShapeDtypeStruct((t_lat, CONV_W), F32)),
        in_specs=[
            pl.BlockSpec((N_HEADS, tm, HEAD_PAD), lambda i: (0, i, 0)),
            pl.BlockSpec((tm, CONV_W), lambda i: (i, 1)), pl.BlockSpec((tm, CONV_W), lambda i: (i, 2)),
            pl.BlockSpec((tm, CONV_W), lambda i: (i, 3)),
            gcp, xip, gcn, xin,
            _const_spec(cw.shape), _resident_spec(wo_attn.shape), _resident_spec(wo_conv.shape),
            tile, tile, _const_spec(mod_b.shape), _const_spec(gf.shape),
            _resident_spec(w1.shape), _resident_spec(w2.shape),
        ],
        out_specs=(wide, wide, tile, tile, tile, half, _const_spec((8, D_MODEL)),
                   tile, pl.BlockSpec((N_HEADS, tm, HEAD_PAD), lambda i: (0, i, 0)), half, half),
        scratch_shapes=[pltpu.VMEM((n_ff, tm, fc), F32)],
        compiler_params=pltpu.CompilerParams(vmem_limit_bytes=VMEM_LIMIT),
    )(o, z, z, z, z, z, z, z, cw, wo_attn, wo_conv, x2, tgt, mod_b, gf, w1, w2)


def _inproj_bwd(x2, ctx2, mod_a, z, dyv, dgb, dx1, dqt, dk, dv, cos_t, sin_a, sin_b, w_in, w_uq, w_ukv, qg, kvg, cw):
    t_lat, t_ctx = x2.shape[0], ctx2.shape[0]
    tm = TOK_TILE
    n_lat = t_lat // tm
    n_all = n_lat + t_ctx // tm
    group = max(g for g in (1, 2, 4) if n_lat % g == 0)

    def body(x_ref, ctx_ref, mod_ref, z_ref, gcp_ref, xip_ref, gcn_ref, xin_ref, dyv_ref, dyvp_ref, dyvn_ref,
             dgb_ref, dx1_ref, dqt_ref, dk_ref, dv_ref, cos_ref, sa_ref, sb_ref, win_ref, wuq_ref, wukv_ref,
             qg_ref, kvg_ref, cw_ref, gx_ref, dwin_ref, dwuq_ref, dwukv_ref, acc_ref, h1_buf, dz_buf):
        i = pl.program_id(0)
        lat = i < n_lat

        @pl.when(i == 0)
        def _():
            dwin_ref[...] = jnp.zeros(dwin_ref.shape, F32)
            dwuq_ref[...] = jnp.zeros(dwuq_ref.shape, F32)
            dwukv_ref[...] = jnp.zeros(dwukv_ref.shape, F32)
            acc_ref[...] = jnp.zeros(acc_ref.shape, F32)

        xin = jnp.where(lat, x_ref[...], ctx_ref[...])
        rstd = lax.rsqrt(jnp.mean(xin * xin, axis=-1, keepdims=True) + EPS)
        xn = xin * rstd
        sc = mod_ref[0, 1:2, :]
        h1 = (xn * (1.0 + sc) + mod_ref[0, 0:1, :]).astype(BF16)
        z = z_ref[...]
        cos, sa, sb = cos_ref[...], sa_ref[...], sb_ref[...]
        qgv, kvgv = qg_ref[...], kvg_ref[...]
        cq = z[:, 0:Q_RANK]
        cqh = cq * lax.rsqrt(jnp.mean(cq * cq, axis=-1, keepdims=True) + EPS)
        rq = lax.rsqrt(jnp.mean(cq * cq, axis=-1, keepdims=True) + EPS)
        cqn = (cqh * qgv).astype(BF16)
        parts = []
        for h in range(N_HEADS):
            g = jnp.where(lat, dqt_ref[h].T, 0.0)
            parts.append(_unrope(g, cos, sa, sb))
        dq = jnp.concatenate(parts, axis=1).astype(BF16)
        dcqn = _dot_nt(dq, wuq_ref[...])
        dwuq_ref[...] += _dot_tn(cqn, dq)
        acc_ref[4:5, 0:Q_RANK] += jnp.sum(dcqn * cqh, axis=0, keepdims=True)
        dxn = dcqn * qgv
        dcq = rq * (dxn - cqh * jnp.mean(dxn * cqh, axis=-1, keepdims=True))
        ckv = z[:, Q_RANK:Q_RANK + KV_RANK]
        rk = lax.rsqrt(jnp.mean(ckv * ckv, axis=-1, keepdims=True) + EPS)
        ckvh = ckv * rk
        ckvn = (ckvh * kvgv).astype(BF16)
        dks = [dk_ref[h] for h in range(N_HEADS)]
        dkr = dks[0]
        for h in range(1, N_HEADS):
            dkr = dkr + dks[h]
        dkv = jnp.concatenate(dks + [dv_ref[h] for h in range(N_HEADS)], axis=1).astype(BF16)
        dckvn = _dot_nt(dkv, wukv_ref[...])
        dwukv_ref[...] += _dot_tn(ckvn, dkv)
        acc_ref[5:6, 0:KV_RANK] += jnp.sum(dckvn * ckvh, axis=0, keepdims=True)
        dxn = dckvn * kvgv
        dckv = rk * (dxn - ckvh * jnp.mean(dxn * ckvh, axis=-1, keepdims=True))
        dkr = _unrope(dkr, cos, sa, sb)
        gb, gc, xi = z[:, 512:1024], z[:, 1024:1536], z[:, 1536:2048]
        u = gc * xi
        u_prev = jnp.where(i > 0, gcp_ref[7:8, :] * xip_ref[7:8, :], 0.0)
        u_next = jnp.where(i < n_lat - 1, gcn_ref[0:1, :] * xin_ref[0:1, :], 0.0)
        um1, up1 = _shift_rows(u, u_prev, u_next)
        dyv = jnp.where(lat, dyv_ref[...], 0.0)
        dyv_prev = jnp.where(jnp.logical_and(i > 0, lat), dyvp_ref[7:8, :], 0.0)
        dyv_next = jnp.where(i < n_lat - 1, dyvn_ref[0:1, :], 0.0)
        dyv_m1, dyv_p1 = _shift_rows(dyv, dyv_prev, dyv_next)
        du = cw_ref[0:1, :] * dyv_p1 + cw_ref[1:2, :] * dyv + cw_ref[2:3, :] * dyv_m1
        dgc = du * xi
        dxi = du * gc
        dgb = jnp.where(lat, dgb_ref[...], 0.0)
        acc_ref[6:7, 0:CONV_W] += jnp.sum(dyv * um1, axis=0, keepdims=True)
        acc_ref[7:8, 0:CONV_W] += jnp.sum(dyv * u, axis=0, keepdims=True)
        acc_ref[8:9, 0:CONV_W] += jnp.sum(dyv * up1, axis=0, keepdims=True)
        dz = jnp.concatenate([dcq, dckv, dkr, dgb, dgc, dxi], axis=1).astype(BF16)
        dh1 = _dot_nt(dz, win_ref[...])
        slot = i % group
        rows_g = pl.ds(pl.multiple_of(slot * tm, tm), tm)
        h1_buf[rows_g, :] = h1
        dz_buf[rows_g, :] = dz

        @pl.when(jnp.logical_and(lat, slot == group - 1))
        def _():
            dwin_ref[...] += _dot_tn(h1_buf[...], dz_buf[...])

        @pl.when(jnp.logical_not(lat))
        def _():
            dwin_ref[...] += _dot_tn(h1, dz)
        s_sh = jnp.sum(dh1, axis=0, keepdims=True)
        s_sc = jnp.sum(dh1 * xn, axis=0, keepdims=True)
        zero = jnp.zeros_like(s_sh)
        acc_ref[0:1, :] += jnp.where(lat, s_sh, zero)
        acc_ref[1:2, :] += jnp.where(lat, s_sc, zero)
        acc_ref[2:3, :] += jnp.where(lat, zero, s_sh)
        acc_ref[3:4, :] += jnp.where(lat, zero, s_sc)
        dxn = dh1 * (1.0 + sc)
        dx = rstd * (dxn - xn * jnp.mean(dxn * xn, axis=-1, keepdims=True))

        @pl.when(lat)
        def _():
            gx_ref[...] = dx1_ref[...] + dx

    last = n_lat - 1
    per = tm // 8
    lat_row = lambda i: (jnp.minimum(i, last), 0)
    row = lambda i: (i, 0)
    gcp, gcn = _halo_specs(tm, 2)
    xip, xin = _halo_specs(tm, 3)
    n_halo = t_lat // 8
    dyvp = pl.BlockSpec((8, CONV_W), lambda i: (jnp.clip(i * per - 1, 0, n_halo - 1), 0))
    dyvn = pl.BlockSpec((8, CONV_W), lambda i: (jnp.minimum((i + 1) * per, n_halo - 1), 0))
    gcn = pl.BlockSpec((8, CONV_W), lambda i: (jnp.minimum((i + 1) * per, (t_lat + t_ctx) // 8 - 1), 2))
    xin = pl.BlockSpec((8, CONV_W), lambda i: (jnp.minimum((i + 1) * per, (t_lat + t_ctx) // 8 - 1), 3))
    head_f32 = pl.BlockSpec((N_HEADS, tm, HEAD_PAD), lambda i: (0, i, 0))
    tab = pl.BlockSpec((tm, HEAD_PAD), row)
    return pl.pallas_call(
        body, name="inproj_bwd", grid=(n_all,),
        out_shape=(jax.ShapeDtypeStruct((t_lat, D_MODEL), F32), jax.ShapeDtypeStruct(w_in.shape, F32),
                   jax.ShapeDtypeStruct(w_uq.shape, F32), jax.ShapeDtypeStruct(w_ukv.shape, F32),
                   jax.ShapeDtypeStruct((16, D_MODEL), F32)),
        in_specs=[
            pl.BlockSpec((tm, D_MODEL), lat_row), _const_spec((tm, D_MODEL)),
            pl.BlockSpec((1, 8, D_MODEL), lambda i: (i // n_lat, 0, 0)),
            pl.BlockSpec((tm, Z_COLS), row), gcp, xip, gcn, xin,
            pl.BlockSpec((tm, CONV_W), lat_row), dyvp, dyvn,
            pl.BlockSpec((tm, CONV_W), lat_row), pl.BlockSpec((tm, D_MODEL), lat_row),
            pl.BlockSpec((N_HEADS, HEAD_PAD, tm), lambda i: (0, 0, jnp.minimum(i, last))),
            head_f32, head_f32, tab, tab, tab,
            _const_spec(w_in.shape), _const_spec(w_uq.shape), _const_spec(w_ukv.shape),
            _const_spec(qg.shape), _const_spec(kvg.shape), _const_spec(cw.shape),
        ],
        out_specs=(pl.BlockSpec((tm, D_MODEL), lat_row), _const_spec(w_in.shape), _const_spec(w_uq.shape),
                   _const_spec(w_ukv.shape), _const_spec((16, D_MODEL))),
        scratch_shapes=[pltpu.VMEM((group * tm, D_MODEL), BF16), pltpu.VMEM((group * tm, Z_COLS), BF16)],
        compiler_params=pltpu.CompilerParams(vmem_limit_bytes=VMEM_LIMIT),
    )(x2, ctx2, mod_a, z, z, z, z, z, dyv, dyv, dyv, dgb, dx1, dqt, dk, dv, cos_t, sin_a, sin_b, w_in, w_uq, w_ukv,
      qg, kvg, cw)


def _wgrad(a, b, name, bm, bn):
    t, m = a.shape
    n = b.shape[1]
    bk = min(t, 4096)
    nk = t // bk
    nj = n // bn

    def body(a_ref, b_ref, o_ref, acc_ref):
        k = pl.program_id(2)
        part = _dot_tn(a_ref[...], b_ref[...])

        @pl.when(k == 0)
        def _():
            acc_ref[...] = part

        @pl.when(k > 0)
        def _():
            acc_ref[...] += part

        @pl.when(k == nk - 1)
        def _():
            o_ref[...] = acc_ref[...].astype(BF16)

    return pl.pallas_call(
        body, name=name, grid=(m // bm, nj, nk), out_shape=jax.ShapeDtypeStruct((m // bm * nj, bm, bn), BF16),
        in_specs=[pl.BlockSpec((bk, bm), lambda i, j, k: (k, i)), pl.BlockSpec((bk, bn), lambda i, j, k: (k, j))],
        out_specs=pl.BlockSpec((None, bm, bn), lambda i, j, k: (i * nj + j, 0, 0)),
        scratch_shapes=[pltpu.VMEM((bm, bn), F32)],
        compiler_params=pltpu.CompilerParams(vmem_limit_bytes=VMEM_LIMIT),
    )(a, b)


def _wgrad_out(o, conv, dy1, sib_arrays):
    t = o.shape[1]
    bk = min(t, 2048)
    nk = t // bk
    rows = N_HEADS * HEAD_PAD + CONV_W
    n_s = len(sib_arrays)

    def body(o_ref, c_ref, d_ref, *rest):
        w_ref, got_w_ref, acc_ref = rest[n_s], rest[2 * n_s + 1], rest[2 * n_s + 2]
        send = _SiblingSend(rest[:n_s], rest[n_s + 1:2 * n_s + 1], rest[2 * n_s + 3], rest[2 * n_s + 4])
        wsend, wrecv = rest[2 * n_s + 5], rest[2 * n_s + 6]
        k = pl.program_id(0)
        pl.when(k == 0)(send.start)
        cat = jnp.concatenate([o_ref[h] for h in range(N_HEADS)] + [c_ref[...]], axis=1)
        part = _dot_tn(cat, d_ref[...])

        @pl.when(k == 0)
        def _():
            acc_ref[...] = part

        @pl.when(k > 0)
        def _():
            acc_ref[...] += part

        @pl.when(k == nk - 1)
        def _():
            for h in range(N_HEADS):
                w_ref[h * V_DIM:(h + 1) * V_DIM, :] = acc_ref[h * HEAD_PAD:h * HEAD_PAD + V_DIM, :].astype(BF16)
            w_ref[N_HEADS * V_DIM:, :] = acc_ref[N_HEADS * HEAD_PAD:, :].astype(BF16)
            x, y, c = _pos()
            own = []
            for s in range(N_SHARD):
                theirs = pl.ds(pl.multiple_of(s * shard_rows + (1 - c) * (shard_rows // 2), 16), shard_rows // 2)
                cp = pltpu.make_async_remote_copy(
                    src_ref=w_ref.at[theirs], dst_ref=got_w_ref.at[s], send_sem=wsend.at[s], recv_sem=wrecv.at[s],
                    device_id=(x, y, 1 - c), device_id_type=MESH)
                cp.start()
                own.append(cp)
            for cp in own:
                cp.wait()

        pl.when(k == nk - 1)(send.finish)

    hbm = pl.BlockSpec(memory_space=pl.ANY)
    shard_rows = D_MODEL // N_SHARD
    return pl.pallas_call(
        body, name="wgrad_out", grid=(nk,),
        out_shape=(jax.ShapeDtypeStruct((D_MODEL, D_MODEL), BF16),)
        + tuple(jax.ShapeDtypeStruct((N_SHARD, a.shape[1] // 2, a.shape[2]), BF16) for a in sib_arrays)
        + (jax.ShapeDtypeStruct((N_SHARD, shard_rows // 2, D_MODEL), BF16),),
        in_specs=[pl.BlockSpec((N_HEADS, bk, HEAD_PAD), lambda k: (0, k, 0)),
                  pl.BlockSpec((bk, CONV_W), lambda k: (k, 0)),
                  pl.BlockSpec((bk, D_MODEL), lambda k: (k, 0))] + [hbm] * n_s,
        out_specs=(_const_spec((D_MODEL, D_MODEL)),) + (hbm,) * (n_s + 1),
        scratch_shapes=[pltpu.VMEM((rows, D_MODEL), F32), pltpu.SemaphoreType.DMA((n_s,)),
                        pltpu.SemaphoreType.DMA((n_s,)), pltpu.SemaphoreType.DMA((N_SHARD,)),
                        pltpu.SemaphoreType.DMA((N_SHARD,))],
        compiler_params=pltpu.CompilerParams(vmem_limit_bytes=VMEM_LIMIT),
    )(o, conv, dy1, *sib_arrays)


def _adamw_call(w, g, m, v, name):
    rows, cols = w.shape
    rb = 256 if rows % 256 == 0 else rows

    def body(w_ref, g_ref, m_ref, v_ref, d_ref, nm_ref, nv_ref):
        d_, m_, v_ = _adamw(w_ref[...], g_ref[...], m_ref[...], v_ref[...])
        d_ref[...] = d_
        nm_ref[...] = m_
        nv_ref[...] = v_

    spec = pl.BlockSpec((rb, cols), lambda i: (i, 0))
    shp = jax.ShapeDtypeStruct((rows, cols), F32)
    return pl.pallas_call(
        body, name=name, grid=(rows // rb,), out_shape=(shp, shp, shp),
        in_specs=[spec] * 4, out_specs=(spec, spec, spec),
    )(w, g, m, v)


def _adamw_halves(w, g_mine, g_theirs, m, v, c_idx, name):
    rows, cols = w.shape
    half = rows // 2
    rb = min(256, half)
    nb = half // rb

    def body(c_ref, w_ref, gm_ref, gt_ref, m_ref, v_ref, g_ref, d_ref, nm_ref, nv_ref):
        mine = pl.program_id(0) // nb == c_ref[0]
        g = jnp.where(mine, gm_ref[...], gt_ref[...])
        d_, m_, v_ = _adamw(w_ref[...], g, m_ref[...], v_ref[...])
        g_ref[...] = g
        d_ref[...] = d_
        nm_ref[...] = m_
        nv_ref[...] = v_

    spec = pl.BlockSpec((rb, cols), lambda i, c_ref: (i, 0))
    hspec = pl.BlockSpec((rb, cols), lambda i, c_ref: (i % nb, 0))
    shp = jax.ShapeDtypeStruct((rows, cols), F32)
    grid_spec = pltpu.PrefetchScalarGridSpec(
        num_scalar_prefetch=1, grid=(rows // rb,), in_specs=[spec, hspec, hspec, spec, spec],
        out_specs=(spec, spec, spec, spec))
    return pl.pallas_call(
        body, name=name, grid_spec=grid_spec, out_shape=(shp, shp, shp, shp),
    )(c_idx, w, g_mine, g_theirs, m, v)


def _wmod_update(s_t, dm, w, m, v):
    rows, cols = w.shape
    cb = 512

    def body(s_ref, dm_ref, w_ref, m_ref, v_ref, g_ref, d_ref, nm_ref, nv_ref):
        g = jnp.dot(s_ref[...], dm_ref[...], precision=HIGHEST, preferred_element_type=F32)
        d_, m_, v_ = _adamw(w_ref[...], g, m_ref[...], v_ref[...])
        g_ref[...] = g
        d_ref[...] = d_
        nm_ref[...] = m_
        nv_ref[...] = v_

    spec = pl.BlockSpec((rows, cb), lambda i: (0, i))
    shp = jax.ShapeDtypeStruct((rows, cols), F32)
    return pl.pallas_call(
        body, name="wmod_update", grid=(cols // cb,), out_shape=(shp, shp, shp, shp),
        in_specs=[_const_spec(s_t.shape), pl.BlockSpec((16, cb), lambda i: (0, i)), spec, spec, spec],
        out_specs=(spec, spec, spec, spec),
        compiler_params=pltpu.CompilerParams(vmem_limit_bytes=VMEM_LIMIT),
    )(s_t, dm, w, m, v)


def _rope_tables(t_lat, t_ctx):
    t = jnp.arange(t_lat)
    pos = jnp.stack([(t // GRID_W).astype(F32), (t % GRID_W).astype(F32)], axis=1)
    half = QK_ROPE // 4
    freqs = ROPE_THETA ** (-jnp.arange(0, 2 * half, 2, dtype=F32) / (2 * half))
    ang = pos[:, :, None] * freqs[None, None, :]
    cos, sin = jnp.cos(ang), jnp.sin(ang)
    zero = jnp.zeros_like(sin)
    cos32 = jnp.concatenate([cos, cos], axis=2).reshape(t_lat, QK_ROPE)
    sa32 = jnp.concatenate([zero, sin], axis=2).reshape(t_lat, QK_ROPE)
    sb32 = jnp.concatenate([-sin, zero], axis=2).reshape(t_lat, QK_ROPE)

    def widen(tab, fill):
        left = jnp.full((t_lat, ROPE_LANE0), fill, F32)
        right = jnp.full((t_lat, HEAD_PAD - ROPE_LANE0 - QK_ROPE), fill, F32)
        lat = jnp.concatenate([left, tab, right], axis=1)
        return jnp.concatenate([lat, jnp.full((t_ctx, HEAD_PAD), fill, F32)], axis=0)

    return widen(cos32, 1.0), widen(sa32, 0.0), widen(sb32, 0.0)


def _cols_from_shards(s):
    return jnp.transpose(s, (1, 0, 2)).reshape(s.shape[1], -1)


def _cols_to_shards(w):
    k, n = w.shape
    return jnp.transpose(w.reshape(k, N_SHARD, n // N_SHARD), (1, 0, 2))


def kernel(x, c, ctx, c_ctx, w_mod, b_mod, w_in, q_norm_g, w_uq, kv_norm_g, w_ukv, conv_w, w_out, w_mlp1, w_mlp2, final_norm_g, loss_target, m_c_ctx, m_w_mod, m_b_mod, m_w_in, m_q_norm_g, m_w_uq, m_kv_norm_g, m_w_ukv, m_conv_w, m_w_out, m_w_mlp1, m_w_mlp2, m_final_norm_g, v_c_ctx, v_w_mod, v_b_mod, v_w_in, v_q_norm_g, v_w_uq, v_kv_norm_g, v_w_ukv, v_conv_w, v_w_out, v_w_mlp1, v_w_mlp2, v_final_norm_g):
    t_lat, t_ctx = x.shape[1], ctx.shape[1]
    assert t_ctx == TOK_TILE and t_lat % TOK_TILE == 0 and t_lat % GRID_W == 0
    mx, my, mc = _pos()
    j = 2 * mx + my
    ncol = w_mod.shape[2]
    x2, ctx2, tgt = x[0], ctx[0], loss_target[0]
    cctx_row = c_ctx.reshape(1, D_MODEL)

    b_sh = lax.dynamic_slice(b_mod, (0, j * ncol), (1, ncol))
    cw_pad = jnp.zeros((8, 128), F32).at[0:3, :].set(conv_w[0])
    c8, m_all, g_in, g_uq, g_ukv, g_out, g_m1, g_m2 = _prologue(
        c, cctx_row, w_mod[0], b_sh, cw_pad, (w_in[0], w_uq[0], w_ukv[0], w_out[0], w_mlp1[0], w_mlp2[0]), 3)
    mvec = m_all[:, 0, :].reshape(6, D_MODEL)
    mctx = m_all[:, 8, :].reshape(6, D_MODEL)
    zeros6 = jnp.zeros((6, D_MODEL), F32)
    mod_a = jnp.stack([jnp.concatenate([mvec[0:2], zeros6], axis=0), jnp.concatenate([mctx[0:2], zeros6], axis=0)])
    mod_b = jnp.concatenate([mvec[2:6], jnp.zeros((4, D_MODEL), F32)], axis=0)
    cw_full = jnp.pad(jnp.transpose(m_all[:, 9:12, 0:128], (1, 0, 2)).reshape(3, CONV_W), ((0, 5), (0, 0)))

    w_in_f = _cols_from_shards(g_in)
    zc = lambda n: jnp.zeros((D_MODEL, n), BF16)
    w_in_p = jnp.concatenate([w_in_f[:, 0:384], zc(64), w_in_f[:, 384:416], zc(32), w_in_f[:, 416:]], axis=1)
    w_uq_f = _cols_from_shards(g_uq).reshape(Q_RANK, N_HEADS, QK_DIM)
    w_uq_p = jnp.pad(w_uq_f, ((0, 0), (0, 0), (0, HEAD_PAD - QK_DIM))).reshape(Q_RANK, N_HEADS * HEAD_PAD)
    w_ukv_f = _cols_from_shards(g_ukv).reshape(KV_RANK, N_HEADS, QK_NOPE + V_DIM)
    padh = lambda a: jnp.pad(a, ((0, 0), (0, 0), (0, HEAD_PAD - a.shape[2]))).reshape(KV_RANK, N_HEADS * HEAD_PAD)
    w_ukv_p = jnp.concatenate([padh(w_ukv_f[:, :, :QK_NOPE]), padh(w_ukv_f[:, :, QK_NOPE:])], axis=1)
    cos_t, sin_a, sin_b = _rope_tables(t_lat, t_ctx)
    gf_row = final_norm_g.reshape(1, D_MODEL)
    c_idx = mc.reshape(1).astype(jnp.int32)
    j_idx = j.reshape(1).astype(jnp.int32)

    z, q, k, v, kt = _inproj_fwd(x2, ctx2, mod_a, w_in_p, q_norm_g, kv_norm_g, w_uq_p, w_ukv_p, cos_t, sin_a, sin_b)
    o, lse, g_out, w1, g_m2 = _attn_fwd(q, k, v, t_lat, (g_out, g_m1, g_m2))
    w_out_f = g_out.reshape(D_MODEL, D_MODEL)
    wo_attn = jnp.pad(w_out_f[0:512].reshape(N_HEADS, V_DIM, D_MODEL), ((0, 0), (0, HEAD_PAD - V_DIM), (0, 0)))
    wo_conv = w_out_f[512:]
    w2 = g_m2.reshape(D_FF, D_MODEL)
    r, da, h2, dy2, dx1, conv, acc_mlp, dy1, do, dgb, dyv = _mlp_fwdbwd(o, z, x2, tgt, mod_b, gf_row, cw_full, wo_attn,
                                                                         wo_conv, w1, w2)
    d_w1 = _wgrad(h2, da, "wgrad_mlp1", D_MODEL, FF_CHUNK)
    d_w2 = _wgrad(r, dy2, "wgrad_mlp2", FF_CHUNK, D_MODEL)
    d_wout, *big_got = _wgrad_out(o, conv, dy1, (d_w1, d_w2))
    d_wout = d_wout.reshape(N_SHARD, D_MODEL // N_SHARD, D_MODEL)
    big_grads, big_names = (d_w1, d_w2, d_wout), ("mlp1", "mlp2", "out")
    big_parts = [_add_pairs(a, g, c_idx, "rs_add_pairs_" + n) for a, g, n in zip(big_grads, big_got, big_names)]
    dqt, dk, dv, *big_recv = _attn_bwd(q, k, v, kt, o, do, lse, t_lat, big_parts)
    big_halves = [_add_chips(p, g, j_idx, "rs_add_chips_" + n) for p, g, n in zip(big_parts, big_recv, big_names)]
    gx, d_win, d_wuq, d_wukv, acc_in = _inproj_bwd(x2, ctx2, mod_a, z, dyv, dgb, dx1, dqt, dk, dv, cos_t, sin_a, sin_b,
                                                   w_in_p, w_uq_p, w_ukv_p, q_norm_g, kv_norm_g, cw_full)

    d_win_f = jnp.concatenate([d_win[:, 0:384], d_win[:, 448:480], d_win[:, 512:]], axis=1)
    d_wuq_f = d_wuq.reshape(Q_RANK, N_HEADS, HEAD_PAD)[:, :, 0:QK_DIM].reshape(Q_RANK, N_HEADS * QK_DIM)
    d_wukv3 = d_wukv.reshape(KV_RANK, 2, N_HEADS, HEAD_PAD)
    d_wukv_f = jnp.concatenate([d_wukv3[:, 0, :, 0:QK_NOPE], d_wukv3[:, 1, :, 0:V_DIM]], axis=2).reshape(KV_RANK, -1)
    rest = tuple(_cols_to_shards(a).astype(BF16) for a in (d_win_f, d_wuq_f, d_wukv_f))
    rest_names = ("in", "uq", "ukv")
    rest_got = _rs_sibling(rest, "rs_sibling_rest")
    rest_parts = [_add_pairs(a, g, c_idx, "rs_add_pairs_" + n) for a, g, n in zip(rest, rest_got, rest_names)]

    sv = jnp.concatenate([
        acc_in[0:2], acc_mlp[5:6], acc_mlp[2:4], acc_mlp[1:2],
        acc_in[2:4], acc_mlp[0:1], acc_in[4:5], acc_in[5:6], acc_in[6:9], acc_mlp[4:5],
        jnp.zeros((1, D_MODEL), F32)], axis=0)
    all_sv, red, o_cc, o_b, o_q, o_k, o_gf, *exchanged = _small_exchange(
        sv, w_mod[0], cctx_row, m_c_ctx.reshape(1, D_MODEL), v_c_ctx.reshape(1, D_MODEL),
        b_mod.reshape(6, D_MODEL), m_b_mod.reshape(6, D_MODEL), v_b_mod.reshape(6, D_MODEL),
        q_norm_g, m_q_norm_g, v_q_norm_g, kv_norm_g, m_kv_norm_g, v_kv_norm_g,
        gf_row, m_final_norm_g.reshape(1, D_MODEL), v_final_norm_g.reshape(1, D_MODEL), big_halves, rest_parts)
    big_theirs, rest_recv = exchanged[:len(big_halves)], exchanged[len(big_halves):]
    loss = red[14, 0]

    c9 = jnp.concatenate([c8[0::8], jnp.zeros((7, D_MODEL), F32)], axis=0)
    s_t = jnp.transpose(c9 * jax.nn.sigmoid(c9))
    dm_ex = all_sv[:, 0:6, :].reshape(8, 6 * D_MODEL)
    dm_ctx = jnp.concatenate([red[6:8].reshape(1, 2 * D_MODEL), jnp.zeros((1, 4 * D_MODEL), F32)], axis=1)
    dm16 = jnp.concatenate([dm_ex, dm_ctx, jnp.zeros((7, 6 * D_MODEL), F32)], axis=0)
    dm_sh = lax.dynamic_slice(dm16, (0, j * ncol), (16, ncol))
    g_wmod, d_wmod, nm_wmod, nv_wmod = _wmod_update(s_t, dm_sh, w_mod[0], m_w_mod[0], v_w_mod[0])

    g_cw = lax.dynamic_slice(red[11:14, 0:CONV_W], (0, j * 128), (3, 128))
    d_cw, nm_cw, nv_cw = _adamw_call(conv_w[0], g_cw, m_conv_w[0], v_conv_w[0], "adamw_conv")

    rest_halves = [_add_chips(p, g, j_idx, "rs_add_chips_" + n) for p, g, n in zip(rest_parts, rest_recv, rest_names)]
    g_win, g_wuq, g_wukv = _rs_join(rest_halves, "rs_join_rest")
    upd = {}
    for name, w_, g_, m_, v_ in (("in", w_in, g_win, m_w_in, v_w_in), ("uq", w_uq, g_wuq, m_w_uq, v_w_uq),
                                 ("ukv", w_ukv, g_wukv, m_w_ukv, v_w_ukv)):
        upd[name] = _adamw_call(w_[0], g_, m_[0], v_[0], "adamw_" + name)
    g_w1, *upd["mlp1"] = _adamw_halves(w_mlp1[0], big_halves[0], big_theirs[0], m_w_mlp1[0], v_w_mlp1[0], c_idx,
                                       "adamw_mlp1")
    g_w2, *upd["mlp2"] = _adamw_halves(w_mlp2[0], big_halves[1], big_theirs[1], m_w_mlp2[0], v_w_mlp2[0], c_idx,
                                       "adamw_mlp2")
    g_wout, *upd["out"] = _adamw_halves(w_out[0], big_halves[2], big_theirs[2], m_w_out[0], v_w_out[0], c_idx,
                                        "adamw_out")

    def four(o4, shape):
        return [o4[r].reshape(shape) for r in range(4)]

    cc4 = four(o_cc, (D_MODEL,))
    b4 = [o_b[r].reshape(1, 6 * D_MODEL) for r in range(4)]
    q4 = four(o_q, (1, Q_RANK))
    k4 = four(o_k, (1, KV_RANK))
    gf4 = four(o_gf, (D_MODEL,))
    big = {"in": g_win, "uq": g_wuq, "ukv": g_wukv, "out": g_wout, "mlp1": g_w1, "mlp2": g_w2}

    def leaf(idx):
        wm = (g_wmod, d_wmod, nm_wmod, nv_wmod)[idx]
        cwv = (g_cw, d_cw, nm_cw, nv_cw)[idx]
        bigv = {n: (big[n] if idx == 0 else upd[n][idx - 1]) for n in big}
        return [cc4[idx], wm[None], b4[idx], bigv["in"][None], q4[idx], bigv["uq"][None], k4[idx], bigv["ukv"][None],
                cwv[None], bigv["out"][None], bigv["mlp1"][None], bigv["mlp2"][None], gf4[idx]]

    return (loss, gx[None], *leaf(0), *leaf(1), *leaf(2), *leaf(3))
```

```python
import functools
import math

import jax
import jax.numpy as jnp
from jax import lax
from jax.experimental import pallas as pl
from jax.experimental.pallas import tpu as pltpu

F32 = jnp.float32
BF16 = jnp.bfloat16
MESH = pl.DeviceIdType.MESH
HIGHEST = lax.Precision.HIGHEST

D_MODEL = 1024
N_HEADS = 8
QK_NOPE = 64
QK_ROPE = 32
QK_DIM = QK_NOPE + QK_ROPE
V_DIM = 64
Q_RANK = 256
KV_RANK = 128
CONV_W = 512
D_FF = 4096
GRID_W = 64
ROPE_THETA = 10000.0
EPS = 1e-6
ATTN_SCALE = 1.0 / math.sqrt(QK_DIM)
HEAD_PAD = 128
Z_COLS = 2048
ROPE_LANE0 = QK_NOPE
N_SHARD = 4
TOK_TILE = 256
FF_CHUNK = 1024
MLP_FF_CHUNK = 1024
KEY_CHUNK = 512
ATTN_FWD_Q_BLOCK = 1024
ATTN_HEADS_PER_STEP = 8
ATTN_BWD_HEADS_PER_STEP = 2
ATTN_BWD_Q_BLOCK = 1024
KEY_CHUNK_BWD = 256

ADAM_LR = 0.001
ADAM_B1 = 0.9
ADAM_B2 = 0.999
ADAM_EPS = 1e-08
ADAM_WD = 0.01
ADAM_STEP = 10

LOG2E = 1.4426950408889634

VMEM_LIMIT = 56 * 1024 * 1024
STAGE_VMEM_LIMIT = 32 * 1024 * 1024


def _pos():
    return lax.axis_index("x"), lax.axis_index("y"), lax.axis_index("c")


def _dot(a, b):
    return jnp.dot(a, b, preferred_element_type=F32)


def _dot_nt(a, b):
    return lax.dot_general(a, b, (((1,), (1,)), ((), ())), preferred_element_type=F32)


def _dot_tn(a, b):
    return lax.dot_general(a, b, (((0,), (0,)), ((), ())), preferred_element_type=F32)


def _rope(v, cos, sa, sb):
    return v * cos + pltpu.roll(v, 8, 1) * sa + pltpu.roll(v, HEAD_PAD - 8, 1) * sb


def _unrope(g, cos, sa, sb):
    return g * cos + pltpu.roll(g * sa, HEAD_PAD - 8, 1) + pltpu.roll(g * sb, 8, 1)


def _sigmoid(v):
    return 1.0 / (1.0 + jnp.exp(-v))


def _adamw(w, g, m, v):
    m = ADAM_B1 * m + (1.0 - ADAM_B1) * g
    v = ADAM_B2 * v + (1.0 - ADAM_B2) * (g * g)
    m_hat = m / (1.0 - ADAM_B1 ** ADAM_STEP)
    v_hat = v / (1.0 - ADAM_B2 ** ADAM_STEP)
    delta = -ADAM_LR * (m_hat / (jnp.sqrt(v_hat) + ADAM_EPS) + ADAM_WD * w)
    return delta, m, v


def _shift_rows(u, prev_row, next_row):
    n = u.shape[0]
    rows = lax.broadcasted_iota(jnp.int32, u.shape, 0)
    um1 = jnp.where(rows == 0, prev_row, pltpu.roll(u, 1, 0))
    up1 = jnp.where(rows == n - 1, next_row, pltpu.roll(u, n - 1, 0))
    return um1, up1


def _const_spec(shape):
    nd = len(shape)
    return pl.BlockSpec(shape, lambda *_: (0,) * nd)


def _resident_spec(shape):
    nd = len(shape)
    return pl.BlockSpec(shape, lambda *_: (0,) * nd, pipeline_mode=pl.Buffered(1))


def _peer(r, x, y, c):
    px = 1 - x if r & 4 else x
    py = 1 - y if r & 2 else y
    pc = 1 - c if r & 1 else c
    return (px, py, pc)


def _prologue(c_row, cctx_row, w_mod_sh, b_sh, cw_sh, srcs, n_gather):
    ncol = w_mod_sh.shape[1]
    n = len(srcs)
    n_split = 4

    def body(c_ref, cctx_ref, w_ref, b_ref, cw_ref, *refs):
        ins, (c8_ref, m_ref), outs = refs[:n], refs[n:n + 2], refs[n + 2:2 * n + 2]
        mine_ref, msh_ref = refs[2 * n + 2:2 * n + 4]
        f32s, bfs = refs[2 * n + 4:3 * n + 4], refs[3 * n + 4:4 * n + 4]
        ssem, rsem, ssem2, rsem2, lsem_in, lsem_out = refs[4 * n + 4:4 * n + 10]
        x, y, c = _pos()
        me = 4 * x + 2 * y + c
        j = 2 * x + y

        def pieces(rows):
            step = rows // n_split
            return [pl.ds(q * step, step) for q in range(n_split)]

        for t in range(n):
            for sl in pieces(ins[t].shape[0]):
                pltpu.make_async_copy(ins[t].at[sl], f32s[t].at[sl], lsem_in.at[t]).start()
        mine_ref[...] = jnp.zeros(mine_ref.shape, F32)
        mine_ref[0:1, :] = c_ref[...]
        my_rows = pl.ds(pl.multiple_of(8 * me, 8), 8)
        sends = []
        for r in range(1, 8):
            cp = pltpu.make_async_remote_copy(
                src_ref=mine_ref, dst_ref=c8_ref.at[my_rows], send_sem=ssem.at[r - 1], recv_sem=rsem.at[r - 1],
                device_id=_peer(r, x, y, c), device_id_type=MESH)
            cp.start()
            sends.append(cp)

        def cast_and_store(t):
            pltpu.make_async_copy(ins[t], f32s[t], lsem_in.at[t]).wait()
            bfs[t][...] = f32s[t][...].astype(BF16)
            for sl in pieces(ins[t].shape[0]):
                pltpu.make_async_copy(bfs[t].at[sl], outs[t].at[j, sl], lsem_out.at[t]).start()

        gather = _ShardGather(outs[:n_gather], *refs[4 * n + 10:])
        for t in range(n_gather):
            cast_and_store(t)
        for t in range(n_gather):
            pltpu.make_async_copy(bfs[t], outs[t].at[j], lsem_out.at[t]).wait()
        gather.start()
        for cp in sends:
            cp.wait()
        c8_ref[my_rows, :] = mine_ref[...]
        c8_ref[64:72, :] = jnp.zeros((8, D_MODEL), F32)
        c8_ref[64:65, :] = cctx_ref[...]
        cv = c8_ref[...]
        s = cv * _sigmoid(cv)
        m = jnp.dot(s, w_ref[...], precision=HIGHEST, preferred_element_type=F32) + b_ref[...]
        msh_ref[0:64, :] = m[0:64, :]
        msh_ref[64:72, :] = jnp.zeros((8, ncol), F32)
        msh_ref[64:65, :] = m[64:65, :]
        msh_ref[65:68, 0:128] = cw_ref[0:3, :]
        m_ref[j, 0:8, :] = msh_ref[my_rows, :]
        m_ref[j, 8:16, :] = msh_ref[64:72, :]
        sends2 = []
        for k, (px, py) in enumerate(_chips(x, y)):
            theirs = pl.ds(pl.multiple_of(8 * (4 * px + 2 * py + c), 8), 8)
            for half, src in enumerate((msh_ref.at[theirs], msh_ref.at[64:72])):
                cp = pltpu.make_async_remote_copy(
                    src_ref=src, dst_ref=m_ref.at[j, 8 * half:8 * half + 8], send_sem=ssem2.at[2 * k + half],
                    recv_sem=rsem2.at[2 * k + half], device_id=(px, py, c), device_id_type=MESH)
                cp.start()
                sends2.append(cp)
        for t in range(n_gather, n):
            cast_and_store(t)
        gather.forward()
        gather.finish()
        for t in range(n_gather, n):
            pltpu.make_async_copy(bfs[t], outs[t].at[j], lsem_out.at[t]).wait()
        for cp in sends2:
            cp.wait()

    vm = pl.BlockSpec(memory_space=pltpu.VMEM)
    hbm = pl.BlockSpec(memory_space=pl.ANY)
    return pl.pallas_call(
        body, name="prologue",
        out_shape=(jax.ShapeDtypeStruct((72, D_MODEL), F32), jax.ShapeDtypeStruct((N_SHARD, 16, ncol), F32))
        + tuple(jax.ShapeDtypeStruct((N_SHARD,) + a.shape, BF16) for a in srcs),
        in_specs=[vm] * 5 + [hbm] * n, out_specs=(vm, vm) + (hbm,) * n,
        scratch_shapes=[pltpu.VMEM((8, D_MODEL), F32), pltpu.VMEM((72, ncol), F32)]
        + [pltpu.VMEM(a.shape, F32) for a in srcs] + [pltpu.VMEM(a.shape, BF16) for a in srcs]
        + [pltpu.SemaphoreType.DMA((7,)), pltpu.SemaphoreType.DMA((7,)),
           pltpu.SemaphoreType.DMA((6,)), pltpu.SemaphoreType.DMA((6,)),
           pltpu.SemaphoreType.DMA((n,)), pltpu.SemaphoreType.DMA((n,))] + _gather_sems(n_gather),
        compiler_params=pltpu.CompilerParams(vmem_limit_bytes=VMEM_LIMIT),
    )(c_row, cctx_row, w_mod_sh, b_sh, cw_sh, *srcs)


def _chips(x, y):
    return [(1 - x, y), (x, 1 - y), (1 - x, 1 - y)]


def _halves(ref, c, align):
    hr = ref.shape[-2] // 2
    return (pl.ds(pl.multiple_of(c * hr, align), hr), pl.ds(pl.multiple_of((1 - c) * hr, align), hr))


class _ShardGather:
    def __init__(self, refs, ssem, rsem, fsend, frecv):
        self.refs, self.sems = refs, (ssem, rsem, fsend, frecv)
        self.x, self.y, self.c = _pos()
        self.j = 2 * self.x + self.y

    def _ici(self, a, k, slot):
        g = self.refs[a]
        ssem, rsem, _, _ = self.sems
        mine, _ = _halves(g, self.c, 16)
        px, py = _chips(self.x, self.y)[k]
        return pltpu.make_async_remote_copy(
            src_ref=g.at[self.j, mine], dst_ref=g.at[slot, mine], send_sem=ssem.at[3 * a + k],
            recv_sem=rsem.at[3 * a + k], device_id=(px, py, self.c), device_id_type=MESH)

    def _d2d(self, a, k, to_other_half):
        g = self.refs[a]
        _, _, fsend, frecv = self.sems
        mine, theirs = _halves(g, self.c, 16)
        px, py = _chips(self.x, self.y)[k]
        jk = 2 * px + py
        return pltpu.make_async_remote_copy(
            src_ref=g.at[jk, mine], dst_ref=g.at[jk, theirs if to_other_half else mine],
            send_sem=fsend.at[3 * a + k], recv_sem=frecv.at[3 * a + k],
            device_id=(self.x, self.y, 1 - self.c), device_id_type=MESH)

    def start(self):
        for a in range(len(self.refs)):
            for k in range(3):
                self._ici(a, k, self.j).start()

    def forward(self):
        for a in range(len(self.refs)):
            for k, (px, py) in enumerate(_chips(self.x, self.y)):
                self._ici(a, k, 2 * px + py).wait_recv()
                self._d2d(a, k, False).start()

    def finish(self):
        for a in range(len(self.refs)):
            for k in range(3):
                self._d2d(a, k, True).wait()
                self._ici(a, k, self.j).wait_send()


def _gather_sems(n_arrays):
    return [pltpu.SemaphoreType.DMA((3 * n_arrays,)) for _ in range(4)]


class _SiblingSend:
    def __init__(self, g_refs, got_refs, ssem, rsem):
        self.g_refs, self.got_refs, self.ssem, self.rsem = g_refs, got_refs, ssem, rsem
        self.x, self.y, self.c = _pos()

    def _copy(self, a, shard):
        _, theirs = _halves(self.g_refs[a], self.c, 16)
        src = self.g_refs[a].at[:, theirs] if shard is None else self.g_refs[a].at[shard, theirs]
        dst = self.got_refs[a] if shard is None else self.got_refs[a].at[shard]
        return pltpu.make_async_remote_copy(
            src_ref=src, dst_ref=dst, send_sem=self.ssem.at[a], recv_sem=self.rsem.at[a],
            device_id=(self.x, self.y, 1 - self.c), device_id_type=MESH)

    def start(self):
        for a in range(len(self.g_refs)):
            for s in range(N_SHARD):
                self._copy(a, s).start()

    def finish(self):
        for a in range(len(self.g_refs)):
            self._copy(a, None).wait()


class _SiblingSwap:
    def __init__(self, h_refs, t_refs, ssem, rsem):
        self.h_refs, self.t_refs, self.ssem, self.rsem = h_refs, t_refs, ssem, rsem
        self.x, self.y, self.c = _pos()

    def _copy(self, a):
        return pltpu.make_async_remote_copy(
            src_ref=self.h_refs[a], dst_ref=self.t_refs[a], send_sem=self.ssem.at[a], recv_sem=self.rsem.at[a],
            device_id=(self.x, self.y, 1 - self.c), device_id_type=MESH)

    def start(self):
        for a in range(len(self.h_refs)):
            self._copy(a).start()

    def finish(self):
        for a in range(len(self.h_refs)):
            self._copy(a).wait()


def _rs_sibling(arrs, name):
    n = len(arrs)

    def body(*refs):
        send = _SiblingSend(refs[:n], refs[n:2 * n], refs[2 * n], refs[2 * n + 1])
        send.start()
        send.finish()

    hbm = pl.BlockSpec(memory_space=pl.ANY)
    return pl.pallas_call(
        body, name=name,
        out_shape=tuple(jax.ShapeDtypeStruct((N_SHARD, a.shape[1] // 2, a.shape[2]), BF16) for a in arrs),
        in_specs=[hbm] * n, out_specs=(hbm,) * n,
        scratch_shapes=[pltpu.SemaphoreType.DMA((n,)), pltpu.SemaphoreType.DMA((n,))],
    )(*arrs)


class _ChipScatter:
    def __init__(self, parts, gots, ssem, rsem):
        self.parts, self.gots, self.ssem, self.rsem = parts, gots, ssem, rsem
        self.x, self.y, self.c = _pos()

    def _copy(self, a, k):
        px, py = _chips(self.x, self.y)[k]
        return pltpu.make_async_remote_copy(
            src_ref=self.parts[a].at[2 * px + py], dst_ref=self.gots[a].at[k], send_sem=self.ssem.at[3 * a + k],
            recv_sem=self.rsem.at[3 * a + k], device_id=(px, py, self.c), device_id_type=MESH)

    def start(self):
        for a in range(len(self.parts)):
            for k in range(3):
                self._copy(a, k).start()

    def finish(self):
        for a in range(len(self.parts)):
            for k in range(3):
                self._copy(a, k).wait()


def _rs_join(halves, name):
    n = len(halves)

    def body(*refs):
        h_refs, f_refs, stages = refs[:n], refs[n:2 * n], refs[2 * n:3 * n]
        lsem_in, lsem_out, ssem, rsem = refs[3 * n:]
        x, y, c = _pos()
        remote = []
        for a in range(n):
            mine, _ = _halves(f_refs[a], c, 8)
            cp = pltpu.make_async_remote_copy(
                src_ref=h_refs[a], dst_ref=f_refs[a].at[mine], send_sem=ssem.at[a], recv_sem=rsem.at[a],
                device_id=(x, y, 1 - c), device_id_type=MESH)
            cp.start()
            remote.append(cp)
            pltpu.make_async_copy(h_refs[a], stages[a], lsem_in.at[a]).start()
        local = []
        for a in range(n):
            mine, _ = _halves(f_refs[a], c, 8)
            pltpu.make_async_copy(h_refs[a], stages[a], lsem_in.at[a]).wait()
            cp = pltpu.make_async_copy(stages[a], f_refs[a].at[mine], lsem_out.at[a])
            cp.start()
            local.append(cp)
        for cp in remote + local:
            cp.wait()

    hbm = pl.BlockSpec(memory_space=pl.ANY)
    return pl.pallas_call(
        body, name=name,
        out_shape=tuple(jax.ShapeDtypeStruct((2 * h.shape[0], h.shape[1]), F32) for h in halves),
        in_specs=[hbm] * n, out_specs=(hbm,) * n,
        scratch_shapes=[pltpu.VMEM(h.shape, F32) for h in halves]
        + [pltpu.SemaphoreType.DMA((n,)) for _ in range(4)],
        compiler_params=pltpu.CompilerParams(vmem_limit_bytes=STAGE_VMEM_LIMIT),
    )(*halves)


def _row_block(rows):
    return max(b for b in range(16, 513, 16) if rows % b == 0)


def _add_pairs(arr, got, c_idx, name):
    hr, cols = got.shape[1], got.shape[2]
    rb = _row_block(hr)
    nb = hr // rb

    def body(c_ref, a_ref, b_ref, o_ref):
        o_ref[...] = (a_ref[...].astype(F32) + b_ref[...].astype(F32)).astype(BF16)

    spec = pl.BlockSpec((1, rb, cols), lambda s, r, c_ref: (s, r, 0))
    grid_spec = pltpu.PrefetchScalarGridSpec(
        num_scalar_prefetch=1, grid=(N_SHARD, nb),
        in_specs=[pl.BlockSpec((1, rb, cols), lambda s, r, c_ref: (s, c_ref[0] * nb + r, 0)), spec],
        out_specs=spec)
    return pl.pallas_call(
        body, name=name, grid_spec=grid_spec, out_shape=jax.ShapeDtypeStruct(got.shape, BF16),
    )(c_idx, arr, got)


def _add_chips(part, got, j_idx, name):
    hr, cols = got.shape[1], got.shape[2]
    rb = _row_block(hr)

    def body(j_ref, p_ref, g_ref, o_ref):
        acc = p_ref[0].astype(F32)
        for k in range(3):
            acc = acc + g_ref[k].astype(F32)
        o_ref[...] = acc

    grid_spec = pltpu.PrefetchScalarGridSpec(
        num_scalar_prefetch=1, grid=(hr // rb,),
        in_specs=[pl.BlockSpec((1, rb, cols), lambda r, j_ref: (j_ref[0], r, 0)),
                  pl.BlockSpec((3, rb, cols), lambda r, j_ref: (0, r, 0))],
        out_specs=pl.BlockSpec((rb, cols), lambda r, j_ref: (r, 0)))
    return pl.pallas_call(
        body, name=name, grid_spec=grid_spec, out_shape=jax.ShapeDtypeStruct((hr, cols), F32),
    )(j_idx, part, got)


def _small_exchange(sv, w_mod_sh, cctx, m_cctx, v_cctx, bmod, m_bmod, v_bmod, qg, m_qg, v_qg, kvg, m_kvg, v_kvg,
                    gf, m_gf, v_gf, swap_halves, scatter_parts):
    ncol = w_mod_sh.shape[1]
    n_s, n_p = len(swap_halves), len(scatter_parts)
    n_x = n_s + n_p

    def body(sv_ref, w_ref, cctx_ref, mcc_ref, vcc_ref, b_ref, mb_ref, vb_ref, qg_ref, mq_ref, vq_ref,
             kg_ref, mk_ref, vk_ref, gf_ref, mgf_ref, vgf_ref, *rest):
        all_ref, red_ref, occ_ref, ob_ref, oq_ref, ok_ref, ogf_ref = rest[n_x:n_x + 7]
        vec_ref, part_ref, ssem, rsem, ssem2, rsem2, wsend, wrecv, psend, precv = rest[2 * n_x + 7:]
        swap = _SiblingSwap(rest[:n_s], rest[n_x + 7:n_x + 7 + n_s], wsend, wrecv)
        scatter = _ChipScatter(rest[n_s:n_x], rest[n_x + 7 + n_s:2 * n_x + 7], psend, precv)
        swap.start()
        scatter.start()
        x, y, c = _pos()
        me = 4 * x + 2 * y + c
        j = 2 * x + y
        sends = []
        for r in range(1, 8):
            cp = pltpu.make_async_remote_copy(
                src_ref=sv_ref, dst_ref=all_ref.at[me], send_sem=ssem.at[r - 1], recv_sem=rsem.at[r - 1],
                device_id=_peer(r, x, y, c), device_id_type=MESH)
            cp.start()
            sends.append(cp)
        for cp in sends:
            cp.wait()
        all_ref[me] = sv_ref[...]
        red = all_ref[0]
        for d in range(1, 8):
            red = red + all_ref[d]
        red_ref[...] = red
        vec_ref[...] = jnp.zeros(vec_ref.shape, F32)

        @pl.when(j == 0)
        def _():
            vec_ref[0:1, 0:1024] = red[6:7, :]
            vec_ref[0:1, 1024:1536] = red[7:8, 0:512]

        @pl.when(j == 1)
        def _():
            vec_ref[0:1, 0:512] = red[7:8, 512:1024]

        part = lax.dot_general(vec_ref[...], w_ref[...], (((1,), (1,)), ((), ())), precision=HIGHEST,
                               preferred_element_type=F32)
        part_ref[j] = part
        sends2 = []
        for k, r in enumerate((4, 2, 6)):
            cp = pltpu.make_async_remote_copy(
                src_ref=part_ref.at[j], dst_ref=part_ref.at[j], send_sem=ssem2.at[k], recv_sem=rsem2.at[k],
                device_id=_peer(r, x, y, c), device_id_type=MESH)
            cp.start()
            sends2.append(cp)
        for cp in sends2:
            cp.wait()
        tot = part_ref[0]
        for s in range(1, N_SHARD):
            tot = tot + part_ref[s]
        cc = cctx_ref[...]
        sg = _sigmoid(cc)
        g_cc = tot[0:1, :] * (sg * (1.0 + cc * (1.0 - sg)))
        d_, m_, v_ = _adamw(cc, g_cc, mcc_ref[...], vcc_ref[...])
        occ_ref[0:1, :] = g_cc
        occ_ref[1:2, :] = d_
        occ_ref[2:3, :] = m_
        occ_ref[3:4, :] = v_
        occ_ref[4:8, :] = jnp.zeros((4, D_MODEL), F32)
        g_b = red[0:6, :]
        pad = jnp.concatenate([red[6:8, :], jnp.zeros((4, D_MODEL), F32)], axis=0)
        g_b = g_b + pad
        d_, m_, v_ = _adamw(b_ref[...], g_b, mb_ref[...], vb_ref[...])
        ob_ref[0] = g_b
        ob_ref[1] = d_
        ob_ref[2] = m_
        ob_ref[3] = v_
        g_q = red[9:10, 0:Q_RANK]
        d_, m_, v_ = _adamw(qg_ref[...], g_q, mq_ref[...], vq_ref[...])
        oq_ref[0:1, :] = g_q
        oq_ref[1:2, :] = d_
        oq_ref[2:3, :] = m_
        oq_ref[3:4, :] = v_
        oq_ref[4:8, :] = jnp.zeros((4, Q_RANK), F32)
        g_k = red[10:11, 0:KV_RANK]
        d_, m_, v_ = _adamw(kg_ref[...], g_k, mk_ref[...], vk_ref[...])
        ok_ref[0:1, :] = g_k
        ok_ref[1:2, :] = d_
        ok_ref[2:3, :] = m_
        ok_ref[3:4, :] = v_
        ok_ref[4:8, :] = jnp.zeros((4, KV_RANK), F32)
        g_f = red[8:9, :]
        d_, m_, v_ = _adamw(gf_ref[...], g_f, mgf_ref[...], vgf_ref[...])
        ogf_ref[0:1, :] = g_f
        ogf_ref[1:2, :] = d_
        ogf_ref[2:3, :] = m_
        ogf_ref[3:4, :] = v_
        ogf_ref[4:8, :] = jnp.zeros((4, D_MODEL), F32)
        swap.finish()
        scatter.finish()

    vm = pl.BlockSpec(memory_space=pltpu.VMEM)
    hbm = pl.BlockSpec(memory_space=pl.ANY)
    out_shape = (
        jax.ShapeDtypeStruct((8, 16, D_MODEL), F32),
        jax.ShapeDtypeStruct((16, D_MODEL), F32),
        jax.ShapeDtypeStruct((8, D_MODEL), F32),
        jax.ShapeDtypeStruct((4, 6, D_MODEL), F32),
        jax.ShapeDtypeStruct((8, Q_RANK), F32),
        jax.ShapeDtypeStruct((8, KV_RANK), F32),
        jax.ShapeDtypeStruct((8, D_MODEL), F32),
    ) + tuple(jax.ShapeDtypeStruct(h.shape, F32) for h in swap_halves) + tuple(
        jax.ShapeDtypeStruct((3,) + p.shape[1:], BF16) for p in scatter_parts)
    return pl.pallas_call(
        body, name="small_exchange", out_shape=out_shape, in_specs=[vm] * 17 + [hbm] * n_x,
        out_specs=tuple([vm] * 7) + (hbm,) * n_x,
        scratch_shapes=[pltpu.VMEM((8, ncol), F32), pltpu.VMEM((N_SHARD, 8, D_MODEL), F32),
                        pltpu.SemaphoreType.DMA((7,)), pltpu.SemaphoreType.DMA((7,)),
                        pltpu.SemaphoreType.DMA((3,)), pltpu.SemaphoreType.DMA((3,)),
                        pltpu.SemaphoreType.DMA((n_s,)), pltpu.SemaphoreType.DMA((n_s,)),
                        pltpu.SemaphoreType.DMA((3 * n_p,)), pltpu.SemaphoreType.DMA((3 * n_p,))],
        compiler_params=pltpu.CompilerParams(vmem_limit_bytes=VMEM_LIMIT),
    )(sv, w_mod_sh, cctx, m_cctx, v_cctx, bmod, m_bmod, v_bmod, qg, m_qg, v_qg, kvg, m_kvg, v_kvg, gf, m_gf, v_gf,
      *swap_halves, *scatter_parts)


def _inproj_fwd(x2, ctx2, mod_a, w_in, qg, kvg, w_uq, w_ukv, cos_t, sin_a, sin_b):
    t_lat, t_ctx = x2.shape[0], ctx2.shape[0]
    tm = TOK_TILE
    n_lat = t_lat // tm
    n_all = n_lat + t_ctx // tm
    e_rows = t_lat + t_ctx

    def body(x_ref, ctx_ref, mod_ref, win_ref, qg_ref, kvg_ref, wuq_ref, wukv_ref, cos_ref, sa_ref, sb_ref,
             z_ref, q_ref, k_ref, v_ref, kt_ref):
        i = pl.program_id(0)
        xin = jnp.where(i < n_lat, x_ref[...], ctx_ref[...])
        xn = xin * lax.rsqrt(jnp.mean(xin * xin, axis=-1, keepdims=True) + EPS)
        h1 = (xn * (1.0 + mod_ref[0, 1:2, :]) + mod_ref[0, 0:1, :]).astype(BF16)
        z = _dot(h1, win_ref[...])
        z_ref[...] = z
        cos, sa, sb = cos_ref[...], sa_ref[...], sb_ref[...]
        cq = z[:, 0:Q_RANK]
        cqn = (cq * lax.rsqrt(jnp.mean(cq * cq, axis=-1, keepdims=True) + EPS) * qg_ref[...]).astype(BF16)
        q = _dot(cqn, wuq_ref[...])
        ckv = z[:, Q_RANK:Q_RANK + KV_RANK]
        ckvn = (ckv * lax.rsqrt(jnp.mean(ckv * ckv, axis=-1, keepdims=True) + EPS) * kvg_ref[...]).astype(BF16)
        kv = _dot(ckvn, wukv_ref[...])
        kr = _rope(z[:, Q_RANK + KV_RANK:Q_RANK + KV_RANK + HEAD_PAD], cos, sa, sb)
        ones_lane = lax.broadcasted_iota(jnp.int32, (tm, HEAD_PAD), 1) == V_DIM
        for h in range(N_HEADS):
            lo = h * HEAD_PAD
            q_ref[h] = _rope(q[:, lo:lo + HEAD_PAD], cos, sa, sb).astype(BF16)
            kh = kv[:, lo:lo + HEAD_PAD] + kr
            k_ref[h] = kh.astype(BF16)
            kt_ref[h] = kh.T.astype(BF16)
            vh = kv[:, N_HEADS * HEAD_PAD + lo:N_HEADS * HEAD_PAD + lo + HEAD_PAD]
            v_ref[h] = jnp.where(ones_lane, 1.0, vh).astype(BF16)

    row = lambda i: (i, 0)
    head_spec = pl.BlockSpec((N_HEADS, tm, HEAD_PAD), lambda i: (0, i, 0))
    head_shape = jax.ShapeDtypeStruct((N_HEADS, e_rows, HEAD_PAD), BF16)
    return pl.pallas_call(
        body, name="inproj_fwd", grid=(n_all,),
        out_shape=(jax.ShapeDtypeStruct((e_rows, Z_COLS), F32), head_shape, head_shape, head_shape,
                   jax.ShapeDtypeStruct((N_HEADS, HEAD_PAD, e_rows), BF16)),
        in_specs=[
            pl.BlockSpec((tm, D_MODEL), lambda i: (jnp.minimum(i, n_lat - 1), 0)),
            _const_spec((tm, D_MODEL)),
            pl.BlockSpec((1, 8, D_MODEL), lambda i: (i // n_lat, 0, 0)),
            _const_spec(w_in.shape), _const_spec(qg.shape), _const_spec(kvg.shape),
            _const_spec(w_uq.shape), _const_spec(w_ukv.shape),
            pl.BlockSpec((tm, HEAD_PAD), row), pl.BlockSpec((tm, HEAD_PAD), row), pl.BlockSpec((tm, HEAD_PAD), row),
        ],
        out_specs=(pl.BlockSpec((tm, Z_COLS), row), head_spec, head_spec, head_spec,
                   pl.BlockSpec((N_HEADS, HEAD_PAD, tm), lambda i: (0, 0, i))),
        compiler_params=pltpu.CompilerParams(vmem_limit_bytes=VMEM_LIMIT),
    )(x2, ctx2, mod_a, w_in, qg, kvg, w_uq, w_ukv, cos_t, sin_a, sin_b)


def _key_chunks(e_rows, size):
    n_chunks = max(1, e_rows // size)
    return [(ci * size, size if ci < n_chunks - 1 else e_rows - ci * size) for ci in range(n_chunks)]


def _attn_fwd(q, k, v, t_lat, shard_arrays):
    e_rows = k.shape[1]
    tq = min(t_lat, ATTN_FWD_Q_BLOCK)
    bounds = _key_chunks(e_rows, KEY_CHUNK)
    c2 = ATTN_SCALE * LOG2E

    hb = ATTN_HEADS_PER_STEP
    n_hb = N_HEADS // hb

    def body(q_ref, k_ref, v_ref, o_ref, lse_ref):
        qs = [q_ref[b] for b in range(hb)]
        m, acc = [None] * hb, [None] * hb
        for lo, n in bounds:
            for b in range(hb):
                s = _dot_nt(qs[b], k_ref[b, lo:lo + n, :])
                mc = jnp.max(s, axis=-1, keepdims=True)
                m_new = mc if m[b] is None else jnp.maximum(m[b], mc)
                p = jnp.exp2((s - m_new) * c2)
                pv = _dot(p.astype(BF16), v_ref[b, lo:lo + n, :])
                acc[b] = pv if m[b] is None else acc[b] * jnp.exp2((m[b] - m_new) * c2) + pv
                m[b] = m_new
        for b in range(hb):
            l = acc[b][:, V_DIM:V_DIM + 1]
            o_ref[b] = (acc[b] * (1.0 / l)).astype(BF16)
            lse = (m[b] * ATTN_SCALE + jnp.log(l)) * LOG2E
            lse_ref[b] = jnp.broadcast_to(lse, (tq, HEAD_PAD)).T[0:1, :]

    n_w = len(shard_arrays)
    n_q = t_lat // tq

    def body_with_gather(q_ref, k_ref, v_ref, *rest):
        o_ref, lse_ref = rest[n_w], rest[n_w + 1]
        gather = _ShardGather(rest[n_w + 2:2 * n_w + 2], *rest[2 * n_w + 2:])
        step = pl.program_id(0) * n_q + pl.program_id(1)
        pl.when(step == 0)(gather.start)
        pl.when(step == n_hb * n_q // 2)(gather.forward)
        body(q_ref, k_ref, v_ref, o_ref, lse_ref)
        pl.when(step == n_hb * n_q - 1)(gather.finish)

    hbm = pl.BlockSpec(memory_space=pl.ANY)
    return pl.pallas_call(
        body_with_gather, name="attn_fwd", grid=(n_hb, n_q),
        out_shape=(jax.ShapeDtypeStruct((N_HEADS, t_lat, HEAD_PAD), BF16),
                   jax.ShapeDtypeStruct((N_HEADS, 1, t_lat), F32))
        + tuple(jax.ShapeDtypeStruct(a.shape, a.dtype) for a in shard_arrays),
        in_specs=[pl.BlockSpec((hb, tq, HEAD_PAD), lambda h, i: (h, i, 0)),
                  pl.BlockSpec((hb, e_rows, HEAD_PAD), lambda h, i: (h, 0, 0)),
                  pl.BlockSpec((hb, e_rows, HEAD_PAD), lambda h, i: (h, 0, 0))] + [hbm] * n_w,
        out_specs=(pl.BlockSpec((hb, tq, HEAD_PAD), lambda h, i: (h, i, 0)),
                   pl.BlockSpec((hb, 1, tq), lambda h, i: (h, 0, i))) + (hbm,) * n_w,
        input_output_aliases={3 + a: 2 + a for a in range(n_w)},
        scratch_shapes=_gather_sems(n_w),
        compiler_params=pltpu.CompilerParams(vmem_limit_bytes=VMEM_LIMIT),
    )(q, k, v, *shard_arrays)


def _attn_bwd(q, k, v, kt, o, do, lse_row, t_lat, parts):
    e_rows = k.shape[1]
    tq = min(t_lat, ATTN_BWD_Q_BLOCK)
    n_p = len(parts)
    n_q = t_lat // tq
    bounds = _key_chunks(e_rows, KEY_CHUNK_BWD)

    hb = ATTN_BWD_HEADS_PER_STEP
    n_hb = N_HEADS // hb

    def body(q_ref, k_ref, v_ref, kt_ref, o_ref, do_ref, lse_ref, *rest):
        dqt_ref, dk_ref, dv_ref = rest[n_p:n_p + 3]
        scatter = _ChipScatter(rest[:n_p], rest[n_p + 3:2 * n_p + 3], rest[2 * n_p + 3], rest[2 * n_p + 4])
        h, i = pl.program_id(0), pl.program_id(1)
        pl.when(jnp.logical_and(h == 0, i == 0))(scatter.start)

        @pl.when(i == 0)
        def _():
            dk_ref[...] = jnp.zeros(dk_ref.shape, F32)
            dv_ref[...] = jnp.zeros(dv_ref.shape, F32)

        qs, dos, lses, deltas = [], [], [], []
        for b in range(hb):
            qs.append(q_ref[b])
            dos.append(do_ref[b])
            lses.append(lse_ref[b])
            prod = o_ref[b].astype(F32) * dos[b].astype(F32)
            deltas.append(lax.dot_general(jnp.ones((8, HEAD_PAD), F32), prod, (((1,), (1,)), ((), ())),
                                          precision=HIGHEST, preferred_element_type=F32)[0:1, :])
        dqt = [None] * hb
        for lo, n in bounds:
            for b in range(hb):
                pt = jnp.exp2(_dot_nt(k_ref[b, lo:lo + n, :], qs[b]) * (ATTN_SCALE * LOG2E) - lses[b])
                dpt = _dot_nt(v_ref[b, lo:lo + n, :], dos[b])
                dst = (pt * (dpt - deltas[b])).astype(BF16)
                dv_c = _dot(pt.astype(BF16), dos[b])
                dk_c = _dot(dst, qs[b])
                part = _dot(kt_ref[b, :, lo:lo + n], dst)
                dqt[b] = part if dqt[b] is None else dqt[b] + part
                dk_ref[b, lo:lo + n, :] += dk_c * ATTN_SCALE
                dv_ref[b, lo:lo + n, :] += dv_c
        for b in range(hb):
            dqt_ref[b] = dqt[b] * ATTN_SCALE

        pl.when(jnp.logical_and(h == n_hb - 1, i == n_q - 1))(scatter.finish)

    hbm = pl.BlockSpec(memory_space=pl.ANY)
    qspec = pl.BlockSpec((hb, tq, HEAD_PAD), lambda h, i: (h, i, 0))
    kspec = pl.BlockSpec((hb, e_rows, HEAD_PAD), lambda h, i: (h, 0, 0))
    return pl.pallas_call(
        body, name="attn_bwd", grid=(n_hb, n_q),
        out_shape=(jax.ShapeDtypeStruct((N_HEADS, HEAD_PAD, t_lat), F32),
                   jax.ShapeDtypeStruct((N_HEADS, e_rows, HEAD_PAD), F32),
                   jax.ShapeDtypeStruct((N_HEADS, e_rows, HEAD_PAD), F32))
        + tuple(jax.ShapeDtypeStruct((3,) + p.shape[1:], BF16) for p in parts),
        in_specs=[qspec, kspec, kspec, pl.BlockSpec((hb, HEAD_PAD, e_rows), lambda h, i: (h, 0, 0)), qspec, qspec,
                  pl.BlockSpec((hb, 1, tq), lambda h, i: (h, 0, i))] + [hbm] * n_p,
        out_specs=(pl.BlockSpec((hb, HEAD_PAD, tq), lambda h, i: (h, 0, i)), kspec, kspec) + (hbm,) * n_p,
        scratch_shapes=[pltpu.SemaphoreType.DMA((3 * n_p,)), pltpu.SemaphoreType.DMA((3 * n_p,))],
        compiler_params=pltpu.CompilerParams(vmem_limit_bytes=VMEM_LIMIT),
    )(q, k, v, kt, o, do, lse_row, *parts)


def _halo_specs(tm, col_block):
    per = tm // 8
    prev = pl.BlockSpec((8, CONV_W), lambda i: (jnp.maximum(i * per - 1, 0), col_block))
    nxt = pl.BlockSpec((8, CONV_W), lambda i: ((i + 1) * per, col_block))
    return prev, nxt


def _mlp_fwdbwd(o, z, x2, tgt, mod_b, gf, cw, wo_attn, wo_conv, w1, w2):
    t_lat = x2.shape[0]
    tm = TOK_TILE
    n_lat = t_lat // tm
    fc = MLP_FF_CHUNK
    n_ff = D_FF // fc

    def body(o_ref, gb_ref, gc_ref, xi_ref, gcp_ref, xip_ref, gcn_ref, xin_ref, cw_ref, woa_ref, woc_ref,
             x_ref, t_ref, mod_ref, gf_ref, w1_ref, w2_ref,
             r_ref, da_ref, h2_ref, dy2_ref, dx1_ref, conv_ref, acc_ref, dy1_ref, do_ref, dgb_ref, dyv_ref, ra_ref):
        i = pl.program_id(0)

        @pl.when(i == 0)
        def _():
            acc_ref[...] = jnp.zeros(acc_ref.shape, F32)

        g1, sh2, sc2, g2 = mod_ref[0:1, :], mod_ref[1:2, :], mod_ref[2:3, :], mod_ref[3:4, :]
        u = gc_ref[...] * xi_ref[...]
        u_prev = jnp.where(i > 0, gcp_ref[7:8, :] * xip_ref[7:8, :], 0.0)
        u_next = jnp.where(i < n_lat - 1, gcn_ref[0:1, :] * xin_ref[0:1, :], 0.0)
        um1, up1 = _shift_rows(u, u_prev, u_next)
        yv = cw_ref[0:1, :] * um1 + cw_ref[1:2, :] * u + cw_ref[2:3, :] * up1
        gb = gb_ref[...]
        conv = (gb * yv).astype(BF16)
        conv_ref[...] = conv
        y1 = _dot(conv, woc_ref[...])
        for h in range(N_HEADS):
            y1 = y1 + _dot(o_ref[h], woa_ref[h])
        x1 = x_ref[...] + g1 * y1
        rstd2 = lax.rsqrt(jnp.mean(x1 * x1, axis=-1, keepdims=True) + EPS)
        xn1 = x1 * rstd2
        h2 = (xn1 * (1.0 + sc2) + sh2).astype(BF16)
        h2_ref[...] = h2
        y2 = jnp.zeros((tm, D_MODEL), F32)
        for jj in range(n_ff):
            lo = jj * fc
            ra = jnp.maximum(_dot(h2, w1_ref[lo // FF_CHUNK, :, lo % FF_CHUNK:lo % FF_CHUNK + fc]), 0.0)
            ra_ref[jj] = ra
            r = (ra * ra).astype(BF16)
            r_ref[:, lo:lo + fc] = r
            y2 = y2 + _dot(r, w2_ref[lo:lo + fc, :])
        x2v = x1 + g2 * y2
        rstd3 = lax.rsqrt(jnp.mean(x2v * x2v, axis=-1, keepdims=True) + EPS)
        xn3 = x2v * rstd3
        gfv = gf_ref[...]
        diff = xn3 * gfv - t_ref[...]
        loss_t = 0.5 * jnp.sum(jnp.sum(diff * diff, axis=-1, keepdims=True), axis=0, keepdims=True) * (1.0 / D_MODEL)
        dy = diff * (1.0 / D_MODEL)
        dxn3 = dy * gfv
        dx2 = rstd3 * (dxn3 - xn3 * jnp.mean(dxn3 * xn3, axis=-1, keepdims=True))
        dy2 = (dx2 * g2).astype(BF16)
        dy2_ref[...] = dy2
        dh2 = jnp.zeros((tm, D_MODEL), F32)
        for jj in range(n_ff):
            lo = jj * fc
            dr = _dot_nt(dy2, w2_ref[lo:lo + fc, :])
            da = (2.0 * ra_ref[jj] * dr).astype(BF16)
            da_ref[:, lo:lo + fc] = da
            dh2 = dh2 + _dot_nt(da, w1_ref[lo // FF_CHUNK, :, lo % FF_CHUNK:lo % FF_CHUNK + fc])
        dxn1 = dh2 * (1.0 + sc2)
        dx1 = dx2 + rstd2 * (dxn1 - xn1 * jnp.mean(dxn1 * xn1, axis=-1, keepdims=True))
        dx1_ref[...] = dx1
        dy1 = (dx1 * g1).astype(BF16)
        dy1_ref[...] = dy1
        for h in range(N_HEADS):
            do_ref[h] = _dot_nt(dy1, woa_ref[h]).astype(BF16)
        dconv = _dot_nt(dy1, woc_ref[...])
        dgb_ref[...] = dconv * yv
        dyv_ref[...] = dconv * gb
        acc_ref[5:6, :] += jnp.sum(dx1 * y1, axis=0, keepdims=True)
        acc_ref[0:1, :] += jnp.sum(dy * xn3, axis=0, keepdims=True)
        acc_ref[1:2, :] += jnp.sum(dx2 * y2, axis=0, keepdims=True)
        acc_ref[2:3, :] += jnp.sum(dh2, axis=0, keepdims=True)
        acc_ref[3:4, :] += jnp.sum(dh2 * xn1, axis=0, keepdims=True)
        acc_ref[4:5, :] += jnp.broadcast_to(loss_t, (1, D_MODEL))

    row = lambda i: (i, 0)
    gcp, gcn = _halo_specs(tm, 2)
    xip, xin = _halo_specs(tm, 3)
    tile = pl.BlockSpec((tm, D_MODEL), row)
    wide = pl.BlockSpec((tm, D_FF), row)
    half = pl.BlockSpec((tm, CONV_W), row)
    return pl.pallas_call(
        body, name="mlp_fwdbwd", grid=(n_lat,),
        out_shape=(jax.ShapeDtypeStruct((t_lat, D_FF), BF16), jax.ShapeDtypeStruct((t_lat, D_FF), BF16),
                   jax.ShapeDtypeStruct((t_lat, D_MODEL), BF16), jax.ShapeDtypeStruct((t_lat, D_MODEL), BF16),
                   jax.ShapeDtypeStruct((t_lat, D_MODEL), F32), jax.ShapeDtypeStruct((t_lat, CONV_W), BF16),
                   jax.ShapeDtypeStruct((8, D_MODEL), F32),
                   jax.ShapeDtypeStruct((t_lat, D_MODEL), BF16),
                   jax.ShapeDtypeStruct((N_HEADS, t_lat, HEAD_PAD), BF16),
                   jax.ShapeDtypeStruct((t_lat, CONV_W), F32), jax.ShapeDtypeStruct((t_lat, CONV_W), F32)),
        in_specs=[
            pl.BlockSpec((N_HEADS, tm, HEAD_PAD), lambda i: (0, i, 0)),
            pl.BlockSpec((tm, CONV_W), lambda i: (i, 1)), pl.BlockSpec((tm, CONV_W), lambda i: (i, 2)),
            pl.BlockSpec((tm, CONV_W), lambda i: (i, 3)),
            gcp, xip, gcn, xin,
            _const_spec(cw.shape), _resident_spec(wo_attn.shape), _resident_spec(wo_conv.shape),
            tile, tile, _const_spec(mod_b.shape), _const_spec(gf.shape),
            _resident_spec(w1.shape), _resident_spec(w2.shape),
        ],
        out_specs=(wide, wide, tile, tile, tile, half, _const_spec((8, D_MODEL)),
                   tile, pl.BlockSpec((N_HEADS, tm, HEAD_PAD), lambda i: (0, i, 0)), half, half),
        scratch_shapes=[pltpu.VMEM((n_ff, tm, fc), F32)],
        compiler_params=pltpu.CompilerParams(vmem_limit_bytes=VMEM_LIMIT),
    )(o, z, z, z, z, z, z, z, cw, wo_attn, wo_conv, x2, tgt, mod_b, gf, w1, w2)


def _inproj_bwd(x2, ctx2, mod_a, z, dyv, dgb, dx1, dqt, dk, dv, cos_t, sin_a, sin_b, w_in, w_uq, w_ukv, qg, kvg, cw):
    t_lat, t_ctx = x2.shape[0], ctx2.shape[0]
    tm = TOK_TILE
    n_lat = t_lat // tm
    n_all = n_lat + t_ctx // tm
    group = max(g for g in (1, 2, 4) if n_lat % g == 0)

    def body(x_ref, ctx_ref, mod_ref, z_ref, gcp_ref, xip_ref, gcn_ref, xin_ref, dyv_ref, dyvp_ref, dyvn_ref,
             dgb_ref, dx1_ref, dqt_ref, dk_ref, dv_ref, cos_ref, sa_ref, sb_ref, win_ref, wuq_ref, wukv_ref,
             qg_ref, kvg_ref, cw_ref, gx_ref, dwin_ref, dwuq_ref, dwukv_ref, acc_ref, h1_buf, dz_buf):
        i = pl.program_id(0)
        lat = i < n_lat

        @pl.when(i == 0)
        def _():
            dwin_ref[...] = jnp.zeros(dwin_ref.shape, F32)
            dwuq_ref[...] = jnp.zeros(dwuq_ref.shape, F32)
            dwukv_ref[...] = jnp.zeros(dwukv_ref.shape, F32)
            acc_ref[...] = jnp.zeros(acc_ref.shape, F32)

        xin = jnp.where(lat, x_ref[...], ctx_ref[...])
        rstd = lax.rsqrt(jnp.mean(xin * xin, axis=-1, keepdims=True) + EPS)
        xn = xin * rstd
        sc = mod_ref[0, 1:2, :]
        h1 = (xn * (1.0 + sc) + mod_ref[0, 0:1, :]).astype(BF16)
        z = z_ref[...]
        cos, sa, sb = cos_ref[...], sa_ref[...], sb_ref[...]
        qgv, kvgv = qg_ref[...], kvg_ref[...]
        cq = z[:, 0:Q_RANK]
        cqh = cq * lax.rsqrt(jnp.mean(cq * cq, axis=-1, keepdims=True) + EPS)
        rq = lax.rsqrt(jnp.mean(cq * cq, axis=-1, keepdims=True) + EPS)
        cqn = (cqh * qgv).astype(BF16)
        parts = []
        for h in range(N_HEADS):
            g = jnp.where(lat, dqt_ref[h].T, 0.0)
            parts.append(_unrope(g, cos, sa, sb))
        dq = jnp.concatenate(parts, axis=1).astype(BF16)
        dcqn = _dot_nt(dq, wuq_ref[...])
        dwuq_ref[...] += _dot_tn(cqn, dq)
        acc_ref[4:5, 0:Q_RANK] += jnp.sum(dcqn * cqh, axis=0, keepdims=True)
        dxn = dcqn * qgv
        dcq = rq * (dxn - cqh * jnp.mean(dxn * cqh, axis=-1, keepdims=True))
        ckv = z[:, Q_RANK:Q_RANK + KV_RANK]
        rk = lax.rsqrt(jnp.mean(ckv * ckv, axis=-1, keepdims=True) + EPS)
        ckvh = ckv * rk
        ckvn = (ckvh * kvgv).astype(BF16)
        dks = [dk_ref[h] for h in range(N_HEADS)]
        dkr = dks[0]
        for h in range(1, N_HEADS):
            dkr = dkr + dks[h]
        dkv = jnp.concatenate(dks + [dv_ref[h] for h in range(N_HEADS)], axis=1).astype(BF16)
        dckvn = _dot_nt(dkv, wukv_ref[...])
        dwukv_ref[...] += _dot_tn(ckvn, dkv)
        acc_ref[5:6, 0:KV_RANK] += jnp.sum(dckvn * ckvh, axis=0, keepdims=True)
        dxn = dckvn * kvgv
        dckv = rk * (dxn - ckvh * jnp.mean(dxn * ckvh, axis=-1, keepdims=True))
        dkr = _unrope(dkr, cos, sa, sb)
        gb, gc, xi = z[:, 512:1024], z[:, 1024:1536], z[:, 1536:2048]
        u = gc * xi
        u_prev = jnp.where(i > 0, gcp_ref[7:8, :] * xip_ref[7:8, :], 0.0)
        u_next = jnp.where(i < n_lat - 1, gcn_ref[0:1, :] * xin_ref[0:1, :], 0.0)
        um1, up1 = _shift_rows(u, u_prev, u_next)
        dyv = jnp.where(lat, dyv_ref[...], 0.0)
        dyv_prev = jnp.where(jnp.logical_and(i > 0, lat), dyvp_ref[7:8, :], 0.0)
        dyv_next = jnp.where(i < n_lat - 1, dyvn_ref[0:1, :], 0.0)
        dyv_m1, dyv_p1 = _shift_rows(dyv, dyv_prev, dyv_next)
        du = cw_ref[0:1, :] * dyv_p1 + cw_ref[1:2, :] * dyv + cw_ref[2:3, :] * dyv_m1
        dgc = du * xi
        dxi = du * gc
        dgb = jnp.where(lat, dgb_ref[...], 0.0)
        acc_ref[6:7, 0:CONV_W] += jnp.sum(dyv * um1, axis=0, keepdims=True)
        acc_ref[7:8, 0:CONV_W] += jnp.sum(dyv * u, axis=0, keepdims=True)
        acc_ref[8:9, 0:CONV_W] += jnp.sum(dyv * up1, axis=0, keepdims=True)
        dz = jnp.concatenate([dcq, dckv, dkr, dgb, dgc, dxi], axis=1).astype(BF16)
        dh1 = _dot_nt(dz, win_ref[...])
        slot = i % group
        rows_g = pl.ds(pl.multiple_of(slot * tm, tm), tm)
        h1_buf[rows_g, :] = h1
        dz_buf[rows_g, :] = dz

        @pl.when(jnp.logical_and(lat, slot == group - 1))
        def _():
            dwin_ref[...] += _dot_tn(h1_buf[...], dz_buf[...])

        @pl.when(jnp.logical_not(lat))
        def _():
            dwin_ref[...] += _dot_tn(h1, dz)
        s_sh = jnp.sum(dh1, axis=0, keepdims=True)
        s_sc = jnp.sum(dh1 * xn, axis=0, keepdims=True)
        zero = jnp.zeros_like(s_sh)
        acc_ref[0:1, :] += jnp.where(lat, s_sh, zero)
        acc_ref[1:2, :] += jnp.where(lat, s_sc, zero)
        acc_ref[2:3, :] += jnp.where(lat, zero, s_sh)
        acc_ref[3:4, :] += jnp.where(lat, zero, s_sc)
        dxn = dh1 * (1.0 + sc)
        dx = rstd * (dxn - xn * jnp.mean(dxn * xn, axis=-1, keepdims=True))

        @pl.when(lat)
        def _():
            gx_ref[...] = dx1_ref[...] + dx

    last = n_lat - 1
    per = tm // 8
    lat_row = lambda i: (jnp.minimum(i, last), 0)
    row = lambda i: (i, 0)
    gcp, gcn = _halo_specs(tm, 2)
    xip, xin = _halo_specs(tm, 3)
    n_halo = t_lat // 8
    dyvp = pl.BlockSpec((8, CONV_W), lambda i: (jnp.clip(i * per - 1, 0, n_halo - 1), 0))
    dyvn = pl.BlockSpec((8, CONV_W), lambda i: (jnp.minimum((i + 1) * per, n_halo - 1), 0))
    gcn = pl.BlockSpec((8, CONV_W), lambda i: (jnp.minimum((i + 1) * per, (t_lat + t_ctx) // 8 - 1), 2))
    xin = pl.BlockSpec((8, CONV_W), lambda i: (jnp.minimum((i + 1) * per, (t_lat + t_ctx) // 8 - 1), 3))
    head_f32 = pl.BlockSpec((N_HEADS, tm, HEAD_PAD), lambda i: (0, i, 0))
    tab = pl.BlockSpec((tm, HEAD_PAD), row)
    return pl.pallas_call(
        body, name="inproj_bwd", grid=(n_all,),
        out_shape=(jax.ShapeDtypeStruct((t_lat, D_MODEL), F32), jax.ShapeDtypeStruct(w_in.shape, F32),
                   jax.ShapeDtypeStruct(w_uq.shape, F32), jax.ShapeDtypeStruct(w_ukv.shape, F32),
                   jax.ShapeDtypeStruct((16, D_MODEL), F32)),
        in_specs=[
            pl.BlockSpec((tm, D_MODEL), lat_row), _const_spec((tm, D_MODEL)),
            pl.BlockSpec((1, 8, D_MODEL), lambda i: (i // n_lat, 0, 0)),
            pl.BlockSpec((tm, Z_COLS), row), gcp, xip, gcn, xin,
            pl.BlockSpec((tm, CONV_W), lat_row), dyvp, dyvn,
            pl.BlockSpec((tm, CONV_W), lat_row), pl.BlockSpec((tm, D_MODEL), lat_row),
            pl.BlockSpec((N_HEADS, HEAD_PAD, tm), lambda i: (0, 0, jnp.minimum(i, last))),
            head_f32, head_f32, tab, tab, tab,
            _const_spec(w_in.shape), _const_spec(w_uq.shape), _const_spec(w_ukv.shape),
            _const_spec(qg.shape), _const_spec(kvg.shape), _const_spec(cw.shape),
        ],
        out_specs=(pl.BlockSpec((tm, D_MODEL), lat_row), _const_spec(w_in.shape), _const_spec(w_uq.shape),
                   _const_spec(w_ukv.shape), _const_spec((16, D_MODEL))),
        scratch_shapes=[pltpu.VMEM((group * tm, D_MODEL), BF16), pltpu.VMEM((group * tm, Z_COLS), BF16)],
        compiler_params=pltpu.CompilerParams(vmem_limit_bytes=VMEM_LIMIT),
    )(x2, ctx2, mod_a, z, z, z, z, z, dyv, dyv, dyv, dgb, dx1, dqt, dk, dv, cos_t, sin_a, sin_b, w_in, w_uq, w_ukv,
      qg, kvg, cw)


def _wgrad(a, b, name, bm, bn):
    t, m = a.shape
    n = b.shape[1]
    bk = min(t, 4096)
    nk = t // bk
    nj = n // bn

    def body(a_ref, b_ref, o_ref, acc_ref):
        k = pl.program_id(2)
        part = _dot_tn(a_ref[...], b_ref[...])

        @pl.when(k == 0)
        def _():
            acc_ref[...] = part

        @pl.when(k > 0)
        def _():
            acc_ref[...] += part

        @pl.when(k == nk - 1)
        def _():
            o_ref[...] = acc_ref[...].astype(BF16)

    return pl.pallas_call(
        body, name=name, grid=(m // bm, nj, nk), out_shape=jax.ShapeDtypeStruct((m // bm * nj, bm, bn), BF16),
        in_specs=[pl.BlockSpec((bk, bm), lambda i, j, k: (k, i)), pl.BlockSpec((bk, bn), lambda i, j, k: (k, j))],
        out_specs=pl.BlockSpec((None, bm, bn), lambda i, j, k: (i * nj + j, 0, 0)),
        scratch_shapes=[pltpu.VMEM((bm, bn), F32)],
        compiler_params=pltpu.CompilerParams(vmem_limit_bytes=VMEM_LIMIT),
    )(a, b)


def _wgrad_out(o, conv, dy1, sib_arrays):
    t = o.shape[1]
    bk = min(t, 2048)
    nk = t // bk
    rows = N_HEADS * HEAD_PAD + CONV_W
    n_s = len(sib_arrays)

    def body(o_ref, c_ref, d_ref, *rest):
        w_ref, got_w_ref, acc_ref = rest[n_s], rest[2 * n_s + 1], rest[2 * n_s + 2]
        send = _SiblingSend(rest[:n_s], rest[n_s + 1:2 * n_s + 1], rest[2 * n_s + 3], rest[2 * n_s + 4])
        wsend, wrecv = rest[2 * n_s + 5], rest[2 * n_s + 6]
        k = pl.program_id(0)
        pl.when(k == 0)(send.start)
        cat = jnp.concatenate([o_ref[h] for h in range(N_HEADS)] + [c_ref[...]], axis=1)
        part = _dot_tn(cat, d_ref[...])

        @pl.when(k == 0)
        def _():
            acc_ref[...] = part

        @pl.when(k > 0)
        def _():
            acc_ref[...] += part

        @pl.when(k == nk - 1)
        def _():
            for h in range(N_HEADS):
                w_ref[h * V_DIM:(h + 1) * V_DIM, :] = acc_ref[h * HEAD_PAD:h * HEAD_PAD + V_DIM, :].astype(BF16)
            w_ref[N_HEADS * V_DIM:, :] = acc_ref[N_HEADS * HEAD_PAD:, :].astype(BF16)
            x, y, c = _pos()
            own = []
            for s in range(N_SHARD):
                theirs = pl.ds(pl.multiple_of(s * shard_rows + (1 - c) * (shard_rows // 2), 16), shard_rows // 2)
                cp = pltpu.make_async_remote_copy(
                    src_ref=w_ref.at[theirs], dst_ref=got_w_ref.at[s], send_sem=wsend.at[s], recv_sem=wrecv.at[s],
                    device_id=(x, y, 1 - c), device_id_type=MESH)
                cp.start()
                own.append(cp)
            for cp in own:
                cp.wait()

        pl.when(k == nk - 1)(send.finish)

    hbm = pl.BlockSpec(memory_space=pl.ANY)
    shard_rows = D_MODEL // N_SHARD
    return pl.pallas_call(
        body, name="wgrad_out", grid=(nk,),
        out_shape=(jax.ShapeDtypeStruct((D_MODEL, D_MODEL), BF16),)
        + tuple(jax.ShapeDtypeStruct((N_SHARD, a.shape[1] // 2, a.shape[2]), BF16) for a in sib_arrays)
        + (jax.ShapeDtypeStruct((N_SHARD, shard_rows // 2, D_MODEL), BF16),),
        in_specs=[pl.BlockSpec((N_HEADS, bk, HEAD_PAD), lambda k: (0, k, 0)),
                  pl.BlockSpec((bk, CONV_W), lambda k: (k, 0)),
                  pl.BlockSpec((bk, D_MODEL), lambda k: (k, 0))] + [hbm] * n_s,
        out_specs=(_const_spec((D_MODEL, D_MODEL)),) + (hbm,) * (n_s + 1),
        scratch_shapes=[pltpu.VMEM((rows, D_MODEL), F32), pltpu.SemaphoreType.DMA((n_s,)),
                        pltpu.SemaphoreType.DMA((n_s,)), pltpu.SemaphoreType.DMA((N_SHARD,)),
                        pltpu.SemaphoreType.DMA((N_SHARD,))],
        compiler_params=pltpu.CompilerParams(vmem_limit_bytes=VMEM_LIMIT),
    )(o, conv, dy1, *sib_arrays)


def _adamw_call(w, g, m, v, name):
    rows, cols = w.shape
    rb = 256 if rows % 256 == 0 else rows

    def body(w_ref, g_ref, m_ref, v_ref, d_ref, nm_ref, nv_ref):
        d_, m_, v_ = _adamw(w_ref[...], g_ref[...], m_ref[...], v_ref[...])
        d_ref[...] = d_
        nm_ref[...] = m_
        nv_ref[...] = v_

    spec = pl.BlockSpec((rb, cols), lambda i: (i, 0))
    shp = jax.ShapeDtypeStruct((rows, cols), F32)
    return pl.pallas_call(
        body, name=name, grid=(rows // rb,), out_shape=(shp, shp, shp),
        in_specs=[spec] * 4, out_specs=(spec, spec, spec),
    )(w, g, m, v)


def _adamw_halves(w, g_mine, g_theirs, m, v, c_idx, name):
    rows, cols = w.shape
    half = rows // 2
    rb = min(256, half)
    nb = half // rb

    def body(c_ref, w_ref, gm_ref, gt_ref, m_ref, v_ref, g_ref, d_ref, nm_ref, nv_ref):
        mine = pl.program_id(0) // nb == c_ref[0]
        g = jnp.where(mine, gm_ref[...], gt_ref[...])
        d_, m_, v_ = _adamw(w_ref[...], g, m_ref[...], v_ref[...])
        g_ref[...] = g
        d_ref[...] = d_
        nm_ref[...] = m_
        nv_ref[...] = v_

    spec = pl.BlockSpec((rb, cols), lambda i, c_ref: (i, 0))
    hspec = pl.BlockSpec((rb, cols), lambda i, c_ref: (i % nb, 0))
    shp = jax.ShapeDtypeStruct((rows, cols), F32)
    grid_spec = pltpu.PrefetchScalarGridSpec(
        num_scalar_prefetch=1, grid=(rows // rb,), in_specs=[spec, hspec, hspec, spec, spec],
        out_specs=(spec, spec, spec, spec))
    return pl.pallas_call(
        body, name=name, grid_spec=grid_spec, out_shape=(shp, shp, shp, shp),
    )(c_idx, w, g_mine, g_theirs, m, v)


def _wmod_update(s_t, dm, w, m, v):
    rows, cols = w.shape
    cb = 512

    def body(s_ref, dm_ref, w_ref, m_ref, v_ref, g_ref, d_ref, nm_ref, nv_ref):
        g = jnp.dot(s_ref[...], dm_ref[...], precision=HIGHEST, preferred_element_type=F32)
        d_, m_, v_ = _adamw(w_ref[...], g, m_ref[...], v_ref[...])
        g_ref[...] = g
        d_ref[...] = d_
        nm_ref[...] = m_
        nv_ref[...] = v_

    spec = pl.BlockSpec((rows, cb), lambda i: (0, i))
    shp = jax.ShapeDtypeStruct((rows, cols), F32)
    return pl.pallas_call(
        body, name="wmod_update", grid=(cols // cb,), out_shape=(shp, shp, shp, shp),
        in_specs=[_const_spec(s_t.shape), pl.BlockSpec((16, cb), lambda i: (0, i)), spec, spec, spec],
        out_specs=(spec, spec, spec, spec),
        compiler_params=pltpu.CompilerParams(vmem_limit_bytes=VMEM_LIMIT),
    )(s_t, dm, w, m, v)


def _rope_tables(t_lat, t_ctx):
    t = jnp.arange(t_lat)
    pos = jnp.stack([(t // GRID_W).astype(F32), (t % GRID_W).astype(F32)], axis=1)
    half = QK_ROPE // 4
    freqs = ROPE_THETA ** (-jnp.arange(0, 2 * half, 2, dtype=F32) / (2 * half))
    ang = pos[:, :, None] * freqs[None, None, :]
    cos, sin = jnp.cos(ang), jnp.sin(ang)
    zero = jnp.zeros_like(sin)
    cos32 = jnp.concatenate([cos, cos], axis=2).reshape(t_lat, QK_ROPE)
    sa32 = jnp.concatenate([zero, sin], axis=2).reshape(t_lat, QK_ROPE)
    sb32 = jnp.concatenate([-sin, zero], axis=2).reshape(t_lat, QK_ROPE)

    def widen(tab, fill):
        left = jnp.full((t_lat, ROPE_LANE0), fill, F32)
        right = jnp.full((t_lat, HEAD_PAD - ROPE_LANE0 - QK_ROPE), fill, F32)
        lat = jnp.concatenate([left, tab, right], axis=1)
        return jnp.concatenate([lat, jnp.full((t_ctx, HEAD_PAD), fill, F32)], axis=0)

    return widen(cos32, 1.0), widen(sa32, 0.0), widen(sb32, 0.0)


def _cols_from_shards(s):
    return jnp.transpose(s, (1, 0, 2)).reshape(s.shape[1], -1)


def _cols_to_shards(w):
    k, n = w.shape
    return jnp.transpose(w.reshape(k, N_SHARD, n // N_SHARD), (1, 0, 2))


def kernel(x, c, ctx, c_ctx, w_mod, b_mod, w_in, q_norm_g, w_uq, kv_norm_g, w_ukv, conv_w, w_out, w_mlp1, w_mlp2, final_norm_g, loss_target, m_c_ctx, m_w_mod, m_b_mod, m_w_in, m_q_norm_g, m_w_uq, m_kv_norm_g, m_w_ukv, m_conv_w, m_w_out, m_w_mlp1, m_w_mlp2, m_final_norm_g, v_c_ctx, v_w_mod, v_b_mod, v_w_in, v_q_norm_g, v_w_uq, v_kv_norm_g, v_w_ukv, v_conv_w, v_w_out, v_w_mlp1, v_w_mlp2, v_final_norm_g):
    t_lat, t_ctx = x.shape[1], ctx.shape[1]
    assert t_ctx == TOK_TILE and t_lat % TOK_TILE == 0 and t_lat % GRID_W == 0
    mx, my, mc = _pos()
    j = 2 * mx + my
    ncol = w_mod.shape[2]
    x2, ctx2, tgt = x[0], ctx[0], loss_target[0]
    cctx_row = c_ctx.reshape(1, D_MODEL)

    b_sh = lax.dynamic_slice(b_mod, (0, j * ncol), (1, ncol))
    cw_pad = jnp.zeros((8, 128), F32).at[0:3, :].set(conv_w[0])
    c8, m_all, g_in, g_uq, g_ukv, g_out, g_m1, g_m2 = _prologue(
        c, cctx_row, w_mod[0], b_sh, cw_pad, (w_in[0], w_uq[0], w_ukv[0], w_out[0], w_mlp1[0], w_mlp2[0]), 3)
    mvec = m_all[:, 0, :].reshape(6, D_MODEL)
    mctx = m_all[:, 8, :].reshape(6, D_MODEL)
    zeros6 = jnp.zeros((6, D_MODEL), F32)
    mod_a = jnp.stack([jnp.concatenate([mvec[0:2], zeros6], axis=0), jnp.concatenate([mctx[0:2], zeros6], axis=0)])
    mod_b = jnp.concatenate([mvec[2:6], jnp.zeros((4, D_MODEL), F32)], axis=0)
    cw_full = jnp.pad(jnp.transpose(m_all[:, 9:12, 0:128], (1, 0, 2)).reshape(3, CONV_W), ((0, 5), (0, 0)))

    w_in_f = _cols_from_shards(g_in)
    zc = lambda n: jnp.zeros((D_MODEL, n), BF16)
    w_in_p = jnp.concatenate([w_in_f[:, 0:384], zc(64), w_in_f[:, 384:416], zc(32), w_in_f[:, 416:]], axis=1)
    w_uq_f = _cols_from_shards(g_uq).reshape(Q_RANK, N_HEADS, QK_DIM)
    w_uq_p = jnp.pad(w_uq_f, ((0, 0), (0, 0), (0, HEAD_PAD - QK_DIM))).reshape(Q_RANK, N_HEADS * HEAD_PAD)
    w_ukv_f = _cols_from_shards(g_ukv).reshape(KV_RANK, N_HEADS, QK_NOPE + V_DIM)
    padh = lambda a: jnp.pad(a, ((0, 0), (0, 0), (0, HEAD_PAD - a.shape[2]))).reshape(KV_RANK, N_HEADS * HEAD_PAD)
    w_ukv_p = jnp.concatenate([padh(w_ukv_f[:, :, :QK_NOPE]), padh(w_ukv_f[:, :, QK_NOPE:])], axis=1)
    cos_t, sin_a, sin_b = _rope_tables(t_lat, t_ctx)
    gf_row = final_norm_g.reshape(1, D_MODEL)
    c_idx = mc.reshape(1).astype(jnp.int32)
    j_idx = j.reshape(1).astype(jnp.int32)

    z, q, k, v, kt = _inproj_fwd(x2, ctx2, mod_a, w_in_p, q_norm_g, kv_norm_g, w_uq_p, w_ukv_p, cos_t, sin_a, sin_b)
    o, lse, g_out, w1, g_m2 = _attn_fwd(q, k, v, t_lat, (g_out, g_m1, g_m2))
    w_out_f = g_out.reshape(D_MODEL, D_MODEL)
    wo_attn = jnp.pad(w_out_f[0:512].reshape(N_HEADS, V_DIM, D_MODEL), ((0, 0), (0, HEAD_PAD - V_DIM), (0, 0)))
    wo_conv = w_out_f[512:]
    w2 = g_m2.reshape(D_FF, D_MODEL)
    r, da, h2, dy2, dx1, conv, acc_mlp, dy1, do, dgb, dyv = _mlp_fwdbwd(o, z, x2, tgt, mod_b, gf_row, cw_full, wo_attn,
                                                                         wo_conv, w1, w2)
    d_w1 = _wgrad(h2, da, "wgrad_mlp1", D_MODEL, FF_CHUNK)
    d_w2 = _wgrad(r, dy2, "wgrad_mlp2", FF_CHUNK, D_MODEL)
    d_wout, *big_got = _wgrad_out(o, conv, dy1, (d_w1, d_w2))
    d_wout = d_wout.reshape(N_SHARD, D_MODEL // N_SHARD, D_MODEL)
    big_grads, big_names = (d_w1, d_w2, d_wout), ("mlp1", "mlp2", "out")
    big_parts = [_add_pairs(a, g, c_idx, "rs_add_pairs_" + n) for a, g, n in zip(big_grads, big_got, big_names)]
    dqt, dk, dv, *big_recv = _attn_bwd(q, k, v, kt, o, do, lse, t_lat, big_parts)
    big_halves = [_add_chips(p, g, j_idx, "rs_add_chips_" + n) for p, g, n in zip(big_parts, big_recv, big_names)]
    gx, d_win, d_wuq, d_wukv, acc_in = _inproj_bwd(x2, ctx2, mod_a, z, dyv, dgb, dx1, dqt, dk, dv, cos_t, sin_a, sin_b,
                                                   w_in_p, w_uq_p, w_ukv_p, q_norm_g, kv_norm_g, cw_full)

    d_win_f = jnp.concatenate([d_win[:, 0:384], d_win[:, 448:480], d_win[:, 512:]], axis=1)
    d_wuq_f = d_wuq.reshape(Q_RANK, N_HEADS, HEAD_PAD)[:, :, 0:QK_DIM].reshape(Q_RANK, N_HEADS * QK_DIM)
    d_wukv3 = d_wukv.reshape(KV_RANK, 2, N_HEADS, HEAD_PAD)
    d_wukv_f = jnp.concatenate([d_wukv3[:, 0, :, 0:QK_NOPE], d_wukv3[:, 1, :, 0:V_DIM]], axis=2).reshape(KV_RANK, -1)
    rest = tuple(_cols_to_shards(a).astype(BF16) for a in (d_win_f, d_wuq_f, d_wukv_f))
    rest_names = ("in", "uq", "ukv")
    rest_got = _rs_sibling(rest, "rs_sibling_rest")
    rest_parts = [_add_pairs(a, g, c_idx, "rs_add_pairs_" + n) for a, g, n in zip(rest, rest_got, rest_names)]

    sv = jnp.concatenate([
        acc_in[0:2], acc_mlp[5:6], acc_mlp[2:4], acc_mlp[1:2],
        acc_in[2:4], acc_mlp[0:1], acc_in[4:5], acc_in[5:6], acc_in[6:9], acc_mlp[4:5],
        jnp.zeros((1, D_MODEL), F32)], axis=0)
    all_sv, red, o_cc, o_b, o_q, o_k, o_gf, *exchanged = _small_exchange(
        sv, w_mod[0], cctx_row, m_c_ctx.reshape(1, D_MODEL), v_c_ctx.reshape(1, D_MODEL),
        b_mod.reshape(6, D_MODEL), m_b_mod.reshape(6, D_MODEL), v_b_mod.reshape(6, D_MODEL),
        q_norm_g, m_q_norm_g, v_q_norm_g, kv_norm_g, m_kv_norm_g, v_kv_norm_g,
        gf_row, m_final_norm_g.reshape(1, D_MODEL), v_final_norm_g.reshape(1, D_MODEL), big_halves, rest_parts)
    big_theirs, rest_recv = exchanged[:len(big_halves)], exchanged[len(big_halves):]
    loss = red[14, 0]

    c9 = jnp.concatenate([c8[0::8], jnp.zeros((7, D_MODEL), F32)], axis=0)
    s_t = jnp.transpose(c9 * jax.nn.sigmoid(c9))
    dm_ex = all_sv[:, 0:6, :].reshape(8, 6 * D_MODEL)
    dm_ctx = jnp.concatenate([red[6:8].reshape(1, 2 * D_MODEL), jnp.zeros((1, 4 * D_MODEL), F32)], axis=1)
    dm16 = jnp.concatenate([dm_ex, dm_ctx, jnp.zeros((7, 6 * D_MODEL), F32)], axis=0)
    dm_sh = lax.dynamic_slice(dm16, (0, j * ncol), (16, ncol))
    g_wmod, d_wmod, nm_wmod, nv_wmod = _wmod_update(s_t, dm_sh, w_mod[0], m_w_mod[0], v_w_mod[0])

    g_cw = lax.dynamic_slice(red[11:14, 0:CONV_W], (0, j * 128), (3, 128))
    d_cw, nm_cw, nv_cw = _adamw_call(conv_w[0], g_cw, m_conv_w[0], v_conv_w[0], "adamw_conv")

    rest_halves = [_add_chips(p, g, j_idx, "rs_add_chips_" + n) for p, g, n in zip(rest_parts, rest_recv, rest_names)]
    g_win, g_wuq, g_wukv = _rs_join(rest_halves, "rs_join_rest")
    upd = {}
    for name, w_, g_, m_, v_ in (("in", w_in, g_win, m_w_in, v_w_in), ("uq", w_uq, g_wuq, m_w_uq, v_w_uq),
                                 ("ukv", w_ukv, g_wukv, m_w_ukv, v_w_ukv)):
        upd[name] = _adamw_call(w_[0], g_, m_[0], v_[0], "adamw_" + name)
    g_w1, *upd["mlp1"] = _adamw_halves(w_mlp1[0], big_halves[0], big_theirs[0], m_w_mlp1[0], v_w_mlp1[0], c_idx,
                                       "adamw_mlp1")
    g_w2, *upd["mlp2"] = _adamw_halves(w_mlp2[0], big_halves[1], big_theirs[1], m_w_mlp2[0], v_w_mlp2[0], c_idx,
                                       "adamw_mlp2")
    g_wout, *upd["out"] = _adamw_halves(w_out[0], big_halves[2], big_theirs[2], m_w_out[0], v_w_out[0], c_idx,
                                        "adamw_out")

    def four(o4, shape):
        return [o4[r].reshape(shape) for r in range(4)]

    cc4 = four(o_cc, (D_MODEL,))
    b4 = [o_b[r].reshape(1, 6 * D_MODEL) for r in range(4)]
    q4 = four(o_q, (1, Q_RANK))
    k4 = four(o_k, (1, KV_RANK))
    gf4 = four(o_gf, (D_MODEL,))
    big = {"in": g_win, "uq": g_wuq, "ukv": g_wukv, "out": g_wout, "mlp1": g_w1, "mlp2": g_w2}

    def leaf(idx):
        wm = (g_wmod, d_wmod, nm_wmod, nv_wmod)[idx]
        cwv = (g_cw, d_cw, nm_cw, nv_cw)[idx]
        bigv = {n: (big[n] if idx == 0 else upd[n][idx - 1]) for n in big}
        return [cc4[idx], wm[None], b4[idx], bigv["in"][None], q4[idx], bigv["uq"][None], k4[idx], bigv["ukv"][None],
                cwv[None], bigv["out"][None], bigv["mlp1"][None], bigv["mlp2"][None], gf4[idx]]

    return (loss, gx[None], *leaf(0), *leaf(1), *leaf(2), *leaf(3))
```

```python
import functools
import math

import jax
import jax.numpy as jnp
from jax import lax
from jax.experimental import pallas as pl
from jax.experimental.pallas import tpu as pltpu

F32 = jnp.float32
BF16 = jnp.bfloat16
MESH = pl.DeviceIdType.MESH
HIGHEST = lax.Precision.HIGHEST

D_MODEL = 1024
N_HEADS = 8
QK_NOPE = 64
QK_ROPE = 32
QK_DIM = QK_NOPE + QK_ROPE
V_DIM = 64
Q_RANK = 256
KV_RANK = 128
CONV_W = 512
D_FF = 4096
GRID_W = 64
ROPE_THETA = 10000.0
EPS = 1e-6
ATTN_SCALE = 1.0 / math.sqrt(QK_DIM)
HEAD_PAD = 128
Z_COLS = 2048
ROPE_LANE0 = QK_NOPE
N_SHARD = 4
TOK_TILE = 256
FF_CHUNK = 1024
MLP_FF_CHUNK = 1024
KEY_CHUNK = 512
ATTN_FWD_Q_BLOCK = 1024
ATTN_HEADS_PER_STEP = 4
ATTN_BWD_HEADS_PER_STEP = 2
ATTN_BWD_Q_BLOCK = 512
KEY_CHUNK_BWD = 512

ADAM_LR = 0.001
ADAM_B1 = 0.9
ADAM_B2 = 0.999
ADAM_EPS = 1e-08
ADAM_WD = 0.01
ADAM_STEP = 10

LOG2E = 1.4426950408889634

VMEM_LIMIT = 56 * 1024 * 1024
STAGE_VMEM_LIMIT = 32 * 1024 * 1024


def _pos():
    return lax.axis_index("x"), lax.axis_index("y"), lax.axis_index("c")


def _dot(a, b):
    return jnp.dot(a, b, preferred_element_type=F32)


def _dot_nt(a, b):
    return lax.dot_general(a, b, (((1,), (1,)), ((), ())), preferred_element_type=F32)


def _dot_tn(a, b):
    return lax.dot_general(a, b, (((0,), (0,)), ((), ())), preferred_element_type=F32)


def _rope(v, cos, sa, sb):
    return v * cos + pltpu.roll(v, 8, 1) * sa + pltpu.roll(v, HEAD_PAD - 8, 1) * sb


def _unrope(g, cos, sa, sb):
    return g * cos + pltpu.roll(g * sa, HEAD_PAD - 8, 1) + pltpu.roll(g * sb, 8, 1)


def _sigmoid(v):
    return 1.0 / (1.0 + jnp.exp(-v))


def _adamw(w, g, m, v):
    m = ADAM_B1 * m + (1.0 - ADAM_B1) * g
    v = ADAM_B2 * v + (1.0 - ADAM_B2) * (g * g)
    m_hat = m / (1.0 - ADAM_B1 ** ADAM_STEP)
    v_hat = v / (1.0 - ADAM_B2 ** ADAM_STEP)
    delta = -ADAM_LR * (m_hat / (jnp.sqrt(v_hat) + ADAM_EPS) + ADAM_WD * w)
    return delta, m, v


def _shift_rows(u, prev_row, next_row):
    n = u.shape[0]
    rows = lax.broadcasted_iota(jnp.int32, u.shape, 0)
    um1 = jnp.where(rows == 0, prev_row, pltpu.roll(u, 1, 0))
    up1 = jnp.where(rows == n - 1, next_row, pltpu.roll(u, n - 1, 0))
    return um1, up1


def _const_spec(shape):
    nd = len(shape)
    return pl.BlockSpec(shape, lambda *_: (0,) * nd)


def _resident_spec(shape):
    nd = len(shape)
    return pl.BlockSpec(shape, lambda *_: (0,) * nd, pipeline_mode=pl.Buffered(1))


def _peer(r, x, y, c):
    px = 1 - x if r & 4 else x
    py = 1 - y if r & 2 else y
    pc = 1 - c if r & 1 else c
    return (px, py, pc)


def _prologue(c_row, cctx_row, w_mod_sh, b_sh, cw_sh, srcs, n_gather):
    ncol = w_mod_sh.shape[1]
    n = len(srcs)
    n_split = 4

    def body(c_ref, cctx_ref, w_ref, b_ref, cw_ref, *refs):
        ins, (c8_ref, m_ref), outs = refs[:n], refs[n:n + 2], refs[n + 2:2 * n + 2]
        mine_ref, msh_ref = refs[2 * n + 2:2 * n + 4]
        f32s, bfs = refs[2 * n + 4:3 * n + 4], refs[3 * n + 4:4 * n + 4]
        ssem, rsem, ssem2, rsem2, lsem_in, lsem_out = refs[4 * n + 4:4 * n + 10]
        x, y, c = _pos()
        me = 4 * x + 2 * y + c
        j = 2 * x + y

        def pieces(rows):
            step = rows // n_split
            return [pl.ds(q * step, step) for q in range(n_split)]

        for t in range(n):
            for sl in pieces(ins[t].shape[0]):
                pltpu.make_async_copy(ins[t].at[sl], f32s[t].at[sl], lsem_in.at[t]).start()
        mine_ref[...] = jnp.zeros(mine_ref.shape, F32)
        mine_ref[0:1, :] = c_ref[...]
        my_rows = pl.ds(pl.multiple_of(8 * me, 8), 8)
        sends = []
        for r in range(1, 8):
            cp = pltpu.make_async_remote_copy(
                src_ref=mine_ref, dst_ref=c8_ref.at[my_rows], send_sem=ssem.at[r - 1], recv_sem=rsem.at[r - 1],
                device_id=_peer(r, x, y, c), device_id_type=MESH)
            cp.start()
            sends.append(cp)

        def cast_and_store(t):
            pltpu.make_async_copy(ins[t], f32s[t], lsem_in.at[t]).wait()
            bfs[t][...] = f32s[t][...].astype(BF16)
            for sl in pieces(ins[t].shape[0]):
                pltpu.make_async_copy(bfs[t].at[sl], outs[t].at[j, sl], lsem_out.at[t]).start()

        gather = _ShardGather(outs[:n_gather], *refs[4 * n + 10:])
        for t in range(n_gather):
            cast_and_store(t)
        for t in range(n_gather):
            pltpu.make_async_copy(bfs[t], outs[t].at[j], lsem_out.at[t]).wait()
        gather.start()
        for cp in sends:
            cp.wait()
        c8_ref[my_rows, :] = mine_ref[...]
        c8_ref[64:72, :] = jnp.zeros((8, D_MODEL), F32)
        c8_ref[64:65, :] = cctx_ref[...]
        cv = c8_ref[...]
        s = cv * _sigmoid(cv)
        m = jnp.dot(s, w_ref[...], precision=HIGHEST, preferred_element_type=F32) + b_ref[...]
        msh_ref[0:64, :] = m[0:64, :]
        msh_ref[64:72, :] = jnp.zeros((8, ncol), F32)
        msh_ref[64:65, :] = m[64:65, :]
        msh_ref[65:68, 0:128] = cw_ref[0:3, :]
        m_ref[j, 0:8, :] = msh_ref[my_rows, :]
        m_ref[j, 8:16, :] = msh_ref[64:72, :]
        sends2 = []
        for k, (px, py) in enumerate(_chips(x, y)):
            theirs = pl.ds(pl.multiple_of(8 * (4 * px + 2 * py + c), 8), 8)
            for half, src in enumerate((msh_ref.at[theirs], msh_ref.at[64:72])):
                cp = pltpu.make_async_remote_copy(
                    src_ref=src, dst_ref=m_ref.at[j, 8 * half:8 * half + 8], send_sem=ssem2.at[2 * k + half],
                    recv_sem=rsem2.at[2 * k + half], device_id=(px, py, c), device_id_type=MESH)
                cp.start()
                sends2.append(cp)
        for t in range(n_gather, n):
            cast_and_store(t)
        gather.forward()
        gather.finish()
        for t in range(n_gather, n):
            pltpu.make_async_copy(bfs[t], outs[t].at[j], lsem_out.at[t]).wait()
        for cp in sends2:
            cp.wait()

    vm = pl.BlockSpec(memory_space=pltpu.VMEM)
    hbm = pl.BlockSpec(memory_space=pl.ANY)
    return pl.pallas_call(
        body, name="prologue",
        out_shape=(jax.ShapeDtypeStruct((72, D_MODEL), F32), jax.ShapeDtypeStruct((N_SHARD, 16, ncol), F32))
        + tuple(jax.ShapeDtypeStruct((N_SHARD,) + a.shape, BF16) for a in srcs),
        in_specs=[vm] * 5 + [hbm] * n, out_specs=(vm, vm) + (hbm,) * n,
        scratch_shapes=[pltpu.VMEM((8, D_MODEL), F32), pltpu.VMEM((72, ncol), F32)]
        + [pltpu.VMEM(a.shape, F32) for a in srcs] + [pltpu.VMEM(a.shape, BF16) for a in srcs]
        + [pltpu.SemaphoreType.DMA((7,)), pltpu.SemaphoreType.DMA((7,)),
           pltpu.SemaphoreType.DMA((6,)), pltpu.SemaphoreType.DMA((6,)),
           pltpu.SemaphoreType.DMA((n,)), pltpu.SemaphoreType.DMA((n,))] + _gather_sems(n_gather),
        compiler_params=pltpu.CompilerParams(vmem_limit_bytes=VMEM_LIMIT),
    )(c_row, cctx_row, w_mod_sh, b_sh, cw_sh, *srcs)


def _chips(x, y):
    return [(1 - x, y), (x, 1 - y), (1 - x, 1 - y)]


def _halves(ref, c, align):
    hr = ref.shape[-2] // 2
    return (pl.ds(pl.multiple_of(c * hr, align), hr), pl.ds(pl.multiple_of((1 - c) * hr, align), hr))


class _ShardGather:
    def __init__(self, refs, ssem, rsem, fsend, frecv):
        self.refs, self.sems = refs, (ssem, rsem, fsend, frecv)
        self.x, self.y, self.c = _pos()
        self.j = 2 * self.x + self.y

    def _ici(self, a, k, slot):
        g = self.refs[a]
        ssem, rsem, _, _ = self.sems
        mine, _ = _halves(g, self.c, 16)
        px, py = _chips(self.x, self.y)[k]
        return pltpu.make_async_remote_copy(
            src_ref=g.at[self.j, mine], dst_ref=g.at[slot, mine], send_sem=ssem.at[3 * a + k],
            recv_sem=rsem.at[3 * a + k], device_id=(px, py, self.c), device_id_type=MESH)

    def _d2d(self, a, k, to_other_half):
        g = self.refs[a]
        _, _, fsend, frecv = self.sems
        mine, theirs = _halves(g, self.c, 16)
        px, py = _chips(self.x, self.y)[k]
        jk = 2 * px + py
        return pltpu.make_async_remote_copy(
            src_ref=g.at[jk, mine], dst_ref=g.at[jk, theirs if to_other_half else mine],
            send_sem=fsend.at[3 * a + k], recv_sem=frecv.at[3 * a + k],
            device_id=(self.x, self.y, 1 - self.c), device_id_type=MESH)

    def start(self):
        for a in range(len(self.refs)):
            for k in range(3):
                self._ici(a, k, self.j).start()

    def forward(self):
        for a in range(len(self.refs)):
            for k, (px, py) in enumerate(_chips(self.x, self.y)):
                self._ici(a, k, 2 * px + py).wait_recv()
                self._d2d(a, k, False).start()

    def finish(self):
        for a in range(len(self.refs)):
            for k in range(3):
                self._d2d(a, k, True).wait()
                self._ici(a, k, self.j).wait_send()


def _gather_sems(n_arrays):
    return [pltpu.SemaphoreType.DMA((3 * n_arrays,)) for _ in range(4)]


class _SiblingSend:
    def __init__(self, g_refs, got_refs, ssem, rsem):
        self.g_refs, self.got_refs, self.ssem, self.rsem = g_refs, got_refs, ssem, rsem
        self.x, self.y, self.c = _pos()

    def _copy(self, a, shard):
        _, theirs = _halves(self.g_refs[a], self.c, 16)
        src = self.g_refs[a].at[:, theirs] if shard is None else self.g_refs[a].at[shard, theirs]
        dst = self.got_refs[a] if shard is None else self.got_refs[a].at[shard]
        return pltpu.make_async_remote_copy(
            src_ref=src, dst_ref=dst, send_sem=self.ssem.at[a], recv_sem=self.rsem.at[a],
            device_id=(self.x, self.y, 1 - self.c), device_id_type=MESH)

    def start(self):
        for a in range(len(self.g_refs)):
            for s in range(N_SHARD):
                self._copy(a, s).start()

    def finish(self):
        for a in range(len(self.g_refs)):
            self._copy(a, None).wait()


class _SiblingSwap:
    def __init__(self, h_refs, t_refs, ssem, rsem):
        self.h_refs, self.t_refs, self.ssem, self.rsem = h_refs, t_refs, ssem, rsem
        self.x, self.y, self.c = _pos()

    def _copy(self, a):
        return pltpu.make_async_remote_copy(
            src_ref=self.h_refs[a], dst_ref=self.t_refs[a], send_sem=self.ssem.at[a], recv_sem=self.rsem.at[a],
            device_id=(self.x, self.y, 1 - self.c), device_id_type=MESH)

    def start(self):
        for a in range(len(self.h_refs)):
            self._copy(a).start()

    def finish(self):
        for a in range(len(self.h_refs)):
            self._copy(a).wait()


def _rs_sibling(arrs, name):
    n = len(arrs)

    def body(*refs):
        send = _SiblingSend(refs[:n], refs[n:2 * n], refs[2 * n], refs[2 * n + 1])
        send.start()
        send.finish()

    hbm = pl.BlockSpec(memory_space=pl.ANY)
    return pl.pallas_call(
        body, name=name,
        out_shape=tuple(jax.ShapeDtypeStruct((N_SHARD, a.shape[1] // 2, a.shape[2]), BF16) for a in arrs),
        in_specs=[hbm] * n, out_specs=(hbm,) * n,
        scratch_shapes=[pltpu.SemaphoreType.DMA((n,)), pltpu.SemaphoreType.DMA((n,))],
    )(*arrs)


class _ChipScatter:
    def __init__(self, parts, gots, ssem, rsem):
        self.parts, self.gots, self.ssem, self.rsem = parts, gots, ssem, rsem
        self.x, self.y, self.c = _pos()

    def _copy(self, a, k):
        px, py = _chips(self.x, self.y)[k]
        return pltpu.make_async_remote_copy(
            src_ref=self.parts[a].at[2 * px + py], dst_ref=self.gots[a].at[k], send_sem=self.ssem.at[3 * a + k],
            recv_sem=self.rsem.at[3 * a + k], device_id=(px, py, self.c), device_id_type=MESH)

    def start(self):
        for a in range(len(self.parts)):
            for k in range(3):
                self._copy(a, k).start()

    def finish(self):
        for a in range(len(self.parts)):
            for k in range(3):
                self._copy(a, k).wait()


def _rs_join(halves, name):
    n = len(halves)

    def body(*refs):
        h_refs, f_refs, stages = refs[:n], refs[n:2 * n], refs[2 * n:3 * n]
        lsem_in, lsem_out, ssem, rsem = refs[3 * n:]
        x, y, c = _pos()
        remote = []
        for a in range(n):
            mine, _ = _halves(f_refs[a], c, 8)
            cp = pltpu.make_async_remote_copy(
                src_ref=h_refs[a], dst_ref=f_refs[a].at[mine], send_sem=ssem.at[a], recv_sem=rsem.at[a],
                device_id=(x, y, 1 - c), device_id_type=MESH)
            cp.start()
            remote.append(cp)
            pltpu.make_async_copy(h_refs[a], stages[a], lsem_in.at[a]).start()
        local = []
        for a in range(n):
            mine, _ = _halves(f_refs[a], c, 8)
            pltpu.make_async_copy(h_refs[a], stages[a], lsem_in.at[a]).wait()
            cp = pltpu.make_async_copy(stages[a], f_refs[a].at[mine], lsem_out.at[a])
            cp.start()
            local.append(cp)
        for cp in remote + local:
            cp.wait()

    hbm = pl.BlockSpec(memory_space=pl.ANY)
    return pl.pallas_call(
        body, name=name,
        out_shape=tuple(jax.ShapeDtypeStruct((2 * h.shape[0], h.shape[1]), F32) for h in halves),
        in_specs=[hbm] * n, out_specs=(hbm,) * n,
        scratch_shapes=[pltpu.VMEM(h.shape, F32) for h in halves]
        + [pltpu.SemaphoreType.DMA((n,)) for _ in range(4)],
        compiler_params=pltpu.CompilerParams(vmem_limit_bytes=STAGE_VMEM_LIMIT),
    )(*halves)


def _row_block(rows):
    return max(b for b in range(16, 513, 16) if rows % b == 0)


def _add_pairs(arr, got, c_idx, name):
    hr, cols = got.shape[1], got.shape[2]
    rb = _row_block(hr)
    nb = hr // rb

    def body(c_ref, a_ref, b_ref, o_ref):
        o_ref[...] = (a_ref[...].astype(F32) + b_ref[...].astype(F32)).astype(BF16)

    spec = pl.BlockSpec((1, rb, cols), lambda s, r, c_ref: (s, r, 0))
    grid_spec = pltpu.PrefetchScalarGridSpec(
        num_scalar_prefetch=1, grid=(N_SHARD, nb),
        in_specs=[pl.BlockSpec((1, rb, cols), lambda s, r, c_ref: (s, c_ref[0] * nb + r, 0)), spec],
        out_specs=spec)
    return pl.pallas_call(
        body, name=name, grid_spec=grid_spec, out_shape=jax.ShapeDtypeStruct(got.shape, BF16),
    )(c_idx, arr, got)


def _add_chips(part, got, j_idx, name):
    hr, cols = got.shape[1], got.shape[2]
    rb = _row_block(hr)

    def body(j_ref, p_ref, g_ref, o_ref):
        acc = p_ref[0].astype(F32)
        for k in range(3):
            acc = acc + g_ref[k].astype(F32)
        o_ref[...] = acc

    grid_spec = pltpu.PrefetchScalarGridSpec(
        num_scalar_prefetch=1, grid=(hr // rb,),
        in_specs=[pl.BlockSpec((1, rb, cols), lambda r, j_ref: (j_ref[0], r, 0)),
                  pl.BlockSpec((3, rb, cols), lambda r, j_ref: (0, r, 0))],
        out_specs=pl.BlockSpec((rb, cols), lambda r, j_ref: (r, 0)))
    return pl.pallas_call(
        body, name=name, grid_spec=grid_spec, out_shape=jax.ShapeDtypeStruct((hr, cols), F32),
    )(j_idx, part, got)


def _small_exchange(sv, w_mod_sh, cctx, m_cctx, v_cctx, bmod, m_bmod, v_bmod, qg, m_qg, v_qg, kvg, m_kvg, v_kvg,
                    gf, m_gf, v_gf, swap_halves, scatter_parts):
    ncol = w_mod_sh.shape[1]
    n_s, n_p = len(swap_halves), len(scatter_parts)
    n_x = n_s + n_p

    def body(sv_ref, w_ref, cctx_ref, mcc_ref, vcc_ref, b_ref, mb_ref, vb_ref, qg_ref, mq_ref, vq_ref,
             kg_ref, mk_ref, vk_ref, gf_ref, mgf_ref, vgf_ref, *rest):
        all_ref, red_ref, occ_ref, ob_ref, oq_ref, ok_ref, ogf_ref = rest[n_x:n_x + 7]
        vec_ref, part_ref, ssem, rsem, ssem2, rsem2, wsend, wrecv, psend, precv = rest[2 * n_x + 7:]
        swap = _SiblingSwap(rest[:n_s], rest[n_x + 7:n_x + 7 + n_s], wsend, wrecv)
        scatter = _ChipScatter(rest[n_s:n_x], rest[n_x + 7 + n_s:2 * n_x + 7], psend, precv)
        swap.start()
        scatter.start()
        x, y, c = _pos()
        me = 4 * x + 2 * y + c
        j = 2 * x + y
        sends = []
        for r in range(1, 8):
            cp = pltpu.make_async_remote_copy(
                src_ref=sv_ref, dst_ref=all_ref.at[me], send_sem=ssem.at[r - 1], recv_sem=rsem.at[r - 1],
                device_id=_peer(r, x, y, c), device_id_type=MESH)
            cp.start()
            sends.append(cp)
        for cp in sends:
            cp.wait()
        all_ref[me] = sv_ref[...]
        red = all_ref[0]
        for d in range(1, 8):
            red = red + all_ref[d]
        red_ref[...] = red
        vec_ref[...] = jnp.zeros(vec_ref.shape, F32)

        @pl.when(j == 0)
        def _():
            vec_ref[0:1, 0:1024] = red[6:7, :]
            vec_ref[0:1, 1024:1536] = red[7:8, 0:512]

        @pl.when(j == 1)
        def _():
            vec_ref[0:1, 0:512] = red[7:8, 512:1024]

        part = lax.dot_general(vec_ref[...], w_ref[...], (((1,), (1,)), ((), ())), precision=HIGHEST,
                               preferred_element_type=F32)
        part_ref[j] = part
        sends2 = []
        for k, r in enumerate((4, 2, 6)):
            cp = pltpu.make_async_remote_copy(
                src_ref=part_ref.at[j], dst_ref=part_ref.at[j], send_sem=ssem2.at[k], recv_sem=rsem2.at[k],
                device_id=_peer(r, x, y, c), device_id_type=MESH)
            cp.start()
            sends2.append(cp)
        for cp in sends2:
            cp.wait()
        tot = part_ref[0]
        for s in range(1, N_SHARD):
            tot = tot + part_ref[s]
        cc = cctx_ref[...]
        sg = _sigmoid(cc)
        g_cc = tot[0:1, :] * (sg * (1.0 + cc * (1.0 - sg)))
        d_, m_, v_ = _adamw(cc, g_cc, mcc_ref[...], vcc_ref[...])
        occ_ref[0:1, :] = g_cc
        occ_ref[1:2, :] = d_
        occ_ref[2:3, :] = m_
        occ_ref[3:4, :] = v_
        occ_ref[4:8, :] = jnp.zeros((4, D_MODEL), F32)
        g_b = red[0:6, :]
        pad = jnp.concatenate([red[6:8, :], jnp.zeros((4, D_MODEL), F32)], axis=0)
        g_b = g_b + pad
        d_, m_, v_ = _adamw(b_ref[...], g_b, mb_ref[...], vb_ref[...])
        ob_ref[0] = g_b
        ob_ref[1] = d_
        ob_ref[2] = m_
        ob_ref[3] = v_
        g_q = red[9:10, 0:Q_RANK]
        d_, m_, v_ = _adamw(qg_ref[...], g_q, mq_ref[...], vq_ref[...])
        oq_ref[0:1, :] = g_q
        oq_ref[1:2, :] = d_
        oq_ref[2:3, :] = m_
        oq_ref[3:4, :] = v_
        oq_ref[4:8, :] = jnp.zeros((4, Q_RANK), F32)
        g_k = red[10:11, 0:KV_RANK]
        d_, m_, v_ = _adamw(kg_ref[...], g_k, mk_ref[...], vk_ref[...])
        ok_ref[0:1, :] = g_k
        ok_ref[1:2, :] = d_
        ok_ref[2:3, :] = m_
        ok_ref[3:4, :] = v_
        ok_ref[4:8, :] = jnp.zeros((4, KV_RANK), F32)
        g_f = red[8:9, :]
        d_, m_, v_ = _adamw(gf_ref[...], g_f, mgf_ref[...], vgf_ref[...])
        ogf_ref[0:1, :] = g_f
        ogf_ref[1:2, :] = d_
        ogf_ref[2:3, :] = m_
        ogf_ref[3:4, :] = v_
        ogf_ref[4:8, :] = jnp.zeros((4, D_MODEL), F32)
        swap.finish()
        scatter.finish()

    vm = pl.BlockSpec(memory_space=pltpu.VMEM)
    hbm = pl.BlockSpec(memory_space=pl.ANY)
    out_shape = (
        jax.ShapeDtypeStruct((8, 16, D_MODEL), F32),
        jax.ShapeDtypeStruct((16, D_MODEL), F32),
        jax.ShapeDtypeStruct((8, D_MODEL), F32),
        jax.ShapeDtypeStruct((4, 6, D_MODEL), F32),
        jax.ShapeDtypeStruct((8, Q_RANK), F32),
        jax.ShapeDtypeStruct((8, KV_RANK), F32),
        jax.ShapeDtypeStruct((8, D_MODEL), F32),
    ) + tuple(jax.ShapeDtypeStruct(h.shape, F32) for h in swap_halves) + tuple(
        jax.ShapeDtypeStruct((3,) + p.shape[1:], BF16) for p in scatter_parts)
    return pl.pallas_call(
        body, name="small_exchange", out_shape=out_shape, in_specs=[vm] * 17 + [hbm] * n_x,
        out_specs=tuple([vm] * 7) + (hbm,) * n_x,
        scratch_shapes=[pltpu.VMEM((8, ncol), F32), pltpu.VMEM((N_SHARD, 8, D_MODEL), F32),
                        pltpu.SemaphoreType.DMA((7,)), pltpu.SemaphoreType.DMA((7,)),
                        pltpu.SemaphoreType.DMA((3,)), pltpu.SemaphoreType.DMA((3,)),
                        pltpu.SemaphoreType.DMA((n_s,)), pltpu.SemaphoreType.DMA((n_s,)),
                        pltpu.SemaphoreType.DMA((3 * n_p,)), pltpu.SemaphoreType.DMA((3 * n_p,))],
        compiler_params=pltpu.CompilerParams(vmem_limit_bytes=VMEM_LIMIT),
    )(sv, w_mod_sh, cctx, m_cctx, v_cctx, bmod, m_bmod, v_bmod, qg, m_qg, v_qg, kvg, m_kvg, v_kvg, gf, m_gf, v_gf,
      *swap_halves, *scatter_parts)


def _inproj_fwd(x2, ctx2, mod_a, w_in, qg, kvg, w_uq, w_ukv, cos_t, sin_a, sin_b):
    t_lat, t_ctx = x2.shape[0], ctx2.shape[0]
    tm = TOK_TILE
    n_lat = t_lat // tm
    n_all = n_lat + t_ctx // tm
    e_rows = t_lat + t_ctx

    def body(x_ref, ctx_ref, mod_ref, win_ref, qg_ref, kvg_ref, wuq_ref, wukv_ref, cos_ref, sa_ref, sb_ref,
             z_ref, q_ref, k_ref, v_ref, kt_ref):
        i = pl.program_id(0)
        xin = jnp.where(i < n_lat, x_ref[...], ctx_ref[...])
        xn = xin * lax.rsqrt(jnp.mean(xin * xin, axis=-1, keepdims=True) + EPS)
        h1 = (xn * (1.0 + mod_ref[0, 1:2, :]) + mod_ref[0, 0:1, :]).astype(BF16)
        z = _dot(h1, win_ref[...])
        z_ref[...] = z
        cos, sa, sb = cos_ref[...], sa_ref[...], sb_ref[...]
        cq = z[:, 0:Q_RANK]
        cqn = (cq * lax.rsqrt(jnp.mean(cq * cq, axis=-1, keepdims=True) + EPS) * qg_ref[...]).astype(BF16)
        q = _dot(cqn, wuq_ref[...])
        ckv = z[:, Q_RANK:Q_RANK + KV_RANK]
        ckvn = (ckv * lax.rsqrt(jnp.mean(ckv * ckv, axis=-1, keepdims=True) + EPS) * kvg_ref[...]).astype(BF16)
        kv = _dot(ckvn, wukv_ref[...])
        kr = _rope(z[:, Q_RANK + KV_RANK:Q_RANK + KV_RANK + HEAD_PAD], cos, sa, sb)
        ones_lane = lax.broadcasted_iota(jnp.int32, (tm, HEAD_PAD), 1) == V_DIM
        for h in range(N_HEADS):
            lo = h * HEAD_PAD
            q_ref[h] = _rope(q[:, lo:lo + HEAD_PAD], cos, sa, sb).astype(BF16)
            kh = kv[:, lo:lo + HEAD_PAD] + kr
            k_ref[h] = kh.astype(BF16)
            kt_ref[h] = kh.T.astype(BF16)
            vh = kv[:, N_HEADS * HEAD_PAD + lo:N_HEADS * HEAD_PAD + lo + HEAD_PAD]
            v_ref[h] = jnp.where(ones_lane, 1.0, vh).astype(BF16)

    row = lambda i: (i, 0)
    head_spec = pl.BlockSpec((N_HEADS, tm, HEAD_PAD), lambda i: (0, i, 0))
    head_shape = jax.ShapeDtypeStruct((N_HEADS, e_rows, HEAD_PAD), BF16)
    return pl.pallas_call(
        body, name="inproj_fwd", grid=(n_all,),
        out_shape=(jax.ShapeDtypeStruct((e_rows, Z_COLS), F32), head_shape, head_shape, head_shape,
                   jax.ShapeDtypeStruct((N_HEADS, HEAD_PAD, e_rows), BF16)),
        in_specs=[
            pl.BlockSpec((tm, D_MODEL), lambda i: (jnp.minimum(i, n_lat - 1), 0)),
            _const_spec((tm, D_MODEL)),
            pl.BlockSpec((1, 8, D_MODEL), lambda i: (i // n_lat, 0, 0)),
            _const_spec(w_in.shape), _const_spec(qg.shape), _const_spec(kvg.shape),
            _const_spec(w_uq.shape), _const_spec(w_ukv.shape),
            pl.BlockSpec((tm, HEAD_PAD), row), pl.BlockSpec((tm, HEAD_PAD), row), pl.BlockSpec((tm, HEAD_PAD), row),
        ],
        out_specs=(pl.BlockSpec((tm, Z_COLS), row), head_spec, head_spec, head_spec,
                   pl.BlockSpec((N_HEADS, HEAD_PAD, tm), lambda i: (0, 0, i))),
        compiler_params=pltpu.CompilerParams(vmem_limit_bytes=VMEM_LIMIT),
    )(x2, ctx2, mod_a, w_in, qg, kvg, w_uq, w_ukv, cos_t, sin_a, sin_b)


def _key_chunks(e_rows, size):
    n_chunks = max(1, e_rows // size)
    return [(ci * size, size if ci < n_chunks - 1 else e_rows - ci * size) for ci in range(n_chunks)]


def _attn_fwd(q, k, v, t_lat, shard_arrays):
    e_rows = k.shape[1]
    tq = min(t_lat, ATTN_FWD_Q_BLOCK)
    bounds = _key_chunks(e_rows, KEY_CHUNK)
    c2 = ATTN_SCALE * LOG2E

    hb = ATTN_HEADS_PER_STEP
    n_hb = N_HEADS // hb

    def body(q_ref, k_ref, v_ref, o_ref, lse_ref):
        qs = [q_ref[b] for b in range(hb)]
        m, acc = [None] * hb, [None] * hb
        for lo, n in bounds:
            for b in range(hb):
                s = _dot_nt(qs[b], k_ref[b, lo:lo + n, :])
                mc = jnp.max(s, axis=-1, keepdims=True)
                m_new = mc if m[b] is None else jnp.maximum(m[b], mc)
                p = jnp.exp2((s - m_new) * c2)
                pv = _dot(p.astype(BF16), v_ref[b, lo:lo + n, :])
                acc[b] = pv if m[b] is None else acc[b] * jnp.exp2((m[b] - m_new) * c2) + pv
                m[b] = m_new
        for b in range(hb):
            l = acc[b][:, V_DIM:V_DIM + 1]
            o_ref[b] = (acc[b] * (1.0 / l)).astype(BF16)
            lse = (m[b] * ATTN_SCALE + jnp.log(l)) * LOG2E
            lse_ref[b] = jnp.broadcast_to(lse, (tq, HEAD_PAD)).T[0:1, :]

    n_w = len(shard_arrays)
    n_q = t_lat // tq

    def body_with_gather(q_ref, k_ref, v_ref, *rest):
        o_ref, lse_ref = rest[n_w], rest[n_w + 1]
        gather = _ShardGather(rest[n_w + 2:2 * n_w + 2], *rest[2 * n_w + 2:])
        step = pl.program_id(0) * n_q + pl.program_id(1)
        pl.when(step == 0)(gather.start)
        pl.when(step == n_hb * n_q // 2)(gather.forward)
        body(q_ref, k_ref, v_ref, o_ref, lse_ref)
        pl.when(step == n_hb * n_q - 1)(gather.finish)

    hbm = pl.BlockSpec(memory_space=pl.ANY)
    return pl.pallas_call(
        body_with_gather, name="attn_fwd", grid=(n_hb, n_q),
        out_shape=(jax.ShapeDtypeStruct((N_HEADS, t_lat, HEAD_PAD), BF16),
                   jax.ShapeDtypeStruct((N_HEADS, 1, t_lat), F32))
        + tuple(jax.ShapeDtypeStruct(a.shape, a.dtype) for a in shard_arrays),
        in_specs=[pl.BlockSpec((hb, tq, HEAD_PAD), lambda h, i: (h, i, 0)),
                  pl.BlockSpec((hb, e_rows, HEAD_PAD), lambda h, i: (h, 0, 0)),
                  pl.BlockSpec((hb, e_rows, HEAD_PAD), lambda h, i: (h, 0, 0))] + [hbm] * n_w,
        out_specs=(pl.BlockSpec((hb, tq, HEAD_PAD), lambda h, i: (h, i, 0)),
                   pl.BlockSpec((hb, 1, tq), lambda h, i: (h, 0, i))) + (hbm,) * n_w,
        input_output_aliases={3 + a: 2 + a for a in range(n_w)},
        scratch_shapes=_gather_sems(n_w),
        compiler_params=pltpu.CompilerParams(vmem_limit_bytes=VMEM_LIMIT),
    )(q, k, v, *shard_arrays)


def _attn_bwd(q, k, v, kt, o, do, lse_row, t_lat, parts):
    e_rows = k.shape[1]
    tq = min(t_lat, ATTN_BWD_Q_BLOCK)
    n_p = len(parts)
    n_q = t_lat // tq
    bounds = _key_chunks(e_rows, KEY_CHUNK_BWD)

    hb = ATTN_BWD_HEADS_PER_STEP
    n_hb = N_HEADS // hb

    def body(q_ref, k_ref, v_ref, kt_ref, o_ref, do_ref, lse_ref, *rest):
        dqt_ref, dk_ref, dv_ref = rest[n_p:n_p + 3]
        scatter = _ChipScatter(rest[:n_p], rest[n_p + 3:2 * n_p + 3], rest[2 * n_p + 3], rest[2 * n_p + 4])
        h, i = pl.program_id(0), pl.program_id(1)
        pl.when(jnp.logical_and(h == 0, i == 0))(scatter.start)

        @pl.when(i == 0)
        def _():
            dk_ref[...] = jnp.zeros(dk_ref.shape, F32)
            dv_ref[...] = jnp.zeros(dv_ref.shape, F32)

        qs, dos, lses, deltas = [], [], [], []
        for b in range(hb):
            qs.append(q_ref[b])
            dos.append(do_ref[b])
            lses.append(lse_ref[b])
            prod = o_ref[b].astype(F32) * dos[b].astype(F32)
            deltas.append(lax.dot_general(jnp.ones((8, HEAD_PAD), F32), prod, (((1,), (1,)), ((), ())),
                                          precision=HIGHEST, preferred_element_type=F32)[0:1, :])
        dqt = [None] * hb
        for lo, n in bounds:
            for b in range(hb):
                pt = jnp.exp2(_dot_nt(k_ref[b, lo:lo + n, :], qs[b]) * (ATTN_SCALE * LOG2E) - lses[b])
                dpt = _dot_nt(v_ref[b, lo:lo + n, :], dos[b])
                dst = (pt * (dpt - deltas[b])).astype(BF16)
                dv_c = _dot(pt.astype(BF16), dos[b])
                dk_c = _dot(dst, qs[b])
                part = _dot(kt_ref[b, :, lo:lo + n], dst)
                dqt[b] = part if dqt[b] is None else dqt[b] + part
                dk_ref[b, lo:lo + n, :] += dk_c * ATTN_SCALE
                dv_ref[b, lo:lo + n, :] += dv_c
        for b in range(hb):
            dqt_ref[b] = dqt[b] * ATTN_SCALE

        pl.when(jnp.logical_and(h == n_hb - 1, i == n_q - 1))(scatter.finish)

    hbm = pl.BlockSpec(memory_space=pl.ANY)
    qspec = pl.BlockSpec((hb, tq, HEAD_PAD), lambda h, i: (h, i, 0))
    kspec = pl.BlockSpec((hb, e_rows, HEAD_PAD), lambda h, i: (h, 0, 0))
    return pl.pallas_call(
        body, name="attn_bwd", grid=(n_hb, n_q),
        out_shape=(jax.ShapeDtypeStruct((N_HEADS, HEAD_PAD, t_lat), F32),
                   jax.ShapeDtypeStruct((N_HEADS, e_rows, HEAD_PAD), F32),
                   jax.ShapeDtypeStruct((N_HEADS, e_rows, HEAD_PAD), F32))
        + tuple(jax.ShapeDtypeStruct((3,) + p.shape[1:], BF16) for p in parts),
        in_specs=[qspec, kspec, kspec, pl.BlockSpec((hb, HEAD_PAD, e_rows), lambda h, i: (h, 0, 0)), qspec, qspec,
                  pl.BlockSpec((hb, 1, tq), lambda h, i: (h, 0, i))] + [hbm] * n_p,
        out_specs=(pl.BlockSpec((hb, HEAD_PAD, tq), lambda h, i: (h, 0, i)), kspec, kspec) + (hbm,) * n_p,
        scratch_shapes=[pltpu.SemaphoreType.DMA((3 * n_p,)), pltpu.SemaphoreType.DMA((3 * n_p,))],
        compiler_params=pltpu.CompilerParams(vmem_limit_bytes=VMEM_LIMIT),
    )(q, k, v, kt, o, do, lse_row, *parts)


def _halo_specs(tm, col_block):
    per = tm // 8
    prev = pl.BlockSpec((8, CONV_W), lambda i: (jnp.maximum(i * per - 1, 0), col_block))
    nxt = pl.BlockSpec((8, CONV_W), lambda i: ((i + 1) * per, col_block))
    return prev, nxt


def _mlp_fwdbwd(o, z, x2, tgt, mod_b, gf, cw, wo_attn, wo_conv, w1, w2):
    t_lat = x2.shape[0]
    tm = TOK_TILE
    n_lat = t_lat // tm
    fc = MLP_FF_CHUNK
    n_ff = D_FF // fc

    def body(o_ref, gb_ref, gc_ref, xi_ref, gcp_ref, xip_ref, gcn_ref, xin_ref, cw_ref, woa_ref, woc_ref,
             x_ref, t_ref, mod_ref, gf_ref, w1_ref, w2_ref,
             r_ref, da_ref, h2_ref, dy2_ref, dx1_ref, conv_ref, acc_ref, dy1_ref, do_ref, dgb_ref, dyv_ref, ra_ref):
        i = pl.program_id(0)

        @pl.when(i == 0)
        def _():
            acc_ref[...] = jnp.zeros(acc_ref.shape, F32)

        g1, sh2, sc2, g2 = mod_ref[0:1, :], mod_ref[1:2, :], mod_ref[2:3, :], mod_ref[3:4, :]
        u = gc_ref[...] * xi_ref[...]
        u_prev = jnp.where(i > 0, gcp_ref[7:8, :] * xip_ref[7:8, :], 0.0)
        u_next = jnp.where(i < n_lat - 1, gcn_ref[0:1, :] * xin_ref[0:1, :], 0.0)
        um1, up1 = _shift_rows(u, u_prev, u_next)
        yv = cw_ref[0:1, :] * um1 + cw_ref[1:2, :] * u + cw_ref[2:3, :] * up1
        gb = gb_ref[...]
        conv = (gb * yv).astype(BF16)
        conv_ref[...] = conv
        y1 = _dot(conv, woc_ref[...])
        for h in range(N_HEADS):
            y1 = y1 + _dot(o_ref[h], woa_ref[h])
        x1 = x_ref[...] + g1 * y1
        rstd2 = lax.rsqrt(jnp.mean(x1 * x1, axis=-1, keepdims=True) + EPS)
        xn1 = x1 * rstd2
        h2 = (xn1 * (1.0 + sc2) + sh2).astype(BF16)
        h2_ref[...] = h2
        y2 = jnp.zeros((tm, D_MODEL), F32)
        for jj in range(n_ff):
            lo = jj * fc
            ra = jnp.maximum(_dot(h2, w1_ref[lo // FF_CHUNK, :, lo % FF_CHUNK:lo % FF_CHUNK + fc]), 0.0)
            ra_ref[jj] = ra
            r = (ra * ra).astype(BF16)
            r_ref[:, lo:lo + fc] = r
            y2 = y2 + _dot(r, w2_ref[lo:lo + fc, :])
        x2v = x1 + g2 * y2
        rstd3 = lax.rsqrt(jnp.mean(x2v * x2v, axis=-1, keepdims=True) + EPS)
        xn3 = x2v * rstd3
        gfv = gf_ref[...]
        diff = xn3 * gfv - t_ref[...]
        loss_t = 0.5 * jnp.sum(jnp.sum(diff * diff, axis=-1, keepdims=True), axis=0, keepdims=True) * (1.0 / D_MODEL)
        dy = diff * (1.0 / D_MODEL)
        dxn3 = dy * gfv
        dx2 = rstd3 * (dxn3 - xn3 * jnp.mean(dxn3 * xn3, axis=-1, keepdims=True))
        dy2 = (dx2 * g2).astype(BF16)
        dy2_ref[...] = dy2
        dh2 = jnp.zeros((tm, D_MODEL), F32)
        for jj in range(n_ff):
            lo = jj * fc
            dr = _dot_nt(dy2, w2_ref[lo:lo + fc, :])
            da = (2.0 * ra_ref[jj] * dr).astype(BF16)
            da_ref[:, lo:lo + fc] = da
            dh2 = dh2 + _dot_nt(da, w1_ref[lo // FF_CHUNK, :, lo % FF_CHUNK:lo % FF_CHUNK + fc])
        dxn1 = dh2 * (1.0 + sc2)
        dx1 = dx2 + rstd2 * (dxn1 - xn1 * jnp.mean(dxn1 * xn1, axis=-1, keepdims=True))
        dx1_ref[...] = dx1
        dy1 = (dx1 * g1).astype(BF16)
        dy1_ref[...] = dy1
        for h in range(N_HEADS):
            do_ref[h] = _dot_nt(dy1, woa_ref[h]).astype(BF16)
        dconv = _dot_nt(dy1, woc_ref[...])
        dgb_ref[...] = dconv * yv
        dyv_ref[...] = dconv * gb
        acc_ref[5:6, :] += jnp.sum(dx1 * y1, axis=0, keepdims=True)
        acc_ref[0:1, :] += jnp.sum(dy * xn3, axis=0, keepdims=True)
        acc_ref[1:2, :] += jnp.sum(dx2 * y2, axis=0, keepdims=True)
        acc_ref[2:3, :] += jnp.sum(dh2, axis=0, keepdims=True)
        acc_ref[3:4, :] += jnp.sum(dh2 * xn1, axis=0, keepdims=True)
        acc_ref[4:5, :] += jnp.broadcast_to(loss_t, (1, D_MODEL))

    row = lambda i: (i, 0)
    gcp, gcn = _halo_specs(tm, 2)
    xip, xin = _halo_specs(tm, 3)
    tile = pl.BlockSpec((tm, D_MODEL), row)
    wide = pl.BlockSpec((tm, D_FF), row)
    half = pl.BlockSpec((tm, CONV_W), row)
    return pl.pallas_call(
        body, name="mlp_fwdbwd", grid=(n_lat,),
        out_shape=(jax.ShapeDtypeStruct((t_lat, D_FF), BF16), jax.ShapeDtypeStruct((t_lat, D_FF), BF16),
                   jax.ShapeDtypeStruct((t_lat, D_MODEL), BF16), jax.ShapeDtypeStruct((t_lat, D_MODEL), BF16),
                   jax.ShapeDtypeStruct((t_lat, D_MODEL), F32), jax.ShapeDtypeStruct((t_lat, CONV_W), BF16),
                   jax.ShapeDtypeStruct((8, D_MODEL), F32),
                   jax.ShapeDtypeStruct((t_lat, D_MODEL), BF16),
                   jax.ShapeDtypeStruct((N_HEADS, t_lat, HEAD_PAD), BF16),
                   jax.ShapeDtypeStruct((t_lat, CONV_W), F32), jax.ShapeDtypeStruct((t_lat, CONV_W), F32)),
        in_specs=[
            pl.BlockSpec((N_HEADS, tm, HEAD_PAD), lambda i: (0, i, 0)),
            pl.BlockSpec((tm, CONV_W), lambda i: (i, 1)), pl.BlockSpec((tm, CONV_W), lambda i: (i, 2)),
            pl.BlockSpec((tm, CONV_W), lambda i: (i, 3)),
            gcp, xip, gcn, xin,
            _const_spec(cw.shape), _resident_spec(wo_attn.shape), _resident_spec(wo_conv.shape),
            tile, tile, _const_spec(mod_b.shape), _const_spec(gf.shape),
            _resident_spec(w1.shape), _resident_spec(w2.shape),
        ],
        out_specs=(wide, wide, tile, tile, tile, half, _const_spec((8, D_MODEL)),
                   tile, pl.BlockSpec((N_HEADS, tm, HEAD_PAD), lambda i: (0, i, 0)), half, half),
        scratch_shapes=[pltpu.VMEM((n_ff, tm, fc), F32)],
        compiler_params=pltpu.CompilerParams(vmem_limit_bytes=VMEM_LIMIT),
    )(o, z, z, z, z, z, z, z, cw, wo_attn, wo_conv, x2, tgt, mod_b, gf, w1, w2)


def _inproj_bwd(x2, ctx2, mod_a, z, dyv, dgb, dx1, dqt, dk, dv, cos_t, sin_a, sin_b, w_in, w_uq, w_ukv, qg, kvg, cw):
    t_lat, t_ctx = x2.shape[0], ctx2.shape[0]
    tm = TOK_TILE
    n_lat = t_lat // tm
    n_all = n_lat + t_ctx // tm
    group = max(g for g in (1, 2, 4) if n_lat % g == 0)

    def body(x_ref, ctx_ref, mod_ref, z_ref, gcp_ref, xip_ref, gcn_ref, xin_ref, dyv_ref, dyvp_ref, dyvn_ref,
             dgb_ref, dx1_ref, dqt_ref, dk_ref, dv_ref, cos_ref, sa_ref, sb_ref, win_ref, wuq_ref, wukv_ref,
             qg_ref, kvg_ref, cw_ref, gx_ref, dwin_ref, dwuq_ref, dwukv_ref, acc_ref, h1_buf, dz_buf):
        i = pl.program_id(0)
        lat = i < n_lat

        @pl.when(i == 0)
        def _():
            dwin_ref[...] = jnp.zeros(dwin_ref.shape, F32)
            dwuq_ref[...] = jnp.zeros(dwuq_ref.shape, F32)
            dwukv_ref[...] = jnp.zeros(dwukv_ref.shape, F32)
            acc_ref[...] = jnp.zeros(acc_ref.shape, F32)

        xin = jnp.where(lat, x_ref[...], ctx_ref[...])
        rstd = lax.rsqrt(jnp.mean(xin * xin, axis=-1, keepdims=True) + EPS)
        xn = xin * rstd
        sc = mod_ref[0, 1:2, :]
        h1 = (xn * (1.0 + sc) + mod_ref[0, 0:1, :]).astype(BF16)
        z = z_ref[...]
        cos, sa, sb = cos_ref[...], sa_ref[...], sb_ref[...]
        qgv, kvgv = qg_ref[...], kvg_ref[...]
        cq = z[:, 0:Q_RANK]
        cqh = cq * lax.rsqrt(jnp.mean(cq * cq, axis=-1, keepdims=True) + EPS)
        rq = lax.rsqrt(jnp.mean(cq * cq, axis=-1, keepdims=True) + EPS)
        cqn = (cqh * qgv).astype(BF16)
        parts = []
        for h in range(N_HEADS):
            g = jnp.where(lat, dqt_ref[h].T, 0.0)
            parts.append(_unrope(g, cos, sa, sb))
        dq = jnp.concatenate(parts, axis=1).astype(BF16)
        dcqn = _dot_nt(dq, wuq_ref[...])
        dwuq_ref[...] += _dot_tn(cqn, dq)
        acc_ref[4:5, 0:Q_RANK] += jnp.sum(dcqn * cqh, axis=0, keepdims=True)
        dxn = dcqn * qgv
        dcq = rq * (dxn - cqh * jnp.mean(dxn * cqh, axis=-1, keepdims=True))
        ckv = z[:, Q_RANK:Q_RANK + KV_RANK]
        rk = lax.rsqrt(jnp.mean(ckv * ckv, axis=-1, keepdims=True) + EPS)
        ckvh = ckv * rk
        ckvn = (ckvh * kvgv).astype(BF16)
        dks = [dk_ref[h] for h in range(N_HEADS)]
        dkr = dks[0]
        for h in range(1, N_HEADS):
            dkr = dkr + dks[h]
        dkv = jnp.concatenate(dks + [dv_ref[h] for h in range(N_HEADS)], axis=1).astype(BF16)
        dckvn = _dot_nt(dkv, wukv_ref[...])
        dwukv_ref[...] += _dot_tn(ckvn, dkv)
        acc_ref[5:6, 0:KV_RANK] += jnp.sum(dckvn * ckvh, axis=0, keepdims=True)
        dxn = dckvn * kvgv
        dckv = rk * (dxn - ckvh * jnp.mean(dxn * ckvh, axis=-1, keepdims=True))
        dkr = _unrope(dkr, cos, sa, sb)
        gb, gc, xi = z[:, 512:1024], z[:, 1024:1536], z[:, 1536:2048]
        u = gc * xi
        u_prev = jnp.where(i > 0, gcp_ref[7:8, :] * xip_ref[7:8, :], 0.0)
        u_next = jnp.where(i < n_lat - 1, gcn_ref[0:1, :] * xin_ref[0:1, :], 0.0)
        um1, up1 = _shift_rows(u, u_prev, u_next)
        dyv = jnp.where(lat, dyv_ref[...], 0.0)
        dyv_prev = jnp.where(jnp.logical_and(i > 0, lat), dyvp_ref[7:8, :], 0.0)
        dyv_next = jnp.where(i < n_lat - 1, dyvn_ref[0:1, :], 0.0)
        dyv_m1, dyv_p1 = _shift_rows(dyv, dyv_prev, dyv_next)
        du = cw_ref[0:1, :] * dyv_p1 + cw_ref[1:2, :] * dyv + cw_ref[2:3, :] * dyv_m1
        dgc = du * xi
        dxi = du * gc
        dgb = jnp.where(lat, dgb_ref[...], 0.0)
        acc_ref[6:7, 0:CONV_W] += jnp.sum(dyv * um1, axis=0, keepdims=True)
        acc_ref[7:8, 0:CONV_W] += jnp.sum(dyv * u, axis=0, keepdims=True)
        acc_ref[8:9, 0:CONV_W] += jnp.sum(dyv * up1, axis=0, keepdims=True)
        dz = jnp.concatenate([dcq, dckv, dkr, dgb, dgc, dxi], axis=1).astype(BF16)
        dh1 = _dot_nt(dz, win_ref[...])
        slot = i % group
        rows_g = pl.ds(pl.multiple_of(slot * tm, tm), tm)
        h1_buf[rows_g, :] = h1
        dz_buf[rows_g, :] = dz

        @pl.when(jnp.logical_and(lat, slot == group - 1))
        def _():
            dwin_ref[...] += _dot_tn(h1_buf[...], dz_buf[...])

        @pl.when(jnp.logical_not(lat))
        def _():
            dwin_ref[...] += _dot_tn(h1, dz)
        s_sh = jnp.sum(dh1, axis=0, keepdims=True)
        s_sc = jnp.sum(dh1 * xn, axis=0, keepdims=True)
        zero = jnp.zeros_like(s_sh)
        acc_ref[0:1, :] += jnp.where(lat, s_sh, zero)
        acc_ref[1:2, :] += jnp.where(lat, s_sc, zero)
        acc_ref[2:3, :] += jnp.where(lat, zero, s_sh)
        acc_ref[3:4, :] += jnp.where(lat, zero, s_sc)
        dxn = dh1 * (1.0 + sc)
        dx = rstd * (dxn - xn * jnp.mean(dxn * xn, axis=-1, keepdims=True))

        @pl.when(lat)
        def _():
            gx_ref[...] = dx1_ref[...] + dx

    last = n_lat - 1
    per = tm // 8
    lat_row = lambda i: (jnp.minimum(i, last), 0)
    row = lambda i: (i, 0)
    gcp, gcn = _halo_specs(tm, 2)
    xip, xin = _halo_specs(tm, 3)
    n_halo = t_lat // 8
    dyvp = pl.BlockSpec((8, CONV_W), lambda i: (jnp.clip(i * per - 1, 0, n_halo - 1), 0))
    dyvn = pl.BlockSpec((8, CONV_W), lambda i: (jnp.minimum((i + 1) * per, n_halo - 1), 0))
    gcn = pl.BlockSpec((8, CONV_W), lambda i: (jnp.minimum((i + 1) * per, (t_lat + t_ctx) // 8 - 1), 2))
    xin = pl.BlockSpec((8, CONV_W), lambda i: (jnp.minimum((i + 1) * per, (t_lat + t_ctx) // 8 - 1), 3))
    head_f32 = pl.BlockSpec((N_HEADS, tm, HEAD_PAD), lambda i: (0, i, 0))
    tab = pl.BlockSpec((tm, HEAD_PAD), row)
    return pl.pallas_call(
        body, name="inproj_bwd", grid=(n_all,),
        out_shape=(jax.ShapeDtypeStruct((t_lat, D_MODEL), F32), jax.ShapeDtypeStruct(w_in.shape, F32),
                   jax.ShapeDtypeStruct(w_uq.shape, F32), jax.ShapeDtypeStruct(w_ukv.shape, F32),
                   jax.ShapeDtypeStruct((16, D_MODEL), F32)),
        in_specs=[
            pl.BlockSpec((tm, D_MODEL), lat_row), _const_spec((tm, D_MODEL)),
            pl.BlockSpec((1, 8, D_MODEL), lambda i: (i // n_lat, 0, 0)),
            pl.BlockSpec((tm, Z_COLS), row), gcp, xip, gcn, xin,
            pl.BlockSpec((tm, CONV_W), lat_row), dyvp, dyvn,
            pl.BlockSpec((tm, CONV_W), lat_row), pl.BlockSpec((tm, D_MODEL), lat_row),
            pl.BlockSpec((N_HEADS, HEAD_PAD, tm), lambda i: (0, 0, jnp.minimum(i, last))),
            head_f32, head_f32, tab, tab, tab,
            _const_spec(w_in.shape), _const_spec(w_uq.shape), _const_spec(w_ukv.shape),
            _const_spec(qg.shape), _const_spec(kvg.shape), _const_spec(cw.shape),
        ],
        out_specs=(pl.BlockSpec((tm, D_MODEL), lat_row), _const_spec(w_in.shape), _const_spec(w_uq.shape),
                   _const_spec(w_ukv.shape), _const_spec((16, D_MODEL))),
        scratch_shapes=[pltpu.VMEM((group * tm, D_MODEL), BF16), pltpu.VMEM((group * tm, Z_COLS), BF16)],
        compiler_params=pltpu.CompilerParams(vmem_limit_bytes=VMEM_LIMIT),
    )(x2, ctx2, mod_a, z, z, z, z, z, dyv, dyv, dyv, dgb, dx1, dqt, dk, dv, cos_t, sin_a, sin_b, w_in, w_uq, w_ukv,
      qg, kvg, cw)


def _wgrad(a, b, name, bm, bn):
    t, m = a.shape
    n = b.shape[1]
    bk = min(t, 4096)
    nk = t // bk
    nj = n // bn

    def body(a_ref, b_ref, o_ref, acc_ref):
        k = pl.program_id(2)
        part = _dot_tn(a_ref[...], b_ref[...])

        @pl.when(k == 0)
        def _():
            acc_ref[...] = part

        @pl.when(k > 0)
        def _():
            acc_ref[...] += part

        @pl.when(k == nk - 1)
        def _():
            o_ref[...] = acc_ref[...].astype(BF16)

    return pl.pallas_call(
        body, name=name, grid=(m // bm, nj, nk), out_shape=jax.ShapeDtypeStruct((m // bm * nj, bm, bn), BF16),
        in_specs=[pl.BlockSpec((bk, bm), lambda i, j, k: (k, i)), pl.BlockSpec((bk, bn), lambda i, j, k: (k, j))],
        out_specs=pl.BlockSpec((None, bm, bn), lambda i, j, k: (i * nj + j, 0, 0)),
        scratch_shapes=[pltpu.VMEM((bm, bn), F32)],
        compiler_params=pltpu.CompilerParams(vmem_limit_bytes=VMEM_LIMIT),
    )(a, b)


def _wgrad_out(o, conv, dy1, sib_arrays):
    t = o.shape[1]
    bk = min(t, 2048)
    nk = t // bk
    rows = N_HEADS * HEAD_PAD + CONV_W
    n_s = len(sib_arrays)

    def body(o_ref, c_ref, d_ref, *rest):
        w_ref, got_w_ref, acc_ref = rest[n_s], rest[2 * n_s + 1], rest[2 * n_s + 2]
        send = _SiblingSend(rest[:n_s], rest[n_s + 1:2 * n_s + 1], rest[2 * n_s + 3], rest[2 * n_s + 4])
        wsend, wrecv = rest[2 * n_s + 5], rest[2 * n_s + 6]
        k = pl.program_id(0)
        pl.when(k == 0)(send.start)
        cat = jnp.concatenate([o_ref[h] for h in range(N_HEADS)] + [c_ref[...]], axis=1)
        part = _dot_tn(cat, d_ref[...])

        @pl.when(k == 0)
        def _():
            acc_ref[...] = part

        @pl.when(k > 0)
        def _():
            acc_ref[...] += part

        @pl.when(k == nk - 1)
        def _():
            for h in range(N_HEADS):
                w_ref[h * V_DIM:(h + 1) * V_DIM, :] = acc_ref[h * HEAD_PAD:h * HEAD_PAD + V_DIM, :].astype(BF16)
            w_ref[N_HEADS * V_DIM:, :] = acc_ref[N_HEADS * HEAD_PAD:, :].astype(BF16)
            x, y, c = _pos()
            own = []
            for s in range(N_SHARD):
                theirs = pl.ds(pl.multiple_of(s * shard_rows + (1 - c) * (shard_rows // 2), 16), shard_rows // 2)
                cp = pltpu.make_async_remote_copy(
                    src_ref=w_ref.at[theirs], dst_ref=got_w_ref.at[s], send_sem=wsend.at[s], recv_sem=wrecv.at[s],
                    device_id=(x, y, 1 - c), device_id_type=MESH)
                cp.start()
                own.append(cp)
            for cp in own:
                cp.wait()

        pl.when(k == nk - 1)(send.finish)

    hbm = pl.BlockSpec(memory_space=pl.ANY)
    shard_rows = D_MODEL // N_SHARD
    return pl.pallas_call(
        body, name="wgrad_out", grid=(nk,),
        out_shape=(jax.ShapeDtypeStruct((D_MODEL, D_MODEL), BF16),)
        + tuple(jax.ShapeDtypeStruct((N_SHARD, a.shape[1] // 2, a.shape[2]), BF16) for a in sib_arrays)
        + (jax.ShapeDtypeStruct((N_SHARD, shard_rows // 2, D_MODEL), BF16),),
        in_specs=[pl.BlockSpec((N_HEADS, bk, HEAD_PAD), lambda k: (0, k, 0)),
                  pl.BlockSpec((bk, CONV_W), lambda k: (k, 0)),
                  pl.BlockSpec((bk, D_MODEL), lambda k: (k, 0))] + [hbm] * n_s,
        out_specs=(_const_spec((D_MODEL, D_MODEL)),) + (hbm,) * (n_s + 1),
        scratch_shapes=[pltpu.VMEM((rows, D_MODEL), F32), pltpu.SemaphoreType.DMA((n_s,)),
                        pltpu.SemaphoreType.DMA((n_s,)), pltpu.SemaphoreType.DMA((N_SHARD,)),
                        pltpu.SemaphoreType.DMA((N_SHARD,))],
        compiler_params=pltpu.CompilerParams(vmem_limit_bytes=VMEM_LIMIT),
    )(o, conv, dy1, *sib_arrays)


def _adamw_call(w, g, m, v, name):
    rows, cols = w.shape
    rb = 256 if rows % 256 == 0 else rows

    def body(w_ref, g_ref, m_ref, v_ref, d_ref, nm_ref, nv_ref):
        d_, m_, v_ = _adamw(w_ref[...], g_ref[...], m_ref[...], v_ref[...])
        d_ref[...] = d_
        nm_ref[...] = m_
        nv_ref[...] = v_

    spec = pl.BlockSpec((rb, cols), lambda i: (i, 0))
    shp = jax.ShapeDtypeStruct((rows, cols), F32)
    return pl.pallas_call(
        body, name=name, grid=(rows // rb,), out_shape=(shp, shp, shp),
        in_specs=[spec] * 4, out_specs=(spec, spec, spec),
    )(w, g, m, v)


def _adamw_halves(w, g_mine, g_theirs, m, v, c_idx, name):
    rows, cols = w.shape
    half = rows // 2
    rb = min(256, half)
    nb = half // rb

    def body(c_ref, w_ref, gm_ref, gt_ref, m_ref, v_ref, g_ref, d_ref, nm_ref, nv_ref):
        mine = pl.program_id(0) // nb == c_ref[0]
        g = jnp.where(mine, gm_ref[...], gt_ref[...])
        d_, m_, v_ = _adamw(w_ref[...], g, m_ref[...], v_ref[...])
        g_ref[...] = g
        d_ref[...] = d_
        nm_ref[...] = m_
        nv_ref[...] = v_

    spec = pl.BlockSpec((rb, cols), lambda i, c_ref: (i, 0))
    hspec = pl.BlockSpec((rb, cols), lambda i, c_ref: (i % nb, 0))
    shp = jax.ShapeDtypeStruct((rows, cols), F32)
    grid_spec = pltpu.PrefetchScalarGridSpec(
        num_scalar_prefetch=1, grid=(rows // rb,), in_specs=[spec, hspec, hspec, spec, spec],
        out_specs=(spec, spec, spec, spec))
    return pl.pallas_call(
        body, name=name, grid_spec=grid_spec, out_shape=(shp, shp, shp, shp),
    )(c_idx, w, g_mine, g_theirs, m, v)


def _wmod_update(s_t, dm, w, m, v):
    rows, cols = w.shape
    cb = 512

    def body(s_ref, dm_ref, w_ref, m_ref, v_ref, g_ref, d_ref, nm_ref, nv_ref):
        g = jnp.dot(s_ref[...], dm_ref[...], precision=HIGHEST, preferred_element_type=F32)
        d_, m_, v_ = _adamw(w_ref[...], g, m_ref[...], v_ref[...])
        g_ref[...] = g
        d_ref[...] = d_
        nm_ref[...] = m_
        nv_ref[...] = v_

    spec = pl.BlockSpec((rows, cb), lambda i: (0, i))
    shp = jax.ShapeDtypeStruct((rows, cols), F32)
    return pl.pallas_call(
        body, name="wmod_update", grid=(cols // cb,), out_shape=(shp, shp, shp, shp),
        in_specs=[_const_spec(s_t.shape), pl.BlockSpec((16, cb), lambda i: (0, i)), spec, spec, spec],
        out_specs=(spec, spec, spec, spec),
        compiler_params=pltpu.CompilerParams(vmem_limit_bytes=VMEM_LIMIT),
    )(s_t, dm, w, m, v)


def _rope_tables(t_lat, t_ctx):
    t = jnp.arange(t_lat)
    pos = jnp.stack([(t // GRID_W).astype(F32), (t % GRID_W).astype(F32)], axis=1)
    half = QK_ROPE // 4
    freqs = ROPE_THETA ** (-jnp.arange(0, 2 * half, 2, dtype=F32) / (2 * half))
    ang = pos[:, :, None] * freqs[None, None, :]
    cos, sin = jnp.cos(ang), jnp.sin(ang)
    zero = jnp.zeros_like(sin)
    cos32 = jnp.concatenate([cos, cos], axis=2).reshape(t_lat, QK_ROPE)
    sa32 = jnp.concatenate([zero, sin], axis=2).reshape(t_lat, QK_ROPE)
    sb32 = jnp.concatenate([-sin, zero], axis=2).reshape(t_lat, QK_ROPE)

    def widen(tab, fill):
        left = jnp.full((t_lat, ROPE_LANE0), fill, F32)
        right = jnp.full((t_lat, HEAD_PAD - ROPE_LANE0 - QK_ROPE), fill, F32)
        lat = jnp.concatenate([left, tab, right], axis=1)
        return jnp.concatenate([lat, jnp.full((t_ctx, HEAD_PAD), fill, F32)], axis=0)

    return widen(cos32, 1.0), widen(sa32, 0.0), widen(sb32, 0.0)


def _cols_from_shards(s):
    return jnp.transpose(s, (1, 0, 2)).reshape(s.shape[1], -1)


def _cols_to_shards(w):
    k, n = w.shape
    return jnp.transpose(w.reshape(k, N_SHARD, n // N_SHARD), (1, 0, 2))


def kernel(x, c, ctx, c_ctx, w_mod, b_mod, w_in, q_norm_g, w_uq, kv_norm_g, w_ukv, conv_w, w_out, w_mlp1, w_mlp2, final_norm_g, loss_target, m_c_ctx, m_w_mod, m_b_mod, m_w_in, m_q_norm_g, m_w_uq, m_kv_norm_g, m_w_ukv, m_conv_w, m_w_out, m_w_mlp1, m_w_mlp2, m_final_norm_g, v_c_ctx, v_w_mod, v_b_mod, v_w_in, v_q_norm_g, v_w_uq, v_kv_norm_g, v_w_ukv, v_conv_w, v_w_out, v_w_mlp1, v_w_mlp2, v_final_norm_g):
    t_lat, t_ctx = x.shape[1], ctx.shape[1]
    assert t_ctx == TOK_TILE and t_lat % TOK_TILE == 0 and t_lat % GRID_W == 0
    mx, my, mc = _pos()
    j = 2 * mx + my
    ncol = w_mod.shape[2]
    x2, ctx2, tgt = x[0], ctx[0], loss_target[0]
    cctx_row = c_ctx.reshape(1, D_MODEL)

    b_sh = lax.dynamic_slice(b_mod, (0, j * ncol), (1, ncol))
    cw_pad = jnp.zeros((8, 128), F32).at[0:3, :].set(conv_w[0])
    c8, m_all, g_in, g_uq, g_ukv, g_out, g_m1, g_m2 = _prologue(
        c, cctx_row, w_mod[0], b_sh, cw_pad, (w_in[0], w_uq[0], w_ukv[0], w_out[0], w_mlp1[0], w_mlp2[0]), 3)
    mvec = m_all[:, 0, :].reshape(6, D_MODEL)
    mctx = m_all[:, 8, :].reshape(6, D_MODEL)
    zeros6 = jnp.zeros((6, D_MODEL), F32)
    mod_a = jnp.stack([jnp.concatenate([mvec[0:2], zeros6], axis=0), jnp.concatenate([mctx[0:2], zeros6], axis=0)])
    mod_b = jnp.concatenate([mvec[2:6], jnp.zeros((4, D_MODEL), F32)], axis=0)
    cw_full = jnp.pad(jnp.transpose(m_all[:, 9:12, 0:128], (1, 0, 2)).reshape(3, CONV_W), ((0, 5), (0, 0)))

    w_in_f = _cols_from_shards(g_in)
    zc = lambda n: jnp.zeros((D_MODEL, n), BF16)
    w_in_p = jnp.concatenate([w_in_f[:, 0:384], zc(64), w_in_f[:, 384:416], zc(32), w_in_f[:, 416:]], axis=1)
    w_uq_f = _cols_from_shards(g_uq).reshape(Q_RANK, N_HEADS, QK_DIM)
    w_uq_p = jnp.pad(w_uq_f, ((0, 0), (0, 0), (0, HEAD_PAD - QK_DIM))).reshape(Q_RANK, N_HEADS * HEAD_PAD)
    w_ukv_f = _cols_from_shards(g_ukv).reshape(KV_RANK, N_HEADS, QK_NOPE + V_DIM)
    padh = lambda a: jnp.pad(a, ((0, 0), (0, 0), (0, HEAD_PAD - a.shape[2]))).reshape(KV_RANK, N_HEADS * HEAD_PAD)
    w_ukv_p = jnp.concatenate([padh(w_ukv_f[:, :, :QK_NOPE]), padh(w_ukv_f[:, :, QK_NOPE:])], axis=1)
    cos_t, sin_a, sin_b = _rope_tables(t_lat, t_ctx)
    gf_row = final_norm_g.reshape(1, D_MODEL)
    c_idx = mc.reshape(1).astype(jnp.int32)
    j_idx = j.reshape(1).astype(jnp.int32)

    z, q, k, v, kt = _inproj_fwd(x2, ctx2, mod_a, w_in_p, q_norm_g, kv_norm_g, w_uq_p, w_ukv_p, cos_t, sin_a, sin_b)
    o, lse, g_out, w1, g_m2 = _attn_fwd(q, k, v, t_lat, (g_out, g_m1, g_m2))
    w_out_f = g_out.reshape(D_MODEL, D_MODEL)
    wo_attn = jnp.pad(w_out_f[0:512].reshape(N_HEADS, V_DIM, D_MODEL), ((0, 0), (0, HEAD_PAD - V_DIM), (0, 0)))
    wo_conv = w_out_f[512:]
    w2 = g_m2.reshape(D_FF, D_MODEL)
    r, da, h2, dy2, dx1, conv, acc_mlp, dy1, do, dgb, dyv = _mlp_fwdbwd(o, z, x2, tgt, mod_b, gf_row, cw_full, wo_attn,
                                                                         wo_conv, w1, w2)
    d_w1 = _wgrad(h2, da, "wgrad_mlp1", D_MODEL, FF_CHUNK)
    d_w2 = _wgrad(r, dy2, "wgrad_mlp2", FF_CHUNK, D_MODEL)
    d_wout, *big_got = _wgrad_out(o, conv, dy1, (d_w1, d_w2))
    d_wout = d_wout.reshape(N_SHARD, D_MODEL // N_SHARD, D_MODEL)
    big_grads, big_names = (d_w1, d_w2, d_wout), ("mlp1", "mlp2", "out")
    big_parts = [_add_pairs(a, g, c_idx, "rs_add_pairs_" + n) for a, g, n in zip(big_grads, big_got, big_names)]
    dqt, dk, dv, *big_recv = _attn_bwd(q, k, v, kt, o, do, lse, t_lat, big_parts)
    big_halves = [_add_chips(p, g, j_idx, "rs_add_chips_" + n) for p, g, n in zip(big_parts, big_recv, big_names)]
    gx, d_win, d_wuq, d_wukv, acc_in = _inproj_bwd(x2, ctx2, mod_a, z, dyv, dgb, dx1, dqt, dk, dv, cos_t, sin_a, sin_b,
                                                   w_in_p, w_uq_p, w_ukv_p, q_norm_g, kv_norm_g, cw_full)

    d_win_f = jnp.concatenate([d_win[:, 0:384], d_win[:, 448:480], d_win[:, 512:]], axis=1)
    d_wuq_f = d_wuq.reshape(Q_RANK, N_HEADS, HEAD_PAD)[:, :, 0:QK_DIM].reshape(Q_RANK, N_HEADS * QK_DIM)
    d_wukv3 = d_wukv.reshape(KV_RANK, 2, N_HEADS, HEAD_PAD)
    d_wukv_f = jnp.concatenate([d_wukv3[:, 0, :, 0:QK_NOPE], d_wukv3[:, 1, :, 0:V_DIM]], axis=2).reshape(KV_RANK, -1)
    rest = tuple(_cols_to_shards(a).astype(BF16) for a in (d_win_f, d_wuq_f, d_wukv_f))
    rest_names = ("in", "uq", "ukv")
    rest_got = _rs_sibling(rest, "rs_sibling_rest")
    rest_parts = [_add_pairs(a, g, c_idx, "rs_add_pairs_" + n) for a, g, n in zip(rest, rest_got, rest_names)]

    sv = jnp.concatenate([
        acc_in[0:2], acc_mlp[5:6], acc_mlp[2:4], acc_mlp[1:2],
        acc_in[2:4], acc_mlp[0:1], acc_in[4:5], acc_in[5:6], acc_in[6:9], acc_mlp[4:5],
        jnp.zeros((1, D_MODEL), F32)], axis=0)
    all_sv, red, o_cc, o_b, o_q, o_k, o_gf, *exchanged = _small_exchange(
        sv, w_mod[0], cctx_row, m_c_ctx.reshape(1, D_MODEL), v_c_ctx.reshape(1, D_MODEL),
        b_mod.reshape(6, D_MODEL), m_b_mod.reshape(6, D_MODEL), v_b_mod.reshape(6, D_MODEL),
        q_norm_g, m_q_norm_g, v_q_norm_g, kv_norm_g, m_kv_norm_g, v_kv_norm_g,
        gf_row, m_final_norm_g.reshape(1, D_MODEL), v_final_norm_g.reshape(1, D_MODEL), big_halves, rest_parts)
    big_theirs, rest_recv = exchanged[:len(big_halves)], exchanged[len(big_halves):]
    loss = red[14, 0]

    c9 = jnp.concatenate([c8[0::8], jnp.zeros((7, D_MODEL), F32)], axis=0)
    s_t = jnp.transpose(c9 * jax.nn.sigmoid(c9))
    dm_ex = all_sv[:, 0:6, :].reshape(8, 6 * D_MODEL)
    dm_ctx = jnp.concatenate([red[6:8].reshape(1, 2 * D_MODEL), jnp.zeros((1, 4 * D_MODEL), F32)], axis=1)
    dm16 = jnp.concatenate([dm_ex, dm_ctx, jnp.zeros((7, 6 * D_MODEL), F32)], axis=0)
    dm_sh = lax.dynamic_slice(dm16, (0, j * ncol), (16, ncol))
    g_wmod, d_wmod, nm_wmod, nv_wmod = _wmod_update(s_t, dm_sh, w_mod[0], m_w_mod[0], v_w_mod[0])

    g_cw = lax.dynamic_slice(red[11:14, 0:CONV_W], (0, j * 128), (3, 128))
    d_cw, nm_cw, nv_cw = _adamw_call(conv_w[0], g_cw, m_conv_w[0], v_conv_w[0], "adamw_conv")

    rest_halves = [_add_chips(p, g, j_idx, "rs_add_chips_" + n) for p, g, n in zip(rest_parts, rest_recv, rest_names)]
    g_win, g_wuq, g_wukv = _rs_join(rest_halves, "rs_join_rest")
    upd = {}
    for name, w_, g_, m_, v_ in (("in", w_in, g_win, m_w_in, v_w_in), ("uq", w_uq, g_wuq, m_w_uq, v_w_uq),
                                 ("ukv", w_ukv, g_wukv, m_w_ukv, v_w_ukv)):
        upd[name] = _adamw_call(w_[0], g_, m_[0], v_[0], "adamw_" + name)
    g_w1, *upd["mlp1"] = _adamw_halves(w_mlp1[0], big_halves[0], big_theirs[0], m_w_mlp1[0], v_w_mlp1[0], c_idx,
                                       "adamw_mlp1")
    g_w2, *upd["mlp2"] = _adamw_halves(w_mlp2[0], big_halves[1], big_theirs[1], m_w_mlp2[0], v_w_mlp2[0], c_idx,
                                       "adamw_mlp2")
    g_wout, *upd["out"] = _adamw_halves(w_out[0], big_halves[2], big_theirs[2], m_w_out[0], v_w_out[0], c_idx,
                                        "adamw_out")

    def four(o4, shape):
        return [o4[r].reshape(shape) for r in range(4)]

    cc4 = four(o_cc, (D_MODEL,))
    b4 = [o_b[r].reshape(1, 6 * D_MODEL) for r in range(4)]
    q4 = four(o_q, (1, Q_RANK))
    k4 = four(o_k, (1, KV_RANK))
    gf4 = four(o_gf, (D_MODEL,))
    big = {"in": g_win, "uq": g_wuq, "ukv": g_wukv, "out": g_wout, "mlp1": g_w1, "mlp2": g_w2}

    def leaf(idx):
        wm = (g_wmod, d_wmod, nm_wmod, nv_wmod)[idx]
        cwv = (g_cw, d_cw, nm_cw, nv_cw)[idx]
        bigv = {n: (big[n] if idx == 0 else upd[n][idx - 1]) for n in big}
        return [cc4[idx], wm[None], b4[idx], bigv["in"][None], q4[idx], bigv["uq"][None], k4[idx], bigv["ukv"][None],
                cwv[None], bigv["out"][None], bigv["mlp1"][None], bigv["mlp2"][None], gf4[idx]]

    return (loss, gx[None], *leaf(0), *leaf(1), *leaf(2), *leaf(3))
```

```python
import functools
import math

import jax
import jax.numpy as jnp
from jax import lax
from jax.experimental import pallas as pl
from jax.experimental.pallas import tpu as pltpu

F32 = jnp.float32
BF16 = jnp.bfloat16
MESH = pl.DeviceIdType.MESH
HIGHEST = lax.Precision.HIGHEST

D_MODEL = 1024
N_HEADS = 8
QK_NOPE = 64
QK_ROPE = 32
QK_DIM = QK_NOPE + QK_ROPE
V_DIM = 64
Q_RANK = 256
KV_RANK = 128
CONV_W = 512
D_FF = 4096
GRID_W = 64
ROPE_THETA = 10000.0
EPS = 1e-6
ATTN_SCALE = 1.0 / math.sqrt(QK_DIM)
HEAD_PAD = 128
Z_COLS = 2048
ROPE_LANE0 = QK_NOPE
N_SHARD = 4
TOK_TILE = 256
FF_CHUNK = 1024
MLP_FF_CHUNK = 1024
KEY_CHUNK = 512
ATTN_FWD_Q_BLOCK = 1024
ATTN_HEADS_PER_STEP = 4
ATTN_BWD_HEADS_PER_STEP = 2
ATTN_BWD_Q_BLOCK = 512
KEY_CHUNK_BWD = 512

ADAM_LR = 0.001
ADAM_B1 = 0.9
ADAM_B2 = 0.999
ADAM_EPS = 1e-08
ADAM_WD = 0.01
ADAM_STEP = 10

LOG2E = 1.4426950408889634

VMEM_LIMIT = 56 * 1024 * 1024
STAGE_VMEM_LIMIT = 32 * 1024 * 1024


def _pos():
    return lax.axis_index("x"), lax.axis_index("y"), lax.axis_index("c")


def _dot(a, b):
    return jnp.dot(a, b, preferred_element_type=F32)


def _dot_nt(a, b):
    return lax.dot_general(a, b, (((1,), (1,)), ((), ())), preferred_element_type=F32)


def _dot_tn(a, b):
    return lax.dot_general(a, b, (((0,), (0,)), ((), ())), preferred_element_type=F32)


def _rope(v, cos, sa, sb):
    return v * cos + pltpu.roll(v, 8, 1) * sa + pltpu.roll(v, HEAD_PAD - 8, 1) * sb


def _unrope(g, cos, sa, sb):
    return g * cos + pltpu.roll(g * sa, HEAD_PAD - 8, 1) + pltpu.roll(g * sb, 8, 1)


def _sigmoid(v):
    return 1.0 / (1.0 + jnp.exp(-v))


def _adamw(w, g, m, v):
    m = ADAM_B1 * m + (1.0 - ADAM_B1) * g
    v = ADAM_B2 * v + (1.0 - ADAM_B2) * (g * g)
    m_hat = m / (1.0 - ADAM_B1 ** ADAM_STEP)
    v_hat = v / (1.0 - ADAM_B2 ** ADAM_STEP)
    delta = -ADAM_LR * (m_hat / (jnp.sqrt(v_hat) + ADAM_EPS) + ADAM_WD * w)
    return delta, m, v


def _shift_rows(u, prev_row, next_row):
    n = u.shape[0]
    rows = lax.broadcasted_iota(jnp.int32, u.shape, 0)
    um1 = jnp.where(rows == 0, prev_row, pltpu.roll(u, 1, 0))
    up1 = jnp.where(rows == n - 1, next_row, pltpu.roll(u, n - 1, 0))
    return um1, up1


def _const_spec(shape):
    nd = len(shape)
    return pl.BlockSpec(shape, lambda *_: (0,) * nd)


def _resident_spec(shape):
    nd = len(shape)
    return pl.BlockSpec(shape, lambda *_: (0,) * nd, pipeline_mode=pl.Buffered(1))


def _peer(r, x, y, c):
    px = 1 - x if r & 4 else x
    py = 1 - y if r & 2 else y
    pc = 1 - c if r & 1 else c
    return (px, py, pc)


def _prologue(c_row, cctx_row, w_mod_sh, b_sh, cw_sh, srcs, n_gather):
    ncol = w_mod_sh.shape[1]
    n = len(srcs)
    n_split = 4

    def body(c_ref, cctx_ref, w_ref, b_ref, cw_ref, *refs):
        ins, (c8_ref, m_ref), outs = refs[:n], refs[n:n + 2], refs[n + 2:2 * n + 2]
        mine_ref, msh_ref = refs[2 * n + 2:2 * n + 4]
        f32s, bfs = refs[2 * n + 4:3 * n + 4], refs[3 * n + 4:4 * n + 4]
        ssem, rsem, ssem2, rsem2, lsem_in, lsem_out = refs[4 * n + 4:4 * n + 10]
        x, y, c = _pos()
        me = 4 * x + 2 * y + c
        j = 2 * x + y

        def pieces(rows):
            step = rows // n_split
            return [pl.ds(q * step, step) for q in range(n_split)]

        for t in range(n):
            for sl in pieces(ins[t].shape[0]):
                pltpu.make_async_copy(ins[t].at[sl], f32s[t].at[sl], lsem_in.at[t]).start()
        mine_ref[...] = jnp.zeros(mine_ref.shape, F32)
        mine_ref[0:1, :] = c_ref[...]
        my_rows = pl.ds(pl.multiple_of(8 * me, 8), 8)
        sends = []
        for r in range(1, 8):
            cp = pltpu.make_async_remote_copy(
                src_ref=mine_ref, dst_ref=c8_ref.at[my_rows], send_sem=ssem.at[r - 1], recv_sem=rsem.at[r - 1],
                device_id=_peer(r, x, y, c), device_id_type=MESH)
            cp.start()
            sends.append(cp)

        def cast_and_store(t):
            pltpu.make_async_copy(ins[t], f32s[t], lsem_in.at[t]).wait()
            bfs[t][...] = f32s[t][...].astype(BF16)
            for sl in pieces(ins[t].shape[0]):
                pltpu.make_async_copy(bfs[t].at[sl], outs[t].at[j, sl], lsem_out.at[t]).start()

        gather = _ShardGather(outs[:n_gather], *refs[4 * n + 10:])
        for t in range(n_gather):
            cast_and_store(t)
        for t in range(n_gather):
            pltpu.make_async_copy(bfs[t], outs[t].at[j], lsem_out.at[t]).wait()
        gather.start()
        for cp in sends:
            cp.wait()
        c8_ref[my_rows, :] = mine_ref[...]
        c8_ref[64:72, :] = jnp.zeros((8, D_MODEL), F32)
        c8_ref[64:65, :] = cctx_ref[...]
        cv = c8_ref[...]
        s = cv * _sigmoid(cv)
        m = jnp.dot(s, w_ref[...], precision=HIGHEST, preferred_element_type=F32) + b_ref[...]
        msh_ref[0:64, :] = m[0:64, :]
        msh_ref[64:72, :] = jnp.zeros((8, ncol), F32)
        msh_ref[64:65, :] = m[64:65, :]
        msh_ref[65:68, 0:128] = cw_ref[0:3, :]
        m_ref[j, 0:8, :] = msh_ref[my_rows, :]
        m_ref[j, 8:16, :] = msh_ref[64:72, :]
        sends2 = []
        for k, (px, py) in enumerate(_chips(x, y)):
            theirs = pl.ds(pl.multiple_of(8 * (4 * px + 2 * py + c), 8), 8)
            for half, src in enumerate((msh_ref.at[theirs], msh_ref.at[64:72])):
                cp = pltpu.make_async_remote_copy(
                    src_ref=src, dst_ref=m_ref.at[j, 8 * half:8 * half + 8], send_sem=ssem2.at[2 * k + half],
                    recv_sem=rsem2.at[2 * k + half], device_id=(px, py, c), device_id_type=MESH)
                cp.start()
                sends2.append(cp)
        for t in range(n_gather, n):
            cast_and_store(t)
        gather.forward()
        gather.finish()
        for t in range(n_gather, n):
            pltpu.make_async_copy(bfs[t], outs[t].at[j], lsem_out.at[t]).wait()
        for cp in sends2:
            cp.wait()

    vm = pl.BlockSpec(memory_space=pltpu.VMEM)
    hbm = pl.BlockSpec(memory_space=pl.ANY)
    return pl.pallas_call(
        body, name="prologue",
        out_shape=(jax.ShapeDtypeStruct((72, D_MODEL), F32), jax.ShapeDtypeStruct((N_SHARD, 16, ncol), F32))
        + tuple(jax.ShapeDtypeStruct((N_SHARD,) + a.shape, BF16) for a in srcs),
        in_specs=[vm] * 5 + [hbm] * n, out_specs=(vm, vm) + (hbm,) * n,
        scratch_shapes=[pltpu.VMEM((8, D_MODEL), F32), pltpu.VMEM((72, ncol), F32)]
        + [pltpu.VMEM(a.shape, F32) for a in srcs] + [pltpu.VMEM(a.shape, BF16) for a in srcs]
        + [pltpu.SemaphoreType.DMA((7,)), pltpu.SemaphoreType.DMA((7,)),
           pltpu.SemaphoreType.DMA((6,)), pltpu.SemaphoreType.DMA((6,)),
           pltpu.SemaphoreType.DMA((n,)), pltpu.SemaphoreType.DMA((n,))] + _gather_sems(n_gather),
        compiler_params=pltpu.CompilerParams(vmem_limit_bytes=VMEM_LIMIT),
    )(c_row, cctx_row, w_mod_sh, b_sh, cw_sh, *srcs)


def _chips(x, y):
    return [(1 - x, y), (x, 1 - y), (1 - x, 1 - y)]


def _halves(ref, c, align):
    hr = ref.shape[-2] // 2
    return (pl.ds(pl.multiple_of(c * hr, align), hr), pl.ds(pl.multiple_of((1 - c) * hr, align), hr))


class _ShardGather:
    def __init__(self, refs, ssem, rsem, fsend, frecv):
        self.refs, self.sems = refs, (ssem, rsem, fsend, frecv)
        self.x, self.y, self.c = _pos()
        self.j = 2 * self.x + self.y

    def _ici(self, a, k, slot):
        g = self.refs[a]
        ssem, rsem, _, _ = self.sems
        mine, _ = _halves(g, self.c, 16)
        px, py = _chips(self.x, self.y)[k]
        return pltpu.make_async_remote_copy(
            src_ref=g.at[self.j, mine], dst_ref=g.at[slot, mine], send_sem=ssem.at[3 * a + k],
            recv_sem=rsem.at[3 * a + k], device_id=(px, py, self.c), device_id_type=MESH)

    def _d2d(self, a, k, to_other_half):
        g = self.refs[a]
        _, _, fsend, frecv = self.sems
        mine, theirs = _halves(g, self.c, 16)
        px, py = _chips(self.x, self.y)[k]
        jk = 2 * px + py
        return pltpu.make_async_remote_copy(
            src_ref=g.at[jk, mine], dst_ref=g.at[jk, theirs if to_other_half else mine],
            send_sem=fsend.at[3 * a + k], recv_sem=frecv.at[3 * a + k],
            device_id=(self.x, self.y, 1 - self.c), device_id_type=MESH)

    def start(self):
        for a in range(len(self.refs)):
            for k in range(3):
                self._ici(a, k, self.j).start()

    def forward(self):
        for a in range(len(self.refs)):
            for k, (px, py) in enumerate(_chips(self.x, self.y)):
                self._ici(a, k, 2 * px + py).wait_recv()
                self._d2d(a, k, False).start()

    def finish(self):
        for a in range(len(self.refs)):
            for k in range(3):
                self._d2d(a, k, True).wait()
                self._ici(a, k, self.j).wait_send()


def _gather_sems(n_arrays):
    return [pltpu.SemaphoreType.DMA((3 * n_arrays,)) for _ in range(4)]


class _SiblingSend:
    def __init__(self, g_refs, got_refs, ssem, rsem):
        self.g_refs, self.got_refs, self.ssem, self.rsem = g_refs, got_refs, ssem, rsem
        self.x, self.y, self.c = _pos()

    def _copy(self, a, shard):
        _, theirs = _halves(self.g_refs[a], self.c, 16)
        src = self.g_refs[a].at[:, theirs] if shard is None else self.g_refs[a].at[shard, theirs]
        dst = self.got_refs[a] if shard is None else self.got_refs[a].at[shard]
        return pltpu.make_async_remote_copy(
            src_ref=src, dst_ref=dst, send_sem=self.ssem.at[a], recv_sem=self.rsem.at[a],
            device_id=(self.x, self.y, 1 - self.c), device_id_type=MESH)

    def start(self):
        for a in range(len(self.g_refs)):
            for s in range(N_SHARD):
                self._copy(a, s).start()

    def finish(self):
        for a in range(len(self.g_refs)):
            self._copy(a, None).wait()


class _SiblingSwap:
    def __init__(self, h_refs, t_refs, ssem, rsem):
        self.h_refs, self.t_refs, self.ssem, self.rsem = h_refs, t_refs, ssem, rsem
        self.x, self.y, self.c = _pos()

    def _copy(self, a):
        return pltpu.make_async_remote_copy(
            src_ref=self.h_refs[a], dst_ref=self.t_refs[a], send_sem=self.ssem.at[a], recv_sem=self.rsem.at[a],
            device_id=(self.x, self.y, 1 - self.c), device_id_type=MESH)

    def start(self):
        for a in range(len(self.h_refs)):
            self._copy(a).start()

    def finish(self):
        for a in range(len(self.h_refs)):
            self._copy(a).wait()


def _rs_sibling(arrs, name):
    n = len(arrs)

    def body(*refs):
        send = _SiblingSend(refs[:n], refs[n:2 * n], refs[2 * n], refs[2 * n + 1])
        send.start()
        send.finish()

    hbm = pl.BlockSpec(memory_space=pl.ANY)
    return pl.pallas_call(
        body, name=name,
        out_shape=tuple(jax.ShapeDtypeStruct((N_SHARD, a.shape[1] // 2, a.shape[2]), BF16) for a in arrs),
        in_specs=[hbm] * n, out_specs=(hbm,) * n,
        scratch_shapes=[pltpu.SemaphoreType.DMA((n,)), pltpu.SemaphoreType.DMA((n,))],
    )(*arrs)


class _ChipScatter:
    def __init__(self, parts, gots, ssem, rsem):
        self.parts, self.gots, self.ssem, self.rsem = parts, gots, ssem, rsem
        self.x, self.y, self.c = _pos()

    def _copy(self, a, k):
        px, py = _chips(self.x, self.y)[k]
        return pltpu.make_async_remote_copy(
            src_ref=self.parts[a].at[2 * px + py], dst_ref=self.gots[a].at[k], send_sem=self.ssem.at[3 * a + k],
            recv_sem=self.rsem.at[3 * a + k], device_id=(px, py, self.c), device_id_type=MESH)

    def start(self):
        for a in range(len(self.parts)):
            for k in range(3):
                self._copy(a, k).start()

    def finish(self):
        for a in range(len(self.parts)):
            for k in range(3):
                self._copy(a, k).wait()


def _rs_join(halves, name):
    n = len(halves)

    def body(*refs):
        h_refs, f_refs, stages = refs[:n], refs[n:2 * n], refs[2 * n:3 * n]
        lsem_in, lsem_out, ssem, rsem = refs[3 * n:]
        x, y, c = _pos()
        remote = []
        for a in range(n):
            mine, _ = _halves(f_refs[a], c, 8)
            cp = pltpu.make_async_remote_copy(
                src_ref=h_refs[a], dst_ref=f_refs[a].at[mine], send_sem=ssem.at[a], recv_sem=rsem.at[a],
                device_id=(x, y, 1 - c), device_id_type=MESH)
            cp.start()
            remote.append(cp)
            pltpu.make_async_copy(h_refs[a], stages[a], lsem_in.at[a]).start()
        local = []
        for a in range(n):
            mine, _ = _halves(f_refs[a], c, 8)
            pltpu.make_async_copy(h_refs[a], stages[a], lsem_in.at[a]).wait()
            cp = pltpu.make_async_copy(stages[a], f_refs[a].at[mine], lsem_out.at[a])
            cp.start()
            local.append(cp)
        for cp in remote + local:
            cp.wait()

    hbm = pl.BlockSpec(memory_space=pl.ANY)
    return pl.pallas_call(
        body, name=name,
        out_shape=tuple(jax.ShapeDtypeStruct((2 * h.shape[0], h.shape[1]), F32) for h in halves),
        in_specs=[hbm] * n, out_specs=(hbm,) * n,
        scratch_shapes=[pltpu.VMEM(h.shape, F32) for h in halves]
        + [pltpu.SemaphoreType.DMA((n,)) for _ in range(4)],
        compiler_params=pltpu.CompilerParams(vmem_limit_bytes=STAGE_VMEM_LIMIT),
    )(*halves)


def _add_pairs(arrs, gots, c_idx, name):
    n = len(arrs)

    def body(c_ref, *refs):
        for a in range(n):
            refs[2 * n + a][...] = (refs[a][...].astype(F32) + refs[n + a][...].astype(F32)).astype(BF16)

    def half_spec(g, mine):
        hr, cols = g.shape[1], g.shape[2]
        if mine:
            return pl.BlockSpec((1, hr, cols), lambda s, c_ref: (s, c_ref[0], 0))
        return pl.BlockSpec((1, hr, cols), lambda s, c_ref: (s, 0, 0))

    grid_spec = pltpu.PrefetchScalarGridSpec(
        num_scalar_prefetch=1, grid=(N_SHARD,),
        in_specs=[half_spec(g, True) for g in gots] + [half_spec(g, False) for g in gots],
        out_specs=tuple(half_spec(g, False) for g in gots))
    return pl.pallas_call(
        body, name=name, grid_spec=grid_spec, out_shape=tuple(jax.ShapeDtypeStruct(g.shape, BF16) for g in gots),
        compiler_params=pltpu.CompilerParams(vmem_limit_bytes=STAGE_VMEM_LIMIT),
    )(c_idx, *arrs, *gots)


def _add_chips(parts, gots, j_idx, name):
    n = len(parts)
    n_split = 2

    def body(j_ref, *refs):
        for a in range(n):
            acc = refs[a][0].astype(F32)
            for k in range(3):
                acc = acc + refs[n + a][k].astype(F32)
            refs[2 * n + a][...] = acc

    in_specs, out_specs = [], []
    for g in gots:
        rb, cols = g.shape[1] // n_split, g.shape[2]
        in_specs.append(pl.BlockSpec((1, rb, cols), lambda r, j_ref: (j_ref[0], r, 0)))
        out_specs.append(pl.BlockSpec((rb, cols), lambda r, j_ref: (r, 0)))
    for g in gots:
        rb, cols = g.shape[1] // n_split, g.shape[2]
        in_specs.append(pl.BlockSpec((3, rb, cols), lambda r, j_ref: (0, r, 0)))
    grid_spec = pltpu.PrefetchScalarGridSpec(
        num_scalar_prefetch=1, grid=(n_split,), in_specs=in_specs, out_specs=tuple(out_specs))
    return pl.pallas_call(
        body, name=name, grid_spec=grid_spec,
        out_shape=tuple(jax.ShapeDtypeStruct(g.shape[1:], F32) for g in gots),
        compiler_params=pltpu.CompilerParams(vmem_limit_bytes=STAGE_VMEM_LIMIT),
    )(j_idx, *parts, *gots)


def _small_exchange(sv, w_mod_sh, cctx, m_cctx, v_cctx, bmod, m_bmod, v_bmod, qg, m_qg, v_qg, kvg, m_kvg, v_kvg,
                    gf, m_gf, v_gf, swap_halves, scatter_parts):
    ncol = w_mod_sh.shape[1]
    n_s, n_p = len(swap_halves), len(scatter_parts)
    n_x = n_s + n_p

    def body(sv_ref, w_ref, cctx_ref, mcc_ref, vcc_ref, b_ref, mb_ref, vb_ref, qg_ref, mq_ref, vq_ref,
             kg_ref, mk_ref, vk_ref, gf_ref, mgf_ref, vgf_ref, *rest):
        all_ref, red_ref, occ_ref, ob_ref, oq_ref, ok_ref, ogf_ref = rest[n_x:n_x + 7]
        vec_ref, part_ref, ssem, rsem, ssem2, rsem2, wsend, wrecv, psend, precv = rest[2 * n_x + 7:]
        swap = _SiblingSwap(rest[:n_s], rest[n_x + 7:n_x + 7 + n_s], wsend, wrecv)
        scatter = _ChipScatter(rest[n_s:n_x], rest[n_x + 7 + n_s:2 * n_x + 7], psend, precv)
        swap.start()
        scatter.start()
        x, y, c = _pos()
        me = 4 * x + 2 * y + c
        j = 2 * x + y
        sends = []
        for r in range(1, 8):
            cp = pltpu.make_async_remote_copy(
                src_ref=sv_ref, dst_ref=all_ref.at[me], send_sem=ssem.at[r - 1], recv_sem=rsem.at[r - 1],
                device_id=_peer(r, x, y, c), device_id_type=MESH)
            cp.start()
            sends.append(cp)
        for cp in sends:
            cp.wait()
        all_ref[me] = sv_ref[...]
        red = all_ref[0]
        for d in range(1, 8):
            red = red + all_ref[d]
        red_ref[...] = red
        vec_ref[...] = jnp.zeros(vec_ref.shape, F32)

        @pl.when(j == 0)
        def _():
            vec_ref[0:1, 0:1024] = red[6:7, :]
            vec_ref[0:1, 1024:1536] = red[7:8, 0:512]

        @pl.when(j == 1)
        def _():
            vec_ref[0:1, 0:512] = red[7:8, 512:1024]

        part = lax.dot_general(vec_ref[...], w_ref[...], (((1,), (1,)), ((), ())), precision=HIGHEST,
                               preferred_element_type=F32)
        part_ref[j] = part
        sends2 = []
        for k, r in enumerate((4, 2, 6)):
            cp = pltpu.make_async_remote_copy(
                src_ref=part_ref.at[j], dst_ref=part_ref.at[j], send_sem=ssem2.at[k], recv_sem=rsem2.at[k],
                device_id=_peer(r, x, y, c), device_id_type=MESH)
            cp.start()
            sends2.append(cp)
        for cp in sends2:
            cp.wait()
        tot = part_ref[0]
        for s in range(1, N_SHARD):
            tot = tot + part_ref[s]
        cc = cctx_ref[...]
        sg = _sigmoid(cc)
        g_cc = tot[0:1, :] * (sg * (1.0 + cc * (1.0 - sg)))
        d_, m_, v_ = _adamw(cc, g_cc, mcc_ref[...], vcc_ref[...])
        occ_ref[0:1, :] = g_cc
        occ_ref[1:2, :] = d_
        occ_ref[2:3, :] = m_
        occ_ref[3:4, :] = v_
        occ_ref[4:8, :] = jnp.zeros((4, D_MODEL), F32)
        g_b = red[0:6, :]
        pad = jnp.concatenate([red[6:8, :], jnp.zeros((4, D_MODEL), F32)], axis=0)
        g_b = g_b + pad
        d_, m_, v_ = _adamw(b_ref[...], g_b, mb_ref[...], vb_ref[...])
        ob_ref[0] = g_b
        ob_ref[1] = d_
        ob_ref[2] = m_
        ob_ref[3] = v_
        g_q = red[9:10, 0:Q_RANK]
        d_, m_, v_ = _adamw(qg_ref[...], g_q, mq_ref[...], vq_ref[...])
        oq_ref[0:1, :] = g_q
        oq_ref[1:2, :] = d_
        oq_ref[2:3, :] = m_
        oq_ref[3:4, :] = v_
        oq_ref[4:8, :] = jnp.zeros((4, Q_RANK), F32)
        g_k = red[10:11, 0:KV_RANK]
        d_, m_, v_ = _adamw(kg_ref[...], g_k, mk_ref[...], vk_ref[...])
        ok_ref[0:1, :] = g_k
        ok_ref[1:2, :] = d_
        ok_ref[2:3, :] = m_
        ok_ref[3:4, :] = v_
        ok_ref[4:8, :] = jnp.zeros((4, KV_RANK), F32)
        g_f = red[8:9, :]
        d_, m_, v_ = _adamw(gf_ref[...], g_f, mgf_ref[...], vgf_ref[...])
        ogf_ref[0:1, :] = g_f
        ogf_ref[1:2, :] = d_
        ogf_ref[2:3, :] = m_
        ogf_ref[3:4, :] = v_
        ogf_ref[4:8, :] = jnp.zeros((4, D_MODEL), F32)
        swap.finish()
        scatter.finish()

    vm = pl.BlockSpec(memory_space=pltpu.VMEM)
    hbm = pl.BlockSpec(memory_space=pl.ANY)
    out_shape = (
        jax.ShapeDtypeStruct((8, 16, D_MODEL), F32),
        jax.ShapeDtypeStruct((16, D_MODEL), F32),
        jax.ShapeDtypeStruct((8, D_MODEL), F32),
        jax.ShapeDtypeStruct((4, 6, D_MODEL), F32),
        jax.ShapeDtypeStruct((8, Q_RANK), F32),
        jax.ShapeDtypeStruct((8, KV_RANK), F32),
        jax.ShapeDtypeStruct((8, D_MODEL), F32),
    ) + tuple(jax.ShapeDtypeStruct(h.shape, F32) for h in swap_halves) + tuple(
        jax.ShapeDtypeStruct((3,) + p.shape[1:], BF16) for p in scatter_parts)
    return pl.pallas_call(
        body, name="small_exchange", out_shape=out_shape, in_specs=[vm] * 17 + [hbm] * n_x,
        out_specs=tuple([vm] * 7) + (hbm,) * n_x,
        scratch_shapes=[pltpu.VMEM((8, ncol), F32), pltpu.VMEM((N_SHARD, 8, D_MODEL), F32),
                        pltpu.SemaphoreType.DMA((7,)), pltpu.SemaphoreType.DMA((7,)),
                        pltpu.SemaphoreType.DMA((3,)), pltpu.SemaphoreType.DMA((3,)),
                        pltpu.SemaphoreType.DMA((n_s,)), pltpu.SemaphoreType.DMA((n_s,)),
                        pltpu.SemaphoreType.DMA((3 * n_p,)), pltpu.SemaphoreType.DMA((3 * n_p,))],
        compiler_params=pltpu.CompilerParams(vmem_limit_bytes=VMEM_LIMIT),
    )(sv, w_mod_sh, cctx, m_cctx, v_cctx, bmod, m_bmod, v_bmod, qg, m_qg, v_qg, kvg, m_kvg, v_kvg, gf, m_gf, v_gf,
      *swap_halves, *scatter_parts)


def _inproj_fwd(x2, ctx2, mod_a, w_in, qg, kvg, w_uq, w_ukv, cos_t, sin_a, sin_b):
    t_lat, t_ctx = x2.shape[0], ctx2.shape[0]
    tm = TOK_TILE
    n_lat = t_lat // tm
    n_all = n_lat + t_ctx // tm
    e_rows = t_lat + t_ctx

    def body(x_ref, ctx_ref, mod_ref, win_ref, qg_ref, kvg_ref, wuq_ref, wukv_ref, cos_ref, sa_ref, sb_ref,
             z_ref, q_ref, k_ref, v_ref, kt_ref):
        i = pl.program_id(0)
        xin = jnp.where(i < n_lat, x_ref[...], ctx_ref[...])
        xn = xin * lax.rsqrt(jnp.mean(xin * xin, axis=-1, keepdims=True) + EPS)
        h1 = (xn * (1.0 + mod_ref[0, 1:2, :]) + mod_ref[0, 0:1, :]).astype(BF16)
        z = _dot(h1, win_ref[...])
        z_ref[...] = z
        cos, sa, sb = cos_ref[...], sa_ref[...], sb_ref[...]
        cq = z[:, 0:Q_RANK]
        cqn = (cq * lax.rsqrt(jnp.mean(cq * cq, axis=-1, keepdims=True) + EPS) * qg_ref[...]).astype(BF16)
        q = _dot(cqn, wuq_ref[...])
        ckv = z[:, Q_RANK:Q_RANK + KV_RANK]
        ckvn = (ckv * lax.rsqrt(jnp.mean(ckv * ckv, axis=-1, keepdims=True) + EPS) * kvg_ref[...]).astype(BF16)
        kv = _dot(ckvn, wukv_ref[...])
        kr = _rope(z[:, Q_RANK + KV_RANK:Q_RANK + KV_RANK + HEAD_PAD], cos, sa, sb)
        ones_lane = lax.broadcasted_iota(jnp.int32, (tm, HEAD_PAD), 1) == V_DIM
        for h in range(N_HEADS):
            lo = h * HEAD_PAD
            q_ref[h] = _rope(q[:, lo:lo + HEAD_PAD], cos, sa, sb).astype(BF16)
            kh = kv[:, lo:lo + HEAD_PAD] + kr
            k_ref[h] = kh.astype(BF16)
            kt_ref[h] = kh.T.astype(BF16)
            vh = kv[:, N_HEADS * HEAD_PAD + lo:N_HEADS * HEAD_PAD + lo + HEAD_PAD]
            v_ref[h] = jnp.where(ones_lane, 1.0, vh).astype(BF16)

    row = lambda i: (i, 0)
    head_spec = pl.BlockSpec((N_HEADS, tm, HEAD_PAD), lambda i: (0, i, 0))
    head_shape = jax.ShapeDtypeStruct((N_HEADS, e_rows, HEAD_PAD), BF16)
    return pl.pallas_call(
        body, name="inproj_fwd", grid=(n_all,),
        out_shape=(jax.ShapeDtypeStruct((e_rows, Z_COLS), F32), head_shape, head_shape, head_shape,
                   jax.ShapeDtypeStruct((N_HEADS, HEAD_PAD, e_rows), BF16)),
        in_specs=[
            pl.BlockSpec((tm, D_MODEL), lambda i: (jnp.minimum(i, n_lat - 1), 0)),
            _const_spec((tm, D_MODEL)),
            pl.BlockSpec((1, 8, D_MODEL), lambda i: (i // n_lat, 0, 0)),
            _const_spec(w_in.shape), _const_spec(qg.shape), _const_spec(kvg.shape),
            _const_spec(w_uq.shape), _const_spec(w_ukv.shape),
            pl.BlockSpec((tm, HEAD_PAD), row), pl.BlockSpec((tm, HEAD_PAD), row), pl.BlockSpec((tm, HEAD_PAD), row),
        ],
        out_specs=(pl.BlockSpec((tm, Z_COLS), row), head_spec, head_spec, head_spec,
                   pl.BlockSpec((N_HEADS, HEAD_PAD, tm), lambda i: (0, 0, i))),
        compiler_params=pltpu.CompilerParams(vmem_limit_bytes=VMEM_LIMIT),
    )(x2, ctx2, mod_a, w_in, qg, kvg, w_uq, w_ukv, cos_t, sin_a, sin_b)


def _key_chunks(e_rows, size):
    n_chunks = max(1, e_rows // size)
    return [(ci * size, size if ci < n_chunks - 1 else e_rows - ci * size) for ci in range(n_chunks)]


def _attn_fwd(q, k, v, t_lat, shard_arrays):
    e_rows = k.shape[1]
    tq = min(t_lat, ATTN_FWD_Q_BLOCK)
    bounds = _key_chunks(e_rows, KEY_CHUNK)
    c2 = ATTN_SCALE * LOG2E

    hb = ATTN_HEADS_PER_STEP
    n_hb = N_HEADS // hb

    def body(q_ref, k_ref, v_ref, o_ref, lse_ref):
        qs = [q_ref[b] for b in range(hb)]
        m, acc = [None] * hb, [None] * hb
        for lo, n in bounds:
            for b in range(hb):
                s = _dot_nt(qs[b], k_ref[b, lo:lo + n, :])
                mc = jnp.max(s, axis=-1, keepdims=True)
                m_new = mc if m[b] is None else jnp.maximum(m[b], mc)
                p = jnp.exp2((s - m_new) * c2)
                pv = _dot(p.astype(BF16), v_ref[b, lo:lo + n, :])
                acc[b] = pv if m[b] is None else acc[b] * jnp.exp2((m[b] - m_new) * c2) + pv
                m[b] = m_new
        for b in range(hb):
            l = acc[b][:, V_DIM:V_DIM + 1]
            o_ref[b] = (acc[b] * (1.0 / l)).astype(BF16)
            lse = (m[b] * ATTN_SCALE + jnp.log(l)) * LOG2E
            lse_ref[b] = jnp.broadcast_to(lse, (tq, HEAD_PAD)).T[0:1, :]

    n_w = len(shard_arrays)
    n_q = t_lat // tq

    def body_with_gather(q_ref, k_ref, v_ref, *rest):
        o_ref, lse_ref = rest[n_w], rest[n_w + 1]
        gather = _ShardGather(rest[n_w + 2:2 * n_w + 2], *rest[2 * n_w + 2:])
        step = pl.program_id(0) * n_q + pl.program_id(1)
        pl.when(step == 0)(gather.start)
        pl.when(step == n_hb * n_q // 2)(gather.forward)
        body(q_ref, k_ref, v_ref, o_ref, lse_ref)
        pl.when(step == n_hb * n_q - 1)(gather.finish)

    hbm = pl.BlockSpec(memory_space=pl.ANY)
    return pl.pallas_call(
        body_with_gather, name="attn_fwd", grid=(n_hb, n_q),
        out_shape=(jax.ShapeDtypeStruct((N_HEADS, t_lat, HEAD_PAD), BF16),
                   jax.ShapeDtypeStruct((N_HEADS, 1, t_lat), F32))
        + tuple(jax.ShapeDtypeStruct(a.shape, a.dtype) for a in shard_arrays),
        in_specs=[pl.BlockSpec((hb, tq, HEAD_PAD), lambda h, i: (h, i, 0)),
                  pl.BlockSpec((hb, e_rows, HEAD_PAD), lambda h, i: (h, 0, 0)),
                  pl.BlockSpec((hb, e_rows, HEAD_PAD), lambda h, i: (h, 0, 0))] + [hbm] * n_w,
        out_specs=(pl.BlockSpec((hb, tq, HEAD_PAD), lambda h, i: (h, i, 0)),
                   pl.BlockSpec((hb, 1, tq), lambda h, i: (h, 0, i))) + (hbm,) * n_w,
        input_output_aliases={3 + a: 2 + a for a in range(n_w)},
        scratch_shapes=_gather_sems(n_w),
        compiler_params=pltpu.CompilerParams(vmem_limit_bytes=VMEM_LIMIT),
    )(q, k, v, *shard_arrays)


def _attn_bwd(q, k, v, kt, o, do, lse_row, t_lat, parts):
    e_rows = k.shape[1]
    tq = min(t_lat, ATTN_BWD_Q_BLOCK)
    n_p = len(parts)
    n_q = t_lat // tq
    bounds = _key_chunks(e_rows, KEY_CHUNK_BWD)

    hb = ATTN_BWD_HEADS_PER_STEP
    n_hb = N_HEADS // hb

    def body(q_ref, k_ref, v_ref, kt_ref, o_ref, do_ref, lse_ref, *rest):
        dqt_ref, dk_ref, dv_ref = rest[n_p:n_p + 3]
        scatter = _ChipScatter(rest[:n_p], rest[n_p + 3:2 * n_p + 3], rest[2 * n_p + 3], rest[2 * n_p + 4])
        h, i = pl.program_id(0), pl.program_id(1)
        pl.when(jnp.logical_and(h == 0, i == 0))(scatter.start)

        @pl.when(i == 0)
        def _():
            dk_ref[...] = jnp.zeros(dk_ref.shape, F32)
            dv_ref[...] = jnp.zeros(dv_ref.shape, F32)

        qs, dos, lses, deltas = [], [], [], []
        for b in range(hb):
            qs.append(q_ref[b])
            dos.append(do_ref[b])
            lses.append(lse_ref[b])
            prod = o_ref[b].astype(F32) * dos[b].astype(F32)
            deltas.append(lax.dot_general(jnp.ones((8, HEAD_PAD), F32), prod, (((1,), (1,)), ((), ())),
                                          precision=HIGHEST, preferred_element_type=F32)[0:1, :])
        dqt = [None] * hb
        for lo, n in bounds:
            for b in range(hb):
                pt = jnp.exp2(_dot_nt(k_ref[b, lo:lo + n, :], qs[b]) * (ATTN_SCALE * LOG2E) - lses[b])
                dpt = _dot_nt(v_ref[b, lo:lo + n, :], dos[b])
                dst = (pt * (dpt - deltas[b])).astype(BF16)
                dv_c = _dot(pt.astype(BF16), dos[b])
                dk_c = _dot(dst, qs[b])
                part = _dot(kt_ref[b, :, lo:lo + n], dst)
                dqt[b] = part if dqt[b] is None else dqt[b] + part
                dk_ref[b, lo:lo + n, :] += dk_c * ATTN_SCALE
                dv_ref[b, lo:lo + n, :] += dv_c
        for b in range(hb):
            dqt_ref[b] = dqt[b] * ATTN_SCALE

        pl.when(jnp.logical_and(h == n_hb - 1, i == n_q - 1))(scatter.finish)

    hbm = pl.BlockSpec(memory_space=pl.ANY)
    qspec = pl.BlockSpec((hb, tq, HEAD_PAD), lambda h, i: (h, i, 0))
    kspec = pl.BlockSpec((hb, e_rows, HEAD_PAD), lambda h, i: (h, 0, 0))
    return pl.pallas_call(
        body, name="attn_bwd", grid=(n_hb, n_q),
        out_shape=(jax.ShapeDtypeStruct((N_HEADS, HEAD_PAD, t_lat), F32),
                   jax.ShapeDtypeStruct((N_HEADS, e_rows, HEAD_PAD), F32),
                   jax.ShapeDtypeStruct((N_HEADS, e_rows, HEAD_PAD), F32))
        + tuple(jax.ShapeDtypeStruct((3,) + p.shape[1:], BF16) for p in parts),
        in_specs=[qspec, kspec, kspec, pl.BlockSpec((hb, HEAD_PAD, e_rows), lambda h, i: (h, 0, 0)), qspec, qspec,
                  pl.BlockSpec((hb, 1, tq), lambda h, i: (h, 0, i))] + [hbm] * n_p,
        out_specs=(pl.BlockSpec((hb, HEAD_PAD, tq), lambda h, i: (h, 0, i)), kspec, kspec) + (hbm,) * n_p,
        scratch_shapes=[pltpu.SemaphoreType.DMA((3 * n_p,)), pltpu.SemaphoreType.DMA((3 * n_p,))],
        compiler_params=pltpu.CompilerParams(vmem_limit_bytes=VMEM_LIMIT),
    )(q, k, v, kt, o, do, lse_row, *parts)


def _halo_specs(tm, col_block):
    per = tm // 8
    prev = pl.BlockSpec((8, CONV_W), lambda i: (jnp.maximum(i * per - 1, 0), col_block))
    nxt = pl.BlockSpec((8, CONV_W), lambda i: ((i + 1) * per, col_block))
    return prev, nxt


def _mlp_fwdbwd(o, z, x2, tgt, mod_b, gf, cw, wo_attn, wo_conv, w1, w2):
    t_lat = x2.shape[0]
    tm = TOK_TILE
    n_lat = t_lat // tm
    fc = MLP_FF_CHUNK
    n_ff = D_FF // fc

    def body(o_ref, gb_ref, gc_ref, xi_ref, gcp_ref, xip_ref, gcn_ref, xin_ref, cw_ref, woa_ref, woc_ref,
             x_ref, t_ref, mod_ref, gf_ref, w1_ref, w2_ref,
             r_ref, da_ref, h2_ref, dy2_ref, dx1_ref, conv_ref, acc_ref, dy1_ref, do_ref, dgb_ref, dyv_ref, ra_ref):
        i = pl.program_id(0)

        @pl.when(i == 0)
        def _():
            acc_ref[...] = jnp.zeros(acc_ref.shape, F32)

        g1, sh2, sc2, g2 = mod_ref[0:1, :], mod_ref[1:2, :], mod_ref[2:3, :], mod_ref[3:4, :]
        u = gc_ref[...] * xi_ref[...]
        u_prev = jnp.where(i > 0, gcp_ref[7:8, :] * xip_ref[7:8, :], 0.0)
        u_next = jnp.where(i < n_lat - 1, gcn_ref[0:1, :] * xin_ref[0:1, :], 0.0)
        um1, up1 = _shift_rows(u, u_prev, u_next)
        yv = cw_ref[0:1, :] * um1 + cw_ref[1:2, :] * u + cw_ref[2:3, :] * up1
        gb = gb_ref[...]
        conv = (gb * yv).astype(BF16)
        conv_ref[...] = conv
        y1 = _dot(conv, woc_ref[...])
        for h in range(N_HEADS):
            y1 = y1 + _dot(o_ref[h], woa_ref[h])
        x1 = x_ref[...] + g1 * y1
        rstd2 = lax.rsqrt(jnp.mean(x1 * x1, axis=-1, keepdims=True) + EPS)
        xn1 = x1 * rstd2
        h2 = (xn1 * (1.0 + sc2) + sh2).astype(BF16)
        h2_ref[...] = h2
        y2 = jnp.zeros((tm, D_MODEL), F32)
        for jj in range(n_ff):
            lo = jj * fc
            ra = jnp.maximum(_dot(h2, w1_ref[lo // FF_CHUNK, :, lo % FF_CHUNK:lo % FF_CHUNK + fc]), 0.0)
            ra_ref[jj] = ra
            r = (ra * ra).astype(BF16)
            r_ref[:, lo:lo + fc] = r
            y2 = y2 + _dot(r, w2_ref[lo:lo + fc, :])
        x2v = x1 + g2 * y2
        rstd3 = lax.rsqrt(jnp.mean(x2v * x2v, axis=-1, keepdims=True) + EPS)
        xn3 = x2v * rstd3
        gfv = gf_ref[...]
        diff = xn3 * gfv - t_ref[...]
        loss_t = 0.5 * jnp.sum(jnp.sum(diff * diff, axis=-1, keepdims=True), axis=0, keepdims=True) * (1.0 / D_MODEL)
        dy = diff * (1.0 / D_MODEL)
        dxn3 = dy * gfv
        dx2 = rstd3 * (dxn3 - xn3 * jnp.mean(dxn3 * xn3, axis=-1, keepdims=True))
        dy2 = (dx2 * g2).astype(BF16)
        dy2_ref[...] = dy2
        dh2 = jnp.zeros((tm, D_MODEL), F32)
        for jj in range(n_ff):
            lo = jj * fc
            dr = _dot_nt(dy2, w2_ref[lo:lo + fc, :])
            da = (2.0 * ra_ref[jj] * dr).astype(BF16)
            da_ref[:, lo:lo + fc] = da
            dh2 = dh2 + _dot_nt(da, w1_ref[lo // FF_CHUNK, :, lo % FF_CHUNK:lo % FF_CHUNK + fc])
        dxn1 = dh2 * (1.0 + sc2)
        dx1 = dx2 + rstd2 * (dxn1 - xn1 * jnp.mean(dxn1 * xn1, axis=-1, keepdims=True))
        dx1_ref[...] = dx1
        dy1 = (dx1 * g1).astype(BF16)
        dy1_ref[...] = dy1
        for h in range(N_HEADS):
            do_ref[h] = _dot_nt(dy1, woa_ref[h]).astype(BF16)
        dconv = _dot_nt(dy1, woc_ref[...])
        dgb_ref[...] = dconv * yv
        dyv_ref[...] = dconv * gb
        acc_ref[5:6, :] += jnp.sum(dx1 * y1, axis=0, keepdims=True)
        acc_ref[0:1, :] += jnp.sum(dy * xn3, axis=0, keepdims=True)
        acc_ref[1:2, :] += jnp.sum(dx2 * y2, axis=0, keepdims=True)
        acc_ref[2:3, :] += jnp.sum(dh2, axis=0, keepdims=True)
        acc_ref[3:4, :] += jnp.sum(dh2 * xn1, axis=0, keepdims=True)
        acc_ref[4:5, :] += jnp.broadcast_to(loss_t, (1, D_MODEL))

    row = lambda i: (i, 0)
    gcp, gcn = _halo_specs(tm, 2)
    xip, xin = _halo_specs(tm, 3)
    tile = pl.BlockSpec((tm, D_MODEL), row)
    wide = pl.BlockSpec((tm, D_FF), row)
    half = pl.BlockSpec((tm, CONV_W), row)
    return pl.pallas_call(
        body, name="mlp_fwdbwd", grid=(n_lat,),
        out_shape=(jax.ShapeDtypeStruct((t_lat, D_FF), BF16), jax.ShapeDtypeStruct((t_lat, D_FF), BF16),
                   jax.ShapeDtypeStruct((t_lat, D_MODEL), BF16), jax.ShapeDtypeStruct((t_lat, D_MODEL), BF16),
                   jax.ShapeDtypeStruct((t_lat, D_MODEL), F32), jax.ShapeDtypeStruct((t_lat, CONV_W), BF16),
                   jax.ShapeDtypeStruct((8, D_MODEL), F32),
                   jax.ShapeDtypeStruct((t_lat, D_MODEL), BF16),
                   jax.ShapeDtypeStruct((N_HEADS, t_lat, HEAD_PAD), BF16),
                   jax.ShapeDtypeStruct((t_lat, CONV_W), F32), jax.ShapeDtypeStruct((t_lat, CONV_W), F32)),
        in_specs=[
            pl.BlockSpec((N_HEADS, tm, HEAD_PAD), lambda i: (0, i, 0)),
            pl.BlockSpec((tm, CONV_W), lambda i: (i, 1)), pl.BlockSpec((tm, CONV_W), lambda i: (i, 2)),
            pl.BlockSpec((tm, CONV_W), lambda i: (i, 3)),
            gcp, xip, gcn, xin,
            _const_spec(cw.shape), _resident_spec(wo_attn.shape), _resident_spec(wo_conv.shape),
            tile, tile, _const_spec(mod_b.shape), _const_spec(gf.shape),
            _resident_spec(w1.shape), _resident_spec(w2.shape),
        ],
        out_specs=(wide, wide, tile, tile, tile, half, _const_spec((8, D_MODEL)),
                   tile, pl.BlockSpec((N_HEADS, tm, HEAD_PAD), lambda i: (0, i, 0)), half, half),
        scratch_shapes=[pltpu.VMEM((n_ff, tm, fc), F32)],
        compiler_params=pltpu.CompilerParams(vmem_limit_bytes=VMEM_LIMIT),
    )(o, z, z, z, z, z, z, z, cw, wo_attn, wo_conv, x2, tgt, mod_b, gf, w1, w2)


def _inproj_bwd(x2, ctx2, mod_a, z, dyv, dgb, dx1, dqt, dk, dv, cos_t, sin_a, sin_b, w_in, w_uq, w_ukv, qg, kvg, cw):
    t_lat, t_ctx = x2.shape[0], ctx2.shape[0]
    tm = TOK_TILE
    n_lat = t_lat // tm
    n_all = n_lat + t_ctx // tm
    group = max(g for g in (1, 2, 4) if n_lat % g == 0)

    def body(x_ref, ctx_ref, mod_ref, z_ref, gcp_ref, xip_ref, gcn_ref, xin_ref, dyv_ref, dyvp_ref, dyvn_ref,
             dgb_ref, dx1_ref, dqt_ref, dk_ref, dv_ref, cos_ref, sa_ref, sb_ref, win_ref, wuq_ref, wukv_ref,
             qg_ref, kvg_ref, cw_ref, gx_ref, dwin_out, dwuq_out, dwukv_out, acc_ref, h1_buf, dz_buf,
             dwin_ref, dwuq_ref, dwukv_ref):
        i = pl.program_id(0)
        lat = i < n_lat

        @pl.when(i == 0)
        def _():
            dwin_ref[...] = jnp.zeros(dwin_ref.shape, F32)
            dwuq_ref[...] = jnp.zeros(dwuq_ref.shape, F32)
            dwukv_ref[...] = jnp.zeros(dwukv_ref.shape, F32)
            acc_ref[...] = jnp.zeros(acc_ref.shape, F32)

        xin = jnp.where(lat, x_ref[...], ctx_ref[...])
        rstd = lax.rsqrt(jnp.mean(xin * xin, axis=-1, keepdims=True) + EPS)
        xn = xin * rstd
        sc = mod_ref[0, 1:2, :]
        h1 = (xn * (1.0 + sc) + mod_ref[0, 0:1, :]).astype(BF16)
        z = z_ref[...]
        cos, sa, sb = cos_ref[...], sa_ref[...], sb_ref[...]
        qgv, kvgv = qg_ref[...], kvg_ref[...]
        cq = z[:, 0:Q_RANK]
        cqh = cq * lax.rsqrt(jnp.mean(cq * cq, axis=-1, keepdims=True) + EPS)
        rq = lax.rsqrt(jnp.mean(cq * cq, axis=-1, keepdims=True) + EPS)
        cqn = (cqh * qgv).astype(BF16)
        parts = []
        for h in range(N_HEADS):
            g = jnp.where(lat, dqt_ref[h].T, 0.0)
            parts.append(_unrope(g, cos, sa, sb))
        dq = jnp.concatenate(parts, axis=1).astype(BF16)
        dcqn = _dot_nt(dq, wuq_ref[...])
        dwuq_ref[...] += _dot_tn(cqn, dq)
        acc_ref[4:5, 0:Q_RANK] += jnp.sum(dcqn * cqh, axis=0, keepdims=True)
        dxn = dcqn * qgv
        dcq = rq * (dxn - cqh * jnp.mean(dxn * cqh, axis=-1, keepdims=True))
        ckv = z[:, Q_RANK:Q_RANK + KV_RANK]
        rk = lax.rsqrt(jnp.mean(ckv * ckv, axis=-1, keepdims=True) + EPS)
        ckvh = ckv * rk
        ckvn = (ckvh * kvgv).astype(BF16)
        dks = [dk_ref[h] for h in range(N_HEADS)]
        dkr = dks[0]
        for h in range(1, N_HEADS):
            dkr = dkr + dks[h]
        dkv = jnp.concatenate(dks + [dv_ref[h] for h in range(N_HEADS)], axis=1).astype(BF16)
        dckvn = _dot_nt(dkv, wukv_ref[...])
        dwukv_ref[...] += _dot_tn(ckvn, dkv)
        acc_ref[5:6, 0:KV_RANK] += jnp.sum(dckvn * ckvh, axis=0, keepdims=True)
        dxn = dckvn * kvgv
        dckv = rk * (dxn - ckvh * jnp.mean(dxn * ckvh, axis=-1, keepdims=True))
        dkr = _unrope(dkr, cos, sa, sb)
        gb, gc, xi = z[:, 512:1024], z[:, 1024:1536], z[:, 1536:2048]
        u = gc * xi
        u_prev = jnp.where(i > 0, gcp_ref[7:8, :] * xip_ref[7:8, :], 0.0)
        u_next = jnp.where(i < n_lat - 1, gcn_ref[0:1, :] * xin_ref[0:1, :], 0.0)
        um1, up1 = _shift_rows(u, u_prev, u_next)
        dyv = jnp.where(lat, dyv_ref[...], 0.0)
        dyv_prev = jnp.where(jnp.logical_and(i > 0, lat), dyvp_ref[7:8, :], 0.0)
        dyv_next = jnp.where(i < n_lat - 1, dyvn_ref[0:1, :], 0.0)
        dyv_m1, dyv_p1 = _shift_rows(dyv, dyv_prev, dyv_next)
        du = cw_ref[0:1, :] * dyv_p1 + cw_ref[1:2, :] * dyv + cw_ref[2:3, :] * dyv_m1
        dgc = du * xi
        dxi = du * gc
        dgb = jnp.where(lat, dgb_ref[...], 0.0)
        acc_ref[6:7, 0:CONV_W] += jnp.sum(dyv * um1, axis=0, keepdims=True)
        acc_ref[7:8, 0:CONV_W] += jnp.sum(dyv * u, axis=0, keepdims=True)
        acc_ref[8:9, 0:CONV_W] += jnp.sum(dyv * up1, axis=0, keepdims=True)
        dz = jnp.concatenate([dcq, dckv, dkr, dgb, dgc, dxi], axis=1).astype(BF16)
        dh1 = _dot_nt(dz, win_ref[...])
        slot = i % group
        rows_g = pl.ds(pl.multiple_of(slot * tm, tm), tm)
        h1_buf[rows_g, :] = h1
        dz_buf[rows_g, :] = dz

        @pl.when(jnp.logical_and(lat, slot == group - 1))
        def _():
            dwin_ref[...] += _dot_tn(h1_buf[...], dz_buf[...])

        @pl.when(jnp.logical_not(lat))
        def _():
            dwin_ref[...] += _dot_tn(h1, dz)
        s_sh = jnp.sum(dh1, axis=0, keepdims=True)
        s_sc = jnp.sum(dh1 * xn, axis=0, keepdims=True)
        zero = jnp.zeros_like(s_sh)
        acc_ref[0:1, :] += jnp.where(lat, s_sh, zero)
        acc_ref[1:2, :] += jnp.where(lat, s_sc, zero)
        acc_ref[2:3, :] += jnp.where(lat, zero, s_sh)
        acc_ref[3:4, :] += jnp.where(lat, zero, s_sc)
        dxn = dh1 * (1.0 + sc)
        dx = rstd * (dxn - xn * jnp.mean(dxn * xn, axis=-1, keepdims=True))

        @pl.when(lat)
        def _():
            gx_ref[...] = dx1_ref[...] + dx

        @pl.when(i == n_all - 1)
        def _():
            dwin_out[...] = dwin_ref[...].astype(BF16)
            dwuq_out[...] = dwuq_ref[...].astype(BF16)
            dwukv_out[...] = dwukv_ref[...].astype(BF16)

    last = n_lat - 1
    per = tm // 8
    lat_row = lambda i: (jnp.minimum(i, last), 0)
    row = lambda i: (i, 0)
    gcp, gcn = _halo_specs(tm, 2)
    xip, xin = _halo_specs(tm, 3)
    n_halo = t_lat // 8
    dyvp = pl.BlockSpec((8, CONV_W), lambda i: (jnp.clip(i * per - 1, 0, n_halo - 1), 0))
    dyvn = pl.BlockSpec((8, CONV_W), lambda i: (jnp.minimum((i + 1) * per, n_halo - 1), 0))
    gcn = pl.BlockSpec((8, CONV_W), lambda i: (jnp.minimum((i + 1) * per, (t_lat + t_ctx) // 8 - 1), 2))
    xin = pl.BlockSpec((8, CONV_W), lambda i: (jnp.minimum((i + 1) * per, (t_lat + t_ctx) // 8 - 1), 3))
    head_f32 = pl.BlockSpec((N_HEADS, tm, HEAD_PAD), lambda i: (0, i, 0))
    tab = pl.BlockSpec((tm, HEAD_PAD), row)
    return pl.pallas_call(
        body, name="inproj_bwd", grid=(n_all,),
        out_shape=(jax.ShapeDtypeStruct((t_lat, D_MODEL), F32), jax.ShapeDtypeStruct(w_in.shape, BF16),
                   jax.ShapeDtypeStruct(w_uq.shape, BF16), jax.ShapeDtypeStruct(w_ukv.shape, BF16),
                   jax.ShapeDtypeStruct((16, D_MODEL), F32)),
        in_specs=[
            pl.BlockSpec((tm, D_MODEL), lat_row), _const_spec((tm, D_MODEL)),
            pl.BlockSpec((1, 8, D_MODEL), lambda i: (i // n_lat, 0, 0)),
            pl.BlockSpec((tm, Z_COLS), row), gcp, xip, gcn, xin,
            pl.BlockSpec((tm, CONV_W), lat_row), dyvp, dyvn,
            pl.BlockSpec((tm, CONV_W), lat_row), pl.BlockSpec((tm, D_MODEL), lat_row),
            pl.BlockSpec((N_HEADS, HEAD_PAD, tm), lambda i: (0, 0, jnp.minimum(i, last))),
            head_f32, head_f32, tab, tab, tab,
            _const_spec(w_in.shape), _const_spec(w_uq.shape), _const_spec(w_ukv.shape),
            _const_spec(qg.shape), _const_spec(kvg.shape), _const_spec(cw.shape),
        ],
        out_specs=(pl.BlockSpec((tm, D_MODEL), lat_row), _const_spec(w_in.shape), _const_spec(w_uq.shape),
                   _const_spec(w_ukv.shape), _const_spec((16, D_MODEL))),
        scratch_shapes=[pltpu.VMEM((group * tm, D_MODEL), BF16), pltpu.VMEM((group * tm, Z_COLS), BF16),
                        pltpu.VMEM(w_in.shape, F32), pltpu.VMEM(w_uq.shape, F32), pltpu.VMEM(w_ukv.shape, F32)],
        compiler_params=pltpu.CompilerParams(vmem_limit_bytes=VMEM_LIMIT),
    )(x2, ctx2, mod_a, z, z, z, z, z, dyv, dyv, dyv, dgb, dx1, dqt, dk, dv, cos_t, sin_a, sin_b, w_in, w_uq, w_ukv,
      qg, kvg, cw)


def _wgrad(a, b, name, bm, bn):
    t, m = a.shape
    n = b.shape[1]
    bk = min(t, 4096)
    nk = t // bk
    nj = n // bn

    def body(a_ref, b_ref, o_ref, acc_ref):
        k = pl.program_id(2)
        part = _dot_tn(a_ref[...], b_ref[...])

        @pl.when(k == 0)
        def _():
            acc_ref[...] = part

        @pl.when(k > 0)
        def _():
            acc_ref[...] += part

        @pl.when(k == nk - 1)
        def _():
            o_ref[...] = acc_ref[...].astype(BF16)

    return pl.pallas_call(
        body, name=name, grid=(m // bm, nj, nk), out_shape=jax.ShapeDtypeStruct((m // bm * nj, bm, bn), BF16),
        in_specs=[pl.BlockSpec((bk, bm), lambda i, j, k: (k, i)), pl.BlockSpec((bk, bn), lambda i, j, k: (k, j))],
        out_specs=pl.BlockSpec((None, bm, bn), lambda i, j, k: (i * nj + j, 0, 0)),
        scratch_shapes=[pltpu.VMEM((bm, bn), F32)],
        compiler_params=pltpu.CompilerParams(vmem_limit_bytes=VMEM_LIMIT),
    )(a, b)


def _wgrad_out(o, conv, dy1, sib_arrays):
    t = o.shape[1]
    bk = min(t, 2048)
    nk = t // bk
    rows = N_HEADS * HEAD_PAD + CONV_W
    n_s = len(sib_arrays)

    def body(o_ref, c_ref, d_ref, *rest):
        w_ref, got_w_ref, acc_ref = rest[n_s], rest[2 * n_s + 1], rest[2 * n_s + 2]
        send = _SiblingSend(rest[:n_s], rest[n_s + 1:2 * n_s + 1], rest[2 * n_s + 3], rest[2 * n_s + 4])
        wsend, wrecv = rest[2 * n_s + 5], rest[2 * n_s + 6]
        k = pl.program_id(0)
        pl.when(k == 0)(send.start)
        cat = jnp.concatenate([o_ref[h] for h in range(N_HEADS)] + [c_ref[...]], axis=1)
        part = _dot_tn(cat, d_ref[...])

        @pl.when(k == 0)
        def _():
            acc_ref[...] = part

        @pl.when(k > 0)
        def _():
            acc_ref[...] += part

        @pl.when(k == nk - 1)
        def _():
            for h in range(N_HEADS):
                w_ref[h * V_DIM:(h + 1) * V_DIM, :] = acc_ref[h * HEAD_PAD:h * HEAD_PAD + V_DIM, :].astype(BF16)
            w_ref[N_HEADS * V_DIM:, :] = acc_ref[N_HEADS * HEAD_PAD:, :].astype(BF16)
            x, y, c = _pos()
            own = []
            for s in range(N_SHARD):
                theirs = pl.ds(pl.multiple_of(s * shard_rows + (1 - c) * (shard_rows // 2), 16), shard_rows // 2)
                cp = pltpu.make_async_remote_copy(
                    src_ref=w_ref.at[theirs], dst_ref=got_w_ref.at[s], send_sem=wsend.at[s], recv_sem=wrecv.at[s],
                    device_id=(x, y, 1 - c), device_id_type=MESH)
                cp.start()
                own.append(cp)
            for cp in own:
                cp.wait()

        pl.when(k == nk - 1)(send.finish)

    hbm = pl.BlockSpec(memory_space=pl.ANY)
    shard_rows = D_MODEL // N_SHARD
    return pl.pallas_call(
        body, name="wgrad_out", grid=(nk,),
        out_shape=(jax.ShapeDtypeStruct((D_MODEL, D_MODEL), BF16),)
        + tuple(jax.ShapeDtypeStruct((N_SHARD, a.shape[1] // 2, a.shape[2]), BF16) for a in sib_arrays)
        + (jax.ShapeDtypeStruct((N_SHARD, shard_rows // 2, D_MODEL), BF16),),
        in_specs=[pl.BlockSpec((N_HEADS, bk, HEAD_PAD), lambda k: (0, k, 0)),
                  pl.BlockSpec((bk, CONV_W), lambda k: (k, 0)),
                  pl.BlockSpec((bk, D_MODEL), lambda k: (k, 0))] + [hbm] * n_s,
        out_specs=(_const_spec((D_MODEL, D_MODEL)),) + (hbm,) * (n_s + 1),
        scratch_shapes=[pltpu.VMEM((rows, D_MODEL), F32), pltpu.SemaphoreType.DMA((n_s,)),
                        pltpu.SemaphoreType.DMA((n_s,)), pltpu.SemaphoreType.DMA((N_SHARD,)),
                        pltpu.SemaphoreType.DMA((N_SHARD,))],
        compiler_params=pltpu.CompilerParams(vmem_limit_bytes=VMEM_LIMIT),
    )(o, conv, dy1, *sib_arrays)


def _adamw_call(w, g, m, v, name):
    rows, cols = w.shape
    rb = 256 if rows % 256 == 0 else rows

    def body(w_ref, g_ref, m_ref, v_ref, d_ref, nm_ref, nv_ref):
        d_, m_, v_ = _adamw(w_ref[...], g_ref[...], m_ref[...], v_ref[...])
        d_ref[...] = d_
        nm_ref[...] = m_
        nv_ref[...] = v_

    spec = pl.BlockSpec((rb, cols), lambda i: (i, 0))
    shp = jax.ShapeDtypeStruct((rows, cols), F32)
    return pl.pallas_call(
        body, name=name, grid=(rows // rb,), out_shape=(shp, shp, shp),
        in_specs=[spec] * 4, out_specs=(spec, spec, spec),
    )(w, g, m, v)


def _adamw_halves(w, g_mine, g_theirs, m, v, c_idx, name):
    rows, cols = w.shape
    half = rows // 2
    rb = min(256, half)
    nb = half // rb

    def body(c_ref, w_ref, gm_ref, gt_ref, m_ref, v_ref, g_ref, d_ref, nm_ref, nv_ref):
        mine = pl.program_id(0) // nb == c_ref[0]
        g = jnp.where(mine, gm_ref[...], gt_ref[...])
        d_, m_, v_ = _adamw(w_ref[...], g, m_ref[...], v_ref[...])
        g_ref[...] = g
        d_ref[...] = d_
        nm_ref[...] = m_
        nv_ref[...] = v_

    spec = pl.BlockSpec((rb, cols), lambda i, c_ref: (i, 0))
    hspec = pl.BlockSpec((rb, cols), lambda i, c_ref: (i % nb, 0))
    shp = jax.ShapeDtypeStruct((rows, cols), F32)
    grid_spec = pltpu.PrefetchScalarGridSpec(
        num_scalar_prefetch=1, grid=(rows // rb,), in_specs=[spec, hspec, hspec, spec, spec],
        out_specs=(spec, spec, spec, spec))
    return pl.pallas_call(
        body, name=name, grid_spec=grid_spec, out_shape=(shp, shp, shp, shp),
    )(c_idx, w, g_mine, g_theirs, m, v)


def _wmod_update(s_t, dm, w, m, v):
    rows, cols = w.shape
    cb = 512

    def body(s_ref, dm_ref, w_ref, m_ref, v_ref, g_ref, d_ref, nm_ref, nv_ref):
        g = jnp.dot(s_ref[...], dm_ref[...], precision=HIGHEST, preferred_element_type=F32)
        d_, m_, v_ = _adamw(w_ref[...], g, m_ref[...], v_ref[...])
        g_ref[...] = g
        d_ref[...] = d_
        nm_ref[...] = m_
        nv_ref[...] = v_

    spec = pl.BlockSpec((rows, cb), lambda i: (0, i))
    shp = jax.ShapeDtypeStruct((rows, cols), F32)
    return pl.pallas_call(
        body, name="wmod_update", grid=(cols // cb,), out_shape=(shp, shp, shp, shp),
        in_specs=[_const_spec(s_t.shape), pl.BlockSpec((16, cb), lambda i: (0, i)), spec, spec, spec],
        out_specs=(spec, spec, spec, spec),
        compiler_params=pltpu.CompilerParams(vmem_limit_bytes=VMEM_LIMIT),
    )(s_t, dm, w, m, v)


def _rope_tables(t_lat, t_ctx):
    t = jnp.arange(t_lat)
    pos = jnp.stack([(t // GRID_W).astype(F32), (t % GRID_W).astype(F32)], axis=1)
    half = QK_ROPE // 4
    freqs = ROPE_THETA ** (-jnp.arange(0, 2 * half, 2, dtype=F32) / (2 * half))
    ang = pos[:, :, None] * freqs[None, None, :]
    cos, sin = jnp.cos(ang), jnp.sin(ang)
    zero = jnp.zeros_like(sin)
    cos32 = jnp.concatenate([cos, cos], axis=2).reshape(t_lat, QK_ROPE)
    sa32 = jnp.concatenate([zero, sin], axis=2).reshape(t_lat, QK_ROPE)
    sb32 = jnp.concatenate([-sin, zero], axis=2).reshape(t_lat, QK_ROPE)

    def widen(tab, fill):
        left = jnp.full((t_lat, ROPE_LANE0), fill, F32)
        right = jnp.full((t_lat, HEAD_PAD - ROPE_LANE0 - QK_ROPE), fill, F32)
        lat = jnp.concatenate([left, tab, right], axis=1)
        return jnp.concatenate([lat, jnp.full((t_ctx, HEAD_PAD), fill, F32)], axis=0)

    return widen(cos32, 1.0), widen(sa32, 0.0), widen(sb32, 0.0)


def _cols_from_shards(s):
    return jnp.transpose(s, (1, 0, 2)).reshape(s.shape[1], -1)


def _cols_to_shards(w):
    k, n = w.shape
    return jnp.transpose(w.reshape(k, N_SHARD, n // N_SHARD), (1, 0, 2))


def kernel(x, c, ctx, c_ctx, w_mod, b_mod, w_in, q_norm_g, w_uq, kv_norm_g, w_ukv, conv_w, w_out, w_mlp1, w_mlp2, final_norm_g, loss_target, m_c_ctx, m_w_mod, m_b_mod, m_w_in, m_q_norm_g, m_w_uq, m_kv_norm_g, m_w_ukv, m_conv_w, m_w_out, m_w_mlp1, m_w_mlp2, m_final_norm_g, v_c_ctx, v_w_mod, v_b_mod, v_w_in, v_q_norm_g, v_w_uq, v_kv_norm_g, v_w_ukv, v_conv_w, v_w_out, v_w_mlp1, v_w_mlp2, v_final_norm_g):
    t_lat, t_ctx = x.shape[1], ctx.shape[1]
    assert t_ctx == TOK_TILE and t_lat % TOK_TILE == 0 and t_lat % GRID_W == 0
    mx, my, mc = _pos()
    j = 2 * mx + my
    ncol = w_mod.shape[2]
    x2, ctx2, tgt = x[0], ctx[0], loss_target[0]
    cctx_row = c_ctx.reshape(1, D_MODEL)

    b_sh = lax.dynamic_slice(b_mod, (0, j * ncol), (1, ncol))
    cw_pad = jnp.zeros((8, 128), F32).at[0:3, :].set(conv_w[0])
    c8, m_all, g_in, g_uq, g_ukv, g_out, g_m1, g_m2 = _prologue(
        c, cctx_row, w_mod[0], b_sh, cw_pad, (w_in[0], w_uq[0], w_ukv[0], w_out[0], w_mlp1[0], w_mlp2[0]), 3)
    mvec = m_all[:, 0, :].reshape(6, D_MODEL)
    mctx = m_all[:, 8, :].reshape(6, D_MODEL)
    zeros6 = jnp.zeros((6, D_MODEL), F32)
    mod_a = jnp.stack([jnp.concatenate([mvec[0:2], zeros6], axis=0), jnp.concatenate([mctx[0:2], zeros6], axis=0)])
    mod_b = jnp.concatenate([mvec[2:6], jnp.zeros((4, D_MODEL), F32)], axis=0)
    cw_full = jnp.pad(jnp.transpose(m_all[:, 9:12, 0:128], (1, 0, 2)).reshape(3, CONV_W), ((0, 5), (0, 0)))

    w_in_f = _cols_from_shards(g_in)
    zc = lambda n: jnp.zeros((D_MODEL, n), BF16)
    w_in_p = jnp.concatenate([w_in_f[:, 0:384], zc(64), w_in_f[:, 384:416], zc(32), w_in_f[:, 416:]], axis=1)
    w_uq_f = _cols_from_shards(g_uq).reshape(Q_RANK, N_HEADS, QK_DIM)
    w_uq_p = jnp.pad(w_uq_f, ((0, 0), (0, 0), (0, HEAD_PAD - QK_DIM))).reshape(Q_RANK, N_HEADS * HEAD_PAD)
    w_ukv_f = _cols_from_shards(g_ukv).reshape(KV_RANK, N_HEADS, QK_NOPE + V_DIM)
    padh = lambda a: jnp.pad(a, ((0, 0), (0, 0), (0, HEAD_PAD - a.shape[2]))).reshape(KV_RANK, N_HEADS * HEAD_PAD)
    w_ukv_p = jnp.concatenate([padh(w_ukv_f[:, :, :QK_NOPE]), padh(w_ukv_f[:, :, QK_NOPE:])], axis=1)
    cos_t, sin_a, sin_b = _rope_tables(t_lat, t_ctx)
    gf_row = final_norm_g.reshape(1, D_MODEL)
    c_idx = mc.reshape(1).astype(jnp.int32)
    j_idx = j.reshape(1).astype(jnp.int32)

    z, q, k, v, kt = _inproj_fwd(x2, ctx2, mod_a, w_in_p, q_norm_g, kv_norm_g, w_uq_p, w_ukv_p, cos_t, sin_a, sin_b)
    o, lse, g_out, w1, g_m2 = _attn_fwd(q, k, v, t_lat, (g_out, g_m1, g_m2))
    w_out_f = g_out.reshape(D_MODEL, D_MODEL)
    wo_attn = jnp.pad(w_out_f[0:512].reshape(N_HEADS, V_DIM, D_MODEL), ((0, 0), (0, HEAD_PAD - V_DIM), (0, 0)))
    wo_conv = w_out_f[512:]
    w2 = g_m2.reshape(D_FF, D_MODEL)
    r, da, h2, dy2, dx1, conv, acc_mlp, dy1, do, dgb, dyv = _mlp_fwdbwd(o, z, x2, tgt, mod_b, gf_row, cw_full, wo_attn,
                                                                         wo_conv, w1, w2)
    d_w1 = _wgrad(h2, da, "wgrad_mlp1", D_MODEL, FF_CHUNK)
    d_w2 = _wgrad(r, dy2, "wgrad_mlp2", FF_CHUNK, D_MODEL)
    d_wout, *big_got = _wgrad_out(o, conv, dy1, (d_w1, d_w2))
    d_wout = d_wout.reshape(N_SHARD, D_MODEL // N_SHARD, D_MODEL)
    big_grads = (d_w1, d_w2, d_wout)
    big_parts = _add_pairs(big_grads, big_got, c_idx, "rs_add_pairs_big")
    dqt, dk, dv, *big_recv = _attn_bwd(q, k, v, kt, o, do, lse, t_lat, big_parts)
    big_halves = _add_chips(big_parts, big_recv, j_idx, "rs_add_chips_big")
    gx, d_win, d_wuq, d_wukv, acc_in = _inproj_bwd(x2, ctx2, mod_a, z, dyv, dgb, dx1, dqt, dk, dv, cos_t, sin_a, sin_b,
                                                   w_in_p, w_uq_p, w_ukv_p, q_norm_g, kv_norm_g, cw_full)

    d_win_f = jnp.concatenate([d_win[:, 0:384], d_win[:, 448:480], d_win[:, 512:]], axis=1)
    d_wuq_f = d_wuq.reshape(Q_RANK, N_HEADS, HEAD_PAD)[:, :, 0:QK_DIM].reshape(Q_RANK, N_HEADS * QK_DIM)
    d_wukv3 = d_wukv.reshape(KV_RANK, 2, N_HEADS, HEAD_PAD)
    d_wukv_f = jnp.concatenate([d_wukv3[:, 0, :, 0:QK_NOPE], d_wukv3[:, 1, :, 0:V_DIM]], axis=2).reshape(KV_RANK, -1)
    rest = tuple(_cols_to_shards(a).astype(BF16) for a in (d_win_f, d_wuq_f, d_wukv_f))
    rest_got = _rs_sibling(rest, "rs_sibling_rest")
    rest_parts = _add_pairs(rest, rest_got, c_idx, "rs_add_pairs_rest")

    sv = jnp.concatenate([
        acc_in[0:2], acc_mlp[5:6], acc_mlp[2:4], acc_mlp[1:2],
        acc_in[2:4], acc_mlp[0:1], acc_in[4:5], acc_in[5:6], acc_in[6:9], acc_mlp[4:5],
        jnp.zeros((1, D_MODEL), F32)], axis=0)
    all_sv, red, o_cc, o_b, o_q, o_k, o_gf, *exchanged = _small_exchange(
        sv, w_mod[0], cctx_row, m_c_ctx.reshape(1, D_MODEL), v_c_ctx.reshape(1, D_MODEL),
        b_mod.reshape(6, D_MODEL), m_b_mod.reshape(6, D_MODEL), v_b_mod.reshape(6, D_MODEL),
        q_norm_g, m_q_norm_g, v_q_norm_g, kv_norm_g, m_kv_norm_g, v_kv_norm_g,
        gf_row, m_final_norm_g.reshape(1, D_MODEL), v_final_norm_g.reshape(1, D_MODEL), big_halves, rest_parts)
    big_theirs, rest_recv = exchanged[:len(big_halves)], exchanged[len(big_halves):]
    loss = red[14, 0]

    c9 = jnp.concatenate([c8[0::8], jnp.zeros((7, D_MODEL), F32)], axis=0)
    s_t = jnp.transpose(c9 * jax.nn.sigmoid(c9))
    dm_ex = all_sv[:, 0:6, :].reshape(8, 6 * D_MODEL)
    dm_ctx = jnp.concatenate([red[6:8].reshape(1, 2 * D_MODEL), jnp.zeros((1, 4 * D_MODEL), F32)], axis=1)
    dm16 = jnp.concatenate([dm_ex, dm_ctx, jnp.zeros((7, 6 * D_MODEL), F32)], axis=0)
    dm_sh = lax.dynamic_slice(dm16, (0, j * ncol), (16, ncol))
    g_wmod, d_wmod, nm_wmod, nv_wmod = _wmod_update(s_t, dm_sh, w_mod[0], m_w_mod[0], v_w_mod[0])

    g_cw = lax.dynamic_slice(red[11:14, 0:CONV_W], (0, j * 128), (3, 128))
    d_cw, nm_cw, nv_cw = _adamw_call(conv_w[0], g_cw, m_conv_w[0], v_conv_w[0], "adamw_conv")

    rest_halves = _add_chips(rest_parts, rest_recv, j_idx, "rs_add_chips_rest")
    g_win, g_wuq, g_wukv = _rs_join(rest_halves, "rs_join_rest")
    upd = {}
    for name, w_, g_, m_, v_ in (("in", w_in, g_win, m_w_in, v_w_in), ("uq", w_uq, g_wuq, m_w_uq, v_w_uq),
                                 ("ukv", w_ukv, g_wukv, m_w_ukv, v_w_ukv)):
        upd[name] = _adamw_call(w_[0], g_, m_[0], v_[0], "adamw_" + name)
    g_w1, *upd["mlp1"] = _adamw_halves(w_mlp1[0], big_halves[0], big_theirs[0], m_w_mlp1[0], v_w_mlp1[0], c_idx,
                                       "adamw_mlp1")
    g_w2, *upd["mlp2"] = _adamw_halves(w_mlp2[0], big_halves[1], big_theirs[1], m_w_mlp2[0], v_w_mlp2[0], c_idx,
                                       "adamw_mlp2")
    g_wout, *upd["out"] = _adamw_halves(w_out[0], big_halves[2], big_theirs[2], m_w_out[0], v_w_out[0], c_idx,
                                        "adamw_out")

    def four(o4, shape):
        return [o4[r].reshape(shape) for r in range(4)]

    cc4 = four(o_cc, (D_MODEL,))
    b4 = [o_b[r].reshape(1, 6 * D_MODEL) for r in range(4)]
    q4 = four(o_q, (1, Q_RANK))
    k4 = four(o_k, (1, KV_RANK))
    gf4 = four(o_gf, (D_MODEL,))
    big = {"in": g_win, "uq": g_wuq, "ukv": g_wukv, "out": g_wout, "mlp1": g_w1, "mlp2": g_w2}

    def leaf(idx):
        wm = (g_wmod, d_wmod, nm_wmod, nv_wmod)[idx]
        cwv = (g_cw, d_cw, nm_cw, nv_cw)[idx]
        bigv = {n: (big[n] if idx == 0 else upd[n][idx - 1]) for n in big}
        return [cc4[idx], wm[None], b4[idx], bigv["in"][None], q4[idx], bigv["uq"][None], k4[idx], bigv["ukv"][None],
                cwv[None], bigv["out"][None], bigv["mlp1"][None], bigv["mlp2"][None], gf4[idx]]

    return (loss, gx[None], *leaf(0), *leaf(1), *leaf(2), *leaf(3))
```

```python
import functools
import math

import jax
import jax.numpy as jnp
from jax import lax
from jax.experimental import pallas as pl
from jax.experimental.pallas import tpu as pltpu

F32 = jnp.float32
BF16 = jnp.bfloat16
MESH = pl.DeviceIdType.MESH
HIGHEST = lax.Precision.HIGHEST

D_MODEL = 1024
N_HEADS = 8
QK_NOPE = 64
QK_ROPE = 32
QK_DIM = QK_NOPE + QK_ROPE
V_DIM = 64
Q_RANK = 256
KV_RANK = 128
CONV_W = 512
D_FF = 4096
GRID_W = 64
ROPE_THETA = 10000.0
EPS = 1e-6
ATTN_SCALE = 1.0 / math.sqrt(QK_DIM)
HEAD_PAD = 128
Z_COLS = 2048
ROPE_LANE0 = QK_NOPE
N_SHARD = 4
TOK_TILE = 256
FF_CHUNK = 1024
MLP_FF_CHUNK = 1024
KEY_CHUNK = 512
ATTN_FWD_Q_BLOCK = 1024
ATTN_HEADS_PER_STEP = 4
ATTN_BWD_HEADS_PER_STEP = 2
ATTN_BWD_Q_BLOCK = 512
KEY_CHUNK_BWD = 512

ADAM_LR = 0.001
ADAM_B1 = 0.9
ADAM_B2 = 0.999
ADAM_EPS = 1e-08
ADAM_WD = 0.01
ADAM_STEP = 10

LOG2E = 1.4426950408889634

VMEM_LIMIT = 56 * 1024 * 1024
STAGE_VMEM_LIMIT = 32 * 1024 * 1024


def _pos():
    return lax.axis_index("x"), lax.axis_index("y"), lax.axis_index("c")


def _dot(a, b):
    return jnp.dot(a, b, preferred_element_type=F32)


def _dot_nt(a, b):
    return lax.dot_general(a, b, (((1,), (1,)), ((), ())), preferred_element_type=F32)


def _dot_tn(a, b):
    return lax.dot_general(a, b, (((0,), (0,)), ((), ())), preferred_element_type=F32)


def _rope(v, cos, sa, sb):
    return v * cos + pltpu.roll(v, 8, 1) * sa + pltpu.roll(v, HEAD_PAD - 8, 1) * sb


def _unrope(g, cos, sa, sb):
    return g * cos + pltpu.roll(g * sa, HEAD_PAD - 8, 1) + pltpu.roll(g * sb, 8, 1)


def _sigmoid(v):
    return 1.0 / (1.0 + jnp.exp(-v))


def _adamw(w, g, m, v):
    m = ADAM_B1 * m + (1.0 - ADAM_B1) * g
    v = ADAM_B2 * v + (1.0 - ADAM_B2) * (g * g)
    m_hat = m / (1.0 - ADAM_B1 ** ADAM_STEP)
    v_hat = v / (1.0 - ADAM_B2 ** ADAM_STEP)
    delta = -ADAM_LR * (m_hat / (jnp.sqrt(v_hat) + ADAM_EPS) + ADAM_WD * w)
    return delta, m, v


def _shift_rows(u, prev_row, next_row):
    n = u.shape[0]
    rows = lax.broadcasted_iota(jnp.int32, u.shape, 0)
    um1 = jnp.where(rows == 0, prev_row, pltpu.roll(u, 1, 0))
    up1 = jnp.where(rows == n - 1, next_row, pltpu.roll(u, n - 1, 0))
    return um1, up1


def _const_spec(shape):
    nd = len(shape)
    return pl.BlockSpec(shape, lambda *_: (0,) * nd)


def _resident_spec(shape):
    nd = len(shape)
    return pl.BlockSpec(shape, lambda *_: (0,) * nd, pipeline_mode=pl.Buffered(1))


def _peer(r, x, y, c):
    px = 1 - x if r & 4 else x
    py = 1 - y if r & 2 else y
    pc = 1 - c if r & 1 else c
    return (px, py, pc)


def _prologue(c_row, cctx_row, w_mod_sh, b_sh, cw_sh, srcs, n_gather):
    ncol = w_mod_sh.shape[1]
    n = len(srcs)
    n_split = 4

    def body(c_ref, cctx_ref, w_ref, b_ref, cw_ref, *refs):
        ins, (c8_ref, m_ref), outs = refs[:n], refs[n:n + 2], refs[n + 2:2 * n + 2]
        mine_ref, msh_ref = refs[2 * n + 2:2 * n + 4]
        f32s, bfs = refs[2 * n + 4:3 * n + 4], refs[3 * n + 4:4 * n + 4]
        ssem, rsem, ssem2, rsem2, lsem_in, lsem_out = refs[4 * n + 4:4 * n + 10]
        x, y, c = _pos()
        me = 4 * x + 2 * y + c
        j = 2 * x + y

        def pieces(rows):
            step = rows // n_split
            return [pl.ds(q * step, step) for q in range(n_split)]

        for t in range(n):
            for sl in pieces(ins[t].shape[0]):
                pltpu.make_async_copy(ins[t].at[sl], f32s[t].at[sl], lsem_in.at[t]).start()
        mine_ref[...] = jnp.zeros(mine_ref.shape, F32)
        mine_ref[0:1, :] = c_ref[...]
        my_rows = pl.ds(pl.multiple_of(8 * me, 8), 8)
        sends = []
        for r in range(1, 8):
            cp = pltpu.make_async_remote_copy(
                src_ref=mine_ref, dst_ref=c8_ref.at[my_rows], send_sem=ssem.at[r - 1], recv_sem=rsem.at[r - 1],
                device_id=_peer(r, x, y, c), device_id_type=MESH)
            cp.start()
            sends.append(cp)

        def cast_and_store(t):
            pltpu.make_async_copy(ins[t], f32s[t], lsem_in.at[t]).wait()
            bfs[t][...] = f32s[t][...].astype(BF16)
            for sl in pieces(ins[t].shape[0]):
                pltpu.make_async_copy(bfs[t].at[sl], outs[t].at[j, sl], lsem_out.at[t]).start()

        gather = _ShardGather(outs[:n_gather], *refs[4 * n + 10:])
        for t in range(n_gather):
            cast_and_store(t)
        for t in range(n_gather):
            pltpu.make_async_copy(bfs[t], outs[t].at[j], lsem_out.at[t]).wait()
        gather.start()
        for cp in sends:
            cp.wait()
        c8_ref[my_rows, :] = mine_ref[...]
        c8_ref[64:72, :] = jnp.zeros((8, D_MODEL), F32)
        c8_ref[64:65, :] = cctx_ref[...]
        cv = c8_ref[...]
        s = cv * _sigmoid(cv)
        m = jnp.dot(s, w_ref[...], precision=HIGHEST, preferred_element_type=F32) + b_ref[...]
        msh_ref[0:64, :] = m[0:64, :]
        msh_ref[64:72, :] = jnp.zeros((8, ncol), F32)
        msh_ref[64:65, :] = m[64:65, :]
        msh_ref[65:68, 0:128] = cw_ref[0:3, :]
        m_ref[j, 0:8, :] = msh_ref[my_rows, :]
        m_ref[j, 8:16, :] = msh_ref[64:72, :]
        sends2 = []
        for k, (px, py) in enumerate(_chips(x, y)):
            theirs = pl.ds(pl.multiple_of(8 * (4 * px + 2 * py + c), 8), 8)
            for half, src in enumerate((msh_ref.at[theirs], msh_ref.at[64:72])):
                cp = pltpu.make_async_remote_copy(
                    src_ref=src, dst_ref=m_ref.at[j, 8 * half:8 * half + 8], send_sem=ssem2.at[2 * k + half],
                    recv_sem=rsem2.at[2 * k + half], device_id=(px, py, c), device_id_type=MESH)
                cp.start()
                sends2.append(cp)
        for t in range(n_gather, n):
            cast_and_store(t)
        gather.forward()
        gather.finish()
        for t in range(n_gather, n):
            pltpu.make_async_copy(bfs[t], outs[t].at[j], lsem_out.at[t]).wait()
        for cp in sends2:
            cp.wait()

    vm = pl.BlockSpec(memory_space=pltpu.VMEM)
    hbm = pl.BlockSpec(memory_space=pl.ANY)
    return pl.pallas_call(
        body, name="prologue",
        out_shape=(jax.ShapeDtypeStruct((72, D_MODEL), F32), jax.ShapeDtypeStruct((N_SHARD, 16, ncol), F32))
        + tuple(jax.ShapeDtypeStruct((N_SHARD,) + a.shape, BF16) for a in srcs),
        in_specs=[vm] * 5 + [hbm] * n, out_specs=(vm, vm) + (hbm,) * n,
        scratch_shapes=[pltpu.VMEM((8, D_MODEL), F32), pltpu.VMEM((72, ncol), F32)]
        + [pltpu.VMEM(a.shape, F32) for a in srcs] + [pltpu.VMEM(a.shape, BF16) for a in srcs]
        + [pltpu.SemaphoreType.DMA((7,)), pltpu.SemaphoreType.DMA((7,)),
           pltpu.SemaphoreType.DMA((6,)), pltpu.SemaphoreType.DMA((6,)),
           pltpu.SemaphoreType.DMA((n,)), pltpu.SemaphoreType.DMA((n,))] + _gather_sems(n_gather),
        compiler_params=pltpu.CompilerParams(vmem_limit_bytes=VMEM_LIMIT),
    )(c_row, cctx_row, w_mod_sh, b_sh, cw_sh, *srcs)


def _chips(x, y):
    return [(1 - x, y), (x, 1 - y), (1 - x, 1 - y)]


def _halves(ref, c, align):
    hr = ref.shape[-2] // 2
    return (pl.ds(pl.multiple_of(c * hr, align), hr), pl.ds(pl.multiple_of((1 - c) * hr, align), hr))


class _ShardGather:
    def __init__(self, refs, ssem, rsem, fsend, frecv):
        self.refs, self.sems = refs, (ssem, rsem, fsend, frecv)
        self.x, self.y, self.c = _pos()
        self.j = 2 * self.x + self.y

    def _ici(self, a, k, slot):
        g = self.refs[a]
        ssem, rsem, _, _ = self.sems
        mine, _ = _halves(g, self.c, 16)
        px, py = _chips(self.x, self.y)[k]
        return pltpu.make_async_remote_copy(
            src_ref=g.at[self.j, mine], dst_ref=g.at[slot, mine], send_sem=ssem.at[3 * a + k],
            recv_sem=rsem.at[3 * a + k], device_id=(px, py, self.c), device_id_type=MESH)

    def _d2d(self, a, k, to_other_half):
        g = self.refs[a]
        _, _, fsend, frecv = self.sems
        mine, theirs = _halves(g, self.c, 16)
        px, py = _chips(self.x, self.y)[k]
        jk = 2 * px + py
        return pltpu.make_async_remote_copy(
            src_ref=g.at[jk, mine], dst_ref=g.at[jk, theirs if to_other_half else mine],
            send_sem=fsend.at[3 * a + k], recv_sem=frecv.at[3 * a + k],
            device_id=(self.x, self.y, 1 - self.c), device_id_type=MESH)

    def start(self):
        for a in range(len(self.refs)):
            for k in range(3):
                self._ici(a, k, self.j).start()

    def forward(self):
        for a in range(len(self.refs)):
            for k, (px, py) in enumerate(_chips(self.x, self.y)):
                self._ici(a, k, 2 * px + py).wait_recv()
                self._d2d(a, k, False).start()

    def finish(self):
        for a in range(len(self.refs)):
            for k in range(3):
                self._d2d(a, k, True).wait()
                self._ici(a, k, self.j).wait_send()


def _gather_sems(n_arrays):
    return [pltpu.SemaphoreType.DMA((3 * n_arrays,)) for _ in range(4)]


class _SiblingSend:
    def __init__(self, g_refs, got_refs, ssem, rsem):
        self.g_refs, self.got_refs, self.ssem, self.rsem = g_refs, got_refs, ssem, rsem
        self.x, self.y, self.c = _pos()

    def _copy(self, a, shard):
        _, theirs = _halves(self.g_refs[a], self.c, 16)
        src = self.g_refs[a].at[:, theirs] if shard is None else self.g_refs[a].at[shard, theirs]
        dst = self.got_refs[a] if shard is None else self.got_refs[a].at[shard]
        return pltpu.make_async_remote_copy(
            src_ref=src, dst_ref=dst, send_sem=self.ssem.at[a], recv_sem=self.rsem.at[a],
            device_id=(self.x, self.y, 1 - self.c), device_id_type=MESH)

    def start(self):
        for a in range(len(self.g_refs)):
            for s in range(N_SHARD):
                self._copy(a, s).start()

    def finish(self):
        for a in range(len(self.g_refs)):
            self._copy(a, None).wait()


class _SiblingSwap:
    def __init__(self, h_refs, t_refs, ssem, rsem):
        self.h_refs, self.t_refs, self.ssem, self.rsem = h_refs, t_refs, ssem, rsem
        self.x, self.y, self.c = _pos()

    def _copy(self, a):
        return pltpu.make_async_remote_copy(
            src_ref=self.h_refs[a], dst_ref=self.t_refs[a], send_sem=self.ssem.at[a], recv_sem=self.rsem.at[a],
            device_id=(self.x, self.y, 1 - self.c), device_id_type=MESH)

    def start(self):
        for a in range(len(self.h_refs)):
            self._copy(a).start()

    def finish(self):
        for a in range(len(self.h_refs)):
            self._copy(a).wait()


def _rs_sibling(arrs, name):
    n = len(arrs)

    def body(*refs):
        send = _SiblingSend(refs[:n], refs[n:2 * n], refs[2 * n], refs[2 * n + 1])
        send.start()
        send.finish()

    hbm = pl.BlockSpec(memory_space=pl.ANY)
    return pl.pallas_call(
        body, name=name,
        out_shape=tuple(jax.ShapeDtypeStruct((N_SHARD, a.shape[1] // 2, a.shape[2]), BF16) for a in arrs),
        in_specs=[hbm] * n, out_specs=(hbm,) * n,
        scratch_shapes=[pltpu.SemaphoreType.DMA((n,)), pltpu.SemaphoreType.DMA((n,))],
    )(*arrs)


class _ChipScatter:
    def __init__(self, parts, gots, ssem, rsem):
        self.parts, self.gots, self.ssem, self.rsem = parts, gots, ssem, rsem
        self.x, self.y, self.c = _pos()

    def _copy(self, a, k):
        px, py = _chips(self.x, self.y)[k]
        return pltpu.make_async_remote_copy(
            src_ref=self.parts[a].at[2 * px + py], dst_ref=self.gots[a].at[k], send_sem=self.ssem.at[3 * a + k],
            recv_sem=self.rsem.at[3 * a + k], device_id=(px, py, self.c), device_id_type=MESH)

    def start(self):
        for a in range(len(self.parts)):
            for k in range(3):
                self._copy(a, k).start()

    def finish(self):
        for a in range(len(self.parts)):
            for k in range(3):
                self._copy(a, k).wait()


def _rs_join(halves, name):
    n = len(halves)

    def body(*refs):
        h_refs, f_refs, stages = refs[:n], refs[n:2 * n], refs[2 * n:3 * n]
        lsem_in, lsem_out, ssem, rsem = refs[3 * n:]
        x, y, c = _pos()
        remote = []
        for a in range(n):
            mine, _ = _halves(f_refs[a], c, 8)
            cp = pltpu.make_async_remote_copy(
                src_ref=h_refs[a], dst_ref=f_refs[a].at[mine], send_sem=ssem.at[a], recv_sem=rsem.at[a],
                device_id=(x, y, 1 - c), device_id_type=MESH)
            cp.start()
            remote.append(cp)
            pltpu.make_async_copy(h_refs[a], stages[a], lsem_in.at[a]).start()
        local = []
        for a in range(n):
            mine, _ = _halves(f_refs[a], c, 8)
            pltpu.make_async_copy(h_refs[a], stages[a], lsem_in.at[a]).wait()
            cp = pltpu.make_async_copy(stages[a], f_refs[a].at[mine], lsem_out.at[a])
            cp.start()
            local.append(cp)
        for cp in remote + local:
            cp.wait()

    hbm = pl.BlockSpec(memory_space=pl.ANY)
    return pl.pallas_call(
        body, name=name,
        out_shape=tuple(jax.ShapeDtypeStruct((2 * h.shape[0], h.shape[1]), F32) for h in halves),
        in_specs=[hbm] * n, out_specs=(hbm,) * n,
        scratch_shapes=[pltpu.VMEM(h.shape, F32) for h in halves]
        + [pltpu.SemaphoreType.DMA((n,)) for _ in range(4)],
        compiler_params=pltpu.CompilerParams(vmem_limit_bytes=STAGE_VMEM_LIMIT),
    )(*halves)


def _add_pairs(arrs, gots, c_idx, name):
    n = len(arrs)

    def body(c_ref, *refs):
        for a in range(n):
            refs[2 * n + a][...] = (refs[a][...].astype(F32) + refs[n + a][...].astype(F32)).astype(BF16)

    def half_spec(g, mine):
        hr, cols = g.shape[1], g.shape[2]
        if mine:
            return pl.BlockSpec((1, hr, cols), lambda s, c_ref: (s, c_ref[0], 0))
        return pl.BlockSpec((1, hr, cols), lambda s, c_ref: (s, 0, 0))

    grid_spec = pltpu.PrefetchScalarGridSpec(
        num_scalar_prefetch=1, grid=(N_SHARD,),
        in_specs=[half_spec(g, True) for g in gots] + [half_spec(g, False) for g in gots],
        out_specs=tuple(half_spec(g, False) for g in gots))
    return pl.pallas_call(
        body, name=name, grid_spec=grid_spec, out_shape=tuple(jax.ShapeDtypeStruct(g.shape, BF16) for g in gots),
        compiler_params=pltpu.CompilerParams(vmem_limit_bytes=STAGE_VMEM_LIMIT),
    )(c_idx, *arrs, *gots)


def _add_chips(parts, gots, j_idx, name):
    n = len(parts)
    n_split = 2

    def body(j_ref, *refs):
        for a in range(n):
            acc = refs[a][0].astype(F32)
            for k in range(3):
                acc = acc + refs[n + a][k].astype(F32)
            refs[2 * n + a][...] = acc

    in_specs, out_specs = [], []
    for g in gots:
        rb, cols = g.shape[1] // n_split, g.shape[2]
        in_specs.append(pl.BlockSpec((1, rb, cols), lambda r, j_ref: (j_ref[0], r, 0)))
        out_specs.append(pl.BlockSpec((rb, cols), lambda r, j_ref: (r, 0)))
    for g in gots:
        rb, cols = g.shape[1] // n_split, g.shape[2]
        in_specs.append(pl.BlockSpec((3, rb, cols), lambda r, j_ref: (0, r, 0)))
    grid_spec = pltpu.PrefetchScalarGridSpec(
        num_scalar_prefetch=1, grid=(n_split,), in_specs=in_specs, out_specs=tuple(out_specs))
    return pl.pallas_call(
        body, name=name, grid_spec=grid_spec,
        out_shape=tuple(jax.ShapeDtypeStruct(g.shape[1:], F32) for g in gots),
        compiler_params=pltpu.CompilerParams(vmem_limit_bytes=STAGE_VMEM_LIMIT),
    )(j_idx, *parts, *gots)


def _small_exchange(sv, w_mod_sh, cctx, m_cctx, v_cctx, bmod, m_bmod, v_bmod, qg, m_qg, v_qg, kvg, m_kvg, v_kvg,
                    gf, m_gf, v_gf, swap_halves, scatter_parts):
    ncol = w_mod_sh.shape[1]
    n_s, n_p = len(swap_halves), len(scatter_parts)
    n_x = n_s + n_p

    def body(sv_ref, w_ref, cctx_ref, mcc_ref, vcc_ref, b_ref, mb_ref, vb_ref, qg_ref, mq_ref, vq_ref,
             kg_ref, mk_ref, vk_ref, gf_ref, mgf_ref, vgf_ref, *rest):
        all_ref, red_ref, occ_ref, ob_ref, oq_ref, ok_ref, ogf_ref = rest[n_x:n_x + 7]
        vec_ref, part_ref, ssem, rsem, ssem2, rsem2, wsend, wrecv, psend, precv = rest[2 * n_x + 7:]
        swap = _SiblingSwap(rest[:n_s], rest[n_x + 7:n_x + 7 + n_s], wsend, wrecv)
        scatter = _ChipScatter(rest[n_s:n_x], rest[n_x + 7 + n_s:2 * n_x + 7], psend, precv)
        swap.start()
        scatter.start()
        x, y, c = _pos()
        me = 4 * x + 2 * y + c
        j = 2 * x + y
        sends = []
        for r in range(1, 8):
            cp = pltpu.make_async_remote_copy(
                src_ref=sv_ref, dst_ref=all_ref.at[me], send_sem=ssem.at[r - 1], recv_sem=rsem.at[r - 1],
                device_id=_peer(r, x, y, c), device_id_type=MESH)
            cp.start()
            sends.append(cp)
        for cp in sends:
            cp.wait()
        all_ref[me] = sv_ref[...]
        red = all_ref[0]
        for d in range(1, 8):
            red = red + all_ref[d]
        red_ref[...] = red
        vec_ref[...] = jnp.zeros(vec_ref.shape, F32)

        @pl.when(j == 0)
        def _():
            vec_ref[0:1, 0:1024] = red[6:7, :]
            vec_ref[0:1, 1024:1536] = red[7:8, 0:512]

        @pl.when(j == 1)
        def _():
            vec_ref[0:1, 0:512] = red[7:8, 512:1024]

        part = lax.dot_general(vec_ref[...], w_ref[...], (((1,), (1,)), ((), ())), precision=HIGHEST,
                               preferred_element_type=F32)
        part_ref[j] = part
        sends2 = []
        for k, r in enumerate((4, 2, 6)):
            cp = pltpu.make_async_remote_copy(
                src_ref=part_ref.at[j], dst_ref=part_ref.at[j], send_sem=ssem2.at[k], recv_sem=rsem2.at[k],
                device_id=_peer(r, x, y, c), device_id_type=MESH)
            cp.start()
            sends2.append(cp)
        for cp in sends2:
            cp.wait()
        tot = part_ref[0]
        for s in range(1, N_SHARD):
            tot = tot + part_ref[s]
        cc = cctx_ref[...]
        sg = _sigmoid(cc)
        g_cc = tot[0:1, :] * (sg * (1.0 + cc * (1.0 - sg)))
        d_, m_, v_ = _adamw(cc, g_cc, mcc_ref[...], vcc_ref[...])
        occ_ref[0:1, :] = g_cc
        occ_ref[1:2, :] = d_
        occ_ref[2:3, :] = m_
        occ_ref[3:4, :] = v_
        occ_ref[4:8, :] = jnp.zeros((4, D_MODEL), F32)
        g_b = red[0:6, :]
        pad = jnp.concatenate([red[6:8, :], jnp.zeros((4, D_MODEL), F32)], axis=0)
        g_b = g_b + pad
        d_, m_, v_ = _adamw(b_ref[...], g_b, mb_ref[...], vb_ref[...])
        ob_ref[0] = g_b
        ob_ref[1] = d_
        ob_ref[2] = m_
        ob_ref[3] = v_
        g_q = red[9:10, 0:Q_RANK]
        d_, m_, v_ = _adamw(qg_ref[...], g_q, mq_ref[...], vq_ref[...])
        oq_ref[0:1, :] = g_q
        oq_ref[1:2, :] = d_
        oq_ref[2:3, :] = m_
        oq_ref[3:4, :] = v_
        oq_ref[4:8, :] = jnp.zeros((4, Q_RANK), F32)
        g_k = red[10:11, 0:KV_RANK]
        d_, m_, v_ = _adamw(kg_ref[...], g_k, mk_ref[...], vk_ref[...])
        ok_ref[0:1, :] = g_k
        ok_ref[1:2, :] = d_
        ok_ref[2:3, :] = m_
        ok_ref[3:4, :] = v_
        ok_ref[4:8, :] = jnp.zeros((4, KV_RANK), F32)
        g_f = red[8:9, :]
        d_, m_, v_ = _adamw(gf_ref[...], g_f, mgf_ref[...], vgf_ref[...])
        ogf_ref[0:1, :] = g_f
        ogf_ref[1:2, :] = d_
        ogf_ref[2:3, :] = m_
        ogf_ref[3:4, :] = v_
        ogf_ref[4:8, :] = jnp.zeros((4, D_MODEL), F32)
        swap.finish()
        scatter.finish()

    vm = pl.BlockSpec(memory_space=pltpu.VMEM)
    hbm = pl.BlockSpec(memory_space=pl.ANY)
    out_shape = (
        jax.ShapeDtypeStruct((8, 16, D_MODEL), F32),
        jax.ShapeDtypeStruct((16, D_MODEL), F32),
        jax.ShapeDtypeStruct((8, D_MODEL), F32),
        jax.ShapeDtypeStruct((4, 6, D_MODEL), F32),
        jax.ShapeDtypeStruct((8, Q_RANK), F32),
        jax.ShapeDtypeStruct((8, KV_RANK), F32),
        jax.ShapeDtypeStruct((8, D_MODEL), F32),
    ) + tuple(jax.ShapeDtypeStruct(h.shape, F32) for h in swap_halves) + tuple(
        jax.ShapeDtypeStruct((3,) + p.shape[1:], BF16) for p in scatter_parts)
    return pl.pallas_call(
        body, name="small_exchange", out_shape=out_shape, in_specs=[vm] * 17 + [hbm] * n_x,
        out_specs=tuple([vm] * 7) + (hbm,) * n_x,
        scratch_shapes=[pltpu.VMEM((8, ncol), F32), pltpu.VMEM((N_SHARD, 8, D_MODEL), F32),
                        pltpu.SemaphoreType.DMA((7,)), pltpu.SemaphoreType.DMA((7,)),
                        pltpu.SemaphoreType.DMA((3,)), pltpu.SemaphoreType.DMA((3,)),
                        pltpu.SemaphoreType.DMA((n_s,)), pltpu.SemaphoreType.DMA((n_s,)),
                        pltpu.SemaphoreType.DMA((3 * n_p,)), pltpu.SemaphoreType.DMA((3 * n_p,))],
        compiler_params=pltpu.CompilerParams(vmem_limit_bytes=VMEM_LIMIT),
    )(sv, w_mod_sh, cctx, m_cctx, v_cctx, bmod, m_bmod, v_bmod, qg, m_qg, v_qg, kvg, m_kvg, v_kvg, gf, m_gf, v_gf,
      *swap_halves, *scatter_parts)


def _inproj_fwd(x2, ctx2, mod_a, w_in, qg, kvg, w_uq, w_ukv, cos_t, sin_a, sin_b):
    t_lat, t_ctx = x2.shape[0], ctx2.shape[0]
    tm = TOK_TILE
    n_lat = t_lat // tm
    n_all = n_lat + t_ctx // tm
    e_rows = t_lat + t_ctx

    def body(x_ref, ctx_ref, mod_ref, win_ref, qg_ref, kvg_ref, wuq_ref, wukv_ref, cos_ref, sa_ref, sb_ref,
             z_ref, q_ref, k_ref, v_ref, kt_ref):
        i = pl.program_id(0)
        xin = jnp.where(i < n_lat, x_ref[...], ctx_ref[...])
        xn = xin * lax.rsqrt(jnp.mean(xin * xin, axis=-1, keepdims=True) + EPS)
        h1 = (xn * (1.0 + mod_ref[0, 1:2, :]) + mod_ref[0, 0:1, :]).astype(BF16)
        z = _dot(h1, win_ref[...])
        z_ref[...] = z
        cos, sa, sb = cos_ref[...], sa_ref[...], sb_ref[...]
        cq = z[:, 0:Q_RANK]
        cqn = (cq * lax.rsqrt(jnp.mean(cq * cq, axis=-1, keepdims=True) + EPS) * qg_ref[...]).astype(BF16)
        q = _dot(cqn, wuq_ref[...])
        ckv = z[:, Q_RANK:Q_RANK + KV_RANK]
        ckvn = (ckv * lax.rsqrt(jnp.mean(ckv * ckv, axis=-1, keepdims=True) + EPS) * kvg_ref[...]).astype(BF16)
        kv = _dot(ckvn, wukv_ref[...])
        kr = _rope(z[:, Q_RANK + KV_RANK:Q_RANK + KV_RANK + HEAD_PAD], cos, sa, sb)
        ones_lane = lax.broadcasted_iota(jnp.int32, (tm, HEAD_PAD), 1) == V_DIM
        for h in range(N_HEADS):
            lo = h * HEAD_PAD
            q_ref[h] = _rope(q[:, lo:lo + HEAD_PAD], cos, sa, sb).astype(BF16)
            kh = kv[:, lo:lo + HEAD_PAD] + kr
            k_ref[h] = kh.astype(BF16)
            kt_ref[h] = kh.T.astype(BF16)
            vh = kv[:, N_HEADS * HEAD_PAD + lo:N_HEADS * HEAD_PAD + lo + HEAD_PAD]
            v_ref[h] = jnp.where(ones_lane, 1.0, vh).astype(BF16)

    row = lambda i: (i, 0)
    head_spec = pl.BlockSpec((N_HEADS, tm, HEAD_PAD), lambda i: (0, i, 0))
    head_shape = jax.ShapeDtypeStruct((N_HEADS, e_rows, HEAD_PAD), BF16)
    return pl.pallas_call(
        body, name="inproj_fwd", grid=(n_all,),
        out_shape=(jax.ShapeDtypeStruct((e_rows, Z_COLS), F32), head_shape, head_shape, head_shape,
                   jax.ShapeDtypeStruct((N_HEADS, HEAD_PAD, e_rows), BF16)),
        in_specs=[
            pl.BlockSpec((tm, D_MODEL), lambda i: (jnp.minimum(i, n_lat - 1), 0)),
            _const_spec((tm, D_MODEL)),
            pl.BlockSpec((1, 8, D_MODEL), lambda i: (i // n_lat, 0, 0)),
            _const_spec(w_in.shape), _const_spec(qg.shape), _const_spec(kvg.shape),
            _const_spec(w_uq.shape), _const_spec(w_ukv.shape),
            pl.BlockSpec((tm, HEAD_PAD), row), pl.BlockSpec((tm, HEAD_PAD), row), pl.BlockSpec((tm, HEAD_PAD), row),
        ],
        out_specs=(pl.BlockSpec((tm, Z_COLS), row), head_spec, head_spec, head_spec,
                   pl.BlockSpec((N_HEADS, HEAD_PAD, tm), lambda i: (0, 0, i))),
        compiler_params=pltpu.CompilerParams(vmem_limit_bytes=VMEM_LIMIT),
    )(x2, ctx2, mod_a, w_in, qg, kvg, w_uq, w_ukv, cos_t, sin_a, sin_b)


def _key_chunks(e_rows, size):
    n_chunks = max(1, e_rows // size)
    return [(ci * size, size if ci < n_chunks - 1 else e_rows - ci * size) for ci in range(n_chunks)]


def _attn_fwd(q, k, v, t_lat, shard_arrays):
    e_rows = k.shape[1]
    tq = min(t_lat, ATTN_FWD_Q_BLOCK)
    bounds = _key_chunks(e_rows, KEY_CHUNK)
    c2 = ATTN_SCALE * LOG2E

    hb = ATTN_HEADS_PER_STEP
    n_hb = N_HEADS // hb

    def body(q_ref, k_ref, v_ref, o_ref, lse_ref):
        qs = [q_ref[b] for b in range(hb)]
        m, acc = [None] * hb, [None] * hb
        for lo, n in bounds:
            for b in range(hb):
                s = _dot_nt(qs[b], k_ref[b, lo:lo + n, :])
                mc = jnp.max(s, axis=-1, keepdims=True)
                m_new = mc if m[b] is None else jnp.maximum(m[b], mc)
                p = jnp.exp2((s - m_new) * c2)
                pv = _dot(p.astype(BF16), v_ref[b, lo:lo + n, :])
                acc[b] = pv if m[b] is None else acc[b] * jnp.exp2((m[b] - m_new) * c2) + pv
                m[b] = m_new
        outs = []
        for b in range(hb):
            l = acc[b][:, V_DIM:V_DIM + 1]
            outs.append(acc[b] * (1.0 / l))
            lse = (m[b] * ATTN_SCALE + jnp.log(l)) * LOG2E
            lse_ref[b] = jnp.broadcast_to(lse, (tq, HEAD_PAD)).T[0:1, :]
        low = lax.broadcasted_iota(jnp.int32, (tq, HEAD_PAD), 1) < V_DIM
        pairs = [jnp.where(low, outs[b], pltpu.roll(outs[b + 1], V_DIM, 1)) for b in range(0, hb, 2)]
        o_ref[...] = jnp.concatenate(pairs, axis=1).astype(BF16)

    n_w = len(shard_arrays)
    n_q = t_lat // tq

    def body_with_gather(q_ref, k_ref, v_ref, *rest):
        o_ref, lse_ref = rest[n_w], rest[n_w + 1]
        gather = _ShardGather(rest[n_w + 2:2 * n_w + 2], *rest[2 * n_w + 2:])
        step = pl.program_id(0) * n_q + pl.program_id(1)
        pl.when(step == 0)(gather.start)
        pl.when(step == n_hb * n_q // 2)(gather.forward)
        body(q_ref, k_ref, v_ref, o_ref, lse_ref)
        pl.when(step == n_hb * n_q - 1)(gather.finish)

    hbm = pl.BlockSpec(memory_space=pl.ANY)
    return pl.pallas_call(
        body_with_gather, name="attn_fwd", grid=(n_hb, n_q),
        out_shape=(jax.ShapeDtypeStruct((t_lat, N_HEADS * V_DIM), BF16),
                   jax.ShapeDtypeStruct((N_HEADS, 1, t_lat), F32))
        + tuple(jax.ShapeDtypeStruct(a.shape, a.dtype) for a in shard_arrays),
        in_specs=[pl.BlockSpec((hb, tq, HEAD_PAD), lambda h, i: (h, i, 0)),
                  pl.BlockSpec((hb, e_rows, HEAD_PAD), lambda h, i: (h, 0, 0)),
                  pl.BlockSpec((hb, e_rows, HEAD_PAD), lambda h, i: (h, 0, 0))] + [hbm] * n_w,
        out_specs=(pl.BlockSpec((tq, hb * V_DIM), lambda h, i: (i, h)),
                   pl.BlockSpec((hb, 1, tq), lambda h, i: (h, 0, i))) + (hbm,) * n_w,
        input_output_aliases={3 + a: 2 + a for a in range(n_w)},
        scratch_shapes=_gather_sems(n_w),
        compiler_params=pltpu.CompilerParams(vmem_limit_bytes=VMEM_LIMIT),
    )(q, k, v, *shard_arrays)


def _attn_bwd(q, k, v, kt, o, do, lse_row, t_lat, parts):
    e_rows = k.shape[1]
    tq = min(t_lat, ATTN_BWD_Q_BLOCK)
    n_p = len(parts)
    n_q = t_lat // tq
    bounds = _key_chunks(e_rows, KEY_CHUNK_BWD)

    hb = ATTN_BWD_HEADS_PER_STEP
    assert hb * V_DIM == HEAD_PAD, "a step's heads share one lane tile of the unpadded o / dO"
    n_hb = N_HEADS // hb

    def body(q_ref, k_ref, v_ref, kt_ref, o_ref, do_ref, lse_ref, *rest):
        dqt_ref, dk_ref, dv_ref = rest[n_p:n_p + 3]
        scatter = _ChipScatter(rest[:n_p], rest[n_p + 3:2 * n_p + 3], rest[2 * n_p + 3], rest[2 * n_p + 4])
        h, i = pl.program_id(0), pl.program_id(1)
        pl.when(jnp.logical_and(h == 0, i == 0))(scatter.start)

        @pl.when(i == 0)
        def _():
            dk_ref[...] = jnp.zeros(dk_ref.shape, F32)
            dv_ref[...] = jnp.zeros(dv_ref.shape, F32)

        do_pair = do_ref[...].astype(F32)
        prod = o_ref[...].astype(F32) * do_pair
        lane = lax.broadcasted_iota(jnp.int32, (tq, HEAD_PAD), 1)
        sel = lax.broadcasted_iota(jnp.int32, (8, HEAD_PAD), 1)
        qs, dos, lses, deltas = [], [], [], []
        for b in range(hb):
            qs.append(q_ref[b])
            mine = do_pair if b == 0 else pltpu.roll(do_pair, V_DIM, 1)
            dos.append(jnp.where(lane < V_DIM, mine, 0.0).astype(BF16))
            lses.append(lse_ref[b])
            ones = jnp.where((sel < V_DIM) == (b == 0), 1.0, 0.0)
            deltas.append(lax.dot_general(ones, prod, (((1,), (1,)), ((), ())), precision=HIGHEST,
                                          preferred_element_type=F32)[0:1, :])
        dqt = [None] * hb
        for lo, n in bounds:
            for b in range(hb):
                pt = jnp.exp2(_dot_nt(k_ref[b, lo:lo + n, :], qs[b]) * (ATTN_SCALE * LOG2E) - lses[b])
                dpt = _dot_nt(v_ref[b, lo:lo + n, :], dos[b])
                dst = (pt * (dpt - deltas[b])).astype(BF16)
                dv_c = _dot(pt.astype(BF16), dos[b])
                dk_c = _dot(dst, qs[b])
                part = _dot(kt_ref[b, :, lo:lo + n], dst)
                dqt[b] = part if dqt[b] is None else dqt[b] + part
                dk_ref[b, lo:lo + n, :] += dk_c * ATTN_SCALE
                dv_ref[b, lo:lo + n, :] += dv_c
        for b in range(hb):
            dqt_ref[b] = dqt[b] * ATTN_SCALE

        pl.when(jnp.logical_and(h == n_hb - 1, i == n_q - 1))(scatter.finish)

    hbm = pl.BlockSpec(memory_space=pl.ANY)
    qspec = pl.BlockSpec((hb, tq, HEAD_PAD), lambda h, i: (h, i, 0))
    kspec = pl.BlockSpec((hb, e_rows, HEAD_PAD), lambda h, i: (h, 0, 0))
    pair = pl.BlockSpec((tq, hb * V_DIM), lambda h, i: (i, h))
    return pl.pallas_call(
        body, name="attn_bwd", grid=(n_hb, n_q),
        out_shape=(jax.ShapeDtypeStruct((N_HEADS, HEAD_PAD, t_lat), F32),
                   jax.ShapeDtypeStruct((N_HEADS, e_rows, HEAD_PAD), F32),
                   jax.ShapeDtypeStruct((N_HEADS, e_rows, HEAD_PAD), F32))
        + tuple(jax.ShapeDtypeStruct((3,) + p.shape[1:], BF16) for p in parts),
        in_specs=[qspec, kspec, kspec, pl.BlockSpec((hb, HEAD_PAD, e_rows), lambda h, i: (h, 0, 0)), pair, pair,
                  pl.BlockSpec((hb, 1, tq), lambda h, i: (h, 0, i))] + [hbm] * n_p,
        out_specs=(pl.BlockSpec((hb, HEAD_PAD, tq), lambda h, i: (h, 0, i)), kspec, kspec) + (hbm,) * n_p,
        scratch_shapes=[pltpu.SemaphoreType.DMA((3 * n_p,)), pltpu.SemaphoreType.DMA((3 * n_p,))],
        compiler_params=pltpu.CompilerParams(vmem_limit_bytes=VMEM_LIMIT),
    )(q, k, v, kt, o, do, lse_row, *parts)


def _halo_specs(tm, col_block):
    per = tm // 8
    prev = pl.BlockSpec((8, CONV_W), lambda i: (jnp.maximum(i * per - 1, 0), col_block))
    nxt = pl.BlockSpec((8, CONV_W), lambda i: ((i + 1) * per, col_block))
    return prev, nxt


def _mlp_fwdbwd(o, z, x2, tgt, mod_b, gf, cw, w_out, w1, w2):
    t_lat = x2.shape[0]
    tm = TOK_TILE
    n_lat = t_lat // tm
    fc = MLP_FF_CHUNK
    n_ff = D_FF // fc

    def body(o_ref, gb_ref, gc_ref, xi_ref, gcp_ref, xip_ref, gcn_ref, xin_ref, cw_ref, wout_ref,
             x_ref, t_ref, mod_ref, gf_ref, w1_ref, w2_ref,
             r_ref, da_ref, h2_ref, dy2_ref, dx1_ref, conv_ref, acc_ref, dy1_ref, do_ref, dgb_ref, dyv_ref, ra_ref):
        i = pl.program_id(0)

        @pl.when(i == 0)
        def _():
            acc_ref[...] = jnp.zeros(acc_ref.shape, F32)

        g1, sh2, sc2, g2 = mod_ref[0:1, :], mod_ref[1:2, :], mod_ref[2:3, :], mod_ref[3:4, :]
        u = gc_ref[...] * xi_ref[...]
        u_prev = jnp.where(i > 0, gcp_ref[7:8, :] * xip_ref[7:8, :], 0.0)
        u_next = jnp.where(i < n_lat - 1, gcn_ref[0:1, :] * xin_ref[0:1, :], 0.0)
        um1, up1 = _shift_rows(u, u_prev, u_next)
        yv = cw_ref[0:1, :] * um1 + cw_ref[1:2, :] * u + cw_ref[2:3, :] * up1
        gb = gb_ref[...]
        conv = (gb * yv).astype(BF16)
        conv_ref[...] = conv
        n_attn = N_HEADS * V_DIM
        y1 = _dot(o_ref[...], wout_ref[0:n_attn, :]) + _dot(conv, wout_ref[n_attn:, :])
        x1 = x_ref[...] + g1 * y1
        rstd2 = lax.rsqrt(jnp.mean(x1 * x1, axis=-1, keepdims=True) + EPS)
        xn1 = x1 * rstd2
        h2 = (xn1 * (1.0 + sc2) + sh2).astype(BF16)
        h2_ref[...] = h2
        y2 = jnp.zeros((tm, D_MODEL), F32)
        for jj in range(n_ff):
            lo = jj * fc
            ra = jnp.maximum(_dot(h2, w1_ref[lo // FF_CHUNK, :, lo % FF_CHUNK:lo % FF_CHUNK + fc]), 0.0)
            ra_ref[jj] = ra
            r = (ra * ra).astype(BF16)
            r_ref[:, lo:lo + fc] = r
            y2 = y2 + _dot(r, w2_ref[lo:lo + fc, :])
        x2v = x1 + g2 * y2
        rstd3 = lax.rsqrt(jnp.mean(x2v * x2v, axis=-1, keepdims=True) + EPS)
        xn3 = x2v * rstd3
        gfv = gf_ref[...]
        diff = xn3 * gfv - t_ref[...]
        loss_t = 0.5 * jnp.sum(jnp.sum(diff * diff, axis=-1, keepdims=True), axis=0, keepdims=True) * (1.0 / D_MODEL)
        dy = diff * (1.0 / D_MODEL)
        dxn3 = dy * gfv
        dx2 = rstd3 * (dxn3 - xn3 * jnp.mean(dxn3 * xn3, axis=-1, keepdims=True))
        dy2 = (dx2 * g2).astype(BF16)
        dy2_ref[...] = dy2
        dh2 = jnp.zeros((tm, D_MODEL), F32)
        for jj in range(n_ff):
            lo = jj * fc
            dr = _dot_nt(dy2, w2_ref[lo:lo + fc, :])
            da = (2.0 * ra_ref[jj] * dr).astype(BF16)
            da_ref[:, lo:lo + fc] = da
            dh2 = dh2 + _dot_nt(da, w1_ref[lo // FF_CHUNK, :, lo % FF_CHUNK:lo % FF_CHUNK + fc])
        dxn1 = dh2 * (1.0 + sc2)
        dx1 = dx2 + rstd2 * (dxn1 - xn1 * jnp.mean(dxn1 * xn1, axis=-1, keepdims=True))
        dx1_ref[...] = dx1
        dy1 = (dx1 * g1).astype(BF16)
        dy1_ref[...] = dy1
        do_ref[...] = _dot_nt(dy1, wout_ref[0:n_attn, :]).astype(BF16)
        dconv = _dot_nt(dy1, wout_ref[n_attn:, :])
        dgb_ref[...] = dconv * yv
        dyv_ref[...] = dconv * gb
        acc_ref[5:6, :] += jnp.sum(dx1 * y1, axis=0, keepdims=True)
        acc_ref[0:1, :] += jnp.sum(dy * xn3, axis=0, keepdims=True)
        acc_ref[1:2, :] += jnp.sum(dx2 * y2, axis=0, keepdims=True)
        acc_ref[2:3, :] += jnp.sum(dh2, axis=0, keepdims=True)
        acc_ref[3:4, :] += jnp.sum(dh2 * xn1, axis=0, keepdims=True)
        acc_ref[4:5, :] += jnp.broadcast_to(loss_t, (1, D_MODEL))

    row = lambda i: (i, 0)
    gcp, gcn = _halo_specs(tm, 2)
    xip, xin = _halo_specs(tm, 3)
    tile = pl.BlockSpec((tm, D_MODEL), row)
    wide = pl.BlockSpec((tm, D_FF), row)
    half = pl.BlockSpec((tm, CONV_W), row)
    return pl.pallas_call(
        body, name="mlp_fwdbwd", grid=(n_lat,),
        out_shape=(jax.ShapeDtypeStruct((t_lat, D_FF), BF16), jax.ShapeDtypeStruct((t_lat, D_FF), BF16),
                   jax.ShapeDtypeStruct((t_lat, D_MODEL), BF16), jax.ShapeDtypeStruct((t_lat, D_MODEL), BF16),
                   jax.ShapeDtypeStruct((t_lat, D_MODEL), F32), jax.ShapeDtypeStruct((t_lat, CONV_W), BF16),
                   jax.ShapeDtypeStruct((8, D_MODEL), F32),
                   jax.ShapeDtypeStruct((t_lat, D_MODEL), BF16),
                   jax.ShapeDtypeStruct((t_lat, N_HEADS * V_DIM), BF16),
                   jax.ShapeDtypeStruct((t_lat, CONV_W), F32), jax.ShapeDtypeStruct((t_lat, CONV_W), F32)),
        in_specs=[
            pl.BlockSpec((tm, N_HEADS * V_DIM), row),
            pl.BlockSpec((tm, CONV_W), lambda i: (i, 1)), pl.BlockSpec((tm, CONV_W), lambda i: (i, 2)),
            pl.BlockSpec((tm, CONV_W), lambda i: (i, 3)),
            gcp, xip, gcn, xin,
            _const_spec(cw.shape), _resident_spec(w_out.shape),
            tile, tile, _const_spec(mod_b.shape), _const_spec(gf.shape),
            _resident_spec(w1.shape), _resident_spec(w2.shape),
        ],
        out_specs=(wide, wide, tile, tile, tile, half, _const_spec((8, D_MODEL)),
                   tile, pl.BlockSpec((tm, N_HEADS * V_DIM), row), half, half),
        scratch_shapes=[pltpu.VMEM((n_ff, tm, fc), F32)],
        compiler_params=pltpu.CompilerParams(vmem_limit_bytes=VMEM_LIMIT),
    )(o, z, z, z, z, z, z, z, cw, w_out, x2, tgt, mod_b, gf, w1, w2)


def _inproj_bwd(x2, ctx2, mod_a, z, dyv, dgb, dx1, dqt, dk, dv, cos_t, sin_a, sin_b, w_in, w_uq, w_ukv, qg, kvg, cw):
    t_lat, t_ctx = x2.shape[0], ctx2.shape[0]
    tm = TOK_TILE
    n_lat = t_lat // tm
    n_all = n_lat + t_ctx // tm
    group = max(g for g in (1, 2, 4) if n_lat % g == 0)

    def body(x_ref, ctx_ref, mod_ref, z_ref, gcp_ref, xip_ref, gcn_ref, xin_ref, dyv_ref, dyvp_ref, dyvn_ref,
             dgb_ref, dx1_ref, dqt_ref, dk_ref, dv_ref, cos_ref, sa_ref, sb_ref, win_ref, wuq_ref, wukv_ref,
             qg_ref, kvg_ref, cw_ref, gx_ref, dwin_out, dwuq_out, dwukv_out, acc_ref, h1_buf, dz_buf,
             dwin_ref, dwuq_ref, dwukv_ref):
        i = pl.program_id(0)
        lat = i < n_lat

        @pl.when(i == 0)
        def _():
            dwin_ref[...] = jnp.zeros(dwin_ref.shape, F32)
            dwuq_ref[...] = jnp.zeros(dwuq_ref.shape, F32)
            dwukv_ref[...] = jnp.zeros(dwukv_ref.shape, F32)
            acc_ref[...] = jnp.zeros(acc_ref.shape, F32)

        xin = jnp.where(lat, x_ref[...], ctx_ref[...])
        rstd = lax.rsqrt(jnp.mean(xin * xin, axis=-1, keepdims=True) + EPS)
        xn = xin * rstd
        sc = mod_ref[0, 1:2, :]
        h1 = (xn * (1.0 + sc) + mod_ref[0, 0:1, :]).astype(BF16)
        z = z_ref[...]
        cos, sa, sb = cos_ref[...], sa_ref[...], sb_ref[...]
        qgv, kvgv = qg_ref[...], kvg_ref[...]
        cq = z[:, 0:Q_RANK]
        cqh = cq * lax.rsqrt(jnp.mean(cq * cq, axis=-1, keepdims=True) + EPS)
        rq = lax.rsqrt(jnp.mean(cq * cq, axis=-1, keepdims=True) + EPS)
        cqn = (cqh * qgv).astype(BF16)
        parts = []
        for h in range(N_HEADS):
            g = jnp.where(lat, dqt_ref[h].T, 0.0)
            parts.append(_unrope(g, cos, sa, sb))
        dq = jnp.concatenate(parts, axis=1).astype(BF16)
        dcqn = _dot_nt(dq, wuq_ref[...])
        dwuq_ref[...] += _dot_tn(cqn, dq)
        acc_ref[4:5, 0:Q_RANK] += jnp.sum(dcqn * cqh, axis=0, keepdims=True)
        dxn = dcqn * qgv
        dcq = rq * (dxn - cqh * jnp.mean(dxn * cqh, axis=-1, keepdims=True))
        ckv = z[:, Q_RANK:Q_RANK + KV_RANK]
        rk = lax.rsqrt(jnp.mean(ckv * ckv, axis=-1, keepdims=True) + EPS)
        ckvh = ckv * rk
        ckvn = (ckvh * kvgv).astype(BF16)
        dks = [dk_ref[h] for h in range(N_HEADS)]
        dkr = dks[0]
        for h in range(1, N_HEADS):
            dkr = dkr + dks[h]
        dkv = jnp.concatenate(dks + [dv_ref[h] for h in range(N_HEADS)], axis=1).astype(BF16)
        dckvn = _dot_nt(dkv, wukv_ref[...])
        dwukv_ref[...] += _dot_tn(ckvn, dkv)
        acc_ref[5:6, 0:KV_RANK] += jnp.sum(dckvn * ckvh, axis=0, keepdims=True)
        dxn = dckvn * kvgv
        dckv = rk * (dxn - ckvh * jnp.mean(dxn * ckvh, axis=-1, keepdims=True))
        dkr = _unrope(dkr, cos, sa, sb)
        gb, gc, xi = z[:, 512:1024], z[:, 1024:1536], z[:, 1536:2048]
        u = gc * xi
        u_prev = jnp.where(i > 0, gcp_ref[7:8, :] * xip_ref[7:8, :], 0.0)
        u_next = jnp.where(i < n_lat - 1, gcn_ref[0:1, :] * xin_ref[0:1, :], 0.0)
        um1, up1 = _shift_rows(u, u_prev, u_next)
        dyv = jnp.where(lat, dyv_ref[...], 0.0)
        dyv_prev = jnp.where(jnp.logical_and(i > 0, lat), dyvp_ref[7:8, :], 0.0)
        dyv_next = jnp.where(i < n_lat - 1, dyvn_ref[0:1, :], 0.0)
        dyv_m1, dyv_p1 = _shift_rows(dyv, dyv_prev, dyv_next)
        du = cw_ref[0:1, :] * dyv_p1 + cw_ref[1:2, :] * dyv + cw_ref[2:3, :] * dyv_m1
        dgc = du * xi
        dxi = du * gc
        dgb = jnp.where(lat, dgb_ref[...], 0.0)
        acc_ref[6:7, 0:CONV_W] += jnp.sum(dyv * um1, axis=0, keepdims=True)
        acc_ref[7:8, 0:CONV_W] += jnp.sum(dyv * u, axis=0, keepdims=True)
        acc_ref[8:9, 0:CONV_W] += jnp.sum(dyv * up1, axis=0, keepdims=True)
        dz = jnp.concatenate([dcq, dckv, dkr, dgb, dgc, dxi], axis=1).astype(BF16)
        dh1 = _dot_nt(dz, win_ref[...])
        slot = i % group
        rows_g = pl.ds(pl.multiple_of(slot * tm, tm), tm)
        h1_buf[rows_g, :] = h1
        dz_buf[rows_g, :] = dz

        @pl.when(jnp.logical_and(lat, slot == group - 1))
        def _():
            dwin_ref[...] += _dot_tn(h1_buf[...], dz_buf[...])

        @pl.when(jnp.logical_not(lat))
        def _():
            dwin_ref[...] += _dot_tn(h1, dz)
        s_sh = jnp.sum(dh1, axis=0, keepdims=True)
        s_sc = jnp.sum(dh1 * xn, axis=0, keepdims=True)
        zero = jnp.zeros_like(s_sh)
        acc_ref[0:1, :] += jnp.where(lat, s_sh, zero)
        acc_ref[1:2, :] += jnp.where(lat, s_sc, zero)
        acc_ref[2:3, :] += jnp.where(lat, zero, s_sh)
        acc_ref[3:4, :] += jnp.where(lat, zero, s_sc)
        dxn = dh1 * (1.0 + sc)
        dx = rstd * (dxn - xn * jnp.mean(dxn * xn, axis=-1, keepdims=True))

        @pl.when(lat)
        def _():
            gx_ref[...] = dx1_ref[...] + dx

        @pl.when(i == n_all - 1)
        def _():
            dwin_out[...] = dwin_ref[...].astype(BF16)
            dwuq_out[...] = dwuq_ref[...].astype(BF16)
            dwukv_out[...] = dwukv_ref[...].astype(BF16)

    last = n_lat - 1
    per = tm // 8
    lat_row = lambda i: (jnp.minimum(i, last), 0)
    row = lambda i: (i, 0)
    gcp, gcn = _halo_specs(tm, 2)
    xip, xin = _halo_specs(tm, 3)
    n_halo = t_lat // 8
    dyvp = pl.BlockSpec((8, CONV_W), lambda i: (jnp.clip(i * per - 1, 0, n_halo - 1), 0))
    dyvn = pl.BlockSpec((8, CONV_W), lambda i: (jnp.minimum((i + 1) * per, n_halo - 1), 0))
    gcn = pl.BlockSpec((8, CONV_W), lambda i: (jnp.minimum((i + 1) * per, (t_lat + t_ctx) // 8 - 1), 2))
    xin = pl.BlockSpec((8, CONV_W), lambda i: (jnp.minimum((i + 1) * per, (t_lat + t_ctx) // 8 - 1), 3))
    head_f32 = pl.BlockSpec((N_HEADS, tm, HEAD_PAD), lambda i: (0, i, 0))
    tab = pl.BlockSpec((tm, HEAD_PAD), row)
    return pl.pallas_call(
        body, name="inproj_bwd", grid=(n_all,),
        out_shape=(jax.ShapeDtypeStruct((t_lat, D_MODEL), F32), jax.ShapeDtypeStruct(w_in.shape, BF16),
                   jax.ShapeDtypeStruct(w_uq.shape, BF16), jax.ShapeDtypeStruct(w_ukv.shape, BF16),
                   jax.ShapeDtypeStruct((16, D_MODEL), F32)),
        in_specs=[
            pl.BlockSpec((tm, D_MODEL), lat_row), _const_spec((tm, D_MODEL)),
            pl.BlockSpec((1, 8, D_MODEL), lambda i: (i // n_lat, 0, 0)),
            pl.BlockSpec((tm, Z_COLS), row), gcp, xip, gcn, xin,
            pl.BlockSpec((tm, CONV_W), lat_row), dyvp, dyvn,
            pl.BlockSpec((tm, CONV_W), lat_row), pl.BlockSpec((tm, D_MODEL), lat_row),
            pl.BlockSpec((N_HEADS, HEAD_PAD, tm), lambda i: (0, 0, jnp.minimum(i, last))),
            head_f32, head_f32, tab, tab, tab,
            _const_spec(w_in.shape), _const_spec(w_uq.shape), _const_spec(w_ukv.shape),
            _const_spec(qg.shape), _const_spec(kvg.shape), _const_spec(cw.shape),
        ],
        out_specs=(pl.BlockSpec((tm, D_MODEL), lat_row), _const_spec(w_in.shape), _const_spec(w_uq.shape),
                   _const_spec(w_ukv.shape), _const_spec((16, D_MODEL))),
        scratch_shapes=[pltpu.VMEM((group * tm, D_MODEL), BF16), pltpu.VMEM((group * tm, Z_COLS), BF16),
                        pltpu.VMEM(w_in.shape, F32), pltpu.VMEM(w_uq.shape, F32), pltpu.VMEM(w_ukv.shape, F32)],
        compiler_params=pltpu.CompilerParams(vmem_limit_bytes=VMEM_LIMIT),
    )(x2, ctx2, mod_a, z, z, z, z, z, dyv, dyv, dyv, dgb, dx1, dqt, dk, dv, cos_t, sin_a, sin_b, w_in, w_uq, w_ukv,
      qg, kvg, cw)


def _wgrad(a, b, name, bm, bn):
    t, m = a.shape
    n = b.shape[1]
    bk = min(t, 4096)
    nk = t // bk
    nj = n // bn

    def body(a_ref, b_ref, o_ref, acc_ref):
        k = pl.program_id(2)
        part = _dot_tn(a_ref[...], b_ref[...])

        @pl.when(k == 0)
        def _():
            acc_ref[...] = part

        @pl.when(k > 0)
        def _():
            acc_ref[...] += part

        @pl.when(k == nk - 1)
        def _():
            o_ref[...] = acc_ref[...].astype(BF16)

    return pl.pallas_call(
        body, name=name, grid=(m // bm, nj, nk), out_shape=jax.ShapeDtypeStruct((m // bm * nj, bm, bn), BF16),
        in_specs=[pl.BlockSpec((bk, bm), lambda i, j, k: (k, i)), pl.BlockSpec((bk, bn), lambda i, j, k: (k, j))],
        out_specs=pl.BlockSpec((None, bm, bn), lambda i, j, k: (i * nj + j, 0, 0)),
        scratch_shapes=[pltpu.VMEM((bm, bn), F32)],
        compiler_params=pltpu.CompilerParams(vmem_limit_bytes=VMEM_LIMIT),
    )(a, b)


def _wgrad_out(o, conv, dy1, sib_arrays):
    t = o.shape[0]
    bk = min(t, 2048)
    nk = t // bk
    n_s = len(sib_arrays)

    def body(o_ref, c_ref, d_ref, *rest):
        w_ref, got_w_ref, acc_ref = rest[n_s], rest[2 * n_s + 1], rest[2 * n_s + 2]
        send = _SiblingSend(rest[:n_s], rest[n_s + 1:2 * n_s + 1], rest[2 * n_s + 3], rest[2 * n_s + 4])
        wsend, wrecv = rest[2 * n_s + 5], rest[2 * n_s + 6]
        k = pl.program_id(0)
        pl.when(k == 0)(send.start)
        part = _dot_tn(jnp.concatenate([o_ref[...], c_ref[...]], axis=1), d_ref[...])

        @pl.when(k == 0)
        def _():
            acc_ref[...] = part

        @pl.when(k > 0)
        def _():
            acc_ref[...] += part

        @pl.when(k == nk - 1)
        def _():
            w_ref[...] = acc_ref[...].astype(BF16)
            x, y, c = _pos()
            own = []
            for s in range(N_SHARD):
                theirs = pl.ds(pl.multiple_of(s * shard_rows + (1 - c) * (shard_rows // 2), 16), shard_rows // 2)
                cp = pltpu.make_async_remote_copy(
                    src_ref=w_ref.at[theirs], dst_ref=got_w_ref.at[s], send_sem=wsend.at[s], recv_sem=wrecv.at[s],
                    device_id=(x, y, 1 - c), device_id_type=MESH)
                cp.start()
                own.append(cp)
            for cp in own:
                cp.wait()

        pl.when(k == nk - 1)(send.finish)

    hbm = pl.BlockSpec(memory_space=pl.ANY)
    shard_rows = D_MODEL // N_SHARD
    return pl.pallas_call(
        body, name="wgrad_out", grid=(nk,),
        out_shape=(jax.ShapeDtypeStruct((D_MODEL, D_MODEL), BF16),)
        + tuple(jax.ShapeDtypeStruct((N_SHARD, a.shape[1] // 2, a.shape[2]), BF16) for a in sib_arrays)
        + (jax.ShapeDtypeStruct((N_SHARD, shard_rows // 2, D_MODEL), BF16),),
        in_specs=[pl.BlockSpec((bk, N_HEADS * V_DIM), lambda k: (k, 0)),
                  pl.BlockSpec((bk, CONV_W), lambda k: (k, 0)),
                  pl.BlockSpec((bk, D_MODEL), lambda k: (k, 0))] + [hbm] * n_s,
        out_specs=(_const_spec((D_MODEL, D_MODEL)),) + (hbm,) * (n_s + 1),
        scratch_shapes=[pltpu.VMEM((D_MODEL, D_MODEL), F32), pltpu.SemaphoreType.DMA((n_s,)),
                        pltpu.SemaphoreType.DMA((n_s,)), pltpu.SemaphoreType.DMA((N_SHARD,)),
                        pltpu.SemaphoreType.DMA((N_SHARD,))],
        compiler_params=pltpu.CompilerParams(vmem_limit_bytes=VMEM_LIMIT),
    )(o, conv, dy1, *sib_arrays)


def _adamw_call(w, g, m, v, name):
    rows, cols = w.shape
    rb = 256 if rows % 256 == 0 else rows

    def body(w_ref, g_ref, m_ref, v_ref, d_ref, nm_ref, nv_ref):
        d_, m_, v_ = _adamw(w_ref[...], g_ref[...], m_ref[...], v_ref[...])
        d_ref[...] = d_
        nm_ref[...] = m_
        nv_ref[...] = v_

    spec = pl.BlockSpec((rb, cols), lambda i: (i, 0))
    shp = jax.ShapeDtypeStruct((rows, cols), F32)
    return pl.pallas_call(
        body, name=name, grid=(rows // rb,), out_shape=(shp, shp, shp),
        in_specs=[spec] * 4, out_specs=(spec, spec, spec),
    )(w, g, m, v)


def _adamw_halves(w, g_mine, g_theirs, m, v, c_idx, name):
    rows, cols = w.shape
    half = rows // 2
    rb = min(256, half)
    nb = half // rb

    def body(c_ref, w_ref, gm_ref, gt_ref, m_ref, v_ref, g_ref, d_ref, nm_ref, nv_ref):
        mine = pl.program_id(0) // nb == c_ref[0]
        g = jnp.where(mine, gm_ref[...], gt_ref[...])
        d_, m_, v_ = _adamw(w_ref[...], g, m_ref[...], v_ref[...])
        g_ref[...] = g
        d_ref[...] = d_
        nm_ref[...] = m_
        nv_ref[...] = v_

    spec = pl.BlockSpec((rb, cols), lambda i, c_ref: (i, 0))
    hspec = pl.BlockSpec((rb, cols), lambda i, c_ref: (i % nb, 0))
    shp = jax.ShapeDtypeStruct((rows, cols), F32)
    grid_spec = pltpu.PrefetchScalarGridSpec(
        num_scalar_prefetch=1, grid=(rows // rb,), in_specs=[spec, hspec, hspec, spec, spec],
        out_specs=(spec, spec, spec, spec))
    return pl.pallas_call(
        body, name=name, grid_spec=grid_spec, out_shape=(shp, shp, shp, shp),
    )(c_idx, w, g_mine, g_theirs, m, v)


def _wmod_update(s_t, dm, w, m, v):
    rows, cols = w.shape
    cb = 512

    def body(s_ref, dm_ref, w_ref, m_ref, v_ref, g_ref, d_ref, nm_ref, nv_ref):
        g = jnp.dot(s_ref[...], dm_ref[...], precision=HIGHEST, preferred_element_type=F32)
        d_, m_, v_ = _adamw(w_ref[...], g, m_ref[...], v_ref[...])
        g_ref[...] = g
        d_ref[...] = d_
        nm_ref[...] = m_
        nv_ref[...] = v_

    spec = pl.BlockSpec((rows, cb), lambda i: (0, i))
    shp = jax.ShapeDtypeStruct((rows, cols), F32)
    return pl.pallas_call(
        body, name="wmod_update", grid=(cols // cb,), out_shape=(shp, shp, shp, shp),
        in_specs=[_const_spec(s_t.shape), pl.BlockSpec((16, cb), lambda i: (0, i)), spec, spec, spec],
        out_specs=(spec, spec, spec, spec),
        compiler_params=pltpu.CompilerParams(vmem_limit_bytes=VMEM_LIMIT),
    )(s_t, dm, w, m, v)


def _rope_tables(t_lat, t_ctx):
    t = jnp.arange(t_lat)
    pos = jnp.stack([(t // GRID_W).astype(F32), (t % GRID_W).astype(F32)], axis=1)
    half = QK_ROPE // 4
    freqs = ROPE_THETA ** (-jnp.arange(0, 2 * half, 2, dtype=F32) / (2 * half))
    ang = pos[:, :, None] * freqs[None, None, :]
    cos, sin = jnp.cos(ang), jnp.sin(ang)
    zero = jnp.zeros_like(sin)
    cos32 = jnp.concatenate([cos, cos], axis=2).reshape(t_lat, QK_ROPE)
    sa32 = jnp.concatenate([zero, sin], axis=2).reshape(t_lat, QK_ROPE)
    sb32 = jnp.concatenate([-sin, zero], axis=2).reshape(t_lat, QK_ROPE)

    def widen(tab, fill):
        left = jnp.full((t_lat, ROPE_LANE0), fill, F32)
        right = jnp.full((t_lat, HEAD_PAD - ROPE_LANE0 - QK_ROPE), fill, F32)
        lat = jnp.concatenate([left, tab, right], axis=1)
        return jnp.concatenate([lat, jnp.full((t_ctx, HEAD_PAD), fill, F32)], axis=0)

    return widen(cos32, 1.0), widen(sa32, 0.0), widen(sb32, 0.0)


def _cols_from_shards(s):
    return jnp.transpose(s, (1, 0, 2)).reshape(s.shape[1], -1)


def _cols_to_shards(w):
    k, n = w.shape
    return jnp.transpose(w.reshape(k, N_SHARD, n // N_SHARD), (1, 0, 2))


def kernel(x, c, ctx, c_ctx, w_mod, b_mod, w_in, q_norm_g, w_uq, kv_norm_g, w_ukv, conv_w, w_out, w_mlp1, w_mlp2, final_norm_g, loss_target, m_c_ctx, m_w_mod, m_b_mod, m_w_in, m_q_norm_g, m_w_uq, m_kv_norm_g, m_w_ukv, m_conv_w, m_w_out, m_w_mlp1, m_w_mlp2, m_final_norm_g, v_c_ctx, v_w_mod, v_b_mod, v_w_in, v_q_norm_g, v_w_uq, v_kv_norm_g, v_w_ukv, v_conv_w, v_w_out, v_w_mlp1, v_w_mlp2, v_final_norm_g):
    t_lat, t_ctx = x.shape[1], ctx.shape[1]
    assert t_ctx == TOK_TILE and t_lat % TOK_TILE == 0 and t_lat % GRID_W == 0
    mx, my, mc = _pos()
    j = 2 * mx + my
    ncol = w_mod.shape[2]
    x2, ctx2, tgt = x[0], ctx[0], loss_target[0]
    cctx_row = c_ctx.reshape(1, D_MODEL)

    b_sh = lax.dynamic_slice(b_mod, (0, j * ncol), (1, ncol))
    cw_pad = jnp.zeros((8, 128), F32).at[0:3, :].set(conv_w[0])
    c8, m_all, g_in, g_uq, g_ukv, g_out, g_m1, g_m2 = _prologue(
        c, cctx_row, w_mod[0], b_sh, cw_pad, (w_in[0], w_uq[0], w_ukv[0], w_out[0], w_mlp1[0], w_mlp2[0]), 3)
    mvec = m_all[:, 0, :].reshape(6, D_MODEL)
    mctx = m_all[:, 8, :].reshape(6, D_MODEL)
    zeros6 = jnp.zeros((6, D_MODEL), F32)
    mod_a = jnp.stack([jnp.concatenate([mvec[0:2], zeros6], axis=0), jnp.concatenate([mctx[0:2], zeros6], axis=0)])
    mod_b = jnp.concatenate([mvec[2:6], jnp.zeros((4, D_MODEL), F32)], axis=0)
    cw_full = jnp.pad(jnp.transpose(m_all[:, 9:12, 0:128], (1, 0, 2)).reshape(3, CONV_W), ((0, 5), (0, 0)))

    w_in_f = _cols_from_shards(g_in)
    zc = lambda n: jnp.zeros((D_MODEL, n), BF16)
    w_in_p = jnp.concatenate([w_in_f[:, 0:384], zc(64), w_in_f[:, 384:416], zc(32), w_in_f[:, 416:]], axis=1)
    w_uq_f = _cols_from_shards(g_uq).reshape(Q_RANK, N_HEADS, QK_DIM)
    w_uq_p = jnp.pad(w_uq_f, ((0, 0), (0, 0), (0, HEAD_PAD - QK_DIM))).reshape(Q_RANK, N_HEADS * HEAD_PAD)
    w_ukv_f = _cols_from_shards(g_ukv).reshape(KV_RANK, N_HEADS, QK_NOPE + V_DIM)
    padh = lambda a: jnp.pad(a, ((0, 0), (0, 0), (0, HEAD_PAD - a.shape[2]))).reshape(KV_RANK, N_HEADS * HEAD_PAD)
    w_ukv_p = jnp.concatenate([padh(w_ukv_f[:, :, :QK_NOPE]), padh(w_ukv_f[:, :, QK_NOPE:])], axis=1)
    cos_t, sin_a, sin_b = _rope_tables(t_lat, t_ctx)
    gf_row = final_norm_g.reshape(1, D_MODEL)
    c_idx = mc.reshape(1).astype(jnp.int32)
    j_idx = j.reshape(1).astype(jnp.int32)

    z, q, k, v, kt = _inproj_fwd(x2, ctx2, mod_a, w_in_p, q_norm_g, kv_norm_g, w_uq_p, w_ukv_p, cos_t, sin_a, sin_b)
    o, lse, g_out, w1, g_m2 = _attn_fwd(q, k, v, t_lat, (g_out, g_m1, g_m2))
    w_out_f = g_out.reshape(D_MODEL, D_MODEL)
    w2 = g_m2.reshape(D_FF, D_MODEL)
    r, da, h2, dy2, dx1, conv, acc_mlp, dy1, do, dgb, dyv = _mlp_fwdbwd(o, z, x2, tgt, mod_b, gf_row, cw_full, w_out_f,
                                                                         w1, w2)
    d_w1 = _wgrad(h2, da, "wgrad_mlp1", D_MODEL, FF_CHUNK)
    d_w2 = _wgrad(r, dy2, "wgrad_mlp2", FF_CHUNK, D_MODEL)
    d_wout, *big_got = _wgrad_out(o, conv, dy1, (d_w1, d_w2))
    d_wout = d_wout.reshape(N_SHARD, D_MODEL // N_SHARD, D_MODEL)
    big_grads = (d_w1, d_w2, d_wout)
    big_parts = _add_pairs(big_grads, big_got, c_idx, "rs_add_pairs_big")
    dqt, dk, dv, *big_recv = _attn_bwd(q, k, v, kt, o, do, lse, t_lat, big_parts)
    big_halves = _add_chips(big_parts, big_recv, j_idx, "rs_add_chips_big")
    gx, d_win, d_wuq, d_wukv, acc_in = _inproj_bwd(x2, ctx2, mod_a, z, dyv, dgb, dx1, dqt, dk, dv, cos_t, sin_a, sin_b,
                                                   w_in_p, w_uq_p, w_ukv_p, q_norm_g, kv_norm_g, cw_full)

    d_win_f = jnp.concatenate([d_win[:, 0:384], d_win[:, 448:480], d_win[:, 512:]], axis=1)
    d_wuq_f = d_wuq.reshape(Q_RANK, N_HEADS, HEAD_PAD)[:, :, 0:QK_DIM].reshape(Q_RANK, N_HEADS * QK_DIM)
    d_wukv3 = d_wukv.reshape(KV_RANK, 2, N_HEADS, HEAD_PAD)
    d_wukv_f = jnp.concatenate([d_wukv3[:, 0, :, 0:QK_NOPE], d_wukv3[:, 1, :, 0:V_DIM]], axis=2).reshape(KV_RANK, -1)
    rest = tuple(_cols_to_shards(a).astype(BF16) for a in (d_win_f, d_wuq_f, d_wukv_f))
    rest_got = _rs_sibling(rest, "rs_sibling_rest")
    rest_parts = _add_pairs(rest, rest_got, c_idx, "rs_add_pairs_rest")

    sv = jnp.concatenate([
        acc_in[0:2], acc_mlp[5:6], acc_mlp[2:4], acc_mlp[1:2],
        acc_in[2:4], acc_mlp[0:1], acc_in[4:5], acc_in[5:6], acc_in[6:9], acc_mlp[4:5],
        jnp.zeros((1, D_MODEL), F32)], axis=0)
    all_sv, red, o_cc, o_b, o_q, o_k, o_gf, *exchanged = _small_exchange(
        sv, w_mod[0], cctx_row, m_c_ctx.reshape(1, D_MODEL), v_c_ctx.reshape(1, D_MODEL),
        b_mod.reshape(6, D_MODEL), m_b_mod.reshape(6, D_MODEL), v_b_mod.reshape(6, D_MODEL),
        q_norm_g, m_q_norm_g, v_q_norm_g, kv_norm_g, m_kv_norm_g, v_kv_norm_g,
        gf_row, m_final_norm_g.reshape(1, D_MODEL), v_final_norm_g.reshape(1, D_MODEL), big_halves, rest_parts)
    big_theirs, rest_recv = exchanged[:len(big_halves)], exchanged[len(big_halves):]
    loss = red[14, 0]

    c9 = jnp.concatenate([c8[0::8], jnp.zeros((7, D_MODEL), F32)], axis=0)
    s_t = jnp.transpose(c9 * jax.nn.sigmoid(c9))
    dm_ex = all_sv[:, 0:6, :].reshape(8, 6 * D_MODEL)
    dm_ctx = jnp.concatenate([red[6:8].reshape(1, 2 * D_MODEL), jnp.zeros((1, 4 * D_MODEL), F32)], axis=1)
    dm16 = jnp.concatenate([dm_ex, dm_ctx, jnp.zeros((7, 6 * D_MODEL), F32)], axis=0)
    dm_sh = lax.dynamic_slice(dm16, (0, j * ncol), (16, ncol))
    g_wmod, d_wmod, nm_wmod, nv_wmod = _wmod_update(s_t, dm_sh, w_mod[0], m_w_mod[0], v_w_mod[0])

    g_cw = lax.dynamic_slice(red[11:14, 0:CONV_W], (0, j * 128), (3, 128))
    d_cw, nm_cw, nv_cw = _adamw_call(conv_w[0], g_cw, m_conv_w[0], v_conv_w[0], "adamw_conv")

    rest_halves = _add_chips(rest_parts, rest_recv, j_idx, "rs_add_chips_rest")
    g_win, g_wuq, g_wukv = _rs_join(rest_halves, "rs_join_rest")
    upd = {}
    for name, w_, g_, m_, v_ in (("in", w_in, g_win, m_w_in, v_w_in), ("uq", w_uq, g_wuq, m_w_uq, v_w_uq),
                                 ("ukv", w_ukv, g_wukv, m_w_ukv, v_w_ukv)):
        upd[name] = _adamw_call(w_[0], g_, m_[0], v_[0], "adamw_" + name)
    g_w1, *upd["mlp1"] = _adamw_halves(w_mlp1[0], big_halves[0], big_theirs[0], m_w_mlp1[0], v_w_mlp1[0], c_idx,
                                       "adamw_mlp1")
    g_w2, *upd["mlp2"] = _adamw_halves(w_mlp2[0], big_halves[1], big_theirs[1], m_w_mlp2[0], v_w_mlp2[0], c_idx,
                                       "adamw_mlp2")
    g_wout, *upd["out"] = _adamw_halves(w_out[0], big_halves[2], big_theirs[2], m_w_out[0], v_w_out[0], c_idx,
                                        "adamw_out")

    def four(o4, shape):
        return [o4[r].reshape(shape) for r in range(4)]

    cc4 = four(o_cc, (D_MODEL,))
    b4 = [o_b[r].reshape(1, 6 * D_MODEL) for r in range(4)]
    q4 = four(o_q, (1, Q_RANK))
    k4 = four(o_k, (1, KV_RANK))
    gf4 = four(o_gf, (D_MODEL,))
    big = {"in": g_win, "uq": g_wuq, "ukv": g_wukv, "out": g_wout, "mlp1": g_w1, "mlp2": g_w2}

    def leaf(idx):
        wm = (g_wmod, d_wmod, nm_wmod, nv_wmod)[idx]
        cwv = (g_cw, d_cw, nm_cw, nv_cw)[idx]
        bigv = {n: (big[n] if idx == 0 else upd[n][idx - 1]) for n in big}
        return [cc4[idx], wm[None], b4[idx], bigv["in"][None], q4[idx], bigv["uq"][None], k4[idx], bigv["ukv"][None],
                cwv[None], bigv["out"][None], bigv["mlp1"][None], bigv["mlp2"][None], gf4[idx]]

    return (loss, gx[None], *leaf(0), *leaf(1), *leaf(2), *leaf(3))
```

```python
import functools
import math

import jax
import jax.numpy as jnp
from jax import lax
from jax.experimental import pallas as pl
from jax.experimental.pallas import tpu as pltpu

F32 = jnp.float32
BF16 = jnp.bfloat16
MESH = pl.DeviceIdType.MESH
HIGHEST = lax.Precision.HIGHEST

D_MODEL = 1024
N_HEADS = 8
QK_NOPE = 64
QK_ROPE = 32
QK_DIM = QK_NOPE + QK_ROPE
V_DIM = 64
Q_RANK = 256
KV_RANK = 128
CONV_W = 512
D_FF = 4096
GRID_W = 64
ROPE_THETA = 10000.0
EPS = 1e-6
ATTN_SCALE = 1.0 / math.sqrt(QK_DIM)
HEAD_PAD = 128
Z_COLS = 2048
ROPE_LANE0 = QK_NOPE
N_SHARD = 4
TOK_TILE = 256
FF_CHUNK = 1024
MLP_FF_CHUNK = 1024
KEY_CHUNK = 512
ATTN_FWD_Q_BLOCK = 1024
ATTN_HEADS_PER_STEP = 4
ATTN_BWD_HEADS_PER_STEP = 2
ATTN_BWD_Q_BLOCK = 512
KEY_CHUNK_BWD = 512

ADAM_LR = 0.001
ADAM_B1 = 0.9
ADAM_B2 = 0.999
ADAM_EPS = 1e-08
ADAM_WD = 0.01
ADAM_STEP = 10

LOG2E = 1.4426950408889634

VMEM_LIMIT = 56 * 1024 * 1024
STAGE_VMEM_LIMIT = 32 * 1024 * 1024


def _pos():
    return lax.axis_index("x"), lax.axis_index("y"), lax.axis_index("c")


def _dot(a, b):
    return jnp.dot(a, b, preferred_element_type=F32)


def _dot_nt(a, b):
    return lax.dot_general(a, b, (((1,), (1,)), ((), ())), preferred_element_type=F32)


def _dot_tn(a, b):
    return lax.dot_general(a, b, (((0,), (0,)), ((), ())), preferred_element_type=F32)


def _rope(v, cos, sa, sb):
    return v * cos + pltpu.roll(v, 8, 1) * sa + pltpu.roll(v, HEAD_PAD - 8, 1) * sb


def _unrope(g, cos, sa, sb):
    return g * cos + pltpu.roll(g * sa, HEAD_PAD - 8, 1) + pltpu.roll(g * sb, 8, 1)


def _sigmoid(v):
    return 1.0 / (1.0 + jnp.exp(-v))


def _adamw(w, g, m, v):
    m = ADAM_B1 * m + (1.0 - ADAM_B1) * g
    v = ADAM_B2 * v + (1.0 - ADAM_B2) * (g * g)
    m_hat = m / (1.0 - ADAM_B1 ** ADAM_STEP)
    v_hat = v / (1.0 - ADAM_B2 ** ADAM_STEP)
    delta = -ADAM_LR * (m_hat / (jnp.sqrt(v_hat) + ADAM_EPS) + ADAM_WD * w)
    return delta, m, v


def _shift_rows(u, prev_row, next_row):
    n = u.shape[0]
    rows = lax.broadcasted_iota(jnp.int32, u.shape, 0)
    um1 = jnp.where(rows == 0, prev_row, pltpu.roll(u, 1, 0))
    up1 = jnp.where(rows == n - 1, next_row, pltpu.roll(u, n - 1, 0))
    return um1, up1


def _const_spec(shape):
    nd = len(shape)
    return pl.BlockSpec(shape, lambda *_: (0,) * nd)


def _resident_spec(shape):
    nd = len(shape)
    return pl.BlockSpec(shape, lambda *_: (0,) * nd, pipeline_mode=pl.Buffered(1))


def _peer(r, x, y, c):
    px = 1 - x if r & 4 else x
    py = 1 - y if r & 2 else y
    pc = 1 - c if r & 1 else c
    return (px, py, pc)


def _prologue(c_row, cctx_row, w_mod_sh, b_sh, cw_sh, srcs, n_gather):
    ncol = w_mod_sh.shape[1]
    n = len(srcs)
    n_split = 4

    def body(c_ref, cctx_ref, w_ref, b_ref, cw_ref, *refs):
        ins, (c8_ref, m_ref), outs = refs[:n], refs[n:n + 2], refs[n + 2:2 * n + 2]
        mine_ref, msh_ref = refs[2 * n + 2:2 * n + 4]
        f32s, bfs = refs[2 * n + 4:3 * n + 4], refs[3 * n + 4:4 * n + 4]
        ssem, rsem, ssem2, rsem2, lsem_in, lsem_out = refs[4 * n + 4:4 * n + 10]
        x, y, c = _pos()
        me = 4 * x + 2 * y + c
        j = 2 * x + y

        def pieces(rows):
            step = rows // n_split
            return [pl.ds(q * step, step) for q in range(n_split)]

        for t in range(n):
            for sl in pieces(ins[t].shape[0]):
                pltpu.make_async_copy(ins[t].at[sl], f32s[t].at[sl], lsem_in.at[t]).start()
        mine_ref[...] = jnp.zeros(mine_ref.shape, F32)
        mine_ref[0:1, :] = c_ref[...]
        my_rows = pl.ds(pl.multiple_of(8 * me, 8), 8)
        sends = []
        for r in range(1, 8):
            cp = pltpu.make_async_remote_copy(
                src_ref=mine_ref, dst_ref=c8_ref.at[my_rows], send_sem=ssem.at[r - 1], recv_sem=rsem.at[r - 1],
                device_id=_peer(r, x, y, c), device_id_type=MESH)
            cp.start()
            sends.append(cp)

        def cast_and_store(t):
            pltpu.make_async_copy(ins[t], f32s[t], lsem_in.at[t]).wait()
            bfs[t][...] = f32s[t][...].astype(BF16)
            for sl in pieces(ins[t].shape[0]):
                pltpu.make_async_copy(bfs[t].at[sl], outs[t].at[j, sl], lsem_out.at[t]).start()

        gather = _ShardGather(outs[:n_gather], *refs[4 * n + 10:])
        for t in range(n_gather):
            cast_and_store(t)
        for t in range(n_gather):
            pltpu.make_async_copy(bfs[t], outs[t].at[j], lsem_out.at[t]).wait()
        gather.start()
        for cp in sends:
            cp.wait()
        c8_ref[my_rows, :] = mine_ref[...]
        c8_ref[64:72, :] = jnp.zeros((8, D_MODEL), F32)
        c8_ref[64:65, :] = cctx_ref[...]
        cv = c8_ref[...]
        s = cv * _sigmoid(cv)
        m = jnp.dot(s, w_ref[...], precision=HIGHEST, preferred_element_type=F32) + b_ref[...]
        msh_ref[0:64, :] = m[0:64, :]
        msh_ref[64:72, :] = jnp.zeros((8, ncol), F32)
        msh_ref[64:65, :] = m[64:65, :]
        msh_ref[65:68, 0:128] = cw_ref[0:3, :]
        m_ref[j, 0:8, :] = msh_ref[my_rows, :]
        m_ref[j, 8:16, :] = msh_ref[64:72, :]
        sends2 = []
        for k, (px, py) in enumerate(_chips(x, y)):
            theirs = pl.ds(pl.multiple_of(8 * (4 * px + 2 * py + c), 8), 8)
            for half, src in enumerate((msh_ref.at[theirs], msh_ref.at[64:72])):
                cp = pltpu.make_async_remote_copy(
                    src_ref=src, dst_ref=m_ref.at[j, 8 * half:8 * half + 8], send_sem=ssem2.at[2 * k + half],
                    recv_sem=rsem2.at[2 * k + half], device_id=(px, py, c), device_id_type=MESH)
                cp.start()
                sends2.append(cp)
        for t in range(n_gather, n):
            cast_and_store(t)
        gather.forward()
        gather.finish()
        for t in range(n_gather, n):
            pltpu.make_async_copy(bfs[t], outs[t].at[j], lsem_out.at[t]).wait()
        for cp in sends2:
            cp.wait()

    vm = pl.BlockSpec(memory_space=pltpu.VMEM)
    hbm = pl.BlockSpec(memory_space=pl.ANY)
    return pl.pallas_call(
        body, name="prologue",
        out_shape=(jax.ShapeDtypeStruct((72, D_MODEL), F32), jax.ShapeDtypeStruct((N_SHARD, 16, ncol), F32))
        + tuple(jax.ShapeDtypeStruct((N_SHARD,) + a.shape, BF16) for a in srcs),
        in_specs=[vm] * 5 + [hbm] * n, out_specs=(vm, vm) + (hbm,) * n,
        scratch_shapes=[pltpu.VMEM((8, D_MODEL), F32), pltpu.VMEM((72, ncol), F32)]
        + [pltpu.VMEM(a.shape, F32) for a in srcs] + [pltpu.VMEM(a.shape, BF16) for a in srcs]
        + [pltpu.SemaphoreType.DMA((7,)), pltpu.SemaphoreType.DMA((7,)),
           pltpu.SemaphoreType.DMA((6,)), pltpu.SemaphoreType.DMA((6,)),
           pltpu.SemaphoreType.DMA((n,)), pltpu.SemaphoreType.DMA((n,))] + _gather_sems(n_gather),
        compiler_params=pltpu.CompilerParams(vmem_limit_bytes=VMEM_LIMIT),
    )(c_row, cctx_row, w_mod_sh, b_sh, cw_sh, *srcs)


def _chips(x, y):
    return [(1 - x, y), (x, 1 - y), (1 - x, 1 - y)]


def _halves(ref, c, align):
    hr = ref.shape[-2] // 2
    return (pl.ds(pl.multiple_of(c * hr, align), hr), pl.ds(pl.multiple_of((1 - c) * hr, align), hr))


class _ShardGather:
    def __init__(self, refs, ssem, rsem, fsend, frecv):
        self.refs, self.sems = refs, (ssem, rsem, fsend, frecv)
        self.x, self.y, self.c = _pos()
        self.j = 2 * self.x + self.y

    def _ici(self, a, k, slot):
        g = self.refs[a]
        ssem, rsem, _, _ = self.sems
        mine, _ = _halves(g, self.c, 16)
        px, py = _chips(self.x, self.y)[k]
        return pltpu.make_async_remote_copy(
            src_ref=g.at[self.j, mine], dst_ref=g.at[slot, mine], send_sem=ssem.at[3 * a + k],
            recv_sem=rsem.at[3 * a + k], device_id=(px, py, self.c), device_id_type=MESH)

    def _d2d(self, a, k, to_other_half):
        g = self.refs[a]
        _, _, fsend, frecv = self.sems
        mine, theirs = _halves(g, self.c, 16)
        px, py = _chips(self.x, self.y)[k]
        jk = 2 * px + py
        return pltpu.make_async_remote_copy(
            src_ref=g.at[jk, mine], dst_ref=g.at[jk, theirs if to_other_half else mine],
            send_sem=fsend.at[3 * a + k], recv_sem=frecv.at[3 * a + k],
            device_id=(self.x, self.y, 1 - self.c), device_id_type=MESH)

    def start(self):
        for a in range(len(self.refs)):
            for k in range(3):
                self._ici(a, k, self.j).start()

    def forward(self):
        for a in range(len(self.refs)):
            for k, (px, py) in enumerate(_chips(self.x, self.y)):
                self._ici(a, k, 2 * px + py).wait_recv()
                self._d2d(a, k, False).start()

    def finish(self):
        for a in range(len(self.refs)):
            for k in range(3):
                self._d2d(a, k, True).wait()
                self._ici(a, k, self.j).wait_send()


def _gather_sems(n_arrays):
    return [pltpu.SemaphoreType.DMA((3 * n_arrays,)) for _ in range(4)]


class _SiblingSend:
    def __init__(self, g_refs, got_refs, ssem, rsem):
        self.g_refs, self.got_refs, self.ssem, self.rsem = g_refs, got_refs, ssem, rsem
        self.x, self.y, self.c = _pos()

    def _copy(self, a, shard):
        _, theirs = _halves(self.g_refs[a], self.c, 16)
        src = self.g_refs[a].at[:, theirs] if shard is None else self.g_refs[a].at[shard, theirs]
        dst = self.got_refs[a] if shard is None else self.got_refs[a].at[shard]
        return pltpu.make_async_remote_copy(
            src_ref=src, dst_ref=dst, send_sem=self.ssem.at[a], recv_sem=self.rsem.at[a],
            device_id=(self.x, self.y, 1 - self.c), device_id_type=MESH)

    def start(self):
        for a in range(len(self.g_refs)):
            for s in range(N_SHARD):
                self._copy(a, s).start()

    def finish(self):
        for a in range(len(self.g_refs)):
            self._copy(a, None).wait()


class _SiblingSwap:
    def __init__(self, h_refs, t_refs, ssem, rsem):
        self.h_refs, self.t_refs, self.ssem, self.rsem = h_refs, t_refs, ssem, rsem
        self.x, self.y, self.c = _pos()

    def _copy(self, a):
        return pltpu.make_async_remote_copy(
            src_ref=self.h_refs[a], dst_ref=self.t_refs[a], send_sem=self.ssem.at[a], recv_sem=self.rsem.at[a],
            device_id=(self.x, self.y, 1 - self.c), device_id_type=MESH)

    def start(self):
        for a in range(len(self.h_refs)):
            self._copy(a).start()

    def finish(self):
        for a in range(len(self.h_refs)):
            self._copy(a).wait()


def _rs_sibling(arrs, name):
    n = len(arrs)

    def body(*refs):
        send = _SiblingSend(refs[:n], refs[n:2 * n], refs[2 * n], refs[2 * n + 1])
        send.start()
        send.finish()

    hbm = pl.BlockSpec(memory_space=pl.ANY)
    return pl.pallas_call(
        body, name=name,
        out_shape=tuple(jax.ShapeDtypeStruct((N_SHARD, a.shape[1] // 2, a.shape[2]), BF16) for a in arrs),
        in_specs=[hbm] * n, out_specs=(hbm,) * n,
        scratch_shapes=[pltpu.SemaphoreType.DMA((n,)), pltpu.SemaphoreType.DMA((n,))],
    )(*arrs)


class _ChipScatter:
    def __init__(self, parts, gots, ssem, rsem):
        self.parts, self.gots, self.ssem, self.rsem = parts, gots, ssem, rsem
        self.x, self.y, self.c = _pos()

    def _copy(self, a, k):
        px, py = _chips(self.x, self.y)[k]
        return pltpu.make_async_remote_copy(
            src_ref=self.parts[a].at[2 * px + py], dst_ref=self.gots[a].at[k], send_sem=self.ssem.at[3 * a + k],
            recv_sem=self.rsem.at[3 * a + k], device_id=(px, py, self.c), device_id_type=MESH)

    def start(self):
        for a in range(len(self.parts)):
            for k in range(3):
                self._copy(a, k).start()

    def finish(self):
        for a in range(len(self.parts)):
            for k in range(3):
                self._copy(a, k).wait()


def _rs_join(halves, name):
    n = len(halves)

    def body(*refs):
        h_refs, f_refs, stages = refs[:n], refs[n:2 * n], refs[2 * n:3 * n]
        lsem_in, lsem_out, ssem, rsem = refs[3 * n:]
        x, y, c = _pos()
        remote = []
        for a in range(n):
            mine, _ = _halves(f_refs[a], c, 8)
            cp = pltpu.make_async_remote_copy(
                src_ref=h_refs[a], dst_ref=f_refs[a].at[mine], send_sem=ssem.at[a], recv_sem=rsem.at[a],
                device_id=(x, y, 1 - c), device_id_type=MESH)
            cp.start()
            remote.append(cp)
            pltpu.make_async_copy(h_refs[a], stages[a], lsem_in.at[a]).start()
        local = []
        for a in range(n):
            mine, _ = _halves(f_refs[a], c, 8)
            pltpu.make_async_copy(h_refs[a], stages[a], lsem_in.at[a]).wait()
            cp = pltpu.make_async_copy(stages[a], f_refs[a].at[mine], lsem_out.at[a])
            cp.start()
            local.append(cp)
        for cp in remote + local:
            cp.wait()

    hbm = pl.BlockSpec(memory_space=pl.ANY)
    return pl.pallas_call(
        body, name=name,
        out_shape=tuple(jax.ShapeDtypeStruct((2 * h.shape[0], h.shape[1]), F32) for h in halves),
        in_specs=[hbm] * n, out_specs=(hbm,) * n,
        scratch_shapes=[pltpu.VMEM(h.shape, F32) for h in halves]
        + [pltpu.SemaphoreType.DMA((n,)) for _ in range(4)],
        compiler_params=pltpu.CompilerParams(vmem_limit_bytes=STAGE_VMEM_LIMIT),
    )(*halves)


def _add_pairs(arrs, gots, c_idx, name):
    n = len(arrs)

    def body(c_ref, *refs):
        for a in range(n):
            refs[2 * n + a][...] = (refs[a][...].astype(F32) + refs[n + a][...].astype(F32)).astype(BF16)

    def half_spec(g, mine):
        hr, cols = g.shape[1], g.shape[2]
        if mine:
            return pl.BlockSpec((1, hr, cols), lambda s, c_ref: (s, c_ref[0], 0))
        return pl.BlockSpec((1, hr, cols), lambda s, c_ref: (s, 0, 0))

    grid_spec = pltpu.PrefetchScalarGridSpec(
        num_scalar_prefetch=1, grid=(N_SHARD,),
        in_specs=[half_spec(g, True) for g in gots] + [half_spec(g, False) for g in gots],
        out_specs=tuple(half_spec(g, False) for g in gots))
    return pl.pallas_call(
        body, name=name, grid_spec=grid_spec, out_shape=tuple(jax.ShapeDtypeStruct(g.shape, BF16) for g in gots),
        compiler_params=pltpu.CompilerParams(vmem_limit_bytes=STAGE_VMEM_LIMIT),
    )(c_idx, *arrs, *gots)


def _add_chips(parts, gots, j_idx, name):
    n = len(parts)
    n_split = 2

    def body(j_ref, *refs):
        for a in range(n):
            acc = refs[a][0].astype(F32)
            for k in range(3):
                acc = acc + refs[n + a][k].astype(F32)
            refs[2 * n + a][...] = acc

    in_specs, out_specs = [], []
    for g in gots:
        rb, cols = g.shape[1] // n_split, g.shape[2]
        in_specs.append(pl.BlockSpec((1, rb, cols), lambda r, j_ref: (j_ref[0], r, 0)))
        out_specs.append(pl.BlockSpec((rb, cols), lambda r, j_ref: (r, 0)))
    for g in gots:
        rb, cols = g.shape[1] // n_split, g.shape[2]
        in_specs.append(pl.BlockSpec((3, rb, cols), lambda r, j_ref: (0, r, 0)))
    grid_spec = pltpu.PrefetchScalarGridSpec(
        num_scalar_prefetch=1, grid=(n_split,), in_specs=in_specs, out_specs=tuple(out_specs))
    return pl.pallas_call(
        body, name=name, grid_spec=grid_spec,
        out_shape=tuple(jax.ShapeDtypeStruct(g.shape[1:], F32) for g in gots),
        compiler_params=pltpu.CompilerParams(vmem_limit_bytes=STAGE_VMEM_LIMIT),
    )(j_idx, *parts, *gots)


def _small_exchange(sv, w_mod_sh, cctx, m_cctx, v_cctx, bmod, m_bmod, v_bmod, qg, m_qg, v_qg, kvg, m_kvg, v_kvg,
                    gf, m_gf, v_gf, swap_halves, scatter_parts):
    ncol = w_mod_sh.shape[1]
    n_s, n_p = len(swap_halves), len(scatter_parts)
    n_x = n_s + n_p

    def body(sv_ref, w_ref, cctx_ref, mcc_ref, vcc_ref, b_ref, mb_ref, vb_ref, qg_ref, mq_ref, vq_ref,
             kg_ref, mk_ref, vk_ref, gf_ref, mgf_ref, vgf_ref, *rest):
        all_ref, red_ref, occ_ref, ob_ref, oq_ref, ok_ref, ogf_ref = rest[n_x:n_x + 7]
        vec_ref, part_ref, ssem, rsem, ssem2, rsem2, wsend, wrecv, psend, precv = rest[2 * n_x + 7:]
        swap = _SiblingSwap(rest[:n_s], rest[n_x + 7:n_x + 7 + n_s], wsend, wrecv)
        scatter = _ChipScatter(rest[n_s:n_x], rest[n_x + 7 + n_s:2 * n_x + 7], psend, precv)
        swap.start()
        scatter.start()
        x, y, c = _pos()
        me = 4 * x + 2 * y + c
        j = 2 * x + y
        sends = []
        for r in range(1, 8):
            cp = pltpu.make_async_remote_copy(
                src_ref=sv_ref, dst_ref=all_ref.at[me], send_sem=ssem.at[r - 1], recv_sem=rsem.at[r - 1],
                device_id=_peer(r, x, y, c), device_id_type=MESH)
            cp.start()
            sends.append(cp)
        for cp in sends:
            cp.wait()
        all_ref[me] = sv_ref[...]
        red = all_ref[0]
        for d in range(1, 8):
            red = red + all_ref[d]
        red_ref[...] = red
        vec_ref[...] = jnp.zeros(vec_ref.shape, F32)

        @pl.when(j == 0)
        def _():
            vec_ref[0:1, 0:1024] = red[6:7, :]
            vec_ref[0:1, 1024:1536] = red[7:8, 0:512]

        @pl.when(j == 1)
        def _():
            vec_ref[0:1, 0:512] = red[7:8, 512:1024]

        part = lax.dot_general(vec_ref[...], w_ref[...], (((1,), (1,)), ((), ())), precision=HIGHEST,
                               preferred_element_type=F32)
        part_ref[j] = part
        sends2 = []
        for k, r in enumerate((4, 2, 6)):
            cp = pltpu.make_async_remote_copy(
                src_ref=part_ref.at[j], dst_ref=part_ref.at[j], send_sem=ssem2.at[k], recv_sem=rsem2.at[k],
                device_id=_peer(r, x, y, c), device_id_type=MESH)
            cp.start()
            sends2.append(cp)
        for cp in sends2:
            cp.wait()
        tot = part_ref[0]
        for s in range(1, N_SHARD):
            tot = tot + part_ref[s]
        cc = cctx_ref[...]
        sg = _sigmoid(cc)
        g_cc = tot[0:1, :] * (sg * (1.0 + cc * (1.0 - sg)))
        d_, m_, v_ = _adamw(cc, g_cc, mcc_ref[...], vcc_ref[...])
        occ_ref[0:1, :] = g_cc
        occ_ref[1:2, :] = d_
        occ_ref[2:3, :] = m_
        occ_ref[3:4, :] = v_
        occ_ref[4:8, :] = jnp.zeros((4, D_MODEL), F32)
        g_b = red[0:6, :]
        pad = jnp.concatenate([red[6:8, :], jnp.zeros((4, D_MODEL), F32)], axis=0)
        g_b = g_b + pad
        d_, m_, v_ = _adamw(b_ref[...], g_b, mb_ref[...], vb_ref[...])
        ob_ref[0] = g_b
        ob_ref[1] = d_
        ob_ref[2] = m_
        ob_ref[3] = v_
        g_q = red[9:10, 0:Q_RANK]
        d_, m_, v_ = _adamw(qg_ref[...], g_q, mq_ref[...], vq_ref[...])
        oq_ref[0:1, :] = g_q
        oq_ref[1:2, :] = d_
        oq_ref[2:3, :] = m_
        oq_ref[3:4, :] = v_
        oq_ref[4:8, :] = jnp.zeros((4, Q_RANK), F32)
        g_k = red[10:11, 0:KV_RANK]
        d_, m_, v_ = _adamw(kg_ref[...], g_k, mk_ref[...], vk_ref[...])
        ok_ref[0:1, :] = g_k
        ok_ref[1:2, :] = d_
        ok_ref[2:3, :] = m_
        ok_ref[3:4, :] = v_
        ok_ref[4:8, :] = jnp.zeros((4, KV_RANK), F32)
        g_f = red[8:9, :]
        d_, m_, v_ = _adamw(gf_ref[...], g_f, mgf_ref[...], vgf_ref[...])
        ogf_ref[0:1, :] = g_f
        ogf_ref[1:2, :] = d_
        ogf_ref[2:3, :] = m_
        ogf_ref[3:4, :] = v_
        ogf_ref[4:8, :] = jnp.zeros((4, D_MODEL), F32)
        swap.finish()
        scatter.finish()

    vm = pl.BlockSpec(memory_space=pltpu.VMEM)
    hbm = pl.BlockSpec(memory_space=pl.ANY)
    out_shape = (
        jax.ShapeDtypeStruct((8, 16, D_MODEL), F32),
        jax.ShapeDtypeStruct((16, D_MODEL), F32),
        jax.ShapeDtypeStruct((8, D_MODEL), F32),
        jax.ShapeDtypeStruct((4, 6, D_MODEL), F32),
        jax.ShapeDtypeStruct((8, Q_RANK), F32),
        jax.ShapeDtypeStruct((8, KV_RANK), F32),
        jax.ShapeDtypeStruct((8, D_MODEL), F32),
    ) + tuple(jax.ShapeDtypeStruct(h.shape, F32) for h in swap_halves) + tuple(
        jax.ShapeDtypeStruct((3,) + p.shape[1:], BF16) for p in scatter_parts)
    return pl.pallas_call(
        body, name="small_exchange", out_shape=out_shape, in_specs=[vm] * 17 + [hbm] * n_x,
        out_specs=tuple([vm] * 7) + (hbm,) * n_x,
        scratch_shapes=[pltpu.VMEM((8, ncol), F32), pltpu.VMEM((N_SHARD, 8, D_MODEL), F32),
                        pltpu.SemaphoreType.DMA((7,)), pltpu.SemaphoreType.DMA((7,)),
                        pltpu.SemaphoreType.DMA((3,)), pltpu.SemaphoreType.DMA((3,)),
                        pltpu.SemaphoreType.DMA((n_s,)), pltpu.SemaphoreType.DMA((n_s,)),
                        pltpu.SemaphoreType.DMA((3 * n_p,)), pltpu.SemaphoreType.DMA((3 * n_p,))],
        compiler_params=pltpu.CompilerParams(vmem_limit_bytes=VMEM_LIMIT),
    )(sv, w_mod_sh, cctx, m_cctx, v_cctx, bmod, m_bmod, v_bmod, qg, m_qg, v_qg, kvg, m_kvg, v_kvg, gf, m_gf, v_gf,
      *swap_halves, *scatter_parts)


def _inproj_fwd(x2, ctx2, mod_a, w_in, qg, kvg, w_uq, w_ukv, cos_t, sin_a, sin_b):
    t_lat, t_ctx = x2.shape[0], ctx2.shape[0]
    tm = TOK_TILE
    n_lat = t_lat // tm
    n_all = n_lat + t_ctx // tm
    e_rows = t_lat + t_ctx

    def body(x_ref, ctx_ref, mod_ref, win_ref, qg_ref, kvg_ref, wuq_ref, wukv_ref, cos_ref, sa_ref, sb_ref,
             z_ref, q_ref, k_ref, v_ref, kt_ref):
        i = pl.program_id(0)
        xin = jnp.where(i < n_lat, x_ref[...], ctx_ref[...])
        xn = xin * lax.rsqrt(jnp.mean(xin * xin, axis=-1, keepdims=True) + EPS)
        h1 = (xn * (1.0 + mod_ref[0, 1:2, :]) + mod_ref[0, 0:1, :]).astype(BF16)
        z = _dot(h1, win_ref[...])
        z_ref[...] = z
        cos, sa, sb = cos_ref[...], sa_ref[...], sb_ref[...]
        cq = z[:, 0:Q_RANK]
        cqn = (cq * lax.rsqrt(jnp.mean(cq * cq, axis=-1, keepdims=True) + EPS) * qg_ref[...]).astype(BF16)
        q = _dot(cqn, wuq_ref[...])
        ckv = z[:, Q_RANK:Q_RANK + KV_RANK]
        ckvn = (ckv * lax.rsqrt(jnp.mean(ckv * ckv, axis=-1, keepdims=True) + EPS) * kvg_ref[...]).astype(BF16)
        kv = _dot(ckvn, wukv_ref[...])
        kr = _rope(z[:, Q_RANK + KV_RANK:Q_RANK + KV_RANK + HEAD_PAD], cos, sa, sb)
        ones_lane = lax.broadcasted_iota(jnp.int32, (tm, HEAD_PAD), 1) == V_DIM
        for h in range(N_HEADS):
            lo = h * HEAD_PAD
            q_ref[h] = _rope(q[:, lo:lo + HEAD_PAD], cos, sa, sb).astype(BF16)
            kh = kv[:, lo:lo + HEAD_PAD] + kr
            k_ref[h] = kh.astype(BF16)
            kt_ref[h] = kh.T.astype(BF16)
            vh = kv[:, N_HEADS * HEAD_PAD + lo:N_HEADS * HEAD_PAD + lo + HEAD_PAD]
            v_ref[h] = jnp.where(ones_lane, 1.0, vh).astype(BF16)

    row = lambda i: (i, 0)
    head_spec = pl.BlockSpec((N_HEADS, tm, HEAD_PAD), lambda i: (0, i, 0))
    head_shape = jax.ShapeDtypeStruct((N_HEADS, e_rows, HEAD_PAD), BF16)
    return pl.pallas_call(
        body, name="inproj_fwd", grid=(n_all,),
        out_shape=(jax.ShapeDtypeStruct((e_rows, Z_COLS), F32), head_shape, head_shape, head_shape,
                   jax.ShapeDtypeStruct((N_HEADS, HEAD_PAD, e_rows), BF16)),
        in_specs=[
            pl.BlockSpec((tm, D_MODEL), lambda i: (jnp.minimum(i, n_lat - 1), 0)),
            _const_spec((tm, D_MODEL)),
            pl.BlockSpec((1, 8, D_MODEL), lambda i: (i // n_lat, 0, 0)),
            _const_spec(w_in.shape), _const_spec(qg.shape), _const_spec(kvg.shape),
            _const_spec(w_uq.shape), _const_spec(w_ukv.shape),
            pl.BlockSpec((tm, HEAD_PAD), row), pl.BlockSpec((tm, HEAD_PAD), row), pl.BlockSpec((tm, HEAD_PAD), row),
        ],
        out_specs=(pl.BlockSpec((tm, Z_COLS), row), head_spec, head_spec, head_spec,
                   pl.BlockSpec((N_HEADS, HEAD_PAD, tm), lambda i: (0, 0, i))),
        compiler_params=pltpu.CompilerParams(vmem_limit_bytes=VMEM_LIMIT),
    )(x2, ctx2, mod_a, w_in, qg, kvg, w_uq, w_ukv, cos_t, sin_a, sin_b)


def _key_chunks(e_rows, size):
    n_chunks = max(1, e_rows // size)
    return [(ci * size, size if ci < n_chunks - 1 else e_rows - ci * size) for ci in range(n_chunks)]


def _attn_fwd(q, k, v, t_lat, shard_arrays):
    e_rows = k.shape[1]
    tq = min(t_lat, ATTN_FWD_Q_BLOCK)
    bounds = _key_chunks(e_rows, KEY_CHUNK)
    c2 = ATTN_SCALE * LOG2E

    hb = ATTN_HEADS_PER_STEP
    n_hb = N_HEADS // hb

    def body(q_ref, k_ref, v_ref, o_ref, lse_ref):
        qs = [q_ref[b] for b in range(hb)]
        m, acc = [None] * hb, [None] * hb
        for lo, n in bounds:
            for b in range(hb):
                s = _dot_nt(qs[b], k_ref[b, lo:lo + n, :])
                mc = jnp.max(s, axis=-1, keepdims=True)
                m_new = mc if m[b] is None else jnp.maximum(m[b], mc)
                p = jnp.exp2((s - m_new) * c2)
                pv = _dot(p.astype(BF16), v_ref[b, lo:lo + n, :])
                acc[b] = pv if m[b] is None else acc[b] * jnp.exp2((m[b] - m_new) * c2) + pv
                m[b] = m_new
        outs = []
        for b in range(hb):
            l = acc[b][:, V_DIM:V_DIM + 1]
            outs.append(acc[b] * (1.0 / l))
            lse = (m[b] * ATTN_SCALE + jnp.log(l)) * LOG2E
            lse_ref[b] = jnp.broadcast_to(lse, (tq, HEAD_PAD)).T[0:1, :]
        low = lax.broadcasted_iota(jnp.int32, (tq, HEAD_PAD), 1) < V_DIM
        pairs = [jnp.where(low, outs[b], pltpu.roll(outs[b + 1], V_DIM, 1)) for b in range(0, hb, 2)]
        o_ref[...] = jnp.concatenate(pairs, axis=1).astype(BF16)

    n_w = len(shard_arrays)
    n_q = t_lat // tq

    def body_with_gather(q_ref, k_ref, v_ref, *rest):
        o_ref, lse_ref = rest[n_w], rest[n_w + 1]
        gather = _ShardGather(rest[n_w + 2:2 * n_w + 2], *rest[2 * n_w + 2:])
        step = pl.program_id(0) * n_q + pl.program_id(1)
        pl.when(step == 0)(gather.start)
        pl.when(step == n_hb * n_q // 2)(gather.forward)
        body(q_ref, k_ref, v_ref, o_ref, lse_ref)
        pl.when(step == n_hb * n_q - 1)(gather.finish)

    hbm = pl.BlockSpec(memory_space=pl.ANY)
    return pl.pallas_call(
        body_with_gather, name="attn_fwd", grid=(n_hb, n_q),
        out_shape=(jax.ShapeDtypeStruct((t_lat, N_HEADS * V_DIM), BF16),
                   jax.ShapeDtypeStruct((N_HEADS, 1, t_lat), F32))
        + tuple(jax.ShapeDtypeStruct(a.shape, a.dtype) for a in shard_arrays),
        in_specs=[pl.BlockSpec((hb, tq, HEAD_PAD), lambda h, i: (h, i, 0)),
                  pl.BlockSpec((hb, e_rows, HEAD_PAD), lambda h, i: (h, 0, 0)),
                  pl.BlockSpec((hb, e_rows, HEAD_PAD), lambda h, i: (h, 0, 0))] + [hbm] * n_w,
        out_specs=(pl.BlockSpec((tq, hb * V_DIM), lambda h, i: (i, h)),
                   pl.BlockSpec((hb, 1, tq), lambda h, i: (h, 0, i))) + (hbm,) * n_w,
        input_output_aliases={3 + a: 2 + a for a in range(n_w)},
        scratch_shapes=_gather_sems(n_w),
        compiler_params=pltpu.CompilerParams(vmem_limit_bytes=VMEM_LIMIT),
    )(q, k, v, *shard_arrays)


def _attn_bwd(q, k, v, kt, o, do, lse_row, t_lat, parts):
    e_rows = k.shape[1]
    tq = min(t_lat, ATTN_BWD_Q_BLOCK)
    n_p = len(parts)
    n_q = t_lat // tq
    bounds = _key_chunks(e_rows, KEY_CHUNK_BWD)

    hb = ATTN_BWD_HEADS_PER_STEP
    assert hb * V_DIM == HEAD_PAD, "a step's heads share one lane tile of the unpadded o / dO"
    n_hb = N_HEADS // hb

    def body(q_ref, k_ref, v_ref, kt_ref, o_ref, do_ref, lse_ref, *rest):
        dqt_ref, dk_ref, dv_ref = rest[n_p:n_p + 3]
        scatter = _ChipScatter(rest[:n_p], rest[n_p + 3:2 * n_p + 3], rest[2 * n_p + 3], rest[2 * n_p + 4])
        h, i = pl.program_id(0), pl.program_id(1)
        pl.when(jnp.logical_and(h == 0, i == 0))(scatter.start)

        @pl.when(i == 0)
        def _():
            dk_ref[...] = jnp.zeros(dk_ref.shape, F32)
            dv_ref[...] = jnp.zeros(dv_ref.shape, F32)

        do_pair = do_ref[...].astype(F32)
        prod = o_ref[...].astype(F32) * do_pair
        lane = lax.broadcasted_iota(jnp.int32, (tq, HEAD_PAD), 1)
        sel = lax.broadcasted_iota(jnp.int32, (8, HEAD_PAD), 1)
        qs, dos, lses, deltas = [], [], [], []
        for b in range(hb):
            qs.append(q_ref[b])
            mine = do_pair if b == 0 else pltpu.roll(do_pair, V_DIM, 1)
            dos.append(jnp.where(lane < V_DIM, mine, 0.0).astype(BF16))
            lses.append(lse_ref[b])
            ones = jnp.where((sel < V_DIM) == (b == 0), 1.0, 0.0)
            deltas.append(lax.dot_general(ones, prod, (((1,), (1,)), ((), ())), precision=HIGHEST,
                                          preferred_element_type=F32)[0:1, :])
        dqt = [None] * hb
        for lo, n in bounds:
            for b in range(hb):
                pt = jnp.exp2(_dot_nt(k_ref[b, lo:lo + n, :], qs[b]) * (ATTN_SCALE * LOG2E) - lses[b])
                dpt = _dot_nt(v_ref[b, lo:lo + n, :], dos[b])
                dst = (pt * (dpt - deltas[b])).astype(BF16)
                dv_c = _dot(pt.astype(BF16), dos[b])
                dk_c = _dot(dst, qs[b])
                part = _dot(kt_ref[b, :, lo:lo + n], dst)
                dqt[b] = part if dqt[b] is None else dqt[b] + part
                dk_ref[b, lo:lo + n, :] += dk_c * ATTN_SCALE
                dv_ref[b, lo:lo + n, :] += dv_c
        for b in range(hb):
            dqt_ref[b] = (dqt[b] * ATTN_SCALE).T

        pl.when(jnp.logical_and(h == n_hb - 1, i == n_q - 1))(scatter.finish)

    hbm = pl.BlockSpec(memory_space=pl.ANY)
    qspec = pl.BlockSpec((hb, tq, HEAD_PAD), lambda h, i: (h, i, 0))
    kspec = pl.BlockSpec((hb, e_rows, HEAD_PAD), lambda h, i: (h, 0, 0))
    pair = pl.BlockSpec((tq, hb * V_DIM), lambda h, i: (i, h))
    return pl.pallas_call(
        body, name="attn_bwd", grid=(n_hb, n_q),
        out_shape=(jax.ShapeDtypeStruct((N_HEADS, t_lat, HEAD_PAD), F32),
                   jax.ShapeDtypeStruct((N_HEADS, e_rows, HEAD_PAD), F32),
                   jax.ShapeDtypeStruct((N_HEADS, e_rows, HEAD_PAD), F32))
        + tuple(jax.ShapeDtypeStruct((3,) + p.shape[1:], BF16) for p in parts),
        in_specs=[qspec, kspec, kspec, pl.BlockSpec((hb, HEAD_PAD, e_rows), lambda h, i: (h, 0, 0)), pair, pair,
                  pl.BlockSpec((hb, 1, tq), lambda h, i: (h, 0, i))] + [hbm] * n_p,
        out_specs=(qspec, kspec, kspec) + (hbm,) * n_p,
        scratch_shapes=[pltpu.SemaphoreType.DMA((3 * n_p,)), pltpu.SemaphoreType.DMA((3 * n_p,))],
        compiler_params=pltpu.CompilerParams(vmem_limit_bytes=VMEM_LIMIT),
    )(q, k, v, kt, o, do, lse_row, *parts)


def _halo_specs(tm, col_block):
    per = tm // 8
    prev = pl.BlockSpec((8, CONV_W), lambda i: (jnp.maximum(i * per - 1, 0), col_block))
    nxt = pl.BlockSpec((8, CONV_W), lambda i: ((i + 1) * per, col_block))
    return prev, nxt


def _mlp_fwdbwd(o, z, x2, tgt, mod_b, gf, cw, w_out, w1, w2):
    t_lat = x2.shape[0]
    tm = TOK_TILE
    n_lat = t_lat // tm
    fc = MLP_FF_CHUNK
    n_ff = D_FF // fc

    def body(o_ref, gb_ref, gc_ref, xi_ref, gcp_ref, xip_ref, gcn_ref, xin_ref, cw_ref, wout_ref,
             x_ref, t_ref, mod_ref, gf_ref, w1_ref, w2_ref,
             r_ref, da_ref, h2_ref, dy2_ref, dx1_ref, conv_ref, acc_ref, dy1_ref, do_ref, dgb_ref, dyv_ref, ra_ref):
        i = pl.program_id(0)

        @pl.when(i == 0)
        def _():
            acc_ref[...] = jnp.zeros(acc_ref.shape, F32)

        g1, sh2, sc2, g2 = mod_ref[0:1, :], mod_ref[1:2, :], mod_ref[2:3, :], mod_ref[3:4, :]
        u = gc_ref[...] * xi_ref[...]
        u_prev = jnp.where(i > 0, gcp_ref[7:8, :] * xip_ref[7:8, :], 0.0)
        u_next = jnp.where(i < n_lat - 1, gcn_ref[0:1, :] * xin_ref[0:1, :], 0.0)
        um1, up1 = _shift_rows(u, u_prev, u_next)
        yv = cw_ref[0:1, :] * um1 + cw_ref[1:2, :] * u + cw_ref[2:3, :] * up1
        gb = gb_ref[...]
        conv = (gb * yv).astype(BF16)
        conv_ref[...] = conv
        n_attn = N_HEADS * V_DIM
        y1 = _dot(o_ref[...], wout_ref[0:n_attn, :]) + _dot(conv, wout_ref[n_attn:, :])
        x1 = x_ref[...] + g1 * y1
        rstd2 = lax.rsqrt(jnp.mean(x1 * x1, axis=-1, keepdims=True) + EPS)
        xn1 = x1 * rstd2
        h2 = (xn1 * (1.0 + sc2) + sh2).astype(BF16)
        h2_ref[...] = h2
        y2 = jnp.zeros((tm, D_MODEL), F32)
        for jj in range(n_ff):
            lo = jj * fc
            ra = jnp.maximum(_dot(h2, w1_ref[lo // FF_CHUNK, :, lo % FF_CHUNK:lo % FF_CHUNK + fc]), 0.0)
            ra_ref[jj] = ra
            r = (ra * ra).astype(BF16)
            r_ref[:, lo:lo + fc] = r
            y2 = y2 + _dot(r, w2_ref[lo:lo + fc, :])
        x2v = x1 + g2 * y2
        rstd3 = lax.rsqrt(jnp.mean(x2v * x2v, axis=-1, keepdims=True) + EPS)
        xn3 = x2v * rstd3
        gfv = gf_ref[...]
        diff = xn3 * gfv - t_ref[...]
        loss_t = 0.5 * jnp.sum(jnp.sum(diff * diff, axis=-1, keepdims=True), axis=0, keepdims=True) * (1.0 / D_MODEL)
        dy = diff * (1.0 / D_MODEL)
        dxn3 = dy * gfv
        dx2 = rstd3 * (dxn3 - xn3 * jnp.mean(dxn3 * xn3, axis=-1, keepdims=True))
        dy2 = (dx2 * g2).astype(BF16)
        dy2_ref[...] = dy2
        dh2 = jnp.zeros((tm, D_MODEL), F32)
        for jj in range(n_ff):
            lo = jj * fc
            dr = _dot_nt(dy2, w2_ref[lo:lo + fc, :])
            da = (2.0 * ra_ref[jj] * dr).astype(BF16)
            da_ref[:, lo:lo + fc] = da
            dh2 = dh2 + _dot_nt(da, w1_ref[lo // FF_CHUNK, :, lo % FF_CHUNK:lo % FF_CHUNK + fc])
        dxn1 = dh2 * (1.0 + sc2)
        dx1 = dx2 + rstd2 * (dxn1 - xn1 * jnp.mean(dxn1 * xn1, axis=-1, keepdims=True))
        dx1_ref[...] = dx1
        dy1 = (dx1 * g1).astype(BF16)
        dy1_ref[...] = dy1
        do_ref[...] = _dot_nt(dy1, wout_ref[0:n_attn, :]).astype(BF16)
        dconv = _dot_nt(dy1, wout_ref[n_attn:, :])
        dgb_ref[...] = dconv * yv
        dyv_ref[...] = dconv * gb
        acc_ref[5:6, :] += jnp.sum(dx1 * y1, axis=0, keepdims=True)
        acc_ref[0:1, :] += jnp.sum(dy * xn3, axis=0, keepdims=True)
        acc_ref[1:2, :] += jnp.sum(dx2 * y2, axis=0, keepdims=True)
        acc_ref[2:3, :] += jnp.sum(dh2, axis=0, keepdims=True)
        acc_ref[3:4, :] += jnp.sum(dh2 * xn1, axis=0, keepdims=True)
        acc_ref[4:5, :] += jnp.broadcast_to(loss_t, (1, D_MODEL))

    row = lambda i: (i, 0)
    gcp, gcn = _halo_specs(tm, 2)
    xip, xin = _halo_specs(tm, 3)
    tile = pl.BlockSpec((tm, D_MODEL), row)
    wide = pl.BlockSpec((tm, D_FF), row)
    half = pl.BlockSpec((tm, CONV_W), row)
    return pl.pallas_call(
        body, name="mlp_fwdbwd", grid=(n_lat,),
        out_shape=(jax.ShapeDtypeStruct((t_lat, D_FF), BF16), jax.ShapeDtypeStruct((t_lat, D_FF), BF16),
                   jax.ShapeDtypeStruct((t_lat, D_MODEL), BF16), jax.ShapeDtypeStruct((t_lat, D_MODEL), BF16),
                   jax.ShapeDtypeStruct((t_lat, D_MODEL), F32), jax.ShapeDtypeStruct((t_lat, CONV_W), BF16),
                   jax.ShapeDtypeStruct((8, D_MODEL), F32),
                   jax.ShapeDtypeStruct((t_lat, D_MODEL), BF16),
                   jax.ShapeDtypeStruct((t_lat, N_HEADS * V_DIM), BF16),
                   jax.ShapeDtypeStruct((t_lat, CONV_W), F32), jax.ShapeDtypeStruct((t_lat, CONV_W), F32)),
        in_specs=[
            pl.BlockSpec((tm, N_HEADS * V_DIM), row),
            pl.BlockSpec((tm, CONV_W), lambda i: (i, 1)), pl.BlockSpec((tm, CONV_W), lambda i: (i, 2)),
            pl.BlockSpec((tm, CONV_W), lambda i: (i, 3)),
            gcp, xip, gcn, xin,
            _const_spec(cw.shape), _resident_spec(w_out.shape),
            tile, tile, _const_spec(mod_b.shape), _const_spec(gf.shape),
            _resident_spec(w1.shape), _resident_spec(w2.shape),
        ],
        out_specs=(wide, wide, tile, tile, tile, half, _const_spec((8, D_MODEL)),
                   tile, pl.BlockSpec((tm, N_HEADS * V_DIM), row), half, half),
        scratch_shapes=[pltpu.VMEM((n_ff, tm, fc), F32)],
        compiler_params=pltpu.CompilerParams(vmem_limit_bytes=VMEM_LIMIT),
    )(o, z, z, z, z, z, z, z, cw, w_out, x2, tgt, mod_b, gf, w1, w2)


def _inproj_bwd(x2, ctx2, mod_a, z, dyv, dgb, dx1, dqt, dk, dv, cos_t, sin_a, sin_b, w_in, w_uq, w_ukv, qg, kvg, cw):
    t_lat, t_ctx = x2.shape[0], ctx2.shape[0]
    tm = TOK_TILE
    n_lat = t_lat // tm
    n_all = n_lat + t_ctx // tm
    group = max(g for g in (1, 2, 4) if n_lat % g == 0)

    def body(x_ref, ctx_ref, mod_ref, z_ref, gcp_ref, xip_ref, gcn_ref, xin_ref, dyv_ref, dyvp_ref, dyvn_ref,
             dgb_ref, dx1_ref, dqt_ref, dk_ref, dv_ref, cos_ref, sa_ref, sb_ref, win_ref, wuq_ref, wukv_ref,
             qg_ref, kvg_ref, cw_ref, gx_ref, dwin_out, dwuq_out, dwukv_out, acc_ref, h1_buf, dz_buf,
             dwin_ref, dwuq_ref, dwukv_ref):
        i = pl.program_id(0)
        lat = i < n_lat

        @pl.when(i == 0)
        def _():
            dwin_ref[...] = jnp.zeros(dwin_ref.shape, F32)
            dwuq_ref[...] = jnp.zeros(dwuq_ref.shape, F32)
            dwukv_ref[...] = jnp.zeros(dwukv_ref.shape, F32)
            acc_ref[...] = jnp.zeros(acc_ref.shape, F32)

        xin = jnp.where(lat, x_ref[...], ctx_ref[...])
        rstd = lax.rsqrt(jnp.mean(xin * xin, axis=-1, keepdims=True) + EPS)
        xn = xin * rstd
        sc = mod_ref[0, 1:2, :]
        h1 = (xn * (1.0 + sc) + mod_ref[0, 0:1, :]).astype(BF16)
        z = z_ref[...]
        cos, sa, sb = cos_ref[...], sa_ref[...], sb_ref[...]
        qgv, kvgv = qg_ref[...], kvg_ref[...]
        cq = z[:, 0:Q_RANK]
        cqh = cq * lax.rsqrt(jnp.mean(cq * cq, axis=-1, keepdims=True) + EPS)
        rq = lax.rsqrt(jnp.mean(cq * cq, axis=-1, keepdims=True) + EPS)
        cqn = (cqh * qgv).astype(BF16)
        parts = []
        for h in range(N_HEADS):
            g = jnp.where(lat, dqt_ref[h], 0.0)
            parts.append(_unrope(g, cos, sa, sb))
        dq = jnp.concatenate(parts, axis=1).astype(BF16)
        dcqn = _dot_nt(dq, wuq_ref[...])
        dwuq_ref[...] += _dot_tn(cqn, dq)
        acc_ref[4:5, 0:Q_RANK] += jnp.sum(dcqn * cqh, axis=0, keepdims=True)
        dxn = dcqn * qgv
        dcq = rq * (dxn - cqh * jnp.mean(dxn * cqh, axis=-1, keepdims=True))
        ckv = z[:, Q_RANK:Q_RANK + KV_RANK]
        rk = lax.rsqrt(jnp.mean(ckv * ckv, axis=-1, keepdims=True) + EPS)
        ckvh = ckv * rk
        ckvn = (ckvh * kvgv).astype(BF16)
        dks = [dk_ref[h] for h in range(N_HEADS)]
        dkr = dks[0]
        for h in range(1, N_HEADS):
            dkr = dkr + dks[h]
        dkv = jnp.concatenate(dks + [dv_ref[h] for h in range(N_HEADS)], axis=1).astype(BF16)
        dckvn = _dot_nt(dkv, wukv_ref[...])
        dwukv_ref[...] += _dot_tn(ckvn, dkv)
        acc_ref[5:6, 0:KV_RANK] += jnp.sum(dckvn * ckvh, axis=0, keepdims=True)
        dxn = dckvn * kvgv
        dckv = rk * (dxn - ckvh * jnp.mean(dxn * ckvh, axis=-1, keepdims=True))
        dkr = _unrope(dkr, cos, sa, sb)
        gb, gc, xi = z[:, 512:1024], z[:, 1024:1536], z[:, 1536:2048]
        u = gc * xi
        u_prev = jnp.where(i > 0, gcp_ref[7:8, :] * xip_ref[7:8, :], 0.0)
        u_next = jnp.where(i < n_lat - 1, gcn_ref[0:1, :] * xin_ref[0:1, :], 0.0)
        um1, up1 = _shift_rows(u, u_prev, u_next)
        dyv = jnp.where(lat, dyv_ref[...], 0.0)
        dyv_prev = jnp.where(jnp.logical_and(i > 0, lat), dyvp_ref[7:8, :], 0.0)
        dyv_next = jnp.where(i < n_lat - 1, dyvn_ref[0:1, :], 0.0)
        dyv_m1, dyv_p1 = _shift_rows(dyv, dyv_prev, dyv_next)
        du = cw_ref[0:1, :] * dyv_p1 + cw_ref[1:2, :] * dyv + cw_ref[2:3, :] * dyv_m1
        dgc = du * xi
        dxi = du * gc
        dgb = jnp.where(lat, dgb_ref[...], 0.0)
        acc_ref[6:7, 0:CONV_W] += jnp.sum(dyv * um1, axis=0, keepdims=True)
        acc_ref[7:8, 0:CONV_W] += jnp.sum(dyv * u, axis=0, keepdims=True)
        acc_ref[8:9, 0:CONV_W] += jnp.sum(dyv * up1, axis=0, keepdims=True)
        dz = jnp.concatenate([dcq, dckv, dkr, dgb, dgc, dxi], axis=1).astype(BF16)
        dh1 = _dot_nt(dz, win_ref[...])
        slot = i % group
        rows_g = pl.ds(pl.multiple_of(slot * tm, tm), tm)
        h1_buf[rows_g, :] = h1
        dz_buf[rows_g, :] = dz

        @pl.when(jnp.logical_and(lat, slot == group - 1))
        def _():
            dwin_ref[...] += _dot_tn(h1_buf[...], dz_buf[...])

        @pl.when(jnp.logical_not(lat))
        def _():
            dwin_ref[...] += _dot_tn(h1, dz)
        s_sh = jnp.sum(dh1, axis=0, keepdims=True)
        s_sc = jnp.sum(dh1 * xn, axis=0, keepdims=True)
        zero = jnp.zeros_like(s_sh)
        acc_ref[0:1, :] += jnp.where(lat, s_sh, zero)
        acc_ref[1:2, :] += jnp.where(lat, s_sc, zero)
        acc_ref[2:3, :] += jnp.where(lat, zero, s_sh)
        acc_ref[3:4, :] += jnp.where(lat, zero, s_sc)
        dxn = dh1 * (1.0 + sc)
        dx = rstd * (dxn - xn * jnp.mean(dxn * xn, axis=-1, keepdims=True))

        @pl.when(lat)
        def _():
            gx_ref[...] = dx1_ref[...] + dx

        @pl.when(i == n_all - 1)
        def _():
            dwin_out[...] = dwin_ref[...].astype(BF16)
            dwuq_out[...] = dwuq_ref[...].astype(BF16)
            dwukv_out[...] = dwukv_ref[...].astype(BF16)

    last = n_lat - 1
    per = tm // 8
    lat_row = lambda i: (jnp.minimum(i, last), 0)
    row = lambda i: (i, 0)
    gcp, gcn = _halo_specs(tm, 2)
    xip, xin = _halo_specs(tm, 3)
    n_halo = t_lat // 8
    dyvp = pl.BlockSpec((8, CONV_W), lambda i: (jnp.clip(i * per - 1, 0, n_halo - 1), 0))
    dyvn = pl.BlockSpec((8, CONV_W), lambda i: (jnp.minimum((i + 1) * per, n_halo - 1), 0))
    gcn = pl.BlockSpec((8, CONV_W), lambda i: (jnp.minimum((i + 1) * per, (t_lat + t_ctx) // 8 - 1), 2))
    xin = pl.BlockSpec((8, CONV_W), lambda i: (jnp.minimum((i + 1) * per, (t_lat + t_ctx) // 8 - 1), 3))
    head_f32 = pl.BlockSpec((N_HEADS, tm, HEAD_PAD), lambda i: (0, i, 0))
    tab = pl.BlockSpec((tm, HEAD_PAD), row)
    return pl.pallas_call(
        body, name="inproj_bwd", grid=(n_all,),
        out_shape=(jax.ShapeDtypeStruct((t_lat, D_MODEL), F32), jax.ShapeDtypeStruct(w_in.shape, BF16),
                   jax.ShapeDtypeStruct(w_uq.shape, BF16), jax.ShapeDtypeStruct(w_ukv.shape, BF16),
                   jax.ShapeDtypeStruct((16, D_MODEL), F32)),
        in_specs=[
            pl.BlockSpec((tm, D_MODEL), lat_row), _const_spec((tm, D_MODEL)),
            pl.BlockSpec((1, 8, D_MODEL), lambda i: (i // n_lat, 0, 0)),
            pl.BlockSpec((tm, Z_COLS), row), gcp, xip, gcn, xin,
            pl.BlockSpec((tm, CONV_W), lat_row), dyvp, dyvn,
            pl.BlockSpec((tm, CONV_W), lat_row), pl.BlockSpec((tm, D_MODEL), lat_row),
            pl.BlockSpec((N_HEADS, tm, HEAD_PAD), lambda i: (0, jnp.minimum(i, last), 0)),
            head_f32, head_f32, tab, tab, tab,
            _const_spec(w_in.shape), _const_spec(w_uq.shape), _const_spec(w_ukv.shape),
            _const_spec(qg.shape), _const_spec(kvg.shape), _const_spec(cw.shape),
        ],
        out_specs=(pl.BlockSpec((tm, D_MODEL), lat_row), _const_spec(w_in.shape), _const_spec(w_uq.shape),
                   _const_spec(w_ukv.shape), _const_spec((16, D_MODEL))),
        scratch_shapes=[pltpu.VMEM((group * tm, D_MODEL), BF16), pltpu.VMEM((group * tm, Z_COLS), BF16),
                        pltpu.VMEM(w_in.shape, F32), pltpu.VMEM(w_uq.shape, F32), pltpu.VMEM(w_ukv.shape, F32)],
        compiler_params=pltpu.CompilerParams(vmem_limit_bytes=VMEM_LIMIT),
    )(x2, ctx2, mod_a, z, z, z, z, z, dyv, dyv, dyv, dgb, dx1, dqt, dk, dv, cos_t, sin_a, sin_b, w_in, w_uq, w_ukv,
      qg, kvg, cw)


def _wgrad(a, b, name, bm, bn):
    t, m = a.shape
    n = b.shape[1]
    bk = min(t, 4096)
    nk = t // bk
    nj = n // bn

    def body(a_ref, b_ref, o_ref, acc_ref):
        k = pl.program_id(2)
        part = _dot_tn(a_ref[...], b_ref[...])

        @pl.when(k == 0)
        def _():
            acc_ref[...] = part

        @pl.when(k > 0)
        def _():
            acc_ref[...] += part

        @pl.when(k == nk - 1)
        def _():
            o_ref[...] = acc_ref[...].astype(BF16)

    return pl.pallas_call(
        body, name=name, grid=(m // bm, nj, nk), out_shape=jax.ShapeDtypeStruct((m // bm * nj, bm, bn), BF16),
        in_specs=[pl.BlockSpec((bk, bm), lambda i, j, k: (k, i)), pl.BlockSpec((bk, bn), lambda i, j, k: (k, j))],
        out_specs=pl.BlockSpec((None, bm, bn), lambda i, j, k: (i * nj + j, 0, 0)),
        scratch_shapes=[pltpu.VMEM((bm, bn), F32)],
        compiler_params=pltpu.CompilerParams(vmem_limit_bytes=VMEM_LIMIT),
    )(a, b)


def _wgrad_out(o, conv, dy1, sib_arrays):
    t = o.shape[0]
    bk = min(t, 2048)
    nk = t // bk
    n_s = len(sib_arrays)

    def body(o_ref, c_ref, d_ref, *rest):
        w_ref, got_w_ref, acc_ref = rest[n_s], rest[2 * n_s + 1], rest[2 * n_s + 2]
        send = _SiblingSend(rest[:n_s], rest[n_s + 1:2 * n_s + 1], rest[2 * n_s + 3], rest[2 * n_s + 4])
        wsend, wrecv = rest[2 * n_s + 5], rest[2 * n_s + 6]
        k = pl.program_id(0)
        pl.when(k == 0)(send.start)
        part = _dot_tn(jnp.concatenate([o_ref[...], c_ref[...]], axis=1), d_ref[...])

        @pl.when(k == 0)
        def _():
            acc_ref[...] = part

        @pl.when(k > 0)
        def _():
            acc_ref[...] += part

        @pl.when(k == nk - 1)
        def _():
            w_ref[...] = acc_ref[...].astype(BF16)
            x, y, c = _pos()
            own = []
            for s in range(N_SHARD):
                theirs = pl.ds(pl.multiple_of(s * shard_rows + (1 - c) * (shard_rows // 2), 16), shard_rows // 2)
                cp = pltpu.make_async_remote_copy(
                    src_ref=w_ref.at[theirs], dst_ref=got_w_ref.at[s], send_sem=wsend.at[s], recv_sem=wrecv.at[s],
                    device_id=(x, y, 1 - c), device_id_type=MESH)
                cp.start()
                own.append(cp)
            for cp in own:
                cp.wait()

        pl.when(k == nk - 1)(send.finish)

    hbm = pl.BlockSpec(memory_space=pl.ANY)
    shard_rows = D_MODEL // N_SHARD
    return pl.pallas_call(
        body, name="wgrad_out", grid=(nk,),
        out_shape=(jax.ShapeDtypeStruct((D_MODEL, D_MODEL), BF16),)
        + tuple(jax.ShapeDtypeStruct((N_SHARD, a.shape[1] // 2, a.shape[2]), BF16) for a in sib_arrays)
        + (jax.ShapeDtypeStruct((N_SHARD, shard_rows // 2, D_MODEL), BF16),),
        in_specs=[pl.BlockSpec((bk, N_HEADS * V_DIM), lambda k: (k, 0)),
                  pl.BlockSpec((bk, CONV_W), lambda k: (k, 0)),
                  pl.BlockSpec((bk, D_MODEL), lambda k: (k, 0))] + [hbm] * n_s,
        out_specs=(_const_spec((D_MODEL, D_MODEL)),) + (hbm,) * (n_s + 1),
        scratch_shapes=[pltpu.VMEM((D_MODEL, D_MODEL), F32), pltpu.SemaphoreType.DMA((n_s,)),
                        pltpu.SemaphoreType.DMA((n_s,)), pltpu.SemaphoreType.DMA((N_SHARD,)),
                        pltpu.SemaphoreType.DMA((N_SHARD,))],
        compiler_params=pltpu.CompilerParams(vmem_limit_bytes=VMEM_LIMIT),
    )(o, conv, dy1, *sib_arrays)


def _adamw_call(w, g, m, v, name):
    rows, cols = w.shape
    rb = 256 if rows % 256 == 0 else rows

    def body(w_ref, g_ref, m_ref, v_ref, d_ref, nm_ref, nv_ref):
        d_, m_, v_ = _adamw(w_ref[...], g_ref[...], m_ref[...], v_ref[...])
        d_ref[...] = d_
        nm_ref[...] = m_
        nv_ref[...] = v_

    spec = pl.BlockSpec((rb, cols), lambda i: (i, 0))
    shp = jax.ShapeDtypeStruct((rows, cols), F32)
    return pl.pallas_call(
        body, name=name, grid=(rows // rb,), out_shape=(shp, shp, shp),
        in_specs=[spec] * 4, out_specs=(spec, spec, spec),
    )(w, g, m, v)


def _adamw_halves(w, g_mine, g_theirs, m, v, c_idx, name):
    rows, cols = w.shape
    half = rows // 2
    rb = min(256, half)
    nb = half // rb

    def body(c_ref, w_ref, gm_ref, gt_ref, m_ref, v_ref, g_ref, d_ref, nm_ref, nv_ref):
        mine = pl.program_id(0) // nb == c_ref[0]
        g = jnp.where(mine, gm_ref[...], gt_ref[...])
        d_, m_, v_ = _adamw(w_ref[...], g, m_ref[...], v_ref[...])
        g_ref[...] = g
        d_ref[...] = d_
        nm_ref[...] = m_
        nv_ref[...] = v_

    spec = pl.BlockSpec((rb, cols), lambda i, c_ref: (i, 0))
    hspec = pl.BlockSpec((rb, cols), lambda i, c_ref: (i % nb, 0))
    shp = jax.ShapeDtypeStruct((rows, cols), F32)
    grid_spec = pltpu.PrefetchScalarGridSpec(
        num_scalar_prefetch=1, grid=(rows // rb,), in_specs=[spec, hspec, hspec, spec, spec],
        out_specs=(spec, spec, spec, spec))
    return pl.pallas_call(
        body, name=name, grid_spec=grid_spec, out_shape=(shp, shp, shp, shp),
    )(c_idx, w, g_mine, g_theirs, m, v)


def _wmod_update(s_t, dm, w, m, v):
    rows, cols = w.shape
    cb = 512

    def body(s_ref, dm_ref, w_ref, m_ref, v_ref, g_ref, d_ref, nm_ref, nv_ref):
        g = jnp.dot(s_ref[...], dm_ref[...], precision=HIGHEST, preferred_element_type=F32)
        d_, m_, v_ = _adamw(w_ref[...], g, m_ref[...], v_ref[...])
        g_ref[...] = g
        d_ref[...] = d_
        nm_ref[...] = m_
        nv_ref[...] = v_

    spec = pl.BlockSpec((rows, cb), lambda i: (0, i))
    shp = jax.ShapeDtypeStruct((rows, cols), F32)
    return pl.pallas_call(
        body, name="wmod_update", grid=(cols // cb,), out_shape=(shp, shp, shp, shp),
        in_specs=[_const_spec(s_t.shape), pl.BlockSpec((16, cb), lambda i: (0, i)), spec, spec, spec],
        out_specs=(spec, spec, spec, spec),
        compiler_params=pltpu.CompilerParams(vmem_limit_bytes=VMEM_LIMIT),
    )(s_t, dm, w, m, v)


def _rope_tables(t_lat, t_ctx):
    t = jnp.arange(t_lat)
    pos = jnp.stack([(t // GRID_W).astype(F32), (t % GRID_W).astype(F32)], axis=1)
    half = QK_ROPE // 4
    freqs = ROPE_THETA ** (-jnp.arange(0, 2 * half, 2, dtype=F32) / (2 * half))
    ang = pos[:, :, None] * freqs[None, None, :]
    cos, sin = jnp.cos(ang), jnp.sin(ang)
    zero = jnp.zeros_like(sin)
    cos32 = jnp.concatenate([cos, cos], axis=2).reshape(t_lat, QK_ROPE)
    sa32 = jnp.concatenate([zero, sin], axis=2).reshape(t_lat, QK_ROPE)
    sb32 = jnp.concatenate([-sin, zero], axis=2).reshape(t_lat, QK_ROPE)

    def widen(tab, fill):
        left = jnp.full((t_lat, ROPE_LANE0), fill, F32)
        right = jnp.full((t_lat, HEAD_PAD - ROPE_LANE0 - QK_ROPE), fill, F32)
        lat = jnp.concatenate([left, tab, right], axis=1)
        return jnp.concatenate([lat, jnp.full((t_ctx, HEAD_PAD), fill, F32)], axis=0)

    return widen(cos32, 1.0), widen(sa32, 0.0), widen(sb32, 0.0)


def _cols_from_shards(s):
    return jnp.transpose(s, (1, 0, 2)).reshape(s.shape[1], -1)


def _cols_to_shards(w):
    k, n = w.shape
    return jnp.transpose(w.reshape(k, N_SHARD, n // N_SHARD), (1, 0, 2))


def kernel(x, c, ctx, c_ctx, w_mod, b_mod, w_in, q_norm_g, w_uq, kv_norm_g, w_ukv, conv_w, w_out, w_mlp1, w_mlp2, final_norm_g, loss_target, m_c_ctx, m_w_mod, m_b_mod, m_w_in, m_q_norm_g, m_w_uq, m_kv_norm_g, m_w_ukv, m_conv_w, m_w_out, m_w_mlp1, m_w_mlp2, m_final_norm_g, v_c_ctx, v_w_mod, v_b_mod, v_w_in, v_q_norm_g, v_w_uq, v_kv_norm_g, v_w_ukv, v_conv_w, v_w_out, v_w_mlp1, v_w_mlp2, v_final_norm_g):
    t_lat, t_ctx = x.shape[1], ctx.shape[1]
    assert t_ctx == TOK_TILE and t_lat % TOK_TILE == 0 and t_lat % GRID_W == 0
    mx, my, mc = _pos()
    j = 2 * mx + my
    ncol = w_mod.shape[2]
    x2, ctx2, tgt = x[0], ctx[0], loss_target[0]
    cctx_row = c_ctx.reshape(1, D_MODEL)

    b_sh = lax.dynamic_slice(b_mod, (0, j * ncol), (1, ncol))
    cw_pad = jnp.zeros((8, 128), F32).at[0:3, :].set(conv_w[0])
    c8, m_all, g_in, g_uq, g_ukv, g_out, g_m1, g_m2 = _prologue(
        c, cctx_row, w_mod[0], b_sh, cw_pad, (w_in[0], w_uq[0], w_ukv[0], w_out[0], w_mlp1[0], w_mlp2[0]), 3)
    mvec = m_all[:, 0, :].reshape(6, D_MODEL)
    mctx = m_all[:, 8, :].reshape(6, D_MODEL)
    zeros6 = jnp.zeros((6, D_MODEL), F32)
    mod_a = jnp.stack([jnp.concatenate([mvec[0:2], zeros6], axis=0), jnp.concatenate([mctx[0:2], zeros6], axis=0)])
    mod_b = jnp.concatenate([mvec[2:6], jnp.zeros((4, D_MODEL), F32)], axis=0)
    cw_full = jnp.pad(jnp.transpose(m_all[:, 9:12, 0:128], (1, 0, 2)).reshape(3, CONV_W), ((0, 5), (0, 0)))

    w_in_f = _cols_from_shards(g_in)
    zc = lambda n: jnp.zeros((D_MODEL, n), BF16)
    w_in_p = jnp.concatenate([w_in_f[:, 0:384], zc(64), w_in_f[:, 384:416], zc(32), w_in_f[:, 416:]], axis=1)
    w_uq_f = _cols_from_shards(g_uq).reshape(Q_RANK, N_HEADS, QK_DIM)
    w_uq_p = jnp.pad(w_uq_f, ((0, 0), (0, 0), (0, HEAD_PAD - QK_DIM))).reshape(Q_RANK, N_HEADS * HEAD_PAD)
    w_ukv_f = _cols_from_shards(g_ukv).reshape(KV_RANK, N_HEADS, QK_NOPE + V_DIM)
    padh = lambda a: jnp.pad(a, ((0, 0), (0, 0), (0, HEAD_PAD - a.shape[2]))).reshape(KV_RANK, N_HEADS * HEAD_PAD)
    w_ukv_p = jnp.concatenate([padh(w_ukv_f[:, :, :QK_NOPE]), padh(w_ukv_f[:, :, QK_NOPE:])], axis=1)
    cos_t, sin_a, sin_b = _rope_tables(t_lat, t_ctx)
    gf_row = final_norm_g.reshape(1, D_MODEL)
    c_idx = mc.reshape(1).astype(jnp.int32)
    j_idx = j.reshape(1).astype(jnp.int32)

    z, q, k, v, kt = _inproj_fwd(x2, ctx2, mod_a, w_in_p, q_norm_g, kv_norm_g, w_uq_p, w_ukv_p, cos_t, sin_a, sin_b)
    o, lse, g_out, w1, g_m2 = _attn_fwd(q, k, v, t_lat, (g_out, g_m1, g_m2))
    w_out_f = g_out.reshape(D_MODEL, D_MODEL)
    w2 = g_m2.reshape(D_FF, D_MODEL)
    r, da, h2, dy2, dx1, conv, acc_mlp, dy1, do, dgb, dyv = _mlp_fwdbwd(o, z, x2, tgt, mod_b, gf_row, cw_full, w_out_f,
                                                                         w1, w2)
    d_w1 = _wgrad(h2, da, "wgrad_mlp1", D_MODEL, FF_CHUNK)
    d_w2 = _wgrad(r, dy2, "wgrad_mlp2", FF_CHUNK, D_MODEL)
    d_wout, *big_got = _wgrad_out(o, conv, dy1, (d_w1, d_w2))
    d_wout = d_wout.reshape(N_SHARD, D_MODEL // N_SHARD, D_MODEL)
    big_grads = (d_w1, d_w2, d_wout)
    big_parts = _add_pairs(big_grads, big_got, c_idx, "rs_add_pairs_big")
    dqt, dk, dv, *big_recv = _attn_bwd(q, k, v, kt, o, do, lse, t_lat, big_parts)
    big_halves = _add_chips(big_parts, big_recv, j_idx, "rs_add_chips_big")
    gx, d_win, d_wuq, d_wukv, acc_in = _inproj_bwd(x2, ctx2, mod_a, z, dyv, dgb, dx1, dqt, dk, dv, cos_t, sin_a, sin_b,
                                                   w_in_p, w_uq_p, w_ukv_p, q_norm_g, kv_norm_g, cw_full)

    d_win_f = jnp.concatenate([d_win[:, 0:384], d_win[:, 448:480], d_win[:, 512:]], axis=1)
    d_wuq_f = d_wuq.reshape(Q_RANK, N_HEADS, HEAD_PAD)[:, :, 0:QK_DIM].reshape(Q_RANK, N_HEADS * QK_DIM)
    d_wukv3 = d_wukv.reshape(KV_RANK, 2, N_HEADS, HEAD_PAD)
    d_wukv_f = jnp.concatenate([d_wukv3[:, 0, :, 0:QK_NOPE], d_wukv3[:, 1, :, 0:V_DIM]], axis=2).reshape(KV_RANK, -1)
    rest = tuple(_cols_to_shards(a).astype(BF16) for a in (d_win_f, d_wuq_f, d_wukv_f))
    rest_got = _rs_sibling(rest, "rs_sibling_rest")
    rest_parts = _add_pairs(rest, rest_got, c_idx, "rs_add_pairs_rest")

    sv = jnp.concatenate([
        acc_in[0:2], acc_mlp[5:6], acc_mlp[2:4], acc_mlp[1:2],
        acc_in[2:4], acc_mlp[0:1], acc_in[4:5], acc_in[5:6], acc_in[6:9], acc_mlp[4:5],
        jnp.zeros((1, D_MODEL), F32)], axis=0)
    all_sv, red, o_cc, o_b, o_q, o_k, o_gf, *exchanged = _small_exchange(
        sv, w_mod[0], cctx_row, m_c_ctx.reshape(1, D_MODEL), v_c_ctx.reshape(1, D_MODEL),
        b_mod.reshape(6, D_MODEL), m_b_mod.reshape(6, D_MODEL), v_b_mod.reshape(6, D_MODEL),
        q_norm_g, m_q_norm_g, v_q_norm_g, kv_norm_g, m_kv_norm_g, v_kv_norm_g,
        gf_row, m_final_norm_g.reshape(1, D_MODEL), v_final_norm_g.reshape(1, D_MODEL), big_halves, rest_parts)
    big_theirs, rest_recv = exchanged[:len(big_halves)], exchanged[len(big_halves):]
    loss = red[14, 0]

    c9 = jnp.concatenate([c8[0::8], jnp.zeros((7, D_MODEL), F32)], axis=0)
    s_t = jnp.transpose(c9 * jax.nn.sigmoid(c9))
    dm_ex = all_sv[:, 0:6, :].reshape(8, 6 * D_MODEL)
    dm_ctx = jnp.concatenate([red[6:8].reshape(1, 2 * D_MODEL), jnp.zeros((1, 4 * D_MODEL), F32)], axis=1)
    dm16 = jnp.concatenate([dm_ex, dm_ctx, jnp.zeros((7, 6 * D_MODEL), F32)], axis=0)
    dm_sh = lax.dynamic_slice(dm16, (0, j * ncol), (16, ncol))
    g_wmod, d_wmod, nm_wmod, nv_wmod = _wmod_update(s_t, dm_sh, w_mod[0], m_w_mod[0], v_w_mod[0])

    g_cw = lax.dynamic_slice(red[11:14, 0:CONV_W], (0, j * 128), (3, 128))
    d_cw, nm_cw, nv_cw = _adamw_call(conv_w[0], g_cw, m_conv_w[0], v_conv_w[0], "adamw_conv")

    rest_halves = _add_chips(rest_parts, rest_recv, j_idx, "rs_add_chips_rest")
    g_win, g_wuq, g_wukv = _rs_join(rest_halves, "rs_join_rest")
    upd = {}
    for name, w_, g_, m_, v_ in (("in", w_in, g_win, m_w_in, v_w_in), ("uq", w_uq, g_wuq, m_w_uq, v_w_uq),
                                 ("ukv", w_ukv, g_wukv, m_w_ukv, v_w_ukv)):
        upd[name] = _adamw_call(w_[0], g_, m_[0], v_[0], "adamw_" + name)
    g_w1, *upd["mlp1"] = _adamw_halves(w_mlp1[0], big_halves[0], big_theirs[0], m_w_mlp1[0], v_w_mlp1[0], c_idx,
                                       "adamw_mlp1")
    g_w2, *upd["mlp2"] = _adamw_halves(w_mlp2[0], big_halves[1], big_theirs[1], m_w_mlp2[0], v_w_mlp2[0], c_idx,
                                       "adamw_mlp2")
    g_wout, *upd["out"] = _adamw_halves(w_out[0], big_halves[2], big_theirs[2], m_w_out[0], v_w_out[0], c_idx,
                                        "adamw_out")

    def four(o4, shape):
        return [o4[r].reshape(shape) for r in range(4)]

    cc4 = four(o_cc, (D_MODEL,))
    b4 = [o_b[r].reshape(1, 6 * D_MODEL) for r in range(4)]
    q4 = four(o_q, (1, Q_RANK))
    k4 = four(o_k, (1, KV_RANK))
    gf4 = four(o_gf, (D_MODEL,))
    big = {"in": g_win, "uq": g_wuq, "ukv": g_wukv, "out": g_wout, "mlp1": g_w1, "mlp2": g_w2}

    def leaf(idx):
        wm = (g_wmod, d_wmod, nm_wmod, nv_wmod)[idx]
        cwv = (g_cw, d_cw, nm_cw, nv_cw)[idx]
        bigv = {n: (big[n] if idx == 0 else upd[n][idx - 1]) for n in big}
        return [cc4[idx], wm[None], b4[idx], bigv["in"][None], q4[idx], bigv["uq"][None], k4[idx], bigv["ukv"][None],
                cwv[None], bigv["out"][None], bigv["mlp1"][None], bigv["mlp2"][None], gf4[idx]]

    return (loss, gx[None], *leaf(0), *leaf(1), *leaf(2), *leaf(3))
```

```python
import functools
import math

import jax
import jax.numpy as jnp
from jax import lax
from jax.experimental import pallas as pl
from jax.experimental.pallas import tpu as pltpu

F32 = jnp.float32
BF16 = jnp.bfloat16
MESH = pl.DeviceIdType.MESH
HIGHEST = lax.Precision.HIGHEST

D_MODEL = 1024
N_HEADS = 8
QK_NOPE = 64
QK_ROPE = 32
QK_DIM = QK_NOPE + QK_ROPE
V_DIM = 64
Q_RANK = 256
KV_RANK = 128
CONV_W = 512
D_FF = 4096
GRID_W = 64
ROPE_THETA = 10000.0
EPS = 1e-6
ATTN_SCALE = 1.0 / math.sqrt(QK_DIM)
HEAD_PAD = 128
Z_COLS = 2048
ROPE_LANE0 = QK_NOPE
N_SHARD = 4
TOK_TILE = 256
FF_CHUNK = 1024
MLP_FF_CHUNK = 1024
KEY_CHUNK = 512
ATTN_FWD_Q_BLOCK = 1024
ATTN_HEADS_PER_STEP = 4
ATTN_BWD_HEADS_PER_STEP = 2
ATTN_BWD_Q_BLOCK = 512
KEY_CHUNK_BWD = 512

ADAM_LR = 0.001
ADAM_B1 = 0.9
ADAM_B2 = 0.999
ADAM_EPS = 1e-08
ADAM_WD = 0.01
ADAM_STEP = 10

LOG2E = 1.4426950408889634

VMEM_LIMIT = 56 * 1024 * 1024
STAGE_VMEM_LIMIT = 32 * 1024 * 1024


def _pos():
    return lax.axis_index("x"), lax.axis_index("y"), lax.axis_index("c")


def _dot(a, b):
    return jnp.dot(a, b, preferred_element_type=F32)


def _dot_nt(a, b):
    return lax.dot_general(a, b, (((1,), (1,)), ((), ())), preferred_element_type=F32)


def _dot_tn(a, b):
    return lax.dot_general(a, b, (((0,), (0,)), ((), ())), preferred_element_type=F32)


def _rope(v, cos, sa, sb):
    return v * cos + pltpu.roll(v, 8, 1) * sa + pltpu.roll(v, HEAD_PAD - 8, 1) * sb


def _unrope(g, cos, sa, sb):
    return g * cos + pltpu.roll(g * sa, HEAD_PAD - 8, 1) + pltpu.roll(g * sb, 8, 1)


def _sigmoid(v):
    return 1.0 / (1.0 + jnp.exp(-v))


def _adamw(w, g, m, v):
    m = ADAM_B1 * m + (1.0 - ADAM_B1) * g
    v = ADAM_B2 * v + (1.0 - ADAM_B2) * (g * g)
    m_hat = m / (1.0 - ADAM_B1 ** ADAM_STEP)
    v_hat = v / (1.0 - ADAM_B2 ** ADAM_STEP)
    delta = -ADAM_LR * (m_hat / (jnp.sqrt(v_hat) + ADAM_EPS) + ADAM_WD * w)
    return delta, m, v


def _shift_rows(u, prev_row, next_row):
    n = u.shape[0]
    rows = lax.broadcasted_iota(jnp.int32, u.shape, 0)
    um1 = jnp.where(rows == 0, prev_row, pltpu.roll(u, 1, 0))
    up1 = jnp.where(rows == n - 1, next_row, pltpu.roll(u, n - 1, 0))
    return um1, up1


def _const_spec(shape):
    nd = len(shape)
    return pl.BlockSpec(shape, lambda *_: (0,) * nd)


def _resident_spec(shape):
    nd = len(shape)
    return pl.BlockSpec(shape, lambda *_: (0,) * nd, pipeline_mode=pl.Buffered(1))


def _peer(r, x, y, c):
    px = 1 - x if r & 4 else x
    py = 1 - y if r & 2 else y
    pc = 1 - c if r & 1 else c
    return (px, py, pc)


def _prologue(c_row, cctx_row, w_mod_sh, b_sh, cw_sh, srcs, n_gather):
    ncol = w_mod_sh.shape[1]
    n = len(srcs)
    n_split = 4

    def body(c_ref, cctx_ref, w_ref, b_ref, cw_ref, *refs):
        ins, (c8_ref, m_ref), outs = refs[:n], refs[n:n + 2], refs[n + 2:2 * n + 2]
        mine_ref, msh_ref = refs[2 * n + 2:2 * n + 4]
        f32s, bfs = refs[2 * n + 4:3 * n + 4], refs[3 * n + 4:4 * n + 4]
        ssem, rsem, ssem2, rsem2, lsem_in, lsem_out = refs[4 * n + 4:4 * n + 10]
        x, y, c = _pos()
        me = 4 * x + 2 * y + c
        j = 2 * x + y

        def pieces(rows):
            step = rows // n_split
            return [pl.ds(q * step, step) for q in range(n_split)]

        for t in range(n):
            for sl in pieces(ins[t].shape[0]):
                pltpu.make_async_copy(ins[t].at[sl], f32s[t].at[sl], lsem_in.at[t]).start()
        mine_ref[...] = jnp.zeros(mine_ref.shape, F32)
        mine_ref[0:1, :] = c_ref[...]
        my_rows = pl.ds(pl.multiple_of(8 * me, 8), 8)
        sends = []
        for r in range(1, 8):
            cp = pltpu.make_async_remote_copy(
                src_ref=mine_ref, dst_ref=c8_ref.at[my_rows], send_sem=ssem.at[r - 1], recv_sem=rsem.at[r - 1],
                device_id=_peer(r, x, y, c), device_id_type=MESH)
            cp.start()
            sends.append(cp)

        def cast_and_store(t):
            pltpu.make_async_copy(ins[t], f32s[t], lsem_in.at[t]).wait()
            bfs[t][...] = f32s[t][...].astype(BF16)
            for sl in pieces(ins[t].shape[0]):
                pltpu.make_async_copy(bfs[t].at[sl], outs[t].at[j, sl], lsem_out.at[t]).start()

        gather = _ShardGather(outs[:n_gather], *refs[4 * n + 10:])
        for t in range(n_gather):
            cast_and_store(t)
        for t in range(n_gather):
            pltpu.make_async_copy(bfs[t], outs[t].at[j], lsem_out.at[t]).wait()
        gather.start()
        for cp in sends:
            cp.wait()
        c8_ref[my_rows, :] = mine_ref[...]
        c8_ref[64:72, :] = jnp.zeros((8, D_MODEL), F32)
        c8_ref[64:65, :] = cctx_ref[...]
        cv = c8_ref[...]
        s = cv * _sigmoid(cv)
        m = jnp.dot(s, w_ref[...], precision=HIGHEST, preferred_element_type=F32) + b_ref[...]
        msh_ref[0:64, :] = m[0:64, :]
        msh_ref[64:72, :] = jnp.zeros((8, ncol), F32)
        msh_ref[64:65, :] = m[64:65, :]
        msh_ref[65:68, 0:128] = cw_ref[0:3, :]
        m_ref[j, 0:8, :] = msh_ref[my_rows, :]
        m_ref[j, 8:16, :] = msh_ref[64:72, :]
        sends2 = []
        for k, (px, py) in enumerate(_chips(x, y)):
            theirs = pl.ds(pl.multiple_of(8 * (4 * px + 2 * py + c), 8), 8)
            for half, src in enumerate((msh_ref.at[theirs], msh_ref.at[64:72])):
                cp = pltpu.make_async_remote_copy(
                    src_ref=src, dst_ref=m_ref.at[j, 8 * half:8 * half + 8], send_sem=ssem2.at[2 * k + half],
                    recv_sem=rsem2.at[2 * k + half], device_id=(px, py, c), device_id_type=MESH)
                cp.start()
                sends2.append(cp)
        for t in range(n_gather, n):
            cast_and_store(t)
        gather.forward()
        gather.finish()
        for t in range(n_gather, n):
            pltpu.make_async_copy(bfs[t], outs[t].at[j], lsem_out.at[t]).wait()
        for cp in sends2:
            cp.wait()

    vm = pl.BlockSpec(memory_space=pltpu.VMEM)
    hbm = pl.BlockSpec(memory_space=pl.ANY)
    return pl.pallas_call(
        body, name="prologue",
        out_shape=(jax.ShapeDtypeStruct((72, D_MODEL), F32), jax.ShapeDtypeStruct((N_SHARD, 16, ncol), F32))
        + tuple(jax.ShapeDtypeStruct((N_SHARD,) + a.shape, BF16) for a in srcs),
        in_specs=[vm] * 5 + [hbm] * n, out_specs=(vm, vm) + (hbm,) * n,
        scratch_shapes=[pltpu.VMEM((8, D_MODEL), F32), pltpu.VMEM((72, ncol), F32)]
        + [pltpu.VMEM(a.shape, F32) for a in srcs] + [pltpu.VMEM(a.shape, BF16) for a in srcs]
        + [pltpu.SemaphoreType.DMA((7,)), pltpu.SemaphoreType.DMA((7,)),
           pltpu.SemaphoreType.DMA((6,)), pltpu.SemaphoreType.DMA((6,)),
           pltpu.SemaphoreType.DMA((n,)), pltpu.SemaphoreType.DMA((n,))] + _gather_sems(n_gather),
        compiler_params=pltpu.CompilerParams(vmem_limit_bytes=VMEM_LIMIT),
    )(c_row, cctx_row, w_mod_sh, b_sh, cw_sh, *srcs)


def _chips(x, y):
    return [(1 - x, y), (x, 1 - y), (1 - x, 1 - y)]


def _halves(ref, c, align):
    hr = ref.shape[-2] // 2
    return (pl.ds(pl.multiple_of(c * hr, align), hr), pl.ds(pl.multiple_of((1 - c) * hr, align), hr))


class _ShardGather:
    def __init__(self, refs, ssem, rsem, fsend, frecv):
        self.refs, self.sems = refs, (ssem, rsem, fsend, frecv)
        self.x, self.y, self.c = _pos()
        self.j = 2 * self.x + self.y

    def _ici(self, a, k, slot):
        g = self.refs[a]
        ssem, rsem, _, _ = self.sems
        mine, _ = _halves(g, self.c, 16)
        px, py = _chips(self.x, self.y)[k]
        return pltpu.make_async_remote_copy(
            src_ref=g.at[self.j, mine], dst_ref=g.at[slot, mine], send_sem=ssem.at[3 * a + k],
            recv_sem=rsem.at[3 * a + k], device_id=(px, py, self.c), device_id_type=MESH)

    def _d2d(self, a, k, to_other_half):
        g = self.refs[a]
        _, _, fsend, frecv = self.sems
        mine, theirs = _halves(g, self.c, 16)
        px, py = _chips(self.x, self.y)[k]
        jk = 2 * px + py
        return pltpu.make_async_remote_copy(
            src_ref=g.at[jk, mine], dst_ref=g.at[jk, theirs if to_other_half else mine],
            send_sem=fsend.at[3 * a + k], recv_sem=frecv.at[3 * a + k],
            device_id=(self.x, self.y, 1 - self.c), device_id_type=MESH)

    def start(self):
        for a in range(len(self.refs)):
            for k in range(3):
                self._ici(a, k, self.j).start()

    def forward(self):
        for a in range(len(self.refs)):
            for k, (px, py) in enumerate(_chips(self.x, self.y)):
                self._ici(a, k, 2 * px + py).wait_recv()
                self._d2d(a, k, False).start()

    def finish(self):
        for a in range(len(self.refs)):
            for k in range(3):
                self._d2d(a, k, True).wait()
                self._ici(a, k, self.j).wait_send()


def _gather_sems(n_arrays):
    return [pltpu.SemaphoreType.DMA((3 * n_arrays,)) for _ in range(4)]


class _SiblingSend:
    def __init__(self, g_refs, got_refs, ssem, rsem):
        self.g_refs, self.got_refs, self.ssem, self.rsem = g_refs, got_refs, ssem, rsem
        self.x, self.y, self.c = _pos()

    def _copy(self, a, shard):
        _, theirs = _halves(self.g_refs[a], self.c, 16)
        src = self.g_refs[a].at[:, theirs] if shard is None else self.g_refs[a].at[shard, theirs]
        dst = self.got_refs[a] if shard is None else self.got_refs[a].at[shard]
        return pltpu.make_async_remote_copy(
            src_ref=src, dst_ref=dst, send_sem=self.ssem.at[a], recv_sem=self.rsem.at[a],
            device_id=(self.x, self.y, 1 - self.c), device_id_type=MESH)

    def start(self):
        for a in range(len(self.g_refs)):
            for s in range(N_SHARD):
                self._copy(a, s).start()

    def finish(self):
        for a in range(len(self.g_refs)):
            self._copy(a, None).wait()


class _SiblingSwap:
    def __init__(self, h_refs, t_refs, ssem, rsem):
        self.h_refs, self.t_refs, self.ssem, self.rsem = h_refs, t_refs, ssem, rsem
        self.x, self.y, self.c = _pos()

    def _copy(self, a):
        return pltpu.make_async_remote_copy(
            src_ref=self.h_refs[a], dst_ref=self.t_refs[a], send_sem=self.ssem.at[a], recv_sem=self.rsem.at[a],
            device_id=(self.x, self.y, 1 - self.c), device_id_type=MESH)

    def start(self):
        for a in range(len(self.h_refs)):
            self._copy(a).start()

    def finish(self):
        for a in range(len(self.h_refs)):
            self._copy(a).wait()


def _rs_sibling(arrs, name):
    n = len(arrs)

    def body(*refs):
        send = _SiblingSend(refs[:n], refs[n:2 * n], refs[2 * n], refs[2 * n + 1])
        send.start()
        send.finish()

    hbm = pl.BlockSpec(memory_space=pl.ANY)
    return pl.pallas_call(
        body, name=name,
        out_shape=tuple(jax.ShapeDtypeStruct((N_SHARD, a.shape[1] // 2, a.shape[2]), BF16) for a in arrs),
        in_specs=[hbm] * n, out_specs=(hbm,) * n,
        scratch_shapes=[pltpu.SemaphoreType.DMA((n,)), pltpu.SemaphoreType.DMA((n,))],
    )(*arrs)


class _ChipScatter:
    def __init__(self, parts, gots, ssem, rsem):
        self.parts, self.gots, self.ssem, self.rsem = parts, gots, ssem, rsem
        self.x, self.y, self.c = _pos()

    def _copy(self, a, k):
        px, py = _chips(self.x, self.y)[k]
        return pltpu.make_async_remote_copy(
            src_ref=self.parts[a].at[2 * px + py], dst_ref=self.gots[a].at[k], send_sem=self.ssem.at[3 * a + k],
            recv_sem=self.rsem.at[3 * a + k], device_id=(px, py, self.c), device_id_type=MESH)

    def start(self):
        for a in range(len(self.parts)):
            for k in range(3):
                self._copy(a, k).start()

    def finish(self):
        for a in range(len(self.parts)):
            for k in range(3):
                self._copy(a, k).wait()


def _rs_join(halves, name):
    n = len(halves)

    def body(*refs):
        h_refs, f_refs, stages = refs[:n], refs[n:2 * n], refs[2 * n:3 * n]
        lsem_in, lsem_out, ssem, rsem = refs[3 * n:]
        x, y, c = _pos()
        remote = []
        for a in range(n):
            mine, _ = _halves(f_refs[a], c, 8)
            cp = pltpu.make_async_remote_copy(
                src_ref=h_refs[a], dst_ref=f_refs[a].at[mine], send_sem=ssem.at[a], recv_sem=rsem.at[a],
                device_id=(x, y, 1 - c), device_id_type=MESH)
            cp.start()
            remote.append(cp)
            pltpu.make_async_copy(h_refs[a], stages[a], lsem_in.at[a]).start()
        local = []
        for a in range(n):
            mine, _ = _halves(f_refs[a], c, 8)
            pltpu.make_async_copy(h_refs[a], stages[a], lsem_in.at[a]).wait()
            cp = pltpu.make_async_copy(stages[a], f_refs[a].at[mine], lsem_out.at[a])
            cp.start()
            local.append(cp)
        for cp in remote + local:
            cp.wait()

    hbm = pl.BlockSpec(memory_space=pl.ANY)
    return pl.pallas_call(
        body, name=name,
        out_shape=tuple(jax.ShapeDtypeStruct((2 * h.shape[0], h.shape[1]), F32) for h in halves),
        in_specs=[hbm] * n, out_specs=(hbm,) * n,
        scratch_shapes=[pltpu.VMEM(h.shape, F32) for h in halves]
        + [pltpu.SemaphoreType.DMA((n,)) for _ in range(4)],
        compiler_params=pltpu.CompilerParams(vmem_limit_bytes=STAGE_VMEM_LIMIT),
    )(*halves)


def _add_pairs(arrs, gots, c_idx, name):
    n = len(arrs)

    def body(c_ref, *refs):
        for a in range(n):
            refs[2 * n + a][...] = (refs[a][...].astype(F32) + refs[n + a][...].astype(F32)).astype(BF16)

    def half_spec(g, mine):
        hr, cols = g.shape[1], g.shape[2]
        if mine:
            return pl.BlockSpec((1, hr, cols), lambda s, c_ref: (s, c_ref[0], 0))
        return pl.BlockSpec((1, hr, cols), lambda s, c_ref: (s, 0, 0))

    grid_spec = pltpu.PrefetchScalarGridSpec(
        num_scalar_prefetch=1, grid=(N_SHARD,),
        in_specs=[half_spec(g, True) for g in gots] + [half_spec(g, False) for g in gots],
        out_specs=tuple(half_spec(g, False) for g in gots))
    return pl.pallas_call(
        body, name=name, grid_spec=grid_spec, out_shape=tuple(jax.ShapeDtypeStruct(g.shape, BF16) for g in gots),
        compiler_params=pltpu.CompilerParams(vmem_limit_bytes=STAGE_VMEM_LIMIT),
    )(c_idx, *arrs, *gots)


def _add_chips(parts, gots, j_idx, name):
    n = len(parts)
    n_split = 2

    def body(j_ref, *refs):
        for a in range(n):
            acc = refs[a][0].astype(F32)
            for k in range(3):
                acc = acc + refs[n + a][k].astype(F32)
            refs[2 * n + a][...] = acc

    in_specs, out_specs = [], []
    for g in gots:
        rb, cols = g.shape[1] // n_split, g.shape[2]
        in_specs.append(pl.BlockSpec((1, rb, cols), lambda r, j_ref: (j_ref[0], r, 0)))
        out_specs.append(pl.BlockSpec((rb, cols), lambda r, j_ref: (r, 0)))
    for g in gots:
        rb, cols = g.shape[1] // n_split, g.shape[2]
        in_specs.append(pl.BlockSpec((3, rb, cols), lambda r, j_ref: (0, r, 0)))
    grid_spec = pltpu.PrefetchScalarGridSpec(
        num_scalar_prefetch=1, grid=(n_split,), in_specs=in_specs, out_specs=tuple(out_specs))
    return pl.pallas_call(
        body, name=name, grid_spec=grid_spec,
        out_shape=tuple(jax.ShapeDtypeStruct(g.shape[1:], F32) for g in gots),
        compiler_params=pltpu.CompilerParams(vmem_limit_bytes=STAGE_VMEM_LIMIT),
    )(j_idx, *parts, *gots)


def _small_exchange(sv, w_mod_sh, cctx, m_cctx, v_cctx, bmod, m_bmod, v_bmod, qg, m_qg, v_qg, kvg, m_kvg, v_kvg,
                    gf, m_gf, v_gf, swap_halves, scatter_parts):
    ncol = w_mod_sh.shape[1]
    n_s, n_p = len(swap_halves), len(scatter_parts)
    n_x = n_s + n_p

    def body(sv_ref, w_ref, cctx_ref, mcc_ref, vcc_ref, b_ref, mb_ref, vb_ref, qg_ref, mq_ref, vq_ref,
             kg_ref, mk_ref, vk_ref, gf_ref, mgf_ref, vgf_ref, *rest):
        all_ref, red_ref, occ_ref, ob_ref, oq_ref, ok_ref, ogf_ref = rest[n_x:n_x + 7]
        vec_ref, part_ref, ssem, rsem, ssem2, rsem2, wsend, wrecv, psend, precv = rest[2 * n_x + 7:]
        swap = _SiblingSwap(rest[:n_s], rest[n_x + 7:n_x + 7 + n_s], wsend, wrecv)
        scatter = _ChipScatter(rest[n_s:n_x], rest[n_x + 7 + n_s:2 * n_x + 7], psend, precv)
        swap.start()
        scatter.start()
        x, y, c = _pos()
        me = 4 * x + 2 * y + c
        j = 2 * x + y
        sends = []
        for r in range(1, 8):
            cp = pltpu.make_async_remote_copy(
                src_ref=sv_ref, dst_ref=all_ref.at[me], send_sem=ssem.at[r - 1], recv_sem=rsem.at[r - 1],
                device_id=_peer(r, x, y, c), device_id_type=MESH)
            cp.start()
            sends.append(cp)
        for cp in sends:
            cp.wait()
        all_ref[me] = sv_ref[...]
        red = all_ref[0]
        for d in range(1, 8):
            red = red + all_ref[d]
        red_ref[...] = red
        vec_ref[...] = jnp.zeros(vec_ref.shape, F32)

        @pl.when(j == 0)
        def _():
            vec_ref[0:1, 0:1024] = red[6:7, :]
            vec_ref[0:1, 1024:1536] = red[7:8, 0:512]

        @pl.when(j == 1)
        def _():
            vec_ref[0:1, 0:512] = red[7:8, 512:1024]

        part = lax.dot_general(vec_ref[...], w_ref[...], (((1,), (1,)), ((), ())), precision=HIGHEST,
                               preferred_element_type=F32)
        part_ref[j] = part
        sends2 = []
        for k, r in enumerate((4, 2, 6)):
            cp = pltpu.make_async_remote_copy(
                src_ref=part_ref.at[j], dst_ref=part_ref.at[j], send_sem=ssem2.at[k], recv_sem=rsem2.at[k],
                device_id=_peer(r, x, y, c), device_id_type=MESH)
            cp.start()
            sends2.append(cp)
        for cp in sends2:
            cp.wait()
        tot = part_ref[0]
        for s in range(1, N_SHARD):
            tot = tot + part_ref[s]
        cc = cctx_ref[...]
        sg = _sigmoid(cc)
        g_cc = tot[0:1, :] * (sg * (1.0 + cc * (1.0 - sg)))
        d_, m_, v_ = _adamw(cc, g_cc, mcc_ref[...], vcc_ref[...])
        occ_ref[0:1, :] = g_cc
        occ_ref[1:2, :] = d_
        occ_ref[2:3, :] = m_
        occ_ref[3:4, :] = v_
        occ_ref[4:8, :] = jnp.zeros((4, D_MODEL), F32)
        g_b = red[0:6, :]
        pad = jnp.concatenate([red[6:8, :], jnp.zeros((4, D_MODEL), F32)], axis=0)
        g_b = g_b + pad
        d_, m_, v_ = _adamw(b_ref[...], g_b, mb_ref[...], vb_ref[...])
        ob_ref[0] = g_b
        ob_ref[1] = d_
        ob_ref[2] = m_
        ob_ref[3] = v_
        g_q = red[9:10, 0:Q_RANK]
        d_, m_, v_ = _adamw(qg_ref[...], g_q, mq_ref[...], vq_ref[...])
        oq_ref[0:1, :] = g_q
        oq_ref[1:2, :] = d_
        oq_ref[2:3, :] = m_
        oq_ref[3:4, :] = v_
        oq_ref[4:8, :] = jnp.zeros((4, Q_RANK), F32)
        g_k = red[10:11, 0:KV_RANK]
        d_, m_, v_ = _adamw(kg_ref[...], g_k, mk_ref[...], vk_ref[...])
        ok_ref[0:1, :] = g_k
        ok_ref[1:2, :] = d_
        ok_ref[2:3, :] = m_
        ok_ref[3:4, :] = v_
        ok_ref[4:8, :] = jnp.zeros((4, KV_RANK), F32)
        g_f = red[8:9, :]
        d_, m_, v_ = _adamw(gf_ref[...], g_f, mgf_ref[...], vgf_ref[...])
        ogf_ref[0:1, :] = g_f
        ogf_ref[1:2, :] = d_
        ogf_ref[2:3, :] = m_
        ogf_ref[3:4, :] = v_
        ogf_ref[4:8, :] = jnp.zeros((4, D_MODEL), F32)
        swap.finish()
        scatter.finish()

    vm = pl.BlockSpec(memory_space=pltpu.VMEM)
    hbm = pl.BlockSpec(memory_space=pl.ANY)
    out_shape = (
        jax.ShapeDtypeStruct((8, 16, D_MODEL), F32),
        jax.ShapeDtypeStruct((16, D_MODEL), F32),
        jax.ShapeDtypeStruct((8, D_MODEL), F32),
        jax.ShapeDtypeStruct((4, 6, D_MODEL), F32),
        jax.ShapeDtypeStruct((8, Q_RANK), F32),
        jax.ShapeDtypeStruct((8, KV_RANK), F32),
        jax.ShapeDtypeStruct((8, D_MODEL), F32),
    ) + tuple(jax.ShapeDtypeStruct(h.shape, F32) for h in swap_halves) + tuple(
        jax.ShapeDtypeStruct((3,) + p.shape[1:], BF16) for p in scatter_parts)
    return pl.pallas_call(
        body, name="small_exchange", out_shape=out_shape, in_specs=[vm] * 17 + [hbm] * n_x,
        out_specs=tuple([vm] * 7) + (hbm,) * n_x,
        scratch_shapes=[pltpu.VMEM((8, ncol), F32), pltpu.VMEM((N_SHARD, 8, D_MODEL), F32),
                        pltpu.SemaphoreType.DMA((7,)), pltpu.SemaphoreType.DMA((7,)),
                        pltpu.SemaphoreType.DMA((3,)), pltpu.SemaphoreType.DMA((3,)),
                        pltpu.SemaphoreType.DMA((n_s,)), pltpu.SemaphoreType.DMA((n_s,)),
                        pltpu.SemaphoreType.DMA((3 * n_p,)), pltpu.SemaphoreType.DMA((3 * n_p,))],
        compiler_params=pltpu.CompilerParams(vmem_limit_bytes=VMEM_LIMIT),
    )(sv, w_mod_sh, cctx, m_cctx, v_cctx, bmod, m_bmod, v_bmod, qg, m_qg, v_qg, kvg, m_kvg, v_kvg, gf, m_gf, v_gf,
      *swap_halves, *scatter_parts)


def _inproj_fwd(x2, ctx2, mod_a, w_in, qg, kvg, w_uq, w_ukv, cos_t, sin_a, sin_b):
    t_lat, t_ctx = x2.shape[0], ctx2.shape[0]
    tm = TOK_TILE
    n_lat = t_lat // tm
    n_all = n_lat + t_ctx // tm
    e_rows = t_lat + t_ctx

    def body(x_ref, ctx_ref, mod_ref, win_ref, qg_ref, kvg_ref, wuq_ref, wukv_ref, cos_ref, sa_ref, sb_ref,
             z_ref, q_ref, k_ref, v_ref, kt_ref):
        i = pl.program_id(0)
        xin = jnp.where(i < n_lat, x_ref[...], ctx_ref[...])
        xn = xin * lax.rsqrt(jnp.mean(xin * xin, axis=-1, keepdims=True) + EPS)
        h1 = (xn * (1.0 + mod_ref[0, 1:2, :]) + mod_ref[0, 0:1, :]).astype(BF16)
        z = _dot(h1, win_ref[...])
        z_ref[...] = z
        cos, sa, sb = cos_ref[...], sa_ref[...], sb_ref[...]
        cq = z[:, 0:Q_RANK]
        cqn = (cq * lax.rsqrt(jnp.mean(cq * cq, axis=-1, keepdims=True) + EPS) * qg_ref[...]).astype(BF16)
        q = _dot(cqn, wuq_ref[...])
        ckv = z[:, Q_RANK:Q_RANK + KV_RANK]
        ckvn = (ckv * lax.rsqrt(jnp.mean(ckv * ckv, axis=-1, keepdims=True) + EPS) * kvg_ref[...]).astype(BF16)
        kv = _dot(ckvn, wukv_ref[...])
        kr = _rope(z[:, Q_RANK + KV_RANK:Q_RANK + KV_RANK + HEAD_PAD], cos, sa, sb)
        ones_lane = lax.broadcasted_iota(jnp.int32, (tm, HEAD_PAD), 1) == V_DIM
        for h in range(N_HEADS):
            lo = h * HEAD_PAD
            q_ref[h] = _rope(q[:, lo:lo + HEAD_PAD], cos, sa, sb).astype(BF16)
            kh = kv[:, lo:lo + HEAD_PAD] + kr
            k_ref[h] = kh.astype(BF16)
            kt_ref[h] = kh.T.astype(BF16)
            vh = kv[:, N_HEADS * HEAD_PAD + lo:N_HEADS * HEAD_PAD + lo + HEAD_PAD]
            v_ref[h] = jnp.where(ones_lane, 1.0, vh).astype(BF16)

    row = lambda i: (i, 0)
    head_spec = pl.BlockSpec((N_HEADS, tm, HEAD_PAD), lambda i: (0, i, 0))
    head_shape = jax.ShapeDtypeStruct((N_HEADS, e_rows, HEAD_PAD), BF16)
    return pl.pallas_call(
        body, name="inproj_fwd", grid=(n_all,),
        out_shape=(jax.ShapeDtypeStruct((e_rows, Z_COLS), F32), head_shape, head_shape, head_shape,
                   jax.ShapeDtypeStruct((N_HEADS, HEAD_PAD, e_rows), BF16)),
        in_specs=[
            pl.BlockSpec((tm, D_MODEL), lambda i: (jnp.minimum(i, n_lat - 1), 0)),
            _const_spec((tm, D_MODEL)),
            pl.BlockSpec((1, 8, D_MODEL), lambda i: (i // n_lat, 0, 0)),
            _const_spec(w_in.shape), _const_spec(qg.shape), _const_spec(kvg.shape),
            _const_spec(w_uq.shape), _const_spec(w_ukv.shape),
            pl.BlockSpec((tm, HEAD_PAD), row), pl.BlockSpec((tm, HEAD_PAD), row), pl.BlockSpec((tm, HEAD_PAD), row),
        ],
        out_specs=(pl.BlockSpec((tm, Z_COLS), row), head_spec, head_spec, head_spec,
                   pl.BlockSpec((N_HEADS, HEAD_PAD, tm), lambda i: (0, 0, i))),
        compiler_params=pltpu.CompilerParams(vmem_limit_bytes=VMEM_LIMIT),
    )(x2, ctx2, mod_a, w_in, qg, kvg, w_uq, w_ukv, cos_t, sin_a, sin_b)


def _key_chunks(e_rows, size):
    n_chunks = max(1, e_rows // size)
    return [(ci * size, size if ci < n_chunks - 1 else e_rows - ci * size) for ci in range(n_chunks)]


def _attn_fwd(q, k, v, t_lat, shard_arrays):
    e_rows = k.shape[1]
    tq = min(t_lat, ATTN_FWD_Q_BLOCK)
    bounds = _key_chunks(e_rows, KEY_CHUNK)
    c2 = ATTN_SCALE * LOG2E

    hb = ATTN_HEADS_PER_STEP
    n_hb = N_HEADS // hb

    def body(q_ref, k_ref, v_ref, o_ref, lse_ref):
        qs = [q_ref[b] for b in range(hb)]
        m, acc = [None] * hb, [None] * hb
        for lo, n in bounds:
            for b in range(hb):
                s = _dot_nt(qs[b], k_ref[b, lo:lo + n, :])
                mc = jnp.max(s, axis=-1, keepdims=True)
                m_new = mc if m[b] is None else jnp.maximum(m[b], mc)
                p = jnp.exp2((s - m_new) * c2)
                pv = _dot(p.astype(BF16), v_ref[b, lo:lo + n, :])
                acc[b] = pv if m[b] is None else acc[b] * jnp.exp2((m[b] - m_new) * c2) + pv
                m[b] = m_new
        outs = []
        for b in range(hb):
            l = acc[b][:, V_DIM:V_DIM + 1]
            outs.append(acc[b] * (1.0 / l))
            lse = (m[b] * ATTN_SCALE + jnp.log(l)) * LOG2E
            lse_ref[b] = jnp.broadcast_to(lse, (tq, HEAD_PAD)).T[0:1, :]
        low = lax.broadcasted_iota(jnp.int32, (tq, HEAD_PAD), 1) < V_DIM
        pairs = [jnp.where(low, outs[b], pltpu.roll(outs[b + 1], V_DIM, 1)) for b in range(0, hb, 2)]
        o_ref[...] = jnp.concatenate(pairs, axis=1).astype(BF16)

    n_w = len(shard_arrays)
    n_q = t_lat // tq

    def body_with_gather(q_ref, k_ref, v_ref, *rest):
        o_ref, lse_ref = rest[n_w], rest[n_w + 1]
        gather = _ShardGather(rest[n_w + 2:2 * n_w + 2], *rest[2 * n_w + 2:])
        step = pl.program_id(0) * n_q + pl.program_id(1)
        pl.when(step == 0)(gather.start)
        pl.when(step == n_hb * n_q // 2)(gather.forward)
        body(q_ref, k_ref, v_ref, o_ref, lse_ref)
        pl.when(step == n_hb * n_q - 1)(gather.finish)

    hbm = pl.BlockSpec(memory_space=pl.ANY)
    return pl.pallas_call(
        body_with_gather, name="attn_fwd", grid=(n_hb, n_q),
        out_shape=(jax.ShapeDtypeStruct((t_lat, N_HEADS * V_DIM), BF16),
                   jax.ShapeDtypeStruct((N_HEADS, 1, t_lat), F32))
        + tuple(jax.ShapeDtypeStruct(a.shape, a.dtype) for a in shard_arrays),
        in_specs=[pl.BlockSpec((hb, tq, HEAD_PAD), lambda h, i: (h, i, 0)),
                  pl.BlockSpec((hb, e_rows, HEAD_PAD), lambda h, i: (h, 0, 0)),
                  pl.BlockSpec((hb, e_rows, HEAD_PAD), lambda h, i: (h, 0, 0))] + [hbm] * n_w,
        out_specs=(pl.BlockSpec((tq, hb * V_DIM), lambda h, i: (i, h)),
                   pl.BlockSpec((hb, 1, tq), lambda h, i: (h, 0, i))) + (hbm,) * n_w,
        input_output_aliases={3 + a: 2 + a for a in range(n_w)},
        scratch_shapes=_gather_sems(n_w),
        compiler_params=pltpu.CompilerParams(vmem_limit_bytes=VMEM_LIMIT),
    )(q, k, v, *shard_arrays)


def _attn_bwd(q, k, v, kt, o, do, lse_row, t_lat, parts):
    e_rows = k.shape[1]
    tq = min(t_lat, ATTN_BWD_Q_BLOCK)
    n_p = len(parts)
    n_q = t_lat // tq
    bounds = _key_chunks(e_rows, KEY_CHUNK_BWD)

    hb = ATTN_BWD_HEADS_PER_STEP
    assert hb * V_DIM == HEAD_PAD, "a step's heads share one lane tile of the unpadded o / dO"
    n_hb = N_HEADS // hb

    def body(q_ref, k_ref, v_ref, kt_ref, o_ref, do_ref, lse_ref, *rest):
        dqt_ref, dk_ref, dv_ref = rest[n_p:n_p + 3]
        scatter = _ChipScatter(rest[:n_p], rest[n_p + 3:2 * n_p + 3], rest[2 * n_p + 3], rest[2 * n_p + 4])
        h, i = pl.program_id(0), pl.program_id(1)
        pl.when(jnp.logical_and(h == 0, i == 0))(scatter.start)

        @pl.when(i == 0)
        def _():
            dk_ref[...] = jnp.zeros(dk_ref.shape, F32)
            dv_ref[...] = jnp.zeros(dv_ref.shape, F32)

        do_pair = do_ref[...].astype(F32)
        prod = o_ref[...].astype(F32) * do_pair
        lane = lax.broadcasted_iota(jnp.int32, (tq, HEAD_PAD), 1)
        sel = lax.broadcasted_iota(jnp.int32, (8, HEAD_PAD), 1)
        qs, dos, lses, deltas = [], [], [], []
        for b in range(hb):
            qs.append(q_ref[b])
            mine = do_pair if b == 0 else pltpu.roll(do_pair, V_DIM, 1)
            dos.append(jnp.where(lane < V_DIM, mine, 0.0).astype(BF16))
            lses.append(lse_ref[b])
            ones = jnp.where((sel < V_DIM) == (b == 0), 1.0, 0.0)
            deltas.append(lax.dot_general(ones, prod, (((1,), (1,)), ((), ())), precision=HIGHEST,
                                          preferred_element_type=F32)[0:1, :])
        dqt = [None] * hb
        for lo, n in bounds:
            for b in range(hb):
                pt = jnp.exp2(_dot_nt(k_ref[b, lo:lo + n, :], qs[b]) * (ATTN_SCALE * LOG2E) - lses[b])
                dpt = _dot_nt(v_ref[b, lo:lo + n, :], dos[b])
                dst = (pt * (dpt - deltas[b])).astype(BF16)
                dv_c = _dot(pt.astype(BF16), dos[b])
                dk_c = _dot(dst, qs[b])
                part = _dot(kt_ref[b, :, lo:lo + n], dst)
                dqt[b] = part if dqt[b] is None else dqt[b] + part
                dk_ref[b, lo:lo + n, :] += dk_c * ATTN_SCALE
                dv_ref[b, lo:lo + n, :] += dv_c
        for b in range(hb):
            dqt_ref[b] = (dqt[b] * ATTN_SCALE).T

        pl.when(jnp.logical_and(h == n_hb - 1, i == n_q - 1))(scatter.finish)

    hbm = pl.BlockSpec(memory_space=pl.ANY)
    qspec = pl.BlockSpec((hb, tq, HEAD_PAD), lambda h, i: (h, i, 0))
    kspec = pl.BlockSpec((hb, e_rows, HEAD_PAD), lambda h, i: (h, 0, 0))
    pair = pl.BlockSpec((tq, hb * V_DIM), lambda h, i: (i, h))
    return pl.pallas_call(
        body, name="attn_bwd", grid=(n_hb, n_q),
        out_shape=(jax.ShapeDtypeStruct((N_HEADS, t_lat, HEAD_PAD), F32),
                   jax.ShapeDtypeStruct((N_HEADS, e_rows, HEAD_PAD), F32),
                   jax.ShapeDtypeStruct((N_HEADS, e_rows, HEAD_PAD), F32))
        + tuple(jax.ShapeDtypeStruct((3,) + p.shape[1:], BF16) for p in parts),
        in_specs=[qspec, kspec, kspec, pl.BlockSpec((hb, HEAD_PAD, e_rows), lambda h, i: (h, 0, 0)), pair, pair,
                  pl.BlockSpec((hb, 1, tq), lambda h, i: (h, 0, i))] + [hbm] * n_p,
        out_specs=(qspec, kspec, kspec) + (hbm,) * n_p,
        scratch_shapes=[pltpu.SemaphoreType.DMA((3 * n_p,)), pltpu.SemaphoreType.DMA((3 * n_p,))],
        compiler_params=pltpu.CompilerParams(vmem_limit_bytes=VMEM_LIMIT),
    )(q, k, v, kt, o, do, lse_row, *parts)


def _halo_specs(tm, col_block):
    per = tm // 8
    prev = pl.BlockSpec((8, CONV_W), lambda i: (jnp.maximum(i * per - 1, 0), col_block))
    nxt = pl.BlockSpec((8, CONV_W), lambda i: ((i + 1) * per, col_block))
    return prev, nxt


def _mlp_fwdbwd(o, z, x2, tgt, mod_b, gf, cw, w_out, w1, w2):
    t_lat = x2.shape[0]
    tm = TOK_TILE
    n_lat = t_lat // tm
    fc = MLP_FF_CHUNK
    n_ff = D_FF // fc

    def body(o_ref, gb_ref, gc_ref, xi_ref, gcp_ref, xip_ref, gcn_ref, xin_ref, cw_ref, wout_ref,
             x_ref, t_ref, mod_ref, gf_ref, w1_ref, w2_ref,
             r_ref, da_ref, h2_ref, dy2_ref, dx1_ref, conv_ref, acc_ref, dy1_ref, do_ref, dgb_ref, dyv_ref, ra_ref):
        i = pl.program_id(0)

        @pl.when(i == 0)
        def _():
            acc_ref[...] = jnp.zeros(acc_ref.shape, F32)

        g1, sh2, sc2, g2 = mod_ref[0:1, :], mod_ref[1:2, :], mod_ref[2:3, :], mod_ref[3:4, :]
        u = gc_ref[...] * xi_ref[...]
        u_prev = jnp.where(i > 0, gcp_ref[7:8, :] * xip_ref[7:8, :], 0.0)
        u_next = jnp.where(i < n_lat - 1, gcn_ref[0:1, :] * xin_ref[0:1, :], 0.0)
        um1, up1 = _shift_rows(u, u_prev, u_next)
        yv = cw_ref[0:1, :] * um1 + cw_ref[1:2, :] * u + cw_ref[2:3, :] * up1
        gb = gb_ref[...]
        conv = (gb * yv).astype(BF16)
        conv_ref[...] = conv
        n_attn = N_HEADS * V_DIM
        y1 = _dot(o_ref[...], wout_ref[0:n_attn, :]) + _dot(conv, wout_ref[n_attn:, :])
        x1 = x_ref[...] + g1 * y1
        rstd2 = lax.rsqrt(jnp.mean(x1 * x1, axis=-1, keepdims=True) + EPS)
        xn1 = x1 * rstd2
        h2 = (xn1 * (1.0 + sc2) + sh2).astype(BF16)
        h2_ref[...] = h2
        y2 = jnp.zeros((tm, D_MODEL), F32)
        for jj in range(n_ff):
            lo = jj * fc
            ra = jnp.maximum(_dot(h2, w1_ref[lo // FF_CHUNK, :, lo % FF_CHUNK:lo % FF_CHUNK + fc]), 0.0)
            ra_ref[jj] = ra
            r = (ra * ra).astype(BF16)
            r_ref[:, lo:lo + fc] = r
            y2 = y2 + _dot(r, w2_ref[lo:lo + fc, :])
        x2v = x1 + g2 * y2
        rstd3 = lax.rsqrt(jnp.mean(x2v * x2v, axis=-1, keepdims=True) + EPS)
        xn3 = x2v * rstd3
        gfv = gf_ref[...]
        diff = xn3 * gfv - t_ref[...]
        loss_t = 0.5 * jnp.sum(jnp.sum(diff * diff, axis=-1, keepdims=True), axis=0, keepdims=True) * (1.0 / D_MODEL)
        dy = diff * (1.0 / D_MODEL)
        dxn3 = dy * gfv
        dx2 = rstd3 * (dxn3 - xn3 * jnp.mean(dxn3 * xn3, axis=-1, keepdims=True))
        dy2 = (dx2 * g2).astype(BF16)
        dy2_ref[...] = dy2
        dh2 = jnp.zeros((tm, D_MODEL), F32)
        for jj in range(n_ff):
            lo = jj * fc
            dr = _dot_nt(dy2, w2_ref[lo:lo + fc, :])
            da = (2.0 * ra_ref[jj] * dr).astype(BF16)
            da_ref[:, lo:lo + fc] = da
            dh2 = dh2 + _dot_nt(da, w1_ref[lo // FF_CHUNK, :, lo % FF_CHUNK:lo % FF_CHUNK + fc])
        dxn1 = dh2 * (1.0 + sc2)
        dx1 = dx2 + rstd2 * (dxn1 - xn1 * jnp.mean(dxn1 * xn1, axis=-1, keepdims=True))
        dx1_ref[...] = dx1
        dy1 = (dx1 * g1).astype(BF16)
        dy1_ref[...] = dy1
        do_ref[...] = _dot_nt(dy1, wout_ref[0:n_attn, :]).astype(BF16)
        dconv = _dot_nt(dy1, wout_ref[n_attn:, :])
        dgb_ref[...] = dconv * yv
        dyv_ref[...] = dconv * gb
        acc_ref[5:6, :] += jnp.sum(dx1 * y1, axis=0, keepdims=True)
        acc_ref[0:1, :] += jnp.sum(dy * xn3, axis=0, keepdims=True)
        acc_ref[1:2, :] += jnp.sum(dx2 * y2, axis=0, keepdims=True)
        acc_ref[2:3, :] += jnp.sum(dh2, axis=0, keepdims=True)
        acc_ref[3:4, :] += jnp.sum(dh2 * xn1, axis=0, keepdims=True)
        acc_ref[4:5, :] += jnp.broadcast_to(loss_t, (1, D_MODEL))

    row = lambda i: (i, 0)
    gcp, gcn = _halo_specs(tm, 2)
    xip, xin = _halo_specs(tm, 3)
    tile = pl.BlockSpec((tm, D_MODEL), row)
    wide = pl.BlockSpec((tm, D_FF), row)
    half = pl.BlockSpec((tm, CONV_W), row)
    return pl.pallas_call(
        body, name="mlp_fwdbwd", grid=(n_lat,),
        out_shape=(jax.ShapeDtypeStruct((t_lat, D_FF), BF16), jax.ShapeDtypeStruct((t_lat, D_FF), BF16),
                   jax.ShapeDtypeStruct((t_lat, D_MODEL), BF16), jax.ShapeDtypeStruct((t_lat, D_MODEL), BF16),
                   jax.ShapeDtypeStruct((t_lat, D_MODEL), F32), jax.ShapeDtypeStruct((t_lat, CONV_W), BF16),
                   jax.ShapeDtypeStruct((8, D_MODEL), F32),
                   jax.ShapeDtypeStruct((t_lat, D_MODEL), BF16),
                   jax.ShapeDtypeStruct((t_lat, N_HEADS * V_DIM), BF16),
                   jax.ShapeDtypeStruct((t_lat, CONV_W), F32), jax.ShapeDtypeStruct((t_lat, CONV_W), F32)),
        in_specs=[
            pl.BlockSpec((tm, N_HEADS * V_DIM), row),
            pl.BlockSpec((tm, CONV_W), lambda i: (i, 1)), pl.BlockSpec((tm, CONV_W), lambda i: (i, 2)),
            pl.BlockSpec((tm, CONV_W), lambda i: (i, 3)),
            gcp, xip, gcn, xin,
            _const_spec(cw.shape), _resident_spec(w_out.shape),
            tile, tile, _const_spec(mod_b.shape), _const_spec(gf.shape),
            _resident_spec(w1.shape), _resident_spec(w2.shape),
        ],
        out_specs=(wide, wide, tile, tile, tile, half, _const_spec((8, D_MODEL)),
                   tile, pl.BlockSpec((tm, N_HEADS * V_DIM), row), half, half),
        scratch_shapes=[pltpu.VMEM((n_ff, tm, fc), F32)],
        compiler_params=pltpu.CompilerParams(vmem_limit_bytes=VMEM_LIMIT),
    )(o, z, z, z, z, z, z, z, cw, w_out, x2, tgt, mod_b, gf, w1, w2)


def _inproj_bwd(x2, ctx2, mod_a, z, dyv, dgb, dx1, dqt, dk, dv, cos_t, sin_a, sin_b, w_in, w_uq, w_ukv, qg, kvg, cw):
    t_lat, t_ctx = x2.shape[0], ctx2.shape[0]
    tm = TOK_TILE
    n_lat = t_lat // tm
    n_all = n_lat + t_ctx // tm
    group = max(g for g in (1, 2, 4) if n_lat % g == 0)

    def body(x_ref, ctx_ref, mod_ref, z_ref, gcp_ref, xip_ref, gcn_ref, xin_ref, dyv_ref, dyvp_ref, dyvn_ref,
             dgb_ref, dx1_ref, dqt_ref, dk_ref, dv_ref, cos_ref, sa_ref, sb_ref, win_ref, wuq_ref, wukv_ref,
             qg_ref, kvg_ref, cw_ref, gx_ref, dwin_out, dwuq_out, dwukv_out, acc_ref, h1_buf, dz_buf,
             dwin_ref, dwuq_ref, dwukv_ref):
        i = pl.program_id(0)
        lat = i < n_lat

        @pl.when(i == 0)
        def _():
            dwin_ref[...] = jnp.zeros(dwin_ref.shape, F32)
            dwuq_ref[...] = jnp.zeros(dwuq_ref.shape, F32)
            dwukv_ref[...] = jnp.zeros(dwukv_ref.shape, F32)
            acc_ref[...] = jnp.zeros(acc_ref.shape, F32)

        xin = jnp.where(lat, x_ref[...], ctx_ref[...])
        rstd = lax.rsqrt(jnp.mean(xin * xin, axis=-1, keepdims=True) + EPS)
        xn = xin * rstd
        sc = mod_ref[0, 1:2, :]
        h1 = (xn * (1.0 + sc) + mod_ref[0, 0:1, :]).astype(BF16)
        z = z_ref[...]
        cos, sa, sb = cos_ref[...], sa_ref[...], sb_ref[...]
        qgv, kvgv = qg_ref[...], kvg_ref[...]
        cq = z[:, 0:Q_RANK]
        cqh = cq * lax.rsqrt(jnp.mean(cq * cq, axis=-1, keepdims=True) + EPS)
        rq = lax.rsqrt(jnp.mean(cq * cq, axis=-1, keepdims=True) + EPS)
        cqn = (cqh * qgv).astype(BF16)
        parts = []
        for h in range(N_HEADS):
            g = jnp.where(lat, dqt_ref[h], 0.0)
            parts.append(_unrope(g, cos, sa, sb))
        dq = jnp.concatenate(parts, axis=1).astype(BF16)
        dcqn = _dot_nt(dq, wuq_ref[...])
        dwuq_ref[...] += _dot_tn(cqn, dq)
        acc_ref[4:5, 0:Q_RANK] += jnp.sum(dcqn * cqh, axis=0, keepdims=True)
        dxn = dcqn * qgv
        dcq = rq * (dxn - cqh * jnp.mean(dxn * cqh, axis=-1, keepdims=True))
        ckv = z[:, Q_RANK:Q_RANK + KV_RANK]
        rk = lax.rsqrt(jnp.mean(ckv * ckv, axis=-1, keepdims=True) + EPS)
        ckvh = ckv * rk
        ckvn = (ckvh * kvgv).astype(BF16)
        dks = [dk_ref[h] for h in range(N_HEADS)]
        dkr = dks[0]
        for h in range(1, N_HEADS):
            dkr = dkr + dks[h]
        dkv = jnp.concatenate(dks + [dv_ref[h] for h in range(N_HEADS)], axis=1).astype(BF16)
        dckvn = _dot_nt(dkv, wukv_ref[...])
        dwukv_ref[...] += _dot_tn(ckvn, dkv)
        acc_ref[5:6, 0:KV_RANK] += jnp.sum(dckvn * ckvh, axis=0, keepdims=True)
        dxn = dckvn * kvgv
        dckv = rk * (dxn - ckvh * jnp.mean(dxn * ckvh, axis=-1, keepdims=True))
        dkr = _unrope(dkr, cos, sa, sb)
        gb, gc, xi = z[:, 512:1024], z[:, 1024:1536], z[:, 1536:2048]
        u = gc * xi
        u_prev = jnp.where(i > 0, gcp_ref[7:8, :] * xip_ref[7:8, :], 0.0)
        u_next = jnp.where(i < n_lat - 1, gcn_ref[0:1, :] * xin_ref[0:1, :], 0.0)
        um1, up1 = _shift_rows(u, u_prev, u_next)
        dyv = jnp.where(lat, dyv_ref[...], 0.0)
        dyv_prev = jnp.where(jnp.logical_and(i > 0, lat), dyvp_ref[7:8, :], 0.0)
        dyv_next = jnp.where(i < n_lat - 1, dyvn_ref[0:1, :], 0.0)
        dyv_m1, dyv_p1 = _shift_rows(dyv, dyv_prev, dyv_next)
        du = cw_ref[0:1, :] * dyv_p1 + cw_ref[1:2, :] * dyv + cw_ref[2:3, :] * dyv_m1
        dgc = du * xi
        dxi = du * gc
        dgb = jnp.where(lat, dgb_ref[...], 0.0)
        acc_ref[6:7, 0:CONV_W] += jnp.sum(dyv * um1, axis=0, keepdims=True)
        acc_ref[7:8, 0:CONV_W] += jnp.sum(dyv * u, axis=0, keepdims=True)
        acc_ref[8:9, 0:CONV_W] += jnp.sum(dyv * up1, axis=0, keepdims=True)
        dz = jnp.concatenate([dcq, dckv, dkr, dgb, dgc, dxi], axis=1).astype(BF16)
        dh1 = _dot_nt(dz, win_ref[...])
        slot = i % group
        rows_g = pl.ds(pl.multiple_of(slot * tm, tm), tm)
        h1_buf[rows_g, :] = h1
        dz_buf[rows_g, :] = dz

        @pl.when(jnp.logical_and(lat, slot == group - 1))
        def _():
            dwin_ref[...] += _dot_tn(h1_buf[...], dz_buf[...])

        @pl.when(jnp.logical_not(lat))
        def _():
            dwin_ref[...] += _dot_tn(h1, dz)
        s_sh = jnp.sum(dh1, axis=0, keepdims=True)
        s_sc = jnp.sum(dh1 * xn, axis=0, keepdims=True)
        zero = jnp.zeros_like(s_sh)
        acc_ref[0:1, :] += jnp.where(lat, s_sh, zero)
        acc_ref[1:2, :] += jnp.where(lat, s_sc, zero)
        acc_ref[2:3, :] += jnp.where(lat, zero, s_sh)
        acc_ref[3:4, :] += jnp.where(lat, zero, s_sc)
        dxn = dh1 * (1.0 + sc)
        dx = rstd * (dxn - xn * jnp.mean(dxn * xn, axis=-1, keepdims=True))

        @pl.when(lat)
        def _():
            gx_ref[...] = dx1_ref[...] + dx

        @pl.when(i == n_all - 1)
        def _():
            dwin_out[...] = dwin_ref[...].astype(BF16)
            dwuq_out[...] = dwuq_ref[...].astype(BF16)
            dwukv_out[...] = dwukv_ref[...].astype(BF16)

    last = n_lat - 1
    per = tm // 8
    lat_row = lambda i: (jnp.minimum(i, last), 0)
    row = lambda i: (i, 0)
    gcp, gcn = _halo_specs(tm, 2)
    xip, xin = _halo_specs(tm, 3)
    n_halo = t_lat // 8
    dyvp = pl.BlockSpec((8, CONV_W), lambda i: (jnp.clip(i * per - 1, 0, n_halo - 1), 0))
    dyvn = pl.BlockSpec((8, CONV_W), lambda i: (jnp.minimum((i + 1) * per, n_halo - 1), 0))
    gcn = pl.BlockSpec((8, CONV_W), lambda i: (jnp.minimum((i + 1) * per, (t_lat + t_ctx) // 8 - 1), 2))
    xin = pl.BlockSpec((8, CONV_W), lambda i: (jnp.minimum((i + 1) * per, (t_lat + t_ctx) // 8 - 1), 3))
    head_f32 = pl.BlockSpec((N_HEADS, tm, HEAD_PAD), lambda i: (0, i, 0))
    tab = pl.BlockSpec((tm, HEAD_PAD), row)
    return pl.pallas_call(
        body, name="inproj_bwd", grid=(n_all,),
        out_shape=(jax.ShapeDtypeStruct((t_lat, D_MODEL), F32), jax.ShapeDtypeStruct(w_in.shape, BF16),
                   jax.ShapeDtypeStruct(w_uq.shape, BF16), jax.ShapeDtypeStruct(w_ukv.shape, BF16),
                   jax.ShapeDtypeStruct((16, D_MODEL), F32)),
        in_specs=[
            pl.BlockSpec((tm, D_MODEL), lat_row), _const_spec((tm, D_MODEL)),
            pl.BlockSpec((1, 8, D_MODEL), lambda i: (i // n_lat, 0, 0)),
            pl.BlockSpec((tm, Z_COLS), row), gcp, xip, gcn, xin,
            pl.BlockSpec((tm, CONV_W), lat_row), dyvp, dyvn,
            pl.BlockSpec((tm, CONV_W), lat_row), pl.BlockSpec((tm, D_MODEL), lat_row),
            pl.BlockSpec((N_HEADS, tm, HEAD_PAD), lambda i: (0, jnp.minimum(i, last), 0)),
            head_f32, head_f32, tab, tab, tab,
            _const_spec(w_in.shape), _const_spec(w_uq.shape), _const_spec(w_ukv.shape),
            _const_spec(qg.shape), _const_spec(kvg.shape), _const_spec(cw.shape),
        ],
        out_specs=(pl.BlockSpec((tm, D_MODEL), lat_row), _const_spec(w_in.shape), _const_spec(w_uq.shape),
                   _const_spec(w_ukv.shape), _const_spec((16, D_MODEL))),
        scratch_shapes=[pltpu.VMEM((group * tm, D_MODEL), BF16), pltpu.VMEM((group * tm, Z_COLS), BF16),
                        pltpu.VMEM(w_in.shape, F32), pltpu.VMEM(w_uq.shape, F32), pltpu.VMEM(w_ukv.shape, F32)],
        compiler_params=pltpu.CompilerParams(vmem_limit_bytes=VMEM_LIMIT),
    )(x2, ctx2, mod_a, z, z, z, z, z, dyv, dyv, dyv, dgb, dx1, dqt, dk, dv, cos_t, sin_a, sin_b, w_in, w_uq, w_ukv,
      qg, kvg, cw)


def _wgrad(a, b, name, bm, bn):
    t, m = a.shape
    n = b.shape[1]
    bk = min(t, 4096)
    nk = t // bk
    nj = n // bn

    def body(a_ref, b_ref, o_ref, acc_ref):
        k = pl.program_id(2)
        part = _dot_tn(a_ref[...], b_ref[...])

        @pl.when(k == 0)
        def _():
            acc_ref[...] = part

        @pl.when(k > 0)
        def _():
            acc_ref[...] += part

        @pl.when(k == nk - 1)
        def _():
            o_ref[...] = acc_ref[...].astype(BF16)

    return pl.pallas_call(
        body, name=name, grid=(m // bm, nj, nk), out_shape=jax.ShapeDtypeStruct((m // bm * nj, bm, bn), BF16),
        in_specs=[pl.BlockSpec((bk, bm), lambda i, j, k: (k, i)), pl.BlockSpec((bk, bn), lambda i, j, k: (k, j))],
        out_specs=pl.BlockSpec((None, bm, bn), lambda i, j, k: (i * nj + j, 0, 0)),
        scratch_shapes=[pltpu.VMEM((bm, bn), F32)],
        compiler_params=pltpu.CompilerParams(vmem_limit_bytes=VMEM_LIMIT),
    )(a, b)


def _wgrad_out(o, conv, dy1, sib_arrays):
    t = o.shape[0]
    bk = min(t, 2048)
    nk = t // bk
    n_s = len(sib_arrays)

    def body(o_ref, c_ref, d_ref, *rest):
        w_ref, got_w_ref, acc_ref = rest[n_s], rest[2 * n_s + 1], rest[2 * n_s + 2]
        send = _SiblingSend(rest[:n_s], rest[n_s + 1:2 * n_s + 1], rest[2 * n_s + 3], rest[2 * n_s + 4])
        wsend, wrecv = rest[2 * n_s + 5], rest[2 * n_s + 6]
        k = pl.program_id(0)
        pl.when(k == 0)(send.start)
        part = _dot_tn(jnp.concatenate([o_ref[...], c_ref[...]], axis=1), d_ref[...])

        @pl.when(k == 0)
        def _():
            acc_ref[...] = part

        @pl.when(k > 0)
        def _():
            acc_ref[...] += part

        @pl.when(k == nk - 1)
        def _():
            w_ref[...] = acc_ref[...].astype(BF16)
            x, y, c = _pos()
            own = []
            for s in range(N_SHARD):
                theirs = pl.ds(pl.multiple_of(s * shard_rows + (1 - c) * (shard_rows // 2), 16), shard_rows // 2)
                cp = pltpu.make_async_remote_copy(
                    src_ref=w_ref.at[theirs], dst_ref=got_w_ref.at[s], send_sem=wsend.at[s], recv_sem=wrecv.at[s],
                    device_id=(x, y, 1 - c), device_id_type=MESH)
                cp.start()
                own.append(cp)
            for cp in own:
                cp.wait()

        pl.when(k == nk - 1)(send.finish)

    hbm = pl.BlockSpec(memory_space=pl.ANY)
    shard_rows = D_MODEL // N_SHARD
    return pl.pallas_call(
        body, name="wgrad_out", grid=(nk,),
        out_shape=(jax.ShapeDtypeStruct((D_MODEL, D_MODEL), BF16),)
        + tuple(jax.ShapeDtypeStruct((N_SHARD, a.shape[1] // 2, a.shape[2]), BF16) for a in sib_arrays)
        + (jax.ShapeDtypeStruct((N_SHARD, shard_rows // 2, D_MODEL), BF16),),
        in_specs=[pl.BlockSpec((bk, N_HEADS * V_DIM), lambda k: (k, 0)),
                  pl.BlockSpec((bk, CONV_W), lambda k: (k, 0)),
                  pl.BlockSpec((bk, D_MODEL), lambda k: (k, 0))] + [hbm] * n_s,
        out_specs=(_const_spec((D_MODEL, D_MODEL)),) + (hbm,) * (n_s + 1),
        scratch_shapes=[pltpu.VMEM((D_MODEL, D_MODEL), F32), pltpu.SemaphoreType.DMA((n_s,)),
                        pltpu.SemaphoreType.DMA((n_s,)), pltpu.SemaphoreType.DMA((N_SHARD,)),
                        pltpu.SemaphoreType.DMA((N_SHARD,))],
        compiler_params=pltpu.CompilerParams(vmem_limit_bytes=VMEM_LIMIT),
    )(o, conv, dy1, *sib_arrays)


ADAMW_STEPS = 4


def _adamw_call(ws, gs, ms, vs, name):
    n = len(ws)
    steps = ADAMW_STEPS if all(w.shape[0] % (8 * ADAMW_STEPS) == 0 for w in ws) else 1

    def body(*refs):
        for a in range(n):
            w_ref, g_ref, m_ref, v_ref = (refs[k * n + a] for k in range(4))
            d_, m_, v_ = _adamw(w_ref[...], g_ref[...], m_ref[...], v_ref[...])
            for k, val in enumerate((d_, m_, v_)):
                refs[(4 + k) * n + a][...] = val

    specs = [pl.BlockSpec((w.shape[0] // steps, w.shape[1]), lambda i: (i, 0)) for w in ws]
    shapes = [jax.ShapeDtypeStruct(w.shape, F32) for w in ws]
    outs = pl.pallas_call(
        body, name=name, grid=(steps,), out_shape=tuple(shapes * 3),
        in_specs=specs * 4, out_specs=tuple(specs * 3),
    )(*ws, *gs, *ms, *vs)
    return [tuple(outs[k * n + a] for k in range(3)) for a in range(n)]


def _adamw_halves(ws, g_mine, g_theirs, ms, vs, c_idx, name):
    n = len(ws)
    steps = ADAMW_STEPS
    nb = steps // 2

    def body(c_ref, *refs):
        mine = pl.program_id(0) // nb == c_ref[0]
        for a in range(n):
            w_ref, gm_ref, gt_ref, m_ref, v_ref = (refs[k * n + a] for k in range(5))
            g = jnp.where(mine, gm_ref[...], gt_ref[...])
            d_, m_, v_ = _adamw(w_ref[...], g, m_ref[...], v_ref[...])
            for k, val in enumerate((g, d_, m_, v_)):
                refs[(5 + k) * n + a][...] = val

    specs = [pl.BlockSpec((w.shape[0] // steps, w.shape[1]), lambda i, c_ref: (i, 0)) for w in ws]
    hspecs = [pl.BlockSpec((w.shape[0] // steps, w.shape[1]), lambda i, c_ref: (i % nb, 0)) for w in ws]
    shapes = [jax.ShapeDtypeStruct(w.shape, F32) for w in ws]
    grid_spec = pltpu.PrefetchScalarGridSpec(
        num_scalar_prefetch=1, grid=(steps,), in_specs=specs + hspecs + hspecs + specs + specs,
        out_specs=tuple(specs * 4))
    outs = pl.pallas_call(
        body, name=name, grid_spec=grid_spec, out_shape=tuple(shapes * 4),
        compiler_params=pltpu.CompilerParams(vmem_limit_bytes=VMEM_LIMIT),
    )(c_idx, *ws, *g_mine, *g_theirs, *ms, *vs)
    return [tuple(outs[k * n + a] for k in range(4)) for a in range(n)]


def _wmod_update(s_t, dm, w, m, v):
    rows, cols = w.shape
    cb = 512

    def body(s_ref, dm_ref, w_ref, m_ref, v_ref, g_ref, d_ref, nm_ref, nv_ref):
        g = jnp.dot(s_ref[...], dm_ref[...], precision=HIGHEST, preferred_element_type=F32)
        d_, m_, v_ = _adamw(w_ref[...], g, m_ref[...], v_ref[...])
        g_ref[...] = g
        d_ref[...] = d_
        nm_ref[...] = m_
        nv_ref[...] = v_

    spec = pl.BlockSpec((rows, cb), lambda i: (0, i))
    shp = jax.ShapeDtypeStruct((rows, cols), F32)
    return pl.pallas_call(
        body, name="wmod_update", grid=(cols // cb,), out_shape=(shp, shp, shp, shp),
        in_specs=[_const_spec(s_t.shape), pl.BlockSpec((16, cb), lambda i: (0, i)), spec, spec, spec],
        out_specs=(spec, spec, spec, spec),
        compiler_params=pltpu.CompilerParams(vmem_limit_bytes=VMEM_LIMIT),
    )(s_t, dm, w, m, v)


def _rope_tables(t_lat, t_ctx):
    t = jnp.arange(t_lat)
    pos = jnp.stack([(t // GRID_W).astype(F32), (t % GRID_W).astype(F32)], axis=1)
    half = QK_ROPE // 4
    freqs = ROPE_THETA ** (-jnp.arange(0, 2 * half, 2, dtype=F32) / (2 * half))
    ang = pos[:, :, None] * freqs[None, None, :]
    cos, sin = jnp.cos(ang), jnp.sin(ang)
    zero = jnp.zeros_like(sin)
    cos32 = jnp.concatenate([cos, cos], axis=2).reshape(t_lat, QK_ROPE)
    sa32 = jnp.concatenate([zero, sin], axis=2).reshape(t_lat, QK_ROPE)
    sb32 = jnp.concatenate([-sin, zero], axis=2).reshape(t_lat, QK_ROPE)

    def widen(tab, fill):
        left = jnp.full((t_lat, ROPE_LANE0), fill, F32)
        right = jnp.full((t_lat, HEAD_PAD - ROPE_LANE0 - QK_ROPE), fill, F32)
        lat = jnp.concatenate([left, tab, right], axis=1)
        return jnp.concatenate([lat, jnp.full((t_ctx, HEAD_PAD), fill, F32)], axis=0)

    return widen(cos32, 1.0), widen(sa32, 0.0), widen(sb32, 0.0)


def _cols_from_shards(s):
    return jnp.transpose(s, (1, 0, 2)).reshape(s.shape[1], -1)


def _cols_to_shards(w):
    k, n = w.shape
    return jnp.transpose(w.reshape(k, N_SHARD, n // N_SHARD), (1, 0, 2))


def kernel(x, c, ctx, c_ctx, w_mod, b_mod, w_in, q_norm_g, w_uq, kv_norm_g, w_ukv, conv_w, w_out, w_mlp1, w_mlp2, final_norm_g, loss_target, m_c_ctx, m_w_mod, m_b_mod, m_w_in, m_q_norm_g, m_w_uq, m_kv_norm_g, m_w_ukv, m_conv_w, m_w_out, m_w_mlp1, m_w_mlp2, m_final_norm_g, v_c_ctx, v_w_mod, v_b_mod, v_w_in, v_q_norm_g, v_w_uq, v_kv_norm_g, v_w_ukv, v_conv_w, v_w_out, v_w_mlp1, v_w_mlp2, v_final_norm_g):
    t_lat, t_ctx = x.shape[1], ctx.shape[1]
    assert t_ctx == TOK_TILE and t_lat % TOK_TILE == 0 and t_lat % GRID_W == 0
    mx, my, mc = _pos()
    j = 2 * mx + my
    ncol = w_mod.shape[2]
    x2, ctx2, tgt = x[0], ctx[0], loss_target[0]
    cctx_row = c_ctx.reshape(1, D_MODEL)

    b_sh = lax.dynamic_slice(b_mod, (0, j * ncol), (1, ncol))
    cw_pad = jnp.zeros((8, 128), F32).at[0:3, :].set(conv_w[0])
    c8, m_all, g_in, g_uq, g_ukv, g_out, g_m1, g_m2 = _prologue(
        c, cctx_row, w_mod[0], b_sh, cw_pad, (w_in[0], w_uq[0], w_ukv[0], w_out[0], w_mlp1[0], w_mlp2[0]), 3)
    mvec = m_all[:, 0, :].reshape(6, D_MODEL)
    mctx = m_all[:, 8, :].reshape(6, D_MODEL)
    zeros6 = jnp.zeros((6, D_MODEL), F32)
    mod_a = jnp.stack([jnp.concatenate([mvec[0:2], zeros6], axis=0), jnp.concatenate([mctx[0:2], zeros6], axis=0)])
    mod_b = jnp.concatenate([mvec[2:6], jnp.zeros((4, D_MODEL), F32)], axis=0)
    cw_full = jnp.pad(jnp.transpose(m_all[:, 9:12, 0:128], (1, 0, 2)).reshape(3, CONV_W), ((0, 5), (0, 0)))

    w_in_f = _cols_from_shards(g_in)
    zc = lambda n: jnp.zeros((D_MODEL, n), BF16)
    w_in_p = jnp.concatenate([w_in_f[:, 0:384], zc(64), w_in_f[:, 384:416], zc(32), w_in_f[:, 416:]], axis=1)
    w_uq_f = _cols_from_shards(g_uq).reshape(Q_RANK, N_HEADS, QK_DIM)
    w_uq_p = jnp.pad(w_uq_f, ((0, 0), (0, 0), (0, HEAD_PAD - QK_DIM))).reshape(Q_RANK, N_HEADS * HEAD_PAD)
    w_ukv_f = _cols_from_shards(g_ukv).reshape(KV_RANK, N_HEADS, QK_NOPE + V_DIM)
    padh = lambda a: jnp.pad(a, ((0, 0), (0, 0), (0, HEAD_PAD - a.shape[2]))).reshape(KV_RANK, N_HEADS * HEAD_PAD)
    w_ukv_p = jnp.concatenate([padh(w_ukv_f[:, :, :QK_NOPE]), padh(w_ukv_f[:, :, QK_NOPE:])], axis=1)
    cos_t, sin_a, sin_b = _rope_tables(t_lat, t_ctx)
    gf_row = final_norm_g.reshape(1, D_MODEL)
    c_idx = mc.reshape(1).astype(jnp.int32)
    j_idx = j.reshape(1).astype(jnp.int32)

    z, q, k, v, kt = _inproj_fwd(x2, ctx2, mod_a, w_in_p, q_norm_g, kv_norm_g, w_uq_p, w_ukv_p, cos_t, sin_a, sin_b)
    o, lse, g_out, w1, g_m2 = _attn_fwd(q, k, v, t_lat, (g_out, g_m1, g_m2))
    w_out_f = g_out.reshape(D_MODEL, D_MODEL)
    w2 = g_m2.reshape(D_FF, D_MODEL)
    r, da, h2, dy2, dx1, conv, acc_mlp, dy1, do, dgb, dyv = _mlp_fwdbwd(o, z, x2, tgt, mod_b, gf_row, cw_full, w_out_f,
                                                                         w1, w2)
    d_w1 = _wgrad(h2, da, "wgrad_mlp1", D_MODEL, FF_CHUNK)
    d_w2 = _wgrad(r, dy2, "wgrad_mlp2", FF_CHUNK, D_MODEL)
    d_wout, *big_got = _wgrad_out(o, conv, dy1, (d_w1, d_w2))
    d_wout = d_wout.reshape(N_SHARD, D_MODEL // N_SHARD, D_MODEL)
    big_grads = (d_w1, d_w2, d_wout)
    big_parts = _add_pairs(big_grads, big_got, c_idx, "rs_add_pairs_big")
    dqt, dk, dv, *big_recv = _attn_bwd(q, k, v, kt, o, do, lse, t_lat, big_parts)
    big_halves = _add_chips(big_parts, big_recv, j_idx, "rs_add_chips_big")
    gx, d_win, d_wuq, d_wukv, acc_in = _inproj_bwd(x2, ctx2, mod_a, z, dyv, dgb, dx1, dqt, dk, dv, cos_t, sin_a, sin_b,
                                                   w_in_p, w_uq_p, w_ukv_p, q_norm_g, kv_norm_g, cw_full)

    d_win_f = jnp.concatenate([d_win[:, 0:384], d_win[:, 448:480], d_win[:, 512:]], axis=1)
    d_wuq_f = d_wuq.reshape(Q_RANK, N_HEADS, HEAD_PAD)[:, :, 0:QK_DIM].reshape(Q_RANK, N_HEADS * QK_DIM)
    d_wukv3 = d_wukv.reshape(KV_RANK, 2, N_HEADS, HEAD_PAD)
    d_wukv_f = jnp.concatenate([d_wukv3[:, 0, :, 0:QK_NOPE], d_wukv3[:, 1, :, 0:V_DIM]], axis=2).reshape(KV_RANK, -1)
    rest = tuple(_cols_to_shards(a).astype(BF16) for a in (d_win_f, d_wuq_f, d_wukv_f))
    rest_got = _rs_sibling(rest, "rs_sibling_rest")
    rest_parts = _add_pairs(rest, rest_got, c_idx, "rs_add_pairs_rest")

    sv = jnp.concatenate([
        acc_in[0:2], acc_mlp[5:6], acc_mlp[2:4], acc_mlp[1:2],
        acc_in[2:4], acc_mlp[0:1], acc_in[4:5], acc_in[5:6], acc_in[6:9], acc_mlp[4:5],
        jnp.zeros((1, D_MODEL), F32)], axis=0)
    all_sv, red, o_cc, o_b, o_q, o_k, o_gf, *exchanged = _small_exchange(
        sv, w_mod[0], cctx_row, m_c_ctx.reshape(1, D_MODEL), v_c_ctx.reshape(1, D_MODEL),
        b_mod.reshape(6, D_MODEL), m_b_mod.reshape(6, D_MODEL), v_b_mod.reshape(6, D_MODEL),
        q_norm_g, m_q_norm_g, v_q_norm_g, kv_norm_g, m_kv_norm_g, v_kv_norm_g,
        gf_row, m_final_norm_g.reshape(1, D_MODEL), v_final_norm_g.reshape(1, D_MODEL), big_halves, rest_parts)
    big_theirs, rest_recv = exchanged[:len(big_halves)], exchanged[len(big_halves):]
    loss = red[14, 0]

    c9 = jnp.concatenate([c8[0::8], jnp.zeros((7, D_MODEL), F32)], axis=0)
    s_t = jnp.transpose(c9 * jax.nn.sigmoid(c9))
    dm_ex = all_sv[:, 0:6, :].reshape(8, 6 * D_MODEL)
    dm_ctx = jnp.concatenate([red[6:8].reshape(1, 2 * D_MODEL), jnp.zeros((1, 4 * D_MODEL), F32)], axis=1)
    dm16 = jnp.concatenate([dm_ex, dm_ctx, jnp.zeros((7, 6 * D_MODEL), F32)], axis=0)
    dm_sh = lax.dynamic_slice(dm16, (0, j * ncol), (16, ncol))
    g_wmod, d_wmod, nm_wmod, nv_wmod = _wmod_update(s_t, dm_sh, w_mod[0], m_w_mod[0], v_w_mod[0])

    g_cw = lax.dynamic_slice(red[11:14, 0:CONV_W], (0, j * 128), (3, 128))
    (d_cw, nm_cw, nv_cw), = _adamw_call([conv_w[0]], [g_cw], [m_conv_w[0]], [v_conv_w[0]], "adamw_conv")

    rest_halves = _add_chips(rest_parts, rest_recv, j_idx, "rs_add_chips_rest")
    g_win, g_wuq, g_wukv = _rs_join(rest_halves, "rs_join_rest")
    upd = dict(zip(("in", "uq", "ukv"), _adamw_call(
        [w_in[0], w_uq[0], w_ukv[0]], [g_win, g_wuq, g_wukv], [m_w_in[0], m_w_uq[0], m_w_ukv[0]],
        [v_w_in[0], v_w_uq[0], v_w_ukv[0]], "adamw_rest")))
    (g_w1, *upd["mlp1"]), (g_w2, *upd["mlp2"]), (g_wout, *upd["out"]) = _adamw_halves(
        [w_mlp1[0], w_mlp2[0], w_out[0]], big_halves, big_theirs, [m_w_mlp1[0], m_w_mlp2[0], m_w_out[0]],
        [v_w_mlp1[0], v_w_mlp2[0], v_w_out[0]], c_idx, "adamw_big")

    def four(o4, shape):
        return [o4[r].reshape(shape) for r in range(4)]

    cc4 = four(o_cc, (D_MODEL,))
    b4 = [o_b[r].reshape(1, 6 * D_MODEL) for r in range(4)]
    q4 = four(o_q, (1, Q_RANK))
    k4 = four(o_k, (1, KV_RANK))
    gf4 = four(o_gf, (D_MODEL,))
    big = {"in": g_win, "uq": g_wuq, "ukv": g_wukv, "out": g_wout, "mlp1": g_w1, "mlp2": g_w2}

    def leaf(idx):
        wm = (g_wmod, d_wmod, nm_wmod, nv_wmod)[idx]
        cwv = (g_cw, d_cw, nm_cw, nv_cw)[idx]
        bigv = {n: (big[n] if idx == 0 else upd[n][idx - 1]) for n in big}
        return [cc4[idx], wm[None], b4[idx], bigv["in"][None], q4[idx], bigv["uq"][None], k4[idx], bigv["ukv"][None],
                cwv[None], bigv["out"][None], bigv["mlp1"][None], bigv["mlp2"][None], gf4[idx]]

    return (loss, gx[None], *leaf(0), *leaf(1), *leaf(2), *leaf(3))
```

```python
import functools
import math

import jax
import jax.numpy as jnp
from jax import lax
from jax.experimental import pallas as pl
from jax.experimental.pallas import tpu as pltpu

F32 = jnp.float32
BF16 = jnp.bfloat16
MESH = pl.DeviceIdType.MESH
HIGHEST = lax.Precision.HIGHEST

D_MODEL = 1024
N_HEADS = 8
QK_NOPE = 64
QK_ROPE = 32
QK_DIM = QK_NOPE + QK_ROPE
V_DIM = 64
Q_RANK = 256
KV_RANK = 128
CONV_W = 512
D_FF = 4096
GRID_W = 64
ROPE_THETA = 10000.0
EPS = 1e-6
ATTN_SCALE = 1.0 / math.sqrt(QK_DIM)
HEAD_PAD = 128
Z_COLS = 2048
ROPE_LANE0 = QK_NOPE
N_SHARD = 4
TOK_TILE = 256
FF_CHUNK = 1024
MLP_FF_CHUNK = 1024
KEY_CHUNK = 1024
ATTN_FWD_Q_BLOCK = 1024
ATTN_HEADS_PER_STEP = 4
ATTN_BWD_HEADS_PER_STEP = 2
ATTN_BWD_Q_BLOCK = 512
KEY_CHUNK_BWD = 512

ADAM_LR = 0.001
ADAM_B1 = 0.9
ADAM_B2 = 0.999
ADAM_EPS = 1e-08
ADAM_WD = 0.01
ADAM_STEP = 10

LOG2E = 1.4426950408889634

VMEM_LIMIT = 56 * 1024 * 1024
STAGE_VMEM_LIMIT = 32 * 1024 * 1024


def _pos():
    return lax.axis_index("x"), lax.axis_index("y"), lax.axis_index("c")


def _dot(a, b):
    return jnp.dot(a, b, preferred_element_type=F32)


def _dot_nt(a, b):
    return lax.dot_general(a, b, (((1,), (1,)), ((), ())), preferred_element_type=F32)


def _dot_tn(a, b):
    return lax.dot_general(a, b, (((0,), (0,)), ((), ())), preferred_element_type=F32)


def _rope(v, cos, sa, sb):
    return v * cos + pltpu.roll(v, 8, 1) * sa + pltpu.roll(v, HEAD_PAD - 8, 1) * sb


def _unrope(g, cos, sa, sb):
    return g * cos + pltpu.roll(g * sa, HEAD_PAD - 8, 1) + pltpu.roll(g * sb, 8, 1)


def _sigmoid(v):
    return 1.0 / (1.0 + jnp.exp(-v))


def _adamw(w, g, m, v):
    m = ADAM_B1 * m + (1.0 - ADAM_B1) * g
    v = ADAM_B2 * v + (1.0 - ADAM_B2) * (g * g)
    m_hat = m / (1.0 - ADAM_B1 ** ADAM_STEP)
    v_hat = v / (1.0 - ADAM_B2 ** ADAM_STEP)
    delta = -ADAM_LR * (m_hat / (jnp.sqrt(v_hat) + ADAM_EPS) + ADAM_WD * w)
    return delta, m, v


def _shift_rows(u, prev_row, next_row):
    n = u.shape[0]
    rows = lax.broadcasted_iota(jnp.int32, u.shape, 0)
    um1 = jnp.where(rows == 0, prev_row, pltpu.roll(u, 1, 0))
    up1 = jnp.where(rows == n - 1, next_row, pltpu.roll(u, n - 1, 0))
    return um1, up1


def _const_spec(shape):
    nd = len(shape)
    return pl.BlockSpec(shape, lambda *_: (0,) * nd)


def _resident_spec(shape):
    nd = len(shape)
    return pl.BlockSpec(shape, lambda *_: (0,) * nd, pipeline_mode=pl.Buffered(1))


def _peer(r, x, y, c):
    px = 1 - x if r & 4 else x
    py = 1 - y if r & 2 else y
    pc = 1 - c if r & 1 else c
    return (px, py, pc)


def _prologue(c_row, cctx_row, w_mod_sh, b_sh, cw_sh, srcs, n_gather):
    ncol = w_mod_sh.shape[1]
    n = len(srcs)
    n_split = 4

    def body(c_ref, cctx_ref, w_ref, b_ref, cw_ref, *refs):
        ins, (c8_ref, m_ref), outs = refs[:n], refs[n:n + 2], refs[n + 2:2 * n + 2]
        mine_ref, msh_ref = refs[2 * n + 2:2 * n + 4]
        f32s, bfs = refs[2 * n + 4:3 * n + 4], refs[3 * n + 4:4 * n + 4]
        ssem, rsem, ssem2, rsem2, lsem_in, lsem_out = refs[4 * n + 4:4 * n + 10]
        x, y, c = _pos()
        me = 4 * x + 2 * y + c
        j = 2 * x + y

        def pieces(rows):
            step = rows // n_split
            return [pl.ds(q * step, step) for q in range(n_split)]

        for t in range(n):
            for sl in pieces(ins[t].shape[0]):
                pltpu.make_async_copy(ins[t].at[sl], f32s[t].at[sl], lsem_in.at[t]).start()
        mine_ref[...] = jnp.zeros(mine_ref.shape, F32)
        mine_ref[0:1, :] = c_ref[...]
        my_rows = pl.ds(pl.multiple_of(8 * me, 8), 8)
        sends = []
        for r in range(1, 8):
            cp = pltpu.make_async_remote_copy(
                src_ref=mine_ref, dst_ref=c8_ref.at[my_rows], send_sem=ssem.at[r - 1], recv_sem=rsem.at[r - 1],
                device_id=_peer(r, x, y, c), device_id_type=MESH)
            cp.start()
            sends.append(cp)

        def cast_and_store(t):
            pltpu.make_async_copy(ins[t], f32s[t], lsem_in.at[t]).wait()
            bfs[t][...] = f32s[t][...].astype(BF16)
            for sl in pieces(ins[t].shape[0]):
                pltpu.make_async_copy(bfs[t].at[sl], outs[t].at[j, sl], lsem_out.at[t]).start()

        gather = _ShardGather(outs[:n_gather], *refs[4 * n + 10:])
        for t in range(n_gather):
            cast_and_store(t)
        for t in range(n_gather):
            pltpu.make_async_copy(bfs[t], outs[t].at[j], lsem_out.at[t]).wait()
        gather.start()
        for cp in sends:
            cp.wait()
        c8_ref[my_rows, :] = mine_ref[...]
        c8_ref[64:72, :] = jnp.zeros((8, D_MODEL), F32)
        c8_ref[64:65, :] = cctx_ref[...]
        cv = c8_ref[...]
        s = cv * _sigmoid(cv)
        m = jnp.dot(s, w_ref[...], precision=HIGHEST, preferred_element_type=F32) + b_ref[...]
        msh_ref[0:64, :] = m[0:64, :]
        msh_ref[64:72, :] = jnp.zeros((8, ncol), F32)
        msh_ref[64:65, :] = m[64:65, :]
        msh_ref[65:68, 0:128] = cw_ref[0:3, :]
        m_ref[j, 0:8, :] = msh_ref[my_rows, :]
        m_ref[j, 8:16, :] = msh_ref[64:72, :]
        sends2 = []
        for k, (px, py) in enumerate(_chips(x, y)):
            theirs = pl.ds(pl.multiple_of(8 * (4 * px + 2 * py + c), 8), 8)
            for half, src in enumerate((msh_ref.at[theirs], msh_ref.at[64:72])):
                cp = pltpu.make_async_remote_copy(
                    src_ref=src, dst_ref=m_ref.at[j, 8 * half:8 * half + 8], send_sem=ssem2.at[2 * k + half],
                    recv_sem=rsem2.at[2 * k + half], device_id=(px, py, c), device_id_type=MESH)
                cp.start()
                sends2.append(cp)
        for t in range(n_gather, n):
            cast_and_store(t)
        gather.forward()
        gather.finish()
        for t in range(n_gather, n):
            pltpu.make_async_copy(bfs[t], outs[t].at[j], lsem_out.at[t]).wait()
        for cp in sends2:
            cp.wait()

    vm = pl.BlockSpec(memory_space=pltpu.VMEM)
    hbm = pl.BlockSpec(memory_space=pl.ANY)
    return pl.pallas_call(
        body, name="prologue",
        out_shape=(jax.ShapeDtypeStruct((72, D_MODEL), F32), jax.ShapeDtypeStruct((N_SHARD, 16, ncol), F32))
        + tuple(jax.ShapeDtypeStruct((N_SHARD,) + a.shape, BF16) for a in srcs),
        in_specs=[vm] * 5 + [hbm] * n, out_specs=(vm, vm) + (hbm,) * n,
        scratch_shapes=[pltpu.VMEM((8, D_MODEL), F32), pltpu.VMEM((72, ncol), F32)]
        + [pltpu.VMEM(a.shape, F32) for a in srcs] + [pltpu.VMEM(a.shape, BF16) for a in srcs]
        + [pltpu.SemaphoreType.DMA((7,)), pltpu.SemaphoreType.DMA((7,)),
           pltpu.SemaphoreType.DMA((6,)), pltpu.SemaphoreType.DMA((6,)),
           pltpu.SemaphoreType.DMA((n,)), pltpu.SemaphoreType.DMA((n,))] + _gather_sems(n_gather),
        compiler_params=pltpu.CompilerParams(vmem_limit_bytes=VMEM_LIMIT),
    )(c_row, cctx_row, w_mod_sh, b_sh, cw_sh, *srcs)


def _chips(x, y):
    return [(1 - x, y), (x, 1 - y), (1 - x, 1 - y)]


def _halves(ref, c, align):
    hr = ref.shape[-2] // 2
    return (pl.ds(pl.multiple_of(c * hr, align), hr), pl.ds(pl.multiple_of((1 - c) * hr, align), hr))


class _ShardGather:
    def __init__(self, refs, ssem, rsem, fsend, frecv):
        self.refs, self.sems = refs, (ssem, rsem, fsend, frecv)
        self.x, self.y, self.c = _pos()
        self.j = 2 * self.x + self.y

    def _ici(self, a, k, slot):
        g = self.refs[a]
        ssem, rsem, _, _ = self.sems
        mine, _ = _halves(g, self.c, 16)
        px, py = _chips(self.x, self.y)[k]
        return pltpu.make_async_remote_copy(
            src_ref=g.at[self.j, mine], dst_ref=g.at[slot, mine], send_sem=ssem.at[3 * a + k],
            recv_sem=rsem.at[3 * a + k], device_id=(px, py, self.c), device_id_type=MESH)

    def _d2d(self, a, k, to_other_half):
        g = self.refs[a]
        _, _, fsend, frecv = self.sems
        mine, theirs = _halves(g, self.c, 16)
        px, py = _chips(self.x, self.y)[k]
        jk = 2 * px + py
        return pltpu.make_async_remote_copy(
            src_ref=g.at[jk, mine], dst_ref=g.at[jk, theirs if to_other_half else mine],
            send_sem=fsend.at[3 * a + k], recv_sem=frecv.at[3 * a + k],
            device_id=(self.x, self.y, 1 - self.c), device_id_type=MESH)

    def start(self):
        for a in range(len(self.refs)):
            for k in range(3):
                self._ici(a, k, self.j).start()

    def forward(self):
        for a in range(len(self.refs)):
            for k, (px, py) in enumerate(_chips(self.x, self.y)):
                self._ici(a, k, 2 * px + py).wait_recv()
                self._d2d(a, k, False).start()

    def finish(self):
        for a in range(len(self.refs)):
            for k in range(3):
                self._d2d(a, k, True).wait()
                self._ici(a, k, self.j).wait_send()


def _gather_sems(n_arrays):
    return [pltpu.SemaphoreType.DMA((3 * n_arrays,)) for _ in range(4)]


class _SiblingSend:
    def __init__(self, g_refs, got_refs, ssem, rsem):
        self.g_refs, self.got_refs, self.ssem, self.rsem = g_refs, got_refs, ssem, rsem
        self.x, self.y, self.c = _pos()

    def _copy(self, a, shard):
        _, theirs = _halves(self.g_refs[a], self.c, 16)
        src = self.g_refs[a].at[:, theirs] if shard is None else self.g_refs[a].at[shard, theirs]
        dst = self.got_refs[a] if shard is None else self.got_refs[a].at[shard]
        return pltpu.make_async_remote_copy(
            src_ref=src, dst_ref=dst, send_sem=self.ssem.at[a], recv_sem=self.rsem.at[a],
            device_id=(self.x, self.y, 1 - self.c), device_id_type=MESH)

    def start(self):
        for a in range(len(self.g_refs)):
            for s in range(N_SHARD):
                self._copy(a, s).start()

    def finish(self):
        for a in range(len(self.g_refs)):
            self._copy(a, None).wait()


class _SiblingSwap:
    def __init__(self, h_refs, t_refs, ssem, rsem):
        self.h_refs, self.t_refs, self.ssem, self.rsem = h_refs, t_refs, ssem, rsem
        self.x, self.y, self.c = _pos()

    def _copy(self, a):
        return pltpu.make_async_remote_copy(
            src_ref=self.h_refs[a], dst_ref=self.t_refs[a], send_sem=self.ssem.at[a], recv_sem=self.rsem.at[a],
            device_id=(self.x, self.y, 1 - self.c), device_id_type=MESH)

    def start(self):
        for a in range(len(self.h_refs)):
            self._copy(a).start()

    def finish(self):
        for a in range(len(self.h_refs)):
            self._copy(a).wait()


def _rs_sibling(arrs, name):
    n = len(arrs)

    def body(*refs):
        send = _SiblingSend(refs[:n], refs[n:2 * n], refs[2 * n], refs[2 * n + 1])
        send.start()
        send.finish()

    hbm = pl.BlockSpec(memory_space=pl.ANY)
    return pl.pallas_call(
        body, name=name,
        out_shape=tuple(jax.ShapeDtypeStruct((N_SHARD, a.shape[1] // 2, a.shape[2]), BF16) for a in arrs),
        in_specs=[hbm] * n, out_specs=(hbm,) * n,
        scratch_shapes=[pltpu.SemaphoreType.DMA((n,)), pltpu.SemaphoreType.DMA((n,))],
    )(*arrs)


class _ChipScatter:
    def __init__(self, parts, gots, ssem, rsem):
        self.parts, self.gots, self.ssem, self.rsem = parts, gots, ssem, rsem
        self.x, self.y, self.c = _pos()

    def _copy(self, a, k):
        px, py = _chips(self.x, self.y)[k]
        return pltpu.make_async_remote_copy(
            src_ref=self.parts[a].at[2 * px + py], dst_ref=self.gots[a].at[k], send_sem=self.ssem.at[3 * a + k],
            recv_sem=self.rsem.at[3 * a + k], device_id=(px, py, self.c), device_id_type=MESH)

    def start(self):
        for a in range(len(self.parts)):
            for k in range(3):
                self._copy(a, k).start()

    def finish(self):
        for a in range(len(self.parts)):
            for k in range(3):
                self._copy(a, k).wait()


def _rs_join(halves, name):
    n = len(halves)

    def body(*refs):
        h_refs, f_refs, stages = refs[:n], refs[n:2 * n], refs[2 * n:3 * n]
        lsem_in, lsem_out, ssem, rsem = refs[3 * n:]
        x, y, c = _pos()
        remote = []
        for a in range(n):
            mine, _ = _halves(f_refs[a], c, 8)
            cp = pltpu.make_async_remote_copy(
                src_ref=h_refs[a], dst_ref=f_refs[a].at[mine], send_sem=ssem.at[a], recv_sem=rsem.at[a],
                device_id=(x, y, 1 - c), device_id_type=MESH)
            cp.start()
            remote.append(cp)
            pltpu.make_async_copy(h_refs[a], stages[a], lsem_in.at[a]).start()
        local = []
        for a in range(n):
            mine, _ = _halves(f_refs[a], c, 8)
            pltpu.make_async_copy(h_refs[a], stages[a], lsem_in.at[a]).wait()
            cp = pltpu.make_async_copy(stages[a], f_refs[a].at[mine], lsem_out.at[a])
            cp.start()
            local.append(cp)
        for cp in remote + local:
            cp.wait()

    hbm = pl.BlockSpec(memory_space=pl.ANY)
    return pl.pallas_call(
        body, name=name,
        out_shape=tuple(jax.ShapeDtypeStruct((2 * h.shape[0], h.shape[1]), F32) for h in halves),
        in_specs=[hbm] * n, out_specs=(hbm,) * n,
        scratch_shapes=[pltpu.VMEM(h.shape, F32) for h in halves]
        + [pltpu.SemaphoreType.DMA((n,)) for _ in range(4)],
        compiler_params=pltpu.CompilerParams(vmem_limit_bytes=STAGE_VMEM_LIMIT),
    )(*halves)


def _add_pairs(arrs, gots, c_idx, name):
    n = len(arrs)

    def body(c_ref, *refs):
        for a in range(n):
            refs[2 * n + a][...] = (refs[a][...].astype(F32) + refs[n + a][...].astype(F32)).astype(BF16)

    def half_spec(g, mine):
        hr, cols = g.shape[1], g.shape[2]
        if mine:
            return pl.BlockSpec((1, hr, cols), lambda s, c_ref: (s, c_ref[0], 0))
        return pl.BlockSpec((1, hr, cols), lambda s, c_ref: (s, 0, 0))

    grid_spec = pltpu.PrefetchScalarGridSpec(
        num_scalar_prefetch=1, grid=(N_SHARD,),
        in_specs=[half_spec(g, True) for g in gots] + [half_spec(g, False) for g in gots],
        out_specs=tuple(half_spec(g, False) for g in gots))
    return pl.pallas_call(
        body, name=name, grid_spec=grid_spec, out_shape=tuple(jax.ShapeDtypeStruct(g.shape, BF16) for g in gots),
        compiler_params=pltpu.CompilerParams(vmem_limit_bytes=STAGE_VMEM_LIMIT),
    )(c_idx, *arrs, *gots)


def _add_chips(parts, gots, j_idx, name):
    n = len(parts)
    n_split = 2

    def body(j_ref, *refs):
        for a in range(n):
            acc = refs[a][0].astype(F32)
            for k in range(3):
                acc = acc + refs[n + a][k].astype(F32)
            refs[2 * n + a][...] = acc

    in_specs, out_specs = [], []
    for g in gots:
        rb, cols = g.shape[1] // n_split, g.shape[2]
        in_specs.append(pl.BlockSpec((1, rb, cols), lambda r, j_ref: (j_ref[0], r, 0)))
        out_specs.append(pl.BlockSpec((rb, cols), lambda r, j_ref: (r, 0)))
    for g in gots:
        rb, cols = g.shape[1] // n_split, g.shape[2]
        in_specs.append(pl.BlockSpec((3, rb, cols), lambda r, j_ref: (0, r, 0)))
    grid_spec = pltpu.PrefetchScalarGridSpec(
        num_scalar_prefetch=1, grid=(n_split,), in_specs=in_specs, out_specs=tuple(out_specs))
    return pl.pallas_call(
        body, name=name, grid_spec=grid_spec,
        out_shape=tuple(jax.ShapeDtypeStruct(g.shape[1:], F32) for g in gots),
        compiler_params=pltpu.CompilerParams(vmem_limit_bytes=STAGE_VMEM_LIMIT),
    )(j_idx, *parts, *gots)


def _small_exchange(sv, w_mod_sh, cctx, m_cctx, v_cctx, bmod, m_bmod, v_bmod, qg, m_qg, v_qg, kvg, m_kvg, v_kvg,
                    gf, m_gf, v_gf, swap_halves, scatter_parts):
    ncol = w_mod_sh.shape[1]
    n_s, n_p = len(swap_halves), len(scatter_parts)
    n_x = n_s + n_p

    def body(sv_ref, w_ref, cctx_ref, mcc_ref, vcc_ref, b_ref, mb_ref, vb_ref, qg_ref, mq_ref, vq_ref,
             kg_ref, mk_ref, vk_ref, gf_ref, mgf_ref, vgf_ref, *rest):
        all_ref, red_ref, occ_ref, ob_ref, oq_ref, ok_ref, ogf_ref = rest[n_x:n_x + 7]
        vec_ref, part_ref, ssem, rsem, ssem2, rsem2, wsend, wrecv, psend, precv = rest[2 * n_x + 7:]
        swap = _SiblingSwap(rest[:n_s], rest[n_x + 7:n_x + 7 + n_s], wsend, wrecv)
        scatter = _ChipScatter(rest[n_s:n_x], rest[n_x + 7 + n_s:2 * n_x + 7], psend, precv)
        swap.start()
        scatter.start()
        x, y, c = _pos()
        me = 4 * x + 2 * y + c
        j = 2 * x + y
        sends = []
        for r in range(1, 8):
            cp = pltpu.make_async_remote_copy(
                src_ref=sv_ref, dst_ref=all_ref.at[me], send_sem=ssem.at[r - 1], recv_sem=rsem.at[r - 1],
                device_id=_peer(r, x, y, c), device_id_type=MESH)
            cp.start()
            sends.append(cp)
        for cp in sends:
            cp.wait()
        all_ref[me] = sv_ref[...]
        red = all_ref[0]
        for d in range(1, 8):
            red = red + all_ref[d]
        red_ref[...] = red
        vec_ref[...] = jnp.zeros(vec_ref.shape, F32)

        @pl.when(j == 0)
        def _():
            vec_ref[0:1, 0:1024] = red[6:7, :]
            vec_ref[0:1, 1024:1536] = red[7:8, 0:512]

        @pl.when(j == 1)
        def _():
            vec_ref[0:1, 0:512] = red[7:8, 512:1024]

        part = lax.dot_general(vec_ref[...], w_ref[...], (((1,), (1,)), ((), ())), precision=HIGHEST,
                               preferred_element_type=F32)
        part_ref[j] = part
        sends2 = []
        for k, r in enumerate((4, 2, 6)):
            cp = pltpu.make_async_remote_copy(
                src_ref=part_ref.at[j], dst_ref=part_ref.at[j], send_sem=ssem2.at[k], recv_sem=rsem2.at[k],
                device_id=_peer(r, x, y, c), device_id_type=MESH)
            cp.start()
            sends2.append(cp)
        for cp in sends2:
            cp.wait()
        tot = part_ref[0]
        for s in range(1, N_SHARD):
            tot = tot + part_ref[s]
        cc = cctx_ref[...]
        sg = _sigmoid(cc)
        g_cc = tot[0:1, :] * (sg * (1.0 + cc * (1.0 - sg)))
        d_, m_, v_ = _adamw(cc, g_cc, mcc_ref[...], vcc_ref[...])
        occ_ref[0:1, :] = g_cc
        occ_ref[1:2, :] = d_
        occ_ref[2:3, :] = m_
        occ_ref[3:4, :] = v_
        occ_ref[4:8, :] = jnp.zeros((4, D_MODEL), F32)
        g_b = red[0:6, :]
        pad = jnp.concatenate([red[6:8, :], jnp.zeros((4, D_MODEL), F32)], axis=0)
        g_b = g_b + pad
        d_, m_, v_ = _adamw(b_ref[...], g_b, mb_ref[...], vb_ref[...])
        ob_ref[0] = g_b
        ob_ref[1] = d_
        ob_ref[2] = m_
        ob_ref[3] = v_
        g_q = red[9:10, 0:Q_RANK]
        d_, m_, v_ = _adamw(qg_ref[...], g_q, mq_ref[...], vq_ref[...])
        oq_ref[0:1, :] = g_q
        oq_ref[1:2, :] = d_
        oq_ref[2:3, :] = m_
        oq_ref[3:4, :] = v_
        oq_ref[4:8, :] = jnp.zeros((4, Q_RANK), F32)
        g_k = red[10:11, 0:KV_RANK]
        d_, m_, v_ = _adamw(kg_ref[...], g_k, mk_ref[...], vk_ref[...])
        ok_ref[0:1, :] = g_k
        ok_ref[1:2, :] = d_
        ok_ref[2:3, :] = m_
        ok_ref[3:4, :] = v_
        ok_ref[4:8, :] = jnp.zeros((4, KV_RANK), F32)
        g_f = red[8:9, :]
        d_, m_, v_ = _adamw(gf_ref[...], g_f, mgf_ref[...], vgf_ref[...])
        ogf_ref[0:1, :] = g_f
        ogf_ref[1:2, :] = d_
        ogf_ref[2:3, :] = m_
        ogf_ref[3:4, :] = v_
        ogf_ref[4:8, :] = jnp.zeros((4, D_MODEL), F32)
        swap.finish()
        scatter.finish()

    vm = pl.BlockSpec(memory_space=pltpu.VMEM)
    hbm = pl.BlockSpec(memory_space=pl.ANY)
    out_shape = (
        jax.ShapeDtypeStruct((8, 16, D_MODEL), F32),
        jax.ShapeDtypeStruct((16, D_MODEL), F32),
        jax.ShapeDtypeStruct((8, D_MODEL), F32),
        jax.ShapeDtypeStruct((4, 6, D_MODEL), F32),
        jax.ShapeDtypeStruct((8, Q_RANK), F32),
        jax.ShapeDtypeStruct((8, KV_RANK), F32),
        jax.ShapeDtypeStruct((8, D_MODEL), F32),
    ) + tuple(jax.ShapeDtypeStruct(h.shape, F32) for h in swap_halves) + tuple(
        jax.ShapeDtypeStruct((3,) + p.shape[1:], BF16) for p in scatter_parts)
    return pl.pallas_call(
        body, name="small_exchange", out_shape=out_shape, in_specs=[vm] * 17 + [hbm] * n_x,
        out_specs=tuple([vm] * 7) + (hbm,) * n_x,
        scratch_shapes=[pltpu.VMEM((8, ncol), F32), pltpu.VMEM((N_SHARD, 8, D_MODEL), F32),
                        pltpu.SemaphoreType.DMA((7,)), pltpu.SemaphoreType.DMA((7,)),
                        pltpu.SemaphoreType.DMA((3,)), pltpu.SemaphoreType.DMA((3,)),
                        pltpu.SemaphoreType.DMA((n_s,)), pltpu.SemaphoreType.DMA((n_s,)),
                        pltpu.SemaphoreType.DMA((3 * n_p,)), pltpu.SemaphoreType.DMA((3 * n_p,))],
        compiler_params=pltpu.CompilerParams(vmem_limit_bytes=VMEM_LIMIT),
    )(sv, w_mod_sh, cctx, m_cctx, v_cctx, bmod, m_bmod, v_bmod, qg, m_qg, v_qg, kvg, m_kvg, v_kvg, gf, m_gf, v_gf,
      *swap_halves, *scatter_parts)


def _inproj_fwd(x2, ctx2, mod_a, w_in, qg, kvg, w_uq, w_ukv, cos_t, sin_a, sin_b):
    t_lat, t_ctx = x2.shape[0], ctx2.shape[0]
    tm = TOK_TILE
    n_lat = t_lat // tm
    n_all = n_lat + t_ctx // tm
    e_rows = t_lat + t_ctx

    def body(x_ref, ctx_ref, mod_ref, win_ref, qg_ref, kvg_ref, wuq_ref, wukv_ref, cos_ref, sa_ref, sb_ref,
             z_ref, q_ref, k_ref, v_ref, kt_ref):
        i = pl.program_id(0)
        xin = jnp.where(i < n_lat, x_ref[...], ctx_ref[...])
        xn = xin * lax.rsqrt(jnp.mean(xin * xin, axis=-1, keepdims=True) + EPS)
        h1 = (xn * (1.0 + mod_ref[0, 1:2, :]) + mod_ref[0, 0:1, :]).astype(BF16)
        z = _dot(h1, win_ref[...])
        z_ref[...] = z
        cos, sa, sb = cos_ref[...], sa_ref[...], sb_ref[...]
        cq = z[:, 0:Q_RANK]
        cqn = (cq * lax.rsqrt(jnp.mean(cq * cq, axis=-1, keepdims=True) + EPS) * qg_ref[...]).astype(BF16)
        q = _dot(cqn, wuq_ref[...])
        ckv = z[:, Q_RANK:Q_RANK + KV_RANK]
        ckvn = (ckv * lax.rsqrt(jnp.mean(ckv * ckv, axis=-1, keepdims=True) + EPS) * kvg_ref[...]).astype(BF16)
        kv = _dot(ckvn, wukv_ref[...])
        kr = _rope(z[:, Q_RANK + KV_RANK:Q_RANK + KV_RANK + HEAD_PAD], cos, sa, sb)
        ones_lane = lax.broadcasted_iota(jnp.int32, (tm, HEAD_PAD), 1) == V_DIM
        for h in range(N_HEADS):
            lo = h * HEAD_PAD
            q_ref[h] = _rope(q[:, lo:lo + HEAD_PAD], cos, sa, sb).astype(BF16)
            kh = kv[:, lo:lo + HEAD_PAD] + kr
            k_ref[h] = kh.astype(BF16)
            kt_ref[h] = kh.T.astype(BF16)
            vh = kv[:, N_HEADS * HEAD_PAD + lo:N_HEADS * HEAD_PAD + lo + HEAD_PAD]
            v_ref[h] = jnp.where(ones_lane, 1.0, vh).astype(BF16)

    row = lambda i: (i, 0)
    head_spec = pl.BlockSpec((N_HEADS, tm, HEAD_PAD), lambda i: (0, i, 0))
    head_shape = jax.ShapeDtypeStruct((N_HEADS, e_rows, HEAD_PAD), BF16)
    return pl.pallas_call(
        body, name="inproj_fwd", grid=(n_all,),
        out_shape=(jax.ShapeDtypeStruct((e_rows, Z_COLS), F32), head_shape, head_shape, head_shape,
                   jax.ShapeDtypeStruct((N_HEADS, HEAD_PAD, e_rows), BF16)),
        in_specs=[
            pl.BlockSpec((tm, D_MODEL), lambda i: (jnp.minimum(i, n_lat - 1), 0)),
            _const_spec((tm, D_MODEL)),
            pl.BlockSpec((1, 8, D_MODEL), lambda i: (i // n_lat, 0, 0)),
            _const_spec(w_in.shape), _const_spec(qg.shape), _const_spec(kvg.shape),
            _const_spec(w_uq.shape), _const_spec(w_ukv.shape),
            pl.BlockSpec((tm, HEAD_PAD), row), pl.BlockSpec((tm, HEAD_PAD), row), pl.BlockSpec((tm, HEAD_PAD), row),
        ],
        out_specs=(pl.BlockSpec((tm, Z_COLS), row), head_spec, head_spec, head_spec,
                   pl.BlockSpec((N_HEADS, HEAD_PAD, tm), lambda i: (0, 0, i))),
        compiler_params=pltpu.CompilerParams(vmem_limit_bytes=VMEM_LIMIT),
    )(x2, ctx2, mod_a, w_in, qg, kvg, w_uq, w_ukv, cos_t, sin_a, sin_b)


def _key_chunks(e_rows, size):
    n_chunks = max(1, e_rows // size)
    return [(ci * size, size if ci < n_chunks - 1 else e_rows - ci * size) for ci in range(n_chunks)]


def _attn_fwd(q, k, v, t_lat, shard_arrays):
    e_rows = k.shape[1]
    tq = min(t_lat, ATTN_FWD_Q_BLOCK)
    bounds = _key_chunks(e_rows, KEY_CHUNK)
    c2 = ATTN_SCALE * LOG2E

    hb = ATTN_HEADS_PER_STEP
    n_hb = N_HEADS // hb

    def body(q_ref, k_ref, v_ref, o_ref, lse_ref):
        qs = [q_ref[b] for b in range(hb)]
        m, acc = [None] * hb, [None] * hb
        for lo, n in bounds:
            for b in range(hb):
                s = _dot_nt(qs[b], k_ref[b, lo:lo + n, :])
                mc = jnp.max(s, axis=-1, keepdims=True)
                m_new = mc if m[b] is None else jnp.maximum(m[b], mc)
                p = jnp.exp2((s - m_new) * c2)
                pv = _dot(p.astype(BF16), v_ref[b, lo:lo + n, :])
                acc[b] = pv if m[b] is None else acc[b] * jnp.exp2((m[b] - m_new) * c2) + pv
                m[b] = m_new
        outs = []
        for b in range(hb):
            l = acc[b][:, V_DIM:V_DIM + 1]
            outs.append(acc[b] * (1.0 / l))
            lse = (m[b] * ATTN_SCALE + jnp.log(l)) * LOG2E
            lse_ref[b] = jnp.broadcast_to(lse, (tq, HEAD_PAD)).T[0:1, :]
        low = lax.broadcasted_iota(jnp.int32, (tq, HEAD_PAD), 1) < V_DIM
        pairs = [jnp.where(low, outs[b], pltpu.roll(outs[b + 1], V_DIM, 1)) for b in range(0, hb, 2)]
        o_ref[...] = jnp.concatenate(pairs, axis=1).astype(BF16)

    n_w = len(shard_arrays)
    n_q = t_lat // tq

    def body_with_gather(q_ref, k_ref, v_ref, *rest):
        o_ref, lse_ref = rest[n_w], rest[n_w + 1]
        gather = _ShardGather(rest[n_w + 2:2 * n_w + 2], *rest[2 * n_w + 2:])
        step = pl.program_id(0) * n_q + pl.program_id(1)
        pl.when(step == 0)(gather.start)
        pl.when(step == n_hb * n_q // 2)(gather.forward)
        body(q_ref, k_ref, v_ref, o_ref, lse_ref)
        pl.when(step == n_hb * n_q - 1)(gather.finish)

    hbm = pl.BlockSpec(memory_space=pl.ANY)
    return pl.pallas_call(
        body_with_gather, name="attn_fwd", grid=(n_hb, n_q),
        out_shape=(jax.ShapeDtypeStruct((t_lat, N_HEADS * V_DIM), BF16),
                   jax.ShapeDtypeStruct((N_HEADS, 1, t_lat), F32))
        + tuple(jax.ShapeDtypeStruct(a.shape, a.dtype) for a in shard_arrays),
        in_specs=[pl.BlockSpec((hb, tq, HEAD_PAD), lambda h, i: (h, i, 0)),
                  pl.BlockSpec((hb, e_rows, HEAD_PAD), lambda h, i: (h, 0, 0)),
                  pl.BlockSpec((hb, e_rows, HEAD_PAD), lambda h, i: (h, 0, 0))] + [hbm] * n_w,
        out_specs=(pl.BlockSpec((tq, hb * V_DIM), lambda h, i: (i, h)),
                   pl.BlockSpec((hb, 1, tq), lambda h, i: (h, 0, i))) + (hbm,) * n_w,
        input_output_aliases={3 + a: 2 + a for a in range(n_w)},
        scratch_shapes=_gather_sems(n_w),
        compiler_params=pltpu.CompilerParams(vmem_limit_bytes=VMEM_LIMIT),
    )(q, k, v, *shard_arrays)


def _attn_bwd(q, k, v, kt, o, do, lse_row, t_lat, parts):
    e_rows = k.shape[1]
    tq = min(t_lat, ATTN_BWD_Q_BLOCK)
    n_p = len(parts)
    n_q = t_lat // tq
    bounds = _key_chunks(e_rows, KEY_CHUNK_BWD)

    hb = ATTN_BWD_HEADS_PER_STEP
    assert hb * V_DIM == HEAD_PAD, "a step's heads share one lane tile of the unpadded o / dO"
    n_hb = N_HEADS // hb

    def body(q_ref, k_ref, v_ref, kt_ref, o_ref, do_ref, lse_ref, *rest):
        dqt_ref, dk_ref, dv_ref = rest[n_p:n_p + 3]
        scatter = _ChipScatter(rest[:n_p], rest[n_p + 3:2 * n_p + 3], rest[2 * n_p + 3], rest[2 * n_p + 4])
        h, i = pl.program_id(0), pl.program_id(1)
        pl.when(jnp.logical_and(h == 0, i == 0))(scatter.start)

        @pl.when(i == 0)
        def _():
            dk_ref[...] = jnp.zeros(dk_ref.shape, F32)
            dv_ref[...] = jnp.zeros(dv_ref.shape, F32)

        do_pair = do_ref[...].astype(F32)
        prod = o_ref[...].astype(F32) * do_pair
        lane = lax.broadcasted_iota(jnp.int32, (tq, HEAD_PAD), 1)
        sel = lax.broadcasted_iota(jnp.int32, (8, HEAD_PAD), 1)
        qs, dos, lses, deltas = [], [], [], []
        for b in range(hb):
            qs.append(q_ref[b])
            mine = do_pair if b == 0 else pltpu.roll(do_pair, V_DIM, 1)
            dos.append(jnp.where(lane < V_DIM, mine, 0.0).astype(BF16))
            lses.append(lse_ref[b])
            ones = jnp.where((sel < V_DIM) == (b == 0), 1.0, 0.0)
            deltas.append(lax.dot_general(ones, prod, (((1,), (1,)), ((), ())), precision=HIGHEST,
                                          preferred_element_type=F32)[0:1, :])
        dqt = [None] * hb
        for lo, n in bounds:
            for b in range(hb):
                pt = jnp.exp2(_dot_nt(k_ref[b, lo:lo + n, :], qs[b]) * (ATTN_SCALE * LOG2E) - lses[b])
                dpt = _dot_nt(v_ref[b, lo:lo + n, :], dos[b])
                dst = (pt * (dpt - deltas[b])).astype(BF16)
                dv_c = _dot(pt.astype(BF16), dos[b])
                dk_c = _dot(dst, qs[b])
                part = _dot(kt_ref[b, :, lo:lo + n], dst)
                dqt[b] = part if dqt[b] is None else dqt[b] + part
                dk_ref[b, lo:lo + n, :] += dk_c * ATTN_SCALE
                dv_ref[b, lo:lo + n, :] += dv_c
        for b in range(hb):
            dqt_ref[b] = (dqt[b] * ATTN_SCALE).T

        pl.when(jnp.logical_and(h == n_hb - 1, i == n_q - 1))(scatter.finish)

    hbm = pl.BlockSpec(memory_space=pl.ANY)
    qspec = pl.BlockSpec((hb, tq, HEAD_PAD), lambda h, i: (h, i, 0))
    kspec = pl.BlockSpec((hb, e_rows, HEAD_PAD), lambda h, i: (h, 0, 0))
    pair = pl.BlockSpec((tq, hb * V_DIM), lambda h, i: (i, h))
    return pl.pallas_call(
        body, name="attn_bwd", grid=(n_hb, n_q),
        out_shape=(jax.ShapeDtypeStruct((N_HEADS, t_lat, HEAD_PAD), F32),
                   jax.ShapeDtypeStruct((N_HEADS, e_rows, HEAD_PAD), F32),
                   jax.ShapeDtypeStruct((N_HEADS, e_rows, HEAD_PAD), F32))
        + tuple(jax.ShapeDtypeStruct((3,) + p.shape[1:], BF16) for p in parts),
        in_specs=[qspec, kspec, kspec, pl.BlockSpec((hb, HEAD_PAD, e_rows), lambda h, i: (h, 0, 0)), pair, pair,
                  pl.BlockSpec((hb, 1, tq), lambda h, i: (h, 0, i))] + [hbm] * n_p,
        out_specs=(qspec, kspec, kspec) + (hbm,) * n_p,
        scratch_shapes=[pltpu.SemaphoreType.DMA((3 * n_p,)), pltpu.SemaphoreType.DMA((3 * n_p,))],
        compiler_params=pltpu.CompilerParams(vmem_limit_bytes=VMEM_LIMIT),
    )(q, k, v, kt, o, do, lse_row, *parts)


def _halo_specs(tm, col_block):
    per = tm // 8
    prev = pl.BlockSpec((8, CONV_W), lambda i: (jnp.maximum(i * per - 1, 0), col_block))
    nxt = pl.BlockSpec((8, CONV_W), lambda i: ((i + 1) * per, col_block))
    return prev, nxt


def _mlp_fwdbwd(o, z, x2, tgt, mod_b, gf, cw, w_out, w1, w2):
    t_lat = x2.shape[0]
    tm = TOK_TILE
    n_lat = t_lat // tm
    fc = MLP_FF_CHUNK
    n_ff = D_FF // fc

    def body(o_ref, gb_ref, gc_ref, xi_ref, gcp_ref, xip_ref, gcn_ref, xin_ref, cw_ref, wout_ref,
             x_ref, t_ref, mod_ref, gf_ref, w1_ref, w2_ref,
             r_ref, da_ref, h2_ref, dy2_ref, dx1_ref, conv_ref, acc_ref, dy1_ref, do_ref, dgb_ref, dyv_ref, ra_ref):
        i = pl.program_id(0)

        @pl.when(i == 0)
        def _():
            acc_ref[...] = jnp.zeros(acc_ref.shape, F32)

        g1, sh2, sc2, g2 = mod_ref[0:1, :], mod_ref[1:2, :], mod_ref[2:3, :], mod_ref[3:4, :]
        u = gc_ref[...] * xi_ref[...]
        u_prev = jnp.where(i > 0, gcp_ref[7:8, :] * xip_ref[7:8, :], 0.0)
        u_next = jnp.where(i < n_lat - 1, gcn_ref[0:1, :] * xin_ref[0:1, :], 0.0)
        um1, up1 = _shift_rows(u, u_prev, u_next)
        yv = cw_ref[0:1, :] * um1 + cw_ref[1:2, :] * u + cw_ref[2:3, :] * up1
        gb = gb_ref[...]
        conv = (gb * yv).astype(BF16)
        conv_ref[...] = conv
        n_attn = N_HEADS * V_DIM
        y1 = _dot(o_ref[...], wout_ref[0:n_attn, :]) + _dot(conv, wout_ref[n_attn:, :])
        x1 = x_ref[...] + g1 * y1
        rstd2 = lax.rsqrt(jnp.mean(x1 * x1, axis=-1, keepdims=True) + EPS)
        xn1 = x1 * rstd2
        h2 = (xn1 * (1.0 + sc2) + sh2).astype(BF16)
        h2_ref[...] = h2
        y2 = jnp.zeros((tm, D_MODEL), F32)
        for jj in range(n_ff):
            lo = jj * fc
            ra = jnp.maximum(_dot(h2, w1_ref[lo // FF_CHUNK, :, lo % FF_CHUNK:lo % FF_CHUNK + fc]), 0.0)
            ra_ref[jj] = ra
            r = (ra * ra).astype(BF16)
            r_ref[:, lo:lo + fc] = r
            y2 = y2 + _dot(r, w2_ref[lo:lo + fc, :])
        x2v = x1 + g2 * y2
        rstd3 = lax.rsqrt(jnp.mean(x2v * x2v, axis=-1, keepdims=True) + EPS)
        xn3 = x2v * rstd3
        gfv = gf_ref[...]
        diff = xn3 * gfv - t_ref[...]
        loss_t = 0.5 * jnp.sum(jnp.sum(diff * diff, axis=-1, keepdims=True), axis=0, keepdims=True) * (1.0 / D_MODEL)
        dy = diff * (1.0 / D_MODEL)
        dxn3 = dy * gfv
        dx2 = rstd3 * (dxn3 - xn3 * jnp.mean(dxn3 * xn3, axis=-1, keepdims=True))
        dy2 = (dx2 * g2).astype(BF16)
        dy2_ref[...] = dy2
        dh2 = jnp.zeros((tm, D_MODEL), F32)
        for jj in range(n_ff):
            lo = jj * fc
            dr = _dot_nt(dy2, w2_ref[lo:lo + fc, :])
            da = (2.0 * ra_ref[jj] * dr).astype(BF16)
            da_ref[:, lo:lo + fc] = da
            dh2 = dh2 + _dot_nt(da, w1_ref[lo // FF_CHUNK, :, lo % FF_CHUNK:lo % FF_CHUNK + fc])
        dxn1 = dh2 * (1.0 + sc2)
        dx1 = dx2 + rstd2 * (dxn1 - xn1 * jnp.mean(dxn1 * xn1, axis=-1, keepdims=True))
        dx1_ref[...] = dx1
        dy1 = (dx1 * g1).astype(BF16)
        dy1_ref[...] = dy1
        do_ref[...] = _dot_nt(dy1, wout_ref[0:n_attn, :]).astype(BF16)
        dconv = _dot_nt(dy1, wout_ref[n_attn:, :])
        dgb_ref[...] = dconv * yv
        dyv_ref[...] = dconv * gb
        acc_ref[5:6, :] += jnp.sum(dx1 * y1, axis=0, keepdims=True)
        acc_ref[0:1, :] += jnp.sum(dy * xn3, axis=0, keepdims=True)
        acc_ref[1:2, :] += jnp.sum(dx2 * y2, axis=0, keepdims=True)
        acc_ref[2:3, :] += jnp.sum(dh2, axis=0, keepdims=True)
        acc_ref[3:4, :] += jnp.sum(dh2 * xn1, axis=0, keepdims=True)
        acc_ref[4:5, :] += jnp.broadcast_to(loss_t, (1, D_MODEL))

    row = lambda i: (i, 0)
    gcp, gcn = _halo_specs(tm, 2)
    xip, xin = _halo_specs(tm, 3)
    tile = pl.BlockSpec((tm, D_MODEL), row)
    wide = pl.BlockSpec((tm, D_FF), row)
    half = pl.BlockSpec((tm, CONV_W), row)
    return pl.pallas_call(
        body, name="mlp_fwdbwd", grid=(n_lat,),
        out_shape=(jax.ShapeDtypeStruct((t_lat, D_FF), BF16), jax.ShapeDtypeStruct((t_lat, D_FF), BF16),
                   jax.ShapeDtypeStruct((t_lat, D_MODEL), BF16), jax.ShapeDtypeStruct((t_lat, D_MODEL), BF16),
                   jax.ShapeDtypeStruct((t_lat, D_MODEL), F32), jax.ShapeDtypeStruct((t_lat, CONV_W), BF16),
                   jax.ShapeDtypeStruct((8, D_MODEL), F32),
                   jax.ShapeDtypeStruct((t_lat, D_MODEL), BF16),
                   jax.ShapeDtypeStruct((t_lat, N_HEADS * V_DIM), BF16),
                   jax.ShapeDtypeStruct((t_lat, CONV_W), F32), jax.ShapeDtypeStruct((t_lat, CONV_W), F32)),
        in_specs=[
            pl.BlockSpec((tm, N_HEADS * V_DIM), row),
            pl.BlockSpec((tm, CONV_W), lambda i: (i, 1)), pl.BlockSpec((tm, CONV_W), lambda i: (i, 2)),
            pl.BlockSpec((tm, CONV_W), lambda i: (i, 3)),
            gcp, xip, gcn, xin,
            _const_spec(cw.shape), _resident_spec(w_out.shape),
            tile, tile, _const_spec(mod_b.shape), _const_spec(gf.shape),
            _resident_spec(w1.shape), _resident_spec(w2.shape),
        ],
        out_specs=(wide, wide, tile, tile, tile, half, _const_spec((8, D_MODEL)),
                   tile, pl.BlockSpec((tm, N_HEADS * V_DIM), row), half, half),
        scratch_shapes=[pltpu.VMEM((n_ff, tm, fc), F32)],
        compiler_params=pltpu.CompilerParams(vmem_limit_bytes=VMEM_LIMIT),
    )(o, z, z, z, z, z, z, z, cw, w_out, x2, tgt, mod_b, gf, w1, w2)


def _inproj_bwd(x2, ctx2, mod_a, z, dyv, dgb, dx1, dqt, dk, dv, cos_t, sin_a, sin_b, w_in, w_uq, w_ukv, qg, kvg, cw):
    t_lat, t_ctx = x2.shape[0], ctx2.shape[0]
    tm = TOK_TILE
    n_lat = t_lat // tm
    n_all = n_lat + t_ctx // tm
    group = max(g for g in (1, 2, 4) if n_lat % g == 0)

    def body(x_ref, ctx_ref, mod_ref, z_ref, gcp_ref, xip_ref, gcn_ref, xin_ref, dyv_ref, dyvp_ref, dyvn_ref,
             dgb_ref, dx1_ref, dqt_ref, dk_ref, dv_ref, cos_ref, sa_ref, sb_ref, win_ref, wuq_ref, wukv_ref,
             qg_ref, kvg_ref, cw_ref, gx_ref, dwin_out, dwuq_out, dwukv_out, acc_ref, h1_buf, dz_buf,
             dwin_ref, dwuq_ref, dwukv_ref):
        i = pl.program_id(0)
        lat = i < n_lat

        @pl.when(i == 0)
        def _():
            dwin_ref[...] = jnp.zeros(dwin_ref.shape, F32)
            dwuq_ref[...] = jnp.zeros(dwuq_ref.shape, F32)
            dwukv_ref[...] = jnp.zeros(dwukv_ref.shape, F32)
            acc_ref[...] = jnp.zeros(acc_ref.shape, F32)

        xin = jnp.where(lat, x_ref[...], ctx_ref[...])
        rstd = lax.rsqrt(jnp.mean(xin * xin, axis=-1, keepdims=True) + EPS)
        xn = xin * rstd
        sc = mod_ref[0, 1:2, :]
        h1 = (xn * (1.0 + sc) + mod_ref[0, 0:1, :]).astype(BF16)
        z = z_ref[...]
        cos, sa, sb = cos_ref[...], sa_ref[...], sb_ref[...]
        qgv, kvgv = qg_ref[...], kvg_ref[...]
        cq = z[:, 0:Q_RANK]
        cqh = cq * lax.rsqrt(jnp.mean(cq * cq, axis=-1, keepdims=True) + EPS)
        rq = lax.rsqrt(jnp.mean(cq * cq, axis=-1, keepdims=True) + EPS)
        cqn = (cqh * qgv).astype(BF16)
        parts = []
        for h in range(N_HEADS):
            g = jnp.where(lat, dqt_ref[h], 0.0)
            parts.append(_unrope(g, cos, sa, sb))
        dq = jnp.concatenate(parts, axis=1).astype(BF16)
        dcqn = _dot_nt(dq, wuq_ref[...])
        dwuq_ref[...] += _dot_tn(cqn, dq)
        acc_ref[4:5, 0:Q_RANK] += jnp.sum(dcqn * cqh, axis=0, keepdims=True)
        dxn = dcqn * qgv
        dcq = rq * (dxn - cqh * jnp.mean(dxn * cqh, axis=-1, keepdims=True))
        ckv = z[:, Q_RANK:Q_RANK + KV_RANK]
        rk = lax.rsqrt(jnp.mean(ckv * ckv, axis=-1, keepdims=True) + EPS)
        ckvh = ckv * rk
        ckvn = (ckvh * kvgv).astype(BF16)
        dks = [dk_ref[h] for h in range(N_HEADS)]
        dkr = dks[0]
        for h in range(1, N_HEADS):
            dkr = dkr + dks[h]
        dkv = jnp.concatenate(dks + [dv_ref[h] for h in range(N_HEADS)], axis=1).astype(BF16)
        dckvn = _dot_nt(dkv, wukv_ref[...])
        dwukv_ref[...] += _dot_tn(ckvn, dkv)
        acc_ref[5:6, 0:KV_RANK] += jnp.sum(dckvn * ckvh, axis=0, keepdims=True)
        dxn = dckvn * kvgv
        dckv = rk * (dxn - ckvh * jnp.mean(dxn * ckvh, axis=-1, keepdims=True))
        dkr = _unrope(dkr, cos, sa, sb)
        gb, gc, xi = z[:, 512:1024], z[:, 1024:1536], z[:, 1536:2048]
        u = gc * xi
        u_prev = jnp.where(i > 0, gcp_ref[7:8, :] * xip_ref[7:8, :], 0.0)
        u_next = jnp.where(i < n_lat - 1, gcn_ref[0:1, :] * xin_ref[0:1, :], 0.0)
        um1, up1 = _shift_rows(u, u_prev, u_next)
        dyv = jnp.where(lat, dyv_ref[...], 0.0)
        dyv_prev = jnp.where(jnp.logical_and(i > 0, lat), dyvp_ref[7:8, :], 0.0)
        dyv_next = jnp.where(i < n_lat - 1, dyvn_ref[0:1, :], 0.0)
        dyv_m1, dyv_p1 = _shift_rows(dyv, dyv_prev, dyv_next)
        du = cw_ref[0:1, :] * dyv_p1 + cw_ref[1:2, :] * dyv + cw_ref[2:3, :] * dyv_m1
        dgc = du * xi
        dxi = du * gc
        dgb = jnp.where(lat, dgb_ref[...], 0.0)
        acc_ref[6:7, 0:CONV_W] += jnp.sum(dyv * um1, axis=0, keepdims=True)
        acc_ref[7:8, 0:CONV_W] += jnp.sum(dyv * u, axis=0, keepdims=True)
        acc_ref[8:9, 0:CONV_W] += jnp.sum(dyv * up1, axis=0, keepdims=True)
        dz = jnp.concatenate([dcq, dckv, dkr, dgb, dgc, dxi], axis=1).astype(BF16)
        dh1 = _dot_nt(dz, win_ref[...])
        slot = i % group
        rows_g = pl.ds(pl.multiple_of(slot * tm, tm), tm)
        h1_buf[rows_g, :] = h1
        dz_buf[rows_g, :] = dz

        @pl.when(jnp.logical_and(lat, slot == group - 1))
        def _():
            dwin_ref[...] += _dot_tn(h1_buf[...], dz_buf[...])

        @pl.when(jnp.logical_not(lat))
        def _():
            dwin_ref[...] += _dot_tn(h1, dz)
        s_sh = jnp.sum(dh1, axis=0, keepdims=True)
        s_sc = jnp.sum(dh1 * xn, axis=0, keepdims=True)
        zero = jnp.zeros_like(s_sh)
        acc_ref[0:1, :] += jnp.where(lat, s_sh, zero)
        acc_ref[1:2, :] += jnp.where(lat, s_sc, zero)
        acc_ref[2:3, :] += jnp.where(lat, zero, s_sh)
        acc_ref[3:4, :] += jnp.where(lat, zero, s_sc)
        dxn = dh1 * (1.0 + sc)
        dx = rstd * (dxn - xn * jnp.mean(dxn * xn, axis=-1, keepdims=True))

        @pl.when(lat)
        def _():
            gx_ref[...] = dx1_ref[...] + dx

        @pl.when(i == n_all - 1)
        def _():
            dwin_out[...] = dwin_ref[...].astype(BF16)
            dwuq_out[...] = dwuq_ref[...].astype(BF16)
            dwukv_out[...] = dwukv_ref[...].astype(BF16)

    last = n_lat - 1
    per = tm // 8
    lat_row = lambda i: (jnp.minimum(i, last), 0)
    row = lambda i: (i, 0)
    gcp, gcn = _halo_specs(tm, 2)
    xip, xin = _halo_specs(tm, 3)
    n_halo = t_lat // 8
    dyvp = pl.BlockSpec((8, CONV_W), lambda i: (jnp.clip(i * per - 1, 0, n_halo - 1), 0))
    dyvn = pl.BlockSpec((8, CONV_W), lambda i: (jnp.minimum((i + 1) * per, n_halo - 1), 0))
    gcn = pl.BlockSpec((8, CONV_W), lambda i: (jnp.minimum((i + 1) * per, (t_lat + t_ctx) // 8 - 1), 2))
    xin = pl.BlockSpec((8, CONV_W), lambda i: (jnp.minimum((i + 1) * per, (t_lat + t_ctx) // 8 - 1), 3))
    head_f32 = pl.BlockSpec((N_HEADS, tm, HEAD_PAD), lambda i: (0, i, 0))
    tab = pl.BlockSpec((tm, HEAD_PAD), row)
    return pl.pallas_call(
        body, name="inproj_bwd", grid=(n_all,),
        out_shape=(jax.ShapeDtypeStruct((t_lat, D_MODEL), F32), jax.ShapeDtypeStruct(w_in.shape, BF16),
                   jax.ShapeDtypeStruct(w_uq.shape, BF16), jax.ShapeDtypeStruct(w_ukv.shape, BF16),
                   jax.ShapeDtypeStruct((16, D_MODEL), F32)),
        in_specs=[
            pl.BlockSpec((tm, D_MODEL), lat_row), _const_spec((tm, D_MODEL)),
            pl.BlockSpec((1, 8, D_MODEL), lambda i: (i // n_lat, 0, 0)),
            pl.BlockSpec((tm, Z_COLS), row), gcp, xip, gcn, xin,
            pl.BlockSpec((tm, CONV_W), lat_row), dyvp, dyvn,
            pl.BlockSpec((tm, CONV_W), lat_row), pl.BlockSpec((tm, D_MODEL), lat_row),
            pl.BlockSpec((N_HEADS, tm, HEAD_PAD), lambda i: (0, jnp.minimum(i, last), 0)),
            head_f32, head_f32, tab, tab, tab,
            _const_spec(w_in.shape), _const_spec(w_uq.shape), _const_spec(w_ukv.shape),
            _const_spec(qg.shape), _const_spec(kvg.shape), _const_spec(cw.shape),
        ],
        out_specs=(pl.BlockSpec((tm, D_MODEL), lat_row), _const_spec(w_in.shape), _const_spec(w_uq.shape),
                   _const_spec(w_ukv.shape), _const_spec((16, D_MODEL))),
        scratch_shapes=[pltpu.VMEM((group * tm, D_MODEL), BF16), pltpu.VMEM((group * tm, Z_COLS), BF16),
                        pltpu.VMEM(w_in.shape, F32), pltpu.VMEM(w_uq.shape, F32), pltpu.VMEM(w_ukv.shape, F32)],
        compiler_params=pltpu.CompilerParams(vmem_limit_bytes=VMEM_LIMIT),
    )(x2, ctx2, mod_a, z, z, z, z, z, dyv, dyv, dyv, dgb, dx1, dqt, dk, dv, cos_t, sin_a, sin_b, w_in, w_uq, w_ukv,
      qg, kvg, cw)


def _wgrad(a, b, name, bm, bn):
    t, m = a.shape
    n = b.shape[1]
    bk = min(t, 4096)
    nk = t // bk
    nj = n // bn

    def body(a_ref, b_ref, o_ref, acc_ref):
        k = pl.program_id(2)
        part = _dot_tn(a_ref[...], b_ref[...])

        @pl.when(k == 0)
        def _():
            acc_ref[...] = part

        @pl.when(k > 0)
        def _():
            acc_ref[...] += part

        @pl.when(k == nk - 1)
        def _():
            o_ref[...] = acc_ref[...].astype(BF16)

    return pl.pallas_call(
        body, name=name, grid=(m // bm, nj, nk), out_shape=jax.ShapeDtypeStruct((m // bm * nj, bm, bn), BF16),
        in_specs=[pl.BlockSpec((bk, bm), lambda i, j, k: (k, i)), pl.BlockSpec((bk, bn), lambda i, j, k: (k, j))],
        out_specs=pl.BlockSpec((None, bm, bn), lambda i, j, k: (i * nj + j, 0, 0)),
        scratch_shapes=[pltpu.VMEM((bm, bn), F32)],
        compiler_params=pltpu.CompilerParams(vmem_limit_bytes=VMEM_LIMIT),
    )(a, b)


def _wgrad_out(o, conv, dy1, sib_arrays):
    t = o.shape[0]
    bk = min(t, 2048)
    nk = t // bk
    n_s = len(sib_arrays)

    def body(o_ref, c_ref, d_ref, *rest):
        w_ref, got_w_ref, acc_ref = rest[n_s], rest[2 * n_s + 1], rest[2 * n_s + 2]
        send = _SiblingSend(rest[:n_s], rest[n_s + 1:2 * n_s + 1], rest[2 * n_s + 3], rest[2 * n_s + 4])
        wsend, wrecv = rest[2 * n_s + 5], rest[2 * n_s + 6]
        k = pl.program_id(0)
        pl.when(k == 0)(send.start)
        part = _dot_tn(jnp.concatenate([o_ref[...], c_ref[...]], axis=1), d_ref[...])

        @pl.when(k == 0)
        def _():
            acc_ref[...] = part

        @pl.when(k > 0)
        def _():
            acc_ref[...] += part

        @pl.when(k == nk - 1)
        def _():
            w_ref[...] = acc_ref[...].astype(BF16)
            x, y, c = _pos()
            own = []
            for s in range(N_SHARD):
                theirs = pl.ds(pl.multiple_of(s * shard_rows + (1 - c) * (shard_rows // 2), 16), shard_rows // 2)
                cp = pltpu.make_async_remote_copy(
                    src_ref=w_ref.at[theirs], dst_ref=got_w_ref.at[s], send_sem=wsend.at[s], recv_sem=wrecv.at[s],
                    device_id=(x, y, 1 - c), device_id_type=MESH)
                cp.start()
                own.append(cp)
            for cp in own:
                cp.wait()

        pl.when(k == nk - 1)(send.finish)

    hbm = pl.BlockSpec(memory_space=pl.ANY)
    shard_rows = D_MODEL // N_SHARD
    return pl.pallas_call(
        body, name="wgrad_out", grid=(nk,),
        out_shape=(jax.ShapeDtypeStruct((D_MODEL, D_MODEL), BF16),)
        + tuple(jax.ShapeDtypeStruct((N_SHARD, a.shape[1] // 2, a.shape[2]), BF16) for a in sib_arrays)
        + (jax.ShapeDtypeStruct((N_SHARD, shard_rows // 2, D_MODEL), BF16),),
        in_specs=[pl.BlockSpec((bk, N_HEADS * V_DIM), lambda k: (k, 0)),
                  pl.BlockSpec((bk, CONV_W), lambda k: (k, 0)),
                  pl.BlockSpec((bk, D_MODEL), lambda k: (k, 0))] + [hbm] * n_s,
        out_specs=(_const_spec((D_MODEL, D_MODEL)),) + (hbm,) * (n_s + 1),
        scratch_shapes=[pltpu.VMEM((D_MODEL, D_MODEL), F32), pltpu.SemaphoreType.DMA((n_s,)),
                        pltpu.SemaphoreType.DMA((n_s,)), pltpu.SemaphoreType.DMA((N_SHARD,)),
                        pltpu.SemaphoreType.DMA((N_SHARD,))],
        compiler_params=pltpu.CompilerParams(vmem_limit_bytes=VMEM_LIMIT),
    )(o, conv, dy1, *sib_arrays)


ADAMW_STEPS = 4


def _adamw_call(ws, gs, ms, vs, name):
    n = len(ws)
    steps = ADAMW_STEPS if all(w.shape[0] % (8 * ADAMW_STEPS) == 0 for w in ws) else 1

    def body(*refs):
        for a in range(n):
            w_ref, g_ref, m_ref, v_ref = (refs[k * n + a] for k in range(4))
            d_, m_, v_ = _adamw(w_ref[...], g_ref[...], m_ref[...], v_ref[...])
            for k, val in enumerate((d_, m_, v_)):
                refs[(4 + k) * n + a][...] = val

    specs = [pl.BlockSpec((w.shape[0] // steps, w.shape[1]), lambda i: (i, 0)) for w in ws]
    shapes = [jax.ShapeDtypeStruct(w.shape, F32) for w in ws]
    outs = pl.pallas_call(
        body, name=name, grid=(steps,), out_shape=tuple(shapes * 3),
        in_specs=specs * 4, out_specs=tuple(specs * 3),
    )(*ws, *gs, *ms, *vs)
    return [tuple(outs[k * n + a] for k in range(3)) for a in range(n)]


def _adamw_halves(ws, g_mine, g_theirs, ms, vs, c_idx, name):
    n = len(ws)
    steps = ADAMW_STEPS
    nb = steps // 2

    def body(c_ref, *refs):
        mine = pl.program_id(0) // nb == c_ref[0]
        for a in range(n):
            w_ref, gm_ref, gt_ref, m_ref, v_ref = (refs[k * n + a] for k in range(5))
            g = jnp.where(mine, gm_ref[...], gt_ref[...])
            d_, m_, v_ = _adamw(w_ref[...], g, m_ref[...], v_ref[...])
            for k, val in enumerate((g, d_, m_, v_)):
                refs[(5 + k) * n + a][...] = val

    specs = [pl.BlockSpec((w.shape[0] // steps, w.shape[1]), lambda i, c_ref: (i, 0)) for w in ws]
    hspecs = [pl.BlockSpec((w.shape[0] // steps, w.shape[1]), lambda i, c_ref: (i % nb, 0)) for w in ws]
    shapes = [jax.ShapeDtypeStruct(w.shape, F32) for w in ws]
    grid_spec = pltpu.PrefetchScalarGridSpec(
        num_scalar_prefetch=1, grid=(steps,), in_specs=specs + hspecs + hspecs + specs + specs,
        out_specs=tuple(specs * 4))
    outs = pl.pallas_call(
        body, name=name, grid_spec=grid_spec, out_shape=tuple(shapes * 4),
        compiler_params=pltpu.CompilerParams(vmem_limit_bytes=VMEM_LIMIT),
    )(c_idx, *ws, *g_mine, *g_theirs, *ms, *vs)
    return [tuple(outs[k * n + a] for k in range(4)) for a in range(n)]


def _wmod_update(s_t, dm, w, m, v):
    rows, cols = w.shape
    cb = 512

    def body(s_ref, dm_ref, w_ref, m_ref, v_ref, g_ref, d_ref, nm_ref, nv_ref):
        g = jnp.dot(s_ref[...], dm_ref[...], precision=HIGHEST, preferred_element_type=F32)
        d_, m_, v_ = _adamw(w_ref[...], g, m_ref[...], v_ref[...])
        g_ref[...] = g
        d_ref[...] = d_
        nm_ref[...] = m_
        nv_ref[...] = v_

    spec = pl.BlockSpec((rows, cb), lambda i: (0, i))
    shp = jax.ShapeDtypeStruct((rows, cols), F32)
    return pl.pallas_call(
        body, name="wmod_update", grid=(cols // cb,), out_shape=(shp, shp, shp, shp),
        in_specs=[_const_spec(s_t.shape), pl.BlockSpec((16, cb), lambda i: (0, i)), spec, spec, spec],
        out_specs=(spec, spec, spec, spec),
        compiler_params=pltpu.CompilerParams(vmem_limit_bytes=VMEM_LIMIT),
    )(s_t, dm, w, m, v)


def _rope_tables(t_lat, t_ctx):
    t = jnp.arange(t_lat)
    pos = jnp.stack([(t // GRID_W).astype(F32), (t % GRID_W).astype(F32)], axis=1)
    half = QK_ROPE // 4
    freqs = ROPE_THETA ** (-jnp.arange(0, 2 * half, 2, dtype=F32) / (2 * half))
    ang = pos[:, :, None] * freqs[None, None, :]
    cos, sin = jnp.cos(ang), jnp.sin(ang)
    zero = jnp.zeros_like(sin)
    cos32 = jnp.concatenate([cos, cos], axis=2).reshape(t_lat, QK_ROPE)
    sa32 = jnp.concatenate([zero, sin], axis=2).reshape(t_lat, QK_ROPE)
    sb32 = jnp.concatenate([-sin, zero], axis=2).reshape(t_lat, QK_ROPE)

    def widen(tab, fill):
        left = jnp.full((t_lat, ROPE_LANE0), fill, F32)
        right = jnp.full((t_lat, HEAD_PAD - ROPE_LANE0 - QK_ROPE), fill, F32)
        lat = jnp.concatenate([left, tab, right], axis=1)
        return jnp.concatenate([lat, jnp.full((t_ctx, HEAD_PAD), fill, F32)], axis=0)

    return widen(cos32, 1.0), widen(sa32, 0.0), widen(sb32, 0.0)


def _cols_from_shards(s):
    return jnp.transpose(s, (1, 0, 2)).reshape(s.shape[1], -1)


def _cols_to_shards(w):
    k, n = w.shape
    return jnp.transpose(w.reshape(k, N_SHARD, n // N_SHARD), (1, 0, 2))


def kernel(x, c, ctx, c_ctx, w_mod, b_mod, w_in, q_norm_g, w_uq, kv_norm_g, w_ukv, conv_w, w_out, w_mlp1, w_mlp2, final_norm_g, loss_target, m_c_ctx, m_w_mod, m_b_mod, m_w_in, m_q_norm_g, m_w_uq, m_kv_norm_g, m_w_ukv, m_conv_w, m_w_out, m_w_mlp1, m_w_mlp2, m_final_norm_g, v_c_ctx, v_w_mod, v_b_mod, v_w_in, v_q_norm_g, v_w_uq, v_kv_norm_g, v_w_ukv, v_conv_w, v_w_out, v_w_mlp1, v_w_mlp2, v_final_norm_g):
    t_lat, t_ctx = x.shape[1], ctx.shape[1]
    assert t_ctx == TOK_TILE and t_lat % TOK_TILE == 0 and t_lat % GRID_W == 0
    mx, my, mc = _pos()
    j = 2 * mx + my
    ncol = w_mod.shape[2]
    x2, ctx2, tgt = x[0], ctx[0], loss_target[0]
    cctx_row = c_ctx.reshape(1, D_MODEL)

    b_sh = lax.dynamic_slice(b_mod, (0, j * ncol), (1, ncol))
    cw_pad = jnp.zeros((8, 128), F32).at[0:3, :].set(conv_w[0])
    c8, m_all, g_in, g_uq, g_ukv, g_out, g_m1, g_m2 = _prologue(
        c, cctx_row, w_mod[0], b_sh, cw_pad, (w_in[0], w_uq[0], w_ukv[0], w_out[0], w_mlp1[0], w_mlp2[0]), 3)
    mvec = m_all[:, 0, :].reshape(6, D_MODEL)
    mctx = m_all[:, 8, :].reshape(6, D_MODEL)
    zeros6 = jnp.zeros((6, D_MODEL), F32)
    mod_a = jnp.stack([jnp.concatenate([mvec[0:2], zeros6], axis=0), jnp.concatenate([mctx[0:2], zeros6], axis=0)])
    mod_b = jnp.concatenate([mvec[2:6], jnp.zeros((4, D_MODEL), F32)], axis=0)
    cw_full = jnp.pad(jnp.transpose(m_all[:, 9:12, 0:128], (1, 0, 2)).reshape(3, CONV_W), ((0, 5), (0, 0)))

    w_in_f = _cols_from_shards(g_in)
    zc = lambda n: jnp.zeros((D_MODEL, n), BF16)
    w_in_p = jnp.concatenate([w_in_f[:, 0:384], zc(64), w_in_f[:, 384:416], zc(32), w_in_f[:, 416:]], axis=1)
    w_uq_f = _cols_from_shards(g_uq).reshape(Q_RANK, N_HEADS, QK_DIM)
    w_uq_p = jnp.pad(w_uq_f, ((0, 0), (0, 0), (0, HEAD_PAD - QK_DIM))).reshape(Q_RANK, N_HEADS * HEAD_PAD)
    w_ukv_f = _cols_from_shards(g_ukv).reshape(KV_RANK, N_HEADS, QK_NOPE + V_DIM)
    padh = lambda a: jnp.pad(a, ((0, 0), (0, 0), (0, HEAD_PAD - a.shape[2]))).reshape(KV_RANK, N_HEADS * HEAD_PAD)
    w_ukv_p = jnp.concatenate([padh(w_ukv_f[:, :, :QK_NOPE]), padh(w_ukv_f[:, :, QK_NOPE:])], axis=1)
    cos_t, sin_a, sin_b = _rope_tables(t_lat, t_ctx)
    gf_row = final_norm_g.reshape(1, D_MODEL)
    c_idx = mc.reshape(1).astype(jnp.int32)
    j_idx = j.reshape(1).astype(jnp.int32)

    z, q, k, v, kt = _inproj_fwd(x2, ctx2, mod_a, w_in_p, q_norm_g, kv_norm_g, w_uq_p, w_ukv_p, cos_t, sin_a, sin_b)
    o, lse, g_out, w1, g_m2 = _attn_fwd(q, k, v, t_lat, (g_out, g_m1, g_m2))
    w_out_f = g_out.reshape(D_MODEL, D_MODEL)
    w2 = g_m2.reshape(D_FF, D_MODEL)
    r, da, h2, dy2, dx1, conv, acc_mlp, dy1, do, dgb, dyv = _mlp_fwdbwd(o, z, x2, tgt, mod_b, gf_row, cw_full, w_out_f,
                                                                         w1, w2)
    d_w1 = _wgrad(h2, da, "wgrad_mlp1", D_MODEL, FF_CHUNK)
    d_w2 = _wgrad(r, dy2, "wgrad_mlp2", FF_CHUNK, D_MODEL)
    d_wout, *big_got = _wgrad_out(o, conv, dy1, (d_w1, d_w2))
    d_wout = d_wout.reshape(N_SHARD, D_MODEL // N_SHARD, D_MODEL)
    big_grads = (d_w1, d_w2, d_wout)
    big_parts = _add_pairs(big_grads, big_got, c_idx, "rs_add_pairs_big")
    dqt, dk, dv, *big_recv = _attn_bwd(q, k, v, kt, o, do, lse, t_lat, big_parts)
    big_halves = _add_chips(big_parts, big_recv, j_idx, "rs_add_chips_big")
    gx, d_win, d_wuq, d_wukv, acc_in = _inproj_bwd(x2, ctx2, mod_a, z, dyv, dgb, dx1, dqt, dk, dv, cos_t, sin_a, sin_b,
                                                   w_in_p, w_uq_p, w_ukv_p, q_norm_g, kv_norm_g, cw_full)

    d_win_f = jnp.concatenate([d_win[:, 0:384], d_win[:, 448:480], d_win[:, 512:]], axis=1)
    d_wuq_f = d_wuq.reshape(Q_RANK, N_HEADS, HEAD_PAD)[:, :, 0:QK_DIM].reshape(Q_RANK, N_HEADS * QK_DIM)
    d_wukv3 = d_wukv.reshape(KV_RANK, 2, N_HEADS, HEAD_PAD)
    d_wukv_f = jnp.concatenate([d_wukv3[:, 0, :, 0:QK_NOPE], d_wukv3[:, 1, :, 0:V_DIM]], axis=2).reshape(KV_RANK, -1)
    rest = tuple(_cols_to_shards(a).astype(BF16) for a in (d_win_f, d_wuq_f, d_wukv_f))
    rest_got = _rs_sibling(rest, "rs_sibling_rest")
    rest_parts = _add_pairs(rest, rest_got, c_idx, "rs_add_pairs_rest")

    sv = jnp.concatenate([
        acc_in[0:2], acc_mlp[5:6], acc_mlp[2:4], acc_mlp[1:2],
        acc_in[2:4], acc_mlp[0:1], acc_in[4:5], acc_in[5:6], acc_in[6:9], acc_mlp[4:5],
        jnp.zeros((1, D_MODEL), F32)], axis=0)
    all_sv, red, o_cc, o_b, o_q, o_k, o_gf, *exchanged = _small_exchange(
        sv, w_mod[0], cctx_row, m_c_ctx.reshape(1, D_MODEL), v_c_ctx.reshape(1, D_MODEL),
        b_mod.reshape(6, D_MODEL), m_b_mod.reshape(6, D_MODEL), v_b_mod.reshape(6, D_MODEL),
        q_norm_g, m_q_norm_g, v_q_norm_g, kv_norm_g, m_kv_norm_g, v_kv_norm_g,
        gf_row, m_final_norm_g.reshape(1, D_MODEL), v_final_norm_g.reshape(1, D_MODEL), big_halves, rest_parts)
    big_theirs, rest_recv = exchanged[:len(big_halves)], exchanged[len(big_halves):]
    loss = red[14, 0]

    c9 = jnp.concatenate([c8[0::8], jnp.zeros((7, D_MODEL), F32)], axis=0)
    s_t = jnp.transpose(c9 * jax.nn.sigmoid(c9))
    dm_ex = all_sv[:, 0:6, :].reshape(8, 6 * D_MODEL)
    dm_ctx = jnp.concatenate([red[6:8].reshape(1, 2 * D_MODEL), jnp.zeros((1, 4 * D_MODEL), F32)], axis=1)
    dm16 = jnp.concatenate([dm_ex, dm_ctx, jnp.zeros((7, 6 * D_MODEL), F32)], axis=0)
    dm_sh = lax.dynamic_slice(dm16, (0, j * ncol), (16, ncol))
    g_wmod, d_wmod, nm_wmod, nv_wmod = _wmod_update(s_t, dm_sh, w_mod[0], m_w_mod[0], v_w_mod[0])

    g_cw = lax.dynamic_slice(red[11:14, 0:CONV_W], (0, j * 128), (3, 128))
    (d_cw, nm_cw, nv_cw), = _adamw_call([conv_w[0]], [g_cw], [m_conv_w[0]], [v_conv_w[0]], "adamw_conv")

    rest_halves = _add_chips(rest_parts, rest_recv, j_idx, "rs_add_chips_rest")
    g_win, g_wuq, g_wukv = _rs_join(rest_halves, "rs_join_rest")
    upd = dict(zip(("in", "uq", "ukv"), _adamw_call(
        [w_in[0], w_uq[0], w_ukv[0]], [g_win, g_wuq, g_wukv], [m_w_in[0], m_w_uq[0], m_w_ukv[0]],
        [v_w_in[0], v_w_uq[0], v_w_ukv[0]], "adamw_rest")))
    (g_w1, *upd["mlp1"]), (g_w2, *upd["mlp2"]), (g_wout, *upd["out"]) = _adamw_halves(
        [w_mlp1[0], w_mlp2[0], w_out[0]], big_halves, big_theirs, [m_w_mlp1[0], m_w_mlp2[0], m_w_out[0]],
        [v_w_mlp1[0], v_w_mlp2[0], v_w_out[0]], c_idx, "adamw_big")

    def four(o4, shape):
        return [o4[r].reshape(shape) for r in range(4)]

    cc4 = four(o_cc, (D_MODEL,))
    b4 = [o_b[r].reshape(1, 6 * D_MODEL) for r in range(4)]
    q4 = four(o_q, (1, Q_RANK))
    k4 = four(o_k, (1, KV_RANK))
    gf4 = four(o_gf, (D_MODEL,))
    big = {"in": g_win, "uq": g_wuq, "ukv": g_wukv, "out": g_wout, "mlp1": g_w1, "mlp2": g_w2}

    def leaf(idx):
        wm = (g_wmod, d_wmod, nm_wmod, nv_wmod)[idx]
        cwv = (g_cw, d_cw, nm_cw, nv_cw)[idx]
        bigv = {n: (big[n] if idx == 0 else upd[n][idx - 1]) for n in big}
        return [cc4[idx], wm[None], b4[idx], bigv["in"][None], q4[idx], bigv["uq"][None], k4[idx], bigv["ukv"][None],
                cwv[None], bigv["out"][None], bigv["mlp1"][None], bigv["mlp2"][None], gf4[idx]]

    return (loss, gx[None], *leaf(0), *leaf(1), *leaf(2), *leaf(3))
```

```python
import functools
import math

import jax
import jax.numpy as jnp
from jax import lax
from jax.experimental import pallas as pl
from jax.experimental.pallas import tpu as pltpu

F32 = jnp.float32
BF16 = jnp.bfloat16
MESH = pl.DeviceIdType.MESH
HIGHEST = lax.Precision.HIGHEST

D_MODEL = 1024
N_HEADS = 8
QK_NOPE = 64
QK_ROPE = 32
QK_DIM = QK_NOPE + QK_ROPE
V_DIM = 64
Q_RANK = 256
KV_RANK = 128
CONV_W = 512
D_FF = 4096
GRID_W = 64
ROPE_THETA = 10000.0
EPS = 1e-6
ATTN_SCALE = 1.0 / math.sqrt(QK_DIM)
HEAD_PAD = 128
Z_COLS = 2048
ROPE_LANE0 = QK_NOPE
N_SHARD = 4
TOK_TILE = 256
FF_CHUNK = 1024
MLP_FF_CHUNK = 1024
KEY_CHUNK = 1024
ATTN_FWD_Q_BLOCK = 1024
ATTN_HEADS_PER_STEP = 4
ATTN_BWD_HEADS_PER_STEP = 2
ATTN_BWD_Q_BLOCK = 512
KEY_CHUNK_BWD = 512

ADAM_LR = 0.001
ADAM_B1 = 0.9
ADAM_B2 = 0.999
ADAM_EPS = 1e-08
ADAM_WD = 0.01
ADAM_STEP = 10

LOG2E = 1.4426950408889634

VMEM_LIMIT = 56 * 1024 * 1024
STAGE_VMEM_LIMIT = 32 * 1024 * 1024


def _pos():
    return lax.axis_index("x"), lax.axis_index("y"), lax.axis_index("c")


def _dot(a, b):
    return jnp.dot(a, b, preferred_element_type=F32)


def _dot_nt(a, b):
    return lax.dot_general(a, b, (((1,), (1,)), ((), ())), preferred_element_type=F32)


def _dot_tn(a, b):
    return lax.dot_general(a, b, (((0,), (0,)), ((), ())), preferred_element_type=F32)


def _rope(v, cos, sa, sb):
    return v * cos + pltpu.roll(v, 8, 1) * sa + pltpu.roll(v, HEAD_PAD - 8, 1) * sb


def _unrope(g, cos, sa, sb):
    return g * cos + pltpu.roll(g * sa, HEAD_PAD - 8, 1) + pltpu.roll(g * sb, 8, 1)


def _sigmoid(v):
    return 1.0 / (1.0 + jnp.exp(-v))


def _adamw(w, g, m, v):
    m = ADAM_B1 * m + (1.0 - ADAM_B1) * g
    v = ADAM_B2 * v + (1.0 - ADAM_B2) * (g * g)
    m_hat = m / (1.0 - ADAM_B1 ** ADAM_STEP)
    v_hat = v / (1.0 - ADAM_B2 ** ADAM_STEP)
    delta = -ADAM_LR * (m_hat / (jnp.sqrt(v_hat) + ADAM_EPS) + ADAM_WD * w)
    return delta, m, v


def _shift_rows(u, prev_row, next_row):
    n = u.shape[0]
    rows = lax.broadcasted_iota(jnp.int32, u.shape, 0)
    um1 = jnp.where(rows == 0, prev_row, pltpu.roll(u, 1, 0))
    up1 = jnp.where(rows == n - 1, next_row, pltpu.roll(u, n - 1, 0))
    return um1, up1


def _const_spec(shape):
    nd = len(shape)
    return pl.BlockSpec(shape, lambda *_: (0,) * nd)


def _resident_spec(shape):
    nd = len(shape)
    return pl.BlockSpec(shape, lambda *_: (0,) * nd, pipeline_mode=pl.Buffered(1))


def _peer(r, x, y, c):
    px = 1 - x if r & 4 else x
    py = 1 - y if r & 2 else y
    pc = 1 - c if r & 1 else c
    return (px, py, pc)


def _prologue(c_row, cctx_row, w_mod_sh, b_sh, cw_sh, srcs, n_gather):
    ncol = w_mod_sh.shape[1]
    n = len(srcs)
    n_split = 4

    def body(c_ref, cctx_ref, w_ref, b_ref, cw_ref, *refs):
        ins, (c8_ref, m_ref), outs = refs[:n], refs[n:n + 2], refs[n + 2:2 * n + 2]
        mine_ref, msh_ref = refs[2 * n + 2:2 * n + 4]
        f32s, bfs = refs[2 * n + 4:3 * n + 4], refs[3 * n + 4:4 * n + 4]
        ssem, rsem, ssem2, rsem2, lsem_in, lsem_out = refs[4 * n + 4:4 * n + 10]
        x, y, c = _pos()
        me = 4 * x + 2 * y + c
        j = 2 * x + y

        def pieces(rows):
            step = rows // n_split
            return [pl.ds(q * step, step) for q in range(n_split)]

        for t in range(n):
            for sl in pieces(ins[t].shape[0]):
                pltpu.make_async_copy(ins[t].at[sl], f32s[t].at[sl], lsem_in.at[t]).start()
        mine_ref[...] = jnp.zeros(mine_ref.shape, F32)
        mine_ref[0:1, :] = c_ref[...]
        my_rows = pl.ds(pl.multiple_of(8 * me, 8), 8)
        sends = []
        for r in range(1, 8):
            cp = pltpu.make_async_remote_copy(
                src_ref=mine_ref, dst_ref=c8_ref.at[my_rows], send_sem=ssem.at[r - 1], recv_sem=rsem.at[r - 1],
                device_id=_peer(r, x, y, c), device_id_type=MESH)
            cp.start()
            sends.append(cp)

        def cast_and_store(t):
            pltpu.make_async_copy(ins[t], f32s[t], lsem_in.at[t]).wait()
            bfs[t][...] = f32s[t][...].astype(BF16)
            for sl in pieces(ins[t].shape[0]):
                pltpu.make_async_copy(bfs[t].at[sl], outs[t].at[j, sl], lsem_out.at[t]).start()

        gather = _ShardGather(outs[:n_gather], *refs[4 * n + 10:])
        for t in range(n_gather):
            cast_and_store(t)
        for t in range(n_gather):
            pltpu.make_async_copy(bfs[t], outs[t].at[j], lsem_out.at[t]).wait()
        gather.start()
        for cp in sends:
            cp.wait()
        c8_ref[my_rows, :] = mine_ref[...]
        c8_ref[64:72, :] = jnp.zeros((8, D_MODEL), F32)
        c8_ref[64:65, :] = cctx_ref[...]
        cv = c8_ref[...]
        s = cv * _sigmoid(cv)
        m = jnp.dot(s, w_ref[...], precision=HIGHEST, preferred_element_type=F32) + b_ref[...]
        msh_ref[0:64, :] = m[0:64, :]
        msh_ref[64:72, :] = jnp.zeros((8, ncol), F32)
        msh_ref[64:65, :] = m[64:65, :]
        msh_ref[65:68, 0:128] = cw_ref[0:3, :]
        m_ref[j, 0:8, :] = msh_ref[my_rows, :]
        m_ref[j, 8:16, :] = msh_ref[64:72, :]
        sends2 = []
        for k, (px, py) in enumerate(_chips(x, y)):
            theirs = pl.ds(pl.multiple_of(8 * (4 * px + 2 * py + c), 8), 8)
            for half, src in enumerate((msh_ref.at[theirs], msh_ref.at[64:72])):
                cp = pltpu.make_async_remote_copy(
                    src_ref=src, dst_ref=m_ref.at[j, 8 * half:8 * half + 8], send_sem=ssem2.at[2 * k + half],
                    recv_sem=rsem2.at[2 * k + half], device_id=(px, py, c), device_id_type=MESH)
                cp.start()
                sends2.append(cp)
        for t in range(n_gather, n):
            cast_and_store(t)
        gather.forward()
        gather.finish()
        for t in range(n_gather, n):
            pltpu.make_async_copy(bfs[t], outs[t].at[j], lsem_out.at[t]).wait()
        for cp in sends2:
            cp.wait()

    vm = pl.BlockSpec(memory_space=pltpu.VMEM)
    hbm = pl.BlockSpec(memory_space=pl.ANY)
    return pl.pallas_call(
        body, name="prologue",
        out_shape=(jax.ShapeDtypeStruct((72, D_MODEL), F32), jax.ShapeDtypeStruct((N_SHARD, 16, ncol), F32))
        + tuple(jax.ShapeDtypeStruct((N_SHARD,) + a.shape, BF16) for a in srcs),
        in_specs=[vm] * 5 + [hbm] * n, out_specs=(vm, vm) + (hbm,) * n,
        scratch_shapes=[pltpu.VMEM((8, D_MODEL), F32), pltpu.VMEM((72, ncol), F32)]
        + [pltpu.VMEM(a.shape, F32) for a in srcs] + [pltpu.VMEM(a.shape, BF16) for a in srcs]
        + [pltpu.SemaphoreType.DMA((7,)), pltpu.SemaphoreType.DMA((7,)),
           pltpu.SemaphoreType.DMA((6,)), pltpu.SemaphoreType.DMA((6,)),
           pltpu.SemaphoreType.DMA((n,)), pltpu.SemaphoreType.DMA((n,))] + _gather_sems(n_gather),
        compiler_params=pltpu.CompilerParams(vmem_limit_bytes=VMEM_LIMIT),
    )(c_row, cctx_row, w_mod_sh, b_sh, cw_sh, *srcs)


def _chips(x, y):
    return [(1 - x, y), (x, 1 - y), (1 - x, 1 - y)]


def _halves(ref, c, align):
    hr = ref.shape[-2] // 2
    return (pl.ds(pl.multiple_of(c * hr, align), hr), pl.ds(pl.multiple_of((1 - c) * hr, align), hr))


class _ShardGather:
    def __init__(self, refs, ssem, rsem, fsend, frecv):
        self.refs, self.sems = refs, (ssem, rsem, fsend, frecv)
        self.x, self.y, self.c = _pos()
        self.j = 2 * self.x + self.y

    def _ici(self, a, k, slot):
        g = self.refs[a]
        ssem, rsem, _, _ = self.sems
        mine, _ = _halves(g, self.c, 16)
        px, py = _chips(self.x, self.y)[k]
        return pltpu.make_async_remote_copy(
            src_ref=g.at[self.j, mine], dst_ref=g.at[slot, mine], send_sem=ssem.at[3 * a + k],
            recv_sem=rsem.at[3 * a + k], device_id=(px, py, self.c), device_id_type=MESH)

    def _d2d(self, a, k, to_other_half):
        g = self.refs[a]
        _, _, fsend, frecv = self.sems
        mine, theirs = _halves(g, self.c, 16)
        px, py = _chips(self.x, self.y)[k]
        jk = 2 * px + py
        return pltpu.make_async_remote_copy(
            src_ref=g.at[jk, mine], dst_ref=g.at[jk, theirs if to_other_half else mine],
            send_sem=fsend.at[3 * a + k], recv_sem=frecv.at[3 * a + k],
            device_id=(self.x, self.y, 1 - self.c), device_id_type=MESH)

    def start(self):
        for a in range(len(self.refs)):
            for k in range(3):
                self._ici(a, k, self.j).start()

    def forward(self):
        for a in range(len(self.refs)):
            for k, (px, py) in enumerate(_chips(self.x, self.y)):
                self._ici(a, k, 2 * px + py).wait_recv()
                self._d2d(a, k, False).start()

    def finish(self):
        for a in range(len(self.refs)):
            for k in range(3):
                self._d2d(a, k, True).wait()
                self._ici(a, k, self.j).wait_send()


def _gather_sems(n_arrays):
    return [pltpu.SemaphoreType.DMA((3 * n_arrays,)) for _ in range(4)]


class _SiblingSend:
    def __init__(self, g_refs, got_refs, ssem, rsem):
        self.g_refs, self.got_refs, self.ssem, self.rsem = g_refs, got_refs, ssem, rsem
        self.x, self.y, self.c = _pos()

    def _copy(self, a, shard):
        _, theirs = _halves(self.g_refs[a], self.c, 16)
        src = self.g_refs[a].at[:, theirs] if shard is None else self.g_refs[a].at[shard, theirs]
        dst = self.got_refs[a] if shard is None else self.got_refs[a].at[shard]
        return pltpu.make_async_remote_copy(
            src_ref=src, dst_ref=dst, send_sem=self.ssem.at[a], recv_sem=self.rsem.at[a],
            device_id=(self.x, self.y, 1 - self.c), device_id_type=MESH)

    def start(self):
        for a in range(len(self.g_refs)):
            for s in range(N_SHARD):
                self._copy(a, s).start()

    def finish(self):
        for a in range(len(self.g_refs)):
            self._copy(a, None).wait()


class _SiblingSwap:
    def __init__(self, h_refs, t_refs, ssem, rsem):
        self.h_refs, self.t_refs, self.ssem, self.rsem = h_refs, t_refs, ssem, rsem
        self.x, self.y, self.c = _pos()

    def _copy(self, a):
        return pltpu.make_async_remote_copy(
            src_ref=self.h_refs[a], dst_ref=self.t_refs[a], send_sem=self.ssem.at[a], recv_sem=self.rsem.at[a],
            device_id=(self.x, self.y, 1 - self.c), device_id_type=MESH)

    def start(self):
        for a in range(len(self.h_refs)):
            self._copy(a).start()

    def finish(self):
        for a in range(len(self.h_refs)):
            self._copy(a).wait()


def _rs_sibling(arrs, name):
    n = len(arrs)

    def body(*refs):
        send = _SiblingSend(refs[:n], refs[n:2 * n], refs[2 * n], refs[2 * n + 1])
        send.start()
        send.finish()

    hbm = pl.BlockSpec(memory_space=pl.ANY)
    return pl.pallas_call(
        body, name=name,
        out_shape=tuple(jax.ShapeDtypeStruct((N_SHARD, a.shape[1] // 2, a.shape[2]), BF16) for a in arrs),
        in_specs=[hbm] * n, out_specs=(hbm,) * n,
        scratch_shapes=[pltpu.SemaphoreType.DMA((n,)), pltpu.SemaphoreType.DMA((n,))],
    )(*arrs)


class _ChipScatter:
    def __init__(self, parts, gots, ssem, rsem):
        self.parts, self.gots, self.ssem, self.rsem = parts, gots, ssem, rsem
        self.x, self.y, self.c = _pos()

    def _copy(self, a, k):
        px, py = _chips(self.x, self.y)[k]
        return pltpu.make_async_remote_copy(
            src_ref=self.parts[a].at[2 * px + py], dst_ref=self.gots[a].at[k], send_sem=self.ssem.at[3 * a + k],
            recv_sem=self.rsem.at[3 * a + k], device_id=(px, py, self.c), device_id_type=MESH)

    def start(self):
        for a in range(len(self.parts)):
            for k in range(3):
                self._copy(a, k).start()

    def finish(self):
        for a in range(len(self.parts)):
            for k in range(3):
                self._copy(a, k).wait()


def _rs_join(halves, name):
    n = len(halves)

    def body(*refs):
        h_refs, f_refs, stages = refs[:n], refs[n:2 * n], refs[2 * n:3 * n]
        lsem_in, lsem_out, ssem, rsem = refs[3 * n:]
        x, y, c = _pos()
        remote = []
        for a in range(n):
            mine, _ = _halves(f_refs[a], c, 8)
            cp = pltpu.make_async_remote_copy(
                src_ref=h_refs[a], dst_ref=f_refs[a].at[mine], send_sem=ssem.at[a], recv_sem=rsem.at[a],
                device_id=(x, y, 1 - c), device_id_type=MESH)
            cp.start()
            remote.append(cp)
            pltpu.make_async_copy(h_refs[a], stages[a], lsem_in.at[a]).start()
        local = []
        for a in range(n):
            mine, _ = _halves(f_refs[a], c, 8)
            pltpu.make_async_copy(h_refs[a], stages[a], lsem_in.at[a]).wait()
            cp = pltpu.make_async_copy(stages[a], f_refs[a].at[mine], lsem_out.at[a])
            cp.start()
            local.append(cp)
        for cp in remote + local:
            cp.wait()

    hbm = pl.BlockSpec(memory_space=pl.ANY)
    return pl.pallas_call(
        body, name=name,
        out_shape=tuple(jax.ShapeDtypeStruct((2 * h.shape[0], h.shape[1]), F32) for h in halves),
        in_specs=[hbm] * n, out_specs=(hbm,) * n,
        scratch_shapes=[pltpu.VMEM(h.shape, F32) for h in halves]
        + [pltpu.SemaphoreType.DMA((n,)) for _ in range(4)],
        compiler_params=pltpu.CompilerParams(vmem_limit_bytes=STAGE_VMEM_LIMIT),
    )(*halves)


def _add_pairs(arrs, gots, c_idx, name):
    n = len(arrs)

    def body(c_ref, *refs):
        for a in range(n):
            refs[2 * n + a][...] = (refs[a][...].astype(F32) + refs[n + a][...].astype(F32)).astype(BF16)

    def half_spec(g, mine):
        hr, cols = g.shape[1], g.shape[2]
        if mine:
            return pl.BlockSpec((1, hr, cols), lambda s, c_ref: (s, c_ref[0], 0))
        return pl.BlockSpec((1, hr, cols), lambda s, c_ref: (s, 0, 0))

    grid_spec = pltpu.PrefetchScalarGridSpec(
        num_scalar_prefetch=1, grid=(N_SHARD,),
        in_specs=[half_spec(g, True) for g in gots] + [half_spec(g, False) for g in gots],
        out_specs=tuple(half_spec(g, False) for g in gots))
    return pl.pallas_call(
        body, name=name, grid_spec=grid_spec, out_shape=tuple(jax.ShapeDtypeStruct(g.shape, BF16) for g in gots),
        compiler_params=pltpu.CompilerParams(vmem_limit_bytes=STAGE_VMEM_LIMIT),
    )(c_idx, *arrs, *gots)


def _add_chips(parts, gots, j_idx, name):
    n = len(parts)
    n_split = 2

    def body(j_ref, *refs):
        for a in range(n):
            acc = refs[a][0].astype(F32)
            for k in range(3):
                acc = acc + refs[n + a][k].astype(F32)
            refs[2 * n + a][...] = acc

    in_specs, out_specs = [], []
    for g in gots:
        rb, cols = g.shape[1] // n_split, g.shape[2]
        in_specs.append(pl.BlockSpec((1, rb, cols), lambda r, j_ref: (j_ref[0], r, 0)))
        out_specs.append(pl.BlockSpec((rb, cols), lambda r, j_ref: (r, 0)))
    for g in gots:
        rb, cols = g.shape[1] // n_split, g.shape[2]
        in_specs.append(pl.BlockSpec((3, rb, cols), lambda r, j_ref: (0, r, 0)))
    grid_spec = pltpu.PrefetchScalarGridSpec(
        num_scalar_prefetch=1, grid=(n_split,), in_specs=in_specs, out_specs=tuple(out_specs))
    return pl.pallas_call(
        body, name=name, grid_spec=grid_spec,
        out_shape=tuple(jax.ShapeDtypeStruct(g.shape[1:], F32) for g in gots),
        compiler_params=pltpu.CompilerParams(vmem_limit_bytes=STAGE_VMEM_LIMIT),
    )(j_idx, *parts, *gots)


def _small_exchange(sv, w_mod_sh, cctx, m_cctx, v_cctx, bmod, m_bmod, v_bmod, qg, m_qg, v_qg, kvg, m_kvg, v_kvg,
                    gf, m_gf, v_gf, swap_halves, scatter_parts):
    ncol = w_mod_sh.shape[1]
    n_s, n_p = len(swap_halves), len(scatter_parts)
    n_x = n_s + n_p

    def body(sv_ref, w_ref, cctx_ref, mcc_ref, vcc_ref, b_ref, mb_ref, vb_ref, qg_ref, mq_ref, vq_ref,
             kg_ref, mk_ref, vk_ref, gf_ref, mgf_ref, vgf_ref, *rest):
        all_ref, red_ref, occ_ref, ob_ref, oq_ref, ok_ref, ogf_ref = rest[n_x:n_x + 7]
        vec_ref, part_ref, ssem, rsem, ssem2, rsem2, wsend, wrecv, psend, precv = rest[2 * n_x + 7:]
        swap = _SiblingSwap(rest[:n_s], rest[n_x + 7:n_x + 7 + n_s], wsend, wrecv)
        scatter = _ChipScatter(rest[n_s:n_x], rest[n_x + 7 + n_s:2 * n_x + 7], psend, precv)
        swap.start()
        scatter.start()
        x, y, c = _pos()
        me = 4 * x + 2 * y + c
        j = 2 * x + y
        sends = []
        for r in range(1, 8):
            cp = pltpu.make_async_remote_copy(
                src_ref=sv_ref, dst_ref=all_ref.at[me], send_sem=ssem.at[r - 1], recv_sem=rsem.at[r - 1],
                device_id=_peer(r, x, y, c), device_id_type=MESH)
            cp.start()
            sends.append(cp)
        for cp in sends:
            cp.wait()
        all_ref[me] = sv_ref[...]
        red = all_ref[0]
        for d in range(1, 8):
            red = red + all_ref[d]
        red_ref[...] = red
        vec_ref[...] = jnp.zeros(vec_ref.shape, F32)

        @pl.when(j == 0)
        def _():
            vec_ref[0:1, 0:1024] = red[6:7, :]
            vec_ref[0:1, 1024:1536] = red[7:8, 0:512]

        @pl.when(j == 1)
        def _():
            vec_ref[0:1, 0:512] = red[7:8, 512:1024]

        part = lax.dot_general(vec_ref[...], w_ref[...], (((1,), (1,)), ((), ())), precision=HIGHEST,
                               preferred_element_type=F32)
        part_ref[j] = part
        sends2 = []
        for k, r in enumerate((4, 2, 6)):
            cp = pltpu.make_async_remote_copy(
                src_ref=part_ref.at[j], dst_ref=part_ref.at[j], send_sem=ssem2.at[k], recv_sem=rsem2.at[k],
                device_id=_peer(r, x, y, c), device_id_type=MESH)
            cp.start()
            sends2.append(cp)
        for cp in sends2:
            cp.wait()
        tot = part_ref[0]
        for s in range(1, N_SHARD):
            tot = tot + part_ref[s]
        cc = cctx_ref[...]
        sg = _sigmoid(cc)
        g_cc = tot[0:1, :] * (sg * (1.0 + cc * (1.0 - sg)))
        d_, m_, v_ = _adamw(cc, g_cc, mcc_ref[...], vcc_ref[...])
        occ_ref[0:1, :] = g_cc
        occ_ref[1:2, :] = d_
        occ_ref[2:3, :] = m_
        occ_ref[3:4, :] = v_
        occ_ref[4:8, :] = jnp.zeros((4, D_MODEL), F32)
        g_b = red[0:6, :]
        pad = jnp.concatenate([red[6:8, :], jnp.zeros((4, D_MODEL), F32)], axis=0)
        g_b = g_b + pad
        d_, m_, v_ = _adamw(b_ref[...], g_b, mb_ref[...], vb_ref[...])
        ob_ref[0] = g_b
        ob_ref[1] = d_
        ob_ref[2] = m_
        ob_ref[3] = v_
        g_q = red[9:10, 0:Q_RANK]
        d_, m_, v_ = _adamw(qg_ref[...], g_q, mq_ref[...], vq_ref[...])
        oq_ref[0:1, :] = g_q
        oq_ref[1:2, :] = d_
        oq_ref[2:3, :] = m_
        oq_ref[3:4, :] = v_
        oq_ref[4:8, :] = jnp.zeros((4, Q_RANK), F32)
        g_k = red[10:11, 0:KV_RANK]
        d_, m_, v_ = _adamw(kg_ref[...], g_k, mk_ref[...], vk_ref[...])
        ok_ref[0:1, :] = g_k
        ok_ref[1:2, :] = d_
        ok_ref[2:3, :] = m_
        ok_ref[3:4, :] = v_
        ok_ref[4:8, :] = jnp.zeros((4, KV_RANK), F32)
        g_f = red[8:9, :]
        d_, m_, v_ = _adamw(gf_ref[...], g_f, mgf_ref[...], vgf_ref[...])
        ogf_ref[0:1, :] = g_f
        ogf_ref[1:2, :] = d_
        ogf_ref[2:3, :] = m_
        ogf_ref[3:4, :] = v_
        ogf_ref[4:8, :] = jnp.zeros((4, D_MODEL), F32)
        swap.finish()
        scatter.finish()

    vm = pl.BlockSpec(memory_space=pltpu.VMEM)
    hbm = pl.BlockSpec(memory_space=pl.ANY)
    out_shape = (
        jax.ShapeDtypeStruct((8, 16, D_MODEL), F32),
        jax.ShapeDtypeStruct((16, D_MODEL), F32),
        jax.ShapeDtypeStruct((8, D_MODEL), F32),
        jax.ShapeDtypeStruct((4, 6, D_MODEL), F32),
        jax.ShapeDtypeStruct((8, Q_RANK), F32),
        jax.ShapeDtypeStruct((8, KV_RANK), F32),
        jax.ShapeDtypeStruct((8, D_MODEL), F32),
    ) + tuple(jax.ShapeDtypeStruct(h.shape, F32) for h in swap_halves) + tuple(
        jax.ShapeDtypeStruct((3,) + p.shape[1:], BF16) for p in scatter_parts)
    return pl.pallas_call(
        body, name="small_exchange", out_shape=out_shape, in_specs=[vm] * 17 + [hbm] * n_x,
        out_specs=tuple([vm] * 7) + (hbm,) * n_x,
        scratch_shapes=[pltpu.VMEM((8, ncol), F32), pltpu.VMEM((N_SHARD, 8, D_MODEL), F32),
                        pltpu.SemaphoreType.DMA((7,)), pltpu.SemaphoreType.DMA((7,)),
                        pltpu.SemaphoreType.DMA((3,)), pltpu.SemaphoreType.DMA((3,)),
                        pltpu.SemaphoreType.DMA((n_s,)), pltpu.SemaphoreType.DMA((n_s,)),
                        pltpu.SemaphoreType.DMA((3 * n_p,)), pltpu.SemaphoreType.DMA((3 * n_p,))],
        compiler_params=pltpu.CompilerParams(vmem_limit_bytes=VMEM_LIMIT),
    )(sv, w_mod_sh, cctx, m_cctx, v_cctx, bmod, m_bmod, v_bmod, qg, m_qg, v_qg, kvg, m_kvg, v_kvg, gf, m_gf, v_gf,
      *swap_halves, *scatter_parts)


def _inproj_fwd(x2, ctx2, mod_a, w_in, qg, kvg, w_uq, w_ukv, cos_t, sin_a, sin_b):
    t_lat, t_ctx = x2.shape[0], ctx2.shape[0]
    tm = TOK_TILE
    n_lat = t_lat // tm
    n_all = n_lat + t_ctx // tm
    e_rows = t_lat + t_ctx

    def body(x_ref, ctx_ref, mod_ref, win_ref, qg_ref, kvg_ref, wuq_ref, wukv_ref, cos_ref, sa_ref, sb_ref,
             z_ref, q_ref, k_ref, v_ref, kt_ref):
        i = pl.program_id(0)
        xin = jnp.where(i < n_lat, x_ref[...], ctx_ref[...])
        xn = xin * lax.rsqrt(jnp.mean(xin * xin, axis=-1, keepdims=True) + EPS)
        h1 = (xn * (1.0 + mod_ref[0, 1:2, :]) + mod_ref[0, 0:1, :]).astype(BF16)
        z = _dot(h1, win_ref[...])
        z_ref[...] = z
        cos, sa, sb = cos_ref[...], sa_ref[...], sb_ref[...]
        cq = z[:, 0:Q_RANK]
        cqn = (cq * lax.rsqrt(jnp.mean(cq * cq, axis=-1, keepdims=True) + EPS) * qg_ref[...]).astype(BF16)
        q = _dot(cqn, wuq_ref[...])
        ckv = z[:, Q_RANK:Q_RANK + KV_RANK]
        ckvn = (ckv * lax.rsqrt(jnp.mean(ckv * ckv, axis=-1, keepdims=True) + EPS) * kvg_ref[...]).astype(BF16)
        kv = _dot(ckvn, wukv_ref[...])
        kr = _rope(z[:, Q_RANK + KV_RANK:Q_RANK + KV_RANK + HEAD_PAD], cos, sa, sb)
        ones_lane = lax.broadcasted_iota(jnp.int32, (tm, HEAD_PAD), 1) == V_DIM
        for h in range(N_HEADS):
            lo = h * HEAD_PAD
            q_ref[h] = _rope(q[:, lo:lo + HEAD_PAD], cos, sa, sb).astype(BF16)
            kh = kv[:, lo:lo + HEAD_PAD] + kr
            k_ref[h] = kh.astype(BF16)
            kt_ref[h] = kh.T.astype(BF16)
            vh = kv[:, N_HEADS * HEAD_PAD + lo:N_HEADS * HEAD_PAD + lo + HEAD_PAD]
            v_ref[h] = jnp.where(ones_lane, 1.0, vh).astype(BF16)

    row = lambda i: (i, 0)
    head_spec = pl.BlockSpec((N_HEADS, tm, HEAD_PAD), lambda i: (0, i, 0))
    head_shape = jax.ShapeDtypeStruct((N_HEADS, e_rows, HEAD_PAD), BF16)
    return pl.pallas_call(
        body, name="inproj_fwd", grid=(n_all,),
        out_shape=(jax.ShapeDtypeStruct((e_rows, Z_COLS), F32), head_shape, head_shape, head_shape,
                   jax.ShapeDtypeStruct((N_HEADS, HEAD_PAD, e_rows), BF16)),
        in_specs=[
            pl.BlockSpec((tm, D_MODEL), lambda i: (jnp.minimum(i, n_lat - 1), 0)),
            _const_spec((tm, D_MODEL)),
            pl.BlockSpec((1, 8, D_MODEL), lambda i: (i // n_lat, 0, 0)),
            _const_spec(w_in.shape), _const_spec(qg.shape), _const_spec(kvg.shape),
            _const_spec(w_uq.shape), _const_spec(w_ukv.shape),
            pl.BlockSpec((tm, HEAD_PAD), row), pl.BlockSpec((tm, HEAD_PAD), row), pl.BlockSpec((tm, HEAD_PAD), row),
        ],
        out_specs=(pl.BlockSpec((tm, Z_COLS), row), head_spec, head_spec, head_spec,
                   pl.BlockSpec((N_HEADS, HEAD_PAD, tm), lambda i: (0, 0, i))),
        compiler_params=pltpu.CompilerParams(vmem_limit_bytes=VMEM_LIMIT),
    )(x2, ctx2, mod_a, w_in, qg, kvg, w_uq, w_ukv, cos_t, sin_a, sin_b)


def _key_chunks(e_rows, size):
    n_chunks = max(1, e_rows // size)
    return [(ci * size, size if ci < n_chunks - 1 else e_rows - ci * size) for ci in range(n_chunks)]


def _attn_fwd(q, k, v, t_lat, shard_arrays):
    e_rows = k.shape[1]
    tq = min(t_lat, ATTN_FWD_Q_BLOCK)
    bounds = _key_chunks(e_rows, KEY_CHUNK)
    c2 = ATTN_SCALE * LOG2E

    hb = ATTN_HEADS_PER_STEP
    n_hb = N_HEADS // hb

    def body(q_ref, k_ref, v_ref, o_ref, lse_ref):
        qs = [q_ref[b] for b in range(hb)]
        m, acc = [None] * hb, [None] * hb
        for lo, n in bounds:
            for b in range(hb):
                s = _dot_nt(qs[b], k_ref[b, lo:lo + n, :])
                mc = jnp.max(s, axis=-1, keepdims=True)
                m_new = mc if m[b] is None else jnp.maximum(m[b], mc)
                p = jnp.exp2((s - m_new) * c2)
                pv = _dot(p.astype(BF16), v_ref[b, lo:lo + n, :])
                acc[b] = pv if m[b] is None else acc[b] * jnp.exp2((m[b] - m_new) * c2) + pv
                m[b] = m_new
        outs = []
        for b in range(hb):
            l = acc[b][:, V_DIM:V_DIM + 1]
            outs.append(acc[b] * (1.0 / l))
            lse = (m[b] * ATTN_SCALE + jnp.log(l)) * LOG2E
            lse_ref[b] = jnp.broadcast_to(lse, (tq, HEAD_PAD)).T[0:1, :]
        low = lax.broadcasted_iota(jnp.int32, (tq, HEAD_PAD), 1) < V_DIM
        pairs = [jnp.where(low, outs[b], pltpu.roll(outs[b + 1], V_DIM, 1)) for b in range(0, hb, 2)]
        o_ref[...] = jnp.concatenate(pairs, axis=1).astype(BF16)

    n_w = len(shard_arrays)
    n_q = t_lat // tq

    def body_with_gather(q_ref, k_ref, v_ref, *rest):
        o_ref, lse_ref = rest[n_w], rest[n_w + 1]
        gather = _ShardGather(rest[n_w + 2:2 * n_w + 2], *rest[2 * n_w + 2:])
        step = pl.program_id(0) * n_q + pl.program_id(1)
        pl.when(step == 0)(gather.start)
        pl.when(step == n_hb * n_q // 2)(gather.forward)
        body(q_ref, k_ref, v_ref, o_ref, lse_ref)
        pl.when(step == n_hb * n_q - 1)(gather.finish)

    hbm = pl.BlockSpec(memory_space=pl.ANY)
    return pl.pallas_call(
        body_with_gather, name="attn_fwd", grid=(n_hb, n_q),
        out_shape=(jax.ShapeDtypeStruct((t_lat, N_HEADS * V_DIM), BF16),
                   jax.ShapeDtypeStruct((N_HEADS, 1, t_lat), F32))
        + tuple(jax.ShapeDtypeStruct(a.shape, a.dtype) for a in shard_arrays),
        in_specs=[pl.BlockSpec((hb, tq, HEAD_PAD), lambda h, i: (h, i, 0)),
                  pl.BlockSpec((hb, e_rows, HEAD_PAD), lambda h, i: (h, 0, 0)),
                  pl.BlockSpec((hb, e_rows, HEAD_PAD), lambda h, i: (h, 0, 0))] + [hbm] * n_w,
        out_specs=(pl.BlockSpec((tq, hb * V_DIM), lambda h, i: (i, h)),
                   pl.BlockSpec((hb, 1, tq), lambda h, i: (h, 0, i))) + (hbm,) * n_w,
        input_output_aliases={3 + a: 2 + a for a in range(n_w)},
        scratch_shapes=_gather_sems(n_w),
        compiler_params=pltpu.CompilerParams(vmem_limit_bytes=VMEM_LIMIT),
    )(q, k, v, *shard_arrays)


def _attn_bwd(q, k, v, kt, o, do, lse_row, t_lat, parts):
    e_rows = k.shape[1]
    tq = min(t_lat, ATTN_BWD_Q_BLOCK)
    n_p = len(parts)
    n_q = t_lat // tq
    bounds = _key_chunks(e_rows, KEY_CHUNK_BWD)

    hb = ATTN_BWD_HEADS_PER_STEP
    assert hb * V_DIM == HEAD_PAD, "a step's heads share one lane tile of the unpadded o / dO"
    n_hb = N_HEADS // hb

    def body(q_ref, k_ref, v_ref, kt_ref, o_ref, do_ref, lse_ref, *rest):
        dqt_ref, dk_ref, dv_ref = rest[n_p:n_p + 3]
        scatter = _ChipScatter(rest[:n_p], rest[n_p + 3:2 * n_p + 3], rest[2 * n_p + 3], rest[2 * n_p + 4])
        h, i = pl.program_id(0), pl.program_id(1)
        pl.when(jnp.logical_and(h == 0, i == 0))(scatter.start)

        @pl.when(i == 0)
        def _():
            dk_ref[...] = jnp.zeros(dk_ref.shape, F32)
            dv_ref[...] = jnp.zeros(dv_ref.shape, F32)

        do_pair = do_ref[...].astype(F32)
        prod = o_ref[...].astype(F32) * do_pair
        lane = lax.broadcasted_iota(jnp.int32, (tq, HEAD_PAD), 1)
        sel = lax.broadcasted_iota(jnp.int32, (8, HEAD_PAD), 1)
        qs, dos, lses, deltas = [], [], [], []
        for b in range(hb):
            qs.append(q_ref[b])
            mine = do_pair if b == 0 else pltpu.roll(do_pair, V_DIM, 1)
            dos.append(jnp.where(lane < V_DIM, mine, 0.0).astype(BF16))
            lses.append(lse_ref[b])
            ones = jnp.where((sel < V_DIM) == (b == 0), 1.0, 0.0)
            deltas.append(lax.dot_general(ones, prod, (((1,), (1,)), ((), ())), precision=HIGHEST,
                                          preferred_element_type=F32)[0:1, :])
        dqt = [None] * hb
        for lo, n in bounds:
            for b in range(hb):
                pt = jnp.exp2(_dot_nt(k_ref[b, lo:lo + n, :], qs[b]) * (ATTN_SCALE * LOG2E) - lses[b])
                dpt = _dot_nt(v_ref[b, lo:lo + n, :], dos[b])
                dst = (pt * (dpt - deltas[b])).astype(BF16)
                dv_c = _dot(pt.astype(BF16), dos[b])
                dk_c = _dot(dst, qs[b])
                part = _dot(kt_ref[b, :, lo:lo + n], dst)
                dqt[b] = part if dqt[b] is None else dqt[b] + part
                dk_ref[b, lo:lo + n, :] += dk_c * ATTN_SCALE
                dv_ref[b, lo:lo + n, :] += dv_c
        for b in range(hb):
            dqt_ref[b] = (dqt[b] * ATTN_SCALE).T

        pl.when(jnp.logical_and(h == n_hb - 1, i == n_q - 1))(scatter.finish)

    hbm = pl.BlockSpec(memory_space=pl.ANY)
    qspec = pl.BlockSpec((hb, tq, HEAD_PAD), lambda h, i: (h, i, 0))
    kspec = pl.BlockSpec((hb, e_rows, HEAD_PAD), lambda h, i: (h, 0, 0))
    pair = pl.BlockSpec((tq, hb * V_DIM), lambda h, i: (i, h))
    return pl.pallas_call(
        body, name="attn_bwd", grid=(n_hb, n_q),
        out_shape=(jax.ShapeDtypeStruct((N_HEADS, t_lat, HEAD_PAD), F32),
                   jax.ShapeDtypeStruct((N_HEADS, e_rows, HEAD_PAD), F32),
                   jax.ShapeDtypeStruct((N_HEADS, e_rows, HEAD_PAD), F32))
        + tuple(jax.ShapeDtypeStruct((3,) + p.shape[1:], BF16) for p in parts),
        in_specs=[qspec, kspec, kspec, pl.BlockSpec((hb, HEAD_PAD, e_rows), lambda h, i: (h, 0, 0)), pair, pair,
                  pl.BlockSpec((hb, 1, tq), lambda h, i: (h, 0, i))] + [hbm] * n_p,
        out_specs=(qspec, kspec, kspec) + (hbm,) * n_p,
        scratch_shapes=[pltpu.SemaphoreType.DMA((3 * n_p,)), pltpu.SemaphoreType.DMA((3 * n_p,))],
        compiler_params=pltpu.CompilerParams(vmem_limit_bytes=VMEM_LIMIT),
    )(q, k, v, kt, o, do, lse_row, *parts)


def _halo_specs(tm, col_block):
    per = tm // 8
    prev = pl.BlockSpec((8, CONV_W), lambda i: (jnp.maximum(i * per - 1, 0), col_block))
    nxt = pl.BlockSpec((8, CONV_W), lambda i: ((i + 1) * per, col_block))
    return prev, nxt


def _mlp_fwdbwd(o, z, x2, tgt, mod_b, gf, cw, w_out, w1, w2):
    t_lat = x2.shape[0]
    tm = TOK_TILE
    n_lat = t_lat // tm
    fc = MLP_FF_CHUNK
    n_ff = D_FF // fc

    def body(o_ref, gb_ref, gc_ref, xi_ref, gcp_ref, xip_ref, gcn_ref, xin_ref, cw_ref, wout_ref,
             x_ref, t_ref, mod_ref, gf_ref, w1_ref, w2_ref,
             r_ref, da_ref, h2_ref, dy2_ref, dx1_ref, conv_ref, acc_ref, dy1_ref, do_ref, dgb_ref, dyv_ref, ra_ref):
        i = pl.program_id(0)

        @pl.when(i == 0)
        def _():
            acc_ref[...] = jnp.zeros(acc_ref.shape, F32)

        g1, sh2, sc2, g2 = mod_ref[0:1, :], mod_ref[1:2, :], mod_ref[2:3, :], mod_ref[3:4, :]
        u = gc_ref[...] * xi_ref[...]
        u_prev = jnp.where(i > 0, gcp_ref[7:8, :] * xip_ref[7:8, :], 0.0)
        u_next = jnp.where(i < n_lat - 1, gcn_ref[0:1, :] * xin_ref[0:1, :], 0.0)
        um1, up1 = _shift_rows(u, u_prev, u_next)
        yv = cw_ref[0:1, :] * um1 + cw_ref[1:2, :] * u + cw_ref[2:3, :] * up1
        gb = gb_ref[...]
        conv = (gb * yv).astype(BF16)
        conv_ref[...] = conv
        n_attn = N_HEADS * V_DIM
        y1 = _dot(o_ref[...], wout_ref[0:n_attn, :]) + _dot(conv, wout_ref[n_attn:, :])
        x1 = x_ref[...] + g1 * y1
        rstd2 = lax.rsqrt(jnp.mean(x1 * x1, axis=-1, keepdims=True) + EPS)
        xn1 = x1 * rstd2
        h2 = (xn1 * (1.0 + sc2) + sh2).astype(BF16)
        h2_ref[...] = h2
        y2 = jnp.zeros((tm, D_MODEL), F32)
        for jj in range(n_ff):
            lo = jj * fc
            ra = jnp.maximum(_dot(h2, w1_ref[lo // FF_CHUNK, :, lo % FF_CHUNK:lo % FF_CHUNK + fc]), 0.0)
            ra_ref[jj] = ra
            r = (ra * ra).astype(BF16)
            r_ref[:, lo:lo + fc] = r
            y2 = y2 + _dot(r, w2_ref[lo:lo + fc, :])
        x2v = x1 + g2 * y2
        rstd3 = lax.rsqrt(jnp.mean(x2v * x2v, axis=-1, keepdims=True) + EPS)
        xn3 = x2v * rstd3
        gfv = gf_ref[...]
        diff = xn3 * gfv - t_ref[...]
        loss_t = 0.5 * jnp.sum(jnp.sum(diff * diff, axis=-1, keepdims=True), axis=0, keepdims=True) * (1.0 / D_MODEL)
        dy = diff * (1.0 / D_MODEL)
        dxn3 = dy * gfv
        dx2 = rstd3 * (dxn3 - xn3 * jnp.mean(dxn3 * xn3, axis=-1, keepdims=True))
        dy2 = (dx2 * g2).astype(BF16)
        dy2_ref[...] = dy2
        dh2 = jnp.zeros((tm, D_MODEL), F32)
        for jj in range(n_ff):
            lo = jj * fc
            dr = _dot_nt(dy2, w2_ref[lo:lo + fc, :])
            da = (2.0 * ra_ref[jj] * dr).astype(BF16)
            da_ref[:, lo:lo + fc] = da
            dh2 = dh2 + _dot_nt(da, w1_ref[lo // FF_CHUNK, :, lo % FF_CHUNK:lo % FF_CHUNK + fc])
        dxn1 = dh2 * (1.0 + sc2)
        dx1 = dx2 + rstd2 * (dxn1 - xn1 * jnp.mean(dxn1 * xn1, axis=-1, keepdims=True))
        dx1_ref[...] = dx1
        dy1 = (dx1 * g1).astype(BF16)
        dy1_ref[...] = dy1
        do_ref[...] = _dot_nt(dy1, wout_ref[0:n_attn, :]).astype(BF16)
        dconv = _dot_nt(dy1, wout_ref[n_attn:, :])
        dgb_ref[...] = dconv * yv
        dyv_ref[...] = dconv * gb
        acc_ref[5:6, :] += jnp.sum(dx1 * y1, axis=0, keepdims=True)
        acc_ref[0:1, :] += jnp.sum(dy * xn3, axis=0, keepdims=True)
        acc_ref[1:2, :] += jnp.sum(dx2 * y2, axis=0, keepdims=True)
        acc_ref[2:3, :] += jnp.sum(dh2, axis=0, keepdims=True)
        acc_ref[3:4, :] += jnp.sum(dh2 * xn1, axis=0, keepdims=True)
        acc_ref[4:5, :] += jnp.broadcast_to(loss_t, (1, D_MODEL))

    row = lambda i: (i, 0)
    gcp, gcn = _halo_specs(tm, 2)
    xip, xin = _halo_specs(tm, 3)
    tile = pl.BlockSpec((tm, D_MODEL), row)
    wide = pl.BlockSpec((tm, D_FF), row)
    half = pl.BlockSpec((tm, CONV_W), row)
    return pl.pallas_call(
        body, name="mlp_fwdbwd", grid=(n_lat,),
        out_shape=(jax.ShapeDtypeStruct((t_lat, D_FF), BF16), jax.ShapeDtypeStruct((t_lat, D_FF), BF16),
                   jax.ShapeDtypeStruct((t_lat, D_MODEL), BF16), jax.ShapeDtypeStruct((t_lat, D_MODEL), BF16),
                   jax.ShapeDtypeStruct((t_lat, D_MODEL), F32), jax.ShapeDtypeStruct((t_lat, CONV_W), BF16),
                   jax.ShapeDtypeStruct((8, D_MODEL), F32),
                   jax.ShapeDtypeStruct((t_lat, D_MODEL), BF16),
                   jax.ShapeDtypeStruct((t_lat, N_HEADS * V_DIM), BF16),
                   jax.ShapeDtypeStruct((t_lat, CONV_W), F32), jax.ShapeDtypeStruct((t_lat, CONV_W), F32)),
        in_specs=[
            pl.BlockSpec((tm, N_HEADS * V_DIM), row),
            pl.BlockSpec((tm, CONV_W), lambda i: (i, 1)), pl.BlockSpec((tm, CONV_W), lambda i: (i, 2)),
            pl.BlockSpec((tm, CONV_W), lambda i: (i, 3)),
            gcp, xip, gcn, xin,
            _const_spec(cw.shape), _resident_spec(w_out.shape),
            tile, tile, _const_spec(mod_b.shape), _const_spec(gf.shape),
            _resident_spec(w1.shape), _resident_spec(w2.shape),
        ],
        out_specs=(wide, wide, tile, tile, tile, half, _const_spec((8, D_MODEL)),
                   tile, pl.BlockSpec((tm, N_HEADS * V_DIM), row), half, half),
        scratch_shapes=[pltpu.VMEM((n_ff, tm, fc), F32)],
        compiler_params=pltpu.CompilerParams(vmem_limit_bytes=VMEM_LIMIT),
    )(o, z, z, z, z, z, z, z, cw, w_out, x2, tgt, mod_b, gf, w1, w2)


def _inproj_bwd(x2, ctx2, mod_a, z, dyv, dgb, dx1, dqt, dk, dv, cos_t, sin_a, sin_b, w_in, w_uq, w_ukv, qg, kvg, cw):
    t_lat, t_ctx = x2.shape[0], ctx2.shape[0]
    tm = TOK_TILE
    n_lat = t_lat // tm
    n_all = n_lat + t_ctx // tm
    group = max(g for g in (1, 2, 4) if n_lat % g == 0)

    def body(x_ref, ctx_ref, mod_ref, z_ref, gcp_ref, xip_ref, gcn_ref, xin_ref, dyv_ref, dyvp_ref, dyvn_ref,
             dgb_ref, dx1_ref, dqt_ref, dk_ref, dv_ref, cos_ref, sa_ref, sb_ref, win_ref, wuq_ref, wukv_ref,
             qg_ref, kvg_ref, cw_ref, gx_ref, dwin_out, dwuq_out, dwukv_out, acc_ref, h1_buf, dz_buf,
             dwin_ref, dwuq_ref, dwukv_ref):
        i = pl.program_id(0)
        lat = i < n_lat

        @pl.when(i == 0)
        def _():
            dwin_ref[...] = jnp.zeros(dwin_ref.shape, F32)
            dwuq_ref[...] = jnp.zeros(dwuq_ref.shape, F32)
            dwukv_ref[...] = jnp.zeros(dwukv_ref.shape, F32)
            acc_ref[...] = jnp.zeros(acc_ref.shape, F32)

        xin = jnp.where(lat, x_ref[...], ctx_ref[...])
        rstd = lax.rsqrt(jnp.mean(xin * xin, axis=-1, keepdims=True) + EPS)
        xn = xin * rstd
        sc = mod_ref[0, 1:2, :]
        h1 = (xn * (1.0 + sc) + mod_ref[0, 0:1, :]).astype(BF16)
        z = z_ref[...]
        cos, sa, sb = cos_ref[...], sa_ref[...], sb_ref[...]
        qgv, kvgv = qg_ref[...], kvg_ref[...]
        cq = z[:, 0:Q_RANK]
        cqh = cq * lax.rsqrt(jnp.mean(cq * cq, axis=-1, keepdims=True) + EPS)
        rq = lax.rsqrt(jnp.mean(cq * cq, axis=-1, keepdims=True) + EPS)
        cqn = (cqh * qgv).astype(BF16)
        parts = []
        for h in range(N_HEADS):
            g = jnp.where(lat, dqt_ref[h], 0.0)
            parts.append(_unrope(g, cos, sa, sb))
        dq = jnp.concatenate([p.astype(BF16) for p in parts], axis=1)
        dcqn = _dot_nt(dq, wuq_ref[...])
        dwuq_ref[...] += _dot_tn(cqn, dq)
        acc_ref[4:5, 0:Q_RANK] += jnp.sum(dcqn * cqh, axis=0, keepdims=True)
        dxn = dcqn * qgv
        dcq = rq * (dxn - cqh * jnp.mean(dxn * cqh, axis=-1, keepdims=True))
        ckv = z[:, Q_RANK:Q_RANK + KV_RANK]
        rk = lax.rsqrt(jnp.mean(ckv * ckv, axis=-1, keepdims=True) + EPS)
        ckvh = ckv * rk
        ckvn = (ckvh * kvgv).astype(BF16)
        dks = [dk_ref[h] for h in range(N_HEADS)]
        dkr = dks[0]
        for h in range(1, N_HEADS):
            dkr = dkr + dks[h]
        dkv = jnp.concatenate([p.astype(BF16) for p in dks + [dv_ref[h] for h in range(N_HEADS)]], axis=1)
        dckvn = _dot_nt(dkv, wukv_ref[...])
        dwukv_ref[...] += _dot_tn(ckvn, dkv)
        acc_ref[5:6, 0:KV_RANK] += jnp.sum(dckvn * ckvh, axis=0, keepdims=True)
        dxn = dckvn * kvgv
        dckv = rk * (dxn - ckvh * jnp.mean(dxn * ckvh, axis=-1, keepdims=True))
        dkr = _unrope(dkr, cos, sa, sb)
        gb, gc, xi = z[:, 512:1024], z[:, 1024:1536], z[:, 1536:2048]
        u = gc * xi
        u_prev = jnp.where(i > 0, gcp_ref[7:8, :] * xip_ref[7:8, :], 0.0)
        u_next = jnp.where(i < n_lat - 1, gcn_ref[0:1, :] * xin_ref[0:1, :], 0.0)
        um1, up1 = _shift_rows(u, u_prev, u_next)
        dyv = jnp.where(lat, dyv_ref[...], 0.0)
        dyv_prev = jnp.where(jnp.logical_and(i > 0, lat), dyvp_ref[7:8, :], 0.0)
        dyv_next = jnp.where(i < n_lat - 1, dyvn_ref[0:1, :], 0.0)
        dyv_m1, dyv_p1 = _shift_rows(dyv, dyv_prev, dyv_next)
        du = cw_ref[0:1, :] * dyv_p1 + cw_ref[1:2, :] * dyv + cw_ref[2:3, :] * dyv_m1
        dgc = du * xi
        dxi = du * gc
        dgb = jnp.where(lat, dgb_ref[...], 0.0)
        acc_ref[6:7, 0:CONV_W] += jnp.sum(dyv * um1, axis=0, keepdims=True)
        acc_ref[7:8, 0:CONV_W] += jnp.sum(dyv * u, axis=0, keepdims=True)
        acc_ref[8:9, 0:CONV_W] += jnp.sum(dyv * up1, axis=0, keepdims=True)
        dz = jnp.concatenate([p.astype(BF16) for p in (dcq, dckv, dkr, dgb, dgc, dxi)], axis=1)
        dh1 = _dot_nt(dz, win_ref[...])
        slot = i % group
        rows_g = pl.ds(pl.multiple_of(slot * tm, tm), tm)
        h1_buf[rows_g, :] = h1
        dz_buf[rows_g, :] = dz

        @pl.when(jnp.logical_and(lat, slot == group - 1))
        def _():
            dwin_ref[...] += _dot_tn(h1_buf[...], dz_buf[...])

        @pl.when(jnp.logical_not(lat))
        def _():
            dwin_ref[...] += _dot_tn(h1, dz)
        s_sh = jnp.sum(dh1, axis=0, keepdims=True)
        s_sc = jnp.sum(dh1 * xn, axis=0, keepdims=True)
        zero = jnp.zeros_like(s_sh)
        acc_ref[0:1, :] += jnp.where(lat, s_sh, zero)
        acc_ref[1:2, :] += jnp.where(lat, s_sc, zero)
        acc_ref[2:3, :] += jnp.where(lat, zero, s_sh)
        acc_ref[3:4, :] += jnp.where(lat, zero, s_sc)
        dxn = dh1 * (1.0 + sc)
        dx = rstd * (dxn - xn * jnp.mean(dxn * xn, axis=-1, keepdims=True))

        @pl.when(lat)
        def _():
            gx_ref[...] = dx1_ref[...] + dx

        @pl.when(i == n_all - 1)
        def _():
            dwin_out[...] = dwin_ref[...].astype(BF16)
            dwuq_out[...] = dwuq_ref[...].astype(BF16)
            dwukv_out[...] = dwukv_ref[...].astype(BF16)

    last = n_lat - 1
    per = tm // 8
    lat_row = lambda i: (jnp.minimum(i, last), 0)
    row = lambda i: (i, 0)
    gcp, gcn = _halo_specs(tm, 2)
    xip, xin = _halo_specs(tm, 3)
    n_halo = t_lat // 8
    dyvp = pl.BlockSpec((8, CONV_W), lambda i: (jnp.clip(i * per - 1, 0, n_halo - 1), 0))
    dyvn = pl.BlockSpec((8, CONV_W), lambda i: (jnp.minimum((i + 1) * per, n_halo - 1), 0))
    gcn = pl.BlockSpec((8, CONV_W), lambda i: (jnp.minimum((i + 1) * per, (t_lat + t_ctx) // 8 - 1), 2))
    xin = pl.BlockSpec((8, CONV_W), lambda i: (jnp.minimum((i + 1) * per, (t_lat + t_ctx) // 8 - 1), 3))
    head_f32 = pl.BlockSpec((N_HEADS, tm, HEAD_PAD), lambda i: (0, i, 0))
    tab = pl.BlockSpec((tm, HEAD_PAD), row)
    return pl.pallas_call(
        body, name="inproj_bwd", grid=(n_all,),
        out_shape=(jax.ShapeDtypeStruct((t_lat, D_MODEL), F32), jax.ShapeDtypeStruct(w_in.shape, BF16),
                   jax.ShapeDtypeStruct(w_uq.shape, BF16), jax.ShapeDtypeStruct(w_ukv.shape, BF16),
                   jax.ShapeDtypeStruct((16, D_MODEL), F32)),
        in_specs=[
            pl.BlockSpec((tm, D_MODEL), lat_row), _const_spec((tm, D_MODEL)),
            pl.BlockSpec((1, 8, D_MODEL), lambda i: (i // n_lat, 0, 0)),
            pl.BlockSpec((tm, Z_COLS), row), gcp, xip, gcn, xin,
            pl.BlockSpec((tm, CONV_W), lat_row), dyvp, dyvn,
            pl.BlockSpec((tm, CONV_W), lat_row), pl.BlockSpec((tm, D_MODEL), lat_row),
            pl.BlockSpec((N_HEADS, tm, HEAD_PAD), lambda i: (0, jnp.minimum(i, last), 0)),
            head_f32, head_f32, tab, tab, tab,
            _const_spec(w_in.shape), _const_spec(w_uq.shape), _const_spec(w_ukv.shape),
            _const_spec(qg.shape), _const_spec(kvg.shape), _const_spec(cw.shape),
        ],
        out_specs=(pl.BlockSpec((tm, D_MODEL), lat_row), _const_spec(w_in.shape), _const_spec(w_uq.shape),
                   _const_spec(w_ukv.shape), _const_spec((16, D_MODEL))),
        scratch_shapes=[pltpu.VMEM((group * tm, D_MODEL), BF16), pltpu.VMEM((group * tm, Z_COLS), BF16),
                        pltpu.VMEM(w_in.shape, F32), pltpu.VMEM(w_uq.shape, F32), pltpu.VMEM(w_ukv.shape, F32)],
        compiler_params=pltpu.CompilerParams(vmem_limit_bytes=VMEM_LIMIT),
    )(x2, ctx2, mod_a, z, z, z, z, z, dyv, dyv, dyv, dgb, dx1, dqt, dk, dv, cos_t, sin_a, sin_b, w_in, w_uq, w_ukv,
      qg, kvg, cw)


def _wgrad(a, b, name, bm, bn):
    t, m = a.shape
    n = b.shape[1]
    bk = min(t, 4096)
    nk = t // bk
    nj = n // bn

    def body(a_ref, b_ref, o_ref, acc_ref):
        k = pl.program_id(2)
        part = _dot_tn(a_ref[...], b_ref[...])

        @pl.when(k == 0)
        def _():
            acc_ref[...] = part

        @pl.when(k > 0)
        def _():
            acc_ref[...] += part

        @pl.when(k == nk - 1)
        def _():
            o_ref[...] = acc_ref[...].astype(BF16)

    return pl.pallas_call(
        body, name=name, grid=(m // bm, nj, nk), out_shape=jax.ShapeDtypeStruct((m // bm * nj, bm, bn), BF16),
        in_specs=[pl.BlockSpec((bk, bm), lambda i, j, k: (k, i)), pl.BlockSpec((bk, bn), lambda i, j, k: (k, j))],
        out_specs=pl.BlockSpec((None, bm, bn), lambda i, j, k: (i * nj + j, 0, 0)),
        scratch_shapes=[pltpu.VMEM((bm, bn), F32)],
        compiler_params=pltpu.CompilerParams(vmem_limit_bytes=VMEM_LIMIT),
    )(a, b)


def _wgrad_out(o, conv, dy1, sib_arrays):
    t = o.shape[0]
    bk = min(t, 2048)
    nk = t // bk
    n_s = len(sib_arrays)

    def body(o_ref, c_ref, d_ref, *rest):
        w_ref, got_w_ref, acc_ref = rest[n_s], rest[2 * n_s + 1], rest[2 * n_s + 2]
        send = _SiblingSend(rest[:n_s], rest[n_s + 1:2 * n_s + 1], rest[2 * n_s + 3], rest[2 * n_s + 4])
        wsend, wrecv = rest[2 * n_s + 5], rest[2 * n_s + 6]
        k = pl.program_id(0)
        pl.when(k == 0)(send.start)
        part = _dot_tn(jnp.concatenate([o_ref[...], c_ref[...]], axis=1), d_ref[...])

        @pl.when(k == 0)
        def _():
            acc_ref[...] = part

        @pl.when(k > 0)
        def _():
            acc_ref[...] += part

        @pl.when(k == nk - 1)
        def _():
            w_ref[...] = acc_ref[...].astype(BF16)
            x, y, c = _pos()
            own = []
            for s in range(N_SHARD):
                theirs = pl.ds(pl.multiple_of(s * shard_rows + (1 - c) * (shard_rows // 2), 16), shard_rows // 2)
                cp = pltpu.make_async_remote_copy(
                    src_ref=w_ref.at[theirs], dst_ref=got_w_ref.at[s], send_sem=wsend.at[s], recv_sem=wrecv.at[s],
                    device_id=(x, y, 1 - c), device_id_type=MESH)
                cp.start()
                own.append(cp)
            for cp in own:
                cp.wait()

        pl.when(k == nk - 1)(send.finish)

    hbm = pl.BlockSpec(memory_space=pl.ANY)
    shard_rows = D_MODEL // N_SHARD
    return pl.pallas_call(
        body, name="wgrad_out", grid=(nk,),
        out_shape=(jax.ShapeDtypeStruct((D_MODEL, D_MODEL), BF16),)
        + tuple(jax.ShapeDtypeStruct((N_SHARD, a.shape[1] // 2, a.shape[2]), BF16) for a in sib_arrays)
        + (jax.ShapeDtypeStruct((N_SHARD, shard_rows // 2, D_MODEL), BF16),),
        in_specs=[pl.BlockSpec((bk, N_HEADS * V_DIM), lambda k: (k, 0)),
                  pl.BlockSpec((bk, CONV_W), lambda k: (k, 0)),
                  pl.BlockSpec((bk, D_MODEL), lambda k: (k, 0))] + [hbm] * n_s,
        out_specs=(_const_spec((D_MODEL, D_MODEL)),) + (hbm,) * (n_s + 1),
        scratch_shapes=[pltpu.VMEM((D_MODEL, D_MODEL), F32), pltpu.SemaphoreType.DMA((n_s,)),
                        pltpu.SemaphoreType.DMA((n_s,)), pltpu.SemaphoreType.DMA((N_SHARD,)),
                        pltpu.SemaphoreType.DMA((N_SHARD,))],
        compiler_params=pltpu.CompilerParams(vmem_limit_bytes=VMEM_LIMIT),
    )(o, conv, dy1, *sib_arrays)


ADAMW_STEPS = 4


def _adamw_call(ws, gs, ms, vs, name):
    n = len(ws)
    steps = ADAMW_STEPS if all(w.shape[0] % (8 * ADAMW_STEPS) == 0 for w in ws) else 1

    def body(*refs):
        for a in range(n):
            w_ref, g_ref, m_ref, v_ref = (refs[k * n + a] for k in range(4))
            d_, m_, v_ = _adamw(w_ref[...], g_ref[...], m_ref[...], v_ref[...])
            for k, val in enumerate((d_, m_, v_)):
                refs[(4 + k) * n + a][...] = val

    specs = [pl.BlockSpec((w.shape[0] // steps, w.shape[1]), lambda i: (i, 0)) for w in ws]
    shapes = [jax.ShapeDtypeStruct(w.shape, F32) for w in ws]
    outs = pl.pallas_call(
        body, name=name, grid=(steps,), out_shape=tuple(shapes * 3),
        in_specs=specs * 4, out_specs=tuple(specs * 3),
    )(*ws, *gs, *ms, *vs)
    return [tuple(outs[k * n + a] for k in range(3)) for a in range(n)]


def _adamw_halves(ws, g_mine, g_theirs, ms, vs, c_idx, name):
    n = len(ws)
    steps = ADAMW_STEPS
    nb = steps // 2

    def body(c_ref, *refs):
        mine = pl.program_id(0) // nb == c_ref[0]
        for a in range(n):
            w_ref, gm_ref, gt_ref, m_ref, v_ref = (refs[k * n + a] for k in range(5))
            g = jnp.where(mine, gm_ref[...], gt_ref[...])
            d_, m_, v_ = _adamw(w_ref[...], g, m_ref[...], v_ref[...])
            for k, val in enumerate((g, d_, m_, v_)):
                refs[(5 + k) * n + a][...] = val

    specs = [pl.BlockSpec((w.shape[0] // steps, w.shape[1]), lambda i, c_ref: (i, 0)) for w in ws]
    hspecs = [pl.BlockSpec((w.shape[0] // steps, w.shape[1]), lambda i, c_ref: (i % nb, 0)) for w in ws]
    shapes = [jax.ShapeDtypeStruct(w.shape, F32) for w in ws]
    grid_spec = pltpu.PrefetchScalarGridSpec(
        num_scalar_prefetch=1, grid=(steps,), in_specs=specs + hspecs + hspecs + specs + specs,
        out_specs=tuple(specs * 4))
    outs = pl.pallas_call(
        body, name=name, grid_spec=grid_spec, out_shape=tuple(shapes * 4),
        compiler_params=pltpu.CompilerParams(vmem_limit_bytes=VMEM_LIMIT),
    )(c_idx, *ws, *g_mine, *g_theirs, *ms, *vs)
    return [tuple(outs[k * n + a] for k in range(4)) for a in range(n)]


def _wmod_update(s_t, dm, w, m, v):
    rows, cols = w.shape
    cb = 512

    def body(s_ref, dm_ref, w_ref, m_ref, v_ref, g_ref, d_ref, nm_ref, nv_ref):
        g = jnp.dot(s_ref[...], dm_ref[...], precision=HIGHEST, preferred_element_type=F32)
        d_, m_, v_ = _adamw(w_ref[...], g, m_ref[...], v_ref[...])
        g_ref[...] = g
        d_ref[...] = d_
        nm_ref[...] = m_
        nv_ref[...] = v_

    spec = pl.BlockSpec((rows, cb), lambda i: (0, i))
    shp = jax.ShapeDtypeStruct((rows, cols), F32)
    return pl.pallas_call(
        body, name="wmod_update", grid=(cols // cb,), out_shape=(shp, shp, shp, shp),
        in_specs=[_const_spec(s_t.shape), pl.BlockSpec((16, cb), lambda i: (0, i)), spec, spec, spec],
        out_specs=(spec, spec, spec, spec),
        compiler_params=pltpu.CompilerParams(vmem_limit_bytes=VMEM_LIMIT),
    )(s_t, dm, w, m, v)


def _rope_tables(t_lat, t_ctx):
    t = jnp.arange(t_lat)
    pos = jnp.stack([(t // GRID_W).astype(F32), (t % GRID_W).astype(F32)], axis=1)
    half = QK_ROPE // 4
    freqs = ROPE_THETA ** (-jnp.arange(0, 2 * half, 2, dtype=F32) / (2 * half))
    ang = pos[:, :, None] * freqs[None, None, :]
    cos, sin = jnp.cos(ang), jnp.sin(ang)
    zero = jnp.zeros_like(sin)
    cos32 = jnp.concatenate([cos, cos], axis=2).reshape(t_lat, QK_ROPE)
    sa32 = jnp.concatenate([zero, sin], axis=2).reshape(t_lat, QK_ROPE)
    sb32 = jnp.concatenate([-sin, zero], axis=2).reshape(t_lat, QK_ROPE)

    def widen(tab, fill):
        left = jnp.full((t_lat, ROPE_LANE0), fill, F32)
        right = jnp.full((t_lat, HEAD_PAD - ROPE_LANE0 - QK_ROPE), fill, F32)
        lat = jnp.concatenate([left, tab, right], axis=1)
        return jnp.concatenate([lat, jnp.full((t_ctx, HEAD_PAD), fill, F32)], axis=0)

    return widen(cos32, 1.0), widen(sa32, 0.0), widen(sb32, 0.0)


def _cols_from_shards(s):
    return jnp.transpose(s, (1, 0, 2)).reshape(s.shape[1], -1)


def _cols_to_shards(w):
    k, n = w.shape
    return jnp.transpose(w.reshape(k, N_SHARD, n // N_SHARD), (1, 0, 2))


def kernel(x, c, ctx, c_ctx, w_mod, b_mod, w_in, q_norm_g, w_uq, kv_norm_g, w_ukv, conv_w, w_out, w_mlp1, w_mlp2, final_norm_g, loss_target, m_c_ctx, m_w_mod, m_b_mod, m_w_in, m_q_norm_g, m_w_uq, m_kv_norm_g, m_w_ukv, m_conv_w, m_w_out, m_w_mlp1, m_w_mlp2, m_final_norm_g, v_c_ctx, v_w_mod, v_b_mod, v_w_in, v_q_norm_g, v_w_uq, v_kv_norm_g, v_w_ukv, v_conv_w, v_w_out, v_w_mlp1, v_w_mlp2, v_final_norm_g):
    t_lat, t_ctx = x.shape[1], ctx.shape[1]
    assert t_ctx == TOK_TILE and t_lat % TOK_TILE == 0 and t_lat % GRID_W == 0
    mx, my, mc = _pos()
    j = 2 * mx + my
    ncol = w_mod.shape[2]
    x2, ctx2, tgt = x[0], ctx[0], loss_target[0]
    cctx_row = c_ctx.reshape(1, D_MODEL)

    b_sh = lax.dynamic_slice(b_mod, (0, j * ncol), (1, ncol))
    cw_pad = jnp.zeros((8, 128), F32).at[0:3, :].set(conv_w[0])
    c8, m_all, g_in, g_uq, g_ukv, g_out, g_m1, g_m2 = _prologue(
        c, cctx_row, w_mod[0], b_sh, cw_pad, (w_in[0], w_uq[0], w_ukv[0], w_out[0], w_mlp1[0], w_mlp2[0]), 3)
    mvec = m_all[:, 0, :].reshape(6, D_MODEL)
    mctx = m_all[:, 8, :].reshape(6, D_MODEL)
    zeros6 = jnp.zeros((6, D_MODEL), F32)
    mod_a = jnp.stack([jnp.concatenate([mvec[0:2], zeros6], axis=0), jnp.concatenate([mctx[0:2], zeros6], axis=0)])
    mod_b = jnp.concatenate([mvec[2:6], jnp.zeros((4, D_MODEL), F32)], axis=0)
    cw_full = jnp.pad(jnp.transpose(m_all[:, 9:12, 0:128], (1, 0, 2)).reshape(3, CONV_W), ((0, 5), (0, 0)))

    w_in_f = _cols_from_shards(g_in)
    zc = lambda n: jnp.zeros((D_MODEL, n), BF16)
    w_in_p = jnp.concatenate([w_in_f[:, 0:384], zc(64), w_in_f[:, 384:416], zc(32), w_in_f[:, 416:]], axis=1)
    w_uq_f = _cols_from_shards(g_uq).reshape(Q_RANK, N_HEADS, QK_DIM)
    w_uq_p = jnp.pad(w_uq_f, ((0, 0), (0, 0), (0, HEAD_PAD - QK_DIM))).reshape(Q_RANK, N_HEADS * HEAD_PAD)
    w_ukv_f = _cols_from_shards(g_ukv).reshape(KV_RANK, N_HEADS, QK_NOPE + V_DIM)
    padh = lambda a: jnp.pad(a, ((0, 0), (0, 0), (0, HEAD_PAD - a.shape[2]))).reshape(KV_RANK, N_HEADS * HEAD_PAD)
    w_ukv_p = jnp.concatenate([padh(w_ukv_f[:, :, :QK_NOPE]), padh(w_ukv_f[:, :, QK_NOPE:])], axis=1)
    cos_t, sin_a, sin_b = _rope_tables(t_lat, t_ctx)
    gf_row = final_norm_g.reshape(1, D_MODEL)
    c_idx = mc.reshape(1).astype(jnp.int32)
    j_idx = j.reshape(1).astype(jnp.int32)

    z, q, k, v, kt = _inproj_fwd(x2, ctx2, mod_a, w_in_p, q_norm_g, kv_norm_g, w_uq_p, w_ukv_p, cos_t, sin_a, sin_b)
    o, lse, g_out, w1, g_m2 = _attn_fwd(q, k, v, t_lat, (g_out, g_m1, g_m2))
    w_out_f = g_out.reshape(D_MODEL, D_MODEL)
    w2 = g_m2.reshape(D_FF, D_MODEL)
    r, da, h2, dy2, dx1, conv, acc_mlp, dy1, do, dgb, dyv = _mlp_fwdbwd(o, z, x2, tgt, mod_b, gf_row, cw_full, w_out_f,
                                                                         w1, w2)
    d_w1 = _wgrad(h2, da, "wgrad_mlp1", D_MODEL, FF_CHUNK)
    d_w2 = _wgrad(r, dy2, "wgrad_mlp2", FF_CHUNK, D_MODEL)
    d_wout, *big_got = _wgrad_out(o, conv, dy1, (d_w1, d_w2))
    d_wout = d_wout.reshape(N_SHARD, D_MODEL // N_SHARD, D_MODEL)
    big_grads = (d_w1, d_w2, d_wout)
    big_parts = _add_pairs(big_grads, big_got, c_idx, "rs_add_pairs_big")
    dqt, dk, dv, *big_recv = _attn_bwd(q, k, v, kt, o, do, lse, t_lat, big_parts)
    big_halves = _add_chips(big_parts, big_recv, j_idx, "rs_add_chips_big")
    gx, d_win, d_wuq, d_wukv, acc_in = _inproj_bwd(x2, ctx2, mod_a, z, dyv, dgb, dx1, dqt, dk, dv, cos_t, sin_a, sin_b,
                                                   w_in_p, w_uq_p, w_ukv_p, q_norm_g, kv_norm_g, cw_full)

    d_win_f = jnp.concatenate([d_win[:, 0:384], d_win[:, 448:480], d_win[:, 512:]], axis=1)
    d_wuq_f = d_wuq.reshape(Q_RANK, N_HEADS, HEAD_PAD)[:, :, 0:QK_DIM].reshape(Q_RANK, N_HEADS * QK_DIM)
    d_wukv3 = d_wukv.reshape(KV_RANK, 2, N_HEADS, HEAD_PAD)
    d_wukv_f = jnp.concatenate([d_wukv3[:, 0, :, 0:QK_NOPE], d_wukv3[:, 1, :, 0:V_DIM]], axis=2).reshape(KV_RANK, -1)
    rest = tuple(_cols_to_shards(a).astype(BF16) for a in (d_win_f, d_wuq_f, d_wukv_f))
    rest_got = _rs_sibling(rest, "rs_sibling_rest")
    rest_parts = _add_pairs(rest, rest_got, c_idx, "rs_add_pairs_rest")

    sv = jnp.concatenate([
        acc_in[0:2], acc_mlp[5:6], acc_mlp[2:4], acc_mlp[1:2],
        acc_in[2:4], acc_mlp[0:1], acc_in[4:5], acc_in[5:6], acc_in[6:9], acc_mlp[4:5],
        jnp.zeros((1, D_MODEL), F32)], axis=0)
    all_sv, red, o_cc, o_b, o_q, o_k, o_gf, *exchanged = _small_exchange(
        sv, w_mod[0], cctx_row, m_c_ctx.reshape(1, D_MODEL), v_c_ctx.reshape(1, D_MODEL),
        b_mod.reshape(6, D_MODEL), m_b_mod.reshape(6, D_MODEL), v_b_mod.reshape(6, D_MODEL),
        q_norm_g, m_q_norm_g, v_q_norm_g, kv_norm_g, m_kv_norm_g, v_kv_norm_g,
        gf_row, m_final_norm_g.reshape(1, D_MODEL), v_final_norm_g.reshape(1, D_MODEL), big_halves, rest_parts)
    big_theirs, rest_recv = exchanged[:len(big_halves)], exchanged[len(big_halves):]
    loss = red[14, 0]

    c9 = jnp.concatenate([c8[0::8], jnp.zeros((7, D_MODEL), F32)], axis=0)
    s_t = jnp.transpose(c9 * jax.nn.sigmoid(c9))
    dm_ex = all_sv[:, 0:6, :].reshape(8, 6 * D_MODEL)
    dm_ctx = jnp.concatenate([red[6:8].reshape(1, 2 * D_MODEL), jnp.zeros((1, 4 * D_MODEL), F32)], axis=1)
    dm16 = jnp.concatenate([dm_ex, dm_ctx, jnp.zeros((7, 6 * D_MODEL), F32)], axis=0)
    dm_sh = lax.dynamic_slice(dm16, (0, j * ncol), (16, ncol))
    g_wmod, d_wmod, nm_wmod, nv_wmod = _wmod_update(s_t, dm_sh, w_mod[0], m_w_mod[0], v_w_mod[0])

    g_cw = lax.dynamic_slice(red[11:14, 0:CONV_W], (0, j * 128), (3, 128))
    (d_cw, nm_cw, nv_cw), = _adamw_call([conv_w[0]], [g_cw], [m_conv_w[0]], [v_conv_w[0]], "adamw_conv")

    rest_halves = _add_chips(rest_parts, rest_recv, j_idx, "rs_add_chips_rest")
    g_win, g_wuq, g_wukv = _rs_join(rest_halves, "rs_join_rest")
    upd = dict(zip(("in", "uq", "ukv"), _adamw_call(
        [w_in[0], w_uq[0], w_ukv[0]], [g_win, g_wuq, g_wukv], [m_w_in[0], m_w_uq[0], m_w_ukv[0]],
        [v_w_in[0], v_w_uq[0], v_w_ukv[0]], "adamw_rest")))
    (g_w1, *upd["mlp1"]), (g_w2, *upd["mlp2"]), (g_wout, *upd["out"]) = _adamw_halves(
        [w_mlp1[0], w_mlp2[0], w_out[0]], big_halves, big_theirs, [m_w_mlp1[0], m_w_mlp2[0], m_w_out[0]],
        [v_w_mlp1[0], v_w_mlp2[0], v_w_out[0]], c_idx, "adamw_big")

    def four(o4, shape):
        return [o4[r].reshape(shape) for r in range(4)]

    cc4 = four(o_cc, (D_MODEL,))
    b4 = [o_b[r].reshape(1, 6 * D_MODEL) for r in range(4)]
    q4 = four(o_q, (1, Q_RANK))
    k4 = four(o_k, (1, KV_RANK))
    gf4 = four(o_gf, (D_MODEL,))
    big = {"in": g_win, "uq": g_wuq, "ukv": g_wukv, "out": g_wout, "mlp1": g_w1, "mlp2": g_w2}

    def leaf(idx):
        wm = (g_wmod, d_wmod, nm_wmod, nv_wmod)[idx]
        cwv = (g_cw, d_cw, nm_cw, nv_cw)[idx]
        bigv = {n: (big[n] if idx == 0 else upd[n][idx - 1]) for n in big}
        return [cc4[idx], wm[None], b4[idx], bigv["in"][None], q4[idx], bigv["uq"][None], k4[idx], bigv["ukv"][None],
                cwv[None], bigv["out"][None], bigv["mlp1"][None], bigv["mlp2"][None], gf4[idx]]

    return (loss, gx[None], *leaf(0), *leaf(1), *leaf(2), *leaf(3))
```

```python
import functools
import math

import jax
import jax.numpy as jnp
from jax import lax
from jax.experimental import pallas as pl
from jax.experimental.pallas import tpu as pltpu

F32 = jnp.float32
BF16 = jnp.bfloat16
MESH = pl.DeviceIdType.MESH
HIGHEST = lax.Precision.HIGHEST

D_MODEL = 1024
N_HEADS = 8
QK_NOPE = 64
QK_ROPE = 32
QK_DIM = QK_NOPE + QK_ROPE
V_DIM = 64
Q_RANK = 256
KV_RANK = 128
CONV_W = 512
D_FF = 4096
GRID_W = 64
ROPE_THETA = 10000.0
EPS = 1e-6
ATTN_SCALE = 1.0 / math.sqrt(QK_DIM)
HEAD_PAD = 128
Z_COLS = 2048
ROPE_LANE0 = QK_NOPE
N_SHARD = 4
TOK_TILE = 256
FF_CHUNK = 1024
MLP_FF_CHUNK = 1024
KEY_CHUNK = 1024
ATTN_FWD_Q_BLOCK = 1024
ATTN_HEADS_PER_STEP = 4
ATTN_BWD_HEADS_PER_STEP = 2
ATTN_BWD_Q_BLOCK = 512
KEY_CHUNK_BWD = 512

ADAM_LR = 0.001
ADAM_B1 = 0.9
ADAM_B2 = 0.999
ADAM_EPS = 1e-08
ADAM_WD = 0.01
ADAM_STEP = 10

LOG2E = 1.4426950408889634

VMEM_LIMIT = 56 * 1024 * 1024
STAGE_VMEM_LIMIT = 32 * 1024 * 1024


def _pos():
    return lax.axis_index("x"), lax.axis_index("y"), lax.axis_index("c")


def _dot(a, b):
    return jnp.dot(a, b, preferred_element_type=F32)


def _dot_nt(a, b):
    return lax.dot_general(a, b, (((1,), (1,)), ((), ())), preferred_element_type=F32)


def _dot_tn(a, b):
    return lax.dot_general(a, b, (((0,), (0,)), ((), ())), preferred_element_type=F32)


def _rope(v, cos, sa, sb):
    return v * cos + pltpu.roll(v, 8, 1) * sa + pltpu.roll(v, HEAD_PAD - 8, 1) * sb


def _unrope(g, cos, sa, sb):
    return g * cos + pltpu.roll(g * sa, HEAD_PAD - 8, 1) + pltpu.roll(g * sb, 8, 1)


def _sigmoid(v):
    return 1.0 / (1.0 + jnp.exp(-v))


def _adamw(w, g, m, v):
    m = ADAM_B1 * m + (1.0 - ADAM_B1) * g
    v = ADAM_B2 * v + (1.0 - ADAM_B2) * (g * g)
    m_hat = m / (1.0 - ADAM_B1 ** ADAM_STEP)
    v_hat = v / (1.0 - ADAM_B2 ** ADAM_STEP)
    delta = -ADAM_LR * (m_hat / (jnp.sqrt(v_hat) + ADAM_EPS) + ADAM_WD * w)
    return delta, m, v


def _shift_rows(u, prev_row, next_row):
    n = u.shape[0]
    rows = lax.broadcasted_iota(jnp.int32, u.shape, 0)
    um1 = jnp.where(rows == 0, prev_row, pltpu.roll(u, 1, 0))
    up1 = jnp.where(rows == n - 1, next_row, pltpu.roll(u, n - 1, 0))
    return um1, up1


def _const_spec(shape):
    nd = len(shape)
    return pl.BlockSpec(shape, lambda *_: (0,) * nd)


def _resident_spec(shape):
    nd = len(shape)
    return pl.BlockSpec(shape, lambda *_: (0,) * nd, pipeline_mode=pl.Buffered(1))


def _peer(r, x, y, c):
    px = 1 - x if r & 4 else x
    py = 1 - y if r & 2 else y
    pc = 1 - c if r & 1 else c
    return (px, py, pc)


def _prologue(c_row, cctx_row, w_mod_sh, b_sh, cw_sh, srcs, n_gather):
    ncol = w_mod_sh.shape[1]
    n = len(srcs)
    n_split = 4

    def body(c_ref, cctx_ref, w_ref, b_ref, cw_ref, *refs):
        ins, (c8_ref, m_ref), outs = refs[:n], refs[n:n + 2], refs[n + 2:2 * n + 2]
        mine_ref, msh_ref = refs[2 * n + 2:2 * n + 4]
        f32s, bfs = refs[2 * n + 4:3 * n + 4], refs[3 * n + 4:4 * n + 4]
        ssem, rsem, ssem2, rsem2, lsem_in, lsem_out = refs[4 * n + 4:4 * n + 10]
        x, y, c = _pos()
        me = 4 * x + 2 * y + c
        j = 2 * x + y

        def pieces(rows):
            step = rows // n_split
            return [pl.ds(q * step, step) for q in range(n_split)]

        for t in range(n):
            for sl in pieces(ins[t].shape[0]):
                pltpu.make_async_copy(ins[t].at[sl], f32s[t].at[sl], lsem_in.at[t]).start()
        mine_ref[...] = jnp.zeros(mine_ref.shape, F32)
        mine_ref[0:1, :] = c_ref[...]
        my_rows = pl.ds(pl.multiple_of(8 * me, 8), 8)
        sends = []
        for r in range(1, 8):
            cp = pltpu.make_async_remote_copy(
                src_ref=mine_ref, dst_ref=c8_ref.at[my_rows], send_sem=ssem.at[r - 1], recv_sem=rsem.at[r - 1],
                device_id=_peer(r, x, y, c), device_id_type=MESH)
            cp.start()
            sends.append(cp)

        def cast_and_store(t):
            pltpu.make_async_copy(ins[t], f32s[t], lsem_in.at[t]).wait()
            bfs[t][...] = f32s[t][...].astype(BF16)
            for sl in pieces(ins[t].shape[0]):
                pltpu.make_async_copy(bfs[t].at[sl], outs[t].at[j, sl], lsem_out.at[t]).start()

        gather = _ShardGather(outs[:n_gather], *refs[4 * n + 10:])
        for t in range(n_gather):
            cast_and_store(t)
        for t in range(n_gather):
            pltpu.make_async_copy(bfs[t], outs[t].at[j], lsem_out.at[t]).wait()
        gather.start()
        for cp in sends:
            cp.wait()
        c8_ref[my_rows, :] = mine_ref[...]
        c8_ref[64:72, :] = jnp.zeros((8, D_MODEL), F32)
        c8_ref[64:65, :] = cctx_ref[...]
        cv = c8_ref[...]
        s = cv * _sigmoid(cv)
        m = jnp.dot(s, w_ref[...], precision=HIGHEST, preferred_element_type=F32) + b_ref[...]
        msh_ref[0:64, :] = m[0:64, :]
        msh_ref[64:72, :] = jnp.zeros((8, ncol), F32)
        msh_ref[64:65, :] = m[64:65, :]
        msh_ref[65:68, 0:128] = cw_ref[0:3, :]
        m_ref[j, 0:8, :] = msh_ref[my_rows, :]
        m_ref[j, 8:16, :] = msh_ref[64:72, :]
        sends2 = []
        for k, (px, py) in enumerate(_chips(x, y)):
            theirs = pl.ds(pl.multiple_of(8 * (4 * px + 2 * py + c), 8), 8)
            for half, src in enumerate((msh_ref.at[theirs], msh_ref.at[64:72])):
                cp = pltpu.make_async_remote_copy(
                    src_ref=src, dst_ref=m_ref.at[j, 8 * half:8 * half + 8], send_sem=ssem2.at[2 * k + half],
                    recv_sem=rsem2.at[2 * k + half], device_id=(px, py, c), device_id_type=MESH)
                cp.start()
                sends2.append(cp)
        for t in range(n_gather, n):
            cast_and_store(t)
        gather.forward()
        gather.finish()
        for t in range(n_gather, n):
            pltpu.make_async_copy(bfs[t], outs[t].at[j], lsem_out.at[t]).wait()
        for cp in sends2:
            cp.wait()

    vm = pl.BlockSpec(memory_space=pltpu.VMEM)
    hbm = pl.BlockSpec(memory_space=pl.ANY)
    return pl.pallas_call(
        body, name="prologue",
        out_shape=(jax.ShapeDtypeStruct((72, D_MODEL), F32), jax.ShapeDtypeStruct((N_SHARD, 16, ncol), F32))
        + tuple(jax.ShapeDtypeStruct((N_SHARD,) + a.shape, BF16) for a in srcs),
        in_specs=[vm] * 5 + [hbm] * n, out_specs=(vm, vm) + (hbm,) * n,
        scratch_shapes=[pltpu.VMEM((8, D_MODEL), F32), pltpu.VMEM((72, ncol), F32)]
        + [pltpu.VMEM(a.shape, F32) for a in srcs] + [pltpu.VMEM(a.shape, BF16) for a in srcs]
        + [pltpu.SemaphoreType.DMA((7,)), pltpu.SemaphoreType.DMA((7,)),
           pltpu.SemaphoreType.DMA((6,)), pltpu.SemaphoreType.DMA((6,)),
           pltpu.SemaphoreType.DMA((n,)), pltpu.SemaphoreType.DMA((n,))] + _gather_sems(n_gather),
        compiler_params=pltpu.CompilerParams(vmem_limit_bytes=VMEM_LIMIT),
    )(c_row, cctx_row, w_mod_sh, b_sh, cw_sh, *srcs)


def _chips(x, y):
    return [(1 - x, y), (x, 1 - y), (1 - x, 1 - y)]


def _halves(ref, c, align):
    hr = ref.shape[-2] // 2
    return (pl.ds(pl.multiple_of(c * hr, align), hr), pl.ds(pl.multiple_of((1 - c) * hr, align), hr))


class _ShardGather:
    def __init__(self, refs, ssem, rsem, fsend, frecv):
        self.refs, self.sems = refs, (ssem, rsem, fsend, frecv)
        self.x, self.y, self.c = _pos()
        self.j = 2 * self.x + self.y

    def _ici(self, a, k, slot):
        g = self.refs[a]
        ssem, rsem, _, _ = self.sems
        mine, _ = _halves(g, self.c, 16)
        px, py = _chips(self.x, self.y)[k]
        return pltpu.make_async_remote_copy(
            src_ref=g.at[self.j, mine], dst_ref=g.at[slot, mine], send_sem=ssem.at[3 * a + k],
            recv_sem=rsem.at[3 * a + k], device_id=(px, py, self.c), device_id_type=MESH)

    def _d2d(self, a, k, to_other_half):
        g = self.refs[a]
        _, _, fsend, frecv = self.sems
        mine, theirs = _halves(g, self.c, 16)
        px, py = _chips(self.x, self.y)[k]
        jk = 2 * px + py
        return pltpu.make_async_remote_copy(
            src_ref=g.at[jk, mine], dst_ref=g.at[jk, theirs if to_other_half else mine],
            send_sem=fsend.at[3 * a + k], recv_sem=frecv.at[3 * a + k],
            device_id=(self.x, self.y, 1 - self.c), device_id_type=MESH)

    def start(self):
        for a in range(len(self.refs)):
            for k in range(3):
                self._ici(a, k, self.j).start()

    def forward(self):
        for a in range(len(self.refs)):
            for k, (px, py) in enumerate(_chips(self.x, self.y)):
                self._ici(a, k, 2 * px + py).wait_recv()
                self._d2d(a, k, False).start()

    def finish(self):
        for a in range(len(self.refs)):
            for k in range(3):
                self._d2d(a, k, True).wait()
                self._ici(a, k, self.j).wait_send()


def _gather_sems(n_arrays):
    return [pltpu.SemaphoreType.DMA((3 * n_arrays,)) for _ in range(4)]


class _SiblingSend:
    def __init__(self, g_refs, got_refs, ssem, rsem):
        self.g_refs, self.got_refs, self.ssem, self.rsem = g_refs, got_refs, ssem, rsem
        self.x, self.y, self.c = _pos()

    def _copy(self, a, shard):
        _, theirs = _halves(self.g_refs[a], self.c, 16)
        src = self.g_refs[a].at[:, theirs] if shard is None else self.g_refs[a].at[shard, theirs]
        dst = self.got_refs[a] if shard is None else self.got_refs[a].at[shard]
        return pltpu.make_async_remote_copy(
            src_ref=src, dst_ref=dst, send_sem=self.ssem.at[a], recv_sem=self.rsem.at[a],
            device_id=(self.x, self.y, 1 - self.c), device_id_type=MESH)

    def start(self):
        for a in range(len(self.g_refs)):
            for s in range(N_SHARD):
                self._copy(a, s).start()

    def finish(self):
        for a in range(len(self.g_refs)):
            self._copy(a, None).wait()


class _SiblingSwap:
    def __init__(self, h_refs, t_refs, ssem, rsem):
        self.h_refs, self.t_refs, self.ssem, self.rsem = h_refs, t_refs, ssem, rsem
        self.x, self.y, self.c = _pos()

    def _copy(self, a):
        return pltpu.make_async_remote_copy(
            src_ref=self.h_refs[a], dst_ref=self.t_refs[a], send_sem=self.ssem.at[a], recv_sem=self.rsem.at[a],
            device_id=(self.x, self.y, 1 - self.c), device_id_type=MESH)

    def start(self):
        for a in range(len(self.h_refs)):
            self._copy(a).start()

    def finish(self):
        for a in range(len(self.h_refs)):
            self._copy(a).wait()


def _rs_sibling(arrs, name):
    n = len(arrs)

    def body(*refs):
        send = _SiblingSend(refs[:n], refs[n:2 * n], refs[2 * n], refs[2 * n + 1])
        send.start()
        send.finish()

    hbm = pl.BlockSpec(memory_space=pl.ANY)
    return pl.pallas_call(
        body, name=name,
        out_shape=tuple(jax.ShapeDtypeStruct((N_SHARD, a.shape[1] // 2, a.shape[2]), BF16) for a in arrs),
        in_specs=[hbm] * n, out_specs=(hbm,) * n,
        scratch_shapes=[pltpu.SemaphoreType.DMA((n,)), pltpu.SemaphoreType.DMA((n,))],
    )(*arrs)


class _ChipScatter:
    def __init__(self, parts, gots, ssem, rsem):
        self.parts, self.gots, self.ssem, self.rsem = parts, gots, ssem, rsem
        self.x, self.y, self.c = _pos()

    def _copy(self, a, k):
        px, py = _chips(self.x, self.y)[k]
        return pltpu.make_async_remote_copy(
            src_ref=self.parts[a].at[2 * px + py], dst_ref=self.gots[a].at[k], send_sem=self.ssem.at[3 * a + k],
            recv_sem=self.rsem.at[3 * a + k], device_id=(px, py, self.c), device_id_type=MESH)

    def start(self):
        for a in range(len(self.parts)):
            for k in range(3):
                self._copy(a, k).start()

    def finish(self):
        for a in range(len(self.parts)):
            for k in range(3):
                self._copy(a, k).wait()


def _rs_join(halves, name):
    n = len(halves)

    def body(*refs):
        h_refs, f_refs, stages = refs[:n], refs[n:2 * n], refs[2 * n:3 * n]
        lsem_in, lsem_out, ssem, rsem = refs[3 * n:]
        x, y, c = _pos()
        remote = []
        for a in range(n):
            mine, _ = _halves(f_refs[a], c, 8)
            cp = pltpu.make_async_remote_copy(
                src_ref=h_refs[a], dst_ref=f_refs[a].at[mine], send_sem=ssem.at[a], recv_sem=rsem.at[a],
                device_id=(x, y, 1 - c), device_id_type=MESH)
            cp.start()
            remote.append(cp)
            pltpu.make_async_copy(h_refs[a], stages[a], lsem_in.at[a]).start()
        local = []
        for a in range(n):
            mine, _ = _halves(f_refs[a], c, 8)
            pltpu.make_async_copy(h_refs[a], stages[a], lsem_in.at[a]).wait()
            cp = pltpu.make_async_copy(stages[a], f_refs[a].at[mine], lsem_out.at[a])
            cp.start()
            local.append(cp)
        for cp in remote + local:
            cp.wait()

    hbm = pl.BlockSpec(memory_space=pl.ANY)
    return pl.pallas_call(
        body, name=name,
        out_shape=tuple(jax.ShapeDtypeStruct((2 * h.shape[0], h.shape[1]), F32) for h in halves),
        in_specs=[hbm] * n, out_specs=(hbm,) * n,
        scratch_shapes=[pltpu.VMEM(h.shape, F32) for h in halves]
        + [pltpu.SemaphoreType.DMA((n,)) for _ in range(4)],
        compiler_params=pltpu.CompilerParams(vmem_limit_bytes=STAGE_VMEM_LIMIT),
    )(*halves)


def _add_pairs(arrs, gots, c_idx, name):
    n = len(arrs)

    def body(c_ref, *refs):
        for a in range(n):
            refs[2 * n + a][...] = (refs[a][...].astype(F32) + refs[n + a][...].astype(F32)).astype(BF16)

    def half_spec(g, mine):
        hr, cols = g.shape[1], g.shape[2]
        if mine:
            return pl.BlockSpec((1, hr, cols), lambda s, c_ref: (s, c_ref[0], 0))
        return pl.BlockSpec((1, hr, cols), lambda s, c_ref: (s, 0, 0))

    grid_spec = pltpu.PrefetchScalarGridSpec(
        num_scalar_prefetch=1, grid=(N_SHARD,),
        in_specs=[half_spec(g, True) for g in gots] + [half_spec(g, False) for g in gots],
        out_specs=tuple(half_spec(g, False) for g in gots))
    return pl.pallas_call(
        body, name=name, grid_spec=grid_spec, out_shape=tuple(jax.ShapeDtypeStruct(g.shape, BF16) for g in gots),
        compiler_params=pltpu.CompilerParams(vmem_limit_bytes=STAGE_VMEM_LIMIT),
    )(c_idx, *arrs, *gots)


def _add_chips(parts, gots, j_idx, name):
    n = len(parts)
    n_split = 2

    def body(j_ref, *refs):
        for a in range(n):
            acc = refs[a][0].astype(F32)
            for k in range(3):
                acc = acc + refs[n + a][k].astype(F32)
            refs[2 * n + a][...] = acc

    in_specs, out_specs = [], []
    for g in gots:
        rb, cols = g.shape[1] // n_split, g.shape[2]
        in_specs.append(pl.BlockSpec((1, rb, cols), lambda r, j_ref: (j_ref[0], r, 0)))
        out_specs.append(pl.BlockSpec((rb, cols), lambda r, j_ref: (r, 0)))
    for g in gots:
        rb, cols = g.shape[1] // n_split, g.shape[2]
        in_specs.append(pl.BlockSpec((3, rb, cols), lambda r, j_ref: (0, r, 0)))
    grid_spec = pltpu.PrefetchScalarGridSpec(
        num_scalar_prefetch=1, grid=(n_split,), in_specs=in_specs, out_specs=tuple(out_specs))
    return pl.pallas_call(
        body, name=name, grid_spec=grid_spec,
        out_shape=tuple(jax.ShapeDtypeStruct(g.shape[1:], F32) for g in gots),
        compiler_params=pltpu.CompilerParams(vmem_limit_bytes=STAGE_VMEM_LIMIT),
    )(j_idx, *parts, *gots)


def _small_exchange(sv, w_mod_sh, cctx, m_cctx, v_cctx, bmod, m_bmod, v_bmod, qg, m_qg, v_qg, kvg, m_kvg, v_kvg,
                    gf, m_gf, v_gf, swap_halves, scatter_parts):
    ncol = w_mod_sh.shape[1]
    n_s, n_p = len(swap_halves), len(scatter_parts)
    n_x = n_s + n_p

    def body(sv_ref, w_ref, cctx_ref, mcc_ref, vcc_ref, b_ref, mb_ref, vb_ref, qg_ref, mq_ref, vq_ref,
             kg_ref, mk_ref, vk_ref, gf_ref, mgf_ref, vgf_ref, *rest):
        all_ref, red_ref, occ_ref, ob_ref, oq_ref, ok_ref, ogf_ref = rest[n_x:n_x + 7]
        vec_ref, part_ref, ssem, rsem, ssem2, rsem2, wsend, wrecv, psend, precv = rest[2 * n_x + 7:]
        swap = _SiblingSwap(rest[:n_s], rest[n_x + 7:n_x + 7 + n_s], wsend, wrecv)
        scatter = _ChipScatter(rest[n_s:n_x], rest[n_x + 7 + n_s:2 * n_x + 7], psend, precv)
        swap.start()
        scatter.start()
        x, y, c = _pos()
        me = 4 * x + 2 * y + c
        j = 2 * x + y
        sends = []
        for r in range(1, 8):
            cp = pltpu.make_async_remote_copy(
                src_ref=sv_ref, dst_ref=all_ref.at[me], send_sem=ssem.at[r - 1], recv_sem=rsem.at[r - 1],
                device_id=_peer(r, x, y, c), device_id_type=MESH)
            cp.start()
            sends.append(cp)
        for cp in sends:
            cp.wait()
        all_ref[me] = sv_ref[...]
        red = all_ref[0]
        for d in range(1, 8):
            red = red + all_ref[d]
        red_ref[...] = red
        vec_ref[...] = jnp.zeros(vec_ref.shape, F32)

        @pl.when(j == 0)
        def _():
            vec_ref[0:1, 0:1024] = red[6:7, :]
            vec_ref[0:1, 1024:1536] = red[7:8, 0:512]

        @pl.when(j == 1)
        def _():
            vec_ref[0:1, 0:512] = red[7:8, 512:1024]

        part = lax.dot_general(vec_ref[...], w_ref[...], (((1,), (1,)), ((), ())), precision=HIGHEST,
                               preferred_element_type=F32)
        part_ref[j] = part
        sends2 = []
        for k, r in enumerate((4, 2, 6)):
            cp = pltpu.make_async_remote_copy(
                src_ref=part_ref.at[j], dst_ref=part_ref.at[j], send_sem=ssem2.at[k], recv_sem=rsem2.at[k],
                device_id=_peer(r, x, y, c), device_id_type=MESH)
            cp.start()
            sends2.append(cp)
        for cp in sends2:
            cp.wait()
        tot = part_ref[0]
        for s in range(1, N_SHARD):
            tot = tot + part_ref[s]
        cc = cctx_ref[...]
        sg = _sigmoid(cc)
        g_cc = tot[0:1, :] * (sg * (1.0 + cc * (1.0 - sg)))
        d_, m_, v_ = _adamw(cc, g_cc, mcc_ref[...], vcc_ref[...])
        occ_ref[0:1, :] = g_cc
        occ_ref[1:2, :] = d_
        occ_ref[2:3, :] = m_
        occ_ref[3:4, :] = v_
        occ_ref[4:8, :] = jnp.zeros((4, D_MODEL), F32)
        g_b = red[0:6, :]
        pad = jnp.concatenate([red[6:8, :], jnp.zeros((4, D_MODEL), F32)], axis=0)
        g_b = g_b + pad
        d_, m_, v_ = _adamw(b_ref[...], g_b, mb_ref[...], vb_ref[...])
        ob_ref[0] = g_b
        ob_ref[1] = d_
        ob_ref[2] = m_
        ob_ref[3] = v_
        g_q = red[9:10, 0:Q_RANK]
        d_, m_, v_ = _adamw(qg_ref[...], g_q, mq_ref[...], vq_ref[...])
        oq_ref[0:1, :] = g_q
        oq_ref[1:2, :] = d_
        oq_ref[2:3, :] = m_
        oq_ref[3:4, :] = v_
        oq_ref[4:8, :] = jnp.zeros((4, Q_RANK), F32)
        g_k = red[10:11, 0:KV_RANK]
        d_, m_, v_ = _adamw(kg_ref[...], g_k, mk_ref[...], vk_ref[...])
        ok_ref[0:1, :] = g_k
        ok_ref[1:2, :] = d_
        ok_ref[2:3, :] = m_
        ok_ref[3:4, :] = v_
        ok_ref[4:8, :] = jnp.zeros((4, KV_RANK), F32)
        g_f = red[8:9, :]
        d_, m_, v_ = _adamw(gf_ref[...], g_f, mgf_ref[...], vgf_ref[...])
        ogf_ref[0:1, :] = g_f
        ogf_ref[1:2, :] = d_
        ogf_ref[2:3, :] = m_
        ogf_ref[3:4, :] = v_
        ogf_ref[4:8, :] = jnp.zeros((4, D_MODEL), F32)
        swap.finish()
        scatter.finish()

    vm = pl.BlockSpec(memory_space=pltpu.VMEM)
    hbm = pl.BlockSpec(memory_space=pl.ANY)
    out_shape = (
        jax.ShapeDtypeStruct((8, 16, D_MODEL), F32),
        jax.ShapeDtypeStruct((16, D_MODEL), F32),
        jax.ShapeDtypeStruct((8, D_MODEL), F32),
        jax.ShapeDtypeStruct((4, 6, D_MODEL), F32),
        jax.ShapeDtypeStruct((8, Q_RANK), F32),
        jax.ShapeDtypeStruct((8, KV_RANK), F32),
        jax.ShapeDtypeStruct((8, D_MODEL), F32),
    ) + tuple(jax.ShapeDtypeStruct(h.shape, F32) for h in swap_halves) + tuple(
        jax.ShapeDtypeStruct((3,) + p.shape[1:], BF16) for p in scatter_parts)
    return pl.pallas_call(
        body, name="small_exchange", out_shape=out_shape, in_specs=[vm] * 17 + [hbm] * n_x,
        out_specs=tuple([vm] * 7) + (hbm,) * n_x,
        scratch_shapes=[pltpu.VMEM((8, ncol), F32), pltpu.VMEM((N_SHARD, 8, D_MODEL), F32),
                        pltpu.SemaphoreType.DMA((7,)), pltpu.SemaphoreType.DMA((7,)),
                        pltpu.SemaphoreType.DMA((3,)), pltpu.SemaphoreType.DMA((3,)),
                        pltpu.SemaphoreType.DMA((n_s,)), pltpu.SemaphoreType.DMA((n_s,)),
                        pltpu.SemaphoreType.DMA((3 * n_p,)), pltpu.SemaphoreType.DMA((3 * n_p,))],
        compiler_params=pltpu.CompilerParams(vmem_limit_bytes=VMEM_LIMIT),
    )(sv, w_mod_sh, cctx, m_cctx, v_cctx, bmod, m_bmod, v_bmod, qg, m_qg, v_qg, kvg, m_kvg, v_kvg, gf, m_gf, v_gf,
      *swap_halves, *scatter_parts)


def _inproj_fwd(x2, ctx2, mod_a, w_in, qg, kvg, w_uq, w_ukv, cos_t, sin_a, sin_b):
    t_lat, t_ctx = x2.shape[0], ctx2.shape[0]
    tm = TOK_TILE
    n_lat = t_lat // tm
    n_all = n_lat + t_ctx // tm
    e_rows = t_lat + t_ctx

    def body(x_ref, ctx_ref, mod_ref, win_ref, qg_ref, kvg_ref, wuq_ref, wukv_ref, cos_ref, sa_ref, sb_ref,
             z_ref, q_ref, k_ref, v_ref, kt_ref):
        i = pl.program_id(0)
        xin = jnp.where(i < n_lat, x_ref[...], ctx_ref[...])
        xn = xin * lax.rsqrt(jnp.mean(xin * xin, axis=-1, keepdims=True) + EPS)
        h1 = (xn * (1.0 + mod_ref[0, 1:2, :]) + mod_ref[0, 0:1, :]).astype(BF16)
        z = _dot(h1, win_ref[...])
        z_ref[...] = z
        cos, sa, sb = cos_ref[...], sa_ref[...], sb_ref[...]
        cq = z[:, 0:Q_RANK]
        cqn = (cq * lax.rsqrt(jnp.mean(cq * cq, axis=-1, keepdims=True) + EPS) * qg_ref[...]).astype(BF16)
        q = _dot(cqn, wuq_ref[...])
        ckv = z[:, Q_RANK:Q_RANK + KV_RANK]
        ckvn = (ckv * lax.rsqrt(jnp.mean(ckv * ckv, axis=-1, keepdims=True) + EPS) * kvg_ref[...]).astype(BF16)
        kv = _dot(ckvn, wukv_ref[...])
        kr = _rope(z[:, Q_RANK + KV_RANK:Q_RANK + KV_RANK + HEAD_PAD], cos, sa, sb)
        ones_lane = lax.broadcasted_iota(jnp.int32, (tm, HEAD_PAD), 1) == V_DIM
        for h in range(N_HEADS):
            lo = h * HEAD_PAD
            q_ref[h] = _rope(q[:, lo:lo + HEAD_PAD], cos, sa, sb).astype(BF16)
            kh = kv[:, lo:lo + HEAD_PAD] + kr
            k_ref[h] = kh.astype(BF16)
            kt_ref[h] = kh.T.astype(BF16)
            vh = kv[:, N_HEADS * HEAD_PAD + lo:N_HEADS * HEAD_PAD + lo + HEAD_PAD]
            v_ref[h] = jnp.where(ones_lane, 1.0, vh).astype(BF16)

    row = lambda i: (i, 0)
    head_spec = pl.BlockSpec((N_HEADS, tm, HEAD_PAD), lambda i: (0, i, 0))
    head_shape = jax.ShapeDtypeStruct((N_HEADS, e_rows, HEAD_PAD), BF16)
    return pl.pallas_call(
        body, name="inproj_fwd", grid=(n_all,),
        out_shape=(jax.ShapeDtypeStruct((e_rows, Z_COLS), F32), head_shape, head_shape, head_shape,
                   jax.ShapeDtypeStruct((N_HEADS, HEAD_PAD, e_rows), BF16)),
        in_specs=[
            pl.BlockSpec((tm, D_MODEL), lambda i: (jnp.minimum(i, n_lat - 1), 0)),
            _const_spec((tm, D_MODEL)),
            pl.BlockSpec((1, 8, D_MODEL), lambda i: (i // n_lat, 0, 0)),
            _const_spec(w_in.shape), _const_spec(qg.shape), _const_spec(kvg.shape),
            _const_spec(w_uq.shape), _const_spec(w_ukv.shape),
            pl.BlockSpec((tm, HEAD_PAD), row), pl.BlockSpec((tm, HEAD_PAD), row), pl.BlockSpec((tm, HEAD_PAD), row),
        ],
        out_specs=(pl.BlockSpec((tm, Z_COLS), row), head_spec, head_spec, head_spec,
                   pl.BlockSpec((N_HEADS, HEAD_PAD, tm), lambda i: (0, 0, i))),
        compiler_params=pltpu.CompilerParams(vmem_limit_bytes=VMEM_LIMIT),
    )(x2, ctx2, mod_a, w_in, qg, kvg, w_uq, w_ukv, cos_t, sin_a, sin_b)


def _key_chunks(e_rows, size):
    n_chunks = max(1, e_rows // size)
    return [(ci * size, size if ci < n_chunks - 1 else e_rows - ci * size) for ci in range(n_chunks)]


def _attn_fwd(q, k, v, t_lat, shard_arrays):
    e_rows = k.shape[1]
    tq = min(t_lat, ATTN_FWD_Q_BLOCK)
    bounds = _key_chunks(e_rows, KEY_CHUNK)
    c2 = ATTN_SCALE * LOG2E

    hb = ATTN_HEADS_PER_STEP
    n_hb = N_HEADS // hb

    def body(q_ref, k_ref, v_ref, o_ref, lse_ref):
        qs = [q_ref[b] for b in range(hb)]
        m, acc = [None] * hb, [None] * hb
        for lo, n in bounds:
            for b in range(hb):
                s = _dot_nt(qs[b], k_ref[b, lo:lo + n, :])
                mc = jnp.max(s, axis=-1, keepdims=True)
                m_new = mc if m[b] is None else jnp.maximum(m[b], mc)
                p = jnp.exp2((s - m_new) * c2)
                pv = _dot(p.astype(BF16), v_ref[b, lo:lo + n, :])
                acc[b] = pv if m[b] is None else acc[b] * jnp.exp2((m[b] - m_new) * c2) + pv
                m[b] = m_new
        outs = []
        for b in range(hb):
            l = acc[b][:, V_DIM:V_DIM + 1]
            outs.append(acc[b] * (1.0 / l))
            lse = (m[b] * ATTN_SCALE + jnp.log(l)) * LOG2E
            lse_ref[b] = jnp.broadcast_to(lse, (tq, HEAD_PAD)).T[0:1, :]
        low = lax.broadcasted_iota(jnp.int32, (tq, HEAD_PAD), 1) < V_DIM
        pairs = [jnp.where(low, outs[b], pltpu.roll(outs[b + 1], V_DIM, 1)) for b in range(0, hb, 2)]
        o_ref[...] = jnp.concatenate(pairs, axis=1).astype(BF16)

    n_w = len(shard_arrays)
    n_q = t_lat // tq

    def body_with_gather(q_ref, k_ref, v_ref, *rest):
        o_ref, lse_ref = rest[n_w], rest[n_w + 1]
        gather = _ShardGather(rest[n_w + 2:2 * n_w + 2], *rest[2 * n_w + 2:])
        step = pl.program_id(0) * n_q + pl.program_id(1)
        pl.when(step == 0)(gather.start)
        pl.when(step == n_hb * n_q // 2)(gather.forward)
        body(q_ref, k_ref, v_ref, o_ref, lse_ref)
        pl.when(step == n_hb * n_q - 1)(gather.finish)

    hbm = pl.BlockSpec(memory_space=pl.ANY)
    return pl.pallas_call(
        body_with_gather, name="attn_fwd", grid=(n_hb, n_q),
        out_shape=(jax.ShapeDtypeStruct((t_lat, N_HEADS * V_DIM), BF16),
                   jax.ShapeDtypeStruct((N_HEADS, 1, t_lat), F32))
        + tuple(jax.ShapeDtypeStruct(a.shape, a.dtype) for a in shard_arrays),
        in_specs=[pl.BlockSpec((hb, tq, HEAD_PAD), lambda h, i: (h, i, 0)),
                  pl.BlockSpec((hb, e_rows, HEAD_PAD), lambda h, i: (h, 0, 0)),
                  pl.BlockSpec((hb, e_rows, HEAD_PAD), lambda h, i: (h, 0, 0))] + [hbm] * n_w,
        out_specs=(pl.BlockSpec((tq, hb * V_DIM), lambda h, i: (i, h)),
                   pl.BlockSpec((hb, 1, tq), lambda h, i: (h, 0, i))) + (hbm,) * n_w,
        input_output_aliases={3 + a: 2 + a for a in range(n_w)},
        scratch_shapes=_gather_sems(n_w),
        compiler_params=pltpu.CompilerParams(vmem_limit_bytes=VMEM_LIMIT),
    )(q, k, v, *shard_arrays)


def _attn_bwd(q, k, v, kt, o, do, lse_row, t_lat, parts):
    e_rows = k.shape[1]
    tq = min(t_lat, ATTN_BWD_Q_BLOCK)
    n_p = len(parts)
    n_q = t_lat // tq
    bounds = _key_chunks(e_rows, KEY_CHUNK_BWD)

    hb = ATTN_BWD_HEADS_PER_STEP
    assert hb * V_DIM == HEAD_PAD, "a step's heads share one lane tile of the unpadded o / dO"
    n_hb = N_HEADS // hb

    def body(q_ref, k_ref, v_ref, kt_ref, o_ref, do_ref, lse_ref, *rest):
        dqt_ref, dk_ref, dv_ref = rest[n_p:n_p + 3]
        scatter = _ChipScatter(rest[:n_p], rest[n_p + 3:2 * n_p + 3], rest[2 * n_p + 3], rest[2 * n_p + 4])
        h, i = pl.program_id(0), pl.program_id(1)
        pl.when(jnp.logical_and(h == 0, i == 0))(scatter.start)

        @pl.when(i == 0)
        def _():
            dk_ref[...] = jnp.zeros(dk_ref.shape, F32)
            dv_ref[...] = jnp.zeros(dv_ref.shape, F32)

        do_pair = do_ref[...].astype(F32)
        prod = o_ref[...].astype(F32) * do_pair
        lane = lax.broadcasted_iota(jnp.int32, (tq, HEAD_PAD), 1)
        sel = lax.broadcasted_iota(jnp.int32, (8, HEAD_PAD), 1)
        qs, dos, lses, deltas = [], [], [], []
        for b in range(hb):
            qs.append(q_ref[b])
            mine = do_pair if b == 0 else pltpu.roll(do_pair, V_DIM, 1)
            dos.append(jnp.where(lane < V_DIM, mine, 0.0).astype(BF16))
            lses.append(lse_ref[b])
            ones = jnp.where((sel < V_DIM) == (b == 0), 1.0, 0.0)
            deltas.append(lax.dot_general(ones, prod, (((1,), (1,)), ((), ())), precision=HIGHEST,
                                          preferred_element_type=F32)[0:1, :])
        dqt = [None] * hb
        for lo, n in bounds:
            for b in range(hb):
                pt = jnp.exp2(_dot_nt(k_ref[b, lo:lo + n, :], qs[b]) * (ATTN_SCALE * LOG2E) - lses[b])
                dpt = _dot_nt(v_ref[b, lo:lo + n, :], dos[b])
                dst = (pt * (dpt - deltas[b])).astype(BF16)
                dv_c = _dot(pt.astype(BF16), dos[b])
                dk_c = _dot(dst, qs[b])
                part = _dot(kt_ref[b, :, lo:lo + n], dst)
                dqt[b] = part if dqt[b] is None else dqt[b] + part
                dk_ref[b, lo:lo + n, :] += dk_c * ATTN_SCALE
                dv_ref[b, lo:lo + n, :] += dv_c
        for b in range(hb):
            dqt_ref[b] = (dqt[b] * ATTN_SCALE).T

        pl.when(jnp.logical_and(h == n_hb - 1, i == n_q - 1))(scatter.finish)

    hbm = pl.BlockSpec(memory_space=pl.ANY)
    qspec = pl.BlockSpec((hb, tq, HEAD_PAD), lambda h, i: (h, i, 0))
    kspec = pl.BlockSpec((hb, e_rows, HEAD_PAD), lambda h, i: (h, 0, 0))
    pair = pl.BlockSpec((tq, hb * V_DIM), lambda h, i: (i, h))
    return pl.pallas_call(
        body, name="attn_bwd", grid=(n_hb, n_q),
        out_shape=(jax.ShapeDtypeStruct((N_HEADS, t_lat, HEAD_PAD), F32),
                   jax.ShapeDtypeStruct((N_HEADS, e_rows, HEAD_PAD), F32),
                   jax.ShapeDtypeStruct((N_HEADS, e_rows, HEAD_PAD), F32))
        + tuple(jax.ShapeDtypeStruct((3,) + p.shape[1:], BF16) for p in parts),
        in_specs=[qspec, kspec, kspec, pl.BlockSpec((hb, HEAD_PAD, e_rows), lambda h, i: (h, 0, 0)), pair, pair,
                  pl.BlockSpec((hb, 1, tq), lambda h, i: (h, 0, i))] + [hbm] * n_p,
        out_specs=(qspec, kspec, kspec) + (hbm,) * n_p,
        scratch_shapes=[pltpu.SemaphoreType.DMA((3 * n_p,)), pltpu.SemaphoreType.DMA((3 * n_p,))],
        compiler_params=pltpu.CompilerParams(vmem_limit_bytes=VMEM_LIMIT),
    )(q, k, v, kt, o, do, lse_row, *parts)


def _halo_specs(tm, col_block):
    per = tm // 8
    prev = pl.BlockSpec((8, CONV_W), lambda i: (jnp.maximum(i * per - 1, 0), col_block))
    nxt = pl.BlockSpec((8, CONV_W), lambda i: ((i + 1) * per, col_block))
    return prev, nxt


def _mlp_fwdbwd(o, z, x2, tgt, mod_b, gf, cw, w_out, w1, w2):
    t_lat = x2.shape[0]
    tm = TOK_TILE
    n_lat = t_lat // tm
    fc = MLP_FF_CHUNK
    n_ff = D_FF // fc

    def body(o_ref, gb_ref, gc_ref, xi_ref, gcp_ref, xip_ref, gcn_ref, xin_ref, cw_ref, wout_ref,
             x_ref, t_ref, mod_ref, gf_ref, w1_ref, w2_ref,
             r_ref, da_ref, h2_ref, dy2_ref, dx1_ref, conv_ref, acc_ref, dy1_ref, do_ref, dgb_ref, dyv_ref, ra_ref):
        i = pl.program_id(0)

        @pl.when(i == 0)
        def _():
            acc_ref[...] = jnp.zeros(acc_ref.shape, F32)

        g1, sh2, sc2, g2 = mod_ref[0:1, :], mod_ref[1:2, :], mod_ref[2:3, :], mod_ref[3:4, :]
        u = gc_ref[...] * xi_ref[...]
        u_prev = jnp.where(i > 0, gcp_ref[7:8, :] * xip_ref[7:8, :], 0.0)
        u_next = jnp.where(i < n_lat - 1, gcn_ref[0:1, :] * xin_ref[0:1, :], 0.0)
        um1, up1 = _shift_rows(u, u_prev, u_next)
        yv = cw_ref[0:1, :] * um1 + cw_ref[1:2, :] * u + cw_ref[2:3, :] * up1
        gb = gb_ref[...]
        conv = (gb * yv).astype(BF16)
        conv_ref[...] = conv
        n_attn = N_HEADS * V_DIM
        y1 = _dot(o_ref[...], wout_ref[0:n_attn, :]) + _dot(conv, wout_ref[n_attn:, :])
        x1 = x_ref[...] + g1 * y1
        rstd2 = lax.rsqrt(jnp.mean(x1 * x1, axis=-1, keepdims=True) + EPS)
        xn1 = x1 * rstd2
        h2 = (xn1 * (1.0 + sc2) + sh2).astype(BF16)
        h2_ref[...] = h2
        y2 = jnp.zeros((tm, D_MODEL), F32)
        for jj in range(n_ff):
            lo = jj * fc
            ra = jnp.maximum(_dot(h2, w1_ref[lo // FF_CHUNK, :, lo % FF_CHUNK:lo % FF_CHUNK + fc]), 0.0)
            ra_ref[jj] = ra
            r = (ra * ra).astype(BF16)
            r_ref[:, lo:lo + fc] = r
            y2 = y2 + _dot(r, w2_ref[lo:lo + fc, :])
        x2v = x1 + g2 * y2
        rstd3 = lax.rsqrt(jnp.mean(x2v * x2v, axis=-1, keepdims=True) + EPS)
        xn3 = x2v * rstd3
        gfv = gf_ref[...]
        diff = xn3 * gfv - t_ref[...]
        loss_t = 0.5 * jnp.sum(jnp.sum(diff * diff, axis=-1, keepdims=True), axis=0, keepdims=True) * (1.0 / D_MODEL)
        dy = diff * (1.0 / D_MODEL)
        dxn3 = dy * gfv
        dx2 = rstd3 * (dxn3 - xn3 * jnp.mean(dxn3 * xn3, axis=-1, keepdims=True))
        dy2 = (dx2 * g2).astype(BF16)
        dy2_ref[...] = dy2
        dh2 = jnp.zeros((tm, D_MODEL), F32)
        for jj in range(n_ff):
            lo = jj * fc
            dr = _dot_nt(dy2, w2_ref[lo:lo + fc, :])
            da = (2.0 * ra_ref[jj] * dr).astype(BF16)
            da_ref[:, lo:lo + fc] = da
            dh2 = dh2 + _dot_nt(da, w1_ref[lo // FF_CHUNK, :, lo % FF_CHUNK:lo % FF_CHUNK + fc])
        dxn1 = dh2 * (1.0 + sc2)
        dx1 = dx2 + rstd2 * (dxn1 - xn1 * jnp.mean(dxn1 * xn1, axis=-1, keepdims=True))
        dx1_ref[...] = dx1
        dy1 = (dx1 * g1).astype(BF16)
        dy1_ref[...] = dy1
        do_ref[...] = _dot_nt(dy1, wout_ref[0:n_attn, :]).astype(BF16)
        dconv = _dot_nt(dy1, wout_ref[n_attn:, :])
        dgb_ref[...] = dconv * yv
        dyv_ref[...] = dconv * gb
        acc_ref[5:6, :] += jnp.sum(dx1 * y1, axis=0, keepdims=True)
        acc_ref[0:1, :] += jnp.sum(dy * xn3, axis=0, keepdims=True)
        acc_ref[1:2, :] += jnp.sum(dx2 * y2, axis=0, keepdims=True)
        acc_ref[2:3, :] += jnp.sum(dh2, axis=0, keepdims=True)
        acc_ref[3:4, :] += jnp.sum(dh2 * xn1, axis=0, keepdims=True)
        acc_ref[4:5, :] += jnp.broadcast_to(loss_t, (1, D_MODEL))

    row = lambda i: (i, 0)
    gcp, gcn = _halo_specs(tm, 2)
    xip, xin = _halo_specs(tm, 3)
    tile = pl.BlockSpec((tm, D_MODEL), row)
    wide = pl.BlockSpec((tm, D_FF), row)
    half = pl.BlockSpec((tm, CONV_W), row)
    return pl.pallas_call(
        body, name="mlp_fwdbwd", grid=(n_lat,),
        out_shape=(jax.ShapeDtypeStruct((t_lat, D_FF), BF16), jax.ShapeDtypeStruct((t_lat, D_FF), BF16),
                   jax.ShapeDtypeStruct((t_lat, D_MODEL), BF16), jax.ShapeDtypeStruct((t_lat, D_MODEL), BF16),
                   jax.ShapeDtypeStruct((t_lat, D_MODEL), F32), jax.ShapeDtypeStruct((t_lat, CONV_W), BF16),
                   jax.ShapeDtypeStruct((8, D_MODEL), F32),
                   jax.ShapeDtypeStruct((t_lat, D_MODEL), BF16),
                   jax.ShapeDtypeStruct((t_lat, N_HEADS * V_DIM), BF16),
                   jax.ShapeDtypeStruct((t_lat, CONV_W), F32), jax.ShapeDtypeStruct((t_lat, CONV_W), F32)),
        in_specs=[
            pl.BlockSpec((tm, N_HEADS * V_DIM), row),
            pl.BlockSpec((tm, CONV_W), lambda i: (i, 1)), pl.BlockSpec((tm, CONV_W), lambda i: (i, 2)),
            pl.BlockSpec((tm, CONV_W), lambda i: (i, 3)),
            gcp, xip, gcn, xin,
            _const_spec(cw.shape), _resident_spec(w_out.shape),
            tile, tile, _const_spec(mod_b.shape), _const_spec(gf.shape),
            _resident_spec(w1.shape), _resident_spec(w2.shape),
        ],
        out_specs=(wide, wide, tile, tile, tile, half, _const_spec((8, D_MODEL)),
                   tile, pl.BlockSpec((tm, N_HEADS * V_DIM), row), half, half),
        scratch_shapes=[pltpu.VMEM((n_ff, tm, fc), F32)],
        compiler_params=pltpu.CompilerParams(vmem_limit_bytes=VMEM_LIMIT),
    )(o, z, z, z, z, z, z, z, cw, w_out, x2, tgt, mod_b, gf, w1, w2)


def _inproj_bwd(x2, ctx2, mod_a, z, dyv, dgb, dx1, dqt, dk, dv, cos_t, sin_a, sin_b, w_in, w_uq, w_ukv, qg, kvg, cw):
    t_lat, t_ctx = x2.shape[0], ctx2.shape[0]
    tm = TOK_TILE
    n_lat = t_lat // tm
    n_all = n_lat + t_ctx // tm
    group = max(g for g in (1, 2, 4) if n_lat % g == 0)

    def body(x_ref, ctx_ref, mod_ref, z_ref, gcp_ref, xip_ref, gcn_ref, xin_ref, dyv_ref, dyvp_ref, dyvn_ref,
             dgb_ref, dx1_ref, dqt_ref, dk_ref, dv_ref, cos_ref, sa_ref, sb_ref, win_ref, wuq_ref, wukv_ref,
             qg_ref, kvg_ref, cw_ref, gx_ref, dwin_out, dwuq_out, dwukv_out, acc_ref, h1_buf, dz_buf,
             dwin_ref, dwuq_ref, dwukv_ref):
        i = pl.program_id(0)
        lat = i < n_lat

        @pl.when(i == 0)
        def _():
            dwin_ref[...] = jnp.zeros(dwin_ref.shape, F32)
            dwuq_ref[...] = jnp.zeros(dwuq_ref.shape, F32)
            dwukv_ref[...] = jnp.zeros(dwukv_ref.shape, F32)
            acc_ref[...] = jnp.zeros(acc_ref.shape, F32)

        xin = jnp.where(lat, x_ref[...], ctx_ref[...])
        rstd = lax.rsqrt(jnp.mean(xin * xin, axis=-1, keepdims=True) + EPS)
        xn = xin * rstd
        sc = mod_ref[0, 1:2, :]
        h1 = (xn * (1.0 + sc) + mod_ref[0, 0:1, :]).astype(BF16)
        z = z_ref[...]
        cos, sa, sb = cos_ref[...], sa_ref[...], sb_ref[...]
        qgv, kvgv = qg_ref[...], kvg_ref[...]
        cq = z[:, 0:Q_RANK]
        cqh = cq * lax.rsqrt(jnp.mean(cq * cq, axis=-1, keepdims=True) + EPS)
        rq = lax.rsqrt(jnp.mean(cq * cq, axis=-1, keepdims=True) + EPS)
        cqn = (cqh * qgv).astype(BF16)
        parts = []
        for h in range(N_HEADS):
            g = jnp.where(lat, dqt_ref[h], 0.0)
            parts.append(_unrope(g, cos, sa, sb))
        dq = jnp.concatenate([p.astype(BF16) for p in parts], axis=1)
        dcqn = _dot_nt(dq, wuq_ref[...])
        dwuq_ref[...] += _dot_tn(cqn, dq)
        acc_ref[4:5, 0:Q_RANK] += jnp.sum(dcqn * cqh, axis=0, keepdims=True)
        dxn = dcqn * qgv
        dcq = rq * (dxn - cqh * jnp.mean(dxn * cqh, axis=-1, keepdims=True))
        ckv = z[:, Q_RANK:Q_RANK + KV_RANK]
        rk = lax.rsqrt(jnp.mean(ckv * ckv, axis=-1, keepdims=True) + EPS)
        ckvh = ckv * rk
        ckvn = (ckvh * kvgv).astype(BF16)
        dks = [dk_ref[h] for h in range(N_HEADS)]
        dkr = dks[0]
        for h in range(1, N_HEADS):
            dkr = dkr + dks[h]
        dkv = jnp.concatenate([p.astype(BF16) for p in dks + [dv_ref[h] for h in range(N_HEADS)]], axis=1)
        dckvn = _dot_nt(dkv, wukv_ref[...])
        dwukv_ref[...] += _dot_tn(ckvn, dkv)
        acc_ref[5:6, 0:KV_RANK] += jnp.sum(dckvn * ckvh, axis=0, keepdims=True)
        dxn = dckvn * kvgv
        dckv = rk * (dxn - ckvh * jnp.mean(dxn * ckvh, axis=-1, keepdims=True))
        dkr = _unrope(dkr, cos, sa, sb)
        gb, gc, xi = z[:, 512:1024], z[:, 1024:1536], z[:, 1536:2048]
        u = gc * xi
        u_prev = jnp.where(i > 0, gcp_ref[7:8, :] * xip_ref[7:8, :], 0.0)
        u_next = jnp.where(i < n_lat - 1, gcn_ref[0:1, :] * xin_ref[0:1, :], 0.0)
        um1, up1 = _shift_rows(u, u_prev, u_next)
        dyv = jnp.where(lat, dyv_ref[...], 0.0)
        dyv_prev = jnp.where(jnp.logical_and(i > 0, lat), dyvp_ref[7:8, :], 0.0)
        dyv_next = jnp.where(i < n_lat - 1, dyvn_ref[0:1, :], 0.0)
        dyv_m1, dyv_p1 = _shift_rows(dyv, dyv_prev, dyv_next)
        du = cw_ref[0:1, :] * dyv_p1 + cw_ref[1:2, :] * dyv + cw_ref[2:3, :] * dyv_m1
        dgc = du * xi
        dxi = du * gc
        dgb = jnp.where(lat, dgb_ref[...], 0.0)
        acc_ref[6:7, 0:CONV_W] += jnp.sum(dyv * um1, axis=0, keepdims=True)
        acc_ref[7:8, 0:CONV_W] += jnp.sum(dyv * u, axis=0, keepdims=True)
        acc_ref[8:9, 0:CONV_W] += jnp.sum(dyv * up1, axis=0, keepdims=True)
        dz = jnp.concatenate([p.astype(BF16) for p in (dcq, dckv, dkr, dgb, dgc, dxi)], axis=1)
        dh1 = _dot_nt(dz, win_ref[...])
        slot = i % group
        rows_g = pl.ds(pl.multiple_of(slot * tm, tm), tm)
        h1_buf[rows_g, :] = h1
        dz_buf[rows_g, :] = dz

        @pl.when(jnp.logical_and(lat, slot == group - 1))
        def _():
            dwin_ref[...] += _dot_tn(h1_buf[...], dz_buf[...])

        @pl.when(jnp.logical_not(lat))
        def _():
            dwin_ref[...] += _dot_tn(h1, dz)
        s_sh = jnp.sum(dh1, axis=0, keepdims=True)
        s_sc = jnp.sum(dh1 * xn, axis=0, keepdims=True)
        zero = jnp.zeros_like(s_sh)
        acc_ref[0:1, :] += jnp.where(lat, s_sh, zero)
        acc_ref[1:2, :] += jnp.where(lat, s_sc, zero)
        acc_ref[2:3, :] += jnp.where(lat, zero, s_sh)
        acc_ref[3:4, :] += jnp.where(lat, zero, s_sc)
        dxn = dh1 * (1.0 + sc)
        dx = rstd * (dxn - xn * jnp.mean(dxn * xn, axis=-1, keepdims=True))

        @pl.when(lat)
        def _():
            gx_ref[...] = dx1_ref[...] + dx

        @pl.when(i == n_all - 1)
        def _():
            dwin_out[...] = dwin_ref[...].astype(BF16)
            dwuq_out[...] = dwuq_ref[...].astype(BF16)
            dwukv_out[...] = dwukv_ref[...].astype(BF16)

    last = n_lat - 1
    per = tm // 8
    lat_row = lambda i: (jnp.minimum(i, last), 0)
    row = lambda i: (i, 0)
    gcp, gcn = _halo_specs(tm, 2)
    xip, xin = _halo_specs(tm, 3)
    n_halo = t_lat // 8
    dyvp = pl.BlockSpec((8, CONV_W), lambda i: (jnp.clip(i * per - 1, 0, n_halo - 1), 0))
    dyvn = pl.BlockSpec((8, CONV_W), lambda i: (jnp.minimum((i + 1) * per, n_halo - 1), 0))
    gcn = pl.BlockSpec((8, CONV_W), lambda i: (jnp.minimum((i + 1) * per, (t_lat + t_ctx) // 8 - 1), 2))
    xin = pl.BlockSpec((8, CONV_W), lambda i: (jnp.minimum((i + 1) * per, (t_lat + t_ctx) // 8 - 1), 3))
    head_f32 = pl.BlockSpec((N_HEADS, tm, HEAD_PAD), lambda i: (0, i, 0))
    tab = pl.BlockSpec((tm, HEAD_PAD), row)
    return pl.pallas_call(
        body, name="inproj_bwd", grid=(n_all,),
        out_shape=(jax.ShapeDtypeStruct((t_lat, D_MODEL), F32), jax.ShapeDtypeStruct(w_in.shape, BF16),
                   jax.ShapeDtypeStruct(w_uq.shape, BF16), jax.ShapeDtypeStruct(w_ukv.shape, BF16),
                   jax.ShapeDtypeStruct((16, D_MODEL), F32)),
        in_specs=[
            pl.BlockSpec((tm, D_MODEL), lat_row), _const_spec((tm, D_MODEL)),
            pl.BlockSpec((1, 8, D_MODEL), lambda i: (i // n_lat, 0, 0)),
            pl.BlockSpec((tm, Z_COLS), row), gcp, xip, gcn, xin,
            pl.BlockSpec((tm, CONV_W), lat_row), dyvp, dyvn,
            pl.BlockSpec((tm, CONV_W), lat_row), pl.BlockSpec((tm, D_MODEL), lat_row),
            pl.BlockSpec((N_HEADS, tm, HEAD_PAD), lambda i: (0, jnp.minimum(i, last), 0)),
            head_f32, head_f32, tab, tab, tab,
            _const_spec(w_in.shape), _const_spec(w_uq.shape), _const_spec(w_ukv.shape),
            _const_spec(qg.shape), _const_spec(kvg.shape), _const_spec(cw.shape),
        ],
        out_specs=(pl.BlockSpec((tm, D_MODEL), lat_row), _const_spec(w_in.shape), _const_spec(w_uq.shape),
                   _const_spec(w_ukv.shape), _const_spec((16, D_MODEL))),
        scratch_shapes=[pltpu.VMEM((group * tm, D_MODEL), BF16), pltpu.VMEM((group * tm, Z_COLS), BF16),
                        pltpu.VMEM(w_in.shape, F32), pltpu.VMEM(w_uq.shape, F32), pltpu.VMEM(w_ukv.shape, F32)],
        compiler_params=pltpu.CompilerParams(vmem_limit_bytes=VMEM_LIMIT),
    )(x2, ctx2, mod_a, z, z, z, z, z, dyv, dyv, dyv, dgb, dx1, dqt, dk, dv, cos_t, sin_a, sin_b, w_in, w_uq, w_ukv,
      qg, kvg, cw)


def _placement(j, shard_cols):
    src = lax.broadcasted_iota(jnp.int32, (shard_cols, Z_COLS), 0) + j * shard_cols
    col = lax.broadcasted_iota(jnp.int32, (shard_cols, Z_COLS), 1)
    rope0 = Q_RANK + KV_RANK
    shift = jnp.where(src < rope0, 0, jnp.where(src < rope0 + QK_ROPE, ROPE_LANE0, HEAD_PAD - QK_ROPE))
    return jnp.where(col == src + shift, 1.0, 0.0).astype(BF16)


def _place_w_in(g_in):
    n_sh, rows, sw = g_in.shape

    def body(g_ref, o_ref):
        acc = _dot(g_ref[0], _placement(0, sw))
        for j in range(1, n_sh):
            acc = acc + _dot(g_ref[j], _placement(j, sw))
        o_ref[...] = acc.astype(BF16)

    return pl.pallas_call(
        body, name="place_w_in", grid=(1,), out_shape=jax.ShapeDtypeStruct((rows, Z_COLS), BF16),
        in_specs=[_const_spec(g_in.shape)], out_specs=_const_spec((rows, Z_COLS)),
        compiler_params=pltpu.CompilerParams(vmem_limit_bytes=STAGE_VMEM_LIMIT),
    )(g_in)


def _unplace_d_w_in(d_win, n_sh, sw):
    rows = d_win.shape[0]

    def body(d_ref, o_ref):
        o_ref[0] = _dot_nt(d_ref[...], _placement(pl.program_id(0), sw)).astype(BF16)

    return pl.pallas_call(
        body, name="unplace_d_w_in", grid=(n_sh,), out_shape=jax.ShapeDtypeStruct((n_sh, rows, sw), BF16),
        in_specs=[_const_spec(d_win.shape)], out_specs=pl.BlockSpec((1, rows, sw), lambda j: (j, 0, 0)),
        compiler_params=pltpu.CompilerParams(vmem_limit_bytes=STAGE_VMEM_LIMIT),
    )(d_win)


def _wgrad(a, b, name, bm, bn):
    t, m = a.shape
    n = b.shape[1]
    bk = min(t, 4096)
    nk = t // bk
    nj = n // bn

    def body(a_ref, b_ref, o_ref, acc_ref):
        k = pl.program_id(2)
        part = _dot_tn(a_ref[...], b_ref[...])

        @pl.when(k == 0)
        def _():
            acc_ref[...] = part

        @pl.when(k > 0)
        def _():
            acc_ref[...] += part

        @pl.when(k == nk - 1)
        def _():
            o_ref[...] = acc_ref[...].astype(BF16)

    return pl.pallas_call(
        body, name=name, grid=(m // bm, nj, nk), out_shape=jax.ShapeDtypeStruct((m // bm * nj, bm, bn), BF16),
        in_specs=[pl.BlockSpec((bk, bm), lambda i, j, k: (k, i)), pl.BlockSpec((bk, bn), lambda i, j, k: (k, j))],
        out_specs=pl.BlockSpec((None, bm, bn), lambda i, j, k: (i * nj + j, 0, 0)),
        scratch_shapes=[pltpu.VMEM((bm, bn), F32)],
        compiler_params=pltpu.CompilerParams(vmem_limit_bytes=VMEM_LIMIT),
    )(a, b)


def _wgrad_out(o, conv, dy1, sib_arrays):
    t = o.shape[0]
    bk = min(t, 2048)
    nk = t // bk
    n_s = len(sib_arrays)

    def body(o_ref, c_ref, d_ref, *rest):
        w_ref, got_w_ref, acc_ref = rest[n_s], rest[2 * n_s + 1], rest[2 * n_s + 2]
        send = _SiblingSend(rest[:n_s], rest[n_s + 1:2 * n_s + 1], rest[2 * n_s + 3], rest[2 * n_s + 4])
        wsend, wrecv = rest[2 * n_s + 5], rest[2 * n_s + 6]
        k = pl.program_id(0)
        pl.when(k == 0)(send.start)
        part = _dot_tn(jnp.concatenate([o_ref[...], c_ref[...]], axis=1), d_ref[...])

        @pl.when(k == 0)
        def _():
            acc_ref[...] = part

        @pl.when(k > 0)
        def _():
            acc_ref[...] += part

        @pl.when(k == nk - 1)
        def _():
            w_ref[...] = acc_ref[...].astype(BF16)
            x, y, c = _pos()
            own = []
            for s in range(N_SHARD):
                theirs = pl.ds(pl.multiple_of(s * shard_rows + (1 - c) * (shard_rows // 2), 16), shard_rows // 2)
                cp = pltpu.make_async_remote_copy(
                    src_ref=w_ref.at[theirs], dst_ref=got_w_ref.at[s], send_sem=wsend.at[s], recv_sem=wrecv.at[s],
                    device_id=(x, y, 1 - c), device_id_type=MESH)
                cp.start()
                own.append(cp)
            for cp in own:
                cp.wait()

        pl.when(k == nk - 1)(send.finish)

    hbm = pl.BlockSpec(memory_space=pl.ANY)
    shard_rows = D_MODEL // N_SHARD
    return pl.pallas_call(
        body, name="wgrad_out", grid=(nk,),
        out_shape=(jax.ShapeDtypeStruct((D_MODEL, D_MODEL), BF16),)
        + tuple(jax.ShapeDtypeStruct((N_SHARD, a.shape[1] // 2, a.shape[2]), BF16) for a in sib_arrays)
        + (jax.ShapeDtypeStruct((N_SHARD, shard_rows // 2, D_MODEL), BF16),),
        in_specs=[pl.BlockSpec((bk, N_HEADS * V_DIM), lambda k: (k, 0)),
                  pl.BlockSpec((bk, CONV_W), lambda k: (k, 0)),
                  pl.BlockSpec((bk, D_MODEL), lambda k: (k, 0))] + [hbm] * n_s,
        out_specs=(_const_spec((D_MODEL, D_MODEL)),) + (hbm,) * (n_s + 1),
        scratch_shapes=[pltpu.VMEM((D_MODEL, D_MODEL), F32), pltpu.SemaphoreType.DMA((n_s,)),
                        pltpu.SemaphoreType.DMA((n_s,)), pltpu.SemaphoreType.DMA((N_SHARD,)),
                        pltpu.SemaphoreType.DMA((N_SHARD,))],
        compiler_params=pltpu.CompilerParams(vmem_limit_bytes=VMEM_LIMIT),
    )(o, conv, dy1, *sib_arrays)


ADAMW_STEPS = 4


def _adamw_call(ws, gs, ms, vs, name):
    n = len(ws)
    steps = ADAMW_STEPS if all(w.shape[0] % (8 * ADAMW_STEPS) == 0 for w in ws) else 1

    def body(*refs):
        for a in range(n):
            w_ref, g_ref, m_ref, v_ref = (refs[k * n + a] for k in range(4))
            d_, m_, v_ = _adamw(w_ref[...], g_ref[...], m_ref[...], v_ref[...])
            for k, val in enumerate((d_, m_, v_)):
                refs[(4 + k) * n + a][...] = val

    specs = [pl.BlockSpec((w.shape[0] // steps, w.shape[1]), lambda i: (i, 0)) for w in ws]
    shapes = [jax.ShapeDtypeStruct(w.shape, F32) for w in ws]
    outs = pl.pallas_call(
        body, name=name, grid=(steps,), out_shape=tuple(shapes * 3),
        in_specs=specs * 4, out_specs=tuple(specs * 3),
    )(*ws, *gs, *ms, *vs)
    return [tuple(outs[k * n + a] for k in range(3)) for a in range(n)]


def _adamw_halves(ws, g_mine, g_theirs, ms, vs, c_idx, name):
    n = len(ws)
    steps = ADAMW_STEPS
    nb = steps // 2

    def body(c_ref, *refs):
        mine = pl.program_id(0) // nb == c_ref[0]
        for a in range(n):
            w_ref, gm_ref, gt_ref, m_ref, v_ref = (refs[k * n + a] for k in range(5))
            g = jnp.where(mine, gm_ref[...], gt_ref[...])
            d_, m_, v_ = _adamw(w_ref[...], g, m_ref[...], v_ref[...])
            for k, val in enumerate((g, d_, m_, v_)):
                refs[(5 + k) * n + a][...] = val

    specs = [pl.BlockSpec((w.shape[0] // steps, w.shape[1]), lambda i, c_ref: (i, 0)) for w in ws]
    hspecs = [pl.BlockSpec((w.shape[0] // steps, w.shape[1]), lambda i, c_ref: (i % nb, 0)) for w in ws]
    shapes = [jax.ShapeDtypeStruct(w.shape, F32) for w in ws]
    grid_spec = pltpu.PrefetchScalarGridSpec(
        num_scalar_prefetch=1, grid=(steps,), in_specs=specs + hspecs + hspecs + specs + specs,
        out_specs=tuple(specs * 4))
    outs = pl.pallas_call(
        body, name=name, grid_spec=grid_spec, out_shape=tuple(shapes * 4),
        compiler_params=pltpu.CompilerParams(vmem_limit_bytes=VMEM_LIMIT),
    )(c_idx, *ws, *g_mine, *g_theirs, *ms, *vs)
    return [tuple(outs[k * n + a] for k in range(4)) for a in range(n)]


def _wmod_update(s_t, dm, w, m, v):
    rows, cols = w.shape
    cb = 512

    def body(s_ref, dm_ref, w_ref, m_ref, v_ref, g_ref, d_ref, nm_ref, nv_ref):
        g = jnp.dot(s_ref[...], dm_ref[...], precision=HIGHEST, preferred_element_type=F32)
        d_, m_, v_ = _adamw(w_ref[...], g, m_ref[...], v_ref[...])
        g_ref[...] = g
        d_ref[...] = d_
        nm_ref[...] = m_
        nv_ref[...] = v_

    spec = pl.BlockSpec((rows, cb), lambda i: (0, i))
    shp = jax.ShapeDtypeStruct((rows, cols), F32)
    return pl.pallas_call(
        body, name="wmod_update", grid=(cols // cb,), out_shape=(shp, shp, shp, shp),
        in_specs=[_const_spec(s_t.shape), pl.BlockSpec((16, cb), lambda i: (0, i)), spec, spec, spec],
        out_specs=(spec, spec, spec, spec),
        compiler_params=pltpu.CompilerParams(vmem_limit_bytes=VMEM_LIMIT),
    )(s_t, dm, w, m, v)


def _rope_tables(t_lat, t_ctx):
    t = jnp.arange(t_lat)
    pos = jnp.stack([(t // GRID_W).astype(F32), (t % GRID_W).astype(F32)], axis=1)
    half = QK_ROPE // 4
    freqs = ROPE_THETA ** (-jnp.arange(0, 2 * half, 2, dtype=F32) / (2 * half))
    ang = pos[:, :, None] * freqs[None, None, :]
    cos, sin = jnp.cos(ang), jnp.sin(ang)
    zero = jnp.zeros_like(sin)
    cos32 = jnp.concatenate([cos, cos], axis=2).reshape(t_lat, QK_ROPE)
    sa32 = jnp.concatenate([zero, sin], axis=2).reshape(t_lat, QK_ROPE)
    sb32 = jnp.concatenate([-sin, zero], axis=2).reshape(t_lat, QK_ROPE)

    def widen(tab, fill):
        left = jnp.full((t_lat, ROPE_LANE0), fill, F32)
        right = jnp.full((t_lat, HEAD_PAD - ROPE_LANE0 - QK_ROPE), fill, F32)
        lat = jnp.concatenate([left, tab, right], axis=1)
        return jnp.concatenate([lat, jnp.full((t_ctx, HEAD_PAD), fill, F32)], axis=0)

    return widen(cos32, 1.0), widen(sa32, 0.0), widen(sb32, 0.0)


def _cols_from_shards(s):
    return jnp.transpose(s, (1, 0, 2)).reshape(s.shape[1], -1)


def _cols_to_shards(w):
    k, n = w.shape
    return jnp.transpose(w.reshape(k, N_SHARD, n // N_SHARD), (1, 0, 2))


def kernel(x, c, ctx, c_ctx, w_mod, b_mod, w_in, q_norm_g, w_uq, kv_norm_g, w_ukv, conv_w, w_out, w_mlp1, w_mlp2, final_norm_g, loss_target, m_c_ctx, m_w_mod, m_b_mod, m_w_in, m_q_norm_g, m_w_uq, m_kv_norm_g, m_w_ukv, m_conv_w, m_w_out, m_w_mlp1, m_w_mlp2, m_final_norm_g, v_c_ctx, v_w_mod, v_b_mod, v_w_in, v_q_norm_g, v_w_uq, v_kv_norm_g, v_w_ukv, v_conv_w, v_w_out, v_w_mlp1, v_w_mlp2, v_final_norm_g):
    t_lat, t_ctx = x.shape[1], ctx.shape[1]
    assert t_ctx == TOK_TILE and t_lat % TOK_TILE == 0 and t_lat % GRID_W == 0
    mx, my, mc = _pos()
    j = 2 * mx + my
    ncol = w_mod.shape[2]
    x2, ctx2, tgt = x[0], ctx[0], loss_target[0]
    cctx_row = c_ctx.reshape(1, D_MODEL)

    b_sh = lax.dynamic_slice(b_mod, (0, j * ncol), (1, ncol))
    cw_pad = jnp.zeros((8, 128), F32).at[0:3, :].set(conv_w[0])
    c8, m_all, g_in, g_uq, g_ukv, g_out, g_m1, g_m2 = _prologue(
        c, cctx_row, w_mod[0], b_sh, cw_pad, (w_in[0], w_uq[0], w_ukv[0], w_out[0], w_mlp1[0], w_mlp2[0]), 3)
    mvec = m_all[:, 0, :].reshape(6, D_MODEL)
    mctx = m_all[:, 8, :].reshape(6, D_MODEL)
    zeros6 = jnp.zeros((6, D_MODEL), F32)
    mod_a = jnp.stack([jnp.concatenate([mvec[0:2], zeros6], axis=0), jnp.concatenate([mctx[0:2], zeros6], axis=0)])
    mod_b = jnp.concatenate([mvec[2:6], jnp.zeros((4, D_MODEL), F32)], axis=0)
    cw_full = jnp.pad(jnp.transpose(m_all[:, 9:12, 0:128], (1, 0, 2)).reshape(3, CONV_W), ((0, 5), (0, 0)))

    w_in_p = _place_w_in(g_in)
    w_uq_f = _cols_from_shards(g_uq).reshape(Q_RANK, N_HEADS, QK_DIM)
    w_uq_p = jnp.pad(w_uq_f, ((0, 0), (0, 0), (0, HEAD_PAD - QK_DIM))).reshape(Q_RANK, N_HEADS * HEAD_PAD)
    w_ukv_f = _cols_from_shards(g_ukv).reshape(KV_RANK, N_HEADS, QK_NOPE + V_DIM)
    padh = lambda a: jnp.pad(a, ((0, 0), (0, 0), (0, HEAD_PAD - a.shape[2]))).reshape(KV_RANK, N_HEADS * HEAD_PAD)
    w_ukv_p = jnp.concatenate([padh(w_ukv_f[:, :, :QK_NOPE]), padh(w_ukv_f[:, :, QK_NOPE:])], axis=1)
    cos_t, sin_a, sin_b = _rope_tables(t_lat, t_ctx)
    gf_row = final_norm_g.reshape(1, D_MODEL)
    c_idx = mc.reshape(1).astype(jnp.int32)
    j_idx = j.reshape(1).astype(jnp.int32)

    z, q, k, v, kt = _inproj_fwd(x2, ctx2, mod_a, w_in_p, q_norm_g, kv_norm_g, w_uq_p, w_ukv_p, cos_t, sin_a, sin_b)
    o, lse, g_out, w1, g_m2 = _attn_fwd(q, k, v, t_lat, (g_out, g_m1, g_m2))
    w_out_f = g_out.reshape(D_MODEL, D_MODEL)
    w2 = g_m2.reshape(D_FF, D_MODEL)
    r, da, h2, dy2, dx1, conv, acc_mlp, dy1, do, dgb, dyv = _mlp_fwdbwd(o, z, x2, tgt, mod_b, gf_row, cw_full, w_out_f,
                                                                         w1, w2)
    d_w1 = _wgrad(h2, da, "wgrad_mlp1", D_MODEL, FF_CHUNK)
    d_w2 = _wgrad(r, dy2, "wgrad_mlp2", FF_CHUNK, D_MODEL)
    d_wout, *big_got = _wgrad_out(o, conv, dy1, (d_w1, d_w2))
    d_wout = d_wout.reshape(N_SHARD, D_MODEL // N_SHARD, D_MODEL)
    big_grads = (d_w1, d_w2, d_wout)
    big_parts = _add_pairs(big_grads, big_got, c_idx, "rs_add_pairs_big")
    dqt, dk, dv, *big_recv = _attn_bwd(q, k, v, kt, o, do, lse, t_lat, big_parts)
    big_halves = _add_chips(big_parts, big_recv, j_idx, "rs_add_chips_big")
    gx, d_win, d_wuq, d_wukv, acc_in = _inproj_bwd(x2, ctx2, mod_a, z, dyv, dgb, dx1, dqt, dk, dv, cos_t, sin_a, sin_b,
                                                   w_in_p, w_uq_p, w_ukv_p, q_norm_g, kv_norm_g, cw_full)

    d_win_sh = _unplace_d_w_in(d_win, N_SHARD, w_in.shape[2])
    d_wuq_f = d_wuq.reshape(Q_RANK, N_HEADS, HEAD_PAD)[:, :, 0:QK_DIM].reshape(Q_RANK, N_HEADS * QK_DIM)
    d_wukv3 = d_wukv.reshape(KV_RANK, 2, N_HEADS, HEAD_PAD)
    d_wukv_f = jnp.concatenate([d_wukv3[:, 0, :, 0:QK_NOPE], d_wukv3[:, 1, :, 0:V_DIM]], axis=2).reshape(KV_RANK, -1)
    rest = (d_win_sh,) + tuple(_cols_to_shards(a).astype(BF16) for a in (d_wuq_f, d_wukv_f))
    rest_got = _rs_sibling(rest, "rs_sibling_rest")
    rest_parts = _add_pairs(rest, rest_got, c_idx, "rs_add_pairs_rest")

    sv = jnp.concatenate([
        acc_in[0:2], acc_mlp[5:6], acc_mlp[2:4], acc_mlp[1:2],
        acc_in[2:4], acc_mlp[0:1], acc_in[4:5], acc_in[5:6], acc_in[6:9], acc_mlp[4:5],
        jnp.zeros((1, D_MODEL), F32)], axis=0)
    all_sv, red, o_cc, o_b, o_q, o_k, o_gf, *exchanged = _small_exchange(
        sv, w_mod[0], cctx_row, m_c_ctx.reshape(1, D_MODEL), v_c_ctx.reshape(1, D_MODEL),
        b_mod.reshape(6, D_MODEL), m_b_mod.reshape(6, D_MODEL), v_b_mod.reshape(6, D_MODEL),
        q_norm_g, m_q_norm_g, v_q_norm_g, kv_norm_g, m_kv_norm_g, v_kv_norm_g,
        gf_row, m_final_norm_g.reshape(1, D_MODEL), v_final_norm_g.reshape(1, D_MODEL), big_halves, rest_parts)
    big_theirs, rest_recv = exchanged[:len(big_halves)], exchanged[len(big_halves):]
    loss = red[14, 0]

    c9 = jnp.concatenate([c8[0::8], jnp.zeros((7, D_MODEL), F32)], axis=0)
    s_t = jnp.transpose(c9 * jax.nn.sigmoid(c9))
    dm_ex = all_sv[:, 0:6, :].reshape(8, 6 * D_MODEL)
    dm_ctx = jnp.concatenate([red[6:8].reshape(1, 2 * D_MODEL), jnp.zeros((1, 4 * D_MODEL), F32)], axis=1)
    dm16 = jnp.concatenate([dm_ex, dm_ctx, jnp.zeros((7, 6 * D_MODEL), F32)], axis=0)
    dm_sh = lax.dynamic_slice(dm16, (0, j * ncol), (16, ncol))
    g_wmod, d_wmod, nm_wmod, nv_wmod = _wmod_update(s_t, dm_sh, w_mod[0], m_w_mod[0], v_w_mod[0])

    g_cw = lax.dynamic_slice(red[11:14, 0:CONV_W], (0, j * 128), (3, 128))
    (d_cw, nm_cw, nv_cw), = _adamw_call([conv_w[0]], [g_cw], [m_conv_w[0]], [v_conv_w[0]], "adamw_conv")

    rest_halves = _add_chips(rest_parts, rest_recv, j_idx, "rs_add_chips_rest")
    g_win, g_wuq, g_wukv = _rs_join(rest_halves, "rs_join_rest")
    upd = dict(zip(("in", "uq", "ukv"), _adamw_call(
        [w_in[0], w_uq[0], w_ukv[0]], [g_win, g_wuq, g_wukv], [m_w_in[0], m_w_uq[0], m_w_ukv[0]],
        [v_w_in[0], v_w_uq[0], v_w_ukv[0]], "adamw_rest")))
    (g_w1, *upd["mlp1"]), (g_w2, *upd["mlp2"]), (g_wout, *upd["out"]) = _adamw_halves(
        [w_mlp1[0], w_mlp2[0], w_out[0]], big_halves, big_theirs, [m_w_mlp1[0], m_w_mlp2[0], m_w_out[0]],
        [v_w_mlp1[0], v_w_mlp2[0], v_w_out[0]], c_idx, "adamw_big")

    def four(o4, shape):
        return [o4[r].reshape(shape) for r in range(4)]

    cc4 = four(o_cc, (D_MODEL,))
    b4 = [o_b[r].reshape(1, 6 * D_MODEL) for r in range(4)]
    q4 = four(o_q, (1, Q_RANK))
    k4 = four(o_k, (1, KV_RANK))
    gf4 = four(o_gf, (D_MODEL,))
    big = {"in": g_win, "uq": g_wuq, "ukv": g_wukv, "out": g_wout, "mlp1": g_w1, "mlp2": g_w2}

    def leaf(idx):
        wm = (g_wmod, d_wmod, nm_wmod, nv_wmod)[idx]
        cwv = (g_cw, d_cw, nm_cw, nv_cw)[idx]
        bigv = {n: (big[n] if idx == 0 else upd[n][idx - 1]) for n in big}
        return [cc4[idx], wm[None], b4[idx], bigv["in"][None], q4[idx], bigv["uq"][None], k4[idx], bigv["ukv"][None],
                cwv[None], bigv["out"][None], bigv["mlp1"][None], bigv["mlp2"][None], gf4[idx]]

    return (loss, gx[None], *leaf(0), *leaf(1), *leaf(2), *leaf(3))
```

```python
import functools
import math

import jax
import jax.numpy as jnp
from jax import lax
from jax.experimental import pallas as pl
from jax.experimental.pallas import tpu as pltpu

F32 = jnp.float32
BF16 = jnp.bfloat16
MESH = pl.DeviceIdType.MESH
HIGHEST = lax.Precision.HIGHEST

D_MODEL = 1024
N_HEADS = 8
QK_NOPE = 64
QK_ROPE = 32
QK_DIM = QK_NOPE + QK_ROPE
V_DIM = 64
Q_RANK = 256
KV_RANK = 128
CONV_W = 512
D_FF = 4096
GRID_W = 64
ROPE_THETA = 10000.0
EPS = 1e-6
ATTN_SCALE = 1.0 / math.sqrt(QK_DIM)
HEAD_PAD = 128
Z_COLS = 2048
ROPE_LANE0 = QK_NOPE
N_SHARD = 4
TOK_TILE = 256
FF_CHUNK = 1024
MLP_FF_CHUNK = 1024
KEY_CHUNK = 1024
ATTN_FWD_Q_BLOCK = 1024
ATTN_HEADS_PER_STEP = 4
ATTN_BWD_HEADS_PER_STEP = 2
ATTN_BWD_Q_BLOCK = 512
KEY_CHUNK_BWD = 512

ADAM_LR = 0.001
ADAM_B1 = 0.9
ADAM_B2 = 0.999
ADAM_EPS = 1e-08
ADAM_WD = 0.01
ADAM_STEP = 10

LOG2E = 1.4426950408889634

VMEM_LIMIT = 56 * 1024 * 1024
STAGE_VMEM_LIMIT = 32 * 1024 * 1024


def _pos():
    return lax.axis_index("x"), lax.axis_index("y"), lax.axis_index("c")


def _dot(a, b):
    return jnp.dot(a, b, preferred_element_type=F32)


def _dot_nt(a, b):
    return lax.dot_general(a, b, (((1,), (1,)), ((), ())), preferred_element_type=F32)


def _dot_tn(a, b):
    return lax.dot_general(a, b, (((0,), (0,)), ((), ())), preferred_element_type=F32)


def _rope(v, cos, sa, sb):
    return v * cos + pltpu.roll(v, 8, 1) * sa + pltpu.roll(v, HEAD_PAD - 8, 1) * sb


def _unrope(g, cos, sa, sb):
    return g * cos + pltpu.roll(g * sa, HEAD_PAD - 8, 1) + pltpu.roll(g * sb, 8, 1)


def _sigmoid(v):
    return 1.0 / (1.0 + jnp.exp(-v))


def _adamw(w, g, m, v):
    m = ADAM_B1 * m + (1.0 - ADAM_B1) * g
    v = ADAM_B2 * v + (1.0 - ADAM_B2) * (g * g)
    m_hat = m / (1.0 - ADAM_B1 ** ADAM_STEP)
    v_hat = v / (1.0 - ADAM_B2 ** ADAM_STEP)
    delta = -ADAM_LR * (m_hat / (jnp.sqrt(v_hat) + ADAM_EPS) + ADAM_WD * w)
    return delta, m, v


def _shift_rows(u, prev_row, next_row):
    n = u.shape[0]
    rows = lax.broadcasted_iota(jnp.int32, u.shape, 0)
    um1 = jnp.where(rows == 0, prev_row, pltpu.roll(u, 1, 0))
    up1 = jnp.where(rows == n - 1, next_row, pltpu.roll(u, n - 1, 0))
    return um1, up1


def _const_spec(shape):
    nd = len(shape)
    return pl.BlockSpec(shape, lambda *_: (0,) * nd)


def _resident_spec(shape):
    nd = len(shape)
    return pl.BlockSpec(shape, lambda *_: (0,) * nd, pipeline_mode=pl.Buffered(1))


def _peer(r, x, y, c):
    px = 1 - x if r & 4 else x
    py = 1 - y if r & 2 else y
    pc = 1 - c if r & 1 else c
    return (px, py, pc)


def _prologue(c_row, cctx_row, w_mod_sh, b_sh, cw_sh, srcs, n_gather):
    ncol = w_mod_sh.shape[1]
    n = len(srcs)
    n_split = 4

    def body(c_ref, cctx_ref, w_ref, b_ref, cw_ref, *refs):
        ins, (c8_ref, m_ref), outs = refs[:n], refs[n:n + 2], refs[n + 2:2 * n + 2]
        mine_ref, msh_ref = refs[2 * n + 2:2 * n + 4]
        f32s, bfs = refs[2 * n + 4:3 * n + 4], refs[3 * n + 4:4 * n + 4]
        ssem, rsem, ssem2, rsem2, lsem_in, lsem_out = refs[4 * n + 4:4 * n + 10]
        x, y, c = _pos()
        me = 4 * x + 2 * y + c
        j = 2 * x + y

        def pieces(rows):
            step = rows // n_split
            return [pl.ds(q * step, step) for q in range(n_split)]

        for t in range(n):
            for sl in pieces(ins[t].shape[0]):
                pltpu.make_async_copy(ins[t].at[sl], f32s[t].at[sl], lsem_in.at[t]).start()
        mine_ref[...] = jnp.zeros(mine_ref.shape, F32)
        mine_ref[0:1, :] = c_ref[...]
        my_rows = pl.ds(pl.multiple_of(8 * me, 8), 8)
        sends = []
        for r in range(1, 8):
            cp = pltpu.make_async_remote_copy(
                src_ref=mine_ref, dst_ref=c8_ref.at[my_rows], send_sem=ssem.at[r - 1], recv_sem=rsem.at[r - 1],
                device_id=_peer(r, x, y, c), device_id_type=MESH)
            cp.start()
            sends.append(cp)

        def cast_and_store(t):
            pltpu.make_async_copy(ins[t], f32s[t], lsem_in.at[t]).wait()
            bfs[t][...] = f32s[t][...].astype(BF16)
            for sl in pieces(ins[t].shape[0]):
                pltpu.make_async_copy(bfs[t].at[sl], outs[t].at[j, sl], lsem_out.at[t]).start()

        gather = _ShardGather(outs[:n_gather], *refs[4 * n + 10:])
        for t in range(n_gather):
            cast_and_store(t)
        for t in range(n_gather):
            pltpu.make_async_copy(bfs[t], outs[t].at[j], lsem_out.at[t]).wait()
        gather.start()
        for cp in sends:
            cp.wait()
        c8_ref[my_rows, :] = mine_ref[...]
        c8_ref[64:72, :] = jnp.zeros((8, D_MODEL), F32)
        c8_ref[64:65, :] = cctx_ref[...]
        cv = c8_ref[...]
        s = cv * _sigmoid(cv)
        m = jnp.dot(s, w_ref[...], precision=HIGHEST, preferred_element_type=F32) + b_ref[...]
        msh_ref[0:64, :] = m[0:64, :]
        msh_ref[64:72, :] = jnp.zeros((8, ncol), F32)
        msh_ref[64:65, :] = m[64:65, :]
        msh_ref[65:68, 0:128] = cw_ref[0:3, :]
        m_ref[j, 0:8, :] = msh_ref[my_rows, :]
        m_ref[j, 8:16, :] = msh_ref[64:72, :]
        sends2 = []
        for k, (px, py) in enumerate(_chips(x, y)):
            theirs = pl.ds(pl.multiple_of(8 * (4 * px + 2 * py + c), 8), 8)
            for half, src in enumerate((msh_ref.at[theirs], msh_ref.at[64:72])):
                cp = pltpu.make_async_remote_copy(
                    src_ref=src, dst_ref=m_ref.at[j, 8 * half:8 * half + 8], send_sem=ssem2.at[2 * k + half],
                    recv_sem=rsem2.at[2 * k + half], device_id=(px, py, c), device_id_type=MESH)
                cp.start()
                sends2.append(cp)
        for t in range(n_gather, n):
            cast_and_store(t)
        gather.forward()
        gather.finish()
        for t in range(n_gather, n):
            pltpu.make_async_copy(bfs[t], outs[t].at[j], lsem_out.at[t]).wait()
        for cp in sends2:
            cp.wait()

    vm = pl.BlockSpec(memory_space=pltpu.VMEM)
    hbm = pl.BlockSpec(memory_space=pl.ANY)
    return pl.pallas_call(
        body, name="prologue",
        out_shape=(jax.ShapeDtypeStruct((72, D_MODEL), F32), jax.ShapeDtypeStruct((N_SHARD, 16, ncol), F32))
        + tuple(jax.ShapeDtypeStruct((N_SHARD,) + a.shape, BF16) for a in srcs),
        in_specs=[vm] * 5 + [hbm] * n, out_specs=(vm, vm) + (hbm,) * n,
        scratch_shapes=[pltpu.VMEM((8, D_MODEL), F32), pltpu.VMEM((72, ncol), F32)]
        + [pltpu.VMEM(a.shape, F32) for a in srcs] + [pltpu.VMEM(a.shape, BF16) for a in srcs]
        + [pltpu.SemaphoreType.DMA((7,)), pltpu.SemaphoreType.DMA((7,)),
           pltpu.SemaphoreType.DMA((6,)), pltpu.SemaphoreType.DMA((6,)),
           pltpu.SemaphoreType.DMA((n,)), pltpu.SemaphoreType.DMA((n,))] + _gather_sems(n_gather),
        compiler_params=pltpu.CompilerParams(vmem_limit_bytes=VMEM_LIMIT),
    )(c_row, cctx_row, w_mod_sh, b_sh, cw_sh, *srcs)


def _chips(x, y):
    return [(1 - x, y), (x, 1 - y), (1 - x, 1 - y)]


def _halves(ref, c, align):
    hr = ref.shape[-2] // 2
    return (pl.ds(pl.multiple_of(c * hr, align), hr), pl.ds(pl.multiple_of((1 - c) * hr, align), hr))


class _ShardGather:
    def __init__(self, refs, ssem, rsem, fsend, frecv):
        self.refs, self.sems = refs, (ssem, rsem, fsend, frecv)
        self.x, self.y, self.c = _pos()
        self.j = 2 * self.x + self.y

    def _ici(self, a, k, slot):
        g = self.refs[a]
        ssem, rsem, _, _ = self.sems
        mine, _ = _halves(g, self.c, 16)
        px, py = _chips(self.x, self.y)[k]
        return pltpu.make_async_remote_copy(
            src_ref=g.at[self.j, mine], dst_ref=g.at[slot, mine], send_sem=ssem.at[3 * a + k],
            recv_sem=rsem.at[3 * a + k], device_id=(px, py, self.c), device_id_type=MESH)

    def _d2d(self, a, k, to_other_half):
        g = self.refs[a]
        _, _, fsend, frecv = self.sems
        mine, theirs = _halves(g, self.c, 16)
        px, py = _chips(self.x, self.y)[k]
        jk = 2 * px + py
        return pltpu.make_async_remote_copy(
            src_ref=g.at[jk, mine], dst_ref=g.at[jk, theirs if to_other_half else mine],
            send_sem=fsend.at[3 * a + k], recv_sem=frecv.at[3 * a + k],
            device_id=(self.x, self.y, 1 - self.c), device_id_type=MESH)

    def start(self):
        for a in range(len(self.refs)):
            for k in range(3):
                self._ici(a, k, self.j).start()

    def forward(self):
        for a in range(len(self.refs)):
            for k, (px, py) in enumerate(_chips(self.x, self.y)):
                self._ici(a, k, 2 * px + py).wait_recv()
                self._d2d(a, k, False).start()

    def finish(self):
        for a in range(len(self.refs)):
            for k in range(3):
                self._d2d(a, k, True).wait()
                self._ici(a, k, self.j).wait_send()


def _gather_sems(n_arrays):
    return [pltpu.SemaphoreType.DMA((3 * n_arrays,)) for _ in range(4)]


class _SiblingSend:
    def __init__(self, g_refs, got_refs, ssem, rsem):
        self.g_refs, self.got_refs, self.ssem, self.rsem = g_refs, got_refs, ssem, rsem
        self.x, self.y, self.c = _pos()

    def _copy(self, a, shard):
        _, theirs = _halves(self.g_refs[a], self.c, 16)
        src = self.g_refs[a].at[:, theirs] if shard is None else self.g_refs[a].at[shard, theirs]
        dst = self.got_refs[a] if shard is None else self.got_refs[a].at[shard]
        return pltpu.make_async_remote_copy(
            src_ref=src, dst_ref=dst, send_sem=self.ssem.at[a], recv_sem=self.rsem.at[a],
            device_id=(self.x, self.y, 1 - self.c), device_id_type=MESH)

    def start(self):
        for a in range(len(self.g_refs)):
            for s in range(N_SHARD):
                self._copy(a, s).start()

    def finish(self):
        for a in range(len(self.g_refs)):
            self._copy(a, None).wait()


class _SiblingSwap:
    def __init__(self, h_refs, t_refs, ssem, rsem):
        self.h_refs, self.t_refs, self.ssem, self.rsem = h_refs, t_refs, ssem, rsem
        self.x, self.y, self.c = _pos()

    def _copy(self, a):
        return pltpu.make_async_remote_copy(
            src_ref=self.h_refs[a], dst_ref=self.t_refs[a], send_sem=self.ssem.at[a], recv_sem=self.rsem.at[a],
            device_id=(self.x, self.y, 1 - self.c), device_id_type=MESH)

    def start(self):
        for a in range(len(self.h_refs)):
            self._copy(a).start()

    def finish(self):
        for a in range(len(self.h_refs)):
            self._copy(a).wait()


def _rs_sibling(arrs, name):
    n = len(arrs)

    def body(*refs):
        send = _SiblingSend(refs[:n], refs[n:2 * n], refs[2 * n], refs[2 * n + 1])
        send.start()
        send.finish()

    hbm = pl.BlockSpec(memory_space=pl.ANY)
    return pl.pallas_call(
        body, name=name,
        out_shape=tuple(jax.ShapeDtypeStruct((N_SHARD, a.shape[1] // 2, a.shape[2]), BF16) for a in arrs),
        in_specs=[hbm] * n, out_specs=(hbm,) * n,
        scratch_shapes=[pltpu.SemaphoreType.DMA((n,)), pltpu.SemaphoreType.DMA((n,))],
    )(*arrs)


class _ChipScatter:
    def __init__(self, parts, gots, ssem, rsem):
        self.parts, self.gots, self.ssem, self.rsem = parts, gots, ssem, rsem
        self.x, self.y, self.c = _pos()

    def _copy(self, a, k):
        px, py = _chips(self.x, self.y)[k]
        return pltpu.make_async_remote_copy(
            src_ref=self.parts[a].at[2 * px + py], dst_ref=self.gots[a].at[k], send_sem=self.ssem.at[3 * a + k],
            recv_sem=self.rsem.at[3 * a + k], device_id=(px, py, self.c), device_id_type=MESH)

    def start(self):
        for a in range(len(self.parts)):
            for k in range(3):
                self._copy(a, k).start()

    def finish(self):
        for a in range(len(self.parts)):
            for k in range(3):
                self._copy(a, k).wait()


def _rs_join(halves, name):
    n = len(halves)

    def body(*refs):
        h_refs, f_refs, stages = refs[:n], refs[n:2 * n], refs[2 * n:3 * n]
        lsem_in, lsem_out, ssem, rsem = refs[3 * n:]
        x, y, c = _pos()
        remote = []
        for a in range(n):
            mine, _ = _halves(f_refs[a], c, 8)
            cp = pltpu.make_async_remote_copy(
                src_ref=h_refs[a], dst_ref=f_refs[a].at[mine], send_sem=ssem.at[a], recv_sem=rsem.at[a],
                device_id=(x, y, 1 - c), device_id_type=MESH)
            cp.start()
            remote.append(cp)
            pltpu.make_async_copy(h_refs[a], stages[a], lsem_in.at[a]).start()
        local = []
        for a in range(n):
            mine, _ = _halves(f_refs[a], c, 8)
            pltpu.make_async_copy(h_refs[a], stages[a], lsem_in.at[a]).wait()
            cp = pltpu.make_async_copy(stages[a], f_refs[a].at[mine], lsem_out.at[a])
            cp.start()
            local.append(cp)
        for cp in remote + local:
            cp.wait()

    hbm = pl.BlockSpec(memory_space=pl.ANY)
    return pl.pallas_call(
        body, name=name,
        out_shape=tuple(jax.ShapeDtypeStruct((2 * h.shape[0], h.shape[1]), F32) for h in halves),
        in_specs=[hbm] * n, out_specs=(hbm,) * n,
        scratch_shapes=[pltpu.VMEM(h.shape, F32) for h in halves]
        + [pltpu.SemaphoreType.DMA((n,)) for _ in range(4)],
        compiler_params=pltpu.CompilerParams(vmem_limit_bytes=STAGE_VMEM_LIMIT),
    )(*halves)


def _add_pairs(arrs, gots, c_idx, name):
    n = len(arrs)

    def body(c_ref, *refs):
        for a in range(n):
            refs[2 * n + a][...] = (refs[a][...].astype(F32) + refs[n + a][...].astype(F32)).astype(BF16)

    def half_spec(g, mine):
        hr, cols = g.shape[1], g.shape[2]
        if mine:
            return pl.BlockSpec((1, hr, cols), lambda s, c_ref: (s, c_ref[0], 0))
        return pl.BlockSpec((1, hr, cols), lambda s, c_ref: (s, 0, 0))

    grid_spec = pltpu.PrefetchScalarGridSpec(
        num_scalar_prefetch=1, grid=(N_SHARD,),
        in_specs=[half_spec(g, True) for g in gots] + [half_spec(g, False) for g in gots],
        out_specs=tuple(half_spec(g, False) for g in gots))
    return pl.pallas_call(
        body, name=name, grid_spec=grid_spec, out_shape=tuple(jax.ShapeDtypeStruct(g.shape, BF16) for g in gots),
        compiler_params=pltpu.CompilerParams(vmem_limit_bytes=STAGE_VMEM_LIMIT),
    )(c_idx, *arrs, *gots)


def _add_chips(parts, gots, j_idx, name):
    n = len(parts)
    n_split = 2

    def body(j_ref, *refs):
        for a in range(n):
            acc = refs[a][0].astype(F32)
            for k in range(3):
                acc = acc + refs[n + a][k].astype(F32)
            refs[2 * n + a][...] = acc

    in_specs, out_specs = [], []
    for g in gots:
        rb, cols = g.shape[1] // n_split, g.shape[2]
        in_specs.append(pl.BlockSpec((1, rb, cols), lambda r, j_ref: (j_ref[0], r, 0)))
        out_specs.append(pl.BlockSpec((rb, cols), lambda r, j_ref: (r, 0)))
    for g in gots:
        rb, cols = g.shape[1] // n_split, g.shape[2]
        in_specs.append(pl.BlockSpec((3, rb, cols), lambda r, j_ref: (0, r, 0)))
    grid_spec = pltpu.PrefetchScalarGridSpec(
        num_scalar_prefetch=1, grid=(n_split,), in_specs=in_specs, out_specs=tuple(out_specs))
    return pl.pallas_call(
        body, name=name, grid_spec=grid_spec,
        out_shape=tuple(jax.ShapeDtypeStruct(g.shape[1:], F32) for g in gots),
        compiler_params=pltpu.CompilerParams(vmem_limit_bytes=STAGE_VMEM_LIMIT),
    )(j_idx, *parts, *gots)


def _small_exchange(sv, w_mod_sh, cctx, m_cctx, v_cctx, bmod, m_bmod, v_bmod, qg, m_qg, v_qg, kvg, m_kvg, v_kvg,
                    gf, m_gf, v_gf, swap_halves, scatter_parts):
    ncol = w_mod_sh.shape[1]
    n_s, n_p = len(swap_halves), len(scatter_parts)
    n_x = n_s + n_p

    def body(sv_ref, w_ref, cctx_ref, mcc_ref, vcc_ref, b_ref, mb_ref, vb_ref, qg_ref, mq_ref, vq_ref,
             kg_ref, mk_ref, vk_ref, gf_ref, mgf_ref, vgf_ref, *rest):
        all_ref, red_ref, occ_ref, ob_ref, oq_ref, ok_ref, ogf_ref = rest[n_x:n_x + 7]
        vec_ref, part_ref, ssem, rsem, ssem2, rsem2, wsend, wrecv, psend, precv = rest[2 * n_x + 7:]
        swap = _SiblingSwap(rest[:n_s], rest[n_x + 7:n_x + 7 + n_s], wsend, wrecv)
        scatter = _ChipScatter(rest[n_s:n_x], rest[n_x + 7 + n_s:2 * n_x + 7], psend, precv)
        swap.start()
        scatter.start()
        x, y, c = _pos()
        me = 4 * x + 2 * y + c
        j = 2 * x + y
        sends = []
        for r in range(1, 8):
            cp = pltpu.make_async_remote_copy(
                src_ref=sv_ref, dst_ref=all_ref.at[me], send_sem=ssem.at[r - 1], recv_sem=rsem.at[r - 1],
                device_id=_peer(r, x, y, c), device_id_type=MESH)
            cp.start()
            sends.append(cp)
        for cp in sends:
            cp.wait()
        all_ref[me] = sv_ref[...]
        red = all_ref[0]
        for d in range(1, 8):
            red = red + all_ref[d]
        red_ref[...] = red
        vec_ref[...] = jnp.zeros(vec_ref.shape, F32)

        @pl.when(j == 0)
        def _():
            vec_ref[0:1, 0:1024] = red[6:7, :]
            vec_ref[0:1, 1024:1536] = red[7:8, 0:512]

        @pl.when(j == 1)
        def _():
            vec_ref[0:1, 0:512] = red[7:8, 512:1024]

        part = lax.dot_general(vec_ref[...], w_ref[...], (((1,), (1,)), ((), ())), precision=HIGHEST,
                               preferred_element_type=F32)
        part_ref[j] = part
        sends2 = []
        for k, r in enumerate((4, 2, 6)):
            cp = pltpu.make_async_remote_copy(
                src_ref=part_ref.at[j], dst_ref=part_ref.at[j], send_sem=ssem2.at[k], recv_sem=rsem2.at[k],
                device_id=_peer(r, x, y, c), device_id_type=MESH)
            cp.start()
            sends2.append(cp)
        for cp in sends2:
            cp.wait()
        tot = part_ref[0]
        for s in range(1, N_SHARD):
            tot = tot + part_ref[s]
        cc = cctx_ref[...]
        sg = _sigmoid(cc)
        g_cc = tot[0:1, :] * (sg * (1.0 + cc * (1.0 - sg)))
        d_, m_, v_ = _adamw(cc, g_cc, mcc_ref[...], vcc_ref[...])
        occ_ref[0:1, :] = g_cc
        occ_ref[1:2, :] = d_
        occ_ref[2:3, :] = m_
        occ_ref[3:4, :] = v_
        occ_ref[4:8, :] = jnp.zeros((4, D_MODEL), F32)
        g_b = red[0:6, :]
        pad = jnp.concatenate([red[6:8, :], jnp.zeros((4, D_MODEL), F32)], axis=0)
        g_b = g_b + pad
        d_, m_, v_ = _adamw(b_ref[...], g_b, mb_ref[...], vb_ref[...])
        ob_ref[0] = g_b
        ob_ref[1] = d_
        ob_ref[2] = m_
        ob_ref[3] = v_
        g_q = red[9:10, 0:Q_RANK]
        d_, m_, v_ = _adamw(qg_ref[...], g_q, mq_ref[...], vq_ref[...])
        oq_ref[0:1, :] = g_q
        oq_ref[1:2, :] = d_
        oq_ref[2:3, :] = m_
        oq_ref[3:4, :] = v_
        oq_ref[4:8, :] = jnp.zeros((4, Q_RANK), F32)
        g_k = red[10:11, 0:KV_RANK]
        d_, m_, v_ = _adamw(kg_ref[...], g_k, mk_ref[...], vk_ref[...])
        ok_ref[0:1, :] = g_k
        ok_ref[1:2, :] = d_
        ok_ref[2:3, :] = m_
        ok_ref[3:4, :] = v_
        ok_ref[4:8, :] = jnp.zeros((4, KV_RANK), F32)
        g_f = red[8:9, :]
        d_, m_, v_ = _adamw(gf_ref[...], g_f, mgf_ref[...], vgf_ref[...])
        ogf_ref[0:1, :] = g_f
        ogf_ref[1:2, :] = d_
        ogf_ref[2:3, :] = m_
        ogf_ref[3:4, :] = v_
        ogf_ref[4:8, :] = jnp.zeros((4, D_MODEL), F32)
        swap.finish()
        scatter.finish()

    vm = pl.BlockSpec(memory_space=pltpu.VMEM)
    hbm = pl.BlockSpec(memory_space=pl.ANY)
    out_shape = (
        jax.ShapeDtypeStruct((8, 16, D_MODEL), F32),
        jax.ShapeDtypeStruct((16, D_MODEL), F32),
        jax.ShapeDtypeStruct((8, D_MODEL), F32),
        jax.ShapeDtypeStruct((4, 6, D_MODEL), F32),
        jax.ShapeDtypeStruct((8, Q_RANK), F32),
        jax.ShapeDtypeStruct((8, KV_RANK), F32),
        jax.ShapeDtypeStruct((8, D_MODEL), F32),
    ) + tuple(jax.ShapeDtypeStruct(h.shape, F32) for h in swap_halves) + tuple(
        jax.ShapeDtypeStruct((3,) + p.shape[1:], BF16) for p in scatter_parts)
    return pl.pallas_call(
        body, name="small_exchange", out_shape=out_shape, in_specs=[vm] * 17 + [hbm] * n_x,
        out_specs=tuple([vm] * 7) + (hbm,) * n_x,
        scratch_shapes=[pltpu.VMEM((8, ncol), F32), pltpu.VMEM((N_SHARD, 8, D_MODEL), F32),
                        pltpu.SemaphoreType.DMA((7,)), pltpu.SemaphoreType.DMA((7,)),
                        pltpu.SemaphoreType.DMA((3,)), pltpu.SemaphoreType.DMA((3,)),
                        pltpu.SemaphoreType.DMA((n_s,)), pltpu.SemaphoreType.DMA((n_s,)),
                        pltpu.SemaphoreType.DMA((3 * n_p,)), pltpu.SemaphoreType.DMA((3 * n_p,))],
        compiler_params=pltpu.CompilerParams(vmem_limit_bytes=VMEM_LIMIT),
    )(sv, w_mod_sh, cctx, m_cctx, v_cctx, bmod, m_bmod, v_bmod, qg, m_qg, v_qg, kvg, m_kvg, v_kvg, gf, m_gf, v_gf,
      *swap_halves, *scatter_parts)


def _inproj_fwd(x2, ctx2, mod_a, w_in, qg, kvg, w_uq, w_ukv, cos_t, sin_a, sin_b):
    t_lat, t_ctx = x2.shape[0], ctx2.shape[0]
    tm = TOK_TILE
    n_lat = t_lat // tm
    n_all = n_lat + t_ctx // tm
    e_rows = t_lat + t_ctx

    def body(x_ref, ctx_ref, mod_ref, win_ref, qg_ref, kvg_ref, wuq_ref, wukv_ref, cos_ref, sa_ref, sb_ref,
             z_ref, q_ref, k_ref, v_ref, kt_ref):
        i = pl.program_id(0)
        xin = jnp.where(i < n_lat, x_ref[...], ctx_ref[...])
        xn = xin * lax.rsqrt(jnp.mean(xin * xin, axis=-1, keepdims=True) + EPS)
        h1 = (xn * (1.0 + mod_ref[0, 1:2, :]) + mod_ref[0, 0:1, :]).astype(BF16)
        z = _dot(h1, win_ref[...])
        z_ref[...] = z
        cos, sa, sb = cos_ref[...], sa_ref[...], sb_ref[...]
        cq = z[:, 0:Q_RANK]
        cqn = (cq * lax.rsqrt(jnp.mean(cq * cq, axis=-1, keepdims=True) + EPS) * qg_ref[...]).astype(BF16)
        q = _dot(cqn, wuq_ref[...])
        ckv = z[:, Q_RANK:Q_RANK + KV_RANK]
        ckvn = (ckv * lax.rsqrt(jnp.mean(ckv * ckv, axis=-1, keepdims=True) + EPS) * kvg_ref[...]).astype(BF16)
        kv = _dot(ckvn, wukv_ref[...])
        kr = _rope(z[:, Q_RANK + KV_RANK:Q_RANK + KV_RANK + HEAD_PAD], cos, sa, sb)
        ones_lane = lax.broadcasted_iota(jnp.int32, (tm, HEAD_PAD), 1) == V_DIM
        for h in range(N_HEADS):
            lo = h * HEAD_PAD
            q_ref[h] = _rope(q[:, lo:lo + HEAD_PAD], cos, sa, sb).astype(BF16)
            kh = kv[:, lo:lo + HEAD_PAD] + kr
            k_ref[h] = kh.astype(BF16)
            kt_ref[h] = kh.T.astype(BF16)
            vh = kv[:, N_HEADS * HEAD_PAD + lo:N_HEADS * HEAD_PAD + lo + HEAD_PAD]
            v_ref[h] = jnp.where(ones_lane, 1.0, vh).astype(BF16)

    row = lambda i: (i, 0)
    head_spec = pl.BlockSpec((N_HEADS, tm, HEAD_PAD), lambda i: (0, i, 0))
    head_shape = jax.ShapeDtypeStruct((N_HEADS, e_rows, HEAD_PAD), BF16)
    return pl.pallas_call(
        body, name="inproj_fwd", grid=(n_all,),
        out_shape=(jax.ShapeDtypeStruct((e_rows, Z_COLS), F32), head_shape, head_shape, head_shape,
                   jax.ShapeDtypeStruct((N_HEADS, HEAD_PAD, e_rows), BF16)),
        in_specs=[
            pl.BlockSpec((tm, D_MODEL), lambda i: (jnp.minimum(i, n_lat - 1), 0)),
            _const_spec((tm, D_MODEL)),
            pl.BlockSpec((1, 8, D_MODEL), lambda i: (i // n_lat, 0, 0)),
            _const_spec(w_in.shape), _const_spec(qg.shape), _const_spec(kvg.shape),
            _const_spec(w_uq.shape), _const_spec(w_ukv.shape),
            pl.BlockSpec((tm, HEAD_PAD), row), pl.BlockSpec((tm, HEAD_PAD), row), pl.BlockSpec((tm, HEAD_PAD), row),
        ],
        out_specs=(pl.BlockSpec((tm, Z_COLS), row), head_spec, head_spec, head_spec,
                   pl.BlockSpec((N_HEADS, HEAD_PAD, tm), lambda i: (0, 0, i))),
        compiler_params=pltpu.CompilerParams(vmem_limit_bytes=VMEM_LIMIT),
    )(x2, ctx2, mod_a, w_in, qg, kvg, w_uq, w_ukv, cos_t, sin_a, sin_b)


def _key_chunks(e_rows, size):
    n_chunks = max(1, e_rows // size)
    return [(ci * size, size if ci < n_chunks - 1 else e_rows - ci * size) for ci in range(n_chunks)]


def _attn_fwd(q, k, v, t_lat, shard_arrays):
    e_rows = k.shape[1]
    tq = min(t_lat, ATTN_FWD_Q_BLOCK)
    bounds = _key_chunks(e_rows, KEY_CHUNK)
    c2 = ATTN_SCALE * LOG2E

    hb = ATTN_HEADS_PER_STEP
    n_hb = N_HEADS // hb

    def body(q_ref, k_ref, v_ref, o_ref, lse_ref):
        qs = [q_ref[b] for b in range(hb)]
        m, acc = [None] * hb, [None] * hb
        for lo, n in bounds:
            for b in range(hb):
                s = _dot_nt(qs[b], k_ref[b, lo:lo + n, :])
                mc = jnp.max(s, axis=-1, keepdims=True)
                m_new = mc if m[b] is None else jnp.maximum(m[b], mc)
                p = jnp.exp2((s - m_new) * c2)
                pv = _dot(p.astype(BF16), v_ref[b, lo:lo + n, :])
                acc[b] = pv if m[b] is None else acc[b] * jnp.exp2((m[b] - m_new) * c2) + pv
                m[b] = m_new
        outs = []
        for b in range(hb):
            l = acc[b][:, V_DIM:V_DIM + 1]
            outs.append(acc[b] * (1.0 / l))
            lse = (m[b] * ATTN_SCALE + jnp.log(l)) * LOG2E
            lse_ref[b] = jnp.broadcast_to(lse, (tq, HEAD_PAD)).T[0:1, :]
        low = lax.broadcasted_iota(jnp.int32, (tq, HEAD_PAD), 1) < V_DIM
        pairs = [jnp.where(low, outs[b], pltpu.roll(outs[b + 1], V_DIM, 1)) for b in range(0, hb, 2)]
        o_ref[...] = jnp.concatenate(pairs, axis=1).astype(BF16)

    n_w = len(shard_arrays)
    n_q = t_lat // tq

    def body_with_gather(q_ref, k_ref, v_ref, *rest):
        o_ref, lse_ref = rest[n_w], rest[n_w + 1]
        gather = _ShardGather(rest[n_w + 2:2 * n_w + 2], *rest[2 * n_w + 2:])
        step = pl.program_id(0) * n_q + pl.program_id(1)
        pl.when(step == 0)(gather.start)
        pl.when(step == n_hb * n_q // 2)(gather.forward)
        body(q_ref, k_ref, v_ref, o_ref, lse_ref)
        pl.when(step == n_hb * n_q - 1)(gather.finish)

    hbm = pl.BlockSpec(memory_space=pl.ANY)
    return pl.pallas_call(
        body_with_gather, name="attn_fwd", grid=(n_hb, n_q),
        out_shape=(jax.ShapeDtypeStruct((t_lat, N_HEADS * V_DIM), BF16),
                   jax.ShapeDtypeStruct((N_HEADS, 1, t_lat), F32))
        + tuple(jax.ShapeDtypeStruct(a.shape, a.dtype) for a in shard_arrays),
        in_specs=[pl.BlockSpec((hb, tq, HEAD_PAD), lambda h, i: (h, i, 0)),
                  pl.BlockSpec((hb, e_rows, HEAD_PAD), lambda h, i: (h, 0, 0)),
                  pl.BlockSpec((hb, e_rows, HEAD_PAD), lambda h, i: (h, 0, 0))] + [hbm] * n_w,
        out_specs=(pl.BlockSpec((tq, hb * V_DIM), lambda h, i: (i, h)),
                   pl.BlockSpec((hb, 1, tq), lambda h, i: (h, 0, i))) + (hbm,) * n_w,
        input_output_aliases={3 + a: 2 + a for a in range(n_w)},
        scratch_shapes=_gather_sems(n_w),
        compiler_params=pltpu.CompilerParams(vmem_limit_bytes=VMEM_LIMIT),
    )(q, k, v, *shard_arrays)


def _attn_bwd(q, k, v, kt, o, do, lse_row, t_lat, parts):
    e_rows = k.shape[1]
    tq = min(t_lat, ATTN_BWD_Q_BLOCK)
    n_p = len(parts)
    n_q = t_lat // tq
    bounds = _key_chunks(e_rows, KEY_CHUNK_BWD)

    hb = ATTN_BWD_HEADS_PER_STEP
    assert hb * V_DIM == HEAD_PAD, "a step's heads share one lane tile of the unpadded o / dO"
    n_hb = N_HEADS // hb

    def body(q_ref, k_ref, v_ref, kt_ref, o_ref, do_ref, lse_ref, *rest):
        dqt_ref, dk_ref, dv_ref = rest[n_p:n_p + 3]
        scatter = _ChipScatter(rest[:n_p], rest[n_p + 3:2 * n_p + 3], rest[2 * n_p + 3], rest[2 * n_p + 4])
        h, i = pl.program_id(0), pl.program_id(1)
        pl.when(jnp.logical_and(h == 0, i == 0))(scatter.start)

        @pl.when(i == 0)
        def _():
            dk_ref[...] = jnp.zeros(dk_ref.shape, F32)
            dv_ref[...] = jnp.zeros(dv_ref.shape, F32)

        do_pair = do_ref[...].astype(F32)
        prod = o_ref[...].astype(F32) * do_pair
        lane = lax.broadcasted_iota(jnp.int32, (tq, HEAD_PAD), 1)
        sel = lax.broadcasted_iota(jnp.int32, (8, HEAD_PAD), 1)
        qs, dos, lses, deltas = [], [], [], []
        for b in range(hb):
            qs.append(q_ref[b])
            mine = do_pair if b == 0 else pltpu.roll(do_pair, V_DIM, 1)
            dos.append(jnp.where(lane < V_DIM, mine, 0.0).astype(BF16))
            lses.append(lse_ref[b])
            ones = jnp.where((sel < V_DIM) == (b == 0), 1.0, 0.0)
            deltas.append(lax.dot_general(ones, prod, (((1,), (1,)), ((), ())), precision=HIGHEST,
                                          preferred_element_type=F32)[0:1, :])
        dqt = [None] * hb
        for lo, n in bounds:
            for b in range(hb):
                pt = jnp.exp2(_dot_nt(k_ref[b, lo:lo + n, :], qs[b]) * (ATTN_SCALE * LOG2E) - lses[b])
                dpt = _dot_nt(v_ref[b, lo:lo + n, :], dos[b])
                dst = (pt * (dpt - deltas[b])).astype(BF16)
                dv_c = _dot(pt.astype(BF16), dos[b])
                dk_c = _dot(dst, qs[b])
                part = _dot(kt_ref[b, :, lo:lo + n], dst)
                dqt[b] = part if dqt[b] is None else dqt[b] + part
                dk_ref[b, lo:lo + n, :] += dk_c * ATTN_SCALE
                dv_ref[b, lo:lo + n, :] += dv_c
        for b in range(hb):
            dqt_ref[b] = (dqt[b] * ATTN_SCALE).T

        pl.when(jnp.logical_and(h == n_hb - 1, i == n_q - 1))(scatter.finish)

    hbm = pl.BlockSpec(memory_space=pl.ANY)
    qspec = pl.BlockSpec((hb, tq, HEAD_PAD), lambda h, i: (h, i, 0))
    kspec = pl.BlockSpec((hb, e_rows, HEAD_PAD), lambda h, i: (h, 0, 0))
    pair = pl.BlockSpec((tq, hb * V_DIM), lambda h, i: (i, h))
    return pl.pallas_call(
        body, name="attn_bwd", grid=(n_hb, n_q),
        out_shape=(jax.ShapeDtypeStruct((N_HEADS, t_lat, HEAD_PAD), F32),
                   jax.ShapeDtypeStruct((N_HEADS, e_rows, HEAD_PAD), F32),
                   jax.ShapeDtypeStruct((N_HEADS, e_rows, HEAD_PAD), F32))
        + tuple(jax.ShapeDtypeStruct((3,) + p.shape[1:], BF16) for p in parts),
        in_specs=[qspec, kspec, kspec, pl.BlockSpec((hb, HEAD_PAD, e_rows), lambda h, i: (h, 0, 0)), pair, pair,
                  pl.BlockSpec((hb, 1, tq), lambda h, i: (h, 0, i))] + [hbm] * n_p,
        out_specs=(qspec, kspec, kspec) + (hbm,) * n_p,
        scratch_shapes=[pltpu.SemaphoreType.DMA((3 * n_p,)), pltpu.SemaphoreType.DMA((3 * n_p,))],
        compiler_params=pltpu.CompilerParams(vmem_limit_bytes=VMEM_LIMIT),
    )(q, k, v, kt, o, do, lse_row, *parts)


def _halo_specs(tm, col_block):
    per = tm // 8
    prev = pl.BlockSpec((8, CONV_W), lambda i: (jnp.maximum(i * per - 1, 0), col_block))
    nxt = pl.BlockSpec((8, CONV_W), lambda i: ((i + 1) * per, col_block))
    return prev, nxt


def _mlp_fwdbwd(o, z, x2, tgt, mod_b, gf, cw, w_out, w1, w2):
    t_lat = x2.shape[0]
    tm = TOK_TILE
    n_lat = t_lat // tm
    fc = MLP_FF_CHUNK
    n_ff = D_FF // fc

    def body(o_ref, gb_ref, gc_ref, xi_ref, gcp_ref, xip_ref, gcn_ref, xin_ref, cw_ref, wout_ref,
             x_ref, t_ref, mod_ref, gf_ref, w1_ref, w2_ref,
             r_ref, da_ref, h2_ref, dy2_ref, dx1_ref, conv_ref, acc_ref, dy1_ref, do_ref, dgb_ref, dyv_ref, ra_ref):
        i = pl.program_id(0)

        @pl.when(i == 0)
        def _():
            acc_ref[...] = jnp.zeros(acc_ref.shape, F32)

        g1, sh2, sc2, g2 = mod_ref[0:1, :], mod_ref[1:2, :], mod_ref[2:3, :], mod_ref[3:4, :]
        u = gc_ref[...] * xi_ref[...]
        u_prev = jnp.where(i > 0, gcp_ref[7:8, :] * xip_ref[7:8, :], 0.0)
        u_next = jnp.where(i < n_lat - 1, gcn_ref[0:1, :] * xin_ref[0:1, :], 0.0)
        um1, up1 = _shift_rows(u, u_prev, u_next)
        yv = cw_ref[0:1, :] * um1 + cw_ref[1:2, :] * u + cw_ref[2:3, :] * up1
        gb = gb_ref[...]
        conv = (gb * yv).astype(BF16)
        conv_ref[...] = conv
        n_attn = N_HEADS * V_DIM
        y1 = _dot(o_ref[...], wout_ref[0:n_attn, :]) + _dot(conv, wout_ref[n_attn:, :])
        x1 = x_ref[...] + g1 * y1
        rstd2 = lax.rsqrt(jnp.mean(x1 * x1, axis=-1, keepdims=True) + EPS)
        xn1 = x1 * rstd2
        h2 = (xn1 * (1.0 + sc2) + sh2).astype(BF16)
        h2_ref[...] = h2
        y2 = jnp.zeros((tm, D_MODEL), F32)
        for jj in range(n_ff):
            lo = jj * fc
            ra = jnp.maximum(_dot(h2, w1_ref[lo // FF_CHUNK, :, lo % FF_CHUNK:lo % FF_CHUNK + fc]), 0.0)
            ra_ref[jj] = ra
            r = (ra * ra).astype(BF16)
            r_ref[:, lo:lo + fc] = r
            y2 = y2 + _dot(r, w2_ref[lo:lo + fc, :])
        x2v = x1 + g2 * y2
        rstd3 = lax.rsqrt(jnp.mean(x2v * x2v, axis=-1, keepdims=True) + EPS)
        xn3 = x2v * rstd3
        gfv = gf_ref[...]
        diff = xn3 * gfv - t_ref[...]
        loss_t = 0.5 * jnp.sum(jnp.sum(diff * diff, axis=-1, keepdims=True), axis=0, keepdims=True) * (1.0 / D_MODEL)
        dy = diff * (1.0 / D_MODEL)
        dxn3 = dy * gfv
        dx2 = rstd3 * (dxn3 - xn3 * jnp.mean(dxn3 * xn3, axis=-1, keepdims=True))
        dy2 = (dx2 * g2).astype(BF16)
        dy2_ref[...] = dy2
        dh2 = jnp.zeros((tm, D_MODEL), F32)
        for jj in range(n_ff):
            lo = jj * fc
            dr = _dot_nt(dy2, w2_ref[lo:lo + fc, :])
            da = (2.0 * ra_ref[jj] * dr).astype(BF16)
            da_ref[:, lo:lo + fc] = da
            dh2 = dh2 + _dot_nt(da, w1_ref[lo // FF_CHUNK, :, lo % FF_CHUNK:lo % FF_CHUNK + fc])
        dxn1 = dh2 * (1.0 + sc2)
        dx1 = dx2 + rstd2 * (dxn1 - xn1 * jnp.mean(dxn1 * xn1, axis=-1, keepdims=True))
        dx1_ref[...] = dx1
        dy1 = (dx1 * g1).astype(BF16)
        dy1_ref[...] = dy1
        do_ref[...] = _dot_nt(dy1, wout_ref[0:n_attn, :]).astype(BF16)
        dconv = _dot_nt(dy1, wout_ref[n_attn:, :])
        dgb_ref[...] = dconv * yv
        dyv_ref[...] = dconv * gb
        acc_ref[5:6, :] += jnp.sum(dx1 * y1, axis=0, keepdims=True)
        acc_ref[0:1, :] += jnp.sum(dy * xn3, axis=0, keepdims=True)
        acc_ref[1:2, :] += jnp.sum(dx2 * y2, axis=0, keepdims=True)
        acc_ref[2:3, :] += jnp.sum(dh2, axis=0, keepdims=True)
        acc_ref[3:4, :] += jnp.sum(dh2 * xn1, axis=0, keepdims=True)
        acc_ref[4:5, :] += jnp.broadcast_to(loss_t, (1, D_MODEL))

    row = lambda i: (i, 0)
    gcp, gcn = _halo_specs(tm, 2)
    xip, xin = _halo_specs(tm, 3)
    tile = pl.BlockSpec((tm, D_MODEL), row)
    wide = pl.BlockSpec((tm, D_FF), row)
    half = pl.BlockSpec((tm, CONV_W), row)
    return pl.pallas_call(
        body, name="mlp_fwdbwd", grid=(n_lat,),
        out_shape=(jax.ShapeDtypeStruct((t_lat, D_FF), BF16), jax.ShapeDtypeStruct((t_lat, D_FF), BF16),
                   jax.ShapeDtypeStruct((t_lat, D_MODEL), BF16), jax.ShapeDtypeStruct((t_lat, D_MODEL), BF16),
                   jax.ShapeDtypeStruct((t_lat, D_MODEL), F32), jax.ShapeDtypeStruct((t_lat, CONV_W), BF16),
                   jax.ShapeDtypeStruct((8, D_MODEL), F32),
                   jax.ShapeDtypeStruct((t_lat, D_MODEL), BF16),
                   jax.ShapeDtypeStruct((t_lat, N_HEADS * V_DIM), BF16),
                   jax.ShapeDtypeStruct((t_lat, CONV_W), F32), jax.ShapeDtypeStruct((t_lat, CONV_W), F32)),
        in_specs=[
            pl.BlockSpec((tm, N_HEADS * V_DIM), row),
            pl.BlockSpec((tm, CONV_W), lambda i: (i, 1)), pl.BlockSpec((tm, CONV_W), lambda i: (i, 2)),
            pl.BlockSpec((tm, CONV_W), lambda i: (i, 3)),
            gcp, xip, gcn, xin,
            _const_spec(cw.shape), _resident_spec(w_out.shape),
            tile, tile, _const_spec(mod_b.shape), _const_spec(gf.shape),
            _resident_spec(w1.shape), _resident_spec(w2.shape),
        ],
        out_specs=(wide, wide, tile, tile, tile, half, _const_spec((8, D_MODEL)),
                   tile, pl.BlockSpec((tm, N_HEADS * V_DIM), row), half, half),
        scratch_shapes=[pltpu.VMEM((n_ff, tm, fc), F32)],
        compiler_params=pltpu.CompilerParams(vmem_limit_bytes=VMEM_LIMIT),
    )(o, z, z, z, z, z, z, z, cw, w_out, x2, tgt, mod_b, gf, w1, w2)


def _inproj_bwd(x2, ctx2, mod_a, z, dyv, dgb, dx1, dqt, dk, dv, cos_t, sin_a, sin_b, w_in, w_uq, w_ukv, qg, kvg, cw):
    t_lat, t_ctx = x2.shape[0], ctx2.shape[0]
    tm = TOK_TILE
    n_lat = t_lat // tm
    n_all = n_lat + t_ctx // tm
    group = max(g for g in (1, 2, 4) if n_lat % g == 0)

    def body(x_ref, ctx_ref, mod_ref, z_ref, gcp_ref, xip_ref, gcn_ref, xin_ref, dyv_ref, dyvp_ref, dyvn_ref,
             dgb_ref, dx1_ref, dqt_ref, dk_ref, dv_ref, cos_ref, sa_ref, sb_ref, win_ref, wuq_ref, wukv_ref,
             qg_ref, kvg_ref, cw_ref, gx_ref, dwin_out, dwuq_out, dwukv_out, acc_ref, h1_buf, dz_buf,
             dwin_ref, dwuq_ref, dwukv_ref):
        i = pl.program_id(0)
        lat = i < n_lat

        @pl.when(i == 0)
        def _():
            dwin_ref[...] = jnp.zeros(dwin_ref.shape, F32)
            dwuq_ref[...] = jnp.zeros(dwuq_ref.shape, F32)
            dwukv_ref[...] = jnp.zeros(dwukv_ref.shape, F32)
            acc_ref[...] = jnp.zeros(acc_ref.shape, F32)

        xin = jnp.where(lat, x_ref[...], ctx_ref[...])
        rstd = lax.rsqrt(jnp.mean(xin * xin, axis=-1, keepdims=True) + EPS)
        xn = xin * rstd
        sc = mod_ref[0, 1:2, :]
        h1 = (xn * (1.0 + sc) + mod_ref[0, 0:1, :]).astype(BF16)
        z = z_ref[...]
        cos, sa, sb = cos_ref[...], sa_ref[...], sb_ref[...]
        qgv, kvgv = qg_ref[...], kvg_ref[...]
        cq = z[:, 0:Q_RANK]
        cqh = cq * lax.rsqrt(jnp.mean(cq * cq, axis=-1, keepdims=True) + EPS)
        rq = lax.rsqrt(jnp.mean(cq * cq, axis=-1, keepdims=True) + EPS)
        cqn = (cqh * qgv).astype(BF16)
        parts = []
        for h in range(N_HEADS):
            g = jnp.where(lat, dqt_ref[h], 0.0)
            parts.append(_unrope(g, cos, sa, sb))
        dq = jnp.concatenate([p.astype(BF16) for p in parts], axis=1)
        dcqn = _dot_nt(dq, wuq_ref[...])
        dwuq_ref[...] += _dot_tn(cqn, dq)
        acc_ref[4:5, 0:Q_RANK] += jnp.sum(dcqn * cqh, axis=0, keepdims=True)
        dxn = dcqn * qgv
        dcq = rq * (dxn - cqh * jnp.mean(dxn * cqh, axis=-1, keepdims=True))
        ckv = z[:, Q_RANK:Q_RANK + KV_RANK]
        rk = lax.rsqrt(jnp.mean(ckv * ckv, axis=-1, keepdims=True) + EPS)
        ckvh = ckv * rk
        ckvn = (ckvh * kvgv).astype(BF16)
        dks = [dk_ref[h] for h in range(N_HEADS)]
        dkr = dks[0]
        for h in range(1, N_HEADS):
            dkr = dkr + dks[h]
        dkv = jnp.concatenate([p.astype(BF16) for p in dks + [dv_ref[h] for h in range(N_HEADS)]], axis=1)
        dckvn = _dot_nt(dkv, wukv_ref[...])
        dwukv_ref[...] += _dot_tn(ckvn, dkv)
        acc_ref[5:6, 0:KV_RANK] += jnp.sum(dckvn * ckvh, axis=0, keepdims=True)
        dxn = dckvn * kvgv
        dckv = rk * (dxn - ckvh * jnp.mean(dxn * ckvh, axis=-1, keepdims=True))
        dkr = _unrope(dkr, cos, sa, sb)
        gb, gc, xi = z[:, 512:1024], z[:, 1024:1536], z[:, 1536:2048]
        u = gc * xi
        u_prev = jnp.where(i > 0, gcp_ref[7:8, :] * xip_ref[7:8, :], 0.0)
        u_next = jnp.where(i < n_lat - 1, gcn_ref[0:1, :] * xin_ref[0:1, :], 0.0)
        um1, up1 = _shift_rows(u, u_prev, u_next)
        dyv = jnp.where(lat, dyv_ref[...], 0.0)
        dyv_prev = jnp.where(jnp.logical_and(i > 0, lat), dyvp_ref[7:8, :], 0.0)
        dyv_next = jnp.where(i < n_lat - 1, dyvn_ref[0:1, :], 0.0)
        dyv_m1, dyv_p1 = _shift_rows(dyv, dyv_prev, dyv_next)
        du = cw_ref[0:1, :] * dyv_p1 + cw_ref[1:2, :] * dyv + cw_ref[2:3, :] * dyv_m1
        dgc = du * xi
        dxi = du * gc
        dgb = jnp.where(lat, dgb_ref[...], 0.0)
        acc_ref[6:7, 0:CONV_W] += jnp.sum(dyv * um1, axis=0, keepdims=True)
        acc_ref[7:8, 0:CONV_W] += jnp.sum(dyv * u, axis=0, keepdims=True)
        acc_ref[8:9, 0:CONV_W] += jnp.sum(dyv * up1, axis=0, keepdims=True)
        dz = jnp.concatenate([p.astype(BF16) for p in (dcq, dckv, dkr, dgb, dgc, dxi)], axis=1)
        dh1 = _dot_nt(dz, win_ref[...])
        slot = i % group
        rows_g = pl.ds(pl.multiple_of(slot * tm, tm), tm)
        h1_buf[rows_g, :] = h1
        dz_buf[rows_g, :] = dz

        @pl.when(jnp.logical_and(lat, slot == group - 1))
        def _():
            dwin_ref[...] += _dot_tn(h1_buf[...], dz_buf[...])

        @pl.when(jnp.logical_not(lat))
        def _():
            dwin_ref[...] += _dot_tn(h1, dz)
        s_sh = jnp.sum(dh1, axis=0, keepdims=True)
        s_sc = jnp.sum(dh1 * xn, axis=0, keepdims=True)
        zero = jnp.zeros_like(s_sh)
        acc_ref[0:1, :] += jnp.where(lat, s_sh, zero)
        acc_ref[1:2, :] += jnp.where(lat, s_sc, zero)
        acc_ref[2:3, :] += jnp.where(lat, zero, s_sh)
        acc_ref[3:4, :] += jnp.where(lat, zero, s_sc)
        dxn = dh1 * (1.0 + sc)
        dx = rstd * (dxn - xn * jnp.mean(dxn * xn, axis=-1, keepdims=True))

        @pl.when(lat)
        def _():
            gx_ref[...] = dx1_ref[...] + dx

        @pl.when(i == n_all - 1)
        def _():
            dwin_out[...] = dwin_ref[...].astype(BF16)
            dwuq_out[...] = dwuq_ref[...].astype(BF16)
            dwukv_out[...] = dwukv_ref[...].astype(BF16)

    last = n_lat - 1
    per = tm // 8
    lat_row = lambda i: (jnp.minimum(i, last), 0)
    row = lambda i: (i, 0)
    gcp, gcn = _halo_specs(tm, 2)
    xip, xin = _halo_specs(tm, 3)
    n_halo = t_lat // 8
    dyvp = pl.BlockSpec((8, CONV_W), lambda i: (jnp.clip(i * per - 1, 0, n_halo - 1), 0))
    dyvn = pl.BlockSpec((8, CONV_W), lambda i: (jnp.minimum((i + 1) * per, n_halo - 1), 0))
    gcn = pl.BlockSpec((8, CONV_W), lambda i: (jnp.minimum((i + 1) * per, (t_lat + t_ctx) // 8 - 1), 2))
    xin = pl.BlockSpec((8, CONV_W), lambda i: (jnp.minimum((i + 1) * per, (t_lat + t_ctx) // 8 - 1), 3))
    head_f32 = pl.BlockSpec((N_HEADS, tm, HEAD_PAD), lambda i: (0, i, 0))
    tab = pl.BlockSpec((tm, HEAD_PAD), row)
    return pl.pallas_call(
        body, name="inproj_bwd", grid=(n_all,),
        out_shape=(jax.ShapeDtypeStruct((t_lat, D_MODEL), F32), jax.ShapeDtypeStruct(w_in.shape, BF16),
                   jax.ShapeDtypeStruct(w_uq.shape, BF16), jax.ShapeDtypeStruct(w_ukv.shape, BF16),
                   jax.ShapeDtypeStruct((16, D_MODEL), F32)),
        in_specs=[
            pl.BlockSpec((tm, D_MODEL), lat_row), _const_spec((tm, D_MODEL)),
            pl.BlockSpec((1, 8, D_MODEL), lambda i: (i // n_lat, 0, 0)),
            pl.BlockSpec((tm, Z_COLS), row), gcp, xip, gcn, xin,
            pl.BlockSpec((tm, CONV_W), lat_row), dyvp, dyvn,
            pl.BlockSpec((tm, CONV_W), lat_row), pl.BlockSpec((tm, D_MODEL), lat_row),
            pl.BlockSpec((N_HEADS, tm, HEAD_PAD), lambda i: (0, jnp.minimum(i, last), 0)),
            head_f32, head_f32, tab, tab, tab,
            _const_spec(w_in.shape), _const_spec(w_uq.shape), _const_spec(w_ukv.shape),
            _const_spec(qg.shape), _const_spec(kvg.shape), _const_spec(cw.shape),
        ],
        out_specs=(pl.BlockSpec((tm, D_MODEL), lat_row), _const_spec(w_in.shape), _const_spec(w_uq.shape),
                   _const_spec(w_ukv.shape), _const_spec((16, D_MODEL))),
        scratch_shapes=[pltpu.VMEM((group * tm, D_MODEL), BF16), pltpu.VMEM((group * tm, Z_COLS), BF16),
                        pltpu.VMEM(w_in.shape, F32), pltpu.VMEM(w_uq.shape, F32), pltpu.VMEM(w_ukv.shape, F32)],
        compiler_params=pltpu.CompilerParams(vmem_limit_bytes=VMEM_LIMIT),
    )(x2, ctx2, mod_a, z, z, z, z, z, dyv, dyv, dyv, dgb, dx1, dqt, dk, dv, cos_t, sin_a, sin_b, w_in, w_uq, w_ukv,
      qg, kvg, cw)


def _placement(j, shard_cols):
    src = lax.broadcasted_iota(jnp.int32, (shard_cols, Z_COLS), 0) + j * shard_cols
    col = lax.broadcasted_iota(jnp.int32, (shard_cols, Z_COLS), 1)
    rope0 = Q_RANK + KV_RANK
    shift = jnp.where(src < rope0, 0, jnp.where(src < rope0 + QK_ROPE, ROPE_LANE0, HEAD_PAD - QK_ROPE))
    return jnp.where(col == src + shift, 1.0, 0.0).astype(BF16)


def _head_placement(j, shard_cols, head_cols, split):
    width = N_HEADS * HEAD_PAD * (2 if split else 1)
    r = lax.broadcasted_iota(jnp.int32, (shard_cols, width), 0)
    col = lax.broadcasted_iota(jnp.int32, (shard_cols, width), 1)
    second = r >= head_cols
    head = 2 * j + jnp.where(second, 1, 0)
    d = r - jnp.where(second, head_cols, 0)
    tgt = head * HEAD_PAD + d
    if split:
        tgt = jnp.where(d < QK_NOPE, tgt, tgt + N_HEADS * HEAD_PAD - QK_NOPE)
    return jnp.where(col == tgt, 1.0, 0.0).astype(BF16)


def _place_heads(g, head_cols, split, name):
    n_sh, rows, sw = g.shape
    width = N_HEADS * HEAD_PAD * (2 if split else 1)

    def body(g_ref, o_ref):
        acc = _dot(g_ref[0], _head_placement(0, sw, head_cols, split))
        for j in range(1, n_sh):
            acc = acc + _dot(g_ref[j], _head_placement(j, sw, head_cols, split))
        o_ref[...] = acc.astype(BF16)

    return pl.pallas_call(
        body, name=name, grid=(1,), out_shape=jax.ShapeDtypeStruct((rows, width), BF16),
        in_specs=[_const_spec(g.shape)], out_specs=_const_spec((rows, width)),
    )(g)


def _unplace_heads(d, n_sh, sw, head_cols, split, name):
    rows = d.shape[0]

    def body(d_ref, o_ref):
        o_ref[0] = _dot_nt(d_ref[...], _head_placement(pl.program_id(0), sw, head_cols, split)).astype(BF16)

    return pl.pallas_call(
        body, name=name, grid=(n_sh,), out_shape=jax.ShapeDtypeStruct((n_sh, rows, sw), BF16),
        in_specs=[_const_spec(d.shape)], out_specs=pl.BlockSpec((1, rows, sw), lambda j: (j, 0, 0)),
    )(d)


def _place_w_in(g_in):
    n_sh, rows, sw = g_in.shape

    def body(g_ref, o_ref):
        acc = _dot(g_ref[0], _placement(0, sw))
        for j in range(1, n_sh):
            acc = acc + _dot(g_ref[j], _placement(j, sw))
        o_ref[...] = acc.astype(BF16)

    return pl.pallas_call(
        body, name="place_w_in", grid=(1,), out_shape=jax.ShapeDtypeStruct((rows, Z_COLS), BF16),
        in_specs=[_const_spec(g_in.shape)], out_specs=_const_spec((rows, Z_COLS)),
        compiler_params=pltpu.CompilerParams(vmem_limit_bytes=STAGE_VMEM_LIMIT),
    )(g_in)


def _unplace_d_w_in(d_win, n_sh, sw):
    rows = d_win.shape[0]

    def body(d_ref, o_ref):
        o_ref[0] = _dot_nt(d_ref[...], _placement(pl.program_id(0), sw)).astype(BF16)

    return pl.pallas_call(
        body, name="unplace_d_w_in", grid=(n_sh,), out_shape=jax.ShapeDtypeStruct((n_sh, rows, sw), BF16),
        in_specs=[_const_spec(d_win.shape)], out_specs=pl.BlockSpec((1, rows, sw), lambda j: (j, 0, 0)),
        compiler_params=pltpu.CompilerParams(vmem_limit_bytes=STAGE_VMEM_LIMIT),
    )(d_win)


def _wgrad(a, b, name, bm, bn):
    t, m = a.shape
    n = b.shape[1]
    bk = min(t, 4096)
    nk = t // bk
    nj = n // bn

    def body(a_ref, b_ref, o_ref, acc_ref):
        k = pl.program_id(2)
        part = _dot_tn(a_ref[...], b_ref[...])

        @pl.when(k == 0)
        def _():
            acc_ref[...] = part

        @pl.when(k > 0)
        def _():
            acc_ref[...] += part

        @pl.when(k == nk - 1)
        def _():
            o_ref[...] = acc_ref[...].astype(BF16)

    return pl.pallas_call(
        body, name=name, grid=(m // bm, nj, nk), out_shape=jax.ShapeDtypeStruct((m // bm * nj, bm, bn), BF16),
        in_specs=[pl.BlockSpec((bk, bm), lambda i, j, k: (k, i)), pl.BlockSpec((bk, bn), lambda i, j, k: (k, j))],
        out_specs=pl.BlockSpec((None, bm, bn), lambda i, j, k: (i * nj + j, 0, 0)),
        scratch_shapes=[pltpu.VMEM((bm, bn), F32)],
        compiler_params=pltpu.CompilerParams(vmem_limit_bytes=VMEM_LIMIT),
    )(a, b)


def _wgrad_out(o, conv, dy1, sib_arrays):
    t = o.shape[0]
    bk = min(t, 2048)
    nk = t // bk
    n_s = len(sib_arrays)

    def body(o_ref, c_ref, d_ref, *rest):
        w_ref, got_w_ref, acc_ref = rest[n_s], rest[2 * n_s + 1], rest[2 * n_s + 2]
        send = _SiblingSend(rest[:n_s], rest[n_s + 1:2 * n_s + 1], rest[2 * n_s + 3], rest[2 * n_s + 4])
        wsend, wrecv = rest[2 * n_s + 5], rest[2 * n_s + 6]
        k = pl.program_id(0)
        pl.when(k == 0)(send.start)
        part = _dot_tn(jnp.concatenate([o_ref[...], c_ref[...]], axis=1), d_ref[...])

        @pl.when(k == 0)
        def _():
            acc_ref[...] = part

        @pl.when(k > 0)
        def _():
            acc_ref[...] += part

        @pl.when(k == nk - 1)
        def _():
            w_ref[...] = acc_ref[...].astype(BF16)
            x, y, c = _pos()
            own = []
            for s in range(N_SHARD):
                theirs = pl.ds(pl.multiple_of(s * shard_rows + (1 - c) * (shard_rows // 2), 16), shard_rows // 2)
                cp = pltpu.make_async_remote_copy(
                    src_ref=w_ref.at[theirs], dst_ref=got_w_ref.at[s], send_sem=wsend.at[s], recv_sem=wrecv.at[s],
                    device_id=(x, y, 1 - c), device_id_type=MESH)
                cp.start()
                own.append(cp)
            for cp in own:
                cp.wait()

        pl.when(k == nk - 1)(send.finish)

    hbm = pl.BlockSpec(memory_space=pl.ANY)
    shard_rows = D_MODEL // N_SHARD
    return pl.pallas_call(
        body, name="wgrad_out", grid=(nk,),
        out_shape=(jax.ShapeDtypeStruct((D_MODEL, D_MODEL), BF16),)
        + tuple(jax.ShapeDtypeStruct((N_SHARD, a.shape[1] // 2, a.shape[2]), BF16) for a in sib_arrays)
        + (jax.ShapeDtypeStruct((N_SHARD, shard_rows // 2, D_MODEL), BF16),),
        in_specs=[pl.BlockSpec((bk, N_HEADS * V_DIM), lambda k: (k, 0)),
                  pl.BlockSpec((bk, CONV_W), lambda k: (k, 0)),
                  pl.BlockSpec((bk, D_MODEL), lambda k: (k, 0))] + [hbm] * n_s,
        out_specs=(_const_spec((D_MODEL, D_MODEL)),) + (hbm,) * (n_s + 1),
        scratch_shapes=[pltpu.VMEM((D_MODEL, D_MODEL), F32), pltpu.SemaphoreType.DMA((n_s,)),
                        pltpu.SemaphoreType.DMA((n_s,)), pltpu.SemaphoreType.DMA((N_SHARD,)),
                        pltpu.SemaphoreType.DMA((N_SHARD,))],
        compiler_params=pltpu.CompilerParams(vmem_limit_bytes=VMEM_LIMIT),
    )(o, conv, dy1, *sib_arrays)


ADAMW_STEPS = 4


def _adamw_call(ws, gs, ms, vs, name):
    n = len(ws)
    steps = ADAMW_STEPS if all(w.shape[0] % (8 * ADAMW_STEPS) == 0 for w in ws) else 1

    def body(*refs):
        for a in range(n):
            w_ref, g_ref, m_ref, v_ref = (refs[k * n + a] for k in range(4))
            d_, m_, v_ = _adamw(w_ref[...], g_ref[...], m_ref[...], v_ref[...])
            for k, val in enumerate((d_, m_, v_)):
                refs[(4 + k) * n + a][...] = val

    specs = [pl.BlockSpec((w.shape[0] // steps, w.shape[1]), lambda i: (i, 0)) for w in ws]
    shapes = [jax.ShapeDtypeStruct(w.shape, F32) for w in ws]
    outs = pl.pallas_call(
        body, name=name, grid=(steps,), out_shape=tuple(shapes * 3),
        in_specs=specs * 4, out_specs=tuple(specs * 3),
    )(*ws, *gs, *ms, *vs)
    return [tuple(outs[k * n + a] for k in range(3)) for a in range(n)]


def _adamw_halves(ws, g_mine, g_theirs, ms, vs, c_idx, name):
    n = len(ws)
    steps = ADAMW_STEPS
    nb = steps // 2

    def body(c_ref, *refs):
        mine = pl.program_id(0) // nb == c_ref[0]
        for a in range(n):
            w_ref, gm_ref, gt_ref, m_ref, v_ref = (refs[k * n + a] for k in range(5))
            g = jnp.where(mine, gm_ref[...], gt_ref[...])
            d_, m_, v_ = _adamw(w_ref[...], g, m_ref[...], v_ref[...])
            for k, val in enumerate((g, d_, m_, v_)):
                refs[(5 + k) * n + a][...] = val

    specs = [pl.BlockSpec((w.shape[0] // steps, w.shape[1]), lambda i, c_ref: (i, 0)) for w in ws]
    hspecs = [pl.BlockSpec((w.shape[0] // steps, w.shape[1]), lambda i, c_ref: (i % nb, 0)) for w in ws]
    shapes = [jax.ShapeDtypeStruct(w.shape, F32) for w in ws]
    grid_spec = pltpu.PrefetchScalarGridSpec(
        num_scalar_prefetch=1, grid=(steps,), in_specs=specs + hspecs + hspecs + specs + specs,
        out_specs=tuple(specs * 4))
    outs = pl.pallas_call(
        body, name=name, grid_spec=grid_spec, out_shape=tuple(shapes * 4),
        compiler_params=pltpu.CompilerParams(vmem_limit_bytes=VMEM_LIMIT),
    )(c_idx, *ws, *g_mine, *g_theirs, *ms, *vs)
    return [tuple(outs[k * n + a] for k in range(4)) for a in range(n)]


def _wmod_update(s_t, dm, w, m, v):
    rows, cols = w.shape
    cb = 512

    def body(s_ref, dm_ref, w_ref, m_ref, v_ref, g_ref, d_ref, nm_ref, nv_ref):
        g = jnp.dot(s_ref[...], dm_ref[...], precision=HIGHEST, preferred_element_type=F32)
        d_, m_, v_ = _adamw(w_ref[...], g, m_ref[...], v_ref[...])
        g_ref[...] = g
        d_ref[...] = d_
        nm_ref[...] = m_
        nv_ref[...] = v_

    spec = pl.BlockSpec((rows, cb), lambda i: (0, i))
    shp = jax.ShapeDtypeStruct((rows, cols), F32)
    return pl.pallas_call(
        body, name="wmod_update", grid=(cols // cb,), out_shape=(shp, shp, shp, shp),
        in_specs=[_const_spec(s_t.shape), pl.BlockSpec((16, cb), lambda i: (0, i)), spec, spec, spec],
        out_specs=(spec, spec, spec, spec),
        compiler_params=pltpu.CompilerParams(vmem_limit_bytes=VMEM_LIMIT),
    )(s_t, dm, w, m, v)


def _rope_tables(t_lat, t_ctx):
    t = jnp.arange(t_lat)
    pos = jnp.stack([(t // GRID_W).astype(F32), (t % GRID_W).astype(F32)], axis=1)
    half = QK_ROPE // 4
    freqs = ROPE_THETA ** (-jnp.arange(0, 2 * half, 2, dtype=F32) / (2 * half))
    ang = pos[:, :, None] * freqs[None, None, :]
    cos, sin = jnp.cos(ang), jnp.sin(ang)
    zero = jnp.zeros_like(sin)
    cos32 = jnp.concatenate([cos, cos], axis=2).reshape(t_lat, QK_ROPE)
    sa32 = jnp.concatenate([zero, sin], axis=2).reshape(t_lat, QK_ROPE)
    sb32 = jnp.concatenate([-sin, zero], axis=2).reshape(t_lat, QK_ROPE)

    def widen(tab, fill):
        left = jnp.full((t_lat, ROPE_LANE0), fill, F32)
        right = jnp.full((t_lat, HEAD_PAD - ROPE_LANE0 - QK_ROPE), fill, F32)
        lat = jnp.concatenate([left, tab, right], axis=1)
        return jnp.concatenate([lat, jnp.full((t_ctx, HEAD_PAD), fill, F32)], axis=0)

    return widen(cos32, 1.0), widen(sa32, 0.0), widen(sb32, 0.0)


def _cols_from_shards(s):
    return jnp.transpose(s, (1, 0, 2)).reshape(s.shape[1], -1)


def _cols_to_shards(w):
    k, n = w.shape
    return jnp.transpose(w.reshape(k, N_SHARD, n // N_SHARD), (1, 0, 2))


def kernel(x, c, ctx, c_ctx, w_mod, b_mod, w_in, q_norm_g, w_uq, kv_norm_g, w_ukv, conv_w, w_out, w_mlp1, w_mlp2, final_norm_g, loss_target, m_c_ctx, m_w_mod, m_b_mod, m_w_in, m_q_norm_g, m_w_uq, m_kv_norm_g, m_w_ukv, m_conv_w, m_w_out, m_w_mlp1, m_w_mlp2, m_final_norm_g, v_c_ctx, v_w_mod, v_b_mod, v_w_in, v_q_norm_g, v_w_uq, v_kv_norm_g, v_w_ukv, v_conv_w, v_w_out, v_w_mlp1, v_w_mlp2, v_final_norm_g):
    t_lat, t_ctx = x.shape[1], ctx.shape[1]
    assert t_ctx == TOK_TILE and t_lat % TOK_TILE == 0 and t_lat % GRID_W == 0
    mx, my, mc = _pos()
    j = 2 * mx + my
    ncol = w_mod.shape[2]
    x2, ctx2, tgt = x[0], ctx[0], loss_target[0]
    cctx_row = c_ctx.reshape(1, D_MODEL)

    b_sh = lax.dynamic_slice(b_mod, (0, j * ncol), (1, ncol))
    cw_pad = jnp.zeros((8, 128), F32).at[0:3, :].set(conv_w[0])
    c8, m_all, g_in, g_uq, g_ukv, g_out, g_m1, g_m2 = _prologue(
        c, cctx_row, w_mod[0], b_sh, cw_pad, (w_in[0], w_uq[0], w_ukv[0], w_out[0], w_mlp1[0], w_mlp2[0]), 3)
    mvec = m_all[:, 0, :].reshape(6, D_MODEL)
    mctx = m_all[:, 8, :].reshape(6, D_MODEL)
    zeros6 = jnp.zeros((6, D_MODEL), F32)
    mod_a = jnp.stack([jnp.concatenate([mvec[0:2], zeros6], axis=0), jnp.concatenate([mctx[0:2], zeros6], axis=0)])
    mod_b = jnp.concatenate([mvec[2:6], jnp.zeros((4, D_MODEL), F32)], axis=0)
    cw_full = jnp.pad(jnp.transpose(m_all[:, 9:12, 0:128], (1, 0, 2)).reshape(3, CONV_W), ((0, 5), (0, 0)))

    w_in_p = _place_w_in(g_in)
    w_uq_p = _place_heads(g_uq, QK_DIM, False, "place_w_uq")
    w_ukv_p = _place_heads(g_ukv, QK_NOPE + V_DIM, True, "place_w_ukv")
    cos_t, sin_a, sin_b = _rope_tables(t_lat, t_ctx)
    gf_row = final_norm_g.reshape(1, D_MODEL)
    c_idx = mc.reshape(1).astype(jnp.int32)
    j_idx = j.reshape(1).astype(jnp.int32)

    z, q, k, v, kt = _inproj_fwd(x2, ctx2, mod_a, w_in_p, q_norm_g, kv_norm_g, w_uq_p, w_ukv_p, cos_t, sin_a, sin_b)
    o, lse, g_out, w1, g_m2 = _attn_fwd(q, k, v, t_lat, (g_out, g_m1, g_m2))
    w_out_f = g_out.reshape(D_MODEL, D_MODEL)
    w2 = g_m2.reshape(D_FF, D_MODEL)
    r, da, h2, dy2, dx1, conv, acc_mlp, dy1, do, dgb, dyv = _mlp_fwdbwd(o, z, x2, tgt, mod_b, gf_row, cw_full, w_out_f,
                                                                         w1, w2)
    d_w1 = _wgrad(h2, da, "wgrad_mlp1", D_MODEL, FF_CHUNK)
    d_w2 = _wgrad(r, dy2, "wgrad_mlp2", FF_CHUNK, D_MODEL)
    d_wout, *big_got = _wgrad_out(o, conv, dy1, (d_w1, d_w2))
    d_wout = d_wout.reshape(N_SHARD, D_MODEL // N_SHARD, D_MODEL)
    big_grads = (d_w1, d_w2, d_wout)
    big_parts = _add_pairs(big_grads, big_got, c_idx, "rs_add_pairs_big")
    dqt, dk, dv, *big_recv = _attn_bwd(q, k, v, kt, o, do, lse, t_lat, big_parts)
    big_halves = _add_chips(big_parts, big_recv, j_idx, "rs_add_chips_big")
    gx, d_win, d_wuq, d_wukv, acc_in = _inproj_bwd(x2, ctx2, mod_a, z, dyv, dgb, dx1, dqt, dk, dv, cos_t, sin_a, sin_b,
                                                   w_in_p, w_uq_p, w_ukv_p, q_norm_g, kv_norm_g, cw_full)

    d_win_sh = _unplace_d_w_in(d_win, N_SHARD, w_in.shape[2])
    d_wuq_sh = _unplace_heads(d_wuq, N_SHARD, w_uq.shape[2], QK_DIM, False, "unplace_d_w_uq")
    d_wukv_sh = _unplace_heads(d_wukv, N_SHARD, w_ukv.shape[2], QK_NOPE + V_DIM, True, "unplace_d_w_ukv")
    rest = (d_win_sh, d_wuq_sh, d_wukv_sh)
    rest_got = _rs_sibling(rest, "rs_sibling_rest")
    rest_parts = _add_pairs(rest, rest_got, c_idx, "rs_add_pairs_rest")

    sv = jnp.concatenate([
        acc_in[0:2], acc_mlp[5:6], acc_mlp[2:4], acc_mlp[1:2],
        acc_in[2:4], acc_mlp[0:1], acc_in[4:5], acc_in[5:6], acc_in[6:9], acc_mlp[4:5],
        jnp.zeros((1, D_MODEL), F32)], axis=0)
    all_sv, red, o_cc, o_b, o_q, o_k, o_gf, *exchanged = _small_exchange(
        sv, w_mod[0], cctx_row, m_c_ctx.reshape(1, D_MODEL), v_c_ctx.reshape(1, D_MODEL),
        b_mod.reshape(6, D_MODEL), m_b_mod.reshape(6, D_MODEL), v_b_mod.reshape(6, D_MODEL),
        q_norm_g, m_q_norm_g, v_q_norm_g, kv_norm_g, m_kv_norm_g, v_kv_norm_g,
        gf_row, m_final_norm_g.reshape(1, D_MODEL), v_final_norm_g.reshape(1, D_MODEL), big_halves, rest_parts)
    big_theirs, rest_recv = exchanged[:len(big_halves)], exchanged[len(big_halves):]
    loss = red[14, 0]

    c9 = jnp.concatenate([c8[0::8], jnp.zeros((7, D_MODEL), F32)], axis=0)
    s_t = jnp.transpose(c9 * jax.nn.sigmoid(c9))
    dm_ex = all_sv[:, 0:6, :].reshape(8, 6 * D_MODEL)
    dm_ctx = jnp.concatenate([red[6:8].reshape(1, 2 * D_MODEL), jnp.zeros((1, 4 * D_MODEL), F32)], axis=1)
    dm16 = jnp.concatenate([dm_ex, dm_ctx, jnp.zeros((7, 6 * D_MODEL), F32)], axis=0)
    dm_sh = lax.dynamic_slice(dm16, (0, j * ncol), (16, ncol))
    g_wmod, d_wmod, nm_wmod, nv_wmod = _wmod_update(s_t, dm_sh, w_mod[0], m_w_mod[0], v_w_mod[0])

    g_cw = lax.dynamic_slice(red[11:14, 0:CONV_W], (0, j * 128), (3, 128))
    (d_cw, nm_cw, nv_cw), = _adamw_call([conv_w[0]], [g_cw], [m_conv_w[0]], [v_conv_w[0]], "adamw_conv")

    rest_halves = _add_chips(rest_parts, rest_recv, j_idx, "rs_add_chips_rest")
    g_win, g_wuq, g_wukv = _rs_join(rest_halves, "rs_join_rest")
    upd = dict(zip(("in", "uq", "ukv"), _adamw_call(
        [w_in[0], w_uq[0], w_ukv[0]], [g_win, g_wuq, g_wukv], [m_w_in[0], m_w_uq[0], m_w_ukv[0]],
        [v_w_in[0], v_w_uq[0], v_w_ukv[0]], "adamw_rest")))
    (g_w1, *upd["mlp1"]), (g_w2, *upd["mlp2"]), (g_wout, *upd["out"]) = _adamw_halves(
        [w_mlp1[0], w_mlp2[0], w_out[0]], big_halves, big_theirs, [m_w_mlp1[0], m_w_mlp2[0], m_w_out[0]],
        [v_w_mlp1[0], v_w_mlp2[0], v_w_out[0]], c_idx, "adamw_big")

    def four(o4, shape):
        return [o4[r].reshape(shape) for r in range(4)]

    cc4 = four(o_cc, (D_MODEL,))
    b4 = [o_b[r].reshape(1, 6 * D_MODEL) for r in range(4)]
    q4 = four(o_q, (1, Q_RANK))
    k4 = four(o_k, (1, KV_RANK))
    gf4 = four(o_gf, (D_MODEL,))
    big = {"in": g_win, "uq": g_wuq, "ukv": g_wukv, "out": g_wout, "mlp1": g_w1, "mlp2": g_w2}

    def leaf(idx):
        wm = (g_wmod, d_wmod, nm_wmod, nv_wmod)[idx]
        cwv = (g_cw, d_cw, nm_cw, nv_cw)[idx]
        bigv = {n: (big[n] if idx == 0 else upd[n][idx - 1]) for n in big}
        return [cc4[idx], wm[None], b4[idx], bigv["in"][None], q4[idx], bigv["uq"][None], k4[idx], bigv["ukv"][None],
                cwv[None], bigv["out"][None], bigv["mlp1"][None], bigv["mlp2"][None], gf4[idx]]

    return (loss, gx[None], *leaf(0), *leaf(1), *leaf(2), *leaf(3))
```
